```python
import jax, jax.numpy as jnp
from jax import lax
import numpy as np

D_MODEL = 1024
BATCH = 8
SEQ = 4096
DEPTH = 1

D_MIX = D_MODEL
D_A = D_MIX // 2
D_B = D_MIX - D_A
N_HEADS_A = 4
HEAD_DIM_A = D_A // N_HEADS_A
CHUNK = 128
POOL_WINDOWS = (2, 4, 8, 16)
N_POOL_GROUPS = len(POOL_WINDOWS)
POOL_GROUP_DIM = D_B // N_POOL_GROUPS
D_FF = 4 * D_MODEL
N_MOD = 6
EPS = 1e-6

kernel_name = "hybrid_gmlp_pool_sqrelu_block"


def rms_norm(x, g):
    xf = x.astype(jnp.float32)
    y = xf * lax.rsqrt(jnp.mean(xf * xf, axis=-1, keepdims=True) + EPS)
    return (y * g.astype(jnp.float32)).astype(x.dtype)


def layer_norm(x, g, b):
    xf = x.astype(jnp.float32)
    mu = jnp.mean(xf, axis=-1, keepdims=True)
    var = jnp.mean(jnp.square(xf - mu), axis=-1, keepdims=True)
    y = (xf - mu) * lax.rsqrt(var + EPS)
    return (y * g.astype(jnp.float32) + b.astype(jnp.float32)).astype(x.dtype)


def spatial_gating(z_a, w_spatial, b_spatial, ln_v_gain, ln_v_bias):
    b, s, _ = z_a.shape
    z_a = jax.nn.gelu(z_a)
    u, v = z_a[..., :D_A], z_a[..., D_A:]
    v = layer_norm(v, ln_v_gain, ln_v_bias)
    v = v.reshape(b, s // CHUNK, CHUNK, N_HEADS_A, HEAD_DIM_A)
    mask = jnp.tril(jnp.ones((CHUNK, CHUNK), dtype=w_spatial.dtype))
    w_causal = w_spatial * mask[None]
    mixed = jnp.einsum("hts,bnshd->bnthd", w_causal, v)
    mixed = mixed + b_spatial.T[:, :, None]
    return u * mixed.reshape(b, s, D_A)


def multiscale_pool(z_b, w_pool, b_pool, pool_scale):
    b, s, _ = z_b.shape
    zg = z_b.reshape(b, s, N_POOL_GROUPS, POOL_GROUP_DIM)
    zf = zg.astype(jnp.float32)
    cs = jnp.concatenate(
        [jnp.zeros((b, 1, N_POOL_GROUPS, POOL_GROUP_DIM), jnp.float32), jnp.cumsum(zf, axis=1)],
        axis=1)
    pos = jnp.arange(s, dtype=jnp.float32)
    pooled = []
    for g, w in enumerate(POOL_WINDOWS):
        csg = cs[:, :, g]
        lower = jnp.concatenate(
            [jnp.zeros((b, w - 1, POOL_GROUP_DIM), jnp.float32), csg[:, : s + 1 - w]], axis=1)
        count = jnp.minimum(pos + 1.0, float(w))[None, :, None]
        pooled.append((csg[:, 1:] - lower) / count)
    pooled = jnp.stack(pooled, axis=2)
    diff = (pooled - zf).astype(z_b.dtype)
    y = jnp.einsum("bsgc,gcd->bsgd", diff, w_pool) + b_pool
    return y.reshape(b, s, D_B) * pool_scale


def _fwd_setup_inputs(seed: int = 0) -> dict:
    key = jax.random.key(seed)
    ks = jax.random.split(key, 20)
    f32 = jnp.float32
    nrm = lambda k, shape, scale: jax.random.normal(k, shape, f32) * scale
    return {
        "x": nrm(ks[0], (BATCH, SEQ, D_MODEL), 1.0),
        "c": nrm(ks[1], (BATCH, D_MODEL), 1.0),
        "w_ada": nrm(ks[2], (D_MODEL, N_MOD * D_MODEL), 0.5 * D_MODEL ** -0.5),
        "b_ada": nrm(ks[3], (N_MOD * D_MODEL,), 0.01),
        "norm1_pre": 1.0 + nrm(ks[4], (D_MODEL,), 0.02),
        "norm1_post": 1.0 + nrm(ks[5], (D_MODEL,), 0.02),
        "w_in": nrm(ks[6], (D_MODEL, 2 * D_A + D_B), D_MODEL ** -0.5),
        "w_spatial": nrm(ks[7], (N_HEADS_A, CHUNK, CHUNK), 0.5 * CHUNK ** -0.5),
        "b_spatial": 1.0 + nrm(ks[8], (N_HEADS_A, CHUNK), 0.02),
        "ln_v_gain": 1.0 + nrm(ks[9], (D_A,), 0.02),
        "ln_v_bias": nrm(ks[10], (D_A,), 0.02),
        "w_pool": nrm(ks[11], (N_POOL_GROUPS, POOL_GROUP_DIM, POOL_GROUP_DIM), POOL_GROUP_DIM ** -0.5),
        "b_pool": nrm(ks[12], (N_POOL_GROUPS, POOL_GROUP_DIM), 0.02),
        "pool_scale": 1.0 + nrm(ks[13], (D_B,), 0.02),
        "w_out": nrm(ks[14], (D_MIX, D_MODEL), D_MIX ** -0.5),
        "norm2_pre": 1.0 + nrm(ks[15], (D_MODEL,), 0.02),
        "norm2_post": 1.0 + nrm(ks[16], (D_MODEL,), 0.02),
        "w_fc1": nrm(ks[17], (DEPTH, D_MODEL, D_FF), D_MODEL ** -0.5)[0],
        "w_fc2": nrm(ks[18], (D_FF, D_MODEL), D_FF ** -0.5),
    }


def _fwd_reference(x, c, w_ada, b_ada, norm1_pre, norm1_post, w_in, w_spatial, b_spatial,
              ln_v_gain, ln_v_bias, w_pool, b_pool, pool_scale, w_out,
              norm2_pre, norm2_post, w_fc1, w_fc2):
    mod = jax.nn.silu(c) @ w_ada + b_ada
    shift1, scale1, gate1, shift2, scale2, gate2 = [
        m[:, None, :] for m in jnp.split(mod, N_MOD, axis=-1)]

    for _ in range(DEPTH):
        h = rms_norm(x, norm1_pre) * (1.0 + scale1) + shift1
        z = h @ w_in
        y_a = spatial_gating(z[..., : 2 * D_A], w_spatial, b_spatial, ln_v_gain, ln_v_bias)
        y_b = multiscale_pool(z[..., 2 * D_A:], w_pool, b_pool, pool_scale)
        mix = jnp.concatenate([y_a, y_b], axis=-1) @ w_out
        x = x + gate1 * rms_norm(mix, norm1_post)

        h = rms_norm(x, norm2_pre) * (1.0 + scale2) + shift2
        f = jnp.square(jax.nn.relu(h @ w_fc1)) @ w_fc2
        x = x + gate2 * rms_norm(f, norm2_post)
    return x


import jax as _jax
import jax.numpy as _jnp

TWIN_FORMAT = 'train_step'
FWD_PARAMS = ['x', 'c', 'w_ada', 'b_ada', 'norm1_pre', 'norm1_post', 'w_in', 'w_spatial', 'b_spatial', 'ln_v_gain', 'ln_v_bias', 'w_pool', 'b_pool', 'pool_scale', 'w_out', 'norm2_pre', 'norm2_post', 'w_fc1', 'w_fc2']
TWIN_WEIGHTS = ['w_ada', 'b_ada', 'norm1_pre', 'norm1_post', 'w_in', 'w_spatial', 'b_spatial', 'ln_v_gain', 'ln_v_bias', 'w_pool', 'b_pool', 'pool_scale', 'w_out', 'norm2_pre', 'norm2_post', 'w_fc1', 'w_fc2']
TWIN_DIFF_INPUT = 'x'
TWIN_INPUTS = ['x', 'c', 'w_ada', 'b_ada', 'norm1_pre', 'norm1_post', 'w_in', 'w_spatial', 'b_spatial', 'ln_v_gain', 'ln_v_bias', 'w_pool', 'b_pool', 'pool_scale', 'w_out', 'norm2_pre', 'norm2_post', 'w_fc1', 'w_fc2', 'loss_target', 'm_w_ada', 'm_b_ada', 'm_norm1_pre', 'm_norm1_post', 'm_w_in', 'm_w_spatial', 'm_b_spatial', 'm_ln_v_gain', 'm_ln_v_bias', 'm_w_pool', 'm_b_pool', 'm_pool_scale', 'm_w_out', 'm_norm2_pre', 'm_norm2_post', 'm_w_fc1', 'm_w_fc2', 'v_w_ada', 'v_b_ada', 'v_norm1_pre', 'v_norm1_post', 'v_w_in', 'v_w_spatial', 'v_b_spatial', 'v_ln_v_gain', 'v_ln_v_bias', 'v_w_pool', 'v_b_pool', 'v_pool_scale', 'v_w_out', 'v_norm2_pre', 'v_norm2_post', 'v_w_fc1', 'v_w_fc2']
TWIN_OUTPUTS = ['loss', 'grad_x', 'grad_w_ada', 'grad_b_ada', 'grad_norm1_pre', 'grad_norm1_post', 'grad_w_in', 'grad_w_spatial', 'grad_b_spatial', 'grad_ln_v_gain', 'grad_ln_v_bias', 'grad_w_pool', 'grad_b_pool', 'grad_pool_scale', 'grad_w_out', 'grad_norm2_pre', 'grad_norm2_post', 'grad_w_fc1', 'grad_w_fc2', 'delta_w_ada', 'delta_b_ada', 'delta_norm1_pre', 'delta_norm1_post', 'delta_w_in', 'delta_w_spatial', 'delta_b_spatial', 'delta_ln_v_gain', 'delta_ln_v_bias', 'delta_w_pool', 'delta_b_pool', 'delta_pool_scale', 'delta_w_out', 'delta_norm2_pre', 'delta_norm2_post', 'delta_w_fc1', 'delta_w_fc2', 'new_m_w_ada', 'new_m_b_ada', 'new_m_norm1_pre', 'new_m_norm1_post', 'new_m_w_in', 'new_m_w_spatial', 'new_m_b_spatial', 'new_m_ln_v_gain', 'new_m_ln_v_bias', 'new_m_w_pool', 'new_m_b_pool', 'new_m_pool_scale', 'new_m_w_out', 'new_m_norm2_pre', 'new_m_norm2_post', 'new_m_w_fc1', 'new_m_w_fc2', 'new_v_w_ada', 'new_v_b_ada', 'new_v_norm1_pre', 'new_v_norm1_post', 'new_v_w_in', 'new_v_w_spatial', 'new_v_b_spatial', 'new_v_ln_v_gain', 'new_v_ln_v_bias', 'new_v_w_pool', 'new_v_b_pool', 'new_v_pool_scale', 'new_v_w_out', 'new_v_norm2_pre', 'new_v_norm2_post', 'new_v_w_fc1', 'new_v_w_fc2']
TWIN_LEAF_KINDS = {'loss': 'loss', 'grad_x': 'grad_x', 'grad_w_ada': 'grad_w', 'grad_b_ada': 'grad_w', 'grad_norm1_pre': 'grad_w', 'grad_norm1_post': 'grad_w', 'grad_w_in': 'grad_w', 'grad_w_spatial': 'grad_w', 'grad_b_spatial': 'grad_w', 'grad_ln_v_gain': 'grad_w', 'grad_ln_v_bias': 'grad_w', 'grad_w_pool': 'grad_w', 'grad_b_pool': 'grad_w', 'grad_pool_scale': 'grad_w', 'grad_w_out': 'grad_w', 'grad_norm2_pre': 'grad_w', 'grad_norm2_post': 'grad_w', 'grad_w_fc1': 'grad_w', 'grad_w_fc2': 'grad_w', 'delta_w_ada': 'delta_w', 'delta_b_ada': 'delta_w', 'delta_norm1_pre': 'delta_w', 'delta_norm1_post': 'delta_w', 'delta_w_in': 'delta_w', 'delta_w_spatial': 'delta_w', 'delta_b_spatial': 'delta_w', 'delta_ln_v_gain': 'delta_w', 'delta_ln_v_bias': 'delta_w', 'delta_w_pool': 'delta_w', 'delta_b_pool': 'delta_w', 'delta_pool_scale': 'delta_w', 'delta_w_out': 'delta_w', 'delta_norm2_pre': 'delta_w', 'delta_norm2_post': 'delta_w', 'delta_w_fc1': 'delta_w', 'delta_w_fc2': 'delta_w', 'new_m_w_ada': 'new_m', 'new_m_b_ada': 'new_m', 'new_m_norm1_pre': 'new_m', 'new_m_norm1_post': 'new_m', 'new_m_w_in': 'new_m', 'new_m_w_spatial': 'new_m', 'new_m_b_spatial': 'new_m', 'new_m_ln_v_gain': 'new_m', 'new_m_ln_v_bias': 'new_m', 'new_m_w_pool': 'new_m', 'new_m_b_pool': 'new_m', 'new_m_pool_scale': 'new_m', 'new_m_w_out': 'new_m', 'new_m_norm2_pre': 'new_m', 'new_m_norm2_post': 'new_m', 'new_m_w_fc1': 'new_m', 'new_m_w_fc2': 'new_m', 'new_v_w_ada': 'new_v', 'new_v_b_ada': 'new_v', 'new_v_norm1_pre': 'new_v', 'new_v_norm1_post': 'new_v', 'new_v_w_in': 'new_v', 'new_v_w_spatial': 'new_v', 'new_v_b_spatial': 'new_v', 'new_v_ln_v_gain': 'new_v', 'new_v_ln_v_bias': 'new_v', 'new_v_w_pool': 'new_v', 'new_v_b_pool': 'new_v', 'new_v_pool_scale': 'new_v', 'new_v_w_out': 'new_v', 'new_v_norm2_pre': 'new_v', 'new_v_norm2_post': 'new_v', 'new_v_w_fc1': 'new_v', 'new_v_w_fc2': 'new_v'}


def _forward(args):
    return _fwd_reference(*[args[k] for k in FWD_PARAMS])


def _output_shape():
    out = _jax.eval_shape(lambda: _forward(_fwd_setup_inputs(0)))
    return out.shape, out.dtype

N_MICROBATCH = 1
ADAM_LR = 0.001
ADAM_B1 = 0.9
ADAM_B2 = 0.999
ADAM_EPS = 1e-08
ADAM_WD = 0.01
ADAM_STEP = 10
PER_EXAMPLE_BATCH_AXIS = {'x': 0, 'c': 0, 'loss_target': 0}
SHARED_INPUTS = []
_WEIGHT_DTYPES = {'w_ada': _jnp.float32, 'b_ada': _jnp.float32, 'norm1_pre': _jnp.float32, 'norm1_post': _jnp.float32, 'w_in': _jnp.float32, 'w_spatial': _jnp.float32, 'b_spatial': _jnp.float32, 'ln_v_gain': _jnp.float32, 'ln_v_bias': _jnp.float32, 'w_pool': _jnp.float32, 'b_pool': _jnp.float32, 'pool_scale': _jnp.float32, 'w_out': _jnp.float32, 'norm2_pre': _jnp.float32, 'norm2_post': _jnp.float32, 'w_fc1': _jnp.float32, 'w_fc2': _jnp.float32}
MOMENT_SCALE = {'w_ada': 1.816806e+00, 'b_ada': 3.472949e+00, 'norm1_pre': 1.266438e-01, 'norm1_post': 3.777276e+00, 'w_in': 1.289118e-01, 'w_spatial': 5.275730e-02, 'b_spatial': 7.327051e-02, 'ln_v_gain': 2.922842e-02, 'ln_v_bias': 2.892298e-02, 'w_pool': 1.567093e-01, 'b_pool': 1.075846e+00, 'pool_scale': 1.919840e-01, 'w_out': 2.771720e-01, 'norm2_pre': 1.412677e-01, 'norm2_post': 3.938913e+00, 'w_fc1': 1.019058e-01, 'w_fc2': 4.745470e-01}


def _to_microbatches(a, axis):
    t = _jnp.moveaxis(a, axis, 0)
    t = t.reshape((N_MICROBATCH, t.shape[0] // N_MICROBATCH) + t.shape[1:])
    return _jnp.moveaxis(t, 1, axis + 1)


def setup_inputs(seed: int = 0) -> dict:
    inp = _fwd_setup_inputs(seed)
    key = _jax.random.fold_in(_jax.random.key(seed), 7919)
    shape, _ = _output_shape()
    out = dict(inp)
    out["loss_target"] = _jax.random.normal(_jax.random.fold_in(key, 0), shape, _jnp.float32)
    for i, name in enumerate(TWIN_WEIGHTS):
        w = inp[name].astype(_jnp.float32)
        if MOMENT_SCALE is None:
            s = _jnp.sqrt(_jnp.mean(_jnp.square(w)) + 1e-30)
        else:
            s = MOMENT_SCALE[name]
        km, kv = _jax.random.split(_jax.random.fold_in(key, i + 1))
        out[name] = w
        out["m_" + name] = s * _jax.random.normal(km, w.shape, _jnp.float32)
        out["v_" + name] = (s * s) * _jax.random.uniform(kv, w.shape, _jnp.float32, 0.5, 1.5)
    if N_MICROBATCH > 1:
        for name, axis in PER_EXAMPLE_BATCH_AXIS.items():
            out[name] = _to_microbatches(out[name], axis)
    return {'x': out['x'], 'c': out['c'], 'w_ada': out['w_ada'], 'b_ada': out['b_ada'], 'norm1_pre': out['norm1_pre'], 'norm1_post': out['norm1_post'], 'w_in': out['w_in'], 'w_spatial': out['w_spatial'], 'b_spatial': out['b_spatial'], 'ln_v_gain': out['ln_v_gain'], 'ln_v_bias': out['ln_v_bias'], 'w_pool': out['w_pool'], 'b_pool': out['b_pool'], 'pool_scale': out['pool_scale'], 'w_out': out['w_out'], 'norm2_pre': out['norm2_pre'], 'norm2_post': out['norm2_post'], 'w_fc1': out['w_fc1'], 'w_fc2': out['w_fc2'], 'loss_target': out['loss_target'], 'm_w_ada': out['m_w_ada'], 'm_b_ada': out['m_b_ada'], 'm_norm1_pre': out['m_norm1_pre'], 'm_norm1_post': out['m_norm1_post'], 'm_w_in': out['m_w_in'], 'm_w_spatial': out['m_w_spatial'], 'm_b_spatial': out['m_b_spatial'], 'm_ln_v_gain': out['m_ln_v_gain'], 'm_ln_v_bias': out['m_ln_v_bias'], 'm_w_pool': out['m_w_pool'], 'm_b_pool': out['m_b_pool'], 'm_pool_scale': out['m_pool_scale'], 'm_w_out': out['m_w_out'], 'm_norm2_pre': out['m_norm2_pre'], 'm_norm2_post': out['m_norm2_post'], 'm_w_fc1': out['m_w_fc1'], 'm_w_fc2': out['m_w_fc2'], 'v_w_ada': out['v_w_ada'], 'v_b_ada': out['v_b_ada'], 'v_norm1_pre': out['v_norm1_pre'], 'v_norm1_post': out['v_norm1_post'], 'v_w_in': out['v_w_in'], 'v_w_spatial': out['v_w_spatial'], 'v_b_spatial': out['v_b_spatial'], 'v_ln_v_gain': out['v_ln_v_gain'], 'v_ln_v_bias': out['v_ln_v_bias'], 'v_w_pool': out['v_w_pool'], 'v_b_pool': out['v_b_pool'], 'v_pool_scale': out['v_pool_scale'], 'v_w_out': out['v_w_out'], 'v_norm2_pre': out['v_norm2_pre'], 'v_norm2_post': out['v_norm2_post'], 'v_w_fc1': out['v_w_fc1'], 'v_w_fc2': out['v_w_fc2']}


def _loss(weights, diff, rest, loss_target):
    with _jax.named_scope("forward"):
        args = {**rest, TWIN_DIFF_INPUT: diff, **{k: w.astype(_WEIGHT_DTYPES[k]) for k, w in weights.items()}}
        y = _forward(args)
    with _jax.named_scope("loss_head"):
        err = _jnp.square(y.astype(_jnp.float32) - loss_target)
        return 0.5 * _jnp.sum(_jnp.mean(err, axis=-1)) if err.ndim else 0.5 * err


def _adamw(w, g, m, v):
    m = ADAM_B1 * m + (1.0 - ADAM_B1) * g
    v = ADAM_B2 * v + (1.0 - ADAM_B2) * _jnp.square(g)
    m_hat = m / (1.0 - ADAM_B1 ** ADAM_STEP)
    v_hat = v / (1.0 - ADAM_B2 ** ADAM_STEP)
    delta = -ADAM_LR * (m_hat / (_jnp.sqrt(v_hat) + ADAM_EPS) + ADAM_WD * w)
    return delta, m, v


def reference(x, c, w_ada, b_ada, norm1_pre, norm1_post, w_in, w_spatial, b_spatial, ln_v_gain, ln_v_bias, w_pool, b_pool, pool_scale, w_out, norm2_pre, norm2_post, w_fc1, w_fc2, loss_target, m_w_ada, m_b_ada, m_norm1_pre, m_norm1_post, m_w_in, m_w_spatial, m_b_spatial, m_ln_v_gain, m_ln_v_bias, m_w_pool, m_b_pool, m_pool_scale, m_w_out, m_norm2_pre, m_norm2_post, m_w_fc1, m_w_fc2, v_w_ada, v_b_ada, v_norm1_pre, v_norm1_post, v_w_in, v_w_spatial, v_b_spatial, v_ln_v_gain, v_ln_v_bias, v_w_pool, v_b_pool, v_pool_scale, v_w_out, v_norm2_pre, v_norm2_post, v_w_fc1, v_w_fc2):
    given = dict(x=x, c=c, w_ada=w_ada, b_ada=b_ada, norm1_pre=norm1_pre, norm1_post=norm1_post, w_in=w_in, w_spatial=w_spatial, b_spatial=b_spatial, ln_v_gain=ln_v_gain, ln_v_bias=ln_v_bias, w_pool=w_pool, b_pool=b_pool, pool_scale=pool_scale, w_out=w_out, norm2_pre=norm2_pre, norm2_post=norm2_post, w_fc1=w_fc1, w_fc2=w_fc2, loss_target=loss_target, m_w_ada=m_w_ada, m_b_ada=m_b_ada, m_norm1_pre=m_norm1_pre, m_norm1_post=m_norm1_post, m_w_in=m_w_in, m_w_spatial=m_w_spatial, m_b_spatial=m_b_spatial, m_ln_v_gain=m_ln_v_gain, m_ln_v_bias=m_ln_v_bias, m_w_pool=m_w_pool, m_b_pool=m_b_pool, m_pool_scale=m_pool_scale, m_w_out=m_w_out, m_norm2_pre=m_norm2_pre, m_norm2_post=m_norm2_post, m_w_fc1=m_w_fc1, m_w_fc2=m_w_fc2, v_w_ada=v_w_ada, v_b_ada=v_b_ada, v_norm1_pre=v_norm1_pre, v_norm1_post=v_norm1_post, v_w_in=v_w_in, v_w_spatial=v_w_spatial, v_b_spatial=v_b_spatial, v_ln_v_gain=v_ln_v_gain, v_ln_v_bias=v_ln_v_bias, v_w_pool=v_w_pool, v_b_pool=v_b_pool, v_pool_scale=v_pool_scale, v_w_out=v_w_out, v_norm2_pre=v_norm2_pre, v_norm2_post=v_norm2_post, v_w_fc1=v_w_fc1, v_w_fc2=v_w_fc2)
    weights = {n: given[n] for n in TWIN_WEIGHTS}
    shared = {n: given[n] for n in SHARED_INPUTS}
    per_example = {n: given[n] for n in ['x', 'c']}
    grad_fn = _jax.value_and_grad(_loss, argnums=(0, 1))

    def one_microbatch(ex, loss_target):
        ex = dict(ex)
        diff = ex.pop(TWIN_DIFF_INPUT)
        return grad_fn(weights, diff, {**shared, **ex}, loss_target)

    if N_MICROBATCH == 1:
        loss, (grad_w, grad_x) = one_microbatch(per_example, given["loss_target"])
    else:
        def body(carry, xs):
            loss_sum, grad_sum = carry
            l_k, (gw_k, gx_k) = one_microbatch(xs[0], xs[1])
            with _jax.named_scope("update"):
                return (loss_sum + l_k, _jax.tree.map(_jnp.add, grad_sum, gw_k)), gx_k

        init = (_jnp.zeros((), _jnp.float32), _jax.tree.map(_jnp.zeros_like, weights))
        (loss, grad_w), grad_x = _jax.lax.scan(body, init, (per_example, given["loss_target"]))
    with _jax.named_scope("update"):
        delta_w, new_m, new_v = {}, {}, {}
        for n in TWIN_WEIGHTS:
            delta_w[n], new_m[n], new_v[n] = _adamw(weights[n], grad_w[n], given["m_" + n], given["v_" + n])
    return (loss, grad_x, *[grad_w[n] for n in TWIN_WEIGHTS], *[delta_w[n] for n in TWIN_WEIGHTS],
            *[new_m[n] for n in TWIN_WEIGHTS], *[new_v[n] for n in TWIN_WEIGHTS])
```

```python
import functools

import jax
import jax.numpy as jnp
from jax import lax
from jax.experimental import pallas as pl
from jax.experimental.pallas import tpu as pltpu

F32 = jnp.float32
BF16 = jnp.bfloat16
MESH = pl.DeviceIdType.MESH

N_DEV = 8
D = 1024
D_A = 512
D_B = 512
D_Z = 2 * D_A + D_B
N_HEADS = 4
CHUNK = 128
WINDOWS = (2, 4, 8, 16)
GROUP = 128
D_FF = 4096
FF_BLK = D_FF // N_DEV
HALO = 16
EPS = 1e-6
VMEM_LIMIT = 60 * 1024 * 1024

ADAM_LR = 0.001
ADAM_B1 = 0.9
ADAM_B2 = 0.999
ADAM_EPS = 1e-08
ADAM_WD = 0.01
ADAM_STEP = 10

ROW_DMOD = 0
ROW_N1PRE, ROW_N1POST, ROW_N2PRE, ROW_N2POST = 8, 9, 10, 11
ROW_LN = 12
ROW_POOL = 13
ROW_LOSS = 14
ROW_BS = 16
ROW_WS = 80
ROW_WP = 144
SMALL_ROWS = 208


def _dot(a, b):
    return jnp.dot(a, b, preferred_element_type=F32)


def _dot_nt(a, b):
    return lax.dot_general(a, b, (((1,), (1,)), ((), ())), preferred_element_type=F32)


def _dot_tn(a, b):
    return lax.dot_general(a, b, (((0,), (0,)), ((), ())), preferred_element_type=F32)


def _rstd(v):
    return lax.rsqrt(jnp.mean(v * v, axis=-1, keepdims=True) + EPS)


def _rms_bwd(d_hat, hat, rstd):
    return rstd * (d_hat - hat * jnp.mean(d_hat * hat, axis=-1, keepdims=True))


_K0 = 0.7978845608028654
_K1 = 0.044715


def _gelu_parts(v):
    t = jnp.tanh(_K0 * (v + _K1 * (v * v * v)))
    return t, v * (0.5 * (1.0 + t))


def _gelu_grad(v, t):
    return 0.5 * (1.0 + t) + (0.5 * v) * (1.0 - t * t) * (_K0 * (1.0 + (3.0 * _K1) * (v * v)))


def _colsum(v):
    return jnp.sum(v, axis=0, keepdims=True)


def _full(shape):
    n = len(shape)
    return pl.BlockSpec(shape, lambda *_: (0,) * n)


def _resident(shape):
    n = len(shape)
    return pl.BlockSpec(shape, lambda *_: (0,) * n, pipeline_mode=pl.Buffered(1))


def _place():
    x, y, c = lax.axis_index("x"), lax.axis_index("y"), lax.axis_index("c")
    return x, y, c


def _flip(v, bit):
    return 1 - v if bit else v


def _peer(x, y, c, k):
    return (_flip(x, (k >> 2) & 1), _flip(y, (k >> 1) & 1), _flip(c, k & 1))


def _index(p):
    return 4 * p[0] + 2 * p[1] + p[2]


def _ada_fwd(c8, w_ada, b_my):
    ncol = w_ada.shape[1]

    def body(c_ref, w_ref, b_ref, modp_ref, sc_ref, cg, mg, part, send_sems, recv_sems):
        x, y, c = _place()
        me = _index((x, y, c))

        def c_copy(k):
            p = _peer(x, y, c, k)
            return pltpu.make_async_remote_copy(
                src_ref=c_ref, dst_ref=cg.at[me], send_sem=send_sems.at[k - 1], recv_sem=recv_sems.at[k - 1],
                device_id=p, device_id_type=MESH)

        def c_arrival(k):
            p = _peer(x, y, c, k)
            return pltpu.make_async_remote_copy(
                src_ref=c_ref, dst_ref=cg.at[_index(p)], send_sem=send_sems.at[k - 1], recv_sem=recv_sems.at[k - 1],
                device_id=p, device_id_type=MESH)

        def m_copy(k):
            p = _peer(x, y, c, k)
            return pltpu.make_async_remote_copy(
                src_ref=part, dst_ref=mg.at[me], send_sem=send_sems.at[6 + k], recv_sem=recv_sems.at[6 + k],
                device_id=p, device_id_type=MESH)

        def m_arrival(k):
            p = _peer(x, y, c, k)
            return pltpu.make_async_remote_copy(
                src_ref=part, dst_ref=mg.at[_index(p)], send_sem=send_sems.at[6 + k], recv_sem=recv_sems.at[6 + k],
                device_id=p, device_id_type=MESH)

        for k in range(1, N_DEV):
            c_copy(k).start()
        cg[me] = c_ref[...]
        for k in range(1, N_DEV):
            c_arrival(k).wait_recv()
        c_all = jnp.concatenate([cg[j, 0:1, :] for j in range(N_DEV)], axis=0)
        sc = c_all * jax.nn.sigmoid(c_all)
        sc_ref[...] = sc
        part[...] = _dot(sc.astype(BF16), w_ref[...].astype(BF16)) + b_ref[...]
        for k in range(1, N_DEV):
            m_copy(k).start()
        mg[me] = part[...]
        for k in range(1, N_DEV):
            m_arrival(k).wait_recv()
        for j in range(N_DEV):
            modp_ref[j:j + 1, :] = mg[j, pl.ds(me, 1), :]
        for k in range(1, N_DEV):
            c_copy(k).wait_send()
            m_copy(k).wait_send()

    vm = pl.BlockSpec(memory_space=pltpu.VMEM)
    return pl.pallas_call(
        body, name="ada_fwd",
        out_shape=(jax.ShapeDtypeStruct((N_DEV, ncol), F32), jax.ShapeDtypeStruct((N_DEV, D), F32)),
        in_specs=[vm, vm, vm], out_specs=(vm, vm),
        scratch_shapes=[
            pltpu.VMEM((N_DEV, 8, D), F32),
            pltpu.VMEM((N_DEV, N_DEV, ncol), F32),
            pltpu.VMEM((N_DEV, ncol), F32),
            pltpu.SemaphoreType.DMA((2 * (N_DEV - 1),)),
            pltpu.SemaphoreType.DMA((2 * (N_DEV - 1),)),
        ],
    )(c8, w_ada, b_my)


def _two_level_gather(x, y, c, out_refs, send_sems, recv_sems):
    me = (x, y, c)
    sibling = (x, y, 1 - c)
    chips = [(1 - x, y), (x, 1 - y), (1 - x, 1 - y)]
    n = len(out_refs)

    def copy(a, k, block, to):
        ref = out_refs[a].at[_index(block)]
        return pltpu.make_async_remote_copy(
            src_ref=ref, dst_ref=ref, send_sem=send_sems.at[7 * a + k], recv_sem=recv_sems.at[7 * a + k],
            device_id=to, device_id_type=MESH)

    first = []
    for a in range(n):
        first.append(copy(a, 0, me, sibling))
        first += [copy(a, 1 + j, me, (*chip, c)) for j, chip in enumerate(chips)]
    for cp in first:
        cp.start()
    passed = []
    for a in range(n):
        for j, chip in enumerate(chips):
            copy(a, 1 + j, (*chip, c), me).wait_recv()
            fwd = copy(a, 4 + j, (*chip, c), sibling)
            fwd.start()
            passed.append(fwd)
    for a in range(n):
        copy(a, 0, sibling, me).wait_recv()
        for j, chip in enumerate(chips):
            copy(a, 4 + j, (*chip, 1 - c), me).wait_recv()
    for cp in first + passed:
        cp.wait_send()


def _ag_weights(shards):
    n = len(shards)

    def body(*refs):
        in_refs, out_refs = refs[:n], refs[n:2 * n]
        send_sems, recv_sems = refs[2 * n], refs[2 * n + 1]
        x, y, c = _place()
        me = _index((x, y, c))
        for a in range(n):
            out_refs[a][me] = in_refs[a][...].astype(BF16)
        _two_level_gather(x, y, c, out_refs, send_sems, recv_sems)

    vm = pl.BlockSpec(memory_space=pltpu.VMEM)
    return pl.pallas_call(
        body, name="ag_weights",
        out_shape=tuple(jax.ShapeDtypeStruct((N_DEV,) + s.shape, BF16) for s in shards),
        in_specs=[vm] * n, out_specs=tuple([vm] * n),
        scratch_shapes=[pltpu.SemaphoreType.DMA((7 * n,)), pltpu.SemaphoreType.DMA((7 * n,))],
        compiler_params=pltpu.CompilerParams(vmem_limit_bytes=VMEM_LIMIT),
    )(*shards)


def _small_ag(pack):
    def body(in_ref, out_ref, send_sems, recv_sems):
        x, y, c = _place()
        out_ref[_index((x, y, c))] = in_ref[...]
        _two_level_gather(x, y, c, [out_ref], send_sems, recv_sems)

    vm = pl.BlockSpec(memory_space=pltpu.VMEM)
    return pl.pallas_call(
        body, name="small_ag",
        out_shape=jax.ShapeDtypeStruct((N_DEV,) + pack.shape, F32),
        in_specs=[vm], out_specs=vm,
        scratch_shapes=[pltpu.SemaphoreType.DMA((7,)), pltpu.SemaphoreType.DMA((7,))],
    )(pack)


def _rs_partials(name, parts, row_chunk):
    n = len(parts)

    def body(*refs):
        p_refs, g_refs = refs[:n], refs[n:2 * n]
        from_sib = refs[2 * n:3 * n]
        chip_out = refs[3 * n:4 * n]
        chip_in = refs[4 * n:5 * n]
        send_a, recv_a, send_b, recv_b = refs[5 * n:5 * n + 4]
        x, y, c = _place()
        sibling = (x, y, 1 - c)
        my_chip = 2 * x + y
        others = [(1 - x, y), (x, 1 - y), (1 - x, 1 - y)]

        def to_sibling(a, k):
            return pltpu.make_async_remote_copy(
                src_ref=p_refs[a].at[2 * k + (1 - c)], dst_ref=from_sib[a].at[k],
                send_sem=send_a.at[a], recv_sem=recv_a.at[a], device_id=sibling, device_id_type=MESH)

        def all_from_sibling(a):
            return pltpu.make_async_remote_copy(
                src_ref=from_sib[a], dst_ref=from_sib[a], send_sem=send_a.at[a], recv_sem=recv_a.at[a],
                device_id=sibling, device_id_type=MESH)

        def to_chip(a, r):
            return pltpu.make_async_remote_copy(
                src_ref=chip_out[a].at[r], dst_ref=chip_in[a].at[r],
                send_sem=send_b.at[3 * a + r], recv_sem=recv_b.at[3 * a + r],
                device_id=(*others[r], c), device_id_type=MESH)

        for a in range(n):
            for k in range(4):
                to_sibling(a, k).start()
        for a in range(n):
            all_from_sibling(a).wait_recv()
            rows = p_refs[a].shape[1]
            for r in range(3):
                k = 2 * others[r][0] + others[r][1]
                for s in range(0, rows, row_chunk):
                    sl = pl.ds(s, row_chunk)
                    chip_out[a][r, sl, :] = (p_refs[a][2 * k + c, sl, :].astype(F32)
                                             + from_sib[a][k, sl, :].astype(F32)).astype(BF16)
                to_chip(a, r).start()
            for s in range(0, rows, row_chunk):
                sl = pl.ds(s, row_chunk)
                g_refs[a][sl, :] = (p_refs[a][2 * my_chip + c, sl, :].astype(F32)
                                    + from_sib[a][my_chip, sl, :].astype(F32))
        for a in range(n):
            rows = p_refs[a].shape[1]
            for r in range(3):
                to_chip(a, r).wait_recv()
                for s in range(0, rows, row_chunk):
                    sl = pl.ds(s, row_chunk)
                    g_refs[a][sl, :] = g_refs[a][sl, :] + chip_in[a][r, sl, :].astype(F32)
        for a in range(n):
            all_from_sibling(a).wait_send()
            for r in range(3):
                to_chip(a, r).wait_send()

    vm = pl.BlockSpec(memory_space=pltpu.VMEM)
    return pl.pallas_call(
        body, name=name,
        out_shape=tuple(jax.ShapeDtypeStruct(p.shape[1:], F32) for p in parts),
        in_specs=[vm] * n, out_specs=tuple([vm] * n),
        scratch_shapes=(
            [pltpu.VMEM((4,) + p.shape[1:], BF16) for p in parts]
            + [pltpu.VMEM((3,) + p.shape[1:], BF16) for p in parts]
            + [pltpu.VMEM((3,) + p.shape[1:], BF16) for p in parts]
            + [pltpu.SemaphoreType.DMA((n,)), pltpu.SemaphoreType.DMA((n,)),
               pltpu.SemaphoreType.DMA((3 * n,)), pltpu.SemaphoreType.DMA((3 * n,))]),
        compiler_params=pltpu.CompilerParams(vmem_limit_bytes=VMEM_LIMIT),
    )(*parts)


def _tril_mask():
    row = lax.broadcasted_iota(jnp.int32, (CHUNK, CHUNK), 0)
    col = lax.broadcasted_iota(jnp.int32, (CHUNK, CHUNK), 1)
    return (col <= row).astype(F32)


def _window_sums(ext):
    s2 = ext + pltpu.roll(ext, 1, 0)
    t4 = s2[:, GROUP:]
    s4 = t4 + pltpu.roll(t4, 2, 0)
    t8 = s4[:, GROUP:]
    s8 = t8 + pltpu.roll(t8, 4, 0)
    t16 = s8[:, GROUP:]
    s16 = t16 + pltpu.roll(t16, 8, 0)
    return [s2[:, :GROUP], s4[:, :GROUP], s8[:, :GROUP], s16]


def _inv_counts(first_pos, rows):
    pos = first_pos + lax.broadcasted_iota(jnp.int32, (rows, 1), 0)
    return [1.0 / jnp.minimum(pos + 1, w).astype(F32) for w in WINDOWS]


def _pool_diff(zb, halo, first_pos):
    tt = zb.shape[0]
    sums = _window_sums(jnp.concatenate([halo, zb], axis=0))
    inv = _inv_counts(first_pos, tt)
    return [sums[g][HALO:, :] * inv[g] - zb[:, g * GROUP:(g + 1) * GROUP] for g in range(len(WINDOWS))]


def _attn_fwd(tt, x, mod, n1pre, n1post, w_in_t, w_out, w_sp, bs_rows, ln_g, ln_b, w_pool, b_pool, pool_scale):
    t_len = x.shape[0]
    nt = t_len // tt

    def body(x_ref, mod_ref, n1pre_ref, n1post_ref, win_ref, wout_ref, wsp_ref, bs_ref, lng_ref, lnb_ref,
             wp_ref, bp_ref, ps_ref, z_ref, cat_ref, mix_ref, x1_ref, carry):
        i = pl.program_id(0)

        @pl.when(i == 0)
        def _():
            carry[...] = jnp.zeros_like(carry)

        xv = x_ref[...]
        shift1, scale1, gate1 = mod_ref[0:1, :], mod_ref[1:2, :], mod_ref[2:3, :]
        h1 = (xv * _rstd(xv) * n1pre_ref[...]) * (1.0 + scale1) + shift1
        z = _dot_nt(h1.astype(BF16), win_ref[...])
        z_ref[...] = z

        _, ga = _gelu_parts(z[:, :2 * D_A])
        u, vr = ga[:, :D_A], ga[:, D_A:]
        dv = vr - jnp.mean(vr, axis=-1, keepdims=True)
        v = (dv * lax.rsqrt(jnp.mean(dv * dv, axis=-1, keepdims=True) + EPS)) * lng_ref[...] + lnb_ref[...]
        vb = v.astype(BF16)
        mask = _tril_mask()
        wc = [(wsp_ref[h] * mask).astype(BF16) for h in range(N_HEADS)]
        for ch in range(tt // CHUNK):
            rows = slice(ch * CHUNK, (ch + 1) * CHUNK)
            for h in range(N_HEADS):
                cols = slice(h * GROUP, (h + 1) * GROUP)
                mixed = _dot(wc[h], vb[rows, cols]) + bs_ref[:, cols]
                cat_ref[rows, cols] = (u[rows, cols] * mixed).astype(BF16)

        zb = z[:, 2 * D_A:]
        diff = _pool_diff(zb, carry[...], i * tt)
        carry[...] = zb[tt - HALO:, :]
        for g in range(len(WINDOWS)):
            cols = slice(g * GROUP, (g + 1) * GROUP)
            pre = _dot(diff[g].astype(BF16), wp_ref[g].astype(BF16)) + bp_ref[:, cols]
            cat_ref[:, D_A + g * GROUP:D_A + (g + 1) * GROUP] = (pre * ps_ref[:, cols]).astype(BF16)

        mix = _dot(cat_ref[...], wout_ref[...])
        mix_ref[...] = mix
        x1_ref[...] = xv + gate1 * (mix * _rstd(mix) * n1post_ref[...])

    tile = lambda w: pl.BlockSpec((tt, w), lambda i: (i, 0))
    return pl.pallas_call(
        body, name="attn_fwd", grid=(nt,),
        out_shape=(jax.ShapeDtypeStruct((t_len, D_Z), F32), jax.ShapeDtypeStruct((t_len, D), BF16),
                   jax.ShapeDtypeStruct((t_len, D), F32), jax.ShapeDtypeStruct((t_len, D), F32)),
        in_specs=[tile(D), _full((8, D)), _full((1, D)), _full((1, D)), _resident((D_Z, D)), _resident((D, D)),
                  _full((N_HEADS, CHUNK, CHUNK)), _full((CHUNK, D_A)), _full((1, D_A)), _full((1, D_A)),
                  _full((len(WINDOWS), GROUP, GROUP)), _full((1, D_B)), _full((1, D_B))],
        out_specs=(tile(D_Z), tile(D), tile(D), tile(D)),
        scratch_shapes=[pltpu.VMEM((HALO, D_B), F32)],
        compiler_params=pltpu.CompilerParams(dimension_semantics=("arbitrary",), vmem_limit_bytes=VMEM_LIMIT),
    )(x, mod, n1pre, n1post, w_in_t, w_out, w_sp, bs_rows, ln_g, ln_b, w_pool, b_pool, pool_scale)


def _mlp_fwd(tt, x1, tgt, mod, n2pre, n2post, w1g, w2):
    t_len = x1.shape[0]
    nt = t_len // tt

    def body(x1_ref, tgt_ref, mod_ref, n2pre_ref, n2post_ref, w1_ref, w2_ref,
             a_ref, h2_ref, df_ref, dy_ref, red_ref, f_acc):
        i = pl.program_id(0)

        @pl.when(i == 0)
        def _():
            red_ref[...] = jnp.zeros_like(red_ref)

        x1v = x1_ref[...]
        shift2, scale2, gate2 = mod_ref[3:4, :], mod_ref[4:5, :], mod_ref[5:6, :]
        h2 = ((x1v * _rstd(x1v) * n2pre_ref[...]) * (1.0 + scale2) + shift2).astype(BF16)
        h2_ref[...] = h2
        for j in range(N_DEV):
            cols = slice(j * FF_BLK, (j + 1) * FF_BLK)
            a = _dot(h2, w1_ref[j])
            a_ref[:, cols] = a
            r = jnp.maximum(a, 0.0)
            contrib = _dot((r * r).astype(BF16), w2_ref[cols, :])
            if j == 0:
                f_acc[...] = contrib
            else:
                f_acc[...] += contrib
        f = f_acc[...]
        rf = _rstd(f)
        fhat = f * rf
        nf = fhat * n2post_ref[...]
        err = (x1v + gate2 * nf) - tgt_ref[...]
        dy = err * (1.0 / D)
        dy_ref[...] = dy
        dnf = dy * gate2
        df_ref[...] = _rms_bwd(dnf * n2post_ref[...], fhat, rf).astype(BF16)
        red_ref[0:1, :] += _colsum(dy * nf)
        red_ref[1:2, :] += _colsum(dnf * fhat)
        red_ref[2:3, :] += _colsum(0.5 * jnp.mean(err * err, axis=-1, keepdims=True)) * jnp.ones((1, D), F32)

    tile = lambda w: pl.BlockSpec((tt, w), lambda i: (i, 0))
    return pl.pallas_call(
        body, name="mlp_fwd", grid=(nt,),
        out_shape=(jax.ShapeDtypeStruct((t_len, D_FF), F32), jax.ShapeDtypeStruct((t_len, D), BF16),
                   jax.ShapeDtypeStruct((t_len, D), BF16), jax.ShapeDtypeStruct((t_len, D), F32),
                   jax.ShapeDtypeStruct((8, D), F32)),
        in_specs=[tile(D), tile(D), _full((8, D)), _full((1, D)), _full((1, D)),
                  _resident((N_DEV, D, FF_BLK)), _resident((D_FF, D))],
        out_specs=(tile(D_FF), tile(D), tile(D), tile(D), _full((8, D))),
        scratch_shapes=[pltpu.VMEM((tt, D), F32)],
        compiler_params=pltpu.CompilerParams(dimension_semantics=("arbitrary",), vmem_limit_bytes=VMEM_LIMIT),
    )(x1, tgt, mod, n2pre, n2post, w1g, w2)


def _mlp_bwd(tt, df, a, dy, x1, mix, mod, n2pre, n1post, w1g, w2):
    t_len = x1.shape[0]
    nt = t_len // tt

    def body(df_ref, a_ref, dy_ref, x1_ref, mix_ref, mod_ref, n2pre_ref, n1post_ref, w1_ref, w2_ref,
             da_ref, r_ref, dmix_ref, dx1_ref, red_ref, dh2_acc):
        i = pl.program_id(0)

        @pl.when(i == 0)
        def _():
            red_ref[...] = jnp.zeros_like(red_ref)

        dfv = df_ref[...]
        for j in range(N_DEV):
            cols = slice(j * FF_BLK, (j + 1) * FF_BLK)
            dr = _dot_nt(dfv, w2_ref[cols, :])
            ra = jnp.maximum(a_ref[:, cols], 0.0)
            da = (dr * (2.0 * ra)).astype(BF16)
            da_ref[:, cols] = da
            r_ref[:, cols] = (ra * ra).astype(BF16)
            contrib = _dot_nt(da, w1_ref[j])
            if j == 0:
                dh2_acc[...] = contrib
            else:
                dh2_acc[...] += contrib
        dh2 = dh2_acc[...]
        gate1, scale2 = mod_ref[2:3, :], mod_ref[4:5, :]
        x1v = x1_ref[...]
        r2 = _rstd(x1v)
        xhat = x1v * r2
        n2 = xhat * n2pre_ref[...]
        dn2 = dh2 * (1.0 + scale2)
        dx1 = dy_ref[...] + _rms_bwd(dn2 * n2pre_ref[...], xhat, r2)
        dx1_ref[...] = dx1
        mixv = mix_ref[...]
        rm = _rstd(mixv)
        mhat = mixv * rm
        dnm = dx1 * gate1
        dmix_ref[...] = _rms_bwd(dnm * n1post_ref[...], mhat, rm).astype(BF16)
        red_ref[0:1, :] += _colsum(dh2)
        red_ref[1:2, :] += _colsum(dh2 * n2)
        red_ref[2:3, :] += _colsum(dn2 * xhat)
        red_ref[3:4, :] += _colsum(dx1 * (mhat * n1post_ref[...]))
        red_ref[4:5, :] += _colsum(dnm * mhat)

    tile = lambda w: pl.BlockSpec((tt, w), lambda i: (i, 0))
    return pl.pallas_call(
        body, name="mlp_bwd", grid=(nt,),
        out_shape=(jax.ShapeDtypeStruct((t_len, D_FF), BF16), jax.ShapeDtypeStruct((t_len, D_FF), BF16),
                   jax.ShapeDtypeStruct((t_len, D), BF16), jax.ShapeDtypeStruct((t_len, D), F32),
                   jax.ShapeDtypeStruct((8, D), F32)),
        in_specs=[tile(D), tile(D_FF), tile(D), tile(D), tile(D), _full((8, D)), _full((1, D)), _full((1, D)),
                  _resident((N_DEV, D, FF_BLK)), _resident((D_FF, D))],
        out_specs=(tile(D_FF), tile(D_FF), tile(D), tile(D), _full((8, D))),
        scratch_shapes=[pltpu.VMEM((tt, D), F32)],
        compiler_params=pltpu.CompilerParams(dimension_semantics=("arbitrary",), vmem_limit_bytes=VMEM_LIMIT),
    )(df, a, dy, x1, mix, mod, n2pre, n1post, w1g, w2)


def _mlp_wgrad(tt, r, da, df, h2):
    t_len = df.shape[0]
    nt = t_len // tt

    def body(r_ref, da_ref, df_ref, h2_ref, g1_ref, g2_ref, acc1, acc2):
        t = pl.program_id(1)
        rows = pl.ds(pl.multiple_of(t * tt, tt), tt)

        @pl.when(t == 0)
        def _():
            acc2[...] = jnp.zeros_like(acc2)
            acc1[...] = jnp.zeros_like(acc1)

        acc2[...] += _dot_tn(r_ref[...], df_ref[rows, :])
        acc1[...] += _dot_tn(h2_ref[rows, :], da_ref[...])

        @pl.when(t == nt - 1)
        def _():
            g2_ref[0] = acc2[...].astype(BF16)
            g1_ref[0] = acc1[...].astype(BF16)

    blk = pl.BlockSpec((tt, FF_BLK), lambda j, t: (t, j))
    return pl.pallas_call(
        body, name="mlp_wgrad", grid=(N_DEV, nt),
        out_shape=(jax.ShapeDtypeStruct((N_DEV, D, FF_BLK), BF16), jax.ShapeDtypeStruct((N_DEV, FF_BLK, D), BF16)),
        in_specs=[blk, blk, _resident((t_len, D)), _resident((t_len, D))],
        out_specs=(pl.BlockSpec((1, D, FF_BLK), lambda j, t: (j, 0, 0)),
                   pl.BlockSpec((1, FF_BLK, D), lambda j, t: (j, 0, 0))),
        scratch_shapes=[pltpu.VMEM((D, FF_BLK), F32), pltpu.VMEM((FF_BLK, D), F32)],
        compiler_params=pltpu.CompilerParams(dimension_semantics=("arbitrary", "arbitrary"),
                                             vmem_limit_bytes=VMEM_LIMIT),
    )(r, da, df, h2)


def _acc_rows(ref, row0, k, val):
    half = CHUNK // 2
    ref[row0:row0 + half, k * GROUP:(k + 1) * GROUP] += val[:half, :]
    ref[row0:row0 + half, D_A + k * GROUP:D_A + (k + 1) * GROUP] += val[half:, :]


def _attn_bwd(tt, dmix, dx1, x, z, cat, mod, n1pre, w_in_t, w_out, w_sp, bs_rows, ln_g, ln_b, w_pool, b_pool,
              pool_scale, red_fwd, red_bwd):
    t_len = x.shape[0]
    nt = t_len // tt
    hb = tt // HALO

    def body(dmix_ref, dx1_ref, x_ref, z_ref, zprev_ref, cat_ref, mod_ref, n1pre_ref, win_ref, wout_ref, wsp_ref,
             bs_ref, lng_ref, lnb_ref, wp_ref, bp_ref, ps_ref, redf_ref, redb_ref,
             gx_ref, gwin_ref, gwout_ref, small_ref, carry, acc_in, acc_out, dz_scr):
        s = pl.program_id(0)
        i = nt - 1 - s

        @pl.when(s == 0)
        def _():
            carry[...] = jnp.zeros_like(carry)
            acc_in[...] = jnp.zeros_like(acc_in)
            acc_out[...] = jnp.zeros_like(acc_out)
            small_ref[...] = jnp.zeros_like(small_ref)
            small_ref[ROW_DMOD + 2:ROW_DMOD + 3, :] = redb_ref[3:4, :]
            small_ref[ROW_DMOD + 3:ROW_DMOD + 5, :] = redb_ref[0:2, :]
            small_ref[ROW_DMOD + 5:ROW_DMOD + 6, :] = redf_ref[0:1, :]
            small_ref[ROW_N1POST:ROW_N1POST + 1, :] = redb_ref[4:5, :]
            small_ref[ROW_N2PRE:ROW_N2PRE + 1, :] = redb_ref[2:3, :]
            small_ref[ROW_N2POST:ROW_N2POST + 1, :] = redf_ref[1:2, :]
            small_ref[ROW_LOSS:ROW_LOSS + 1, :] = redf_ref[2:3, :]

        dmixv = dmix_ref[...]
        dcat = _dot_nt(dmixv, wout_ref[...])
        acc_out[...] += _dot_tn(cat_ref[...], dmixv)

        z = z_ref[...]
        t_g, ga = _gelu_parts(z[:, :2 * D_A])
        u, vr = ga[:, :D_A], ga[:, D_A:]
        dv0 = vr - jnp.mean(vr, axis=-1, keepdims=True)
        rv = lax.rsqrt(jnp.mean(dv0 * dv0, axis=-1, keepdims=True) + EPS)
        vhat = dv0 * rv
        vb = (vhat * lng_ref[...] + lnb_ref[...]).astype(BF16)
        mask = _tril_mask()
        wc = [(wsp_ref[h] * mask).astype(BF16) for h in range(N_HEADS)]

        dya = dcat[:, :D_A]
        for h in range(N_HEADS):
            cols = slice(h * GROUP, (h + 1) * GROUP)
            bs_sum = jnp.zeros((CHUNK, GROUP), F32)
            ws_sum = jnp.zeros((CHUNK, CHUNK), F32)
            for ch in range(tt // CHUNK):
                rows = slice(ch * CHUNK, (ch + 1) * CHUNK)
                v_ch = vb[rows, cols]
                mixed = _dot(wc[h], v_ch) + bs_ref[:, cols]
                dy_ch = dya[rows, cols]
                dz_scr[rows, cols] = dy_ch * mixed
                dmixed = dy_ch * u[rows, cols]
                dmb = dmixed.astype(BF16)
                dz_scr[rows, D_A + h * GROUP:D_A + (h + 1) * GROUP] = _dot_tn(wc[h], dmb)
                bs_sum = bs_sum + dmixed
                ws_sum = ws_sum + _dot_nt(dmb, v_ch)
            _acc_rows(small_ref, ROW_BS, h, bs_sum)
            _acc_rows(small_ref, ROW_WS, h, ws_sum)

        dvl = dz_scr[:, D_A:2 * D_A]
        dvhat = dvl * lng_ref[...]
        dvr = rv * (dvhat - jnp.mean(dvhat, axis=-1, keepdims=True)
                    - vhat * jnp.mean(dvhat * vhat, axis=-1, keepdims=True))
        small_ref[ROW_LN:ROW_LN + 1, 0:D_A] += _colsum(dvl * vhat)
        small_ref[ROW_LN:ROW_LN + 1, D_A:D] += _colsum(dvl)
        dga = jnp.concatenate([dz_scr[:, :D_A], dvr], axis=1)
        dza = dga * _gelu_grad(z[:, :2 * D_A], t_g)

        zb = z[:, 2 * D_A:]
        halo_prev = jnp.where(i == 0, 0.0, zprev_ref[...])
        diff = _pool_diff(zb, halo_prev, i * tt)
        dyb = dcat[:, D_A:]
        inv = _inv_counts(i * tt, tt)
        scaled, ddiffs = [], []
        for g in range(len(WINDOWS)):
            cols = slice(g * GROUP, (g + 1) * GROUP)
            db = diff[g].astype(BF16)
            wpg = wp_ref[g].astype(BF16)
            pre = _dot(db, wpg) + bp_ref[:, cols]
            small_ref[ROW_POOL:ROW_POOL + 1, cols] += _colsum(dyb[:, cols] * pre)
            dpre = dyb[:, cols] * ps_ref[:, cols]
            small_ref[ROW_POOL:ROW_POOL + 1, D_B + g * GROUP:D_B + (g + 1) * GROUP] += _colsum(dpre)
            dpb = dpre.astype(BF16)
            _acc_rows(small_ref, ROW_WP, g, _dot_tn(db, dpb))
            ddiff = _dot_nt(dpb, wpg)
            ddiffs.append(ddiff)
            scaled.append(ddiff * inv[g])
        scaled_all = jnp.concatenate(scaled, axis=1)
        ext = jnp.concatenate([scaled_all, carry[...]], axis=0)
        n_ext = tt + HALO
        s2 = ext + pltpu.roll(ext, n_ext - 1, 0)
        t4 = s2[:, GROUP:]
        s4 = t4 + pltpu.roll(t4, n_ext - 2, 0)
        t8 = s4[:, GROUP:]
        s8 = t8 + pltpu.roll(t8, n_ext - 4, 0)
        t16 = s8[:, GROUP:]
        s16 = t16 + pltpu.roll(t16, n_ext - 8, 0)
        back = [s2[:, :GROUP], s4[:, :GROUP], s8[:, :GROUP], s16]
        carry[...] = scaled_all[:HALO, :]
        dzb = jnp.concatenate([back[g][:tt, :] - ddiffs[g] for g in range(len(WINDOWS))], axis=1)

        dzv = jnp.concatenate([dza, dzb], axis=1).astype(BF16)
        dh1 = _dot(dzv, win_ref[...])
        xv = x_ref[...]
        r1 = _rstd(xv)
        xhat = xv * r1
        shift1, scale1 = mod_ref[0:1, :], mod_ref[1:2, :]
        n1 = xhat * n1pre_ref[...]
        h1 = (n1 * (1.0 + scale1) + shift1).astype(BF16)
        acc_in[...] += _dot_tn(dzv, h1)
        dn1 = dh1 * (1.0 + scale1)
        gx_ref[...] = dx1_ref[...] + _rms_bwd(dn1 * n1pre_ref[...], xhat, r1)
        small_ref[ROW_DMOD:ROW_DMOD + 1, :] += _colsum(dh1)
        small_ref[ROW_DMOD + 1:ROW_DMOD + 2, :] += _colsum(dh1 * n1)
        small_ref[ROW_N1PRE:ROW_N1PRE + 1, :] += _colsum(dn1 * xhat)

        @pl.when(s == nt - 1)
        def _():
            gwin_ref[...] = acc_in[...].astype(BF16)
            gwout_ref[...] = acc_out[...].astype(BF16)

    rev = lambda w: pl.BlockSpec((tt, w), lambda s: (nt - 1 - s, 0))
    zprev = pl.BlockSpec((HALO, D_B), lambda s: (jnp.maximum((nt - 1 - s) * hb - 1, 0), 2))
    return pl.pallas_call(
        body, name="attn_bwd", grid=(nt,),
        out_shape=(jax.ShapeDtypeStruct((t_len, D), F32), jax.ShapeDtypeStruct((D_Z, D), BF16),
                   jax.ShapeDtypeStruct((D, D), BF16), jax.ShapeDtypeStruct((SMALL_ROWS, D), F32)),
        in_specs=[rev(D), rev(D), rev(D), rev(D_Z), zprev, rev(D), _full((8, D)), _full((1, D)),
                  _resident((D_Z, D)), _resident((D, D)), _full((N_HEADS, CHUNK, CHUNK)), _full((CHUNK, D_A)),
                  _full((1, D_A)), _full((1, D_A)), _full((len(WINDOWS), GROUP, GROUP)), _full((1, D_B)),
                  _full((1, D_B)), _full((8, D)), _full((8, D))],
        out_specs=(rev(D), _full((D_Z, D)), _full((D, D)), _full((SMALL_ROWS, D))),
        scratch_shapes=[pltpu.VMEM((HALO, D_B), F32), pltpu.VMEM((D_Z, D), F32), pltpu.VMEM((D, D), F32),
                        pltpu.VMEM((tt, 2 * D_A), F32)],
        compiler_params=pltpu.CompilerParams(dimension_semantics=("arbitrary",), vmem_limit_bytes=VMEM_LIMIT),
    )(dmix, dx1, x, z, z, cat, mod, n1pre, w_in_t, w_out, w_sp, bs_rows, ln_g, ln_b, w_pool, b_pool, pool_scale,
      red_fwd, red_bwd)


def _adam(w, g, m, v):
    m2 = ADAM_B1 * m + (1.0 - ADAM_B1) * g
    v2 = ADAM_B2 * v + (1.0 - ADAM_B2) * (g * g)
    m_hat = m2 / (1.0 - ADAM_B1 ** ADAM_STEP)
    v_hat = v2 / (1.0 - ADAM_B2 ** ADAM_STEP)
    delta = -ADAM_LR * (m_hat / (jnp.sqrt(v_hat) + ADAM_EPS) + ADAM_WD * w)
    return delta, m2, v2


def _adamw_shard(name, rb, w, g, m, v):
    rows, cols = w.shape

    def body(w_ref, g_ref, m_ref, v_ref, d_ref, m2_ref, v2_ref):
        d_ref[...], m2_ref[...], v2_ref[...] = _adam(w_ref[...], g_ref[...], m_ref[...], v_ref[...])

    blk = pl.BlockSpec((rb, cols), lambda i: (i, 0))
    shp = jax.ShapeDtypeStruct((rows, cols), F32)
    return pl.pallas_call(
        body, name=name, grid=(rows // rb,), out_shape=(shp, shp, shp),
        in_specs=[blk] * 4, out_specs=(blk, blk, blk),
        compiler_params=pltpu.CompilerParams(dimension_semantics=("arbitrary",)),
    )(w, g, m, v)


def _adamw_ada(rb, w, sc, dmod_cols, m, v):
    rows, cols = w.shape

    def body(w_ref, sc_ref, dm_ref, m_ref, v_ref, g_ref, d_ref, m2_ref, v2_ref):
        g = _dot_tn(sc_ref[...].astype(BF16), dm_ref[...].astype(BF16))
        g_ref[...] = g
        d_ref[...], m2_ref[...], v2_ref[...] = _adam(w_ref[...], g, m_ref[...], v_ref[...])

    blk = pl.BlockSpec((rb, cols), lambda i: (i, 0))
    shp = jax.ShapeDtypeStruct((rows, cols), F32)
    return pl.pallas_call(
        body, name="adamw_ada", grid=(rows // rb,), out_shape=(shp, shp, shp, shp),
        in_specs=[blk, pl.BlockSpec((N_DEV, rb), lambda i: (0, i)), _full((N_DEV, cols)), blk, blk],
        out_specs=(blk, blk, blk, blk),
        compiler_params=pltpu.CompilerParams(dimension_semantics=("arbitrary",)),
    )(w, sc, dmod_cols, m, v)


def _unfold(acc_rows):
    return jnp.concatenate([acc_rows[:, :D_A], acc_rows[:, D_A:]], axis=0)


def _adamw_small(small_all, params):
    n = len(params)
    flat = [a for p in params for a in p]

    def body(*refs):
        s_ref = refs[0]
        p_refs = refs[1:1 + 3 * n]
        loss_ref = refs[1 + 3 * n]
        o_refs = refs[2 + 3 * n:]
        tot = s_ref[0]
        for j in range(1, N_DEV):
            tot = tot + s_ref[j]
        loss_ref[...] = jnp.broadcast_to(tot[ROW_LOSS:ROW_LOSS + 1, 0:GROUP], (8, GROUP))
        mask = _tril_mask()
        bs = _unfold(tot[ROW_BS:ROW_BS + 64, :])
        ws = _unfold(tot[ROW_WS:ROW_WS + 64, :])
        wp = _unfold(tot[ROW_WP:ROW_WP + 64, :])
        grads = [
            tot[ROW_DMOD:ROW_DMOD + 6, :],
            tot[ROW_N1PRE:ROW_N1PRE + 1, :], tot[ROW_N1POST:ROW_N1POST + 1, :],
            tot[ROW_N2PRE:ROW_N2PRE + 1, :], tot[ROW_N2POST:ROW_N2POST + 1, :],
            tot[ROW_LN:ROW_LN + 1, :D_A], tot[ROW_LN:ROW_LN + 1, D_A:],
            tot[ROW_POOL:ROW_POOL + 1, :D_B], tot[ROW_POOL:ROW_POOL + 1, D_B:],
            jnp.concatenate([jnp.sum(bs[:, h * GROUP:(h + 1) * GROUP].T, axis=0, keepdims=True)
                             for h in range(N_HEADS)], axis=0),
            jnp.stack([ws[:, h * GROUP:(h + 1) * GROUP] * mask for h in range(N_HEADS)]),
            jnp.stack([wp[:, g * GROUP:(g + 1) * GROUP] for g in range(len(WINDOWS))]),
        ]
        for k in range(n):
            w_ref, m_ref, v_ref = p_refs[3 * k:3 * k + 3]
            g = grads[k]
            o_refs[4 * k][...] = g
            o_refs[4 * k + 1][...], o_refs[4 * k + 2][...], o_refs[4 * k + 3][...] = _adam(
                w_ref[...], g, m_ref[...], v_ref[...])

    vm = pl.BlockSpec(memory_space=pltpu.VMEM)
    out_shape = [jax.ShapeDtypeStruct((8, GROUP), F32)]
    for w, _, _ in params:
        out_shape += [jax.ShapeDtypeStruct(w.shape, F32)] * 4
    return pl.pallas_call(
        body, name="adamw_small", out_shape=tuple(out_shape),
        in_specs=[vm] * (1 + 3 * n), out_specs=tuple([vm] * len(out_shape)),
    )(small_all, *flat)


TT_ATTN_FWD = 512
TT_MLP = 256
TT_WGRAD = 1024
TT_ATTN_BWD = 256


def kernel(x, c, w_ada, b_ada, norm1_pre, norm1_post, w_in, w_spatial, b_spatial, ln_v_gain, ln_v_bias, w_pool, b_pool, pool_scale, w_out, norm2_pre, norm2_post, w_fc1, w_fc2, loss_target, m_w_ada, m_b_ada, m_norm1_pre, m_norm1_post, m_w_in, m_w_spatial, m_b_spatial, m_ln_v_gain, m_ln_v_bias, m_w_pool, m_b_pool, m_pool_scale, m_w_out, m_norm2_pre, m_norm2_post, m_w_fc1, m_w_fc2, v_w_ada, v_b_ada, v_norm1_pre, v_norm1_post, v_w_in, v_w_spatial, v_b_spatial, v_ln_v_gain, v_ln_v_bias, v_w_pool, v_b_pool, v_pool_scale, v_w_out, v_norm2_pre, v_norm2_post, v_w_fc1, v_w_fc2):
    t_len = x.shape[1]
    me = 4 * lax.axis_index("x") + 2 * lax.axis_index("y") + lax.axis_index("c")
    ada_cols = w_ada.shape[1]
    tt = lambda want: min(want, t_len)

    x2 = x.reshape(t_len, D)
    tgt = loss_target.reshape(t_len, D)
    row = lambda a: a.reshape(1, -1)

    b_my = lax.dynamic_slice_in_dim(b_ada, me * ada_cols, ada_cols).reshape(1, ada_cols)
    modp, sc = _ada_fwd(jnp.broadcast_to(c, (8, D)), w_ada, b_my)
    mod = jnp.concatenate([modp.reshape(6, D), jnp.zeros((2, D), F32)], axis=0)

    g_in, g_out, g_w1, g_w2 = _ag_weights([w_in.T, w_out, w_fc1, w_fc2])
    w_in_t = g_in.reshape(D_Z, D)
    w_out_all = g_out.reshape(D, D)
    w2_all = g_w2.reshape(D_FF, D)

    bs_rows = jnp.repeat(b_spatial.T, GROUP, axis=1)
    attn_consts = (w_spatial, bs_rows, row(ln_v_gain), row(ln_v_bias), w_pool, row(b_pool), row(pool_scale))

    z, cat, mix, x1 = _attn_fwd(tt(TT_ATTN_FWD), x2, mod, row(norm1_pre), row(norm1_post), w_in_t, w_out_all,
                                *attn_consts)
    a, h2, df, dy, red_fwd = _mlp_fwd(tt(TT_MLP), x1, tgt, mod, row(norm2_pre), row(norm2_post), g_w1, w2_all)
    da, r, dmix, dx1, red_bwd = _mlp_bwd(tt(TT_MLP), df, a, dy, x1, mix, mod, row(norm2_pre), row(norm1_post),
                                         g_w1, w2_all)
    p_w1, p_w2 = _mlp_wgrad(tt(TT_WGRAD), r, da, df, h2)
    grad_w1, grad_w2 = _rs_partials("rs_fc", [p_w1, p_w2], 256)
    grad_x, p_in, p_out, small = _attn_bwd(tt(TT_ATTN_BWD), dmix, dx1, x2, z, cat, mod, row(norm1_pre), w_in_t,
                                           w_out_all, *attn_consts, red_fwd, red_bwd)
    grad_in_t, grad_out = _rs_partials(
        "rs_attn", [p_in.reshape(N_DEV, D_Z // N_DEV, D), p_out.reshape(N_DEV, D // N_DEV, D)], 64)
    small_all = _small_ag(small)

    d_w1, m_w1, v_w1 = _adamw_shard("adamw_fc1", 256, w_fc1, grad_w1, m_w_fc1, v_w_fc1)
    d_w2, m_w2, v_w2 = _adamw_shard("adamw_fc2", 128, w_fc2, grad_w2, m_w_fc2, v_w_fc2)
    d_out, m_out, v_out = _adamw_shard("adamw_out", 128, w_out, grad_out, m_w_out, v_w_out)
    d_in_t, m_in_t, v_in_t = _adamw_shard("adamw_in", D_Z // N_DEV, w_in.T, grad_in_t, m_w_in.T, v_w_in.T)
    dmod_all = small_all[:, ROW_DMOD:ROW_DMOD + 6, :].reshape(N_DEV, 6 * D)
    dmod_cols = lax.dynamic_slice_in_dim(dmod_all, me * ada_cols, ada_cols, axis=1)
    grad_ada, d_ada, m_ada, v_ada = _adamw_ada(256, w_ada, sc, dmod_cols, m_w_ada, v_w_ada)

    six = lambda a: a.reshape(6, D)
    small_params = [
        (six(b_ada), six(m_b_ada), six(v_b_ada)),
        (row(norm1_pre), row(m_norm1_pre), row(v_norm1_pre)),
        (row(norm1_post), row(m_norm1_post), row(v_norm1_post)),
        (row(norm2_pre), row(m_norm2_pre), row(v_norm2_pre)),
        (row(norm2_post), row(m_norm2_post), row(v_norm2_post)),
        (row(ln_v_gain), row(m_ln_v_gain), row(v_ln_v_gain)),
        (row(ln_v_bias), row(m_ln_v_bias), row(v_ln_v_bias)),
        (row(pool_scale), row(m_pool_scale), row(v_pool_scale)),
        (row(b_pool), row(m_b_pool), row(v_b_pool)),
        (b_spatial, m_b_spatial, v_b_spatial),
        (w_spatial, m_w_spatial, v_w_spatial),
        (w_pool, m_w_pool, v_w_pool),
    ]
    outs = _adamw_small(small_all, small_params)
    loss = outs[0][0, 0]
    names = ["b_ada", "norm1_pre", "norm1_post", "norm2_pre", "norm2_post", "ln_v_gain", "ln_v_bias", "pool_scale",
             "b_pool", "b_spatial", "w_spatial", "w_pool"]
    shapes = dict(b_ada=b_ada.shape, norm1_pre=norm1_pre.shape, norm1_post=norm1_post.shape,
                  norm2_pre=norm2_pre.shape, norm2_post=norm2_post.shape, ln_v_gain=ln_v_gain.shape,
                  ln_v_bias=ln_v_bias.shape, pool_scale=pool_scale.shape, b_pool=b_pool.shape,
                  b_spatial=b_spatial.shape, w_spatial=w_spatial.shape, w_pool=w_pool.shape)
    res = {}
    for k, nm in enumerate(names):
        res[nm] = tuple(o.reshape(shapes[nm]) for o in outs[1 + 4 * k:5 + 4 * k])
    res["w_ada"] = (grad_ada, d_ada, m_ada, v_ada)
    res["w_in"] = (grad_in_t.T, d_in_t.T, m_in_t.T, v_in_t.T)
    res["w_out"] = (grad_out, d_out, m_out, v_out)
    res["w_fc1"] = (grad_w1, d_w1, m_w1, v_w1)
    res["w_fc2"] = (grad_w2, d_w2, m_w2, v_w2)

    order = ["w_ada", "b_ada", "norm1_pre", "norm1_post", "w_in", "w_spatial", "b_spatial", "ln_v_gain", "ln_v_bias",
             "w_pool", "b_pool", "pool_scale", "w_out", "norm2_pre", "norm2_post", "w_fc1", "w_fc2"]
    return (loss, grad_x.reshape(x.shape),
            *[res[nm][0] for nm in order], *[res[nm][1] for nm in order],
            *[res[nm][2] for nm in order], *[res[nm][3] for nm in order])
```

```python
import functools

import jax
import jax.numpy as jnp
from jax import lax
from jax.experimental import pallas as pl
from jax.experimental.pallas import tpu as pltpu

F32 = jnp.float32
BF16 = jnp.bfloat16
MESH = pl.DeviceIdType.MESH

N_DEV = 8
D = 1024
D_A = 512
D_B = 512
D_Z = 2 * D_A + D_B
N_HEADS = 4
CHUNK = 128
WINDOWS = (2, 4, 8, 16)
GROUP = 128
D_FF = 4096
FF_BLK = D_FF // N_DEV
HALO = 16
EPS = 1e-6
VMEM_LIMIT = 60 * 1024 * 1024

ADAM_LR = 0.001
ADAM_B1 = 0.9
ADAM_B2 = 0.999
ADAM_EPS = 1e-08
ADAM_WD = 0.01
ADAM_STEP = 10

ROW_DMOD = 0
ROW_N1PRE, ROW_N1POST, ROW_N2PRE, ROW_N2POST = 8, 9, 10, 11
ROW_LN = 12
ROW_POOL = 13
ROW_LOSS = 14
ROW_BS = 16
ROW_WS = 24
ROW_WP = 88
SMALL_ROWS = 152
TABLE_ROWS = 8 * N_DEV
PACK_SHIFT = TABLE_ROWS - 8
PACK_ROWS = SMALL_ROWS + PACK_SHIFT
PACK_HALF = PACK_ROWS // 2


def _dot(a, b):
    return jnp.dot(a, b, preferred_element_type=F32)


def _dot_nt(a, b):
    return lax.dot_general(a, b, (((1,), (1,)), ((), ())), preferred_element_type=F32)


def _dot_tn(a, b):
    return lax.dot_general(a, b, (((0,), (0,)), ((), ())), preferred_element_type=F32)


def _rstd(v):
    return lax.rsqrt(jnp.mean(v * v, axis=-1, keepdims=True) + EPS)


def _rms_bwd(d_hat, hat, rstd):
    return rstd * (d_hat - hat * jnp.mean(d_hat * hat, axis=-1, keepdims=True))


_K0 = 0.7978845608028654
_K1 = 0.044715


def _gelu_parts(v):
    t = jnp.tanh(_K0 * (v + _K1 * (v * v * v)))
    return t, v * (0.5 * (1.0 + t))


def _gelu_grad(v, t):
    return 0.5 * (1.0 + t) + (0.5 * v) * (1.0 - t * t) * (_K0 * (1.0 + (3.0 * _K1) * (v * v)))


def _colsum(v):
    return jnp.sum(v, axis=0, keepdims=True)


def _full(shape):
    n = len(shape)
    return pl.BlockSpec(shape, lambda *_: (0,) * n)


def _resident(shape):
    n = len(shape)
    return pl.BlockSpec(shape, lambda *_: (0,) * n, pipeline_mode=pl.Buffered(1))


def _place():
    x, y, c = lax.axis_index("x"), lax.axis_index("y"), lax.axis_index("c")
    return x, y, c


def _flip(v, bit):
    return 1 - v if bit else v


def _peer(x, y, c, k):
    return (_flip(x, (k >> 2) & 1), _flip(y, (k >> 1) & 1), _flip(c, k & 1))


def _index(p):
    return 4 * p[0] + 2 * p[1] + p[2]


def _ada_fwd(c8, w_ada, b_my):
    ncol = w_ada.shape[1]

    def body(c_ref, w_ref, b_ref, modp_ref, sc_ref, cg, mg, part, send_sems, recv_sems):
        x, y, c = _place()
        me = _index((x, y, c))

        def c_copy(k):
            p = _peer(x, y, c, k)
            return pltpu.make_async_remote_copy(
                src_ref=c_ref, dst_ref=cg.at[me], send_sem=send_sems.at[k - 1], recv_sem=recv_sems.at[k - 1],
                device_id=p, device_id_type=MESH)

        def c_arrival(k):
            p = _peer(x, y, c, k)
            return pltpu.make_async_remote_copy(
                src_ref=c_ref, dst_ref=cg.at[_index(p)], send_sem=send_sems.at[k - 1], recv_sem=recv_sems.at[k - 1],
                device_id=p, device_id_type=MESH)

        def m_copy(k):
            p = _peer(x, y, c, k)
            return pltpu.make_async_remote_copy(
                src_ref=part, dst_ref=mg.at[me], send_sem=send_sems.at[6 + k], recv_sem=recv_sems.at[6 + k],
                device_id=p, device_id_type=MESH)

        def m_arrival(k):
            p = _peer(x, y, c, k)
            return pltpu.make_async_remote_copy(
                src_ref=part, dst_ref=mg.at[_index(p)], send_sem=send_sems.at[6 + k], recv_sem=recv_sems.at[6 + k],
                device_id=p, device_id_type=MESH)

        for k in range(1, N_DEV):
            c_copy(k).start()
        cg[me] = c_ref[...]
        for k in range(1, N_DEV):
            c_arrival(k).wait_recv()
        c_all = jnp.concatenate([cg[j, 0:1, :] for j in range(N_DEV)], axis=0)
        sc = c_all * jax.nn.sigmoid(c_all)
        sc_ref[...] = sc
        part[...] = _dot(sc.astype(BF16), w_ref[...].astype(BF16)) + b_ref[...]
        for k in range(1, N_DEV):
            m_copy(k).start()
        mg[me] = part[...]
        for k in range(1, N_DEV):
            m_arrival(k).wait_recv()
        for j in range(N_DEV):
            modp_ref[j:j + 1, :] = mg[j, pl.ds(me, 1), :]
        for k in range(1, N_DEV):
            c_copy(k).wait_send()
            m_copy(k).wait_send()

    vm = pl.BlockSpec(memory_space=pltpu.VMEM)
    return pl.pallas_call(
        body, name="ada_fwd",
        out_shape=(jax.ShapeDtypeStruct((N_DEV, ncol), F32), jax.ShapeDtypeStruct((N_DEV, D), F32)),
        in_specs=[vm, vm, vm], out_specs=(vm, vm),
        scratch_shapes=[
            pltpu.VMEM((N_DEV, 8, D), F32),
            pltpu.VMEM((N_DEV, N_DEV, ncol), F32),
            pltpu.VMEM((N_DEV, ncol), F32),
            pltpu.SemaphoreType.DMA((2 * (N_DEV - 1),)),
            pltpu.SemaphoreType.DMA((2 * (N_DEV - 1),)),
        ],
    )(c8, w_ada, b_my)


def _two_level_gather(x, y, c, out_refs, send_sems, recv_sems):
    me = (x, y, c)
    sibling = (x, y, 1 - c)
    chips = [(1 - x, y), (x, 1 - y), (1 - x, 1 - y)]
    n = len(out_refs)

    def copy(a, k, block, to):
        ref = out_refs[a].at[_index(block)]
        return pltpu.make_async_remote_copy(
            src_ref=ref, dst_ref=ref, send_sem=send_sems.at[7 * a + k], recv_sem=recv_sems.at[7 * a + k],
            device_id=to, device_id_type=MESH)

    first = []
    for a in range(n):
        first.append(copy(a, 0, me, sibling))
        first += [copy(a, 1 + j, me, (*chip, c)) for j, chip in enumerate(chips)]
    for cp in first:
        cp.start()
    passed = []
    for a in range(n):
        for j, chip in enumerate(chips):
            copy(a, 1 + j, (*chip, c), me).wait_recv()
            fwd = copy(a, 4 + j, (*chip, c), sibling)
            fwd.start()
            passed.append(fwd)
    for a in range(n):
        copy(a, 0, sibling, me).wait_recv()
        for j, chip in enumerate(chips):
            copy(a, 4 + j, (*chip, 1 - c), me).wait_recv()
    for cp in first + passed:
        cp.wait_send()


def _ag_weights(shards):
    n = len(shards)

    def body(*refs):
        in_refs, out_refs = refs[:n], refs[n:2 * n]
        send_sems, recv_sems = refs[2 * n], refs[2 * n + 1]
        x, y, c = _place()
        me = _index((x, y, c))
        for a in range(n):
            out_refs[a][me] = in_refs[a][...].astype(BF16)
        _two_level_gather(x, y, c, out_refs, send_sems, recv_sems)

    vm = pl.BlockSpec(memory_space=pltpu.VMEM)
    return pl.pallas_call(
        body, name="ag_weights",
        out_shape=tuple(jax.ShapeDtypeStruct((N_DEV,) + s.shape, BF16) for s in shards),
        in_specs=[vm] * n, out_specs=tuple([vm] * n),
        scratch_shapes=[pltpu.SemaphoreType.DMA((7 * n,)), pltpu.SemaphoreType.DMA((7 * n,))],
        compiler_params=pltpu.CompilerParams(vmem_limit_bytes=VMEM_LIMIT),
    )(*shards)


def _tail_comm(parts, small, row_chunk):
    n = len(parts)

    def body(*refs):
        p_refs, small_ref = refs[:n], refs[n]
        g_refs, total_ref = refs[n + 1:2 * n + 1], refs[2 * n + 1]
        scr = refs[2 * n + 2:]
        from_sib = scr[0:n]
        chip_out = scr[n:2 * n]
        chip_in = scr[2 * n:3 * n]
        pack, pack_sib, halves = scr[3 * n:3 * n + 3]
        send_a, recv_a, send_b, recv_b, send_s, recv_s = scr[3 * n + 3:]
        x, y, c = _place()
        me = _index((x, y, c))
        sibling = (x, y, 1 - c)
        my_chip = 2 * x + y
        others = [(1 - x, y), (x, 1 - y), (1 - x, 1 - y)]
        my_half = pl.ds(pl.multiple_of(PACK_HALF * c, 8), PACK_HALF)

        def pack_to_sibling():
            return pltpu.make_async_remote_copy(
                src_ref=pack, dst_ref=pack_sib, send_sem=send_s.at[0], recv_sem=recv_s.at[0],
                device_id=sibling, device_id_type=MESH)

        def half_to_chip(r):
            return pltpu.make_async_remote_copy(
                src_ref=halves.at[my_chip], dst_ref=halves.at[my_chip],
                send_sem=send_s.at[1 + r], recv_sem=recv_s.at[1 + r],
                device_id=(*others[r], c), device_id_type=MESH)

        def half_from_chip(r):
            k = 2 * others[r][0] + others[r][1]
            return pltpu.make_async_remote_copy(
                src_ref=halves.at[k], dst_ref=halves.at[k], send_sem=send_s.at[1 + r], recv_sem=recv_s.at[1 + r],
                device_id=(*others[r], c), device_id_type=MESH)

        def total_to_sibling():
            return pltpu.make_async_remote_copy(
                src_ref=total_ref.at[my_half], dst_ref=total_ref.at[my_half],
                send_sem=send_s.at[4], recv_sem=recv_s.at[4], device_id=sibling, device_id_type=MESH)

        def total_from_sibling():
            sib_half = pl.ds(pl.multiple_of(PACK_HALF * (1 - c), 8), PACK_HALF)
            return pltpu.make_async_remote_copy(
                src_ref=total_ref.at[sib_half], dst_ref=total_ref.at[sib_half],
                send_sem=send_s.at[4], recv_sem=recv_s.at[4], device_id=sibling, device_id_type=MESH)

        pack[0:TABLE_ROWS, :] = jnp.zeros((TABLE_ROWS, D), F32)
        pack[pl.ds(pl.multiple_of(8 * me, 8), 8), :] = small_ref[0:8, :]
        pack[TABLE_ROWS:PACK_ROWS, :] = small_ref[8:SMALL_ROWS, :]
        pack_to_sibling().start()

        def to_sibling(a, k):
            return pltpu.make_async_remote_copy(
                src_ref=p_refs[a].at[2 * k + (1 - c)], dst_ref=from_sib[a].at[k],
                send_sem=send_a.at[a], recv_sem=recv_a.at[a], device_id=sibling, device_id_type=MESH)

        def all_from_sibling(a):
            return pltpu.make_async_remote_copy(
                src_ref=from_sib[a], dst_ref=from_sib[a], send_sem=send_a.at[a], recv_sem=recv_a.at[a],
                device_id=sibling, device_id_type=MESH)

        def to_chip(a, r):
            return pltpu.make_async_remote_copy(
                src_ref=chip_out[a].at[r], dst_ref=chip_in[a].at[r],
                send_sem=send_b.at[3 * a + r], recv_sem=recv_b.at[3 * a + r],
                device_id=(*others[r], c), device_id_type=MESH)

        for a in range(n):
            for k in range(4):
                to_sibling(a, k).start()
        pack_to_sibling().wait_recv()
        halves[my_chip] = pack[my_half, :] + pack_sib[my_half, :]
        for r in range(3):
            half_to_chip(r).start()
        for a in range(n):
            all_from_sibling(a).wait_recv()
            rows = p_refs[a].shape[1]
            for r in range(3):
                k = 2 * others[r][0] + others[r][1]
                for s in range(0, rows, row_chunk):
                    sl = pl.ds(s, row_chunk)
                    chip_out[a][r, sl, :] = (p_refs[a][2 * k + c, sl, :].astype(F32)
                                             + from_sib[a][k, sl, :].astype(F32)).astype(BF16)
                to_chip(a, r).start()
            for s in range(0, rows, row_chunk):
                sl = pl.ds(s, row_chunk)
                g_refs[a][sl, :] = (p_refs[a][2 * my_chip + c, sl, :].astype(F32)
                                    + from_sib[a][my_chip, sl, :].astype(F32))
        for r in range(3):
            half_from_chip(r).wait_recv()
        total_ref[my_half, :] = ((halves[0] + halves[1]) + halves[2]) + halves[3]
        total_to_sibling().start()
        for a in range(n):
            rows = p_refs[a].shape[1]
            for r in range(3):
                to_chip(a, r).wait_recv()
                for s in range(0, rows, row_chunk):
                    sl = pl.ds(s, row_chunk)
                    g_refs[a][sl, :] = g_refs[a][sl, :] + chip_in[a][r, sl, :].astype(F32)
        total_from_sibling().wait_recv()
        for a in range(n):
            all_from_sibling(a).wait_send()
            for r in range(3):
                to_chip(a, r).wait_send()
        pack_to_sibling().wait_send()
        for r in range(3):
            half_to_chip(r).wait_send()
        total_to_sibling().wait_send()

    vm = pl.BlockSpec(memory_space=pltpu.VMEM)
    return pl.pallas_call(
        body, name="tail_comm",
        out_shape=tuple([jax.ShapeDtypeStruct(p.shape[1:], F32) for p in parts]
                        + [jax.ShapeDtypeStruct((PACK_ROWS, D), F32)]),
        in_specs=[vm] * (n + 1), out_specs=tuple([vm] * (n + 1)),
        scratch_shapes=(
            [pltpu.VMEM((4,) + p.shape[1:], BF16) for p in parts]
            + [pltpu.VMEM((3,) + p.shape[1:], BF16) for p in parts]
            + [pltpu.VMEM((3,) + p.shape[1:], BF16) for p in parts]
            + [pltpu.VMEM((PACK_ROWS, D), F32), pltpu.VMEM((PACK_ROWS, D), F32),
               pltpu.VMEM((4, PACK_HALF, D), F32)]
            + [pltpu.SemaphoreType.DMA((n,)), pltpu.SemaphoreType.DMA((n,)),
               pltpu.SemaphoreType.DMA((3 * n,)), pltpu.SemaphoreType.DMA((3 * n,)),
               pltpu.SemaphoreType.DMA((5,)), pltpu.SemaphoreType.DMA((5,))]),
        compiler_params=pltpu.CompilerParams(vmem_limit_bytes=VMEM_LIMIT),
    )(*parts, small)


def _tril_mask():
    row = lax.broadcasted_iota(jnp.int32, (CHUNK, CHUNK), 0)
    col = lax.broadcasted_iota(jnp.int32, (CHUNK, CHUNK), 1)
    return (col <= row).astype(F32)


def _window_sums(ext):
    s2 = ext + pltpu.roll(ext, 1, 0)
    t4 = s2[:, GROUP:]
    s4 = t4 + pltpu.roll(t4, 2, 0)
    t8 = s4[:, GROUP:]
    s8 = t8 + pltpu.roll(t8, 4, 0)
    t16 = s8[:, GROUP:]
    s16 = t16 + pltpu.roll(t16, 8, 0)
    return [s2[:, :GROUP], s4[:, :GROUP], s8[:, :GROUP], s16]


def _inv_counts(first_pos, rows):
    pos = first_pos + lax.broadcasted_iota(jnp.int32, (rows, 1), 0)
    return [1.0 / jnp.minimum(pos + 1, w).astype(F32) for w in WINDOWS]


def _pool_diff(zb, halo, first_pos):
    tt = zb.shape[0]
    sums = _window_sums(jnp.concatenate([halo, zb], axis=0))
    inv = _inv_counts(first_pos, tt)
    return [sums[g][HALO:, :] * inv[g] - zb[:, g * GROUP:(g + 1) * GROUP] for g in range(len(WINDOWS))]


def _attn_fwd(tt, x, mod, n1pre, n1post, w_in_t, w_out, w_sp, bs_rows, ln_g, ln_b, w_pool, b_pool, pool_scale):
    t_len = x.shape[0]
    nt = t_len // tt

    def body(x_ref, mod_ref, n1pre_ref, n1post_ref, win_ref, wout_ref, wsp_ref, bs_ref, lng_ref, lnb_ref,
             wp_ref, bp_ref, ps_ref, z_ref, cat_ref, mix_ref, x1_ref, carry):
        i = pl.program_id(0)

        @pl.when(i == 0)
        def _():
            carry[...] = jnp.zeros_like(carry)

        xv = x_ref[...]
        shift1, scale1, gate1 = mod_ref[0:1, :], mod_ref[1:2, :], mod_ref[2:3, :]
        h1 = (xv * _rstd(xv) * n1pre_ref[...]) * (1.0 + scale1) + shift1
        z = _dot_nt(h1.astype(BF16), win_ref[...])
        z_ref[...] = z

        _, ga = _gelu_parts(z[:, :2 * D_A])
        u, vr = ga[:, :D_A], ga[:, D_A:]
        dv = vr - jnp.mean(vr, axis=-1, keepdims=True)
        v = (dv * lax.rsqrt(jnp.mean(dv * dv, axis=-1, keepdims=True) + EPS)) * lng_ref[...] + lnb_ref[...]
        vb = v.astype(BF16)
        mask = _tril_mask()
        wc = [(wsp_ref[h] * mask).astype(BF16) for h in range(N_HEADS)]
        for ch in range(tt // CHUNK):
            rows = slice(ch * CHUNK, (ch + 1) * CHUNK)
            for h in range(N_HEADS):
                cols = slice(h * GROUP, (h + 1) * GROUP)
                mixed = _dot(wc[h], vb[rows, cols]) + bs_ref[:, cols]
                cat_ref[rows, cols] = (u[rows, cols] * mixed).astype(BF16)

        zb = z[:, 2 * D_A:]
        diff = _pool_diff(zb, carry[...], i * tt)
        carry[...] = zb[tt - HALO:, :]
        for g in range(len(WINDOWS)):
            cols = slice(g * GROUP, (g + 1) * GROUP)
            pre = _dot(diff[g].astype(BF16), wp_ref[g].astype(BF16)) + bp_ref[:, cols]
            cat_ref[:, D_A + g * GROUP:D_A + (g + 1) * GROUP] = (pre * ps_ref[:, cols]).astype(BF16)

        mix = _dot(cat_ref[...], wout_ref[...])
        mix_ref[...] = mix
        x1_ref[...] = xv + gate1 * (mix * _rstd(mix) * n1post_ref[...])

    tile = lambda w: pl.BlockSpec((tt, w), lambda i: (i, 0))
    return pl.pallas_call(
        body, name="attn_fwd", grid=(nt,),
        out_shape=(jax.ShapeDtypeStruct((t_len, D_Z), F32), jax.ShapeDtypeStruct((t_len, D), BF16),
                   jax.ShapeDtypeStruct((t_len, D), F32), jax.ShapeDtypeStruct((t_len, D), F32)),
        in_specs=[tile(D), _full((8, D)), _full((1, D)), _full((1, D)), _resident((D_Z, D)), _resident((D, D)),
                  _full((N_HEADS, CHUNK, CHUNK)), _full((CHUNK, D_A)), _full((1, D_A)), _full((1, D_A)),
                  _full((len(WINDOWS), GROUP, GROUP)), _full((1, D_B)), _full((1, D_B))],
        out_specs=(tile(D_Z), tile(D), tile(D), tile(D)),
        scratch_shapes=[pltpu.VMEM((HALO, D_B), F32)],
        compiler_params=pltpu.CompilerParams(dimension_semantics=("arbitrary",), vmem_limit_bytes=VMEM_LIMIT),
    )(x, mod, n1pre, n1post, w_in_t, w_out, w_sp, bs_rows, ln_g, ln_b, w_pool, b_pool, pool_scale)


def _mlp_fwd(tt, x1, tgt, mod, n2pre, n2post, w1g, w2):
    t_len = x1.shape[0]
    nt = t_len // tt

    def body(x1_ref, tgt_ref, mod_ref, n2pre_ref, n2post_ref, w1_ref, w2_ref,
             a_ref, h2_ref, df_ref, dy_ref, red_ref, f_acc):
        i = pl.program_id(0)

        @pl.when(i == 0)
        def _():
            red_ref[...] = jnp.zeros_like(red_ref)

        x1v = x1_ref[...]
        shift2, scale2, gate2 = mod_ref[3:4, :], mod_ref[4:5, :], mod_ref[5:6, :]
        h2 = ((x1v * _rstd(x1v) * n2pre_ref[...]) * (1.0 + scale2) + shift2).astype(BF16)
        h2_ref[...] = h2
        px, py, pc = _place()
        for j in range(N_DEV):
            cols = slice(j * FF_BLK, (j + 1) * FF_BLK)
            blk = _index(_peer(px, py, pc, j))
            a = _dot(h2, w1_ref[blk])
            a_ref[:, cols] = a
            r = jnp.maximum(a, 0.0)
            contrib = _dot((r * r).astype(BF16), w2_ref[blk])
            if j == 0:
                f_acc[...] = contrib
            else:
                f_acc[...] += contrib
        f = f_acc[...]
        rf = _rstd(f)
        fhat = f * rf
        nf = fhat * n2post_ref[...]
        err = (x1v + gate2 * nf) - tgt_ref[...]
        dy = err * (1.0 / D)
        dy_ref[...] = dy
        dnf = dy * gate2
        df_ref[...] = _rms_bwd(dnf * n2post_ref[...], fhat, rf).astype(BF16)
        red_ref[0:1, :] += _colsum(dy * nf)
        red_ref[1:2, :] += _colsum(dnf * fhat)
        red_ref[2:3, :] += _colsum(0.5 * jnp.mean(err * err, axis=-1, keepdims=True)) * jnp.ones((1, D), F32)

    tile = lambda w: pl.BlockSpec((tt, w), lambda i: (i, 0))
    return pl.pallas_call(
        body, name="mlp_fwd", grid=(nt,),
        out_shape=(jax.ShapeDtypeStruct((t_len, D_FF), F32), jax.ShapeDtypeStruct((t_len, D), BF16),
                   jax.ShapeDtypeStruct((t_len, D), BF16), jax.ShapeDtypeStruct((t_len, D), F32),
                   jax.ShapeDtypeStruct((8, D), F32)),
        in_specs=[tile(D), tile(D), _full((8, D)), _full((1, D)), _full((1, D)),
                  _resident((N_DEV, D, FF_BLK)), _resident((N_DEV, FF_BLK, D))],
        out_specs=(tile(D_FF), tile(D), tile(D), tile(D), _full((8, D))),
        scratch_shapes=[pltpu.VMEM((tt, D), F32)],
        compiler_params=pltpu.CompilerParams(dimension_semantics=("arbitrary",), vmem_limit_bytes=VMEM_LIMIT),
    )(x1, tgt, mod, n2pre, n2post, w1g, w2)


def _mlp_bwd(tt, df, a, dy, x1, mix, mod, n2pre, n1post, w1g, w2):
    t_len = x1.shape[0]
    nt = t_len // tt

    def body(df_ref, a_ref, dy_ref, x1_ref, mix_ref, mod_ref, n2pre_ref, n1post_ref, w1_ref, w2_ref,
             da_ref, r_ref, dmix_ref, dx1_ref, red_ref, dh2_acc):
        i = pl.program_id(0)

        @pl.when(i == 0)
        def _():
            red_ref[...] = jnp.zeros_like(red_ref)

        dfv = df_ref[...]
        px, py, pc = _place()
        for j in range(N_DEV):
            cols = slice(j * FF_BLK, (j + 1) * FF_BLK)
            blk = _index(_peer(px, py, pc, j))
            dr = _dot_nt(dfv, w2_ref[blk])
            ra = jnp.maximum(a_ref[:, cols], 0.0)
            da = (dr * (2.0 * ra)).astype(BF16)
            da_ref[:, cols] = da
            r_ref[:, cols] = (ra * ra).astype(BF16)
            contrib = _dot_nt(da, w1_ref[blk])
            if j == 0:
                dh2_acc[...] = contrib
            else:
                dh2_acc[...] += contrib
        dh2 = dh2_acc[...]
        gate1, scale2 = mod_ref[2:3, :], mod_ref[4:5, :]
        x1v = x1_ref[...]
        r2 = _rstd(x1v)
        xhat = x1v * r2
        n2 = xhat * n2pre_ref[...]
        dn2 = dh2 * (1.0 + scale2)
        dx1 = dy_ref[...] + _rms_bwd(dn2 * n2pre_ref[...], xhat, r2)
        dx1_ref[...] = dx1
        mixv = mix_ref[...]
        rm = _rstd(mixv)
        mhat = mixv * rm
        dnm = dx1 * gate1
        dmix_ref[...] = _rms_bwd(dnm * n1post_ref[...], mhat, rm).astype(BF16)
        red_ref[0:1, :] += _colsum(dh2)
        red_ref[1:2, :] += _colsum(dh2 * n2)
        red_ref[2:3, :] += _colsum(dn2 * xhat)
        red_ref[3:4, :] += _colsum(dx1 * (mhat * n1post_ref[...]))
        red_ref[4:5, :] += _colsum(dnm * mhat)

    tile = lambda w: pl.BlockSpec((tt, w), lambda i: (i, 0))
    return pl.pallas_call(
        body, name="mlp_bwd", grid=(nt,),
        out_shape=(jax.ShapeDtypeStruct((t_len, D_FF), BF16), jax.ShapeDtypeStruct((t_len, D_FF), BF16),
                   jax.ShapeDtypeStruct((t_len, D), BF16), jax.ShapeDtypeStruct((t_len, D), F32),
                   jax.ShapeDtypeStruct((8, D), F32)),
        in_specs=[tile(D), tile(D_FF), tile(D), tile(D), tile(D), _full((8, D)), _full((1, D)), _full((1, D)),
                  _resident((N_DEV, D, FF_BLK)), _resident((N_DEV, FF_BLK, D))],
        out_specs=(tile(D_FF), tile(D_FF), tile(D), tile(D), _full((8, D))),
        scratch_shapes=[pltpu.VMEM((tt, D), F32)],
        compiler_params=pltpu.CompilerParams(dimension_semantics=("arbitrary",), vmem_limit_bytes=VMEM_LIMIT),
    )(df, a, dy, x1, mix, mod, n2pre, n1post, w1g, w2)


def _mlp_wgrad(tt, r, da, df, h2):
    t_len = df.shape[0]
    nt = t_len // tt

    def relation(j):
        return jnp.where(j < 4, 2 * j + 1, (2 * j - 6) % N_DEV)

    def body(r_ref, da_ref, df_ref, h2_ref, own1_ref, own2_ref, out1_ref, out2_ref,
             acc1, acc2, snd1, snd2, sib1, sib2, send_sems, recv_sems):
        j, t = pl.program_id(0), pl.program_id(1)
        rows = pl.ds(pl.multiple_of(t * tt, tt), tt)
        x, y, c = _place()
        accs, snds, sibs = (acc1, acc2), (snd1, snd2), (sib1, sib2)

        def to_sibling(a, jj):
            return pltpu.make_async_remote_copy(
                src_ref=snds[a].at[jj % 2], dst_ref=sibs[a].at[jj],
                send_sem=send_sems.at[4 * a + jj], recv_sem=recv_sems.at[4 * a + jj],
                device_id=(x, y, 1 - c), device_id_type=MESH)

        @pl.when(t == 0)
        def _():
            acc2[...] = jnp.zeros_like(acc2)
            acc1[...] = jnp.zeros_like(acc1)

        acc2[...] += _dot_tn(r_ref[...], df_ref[rows, :])
        acc1[...] += _dot_tn(h2_ref[rows, :], da_ref[...])

        @pl.when((t == nt - 1) & (j < 4))
        def _():
            for a in range(2):
                @pl.when(j >= 2)
                def _():
                    to_sibling(a, j - 2).wait_send()

                snds[a][j % 2] = accs[a][...].astype(BF16)
                to_sibling(a, j).start()

        @pl.when((t == nt - 1) & (j >= 4))
        def _():
            jj = (j - 3) % 4
            for a, (own_ref, out_ref) in enumerate(((own1_ref, out1_ref), (own2_ref, out2_ref))):
                to_sibling(a, jj).wait_recv()

                @pl.when(j < N_DEV - 1)
                def _():
                    out_ref[0] = (accs[a][...] + sibs[a][jj].astype(F32)).astype(BF16)

                @pl.when(j == N_DEV - 1)
                def _():
                    own_ref[...] = accs[a][...] + sibs[a][jj].astype(F32)
                    to_sibling(a, 2).wait_send()
                    to_sibling(a, 3).wait_send()

    blk = pl.BlockSpec((tt, FF_BLK), lambda j, t: (t, relation(j)))
    chip = lambda j, t: (jnp.clip(j - 4, 0, 2), 0, 0)
    return pl.pallas_call(
        body, name="mlp_wgrad", grid=(N_DEV, nt),
        out_shape=(jax.ShapeDtypeStruct((D, FF_BLK), F32), jax.ShapeDtypeStruct((FF_BLK, D), F32),
                   jax.ShapeDtypeStruct((3, D, FF_BLK), BF16), jax.ShapeDtypeStruct((3, FF_BLK, D), BF16)),
        in_specs=[blk, blk, _resident((t_len, D)), _resident((t_len, D))],
        out_specs=(_full((D, FF_BLK)), _full((FF_BLK, D)),
                   pl.BlockSpec((1, D, FF_BLK), chip), pl.BlockSpec((1, FF_BLK, D), chip)),
        scratch_shapes=[pltpu.VMEM((D, FF_BLK), F32), pltpu.VMEM((FF_BLK, D), F32),
                        pltpu.VMEM((2, D, FF_BLK), BF16), pltpu.VMEM((2, FF_BLK, D), BF16),
                        pltpu.VMEM((4, D, FF_BLK), BF16), pltpu.VMEM((4, FF_BLK, D), BF16),
                        pltpu.SemaphoreType.DMA((8,)), pltpu.SemaphoreType.DMA((8,))],
        compiler_params=pltpu.CompilerParams(dimension_semantics=("arbitrary", "arbitrary"),
                                             vmem_limit_bytes=VMEM_LIMIT),
    )(r, da, df, h2)


def _acc_rows(ref, row0, k, val):
    half = CHUNK // 2
    ref[row0:row0 + half, k * GROUP:(k + 1) * GROUP] += val[:half, :]
    ref[row0:row0 + half, D_A + k * GROUP:D_A + (k + 1) * GROUP] += val[half:, :]


def _attn_bwd(tt, dmix, dx1, x, z, cat, mod, n1pre, w_in_t, w_out, w_sp, bs_rows, ln_g, ln_b, w_pool, b_pool,
              pool_scale, red_fwd, red_bwd, chip_sums):
    t_len = x.shape[0]
    nt = t_len // tt
    hb = tt // HALO
    n_sums = len(chip_sums)

    def body(dmix_ref, dx1_ref, x_ref, z_ref, zprev_ref, cat_ref, mod_ref, n1pre_ref, win_ref, wout_ref, wsp_ref,
             bs_ref, lng_ref, lnb_ref, wp_ref, bp_ref, ps_ref, redf_ref, redb_ref, *rest):
        sum_out = rest[:n_sums]
        gx_ref, gwin_ref, gwout_ref, small_ref = rest[n_sums:n_sums + 4]
        sum_in = rest[n_sums + 4:2 * n_sums + 4]
        carry, acc_in, acc_out, dz_scr, bs_acc, send_sems, recv_sems = rest[2 * n_sums + 4:]
        s = pl.program_id(0)
        i = nt - 1 - s
        px, py, pc = _place()

        def chip_copy(a, r):
            return pltpu.make_async_remote_copy(
                src_ref=sum_out[a].at[r], dst_ref=sum_in[a].at[r],
                send_sem=send_sems.at[3 * a + r], recv_sem=recv_sems.at[3 * a + r],
                device_id=_peer(px, py, pc, 2 * (r + 1)), device_id_type=MESH)

        @pl.when(s == 0)
        def _():
            for a in range(n_sums):
                for r in range(3):
                    chip_copy(a, r).start()
            carry[...] = jnp.zeros_like(carry)
            acc_in[...] = jnp.zeros_like(acc_in)
            acc_out[...] = jnp.zeros_like(acc_out)
            bs_acc[...] = jnp.zeros_like(bs_acc)
            small_ref[...] = jnp.zeros_like(small_ref)
            small_ref[ROW_DMOD + 2:ROW_DMOD + 3, :] = redb_ref[3:4, :]
            small_ref[ROW_DMOD + 3:ROW_DMOD + 5, :] = redb_ref[0:2, :]
            small_ref[ROW_DMOD + 5:ROW_DMOD + 6, :] = redf_ref[0:1, :]
            small_ref[ROW_N1POST:ROW_N1POST + 1, :] = redb_ref[4:5, :]
            small_ref[ROW_N2PRE:ROW_N2PRE + 1, :] = redb_ref[2:3, :]
            small_ref[ROW_N2POST:ROW_N2POST + 1, :] = redf_ref[1:2, :]
            small_ref[ROW_LOSS:ROW_LOSS + 1, :] = redf_ref[2:3, :]

        dmixv = dmix_ref[...]
        dcat = _dot_nt(dmixv, wout_ref[...])
        acc_out[...] += _dot_tn(cat_ref[...], dmixv)

        z = z_ref[...]
        t_g, ga = _gelu_parts(z[:, :2 * D_A])
        u, vr = ga[:, :D_A], ga[:, D_A:]
        dv0 = vr - jnp.mean(vr, axis=-1, keepdims=True)
        rv = lax.rsqrt(jnp.mean(dv0 * dv0, axis=-1, keepdims=True) + EPS)
        vhat = dv0 * rv
        vb = (vhat * lng_ref[...] + lnb_ref[...]).astype(BF16)
        mask = _tril_mask()
        wc = [(wsp_ref[h] * mask).astype(BF16) for h in range(N_HEADS)]

        dya = dcat[:, :D_A]
        for h in range(N_HEADS):
            cols = slice(h * GROUP, (h + 1) * GROUP)
            bs_sum = jnp.zeros((CHUNK, GROUP), F32)
            ws_sum = jnp.zeros((CHUNK, CHUNK), F32)
            for ch in range(tt // CHUNK):
                rows = slice(ch * CHUNK, (ch + 1) * CHUNK)
                v_ch = vb[rows, cols]
                mixed = _dot(wc[h], v_ch) + bs_ref[:, cols]
                dy_ch = dya[rows, cols]
                dz_scr[rows, cols] = dy_ch * mixed
                dmixed = dy_ch * u[rows, cols]
                dmb = dmixed.astype(BF16)
                dz_scr[rows, D_A + h * GROUP:D_A + (h + 1) * GROUP] = _dot_tn(wc[h], dmb)
                bs_sum = bs_sum + dmixed
                ws_sum = ws_sum + _dot_nt(dmb, v_ch)
            _acc_rows(bs_acc, 0, h, bs_sum)
            _acc_rows(small_ref, ROW_WS, h, ws_sum)

        dvl = dz_scr[:, D_A:2 * D_A]
        dvhat = dvl * lng_ref[...]
        dvr = rv * (dvhat - jnp.mean(dvhat, axis=-1, keepdims=True)
                    - vhat * jnp.mean(dvhat * vhat, axis=-1, keepdims=True))
        small_ref[ROW_LN:ROW_LN + 1, 0:D_A] += _colsum(dvl * vhat)
        small_ref[ROW_LN:ROW_LN + 1, D_A:D] += _colsum(dvl)
        dga = jnp.concatenate([dz_scr[:, :D_A], dvr], axis=1)
        dza = dga * _gelu_grad(z[:, :2 * D_A], t_g)

        zb = z[:, 2 * D_A:]
        halo_prev = jnp.where(i == 0, 0.0, zprev_ref[...])
        diff = _pool_diff(zb, halo_prev, i * tt)
        dyb = dcat[:, D_A:]
        inv = _inv_counts(i * tt, tt)
        scaled, ddiffs = [], []
        for g in range(len(WINDOWS)):
            cols = slice(g * GROUP, (g + 1) * GROUP)
            db = diff[g].astype(BF16)
            wpg = wp_ref[g].astype(BF16)
            pre = _dot(db, wpg) + bp_ref[:, cols]
            small_ref[ROW_POOL:ROW_POOL + 1, cols] += _colsum(dyb[:, cols] * pre)
            dpre = dyb[:, cols] * ps_ref[:, cols]
            small_ref[ROW_POOL:ROW_POOL + 1, D_B + g * GROUP:D_B + (g + 1) * GROUP] += _colsum(dpre)
            dpb = dpre.astype(BF16)
            _acc_rows(small_ref, ROW_WP, g, _dot_tn(db, dpb))
            ddiff = _dot_nt(dpb, wpg)
            ddiffs.append(ddiff)
            scaled.append(ddiff * inv[g])
        scaled_all = jnp.concatenate(scaled, axis=1)
        ext = jnp.concatenate([scaled_all, carry[...]], axis=0)
        n_ext = tt + HALO
        s2 = ext + pltpu.roll(ext, n_ext - 1, 0)
        t4 = s2[:, GROUP:]
        s4 = t4 + pltpu.roll(t4, n_ext - 2, 0)
        t8 = s4[:, GROUP:]
        s8 = t8 + pltpu.roll(t8, n_ext - 4, 0)
        t16 = s8[:, GROUP:]
        s16 = t16 + pltpu.roll(t16, n_ext - 8, 0)
        back = [s2[:, :GROUP], s4[:, :GROUP], s8[:, :GROUP], s16]
        carry[...] = scaled_all[:HALO, :]
        dzb = jnp.concatenate([back[g][:tt, :] - ddiffs[g] for g in range(len(WINDOWS))], axis=1)

        dzv = jnp.concatenate([dza, dzb], axis=1).astype(BF16)
        dh1 = _dot(dzv, win_ref[...])
        xv = x_ref[...]
        r1 = _rstd(xv)
        xhat = xv * r1
        shift1, scale1 = mod_ref[0:1, :], mod_ref[1:2, :]
        n1 = xhat * n1pre_ref[...]
        h1 = (n1 * (1.0 + scale1) + shift1).astype(BF16)
        acc_in[...] += _dot_tn(dzv, h1)
        dn1 = dh1 * (1.0 + scale1)
        gx_ref[...] = dx1_ref[...] + _rms_bwd(dn1 * n1pre_ref[...], xhat, r1)
        small_ref[ROW_DMOD:ROW_DMOD + 1, :] += _colsum(dh1)
        small_ref[ROW_DMOD + 1:ROW_DMOD + 2, :] += _colsum(dh1 * n1)
        small_ref[ROW_N1PRE:ROW_N1PRE + 1, :] += _colsum(dn1 * xhat)

        @pl.when(s == nt - 1)
        def _():
            gwin_ref[...] = acc_in[...].astype(BF16)
            gwout_ref[...] = acc_out[...].astype(BF16)
            bs = _unfold(bs_acc[...])
            for h in range(N_HEADS):
                small_ref[ROW_BS + h:ROW_BS + h + 1, 0:GROUP] = jnp.sum(
                    bs[:, h * GROUP:(h + 1) * GROUP].T, axis=0, keepdims=True)
            for a in range(n_sums):
                for r in range(3):
                    chip_copy(a, r).wait_recv()
                    chip_copy(a, r).wait_send()

    rev = lambda w: pl.BlockSpec((tt, w), lambda s: (nt - 1 - s, 0))
    zprev = pl.BlockSpec((HALO, D_B), lambda s: (jnp.maximum((nt - 1 - s) * hb - 1, 0), 2))
    hbm = pl.BlockSpec(memory_space=pl.ANY)
    outs = pl.pallas_call(
        body, name="attn_bwd", grid=(nt,),
        out_shape=tuple([jax.ShapeDtypeStruct((t_len, D), F32), jax.ShapeDtypeStruct((D_Z, D), BF16),
                         jax.ShapeDtypeStruct((D, D), BF16), jax.ShapeDtypeStruct((SMALL_ROWS, D), F32)]
                        + [jax.ShapeDtypeStruct(cs.shape, cs.dtype) for cs in chip_sums]),
        in_specs=[rev(D), rev(D), rev(D), rev(D_Z), zprev, rev(D), _full((8, D)), _full((1, D)),
                  _resident((D_Z, D)), _resident((D, D)), _full((N_HEADS, CHUNK, CHUNK)), _full((CHUNK, D_A)),
                  _full((1, D_A)), _full((1, D_A)), _full((len(WINDOWS), GROUP, GROUP)), _full((1, D_B)),
                  _full((1, D_B)), _full((8, D)), _full((8, D))] + [hbm] * n_sums,
        out_specs=tuple([rev(D), _full((D_Z, D)), _full((D, D)), _full((SMALL_ROWS, D))] + [hbm] * n_sums),
        scratch_shapes=[pltpu.VMEM((HALO, D_B), F32), pltpu.VMEM((D_Z, D), F32), pltpu.VMEM((D, D), F32),
                        pltpu.VMEM((tt, 2 * D_A), F32), pltpu.VMEM((CHUNK // 2, D), F32),
                        pltpu.SemaphoreType.DMA((3 * n_sums,)), pltpu.SemaphoreType.DMA((3 * n_sums,))],
        compiler_params=pltpu.CompilerParams(dimension_semantics=("arbitrary",), vmem_limit_bytes=VMEM_LIMIT),
    )(dmix, dx1, x, z, z, cat, mod, n1pre, w_in_t, w_out, w_sp, bs_rows, ln_g, ln_b, w_pool, b_pool, pool_scale,
      red_fwd, red_bwd, *chip_sums)
    return outs[:4], outs[4:]


def _adam(w, g, m, v):
    m2 = ADAM_B1 * m + (1.0 - ADAM_B1) * g
    v2 = ADAM_B2 * v + (1.0 - ADAM_B2) * (g * g)
    m_hat = m2 / (1.0 - ADAM_B1 ** ADAM_STEP)
    v_hat = v2 / (1.0 - ADAM_B2 ** ADAM_STEP)
    delta = -ADAM_LR * (m_hat / (jnp.sqrt(v_hat) + ADAM_EPS) + ADAM_WD * w)
    return delta, m2, v2


def _adamw_shard(name, rb, w, g, m, v):
    rows, cols = w.shape

    def body(w_ref, g_ref, m_ref, v_ref, d_ref, m2_ref, v2_ref):
        d_ref[...], m2_ref[...], v2_ref[...] = _adam(w_ref[...], g_ref[...], m_ref[...], v_ref[...])

    blk = pl.BlockSpec((rb, cols), lambda i: (i, 0))
    shp = jax.ShapeDtypeStruct((rows, cols), F32)
    return pl.pallas_call(
        body, name=name, grid=(rows // rb,), out_shape=(shp, shp, shp),
        in_specs=[blk] * 4, out_specs=(blk, blk, blk),
        compiler_params=pltpu.CompilerParams(dimension_semantics=("arbitrary",)),
    )(w, g, m, v)


def _adamw_chip_sums(name, rb, w, own, arrived, m, v):
    rows, cols = w.shape

    def body(w_ref, own_ref, arr_ref, m_ref, v_ref, g_ref, d_ref, m2_ref, v2_ref):
        g = own_ref[...]
        for r in range(3):
            g = g + arr_ref[r].astype(F32)
        g_ref[...] = g
        d_ref[...], m2_ref[...], v2_ref[...] = _adam(w_ref[...], g, m_ref[...], v_ref[...])

    blk = pl.BlockSpec((rb, cols), lambda i: (i, 0))
    shp = jax.ShapeDtypeStruct((rows, cols), F32)
    return pl.pallas_call(
        body, name=name, grid=(rows // rb,), out_shape=(shp, shp, shp, shp),
        in_specs=[blk, blk, pl.BlockSpec((3, rb, cols), lambda i: (0, i, 0)), blk, blk],
        out_specs=(blk, blk, blk, blk),
        compiler_params=pltpu.CompilerParams(dimension_semantics=("arbitrary",)),
    )(w, own, arrived, m, v)


def _adamw_ada(rb, w, sc, dmod_cols, m, v):
    rows, cols = w.shape

    def body(w_ref, sc_ref, dm_ref, m_ref, v_ref, g_ref, d_ref, m2_ref, v2_ref):
        g = _dot_tn(sc_ref[...].astype(BF16), dm_ref[...].astype(BF16))
        g_ref[...] = g
        d_ref[...], m2_ref[...], v2_ref[...] = _adam(w_ref[...], g, m_ref[...], v_ref[...])

    blk = pl.BlockSpec((rb, cols), lambda i: (i, 0))
    shp = jax.ShapeDtypeStruct((rows, cols), F32)
    return pl.pallas_call(
        body, name="adamw_ada", grid=(rows // rb,), out_shape=(shp, shp, shp, shp),
        in_specs=[blk, pl.BlockSpec((N_DEV, rb), lambda i: (0, i)), _full((N_DEV, cols)), blk, blk],
        out_specs=(blk, blk, blk, blk),
        compiler_params=pltpu.CompilerParams(dimension_semantics=("arbitrary",)),
    )(w, sc, dmod_cols, m, v)


def _unfold(acc_rows):
    return jnp.concatenate([acc_rows[:, :D_A], acc_rows[:, D_A:]], axis=0)


def _adamw_small(total, params):
    n = len(params)
    flat = [a for p in params for a in p]

    def body(*refs):
        s_ref = refs[0]
        p_refs = refs[1:1 + 3 * n]
        loss_ref = refs[1 + 3 * n]
        o_refs = refs[2 + 3 * n:]
        d_b_ada = s_ref[0:6, :]
        for b in range(1, N_DEV):
            d_b_ada = d_b_ada + s_ref[8 * b:8 * b + 6, :]
        tot = s_ref[PACK_SHIFT:PACK_ROWS, :]
        loss_ref[...] = jnp.broadcast_to(tot[ROW_LOSS:ROW_LOSS + 1, 0:GROUP], (8, GROUP))
        mask = _tril_mask()
        ws = _unfold(tot[ROW_WS:ROW_WS + 64, :])
        wp = _unfold(tot[ROW_WP:ROW_WP + 64, :])
        grads = [
            d_b_ada,
            tot[ROW_N1PRE:ROW_N1PRE + 1, :], tot[ROW_N1POST:ROW_N1POST + 1, :],
            tot[ROW_N2PRE:ROW_N2PRE + 1, :], tot[ROW_N2POST:ROW_N2POST + 1, :],
            tot[ROW_LN:ROW_LN + 1, :D_A], tot[ROW_LN:ROW_LN + 1, D_A:],
            tot[ROW_POOL:ROW_POOL + 1, :D_B], tot[ROW_POOL:ROW_POOL + 1, D_B:],
            tot[ROW_BS:ROW_BS + N_HEADS, 0:GROUP],
            jnp.stack([ws[:, h * GROUP:(h + 1) * GROUP] * mask for h in range(N_HEADS)]),
            jnp.stack([wp[:, g * GROUP:(g + 1) * GROUP] for g in range(len(WINDOWS))]),
        ]
        for k in range(n):
            w_ref, m_ref, v_ref = p_refs[3 * k:3 * k + 3]
            g = grads[k]
            o_refs[4 * k][...] = g
            o_refs[4 * k + 1][...], o_refs[4 * k + 2][...], o_refs[4 * k + 3][...] = _adam(
                w_ref[...], g, m_ref[...], v_ref[...])

    vm = pl.BlockSpec(memory_space=pltpu.VMEM)
    out_shape = [jax.ShapeDtypeStruct((8, GROUP), F32)]
    for w, _, _ in params:
        out_shape += [jax.ShapeDtypeStruct(w.shape, F32)] * 4
    return pl.pallas_call(
        body, name="adamw_small", out_shape=tuple(out_shape),
        in_specs=[vm] * (1 + 3 * n), out_specs=tuple([vm] * len(out_shape)),
    )(total, *flat)


TT_ATTN_FWD = 512
TT_MLP = 256
TT_WGRAD = 512
TT_ATTN_BWD = 256


def kernel(x, c, w_ada, b_ada, norm1_pre, norm1_post, w_in, w_spatial, b_spatial, ln_v_gain, ln_v_bias, w_pool, b_pool, pool_scale, w_out, norm2_pre, norm2_post, w_fc1, w_fc2, loss_target, m_w_ada, m_b_ada, m_norm1_pre, m_norm1_post, m_w_in, m_w_spatial, m_b_spatial, m_ln_v_gain, m_ln_v_bias, m_w_pool, m_b_pool, m_pool_scale, m_w_out, m_norm2_pre, m_norm2_post, m_w_fc1, m_w_fc2, v_w_ada, v_b_ada, v_norm1_pre, v_norm1_post, v_w_in, v_w_spatial, v_b_spatial, v_ln_v_gain, v_ln_v_bias, v_w_pool, v_b_pool, v_pool_scale, v_w_out, v_norm2_pre, v_norm2_post, v_w_fc1, v_w_fc2):
    t_len = x.shape[1]
    me = 4 * lax.axis_index("x") + 2 * lax.axis_index("y") + lax.axis_index("c")
    ada_cols = w_ada.shape[1]
    tt = lambda want: min(want, t_len)

    x2 = x.reshape(t_len, D)
    tgt = loss_target.reshape(t_len, D)
    row = lambda a: a.reshape(1, -1)

    b_my = lax.dynamic_slice_in_dim(b_ada, me * ada_cols, ada_cols).reshape(1, ada_cols)
    modp, sc = _ada_fwd(jnp.broadcast_to(c, (8, D)), w_ada, b_my)
    mod = jnp.concatenate([modp.reshape(6, D), jnp.zeros((2, D), F32)], axis=0)

    g_in, g_out, g_w1, g_w2 = _ag_weights([w_in.T, w_out, w_fc1, w_fc2])
    w_in_t = g_in.reshape(D_Z, D)
    w_out_all = g_out.reshape(D, D)

    bs_rows = jnp.repeat(b_spatial.T, GROUP, axis=1)
    attn_consts = (w_spatial, bs_rows, row(ln_v_gain), row(ln_v_bias), w_pool, row(b_pool), row(pool_scale))

    z, cat, mix, x1 = _attn_fwd(tt(TT_ATTN_FWD), x2, mod, row(norm1_pre), row(norm1_post), w_in_t, w_out_all,
                                *attn_consts)
    a, h2, df, dy, red_fwd = _mlp_fwd(tt(TT_MLP), x1, tgt, mod, row(norm2_pre), row(norm2_post), g_w1, g_w2)
    da, r, dmix, dx1, red_bwd = _mlp_bwd(tt(TT_MLP), df, a, dy, x1, mix, mod, row(norm2_pre), row(norm1_post),
                                         g_w1, g_w2)
    own_w1, own_w2, sums_w1, sums_w2 = _mlp_wgrad(tt(TT_WGRAD), r, da, df, h2)
    (grad_x, p_in, p_out, small), (arr_w1, arr_w2) = _attn_bwd(
        tt(TT_ATTN_BWD), dmix, dx1, x2, z, cat, mod, row(norm1_pre), w_in_t, w_out_all, *attn_consts,
        red_fwd, red_bwd, [sums_w1, sums_w2])
    grad_in_t, grad_out, total = _tail_comm(
        [p_in.reshape(N_DEV, D_Z // N_DEV, D), p_out.reshape(N_DEV, D // N_DEV, D)], small, 64)

    grad_w1, d_w1, m_w1, v_w1 = _adamw_chip_sums("adamw_fc1", 256, w_fc1, own_w1, arr_w1, m_w_fc1, v_w_fc1)
    grad_w2, d_w2, m_w2, v_w2 = _adamw_chip_sums("adamw_fc2", 128, w_fc2, own_w2, arr_w2, m_w_fc2, v_w_fc2)
    d_out, m_out, v_out = _adamw_shard("adamw_out", 128, w_out, grad_out, m_w_out, v_w_out)
    d_in_t, m_in_t, v_in_t = _adamw_shard("adamw_in", D_Z // N_DEV, w_in.T, grad_in_t, m_w_in.T, v_w_in.T)
    dmod_all = total[0:TABLE_ROWS, :].reshape(N_DEV, 8, D)[:, :6, :].reshape(N_DEV, 6 * D)
    dmod_cols = lax.dynamic_slice_in_dim(dmod_all, me * ada_cols, ada_cols, axis=1)
    grad_ada, d_ada, m_ada, v_ada = _adamw_ada(256, w_ada, sc, dmod_cols, m_w_ada, v_w_ada)

    six = lambda a: a.reshape(6, D)
    small_params = [
        (six(b_ada), six(m_b_ada), six(v_b_ada)),
        (row(norm1_pre), row(m_norm1_pre), row(v_norm1_pre)),
        (row(norm1_post), row(m_norm1_post), row(v_norm1_post)),
        (row(norm2_pre), row(m_norm2_pre), row(v_norm2_pre)),
        (row(norm2_post), row(m_norm2_post), row(v_norm2_post)),
        (row(ln_v_gain), row(m_ln_v_gain), row(v_ln_v_gain)),
        (row(ln_v_bias), row(m_ln_v_bias), row(v_ln_v_bias)),
        (row(pool_scale), row(m_pool_scale), row(v_pool_scale)),
        (row(b_pool), row(m_b_pool), row(v_b_pool)),
        (b_spatial, m_b_spatial, v_b_spatial),
        (w_spatial, m_w_spatial, v_w_spatial),
        (w_pool, m_w_pool, v_w_pool),
    ]
    outs = _adamw_small(total, small_params)
    loss = outs[0][0, 0]
    names = ["b_ada", "norm1_pre", "norm1_post", "norm2_pre", "norm2_post", "ln_v_gain", "ln_v_bias", "pool_scale",
             "b_pool", "b_spatial", "w_spatial", "w_pool"]
    shapes = dict(b_ada=b_ada.shape, norm1_pre=norm1_pre.shape, norm1_post=norm1_post.shape,
                  norm2_pre=norm2_pre.shape, norm2_post=norm2_post.shape, ln_v_gain=ln_v_gain.shape,
                  ln_v_bias=ln_v_bias.shape, pool_scale=pool_scale.shape, b_pool=b_pool.shape,
                  b_spatial=b_spatial.shape, w_spatial=w_spatial.shape, w_pool=w_pool.shape)
    res = {}
    for k, nm in enumerate(names):
        res[nm] = tuple(o.reshape(shapes[nm]) for o in outs[1 + 4 * k:5 + 4 * k])
    res["w_ada"] = (grad_ada, d_ada, m_ada, v_ada)
    res["w_in"] = (grad_in_t.T, d_in_t.T, m_in_t.T, v_in_t.T)
    res["w_out"] = (grad_out, d_out, m_out, v_out)
    res["w_fc1"] = (grad_w1, d_w1, m_w1, v_w1)
    res["w_fc2"] = (grad_w2, d_w2, m_w2, v_w2)

    order = ["w_ada", "b_ada", "norm1_pre", "norm1_post", "w_in", "w_spatial", "b_spatial", "ln_v_gain", "ln_v_bias",
             "w_pool", "b_pool", "pool_scale", "w_out", "norm2_pre", "norm2_post", "w_fc1", "w_fc2"]
    return (loss, grad_x.reshape(x.shape),
            *[res[nm][0] for nm in order], *[res[nm][1] for nm in order],
            *[res[nm][2] for nm in order], *[res[nm][3] for nm in order])
```

```python
import functools

import jax
import jax.numpy as jnp
from jax import lax
from jax.experimental import pallas as pl
from jax.experimental.pallas import tpu as pltpu

F32 = jnp.float32
BF16 = jnp.bfloat16
MESH = pl.DeviceIdType.MESH

N_DEV = 8
D = 1024
D_A = 512
D_B = 512
D_Z = 2 * D_A + D_B
N_HEADS = 4
CHUNK = 128
WINDOWS = (2, 4, 8, 16)
GROUP = 128
D_FF = 4096
FF_BLK = D_FF // N_DEV
HALO = 16
EPS = 1e-6
VMEM_LIMIT = 60 * 1024 * 1024

ADAM_LR = 0.001
ADAM_B1 = 0.9
ADAM_B2 = 0.999
ADAM_EPS = 1e-08
ADAM_WD = 0.01
ADAM_STEP = 10

ROW_DMOD = 0
ROW_N1PRE, ROW_N1POST, ROW_N2PRE, ROW_N2POST = 8, 9, 10, 11
ROW_LN = 12
ROW_POOL = 13
ROW_LOSS = 14
ROW_BS = 16
ROW_WS = 24
ROW_WP = 88
SMALL_ROWS = 152
TABLE_ROWS = 8 * N_DEV
PACK_SHIFT = TABLE_ROWS - 8
PACK_ROWS = SMALL_ROWS + PACK_SHIFT
PACK_HALF = PACK_ROWS // 2


def _dot(a, b):
    return jnp.dot(a, b, preferred_element_type=F32)


def _dot_nt(a, b):
    return lax.dot_general(a, b, (((1,), (1,)), ((), ())), preferred_element_type=F32)


def _dot_tn(a, b):
    return lax.dot_general(a, b, (((0,), (0,)), ((), ())), preferred_element_type=F32)


def _rstd(v):
    return lax.rsqrt(jnp.mean(v * v, axis=-1, keepdims=True) + EPS)


def _rms_bwd(d_hat, hat, rstd):
    return rstd * (d_hat - hat * jnp.mean(d_hat * hat, axis=-1, keepdims=True))


_K0 = 0.7978845608028654
_K1 = 0.044715


def _gelu_parts(v):
    t = jnp.tanh(_K0 * (v + _K1 * (v * v * v)))
    return t, v * (0.5 * (1.0 + t))


def _gelu_grad(v, t):
    return 0.5 * (1.0 + t) + (0.5 * v) * (1.0 - t * t) * (_K0 * (1.0 + (3.0 * _K1) * (v * v)))


def _colsum(v):
    return jnp.sum(v, axis=0, keepdims=True)


def _full(shape):
    n = len(shape)
    return pl.BlockSpec(shape, lambda *_: (0,) * n)


def _resident(shape):
    n = len(shape)
    return pl.BlockSpec(shape, lambda *_: (0,) * n, pipeline_mode=pl.Buffered(1))


def _place():
    x, y, c = lax.axis_index("x"), lax.axis_index("y"), lax.axis_index("c")
    return x, y, c


def _flip(v, bit):
    return 1 - v if bit else v


def _peer(x, y, c, k):
    return (_flip(x, (k >> 2) & 1), _flip(y, (k >> 1) & 1), _flip(c, k & 1))


def _index(p):
    return 4 * p[0] + 2 * p[1] + p[2]


def _two_level_gather_begin(x, y, c, out_refs, send_sems, recv_sems):
    me = (x, y, c)
    sibling = (x, y, 1 - c)
    chips = [(1 - x, y), (x, 1 - y), (1 - x, 1 - y)]

    def copy(a, k, block, to):
        ref = out_refs[a].at[_index(block)]
        return pltpu.make_async_remote_copy(
            src_ref=ref, dst_ref=ref, send_sem=send_sems.at[7 * a + k], recv_sem=recv_sems.at[7 * a + k],
            device_id=to, device_id_type=MESH)

    first = []
    for a in range(len(out_refs)):
        first.append(copy(a, 0, me, sibling))
        first += [copy(a, 1 + j, me, (*chip, c)) for j, chip in enumerate(chips)]
    for cp in first:
        cp.start()
    return copy, first, me, sibling, chips


def _two_level_gather_finish(c, n, begun):
    copy, first, me, sibling, chips = begun
    passed = []
    for a in range(n):
        for j, chip in enumerate(chips):
            copy(a, 1 + j, (*chip, c), me).wait_recv()
            fwd = copy(a, 4 + j, (*chip, c), sibling)
            fwd.start()
            passed.append(fwd)
    for a in range(n):
        copy(a, 0, sibling, me).wait_recv()
        for j, chip in enumerate(chips):
            copy(a, 4 + j, (*chip, 1 - c), me).wait_recv()
    for cp in first + passed:
        cp.wait_send()


def _fwd_comm(c8, w_ada, b_my, gathered, kept):
    ncol = w_ada.shape[1]
    n_g, n_k = len(gathered), len(kept)

    def body(c_ref, w_ref, b_ref, *rest):
        g_in, k_in = rest[:n_g], rest[n_g:n_g + n_k]
        modp_ref, sc_ref = rest[n_g + n_k:n_g + n_k + 2]
        g_out = rest[n_g + n_k + 2:2 * n_g + n_k + 2]
        k_out = rest[2 * n_g + n_k + 2:2 * n_g + 2 * n_k + 2]
        cg, mg, part, send_sems, recv_sems, g_send, g_recv = rest[2 * n_g + 2 * n_k + 2:]
        x, y, c = _place()
        me = _index((x, y, c))
        for a in range(n_g):
            g_out[a][me] = g_in[a][...].astype(BF16)
        begun = _two_level_gather_begin(x, y, c, g_out, g_send, g_recv)
        for a in range(n_k):
            k_out[a][...] = k_in[a][...].astype(BF16)

        def c_copy(k):
            p = _peer(x, y, c, k)
            return pltpu.make_async_remote_copy(
                src_ref=c_ref, dst_ref=cg.at[me], send_sem=send_sems.at[k - 1], recv_sem=recv_sems.at[k - 1],
                device_id=p, device_id_type=MESH)

        def c_arrival(k):
            p = _peer(x, y, c, k)
            return pltpu.make_async_remote_copy(
                src_ref=c_ref, dst_ref=cg.at[_index(p)], send_sem=send_sems.at[k - 1], recv_sem=recv_sems.at[k - 1],
                device_id=p, device_id_type=MESH)

        def m_copy(k):
            p = _peer(x, y, c, k)
            return pltpu.make_async_remote_copy(
                src_ref=part, dst_ref=mg.at[me], send_sem=send_sems.at[6 + k], recv_sem=recv_sems.at[6 + k],
                device_id=p, device_id_type=MESH)

        def m_arrival(k):
            p = _peer(x, y, c, k)
            return pltpu.make_async_remote_copy(
                src_ref=part, dst_ref=mg.at[_index(p)], send_sem=send_sems.at[6 + k], recv_sem=recv_sems.at[6 + k],
                device_id=p, device_id_type=MESH)

        for k in range(1, N_DEV):
            c_copy(k).start()
        cg[me] = c_ref[...]
        for k in range(1, N_DEV):
            c_arrival(k).wait_recv()
        c_all = jnp.concatenate([cg[j, 0:1, :] for j in range(N_DEV)], axis=0)
        sc = c_all * jax.nn.sigmoid(c_all)
        sc_ref[...] = sc
        part[...] = _dot(sc.astype(BF16), w_ref[...].astype(BF16)) + b_ref[...]
        for k in range(1, N_DEV):
            m_copy(k).start()
        mg[me] = part[...]
        for k in range(1, N_DEV):
            m_arrival(k).wait_recv()
        for j in range(N_DEV):
            modp_ref[j:j + 1, :] = mg[j, pl.ds(me, 1), :]
        _two_level_gather_finish(c, n_g, begun)
        for k in range(1, N_DEV):
            c_copy(k).wait_send()
            m_copy(k).wait_send()

    vm = pl.BlockSpec(memory_space=pltpu.VMEM)
    outs = pl.pallas_call(
        body, name="fwd_comm",
        out_shape=tuple([jax.ShapeDtypeStruct((N_DEV, ncol), F32), jax.ShapeDtypeStruct((N_DEV, D), F32)]
                        + [jax.ShapeDtypeStruct((N_DEV,) + s.shape, BF16) for s in gathered]
                        + [jax.ShapeDtypeStruct(s.shape, BF16) for s in kept]),
        in_specs=[vm] * (3 + n_g + n_k), out_specs=tuple([vm] * (2 + n_g + n_k)),
        scratch_shapes=[
            pltpu.VMEM((N_DEV, 8, D), F32),
            pltpu.VMEM((N_DEV, N_DEV, ncol), F32),
            pltpu.VMEM((N_DEV, ncol), F32),
            pltpu.SemaphoreType.DMA((2 * (N_DEV - 1),)),
            pltpu.SemaphoreType.DMA((2 * (N_DEV - 1),)),
            pltpu.SemaphoreType.DMA((7 * n_g,)),
            pltpu.SemaphoreType.DMA((7 * n_g,)),
        ],
        compiler_params=pltpu.CompilerParams(vmem_limit_bytes=VMEM_LIMIT),
    )(c8, w_ada, b_my, *gathered, *kept)
    return outs[0], outs[1], outs[2:2 + n_g], outs[2 + n_g:]


FC_EARLY = 6


class _HostedGather:
    def __init__(self, shard_refs, slot_refs, local, direct, forward, send_sems, recv_sems, local_sems):
        self.x, self.y, self.c = _place()
        self.shards, self.slots = shard_refs, slot_refs
        self.local, self.direct, self.forward = local, direct, forward
        self.send_sems, self.recv_sems, self.local_sems = send_sems, recv_sems, local_sems
        self.n_copies = len(direct) + len(forward)

    def _copy(self, a, i):
        sem = a * self.n_copies + i
        if i < len(self.direct):
            rel, slot = self.direct[i]
            src, dst, to = self.shards[a], self.slots[a].at[slot], _peer(self.x, self.y, self.c, rel)
        else:
            slot, sib_slot = self.forward[i - len(self.direct)]
            src, dst, to = self.slots[a].at[slot], self.slots[a].at[sib_slot], (self.x, self.y, 1 - self.c)
        return pltpu.make_async_remote_copy(
            src_ref=src, dst_ref=dst, send_sem=self.send_sems.at[sem], recv_sem=self.recv_sems.at[sem],
            device_id=to, device_id_type=MESH)

    def _local(self, a):
        return pltpu.make_async_copy(self.shards[a], self.slots[a].at[self.local], self.local_sems.at[a])

    def begin(self):
        for a in range(len(self.shards)):
            if self.local is not None:
                self._local(a).start()
            for i in range(len(self.direct)):
                self._copy(a, i).start()

    def pass_on(self):
        for a in range(len(self.shards)):
            for f, (slot, _) in enumerate(self.forward):
                i = [s for _, s in self.direct].index(slot)
                self._copy(a, i).wait_recv()
                self._copy(a, len(self.direct) + f).start()

    def finish(self):
        passed = [slot for slot, _ in self.forward]
        for a in range(len(self.shards)):
            if self.local is not None:
                self._local(a).wait()
            for i in range(self.n_copies):
                if i >= len(self.direct) or self.direct[i][1] not in passed:
                    self._copy(a, i).wait_recv()
                self._copy(a, i).wait_send()


def _tail_comm(parts, small, row_chunk):
    n = len(parts)

    def body(*refs):
        p_refs, small_ref = refs[:n], refs[n]
        g_refs, total_ref = refs[n + 1:2 * n + 1], refs[2 * n + 1]
        scr = refs[2 * n + 2:]
        from_sib = scr[0:n]
        chip_out = scr[n:2 * n]
        chip_in = scr[2 * n:3 * n]
        pack, pack_sib, halves = scr[3 * n:3 * n + 3]
        send_a, recv_a, send_b, recv_b, send_s, recv_s = scr[3 * n + 3:]
        x, y, c = _place()
        me = _index((x, y, c))
        sibling = (x, y, 1 - c)
        my_chip = 2 * x + y
        others = [(1 - x, y), (x, 1 - y), (1 - x, 1 - y)]
        my_half = pl.ds(pl.multiple_of(PACK_HALF * c, 8), PACK_HALF)

        def pack_to_sibling():
            return pltpu.make_async_remote_copy(
                src_ref=pack, dst_ref=pack_sib, send_sem=send_s.at[0], recv_sem=recv_s.at[0],
                device_id=sibling, device_id_type=MESH)

        def half_to_chip(r):
            return pltpu.make_async_remote_copy(
                src_ref=halves.at[my_chip], dst_ref=halves.at[my_chip],
                send_sem=send_s.at[1 + r], recv_sem=recv_s.at[1 + r],
                device_id=(*others[r], c), device_id_type=MESH)

        def half_from_chip(r):
            k = 2 * others[r][0] + others[r][1]
            return pltpu.make_async_remote_copy(
                src_ref=halves.at[k], dst_ref=halves.at[k], send_sem=send_s.at[1 + r], recv_sem=recv_s.at[1 + r],
                device_id=(*others[r], c), device_id_type=MESH)

        def total_to_sibling():
            return pltpu.make_async_remote_copy(
                src_ref=total_ref.at[my_half], dst_ref=total_ref.at[my_half],
                send_sem=send_s.at[4], recv_sem=recv_s.at[4], device_id=sibling, device_id_type=MESH)

        def total_from_sibling():
            sib_half = pl.ds(pl.multiple_of(PACK_HALF * (1 - c), 8), PACK_HALF)
            return pltpu.make_async_remote_copy(
                src_ref=total_ref.at[sib_half], dst_ref=total_ref.at[sib_half],
                send_sem=send_s.at[4], recv_sem=recv_s.at[4], device_id=sibling, device_id_type=MESH)

        pack[0:TABLE_ROWS, :] = jnp.zeros((TABLE_ROWS, D), F32)
        pack[pl.ds(pl.multiple_of(8 * me, 8), 8), :] = small_ref[0:8, :]
        pack[TABLE_ROWS:PACK_ROWS, :] = small_ref[8:SMALL_ROWS, :]
        pack_to_sibling().start()

        def to_sibling(a, k):
            return pltpu.make_async_remote_copy(
                src_ref=p_refs[a].at[2 * k + (1 - c)], dst_ref=from_sib[a].at[k],
                send_sem=send_a.at[a], recv_sem=recv_a.at[a], device_id=sibling, device_id_type=MESH)

        def all_from_sibling(a):
            return pltpu.make_async_remote_copy(
                src_ref=from_sib[a], dst_ref=from_sib[a], send_sem=send_a.at[a], recv_sem=recv_a.at[a],
                device_id=sibling, device_id_type=MESH)

        def to_chip(a, r):
            return pltpu.make_async_remote_copy(
                src_ref=chip_out[a].at[r], dst_ref=chip_in[a].at[r],
                send_sem=send_b.at[3 * a + r], recv_sem=recv_b.at[3 * a + r],
                device_id=(*others[r], c), device_id_type=MESH)

        for a in range(n):
            for k in range(4):
                to_sibling(a, k).start()
        pack_to_sibling().wait_recv()
        halves[my_chip] = pack[my_half, :] + pack_sib[my_half, :]
        for r in range(3):
            half_to_chip(r).start()
        for a in range(n):
            all_from_sibling(a).wait_recv()
            rows = p_refs[a].shape[1]
            for r in range(3):
                k = 2 * others[r][0] + others[r][1]
                for s in range(0, rows, row_chunk):
                    sl = pl.ds(s, row_chunk)
                    chip_out[a][r, sl, :] = (p_refs[a][2 * k + c, sl, :].astype(F32)
                                             + from_sib[a][k, sl, :].astype(F32)).astype(BF16)
                to_chip(a, r).start()
            for s in range(0, rows, row_chunk):
                sl = pl.ds(s, row_chunk)
                g_refs[a][sl, :] = (p_refs[a][2 * my_chip + c, sl, :].astype(F32)
                                    + from_sib[a][my_chip, sl, :].astype(F32))
        for r in range(3):
            half_from_chip(r).wait_recv()
        total_ref[my_half, :] = ((halves[0] + halves[1]) + halves[2]) + halves[3]
        total_to_sibling().start()
        for a in range(n):
            rows = p_refs[a].shape[1]
            for r in range(3):
                to_chip(a, r).wait_recv()
                for s in range(0, rows, row_chunk):
                    sl = pl.ds(s, row_chunk)
                    g_refs[a][sl, :] = g_refs[a][sl, :] + chip_in[a][r, sl, :].astype(F32)
        total_from_sibling().wait_recv()
        for a in range(n):
            all_from_sibling(a).wait_send()
            for r in range(3):
                to_chip(a, r).wait_send()
        pack_to_sibling().wait_send()
        for r in range(3):
            half_to_chip(r).wait_send()
        total_to_sibling().wait_send()

    vm = pl.BlockSpec(memory_space=pltpu.VMEM)
    return pl.pallas_call(
        body, name="tail_comm",
        out_shape=tuple([jax.ShapeDtypeStruct(p.shape[1:], F32) for p in parts]
                        + [jax.ShapeDtypeStruct((PACK_ROWS, D), F32)]),
        in_specs=[vm] * (n + 1), out_specs=tuple([vm] * (n + 1)),
        scratch_shapes=(
            [pltpu.VMEM((4,) + p.shape[1:], BF16) for p in parts]
            + [pltpu.VMEM((3,) + p.shape[1:], BF16) for p in parts]
            + [pltpu.VMEM((3,) + p.shape[1:], BF16) for p in parts]
            + [pltpu.VMEM((PACK_ROWS, D), F32), pltpu.VMEM((PACK_ROWS, D), F32),
               pltpu.VMEM((4, PACK_HALF, D), F32)]
            + [pltpu.SemaphoreType.DMA((n,)), pltpu.SemaphoreType.DMA((n,)),
               pltpu.SemaphoreType.DMA((3 * n,)), pltpu.SemaphoreType.DMA((3 * n,)),
               pltpu.SemaphoreType.DMA((5,)), pltpu.SemaphoreType.DMA((5,))]),
        compiler_params=pltpu.CompilerParams(vmem_limit_bytes=VMEM_LIMIT),
    )(*parts, small)


def _tril_mask():
    row = lax.broadcasted_iota(jnp.int32, (CHUNK, CHUNK), 0)
    col = lax.broadcasted_iota(jnp.int32, (CHUNK, CHUNK), 1)
    return (col <= row).astype(F32)


def _window_sums(ext):
    s2 = ext + pltpu.roll(ext, 1, 0)
    t4 = s2[:, GROUP:]
    s4 = t4 + pltpu.roll(t4, 2, 0)
    t8 = s4[:, GROUP:]
    s8 = t8 + pltpu.roll(t8, 4, 0)
    t16 = s8[:, GROUP:]
    s16 = t16 + pltpu.roll(t16, 8, 0)
    return [s2[:, :GROUP], s4[:, :GROUP], s8[:, :GROUP], s16]


def _inv_counts(first_pos, rows):
    pos = first_pos + lax.broadcasted_iota(jnp.int32, (rows, 1), 0)
    return [1.0 / jnp.minimum(pos + 1, w).astype(F32) for w in WINDOWS]


def _pool_diff(zb, halo, first_pos):
    tt = zb.shape[0]
    sums = _window_sums(jnp.concatenate([halo, zb], axis=0))
    inv = _inv_counts(first_pos, tt)
    return [sums[g][HALO:, :] * inv[g] - zb[:, g * GROUP:(g + 1) * GROUP] for g in range(len(WINDOWS))]


def _attn_fwd(tt, x, mod, n1pre, n1post, w_in_t, w_out, w_sp, bs_rows, ln_g, ln_b, w_pool, b_pool, pool_scale,
              fc_shards):
    t_len = x.shape[0]
    nt = t_len // tt
    n_fc = len(fc_shards)

    def body(x_ref, mod_ref, n1pre_ref, n1post_ref, win_ref, wout_ref, wsp_ref, bs_ref, lng_ref, lnb_ref,
             wp_ref, bp_ref, ps_ref, *rest):
        shard_refs = rest[:n_fc]
        z_ref, cat_ref, mix_ref, x1_ref = rest[n_fc:n_fc + 4]
        slot_refs = rest[n_fc + 4:2 * n_fc + 4]
        carry, send_sems, recv_sems, local_sems = rest[2 * n_fc + 4:]
        i = pl.program_id(0)
        gather = _HostedGather(shard_refs, slot_refs, 0, [(1, 1), (2, 2), (4, 4)], [(2, 3), (4, 5)],
                               send_sems, recv_sems, local_sems)

        @pl.when(i == 0)
        def _():
            gather.begin()
            carry[...] = jnp.zeros_like(carry)

        @pl.when(i == nt - 1)
        def _():
            gather.pass_on()

        xv = x_ref[...]
        shift1, scale1, gate1 = mod_ref[0:1, :], mod_ref[1:2, :], mod_ref[2:3, :]
        h1 = (xv * _rstd(xv) * n1pre_ref[...]) * (1.0 + scale1) + shift1
        z = _dot_nt(h1.astype(BF16), win_ref[...])
        z_ref[...] = z

        _, ga = _gelu_parts(z[:, :2 * D_A])
        u, vr = ga[:, :D_A], ga[:, D_A:]
        dv = vr - jnp.mean(vr, axis=-1, keepdims=True)
        v = (dv * lax.rsqrt(jnp.mean(dv * dv, axis=-1, keepdims=True) + EPS)) * lng_ref[...] + lnb_ref[...]
        vb = v.astype(BF16)
        mask = _tril_mask()
        wc = [(wsp_ref[h] * mask).astype(BF16) for h in range(N_HEADS)]
        for ch in range(tt // CHUNK):
            rows = slice(ch * CHUNK, (ch + 1) * CHUNK)
            for h in range(N_HEADS):
                cols = slice(h * GROUP, (h + 1) * GROUP)
                mixed = _dot(wc[h], vb[rows, cols]) + bs_ref[:, cols]
                cat_ref[rows, cols] = (u[rows, cols] * mixed).astype(BF16)

        zb = z[:, 2 * D_A:]
        diff = _pool_diff(zb, carry[...], i * tt)
        carry[...] = zb[tt - HALO:, :]
        for g in range(len(WINDOWS)):
            cols = slice(g * GROUP, (g + 1) * GROUP)
            pre = _dot(diff[g].astype(BF16), wp_ref[g].astype(BF16)) + bp_ref[:, cols]
            cat_ref[:, D_A + g * GROUP:D_A + (g + 1) * GROUP] = (pre * ps_ref[:, cols]).astype(BF16)

        mix = _dot(cat_ref[...], wout_ref[...])
        mix_ref[...] = mix
        x1_ref[...] = xv + gate1 * (mix * _rstd(mix) * n1post_ref[...])

        @pl.when(i == nt - 1)
        def _():
            gather.finish()

    tile = lambda w: pl.BlockSpec((tt, w), lambda i: (i, 0))
    hbm = pl.BlockSpec(memory_space=pl.ANY)
    outs = pl.pallas_call(
        body, name="attn_fwd", grid=(nt,),
        out_shape=tuple([jax.ShapeDtypeStruct((t_len, D_Z), F32), jax.ShapeDtypeStruct((t_len, D), BF16),
                         jax.ShapeDtypeStruct((t_len, D), F32), jax.ShapeDtypeStruct((t_len, D), F32)]
                        + [jax.ShapeDtypeStruct((FC_EARLY,) + s.shape, BF16) for s in fc_shards]),
        in_specs=[tile(D), _full((8, D)), _full((1, D)), _full((1, D)), _resident((D_Z, D)), _resident((D, D)),
                  _full((N_HEADS, CHUNK, CHUNK)), _full((CHUNK, D_A)), _full((1, D_A)), _full((1, D_A)),
                  _full((len(WINDOWS), GROUP, GROUP)), _full((1, D_B)), _full((1, D_B))] + [hbm] * n_fc,
        out_specs=tuple([tile(D_Z), tile(D), tile(D), tile(D)] + [hbm] * n_fc),
        scratch_shapes=[pltpu.VMEM((HALO, D_B), F32), pltpu.SemaphoreType.DMA((5 * n_fc,)),
                        pltpu.SemaphoreType.DMA((5 * n_fc,)), pltpu.SemaphoreType.DMA((n_fc,))],
        compiler_params=pltpu.CompilerParams(dimension_semantics=("arbitrary",), vmem_limit_bytes=VMEM_LIMIT),
    )(x, mod, n1pre, n1post, w_in_t, w_out, w_sp, bs_rows, ln_g, ln_b, w_pool, b_pool, pool_scale, *fc_shards)
    return outs[:4], outs[4:]


def _mlp_fwd_early(tt, x1, mod, n2pre, w1_early, w2_early, fc_shards):
    t_len = x1.shape[0]
    nt = t_len // tt
    n_fc = len(fc_shards)
    n_late = N_DEV - FC_EARLY

    def body(x1_ref, mod_ref, n2pre_ref, w1_ref, w2_ref, *rest):
        shard_refs = rest[:n_fc]
        a_ref, h2_ref, f_ref = rest[n_fc:n_fc + 3]
        slot_refs = rest[n_fc + 3:2 * n_fc + 3]
        send_sems, recv_sems, local_sems = rest[2 * n_fc + 3:]
        i = pl.program_id(0)
        gather = _HostedGather(shard_refs, slot_refs, None, [(6, 0)], [(0, 1)], send_sems, recv_sems, local_sems)

        @pl.when(i == 0)
        def _():
            gather.begin()

        @pl.when(i == (3 * nt) // 4)
        def _():
            gather.pass_on()

        x1v = x1_ref[...]
        shift2, scale2 = mod_ref[3:4, :], mod_ref[4:5, :]
        h2 = ((x1v * _rstd(x1v) * n2pre_ref[...]) * (1.0 + scale2) + shift2).astype(BF16)
        h2_ref[...] = h2
        for j in range(FC_EARLY):
            cols = slice(j * FF_BLK, (j + 1) * FF_BLK)
            a = _dot(h2, w1_ref[j])
            a_ref[:, cols] = a
            r = jnp.maximum(a, 0.0)
            contrib = _dot((r * r).astype(BF16), w2_ref[j])
            if j == 0:
                f_ref[...] = contrib
            else:
                f_ref[...] += contrib

        @pl.when(i == nt - 1)
        def _():
            gather.finish()

    tile = lambda w: pl.BlockSpec((tt, w), lambda i: (i, 0))
    hbm = pl.BlockSpec(memory_space=pl.ANY)
    outs = pl.pallas_call(
        body, name="mlp_fwd_early", grid=(nt,),
        out_shape=tuple([jax.ShapeDtypeStruct((t_len, FC_EARLY * FF_BLK), F32),
                         jax.ShapeDtypeStruct((t_len, D), BF16), jax.ShapeDtypeStruct((t_len, D), F32)]
                        + [jax.ShapeDtypeStruct((n_late,) + s.shape, BF16) for s in fc_shards]),
        in_specs=[tile(D), _full((8, D)), _full((1, D)),
                  _resident((FC_EARLY, D, FF_BLK)), _resident((FC_EARLY, FF_BLK, D))] + [hbm] * n_fc,
        out_specs=tuple([tile(FC_EARLY * FF_BLK), tile(D), tile(D)] + [hbm] * n_fc),
        scratch_shapes=[pltpu.SemaphoreType.DMA((2 * n_fc,)), pltpu.SemaphoreType.DMA((2 * n_fc,)),
                        pltpu.SemaphoreType.DMA((n_fc,))],
        compiler_params=pltpu.CompilerParams(dimension_semantics=("arbitrary",), vmem_limit_bytes=VMEM_LIMIT),
    )(x1, mod, n2pre, w1_early, w2_early, *fc_shards)
    return outs[:3], outs[3:]


def _mlp_fwd_late(tt, x1, h2, f_early, tgt, mod, n2post, w1_late, w2_late):
    t_len = x1.shape[0]
    nt = t_len // tt
    n_late = N_DEV - FC_EARLY

    def body(x1_ref, h2_ref, fe_ref, tgt_ref, mod_ref, n2post_ref, w1_ref, w2_ref,
             a_ref, df_ref, dy_ref, red_ref):
        i = pl.program_id(0)

        @pl.when(i == 0)
        def _():
            red_ref[...] = jnp.zeros_like(red_ref)

        x1v = x1_ref[...]
        gate2 = mod_ref[5:6, :]
        h2 = h2_ref[...]
        f = fe_ref[...]
        for j in range(n_late):
            cols = slice(j * FF_BLK, (j + 1) * FF_BLK)
            a = _dot(h2, w1_ref[j])
            a_ref[:, cols] = a
            r = jnp.maximum(a, 0.0)
            f = f + _dot((r * r).astype(BF16), w2_ref[j])
        rf = _rstd(f)
        fhat = f * rf
        nf = fhat * n2post_ref[...]
        err = (x1v + gate2 * nf) - tgt_ref[...]
        dy = err * (1.0 / D)
        dy_ref[...] = dy
        dnf = dy * gate2
        df_ref[...] = _rms_bwd(dnf * n2post_ref[...], fhat, rf).astype(BF16)
        red_ref[0:1, :] += _colsum(dy * nf)
        red_ref[1:2, :] += _colsum(dnf * fhat)
        red_ref[2:3, :] += _colsum(0.5 * jnp.mean(err * err, axis=-1, keepdims=True)) * jnp.ones((1, D), F32)

    tile = lambda w: pl.BlockSpec((tt, w), lambda i: (i, 0))
    return pl.pallas_call(
        body, name="mlp_fwd_late", grid=(nt,),
        out_shape=(jax.ShapeDtypeStruct((t_len, n_late * FF_BLK), F32), jax.ShapeDtypeStruct((t_len, D), BF16),
                   jax.ShapeDtypeStruct((t_len, D), F32), jax.ShapeDtypeStruct((8, D), F32)),
        in_specs=[tile(D), tile(D), tile(D), tile(D), _full((8, D)), _full((1, D)),
                  _resident((n_late, D, FF_BLK)), _resident((n_late, FF_BLK, D))],
        out_specs=(tile(n_late * FF_BLK), tile(D), tile(D), _full((8, D))),
        compiler_params=pltpu.CompilerParams(dimension_semantics=("arbitrary",), vmem_limit_bytes=VMEM_LIMIT),
    )(x1, h2, f_early, tgt, mod, n2post, w1_late, w2_late)


def _mlp_bwd(tt, df, a_early, a_late, dy, x1, mix, mod, n2pre, n1post, w1_early, w2_early, w1_late, w2_late):
    t_len = x1.shape[0]
    nt = t_len // tt
    n_late = N_DEV - FC_EARLY

    def body(df_ref, ae_ref, al_ref, dy_ref, x1_ref, mix_ref, mod_ref, n2pre_ref, n1post_ref,
             w1e_ref, w2e_ref, w1l_ref, w2l_ref, da_ref, r_ref, dmix_ref, dx1_ref, red_ref, dh2_acc):
        i = pl.program_id(0)

        @pl.when(i == 0)
        def _():
            red_ref[...] = jnp.zeros_like(red_ref)

        dfv = df_ref[...]
        for j in range(N_DEV):
            cols = slice(j * FF_BLK, (j + 1) * FF_BLK)
            if j < FC_EARLY:
                w1, w2, a = w1e_ref[j], w2e_ref[j], ae_ref[:, cols]
            else:
                jl = j - FC_EARLY
                w1, w2, a = w1l_ref[jl], w2l_ref[jl], al_ref[:, jl * FF_BLK:(jl + 1) * FF_BLK]
            dr = _dot_nt(dfv, w2)
            ra = jnp.maximum(a, 0.0)
            da = (dr * (2.0 * ra)).astype(BF16)
            da_ref[:, cols] = da
            r_ref[:, cols] = (ra * ra).astype(BF16)
            contrib = _dot_nt(da, w1)
            if j == 0:
                dh2_acc[...] = contrib
            else:
                dh2_acc[...] += contrib
        dh2 = dh2_acc[...]
        gate1, scale2 = mod_ref[2:3, :], mod_ref[4:5, :]
        x1v = x1_ref[...]
        r2 = _rstd(x1v)
        xhat = x1v * r2
        n2 = xhat * n2pre_ref[...]
        dn2 = dh2 * (1.0 + scale2)
        dx1 = dy_ref[...] + _rms_bwd(dn2 * n2pre_ref[...], xhat, r2)
        dx1_ref[...] = dx1
        mixv = mix_ref[...]
        rm = _rstd(mixv)
        mhat = mixv * rm
        dnm = dx1 * gate1
        dmix_ref[...] = _rms_bwd(dnm * n1post_ref[...], mhat, rm).astype(BF16)
        red_ref[0:1, :] += _colsum(dh2)
        red_ref[1:2, :] += _colsum(dh2 * n2)
        red_ref[2:3, :] += _colsum(dn2 * xhat)
        red_ref[3:4, :] += _colsum(dx1 * (mhat * n1post_ref[...]))
        red_ref[4:5, :] += _colsum(dnm * mhat)

    tile = lambda w: pl.BlockSpec((tt, w), lambda i: (i, 0))
    return pl.pallas_call(
        body, name="mlp_bwd", grid=(nt,),
        out_shape=(jax.ShapeDtypeStruct((t_len, D_FF), BF16), jax.ShapeDtypeStruct((t_len, D_FF), BF16),
                   jax.ShapeDtypeStruct((t_len, D), BF16), jax.ShapeDtypeStruct((t_len, D), F32),
                   jax.ShapeDtypeStruct((8, D), F32)),
        in_specs=[tile(D), tile(FC_EARLY * FF_BLK), tile(n_late * FF_BLK), tile(D), tile(D), tile(D),
                  _full((8, D)), _full((1, D)), _full((1, D)),
                  _resident((FC_EARLY, D, FF_BLK)), _resident((FC_EARLY, FF_BLK, D)),
                  _resident((n_late, D, FF_BLK)), _resident((n_late, FF_BLK, D))],
        out_specs=(tile(D_FF), tile(D_FF), tile(D), tile(D), _full((8, D))),
        scratch_shapes=[pltpu.VMEM((tt, D), F32)],
        compiler_params=pltpu.CompilerParams(dimension_semantics=("arbitrary",), vmem_limit_bytes=VMEM_LIMIT),
    )(df, a_early, a_late, dy, x1, mix, mod, n2pre, n1post, w1_early, w2_early, w1_late, w2_late)


def _mlp_wgrad(tt, r, da, df, h2):
    t_len = df.shape[0]
    nt = t_len // tt

    def relation(j):
        return jnp.where(j < 4, 2 * j + 1, (2 * j - 6) % N_DEV)

    def body(r_ref, da_ref, df_ref, h2_ref, own1_ref, own2_ref, out1_ref, out2_ref,
             acc1, acc2, snd1, snd2, sib1, sib2, send_sems, recv_sems):
        j, t = pl.program_id(0), pl.program_id(1)
        rows = pl.ds(pl.multiple_of(t * tt, tt), tt)
        x, y, c = _place()
        accs, snds, sibs = (acc1, acc2), (snd1, snd2), (sib1, sib2)

        def to_sibling(a, jj):
            return pltpu.make_async_remote_copy(
                src_ref=snds[a].at[jj % 2], dst_ref=sibs[a].at[jj],
                send_sem=send_sems.at[4 * a + jj], recv_sem=recv_sems.at[4 * a + jj],
                device_id=(x, y, 1 - c), device_id_type=MESH)

        @pl.when(t == 0)
        def _():
            acc2[...] = jnp.zeros_like(acc2)
            acc1[...] = jnp.zeros_like(acc1)

        acc2[...] += _dot_tn(r_ref[...], df_ref[rows, :])
        acc1[...] += _dot_tn(h2_ref[rows, :], da_ref[...])

        @pl.when((t == nt - 1) & (j < 4))
        def _():
            for a in range(2):
                @pl.when(j >= 2)
                def _():
                    to_sibling(a, j - 2).wait_send()

                snds[a][j % 2] = accs[a][...].astype(BF16)
                to_sibling(a, j).start()

        @pl.when((t == nt - 1) & (j >= 4))
        def _():
            jj = (j - 3) % 4
            for a, (own_ref, out_ref) in enumerate(((own1_ref, out1_ref), (own2_ref, out2_ref))):
                to_sibling(a, jj).wait_recv()

                @pl.when(j < N_DEV - 1)
                def _():
                    out_ref[0] = (accs[a][...] + sibs[a][jj].astype(F32)).astype(BF16)

                @pl.when(j == N_DEV - 1)
                def _():
                    own_ref[...] = accs[a][...] + sibs[a][jj].astype(F32)
                    to_sibling(a, 2).wait_send()
                    to_sibling(a, 3).wait_send()

    blk = pl.BlockSpec((tt, FF_BLK), lambda j, t: (t, relation(j)))
    chip = lambda j, t: (jnp.clip(j - 4, 0, 2), 0, 0)
    return pl.pallas_call(
        body, name="mlp_wgrad", grid=(N_DEV, nt),
        out_shape=(jax.ShapeDtypeStruct((D, FF_BLK), F32), jax.ShapeDtypeStruct((FF_BLK, D), F32),
                   jax.ShapeDtypeStruct((3, D, FF_BLK), BF16), jax.ShapeDtypeStruct((3, FF_BLK, D), BF16)),
        in_specs=[blk, blk, _resident((t_len, D)), _resident((t_len, D))],
        out_specs=(_full((D, FF_BLK)), _full((FF_BLK, D)),
                   pl.BlockSpec((1, D, FF_BLK), chip), pl.BlockSpec((1, FF_BLK, D), chip)),
        scratch_shapes=[pltpu.VMEM((D, FF_BLK), F32), pltpu.VMEM((FF_BLK, D), F32),
                        pltpu.VMEM((2, D, FF_BLK), BF16), pltpu.VMEM((2, FF_BLK, D), BF16),
                        pltpu.VMEM((4, D, FF_BLK), BF16), pltpu.VMEM((4, FF_BLK, D), BF16),
                        pltpu.SemaphoreType.DMA((8,)), pltpu.SemaphoreType.DMA((8,))],
        compiler_params=pltpu.CompilerParams(dimension_semantics=("arbitrary", "arbitrary"),
                                             vmem_limit_bytes=VMEM_LIMIT),
    )(r, da, df, h2)


def _acc_rows(ref, row0, k, val):
    half = CHUNK // 2
    ref[row0:row0 + half, k * GROUP:(k + 1) * GROUP] += val[:half, :]
    ref[row0:row0 + half, D_A + k * GROUP:D_A + (k + 1) * GROUP] += val[half:, :]


def _attn_bwd(tt, dmix, dx1, x, z, cat, mod, n1pre, w_in_t, w_out, w_sp, bs_rows, ln_g, ln_b, w_pool, b_pool,
              pool_scale, red_fwd, red_bwd, chip_sums):
    t_len = x.shape[0]
    nt = t_len // tt
    hb = tt // HALO
    n_sums = len(chip_sums)

    def body(dmix_ref, dx1_ref, x_ref, z_ref, zprev_ref, cat_ref, mod_ref, n1pre_ref, win_ref, wout_ref, wsp_ref,
             bs_ref, lng_ref, lnb_ref, wp_ref, bp_ref, ps_ref, redf_ref, redb_ref, *rest):
        sum_out = rest[:n_sums]
        gx_ref, gwin_ref, gwout_ref, small_ref = rest[n_sums:n_sums + 4]
        sum_in = rest[n_sums + 4:2 * n_sums + 4]
        carry, acc_in, acc_out, dz_scr, bs_acc, send_sems, recv_sems = rest[2 * n_sums + 4:]
        s = pl.program_id(0)
        i = nt - 1 - s
        px, py, pc = _place()

        def chip_copy(a, r):
            return pltpu.make_async_remote_copy(
                src_ref=sum_out[a].at[r], dst_ref=sum_in[a].at[r],
                send_sem=send_sems.at[3 * a + r], recv_sem=recv_sems.at[3 * a + r],
                device_id=_peer(px, py, pc, 2 * (r + 1)), device_id_type=MESH)

        @pl.when(s == 0)
        def _():
            for a in range(n_sums):
                for r in range(3):
                    chip_copy(a, r).start()
            carry[...] = jnp.zeros_like(carry)
            acc_in[...] = jnp.zeros_like(acc_in)
            acc_out[...] = jnp.zeros_like(acc_out)
            bs_acc[...] = jnp.zeros_like(bs_acc)
            small_ref[...] = jnp.zeros_like(small_ref)
            small_ref[ROW_DMOD + 2:ROW_DMOD + 3, :] = redb_ref[3:4, :]
            small_ref[ROW_DMOD + 3:ROW_DMOD + 5, :] = redb_ref[0:2, :]
            small_ref[ROW_DMOD + 5:ROW_DMOD + 6, :] = redf_ref[0:1, :]
            small_ref[ROW_N1POST:ROW_N1POST + 1, :] = redb_ref[4:5, :]
            small_ref[ROW_N2PRE:ROW_N2PRE + 1, :] = redb_ref[2:3, :]
            small_ref[ROW_N2POST:ROW_N2POST + 1, :] = redf_ref[1:2, :]
            small_ref[ROW_LOSS:ROW_LOSS + 1, :] = redf_ref[2:3, :]

        dmixv = dmix_ref[...]
        dcat = _dot_nt(dmixv, wout_ref[...])
        acc_out[...] += _dot_tn(cat_ref[...], dmixv)

        z = z_ref[...]
        t_g, ga = _gelu_parts(z[:, :2 * D_A])
        u, vr = ga[:, :D_A], ga[:, D_A:]
        dv0 = vr - jnp.mean(vr, axis=-1, keepdims=True)
        rv = lax.rsqrt(jnp.mean(dv0 * dv0, axis=-1, keepdims=True) + EPS)
        vhat = dv0 * rv
        vb = (vhat * lng_ref[...] + lnb_ref[...]).astype(BF16)
        mask = _tril_mask()
        wc = [(wsp_ref[h] * mask).astype(BF16) for h in range(N_HEADS)]

        dya = dcat[:, :D_A]
        for h in range(N_HEADS):
            cols = slice(h * GROUP, (h + 1) * GROUP)
            bs_sum = jnp.zeros((CHUNK, GROUP), F32)
            ws_sum = jnp.zeros((CHUNK, CHUNK), F32)
            for ch in range(tt // CHUNK):
                rows = slice(ch * CHUNK, (ch + 1) * CHUNK)
                v_ch = vb[rows, cols]
                mixed = _dot(wc[h], v_ch) + bs_ref[:, cols]
                dy_ch = dya[rows, cols]
                dz_scr[rows, cols] = dy_ch * mixed
                dmixed = dy_ch * u[rows, cols]
                dmb = dmixed.astype(BF16)
                dz_scr[rows, D_A + h * GROUP:D_A + (h + 1) * GROUP] = _dot_tn(wc[h], dmb)
                bs_sum = bs_sum + dmixed
                ws_sum = ws_sum + _dot_nt(dmb, v_ch)
            _acc_rows(bs_acc, 0, h, bs_sum)
            _acc_rows(small_ref, ROW_WS, h, ws_sum)

        dvl = dz_scr[:, D_A:2 * D_A]
        dvhat = dvl * lng_ref[...]
        dvr = rv * (dvhat - jnp.mean(dvhat, axis=-1, keepdims=True)
                    - vhat * jnp.mean(dvhat * vhat, axis=-1, keepdims=True))
        small_ref[ROW_LN:ROW_LN + 1, 0:D_A] += _colsum(dvl * vhat)
        small_ref[ROW_LN:ROW_LN + 1, D_A:D] += _colsum(dvl)
        dga = jnp.concatenate([dz_scr[:, :D_A], dvr], axis=1)
        dza = dga * _gelu_grad(z[:, :2 * D_A], t_g)

        zb = z[:, 2 * D_A:]
        halo_prev = jnp.where(i == 0, 0.0, zprev_ref[...])
        diff = _pool_diff(zb, halo_prev, i * tt)
        dyb = dcat[:, D_A:]
        inv = _inv_counts(i * tt, tt)
        scaled, ddiffs = [], []
        for g in range(len(WINDOWS)):
            cols = slice(g * GROUP, (g + 1) * GROUP)
            db = diff[g].astype(BF16)
            wpg = wp_ref[g].astype(BF16)
            pre = _dot(db, wpg) + bp_ref[:, cols]
            small_ref[ROW_POOL:ROW_POOL + 1, cols] += _colsum(dyb[:, cols] * pre)
            dpre = dyb[:, cols] * ps_ref[:, cols]
            small_ref[ROW_POOL:ROW_POOL + 1, D_B + g * GROUP:D_B + (g + 1) * GROUP] += _colsum(dpre)
            dpb = dpre.astype(BF16)
            _acc_rows(small_ref, ROW_WP, g, _dot_tn(db, dpb))
            ddiff = _dot_nt(dpb, wpg)
            ddiffs.append(ddiff)
            scaled.append(ddiff * inv[g])
        scaled_all = jnp.concatenate(scaled, axis=1)
        ext = jnp.concatenate([scaled_all, carry[...]], axis=0)
        n_ext = tt + HALO
        s2 = ext + pltpu.roll(ext, n_ext - 1, 0)
        t4 = s2[:, GROUP:]
        s4 = t4 + pltpu.roll(t4, n_ext - 2, 0)
        t8 = s4[:, GROUP:]
        s8 = t8 + pltpu.roll(t8, n_ext - 4, 0)
        t16 = s8[:, GROUP:]
        s16 = t16 + pltpu.roll(t16, n_ext - 8, 0)
        back = [s2[:, :GROUP], s4[:, :GROUP], s8[:, :GROUP], s16]
        carry[...] = scaled_all[:HALO, :]
        dzb = jnp.concatenate([back[g][:tt, :] - ddiffs[g] for g in range(len(WINDOWS))], axis=1)

        dzv = jnp.concatenate([dza, dzb], axis=1).astype(BF16)
        dh1 = _dot(dzv, win_ref[...])
        xv = x_ref[...]
        r1 = _rstd(xv)
        xhat = xv * r1
        shift1, scale1 = mod_ref[0:1, :], mod_ref[1:2, :]
        n1 = xhat * n1pre_ref[...]
        h1 = (n1 * (1.0 + scale1) + shift1).astype(BF16)
        acc_in[...] += _dot_tn(dzv, h1)
        dn1 = dh1 * (1.0 + scale1)
        gx_ref[...] = dx1_ref[...] + _rms_bwd(dn1 * n1pre_ref[...], xhat, r1)
        small_ref[ROW_DMOD:ROW_DMOD + 1, :] += _colsum(dh1)
        small_ref[ROW_DMOD + 1:ROW_DMOD + 2, :] += _colsum(dh1 * n1)
        small_ref[ROW_N1PRE:ROW_N1PRE + 1, :] += _colsum(dn1 * xhat)

        @pl.when(s == nt - 1)
        def _():
            gwin_ref[...] = acc_in[...].astype(BF16)
            gwout_ref[...] = acc_out[...].astype(BF16)
            bs = _unfold(bs_acc[...])
            for h in range(N_HEADS):
                small_ref[ROW_BS + h:ROW_BS + h + 1, 0:GROUP] = jnp.sum(
                    bs[:, h * GROUP:(h + 1) * GROUP].T, axis=0, keepdims=True)
            for a in range(n_sums):
                for r in range(3):
                    chip_copy(a, r).wait_recv()
                    chip_copy(a, r).wait_send()

    rev = lambda w: pl.BlockSpec((tt, w), lambda s: (nt - 1 - s, 0))
    zprev = pl.BlockSpec((HALO, D_B), lambda s: (jnp.maximum((nt - 1 - s) * hb - 1, 0), 2))
    hbm = pl.BlockSpec(memory_space=pl.ANY)
    outs = pl.pallas_call(
        body, name="attn_bwd", grid=(nt,),
        out_shape=tuple([jax.ShapeDtypeStruct((t_len, D), F32), jax.ShapeDtypeStruct((D_Z, D), BF16),
                         jax.ShapeDtypeStruct((D, D), BF16), jax.ShapeDtypeStruct((SMALL_ROWS, D), F32)]
                        + [jax.ShapeDtypeStruct(cs.shape, cs.dtype) for cs in chip_sums]),
        in_specs=[rev(D), rev(D), rev(D), rev(D_Z), zprev, rev(D), _full((8, D)), _full((1, D)),
                  _resident((D_Z, D)), _resident((D, D)), _full((N_HEADS, CHUNK, CHUNK)), _full((CHUNK, D_A)),
                  _full((1, D_A)), _full((1, D_A)), _full((len(WINDOWS), GROUP, GROUP)), _full((1, D_B)),
                  _full((1, D_B)), _full((8, D)), _full((8, D))] + [hbm] * n_sums,
        out_specs=tuple([rev(D), _full((D_Z, D)), _full((D, D)), _full((SMALL_ROWS, D))] + [hbm] * n_sums),
        scratch_shapes=[pltpu.VMEM((HALO, D_B), F32), pltpu.VMEM((D_Z, D), F32), pltpu.VMEM((D, D), F32),
                        pltpu.VMEM((tt, 2 * D_A), F32), pltpu.VMEM((CHUNK // 2, D), F32),
                        pltpu.SemaphoreType.DMA((3 * n_sums,)), pltpu.SemaphoreType.DMA((3 * n_sums,))],
        compiler_params=pltpu.CompilerParams(dimension_semantics=("arbitrary",), vmem_limit_bytes=VMEM_LIMIT),
    )(dmix, dx1, x, z, z, cat, mod, n1pre, w_in_t, w_out, w_sp, bs_rows, ln_g, ln_b, w_pool, b_pool, pool_scale,
      red_fwd, red_bwd, *chip_sums)
    return outs[:4], outs[4:]


def _adam(w, g, m, v):
    m2 = ADAM_B1 * m + (1.0 - ADAM_B1) * g
    v2 = ADAM_B2 * v + (1.0 - ADAM_B2) * (g * g)
    m_hat = m2 / (1.0 - ADAM_B1 ** ADAM_STEP)
    v_hat = v2 / (1.0 - ADAM_B2 ** ADAM_STEP)
    delta = -ADAM_LR * (m_hat / (jnp.sqrt(v_hat) + ADAM_EPS) + ADAM_WD * w)
    return delta, m2, v2


def _adamw_shard(name, rb, w, g, m, v):
    rows, cols = w.shape

    def body(w_ref, g_ref, m_ref, v_ref, d_ref, m2_ref, v2_ref):
        d_ref[...], m2_ref[...], v2_ref[...] = _adam(w_ref[...], g_ref[...], m_ref[...], v_ref[...])

    blk = pl.BlockSpec((rb, cols), lambda i: (i, 0))
    shp = jax.ShapeDtypeStruct((rows, cols), F32)
    return pl.pallas_call(
        body, name=name, grid=(rows // rb,), out_shape=(shp, shp, shp),
        in_specs=[blk] * 4, out_specs=(blk, blk, blk),
        compiler_params=pltpu.CompilerParams(dimension_semantics=("arbitrary",)),
    )(w, g, m, v)


def _adamw_chip_sums(name, rb, w, own, arrived, m, v):
    rows, cols = w.shape

    def body(w_ref, own_ref, arr_ref, m_ref, v_ref, g_ref, d_ref, m2_ref, v2_ref):
        g = own_ref[...]
        for r in range(3):
            g = g + arr_ref[r].astype(F32)
        g_ref[...] = g
        d_ref[...], m2_ref[...], v2_ref[...] = _adam(w_ref[...], g, m_ref[...], v_ref[...])

    blk = pl.BlockSpec((rb, cols), lambda i: (i, 0))
    shp = jax.ShapeDtypeStruct((rows, cols), F32)
    return pl.pallas_call(
        body, name=name, grid=(rows // rb,), out_shape=(shp, shp, shp, shp),
        in_specs=[blk, blk, pl.BlockSpec((3, rb, cols), lambda i: (0, i, 0)), blk, blk],
        out_specs=(blk, blk, blk, blk),
        compiler_params=pltpu.CompilerParams(dimension_semantics=("arbitrary",)),
    )(w, own, arrived, m, v)


def _adamw_ada(rb, w, sc, dmod_cols, m, v):
    rows, cols = w.shape

    def body(w_ref, sc_ref, dm_ref, m_ref, v_ref, g_ref, d_ref, m2_ref, v2_ref):
        g = _dot_tn(sc_ref[...].astype(BF16), dm_ref[...].astype(BF16))
        g_ref[...] = g
        d_ref[...], m2_ref[...], v2_ref[...] = _adam(w_ref[...], g, m_ref[...], v_ref[...])

    blk = pl.BlockSpec((rb, cols), lambda i: (i, 0))
    shp = jax.ShapeDtypeStruct((rows, cols), F32)
    return pl.pallas_call(
        body, name="adamw_ada", grid=(rows // rb,), out_shape=(shp, shp, shp, shp),
        in_specs=[blk, pl.BlockSpec((N_DEV, rb), lambda i: (0, i)), _full((N_DEV, cols)), blk, blk],
        out_specs=(blk, blk, blk, blk),
        compiler_params=pltpu.CompilerParams(dimension_semantics=("arbitrary",)),
    )(w, sc, dmod_cols, m, v)


def _unfold(acc_rows):
    return jnp.concatenate([acc_rows[:, :D_A], acc_rows[:, D_A:]], axis=0)


def _adamw_small(total, params):
    n = len(params)
    flat = [a for p in params for a in p]

    def body(*refs):
        s_ref = refs[0]
        p_refs = refs[1:1 + 3 * n]
        loss_ref = refs[1 + 3 * n]
        o_refs = refs[2 + 3 * n:]
        d_b_ada = s_ref[0:6, :]
        for b in range(1, N_DEV):
            d_b_ada = d_b_ada + s_ref[8 * b:8 * b + 6, :]
        tot = s_ref[PACK_SHIFT:PACK_ROWS, :]
        loss_ref[...] = jnp.broadcast_to(tot[ROW_LOSS:ROW_LOSS + 1, 0:GROUP], (8, GROUP))
        mask = _tril_mask()
        ws = _unfold(tot[ROW_WS:ROW_WS + 64, :])
        wp = _unfold(tot[ROW_WP:ROW_WP + 64, :])
        grads = [
            d_b_ada,
            tot[ROW_N1PRE:ROW_N1PRE + 1, :], tot[ROW_N1POST:ROW_N1POST + 1, :],
            tot[ROW_N2PRE:ROW_N2PRE + 1, :], tot[ROW_N2POST:ROW_N2POST + 1, :],
            tot[ROW_LN:ROW_LN + 1, :D_A], tot[ROW_LN:ROW_LN + 1, D_A:],
            tot[ROW_POOL:ROW_POOL + 1, :D_B], tot[ROW_POOL:ROW_POOL + 1, D_B:],
            tot[ROW_BS:ROW_BS + N_HEADS, 0:GROUP],
            jnp.stack([ws[:, h * GROUP:(h + 1) * GROUP] * mask for h in range(N_HEADS)]),
            jnp.stack([wp[:, g * GROUP:(g + 1) * GROUP] for g in range(len(WINDOWS))]),
        ]
        for k in range(n):
            w_ref, m_ref, v_ref = p_refs[3 * k:3 * k + 3]
            g = grads[k]
            o_refs[4 * k][...] = g
            o_refs[4 * k + 1][...], o_refs[4 * k + 2][...], o_refs[4 * k + 3][...] = _adam(
                w_ref[...], g, m_ref[...], v_ref[...])

    vm = pl.BlockSpec(memory_space=pltpu.VMEM)
    out_shape = [jax.ShapeDtypeStruct((8, GROUP), F32)]
    for w, _, _ in params:
        out_shape += [jax.ShapeDtypeStruct(w.shape, F32)] * 4
    return pl.pallas_call(
        body, name="adamw_small", out_shape=tuple(out_shape),
        in_specs=[vm] * (1 + 3 * n), out_specs=tuple([vm] * len(out_shape)),
    )(total, *flat)


TT_ATTN_FWD = 512
TT_MLP = 256
TT_WGRAD = 512
TT_ATTN_BWD = 256


def kernel(x, c, w_ada, b_ada, norm1_pre, norm1_post, w_in, w_spatial, b_spatial, ln_v_gain, ln_v_bias, w_pool, b_pool, pool_scale, w_out, norm2_pre, norm2_post, w_fc1, w_fc2, loss_target, m_w_ada, m_b_ada, m_norm1_pre, m_norm1_post, m_w_in, m_w_spatial, m_b_spatial, m_ln_v_gain, m_ln_v_bias, m_w_pool, m_b_pool, m_pool_scale, m_w_out, m_norm2_pre, m_norm2_post, m_w_fc1, m_w_fc2, v_w_ada, v_b_ada, v_norm1_pre, v_norm1_post, v_w_in, v_w_spatial, v_b_spatial, v_ln_v_gain, v_ln_v_bias, v_w_pool, v_b_pool, v_pool_scale, v_w_out, v_norm2_pre, v_norm2_post, v_w_fc1, v_w_fc2):
    t_len = x.shape[1]
    me = 4 * lax.axis_index("x") + 2 * lax.axis_index("y") + lax.axis_index("c")
    ada_cols = w_ada.shape[1]
    tt = lambda want: min(want, t_len)

    x2 = x.reshape(t_len, D)
    tgt = loss_target.reshape(t_len, D)
    row = lambda a: a.reshape(1, -1)

    b_my = lax.dynamic_slice_in_dim(b_ada, me * ada_cols, ada_cols).reshape(1, ada_cols)
    modp, sc, (g_in, g_out), fc_shards = _fwd_comm(jnp.broadcast_to(c, (8, D)), w_ada, b_my,
                                                   [w_in.T, w_out], [w_fc1, w_fc2])
    mod = jnp.concatenate([modp.reshape(6, D), jnp.zeros((2, D), F32)], axis=0)
    w_in_t = g_in.reshape(D_Z, D)
    w_out_all = g_out.reshape(D, D)

    bs_rows = jnp.repeat(b_spatial.T, GROUP, axis=1)
    attn_consts = (w_spatial, bs_rows, row(ln_v_gain), row(ln_v_bias), w_pool, row(b_pool), row(pool_scale))

    (z, cat, mix, x1), (w1_early, w2_early) = _attn_fwd(
        tt(TT_ATTN_FWD), x2, mod, row(norm1_pre), row(norm1_post), w_in_t, w_out_all, *attn_consts, fc_shards)
    (a_early, h2, f_early), (w1_late, w2_late) = _mlp_fwd_early(
        tt(TT_MLP), x1, mod, row(norm2_pre), w1_early, w2_early, fc_shards)
    a_late, df, dy, red_fwd = _mlp_fwd_late(tt(TT_MLP), x1, h2, f_early, tgt, mod, row(norm2_post), w1_late, w2_late)
    da, r, dmix, dx1, red_bwd = _mlp_bwd(tt(TT_MLP), df, a_early, a_late, dy, x1, mix, mod, row(norm2_pre),
                                         row(norm1_post), w1_early, w2_early, w1_late, w2_late)
    own_w1, own_w2, sums_w1, sums_w2 = _mlp_wgrad(tt(TT_WGRAD), r, da, df, h2)
    (grad_x, p_in, p_out, small), (arr_w1, arr_w2) = _attn_bwd(
        tt(TT_ATTN_BWD), dmix, dx1, x2, z, cat, mod, row(norm1_pre), w_in_t, w_out_all, *attn_consts,
        red_fwd, red_bwd, [sums_w1, sums_w2])
    grad_in_t, grad_out, total = _tail_comm(
        [p_in.reshape(N_DEV, D_Z // N_DEV, D), p_out.reshape(N_DEV, D // N_DEV, D)], small, 64)

    grad_w1, d_w1, m_w1, v_w1 = _adamw_chip_sums("adamw_fc1", 256, w_fc1, own_w1, arr_w1, m_w_fc1, v_w_fc1)
    grad_w2, d_w2, m_w2, v_w2 = _adamw_chip_sums("adamw_fc2", 128, w_fc2, own_w2, arr_w2, m_w_fc2, v_w_fc2)
    d_out, m_out, v_out = _adamw_shard("adamw_out", 128, w_out, grad_out, m_w_out, v_w_out)
    d_in_t, m_in_t, v_in_t = _adamw_shard("adamw_in", D_Z // N_DEV, w_in.T, grad_in_t, m_w_in.T, v_w_in.T)
    dmod_all = total[0:TABLE_ROWS, :].reshape(N_DEV, 8, D)[:, :6, :].reshape(N_DEV, 6 * D)
    dmod_cols = lax.dynamic_slice_in_dim(dmod_all, me * ada_cols, ada_cols, axis=1)
    grad_ada, d_ada, m_ada, v_ada = _adamw_ada(256, w_ada, sc, dmod_cols, m_w_ada, v_w_ada)

    six = lambda a: a.reshape(6, D)
    small_params = [
        (six(b_ada), six(m_b_ada), six(v_b_ada)),
        (row(norm1_pre), row(m_norm1_pre), row(v_norm1_pre)),
        (row(norm1_post), row(m_norm1_post), row(v_norm1_post)),
        (row(norm2_pre), row(m_norm2_pre), row(v_norm2_pre)),
        (row(norm2_post), row(m_norm2_post), row(v_norm2_post)),
        (row(ln_v_gain), row(m_ln_v_gain), row(v_ln_v_gain)),
        (row(ln_v_bias), row(m_ln_v_bias), row(v_ln_v_bias)),
        (row(pool_scale), row(m_pool_scale), row(v_pool_scale)),
        (row(b_pool), row(m_b_pool), row(v_b_pool)),
        (b_spatial, m_b_spatial, v_b_spatial),
        (w_spatial, m_w_spatial, v_w_spatial),
        (w_pool, m_w_pool, v_w_pool),
    ]
    outs = _adamw_small(total, small_params)
    loss = outs[0][0, 0]
    names = ["b_ada", "norm1_pre", "norm1_post", "norm2_pre", "norm2_post", "ln_v_gain", "ln_v_bias", "pool_scale",
             "b_pool", "b_spatial", "w_spatial", "w_pool"]
    shapes = dict(b_ada=b_ada.shape, norm1_pre=norm1_pre.shape, norm1_post=norm1_post.shape,
                  norm2_pre=norm2_pre.shape, norm2_post=norm2_post.shape, ln_v_gain=ln_v_gain.shape,
                  ln_v_bias=ln_v_bias.shape, pool_scale=pool_scale.shape, b_pool=b_pool.shape,
                  b_spatial=b_spatial.shape, w_spatial=w_spatial.shape, w_pool=w_pool.shape)
    res = {}
    for k, nm in enumerate(names):
        res[nm] = tuple(o.reshape(shapes[nm]) for o in outs[1 + 4 * k:5 + 4 * k])
    res["w_ada"] = (grad_ada, d_ada, m_ada, v_ada)
    res["w_in"] = (grad_in_t.T, d_in_t.T, m_in_t.T, v_in_t.T)
    res["w_out"] = (grad_out, d_out, m_out, v_out)
    res["w_fc1"] = (grad_w1, d_w1, m_w1, v_w1)
    res["w_fc2"] = (grad_w2, d_w2, m_w2, v_w2)

    order = ["w_ada", "b_ada", "norm1_pre", "norm1_post", "w_in", "w_spatial", "b_spatial", "ln_v_gain", "ln_v_bias",
             "w_pool", "b_pool", "pool_scale", "w_out", "norm2_pre", "norm2_post", "w_fc1", "w_fc2"]
    return (loss, grad_x.reshape(x.shape),
            *[res[nm][0] for nm in order], *[res[nm][1] for nm in order],
            *[res[nm][2] for nm in order], *[res[nm][3] for nm in order])
```

```python
import functools

import jax
import jax.numpy as jnp
from jax import lax
from jax.experimental import pallas as pl
from jax.experimental.pallas import tpu as pltpu

F32 = jnp.float32
BF16 = jnp.bfloat16
MESH = pl.DeviceIdType.MESH

N_DEV = 8
D = 1024
D_A = 512
D_B = 512
D_Z = 2 * D_A + D_B
N_HEADS = 4
CHUNK = 128
WINDOWS = (2, 4, 8, 16)
GROUP = 128
D_FF = 4096
FF_BLK = D_FF // N_DEV
HALO = 16
EPS = 1e-6
VMEM_LIMIT = 60 * 1024 * 1024

ADAM_LR = 0.001
ADAM_B1 = 0.9
ADAM_B2 = 0.999
ADAM_EPS = 1e-08
ADAM_WD = 0.01
ADAM_STEP = 10

ROW_DMOD = 0
ROW_N1PRE, ROW_N1POST, ROW_N2PRE, ROW_N2POST = 8, 9, 10, 11
ROW_LN = 12
ROW_POOL = 13
ROW_LOSS = 14
ROW_BS = 16
ROW_WS = 24
ROW_WP = 88
SMALL_ROWS = 152
TABLE_ROWS = 8 * N_DEV
PACK_SHIFT = TABLE_ROWS - 8
PACK_ROWS = SMALL_ROWS + PACK_SHIFT
PACK_HALF = PACK_ROWS // 2


def _dot(a, b):
    return jnp.dot(a, b, preferred_element_type=F32)


def _dot_nt(a, b):
    return lax.dot_general(a, b, (((1,), (1,)), ((), ())), preferred_element_type=F32)


def _dot_tn(a, b):
    return lax.dot_general(a, b, (((0,), (0,)), ((), ())), preferred_element_type=F32)


def _rstd(v):
    return lax.rsqrt(jnp.mean(v * v, axis=-1, keepdims=True) + EPS)


def _rms_bwd(d_hat, hat, rstd):
    return rstd * (d_hat - hat * jnp.mean(d_hat * hat, axis=-1, keepdims=True))


_K0 = 0.7978845608028654
_K1 = 0.044715


def _gelu_parts(v):
    t = jnp.tanh(_K0 * (v + _K1 * (v * v * v)))
    return t, v * (0.5 * (1.0 + t))


def _gelu_grad(v, t):
    return 0.5 * (1.0 + t) + (0.5 * v) * (1.0 - t * t) * (_K0 * (1.0 + (3.0 * _K1) * (v * v)))


def _colsum(v):
    return jnp.sum(v, axis=0, keepdims=True)


def _full(shape):
    n = len(shape)
    return pl.BlockSpec(shape, lambda *_: (0,) * n)


def _resident(shape):
    n = len(shape)
    return pl.BlockSpec(shape, lambda *_: (0,) * n, pipeline_mode=pl.Buffered(1))


def _place():
    x, y, c = lax.axis_index("x"), lax.axis_index("y"), lax.axis_index("c")
    return x, y, c


def _flip(v, bit):
    return 1 - v if bit else v


def _peer(x, y, c, k):
    return (_flip(x, (k >> 2) & 1), _flip(y, (k >> 1) & 1), _flip(c, k & 1))


def _index(p):
    return 4 * p[0] + 2 * p[1] + p[2]


def _two_level_gather_begin(x, y, c, out_refs, send_sems, recv_sems):
    me = (x, y, c)
    sibling = (x, y, 1 - c)
    chips = [(1 - x, y), (x, 1 - y), (1 - x, 1 - y)]

    def copy(a, k, block, to):
        ref = out_refs[a].at[_index(block)]
        return pltpu.make_async_remote_copy(
            src_ref=ref, dst_ref=ref, send_sem=send_sems.at[7 * a + k], recv_sem=recv_sems.at[7 * a + k],
            device_id=to, device_id_type=MESH)

    first = []
    for a in range(len(out_refs)):
        first.append(copy(a, 0, me, sibling))
        first += [copy(a, 1 + j, me, (*chip, c)) for j, chip in enumerate(chips)]
    for cp in first:
        cp.start()
    return copy, first, me, sibling, chips


def _two_level_gather_finish(c, n, begun):
    copy, first, me, sibling, chips = begun
    passed = []
    for a in range(n):
        for j, chip in enumerate(chips):
            copy(a, 1 + j, (*chip, c), me).wait_recv()
            fwd = copy(a, 4 + j, (*chip, c), sibling)
            fwd.start()
            passed.append(fwd)
    for a in range(n):
        copy(a, 0, sibling, me).wait_recv()
        for j, chip in enumerate(chips):
            copy(a, 4 + j, (*chip, 1 - c), me).wait_recv()
    for cp in first + passed:
        cp.wait_send()


def _fwd_comm(c8, w_ada, b_my, gathered, kept):
    ncol = w_ada.shape[1]
    n_g, n_k = len(gathered), len(kept)

    def body(c_ref, w_ref, b_ref, *rest):
        g_in, k_in = rest[:n_g], rest[n_g:n_g + n_k]
        modp_ref, sc_ref = rest[n_g + n_k:n_g + n_k + 2]
        g_out = rest[n_g + n_k + 2:2 * n_g + n_k + 2]
        k_out = rest[2 * n_g + n_k + 2:2 * n_g + 2 * n_k + 2]
        cg, mg, part, send_sems, recv_sems, g_send, g_recv = rest[2 * n_g + 2 * n_k + 2:]
        x, y, c = _place()
        me = _index((x, y, c))
        for a in range(n_g):
            g_out[a][me] = g_in[a][...].astype(BF16)
        begun = _two_level_gather_begin(x, y, c, g_out, g_send, g_recv)
        for a in range(n_k):
            k_out[a][...] = k_in[a][...].astype(BF16)

        def c_copy(k):
            p = _peer(x, y, c, k)
            return pltpu.make_async_remote_copy(
                src_ref=c_ref, dst_ref=cg.at[me], send_sem=send_sems.at[k - 1], recv_sem=recv_sems.at[k - 1],
                device_id=p, device_id_type=MESH)

        def c_arrival(k):
            p = _peer(x, y, c, k)
            return pltpu.make_async_remote_copy(
                src_ref=c_ref, dst_ref=cg.at[_index(p)], send_sem=send_sems.at[k - 1], recv_sem=recv_sems.at[k - 1],
                device_id=p, device_id_type=MESH)

        def m_copy(k):
            p = _peer(x, y, c, k)
            return pltpu.make_async_remote_copy(
                src_ref=part, dst_ref=mg.at[me], send_sem=send_sems.at[6 + k], recv_sem=recv_sems.at[6 + k],
                device_id=p, device_id_type=MESH)

        def m_arrival(k):
            p = _peer(x, y, c, k)
            return pltpu.make_async_remote_copy(
                src_ref=part, dst_ref=mg.at[_index(p)], send_sem=send_sems.at[6 + k], recv_sem=recv_sems.at[6 + k],
                device_id=p, device_id_type=MESH)

        for k in range(1, N_DEV):
            c_copy(k).start()
        cg[me] = c_ref[...]
        for k in range(1, N_DEV):
            c_arrival(k).wait_recv()
        c_all = jnp.concatenate([cg[j, 0:1, :] for j in range(N_DEV)], axis=0)
        sc = c_all * jax.nn.sigmoid(c_all)
        sc_ref[...] = sc
        part[...] = _dot(sc.astype(BF16), w_ref[...].astype(BF16)) + b_ref[...]
        for k in range(1, N_DEV):
            m_copy(k).start()
        mg[me] = part[...]
        for k in range(1, N_DEV):
            m_arrival(k).wait_recv()
        for j in range(N_DEV):
            modp_ref[j:j + 1, :] = mg[j, pl.ds(me, 1), :]
        _two_level_gather_finish(c, n_g, begun)
        for k in range(1, N_DEV):
            c_copy(k).wait_send()
            m_copy(k).wait_send()

    vm = pl.BlockSpec(memory_space=pltpu.VMEM)
    outs = pl.pallas_call(
        body, name="fwd_comm",
        out_shape=tuple([jax.ShapeDtypeStruct((N_DEV, ncol), F32), jax.ShapeDtypeStruct((N_DEV, D), F32)]
                        + [jax.ShapeDtypeStruct((N_DEV,) + s.shape, BF16) for s in gathered]
                        + [jax.ShapeDtypeStruct(s.shape, BF16) for s in kept]),
        in_specs=[vm] * (3 + n_g + n_k), out_specs=tuple([vm] * (2 + n_g + n_k)),
        scratch_shapes=[
            pltpu.VMEM((N_DEV, 8, D), F32),
            pltpu.VMEM((N_DEV, N_DEV, ncol), F32),
            pltpu.VMEM((N_DEV, ncol), F32),
            pltpu.SemaphoreType.DMA((2 * (N_DEV - 1),)),
            pltpu.SemaphoreType.DMA((2 * (N_DEV - 1),)),
            pltpu.SemaphoreType.DMA((7 * n_g,)),
            pltpu.SemaphoreType.DMA((7 * n_g,)),
        ],
        compiler_params=pltpu.CompilerParams(vmem_limit_bytes=VMEM_LIMIT),
    )(c8, w_ada, b_my, *gathered, *kept)
    return outs[0], outs[1], outs[2:2 + n_g], outs[2 + n_g:]


FC_EARLY = 6


class _HostedGather:
    def __init__(self, shard_refs, slot_refs, local, direct, forward, send_sems, recv_sems, local_sems):
        self.x, self.y, self.c = _place()
        self.shards, self.slots = shard_refs, slot_refs
        self.local, self.direct, self.forward = local, direct, forward
        self.send_sems, self.recv_sems, self.local_sems = send_sems, recv_sems, local_sems
        self.n_copies = len(direct) + len(forward)

    def _copy(self, a, i):
        sem = a * self.n_copies + i
        if i < len(self.direct):
            rel, slot = self.direct[i]
            src, dst, to = self.shards[a], self.slots[a].at[slot], _peer(self.x, self.y, self.c, rel)
        else:
            slot, sib_slot = self.forward[i - len(self.direct)]
            src, dst, to = self.slots[a].at[slot], self.slots[a].at[sib_slot], (self.x, self.y, 1 - self.c)
        return pltpu.make_async_remote_copy(
            src_ref=src, dst_ref=dst, send_sem=self.send_sems.at[sem], recv_sem=self.recv_sems.at[sem],
            device_id=to, device_id_type=MESH)

    def _local(self, a):
        return pltpu.make_async_copy(self.shards[a], self.slots[a].at[self.local], self.local_sems.at[a])

    def begin(self):
        for a in range(len(self.shards)):
            if self.local is not None:
                self._local(a).start()
            for i in range(len(self.direct)):
                self._copy(a, i).start()

    def pass_on(self):
        for a in range(len(self.shards)):
            for f, (slot, _) in enumerate(self.forward):
                i = [s for _, s in self.direct].index(slot)
                self._copy(a, i).wait_recv()
                self._copy(a, len(self.direct) + f).start()

    def finish(self):
        passed = [slot for slot, _ in self.forward]
        for a in range(len(self.shards)):
            if self.local is not None:
                self._local(a).wait()
            for i in range(self.n_copies):
                if i >= len(self.direct) or self.direct[i][1] not in passed:
                    self._copy(a, i).wait_recv()
                self._copy(a, i).wait_send()


def _tail_comm(parts, small, row_chunk):
    n = len(parts)

    def body(*refs):
        p_refs, small_ref = refs[:n], refs[n]
        g_refs, total_ref = refs[n + 1:2 * n + 1], refs[2 * n + 1]
        scr = refs[2 * n + 2:]
        from_sib = scr[0:n]
        chip_out = scr[n:2 * n]
        chip_in = scr[2 * n:3 * n]
        pack, pack_sib, halves = scr[3 * n:3 * n + 3]
        send_a, recv_a, send_b, recv_b, send_s, recv_s = scr[3 * n + 3:]
        x, y, c = _place()
        me = _index((x, y, c))
        sibling = (x, y, 1 - c)
        my_chip = 2 * x + y
        others = [(1 - x, y), (x, 1 - y), (1 - x, 1 - y)]
        my_half = pl.ds(pl.multiple_of(PACK_HALF * c, 8), PACK_HALF)

        def pack_to_sibling():
            return pltpu.make_async_remote_copy(
                src_ref=pack, dst_ref=pack_sib, send_sem=send_s.at[0], recv_sem=recv_s.at[0],
                device_id=sibling, device_id_type=MESH)

        def half_to_chip(r):
            return pltpu.make_async_remote_copy(
                src_ref=halves.at[my_chip], dst_ref=halves.at[my_chip],
                send_sem=send_s.at[1 + r], recv_sem=recv_s.at[1 + r],
                device_id=(*others[r], c), device_id_type=MESH)

        def half_from_chip(r):
            k = 2 * others[r][0] + others[r][1]
            return pltpu.make_async_remote_copy(
                src_ref=halves.at[k], dst_ref=halves.at[k], send_sem=send_s.at[1 + r], recv_sem=recv_s.at[1 + r],
                device_id=(*others[r], c), device_id_type=MESH)

        def total_to_sibling():
            return pltpu.make_async_remote_copy(
                src_ref=total_ref.at[my_half], dst_ref=total_ref.at[my_half],
                send_sem=send_s.at[4], recv_sem=recv_s.at[4], device_id=sibling, device_id_type=MESH)

        def total_from_sibling():
            sib_half = pl.ds(pl.multiple_of(PACK_HALF * (1 - c), 8), PACK_HALF)
            return pltpu.make_async_remote_copy(
                src_ref=total_ref.at[sib_half], dst_ref=total_ref.at[sib_half],
                send_sem=send_s.at[4], recv_sem=recv_s.at[4], device_id=sibling, device_id_type=MESH)

        pack[0:TABLE_ROWS, :] = jnp.zeros((TABLE_ROWS, D), F32)
        pack[pl.ds(pl.multiple_of(8 * me, 8), 8), :] = small_ref[0:8, :]
        pack[TABLE_ROWS:PACK_ROWS, :] = small_ref[8:SMALL_ROWS, :]
        pack_to_sibling().start()

        def to_sibling(a, k):
            return pltpu.make_async_remote_copy(
                src_ref=p_refs[a].at[2 * k + (1 - c)], dst_ref=from_sib[a].at[k],
                send_sem=send_a.at[a], recv_sem=recv_a.at[a], device_id=sibling, device_id_type=MESH)

        def all_from_sibling(a):
            return pltpu.make_async_remote_copy(
                src_ref=from_sib[a], dst_ref=from_sib[a], send_sem=send_a.at[a], recv_sem=recv_a.at[a],
                device_id=sibling, device_id_type=MESH)

        def to_chip(a, r):
            return pltpu.make_async_remote_copy(
                src_ref=chip_out[a].at[r], dst_ref=chip_in[a].at[r],
                send_sem=send_b.at[3 * a + r], recv_sem=recv_b.at[3 * a + r],
                device_id=(*others[r], c), device_id_type=MESH)

        for a in range(n):
            for k in range(4):
                to_sibling(a, k).start()
        pack_to_sibling().wait_recv()
        halves[my_chip] = pack[my_half, :] + pack_sib[my_half, :]
        for r in range(3):
            half_to_chip(r).start()
        for a in range(n):
            all_from_sibling(a).wait_recv()
            rows = p_refs[a].shape[1]
            for r in range(3):
                k = 2 * others[r][0] + others[r][1]
                for s in range(0, rows, row_chunk):
                    sl = pl.ds(s, row_chunk)
                    chip_out[a][r, sl, :] = (p_refs[a][2 * k + c, sl, :].astype(F32)
                                             + from_sib[a][k, sl, :].astype(F32)).astype(BF16)
                to_chip(a, r).start()
            for s in range(0, rows, row_chunk):
                sl = pl.ds(s, row_chunk)
                g_refs[a][sl, :] = (p_refs[a][2 * my_chip + c, sl, :].astype(F32)
                                    + from_sib[a][my_chip, sl, :].astype(F32))
        for r in range(3):
            half_from_chip(r).wait_recv()
        total_ref[my_half, :] = ((halves[0] + halves[1]) + halves[2]) + halves[3]
        total_to_sibling().start()
        for a in range(n):
            rows = p_refs[a].shape[1]
            for r in range(3):
                to_chip(a, r).wait_recv()
                for s in range(0, rows, row_chunk):
                    sl = pl.ds(s, row_chunk)
                    g_refs[a][sl, :] = g_refs[a][sl, :] + chip_in[a][r, sl, :].astype(F32)
        total_from_sibling().wait_recv()
        for a in range(n):
            all_from_sibling(a).wait_send()
            for r in range(3):
                to_chip(a, r).wait_send()
        pack_to_sibling().wait_send()
        for r in range(3):
            half_to_chip(r).wait_send()
        total_to_sibling().wait_send()

    vm = pl.BlockSpec(memory_space=pltpu.VMEM)
    return pl.pallas_call(
        body, name="tail_comm",
        out_shape=tuple([jax.ShapeDtypeStruct(p.shape[1:], F32) for p in parts]
                        + [jax.ShapeDtypeStruct((PACK_ROWS, D), F32)]),
        in_specs=[vm] * (n + 1), out_specs=tuple([vm] * (n + 1)),
        scratch_shapes=(
            [pltpu.VMEM((4,) + p.shape[1:], BF16) for p in parts]
            + [pltpu.VMEM((3,) + p.shape[1:], BF16) for p in parts]
            + [pltpu.VMEM((3,) + p.shape[1:], BF16) for p in parts]
            + [pltpu.VMEM((PACK_ROWS, D), F32), pltpu.VMEM((PACK_ROWS, D), F32),
               pltpu.VMEM((4, PACK_HALF, D), F32)]
            + [pltpu.SemaphoreType.DMA((n,)), pltpu.SemaphoreType.DMA((n,)),
               pltpu.SemaphoreType.DMA((3 * n,)), pltpu.SemaphoreType.DMA((3 * n,)),
               pltpu.SemaphoreType.DMA((5,)), pltpu.SemaphoreType.DMA((5,))]),
        compiler_params=pltpu.CompilerParams(vmem_limit_bytes=VMEM_LIMIT),
    )(*parts, small)


def _tril_mask():
    row = lax.broadcasted_iota(jnp.int32, (CHUNK, CHUNK), 0)
    col = lax.broadcasted_iota(jnp.int32, (CHUNK, CHUNK), 1)
    return (col <= row).astype(F32)


def _window_sums(ext):
    s2 = ext + pltpu.roll(ext, 1, 0)
    t4 = s2[:, GROUP:]
    s4 = t4 + pltpu.roll(t4, 2, 0)
    t8 = s4[:, GROUP:]
    s8 = t8 + pltpu.roll(t8, 4, 0)
    t16 = s8[:, GROUP:]
    s16 = t16 + pltpu.roll(t16, 8, 0)
    return [s2[:, :GROUP], s4[:, :GROUP], s8[:, :GROUP], s16]


def _inv_counts(first_pos, rows):
    pos = first_pos + lax.broadcasted_iota(jnp.int32, (rows, 1), 0)
    return [1.0 / jnp.minimum(pos + 1, w).astype(F32) for w in WINDOWS]


def _pool_diff(zb, halo, first_pos):
    tt = zb.shape[0]
    sums = _window_sums(jnp.concatenate([halo, zb], axis=0))
    inv = _inv_counts(first_pos, tt)
    return [sums[g][HALO:, :] * inv[g] - zb[:, g * GROUP:(g + 1) * GROUP] for g in range(len(WINDOWS))]


def _attn_fwd(tt, x, mod, n1pre, n1post, w_in_t, w_out, w_sp, bs_rows, ln_g, ln_b, w_pool, b_pool, pool_scale,
              fc_shards):
    t_len = x.shape[0]
    nt = t_len // tt
    n_fc = len(fc_shards)

    def body(x_ref, mod_ref, n1pre_ref, n1post_ref, win_ref, wout_ref, wsp_ref, bs_ref, lng_ref, lnb_ref,
             wp_ref, bp_ref, ps_ref, *rest):
        shard_refs = rest[:n_fc]
        z_ref, cat_ref, mix_ref, x1_ref = rest[n_fc:n_fc + 4]
        slot_refs = rest[n_fc + 4:2 * n_fc + 4]
        carry, send_sems, recv_sems, local_sems = rest[2 * n_fc + 4:]
        i = pl.program_id(0)
        gather = _HostedGather(shard_refs, slot_refs, 0, [(1, 1), (2, 2), (4, 4)], [(2, 3), (4, 5)],
                               send_sems, recv_sems, local_sems)

        @pl.when(i == 0)
        def _():
            gather.begin()
            carry[...] = jnp.zeros_like(carry)

        @pl.when(i == nt - 1)
        def _():
            gather.pass_on()

        xv = x_ref[...]
        shift1, scale1, gate1 = mod_ref[0:1, :], mod_ref[1:2, :], mod_ref[2:3, :]
        h1 = (xv * _rstd(xv) * n1pre_ref[...]) * (1.0 + scale1) + shift1
        z = _dot_nt(h1.astype(BF16), win_ref[...])
        z_ref[...] = z

        _, ga = _gelu_parts(z[:, :2 * D_A])
        u, vr = ga[:, :D_A], ga[:, D_A:]
        dv = vr - jnp.mean(vr, axis=-1, keepdims=True)
        v = (dv * lax.rsqrt(jnp.mean(dv * dv, axis=-1, keepdims=True) + EPS)) * lng_ref[...] + lnb_ref[...]
        vb = v.astype(BF16)
        mask = _tril_mask()
        wc = [(wsp_ref[h] * mask).astype(BF16) for h in range(N_HEADS)]
        for ch in range(tt // CHUNK):
            rows = slice(ch * CHUNK, (ch + 1) * CHUNK)
            for h in range(N_HEADS):
                cols = slice(h * GROUP, (h + 1) * GROUP)
                mixed = _dot(wc[h], vb[rows, cols]) + bs_ref[:, cols]
                cat_ref[rows, cols] = (u[rows, cols] * mixed).astype(BF16)

        zb = z[:, 2 * D_A:]
        diff = _pool_diff(zb, carry[...], i * tt)
        carry[...] = zb[tt - HALO:, :]
        for g in range(len(WINDOWS)):
            cols = slice(g * GROUP, (g + 1) * GROUP)
            pre = _dot(diff[g].astype(BF16), wp_ref[g].astype(BF16)) + bp_ref[:, cols]
            cat_ref[:, D_A + g * GROUP:D_A + (g + 1) * GROUP] = (pre * ps_ref[:, cols]).astype(BF16)

        mix = _dot(cat_ref[...], wout_ref[...])
        mix_ref[...] = mix
        x1_ref[...] = xv + gate1 * (mix * _rstd(mix) * n1post_ref[...])

        @pl.when(i == nt - 1)
        def _():
            gather.finish()

    tile = lambda w: pl.BlockSpec((tt, w), lambda i: (i, 0))
    hbm = pl.BlockSpec(memory_space=pl.ANY)
    outs = pl.pallas_call(
        body, name="attn_fwd", grid=(nt,),
        out_shape=tuple([jax.ShapeDtypeStruct((t_len, D_Z), F32), jax.ShapeDtypeStruct((t_len, D), BF16),
                         jax.ShapeDtypeStruct((t_len, D), F32), jax.ShapeDtypeStruct((t_len, D), F32)]
                        + [jax.ShapeDtypeStruct((FC_EARLY,) + s.shape, BF16) for s in fc_shards]),
        in_specs=[tile(D), _full((8, D)), _full((1, D)), _full((1, D)), _resident((D_Z, D)), _resident((D, D)),
                  _full((N_HEADS, CHUNK, CHUNK)), _full((CHUNK, D_A)), _full((1, D_A)), _full((1, D_A)),
                  _full((len(WINDOWS), GROUP, GROUP)), _full((1, D_B)), _full((1, D_B))] + [hbm] * n_fc,
        out_specs=tuple([tile(D_Z), tile(D), tile(D), tile(D)] + [hbm] * n_fc),
        scratch_shapes=[pltpu.VMEM((HALO, D_B), F32), pltpu.SemaphoreType.DMA((5 * n_fc,)),
                        pltpu.SemaphoreType.DMA((5 * n_fc,)), pltpu.SemaphoreType.DMA((n_fc,))],
        compiler_params=pltpu.CompilerParams(dimension_semantics=("arbitrary",), vmem_limit_bytes=VMEM_LIMIT),
    )(x, mod, n1pre, n1post, w_in_t, w_out, w_sp, bs_rows, ln_g, ln_b, w_pool, b_pool, pool_scale, *fc_shards)
    return outs[:4], outs[4:]


def _mlp_fwd_early(tt, x1, mod, n2pre, w1_early, w2_early, fc_shards):
    t_len = x1.shape[0]
    nt = t_len // tt
    n_fc = len(fc_shards)
    n_late = N_DEV - FC_EARLY

    def body(x1_ref, mod_ref, n2pre_ref, w1_ref, w2_ref, *rest):
        shard_refs = rest[:n_fc]
        r_ref, h2_ref, f_ref = rest[n_fc:n_fc + 3]
        slot_refs = rest[n_fc + 3:2 * n_fc + 3]
        send_sems, recv_sems, local_sems = rest[2 * n_fc + 3:]
        i = pl.program_id(0)
        gather = _HostedGather(shard_refs, slot_refs, None, [(6, 0)], [(0, 1)], send_sems, recv_sems, local_sems)

        @pl.when(i == 0)
        def _():
            gather.begin()

        @pl.when(i == (3 * nt) // 4)
        def _():
            gather.pass_on()

        x1v = x1_ref[...]
        shift2, scale2 = mod_ref[3:4, :], mod_ref[4:5, :]
        h2 = ((x1v * _rstd(x1v) * n2pre_ref[...]) * (1.0 + scale2) + shift2).astype(BF16)
        h2_ref[...] = h2
        for j in range(FC_EARLY):
            cols = slice(j * FF_BLK, (j + 1) * FF_BLK)
            ra = jnp.maximum(_dot(h2, w1_ref[j]), 0.0)
            r = (ra * ra).astype(BF16)
            r_ref[:, cols] = r
            contrib = _dot(r, w2_ref[j])
            if j == 0:
                f_ref[...] = contrib
            else:
                f_ref[...] += contrib

        @pl.when(i == nt - 1)
        def _():
            gather.finish()

    tile = lambda w: pl.BlockSpec((tt, w), lambda i: (i, 0))
    hbm = pl.BlockSpec(memory_space=pl.ANY)
    outs = pl.pallas_call(
        body, name="mlp_fwd_early", grid=(nt,),
        out_shape=tuple([jax.ShapeDtypeStruct((t_len, D_FF), BF16),
                         jax.ShapeDtypeStruct((t_len, D), BF16), jax.ShapeDtypeStruct((t_len, D), F32)]
                        + [jax.ShapeDtypeStruct((n_late,) + s.shape, BF16) for s in fc_shards]),
        in_specs=[tile(D), _full((8, D)), _full((1, D)),
                  _resident((FC_EARLY, D, FF_BLK)), _resident((FC_EARLY, FF_BLK, D))] + [hbm] * n_fc,
        out_specs=tuple([tile(FC_EARLY * FF_BLK), tile(D), tile(D)] + [hbm] * n_fc),
        scratch_shapes=[pltpu.SemaphoreType.DMA((2 * n_fc,)), pltpu.SemaphoreType.DMA((2 * n_fc,)),
                        pltpu.SemaphoreType.DMA((n_fc,))],
        compiler_params=pltpu.CompilerParams(dimension_semantics=("arbitrary",), vmem_limit_bytes=VMEM_LIMIT),
    )(x1, mod, n2pre, w1_early, w2_early, *fc_shards)
    return outs[:3], outs[3:]


def _mlp_fwd_late(tt, r_all, x1, h2, f_early, tgt, mod, n2post, w1_late, w2_late):
    t_len = x1.shape[0]
    nt = t_len // tt
    n_late = N_DEV - FC_EARLY

    def body(r_all_ref, x1_ref, h2_ref, fe_ref, tgt_ref, mod_ref, n2post_ref, w1_ref, w2_ref,
             r_ref, df_ref, dy_ref, red_ref):
        i = pl.program_id(0)

        @pl.when(i == 0)
        def _():
            red_ref[...] = jnp.zeros_like(red_ref)

        x1v = x1_ref[...]
        gate2 = mod_ref[5:6, :]
        h2 = h2_ref[...]
        f = fe_ref[...]
        for j in range(n_late):
            cols = slice(j * FF_BLK, (j + 1) * FF_BLK)
            ra = jnp.maximum(_dot(h2, w1_ref[j]), 0.0)
            r = (ra * ra).astype(BF16)
            r_ref[:, cols] = r
            f = f + _dot(r, w2_ref[j])
        rf = _rstd(f)
        fhat = f * rf
        nf = fhat * n2post_ref[...]
        err = (x1v + gate2 * nf) - tgt_ref[...]
        dy = err * (1.0 / D)
        dy_ref[...] = dy
        dnf = dy * gate2
        df_ref[...] = _rms_bwd(dnf * n2post_ref[...], fhat, rf).astype(BF16)
        red_ref[0:1, :] += _colsum(dy * nf)
        red_ref[1:2, :] += _colsum(dnf * fhat)
        red_ref[2:3, :] += _colsum(0.5 * jnp.mean(err * err, axis=-1, keepdims=True)) * jnp.ones((1, D), F32)

    tile = lambda w: pl.BlockSpec((tt, w), lambda i: (i, 0))
    return pl.pallas_call(
        body, name="mlp_fwd_late", grid=(nt,),
        out_shape=(jax.ShapeDtypeStruct((t_len, D_FF), BF16), jax.ShapeDtypeStruct((t_len, D), BF16),
                   jax.ShapeDtypeStruct((t_len, D), F32), jax.ShapeDtypeStruct((8, D), F32)),
        in_specs=[pl.BlockSpec(memory_space=pl.ANY), tile(D), tile(D), tile(D), tile(D), _full((8, D)),
                  _full((1, D)), _resident((n_late, D, FF_BLK)), _resident((n_late, FF_BLK, D))],
        out_specs=(pl.BlockSpec((tt, n_late * FF_BLK), lambda i: (i, FC_EARLY // n_late)), tile(D), tile(D),
                   _full((8, D))),
        input_output_aliases={0: 0},
        compiler_params=pltpu.CompilerParams(dimension_semantics=("arbitrary",), vmem_limit_bytes=VMEM_LIMIT),
    )(r_all, x1, h2, f_early, tgt, mod, n2post, w1_late, w2_late)


def _mlp_bwd(tt, df, r, dy, x1, mix, mod, n2pre, n1post, w1_early, w2_early, w1_late, w2_late):
    t_len = x1.shape[0]
    nt = t_len // tt
    n_late = N_DEV - FC_EARLY

    def body(df_ref, r_ref, dy_ref, x1_ref, mix_ref, mod_ref, n2pre_ref, n1post_ref,
             w1e_ref, w2e_ref, w1l_ref, w2l_ref, da_ref, dmix_ref, dx1_ref, red_ref, dh2_acc):
        i = pl.program_id(0)

        @pl.when(i == 0)
        def _():
            red_ref[...] = jnp.zeros_like(red_ref)

        dfv = df_ref[...]
        for j in range(N_DEV):
            cols = slice(j * FF_BLK, (j + 1) * FF_BLK)
            if j < FC_EARLY:
                w1, w2 = w1e_ref[j], w2e_ref[j]
            else:
                w1, w2 = w1l_ref[j - FC_EARLY], w2l_ref[j - FC_EARLY]
            dr = _dot_nt(dfv, w2)
            da = (dr * (2.0 * jnp.sqrt(r_ref[:, cols].astype(F32)))).astype(BF16)
            da_ref[:, cols] = da
            contrib = _dot_nt(da, w1)
            if j == 0:
                dh2_acc[...] = contrib
            else:
                dh2_acc[...] += contrib
        dh2 = dh2_acc[...]
        gate1, scale2 = mod_ref[2:3, :], mod_ref[4:5, :]
        x1v = x1_ref[...]
        r2 = _rstd(x1v)
        xhat = x1v * r2
        n2 = xhat * n2pre_ref[...]
        dn2 = dh2 * (1.0 + scale2)
        dx1 = dy_ref[...] + _rms_bwd(dn2 * n2pre_ref[...], xhat, r2)
        dx1_ref[...] = dx1
        mixv = mix_ref[...]
        rm = _rstd(mixv)
        mhat = mixv * rm
        dnm = dx1 * gate1
        dmix_ref[...] = _rms_bwd(dnm * n1post_ref[...], mhat, rm).astype(BF16)
        red_ref[0:1, :] += _colsum(dh2)
        red_ref[1:2, :] += _colsum(dh2 * n2)
        red_ref[2:3, :] += _colsum(dn2 * xhat)
        red_ref[3:4, :] += _colsum(dx1 * (mhat * n1post_ref[...]))
        red_ref[4:5, :] += _colsum(dnm * mhat)

    tile = lambda w: pl.BlockSpec((tt, w), lambda i: (i, 0))
    return pl.pallas_call(
        body, name="mlp_bwd", grid=(nt,),
        out_shape=(jax.ShapeDtypeStruct((t_len, D_FF), BF16),
                   jax.ShapeDtypeStruct((t_len, D), BF16), jax.ShapeDtypeStruct((t_len, D), F32),
                   jax.ShapeDtypeStruct((8, D), F32)),
        in_specs=[tile(D), tile(D_FF), tile(D), tile(D), tile(D),
                  _full((8, D)), _full((1, D)), _full((1, D)),
                  _resident((FC_EARLY, D, FF_BLK)), _resident((FC_EARLY, FF_BLK, D)),
                  _resident((n_late, D, FF_BLK)), _resident((n_late, FF_BLK, D))],
        out_specs=(tile(D_FF), tile(D), tile(D), _full((8, D))),
        scratch_shapes=[pltpu.VMEM((tt, D), F32)],
        compiler_params=pltpu.CompilerParams(dimension_semantics=("arbitrary",), vmem_limit_bytes=VMEM_LIMIT),
    )(df, r, dy, x1, mix, mod, n2pre, n1post, w1_early, w2_early, w1_late, w2_late)


def _mlp_wgrad(tt, r, da, df, h2):
    t_len = df.shape[0]
    nt = t_len // tt

    def relation(j):
        return jnp.where(j < 4, 2 * j + 1, (2 * j - 6) % N_DEV)

    def body(r_ref, da_ref, df_ref, h2_ref, own1_ref, own2_ref, out1_ref, out2_ref,
             acc1, acc2, snd1, snd2, sib1, sib2, send_sems, recv_sems):
        j, t = pl.program_id(0), pl.program_id(1)
        rows = pl.ds(pl.multiple_of(t * tt, tt), tt)
        x, y, c = _place()
        accs, snds, sibs = (acc1, acc2), (snd1, snd2), (sib1, sib2)

        def to_sibling(a, jj):
            return pltpu.make_async_remote_copy(
                src_ref=snds[a].at[jj % 2], dst_ref=sibs[a].at[jj],
                send_sem=send_sems.at[4 * a + jj], recv_sem=recv_sems.at[4 * a + jj],
                device_id=(x, y, 1 - c), device_id_type=MESH)

        @pl.when(t == 0)
        def _():
            acc2[...] = jnp.zeros_like(acc2)
            acc1[...] = jnp.zeros_like(acc1)

        acc2[...] += _dot_tn(r_ref[...], df_ref[rows, :])
        acc1[...] += _dot_tn(h2_ref[rows, :], da_ref[...])

        @pl.when((t == nt - 1) & (j < 4))
        def _():
            for a in range(2):
                @pl.when(j >= 2)
                def _():
                    to_sibling(a, j - 2).wait_send()

                snds[a][j % 2] = accs[a][...].astype(BF16)
                to_sibling(a, j).start()

        @pl.when((t == nt - 1) & (j >= 4))
        def _():
            jj = (j - 3) % 4
            for a, (own_ref, out_ref) in enumerate(((own1_ref, out1_ref), (own2_ref, out2_ref))):
                to_sibling(a, jj).wait_recv()

                @pl.when(j < N_DEV - 1)
                def _():
                    out_ref[0] = (accs[a][...] + sibs[a][jj].astype(F32)).astype(BF16)

                @pl.when(j == N_DEV - 1)
                def _():
                    own_ref[...] = accs[a][...] + sibs[a][jj].astype(F32)
                    to_sibling(a, 2).wait_send()
                    to_sibling(a, 3).wait_send()

    blk = pl.BlockSpec((tt, FF_BLK), lambda j, t: (t, relation(j)))
    chip = lambda j, t: (jnp.clip(j - 4, 0, 2), 0, 0)
    return pl.pallas_call(
        body, name="mlp_wgrad", grid=(N_DEV, nt),
        out_shape=(jax.ShapeDtypeStruct((D, FF_BLK), F32), jax.ShapeDtypeStruct((FF_BLK, D), F32),
                   jax.ShapeDtypeStruct((3, D, FF_BLK), BF16), jax.ShapeDtypeStruct((3, FF_BLK, D), BF16)),
        in_specs=[blk, blk, _resident((t_len, D)), _resident((t_len, D))],
        out_specs=(_full((D, FF_BLK)), _full((FF_BLK, D)),
                   pl.BlockSpec((1, D, FF_BLK), chip), pl.BlockSpec((1, FF_BLK, D), chip)),
        scratch_shapes=[pltpu.VMEM((D, FF_BLK), F32), pltpu.VMEM((FF_BLK, D), F32),
                        pltpu.VMEM((2, D, FF_BLK), BF16), pltpu.VMEM((2, FF_BLK, D), BF16),
                        pltpu.VMEM((4, D, FF_BLK), BF16), pltpu.VMEM((4, FF_BLK, D), BF16),
                        pltpu.SemaphoreType.DMA((8,)), pltpu.SemaphoreType.DMA((8,))],
        compiler_params=pltpu.CompilerParams(dimension_semantics=("arbitrary", "arbitrary"),
                                             vmem_limit_bytes=VMEM_LIMIT),
    )(r, da, df, h2)


def _acc_rows(ref, row0, k, val):
    half = CHUNK // 2
    ref[row0:row0 + half, k * GROUP:(k + 1) * GROUP] += val[:half, :]
    ref[row0:row0 + half, D_A + k * GROUP:D_A + (k + 1) * GROUP] += val[half:, :]


def _attn_bwd(tt, dmix, dx1, x, z, cat, mod, n1pre, w_in_t, w_out, w_sp, bs_rows, ln_g, ln_b, w_pool, b_pool,
              pool_scale, red_fwd, red_bwd, chip_sums):
    t_len = x.shape[0]
    nt = t_len // tt
    hb = tt // HALO
    n_sums = len(chip_sums)

    def body(dmix_ref, dx1_ref, x_ref, z_ref, zprev_ref, cat_ref, mod_ref, n1pre_ref, win_ref, wout_ref, wsp_ref,
             bs_ref, lng_ref, lnb_ref, wp_ref, bp_ref, ps_ref, redf_ref, redb_ref, *rest):
        sum_out = rest[:n_sums]
        gx_ref, gwin_ref, gwout_ref, small_ref = rest[n_sums:n_sums + 4]
        sum_in = rest[n_sums + 4:2 * n_sums + 4]
        carry, acc_in, acc_out, dz_scr, bs_acc, send_sems, recv_sems = rest[2 * n_sums + 4:]
        s = pl.program_id(0)
        i = nt - 1 - s
        px, py, pc = _place()

        def chip_copy(a, r):
            return pltpu.make_async_remote_copy(
                src_ref=sum_out[a].at[r], dst_ref=sum_in[a].at[r],
                send_sem=send_sems.at[3 * a + r], recv_sem=recv_sems.at[3 * a + r],
                device_id=_peer(px, py, pc, 2 * (r + 1)), device_id_type=MESH)

        @pl.when(s == 0)
        def _():
            for a in range(n_sums):
                for r in range(3):
                    chip_copy(a, r).start()
            carry[...] = jnp.zeros_like(carry)
            acc_in[...] = jnp.zeros_like(acc_in)
            acc_out[...] = jnp.zeros_like(acc_out)
            bs_acc[...] = jnp.zeros_like(bs_acc)
            small_ref[...] = jnp.zeros_like(small_ref)
            small_ref[ROW_DMOD + 2:ROW_DMOD + 3, :] = redb_ref[3:4, :]
            small_ref[ROW_DMOD + 3:ROW_DMOD + 5, :] = redb_ref[0:2, :]
            small_ref[ROW_DMOD + 5:ROW_DMOD + 6, :] = redf_ref[0:1, :]
            small_ref[ROW_N1POST:ROW_N1POST + 1, :] = redb_ref[4:5, :]
            small_ref[ROW_N2PRE:ROW_N2PRE + 1, :] = redb_ref[2:3, :]
            small_ref[ROW_N2POST:ROW_N2POST + 1, :] = redf_ref[1:2, :]
            small_ref[ROW_LOSS:ROW_LOSS + 1, :] = redf_ref[2:3, :]

        dmixv = dmix_ref[...]
        dcat = _dot_nt(dmixv, wout_ref[...])
        acc_out[...] += _dot_tn(cat_ref[...], dmixv)

        z = z_ref[...]
        t_g, ga = _gelu_parts(z[:, :2 * D_A])
        u, vr = ga[:, :D_A], ga[:, D_A:]
        dv0 = vr - jnp.mean(vr, axis=-1, keepdims=True)
        rv = lax.rsqrt(jnp.mean(dv0 * dv0, axis=-1, keepdims=True) + EPS)
        vhat = dv0 * rv
        vb = (vhat * lng_ref[...] + lnb_ref[...]).astype(BF16)
        mask = _tril_mask()
        wc = [(wsp_ref[h] * mask).astype(BF16) for h in range(N_HEADS)]

        dya = dcat[:, :D_A]
        for h in range(N_HEADS):
            cols = slice(h * GROUP, (h + 1) * GROUP)
            bs_sum = jnp.zeros((CHUNK, GROUP), F32)
            ws_sum = jnp.zeros((CHUNK, CHUNK), F32)
            for ch in range(tt // CHUNK):
                rows = slice(ch * CHUNK, (ch + 1) * CHUNK)
                v_ch = vb[rows, cols]
                mixed = _dot(wc[h], v_ch) + bs_ref[:, cols]
                dy_ch = dya[rows, cols]
                dz_scr[rows, cols] = dy_ch * mixed
                dmixed = dy_ch * u[rows, cols]
                dmb = dmixed.astype(BF16)
                dz_scr[rows, D_A + h * GROUP:D_A + (h + 1) * GROUP] = _dot_tn(wc[h], dmb)
                bs_sum = bs_sum + dmixed
                ws_sum = ws_sum + _dot_nt(dmb, v_ch)
            _acc_rows(bs_acc, 0, h, bs_sum)
            _acc_rows(small_ref, ROW_WS, h, ws_sum)

        dvl = dz_scr[:, D_A:2 * D_A]
        dvhat = dvl * lng_ref[...]
        dvr = rv * (dvhat - jnp.mean(dvhat, axis=-1, keepdims=True)
                    - vhat * jnp.mean(dvhat * vhat, axis=-1, keepdims=True))
        small_ref[ROW_LN:ROW_LN + 1, 0:D_A] += _colsum(dvl * vhat)
        small_ref[ROW_LN:ROW_LN + 1, D_A:D] += _colsum(dvl)
        dga = jnp.concatenate([dz_scr[:, :D_A], dvr], axis=1)
        dza = dga * _gelu_grad(z[:, :2 * D_A], t_g)

        zb = z[:, 2 * D_A:]
        halo_prev = jnp.where(i == 0, 0.0, zprev_ref[...])
        diff = _pool_diff(zb, halo_prev, i * tt)
        dyb = dcat[:, D_A:]
        inv = _inv_counts(i * tt, tt)
        scaled, ddiffs = [], []
        for g in range(len(WINDOWS)):
            cols = slice(g * GROUP, (g + 1) * GROUP)
            db = diff[g].astype(BF16)
            wpg = wp_ref[g].astype(BF16)
            pre = _dot(db, wpg) + bp_ref[:, cols]
            small_ref[ROW_POOL:ROW_POOL + 1, cols] += _colsum(dyb[:, cols] * pre)
            dpre = dyb[:, cols] * ps_ref[:, cols]
            small_ref[ROW_POOL:ROW_POOL + 1, D_B + g * GROUP:D_B + (g + 1) * GROUP] += _colsum(dpre)
            dpb = dpre.astype(BF16)
            _acc_rows(small_ref, ROW_WP, g, _dot_tn(db, dpb))
            ddiff = _dot_nt(dpb, wpg)
            ddiffs.append(ddiff)
            scaled.append(ddiff * inv[g])
        scaled_all = jnp.concatenate(scaled, axis=1)
        ext = jnp.concatenate([scaled_all, carry[...]], axis=0)
        n_ext = tt + HALO
        s2 = ext + pltpu.roll(ext, n_ext - 1, 0)
        t4 = s2[:, GROUP:]
        s4 = t4 + pltpu.roll(t4, n_ext - 2, 0)
        t8 = s4[:, GROUP:]
        s8 = t8 + pltpu.roll(t8, n_ext - 4, 0)
        t16 = s8[:, GROUP:]
        s16 = t16 + pltpu.roll(t16, n_ext - 8, 0)
        back = [s2[:, :GROUP], s4[:, :GROUP], s8[:, :GROUP], s16]
        carry[...] = scaled_all[:HALO, :]
        dzb = jnp.concatenate([back[g][:tt, :] - ddiffs[g] for g in range(len(WINDOWS))], axis=1)

        dzv = jnp.concatenate([dza, dzb], axis=1).astype(BF16)
        dh1 = _dot(dzv, win_ref[...])
        xv = x_ref[...]
        r1 = _rstd(xv)
        xhat = xv * r1
        shift1, scale1 = mod_ref[0:1, :], mod_ref[1:2, :]
        n1 = xhat * n1pre_ref[...]
        h1 = (n1 * (1.0 + scale1) + shift1).astype(BF16)
        acc_in[...] += _dot_tn(dzv, h1)
        dn1 = dh1 * (1.0 + scale1)
        gx_ref[...] = dx1_ref[...] + _rms_bwd(dn1 * n1pre_ref[...], xhat, r1)
        small_ref[ROW_DMOD:ROW_DMOD + 1, :] += _colsum(dh1)
        small_ref[ROW_DMOD + 1:ROW_DMOD + 2, :] += _colsum(dh1 * n1)
        small_ref[ROW_N1PRE:ROW_N1PRE + 1, :] += _colsum(dn1 * xhat)

        @pl.when(s == nt - 1)
        def _():
            gwin_ref[...] = acc_in[...].astype(BF16)
            gwout_ref[...] = acc_out[...].astype(BF16)
            bs = _unfold(bs_acc[...])
            for h in range(N_HEADS):
                small_ref[ROW_BS + h:ROW_BS + h + 1, 0:GROUP] = jnp.sum(
                    bs[:, h * GROUP:(h + 1) * GROUP].T, axis=0, keepdims=True)
            for a in range(n_sums):
                for r in range(3):
                    chip_copy(a, r).wait_recv()
                    chip_copy(a, r).wait_send()

    rev = lambda w: pl.BlockSpec((tt, w), lambda s: (nt - 1 - s, 0))
    zprev = pl.BlockSpec((HALO, D_B), lambda s: (jnp.maximum((nt - 1 - s) * hb - 1, 0), 2))
    hbm = pl.BlockSpec(memory_space=pl.ANY)
    outs = pl.pallas_call(
        body, name="attn_bwd", grid=(nt,),
        out_shape=tuple([jax.ShapeDtypeStruct((t_len, D), F32), jax.ShapeDtypeStruct((D_Z, D), BF16),
                         jax.ShapeDtypeStruct((D, D), BF16), jax.ShapeDtypeStruct((SMALL_ROWS, D), F32)]
                        + [jax.ShapeDtypeStruct(cs.shape, cs.dtype) for cs in chip_sums]),
        in_specs=[rev(D), rev(D), rev(D), rev(D_Z), zprev, rev(D), _full((8, D)), _full((1, D)),
                  _resident((D_Z, D)), _resident((D, D)), _full((N_HEADS, CHUNK, CHUNK)), _full((CHUNK, D_A)),
                  _full((1, D_A)), _full((1, D_A)), _full((len(WINDOWS), GROUP, GROUP)), _full((1, D_B)),
                  _full((1, D_B)), _full((8, D)), _full((8, D))] + [hbm] * n_sums,
        out_specs=tuple([rev(D), _full((D_Z, D)), _full((D, D)), _full((SMALL_ROWS, D))] + [hbm] * n_sums),
        scratch_shapes=[pltpu.VMEM((HALO, D_B), F32), pltpu.VMEM((D_Z, D), F32), pltpu.VMEM((D, D), F32),
                        pltpu.VMEM((tt, 2 * D_A), F32), pltpu.VMEM((CHUNK // 2, D), F32),
                        pltpu.SemaphoreType.DMA((3 * n_sums,)), pltpu.SemaphoreType.DMA((3 * n_sums,))],
        compiler_params=pltpu.CompilerParams(dimension_semantics=("arbitrary",), vmem_limit_bytes=VMEM_LIMIT),
    )(dmix, dx1, x, z, z, cat, mod, n1pre, w_in_t, w_out, w_sp, bs_rows, ln_g, ln_b, w_pool, b_pool, pool_scale,
      red_fwd, red_bwd, *chip_sums)
    return outs[:4], outs[4:]


def _adam(w, g, m, v):
    m2 = ADAM_B1 * m + (1.0 - ADAM_B1) * g
    v2 = ADAM_B2 * v + (1.0 - ADAM_B2) * (g * g)
    m_hat = m2 / (1.0 - ADAM_B1 ** ADAM_STEP)
    v_hat = v2 / (1.0 - ADAM_B2 ** ADAM_STEP)
    delta = -ADAM_LR * (m_hat / (jnp.sqrt(v_hat) + ADAM_EPS) + ADAM_WD * w)
    return delta, m2, v2


def _adamw_shard(name, rb, w, g, m, v):
    rows, cols = w.shape

    def body(w_ref, g_ref, m_ref, v_ref, d_ref, m2_ref, v2_ref):
        d_ref[...], m2_ref[...], v2_ref[...] = _adam(w_ref[...], g_ref[...], m_ref[...], v_ref[...])

    blk = pl.BlockSpec((rb, cols), lambda i: (i, 0))
    shp = jax.ShapeDtypeStruct((rows, cols), F32)
    return pl.pallas_call(
        body, name=name, grid=(rows // rb,), out_shape=(shp, shp, shp),
        in_specs=[blk] * 4, out_specs=(blk, blk, blk),
        compiler_params=pltpu.CompilerParams(dimension_semantics=("arbitrary",)),
    )(w, g, m, v)


def _adamw_chip_sums(name, rb, w, own, arrived, m, v):
    rows, cols = w.shape

    def body(w_ref, own_ref, arr_ref, m_ref, v_ref, g_ref, d_ref, m2_ref, v2_ref):
        g = own_ref[...]
        for r in range(3):
            g = g + arr_ref[r].astype(F32)
        g_ref[...] = g
        d_ref[...], m2_ref[...], v2_ref[...] = _adam(w_ref[...], g, m_ref[...], v_ref[...])

    blk = pl.BlockSpec((rb, cols), lambda i: (i, 0))
    shp = jax.ShapeDtypeStruct((rows, cols), F32)
    return pl.pallas_call(
        body, name=name, grid=(rows // rb,), out_shape=(shp, shp, shp, shp),
        in_specs=[blk, blk, pl.BlockSpec((3, rb, cols), lambda i: (0, i, 0)), blk, blk],
        out_specs=(blk, blk, blk, blk),
        compiler_params=pltpu.CompilerParams(dimension_semantics=("arbitrary",)),
    )(w, own, arrived, m, v)


def _adamw_ada(rb, w, sc, dmod_cols, m, v):
    rows, cols = w.shape

    def body(w_ref, sc_ref, dm_ref, m_ref, v_ref, g_ref, d_ref, m2_ref, v2_ref):
        g = _dot_tn(sc_ref[...].astype(BF16), dm_ref[...].astype(BF16))
        g_ref[...] = g
        d_ref[...], m2_ref[...], v2_ref[...] = _adam(w_ref[...], g, m_ref[...], v_ref[...])

    blk = pl.BlockSpec((rb, cols), lambda i: (i, 0))
    shp = jax.ShapeDtypeStruct((rows, cols), F32)
    return pl.pallas_call(
        body, name="adamw_ada", grid=(rows // rb,), out_shape=(shp, shp, shp, shp),
        in_specs=[blk, pl.BlockSpec((N_DEV, rb), lambda i: (0, i)), _full((N_DEV, cols)), blk, blk],
        out_specs=(blk, blk, blk, blk),
        compiler_params=pltpu.CompilerParams(dimension_semantics=("arbitrary",)),
    )(w, sc, dmod_cols, m, v)


def _unfold(acc_rows):
    return jnp.concatenate([acc_rows[:, :D_A], acc_rows[:, D_A:]], axis=0)


def _adamw_small(total, params):
    n = len(params)
    flat = [a for p in params for a in p]

    def body(*refs):
        s_ref = refs[0]
        p_refs = refs[1:1 + 3 * n]
        loss_ref = refs[1 + 3 * n]
        o_refs = refs[2 + 3 * n:]
        d_b_ada = s_ref[0:6, :]
        for b in range(1, N_DEV):
            d_b_ada = d_b_ada + s_ref[8 * b:8 * b + 6, :]
        tot = s_ref[PACK_SHIFT:PACK_ROWS, :]
        loss_ref[...] = jnp.broadcast_to(tot[ROW_LOSS:ROW_LOSS + 1, 0:GROUP], (8, GROUP))
        mask = _tril_mask()
        ws = _unfold(tot[ROW_WS:ROW_WS + 64, :])
        wp = _unfold(tot[ROW_WP:ROW_WP + 64, :])
        grads = [
            d_b_ada,
            tot[ROW_N1PRE:ROW_N1PRE + 1, :], tot[ROW_N1POST:ROW_N1POST + 1, :],
            tot[ROW_N2PRE:ROW_N2PRE + 1, :], tot[ROW_N2POST:ROW_N2POST + 1, :],
            tot[ROW_LN:ROW_LN + 1, :D_A], tot[ROW_LN:ROW_LN + 1, D_A:],
            tot[ROW_POOL:ROW_POOL + 1, :D_B], tot[ROW_POOL:ROW_POOL + 1, D_B:],
            tot[ROW_BS:ROW_BS + N_HEADS, 0:GROUP],
            jnp.stack([ws[:, h * GROUP:(h + 1) * GROUP] * mask for h in range(N_HEADS)]),
            jnp.stack([wp[:, g * GROUP:(g + 1) * GROUP] for g in range(len(WINDOWS))]),
        ]
        for k in range(n):
            w_ref, m_ref, v_ref = p_refs[3 * k:3 * k + 3]
            g = grads[k]
            o_refs[4 * k][...] = g
            o_refs[4 * k + 1][...], o_refs[4 * k + 2][...], o_refs[4 * k + 3][...] = _adam(
                w_ref[...], g, m_ref[...], v_ref[...])

    vm = pl.BlockSpec(memory_space=pltpu.VMEM)
    out_shape = [jax.ShapeDtypeStruct((8, GROUP), F32)]
    for w, _, _ in params:
        out_shape += [jax.ShapeDtypeStruct(w.shape, F32)] * 4
    return pl.pallas_call(
        body, name="adamw_small", out_shape=tuple(out_shape),
        in_specs=[vm] * (1 + 3 * n), out_specs=tuple([vm] * len(out_shape)),
    )(total, *flat)


TT_ATTN_FWD = 512
TT_MLP_FWD = 512
TT_MLP = 256
TT_WGRAD = 2048
TT_ATTN_BWD = 256


def kernel(x, c, w_ada, b_ada, norm1_pre, norm1_post, w_in, w_spatial, b_spatial, ln_v_gain, ln_v_bias, w_pool, b_pool, pool_scale, w_out, norm2_pre, norm2_post, w_fc1, w_fc2, loss_target, m_w_ada, m_b_ada, m_norm1_pre, m_norm1_post, m_w_in, m_w_spatial, m_b_spatial, m_ln_v_gain, m_ln_v_bias, m_w_pool, m_b_pool, m_pool_scale, m_w_out, m_norm2_pre, m_norm2_post, m_w_fc1, m_w_fc2, v_w_ada, v_b_ada, v_norm1_pre, v_norm1_post, v_w_in, v_w_spatial, v_b_spatial, v_ln_v_gain, v_ln_v_bias, v_w_pool, v_b_pool, v_pool_scale, v_w_out, v_norm2_pre, v_norm2_post, v_w_fc1, v_w_fc2):
    t_len = x.shape[1]
    me = 4 * lax.axis_index("x") + 2 * lax.axis_index("y") + lax.axis_index("c")
    ada_cols = w_ada.shape[1]
    tt = lambda want: min(want, t_len)

    x2 = x.reshape(t_len, D)
    tgt = loss_target.reshape(t_len, D)
    row = lambda a: a.reshape(1, -1)

    b_my = lax.dynamic_slice_in_dim(b_ada, me * ada_cols, ada_cols).reshape(1, ada_cols)
    modp, sc, (g_in, g_out), fc_shards = _fwd_comm(jnp.broadcast_to(c, (8, D)), w_ada, b_my,
                                                   [w_in.T, w_out], [w_fc1, w_fc2])
    mod = jnp.concatenate([modp.reshape(6, D), jnp.zeros((2, D), F32)], axis=0)
    w_in_t = g_in.reshape(D_Z, D)
    w_out_all = g_out.reshape(D, D)

    bs_rows = jnp.repeat(b_spatial.T, GROUP, axis=1)
    attn_consts = (w_spatial, bs_rows, row(ln_v_gain), row(ln_v_bias), w_pool, row(b_pool), row(pool_scale))

    (z, cat, mix, x1), (w1_early, w2_early) = _attn_fwd(
        tt(TT_ATTN_FWD), x2, mod, row(norm1_pre), row(norm1_post), w_in_t, w_out_all, *attn_consts, fc_shards)
    (r_early, h2, f_early), (w1_late, w2_late) = _mlp_fwd_early(
        tt(TT_MLP_FWD), x1, mod, row(norm2_pre), w1_early, w2_early, fc_shards)
    r, df, dy, red_fwd = _mlp_fwd_late(tt(TT_MLP_FWD), r_early, x1, h2, f_early, tgt, mod, row(norm2_post),
                                       w1_late, w2_late)
    da, dmix, dx1, red_bwd = _mlp_bwd(tt(TT_MLP), df, r, dy, x1, mix, mod, row(norm2_pre), row(norm1_post),
                                      w1_early, w2_early, w1_late, w2_late)
    own_w1, own_w2, sums_w1, sums_w2 = _mlp_wgrad(tt(TT_WGRAD), r, da, df, h2)
    (grad_x, p_in, p_out, small), (arr_w1, arr_w2) = _attn_bwd(
        tt(TT_ATTN_BWD), dmix, dx1, x2, z, cat, mod, row(norm1_pre), w_in_t, w_out_all, *attn_consts,
        red_fwd, red_bwd, [sums_w1, sums_w2])
    grad_in_t, grad_out, total = _tail_comm(
        [p_in.reshape(N_DEV, D_Z // N_DEV, D), p_out.reshape(N_DEV, D // N_DEV, D)], small, 64)

    grad_w1, d_w1, m_w1, v_w1 = _adamw_chip_sums("adamw_fc1", 256, w_fc1, own_w1, arr_w1, m_w_fc1, v_w_fc1)
    grad_w2, d_w2, m_w2, v_w2 = _adamw_chip_sums("adamw_fc2", 128, w_fc2, own_w2, arr_w2, m_w_fc2, v_w_fc2)
    d_out, m_out, v_out = _adamw_shard("adamw_out", 128, w_out, grad_out, m_w_out, v_w_out)
    d_in_t, m_in_t, v_in_t = _adamw_shard("adamw_in", D_Z // N_DEV, w_in.T, grad_in_t, m_w_in.T, v_w_in.T)
    dmod_all = total[0:TABLE_ROWS, :].reshape(N_DEV, 8, D)[:, :6, :].reshape(N_DEV, 6 * D)
    dmod_cols = lax.dynamic_slice_in_dim(dmod_all, me * ada_cols, ada_cols, axis=1)
    grad_ada, d_ada, m_ada, v_ada = _adamw_ada(256, w_ada, sc, dmod_cols, m_w_ada, v_w_ada)

    six = lambda a: a.reshape(6, D)
    small_params = [
        (six(b_ada), six(m_b_ada), six(v_b_ada)),
        (row(norm1_pre), row(m_norm1_pre), row(v_norm1_pre)),
        (row(norm1_post), row(m_norm1_post), row(v_norm1_post)),
        (row(norm2_pre), row(m_norm2_pre), row(v_norm2_pre)),
        (row(norm2_post), row(m_norm2_post), row(v_norm2_post)),
        (row(ln_v_gain), row(m_ln_v_gain), row(v_ln_v_gain)),
        (row(ln_v_bias), row(m_ln_v_bias), row(v_ln_v_bias)),
        (row(pool_scale), row(m_pool_scale), row(v_pool_scale)),
        (row(b_pool), row(m_b_pool), row(v_b_pool)),
        (b_spatial, m_b_spatial, v_b_spatial),
        (w_spatial, m_w_spatial, v_w_spatial),
        (w_pool, m_w_pool, v_w_pool),
    ]
    outs = _adamw_small(total, small_params)
    loss = outs[0][0, 0]
    names = ["b_ada", "norm1_pre", "norm1_post", "norm2_pre", "norm2_post", "ln_v_gain", "ln_v_bias", "pool_scale",
             "b_pool", "b_spatial", "w_spatial", "w_pool"]
    shapes = dict(b_ada=b_ada.shape, norm1_pre=norm1_pre.shape, norm1_post=norm1_post.shape,
                  norm2_pre=norm2_pre.shape, norm2_post=norm2_post.shape, ln_v_gain=ln_v_gain.shape,
                  ln_v_bias=ln_v_bias.shape, pool_scale=pool_scale.shape, b_pool=b_pool.shape,
                  b_spatial=b_spatial.shape, w_spatial=w_spatial.shape, w_pool=w_pool.shape)
    res = {}
    for k, nm in enumerate(names):
        res[nm] = tuple(o.reshape(shapes[nm]) for o in outs[1 + 4 * k:5 + 4 * k])
    res["w_ada"] = (grad_ada, d_ada, m_ada, v_ada)
    res["w_in"] = (grad_in_t.T, d_in_t.T, m_in_t.T, v_in_t.T)
    res["w_out"] = (grad_out, d_out, m_out, v_out)
    res["w_fc1"] = (grad_w1, d_w1, m_w1, v_w1)
    res["w_fc2"] = (grad_w2, d_w2, m_w2, v_w2)

    order = ["w_ada", "b_ada", "norm1_pre", "norm1_post", "w_in", "w_spatial", "b_spatial", "ln_v_gain", "ln_v_bias",
             "w_pool", "b_pool", "pool_scale", "w_out", "norm2_pre", "norm2_post", "w_fc1", "w_fc2"]
    return (loss, grad_x.reshape(x.shape),
            *[res[nm][0] for nm in order], *[res[nm][1] for nm in order],
            *[res[nm][2] for nm in order], *[res[nm][3] for nm in order])
```

```python
import functools

import jax
import jax.numpy as jnp
from jax import lax
from jax.experimental import pallas as pl
from jax.experimental.pallas import tpu as pltpu

F32 = jnp.float32
BF16 = jnp.bfloat16
MESH = pl.DeviceIdType.MESH

N_DEV = 8
D = 1024
D_A = 512
D_B = 512
D_Z = 2 * D_A + D_B
N_HEADS = 4
CHUNK = 128
WINDOWS = (2, 4, 8, 16)
GROUP = 128
D_FF = 4096
FF_BLK = D_FF // N_DEV
HALO = 16
EPS = 1e-6
VMEM_LIMIT = 60 * 1024 * 1024

ADAM_LR = 0.001
ADAM_B1 = 0.9
ADAM_B2 = 0.999
ADAM_EPS = 1e-08
ADAM_WD = 0.01
ADAM_STEP = 10

ROW_DMOD = 0
ROW_N1PRE, ROW_N1POST, ROW_N2PRE, ROW_N2POST = 8, 9, 10, 11
ROW_LN = 12
ROW_POOL = 13
ROW_LOSS = 14
ROW_BS = 16
ROW_WS = 24
ROW_WP = 88
SMALL_ROWS = 152
TABLE_ROWS = 8 * N_DEV
PACK_SHIFT = TABLE_ROWS - 8
PACK_ROWS = SMALL_ROWS + PACK_SHIFT
PACK_HALF = PACK_ROWS // 2


def _dot(a, b):
    return jnp.dot(a, b, preferred_element_type=F32)


def _dot_nt(a, b):
    return lax.dot_general(a, b, (((1,), (1,)), ((), ())), preferred_element_type=F32)


def _dot_tn(a, b):
    return lax.dot_general(a, b, (((0,), (0,)), ((), ())), preferred_element_type=F32)


def _rstd(v):
    return lax.rsqrt(jnp.mean(v * v, axis=-1, keepdims=True) + EPS)


def _rms_bwd(d_hat, hat, rstd):
    return rstd * (d_hat - hat * jnp.mean(d_hat * hat, axis=-1, keepdims=True))


_K0 = 0.7978845608028654
_K1 = 0.044715


def _gelu_parts(v):
    t = jnp.tanh(_K0 * (v + _K1 * (v * v * v)))
    return t, v * (0.5 * (1.0 + t))


def _gelu_grad(v, t):
    return 0.5 * (1.0 + t) + (0.5 * v) * (1.0 - t * t) * (_K0 * (1.0 + (3.0 * _K1) * (v * v)))


def _colsum(v):
    return jnp.sum(v, axis=0, keepdims=True)


def _full(shape):
    n = len(shape)
    return pl.BlockSpec(shape, lambda *_: (0,) * n)


def _resident(shape):
    n = len(shape)
    return pl.BlockSpec(shape, lambda *_: (0,) * n, pipeline_mode=pl.Buffered(1))


def _place():
    x, y, c = lax.axis_index("x"), lax.axis_index("y"), lax.axis_index("c")
    return x, y, c


def _flip(v, bit):
    return 1 - v if bit else v


def _peer(x, y, c, k):
    return (_flip(x, (k >> 2) & 1), _flip(y, (k >> 1) & 1), _flip(c, k & 1))


def _index(p):
    return 4 * p[0] + 2 * p[1] + p[2]


def _two_level_gather_begin(x, y, c, out_refs, send_sems, recv_sems):
    me = (x, y, c)
    sibling = (x, y, 1 - c)
    chips = [(1 - x, y), (x, 1 - y), (1 - x, 1 - y)]

    def copy(a, k, block, to):
        ref = out_refs[a].at[_index(block)]
        return pltpu.make_async_remote_copy(
            src_ref=ref, dst_ref=ref, send_sem=send_sems.at[7 * a + k], recv_sem=recv_sems.at[7 * a + k],
            device_id=to, device_id_type=MESH)

    first = []
    for a in range(len(out_refs)):
        first.append(copy(a, 0, me, sibling))
        first += [copy(a, 1 + j, me, (*chip, c)) for j, chip in enumerate(chips)]
    for cp in first:
        cp.start()
    return copy, first, me, sibling, chips


def _two_level_gather_finish(c, n, begun):
    copy, first, me, sibling, chips = begun
    passed = []
    for a in range(n):
        for j, chip in enumerate(chips):
            copy(a, 1 + j, (*chip, c), me).wait_recv()
            fwd = copy(a, 4 + j, (*chip, c), sibling)
            fwd.start()
            passed.append(fwd)
    for a in range(n):
        copy(a, 0, sibling, me).wait_recv()
        for j, chip in enumerate(chips):
            copy(a, 4 + j, (*chip, 1 - c), me).wait_recv()
    for cp in first + passed:
        cp.wait_send()


def _fwd_comm(c8, w_ada, b_my, gathered, kept):
    ncol = w_ada.shape[1]
    n_g, n_k = len(gathered), len(kept)

    def body(c_ref, w_ref, b_ref, *rest):
        g_in, k_in = rest[:n_g], rest[n_g:n_g + n_k]
        modp_ref, sc_ref = rest[n_g + n_k:n_g + n_k + 2]
        g_out = rest[n_g + n_k + 2:2 * n_g + n_k + 2]
        k_out = rest[2 * n_g + n_k + 2:2 * n_g + 2 * n_k + 2]
        early_ref = rest[2 * n_g + 2 * n_k + 2]
        cg, mg, part, send_sems, recv_sems, g_send, g_recv, e_send, e_recv = rest[2 * n_g + 2 * n_k + 3:]
        x, y, c = _place()
        me = _index((x, y, c))
        for a in range(n_g):
            g_out[a][me] = g_in[a][...].astype(BF16)
        begun = _two_level_gather_begin(x, y, c, g_out, g_send, g_recv)
        for a in range(n_k):
            k_out[a][...] = k_in[a][...].astype(BF16)
        early = _Copies([(k_out[0], early_ref.at[2], 2), (k_out[0], early_ref.at[4], 4)], e_send, e_recv)
        early.start(0, 1)

        def c_copy(k):
            p = _peer(x, y, c, k)
            return pltpu.make_async_remote_copy(
                src_ref=c_ref, dst_ref=cg.at[me], send_sem=send_sems.at[k - 1], recv_sem=recv_sems.at[k - 1],
                device_id=p, device_id_type=MESH)

        def c_arrival(k):
            p = _peer(x, y, c, k)
            return pltpu.make_async_remote_copy(
                src_ref=c_ref, dst_ref=cg.at[_index(p)], send_sem=send_sems.at[k - 1], recv_sem=recv_sems.at[k - 1],
                device_id=p, device_id_type=MESH)

        def m_copy(k):
            p = _peer(x, y, c, k)
            return pltpu.make_async_remote_copy(
                src_ref=part, dst_ref=mg.at[me], send_sem=send_sems.at[6 + k], recv_sem=recv_sems.at[6 + k],
                device_id=p, device_id_type=MESH)

        def m_arrival(k):
            p = _peer(x, y, c, k)
            return pltpu.make_async_remote_copy(
                src_ref=part, dst_ref=mg.at[_index(p)], send_sem=send_sems.at[6 + k], recv_sem=recv_sems.at[6 + k],
                device_id=p, device_id_type=MESH)

        for k in range(1, N_DEV):
            c_copy(k).start()
        cg[me] = c_ref[...]
        for k in range(1, N_DEV):
            c_arrival(k).wait_recv()
        c_all = jnp.concatenate([cg[j, 0:1, :] for j in range(N_DEV)], axis=0)
        sc = c_all * jax.nn.sigmoid(c_all)
        sc_ref[...] = sc
        part[...] = _dot(sc.astype(BF16), w_ref[...].astype(BF16)) + b_ref[...]
        for k in range(1, N_DEV):
            m_copy(k).start()
        mg[me] = part[...]
        for k in range(1, N_DEV):
            m_arrival(k).wait_recv()
        for j in range(N_DEV):
            modp_ref[j:j + 1, :] = mg[j, pl.ds(me, 1), :]
        _two_level_gather_finish(c, n_g, begun)
        for k in range(1, N_DEV):
            c_copy(k).wait_send()
            m_copy(k).wait_send()
        early.wait_recv(0, 1)
        early.wait_send(0, 1)

    vm = pl.BlockSpec(memory_space=pltpu.VMEM)
    outs = pl.pallas_call(
        body, name="fwd_comm",
        out_shape=tuple([jax.ShapeDtypeStruct((N_DEV, ncol), F32), jax.ShapeDtypeStruct((N_DEV, D), F32)]
                        + [jax.ShapeDtypeStruct((N_DEV,) + s.shape, BF16) for s in gathered]
                        + [jax.ShapeDtypeStruct(s.shape, BF16) for s in kept]
                        + [jax.ShapeDtypeStruct((FC_EARLY,) + kept[0].shape, BF16)]),
        in_specs=[vm] * (3 + n_g + n_k),
        out_specs=tuple([vm] * (2 + n_g + n_k) + [pl.BlockSpec(memory_space=pl.ANY)]),
        scratch_shapes=[
            pltpu.VMEM((N_DEV, 8, D), F32),
            pltpu.VMEM((N_DEV, N_DEV, ncol), F32),
            pltpu.VMEM((N_DEV, ncol), F32),
            pltpu.SemaphoreType.DMA((2 * (N_DEV - 1),)),
            pltpu.SemaphoreType.DMA((2 * (N_DEV - 1),)),
            pltpu.SemaphoreType.DMA((7 * n_g,)),
            pltpu.SemaphoreType.DMA((7 * n_g,)),
            pltpu.SemaphoreType.DMA((2,)),
            pltpu.SemaphoreType.DMA((2,)),
        ],
        compiler_params=pltpu.CompilerParams(vmem_limit_bytes=VMEM_LIMIT),
    )(c8, w_ada, b_my, *gathered, *kept)
    return outs[0], outs[1], outs[2:2 + n_g], outs[2 + n_g:2 + n_g + n_k], outs[2 + n_g + n_k]


FC_EARLY = 6


class _Copies:
    def __init__(self, entries, send_sems, recv_sems):
        self.place = _place()
        self.entries, self.send_sems, self.recv_sems = entries, send_sems, recv_sems

    def _copy(self, i, arrival=False):
        src, dst, rel = self.entries[i]
        return pltpu.make_async_remote_copy(
            src_ref=dst if arrival else src, dst_ref=dst, send_sem=self.send_sems.at[i],
            recv_sem=self.recv_sems.at[i], device_id=_peer(*self.place, rel), device_id_type=MESH)

    def start(self, *which):
        for i in which:
            self._copy(i).start()

    def wait_recv(self, *which):
        for i in which:
            self._copy(i, arrival=True).wait_recv()

    def wait_send(self, *which):
        for i in which:
            self._copy(i).wait_send()


def _tail_comm(parts, small, row_chunk):
    n = len(parts)

    def body(*refs):
        p_refs, small_ref = refs[:n], refs[n]
        g_refs, total_ref = refs[n + 1:2 * n + 1], refs[2 * n + 1]
        scr = refs[2 * n + 2:]
        from_sib = scr[0:n]
        chip_out = scr[n:2 * n]
        chip_in = scr[2 * n:3 * n]
        pack, pack_sib, halves = scr[3 * n:3 * n + 3]
        send_a, recv_a, send_b, recv_b, send_s, recv_s = scr[3 * n + 3:]
        x, y, c = _place()
        me = _index((x, y, c))
        sibling = (x, y, 1 - c)
        my_chip = 2 * x + y
        others = [(1 - x, y), (x, 1 - y), (1 - x, 1 - y)]
        my_half = pl.ds(pl.multiple_of(PACK_HALF * c, 8), PACK_HALF)

        def pack_to_sibling():
            return pltpu.make_async_remote_copy(
                src_ref=pack, dst_ref=pack_sib, send_sem=send_s.at[0], recv_sem=recv_s.at[0],
                device_id=sibling, device_id_type=MESH)

        def half_to_chip(r):
            return pltpu.make_async_remote_copy(
                src_ref=halves.at[my_chip], dst_ref=halves.at[my_chip],
                send_sem=send_s.at[1 + r], recv_sem=recv_s.at[1 + r],
                device_id=(*others[r], c), device_id_type=MESH)

        def half_from_chip(r):
            k = 2 * others[r][0] + others[r][1]
            return pltpu.make_async_remote_copy(
                src_ref=halves.at[k], dst_ref=halves.at[k], send_sem=send_s.at[1 + r], recv_sem=recv_s.at[1 + r],
                device_id=(*others[r], c), device_id_type=MESH)

        def total_to_sibling():
            return pltpu.make_async_remote_copy(
                src_ref=total_ref.at[my_half], dst_ref=total_ref.at[my_half],
                send_sem=send_s.at[4], recv_sem=recv_s.at[4], device_id=sibling, device_id_type=MESH)

        def total_from_sibling():
            sib_half = pl.ds(pl.multiple_of(PACK_HALF * (1 - c), 8), PACK_HALF)
            return pltpu.make_async_remote_copy(
                src_ref=total_ref.at[sib_half], dst_ref=total_ref.at[sib_half],
                send_sem=send_s.at[4], recv_sem=recv_s.at[4], device_id=sibling, device_id_type=MESH)

        pack[0:TABLE_ROWS, :] = jnp.zeros((TABLE_ROWS, D), F32)
        pack[pl.ds(pl.multiple_of(8 * me, 8), 8), :] = small_ref[0:8, :]
        pack[TABLE_ROWS:PACK_ROWS, :] = small_ref[8:SMALL_ROWS, :]
        pack_to_sibling().start()

        def to_sibling(a, k):
            return pltpu.make_async_remote_copy(
                src_ref=p_refs[a].at[2 * k + (1 - c)], dst_ref=from_sib[a].at[k],
                send_sem=send_a.at[a], recv_sem=recv_a.at[a], device_id=sibling, device_id_type=MESH)

        def all_from_sibling(a):
            return pltpu.make_async_remote_copy(
                src_ref=from_sib[a], dst_ref=from_sib[a], send_sem=send_a.at[a], recv_sem=recv_a.at[a],
                device_id=sibling, device_id_type=MESH)

        def to_chip(a, r):
            return pltpu.make_async_remote_copy(
                src_ref=chip_out[a].at[r], dst_ref=chip_in[a].at[r],
                send_sem=send_b.at[3 * a + r], recv_sem=recv_b.at[3 * a + r],
                device_id=(*others[r], c), device_id_type=MESH)

        for a in range(n):
            for k in range(4):
                to_sibling(a, k).start()
        pack_to_sibling().wait_recv()
        halves[my_chip] = pack[my_half, :] + pack_sib[my_half, :]
        for r in range(3):
            half_to_chip(r).start()
        for a in range(n):
            all_from_sibling(a).wait_recv()
            rows = p_refs[a].shape[1]
            for r in range(3):
                k = 2 * others[r][0] + others[r][1]
                for s in range(0, rows, row_chunk):
                    sl = pl.ds(s, row_chunk)
                    chip_out[a][r, sl, :] = (p_refs[a][2 * k + c, sl, :].astype(F32)
                                             + from_sib[a][k, sl, :].astype(F32)).astype(BF16)
                to_chip(a, r).start()
            for s in range(0, rows, row_chunk):
                sl = pl.ds(s, row_chunk)
                g_refs[a][sl, :] = (p_refs[a][2 * my_chip + c, sl, :].astype(F32)
                                    + from_sib[a][my_chip, sl, :].astype(F32))
        for r in range(3):
            half_from_chip(r).wait_recv()
        total_ref[my_half, :] = ((halves[0] + halves[1]) + halves[2]) + halves[3]
        total_to_sibling().start()
        for a in range(n):
            rows = p_refs[a].shape[1]
            for r in range(3):
                to_chip(a, r).wait_recv()
                for s in range(0, rows, row_chunk):
                    sl = pl.ds(s, row_chunk)
                    g_refs[a][sl, :] = g_refs[a][sl, :] + chip_in[a][r, sl, :].astype(F32)
        total_from_sibling().wait_recv()
        for a in range(n):
            all_from_sibling(a).wait_send()
            for r in range(3):
                to_chip(a, r).wait_send()
        pack_to_sibling().wait_send()
        for r in range(3):
            half_to_chip(r).wait_send()
        total_to_sibling().wait_send()

    vm = pl.BlockSpec(memory_space=pltpu.VMEM)
    return pl.pallas_call(
        body, name="tail_comm",
        out_shape=tuple([jax.ShapeDtypeStruct(p.shape[1:], F32) for p in parts]
                        + [jax.ShapeDtypeStruct((PACK_ROWS, D), F32)]),
        in_specs=[vm] * (n + 1), out_specs=tuple([vm] * (n + 1)),
        scratch_shapes=(
            [pltpu.VMEM((4,) + p.shape[1:], BF16) for p in parts]
            + [pltpu.VMEM((3,) + p.shape[1:], BF16) for p in parts]
            + [pltpu.VMEM((3,) + p.shape[1:], BF16) for p in parts]
            + [pltpu.VMEM((PACK_ROWS, D), F32), pltpu.VMEM((PACK_ROWS, D), F32),
               pltpu.VMEM((4, PACK_HALF, D), F32)]
            + [pltpu.SemaphoreType.DMA((n,)), pltpu.SemaphoreType.DMA((n,)),
               pltpu.SemaphoreType.DMA((3 * n,)), pltpu.SemaphoreType.DMA((3 * n,)),
               pltpu.SemaphoreType.DMA((5,)), pltpu.SemaphoreType.DMA((5,))]),
        compiler_params=pltpu.CompilerParams(vmem_limit_bytes=VMEM_LIMIT),
    )(*parts, small)


def _tril_mask():
    row = lax.broadcasted_iota(jnp.int32, (CHUNK, CHUNK), 0)
    col = lax.broadcasted_iota(jnp.int32, (CHUNK, CHUNK), 1)
    return (col <= row).astype(F32)


def _window_sums(ext):
    s2 = ext + pltpu.roll(ext, 1, 0)
    t4 = s2[:, GROUP:]
    s4 = t4 + pltpu.roll(t4, 2, 0)
    t8 = s4[:, GROUP:]
    s8 = t8 + pltpu.roll(t8, 4, 0)
    t16 = s8[:, GROUP:]
    s16 = t16 + pltpu.roll(t16, 8, 0)
    return [s2[:, :GROUP], s4[:, :GROUP], s8[:, :GROUP], s16]


def _inv_counts(first_pos, rows):
    pos = first_pos + lax.broadcasted_iota(jnp.int32, (rows, 1), 0)
    return [1.0 / jnp.minimum(pos + 1, w).astype(F32) for w in WINDOWS]


def _pool_diff(zb, halo, first_pos):
    tt = zb.shape[0]
    sums = _window_sums(jnp.concatenate([halo, zb], axis=0))
    inv = _inv_counts(first_pos, tt)
    return [sums[g][HALO:, :] * inv[g] - zb[:, g * GROUP:(g + 1) * GROUP] for g in range(len(WINDOWS))]


def _attn_fwd(tt, x, mod, n1pre, n1post, w_in_t, w_out, w_sp, bs_rows, ln_g, ln_b, w_pool, b_pool, pool_scale,
              fc_shards, w1_early):
    t_len = x.shape[0]
    nt = t_len // tt

    def body(x_ref, mod_ref, n1pre_ref, n1post_ref, win_ref, wout_ref, wsp_ref, bs_ref, lng_ref, lnb_ref,
             wp_ref, bp_ref, ps_ref, w1_ref, w2_ref, e1_in, z_ref, cat_ref, mix_ref, x1_ref, e1_ref, e2_ref,
             carry, send_sems, recv_sems, local_sems):
        i = pl.program_id(0)
        copies = _Copies(
            [(w1_ref, e1_ref.at[1], 1), (w2_ref, e2_ref.at[1], 1),
             (w2_ref, e2_ref.at[2], 2), (w2_ref, e2_ref.at[4], 4),
             (e1_ref.at[2], e1_ref.at[3], 1), (e1_ref.at[4], e1_ref.at[5], 1),
             (e2_ref.at[2], e2_ref.at[3], 1), (e2_ref.at[4], e2_ref.at[5], 1)],
            send_sems, recv_sems)
        own = [pltpu.make_async_copy(w1_ref, e1_ref.at[0], local_sems.at[0]),
               pltpu.make_async_copy(w2_ref, e2_ref.at[0], local_sems.at[1])]

        @pl.when(i == 0)
        def _():
            copies.start(2, 3, 0, 1, 4, 5)
            for cp in own:
                cp.start()
            carry[...] = jnp.zeros_like(carry)

        @pl.when(i == nt - 1)
        def _():
            copies.wait_recv(2, 3)
            copies.start(6, 7)

        xv = x_ref[...]
        shift1, scale1, gate1 = mod_ref[0:1, :], mod_ref[1:2, :], mod_ref[2:3, :]
        h1 = (xv * _rstd(xv) * n1pre_ref[...]) * (1.0 + scale1) + shift1
        z = _dot_nt(h1.astype(BF16), win_ref[...])
        z_ref[...] = z

        _, ga = _gelu_parts(z[:, :2 * D_A])
        u, vr = ga[:, :D_A], ga[:, D_A:]
        dv = vr - jnp.mean(vr, axis=-1, keepdims=True)
        v = (dv * lax.rsqrt(jnp.mean(dv * dv, axis=-1, keepdims=True) + EPS)) * lng_ref[...] + lnb_ref[...]
        vb = v.astype(BF16)
        mask = _tril_mask()
        wc = [(wsp_ref[h] * mask).astype(BF16) for h in range(N_HEADS)]
        for ch in range(tt // CHUNK):
            rows = slice(ch * CHUNK, (ch + 1) * CHUNK)
            for h in range(N_HEADS):
                cols = slice(h * GROUP, (h + 1) * GROUP)
                mixed = _dot(wc[h], vb[rows, cols]) + bs_ref[:, cols]
                cat_ref[rows, cols] = (u[rows, cols] * mixed).astype(BF16)

        zb = z[:, 2 * D_A:]
        diff = _pool_diff(zb, carry[...], i * tt)
        carry[...] = zb[tt - HALO:, :]
        for g in range(len(WINDOWS)):
            cols = slice(g * GROUP, (g + 1) * GROUP)
            pre = _dot(diff[g].astype(BF16), wp_ref[g].astype(BF16)) + bp_ref[:, cols]
            cat_ref[:, D_A + g * GROUP:D_A + (g + 1) * GROUP] = (pre * ps_ref[:, cols]).astype(BF16)

        mix = _dot(cat_ref[...], wout_ref[...])
        mix_ref[...] = mix
        x1_ref[...] = xv + gate1 * (mix * _rstd(mix) * n1post_ref[...])

        @pl.when(i == nt - 1)
        def _():
            copies.wait_recv(0, 1, 4, 5, 6, 7)
            copies.wait_send(*range(8))
            for cp in own:
                cp.wait()

    tile = lambda w: pl.BlockSpec((tt, w), lambda i: (i, 0))
    hbm = pl.BlockSpec(memory_space=pl.ANY)
    outs = pl.pallas_call(
        body, name="attn_fwd", grid=(nt,),
        out_shape=tuple([jax.ShapeDtypeStruct((t_len, D_Z), F32), jax.ShapeDtypeStruct((t_len, D), BF16),
                         jax.ShapeDtypeStruct((t_len, D), F32), jax.ShapeDtypeStruct((t_len, D), F32)]
                        + [jax.ShapeDtypeStruct((FC_EARLY,) + s.shape, BF16) for s in fc_shards]),
        in_specs=[tile(D), _full((8, D)), _full((1, D)), _full((1, D)), _resident((D_Z, D)), _resident((D, D)),
                  _full((N_HEADS, CHUNK, CHUNK)), _full((CHUNK, D_A)), _full((1, D_A)), _full((1, D_A)),
                  _full((len(WINDOWS), GROUP, GROUP)), _full((1, D_B)), _full((1, D_B)), hbm, hbm, hbm],
        out_specs=(tile(D_Z), tile(D), tile(D), tile(D), hbm, hbm),
        input_output_aliases={15: 4},
        scratch_shapes=[pltpu.VMEM((HALO, D_B), F32), pltpu.SemaphoreType.DMA((8,)),
                        pltpu.SemaphoreType.DMA((8,)), pltpu.SemaphoreType.DMA((2,))],
        compiler_params=pltpu.CompilerParams(dimension_semantics=("arbitrary",), vmem_limit_bytes=VMEM_LIMIT),
    )(x, mod, n1pre, n1post, w_in_t, w_out, w_sp, bs_rows, ln_g, ln_b, w_pool, b_pool, pool_scale, *fc_shards,
      w1_early)
    return outs[:4], outs[4:]


def _mlp_fwd_early(tt, x1, mod, n2pre, w1_early, w2_early):
    t_len = x1.shape[0]
    nt = t_len // tt
    n_late = N_DEV - FC_EARLY

    def body(x1_ref, mod_ref, n2pre_ref, w1_ref, w2_ref, e1_ref, e2_ref, r_ref, h2_ref, f_ref, l1_ref, l2_ref,
             send_sems, recv_sems):
        i = pl.program_id(0)
        copies = _Copies(
            [(e1_ref.at[2], l1_ref.at[0], 4), (e2_ref.at[4], l2_ref.at[0], 2),
             (l1_ref.at[0], l1_ref.at[1], 1), (l2_ref.at[0], l2_ref.at[1], 1)],
            send_sems, recv_sems)

        @pl.when(i == 0)
        def _():
            copies.start(0, 1)

        @pl.when(i == (3 * nt) // 4)
        def _():
            copies.wait_recv(0, 1)
            copies.start(2, 3)

        x1v = x1_ref[...]
        shift2, scale2 = mod_ref[3:4, :], mod_ref[4:5, :]
        h2 = ((x1v * _rstd(x1v) * n2pre_ref[...]) * (1.0 + scale2) + shift2).astype(BF16)
        h2_ref[...] = h2
        for j in range(FC_EARLY):
            cols = slice(j * FF_BLK, (j + 1) * FF_BLK)
            ra = jnp.maximum(_dot(h2, w1_ref[j]), 0.0)
            r = (ra * ra).astype(BF16)
            r_ref[:, cols] = r
            contrib = _dot(r, w2_ref[j])
            if j == 0:
                f_ref[...] = contrib
            else:
                f_ref[...] += contrib

        @pl.when(i == nt - 1)
        def _():
            copies.wait_recv(2, 3)
            copies.wait_send(0, 1, 2, 3)

    tile = lambda w: pl.BlockSpec((tt, w), lambda i: (i, 0))
    hbm = pl.BlockSpec(memory_space=pl.ANY)
    outs = pl.pallas_call(
        body, name="mlp_fwd_early", grid=(nt,),
        out_shape=(jax.ShapeDtypeStruct((t_len, D_FF), BF16),
                   jax.ShapeDtypeStruct((t_len, D), BF16), jax.ShapeDtypeStruct((t_len, D), F32),
                   jax.ShapeDtypeStruct((n_late,) + w1_early.shape[1:], BF16),
                   jax.ShapeDtypeStruct((n_late,) + w2_early.shape[1:], BF16)),
        in_specs=[tile(D), _full((8, D)), _full((1, D)),
                  _resident((FC_EARLY, D, FF_BLK)), _resident((FC_EARLY, FF_BLK, D)), hbm, hbm],
        out_specs=(tile(FC_EARLY * FF_BLK), tile(D), tile(D), hbm, hbm),
        scratch_shapes=[pltpu.SemaphoreType.DMA((4,)), pltpu.SemaphoreType.DMA((4,))],
        compiler_params=pltpu.CompilerParams(dimension_semantics=("arbitrary",), vmem_limit_bytes=VMEM_LIMIT),
    )(x1, mod, n2pre, w1_early, w2_early, w1_early, w2_early)
    return outs[:3], outs[3:]


def _mlp_fwd_late(tt, r_all, x1, h2, f_early, tgt, mod, n2post, w1_late, w2_late):
    t_len = x1.shape[0]
    nt = t_len // tt
    n_late = N_DEV - FC_EARLY

    def body(r_all_ref, x1_ref, h2_ref, fe_ref, tgt_ref, mod_ref, n2post_ref, w1_ref, w2_ref,
             r_ref, df_ref, dy_ref, red_ref):
        i = pl.program_id(0)

        @pl.when(i == 0)
        def _():
            red_ref[...] = jnp.zeros_like(red_ref)

        x1v = x1_ref[...]
        gate2 = mod_ref[5:6, :]
        h2 = h2_ref[...]
        f = fe_ref[...]
        for j in range(n_late):
            cols = slice(j * FF_BLK, (j + 1) * FF_BLK)
            ra = jnp.maximum(_dot(h2, w1_ref[j]), 0.0)
            r = (ra * ra).astype(BF16)
            r_ref[:, cols] = r
            f = f + _dot(r, w2_ref[j])
        rf = _rstd(f)
        fhat = f * rf
        nf = fhat * n2post_ref[...]
        err = (x1v + gate2 * nf) - tgt_ref[...]
        dy = err * (1.0 / D)
        dy_ref[...] = dy
        dnf = dy * gate2
        df_ref[...] = _rms_bwd(dnf * n2post_ref[...], fhat, rf).astype(BF16)
        red_ref[0:1, :] += _colsum(dy * nf)
        red_ref[1:2, :] += _colsum(dnf * fhat)
        red_ref[2:3, :] += _colsum(0.5 * jnp.mean(err * err, axis=-1, keepdims=True)) * jnp.ones((1, D), F32)

    tile = lambda w: pl.BlockSpec((tt, w), lambda i: (i, 0))
    return pl.pallas_call(
        body, name="mlp_fwd_late", grid=(nt,),
        out_shape=(jax.ShapeDtypeStruct((t_len, D_FF), BF16), jax.ShapeDtypeStruct((t_len, D), BF16),
                   jax.ShapeDtypeStruct((t_len, D), F32), jax.ShapeDtypeStruct((8, D), F32)),
        in_specs=[pl.BlockSpec(memory_space=pl.ANY), tile(D), tile(D), tile(D), tile(D), _full((8, D)),
                  _full((1, D)), _resident((n_late, D, FF_BLK)), _resident((n_late, FF_BLK, D))],
        out_specs=(pl.BlockSpec((tt, n_late * FF_BLK), lambda i: (i, FC_EARLY // n_late)), tile(D), tile(D),
                   _full((8, D))),
        input_output_aliases={0: 0},
        compiler_params=pltpu.CompilerParams(dimension_semantics=("arbitrary",), vmem_limit_bytes=VMEM_LIMIT),
    )(r_all, x1, h2, f_early, tgt, mod, n2post, w1_late, w2_late)


def _mlp_bwd(tt, df, r, dy, x1, mix, mod, n2pre, n1post, w1_early, w2_early, w1_late, w2_late):
    t_len = x1.shape[0]
    nt = t_len // tt
    n_late = N_DEV - FC_EARLY

    def body(df_ref, r_ref, dy_ref, x1_ref, mix_ref, mod_ref, n2pre_ref, n1post_ref,
             w1e_ref, w2e_ref, w1l_ref, w2l_ref, da_ref, dmix_ref, dx1_ref, red_ref, dh2_acc):
        i = pl.program_id(0)

        @pl.when(i == 0)
        def _():
            red_ref[...] = jnp.zeros_like(red_ref)

        dfv = df_ref[...]
        for j in range(N_DEV):
            cols = slice(j * FF_BLK, (j + 1) * FF_BLK)
            if j < FC_EARLY:
                w1, w2 = w1e_ref[j], w2e_ref[j]
            else:
                w1, w2 = w1l_ref[j - FC_EARLY], w2l_ref[j - FC_EARLY]
            dr = _dot_nt(dfv, w2)
            da = (dr * (2.0 * jnp.sqrt(r_ref[:, cols].astype(F32)))).astype(BF16)
            da_ref[:, cols] = da
            contrib = _dot_nt(da, w1)
            if j == 0:
                dh2_acc[...] = contrib
            else:
                dh2_acc[...] += contrib
        dh2 = dh2_acc[...]
        gate1, scale2 = mod_ref[2:3, :], mod_ref[4:5, :]
        x1v = x1_ref[...]
        r2 = _rstd(x1v)
        xhat = x1v * r2
        n2 = xhat * n2pre_ref[...]
        dn2 = dh2 * (1.0 + scale2)
        dx1 = dy_ref[...] + _rms_bwd(dn2 * n2pre_ref[...], xhat, r2)
        dx1_ref[...] = dx1
        mixv = mix_ref[...]
        rm = _rstd(mixv)
        mhat = mixv * rm
        dnm = dx1 * gate1
        dmix_ref[...] = _rms_bwd(dnm * n1post_ref[...], mhat, rm).astype(BF16)
        red_ref[0:1, :] += _colsum(dh2)
        red_ref[1:2, :] += _colsum(dh2 * n2)
        red_ref[2:3, :] += _colsum(dn2 * xhat)
        red_ref[3:4, :] += _colsum(dx1 * (mhat * n1post_ref[...]))
        red_ref[4:5, :] += _colsum(dnm * mhat)

    tile = lambda w: pl.BlockSpec((tt, w), lambda i: (i, 0))
    return pl.pallas_call(
        body, name="mlp_bwd", grid=(nt,),
        out_shape=(jax.ShapeDtypeStruct((t_len, D_FF), BF16),
                   jax.ShapeDtypeStruct((t_len, D), BF16), jax.ShapeDtypeStruct((t_len, D), F32),
                   jax.ShapeDtypeStruct((8, D), F32)),
        in_specs=[tile(D), tile(D_FF), tile(D), tile(D), tile(D),
                  _full((8, D)), _full((1, D)), _full((1, D)),
                  _resident((FC_EARLY, D, FF_BLK)), _resident((FC_EARLY, FF_BLK, D)),
                  _resident((n_late, D, FF_BLK)), _resident((n_late, FF_BLK, D))],
        out_specs=(tile(D_FF), tile(D), tile(D), _full((8, D))),
        scratch_shapes=[pltpu.VMEM((tt, D), F32)],
        compiler_params=pltpu.CompilerParams(dimension_semantics=("arbitrary",), vmem_limit_bytes=VMEM_LIMIT),
    )(df, r, dy, x1, mix, mod, n2pre, n1post, w1_early, w2_early, w1_late, w2_late)


def _mlp_wgrad(tt, r, da, df, h2):
    t_len = df.shape[0]
    nt = t_len // tt

    def relation(j):
        return jnp.where(j < 4, 2 * j + 1, (2 * j - 6) % N_DEV)

    def body(r_ref, da_ref, df_ref, h2_ref, own1_ref, own2_ref, out1_ref, out2_ref,
             acc1, acc2, snd1, snd2, sib1, sib2, send_sems, recv_sems):
        j, t = pl.program_id(0), pl.program_id(1)
        rows = pl.ds(pl.multiple_of(t * tt, tt), tt)
        x, y, c = _place()
        accs, snds, sibs = (acc1, acc2), (snd1, snd2), (sib1, sib2)

        def to_sibling(a, jj):
            return pltpu.make_async_remote_copy(
                src_ref=snds[a].at[jj % 2], dst_ref=sibs[a].at[jj],
                send_sem=send_sems.at[4 * a + jj], recv_sem=recv_sems.at[4 * a + jj],
                device_id=(x, y, 1 - c), device_id_type=MESH)

        @pl.when(t == 0)
        def _():
            acc2[...] = jnp.zeros_like(acc2)
            acc1[...] = jnp.zeros_like(acc1)

        acc2[...] += _dot_tn(r_ref[...], df_ref[rows, :])
        acc1[...] += _dot_tn(h2_ref[rows, :], da_ref[...])

        @pl.when((t == nt - 1) & (j < 4))
        def _():
            for a in range(2):
                @pl.when(j >= 2)
                def _():
                    to_sibling(a, j - 2).wait_send()

                snds[a][j % 2] = accs[a][...].astype(BF16)
                to_sibling(a, j).start()

        @pl.when((t == nt - 1) & (j >= 4))
        def _():
            jj = (j - 3) % 4
            for a, (own_ref, out_ref) in enumerate(((own1_ref, out1_ref), (own2_ref, out2_ref))):
                to_sibling(a, jj).wait_recv()

                @pl.when(j < N_DEV - 1)
                def _():
                    out_ref[0] = (accs[a][...] + sibs[a][jj].astype(F32)).astype(BF16)

                @pl.when(j == N_DEV - 1)
                def _():
                    own_ref[...] = accs[a][...] + sibs[a][jj].astype(F32)
                    to_sibling(a, 2).wait_send()
                    to_sibling(a, 3).wait_send()

    blk = pl.BlockSpec((tt, FF_BLK), lambda j, t: (t, relation(j)))
    chip = lambda j, t: (jnp.clip(j - 4, 0, 2), 0, 0)
    return pl.pallas_call(
        body, name="mlp_wgrad", grid=(N_DEV, nt),
        out_shape=(jax.ShapeDtypeStruct((D, FF_BLK), F32), jax.ShapeDtypeStruct((FF_BLK, D), F32),
                   jax.ShapeDtypeStruct((3, D, FF_BLK), BF16), jax.ShapeDtypeStruct((3, FF_BLK, D), BF16)),
        in_specs=[blk, blk, _resident((t_len, D)), _resident((t_len, D))],
        out_specs=(_full((D, FF_BLK)), _full((FF_BLK, D)),
                   pl.BlockSpec((1, D, FF_BLK), chip), pl.BlockSpec((1, FF_BLK, D), chip)),
        scratch_shapes=[pltpu.VMEM((D, FF_BLK), F32), pltpu.VMEM((FF_BLK, D), F32),
                        pltpu.VMEM((2, D, FF_BLK), BF16), pltpu.VMEM((2, FF_BLK, D), BF16),
                        pltpu.VMEM((4, D, FF_BLK), BF16), pltpu.VMEM((4, FF_BLK, D), BF16),
                        pltpu.SemaphoreType.DMA((8,)), pltpu.SemaphoreType.DMA((8,))],
        compiler_params=pltpu.CompilerParams(dimension_semantics=("arbitrary", "arbitrary"),
                                             vmem_limit_bytes=VMEM_LIMIT),
    )(r, da, df, h2)


def _acc_rows(ref, row0, k, val):
    half = CHUNK // 2
    ref[row0:row0 + half, k * GROUP:(k + 1) * GROUP] += val[:half, :]
    ref[row0:row0 + half, D_A + k * GROUP:D_A + (k + 1) * GROUP] += val[half:, :]


def _attn_bwd(tt, dmix, dx1, x, z, cat, mod, n1pre, w_in_t, w_out, w_sp, bs_rows, ln_g, ln_b, w_pool, b_pool,
              pool_scale, red_fwd, red_bwd, chip_sums):
    t_len = x.shape[0]
    nt = t_len // tt
    hb = tt // HALO
    n_sums = len(chip_sums)

    def body(dmix_ref, dx1_ref, x_ref, z_ref, zprev_ref, cat_ref, mod_ref, n1pre_ref, win_ref, wout_ref, wsp_ref,
             bs_ref, lng_ref, lnb_ref, wp_ref, bp_ref, ps_ref, redf_ref, redb_ref, *rest):
        sum_out = rest[:n_sums]
        gx_ref, gwin_ref, gwout_ref, small_ref = rest[n_sums:n_sums + 4]
        sum_in = rest[n_sums + 4:2 * n_sums + 4]
        carry, acc_in, acc_out, dz_scr, bs_acc, send_sems, recv_sems = rest[2 * n_sums + 4:]
        s = pl.program_id(0)
        i = nt - 1 - s
        px, py, pc = _place()

        def chip_copy(a, r):
            return pltpu.make_async_remote_copy(
                src_ref=sum_out[a].at[r], dst_ref=sum_in[a].at[r],
                send_sem=send_sems.at[3 * a + r], recv_sem=recv_sems.at[3 * a + r],
                device_id=_peer(px, py, pc, 2 * (r + 1)), device_id_type=MESH)

        @pl.when(s == 0)
        def _():
            for a in range(n_sums):
                for r in range(3):
                    chip_copy(a, r).start()
            carry[...] = jnp.zeros_like(carry)
            acc_in[...] = jnp.zeros_like(acc_in)
            acc_out[...] = jnp.zeros_like(acc_out)
            bs_acc[...] = jnp.zeros_like(bs_acc)
            small_ref[...] = jnp.zeros_like(small_ref)
            small_ref[ROW_DMOD + 2:ROW_DMOD + 3, :] = redb_ref[3:4, :]
            small_ref[ROW_DMOD + 3:ROW_DMOD + 5, :] = redb_ref[0:2, :]
            small_ref[ROW_DMOD + 5:ROW_DMOD + 6, :] = redf_ref[0:1, :]
            small_ref[ROW_N1POST:ROW_N1POST + 1, :] = redb_ref[4:5, :]
            small_ref[ROW_N2PRE:ROW_N2PRE + 1, :] = redb_ref[2:3, :]
            small_ref[ROW_N2POST:ROW_N2POST + 1, :] = redf_ref[1:2, :]
            small_ref[ROW_LOSS:ROW_LOSS + 1, :] = redf_ref[2:3, :]

        dmixv = dmix_ref[...]
        dcat = _dot_nt(dmixv, wout_ref[...])
        acc_out[...] += _dot_tn(cat_ref[...], dmixv)

        z = z_ref[...]
        t_g, ga = _gelu_parts(z[:, :2 * D_A])
        u, vr = ga[:, :D_A], ga[:, D_A:]
        dv0 = vr - jnp.mean(vr, axis=-1, keepdims=True)
        rv = lax.rsqrt(jnp.mean(dv0 * dv0, axis=-1, keepdims=True) + EPS)
        vhat = dv0 * rv
        vb = (vhat * lng_ref[...] + lnb_ref[...]).astype(BF16)
        mask = _tril_mask()
        wc = [(wsp_ref[h] * mask).astype(BF16) for h in range(N_HEADS)]

        dya = dcat[:, :D_A]
        for h in range(N_HEADS):
            cols = slice(h * GROUP, (h + 1) * GROUP)
            bs_sum = jnp.zeros((CHUNK, GROUP), F32)
            ws_sum = jnp.zeros((CHUNK, CHUNK), F32)
            for ch in range(tt // CHUNK):
                rows = slice(ch * CHUNK, (ch + 1) * CHUNK)
                v_ch = vb[rows, cols]
                mixed = _dot(wc[h], v_ch) + bs_ref[:, cols]
                dy_ch = dya[rows, cols]
                dz_scr[rows, cols] = dy_ch * mixed
                dmixed = dy_ch * u[rows, cols]
                dmb = dmixed.astype(BF16)
                dz_scr[rows, D_A + h * GROUP:D_A + (h + 1) * GROUP] = _dot_tn(wc[h], dmb)
                bs_sum = bs_sum + dmixed
                ws_sum = ws_sum + _dot_nt(dmb, v_ch)
            _acc_rows(bs_acc, 0, h, bs_sum)
            _acc_rows(small_ref, ROW_WS, h, ws_sum)

        dvl = dz_scr[:, D_A:2 * D_A]
        dvhat = dvl * lng_ref[...]
        dvr = rv * (dvhat - jnp.mean(dvhat, axis=-1, keepdims=True)
                    - vhat * jnp.mean(dvhat * vhat, axis=-1, keepdims=True))
        small_ref[ROW_LN:ROW_LN + 1, 0:D_A] += _colsum(dvl * vhat)
        small_ref[ROW_LN:ROW_LN + 1, D_A:D] += _colsum(dvl)
        dga = jnp.concatenate([dz_scr[:, :D_A], dvr], axis=1)
        dza = dga * _gelu_grad(z[:, :2 * D_A], t_g)

        zb = z[:, 2 * D_A:]
        halo_prev = jnp.where(i == 0, 0.0, zprev_ref[...])
        diff = _pool_diff(zb, halo_prev, i * tt)
        dyb = dcat[:, D_A:]
        inv = _inv_counts(i * tt, tt)
        scaled, ddiffs = [], []
        for g in range(len(WINDOWS)):
            cols = slice(g * GROUP, (g + 1) * GROUP)
            db = diff[g].astype(BF16)
            wpg = wp_ref[g].astype(BF16)
            pre = _dot(db, wpg) + bp_ref[:, cols]
            small_ref[ROW_POOL:ROW_POOL + 1, cols] += _colsum(dyb[:, cols] * pre)
            dpre = dyb[:, cols] * ps_ref[:, cols]
            small_ref[ROW_POOL:ROW_POOL + 1, D_B + g * GROUP:D_B + (g + 1) * GROUP] += _colsum(dpre)
            dpb = dpre.astype(BF16)
            _acc_rows(small_ref, ROW_WP, g, _dot_tn(db, dpb))
            ddiff = _dot_nt(dpb, wpg)
            ddiffs.append(ddiff)
            scaled.append(ddiff * inv[g])
        scaled_all = jnp.concatenate(scaled, axis=1)
        ext = jnp.concatenate([scaled_all, carry[...]], axis=0)
        n_ext = tt + HALO
        s2 = ext + pltpu.roll(ext, n_ext - 1, 0)
        t4 = s2[:, GROUP:]
        s4 = t4 + pltpu.roll(t4, n_ext - 2, 0)
        t8 = s4[:, GROUP:]
        s8 = t8 + pltpu.roll(t8, n_ext - 4, 0)
        t16 = s8[:, GROUP:]
        s16 = t16 + pltpu.roll(t16, n_ext - 8, 0)
        back = [s2[:, :GROUP], s4[:, :GROUP], s8[:, :GROUP], s16]
        carry[...] = scaled_all[:HALO, :]
        dzb = jnp.concatenate([back[g][:tt, :] - ddiffs[g] for g in range(len(WINDOWS))], axis=1)

        dzv = jnp.concatenate([dza, dzb], axis=1).astype(BF16)
        dh1 = _dot(dzv, win_ref[...])
        xv = x_ref[...]
        r1 = _rstd(xv)
        xhat = xv * r1
        shift1, scale1 = mod_ref[0:1, :], mod_ref[1:2, :]
        n1 = xhat * n1pre_ref[...]
        h1 = (n1 * (1.0 + scale1) + shift1).astype(BF16)
        acc_in[...] += _dot_tn(dzv, h1)
        dn1 = dh1 * (1.0 + scale1)
        gx_ref[...] = dx1_ref[...] + _rms_bwd(dn1 * n1pre_ref[...], xhat, r1)
        small_ref[ROW_DMOD:ROW_DMOD + 1, :] += _colsum(dh1)
        small_ref[ROW_DMOD + 1:ROW_DMOD + 2, :] += _colsum(dh1 * n1)
        small_ref[ROW_N1PRE:ROW_N1PRE + 1, :] += _colsum(dn1 * xhat)

        @pl.when(s == nt - 1)
        def _():
            gwin_ref[...] = acc_in[...].astype(BF16)
            gwout_ref[...] = acc_out[...].astype(BF16)
            bs = _unfold(bs_acc[...])
            for h in range(N_HEADS):
                small_ref[ROW_BS + h:ROW_BS + h + 1, 0:GROUP] = jnp.sum(
                    bs[:, h * GROUP:(h + 1) * GROUP].T, axis=0, keepdims=True)
            for a in range(n_sums):
                for r in range(3):
                    chip_copy(a, r).wait_recv()
                    chip_copy(a, r).wait_send()

    rev = lambda w: pl.BlockSpec((tt, w), lambda s: (nt - 1 - s, 0))
    zprev = pl.BlockSpec((HALO, D_B), lambda s: (jnp.maximum((nt - 1 - s) * hb - 1, 0), 2))
    hbm = pl.BlockSpec(memory_space=pl.ANY)
    outs = pl.pallas_call(
        body, name="attn_bwd", grid=(nt,),
        out_shape=tuple([jax.ShapeDtypeStruct((t_len, D), F32), jax.ShapeDtypeStruct((D_Z, D), BF16),
                         jax.ShapeDtypeStruct((D, D), BF16), jax.ShapeDtypeStruct((SMALL_ROWS, D), F32)]
                        + [jax.ShapeDtypeStruct(cs.shape, cs.dtype) for cs in chip_sums]),
        in_specs=[rev(D), rev(D), rev(D), rev(D_Z), zprev, rev(D), _full((8, D)), _full((1, D)),
                  _resident((D_Z, D)), _resident((D, D)), _full((N_HEADS, CHUNK, CHUNK)), _full((CHUNK, D_A)),
                  _full((1, D_A)), _full((1, D_A)), _full((len(WINDOWS), GROUP, GROUP)), _full((1, D_B)),
                  _full((1, D_B)), _full((8, D)), _full((8, D))] + [hbm] * n_sums,
        out_specs=tuple([rev(D), _full((D_Z, D)), _full((D, D)), _full((SMALL_ROWS, D))] + [hbm] * n_sums),
        scratch_shapes=[pltpu.VMEM((HALO, D_B), F32), pltpu.VMEM((D_Z, D), F32), pltpu.VMEM((D, D), F32),
                        pltpu.VMEM((tt, 2 * D_A), F32), pltpu.VMEM((CHUNK // 2, D), F32),
                        pltpu.SemaphoreType.DMA((3 * n_sums,)), pltpu.SemaphoreType.DMA((3 * n_sums,))],
        compiler_params=pltpu.CompilerParams(dimension_semantics=("arbitrary",), vmem_limit_bytes=VMEM_LIMIT),
    )(dmix, dx1, x, z, z, cat, mod, n1pre, w_in_t, w_out, w_sp, bs_rows, ln_g, ln_b, w_pool, b_pool, pool_scale,
      red_fwd, red_bwd, *chip_sums)
    return outs[:4], outs[4:]


def _adam(w, g, m, v):
    m2 = ADAM_B1 * m + (1.0 - ADAM_B1) * g
    v2 = ADAM_B2 * v + (1.0 - ADAM_B2) * (g * g)
    m_hat = m2 / (1.0 - ADAM_B1 ** ADAM_STEP)
    v_hat = v2 / (1.0 - ADAM_B2 ** ADAM_STEP)
    delta = -ADAM_LR * (m_hat / (jnp.sqrt(v_hat) + ADAM_EPS) + ADAM_WD * w)
    return delta, m2, v2


def _adamw_shard(name, rb, w, g, m, v):
    rows, cols = w.shape

    def body(w_ref, g_ref, m_ref, v_ref, d_ref, m2_ref, v2_ref):
        d_ref[...], m2_ref[...], v2_ref[...] = _adam(w_ref[...], g_ref[...], m_ref[...], v_ref[...])

    blk = pl.BlockSpec((rb, cols), lambda i: (i, 0))
    shp = jax.ShapeDtypeStruct((rows, cols), F32)
    return pl.pallas_call(
        body, name=name, grid=(rows // rb,), out_shape=(shp, shp, shp),
        in_specs=[blk] * 4, out_specs=(blk, blk, blk),
        compiler_params=pltpu.CompilerParams(dimension_semantics=("arbitrary",)),
    )(w, g, m, v)


def _adamw_chip_sums(name, rb, w, own, arrived, m, v):
    rows, cols = w.shape

    def body(w_ref, own_ref, arr_ref, m_ref, v_ref, g_ref, d_ref, m2_ref, v2_ref):
        g = own_ref[...]
        for r in range(3):
            g = g + arr_ref[r].astype(F32)
        g_ref[...] = g
        d_ref[...], m2_ref[...], v2_ref[...] = _adam(w_ref[...], g, m_ref[...], v_ref[...])

    blk = pl.BlockSpec((rb, cols), lambda i: (i, 0))
    shp = jax.ShapeDtypeStruct((rows, cols), F32)
    return pl.pallas_call(
        body, name=name, grid=(rows // rb,), out_shape=(shp, shp, shp, shp),
        in_specs=[blk, blk, pl.BlockSpec((3, rb, cols), lambda i: (0, i, 0)), blk, blk],
        out_specs=(blk, blk, blk, blk),
        compiler_params=pltpu.CompilerParams(dimension_semantics=("arbitrary",)),
    )(w, own, arrived, m, v)


def _adamw_ada(rb, w, sc, dmod_cols, m, v):
    rows, cols = w.shape

    def body(w_ref, sc_ref, dm_ref, m_ref, v_ref, g_ref, d_ref, m2_ref, v2_ref):
        g = _dot_tn(sc_ref[...].astype(BF16), dm_ref[...].astype(BF16))
        g_ref[...] = g
        d_ref[...], m2_ref[...], v2_ref[...] = _adam(w_ref[...], g, m_ref[...], v_ref[...])

    blk = pl.BlockSpec((rb, cols), lambda i: (i, 0))
    shp = jax.ShapeDtypeStruct((rows, cols), F32)
    return pl.pallas_call(
        body, name="adamw_ada", grid=(rows // rb,), out_shape=(shp, shp, shp, shp),
        in_specs=[blk, pl.BlockSpec((N_DEV, rb), lambda i: (0, i)), _full((N_DEV, cols)), blk, blk],
        out_specs=(blk, blk, blk, blk),
        compiler_params=pltpu.CompilerParams(dimension_semantics=("arbitrary",)),
    )(w, sc, dmod_cols, m, v)


def _unfold(acc_rows):
    return jnp.concatenate([acc_rows[:, :D_A], acc_rows[:, D_A:]], axis=0)


def _adamw_small(total, params):
    n = len(params)
    flat = [a for p in params for a in p]

    def body(*refs):
        s_ref = refs[0]
        p_refs = refs[1:1 + 3 * n]
        loss_ref = refs[1 + 3 * n]
        o_refs = refs[2 + 3 * n:]
        d_b_ada = s_ref[0:6, :]
        for b in range(1, N_DEV):
            d_b_ada = d_b_ada + s_ref[8 * b:8 * b + 6, :]
        tot = s_ref[PACK_SHIFT:PACK_ROWS, :]
        loss_ref[...] = jnp.broadcast_to(tot[ROW_LOSS:ROW_LOSS + 1, 0:GROUP], (8, GROUP))
        mask = _tril_mask()
        ws = _unfold(tot[ROW_WS:ROW_WS + 64, :])
        wp = _unfold(tot[ROW_WP:ROW_WP + 64, :])
        grads = [
            d_b_ada,
            tot[ROW_N1PRE:ROW_N1PRE + 1, :], tot[ROW_N1POST:ROW_N1POST + 1, :],
            tot[ROW_N2PRE:ROW_N2PRE + 1, :], tot[ROW_N2POST:ROW_N2POST + 1, :],
            tot[ROW_LN:ROW_LN + 1, :D_A], tot[ROW_LN:ROW_LN + 1, D_A:],
            tot[ROW_POOL:ROW_POOL + 1, :D_B], tot[ROW_POOL:ROW_POOL + 1, D_B:],
            tot[ROW_BS:ROW_BS + N_HEADS, 0:GROUP],
            jnp.stack([ws[:, h * GROUP:(h + 1) * GROUP] * mask for h in range(N_HEADS)]),
            jnp.stack([wp[:, g * GROUP:(g + 1) * GROUP] for g in range(len(WINDOWS))]),
        ]
        for k in range(n):
            w_ref, m_ref, v_ref = p_refs[3 * k:3 * k + 3]
            g = grads[k]
            o_refs[4 * k][...] = g
            o_refs[4 * k + 1][...], o_refs[4 * k + 2][...], o_refs[4 * k + 3][...] = _adam(
                w_ref[...], g, m_ref[...], v_ref[...])

    vm = pl.BlockSpec(memory_space=pltpu.VMEM)
    out_shape = [jax.ShapeDtypeStruct((8, GROUP), F32)]
    for w, _, _ in params:
        out_shape += [jax.ShapeDtypeStruct(w.shape, F32)] * 4
    return pl.pallas_call(
        body, name="adamw_small", out_shape=tuple(out_shape),
        in_specs=[vm] * (1 + 3 * n), out_specs=tuple([vm] * len(out_shape)),
    )(total, *flat)


TT_ATTN_FWD = 512
TT_MLP_FWD = 512
TT_MLP = 256
TT_WGRAD = 2048
TT_ATTN_BWD = 256


def kernel(x, c, w_ada, b_ada, norm1_pre, norm1_post, w_in, w_spatial, b_spatial, ln_v_gain, ln_v_bias, w_pool, b_pool, pool_scale, w_out, norm2_pre, norm2_post, w_fc1, w_fc2, loss_target, m_w_ada, m_b_ada, m_norm1_pre, m_norm1_post, m_w_in, m_w_spatial, m_b_spatial, m_ln_v_gain, m_ln_v_bias, m_w_pool, m_b_pool, m_pool_scale, m_w_out, m_norm2_pre, m_norm2_post, m_w_fc1, m_w_fc2, v_w_ada, v_b_ada, v_norm1_pre, v_norm1_post, v_w_in, v_w_spatial, v_b_spatial, v_ln_v_gain, v_ln_v_bias, v_w_pool, v_b_pool, v_pool_scale, v_w_out, v_norm2_pre, v_norm2_post, v_w_fc1, v_w_fc2):
    t_len = x.shape[1]
    me = 4 * lax.axis_index("x") + 2 * lax.axis_index("y") + lax.axis_index("c")
    ada_cols = w_ada.shape[1]
    tt = lambda want: min(want, t_len)

    x2 = x.reshape(t_len, D)
    tgt = loss_target.reshape(t_len, D)
    row = lambda a: a.reshape(1, -1)

    b_my = lax.dynamic_slice_in_dim(b_ada, me * ada_cols, ada_cols).reshape(1, ada_cols)
    modp, sc, (g_in, g_out), fc_shards, w1_begun = _fwd_comm(jnp.broadcast_to(c, (8, D)), w_ada, b_my,
                                                             [w_in.T, w_out], [w_fc1, w_fc2])
    mod = jnp.concatenate([modp.reshape(6, D), jnp.zeros((2, D), F32)], axis=0)
    w_in_t = g_in.reshape(D_Z, D)
    w_out_all = g_out.reshape(D, D)

    bs_rows = jnp.repeat(b_spatial.T, GROUP, axis=1)
    attn_consts = (w_spatial, bs_rows, row(ln_v_gain), row(ln_v_bias), w_pool, row(b_pool), row(pool_scale))

    (z, cat, mix, x1), (w1_early, w2_early) = _attn_fwd(
        tt(TT_ATTN_FWD), x2, mod, row(norm1_pre), row(norm1_post), w_in_t, w_out_all, *attn_consts, fc_shards,
        w1_begun)
    (r_early, h2, f_early), (w1_late, w2_late) = _mlp_fwd_early(
        tt(TT_MLP_FWD), x1, mod, row(norm2_pre), w1_early, w2_early)
    r, df, dy, red_fwd = _mlp_fwd_late(tt(TT_MLP_FWD), r_early, x1, h2, f_early, tgt, mod, row(norm2_post),
                                       w1_late, w2_late)
    da, dmix, dx1, red_bwd = _mlp_bwd(tt(TT_MLP), df, r, dy, x1, mix, mod, row(norm2_pre), row(norm1_post),
                                      w1_early, w2_early, w1_late, w2_late)
    own_w1, own_w2, sums_w1, sums_w2 = _mlp_wgrad(tt(TT_WGRAD), r, da, df, h2)
    (grad_x, p_in, p_out, small), (arr_w1, arr_w2) = _attn_bwd(
        tt(TT_ATTN_BWD), dmix, dx1, x2, z, cat, mod, row(norm1_pre), w_in_t, w_out_all, *attn_consts,
        red_fwd, red_bwd, [sums_w1, sums_w2])
    grad_in_t, grad_out, total = _tail_comm(
        [p_in.reshape(N_DEV, D_Z // N_DEV, D), p_out.reshape(N_DEV, D // N_DEV, D)], small, 64)

    grad_w1, d_w1, m_w1, v_w1 = _adamw_chip_sums("adamw_fc1", 256, w_fc1, own_w1, arr_w1, m_w_fc1, v_w_fc1)
    grad_w2, d_w2, m_w2, v_w2 = _adamw_chip_sums("adamw_fc2", 128, w_fc2, own_w2, arr_w2, m_w_fc2, v_w_fc2)
    d_out, m_out, v_out = _adamw_shard("adamw_out", 128, w_out, grad_out, m_w_out, v_w_out)
    d_in_t, m_in_t, v_in_t = _adamw_shard("adamw_in", D_Z // N_DEV, w_in.T, grad_in_t, m_w_in.T, v_w_in.T)
    dmod_all = total[0:TABLE_ROWS, :].reshape(N_DEV, 8, D)[:, :6, :].reshape(N_DEV, 6 * D)
    dmod_cols = lax.dynamic_slice_in_dim(dmod_all, me * ada_cols, ada_cols, axis=1)
    grad_ada, d_ada, m_ada, v_ada = _adamw_ada(256, w_ada, sc, dmod_cols, m_w_ada, v_w_ada)

    six = lambda a: a.reshape(6, D)
    small_params = [
        (six(b_ada), six(m_b_ada), six(v_b_ada)),
        (row(norm1_pre), row(m_norm1_pre), row(v_norm1_pre)),
        (row(norm1_post), row(m_norm1_post), row(v_norm1_post)),
        (row(norm2_pre), row(m_norm2_pre), row(v_norm2_pre)),
        (row(norm2_post), row(m_norm2_post), row(v_norm2_post)),
        (row(ln_v_gain), row(m_ln_v_gain), row(v_ln_v_gain)),
        (row(ln_v_bias), row(m_ln_v_bias), row(v_ln_v_bias)),
        (row(pool_scale), row(m_pool_scale), row(v_pool_scale)),
        (row(b_pool), row(m_b_pool), row(v_b_pool)),
        (b_spatial, m_b_spatial, v_b_spatial),
        (w_spatial, m_w_spatial, v_w_spatial),
        (w_pool, m_w_pool, v_w_pool),
    ]
    outs = _adamw_small(total, small_params)
    loss = outs[0][0, 0]
    names = ["b_ada", "norm1_pre", "norm1_post", "norm2_pre", "norm2_post", "ln_v_gain", "ln_v_bias", "pool_scale",
             "b_pool", "b_spatial", "w_spatial", "w_pool"]
    shapes = dict(b_ada=b_ada.shape, norm1_pre=norm1_pre.shape, norm1_post=norm1_post.shape,
                  norm2_pre=norm2_pre.shape, norm2_post=norm2_post.shape, ln_v_gain=ln_v_gain.shape,
                  ln_v_bias=ln_v_bias.shape, pool_scale=pool_scale.shape, b_pool=b_pool.shape,
                  b_spatial=b_spatial.shape, w_spatial=w_spatial.shape, w_pool=w_pool.shape)
    res = {}
    for k, nm in enumerate(names):
        res[nm] = tuple(o.reshape(shapes[nm]) for o in outs[1 + 4 * k:5 + 4 * k])
    res["w_ada"] = (grad_ada, d_ada, m_ada, v_ada)
    res["w_in"] = (grad_in_t.T, d_in_t.T, m_in_t.T, v_in_t.T)
    res["w_out"] = (grad_out, d_out, m_out, v_out)
    res["w_fc1"] = (grad_w1, d_w1, m_w1, v_w1)
    res["w_fc2"] = (grad_w2, d_w2, m_w2, v_w2)

    order = ["w_ada", "b_ada", "norm1_pre", "norm1_post", "w_in", "w_spatial", "b_spatial", "ln_v_gain", "ln_v_bias",
             "w_pool", "b_pool", "pool_scale", "w_out", "norm2_pre", "norm2_post", "w_fc1", "w_fc2"]
    return (loss, grad_x.reshape(x.shape),
            *[res[nm][0] for nm in order], *[res[nm][1] for nm in order],
            *[res[nm][2] for nm in order], *[res[nm][3] for nm in order])
```

```python
import functools

import jax
import jax.numpy as jnp
from jax import lax
from jax.experimental import pallas as pl
from jax.experimental.pallas import tpu as pltpu

F32 = jnp.float32
BF16 = jnp.bfloat16
MESH = pl.DeviceIdType.MESH

N_DEV = 8
D = 1024
D_A = 512
D_B = 512
D_Z = 2 * D_A + D_B
N_HEADS = 4
CHUNK = 128
WINDOWS = (2, 4, 8, 16)
GROUP = 128
D_FF = 4096
FF_BLK = D_FF // N_DEV
HALO = 16
EPS = 1e-6
VMEM_LIMIT = 60 * 1024 * 1024

ADAM_LR = 0.001
ADAM_B1 = 0.9
ADAM_B2 = 0.999
ADAM_EPS = 1e-08
ADAM_WD = 0.01
ADAM_STEP = 10

ROW_DMOD = 0
ROW_N1PRE, ROW_N1POST, ROW_N2PRE, ROW_N2POST = 8, 9, 10, 11
ROW_LN = 12
ROW_POOL = 13
ROW_LOSS = 14
ROW_BS = 16
ROW_WS = 24
ROW_WP = 88
SMALL_ROWS = 152
TABLE_ROWS = 8 * N_DEV
PACK_SHIFT = TABLE_ROWS - 8
PACK_ROWS = SMALL_ROWS + PACK_SHIFT
PACK_HALF = PACK_ROWS // 2


def _dot(a, b):
    return jnp.dot(a, b, preferred_element_type=F32)


def _dot_nt(a, b):
    return lax.dot_general(a, b, (((1,), (1,)), ((), ())), preferred_element_type=F32)


def _dot_tn(a, b):
    return lax.dot_general(a, b, (((0,), (0,)), ((), ())), preferred_element_type=F32)


def _rstd(v):
    return lax.rsqrt(jnp.mean(v * v, axis=-1, keepdims=True) + EPS)


def _rms_bwd(d_hat, hat, rstd):
    return rstd * (d_hat - hat * jnp.mean(d_hat * hat, axis=-1, keepdims=True))


_K0 = 0.7978845608028654
_K1 = 0.044715


def _gelu_parts(v):
    t = jnp.tanh(_K0 * (v + _K1 * (v * v * v)))
    return t, v * (0.5 * (1.0 + t))


def _gelu_grad(v, t):
    return 0.5 * (1.0 + t) + (0.5 * v) * (1.0 - t * t) * (_K0 * (1.0 + (3.0 * _K1) * (v * v)))


def _colsum(v):
    return jnp.sum(v, axis=0, keepdims=True)


def _full(shape):
    n = len(shape)
    return pl.BlockSpec(shape, lambda *_: (0,) * n)


def _resident(shape):
    n = len(shape)
    return pl.BlockSpec(shape, lambda *_: (0,) * n, pipeline_mode=pl.Buffered(1))


def _place():
    x, y, c = lax.axis_index("x"), lax.axis_index("y"), lax.axis_index("c")
    return x, y, c


def _flip(v, bit):
    return 1 - v if bit else v


def _peer(x, y, c, k):
    return (_flip(x, (k >> 2) & 1), _flip(y, (k >> 1) & 1), _flip(c, k & 1))


def _index(p):
    return 4 * p[0] + 2 * p[1] + p[2]


def _two_level_gather_begin(x, y, c, out_refs, send_sems, recv_sems):
    me = (x, y, c)
    sibling = (x, y, 1 - c)
    chips = [(1 - x, y), (x, 1 - y), (1 - x, 1 - y)]

    def copy(a, k, block, to):
        ref = out_refs[a].at[_index(block)]
        return pltpu.make_async_remote_copy(
            src_ref=ref, dst_ref=ref, send_sem=send_sems.at[7 * a + k], recv_sem=recv_sems.at[7 * a + k],
            device_id=to, device_id_type=MESH)

    first = []
    for a in range(len(out_refs)):
        first.append(copy(a, 0, me, sibling))
        first += [copy(a, 1 + j, me, (*chip, c)) for j, chip in enumerate(chips)]
    for cp in first:
        cp.start()
    return copy, first, me, sibling, chips


def _two_level_gather_finish(c, n, begun):
    copy, first, me, sibling, chips = begun
    passed = []
    for a in range(n):
        for j, chip in enumerate(chips):
            copy(a, 1 + j, (*chip, c), me).wait_recv()
            fwd = copy(a, 4 + j, (*chip, c), sibling)
            fwd.start()
            passed.append(fwd)
    for a in range(n):
        copy(a, 0, sibling, me).wait_recv()
        for j, chip in enumerate(chips):
            copy(a, 4 + j, (*chip, 1 - c), me).wait_recv()
    for cp in first + passed:
        cp.wait_send()


def _fwd_comm(c8, w_ada, b_my, gathered, kept):
    ncol = w_ada.shape[1]
    n_g, n_k = len(gathered), len(kept)

    def body(c_ref, w_ref, b_ref, *rest):
        g_in, k_in = rest[:n_g], rest[n_g:n_g + n_k]
        modp_ref, sc_ref = rest[n_g + n_k:n_g + n_k + 2]
        g_out = rest[n_g + n_k + 2:2 * n_g + n_k + 2]
        k_out = rest[2 * n_g + n_k + 2:2 * n_g + 2 * n_k + 2]
        cg, mg, part, send_sems, recv_sems, g_send, g_recv = rest[2 * n_g + 2 * n_k + 2:]
        x, y, c = _place()
        me = _index((x, y, c))
        for a in range(n_g):
            g_out[a][me] = g_in[a][...].astype(BF16)
        begun = _two_level_gather_begin(x, y, c, g_out, g_send, g_recv)
        for a in range(n_k):
            k_out[a][...] = k_in[a][...].astype(BF16)

        def c_copy(k):
            p = _peer(x, y, c, k)
            return pltpu.make_async_remote_copy(
                src_ref=c_ref, dst_ref=cg.at[me], send_sem=send_sems.at[k - 1], recv_sem=recv_sems.at[k - 1],
                device_id=p, device_id_type=MESH)

        def c_arrival(k):
            p = _peer(x, y, c, k)
            return pltpu.make_async_remote_copy(
                src_ref=c_ref, dst_ref=cg.at[_index(p)], send_sem=send_sems.at[k - 1], recv_sem=recv_sems.at[k - 1],
                device_id=p, device_id_type=MESH)

        def m_copy(k):
            p = _peer(x, y, c, k)
            return pltpu.make_async_remote_copy(
                src_ref=part, dst_ref=mg.at[me], send_sem=send_sems.at[6 + k], recv_sem=recv_sems.at[6 + k],
                device_id=p, device_id_type=MESH)

        def m_arrival(k):
            p = _peer(x, y, c, k)
            return pltpu.make_async_remote_copy(
                src_ref=part, dst_ref=mg.at[_index(p)], send_sem=send_sems.at[6 + k], recv_sem=recv_sems.at[6 + k],
                device_id=p, device_id_type=MESH)

        for k in range(1, N_DEV):
            c_copy(k).start()
        cg[me] = c_ref[...]
        for k in range(1, N_DEV):
            c_arrival(k).wait_recv()
        c_all = jnp.concatenate([cg[j, 0:1, :] for j in range(N_DEV)], axis=0)
        sc = c_all * jax.nn.sigmoid(c_all)
        sc_ref[...] = sc
        part[...] = _dot(sc.astype(BF16), w_ref[...].astype(BF16)) + b_ref[...]
        for k in range(1, N_DEV):
            m_copy(k).start()
        mg[me] = part[...]
        for k in range(1, N_DEV):
            m_arrival(k).wait_recv()
        for j in range(N_DEV):
            modp_ref[j:j + 1, :] = mg[j, pl.ds(me, 1), :]
        _two_level_gather_finish(c, n_g, begun)
        for k in range(1, N_DEV):
            c_copy(k).wait_send()
            m_copy(k).wait_send()

    vm = pl.BlockSpec(memory_space=pltpu.VMEM)
    outs = pl.pallas_call(
        body, name="fwd_comm",
        out_shape=tuple([jax.ShapeDtypeStruct((N_DEV, ncol), F32), jax.ShapeDtypeStruct((N_DEV, D), F32)]
                        + [jax.ShapeDtypeStruct((N_DEV,) + s.shape, BF16) for s in gathered]
                        + [jax.ShapeDtypeStruct(s.shape, BF16) for s in kept]),
        in_specs=[vm] * (3 + n_g + n_k), out_specs=tuple([vm] * (2 + n_g + n_k)),
        scratch_shapes=[
            pltpu.VMEM((N_DEV, 8, D), F32),
            pltpu.VMEM((N_DEV, N_DEV, ncol), F32),
            pltpu.VMEM((N_DEV, ncol), F32),
            pltpu.SemaphoreType.DMA((2 * (N_DEV - 1),)),
            pltpu.SemaphoreType.DMA((2 * (N_DEV - 1),)),
            pltpu.SemaphoreType.DMA((7 * n_g,)),
            pltpu.SemaphoreType.DMA((7 * n_g,)),
        ],
        compiler_params=pltpu.CompilerParams(vmem_limit_bytes=VMEM_LIMIT),
    )(c8, w_ada, b_my, *gathered, *kept)
    return outs[0], outs[1], outs[2:2 + n_g], outs[2 + n_g:]


FC_EARLY = 6


class _Copies:
    def __init__(self, entries, send_sems, recv_sems):
        self.place = _place()
        self.entries, self.send_sems, self.recv_sems = entries, send_sems, recv_sems

    def _copy(self, i, arrival=False):
        src, dst, rel = self.entries[i]
        return pltpu.make_async_remote_copy(
            src_ref=dst if arrival else src, dst_ref=dst, send_sem=self.send_sems.at[i],
            recv_sem=self.recv_sems.at[i], device_id=_peer(*self.place, rel), device_id_type=MESH)

    def start(self, *which):
        for i in which:
            self._copy(i).start()

    def wait_recv(self, *which):
        for i in which:
            self._copy(i, arrival=True).wait_recv()

    def wait_send(self, *which):
        for i in which:
            self._copy(i).wait_send()


def _tail_comm(parts, small, row_chunk):
    n = len(parts)

    def body(*refs):
        p_refs, small_ref = refs[:n], refs[n]
        g_refs, total_ref = refs[n + 1:2 * n + 1], refs[2 * n + 1]
        scr = refs[2 * n + 2:]
        from_sib = scr[0:n]
        chip_out = scr[n:2 * n]
        chip_in = scr[2 * n:3 * n]
        pack, pack_sib, halves = scr[3 * n:3 * n + 3]
        send_a, recv_a, send_b, recv_b, send_s, recv_s = scr[3 * n + 3:]
        x, y, c = _place()
        me = _index((x, y, c))
        sibling = (x, y, 1 - c)
        my_chip = 2 * x + y
        others = [(1 - x, y), (x, 1 - y), (1 - x, 1 - y)]
        my_half = pl.ds(pl.multiple_of(PACK_HALF * c, 8), PACK_HALF)

        def pack_to_sibling():
            return pltpu.make_async_remote_copy(
                src_ref=pack, dst_ref=pack_sib, send_sem=send_s.at[0], recv_sem=recv_s.at[0],
                device_id=sibling, device_id_type=MESH)

        def half_to_chip(r):
            return pltpu.make_async_remote_copy(
                src_ref=halves.at[my_chip], dst_ref=halves.at[my_chip],
                send_sem=send_s.at[1 + r], recv_sem=recv_s.at[1 + r],
                device_id=(*others[r], c), device_id_type=MESH)

        def half_from_chip(r):
            k = 2 * others[r][0] + others[r][1]
            return pltpu.make_async_remote_copy(
                src_ref=halves.at[k], dst_ref=halves.at[k], send_sem=send_s.at[1 + r], recv_sem=recv_s.at[1 + r],
                device_id=(*others[r], c), device_id_type=MESH)

        def total_to_sibling():
            return pltpu.make_async_remote_copy(
                src_ref=total_ref.at[my_half], dst_ref=total_ref.at[my_half],
                send_sem=send_s.at[4], recv_sem=recv_s.at[4], device_id=sibling, device_id_type=MESH)

        def total_from_sibling():
            sib_half = pl.ds(pl.multiple_of(PACK_HALF * (1 - c), 8), PACK_HALF)
            return pltpu.make_async_remote_copy(
                src_ref=total_ref.at[sib_half], dst_ref=total_ref.at[sib_half],
                send_sem=send_s.at[4], recv_sem=recv_s.at[4], device_id=sibling, device_id_type=MESH)

        pack[0:TABLE_ROWS, :] = jnp.zeros((TABLE_ROWS, D), F32)
        pack[pl.ds(pl.multiple_of(8 * me, 8), 8), :] = small_ref[0:8, :]
        pack[TABLE_ROWS:PACK_ROWS, :] = small_ref[8:SMALL_ROWS, :]
        pack_to_sibling().start()

        def to_sibling(a, k):
            return pltpu.make_async_remote_copy(
                src_ref=p_refs[a].at[2 * k + (1 - c)], dst_ref=from_sib[a].at[k],
                send_sem=send_a.at[a], recv_sem=recv_a.at[a], device_id=sibling, device_id_type=MESH)

        def all_from_sibling(a):
            return pltpu.make_async_remote_copy(
                src_ref=from_sib[a], dst_ref=from_sib[a], send_sem=send_a.at[a], recv_sem=recv_a.at[a],
                device_id=sibling, device_id_type=MESH)

        def to_chip(a, r):
            return pltpu.make_async_remote_copy(
                src_ref=chip_out[a].at[r], dst_ref=chip_in[a].at[r],
                send_sem=send_b.at[3 * a + r], recv_sem=recv_b.at[3 * a + r],
                device_id=(*others[r], c), device_id_type=MESH)

        for a in range(n):
            for k in range(4):
                to_sibling(a, k).start()
        pack_to_sibling().wait_recv()
        halves[my_chip] = pack[my_half, :] + pack_sib[my_half, :]
        for r in range(3):
            half_to_chip(r).start()
        for a in range(n):
            all_from_sibling(a).wait_recv()
            rows = p_refs[a].shape[1]
            for r in range(3):
                k = 2 * others[r][0] + others[r][1]
                for s in range(0, rows, row_chunk):
                    sl = pl.ds(s, row_chunk)
                    chip_out[a][r, sl, :] = (p_refs[a][2 * k + c, sl, :].astype(F32)
                                             + from_sib[a][k, sl, :].astype(F32)).astype(BF16)
                to_chip(a, r).start()
            for s in range(0, rows, row_chunk):
                sl = pl.ds(s, row_chunk)
                g_refs[a][sl, :] = (p_refs[a][2 * my_chip + c, sl, :].astype(F32)
                                    + from_sib[a][my_chip, sl, :].astype(F32))
        for r in range(3):
            half_from_chip(r).wait_recv()
        total_ref[my_half, :] = ((halves[0] + halves[1]) + halves[2]) + halves[3]
        total_to_sibling().start()
        for a in range(n):
            rows = p_refs[a].shape[1]
            for r in range(3):
                to_chip(a, r).wait_recv()
                for s in range(0, rows, row_chunk):
                    sl = pl.ds(s, row_chunk)
                    g_refs[a][sl, :] = g_refs[a][sl, :] + chip_in[a][r, sl, :].astype(F32)
        total_from_sibling().wait_recv()
        for a in range(n):
            all_from_sibling(a).wait_send()
            for r in range(3):
                to_chip(a, r).wait_send()
        pack_to_sibling().wait_send()
        for r in range(3):
            half_to_chip(r).wait_send()
        total_to_sibling().wait_send()

    vm = pl.BlockSpec(memory_space=pltpu.VMEM)
    return pl.pallas_call(
        body, name="tail_comm",
        out_shape=tuple([jax.ShapeDtypeStruct(p.shape[1:], F32) for p in parts]
                        + [jax.ShapeDtypeStruct((PACK_ROWS, D), F32)]),
        in_specs=[vm] * (n + 1), out_specs=tuple([vm] * (n + 1)),
        scratch_shapes=(
            [pltpu.VMEM((4,) + p.shape[1:], BF16) for p in parts]
            + [pltpu.VMEM((3,) + p.shape[1:], BF16) for p in parts]
            + [pltpu.VMEM((3,) + p.shape[1:], BF16) for p in parts]
            + [pltpu.VMEM((PACK_ROWS, D), F32), pltpu.VMEM((PACK_ROWS, D), F32),
               pltpu.VMEM((4, PACK_HALF, D), F32)]
            + [pltpu.SemaphoreType.DMA((n,)), pltpu.SemaphoreType.DMA((n,)),
               pltpu.SemaphoreType.DMA((3 * n,)), pltpu.SemaphoreType.DMA((3 * n,)),
               pltpu.SemaphoreType.DMA((5,)), pltpu.SemaphoreType.DMA((5,))]),
        compiler_params=pltpu.CompilerParams(vmem_limit_bytes=VMEM_LIMIT),
    )(*parts, small)


def _tril_mask():
    row = lax.broadcasted_iota(jnp.int32, (CHUNK, CHUNK), 0)
    col = lax.broadcasted_iota(jnp.int32, (CHUNK, CHUNK), 1)
    return (col <= row).astype(F32)


def _window_sums(ext):
    s2 = ext + pltpu.roll(ext, 1, 0)
    t4 = s2[:, GROUP:]
    s4 = t4 + pltpu.roll(t4, 2, 0)
    t8 = s4[:, GROUP:]
    s8 = t8 + pltpu.roll(t8, 4, 0)
    t16 = s8[:, GROUP:]
    s16 = t16 + pltpu.roll(t16, 8, 0)
    return [s2[:, :GROUP], s4[:, :GROUP], s8[:, :GROUP], s16]


def _inv_counts(first_pos, rows):
    pos = first_pos + lax.broadcasted_iota(jnp.int32, (rows, 1), 0)
    return [1.0 / jnp.minimum(pos + 1, w).astype(F32) for w in WINDOWS]


def _pool_diff(zb, halo, first_pos):
    tt = zb.shape[0]
    sums = _window_sums(jnp.concatenate([halo, zb], axis=0))
    inv = _inv_counts(first_pos, tt)
    return [sums[g][HALO:, :] * inv[g] - zb[:, g * GROUP:(g + 1) * GROUP] for g in range(len(WINDOWS))]


def _attn_fwd(tt, x, mod, n1pre, n1post, w_in_t, w_out, w_sp, bs_rows, ln_g, ln_b, w_pool, b_pool, pool_scale,
              fc_shards):
    t_len = x.shape[0]
    nt = t_len // tt

    def body(x_ref, mod_ref, n1pre_ref, n1post_ref, win_ref, wout_ref, wsp_ref, bs_ref, lng_ref, lnb_ref,
             wp_ref, bp_ref, ps_ref, w1_ref, w2_ref, z_ref, cat_ref, mix_ref, x1_ref, e1_ref, e2_ref,
             carry, send_sems, recv_sems, local_sems):
        i = pl.program_id(0)
        copies = _Copies(
            [(w1_ref, e1_ref.at[1], 1), (w2_ref, e2_ref.at[1], 1),
             (w1_ref, e1_ref.at[2], 2), (w2_ref, e2_ref.at[2], 2),
             (w1_ref, e1_ref.at[4], 4), (w2_ref, e2_ref.at[4], 4),
             (e1_ref.at[2], e1_ref.at[3], 1), (e2_ref.at[2], e2_ref.at[3], 1),
             (e1_ref.at[4], e1_ref.at[5], 1), (e2_ref.at[4], e2_ref.at[5], 1)],
            send_sems, recv_sems)
        own = [pltpu.make_async_copy(w1_ref, e1_ref.at[0], local_sems.at[0]),
               pltpu.make_async_copy(w2_ref, e2_ref.at[0], local_sems.at[1])]

        @pl.when(i == 0)
        def _():
            copies.start(2, 4, 3, 5, 0, 1)
            for cp in own:
                cp.start()
            carry[...] = jnp.zeros_like(carry)

        @pl.when(i == nt - 1)
        def _():
            copies.wait_recv(2, 3, 4, 5)
            copies.start(6, 7, 8, 9)

        xv = x_ref[...]
        shift1, scale1, gate1 = mod_ref[0:1, :], mod_ref[1:2, :], mod_ref[2:3, :]
        h1 = (xv * _rstd(xv) * n1pre_ref[...]) * (1.0 + scale1) + shift1
        z = _dot_nt(h1.astype(BF16), win_ref[...])
        z_ref[...] = z

        _, ga = _gelu_parts(z[:, :2 * D_A])
        u, vr = ga[:, :D_A], ga[:, D_A:]
        dv = vr - jnp.mean(vr, axis=-1, keepdims=True)
        v = (dv * lax.rsqrt(jnp.mean(dv * dv, axis=-1, keepdims=True) + EPS)) * lng_ref[...] + lnb_ref[...]
        vb = v.astype(BF16)
        mask = _tril_mask()
        wc = [(wsp_ref[h] * mask).astype(BF16) for h in range(N_HEADS)]
        for ch in range(tt // CHUNK):
            rows = slice(ch * CHUNK, (ch + 1) * CHUNK)
            for h in range(N_HEADS):
                cols = slice(h * GROUP, (h + 1) * GROUP)
                mixed = _dot(wc[h], vb[rows, cols]) + bs_ref[:, cols]
                cat_ref[rows, cols] = (u[rows, cols] * mixed).astype(BF16)

        zb = z[:, 2 * D_A:]
        diff = _pool_diff(zb, carry[...], i * tt)
        carry[...] = zb[tt - HALO:, :]
        for g in range(len(WINDOWS)):
            cols = slice(g * GROUP, (g + 1) * GROUP)
            pre = _dot(diff[g].astype(BF16), wp_ref[g].astype(BF16)) + bp_ref[:, cols]
            cat_ref[:, D_A + g * GROUP:D_A + (g + 1) * GROUP] = (pre * ps_ref[:, cols]).astype(BF16)

        mix = _dot(cat_ref[...], wout_ref[...])
        mix_ref[...] = mix
        x1_ref[...] = xv + gate1 * (mix * _rstd(mix) * n1post_ref[...])

        @pl.when(i == nt - 1)
        def _():
            copies.wait_recv(0, 1, 6, 7, 8, 9)
            copies.wait_send(*range(10))
            for cp in own:
                cp.wait()

    tile = lambda w: pl.BlockSpec((tt, w), lambda i: (i, 0))
    hbm = pl.BlockSpec(memory_space=pl.ANY)
    outs = pl.pallas_call(
        body, name="attn_fwd", grid=(nt,),
        out_shape=tuple([jax.ShapeDtypeStruct((t_len, D_Z), F32), jax.ShapeDtypeStruct((t_len, D), BF16),
                         jax.ShapeDtypeStruct((t_len, D), F32), jax.ShapeDtypeStruct((t_len, D), F32)]
                        + [jax.ShapeDtypeStruct((FC_EARLY,) + s.shape, BF16) for s in fc_shards]),
        in_specs=[tile(D), _full((8, D)), _full((1, D)), _full((1, D)), _resident((D_Z, D)), _resident((D, D)),
                  _full((N_HEADS, CHUNK, CHUNK)), _full((CHUNK, D_A)), _full((1, D_A)), _full((1, D_A)),
                  _full((len(WINDOWS), GROUP, GROUP)), _full((1, D_B)), _full((1, D_B)),
                  _resident(fc_shards[0].shape), _resident(fc_shards[1].shape)],
        out_specs=(tile(D_Z), tile(D), tile(D), tile(D), hbm, hbm),
        scratch_shapes=[pltpu.VMEM((HALO, D_B), F32), pltpu.SemaphoreType.DMA((10,)),
                        pltpu.SemaphoreType.DMA((10,)), pltpu.SemaphoreType.DMA((2,))],
        compiler_params=pltpu.CompilerParams(dimension_semantics=("arbitrary",), vmem_limit_bytes=VMEM_LIMIT),
    )(x, mod, n1pre, n1post, w_in_t, w_out, w_sp, bs_rows, ln_g, ln_b, w_pool, b_pool, pool_scale, *fc_shards)
    return outs[:4], outs[4:]


def _mlp_fwd_early(tt, x1, mod, n2pre, w1_early, w2_early):
    t_len = x1.shape[0]
    nt = t_len // tt
    n_late = N_DEV - FC_EARLY

    def body(x1_ref, mod_ref, n2pre_ref, w1_ref, w2_ref, r_ref, h2_ref, f_ref, l1_ref, l2_ref,
             send_sems, recv_sems):
        i = pl.program_id(0)
        copies = _Copies(
            [(w1_ref.at[2], l1_ref.at[0], 4), (w2_ref.at[4], l2_ref.at[0], 2),
             (l1_ref.at[0], l1_ref.at[1], 1), (l2_ref.at[0], l2_ref.at[1], 1)],
            send_sems, recv_sems)

        @pl.when(i == 0)
        def _():
            copies.start(0, 1)

        @pl.when(i == (3 * nt) // 4)
        def _():
            copies.wait_recv(0, 1)
            copies.start(2, 3)

        x1v = x1_ref[...]
        shift2, scale2 = mod_ref[3:4, :], mod_ref[4:5, :]
        h2 = ((x1v * _rstd(x1v) * n2pre_ref[...]) * (1.0 + scale2) + shift2).astype(BF16)
        h2_ref[...] = h2
        for j in range(FC_EARLY):
            cols = slice(j * FF_BLK, (j + 1) * FF_BLK)
            ra = jnp.maximum(_dot(h2, w1_ref[j]), 0.0)
            r = (ra * ra).astype(BF16)
            r_ref[:, cols] = r
            contrib = _dot(r, w2_ref[j])
            if j == 0:
                f_ref[...] = contrib
            else:
                f_ref[...] += contrib

        @pl.when(i == nt - 1)
        def _():
            copies.wait_recv(2, 3)
            copies.wait_send(0, 1, 2, 3)

    tile = lambda w: pl.BlockSpec((tt, w), lambda i: (i, 0))
    hbm = pl.BlockSpec(memory_space=pl.ANY)
    outs = pl.pallas_call(
        body, name="mlp_fwd_early", grid=(nt,),
        out_shape=(jax.ShapeDtypeStruct((t_len, D_FF), BF16),
                   jax.ShapeDtypeStruct((t_len, D), BF16), jax.ShapeDtypeStruct((t_len, D), F32),
                   jax.ShapeDtypeStruct((n_late,) + w1_early.shape[1:], BF16),
                   jax.ShapeDtypeStruct((n_late,) + w2_early.shape[1:], BF16)),
        in_specs=[tile(D), _full((8, D)), _full((1, D)),
                  _resident((FC_EARLY, D, FF_BLK)), _resident((FC_EARLY, FF_BLK, D))],
        out_specs=(tile(FC_EARLY * FF_BLK), tile(D), tile(D), hbm, hbm),
        scratch_shapes=[pltpu.SemaphoreType.DMA((4,)), pltpu.SemaphoreType.DMA((4,))],
        compiler_params=pltpu.CompilerParams(dimension_semantics=("arbitrary",), vmem_limit_bytes=VMEM_LIMIT),
    )(x1, mod, n2pre, w1_early, w2_early)
    return outs[:3], outs[3:]


def _mlp_fwd_late(tt, r_all, x1, h2, f_early, tgt, mod, n2post, w1_late, w2_late):
    t_len = x1.shape[0]
    nt = t_len // tt
    n_late = N_DEV - FC_EARLY

    def body(r_all_ref, x1_ref, h2_ref, fe_ref, tgt_ref, mod_ref, n2post_ref, w1_ref, w2_ref,
             r_ref, df_ref, dy_ref, red_ref):
        i = pl.program_id(0)

        @pl.when(i == 0)
        def _():
            red_ref[...] = jnp.zeros_like(red_ref)

        x1v = x1_ref[...]
        gate2 = mod_ref[5:6, :]
        h2 = h2_ref[...]
        f = fe_ref[...]
        for j in range(n_late):
            cols = slice(j * FF_BLK, (j + 1) * FF_BLK)
            ra = jnp.maximum(_dot(h2, w1_ref[j]), 0.0)
            r = (ra * ra).astype(BF16)
            r_ref[:, cols] = r
            f = f + _dot(r, w2_ref[j])
        rf = _rstd(f)
        fhat = f * rf
        nf = fhat * n2post_ref[...]
        err = (x1v + gate2 * nf) - tgt_ref[...]
        dy = err * (1.0 / D)
        dy_ref[...] = dy
        dnf = dy * gate2
        df_ref[...] = _rms_bwd(dnf * n2post_ref[...], fhat, rf).astype(BF16)
        red_ref[0:1, :] += _colsum(dy * nf)
        red_ref[1:2, :] += _colsum(dnf * fhat)
        red_ref[2:3, :] += _colsum(0.5 * jnp.mean(err * err, axis=-1, keepdims=True)) * jnp.ones((1, D), F32)

    tile = lambda w: pl.BlockSpec((tt, w), lambda i: (i, 0))
    return pl.pallas_call(
        body, name="mlp_fwd_late", grid=(nt,),
        out_shape=(jax.ShapeDtypeStruct((t_len, D_FF), BF16), jax.ShapeDtypeStruct((t_len, D), BF16),
                   jax.ShapeDtypeStruct((t_len, D), F32), jax.ShapeDtypeStruct((8, D), F32)),
        in_specs=[pl.BlockSpec(memory_space=pl.ANY), tile(D), tile(D), tile(D), tile(D), _full((8, D)),
                  _full((1, D)), _resident((n_late, D, FF_BLK)), _resident((n_late, FF_BLK, D))],
        out_specs=(pl.BlockSpec((tt, n_late * FF_BLK), lambda i: (i, FC_EARLY // n_late)), tile(D), tile(D),
                   _full((8, D))),
        input_output_aliases={0: 0},
        compiler_params=pltpu.CompilerParams(dimension_semantics=("arbitrary",), vmem_limit_bytes=VMEM_LIMIT),
    )(r_all, x1, h2, f_early, tgt, mod, n2post, w1_late, w2_late)


def _mlp_bwd(tt, df, r, dy, x1, mix, mod, n2pre, n1post, w1_early, w2_early, w1_late, w2_late):
    t_len = x1.shape[0]
    nt = t_len // tt
    n_late = N_DEV - FC_EARLY

    def body(df_ref, r_ref, dy_ref, x1_ref, mix_ref, mod_ref, n2pre_ref, n1post_ref,
             w1e_ref, w2e_ref, w1l_ref, w2l_ref, da_ref, dmix_ref, dx1_ref, red_ref, dh2_acc):
        i = pl.program_id(0)

        @pl.when(i == 0)
        def _():
            red_ref[...] = jnp.zeros_like(red_ref)

        dfv = df_ref[...]
        for j in range(N_DEV):
            cols = slice(j * FF_BLK, (j + 1) * FF_BLK)
            if j < FC_EARLY:
                w1, w2 = w1e_ref[j], w2e_ref[j]
            else:
                w1, w2 = w1l_ref[j - FC_EARLY], w2l_ref[j - FC_EARLY]
            dr = _dot_nt(dfv, w2)
            da = (dr * (2.0 * jnp.sqrt(r_ref[:, cols].astype(F32)))).astype(BF16)
            da_ref[:, cols] = da
            contrib = _dot_nt(da, w1)
            if j == 0:
                dh2_acc[...] = contrib
            else:
                dh2_acc[...] += contrib
        dh2 = dh2_acc[...]
        gate1, scale2 = mod_ref[2:3, :], mod_ref[4:5, :]
        x1v = x1_ref[...]
        r2 = _rstd(x1v)
        xhat = x1v * r2
        n2 = xhat * n2pre_ref[...]
        dn2 = dh2 * (1.0 + scale2)
        dx1 = dy_ref[...] + _rms_bwd(dn2 * n2pre_ref[...], xhat, r2)
        dx1_ref[...] = dx1
        mixv = mix_ref[...]
        rm = _rstd(mixv)
        mhat = mixv * rm
        dnm = dx1 * gate1
        dmix_ref[...] = _rms_bwd(dnm * n1post_ref[...], mhat, rm).astype(BF16)
        red_ref[0:1, :] += _colsum(dh2)
        red_ref[1:2, :] += _colsum(dh2 * n2)
        red_ref[2:3, :] += _colsum(dn2 * xhat)
        red_ref[3:4, :] += _colsum(dx1 * (mhat * n1post_ref[...]))
        red_ref[4:5, :] += _colsum(dnm * mhat)

    tile = lambda w: pl.BlockSpec((tt, w), lambda i: (i, 0))
    return pl.pallas_call(
        body, name="mlp_bwd", grid=(nt,),
        out_shape=(jax.ShapeDtypeStruct((t_len, D_FF), BF16),
                   jax.ShapeDtypeStruct((t_len, D), BF16), jax.ShapeDtypeStruct((t_len, D), F32),
                   jax.ShapeDtypeStruct((8, D), F32)),
        in_specs=[tile(D), tile(D_FF), tile(D), tile(D), tile(D),
                  _full((8, D)), _full((1, D)), _full((1, D)),
                  _resident((FC_EARLY, D, FF_BLK)), _resident((FC_EARLY, FF_BLK, D)),
                  _resident((n_late, D, FF_BLK)), _resident((n_late, FF_BLK, D))],
        out_specs=(tile(D_FF), tile(D), tile(D), _full((8, D))),
        scratch_shapes=[pltpu.VMEM((tt, D), F32)],
        compiler_params=pltpu.CompilerParams(dimension_semantics=("arbitrary",), vmem_limit_bytes=VMEM_LIMIT),
    )(df, r, dy, x1, mix, mod, n2pre, n1post, w1_early, w2_early, w1_late, w2_late)


def _mlp_wgrad(tt, r, da, df, h2):
    t_len = df.shape[0]
    nt = t_len // tt

    def relation(j):
        return jnp.where(j < 4, 2 * j + 1, (2 * j - 6) % N_DEV)

    def body(r_ref, da_ref, df_ref, h2_ref, own1_ref, own2_ref, out1_ref, out2_ref,
             acc1, acc2, snd1, snd2, sib1, sib2, send_sems, recv_sems):
        j, t = pl.program_id(0), pl.program_id(1)
        rows = pl.ds(pl.multiple_of(t * tt, tt), tt)
        x, y, c = _place()
        accs, snds, sibs = (acc1, acc2), (snd1, snd2), (sib1, sib2)

        def to_sibling(a, jj):
            return pltpu.make_async_remote_copy(
                src_ref=snds[a].at[jj % 2], dst_ref=sibs[a].at[jj],
                send_sem=send_sems.at[4 * a + jj], recv_sem=recv_sems.at[4 * a + jj],
                device_id=(x, y, 1 - c), device_id_type=MESH)

        @pl.when(t == 0)
        def _():
            acc2[...] = jnp.zeros_like(acc2)
            acc1[...] = jnp.zeros_like(acc1)

        acc2[...] += _dot_tn(r_ref[...], df_ref[rows, :])
        acc1[...] += _dot_tn(h2_ref[rows, :], da_ref[...])

        @pl.when((t == nt - 1) & (j < 4))
        def _():
            for a in range(2):
                @pl.when(j >= 2)
                def _():
                    to_sibling(a, j - 2).wait_send()

                snds[a][j % 2] = accs[a][...].astype(BF16)
                to_sibling(a, j).start()

        @pl.when((t == nt - 1) & (j >= 4))
        def _():
            jj = (j - 3) % 4
            for a, (own_ref, out_ref) in enumerate(((own1_ref, out1_ref), (own2_ref, out2_ref))):
                to_sibling(a, jj).wait_recv()

                @pl.when(j < N_DEV - 1)
                def _():
                    out_ref[0] = (accs[a][...] + sibs[a][jj].astype(F32)).astype(BF16)

                @pl.when(j == N_DEV - 1)
                def _():
                    own_ref[...] = accs[a][...] + sibs[a][jj].astype(F32)
                    to_sibling(a, 2).wait_send()
                    to_sibling(a, 3).wait_send()

    blk = pl.BlockSpec((tt, FF_BLK), lambda j, t: (t, relation(j)))
    chip = lambda j, t: (jnp.clip(j - 4, 0, 2), 0, 0)
    return pl.pallas_call(
        body, name="mlp_wgrad", grid=(N_DEV, nt),
        out_shape=(jax.ShapeDtypeStruct((D, FF_BLK), F32), jax.ShapeDtypeStruct((FF_BLK, D), F32),
                   jax.ShapeDtypeStruct((3, D, FF_BLK), BF16), jax.ShapeDtypeStruct((3, FF_BLK, D), BF16)),
        in_specs=[blk, blk, _resident((t_len, D)), _resident((t_len, D))],
        out_specs=(_full((D, FF_BLK)), _full((FF_BLK, D)),
                   pl.BlockSpec((1, D, FF_BLK), chip), pl.BlockSpec((1, FF_BLK, D), chip)),
        scratch_shapes=[pltpu.VMEM((D, FF_BLK), F32), pltpu.VMEM((FF_BLK, D), F32),
                        pltpu.VMEM((2, D, FF_BLK), BF16), pltpu.VMEM((2, FF_BLK, D), BF16),
                        pltpu.VMEM((4, D, FF_BLK), BF16), pltpu.VMEM((4, FF_BLK, D), BF16),
                        pltpu.SemaphoreType.DMA((8,)), pltpu.SemaphoreType.DMA((8,))],
        compiler_params=pltpu.CompilerParams(dimension_semantics=("arbitrary", "arbitrary"),
                                             vmem_limit_bytes=VMEM_LIMIT),
    )(r, da, df, h2)


def _acc_rows(ref, row0, k, val):
    half = CHUNK // 2
    ref[row0:row0 + half, k * GROUP:(k + 1) * GROUP] += val[:half, :]
    ref[row0:row0 + half, D_A + k * GROUP:D_A + (k + 1) * GROUP] += val[half:, :]


def _attn_bwd(tt, dmix, dx1, x, z, cat, mod, n1pre, w_in_t, w_out, w_sp, bs_rows, ln_g, ln_b, w_pool, b_pool,
              pool_scale, red_fwd, red_bwd, chip_sums):
    t_len = x.shape[0]
    nt = t_len // tt
    hb = tt // HALO
    n_sums = len(chip_sums)

    def body(dmix_ref, dx1_ref, x_ref, z_ref, zprev_ref, cat_ref, mod_ref, n1pre_ref, win_ref, wout_ref, wsp_ref,
             bs_ref, lng_ref, lnb_ref, wp_ref, bp_ref, ps_ref, redf_ref, redb_ref, *rest):
        sum_out = rest[:n_sums]
        gx_ref, gwin_ref, gwout_ref, small_ref = rest[n_sums:n_sums + 4]
        sum_in = rest[n_sums + 4:2 * n_sums + 4]
        carry, acc_in, acc_out, dz_scr, bs_acc, send_sems, recv_sems = rest[2 * n_sums + 4:]
        s = pl.program_id(0)
        i = nt - 1 - s
        px, py, pc = _place()

        def chip_copy(a, r):
            return pltpu.make_async_remote_copy(
                src_ref=sum_out[a].at[r], dst_ref=sum_in[a].at[r],
                send_sem=send_sems.at[3 * a + r], recv_sem=recv_sems.at[3 * a + r],
                device_id=_peer(px, py, pc, 2 * (r + 1)), device_id_type=MESH)

        @pl.when(s == 0)
        def _():
            for a in range(n_sums):
                for r in range(3):
                    chip_copy(a, r).start()
            carry[...] = jnp.zeros_like(carry)
            acc_in[...] = jnp.zeros_like(acc_in)
            acc_out[...] = jnp.zeros_like(acc_out)
            bs_acc[...] = jnp.zeros_like(bs_acc)
            small_ref[...] = jnp.zeros_like(small_ref)
            small_ref[ROW_DMOD + 2:ROW_DMOD + 3, :] = redb_ref[3:4, :]
            small_ref[ROW_DMOD + 3:ROW_DMOD + 5, :] = redb_ref[0:2, :]
            small_ref[ROW_DMOD + 5:ROW_DMOD + 6, :] = redf_ref[0:1, :]
            small_ref[ROW_N1POST:ROW_N1POST + 1, :] = redb_ref[4:5, :]
            small_ref[ROW_N2PRE:ROW_N2PRE + 1, :] = redb_ref[2:3, :]
            small_ref[ROW_N2POST:ROW_N2POST + 1, :] = redf_ref[1:2, :]
            small_ref[ROW_LOSS:ROW_LOSS + 1, :] = redf_ref[2:3, :]

        dmixv = dmix_ref[...]
        dcat = _dot_nt(dmixv, wout_ref[...])
        acc_out[...] += _dot_tn(cat_ref[...], dmixv)

        z = z_ref[...]
        t_g, ga = _gelu_parts(z[:, :2 * D_A])
        u, vr = ga[:, :D_A], ga[:, D_A:]
        dv0 = vr - jnp.mean(vr, axis=-1, keepdims=True)
        rv = lax.rsqrt(jnp.mean(dv0 * dv0, axis=-1, keepdims=True) + EPS)
        vhat = dv0 * rv
        vb = (vhat * lng_ref[...] + lnb_ref[...]).astype(BF16)
        mask = _tril_mask()
        wc = [(wsp_ref[h] * mask).astype(BF16) for h in range(N_HEADS)]

        dya = dcat[:, :D_A]
        for h in range(N_HEADS):
            cols = slice(h * GROUP, (h + 1) * GROUP)
            bs_sum = jnp.zeros((CHUNK, GROUP), F32)
            ws_sum = jnp.zeros((CHUNK, CHUNK), F32)
            for ch in range(tt // CHUNK):
                rows = slice(ch * CHUNK, (ch + 1) * CHUNK)
                v_ch = vb[rows, cols]
                mixed = _dot(wc[h], v_ch) + bs_ref[:, cols]
                dy_ch = dya[rows, cols]
                dz_scr[rows, cols] = dy_ch * mixed
                dmixed = dy_ch * u[rows, cols]
                dmb = dmixed.astype(BF16)
                dz_scr[rows, D_A + h * GROUP:D_A + (h + 1) * GROUP] = _dot_tn(wc[h], dmb)
                bs_sum = bs_sum + dmixed
                ws_sum = ws_sum + _dot_nt(dmb, v_ch)
            _acc_rows(bs_acc, 0, h, bs_sum)
            _acc_rows(small_ref, ROW_WS, h, ws_sum)

        dvl = dz_scr[:, D_A:2 * D_A]
        dvhat = dvl * lng_ref[...]
        dvr = rv * (dvhat - jnp.mean(dvhat, axis=-1, keepdims=True)
                    - vhat * jnp.mean(dvhat * vhat, axis=-1, keepdims=True))
        small_ref[ROW_LN:ROW_LN + 1, 0:D_A] += _colsum(dvl * vhat)
        small_ref[ROW_LN:ROW_LN + 1, D_A:D] += _colsum(dvl)
        dga = jnp.concatenate([dz_scr[:, :D_A], dvr], axis=1)
        dza = dga * _gelu_grad(z[:, :2 * D_A], t_g)

        zb = z[:, 2 * D_A:]
        halo_prev = jnp.where(i == 0, 0.0, zprev_ref[...])
        diff = _pool_diff(zb, halo_prev, i * tt)
        dyb = dcat[:, D_A:]
        inv = _inv_counts(i * tt, tt)
        scaled, ddiffs = [], []
        for g in range(len(WINDOWS)):
            cols = slice(g * GROUP, (g + 1) * GROUP)
            db = diff[g].astype(BF16)
            wpg = wp_ref[g].astype(BF16)
            pre = _dot(db, wpg) + bp_ref[:, cols]
            small_ref[ROW_POOL:ROW_POOL + 1, cols] += _colsum(dyb[:, cols] * pre)
            dpre = dyb[:, cols] * ps_ref[:, cols]
            small_ref[ROW_POOL:ROW_POOL + 1, D_B + g * GROUP:D_B + (g + 1) * GROUP] += _colsum(dpre)
            dpb = dpre.astype(BF16)
            _acc_rows(small_ref, ROW_WP, g, _dot_tn(db, dpb))
            ddiff = _dot_nt(dpb, wpg)
            ddiffs.append(ddiff)
            scaled.append(ddiff * inv[g])
        scaled_all = jnp.concatenate(scaled, axis=1)
        ext = jnp.concatenate([scaled_all, carry[...]], axis=0)
        n_ext = tt + HALO
        s2 = ext + pltpu.roll(ext, n_ext - 1, 0)
        t4 = s2[:, GROUP:]
        s4 = t4 + pltpu.roll(t4, n_ext - 2, 0)
        t8 = s4[:, GROUP:]
        s8 = t8 + pltpu.roll(t8, n_ext - 4, 0)
        t16 = s8[:, GROUP:]
        s16 = t16 + pltpu.roll(t16, n_ext - 8, 0)
        back = [s2[:, :GROUP], s4[:, :GROUP], s8[:, :GROUP], s16]
        carry[...] = scaled_all[:HALO, :]
        dzb = jnp.concatenate([back[g][:tt, :] - ddiffs[g] for g in range(len(WINDOWS))], axis=1)

        dzv = jnp.concatenate([dza, dzb], axis=1).astype(BF16)
        dh1 = _dot(dzv, win_ref[...])
        xv = x_ref[...]
        r1 = _rstd(xv)
        xhat = xv * r1
        shift1, scale1 = mod_ref[0:1, :], mod_ref[1:2, :]
        n1 = xhat * n1pre_ref[...]
        h1 = (n1 * (1.0 + scale1) + shift1).astype(BF16)
        acc_in[...] += _dot_tn(dzv, h1)
        dn1 = dh1 * (1.0 + scale1)
        gx_ref[...] = dx1_ref[...] + _rms_bwd(dn1 * n1pre_ref[...], xhat, r1)
        small_ref[ROW_DMOD:ROW_DMOD + 1, :] += _colsum(dh1)
        small_ref[ROW_DMOD + 1:ROW_DMOD + 2, :] += _colsum(dh1 * n1)
        small_ref[ROW_N1PRE:ROW_N1PRE + 1, :] += _colsum(dn1 * xhat)

        @pl.when(s == nt - 1)
        def _():
            gwin_ref[...] = acc_in[...].astype(BF16)
            gwout_ref[...] = acc_out[...].astype(BF16)
            bs = _unfold(bs_acc[...])
            for h in range(N_HEADS):
                small_ref[ROW_BS + h:ROW_BS + h + 1, 0:GROUP] = jnp.sum(
                    bs[:, h * GROUP:(h + 1) * GROUP].T, axis=0, keepdims=True)
            for a in range(n_sums):
                for r in range(3):
                    chip_copy(a, r).wait_recv()
                    chip_copy(a, r).wait_send()

    rev = lambda w: pl.BlockSpec((tt, w), lambda s: (nt - 1 - s, 0))
    zprev = pl.BlockSpec((HALO, D_B), lambda s: (jnp.maximum((nt - 1 - s) * hb - 1, 0), 2))
    hbm = pl.BlockSpec(memory_space=pl.ANY)
    outs = pl.pallas_call(
        body, name="attn_bwd", grid=(nt,),
        out_shape=tuple([jax.ShapeDtypeStruct((t_len, D), F32), jax.ShapeDtypeStruct((D_Z, D), BF16),
                         jax.ShapeDtypeStruct((D, D), BF16), jax.ShapeDtypeStruct((SMALL_ROWS, D), F32)]
                        + [jax.ShapeDtypeStruct(cs.shape, cs.dtype) for cs in chip_sums]),
        in_specs=[rev(D), rev(D), rev(D), rev(D_Z), zprev, rev(D), _full((8, D)), _full((1, D)),
                  _resident((D_Z, D)), _resident((D, D)), _full((N_HEADS, CHUNK, CHUNK)), _full((CHUNK, D_A)),
                  _full((1, D_A)), _full((1, D_A)), _full((len(WINDOWS), GROUP, GROUP)), _full((1, D_B)),
                  _full((1, D_B)), _full((8, D)), _full((8, D))] + [_resident(cs.shape) for cs in chip_sums],
        out_specs=tuple([rev(D), _full((D_Z, D)), _full((D, D)), _full((SMALL_ROWS, D))] + [hbm] * n_sums),
        scratch_shapes=[pltpu.VMEM((HALO, D_B), F32), pltpu.VMEM((D_Z, D), F32), pltpu.VMEM((D, D), F32),
                        pltpu.VMEM((tt, 2 * D_A), F32), pltpu.VMEM((CHUNK // 2, D), F32),
                        pltpu.SemaphoreType.DMA((3 * n_sums,)), pltpu.SemaphoreType.DMA((3 * n_sums,))],
        compiler_params=pltpu.CompilerParams(dimension_semantics=("arbitrary",), vmem_limit_bytes=VMEM_LIMIT),
    )(dmix, dx1, x, z, z, cat, mod, n1pre, w_in_t, w_out, w_sp, bs_rows, ln_g, ln_b, w_pool, b_pool, pool_scale,
      red_fwd, red_bwd, *chip_sums)
    return outs[:4], outs[4:]


def _adam(w, g, m, v):
    m2 = ADAM_B1 * m + (1.0 - ADAM_B1) * g
    v2 = ADAM_B2 * v + (1.0 - ADAM_B2) * (g * g)
    m_hat = m2 / (1.0 - ADAM_B1 ** ADAM_STEP)
    v_hat = v2 / (1.0 - ADAM_B2 ** ADAM_STEP)
    delta = -ADAM_LR * (m_hat / (jnp.sqrt(v_hat) + ADAM_EPS) + ADAM_WD * w)
    return delta, m2, v2


def _adamw_shard(name, rb, w, g, m, v):
    rows, cols = w.shape

    def body(w_ref, g_ref, m_ref, v_ref, d_ref, m2_ref, v2_ref):
        d_ref[...], m2_ref[...], v2_ref[...] = _adam(w_ref[...], g_ref[...], m_ref[...], v_ref[...])

    blk = pl.BlockSpec((rb, cols), lambda i: (i, 0))
    shp = jax.ShapeDtypeStruct((rows, cols), F32)
    return pl.pallas_call(
        body, name=name, grid=(rows // rb,), out_shape=(shp, shp, shp),
        in_specs=[blk] * 4, out_specs=(blk, blk, blk),
        compiler_params=pltpu.CompilerParams(dimension_semantics=("arbitrary",)),
    )(w, g, m, v)


def _adamw_chip_sums(name, rb, w, own, arrived, m, v):
    rows, cols = w.shape

    def body(w_ref, own_ref, arr_ref, m_ref, v_ref, g_ref, d_ref, m2_ref, v2_ref):
        g = own_ref[...]
        for r in range(3):
            g = g + arr_ref[r].astype(F32)
        g_ref[...] = g
        d_ref[...], m2_ref[...], v2_ref[...] = _adam(w_ref[...], g, m_ref[...], v_ref[...])

    blk = pl.BlockSpec((rb, cols), lambda i: (i, 0))
    shp = jax.ShapeDtypeStruct((rows, cols), F32)
    return pl.pallas_call(
        body, name=name, grid=(rows // rb,), out_shape=(shp, shp, shp, shp),
        in_specs=[blk, blk, pl.BlockSpec((3, rb, cols), lambda i: (0, i, 0)), blk, blk],
        out_specs=(blk, blk, blk, blk),
        compiler_params=pltpu.CompilerParams(dimension_semantics=("arbitrary",)),
    )(w, own, arrived, m, v)


def _adamw_ada(rb, w, sc, dmod_cols, m, v):
    rows, cols = w.shape

    def body(w_ref, sc_ref, dm_ref, m_ref, v_ref, g_ref, d_ref, m2_ref, v2_ref):
        g = _dot_tn(sc_ref[...].astype(BF16), dm_ref[...].astype(BF16))
        g_ref[...] = g
        d_ref[...], m2_ref[...], v2_ref[...] = _adam(w_ref[...], g, m_ref[...], v_ref[...])

    blk = pl.BlockSpec((rb, cols), lambda i: (i, 0))
    shp = jax.ShapeDtypeStruct((rows, cols), F32)
    return pl.pallas_call(
        body, name="adamw_ada", grid=(rows // rb,), out_shape=(shp, shp, shp, shp),
        in_specs=[blk, pl.BlockSpec((N_DEV, rb), lambda i: (0, i)), _full((N_DEV, cols)), blk, blk],
        out_specs=(blk, blk, blk, blk),
        compiler_params=pltpu.CompilerParams(dimension_semantics=("arbitrary",)),
    )(w, sc, dmod_cols, m, v)


def _unfold(acc_rows):
    return jnp.concatenate([acc_rows[:, :D_A], acc_rows[:, D_A:]], axis=0)


def _adamw_small(total, params):
    n = len(params)
    flat = [a for p in params for a in p]

    def body(*refs):
        s_ref = refs[0]
        p_refs = refs[1:1 + 3 * n]
        loss_ref = refs[1 + 3 * n]
        o_refs = refs[2 + 3 * n:]
        d_b_ada = s_ref[0:6, :]
        for b in range(1, N_DEV):
            d_b_ada = d_b_ada + s_ref[8 * b:8 * b + 6, :]
        tot = s_ref[PACK_SHIFT:PACK_ROWS, :]
        loss_ref[...] = jnp.broadcast_to(tot[ROW_LOSS:ROW_LOSS + 1, 0:GROUP], (8, GROUP))
        mask = _tril_mask()
        ws = _unfold(tot[ROW_WS:ROW_WS + 64, :])
        wp = _unfold(tot[ROW_WP:ROW_WP + 64, :])
        grads = [
            d_b_ada,
            tot[ROW_N1PRE:ROW_N1PRE + 1, :], tot[ROW_N1POST:ROW_N1POST + 1, :],
            tot[ROW_N2PRE:ROW_N2PRE + 1, :], tot[ROW_N2POST:ROW_N2POST + 1, :],
            tot[ROW_LN:ROW_LN + 1, :D_A], tot[ROW_LN:ROW_LN + 1, D_A:],
            tot[ROW_POOL:ROW_POOL + 1, :D_B], tot[ROW_POOL:ROW_POOL + 1, D_B:],
            tot[ROW_BS:ROW_BS + N_HEADS, 0:GROUP],
            jnp.stack([ws[:, h * GROUP:(h + 1) * GROUP] * mask for h in range(N_HEADS)]),
            jnp.stack([wp[:, g * GROUP:(g + 1) * GROUP] for g in range(len(WINDOWS))]),
        ]
        for k in range(n):
            w_ref, m_ref, v_ref = p_refs[3 * k:3 * k + 3]
            g = grads[k]
            o_refs[4 * k][...] = g
            o_refs[4 * k + 1][...], o_refs[4 * k + 2][...], o_refs[4 * k + 3][...] = _adam(
                w_ref[...], g, m_ref[...], v_ref[...])

    vm = pl.BlockSpec(memory_space=pltpu.VMEM)
    out_shape = [jax.ShapeDtypeStruct((8, GROUP), F32)]
    for w, _, _ in params:
        out_shape += [jax.ShapeDtypeStruct(w.shape, F32)] * 4
    return pl.pallas_call(
        body, name="adamw_small", out_shape=tuple(out_shape),
        in_specs=[vm] * (1 + 3 * n), out_specs=tuple([vm] * len(out_shape)),
    )(total, *flat)


TT_ATTN_FWD = 512
TT_MLP_FWD = 512
TT_MLP = 256
TT_WGRAD = 2048
TT_ATTN_BWD = 256


def kernel(x, c, w_ada, b_ada, norm1_pre, norm1_post, w_in, w_spatial, b_spatial, ln_v_gain, ln_v_bias, w_pool, b_pool, pool_scale, w_out, norm2_pre, norm2_post, w_fc1, w_fc2, loss_target, m_w_ada, m_b_ada, m_norm1_pre, m_norm1_post, m_w_in, m_w_spatial, m_b_spatial, m_ln_v_gain, m_ln_v_bias, m_w_pool, m_b_pool, m_pool_scale, m_w_out, m_norm2_pre, m_norm2_post, m_w_fc1, m_w_fc2, v_w_ada, v_b_ada, v_norm1_pre, v_norm1_post, v_w_in, v_w_spatial, v_b_spatial, v_ln_v_gain, v_ln_v_bias, v_w_pool, v_b_pool, v_pool_scale, v_w_out, v_norm2_pre, v_norm2_post, v_w_fc1, v_w_fc2):
    t_len = x.shape[1]
    me = 4 * lax.axis_index("x") + 2 * lax.axis_index("y") + lax.axis_index("c")
    ada_cols = w_ada.shape[1]
    tt = lambda want: min(want, t_len)

    x2 = x.reshape(t_len, D)
    tgt = loss_target.reshape(t_len, D)
    row = lambda a: a.reshape(1, -1)

    b_my = lax.dynamic_slice_in_dim(b_ada, me * ada_cols, ada_cols).reshape(1, ada_cols)
    modp, sc, (g_in, g_out), fc_shards = _fwd_comm(jnp.broadcast_to(c, (8, D)), w_ada, b_my,
                                                   [w_in.T, w_out], [w_fc1, w_fc2])
    mod = jnp.concatenate([modp.reshape(6, D), jnp.zeros((2, D), F32)], axis=0)
    w_in_t = g_in.reshape(D_Z, D)
    w_out_all = g_out.reshape(D, D)

    bs_rows = jnp.repeat(b_spatial.T, GROUP, axis=1)
    attn_consts = (w_spatial, bs_rows, row(ln_v_gain), row(ln_v_bias), w_pool, row(b_pool), row(pool_scale))

    (z, cat, mix, x1), (w1_early, w2_early) = _attn_fwd(
        tt(TT_ATTN_FWD), x2, mod, row(norm1_pre), row(norm1_post), w_in_t, w_out_all, *attn_consts, fc_shards)
    (r_early, h2, f_early), (w1_late, w2_late) = _mlp_fwd_early(
        tt(TT_MLP_FWD), x1, mod, row(norm2_pre), w1_early, w2_early)
    r, df, dy, red_fwd = _mlp_fwd_late(tt(TT_MLP_FWD), r_early, x1, h2, f_early, tgt, mod, row(norm2_post),
                                       w1_late, w2_late)
    da, dmix, dx1, red_bwd = _mlp_bwd(tt(TT_MLP), df, r, dy, x1, mix, mod, row(norm2_pre), row(norm1_post),
                                      w1_early, w2_early, w1_late, w2_late)
    own_w1, own_w2, sums_w1, sums_w2 = _mlp_wgrad(tt(TT_WGRAD), r, da, df, h2)
    (grad_x, p_in, p_out, small), (arr_w1, arr_w2) = _attn_bwd(
        tt(TT_ATTN_BWD), dmix, dx1, x2, z, cat, mod, row(norm1_pre), w_in_t, w_out_all, *attn_consts,
        red_fwd, red_bwd, [sums_w1, sums_w2])
    grad_in_t, grad_out, total = _tail_comm(
        [p_in.reshape(N_DEV, D_Z // N_DEV, D), p_out.reshape(N_DEV, D // N_DEV, D)], small, 64)

    grad_w1, d_w1, m_w1, v_w1 = _adamw_chip_sums("adamw_fc1", 256, w_fc1, own_w1, arr_w1, m_w_fc1, v_w_fc1)
    grad_w2, d_w2, m_w2, v_w2 = _adamw_chip_sums("adamw_fc2", 128, w_fc2, own_w2, arr_w2, m_w_fc2, v_w_fc2)
    d_out, m_out, v_out = _adamw_shard("adamw_out", 128, w_out, grad_out, m_w_out, v_w_out)
    d_in_t, m_in_t, v_in_t = _adamw_shard("adamw_in", D_Z // N_DEV, w_in.T, grad_in_t, m_w_in.T, v_w_in.T)
    dmod_all = total[0:TABLE_ROWS, :].reshape(N_DEV, 8, D)[:, :6, :].reshape(N_DEV, 6 * D)
    dmod_cols = lax.dynamic_slice_in_dim(dmod_all, me * ada_cols, ada_cols, axis=1)
    grad_ada, d_ada, m_ada, v_ada = _adamw_ada(256, w_ada, sc, dmod_cols, m_w_ada, v_w_ada)

    six = lambda a: a.reshape(6, D)
    small_params = [
        (six(b_ada), six(m_b_ada), six(v_b_ada)),
        (row(norm1_pre), row(m_norm1_pre), row(v_norm1_pre)),
        (row(norm1_post), row(m_norm1_post), row(v_norm1_post)),
        (row(norm2_pre), row(m_norm2_pre), row(v_norm2_pre)),
        (row(norm2_post), row(m_norm2_post), row(v_norm2_post)),
        (row(ln_v_gain), row(m_ln_v_gain), row(v_ln_v_gain)),
        (row(ln_v_bias), row(m_ln_v_bias), row(v_ln_v_bias)),
        (row(pool_scale), row(m_pool_scale), row(v_pool_scale)),
        (row(b_pool), row(m_b_pool), row(v_b_pool)),
        (b_spatial, m_b_spatial, v_b_spatial),
        (w_spatial, m_w_spatial, v_w_spatial),
        (w_pool, m_w_pool, v_w_pool),
    ]
    outs = _adamw_small(total, small_params)
    loss = outs[0][0, 0]
    names = ["b_ada", "norm1_pre", "norm1_post", "norm2_pre", "norm2_post", "ln_v_gain", "ln_v_bias", "pool_scale",
             "b_pool", "b_spatial", "w_spatial", "w_pool"]
    shapes = dict(b_ada=b_ada.shape, norm1_pre=norm1_pre.shape, norm1_post=norm1_post.shape,
                  norm2_pre=norm2_pre.shape, norm2_post=norm2_post.shape, ln_v_gain=ln_v_gain.shape,
                  ln_v_bias=ln_v_bias.shape, pool_scale=pool_scale.shape, b_pool=b_pool.shape,
                  b_spatial=b_spatial.shape, w_spatial=w_spatial.shape, w_pool=w_pool.shape)
    res = {}
    for k, nm in enumerate(names):
        res[nm] = tuple(o.reshape(shapes[nm]) for o in outs[1 + 4 * k:5 + 4 * k])
    res["w_ada"] = (grad_ada, d_ada, m_ada, v_ada)
    res["w_in"] = (grad_in_t.T, d_in_t.T, m_in_t.T, v_in_t.T)
    res["w_out"] = (grad_out, d_out, m_out, v_out)
    res["w_fc1"] = (grad_w1, d_w1, m_w1, v_w1)
    res["w_fc2"] = (grad_w2, d_w2, m_w2, v_w2)

    order = ["w_ada", "b_ada", "norm1_pre", "norm1_post", "w_in", "w_spatial", "b_spatial", "ln_v_gain", "ln_v_bias",
             "w_pool", "b_pool", "pool_scale", "w_out", "norm2_pre", "norm2_post", "w_fc1", "w_fc2"]
    return (loss, grad_x.reshape(x.shape),
            *[res[nm][0] for nm in order], *[res[nm][1] for nm in order],
            *[res[nm][2] for nm in order], *[res[nm][3] for nm in order])
```

```python
import functools

import jax
import jax.numpy as jnp
from jax import lax
from jax.experimental import pallas as pl
from jax.experimental.pallas import tpu as pltpu

F32 = jnp.float32
BF16 = jnp.bfloat16
MESH = pl.DeviceIdType.MESH

N_DEV = 8
D = 1024
D_A = 512
D_B = 512
D_Z = 2 * D_A + D_B
N_HEADS = 4
CHUNK = 128
WINDOWS = (2, 4, 8, 16)
GROUP = 128
D_FF = 4096
FF_BLK = D_FF // N_DEV
HALO = 16
EPS = 1e-6
VMEM_LIMIT = 60 * 1024 * 1024

ADAM_LR = 0.001
ADAM_B1 = 0.9
ADAM_B2 = 0.999
ADAM_EPS = 1e-08
ADAM_WD = 0.01
ADAM_STEP = 10

ROW_DMOD = 0
ROW_N1PRE, ROW_N1POST, ROW_N2PRE, ROW_N2POST = 8, 9, 10, 11
ROW_LN = 12
ROW_POOL = 13
ROW_LOSS = 14
ROW_BS = 16
ROW_WS = 24
ROW_WP = 88
SMALL_ROWS = 152
TABLE_ROWS = 8 * N_DEV
PACK_SHIFT = TABLE_ROWS - 8
PACK_ROWS = SMALL_ROWS + PACK_SHIFT
PACK_HALF = PACK_ROWS // 2


def _dot(a, b):
    return jnp.dot(a, b, preferred_element_type=F32)


def _dot_nt(a, b):
    return lax.dot_general(a, b, (((1,), (1,)), ((), ())), preferred_element_type=F32)


def _dot_tn(a, b):
    return lax.dot_general(a, b, (((0,), (0,)), ((), ())), preferred_element_type=F32)


def _rstd(v):
    return lax.rsqrt(jnp.mean(v * v, axis=-1, keepdims=True) + EPS)


def _rms_bwd(d_hat, hat, rstd):
    return rstd * (d_hat - hat * jnp.mean(d_hat * hat, axis=-1, keepdims=True))


_K0 = 0.7978845608028654
_K1 = 0.044715


def _gelu_parts(v):
    t = jnp.tanh(_K0 * (v + _K1 * (v * v * v)))
    return t, v * (0.5 * (1.0 + t))


def _gelu_grad(v, t):
    return 0.5 * (1.0 + t) + (0.5 * v) * (1.0 - t * t) * (_K0 * (1.0 + (3.0 * _K1) * (v * v)))


def _colsum(v):
    return jnp.sum(v, axis=0, keepdims=True)


def _full(shape):
    n = len(shape)
    return pl.BlockSpec(shape, lambda *_: (0,) * n)


def _resident(shape):
    n = len(shape)
    return pl.BlockSpec(shape, lambda *_: (0,) * n, pipeline_mode=pl.Buffered(1))


def _place():
    x, y, c = lax.axis_index("x"), lax.axis_index("y"), lax.axis_index("c")
    return x, y, c


def _flip(v, bit):
    return 1 - v if bit else v


def _peer(x, y, c, k):
    return (_flip(x, (k >> 2) & 1), _flip(y, (k >> 1) & 1), _flip(c, k & 1))


def _index(p):
    return 4 * p[0] + 2 * p[1] + p[2]


def _two_level_gather_begin(x, y, c, out_refs, send_sems, recv_sems):
    me = (x, y, c)
    sibling = (x, y, 1 - c)
    chips = [(1 - x, y), (x, 1 - y), (1 - x, 1 - y)]

    def copy(a, k, block, to):
        ref = out_refs[a].at[_index(block)]
        return pltpu.make_async_remote_copy(
            src_ref=ref, dst_ref=ref, send_sem=send_sems.at[7 * a + k], recv_sem=recv_sems.at[7 * a + k],
            device_id=to, device_id_type=MESH)

    first = []
    for a in range(len(out_refs)):
        first.append(copy(a, 0, me, sibling))
        first += [copy(a, 1 + j, me, (*chip, c)) for j, chip in enumerate(chips)]
    for cp in first:
        cp.start()
    return copy, first, me, sibling, chips


def _two_level_gather_finish(c, n, begun):
    copy, first, me, sibling, chips = begun
    passed = []
    for a in range(n):
        for j, chip in enumerate(chips):
            copy(a, 1 + j, (*chip, c), me).wait_recv()
            fwd = copy(a, 4 + j, (*chip, c), sibling)
            fwd.start()
            passed.append(fwd)
    for a in range(n):
        copy(a, 0, sibling, me).wait_recv()
        for j, chip in enumerate(chips):
            copy(a, 4 + j, (*chip, 1 - c), me).wait_recv()
    for cp in first + passed:
        cp.wait_send()


def _fwd_comm(c8, w_ada, b_my, gathered, kept):
    ncol = w_ada.shape[1]
    n_g, n_k = len(gathered), len(kept)

    def body(c_ref, w_ref, b_ref, *rest):
        g_in, k_in = rest[:n_g], rest[n_g:n_g + n_k]
        modp_ref, sc_ref = rest[n_g + n_k:n_g + n_k + 2]
        g_out = rest[n_g + n_k + 2:2 * n_g + n_k + 2]
        k_out = rest[2 * n_g + n_k + 2:2 * n_g + 2 * n_k + 2]
        cg, mg, part, send_sems, recv_sems, g_send, g_recv = rest[2 * n_g + 2 * n_k + 2:]
        x, y, c = _place()
        me = _index((x, y, c))
        for a in range(n_g):
            g_out[a][me] = g_in[a][...].astype(BF16)
        begun = _two_level_gather_begin(x, y, c, g_out, g_send, g_recv)
        for a in range(n_k):
            k_out[a][...] = k_in[a][...].astype(BF16)

        def c_copy(k):
            p = _peer(x, y, c, k)
            return pltpu.make_async_remote_copy(
                src_ref=c_ref, dst_ref=cg.at[me], send_sem=send_sems.at[k - 1], recv_sem=recv_sems.at[k - 1],
                device_id=p, device_id_type=MESH)

        def c_arrival(k):
            p = _peer(x, y, c, k)
            return pltpu.make_async_remote_copy(
                src_ref=c_ref, dst_ref=cg.at[_index(p)], send_sem=send_sems.at[k - 1], recv_sem=recv_sems.at[k - 1],
                device_id=p, device_id_type=MESH)

        def m_copy(k):
            p = _peer(x, y, c, k)
            return pltpu.make_async_remote_copy(
                src_ref=part, dst_ref=mg.at[me], send_sem=send_sems.at[6 + k], recv_sem=recv_sems.at[6 + k],
                device_id=p, device_id_type=MESH)

        def m_arrival(k):
            p = _peer(x, y, c, k)
            return pltpu.make_async_remote_copy(
                src_ref=part, dst_ref=mg.at[_index(p)], send_sem=send_sems.at[6 + k], recv_sem=recv_sems.at[6 + k],
                device_id=p, device_id_type=MESH)

        for k in range(1, N_DEV):
            c_copy(k).start()
        cg[me] = c_ref[...]
        for k in range(1, N_DEV):
            c_arrival(k).wait_recv()
        c_all = jnp.concatenate([cg[j, 0:1, :] for j in range(N_DEV)], axis=0)
        sc = c_all * jax.nn.sigmoid(c_all)
        sc_ref[...] = sc
        part[...] = _dot(sc.astype(BF16), w_ref[...].astype(BF16)) + b_ref[...]
        for k in range(1, N_DEV):
            m_copy(k).start()
        mg[me] = part[...]
        for k in range(1, N_DEV):
            m_arrival(k).wait_recv()
        for j in range(N_DEV):
            modp_ref[j:j + 1, :] = mg[j, pl.ds(me, 1), :]
        _two_level_gather_finish(c, n_g, begun)
        for k in range(1, N_DEV):
            c_copy(k).wait_send()
            m_copy(k).wait_send()

    vm = pl.BlockSpec(memory_space=pltpu.VMEM)
    outs = pl.pallas_call(
        body, name="fwd_comm",
        out_shape=tuple([jax.ShapeDtypeStruct((N_DEV, ncol), F32), jax.ShapeDtypeStruct((N_DEV, D), F32)]
                        + [jax.ShapeDtypeStruct((N_DEV,) + s.shape, BF16) for s in gathered]
                        + [jax.ShapeDtypeStruct(s.shape, BF16) for s in kept]),
        in_specs=[vm] * (3 + n_g + n_k), out_specs=tuple([vm] * (2 + n_g + n_k)),
        scratch_shapes=[
            pltpu.VMEM((N_DEV, 8, D), F32),
            pltpu.VMEM((N_DEV, N_DEV, ncol), F32),
            pltpu.VMEM((N_DEV, ncol), F32),
            pltpu.SemaphoreType.DMA((2 * (N_DEV - 1),)),
            pltpu.SemaphoreType.DMA((2 * (N_DEV - 1),)),
            pltpu.SemaphoreType.DMA((7 * n_g,)),
            pltpu.SemaphoreType.DMA((7 * n_g,)),
        ],
        compiler_params=pltpu.CompilerParams(vmem_limit_bytes=VMEM_LIMIT),
    )(c8, w_ada, b_my, *gathered, *kept)
    return outs[0], outs[1], outs[2:2 + n_g], outs[2 + n_g:]


FC_EARLY = 6


class _Copies:
    def __init__(self, entries, send_sems, recv_sems):
        self.place = _place()
        self.entries, self.send_sems, self.recv_sems = entries, send_sems, recv_sems

    def _copy(self, i, arrival=False):
        src, dst, rel = self.entries[i]
        return pltpu.make_async_remote_copy(
            src_ref=dst if arrival else src, dst_ref=dst, send_sem=self.send_sems.at[i],
            recv_sem=self.recv_sems.at[i], device_id=_peer(*self.place, rel), device_id_type=MESH)

    def start(self, *which):
        for i in which:
            self._copy(i).start()

    def wait_recv(self, *which):
        for i in which:
            self._copy(i, arrival=True).wait_recv()

    def wait_send(self, *which):
        for i in which:
            self._copy(i).wait_send()


TAIL_STEPS = 8


def _tail_comm(parts, small, row_chunk, fc):
    n, n_fc = len(parts), len(fc)

    def body(*refs):
        p_refs, small_ref = refs[:n], refs[n]
        fc_in = refs[n + 1:n + 1 + 5 * n_fc]
        outs = refs[n + 1 + 5 * n_fc:]
        g_refs, total_ref = outs[:n], outs[n]
        fc_out = outs[n + 1:n + 1 + 4 * n_fc]
        scr = outs[n + 1 + 4 * n_fc:]
        from_sib = scr[0:n]
        chip_out = scr[n:2 * n]
        chip_in = scr[2 * n:3 * n]
        pack, pack_sib, halves, total_scr = scr[3 * n:3 * n + 4]
        send_a, recv_a, send_b, recv_b, send_s, recv_s = scr[3 * n + 4:]
        step = pl.program_id(0)
        x, y, c = _place()
        me = _index((x, y, c))
        sibling = (x, y, 1 - c)
        my_chip = 2 * x + y
        others = [(1 - x, y), (x, 1 - y), (1 - x, 1 - y)]
        my_half = pl.ds(pl.multiple_of(PACK_HALF * c, 8), PACK_HALF)

        def pack_to_sibling():
            return pltpu.make_async_remote_copy(
                src_ref=pack, dst_ref=pack_sib, send_sem=send_s.at[0], recv_sem=recv_s.at[0],
                device_id=sibling, device_id_type=MESH)

        def half_to_chip(r):
            return pltpu.make_async_remote_copy(
                src_ref=halves.at[my_chip], dst_ref=halves.at[my_chip],
                send_sem=send_s.at[1 + r], recv_sem=recv_s.at[1 + r],
                device_id=(*others[r], c), device_id_type=MESH)

        def half_from_chip(r):
            k = 2 * others[r][0] + others[r][1]
            return pltpu.make_async_remote_copy(
                src_ref=halves.at[k], dst_ref=halves.at[k], send_sem=send_s.at[1 + r], recv_sem=recv_s.at[1 + r],
                device_id=(*others[r], c), device_id_type=MESH)

        def total_to_sibling():
            return pltpu.make_async_remote_copy(
                src_ref=total_scr.at[my_half], dst_ref=total_scr.at[my_half],
                send_sem=send_s.at[4], recv_sem=recv_s.at[4], device_id=sibling, device_id_type=MESH)

        def total_from_sibling():
            sib_half = pl.ds(pl.multiple_of(PACK_HALF * (1 - c), 8), PACK_HALF)
            return pltpu.make_async_remote_copy(
                src_ref=total_scr.at[sib_half], dst_ref=total_scr.at[sib_half],
                send_sem=send_s.at[4], recv_sem=recv_s.at[4], device_id=sibling, device_id_type=MESH)

        def to_sibling(a, k):
            return pltpu.make_async_remote_copy(
                src_ref=p_refs[a].at[2 * k + (1 - c)], dst_ref=from_sib[a].at[k],
                send_sem=send_a.at[a], recv_sem=recv_a.at[a], device_id=sibling, device_id_type=MESH)

        def all_from_sibling(a):
            return pltpu.make_async_remote_copy(
                src_ref=from_sib[a], dst_ref=from_sib[a], send_sem=send_a.at[a], recv_sem=recv_a.at[a],
                device_id=sibling, device_id_type=MESH)

        def to_chip(a, r):
            return pltpu.make_async_remote_copy(
                src_ref=chip_out[a].at[r], dst_ref=chip_in[a].at[r],
                send_sem=send_b.at[3 * a + r], recv_sem=recv_b.at[3 * a + r],
                device_id=(*others[r], c), device_id_type=MESH)

        @pl.when(step == 0)
        def _():
            pack[0:TABLE_ROWS, :] = jnp.zeros((TABLE_ROWS, D), F32)
            pack[pl.ds(pl.multiple_of(8 * me, 8), 8), :] = small_ref[0:8, :]
            pack[TABLE_ROWS:PACK_ROWS, :] = small_ref[8:SMALL_ROWS, :]
            pack_to_sibling().start()
            for a in range(n):
                for k in range(4):
                    to_sibling(a, k).start()

        @pl.when(step == 1)
        def _():
            pack_to_sibling().wait_recv()
            halves[my_chip] = pack[my_half, :] + pack_sib[my_half, :]
            for r in range(3):
                half_to_chip(r).start()
            for a in range(n):
                all_from_sibling(a).wait_recv()
                rows = p_refs[a].shape[1]
                for r in range(3):
                    k = 2 * others[r][0] + others[r][1]
                    for s in range(0, rows, row_chunk):
                        sl = pl.ds(s, row_chunk)
                        chip_out[a][r, sl, :] = (p_refs[a][2 * k + c, sl, :].astype(F32)
                                                 + from_sib[a][k, sl, :].astype(F32)).astype(BF16)
                    to_chip(a, r).start()
                for s in range(0, rows, row_chunk):
                    sl = pl.ds(s, row_chunk)
                    g_refs[a][sl, :] = (p_refs[a][2 * my_chip + c, sl, :].astype(F32)
                                        + from_sib[a][my_chip, sl, :].astype(F32))

        for k in range(n_fc):
            w_ref, own_ref, arr_ref, m_ref, v_ref = fc_in[5 * k:5 * k + 5]
            g = own_ref[...]
            for r in range(3):
                g = g + arr_ref[r].astype(F32)
            fc_out[4 * k][...] = g
            fc_out[4 * k + 1][...], fc_out[4 * k + 2][...], fc_out[4 * k + 3][...] = _adam(
                w_ref[...], g, m_ref[...], v_ref[...])

        @pl.when(step == TAIL_STEPS - 1)
        def _():
            for r in range(3):
                half_from_chip(r).wait_recv()
            total_scr[my_half, :] = ((halves[0] + halves[1]) + halves[2]) + halves[3]
            total_to_sibling().start()
            for a in range(n):
                rows = p_refs[a].shape[1]
                for r in range(3):
                    to_chip(a, r).wait_recv()
                    for s in range(0, rows, row_chunk):
                        sl = pl.ds(s, row_chunk)
                        g_refs[a][sl, :] = g_refs[a][sl, :] + chip_in[a][r, sl, :].astype(F32)
            total_from_sibling().wait_recv()
            total_ref[...] = total_scr[...]
            for a in range(n):
                all_from_sibling(a).wait_send()
                for r in range(3):
                    to_chip(a, r).wait_send()
            pack_to_sibling().wait_send()
            for r in range(3):
                half_to_chip(r).wait_send()
            total_to_sibling().wait_send()

    fc_specs_in, fc_specs_out, fc_shapes, fc_args = [], [], [], []
    for w, own, arrived, m, v in fc:
        rows, cols = w.shape
        blk = pl.BlockSpec((rows // TAIL_STEPS, cols), lambda i: (i, 0))
        fc_specs_in += [blk, blk, pl.BlockSpec((3, rows // TAIL_STEPS, cols), lambda i: (0, i, 0)), blk, blk]
        fc_specs_out += [blk] * 4
        fc_shapes += [jax.ShapeDtypeStruct((rows, cols), F32)] * 4
        fc_args += [w, own, arrived, m, v]
    outs = pl.pallas_call(
        body, name="tail_comm", grid=(TAIL_STEPS,),
        out_shape=tuple([jax.ShapeDtypeStruct(p.shape[1:], F32) for p in parts]
                        + [jax.ShapeDtypeStruct((PACK_ROWS, D), F32)] + fc_shapes),
        in_specs=[_resident(p.shape) for p in parts] + [_resident(small.shape)] + fc_specs_in,
        out_specs=tuple([_full(p.shape[1:]) for p in parts] + [_full((PACK_ROWS, D))] + fc_specs_out),
        scratch_shapes=(
            [pltpu.VMEM((4,) + p.shape[1:], BF16) for p in parts]
            + [pltpu.VMEM((3,) + p.shape[1:], BF16) for p in parts]
            + [pltpu.VMEM((3,) + p.shape[1:], BF16) for p in parts]
            + [pltpu.VMEM((PACK_ROWS, D), F32), pltpu.VMEM((PACK_ROWS, D), F32),
               pltpu.VMEM((4, PACK_HALF, D), F32), pltpu.VMEM((PACK_ROWS, D), F32)]
            + [pltpu.SemaphoreType.DMA((n,)), pltpu.SemaphoreType.DMA((n,)),
               pltpu.SemaphoreType.DMA((3 * n,)), pltpu.SemaphoreType.DMA((3 * n,)),
               pltpu.SemaphoreType.DMA((5,)), pltpu.SemaphoreType.DMA((5,))]),
        compiler_params=pltpu.CompilerParams(dimension_semantics=("arbitrary",), vmem_limit_bytes=VMEM_LIMIT),
    )(*parts, small, *fc_args)
    return outs[:n + 1], [outs[n + 1 + 4 * k:n + 5 + 4 * k] for k in range(n_fc)]


def _tril_mask():
    row = lax.broadcasted_iota(jnp.int32, (CHUNK, CHUNK), 0)
    col = lax.broadcasted_iota(jnp.int32, (CHUNK, CHUNK), 1)
    return (col <= row).astype(F32)


def _window_sums(ext):
    s2 = ext + pltpu.roll(ext, 1, 0)
    t4 = s2[:, GROUP:]
    s4 = t4 + pltpu.roll(t4, 2, 0)
    t8 = s4[:, GROUP:]
    s8 = t8 + pltpu.roll(t8, 4, 0)
    t16 = s8[:, GROUP:]
    s16 = t16 + pltpu.roll(t16, 8, 0)
    return [s2[:, :GROUP], s4[:, :GROUP], s8[:, :GROUP], s16]


def _inv_counts(first_pos, rows):
    pos = first_pos + lax.broadcasted_iota(jnp.int32, (rows, 1), 0)
    return [1.0 / jnp.minimum(pos + 1, w).astype(F32) for w in WINDOWS]


def _pool_diff(zb, halo, first_pos):
    tt = zb.shape[0]
    sums = _window_sums(jnp.concatenate([halo, zb], axis=0))
    inv = _inv_counts(first_pos, tt)
    return [sums[g][HALO:, :] * inv[g] - zb[:, g * GROUP:(g + 1) * GROUP] for g in range(len(WINDOWS))]


def _attn_fwd(tt, x, mod, n1pre, n1post, w_in_t, w_out, w_sp, bs_rows, ln_g, ln_b, w_pool, b_pool, pool_scale,
              fc_shards):
    t_len = x.shape[0]
    nt = t_len // tt

    def body(x_ref, mod_ref, n1pre_ref, n1post_ref, win_ref, wout_ref, wsp_ref, bs_ref, lng_ref, lnb_ref,
             wp_ref, bp_ref, ps_ref, w1_ref, w2_ref, z_ref, cat_ref, mix_ref, x1_ref, e1_ref, e2_ref,
             carry, land1, land2, send_sems, recv_sems, local_sems):
        i = pl.program_id(0)
        copies = _Copies(
            [(w1_ref, e1_ref.at[1], 1), (w2_ref, e2_ref.at[1], 1),
             (w1_ref, land1.at[0], 2), (w2_ref, land2.at[0], 2),
             (w1_ref, land1.at[1], 4), (w2_ref, land2.at[1], 4),
             (land1.at[0], e1_ref.at[3], 1), (land2.at[0], e2_ref.at[3], 1),
             (land1.at[1], e1_ref.at[5], 1), (land2.at[1], e2_ref.at[5], 1)],
            send_sems, recv_sems)
        keep = [pltpu.make_async_copy(w1_ref, e1_ref.at[0], local_sems.at[0]),
                pltpu.make_async_copy(w2_ref, e2_ref.at[0], local_sems.at[1]),
                pltpu.make_async_copy(land1.at[0], e1_ref.at[2], local_sems.at[2]),
                pltpu.make_async_copy(land1.at[1], e1_ref.at[4], local_sems.at[3]),
                pltpu.make_async_copy(land2.at[0], e2_ref.at[2], local_sems.at[4]),
                pltpu.make_async_copy(land2.at[1], e2_ref.at[4], local_sems.at[5])]

        @pl.when(i == 0)
        def _():
            copies.start(2, 4, 3, 5, 0, 1)
            keep[0].start()
            keep[1].start()
            carry[...] = jnp.zeros_like(carry)

        @pl.when(i == nt // 2)
        def _():
            copies.wait_recv(2, 4)
            copies.start(6, 8)
            keep[2].start()
            keep[3].start()

        @pl.when(i == nt - 1)
        def _():
            copies.wait_recv(3, 5)
            copies.start(7, 9)
            keep[4].start()
            keep[5].start()

        xv = x_ref[...]
        shift1, scale1, gate1 = mod_ref[0:1, :], mod_ref[1:2, :], mod_ref[2:3, :]
        h1 = (xv * _rstd(xv) * n1pre_ref[...]) * (1.0 + scale1) + shift1
        z = _dot_nt(h1.astype(BF16), win_ref[...])
        z_ref[...] = z

        _, ga = _gelu_parts(z[:, :2 * D_A])
        u, vr = ga[:, :D_A], ga[:, D_A:]
        dv = vr - jnp.mean(vr, axis=-1, keepdims=True)
        v = (dv * lax.rsqrt(jnp.mean(dv * dv, axis=-1, keepdims=True) + EPS)) * lng_ref[...] + lnb_ref[...]
        vb = v.astype(BF16)
        mask = _tril_mask()
        wc = [(wsp_ref[h] * mask).astype(BF16) for h in range(N_HEADS)]
        for ch in range(tt // CHUNK):
            rows = slice(ch * CHUNK, (ch + 1) * CHUNK)
            for h in range(N_HEADS):
                cols = slice(h * GROUP, (h + 1) * GROUP)
                mixed = _dot(wc[h], vb[rows, cols]) + bs_ref[:, cols]
                cat_ref[rows, cols] = (u[rows, cols] * mixed).astype(BF16)

        zb = z[:, 2 * D_A:]
        diff = _pool_diff(zb, carry[...], i * tt)
        carry[...] = zb[tt - HALO:, :]
        for g in range(len(WINDOWS)):
            cols = slice(g * GROUP, (g + 1) * GROUP)
            pre = _dot(diff[g].astype(BF16), wp_ref[g].astype(BF16)) + bp_ref[:, cols]
            cat_ref[:, D_A + g * GROUP:D_A + (g + 1) * GROUP] = (pre * ps_ref[:, cols]).astype(BF16)

        mix = _dot(cat_ref[...], wout_ref[...])
        mix_ref[...] = mix
        x1_ref[...] = xv + gate1 * (mix * _rstd(mix) * n1post_ref[...])

        @pl.when(i == nt - 1)
        def _():
            copies.wait_recv(0, 1, 6, 7, 8, 9)
            copies.wait_send(*range(10))
            for cp in keep:
                cp.wait()

    tile = lambda w: pl.BlockSpec((tt, w), lambda i: (i, 0))
    hbm = pl.BlockSpec(memory_space=pl.ANY)
    outs = pl.pallas_call(
        body, name="attn_fwd", grid=(nt,),
        out_shape=tuple([jax.ShapeDtypeStruct((t_len, D_Z), F32), jax.ShapeDtypeStruct((t_len, D), BF16),
                         jax.ShapeDtypeStruct((t_len, D), F32), jax.ShapeDtypeStruct((t_len, D), F32)]
                        + [jax.ShapeDtypeStruct((FC_EARLY,) + s.shape, BF16) for s in fc_shards]),
        in_specs=[tile(D), _full((8, D)), _full((1, D)), _full((1, D)), _resident((D_Z, D)), _resident((D, D)),
                  _full((N_HEADS, CHUNK, CHUNK)), _full((CHUNK, D_A)), _full((1, D_A)), _full((1, D_A)),
                  _full((len(WINDOWS), GROUP, GROUP)), _full((1, D_B)), _full((1, D_B)),
                  _resident(fc_shards[0].shape), _resident(fc_shards[1].shape)],
        out_specs=(tile(D_Z), tile(D), tile(D), tile(D), hbm, hbm),
        scratch_shapes=[pltpu.VMEM((HALO, D_B), F32),
                        pltpu.VMEM((2,) + fc_shards[0].shape, BF16), pltpu.VMEM((2,) + fc_shards[1].shape, BF16),
                        pltpu.SemaphoreType.DMA((10,)), pltpu.SemaphoreType.DMA((10,)),
                        pltpu.SemaphoreType.DMA((6,))],
        compiler_params=pltpu.CompilerParams(dimension_semantics=("arbitrary",), vmem_limit_bytes=VMEM_LIMIT),
    )(x, mod, n1pre, n1post, w_in_t, w_out, w_sp, bs_rows, ln_g, ln_b, w_pool, b_pool, pool_scale, *fc_shards)
    return outs[:4], outs[4:]


def _mlp_fwd_early(tt, x1, mod, n2pre, w1_early, w2_early):
    t_len = x1.shape[0]
    nt = t_len // tt
    n_late = N_DEV - FC_EARLY

    def body(x1_ref, mod_ref, n2pre_ref, w1_ref, w2_ref, r_ref, h2_ref, f_ref, l1_ref, l2_ref,
             land1, land2, send_sems, recv_sems, local_sems):
        i = pl.program_id(0)
        copies = _Copies(
            [(w1_ref.at[2], land1, 4), (w2_ref.at[4], land2, 2),
             (land1, l1_ref.at[1], 1), (land2, l2_ref.at[1], 1)],
            send_sems, recv_sems)
        keep = [pltpu.make_async_copy(land1, l1_ref.at[0], local_sems.at[0]),
                pltpu.make_async_copy(land2, l2_ref.at[0], local_sems.at[1])]

        @pl.when(i == 0)
        def _():
            copies.start(0, 1)

        @pl.when(i == nt // 2)
        def _():
            copies.wait_recv(0, 1)
            copies.start(2, 3)
            for cp in keep:
                cp.start()

        x1v = x1_ref[...]
        shift2, scale2 = mod_ref[3:4, :], mod_ref[4:5, :]
        h2 = ((x1v * _rstd(x1v) * n2pre_ref[...]) * (1.0 + scale2) + shift2).astype(BF16)
        h2_ref[...] = h2
        for j in range(FC_EARLY):
            cols = slice(j * FF_BLK, (j + 1) * FF_BLK)
            ra = jnp.maximum(_dot(h2, w1_ref[j]), 0.0)
            r = (ra * ra).astype(BF16)
            r_ref[:, cols] = r
            contrib = _dot(r, w2_ref[j])
            if j == 0:
                f_ref[...] = contrib
            else:
                f_ref[...] += contrib

        @pl.when(i == nt - 1)
        def _():
            copies.wait_recv(2, 3)
            copies.wait_send(0, 1, 2, 3)
            for cp in keep:
                cp.wait()

    tile = lambda w: pl.BlockSpec((tt, w), lambda i: (i, 0))
    hbm = pl.BlockSpec(memory_space=pl.ANY)
    outs = pl.pallas_call(
        body, name="mlp_fwd_early", grid=(nt,),
        out_shape=(jax.ShapeDtypeStruct((t_len, D_FF), BF16),
                   jax.ShapeDtypeStruct((t_len, D), BF16), jax.ShapeDtypeStruct((t_len, D), F32),
                   jax.ShapeDtypeStruct((n_late,) + w1_early.shape[1:], BF16),
                   jax.ShapeDtypeStruct((n_late,) + w2_early.shape[1:], BF16)),
        in_specs=[tile(D), _full((8, D)), _full((1, D)),
                  _resident((FC_EARLY, D, FF_BLK)), _resident((FC_EARLY, FF_BLK, D))],
        out_specs=(tile(FC_EARLY * FF_BLK), tile(D), tile(D), hbm, hbm),
        scratch_shapes=[pltpu.VMEM(w1_early.shape[1:], BF16), pltpu.VMEM(w2_early.shape[1:], BF16),
                        pltpu.SemaphoreType.DMA((4,)), pltpu.SemaphoreType.DMA((4,)),
                        pltpu.SemaphoreType.DMA((2,))],
        compiler_params=pltpu.CompilerParams(dimension_semantics=("arbitrary",), vmem_limit_bytes=VMEM_LIMIT),
    )(x1, mod, n2pre, w1_early, w2_early)
    return outs[:3], outs[3:]


def _mlp_fwd_late(tt, r_all, x1, h2, f_early, tgt, mod, n2post, w1_late, w2_late):
    t_len = x1.shape[0]
    nt = t_len // tt
    n_late = N_DEV - FC_EARLY

    def body(r_all_ref, x1_ref, h2_ref, fe_ref, tgt_ref, mod_ref, n2post_ref, w1_ref, w2_ref,
             r_ref, df_ref, dy_ref, red_ref):
        i = pl.program_id(0)

        @pl.when(i == 0)
        def _():
            red_ref[...] = jnp.zeros_like(red_ref)

        x1v = x1_ref[...]
        gate2 = mod_ref[5:6, :]
        h2 = h2_ref[...]
        f = fe_ref[...]
        for j in range(n_late):
            cols = slice(j * FF_BLK, (j + 1) * FF_BLK)
            ra = jnp.maximum(_dot(h2, w1_ref[j]), 0.0)
            r = (ra * ra).astype(BF16)
            r_ref[:, cols] = r
            f = f + _dot(r, w2_ref[j])
        rf = _rstd(f)
        fhat = f * rf
        nf = fhat * n2post_ref[...]
        err = (x1v + gate2 * nf) - tgt_ref[...]
        dy = err * (1.0 / D)
        dy_ref[...] = dy
        dnf = dy * gate2
        df_ref[...] = _rms_bwd(dnf * n2post_ref[...], fhat, rf).astype(BF16)
        red_ref[0:1, :] += _colsum(dy * nf)
        red_ref[1:2, :] += _colsum(dnf * fhat)
        red_ref[2:3, :] += _colsum(0.5 * jnp.mean(err * err, axis=-1, keepdims=True)) * jnp.ones((1, D), F32)

    tile = lambda w: pl.BlockSpec((tt, w), lambda i: (i, 0))
    return pl.pallas_call(
        body, name="mlp_fwd_late", grid=(nt,),
        out_shape=(jax.ShapeDtypeStruct((t_len, D_FF), BF16), jax.ShapeDtypeStruct((t_len, D), BF16),
                   jax.ShapeDtypeStruct((t_len, D), F32), jax.ShapeDtypeStruct((8, D), F32)),
        in_specs=[pl.BlockSpec(memory_space=pl.ANY), tile(D), tile(D), tile(D), tile(D), _full((8, D)),
                  _full((1, D)), _resident((n_late, D, FF_BLK)), _resident((n_late, FF_BLK, D))],
        out_specs=(pl.BlockSpec((tt, n_late * FF_BLK), lambda i: (i, FC_EARLY // n_late)), tile(D), tile(D),
                   _full((8, D))),
        input_output_aliases={0: 0},
        compiler_params=pltpu.CompilerParams(dimension_semantics=("arbitrary",), vmem_limit_bytes=VMEM_LIMIT),
    )(r_all, x1, h2, f_early, tgt, mod, n2post, w1_late, w2_late)


def _mlp_bwd(tt, df, r, dy, x1, mix, mod, n2pre, n1post, w1_early, w2_early, w1_late, w2_late):
    t_len = x1.shape[0]
    nt = t_len // tt
    n_late = N_DEV - FC_EARLY

    def body(df_ref, r_ref, dy_ref, x1_ref, mix_ref, mod_ref, n2pre_ref, n1post_ref,
             w1e_ref, w2e_ref, w1l_ref, w2l_ref, da_ref, dmix_ref, dx1_ref, red_ref, dh2_acc):
        i = pl.program_id(0)

        @pl.when(i == 0)
        def _():
            red_ref[...] = jnp.zeros_like(red_ref)

        dfv = df_ref[...]
        for j in range(N_DEV):
            cols = slice(j * FF_BLK, (j + 1) * FF_BLK)
            if j < FC_EARLY:
                w1, w2 = w1e_ref[j], w2e_ref[j]
            else:
                w1, w2 = w1l_ref[j - FC_EARLY], w2l_ref[j - FC_EARLY]
            dr = _dot_nt(dfv, w2)
            da = (dr * (2.0 * jnp.sqrt(r_ref[:, cols].astype(F32)))).astype(BF16)
            da_ref[:, cols] = da
            contrib = _dot_nt(da, w1)
            if j == 0:
                dh2_acc[...] = contrib
            else:
                dh2_acc[...] += contrib
        dh2 = dh2_acc[...]
        gate1, scale2 = mod_ref[2:3, :], mod_ref[4:5, :]
        x1v = x1_ref[...]
        r2 = _rstd(x1v)
        xhat = x1v * r2
        n2 = xhat * n2pre_ref[...]
        dn2 = dh2 * (1.0 + scale2)
        dx1 = dy_ref[...] + _rms_bwd(dn2 * n2pre_ref[...], xhat, r2)
        dx1_ref[...] = dx1
        mixv = mix_ref[...]
        rm = _rstd(mixv)
        mhat = mixv * rm
        dnm = dx1 * gate1
        dmix_ref[...] = _rms_bwd(dnm * n1post_ref[...], mhat, rm).astype(BF16)
        red_ref[0:1, :] += _colsum(dh2)
        red_ref[1:2, :] += _colsum(dh2 * n2)
        red_ref[2:3, :] += _colsum(dn2 * xhat)
        red_ref[3:4, :] += _colsum(dx1 * (mhat * n1post_ref[...]))
        red_ref[4:5, :] += _colsum(dnm * mhat)

    tile = lambda w: pl.BlockSpec((tt, w), lambda i: (i, 0))
    return pl.pallas_call(
        body, name="mlp_bwd", grid=(nt,),
        out_shape=(jax.ShapeDtypeStruct((t_len, D_FF), BF16),
                   jax.ShapeDtypeStruct((t_len, D), BF16), jax.ShapeDtypeStruct((t_len, D), F32),
                   jax.ShapeDtypeStruct((8, D), F32)),
        in_specs=[tile(D), tile(D_FF), tile(D), tile(D), tile(D),
                  _full((8, D)), _full((1, D)), _full((1, D)),
                  _resident((FC_EARLY, D, FF_BLK)), _resident((FC_EARLY, FF_BLK, D)),
                  _resident((n_late, D, FF_BLK)), _resident((n_late, FF_BLK, D))],
        out_specs=(tile(D_FF), tile(D), tile(D), _full((8, D))),
        scratch_shapes=[pltpu.VMEM((tt, D), F32)],
        compiler_params=pltpu.CompilerParams(dimension_semantics=("arbitrary",), vmem_limit_bytes=VMEM_LIMIT),
    )(df, r, dy, x1, mix, mod, n2pre, n1post, w1_early, w2_early, w1_late, w2_late)


def _mlp_wgrad(tt, r, da, df, h2):
    t_len = df.shape[0]
    nt = t_len // tt

    def relation(j):
        return jnp.where(j < 4, 2 * j + 1, (2 * j - 6) % N_DEV)

    def body(r_ref, da_ref, df_ref, h2_ref, own1_ref, own2_ref, out1_ref, out2_ref,
             acc1, acc2, snd1, snd2, sib1, sib2, send_sems, recv_sems):
        j, t = pl.program_id(0), pl.program_id(1)
        rows = pl.ds(pl.multiple_of(t * tt, tt), tt)
        x, y, c = _place()
        accs, snds, sibs = (acc1, acc2), (snd1, snd2), (sib1, sib2)

        def to_sibling(a, jj):
            return pltpu.make_async_remote_copy(
                src_ref=snds[a].at[jj % 2], dst_ref=sibs[a].at[jj],
                send_sem=send_sems.at[4 * a + jj], recv_sem=recv_sems.at[4 * a + jj],
                device_id=(x, y, 1 - c), device_id_type=MESH)

        @pl.when(t == 0)
        def _():
            acc2[...] = jnp.zeros_like(acc2)
            acc1[...] = jnp.zeros_like(acc1)

        acc2[...] += _dot_tn(r_ref[...], df_ref[rows, :])
        acc1[...] += _dot_tn(h2_ref[rows, :], da_ref[...])

        @pl.when((t == nt - 1) & (j < 4))
        def _():
            for a in range(2):
                @pl.when(j >= 2)
                def _():
                    to_sibling(a, j - 2).wait_send()

                snds[a][j % 2] = accs[a][...].astype(BF16)
                to_sibling(a, j).start()

        @pl.when((t == nt - 1) & (j >= 4))
        def _():
            jj = (j - 3) % 4
            for a, (own_ref, out_ref) in enumerate(((own1_ref, out1_ref), (own2_ref, out2_ref))):
                to_sibling(a, jj).wait_recv()

                @pl.when(j < N_DEV - 1)
                def _():
                    out_ref[0] = (accs[a][...] + sibs[a][jj].astype(F32)).astype(BF16)

                @pl.when(j == N_DEV - 1)
                def _():
                    own_ref[...] = accs[a][...] + sibs[a][jj].astype(F32)
                    to_sibling(a, 2).wait_send()
                    to_sibling(a, 3).wait_send()

    blk = pl.BlockSpec((tt, FF_BLK), lambda j, t: (t, relation(j)))
    chip = lambda j, t: (jnp.clip(j - 4, 0, 2), 0, 0)
    return pl.pallas_call(
        body, name="mlp_wgrad", grid=(N_DEV, nt),
        out_shape=(jax.ShapeDtypeStruct((D, FF_BLK), F32), jax.ShapeDtypeStruct((FF_BLK, D), F32),
                   jax.ShapeDtypeStruct((3, D, FF_BLK), BF16), jax.ShapeDtypeStruct((3, FF_BLK, D), BF16)),
        in_specs=[blk, blk, _resident((t_len, D)), _resident((t_len, D))],
        out_specs=(_full((D, FF_BLK)), _full((FF_BLK, D)),
                   pl.BlockSpec((1, D, FF_BLK), chip), pl.BlockSpec((1, FF_BLK, D), chip)),
        scratch_shapes=[pltpu.VMEM((D, FF_BLK), F32), pltpu.VMEM((FF_BLK, D), F32),
                        pltpu.VMEM((2, D, FF_BLK), BF16), pltpu.VMEM((2, FF_BLK, D), BF16),
                        pltpu.VMEM((4, D, FF_BLK), BF16), pltpu.VMEM((4, FF_BLK, D), BF16),
                        pltpu.SemaphoreType.DMA((8,)), pltpu.SemaphoreType.DMA((8,))],
        compiler_params=pltpu.CompilerParams(dimension_semantics=("arbitrary", "arbitrary"),
                                             vmem_limit_bytes=VMEM_LIMIT),
    )(r, da, df, h2)


def _acc_rows(ref, row0, k, val):
    half = CHUNK // 2
    ref[row0:row0 + half, k * GROUP:(k + 1) * GROUP] += val[:half, :]
    ref[row0:row0 + half, D_A + k * GROUP:D_A + (k + 1) * GROUP] += val[half:, :]


def _attn_bwd(tt, dmix, dx1, x, z, cat, mod, n1pre, w_in_t, w_out, w_sp, bs_rows, ln_g, ln_b, w_pool, b_pool,
              pool_scale, red_fwd, red_bwd, chip_sums):
    t_len = x.shape[0]
    nt = t_len // tt
    hb = tt // HALO
    n_sums = len(chip_sums)

    def body(dmix_ref, dx1_ref, x_ref, z_ref, zprev_ref, cat_ref, mod_ref, n1pre_ref, win_ref, wout_ref, wsp_ref,
             bs_ref, lng_ref, lnb_ref, wp_ref, bp_ref, ps_ref, redf_ref, redb_ref, *rest):
        sum_out = rest[:n_sums]
        gx_ref, gwin_ref, gwout_ref, small_ref = rest[n_sums:n_sums + 4]
        sum_in = rest[n_sums + 4:2 * n_sums + 4]
        carry, acc_in, acc_out, dz_scr, bs_acc, send_sems, recv_sems = rest[2 * n_sums + 4:]
        s = pl.program_id(0)
        i = nt - 1 - s
        px, py, pc = _place()

        def chip_copy(a, r):
            return pltpu.make_async_remote_copy(
                src_ref=sum_out[a].at[r], dst_ref=sum_in[a].at[r],
                send_sem=send_sems.at[3 * a + r], recv_sem=recv_sems.at[3 * a + r],
                device_id=_peer(px, py, pc, 2 * (r + 1)), device_id_type=MESH)

        @pl.when(s == 0)
        def _():
            for a in range(n_sums):
                for r in range(3):
                    chip_copy(a, r).start()
            carry[...] = jnp.zeros_like(carry)
            acc_in[...] = jnp.zeros_like(acc_in)
            acc_out[...] = jnp.zeros_like(acc_out)
            bs_acc[...] = jnp.zeros_like(bs_acc)
            small_ref[...] = jnp.zeros_like(small_ref)
            small_ref[ROW_DMOD + 2:ROW_DMOD + 3, :] = redb_ref[3:4, :]
            small_ref[ROW_DMOD + 3:ROW_DMOD + 5, :] = redb_ref[0:2, :]
            small_ref[ROW_DMOD + 5:ROW_DMOD + 6, :] = redf_ref[0:1, :]
            small_ref[ROW_N1POST:ROW_N1POST + 1, :] = redb_ref[4:5, :]
            small_ref[ROW_N2PRE:ROW_N2PRE + 1, :] = redb_ref[2:3, :]
            small_ref[ROW_N2POST:ROW_N2POST + 1, :] = redf_ref[1:2, :]
            small_ref[ROW_LOSS:ROW_LOSS + 1, :] = redf_ref[2:3, :]

        dmixv = dmix_ref[...]
        dcat = _dot_nt(dmixv, wout_ref[...])
        acc_out[...] += _dot_tn(cat_ref[...], dmixv)

        z = z_ref[...]
        t_g, ga = _gelu_parts(z[:, :2 * D_A])
        u, vr = ga[:, :D_A], ga[:, D_A:]
        dv0 = vr - jnp.mean(vr, axis=-1, keepdims=True)
        rv = lax.rsqrt(jnp.mean(dv0 * dv0, axis=-1, keepdims=True) + EPS)
        vhat = dv0 * rv
        vb = (vhat * lng_ref[...] + lnb_ref[...]).astype(BF16)
        mask = _tril_mask()
        wc = [(wsp_ref[h] * mask).astype(BF16) for h in range(N_HEADS)]

        dya = dcat[:, :D_A]
        for h in range(N_HEADS):
            cols = slice(h * GROUP, (h + 1) * GROUP)
            bs_sum = jnp.zeros((CHUNK, GROUP), F32)
            ws_sum = jnp.zeros((CHUNK, CHUNK), F32)
            for ch in range(tt // CHUNK):
                rows = slice(ch * CHUNK, (ch + 1) * CHUNK)
                v_ch = vb[rows, cols]
                mixed = _dot(wc[h], v_ch) + bs_ref[:, cols]
                dy_ch = dya[rows, cols]
                dz_scr[rows, cols] = dy_ch * mixed
                dmixed = dy_ch * u[rows, cols]
                dmb = dmixed.astype(BF16)
                dz_scr[rows, D_A + h * GROUP:D_A + (h + 1) * GROUP] = _dot_tn(wc[h], dmb)
                bs_sum = bs_sum + dmixed
                ws_sum = ws_sum + _dot_nt(dmb, v_ch)
            _acc_rows(bs_acc, 0, h, bs_sum)
            _acc_rows(small_ref, ROW_WS, h, ws_sum)

        dvl = dz_scr[:, D_A:2 * D_A]
        dvhat = dvl * lng_ref[...]
        dvr = rv * (dvhat - jnp.mean(dvhat, axis=-1, keepdims=True)
                    - vhat * jnp.mean(dvhat * vhat, axis=-1, keepdims=True))
        small_ref[ROW_LN:ROW_LN + 1, 0:D_A] += _colsum(dvl * vhat)
        small_ref[ROW_LN:ROW_LN + 1, D_A:D] += _colsum(dvl)
        dga = jnp.concatenate([dz_scr[:, :D_A], dvr], axis=1)
        dza = dga * _gelu_grad(z[:, :2 * D_A], t_g)

        zb = z[:, 2 * D_A:]
        halo_prev = jnp.where(i == 0, 0.0, zprev_ref[...])
        diff = _pool_diff(zb, halo_prev, i * tt)
        dyb = dcat[:, D_A:]
        inv = _inv_counts(i * tt, tt)
        scaled, ddiffs = [], []
        for g in range(len(WINDOWS)):
            cols = slice(g * GROUP, (g + 1) * GROUP)
            db = diff[g].astype(BF16)
            wpg = wp_ref[g].astype(BF16)
            pre = _dot(db, wpg) + bp_ref[:, cols]
            small_ref[ROW_POOL:ROW_POOL + 1, cols] += _colsum(dyb[:, cols] * pre)
            dpre = dyb[:, cols] * ps_ref[:, cols]
            small_ref[ROW_POOL:ROW_POOL + 1, D_B + g * GROUP:D_B + (g + 1) * GROUP] += _colsum(dpre)
            dpb = dpre.astype(BF16)
            _acc_rows(small_ref, ROW_WP, g, _dot_tn(db, dpb))
            ddiff = _dot_nt(dpb, wpg)
            ddiffs.append(ddiff)
            scaled.append(ddiff * inv[g])
        scaled_all = jnp.concatenate(scaled, axis=1)
        ext = jnp.concatenate([scaled_all, carry[...]], axis=0)
        n_ext = tt + HALO
        s2 = ext + pltpu.roll(ext, n_ext - 1, 0)
        t4 = s2[:, GROUP:]
        s4 = t4 + pltpu.roll(t4, n_ext - 2, 0)
        t8 = s4[:, GROUP:]
        s8 = t8 + pltpu.roll(t8, n_ext - 4, 0)
        t16 = s8[:, GROUP:]
        s16 = t16 + pltpu.roll(t16, n_ext - 8, 0)
        back = [s2[:, :GROUP], s4[:, :GROUP], s8[:, :GROUP], s16]
        carry[...] = scaled_all[:HALO, :]
        dzb = jnp.concatenate([back[g][:tt, :] - ddiffs[g] for g in range(len(WINDOWS))], axis=1)

        dzv = jnp.concatenate([dza, dzb], axis=1).astype(BF16)
        dh1 = _dot(dzv, win_ref[...])
        xv = x_ref[...]
        r1 = _rstd(xv)
        xhat = xv * r1
        shift1, scale1 = mod_ref[0:1, :], mod_ref[1:2, :]
        n1 = xhat * n1pre_ref[...]
        h1 = (n1 * (1.0 + scale1) + shift1).astype(BF16)
        acc_in[...] += _dot_tn(dzv, h1)
        dn1 = dh1 * (1.0 + scale1)
        gx_ref[...] = dx1_ref[...] + _rms_bwd(dn1 * n1pre_ref[...], xhat, r1)
        small_ref[ROW_DMOD:ROW_DMOD + 1, :] += _colsum(dh1)
        small_ref[ROW_DMOD + 1:ROW_DMOD + 2, :] += _colsum(dh1 * n1)
        small_ref[ROW_N1PRE:ROW_N1PRE + 1, :] += _colsum(dn1 * xhat)

        @pl.when(s == nt - 1)
        def _():
            gwin_ref[...] = acc_in[...].astype(BF16)
            gwout_ref[...] = acc_out[...].astype(BF16)
            bs = _unfold(bs_acc[...])
            for h in range(N_HEADS):
                small_ref[ROW_BS + h:ROW_BS + h + 1, 0:GROUP] = jnp.sum(
                    bs[:, h * GROUP:(h + 1) * GROUP].T, axis=0, keepdims=True)
            for a in range(n_sums):
                for r in range(3):
                    chip_copy(a, r).wait_recv()
                    chip_copy(a, r).wait_send()

    rev = lambda w: pl.BlockSpec((tt, w), lambda s: (nt - 1 - s, 0))
    zprev = pl.BlockSpec((HALO, D_B), lambda s: (jnp.maximum((nt - 1 - s) * hb - 1, 0), 2))
    hbm = pl.BlockSpec(memory_space=pl.ANY)
    outs = pl.pallas_call(
        body, name="attn_bwd", grid=(nt,),
        out_shape=tuple([jax.ShapeDtypeStruct((t_len, D), F32), jax.ShapeDtypeStruct((D_Z, D), BF16),
                         jax.ShapeDtypeStruct((D, D), BF16), jax.ShapeDtypeStruct((SMALL_ROWS, D), F32)]
                        + [jax.ShapeDtypeStruct(cs.shape, cs.dtype) for cs in chip_sums]),
        in_specs=[rev(D), rev(D), rev(D), rev(D_Z), zprev, rev(D), _full((8, D)), _full((1, D)),
                  _resident((D_Z, D)), _resident((D, D)), _full((N_HEADS, CHUNK, CHUNK)), _full((CHUNK, D_A)),
                  _full((1, D_A)), _full((1, D_A)), _full((len(WINDOWS), GROUP, GROUP)), _full((1, D_B)),
                  _full((1, D_B)), _full((8, D)), _full((8, D))] + [_resident(cs.shape) for cs in chip_sums],
        out_specs=tuple([rev(D), _full((D_Z, D)), _full((D, D)), _full((SMALL_ROWS, D))] + [hbm] * n_sums),
        scratch_shapes=[pltpu.VMEM((HALO, D_B), F32), pltpu.VMEM((D_Z, D), F32), pltpu.VMEM((D, D), F32),
                        pltpu.VMEM((tt, 2 * D_A), F32), pltpu.VMEM((CHUNK // 2, D), F32),
                        pltpu.SemaphoreType.DMA((3 * n_sums,)), pltpu.SemaphoreType.DMA((3 * n_sums,))],
        compiler_params=pltpu.CompilerParams(dimension_semantics=("arbitrary",), vmem_limit_bytes=VMEM_LIMIT),
    )(dmix, dx1, x, z, z, cat, mod, n1pre, w_in_t, w_out, w_sp, bs_rows, ln_g, ln_b, w_pool, b_pool, pool_scale,
      red_fwd, red_bwd, *chip_sums)
    return outs[:4], outs[4:]


def _adam(w, g, m, v):
    m2 = ADAM_B1 * m + (1.0 - ADAM_B1) * g
    v2 = ADAM_B2 * v + (1.0 - ADAM_B2) * (g * g)
    m_hat = m2 / (1.0 - ADAM_B1 ** ADAM_STEP)
    v_hat = v2 / (1.0 - ADAM_B2 ** ADAM_STEP)
    delta = -ADAM_LR * (m_hat / (jnp.sqrt(v_hat) + ADAM_EPS) + ADAM_WD * w)
    return delta, m2, v2


def _adamw_shard(name, rb, w, g, m, v):
    rows, cols = w.shape

    def body(w_ref, g_ref, m_ref, v_ref, d_ref, m2_ref, v2_ref):
        d_ref[...], m2_ref[...], v2_ref[...] = _adam(w_ref[...], g_ref[...], m_ref[...], v_ref[...])

    blk = pl.BlockSpec((rb, cols), lambda i: (i, 0))
    shp = jax.ShapeDtypeStruct((rows, cols), F32)
    return pl.pallas_call(
        body, name=name, grid=(rows // rb,), out_shape=(shp, shp, shp),
        in_specs=[blk] * 4, out_specs=(blk, blk, blk),
        compiler_params=pltpu.CompilerParams(dimension_semantics=("arbitrary",)),
    )(w, g, m, v)


def _adamw_ada(rb, w, sc, dmod_cols, m, v):
    rows, cols = w.shape

    def body(w_ref, sc_ref, dm_ref, m_ref, v_ref, g_ref, d_ref, m2_ref, v2_ref):
        g = _dot_tn(sc_ref[...].astype(BF16), dm_ref[...].astype(BF16))
        g_ref[...] = g
        d_ref[...], m2_ref[...], v2_ref[...] = _adam(w_ref[...], g, m_ref[...], v_ref[...])

    blk = pl.BlockSpec((rb, cols), lambda i: (i, 0))
    shp = jax.ShapeDtypeStruct((rows, cols), F32)
    return pl.pallas_call(
        body, name="adamw_ada", grid=(rows // rb,), out_shape=(shp, shp, shp, shp),
        in_specs=[blk, pl.BlockSpec((N_DEV, rb), lambda i: (0, i)), _full((N_DEV, cols)), blk, blk],
        out_specs=(blk, blk, blk, blk),
        compiler_params=pltpu.CompilerParams(dimension_semantics=("arbitrary",)),
    )(w, sc, dmod_cols, m, v)


def _unfold(acc_rows):
    return jnp.concatenate([acc_rows[:, :D_A], acc_rows[:, D_A:]], axis=0)


def _adamw_small(total, params):
    n = len(params)
    flat = [a for p in params for a in p]

    def body(*refs):
        s_ref = refs[0]
        p_refs = refs[1:1 + 3 * n]
        loss_ref = refs[1 + 3 * n]
        o_refs = refs[2 + 3 * n:]
        d_b_ada = s_ref[0:6, :]
        for b in range(1, N_DEV):
            d_b_ada = d_b_ada + s_ref[8 * b:8 * b + 6, :]
        tot = s_ref[PACK_SHIFT:PACK_ROWS, :]
        loss_ref[...] = jnp.broadcast_to(tot[ROW_LOSS:ROW_LOSS + 1, 0:GROUP], (8, GROUP))
        mask = _tril_mask()
        ws = _unfold(tot[ROW_WS:ROW_WS + 64, :])
        wp = _unfold(tot[ROW_WP:ROW_WP + 64, :])
        grads = [
            d_b_ada,
            tot[ROW_N1PRE:ROW_N1PRE + 1, :], tot[ROW_N1POST:ROW_N1POST + 1, :],
            tot[ROW_N2PRE:ROW_N2PRE + 1, :], tot[ROW_N2POST:ROW_N2POST + 1, :],
            tot[ROW_LN:ROW_LN + 1, :D_A], tot[ROW_LN:ROW_LN + 1, D_A:],
            tot[ROW_POOL:ROW_POOL + 1, :D_B], tot[ROW_POOL:ROW_POOL + 1, D_B:],
            tot[ROW_BS:ROW_BS + N_HEADS, 0:GROUP],
            jnp.stack([ws[:, h * GROUP:(h + 1) * GROUP] * mask for h in range(N_HEADS)]),
            jnp.stack([wp[:, g * GROUP:(g + 1) * GROUP] for g in range(len(WINDOWS))]),
        ]
        for k in range(n):
            w_ref, m_ref, v_ref = p_refs[3 * k:3 * k + 3]
            g = grads[k]
            o_refs[4 * k][...] = g
            o_refs[4 * k + 1][...], o_refs[4 * k + 2][...], o_refs[4 * k + 3][...] = _adam(
                w_ref[...], g, m_ref[...], v_ref[...])

    vm = pl.BlockSpec(memory_space=pltpu.VMEM)
    out_shape = [jax.ShapeDtypeStruct((8, GROUP), F32)]
    for w, _, _ in params:
        out_shape += [jax.ShapeDtypeStruct(w.shape, F32)] * 4
    return pl.pallas_call(
        body, name="adamw_small", out_shape=tuple(out_shape),
        in_specs=[vm] * (1 + 3 * n), out_specs=tuple([vm] * len(out_shape)),
    )(total, *flat)


TT_ATTN_FWD = 512
TT_MLP_FWD = 512
TT_MLP = 256
TT_WGRAD = 2048
TT_ATTN_BWD = 256


def kernel(x, c, w_ada, b_ada, norm1_pre, norm1_post, w_in, w_spatial, b_spatial, ln_v_gain, ln_v_bias, w_pool, b_pool, pool_scale, w_out, norm2_pre, norm2_post, w_fc1, w_fc2, loss_target, m_w_ada, m_b_ada, m_norm1_pre, m_norm1_post, m_w_in, m_w_spatial, m_b_spatial, m_ln_v_gain, m_ln_v_bias, m_w_pool, m_b_pool, m_pool_scale, m_w_out, m_norm2_pre, m_norm2_post, m_w_fc1, m_w_fc2, v_w_ada, v_b_ada, v_norm1_pre, v_norm1_post, v_w_in, v_w_spatial, v_b_spatial, v_ln_v_gain, v_ln_v_bias, v_w_pool, v_b_pool, v_pool_scale, v_w_out, v_norm2_pre, v_norm2_post, v_w_fc1, v_w_fc2):
    t_len = x.shape[1]
    me = 4 * lax.axis_index("x") + 2 * lax.axis_index("y") + lax.axis_index("c")
    ada_cols = w_ada.shape[1]
    tt = lambda want: min(want, t_len)

    x2 = x.reshape(t_len, D)
    tgt = loss_target.reshape(t_len, D)
    row = lambda a: a.reshape(1, -1)

    b_my = lax.dynamic_slice_in_dim(b_ada, me * ada_cols, ada_cols).reshape(1, ada_cols)
    modp, sc, (g_in, g_out), fc_shards = _fwd_comm(jnp.broadcast_to(c, (8, D)), w_ada, b_my,
                                                   [w_in.T, w_out], [w_fc1, w_fc2])
    mod = jnp.concatenate([modp.reshape(6, D), jnp.zeros((2, D), F32)], axis=0)
    w_in_t = g_in.reshape(D_Z, D)
    w_out_all = g_out.reshape(D, D)

    bs_rows = jnp.repeat(b_spatial.T, GROUP, axis=1)
    attn_consts = (w_spatial, bs_rows, row(ln_v_gain), row(ln_v_bias), w_pool, row(b_pool), row(pool_scale))

    (z, cat, mix, x1), (w1_early, w2_early) = _attn_fwd(
        tt(TT_ATTN_FWD), x2, mod, row(norm1_pre), row(norm1_post), w_in_t, w_out_all, *attn_consts, fc_shards)
    (r_early, h2, f_early), (w1_late, w2_late) = _mlp_fwd_early(
        tt(TT_MLP_FWD), x1, mod, row(norm2_pre), w1_early, w2_early)
    r, df, dy, red_fwd = _mlp_fwd_late(tt(TT_MLP_FWD), r_early, x1, h2, f_early, tgt, mod, row(norm2_post),
                                       w1_late, w2_late)
    da, dmix, dx1, red_bwd = _mlp_bwd(tt(TT_MLP), df, r, dy, x1, mix, mod, row(norm2_pre), row(norm1_post),
                                      w1_early, w2_early, w1_late, w2_late)
    own_w1, own_w2, sums_w1, sums_w2 = _mlp_wgrad(tt(TT_WGRAD), r, da, df, h2)
    (grad_x, p_in, p_out, small), (arr_w1, arr_w2) = _attn_bwd(
        tt(TT_ATTN_BWD), dmix, dx1, x2, z, cat, mod, row(norm1_pre), w_in_t, w_out_all, *attn_consts,
        red_fwd, red_bwd, [sums_w1, sums_w2])
    (grad_in_t, grad_out, total), ((grad_w1, d_w1, m_w1, v_w1), (grad_w2, d_w2, m_w2, v_w2)) = _tail_comm(
        [p_in.reshape(N_DEV, D_Z // N_DEV, D), p_out.reshape(N_DEV, D // N_DEV, D)], small, 64,
        [(w_fc1, own_w1, arr_w1, m_w_fc1, v_w_fc1), (w_fc2, own_w2, arr_w2, m_w_fc2, v_w_fc2)])

    d_out, m_out, v_out = _adamw_shard("adamw_out", 128, w_out, grad_out, m_w_out, v_w_out)
    d_in_t, m_in_t, v_in_t = _adamw_shard("adamw_in", D_Z // N_DEV, w_in.T, grad_in_t, m_w_in.T, v_w_in.T)
    dmod_all = total[0:TABLE_ROWS, :].reshape(N_DEV, 8, D)[:, :6, :].reshape(N_DEV, 6 * D)
    dmod_cols = lax.dynamic_slice_in_dim(dmod_all, me * ada_cols, ada_cols, axis=1)
    grad_ada, d_ada, m_ada, v_ada = _adamw_ada(256, w_ada, sc, dmod_cols, m_w_ada, v_w_ada)

    six = lambda a: a.reshape(6, D)
    small_params = [
        (six(b_ada), six(m_b_ada), six(v_b_ada)),
        (row(norm1_pre), row(m_norm1_pre), row(v_norm1_pre)),
        (row(norm1_post), row(m_norm1_post), row(v_norm1_post)),
        (row(norm2_pre), row(m_norm2_pre), row(v_norm2_pre)),
        (row(norm2_post), row(m_norm2_post), row(v_norm2_post)),
        (row(ln_v_gain), row(m_ln_v_gain), row(v_ln_v_gain)),
        (row(ln_v_bias), row(m_ln_v_bias), row(v_ln_v_bias)),
        (row(pool_scale), row(m_pool_scale), row(v_pool_scale)),
        (row(b_pool), row(m_b_pool), row(v_b_pool)),
        (b_spatial, m_b_spatial, v_b_spatial),
        (w_spatial, m_w_spatial, v_w_spatial),
        (w_pool, m_w_pool, v_w_pool),
    ]
    outs = _adamw_small(total, small_params)
    loss = outs[0][0, 0]
    names = ["b_ada", "norm1_pre", "norm1_post", "norm2_pre", "norm2_post", "ln_v_gain", "ln_v_bias", "pool_scale",
             "b_pool", "b_spatial", "w_spatial", "w_pool"]
    shapes = dict(b_ada=b_ada.shape, norm1_pre=norm1_pre.shape, norm1_post=norm1_post.shape,
                  norm2_pre=norm2_pre.shape, norm2_post=norm2_post.shape, ln_v_gain=ln_v_gain.shape,
                  ln_v_bias=ln_v_bias.shape, pool_scale=pool_scale.shape, b_pool=b_pool.shape,
                  b_spatial=b_spatial.shape, w_spatial=w_spatial.shape, w_pool=w_pool.shape)
    res = {}
    for k, nm in enumerate(names):
        res[nm] = tuple(o.reshape(shapes[nm]) for o in outs[1 + 4 * k:5 + 4 * k])
    res["w_ada"] = (grad_ada, d_ada, m_ada, v_ada)
    res["w_in"] = (grad_in_t.T, d_in_t.T, m_in_t.T, v_in_t.T)
    res["w_out"] = (grad_out, d_out, m_out, v_out)
    res["w_fc1"] = (grad_w1, d_w1, m_w1, v_w1)
    res["w_fc2"] = (grad_w2, d_w2, m_w2, v_w2)

    order = ["w_ada", "b_ada", "norm1_pre", "norm1_post", "w_in", "w_spatial", "b_spatial", "ln_v_gain", "ln_v_bias",
             "w_pool", "b_pool", "pool_scale", "w_out", "norm2_pre", "norm2_post", "w_fc1", "w_fc2"]
    return (loss, grad_x.reshape(x.shape),
            *[res[nm][0] for nm in order], *[res[nm][1] for nm in order],
            *[res[nm][2] for nm in order], *[res[nm][3] for nm in order])
```

```python
import functools

import jax
import jax.numpy as jnp
from jax import lax
from jax.experimental import pallas as pl
from jax.experimental.pallas import tpu as pltpu

F32 = jnp.float32
BF16 = jnp.bfloat16
MESH = pl.DeviceIdType.MESH

N_DEV = 8
D = 1024
D_A = 512
D_B = 512
D_Z = 2 * D_A + D_B
N_HEADS = 4
CHUNK = 128
WINDOWS = (2, 4, 8, 16)
GROUP = 128
D_FF = 4096
FF_BLK = D_FF // N_DEV
HALO = 16
EPS = 1e-6
VMEM_LIMIT = 60 * 1024 * 1024

ADAM_LR = 0.001
ADAM_B1 = 0.9
ADAM_B2 = 0.999
ADAM_EPS = 1e-08
ADAM_WD = 0.01
ADAM_STEP = 10

ROW_DMOD = 0
ROW_N1PRE, ROW_N1POST, ROW_N2PRE, ROW_N2POST = 8, 9, 10, 11
ROW_LN = 12
ROW_POOL = 13
ROW_LOSS = 14
ROW_BS = 16
ROW_WS = 24
ROW_WP = 88
SMALL_ROWS = 152
TABLE_ROWS = 8 * N_DEV
PACK_SHIFT = TABLE_ROWS - 8
PACK_ROWS = SMALL_ROWS + PACK_SHIFT
PACK_HALF = PACK_ROWS // 2


def _dot(a, b):
    return jnp.dot(a, b, preferred_element_type=F32)


def _dot_nt(a, b):
    return lax.dot_general(a, b, (((1,), (1,)), ((), ())), preferred_element_type=F32)


def _dot_tn(a, b):
    return lax.dot_general(a, b, (((0,), (0,)), ((), ())), preferred_element_type=F32)


def _rstd(v):
    return lax.rsqrt(jnp.mean(v * v, axis=-1, keepdims=True) + EPS)


def _rms_bwd(d_hat, hat, rstd):
    return rstd * (d_hat - hat * jnp.mean(d_hat * hat, axis=-1, keepdims=True))


_K0 = 0.7978845608028654
_K1 = 0.044715


def _gelu_parts(v):
    t = jnp.tanh(_K0 * (v + _K1 * (v * v * v)))
    return t, v * (0.5 * (1.0 + t))


def _gelu_grad(v, t):
    return 0.5 * (1.0 + t) + (0.5 * v) * (1.0 - t * t) * (_K0 * (1.0 + (3.0 * _K1) * (v * v)))


def _colsum(v):
    return jnp.sum(v, axis=0, keepdims=True)


def _full(shape):
    n = len(shape)
    return pl.BlockSpec(shape, lambda *_: (0,) * n)


def _resident(shape):
    n = len(shape)
    return pl.BlockSpec(shape, lambda *_: (0,) * n, pipeline_mode=pl.Buffered(1))


def _place():
    x, y, c = lax.axis_index("x"), lax.axis_index("y"), lax.axis_index("c")
    return x, y, c


def _flip(v, bit):
    return 1 - v if bit else v


def _peer(x, y, c, k):
    return (_flip(x, (k >> 2) & 1), _flip(y, (k >> 1) & 1), _flip(c, k & 1))


def _index(p):
    return 4 * p[0] + 2 * p[1] + p[2]


def _two_level_gather_begin(x, y, c, out_refs, send_sems, recv_sems):
    me = (x, y, c)
    sibling = (x, y, 1 - c)
    chips = [(1 - x, y), (x, 1 - y), (1 - x, 1 - y)]

    def copy(a, k, block, to):
        ref = out_refs[a].at[_index(block)]
        return pltpu.make_async_remote_copy(
            src_ref=ref, dst_ref=ref, send_sem=send_sems.at[7 * a + k], recv_sem=recv_sems.at[7 * a + k],
            device_id=to, device_id_type=MESH)

    first = []
    for a in range(len(out_refs)):
        first.append(copy(a, 0, me, sibling))
        first += [copy(a, 1 + j, me, (*chip, c)) for j, chip in enumerate(chips)]
    for cp in first:
        cp.start()
    return copy, first, me, sibling, chips


def _two_level_gather_finish(c, n, begun):
    copy, first, me, sibling, chips = begun
    passed = []
    for a in range(n):
        for j, chip in enumerate(chips):
            copy(a, 1 + j, (*chip, c), me).wait_recv()
            fwd = copy(a, 4 + j, (*chip, c), sibling)
            fwd.start()
            passed.append(fwd)
    for a in range(n):
        copy(a, 0, sibling, me).wait_recv()
        for j, chip in enumerate(chips):
            copy(a, 4 + j, (*chip, 1 - c), me).wait_recv()
    for cp in first + passed:
        cp.wait_send()


def _fwd_comm(c8, w_ada, b_my, gathered, kept):
    ncol = w_ada.shape[1]
    n_g, n_k = len(gathered), len(kept)

    def body(c_ref, w_ref, b_ref, *rest):
        g_in, k_in = rest[:n_g], rest[n_g:n_g + n_k]
        modp_ref, sc_ref = rest[n_g + n_k:n_g + n_k + 2]
        g_out = rest[n_g + n_k + 2:2 * n_g + n_k + 2]
        k_out = rest[2 * n_g + n_k + 2:2 * n_g + 2 * n_k + 2]
        cg, mg, part, send_sems, recv_sems, g_send, g_recv = rest[2 * n_g + 2 * n_k + 2:]
        x, y, c = _place()
        me = _index((x, y, c))
        for a in range(n_g):
            g_out[a][me] = g_in[a][...].astype(BF16)
        begun = _two_level_gather_begin(x, y, c, g_out, g_send, g_recv)
        for a in range(n_k):
            k_out[a][...] = k_in[a][...].astype(BF16)

        def c_copy(k):
            p = _peer(x, y, c, k)
            return pltpu.make_async_remote_copy(
                src_ref=c_ref, dst_ref=cg.at[me], send_sem=send_sems.at[k - 1], recv_sem=recv_sems.at[k - 1],
                device_id=p, device_id_type=MESH)

        def c_arrival(k):
            p = _peer(x, y, c, k)
            return pltpu.make_async_remote_copy(
                src_ref=c_ref, dst_ref=cg.at[_index(p)], send_sem=send_sems.at[k - 1], recv_sem=recv_sems.at[k - 1],
                device_id=p, device_id_type=MESH)

        def m_copy(k):
            p = _peer(x, y, c, k)
            return pltpu.make_async_remote_copy(
                src_ref=part, dst_ref=mg.at[me], send_sem=send_sems.at[6 + k], recv_sem=recv_sems.at[6 + k],
                device_id=p, device_id_type=MESH)

        def m_arrival(k):
            p = _peer(x, y, c, k)
            return pltpu.make_async_remote_copy(
                src_ref=part, dst_ref=mg.at[_index(p)], send_sem=send_sems.at[6 + k], recv_sem=recv_sems.at[6 + k],
                device_id=p, device_id_type=MESH)

        for k in range(1, N_DEV):
            c_copy(k).start()
        cg[me] = c_ref[...]
        for k in range(1, N_DEV):
            c_arrival(k).wait_recv()
        c_all = jnp.concatenate([cg[j, 0:1, :] for j in range(N_DEV)], axis=0)
        sc = c_all * jax.nn.sigmoid(c_all)
        sc_ref[...] = sc
        part[...] = _dot(sc.astype(BF16), w_ref[...].astype(BF16)) + b_ref[...]
        for k in range(1, N_DEV):
            m_copy(k).start()
        mg[me] = part[...]
        for k in range(1, N_DEV):
            m_arrival(k).wait_recv()
        for j in range(N_DEV):
            modp_ref[j:j + 1, :] = mg[j, pl.ds(me, 1), :]
        _two_level_gather_finish(c, n_g, begun)
        for k in range(1, N_DEV):
            c_copy(k).wait_send()
            m_copy(k).wait_send()

    vm = pl.BlockSpec(memory_space=pltpu.VMEM)
    outs = pl.pallas_call(
        body, name="fwd_comm",
        out_shape=tuple([jax.ShapeDtypeStruct((N_DEV, ncol), F32), jax.ShapeDtypeStruct((N_DEV, D), F32)]
                        + [jax.ShapeDtypeStruct((N_DEV,) + s.shape, BF16) for s in gathered]
                        + [jax.ShapeDtypeStruct(s.shape, BF16) for s in kept]),
        in_specs=[vm] * (3 + n_g + n_k), out_specs=tuple([vm] * (2 + n_g + n_k)),
        scratch_shapes=[
            pltpu.VMEM((N_DEV, 8, D), F32),
            pltpu.VMEM((N_DEV, N_DEV, ncol), F32),
            pltpu.VMEM((N_DEV, ncol), F32),
            pltpu.SemaphoreType.DMA((2 * (N_DEV - 1),)),
            pltpu.SemaphoreType.DMA((2 * (N_DEV - 1),)),
            pltpu.SemaphoreType.DMA((7 * n_g,)),
            pltpu.SemaphoreType.DMA((7 * n_g,)),
        ],
        compiler_params=pltpu.CompilerParams(vmem_limit_bytes=VMEM_LIMIT),
    )(c8, w_ada, b_my, *gathered, *kept)
    return outs[0], outs[1], outs[2:2 + n_g], outs[2 + n_g:]


FC_EARLY = 6


class _Copies:
    def __init__(self, entries, send_sems, recv_sems):
        self.place = _place()
        self.entries, self.send_sems, self.recv_sems = entries, send_sems, recv_sems

    def _copy(self, i, arrival=False):
        src, dst, rel = self.entries[i]
        return pltpu.make_async_remote_copy(
            src_ref=dst if arrival else src, dst_ref=dst, send_sem=self.send_sems.at[i],
            recv_sem=self.recv_sems.at[i], device_id=_peer(*self.place, rel), device_id_type=MESH)

    def start(self, *which):
        for i in which:
            self._copy(i).start()

    def wait_recv(self, *which):
        for i in which:
            self._copy(i, arrival=True).wait_recv()

    def wait_send(self, *which):
        for i in which:
            self._copy(i).wait_send()


TAIL_STEPS = 8


def _tail_comm(parts, small, row_chunk, fc):
    n, n_fc = len(parts), len(fc)

    def body(*refs):
        p_refs, small_ref = refs[:n], refs[n]
        fc_in = refs[n + 1:n + 1 + 5 * n_fc]
        outs = refs[n + 1 + 5 * n_fc:]
        g_refs, total_ref = outs[:n], outs[n]
        fc_out = outs[n + 1:n + 1 + 4 * n_fc]
        scr = outs[n + 1 + 4 * n_fc:]
        from_sib = scr[0:n]
        chip_out = scr[n:2 * n]
        chip_in = scr[2 * n:3 * n]
        pack, pack_sib, halves, total_scr = scr[3 * n:3 * n + 4]
        send_a, recv_a, send_b, recv_b, send_s, recv_s = scr[3 * n + 4:]
        step = pl.program_id(0)
        x, y, c = _place()
        me = _index((x, y, c))
        sibling = (x, y, 1 - c)
        my_chip = 2 * x + y
        others = [(1 - x, y), (x, 1 - y), (1 - x, 1 - y)]
        my_half = pl.ds(pl.multiple_of(PACK_HALF * c, 8), PACK_HALF)

        def pack_to_sibling():
            return pltpu.make_async_remote_copy(
                src_ref=pack, dst_ref=pack_sib, send_sem=send_s.at[0], recv_sem=recv_s.at[0],
                device_id=sibling, device_id_type=MESH)

        def half_to_chip(r):
            return pltpu.make_async_remote_copy(
                src_ref=halves.at[my_chip], dst_ref=halves.at[my_chip],
                send_sem=send_s.at[1 + r], recv_sem=recv_s.at[1 + r],
                device_id=(*others[r], c), device_id_type=MESH)

        def half_from_chip(r):
            k = 2 * others[r][0] + others[r][1]
            return pltpu.make_async_remote_copy(
                src_ref=halves.at[k], dst_ref=halves.at[k], send_sem=send_s.at[1 + r], recv_sem=recv_s.at[1 + r],
                device_id=(*others[r], c), device_id_type=MESH)

        def total_to_sibling():
            return pltpu.make_async_remote_copy(
                src_ref=total_scr.at[my_half], dst_ref=total_scr.at[my_half],
                send_sem=send_s.at[4], recv_sem=recv_s.at[4], device_id=sibling, device_id_type=MESH)

        def total_from_sibling():
            sib_half = pl.ds(pl.multiple_of(PACK_HALF * (1 - c), 8), PACK_HALF)
            return pltpu.make_async_remote_copy(
                src_ref=total_scr.at[sib_half], dst_ref=total_scr.at[sib_half],
                send_sem=send_s.at[4], recv_sem=recv_s.at[4], device_id=sibling, device_id_type=MESH)

        def to_sibling(a, k):
            return pltpu.make_async_remote_copy(
                src_ref=p_refs[a].at[2 * k + (1 - c)], dst_ref=from_sib[a].at[k],
                send_sem=send_a.at[a], recv_sem=recv_a.at[a], device_id=sibling, device_id_type=MESH)

        def all_from_sibling(a):
            return pltpu.make_async_remote_copy(
                src_ref=from_sib[a], dst_ref=from_sib[a], send_sem=send_a.at[a], recv_sem=recv_a.at[a],
                device_id=sibling, device_id_type=MESH)

        def to_chip(a, r):
            return pltpu.make_async_remote_copy(
                src_ref=chip_out[a].at[r], dst_ref=chip_in[a].at[r],
                send_sem=send_b.at[3 * a + r], recv_sem=recv_b.at[3 * a + r],
                device_id=(*others[r], c), device_id_type=MESH)

        @pl.when(step == 0)
        def _():
            pack[0:TABLE_ROWS, :] = jnp.zeros((TABLE_ROWS, D), F32)
            pack[pl.ds(pl.multiple_of(8 * me, 8), 8), :] = small_ref[0:8, :]
            pack[TABLE_ROWS:PACK_ROWS, :] = small_ref[8:SMALL_ROWS, :]
            pack_to_sibling().start()
            for a in range(n):
                for k in range(4):
                    to_sibling(a, k).start()

        @pl.when(step == 1)
        def _():
            pack_to_sibling().wait_recv()
            halves[my_chip] = pack[my_half, :] + pack_sib[my_half, :]
            for r in range(3):
                half_to_chip(r).start()
            for a in range(n):
                all_from_sibling(a).wait_recv()
                rows = p_refs[a].shape[1]
                for r in range(3):
                    k = 2 * others[r][0] + others[r][1]
                    for s in range(0, rows, row_chunk):
                        sl = pl.ds(s, row_chunk)
                        chip_out[a][r, sl, :] = (p_refs[a][2 * k + c, sl, :].astype(F32)
                                                 + from_sib[a][k, sl, :].astype(F32)).astype(BF16)
                    to_chip(a, r).start()
                for s in range(0, rows, row_chunk):
                    sl = pl.ds(s, row_chunk)
                    g_refs[a][sl, :] = (p_refs[a][2 * my_chip + c, sl, :].astype(F32)
                                        + from_sib[a][my_chip, sl, :].astype(F32))

        for k in range(n_fc):
            w_ref, own_ref, arr_ref, m_ref, v_ref = fc_in[5 * k:5 * k + 5]
            g = own_ref[...]
            for r in range(3):
                g = g + arr_ref[r].astype(F32)
            fc_out[4 * k][...] = g
            fc_out[4 * k + 1][...], fc_out[4 * k + 2][...], fc_out[4 * k + 3][...] = _adam(
                w_ref[...], g, m_ref[...], v_ref[...])

        @pl.when(step == TAIL_STEPS - 1)
        def _():
            for r in range(3):
                half_from_chip(r).wait_recv()
            total_scr[my_half, :] = ((halves[0] + halves[1]) + halves[2]) + halves[3]
            total_to_sibling().start()
            for a in range(n):
                rows = p_refs[a].shape[1]
                for r in range(3):
                    to_chip(a, r).wait_recv()
                    for s in range(0, rows, row_chunk):
                        sl = pl.ds(s, row_chunk)
                        g_refs[a][sl, :] = g_refs[a][sl, :] + chip_in[a][r, sl, :].astype(F32)
            total_from_sibling().wait_recv()
            total_ref[...] = total_scr[...]
            for a in range(n):
                all_from_sibling(a).wait_send()
                for r in range(3):
                    to_chip(a, r).wait_send()
            pack_to_sibling().wait_send()
            for r in range(3):
                half_to_chip(r).wait_send()
            total_to_sibling().wait_send()

    fc_specs_in, fc_specs_out, fc_shapes, fc_args = [], [], [], []
    for w, own, arrived, m, v in fc:
        rows, cols = w.shape
        blk = pl.BlockSpec((rows // TAIL_STEPS, cols), lambda i: (i, 0))
        fc_specs_in += [blk, blk, pl.BlockSpec((3, rows // TAIL_STEPS, cols), lambda i: (0, i, 0)), blk, blk]
        fc_specs_out += [blk] * 4
        fc_shapes += [jax.ShapeDtypeStruct((rows, cols), F32)] * 4
        fc_args += [w, own, arrived, m, v]
    outs = pl.pallas_call(
        body, name="tail_comm", grid=(TAIL_STEPS,),
        out_shape=tuple([jax.ShapeDtypeStruct(p.shape[1:], F32) for p in parts]
                        + [jax.ShapeDtypeStruct((PACK_ROWS, D), F32)] + fc_shapes),
        in_specs=[_resident(p.shape) for p in parts] + [_resident(small.shape)] + fc_specs_in,
        out_specs=tuple([_full(p.shape[1:]) for p in parts] + [_full((PACK_ROWS, D))] + fc_specs_out),
        scratch_shapes=(
            [pltpu.VMEM((4,) + p.shape[1:], BF16) for p in parts]
            + [pltpu.VMEM((3,) + p.shape[1:], BF16) for p in parts]
            + [pltpu.VMEM((3,) + p.shape[1:], BF16) for p in parts]
            + [pltpu.VMEM((PACK_ROWS, D), F32), pltpu.VMEM((PACK_ROWS, D), F32),
               pltpu.VMEM((4, PACK_HALF, D), F32), pltpu.VMEM((PACK_ROWS, D), F32)]
            + [pltpu.SemaphoreType.DMA((n,)), pltpu.SemaphoreType.DMA((n,)),
               pltpu.SemaphoreType.DMA((3 * n,)), pltpu.SemaphoreType.DMA((3 * n,)),
               pltpu.SemaphoreType.DMA((5,)), pltpu.SemaphoreType.DMA((5,))]),
        compiler_params=pltpu.CompilerParams(dimension_semantics=("arbitrary",), vmem_limit_bytes=VMEM_LIMIT),
    )(*parts, small, *fc_args)
    return outs[:n + 1], [outs[n + 1 + 4 * k:n + 5 + 4 * k] for k in range(n_fc)]


def _tril_mask():
    row = lax.broadcasted_iota(jnp.int32, (CHUNK, CHUNK), 0)
    col = lax.broadcasted_iota(jnp.int32, (CHUNK, CHUNK), 1)
    return (col <= row).astype(F32)


def _window_sums(ext):
    s2 = ext + pltpu.roll(ext, 1, 0)
    t4 = s2[:, GROUP:]
    s4 = t4 + pltpu.roll(t4, 2, 0)
    t8 = s4[:, GROUP:]
    s8 = t8 + pltpu.roll(t8, 4, 0)
    t16 = s8[:, GROUP:]
    s16 = t16 + pltpu.roll(t16, 8, 0)
    return [s2[:, :GROUP], s4[:, :GROUP], s8[:, :GROUP], s16]


def _inv_counts(first_pos, rows):
    pos = first_pos + lax.broadcasted_iota(jnp.int32, (rows, 1), 0)
    return [1.0 / jnp.minimum(pos + 1, w).astype(F32) for w in WINDOWS]


def _pool_diff(zb, halo, first_pos):
    tt = zb.shape[0]
    sums = _window_sums(jnp.concatenate([halo, zb], axis=0))
    inv = _inv_counts(first_pos, tt)
    return [sums[g][HALO:, :] * inv[g] - zb[:, g * GROUP:(g + 1) * GROUP] for g in range(len(WINDOWS))]


def _attn_fwd(tt, x, mod, n1pre, n1post, w_in_t, w_out, w_sp, bs_rows, ln_g, ln_b, w_pool, b_pool, pool_scale,
              fc_shards):
    t_len = x.shape[0]
    nt = t_len // tt

    def body(x_ref, mod_ref, n1pre_ref, n1post_ref, win_ref, wout_ref, wsp_ref, bs_ref, lng_ref, lnb_ref,
             wp_ref, bp_ref, ps_ref, w1_ref, w2_ref, z_ref, cat_ref, mix_ref, x1_ref, e1_ref, e2_ref,
             carry, land1, land2, send_sems, recv_sems, local_sems):
        i = pl.program_id(0)
        copies = _Copies(
            [(w1_ref, e1_ref.at[1], 1), (w2_ref, e2_ref.at[1], 1),
             (w1_ref, land1.at[0], 2), (w2_ref, land2.at[0], 2),
             (w1_ref, land1.at[1], 4), (w2_ref, land2.at[1], 4),
             (land1.at[0], e1_ref.at[3], 1), (land2.at[0], e2_ref.at[3], 1),
             (land1.at[1], e1_ref.at[5], 1), (land2.at[1], e2_ref.at[5], 1)],
            send_sems, recv_sems)
        keep = [pltpu.make_async_copy(w1_ref, e1_ref.at[0], local_sems.at[0]),
                pltpu.make_async_copy(w2_ref, e2_ref.at[0], local_sems.at[1]),
                pltpu.make_async_copy(land1.at[0], e1_ref.at[2], local_sems.at[2]),
                pltpu.make_async_copy(land1.at[1], e1_ref.at[4], local_sems.at[3]),
                pltpu.make_async_copy(land2.at[0], e2_ref.at[2], local_sems.at[4]),
                pltpu.make_async_copy(land2.at[1], e2_ref.at[4], local_sems.at[5])]

        @pl.when(i == 0)
        def _():
            copies.start(2, 4, 3, 5, 0, 1)
            keep[0].start()
            keep[1].start()
            carry[...] = jnp.zeros_like(carry)

        @pl.when(i == nt // 2)
        def _():
            copies.wait_recv(2, 4)
            copies.start(6, 8)
            keep[2].start()
            keep[3].start()

        @pl.when(i == nt - 1)
        def _():
            copies.wait_recv(3, 5)
            copies.start(7, 9)
            keep[4].start()
            keep[5].start()

        xv = x_ref[...]
        shift1, scale1, gate1 = mod_ref[0:1, :], mod_ref[1:2, :], mod_ref[2:3, :]
        h1 = (xv * _rstd(xv) * n1pre_ref[...]) * (1.0 + scale1) + shift1
        z = _dot_nt(h1.astype(BF16), win_ref[...])
        z_ref[...] = z

        _, ga = _gelu_parts(z[:, :2 * D_A])
        u, vr = ga[:, :D_A], ga[:, D_A:]
        dv = vr - jnp.mean(vr, axis=-1, keepdims=True)
        v = (dv * lax.rsqrt(jnp.mean(dv * dv, axis=-1, keepdims=True) + EPS)) * lng_ref[...] + lnb_ref[...]
        vb = v.astype(BF16)
        mask = _tril_mask()
        wc = [(wsp_ref[h] * mask).astype(BF16) for h in range(N_HEADS)]
        for ch in range(tt // CHUNK):
            rows = slice(ch * CHUNK, (ch + 1) * CHUNK)
            for h in range(N_HEADS):
                cols = slice(h * GROUP, (h + 1) * GROUP)
                mixed = _dot(wc[h], vb[rows, cols]) + bs_ref[:, cols]
                cat_ref[rows, cols] = (u[rows, cols] * mixed).astype(BF16)

        zb = z[:, 2 * D_A:]
        diff = _pool_diff(zb, carry[...], i * tt)
        carry[...] = zb[tt - HALO:, :]
        for g in range(len(WINDOWS)):
            cols = slice(g * GROUP, (g + 1) * GROUP)
            pre = _dot(diff[g].astype(BF16), wp_ref[g].astype(BF16)) + bp_ref[:, cols]
            cat_ref[:, D_A + g * GROUP:D_A + (g + 1) * GROUP] = (pre * ps_ref[:, cols]).astype(BF16)

        mix = _dot(cat_ref[...], wout_ref[...])
        mix_ref[...] = mix
        x1_ref[...] = xv + gate1 * (mix * _rstd(mix) * n1post_ref[...])

        @pl.when(i == nt - 1)
        def _():
            copies.wait_recv(0, 1, 6, 7, 8, 9)
            copies.wait_send(*range(10))
            for cp in keep:
                cp.wait()

    tile = lambda w: pl.BlockSpec((tt, w), lambda i: (i, 0))
    hbm = pl.BlockSpec(memory_space=pl.ANY)
    outs = pl.pallas_call(
        body, name="attn_fwd", grid=(nt,),
        out_shape=tuple([jax.ShapeDtypeStruct((t_len, D_Z), F32), jax.ShapeDtypeStruct((t_len, D), BF16),
                         jax.ShapeDtypeStruct((t_len, D), F32), jax.ShapeDtypeStruct((t_len, D), F32)]
                        + [jax.ShapeDtypeStruct((FC_EARLY,) + s.shape, BF16) for s in fc_shards]),
        in_specs=[tile(D), _full((8, D)), _full((1, D)), _full((1, D)), _resident((D_Z, D)), _resident((D, D)),
                  _full((N_HEADS, CHUNK, CHUNK)), _full((CHUNK, D_A)), _full((1, D_A)), _full((1, D_A)),
                  _full((len(WINDOWS), GROUP, GROUP)), _full((1, D_B)), _full((1, D_B)),
                  _resident(fc_shards[0].shape), _resident(fc_shards[1].shape)],
        out_specs=(tile(D_Z), tile(D), tile(D), tile(D), hbm, hbm),
        scratch_shapes=[pltpu.VMEM((HALO, D_B), F32),
                        pltpu.VMEM((2,) + fc_shards[0].shape, BF16), pltpu.VMEM((2,) + fc_shards[1].shape, BF16),
                        pltpu.SemaphoreType.DMA((10,)), pltpu.SemaphoreType.DMA((10,)),
                        pltpu.SemaphoreType.DMA((6,))],
        compiler_params=pltpu.CompilerParams(dimension_semantics=("arbitrary",), vmem_limit_bytes=VMEM_LIMIT),
    )(x, mod, n1pre, n1post, w_in_t, w_out, w_sp, bs_rows, ln_g, ln_b, w_pool, b_pool, pool_scale, *fc_shards)
    return outs[:4], outs[4:]


def _mlp_fwd_early(tt, x1, mod, n2pre, w1_early, w2_early):
    t_len = x1.shape[0]
    nt = t_len // tt
    n_late = N_DEV - FC_EARLY

    def body(x1_ref, mod_ref, n2pre_ref, w1_ref, w2_ref, r_ref, h2_ref, f_ref, l1_ref, l2_ref,
             land1, land2, send_sems, recv_sems, local_sems):
        i = pl.program_id(0)
        copies = _Copies(
            [(w1_ref.at[2], land1, 4), (w2_ref.at[4], land2, 2),
             (land1, l1_ref.at[1], 1), (land2, l2_ref.at[1], 1)],
            send_sems, recv_sems)
        keep = [pltpu.make_async_copy(land1, l1_ref.at[0], local_sems.at[0]),
                pltpu.make_async_copy(land2, l2_ref.at[0], local_sems.at[1])]

        @pl.when(i == 0)
        def _():
            copies.start(0, 1)

        @pl.when(i == nt // 2)
        def _():
            copies.wait_recv(0, 1)
            copies.start(2, 3)
            for cp in keep:
                cp.start()

        x1v = x1_ref[...]
        shift2, scale2 = mod_ref[3:4, :], mod_ref[4:5, :]
        h2 = ((x1v * _rstd(x1v) * n2pre_ref[...]) * (1.0 + scale2) + shift2).astype(BF16)
        h2_ref[...] = h2
        for j in range(FC_EARLY):
            cols = slice(j * FF_BLK, (j + 1) * FF_BLK)
            ra = jnp.maximum(_dot(h2, w1_ref[j]), 0.0)
            r = (ra * ra).astype(BF16)
            r_ref[:, cols] = r
            contrib = _dot(r, w2_ref[j])
            if j == 0:
                f_ref[...] = contrib
            else:
                f_ref[...] += contrib

        @pl.when(i == nt - 1)
        def _():
            copies.wait_recv(2, 3)
            copies.wait_send(0, 1, 2, 3)
            for cp in keep:
                cp.wait()

    tile = lambda w: pl.BlockSpec((tt, w), lambda i: (i, 0))
    hbm = pl.BlockSpec(memory_space=pl.ANY)
    outs = pl.pallas_call(
        body, name="mlp_fwd_early", grid=(nt,),
        out_shape=(jax.ShapeDtypeStruct((t_len, D_FF), BF16),
                   jax.ShapeDtypeStruct((t_len, D), BF16), jax.ShapeDtypeStruct((t_len, D), F32),
                   jax.ShapeDtypeStruct((n_late,) + w1_early.shape[1:], BF16),
                   jax.ShapeDtypeStruct((n_late,) + w2_early.shape[1:], BF16)),
        in_specs=[tile(D), _full((8, D)), _full((1, D)),
                  _resident((FC_EARLY, D, FF_BLK)), _resident((FC_EARLY, FF_BLK, D))],
        out_specs=(tile(FC_EARLY * FF_BLK), tile(D), tile(D), hbm, hbm),
        scratch_shapes=[pltpu.VMEM(w1_early.shape[1:], BF16), pltpu.VMEM(w2_early.shape[1:], BF16),
                        pltpu.SemaphoreType.DMA((4,)), pltpu.SemaphoreType.DMA((4,)),
                        pltpu.SemaphoreType.DMA((2,))],
        compiler_params=pltpu.CompilerParams(dimension_semantics=("arbitrary",), vmem_limit_bytes=VMEM_LIMIT),
    )(x1, mod, n2pre, w1_early, w2_early)
    return outs[:3], outs[3:]


def _mlp_fwd_late(tt, r_all, x1, h2, f_early, tgt, mod, n2post, w1_late, w2_late):
    t_len = x1.shape[0]
    nt = t_len // tt
    n_late = N_DEV - FC_EARLY

    def body(r_all_ref, x1_ref, h2_ref, fe_ref, tgt_ref, mod_ref, n2post_ref, w1_ref, w2_ref,
             r_ref, df_ref, dy_ref, red_ref):
        i = pl.program_id(0)

        @pl.when(i == 0)
        def _():
            red_ref[...] = jnp.zeros_like(red_ref)

        x1v = x1_ref[...]
        gate2 = mod_ref[5:6, :]
        h2 = h2_ref[...]
        f = fe_ref[...]
        for j in range(n_late):
            cols = slice(j * FF_BLK, (j + 1) * FF_BLK)
            ra = jnp.maximum(_dot(h2, w1_ref[j]), 0.0)
            r = (ra * ra).astype(BF16)
            r_ref[:, cols] = r
            f = f + _dot(r, w2_ref[j])
        rf = _rstd(f)
        fhat = f * rf
        nf = fhat * n2post_ref[...]
        err = (x1v + gate2 * nf) - tgt_ref[...]
        dy = err * (1.0 / D)
        dy_ref[...] = dy
        dnf = dy * gate2
        df_ref[...] = _rms_bwd(dnf * n2post_ref[...], fhat, rf).astype(BF16)
        red_ref[0:1, :] += _colsum(dy * nf)
        red_ref[1:2, :] += _colsum(dnf * fhat)
        red_ref[2:3, :] += _colsum(0.5 * jnp.mean(err * err, axis=-1, keepdims=True)) * jnp.ones((1, D), F32)

    tile = lambda w: pl.BlockSpec((tt, w), lambda i: (i, 0))
    return pl.pallas_call(
        body, name="mlp_fwd_late", grid=(nt,),
        out_shape=(jax.ShapeDtypeStruct((t_len, D_FF), BF16), jax.ShapeDtypeStruct((t_len, D), BF16),
                   jax.ShapeDtypeStruct((t_len, D), F32), jax.ShapeDtypeStruct((8, D), F32)),
        in_specs=[pl.BlockSpec(memory_space=pl.ANY), tile(D), tile(D), tile(D), tile(D), _full((8, D)),
                  _full((1, D)), _resident((n_late, D, FF_BLK)), _resident((n_late, FF_BLK, D))],
        out_specs=(pl.BlockSpec((tt, n_late * FF_BLK), lambda i: (i, FC_EARLY // n_late)), tile(D), tile(D),
                   _full((8, D))),
        input_output_aliases={0: 0},
        compiler_params=pltpu.CompilerParams(dimension_semantics=("arbitrary",), vmem_limit_bytes=VMEM_LIMIT),
    )(r_all, x1, h2, f_early, tgt, mod, n2post, w1_late, w2_late)


def _mlp_bwd(tt, df, r, dy, x1, mix, mod, n2pre, n1post, w1_early, w2_early, w1_late, w2_late):
    t_len = x1.shape[0]
    nt = t_len // tt
    n_late = N_DEV - FC_EARLY

    def body(df_ref, r_ref, dy_ref, x1_ref, mix_ref, mod_ref, n2pre_ref, n1post_ref,
             w1e_ref, w2e_ref, w1l_ref, w2l_ref, da_ref, dmix_ref, dx1_ref, red_ref, dh2_acc):
        i = pl.program_id(0)

        @pl.when(i == 0)
        def _():
            red_ref[...] = jnp.zeros_like(red_ref)

        dfv = df_ref[...]
        for j in range(N_DEV):
            cols = slice(j * FF_BLK, (j + 1) * FF_BLK)
            if j < FC_EARLY:
                w1, w2 = w1e_ref[j], w2e_ref[j]
            else:
                w1, w2 = w1l_ref[j - FC_EARLY], w2l_ref[j - FC_EARLY]
            dr = _dot_nt(dfv, w2)
            da = (dr * (2.0 * jnp.sqrt(r_ref[:, cols].astype(F32)))).astype(BF16)
            da_ref[:, cols] = da
            contrib = _dot_nt(da, w1)
            if j == 0:
                dh2_acc[...] = contrib
            else:
                dh2_acc[...] += contrib
        dh2 = dh2_acc[...]
        gate1, scale2 = mod_ref[2:3, :], mod_ref[4:5, :]
        x1v = x1_ref[...]
        r2 = _rstd(x1v)
        xhat = x1v * r2
        n2 = xhat * n2pre_ref[...]
        dn2 = dh2 * (1.0 + scale2)
        dx1 = dy_ref[...] + _rms_bwd(dn2 * n2pre_ref[...], xhat, r2)
        dx1_ref[...] = dx1
        mixv = mix_ref[...]
        rm = _rstd(mixv)
        mhat = mixv * rm
        dnm = dx1 * gate1
        dmix_ref[...] = _rms_bwd(dnm * n1post_ref[...], mhat, rm).astype(BF16)
        red_ref[0:1, :] += _colsum(dh2)
        red_ref[1:2, :] += _colsum(dh2 * n2)
        red_ref[2:3, :] += _colsum(dn2 * xhat)
        red_ref[3:4, :] += _colsum(dx1 * (mhat * n1post_ref[...]))
        red_ref[4:5, :] += _colsum(dnm * mhat)

    tile = lambda w: pl.BlockSpec((tt, w), lambda i: (i, 0))
    return pl.pallas_call(
        body, name="mlp_bwd", grid=(nt,),
        out_shape=(jax.ShapeDtypeStruct((t_len, D_FF), BF16),
                   jax.ShapeDtypeStruct((t_len, D), BF16), jax.ShapeDtypeStruct((t_len, D), F32),
                   jax.ShapeDtypeStruct((8, D), F32)),
        in_specs=[tile(D), tile(D_FF), tile(D), tile(D), tile(D),
                  _full((8, D)), _full((1, D)), _full((1, D)),
                  _resident((FC_EARLY, D, FF_BLK)), _resident((FC_EARLY, FF_BLK, D)),
                  _resident((n_late, D, FF_BLK)), _resident((n_late, FF_BLK, D))],
        out_specs=(tile(D_FF), tile(D), tile(D), _full((8, D))),
        scratch_shapes=[pltpu.VMEM((tt, D), F32)],
        compiler_params=pltpu.CompilerParams(dimension_semantics=("arbitrary",), vmem_limit_bytes=VMEM_LIMIT),
    )(df, r, dy, x1, mix, mod, n2pre, n1post, w1_early, w2_early, w1_late, w2_late)


def _mlp_wgrad(tt, r, da, df, h2):
    t_len = df.shape[0]
    nt = t_len // tt

    def relation(j):
        return jnp.where(j < 4, 2 * j + 1, (2 * j - 6) % N_DEV)

    def body(r_ref, da_ref, df_ref, h2_ref, own1_ref, own2_ref, out1_ref, out2_ref,
             acc1, acc2, snd1, snd2, sib1, sib2, send_sems, recv_sems):
        j, t = pl.program_id(0), pl.program_id(1)
        rows = pl.ds(pl.multiple_of(t * tt, tt), tt)
        x, y, c = _place()
        accs, snds, sibs = (acc1, acc2), (snd1, snd2), (sib1, sib2)

        def to_sibling(a, jj):
            return pltpu.make_async_remote_copy(
                src_ref=snds[a].at[jj % 2], dst_ref=sibs[a].at[jj],
                send_sem=send_sems.at[4 * a + jj], recv_sem=recv_sems.at[4 * a + jj],
                device_id=(x, y, 1 - c), device_id_type=MESH)

        @pl.when(t == 0)
        def _():
            acc2[...] = jnp.zeros_like(acc2)
            acc1[...] = jnp.zeros_like(acc1)

        acc2[...] += _dot_tn(r_ref[...], df_ref[rows, :])
        acc1[...] += _dot_tn(h2_ref[rows, :], da_ref[...])

        @pl.when((t == nt - 1) & (j < 4))
        def _():
            for a in range(2):
                @pl.when(j >= 2)
                def _():
                    to_sibling(a, j - 2).wait_send()

                snds[a][j % 2] = accs[a][...].astype(BF16)
                to_sibling(a, j).start()

        @pl.when((t == nt - 1) & (j >= 4))
        def _():
            jj = (j - 3) % 4
            for a, (own_ref, out_ref) in enumerate(((own1_ref, out1_ref), (own2_ref, out2_ref))):
                to_sibling(a, jj).wait_recv()

                @pl.when(j < N_DEV - 1)
                def _():
                    out_ref[0] = (accs[a][...] + sibs[a][jj].astype(F32)).astype(BF16)

                @pl.when(j == N_DEV - 1)
                def _():
                    own_ref[...] = accs[a][...] + sibs[a][jj].astype(F32)
                    to_sibling(a, 2).wait_send()
                    to_sibling(a, 3).wait_send()

    blk = pl.BlockSpec((tt, FF_BLK), lambda j, t: (t, relation(j)))
    chip = lambda j, t: (jnp.clip(j - 4, 0, 2), 0, 0)
    return pl.pallas_call(
        body, name="mlp_wgrad", grid=(N_DEV, nt),
        out_shape=(jax.ShapeDtypeStruct((D, FF_BLK), F32), jax.ShapeDtypeStruct((FF_BLK, D), F32),
                   jax.ShapeDtypeStruct((3, D, FF_BLK), BF16), jax.ShapeDtypeStruct((3, FF_BLK, D), BF16)),
        in_specs=[blk, blk, _resident((t_len, D)), _resident((t_len, D))],
        out_specs=(_full((D, FF_BLK)), _full((FF_BLK, D)),
                   pl.BlockSpec((1, D, FF_BLK), chip), pl.BlockSpec((1, FF_BLK, D), chip)),
        scratch_shapes=[pltpu.VMEM((D, FF_BLK), F32), pltpu.VMEM((FF_BLK, D), F32),
                        pltpu.VMEM((2, D, FF_BLK), BF16), pltpu.VMEM((2, FF_BLK, D), BF16),
                        pltpu.VMEM((4, D, FF_BLK), BF16), pltpu.VMEM((4, FF_BLK, D), BF16),
                        pltpu.SemaphoreType.DMA((8,)), pltpu.SemaphoreType.DMA((8,))],
        compiler_params=pltpu.CompilerParams(dimension_semantics=("arbitrary", "arbitrary"),
                                             vmem_limit_bytes=VMEM_LIMIT),
    )(r, da, df, h2)


def _acc_rows(ref, row0, k, val):
    half = CHUNK // 2
    ref[row0:row0 + half, k * GROUP:(k + 1) * GROUP] += val[:half, :]
    ref[row0:row0 + half, D_A + k * GROUP:D_A + (k + 1) * GROUP] += val[half:, :]


def _attn_bwd(tt, dmix, dx1, x, z, cat, mod, n1pre, w_in_t, w_out, w_sp, bs_rows, ln_g, ln_b, w_pool, b_pool,
              pool_scale, red_fwd, red_bwd, chip_sums):
    t_len = x.shape[0]
    nt = t_len // tt
    hb = tt // HALO
    n_sums = len(chip_sums)

    def body(dmix_ref, dx1_ref, x_ref, z_ref, zprev_ref, cat_ref, mod_ref, n1pre_ref, win_ref, wout_ref, wsp_ref,
             bs_ref, lng_ref, lnb_ref, wp_ref, bp_ref, ps_ref, redf_ref, redb_ref, *rest):
        sum_out = rest[:n_sums]
        gx_ref, gwin_ref, gwout_ref, small_ref = rest[n_sums:n_sums + 4]
        sum_in = rest[n_sums + 4:2 * n_sums + 4]
        carry, acc_in, acc_out, dz_scr, bs_acc, send_sems, recv_sems = rest[2 * n_sums + 4:]
        s = pl.program_id(0)
        i = nt - 1 - s
        px, py, pc = _place()

        def chip_copy(a, r):
            return pltpu.make_async_remote_copy(
                src_ref=sum_out[a].at[r], dst_ref=sum_in[a].at[r],
                send_sem=send_sems.at[3 * a + r], recv_sem=recv_sems.at[3 * a + r],
                device_id=_peer(px, py, pc, 2 * (r + 1)), device_id_type=MESH)

        @pl.when(s == 0)
        def _():
            for a in range(n_sums):
                for r in range(3):
                    chip_copy(a, r).start()
            carry[...] = jnp.zeros_like(carry)
            acc_in[...] = jnp.zeros_like(acc_in)
            acc_out[...] = jnp.zeros_like(acc_out)
            bs_acc[...] = jnp.zeros_like(bs_acc)
            small_ref[...] = jnp.zeros_like(small_ref)
            small_ref[ROW_DMOD + 2:ROW_DMOD + 3, :] = redb_ref[3:4, :]
            small_ref[ROW_DMOD + 3:ROW_DMOD + 5, :] = redb_ref[0:2, :]
            small_ref[ROW_DMOD + 5:ROW_DMOD + 6, :] = redf_ref[0:1, :]
            small_ref[ROW_N1POST:ROW_N1POST + 1, :] = redb_ref[4:5, :]
            small_ref[ROW_N2PRE:ROW_N2PRE + 1, :] = redb_ref[2:3, :]
            small_ref[ROW_N2POST:ROW_N2POST + 1, :] = redf_ref[1:2, :]
            small_ref[ROW_LOSS:ROW_LOSS + 1, :] = redf_ref[2:3, :]

        dmixv = dmix_ref[...]
        dcat = _dot_nt(dmixv, wout_ref[...])
        acc_out[...] += _dot_tn(cat_ref[...], dmixv)

        z = z_ref[...]
        t_g, ga = _gelu_parts(z[:, :2 * D_A])
        u, vr = ga[:, :D_A], ga[:, D_A:]
        dv0 = vr - jnp.mean(vr, axis=-1, keepdims=True)
        rv = lax.rsqrt(jnp.mean(dv0 * dv0, axis=-1, keepdims=True) + EPS)
        vhat = dv0 * rv
        vb = (vhat * lng_ref[...] + lnb_ref[...]).astype(BF16)
        mask = _tril_mask()
        wc = [(wsp_ref[h] * mask).astype(BF16) for h in range(N_HEADS)]

        dya = dcat[:, :D_A]
        for h in range(N_HEADS):
            cols = slice(h * GROUP, (h + 1) * GROUP)
            bs_sum = jnp.zeros((CHUNK, GROUP), F32)
            ws_sum = jnp.zeros((CHUNK, CHUNK), F32)
            for ch in range(tt // CHUNK):
                rows = slice(ch * CHUNK, (ch + 1) * CHUNK)
                v_ch = vb[rows, cols]
                mixed = _dot(wc[h], v_ch) + bs_ref[:, cols]
                dy_ch = dya[rows, cols]
                dz_scr[rows, cols] = dy_ch * mixed
                dmixed = dy_ch * u[rows, cols]
                dmb = dmixed.astype(BF16)
                dz_scr[rows, D_A + h * GROUP:D_A + (h + 1) * GROUP] = _dot_tn(wc[h], dmb)
                bs_sum = bs_sum + dmixed
                ws_sum = ws_sum + _dot_nt(dmb, v_ch)
            _acc_rows(bs_acc, 0, h, bs_sum)
            _acc_rows(small_ref, ROW_WS, h, ws_sum)

        dvl = dz_scr[:, D_A:2 * D_A]
        dvhat = dvl * lng_ref[...]
        dvr = rv * (dvhat - jnp.mean(dvhat, axis=-1, keepdims=True)
                    - vhat * jnp.mean(dvhat * vhat, axis=-1, keepdims=True))
        small_ref[ROW_LN:ROW_LN + 1, 0:D_A] += _colsum(dvl * vhat)
        small_ref[ROW_LN:ROW_LN + 1, D_A:D] += _colsum(dvl)
        dga = jnp.concatenate([dz_scr[:, :D_A], dvr], axis=1)
        dza = dga * _gelu_grad(z[:, :2 * D_A], t_g)

        zb = z[:, 2 * D_A:]
        halo_prev = jnp.where(i == 0, 0.0, zprev_ref[...])
        diff = _pool_diff(zb, halo_prev, i * tt)
        dyb = dcat[:, D_A:]
        inv = _inv_counts(i * tt, tt)
        scaled, ddiffs = [], []
        for g in range(len(WINDOWS)):
            cols = slice(g * GROUP, (g + 1) * GROUP)
            db = diff[g].astype(BF16)
            wpg = wp_ref[g].astype(BF16)
            pre = _dot(db, wpg) + bp_ref[:, cols]
            small_ref[ROW_POOL:ROW_POOL + 1, cols] += _colsum(dyb[:, cols] * pre)
            dpre = dyb[:, cols] * ps_ref[:, cols]
            small_ref[ROW_POOL:ROW_POOL + 1, D_B + g * GROUP:D_B + (g + 1) * GROUP] += _colsum(dpre)
            dpb = dpre.astype(BF16)
            _acc_rows(small_ref, ROW_WP, g, _dot_tn(db, dpb))
            ddiff = _dot_nt(dpb, wpg)
            ddiffs.append(ddiff)
            scaled.append(ddiff * inv[g])
        scaled_all = jnp.concatenate(scaled, axis=1)
        ext = jnp.concatenate([scaled_all, carry[...]], axis=0)
        n_ext = tt + HALO
        s2 = ext + pltpu.roll(ext, n_ext - 1, 0)
        t4 = s2[:, GROUP:]
        s4 = t4 + pltpu.roll(t4, n_ext - 2, 0)
        t8 = s4[:, GROUP:]
        s8 = t8 + pltpu.roll(t8, n_ext - 4, 0)
        t16 = s8[:, GROUP:]
        s16 = t16 + pltpu.roll(t16, n_ext - 8, 0)
        back = [s2[:, :GROUP], s4[:, :GROUP], s8[:, :GROUP], s16]
        carry[...] = scaled_all[:HALO, :]
        dzb = jnp.concatenate([back[g][:tt, :] - ddiffs[g] for g in range(len(WINDOWS))], axis=1)

        dzv = jnp.concatenate([dza, dzb], axis=1).astype(BF16)
        dh1 = _dot(dzv, win_ref[...])
        xv = x_ref[...]
        r1 = _rstd(xv)
        xhat = xv * r1
        shift1, scale1 = mod_ref[0:1, :], mod_ref[1:2, :]
        n1 = xhat * n1pre_ref[...]
        h1 = (n1 * (1.0 + scale1) + shift1).astype(BF16)
        acc_in[...] += _dot_tn(dzv, h1)
        dn1 = dh1 * (1.0 + scale1)
        gx_ref[...] = dx1_ref[...] + _rms_bwd(dn1 * n1pre_ref[...], xhat, r1)
        small_ref[ROW_DMOD:ROW_DMOD + 1, :] += _colsum(dh1)
        small_ref[ROW_DMOD + 1:ROW_DMOD + 2, :] += _colsum(dh1 * n1)
        small_ref[ROW_N1PRE:ROW_N1PRE + 1, :] += _colsum(dn1 * xhat)

        @pl.when(s == nt - 1)
        def _():
            gwin_ref[...] = acc_in[...].astype(BF16)
            gwout_ref[...] = acc_out[...].astype(BF16)
            bs = _unfold(bs_acc[...])
            for h in range(N_HEADS):
                small_ref[ROW_BS + h:ROW_BS + h + 1, 0:GROUP] = jnp.sum(
                    bs[:, h * GROUP:(h + 1) * GROUP].T, axis=0, keepdims=True)
            for a in range(n_sums):
                for r in range(3):
                    chip_copy(a, r).wait_recv()
                    chip_copy(a, r).wait_send()

    rev = lambda w: pl.BlockSpec((tt, w), lambda s: (nt - 1 - s, 0))
    zprev = pl.BlockSpec((HALO, D_B), lambda s: (jnp.maximum((nt - 1 - s) * hb - 1, 0), 2))
    hbm = pl.BlockSpec(memory_space=pl.ANY)
    outs = pl.pallas_call(
        body, name="attn_bwd", grid=(nt,),
        out_shape=tuple([jax.ShapeDtypeStruct((t_len, D), F32), jax.ShapeDtypeStruct((D_Z, D), BF16),
                         jax.ShapeDtypeStruct((D, D), BF16), jax.ShapeDtypeStruct((SMALL_ROWS, D), F32)]
                        + [jax.ShapeDtypeStruct(cs.shape, cs.dtype) for cs in chip_sums]),
        in_specs=[rev(D), rev(D), rev(D), rev(D_Z), zprev, rev(D), _full((8, D)), _full((1, D)),
                  _resident((D_Z, D)), _resident((D, D)), _full((N_HEADS, CHUNK, CHUNK)), _full((CHUNK, D_A)),
                  _full((1, D_A)), _full((1, D_A)), _full((len(WINDOWS), GROUP, GROUP)), _full((1, D_B)),
                  _full((1, D_B)), _full((8, D)), _full((8, D))] + [_resident(cs.shape) for cs in chip_sums],
        out_specs=tuple([rev(D), _resident((D_Z, D)), _resident((D, D)), _full((SMALL_ROWS, D))] + [hbm] * n_sums),
        scratch_shapes=[pltpu.VMEM((HALO, D_B), F32), pltpu.VMEM((D_Z, D), F32), pltpu.VMEM((D, D), F32),
                        pltpu.VMEM((tt, 2 * D_A), F32), pltpu.VMEM((CHUNK // 2, D), F32),
                        pltpu.SemaphoreType.DMA((3 * n_sums,)), pltpu.SemaphoreType.DMA((3 * n_sums,))],
        compiler_params=pltpu.CompilerParams(dimension_semantics=("arbitrary",), vmem_limit_bytes=VMEM_LIMIT),
    )(dmix, dx1, x, z, z, cat, mod, n1pre, w_in_t, w_out, w_sp, bs_rows, ln_g, ln_b, w_pool, b_pool, pool_scale,
      red_fwd, red_bwd, *chip_sums)
    return outs[:4], outs[4:]


def _adam(w, g, m, v):
    m2 = ADAM_B1 * m + (1.0 - ADAM_B1) * g
    v2 = ADAM_B2 * v + (1.0 - ADAM_B2) * (g * g)
    m_hat = m2 / (1.0 - ADAM_B1 ** ADAM_STEP)
    v_hat = v2 / (1.0 - ADAM_B2 ** ADAM_STEP)
    delta = -ADAM_LR * (m_hat / (jnp.sqrt(v_hat) + ADAM_EPS) + ADAM_WD * w)
    return delta, m2, v2


def _adamw_shard(name, rb, w, g, m, v):
    rows, cols = w.shape

    def body(w_ref, g_ref, m_ref, v_ref, d_ref, m2_ref, v2_ref):
        d_ref[...], m2_ref[...], v2_ref[...] = _adam(w_ref[...], g_ref[...], m_ref[...], v_ref[...])

    blk = pl.BlockSpec((rb, cols), lambda i: (i, 0))
    shp = jax.ShapeDtypeStruct((rows, cols), F32)
    return pl.pallas_call(
        body, name=name, grid=(rows // rb,), out_shape=(shp, shp, shp),
        in_specs=[blk] * 4, out_specs=(blk, blk, blk),
        compiler_params=pltpu.CompilerParams(dimension_semantics=("arbitrary",)),
    )(w, g, m, v)


def _adamw_ada(rb, w, sc, dmod_cols, m, v):
    rows, cols = w.shape

    def body(w_ref, sc_ref, dm_ref, m_ref, v_ref, g_ref, d_ref, m2_ref, v2_ref):
        g = _dot_tn(sc_ref[...].astype(BF16), dm_ref[...].astype(BF16))
        g_ref[...] = g
        d_ref[...], m2_ref[...], v2_ref[...] = _adam(w_ref[...], g, m_ref[...], v_ref[...])

    blk = pl.BlockSpec((rb, cols), lambda i: (i, 0))
    shp = jax.ShapeDtypeStruct((rows, cols), F32)
    return pl.pallas_call(
        body, name="adamw_ada", grid=(rows // rb,), out_shape=(shp, shp, shp, shp),
        in_specs=[blk, pl.BlockSpec((N_DEV, rb), lambda i: (0, i)), _full((N_DEV, cols)), blk, blk],
        out_specs=(blk, blk, blk, blk),
        compiler_params=pltpu.CompilerParams(dimension_semantics=("arbitrary",)),
    )(w, sc, dmod_cols, m, v)


def _unfold(acc_rows):
    return jnp.concatenate([acc_rows[:, :D_A], acc_rows[:, D_A:]], axis=0)


def _adamw_small(total, params):
    n = len(params)
    flat = [a for p in params for a in p]

    def body(*refs):
        s_ref = refs[0]
        p_refs = refs[1:1 + 3 * n]
        loss_ref = refs[1 + 3 * n]
        o_refs = refs[2 + 3 * n:]
        d_b_ada = s_ref[0:6, :]
        for b in range(1, N_DEV):
            d_b_ada = d_b_ada + s_ref[8 * b:8 * b + 6, :]
        tot = s_ref[PACK_SHIFT:PACK_ROWS, :]
        loss_ref[...] = jnp.broadcast_to(tot[ROW_LOSS:ROW_LOSS + 1, 0:GROUP], (8, GROUP))
        mask = _tril_mask()
        ws = _unfold(tot[ROW_WS:ROW_WS + 64, :])
        wp = _unfold(tot[ROW_WP:ROW_WP + 64, :])
        grads = [
            d_b_ada,
            tot[ROW_N1PRE:ROW_N1PRE + 1, :], tot[ROW_N1POST:ROW_N1POST + 1, :],
            tot[ROW_N2PRE:ROW_N2PRE + 1, :], tot[ROW_N2POST:ROW_N2POST + 1, :],
            tot[ROW_LN:ROW_LN + 1, :D_A], tot[ROW_LN:ROW_LN + 1, D_A:],
            tot[ROW_POOL:ROW_POOL + 1, :D_B], tot[ROW_POOL:ROW_POOL + 1, D_B:],
            tot[ROW_BS:ROW_BS + N_HEADS, 0:GROUP],
            jnp.stack([ws[:, h * GROUP:(h + 1) * GROUP] * mask for h in range(N_HEADS)]),
            jnp.stack([wp[:, g * GROUP:(g + 1) * GROUP] for g in range(len(WINDOWS))]),
        ]
        for k in range(n):
            w_ref, m_ref, v_ref = p_refs[3 * k:3 * k + 3]
            g = grads[k]
            o_refs[4 * k][...] = g
            o_refs[4 * k + 1][...], o_refs[4 * k + 2][...], o_refs[4 * k + 3][...] = _adam(
                w_ref[...], g, m_ref[...], v_ref[...])

    vm = pl.BlockSpec(memory_space=pltpu.VMEM)
    out_shape = [jax.ShapeDtypeStruct((8, GROUP), F32)]
    for w, _, _ in params:
        out_shape += [jax.ShapeDtypeStruct(w.shape, F32)] * 4
    return pl.pallas_call(
        body, name="adamw_small", out_shape=tuple(out_shape),
        in_specs=[vm] * (1 + 3 * n), out_specs=tuple([vm] * len(out_shape)),
    )(total, *flat)


TT_ATTN_FWD = 512
TT_MLP_FWD = 512
TT_MLP = 256
TT_WGRAD = 2048
TT_ATTN_BWD = 512


def kernel(x, c, w_ada, b_ada, norm1_pre, norm1_post, w_in, w_spatial, b_spatial, ln_v_gain, ln_v_bias, w_pool, b_pool, pool_scale, w_out, norm2_pre, norm2_post, w_fc1, w_fc2, loss_target, m_w_ada, m_b_ada, m_norm1_pre, m_norm1_post, m_w_in, m_w_spatial, m_b_spatial, m_ln_v_gain, m_ln_v_bias, m_w_pool, m_b_pool, m_pool_scale, m_w_out, m_norm2_pre, m_norm2_post, m_w_fc1, m_w_fc2, v_w_ada, v_b_ada, v_norm1_pre, v_norm1_post, v_w_in, v_w_spatial, v_b_spatial, v_ln_v_gain, v_ln_v_bias, v_w_pool, v_b_pool, v_pool_scale, v_w_out, v_norm2_pre, v_norm2_post, v_w_fc1, v_w_fc2):
    t_len = x.shape[1]
    me = 4 * lax.axis_index("x") + 2 * lax.axis_index("y") + lax.axis_index("c")
    ada_cols = w_ada.shape[1]
    tt = lambda want: min(want, t_len)

    x2 = x.reshape(t_len, D)
    tgt = loss_target.reshape(t_len, D)
    row = lambda a: a.reshape(1, -1)

    b_my = lax.dynamic_slice_in_dim(b_ada, me * ada_cols, ada_cols).reshape(1, ada_cols)
    modp, sc, (g_in, g_out), fc_shards = _fwd_comm(jnp.broadcast_to(c, (8, D)), w_ada, b_my,
                                                   [w_in.T, w_out], [w_fc1, w_fc2])
    mod = jnp.concatenate([modp.reshape(6, D), jnp.zeros((2, D), F32)], axis=0)
    w_in_t = g_in.reshape(D_Z, D)
    w_out_all = g_out.reshape(D, D)

    bs_rows = jnp.repeat(b_spatial.T, GROUP, axis=1)
    attn_consts = (w_spatial, bs_rows, row(ln_v_gain), row(ln_v_bias), w_pool, row(b_pool), row(pool_scale))

    (z, cat, mix, x1), (w1_early, w2_early) = _attn_fwd(
        tt(TT_ATTN_FWD), x2, mod, row(norm1_pre), row(norm1_post), w_in_t, w_out_all, *attn_consts, fc_shards)
    (r_early, h2, f_early), (w1_late, w2_late) = _mlp_fwd_early(
        tt(TT_MLP_FWD), x1, mod, row(norm2_pre), w1_early, w2_early)
    r, df, dy, red_fwd = _mlp_fwd_late(tt(TT_MLP_FWD), r_early, x1, h2, f_early, tgt, mod, row(norm2_post),
                                       w1_late, w2_late)
    da, dmix, dx1, red_bwd = _mlp_bwd(tt(TT_MLP), df, r, dy, x1, mix, mod, row(norm2_pre), row(norm1_post),
                                      w1_early, w2_early, w1_late, w2_late)
    own_w1, own_w2, sums_w1, sums_w2 = _mlp_wgrad(tt(TT_WGRAD), r, da, df, h2)
    (grad_x, p_in, p_out, small), (arr_w1, arr_w2) = _attn_bwd(
        tt(TT_ATTN_BWD), dmix, dx1, x2, z, cat, mod, row(norm1_pre), w_in_t, w_out_all, *attn_consts,
        red_fwd, red_bwd, [sums_w1, sums_w2])
    (grad_in_t, grad_out, total), ((grad_w1, d_w1, m_w1, v_w1), (grad_w2, d_w2, m_w2, v_w2)) = _tail_comm(
        [p_in.reshape(N_DEV, D_Z // N_DEV, D), p_out.reshape(N_DEV, D // N_DEV, D)], small, 64,
        [(w_fc1, own_w1, arr_w1, m_w_fc1, v_w_fc1), (w_fc2, own_w2, arr_w2, m_w_fc2, v_w_fc2)])

    d_out, m_out, v_out = _adamw_shard("adamw_out", 128, w_out, grad_out, m_w_out, v_w_out)
    d_in_t, m_in_t, v_in_t = _adamw_shard("adamw_in", D_Z // N_DEV, w_in.T, grad_in_t, m_w_in.T, v_w_in.T)
    dmod_all = total[0:TABLE_ROWS, :].reshape(N_DEV, 8, D)[:, :6, :].reshape(N_DEV, 6 * D)
    dmod_cols = lax.dynamic_slice_in_dim(dmod_all, me * ada_cols, ada_cols, axis=1)
    grad_ada, d_ada, m_ada, v_ada = _adamw_ada(256, w_ada, sc, dmod_cols, m_w_ada, v_w_ada)

    six = lambda a: a.reshape(6, D)
    small_params = [
        (six(b_ada), six(m_b_ada), six(v_b_ada)),
        (row(norm1_pre), row(m_norm1_pre), row(v_norm1_pre)),
        (row(norm1_post), row(m_norm1_post), row(v_norm1_post)),
        (row(norm2_pre), row(m_norm2_pre), row(v_norm2_pre)),
        (row(norm2_post), row(m_norm2_post), row(v_norm2_post)),
        (row(ln_v_gain), row(m_ln_v_gain), row(v_ln_v_gain)),
        (row(ln_v_bias), row(m_ln_v_bias), row(v_ln_v_bias)),
        (row(pool_scale), row(m_pool_scale), row(v_pool_scale)),
        (row(b_pool), row(m_b_pool), row(v_b_pool)),
        (b_spatial, m_b_spatial, v_b_spatial),
        (w_spatial, m_w_spatial, v_w_spatial),
        (w_pool, m_w_pool, v_w_pool),
    ]
    outs = _adamw_small(total, small_params)
    loss = outs[0][0, 0]
    names = ["b_ada", "norm1_pre", "norm1_post", "norm2_pre", "norm2_post", "ln_v_gain", "ln_v_bias", "pool_scale",
             "b_pool", "b_spatial", "w_spatial", "w_pool"]
    shapes = dict(b_ada=b_ada.shape, norm1_pre=norm1_pre.shape, norm1_post=norm1_post.shape,
                  norm2_pre=norm2_pre.shape, norm2_post=norm2_post.shape, ln_v_gain=ln_v_gain.shape,
                  ln_v_bias=ln_v_bias.shape, pool_scale=pool_scale.shape, b_pool=b_pool.shape,
                  b_spatial=b_spatial.shape, w_spatial=w_spatial.shape, w_pool=w_pool.shape)
    res = {}
    for k, nm in enumerate(names):
        res[nm] = tuple(o.reshape(shapes[nm]) for o in outs[1 + 4 * k:5 + 4 * k])
    res["w_ada"] = (grad_ada, d_ada, m_ada, v_ada)
    res["w_in"] = (grad_in_t.T, d_in_t.T, m_in_t.T, v_in_t.T)
    res["w_out"] = (grad_out, d_out, m_out, v_out)
    res["w_fc1"] = (grad_w1, d_w1, m_w1, v_w1)
    res["w_fc2"] = (grad_w2, d_w2, m_w2, v_w2)

    order = ["w_ada", "b_ada", "norm1_pre", "norm1_post", "w_in", "w_spatial", "b_spatial", "ln_v_gain", "ln_v_bias",
             "w_pool", "b_pool", "pool_scale", "w_out", "norm2_pre", "norm2_post", "w_fc1", "w_fc2"]
    return (loss, grad_x.reshape(x.shape),
            *[res[nm][0] for nm in order], *[res[nm][1] for nm in order],
            *[res[nm][2] for nm in order], *[res[nm][3] for nm in order])
```

```python
import functools

import jax
import jax.numpy as jnp
from jax import lax
from jax.experimental import pallas as pl
from jax.experimental.pallas import tpu as pltpu

F32 = jnp.float32
BF16 = jnp.bfloat16
MESH = pl.DeviceIdType.MESH

N_DEV = 8
D = 1024
D_A = 512
D_B = 512
D_Z = 2 * D_A + D_B
N_HEADS = 4
CHUNK = 128
WINDOWS = (2, 4, 8, 16)
GROUP = 128
D_FF = 4096
FF_BLK = D_FF // N_DEV
HALO = 16
EPS = 1e-6
VMEM_LIMIT = 60 * 1024 * 1024

ADAM_LR = 0.001
ADAM_B1 = 0.9
ADAM_B2 = 0.999
ADAM_EPS = 1e-08
ADAM_WD = 0.01
ADAM_STEP = 10

ROW_DMOD = 0
ROW_N1PRE, ROW_N1POST, ROW_N2PRE, ROW_N2POST = 8, 9, 10, 11
ROW_LN = 12
ROW_POOL = 13
ROW_LOSS = 14
ROW_BS = 16
ROW_WS = 24
ROW_WP = 88
SMALL_ROWS = 152
TABLE_ROWS = 8 * N_DEV
PACK_SHIFT = TABLE_ROWS - 8
PACK_ROWS = SMALL_ROWS + PACK_SHIFT
PACK_HALF = PACK_ROWS // 2


def _dot(a, b):
    return jnp.dot(a, b, preferred_element_type=F32)


def _dot_nt(a, b):
    return lax.dot_general(a, b, (((1,), (1,)), ((), ())), preferred_element_type=F32)


def _dot_tn(a, b):
    return lax.dot_general(a, b, (((0,), (0,)), ((), ())), preferred_element_type=F32)


def _rstd(v):
    return lax.rsqrt(jnp.mean(v * v, axis=-1, keepdims=True) + EPS)


def _rms_bwd(d_hat, hat, rstd):
    return rstd * (d_hat - hat * jnp.mean(d_hat * hat, axis=-1, keepdims=True))


_K0 = 0.7978845608028654
_K1 = 0.044715


def _gelu_parts(v):
    t = jnp.tanh(_K0 * (v + _K1 * (v * v * v)))
    return t, v * (0.5 * (1.0 + t))


def _gelu_grad(v, t):
    return 0.5 * (1.0 + t) + (0.5 * v) * (1.0 - t * t) * (_K0 * (1.0 + (3.0 * _K1) * (v * v)))


def _colsum(v):
    return jnp.sum(v, axis=0, keepdims=True)


def _full(shape):
    n = len(shape)
    return pl.BlockSpec(shape, lambda *_: (0,) * n)


def _resident(shape):
    n = len(shape)
    return pl.BlockSpec(shape, lambda *_: (0,) * n, pipeline_mode=pl.Buffered(1))


def _place():
    x, y, c = lax.axis_index("x"), lax.axis_index("y"), lax.axis_index("c")
    return x, y, c


def _flip(v, bit):
    return 1 - v if bit else v


def _peer(x, y, c, k):
    return (_flip(x, (k >> 2) & 1), _flip(y, (k >> 1) & 1), _flip(c, k & 1))


def _index(p):
    return 4 * p[0] + 2 * p[1] + p[2]


def _two_level_gather_begin(x, y, c, out_refs, send_sems, recv_sems):
    me = (x, y, c)
    sibling = (x, y, 1 - c)
    chips = [(1 - x, y), (x, 1 - y), (1 - x, 1 - y)]

    def copy(a, k, block, to):
        ref = out_refs[a].at[_index(block)]
        return pltpu.make_async_remote_copy(
            src_ref=ref, dst_ref=ref, send_sem=send_sems.at[7 * a + k], recv_sem=recv_sems.at[7 * a + k],
            device_id=to, device_id_type=MESH)

    first = []
    for a in range(len(out_refs)):
        first.append(copy(a, 0, me, sibling))
        first += [copy(a, 1 + j, me, (*chip, c)) for j, chip in enumerate(chips)]
    for cp in first:
        cp.start()
    return copy, first, me, sibling, chips


def _two_level_gather_finish(c, n, begun):
    copy, first, me, sibling, chips = begun
    passed = []
    for a in range(n):
        for j, chip in enumerate(chips):
            copy(a, 1 + j, (*chip, c), me).wait_recv()
            fwd = copy(a, 4 + j, (*chip, c), sibling)
            fwd.start()
            passed.append(fwd)
    for a in range(n):
        copy(a, 0, sibling, me).wait_recv()
        for j, chip in enumerate(chips):
            copy(a, 4 + j, (*chip, 1 - c), me).wait_recv()
    for cp in first + passed:
        cp.wait_send()


def _fwd_comm(c8, w_ada, b_my, gathered, kept):
    ncol = w_ada.shape[1]
    n_g, n_k = len(gathered), len(kept)

    def body(c_ref, w_ref, b_ref, *rest):
        g_in, k_in = rest[:n_g], rest[n_g:n_g + n_k]
        modp_ref, sc_ref = rest[n_g + n_k:n_g + n_k + 2]
        g_out = rest[n_g + n_k + 2:2 * n_g + n_k + 2]
        k_out = rest[2 * n_g + n_k + 2:2 * n_g + 2 * n_k + 2]
        cg, mg, part, send_sems, recv_sems, g_send, g_recv = rest[2 * n_g + 2 * n_k + 2:]
        x, y, c = _place()
        me = _index((x, y, c))
        for a in range(n_g):
            g_out[a][me] = g_in[a][...].astype(BF16)
        begun = _two_level_gather_begin(x, y, c, g_out, g_send, g_recv)
        for a in range(n_k):
            k_out[a][...] = k_in[a][...].astype(BF16)

        def c_copy(k):
            p = _peer(x, y, c, k)
            return pltpu.make_async_remote_copy(
                src_ref=c_ref, dst_ref=cg.at[me], send_sem=send_sems.at[k - 1], recv_sem=recv_sems.at[k - 1],
                device_id=p, device_id_type=MESH)

        def c_arrival(k):
            p = _peer(x, y, c, k)
            return pltpu.make_async_remote_copy(
                src_ref=c_ref, dst_ref=cg.at[_index(p)], send_sem=send_sems.at[k - 1], recv_sem=recv_sems.at[k - 1],
                device_id=p, device_id_type=MESH)

        def m_copy(k):
            p = _peer(x, y, c, k)
            return pltpu.make_async_remote_copy(
                src_ref=part, dst_ref=mg.at[me], send_sem=send_sems.at[6 + k], recv_sem=recv_sems.at[6 + k],
                device_id=p, device_id_type=MESH)

        def m_arrival(k):
            p = _peer(x, y, c, k)
            return pltpu.make_async_remote_copy(
                src_ref=part, dst_ref=mg.at[_index(p)], send_sem=send_sems.at[6 + k], recv_sem=recv_sems.at[6 + k],
                device_id=p, device_id_type=MESH)

        for k in range(1, N_DEV):
            c_copy(k).start()
        cg[me] = c_ref[...]
        for k in range(1, N_DEV):
            c_arrival(k).wait_recv()
        c_all = jnp.concatenate([cg[j, 0:1, :] for j in range(N_DEV)], axis=0)
        sc = c_all * jax.nn.sigmoid(c_all)
        sc_ref[...] = sc
        part[...] = _dot(sc.astype(BF16), w_ref[...].astype(BF16)) + b_ref[...]
        for k in range(1, N_DEV):
            m_copy(k).start()
        mg[me] = part[...]
        for k in range(1, N_DEV):
            m_arrival(k).wait_recv()
        for j in range(N_DEV):
            modp_ref[j:j + 1, :] = mg[j, pl.ds(me, 1), :]
        _two_level_gather_finish(c, n_g, begun)
        for k in range(1, N_DEV):
            c_copy(k).wait_send()
            m_copy(k).wait_send()

    vm = pl.BlockSpec(memory_space=pltpu.VMEM)
    outs = pl.pallas_call(
        body, name="fwd_comm",
        out_shape=tuple([jax.ShapeDtypeStruct((N_DEV, ncol), F32), jax.ShapeDtypeStruct((N_DEV, D), F32)]
                        + [jax.ShapeDtypeStruct((N_DEV,) + s.shape, BF16) for s in gathered]
                        + [jax.ShapeDtypeStruct(s.shape, BF16) for s in kept]),
        in_specs=[vm] * (3 + n_g + n_k), out_specs=tuple([vm] * (2 + n_g + n_k)),
        scratch_shapes=[
            pltpu.VMEM((N_DEV, 8, D), F32),
            pltpu.VMEM((N_DEV, N_DEV, ncol), F32),
            pltpu.VMEM((N_DEV, ncol), F32),
            pltpu.SemaphoreType.DMA((2 * (N_DEV - 1),)),
            pltpu.SemaphoreType.DMA((2 * (N_DEV - 1),)),
            pltpu.SemaphoreType.DMA((7 * n_g,)),
            pltpu.SemaphoreType.DMA((7 * n_g,)),
        ],
        compiler_params=pltpu.CompilerParams(vmem_limit_bytes=VMEM_LIMIT),
    )(c8, w_ada, b_my, *gathered, *kept)
    return outs[0], outs[1], outs[2:2 + n_g], outs[2 + n_g:]


FC_EARLY = 6


class _Copies:
    def __init__(self, entries, send_sems, recv_sems):
        self.place = _place()
        self.entries, self.send_sems, self.recv_sems = entries, send_sems, recv_sems

    def _copy(self, i, arrival=False):
        src, dst, rel = self.entries[i]
        return pltpu.make_async_remote_copy(
            src_ref=dst if arrival else src, dst_ref=dst, send_sem=self.send_sems.at[i],
            recv_sem=self.recv_sems.at[i], device_id=_peer(*self.place, rel), device_id_type=MESH)

    def start(self, *which):
        for i in which:
            self._copy(i).start()

    def wait_recv(self, *which):
        for i in which:
            self._copy(i, arrival=True).wait_recv()

    def wait_send(self, *which):
        for i in which:
            self._copy(i).wait_send()


TAIL_STEPS = 8


def _tail_comm(parts, small, row_chunk, fc):
    n, n_fc = len(parts), len(fc)

    def body(*refs):
        p_refs, small_ref = refs[:n], refs[n]
        fc_in = refs[n + 1:n + 1 + 6 * n_fc]
        outs = refs[n + 1 + 6 * n_fc:]
        g_refs, total_ref = outs[:n], outs[n]
        fc_out = outs[n + 1:n + 1 + 4 * n_fc]
        scr = outs[n + 1 + 4 * n_fc:]
        from_sib = scr[0:n]
        chip_out = scr[n:2 * n]
        chip_in = scr[2 * n:3 * n]
        pack, pack_sib, halves, total_scr = scr[3 * n:3 * n + 4]
        send_a, recv_a, send_b, recv_b, send_s, recv_s = scr[3 * n + 4:]
        step = pl.program_id(0)
        x, y, c = _place()
        me = _index((x, y, c))
        sibling = (x, y, 1 - c)
        my_chip = 2 * x + y
        others = [(1 - x, y), (x, 1 - y), (1 - x, 1 - y)]
        my_half = pl.ds(pl.multiple_of(PACK_HALF * c, 8), PACK_HALF)

        def pack_to_sibling():
            return pltpu.make_async_remote_copy(
                src_ref=pack, dst_ref=pack_sib, send_sem=send_s.at[0], recv_sem=recv_s.at[0],
                device_id=sibling, device_id_type=MESH)

        def half_to_chip(r):
            return pltpu.make_async_remote_copy(
                src_ref=halves.at[my_chip], dst_ref=halves.at[my_chip],
                send_sem=send_s.at[1 + r], recv_sem=recv_s.at[1 + r],
                device_id=(*others[r], c), device_id_type=MESH)

        def half_from_chip(r):
            k = 2 * others[r][0] + others[r][1]
            return pltpu.make_async_remote_copy(
                src_ref=halves.at[k], dst_ref=halves.at[k], send_sem=send_s.at[1 + r], recv_sem=recv_s.at[1 + r],
                device_id=(*others[r], c), device_id_type=MESH)

        def total_to_sibling():
            return pltpu.make_async_remote_copy(
                src_ref=total_scr.at[my_half], dst_ref=total_scr.at[my_half],
                send_sem=send_s.at[4], recv_sem=recv_s.at[4], device_id=sibling, device_id_type=MESH)

        def total_from_sibling():
            sib_half = pl.ds(pl.multiple_of(PACK_HALF * (1 - c), 8), PACK_HALF)
            return pltpu.make_async_remote_copy(
                src_ref=total_scr.at[sib_half], dst_ref=total_scr.at[sib_half],
                send_sem=send_s.at[4], recv_sem=recv_s.at[4], device_id=sibling, device_id_type=MESH)

        def to_sibling(a, k):
            return pltpu.make_async_remote_copy(
                src_ref=p_refs[a].at[2 * k + (1 - c)], dst_ref=from_sib[a].at[k],
                send_sem=send_a.at[a], recv_sem=recv_a.at[a], device_id=sibling, device_id_type=MESH)

        def all_from_sibling(a):
            return pltpu.make_async_remote_copy(
                src_ref=from_sib[a], dst_ref=from_sib[a], send_sem=send_a.at[a], recv_sem=recv_a.at[a],
                device_id=sibling, device_id_type=MESH)

        def to_chip(a, r):
            return pltpu.make_async_remote_copy(
                src_ref=chip_out[a].at[r], dst_ref=chip_in[a].at[r],
                send_sem=send_b.at[3 * a + r], recv_sem=recv_b.at[3 * a + r],
                device_id=(*others[r], c), device_id_type=MESH)

        @pl.when(step == 0)
        def _():
            pack[0:TABLE_ROWS, :] = jnp.zeros((TABLE_ROWS, D), F32)
            pack[pl.ds(pl.multiple_of(8 * me, 8), 8), :] = small_ref[0:8, :]
            pack[TABLE_ROWS:PACK_ROWS, :] = small_ref[8:SMALL_ROWS, :]
            pack_to_sibling().start()
            for a in range(n):
                for k in range(4):
                    to_sibling(a, k).start()

        @pl.when(step == 1)
        def _():
            pack_to_sibling().wait_recv()
            halves[my_chip] = pack[my_half, :] + pack_sib[my_half, :]
            for r in range(3):
                half_to_chip(r).start()
            for a in range(n):
                all_from_sibling(a).wait_recv()
                rows = p_refs[a].shape[1]
                for r in range(3):
                    k = 2 * others[r][0] + others[r][1]
                    for s in range(0, rows, row_chunk):
                        sl = pl.ds(s, row_chunk)
                        chip_out[a][r, sl, :] = (p_refs[a][2 * k + c, sl, :].astype(F32)
                                                 + from_sib[a][k, sl, :].astype(F32)).astype(BF16)
                    to_chip(a, r).start()
                for s in range(0, rows, row_chunk):
                    sl = pl.ds(s, row_chunk)
                    g_refs[a][sl, :] = (p_refs[a][2 * my_chip + c, sl, :].astype(F32)
                                        + from_sib[a][my_chip, sl, :].astype(F32))

        for k in range(n_fc):
            w_ref, own_ref, arr_ref, diag_ref, m_ref, v_ref = fc_in[6 * k:6 * k + 6]
            g = own_ref[...]
            for r in range(2):
                g = g + arr_ref[r].astype(F32)
            g = g + diag_ref[...].astype(F32)
            fc_out[4 * k][...] = g
            fc_out[4 * k + 1][...], fc_out[4 * k + 2][...], fc_out[4 * k + 3][...] = _adam(
                w_ref[...], g, m_ref[...], v_ref[...])

        @pl.when(step == TAIL_STEPS - 1)
        def _():
            for r in range(3):
                half_from_chip(r).wait_recv()
            total_scr[my_half, :] = ((halves[0] + halves[1]) + halves[2]) + halves[3]
            total_to_sibling().start()
            for a in range(n):
                rows = p_refs[a].shape[1]
                for r in range(3):
                    to_chip(a, r).wait_recv()
                    for s in range(0, rows, row_chunk):
                        sl = pl.ds(s, row_chunk)
                        g_refs[a][sl, :] = g_refs[a][sl, :] + chip_in[a][r, sl, :].astype(F32)
            total_from_sibling().wait_recv()
            total_ref[...] = total_scr[...]
            for a in range(n):
                all_from_sibling(a).wait_send()
                for r in range(3):
                    to_chip(a, r).wait_send()
            pack_to_sibling().wait_send()
            for r in range(3):
                half_to_chip(r).wait_send()
            total_to_sibling().wait_send()

    fc_specs_in, fc_specs_out, fc_shapes, fc_args = [], [], [], []
    for w, own, arrived, diagonal, m, v in fc:
        rows, cols = w.shape
        blk = pl.BlockSpec((rows // TAIL_STEPS, cols), lambda i: (i, 0))
        fc_specs_in += [blk, blk, pl.BlockSpec((2, rows // TAIL_STEPS, cols), lambda i: (0, i, 0)), blk, blk, blk]
        fc_specs_out += [blk] * 4
        fc_shapes += [jax.ShapeDtypeStruct((rows, cols), F32)] * 4
        fc_args += [w, own, arrived, diagonal, m, v]
    outs = pl.pallas_call(
        body, name="tail_comm", grid=(TAIL_STEPS,),
        out_shape=tuple([jax.ShapeDtypeStruct(p.shape[1:], F32) for p in parts]
                        + [jax.ShapeDtypeStruct((PACK_ROWS, D), F32)] + fc_shapes),
        in_specs=[_resident(p.shape) for p in parts] + [_resident(small.shape)] + fc_specs_in,
        out_specs=tuple([_full(p.shape[1:]) for p in parts] + [_full((PACK_ROWS, D))] + fc_specs_out),
        scratch_shapes=(
            [pltpu.VMEM((4,) + p.shape[1:], BF16) for p in parts]
            + [pltpu.VMEM((3,) + p.shape[1:], BF16) for p in parts]
            + [pltpu.VMEM((3,) + p.shape[1:], BF16) for p in parts]
            + [pltpu.VMEM((PACK_ROWS, D), F32), pltpu.VMEM((PACK_ROWS, D), F32),
               pltpu.VMEM((4, PACK_HALF, D), F32), pltpu.VMEM((PACK_ROWS, D), F32)]
            + [pltpu.SemaphoreType.DMA((n,)), pltpu.SemaphoreType.DMA((n,)),
               pltpu.SemaphoreType.DMA((3 * n,)), pltpu.SemaphoreType.DMA((3 * n,)),
               pltpu.SemaphoreType.DMA((5,)), pltpu.SemaphoreType.DMA((5,))]),
        compiler_params=pltpu.CompilerParams(dimension_semantics=("arbitrary",), vmem_limit_bytes=VMEM_LIMIT),
    )(*parts, small, *fc_args)
    return outs[:n + 1], [outs[n + 1 + 4 * k:n + 5 + 4 * k] for k in range(n_fc)]


def _tril_mask():
    row = lax.broadcasted_iota(jnp.int32, (CHUNK, CHUNK), 0)
    col = lax.broadcasted_iota(jnp.int32, (CHUNK, CHUNK), 1)
    return (col <= row).astype(F32)


def _window_sums(ext):
    s2 = ext + pltpu.roll(ext, 1, 0)
    t4 = s2[:, GROUP:]
    s4 = t4 + pltpu.roll(t4, 2, 0)
    t8 = s4[:, GROUP:]
    s8 = t8 + pltpu.roll(t8, 4, 0)
    t16 = s8[:, GROUP:]
    s16 = t16 + pltpu.roll(t16, 8, 0)
    return [s2[:, :GROUP], s4[:, :GROUP], s8[:, :GROUP], s16]


def _inv_counts(first_pos, rows):
    pos = first_pos + lax.broadcasted_iota(jnp.int32, (rows, 1), 0)
    return [1.0 / jnp.minimum(pos + 1, w).astype(F32) for w in WINDOWS]


def _pool_diff(zb, halo, first_pos):
    tt = zb.shape[0]
    sums = _window_sums(jnp.concatenate([halo, zb], axis=0))
    inv = _inv_counts(first_pos, tt)
    return [sums[g][HALO:, :] * inv[g] - zb[:, g * GROUP:(g + 1) * GROUP] for g in range(len(WINDOWS))]


def _attn_fwd(tt, x, mod, n1pre, n1post, w_in_t, w_out, w_sp, bs_rows, ln_g, ln_b, w_pool, b_pool, pool_scale,
              fc_shards):
    t_len = x.shape[0]
    nt = t_len // tt

    def body(x_ref, mod_ref, n1pre_ref, n1post_ref, win_ref, wout_ref, wsp_ref, bs_ref, lng_ref, lnb_ref,
             wp_ref, bp_ref, ps_ref, w1_ref, w2_ref, z_ref, cat_ref, mix_ref, x1_ref, e1_ref, e2_ref,
             carry, land1, land2, send_sems, recv_sems, local_sems):
        i = pl.program_id(0)
        copies = _Copies(
            [(w1_ref, e1_ref.at[1], 1), (w2_ref, e2_ref.at[1], 1),
             (w1_ref, land1.at[0], 2), (w2_ref, land2.at[0], 2),
             (w1_ref, land1.at[1], 4), (w2_ref, land2.at[1], 4),
             (land1.at[0], e1_ref.at[3], 1), (land2.at[0], e2_ref.at[3], 1),
             (land1.at[1], e1_ref.at[5], 1), (land2.at[1], e2_ref.at[5], 1)],
            send_sems, recv_sems)
        keep = [pltpu.make_async_copy(w1_ref, e1_ref.at[0], local_sems.at[0]),
                pltpu.make_async_copy(w2_ref, e2_ref.at[0], local_sems.at[1]),
                pltpu.make_async_copy(land1.at[0], e1_ref.at[2], local_sems.at[2]),
                pltpu.make_async_copy(land1.at[1], e1_ref.at[4], local_sems.at[3]),
                pltpu.make_async_copy(land2.at[0], e2_ref.at[2], local_sems.at[4]),
                pltpu.make_async_copy(land2.at[1], e2_ref.at[4], local_sems.at[5])]

        @pl.when(i == 0)
        def _():
            copies.start(2, 4, 3, 5, 0, 1)
            keep[0].start()
            keep[1].start()
            carry[...] = jnp.zeros_like(carry)

        @pl.when(i == nt // 2)
        def _():
            copies.wait_recv(2, 4)
            copies.start(6, 8)
            keep[2].start()
            keep[3].start()

        @pl.when(i == nt - 1)
        def _():
            copies.wait_recv(3, 5)
            copies.start(7, 9)
            keep[4].start()
            keep[5].start()

        xv = x_ref[...]
        shift1, scale1, gate1 = mod_ref[0:1, :], mod_ref[1:2, :], mod_ref[2:3, :]
        h1 = (xv * _rstd(xv) * n1pre_ref[...]) * (1.0 + scale1) + shift1
        z = _dot_nt(h1.astype(BF16), win_ref[...])
        z_ref[...] = z

        _, ga = _gelu_parts(z[:, :2 * D_A])
        u, vr = ga[:, :D_A], ga[:, D_A:]
        dv = vr - jnp.mean(vr, axis=-1, keepdims=True)
        v = (dv * lax.rsqrt(jnp.mean(dv * dv, axis=-1, keepdims=True) + EPS)) * lng_ref[...] + lnb_ref[...]
        vb = v.astype(BF16)
        mask = _tril_mask()
        wc = [(wsp_ref[h] * mask).astype(BF16) for h in range(N_HEADS)]
        for ch in range(tt // CHUNK):
            rows = slice(ch * CHUNK, (ch + 1) * CHUNK)
            for h in range(N_HEADS):
                cols = slice(h * GROUP, (h + 1) * GROUP)
                mixed = _dot(wc[h], vb[rows, cols]) + bs_ref[:, cols]
                cat_ref[rows, cols] = (u[rows, cols] * mixed).astype(BF16)

        zb = z[:, 2 * D_A:]
        diff = _pool_diff(zb, carry[...], i * tt)
        carry[...] = zb[tt - HALO:, :]
        for g in range(len(WINDOWS)):
            cols = slice(g * GROUP, (g + 1) * GROUP)
            pre = _dot(diff[g].astype(BF16), wp_ref[g].astype(BF16)) + bp_ref[:, cols]
            cat_ref[:, D_A + g * GROUP:D_A + (g + 1) * GROUP] = (pre * ps_ref[:, cols]).astype(BF16)

        mix = _dot(cat_ref[...], wout_ref[...])
        mix_ref[...] = mix
        x1_ref[...] = xv + gate1 * (mix * _rstd(mix) * n1post_ref[...])

        @pl.when(i == nt - 1)
        def _():
            copies.wait_recv(0, 1, 6, 7, 8, 9)
            copies.wait_send(*range(10))
            for cp in keep:
                cp.wait()

    tile = lambda w: pl.BlockSpec((tt, w), lambda i: (i, 0))
    hbm = pl.BlockSpec(memory_space=pl.ANY)
    outs = pl.pallas_call(
        body, name="attn_fwd", grid=(nt,),
        out_shape=tuple([jax.ShapeDtypeStruct((t_len, D_Z), F32), jax.ShapeDtypeStruct((t_len, D), BF16),
                         jax.ShapeDtypeStruct((t_len, D), F32), jax.ShapeDtypeStruct((t_len, D), F32)]
                        + [jax.ShapeDtypeStruct((FC_EARLY,) + s.shape, BF16) for s in fc_shards]),
        in_specs=[tile(D), _full((8, D)), _full((1, D)), _full((1, D)), _resident((D_Z, D)), _resident((D, D)),
                  _full((N_HEADS, CHUNK, CHUNK)), _full((CHUNK, D_A)), _full((1, D_A)), _full((1, D_A)),
                  _full((len(WINDOWS), GROUP, GROUP)), _full((1, D_B)), _full((1, D_B)),
                  _resident(fc_shards[0].shape), _resident(fc_shards[1].shape)],
        out_specs=(tile(D_Z), tile(D), tile(D), tile(D), hbm, hbm),
        scratch_shapes=[pltpu.VMEM((HALO, D_B), F32),
                        pltpu.VMEM((2,) + fc_shards[0].shape, BF16), pltpu.VMEM((2,) + fc_shards[1].shape, BF16),
                        pltpu.SemaphoreType.DMA((10,)), pltpu.SemaphoreType.DMA((10,)),
                        pltpu.SemaphoreType.DMA((6,))],
        compiler_params=pltpu.CompilerParams(dimension_semantics=("arbitrary",), vmem_limit_bytes=VMEM_LIMIT),
    )(x, mod, n1pre, n1post, w_in_t, w_out, w_sp, bs_rows, ln_g, ln_b, w_pool, b_pool, pool_scale, *fc_shards)
    return outs[:4], outs[4:]


def _mlp_fwd_early(tt, x1, mod, n2pre, w1_early, w2_early):
    t_len = x1.shape[0]
    nt = t_len // tt
    n_late = N_DEV - FC_EARLY

    def body(x1_ref, mod_ref, n2pre_ref, w1_ref, w2_ref, r_ref, h2_ref, f_ref, l1_ref, l2_ref,
             land1, land2, send_sems, recv_sems, local_sems):
        i = pl.program_id(0)
        copies = _Copies(
            [(w1_ref.at[2], land1, 4), (w2_ref.at[4], land2, 2),
             (land1, l1_ref.at[1], 1), (land2, l2_ref.at[1], 1)],
            send_sems, recv_sems)
        keep = [pltpu.make_async_copy(land1, l1_ref.at[0], local_sems.at[0]),
                pltpu.make_async_copy(land2, l2_ref.at[0], local_sems.at[1])]

        @pl.when(i == 0)
        def _():
            copies.start(0, 1)

        @pl.when(i == nt // 2)
        def _():
            copies.wait_recv(0, 1)
            copies.start(2, 3)
            for cp in keep:
                cp.start()

        x1v = x1_ref[...]
        shift2, scale2 = mod_ref[3:4, :], mod_ref[4:5, :]
        h2 = ((x1v * _rstd(x1v) * n2pre_ref[...]) * (1.0 + scale2) + shift2).astype(BF16)
        h2_ref[...] = h2
        for j in range(FC_EARLY):
            cols = slice(j * FF_BLK, (j + 1) * FF_BLK)
            ra = jnp.maximum(_dot(h2, w1_ref[j]), 0.0)
            r = (ra * ra).astype(BF16)
            r_ref[:, cols] = r
            contrib = _dot(r, w2_ref[j])
            if j == 0:
                f_ref[...] = contrib
            else:
                f_ref[...] += contrib

        @pl.when(i == nt - 1)
        def _():
            copies.wait_recv(2, 3)
            copies.wait_send(0, 1, 2, 3)
            for cp in keep:
                cp.wait()

    tile = lambda w: pl.BlockSpec((tt, w), lambda i: (i, 0))
    hbm = pl.BlockSpec(memory_space=pl.ANY)
    outs = pl.pallas_call(
        body, name="mlp_fwd_early", grid=(nt,),
        out_shape=(jax.ShapeDtypeStruct((t_len, D_FF), BF16),
                   jax.ShapeDtypeStruct((t_len, D), BF16), jax.ShapeDtypeStruct((t_len, D), F32),
                   jax.ShapeDtypeStruct((n_late,) + w1_early.shape[1:], BF16),
                   jax.ShapeDtypeStruct((n_late,) + w2_early.shape[1:], BF16)),
        in_specs=[tile(D), _full((8, D)), _full((1, D)),
                  _resident((FC_EARLY, D, FF_BLK)), _resident((FC_EARLY, FF_BLK, D))],
        out_specs=(tile(FC_EARLY * FF_BLK), tile(D), tile(D), hbm, hbm),
        scratch_shapes=[pltpu.VMEM(w1_early.shape[1:], BF16), pltpu.VMEM(w2_early.shape[1:], BF16),
                        pltpu.SemaphoreType.DMA((4,)), pltpu.SemaphoreType.DMA((4,)),
                        pltpu.SemaphoreType.DMA((2,))],
        compiler_params=pltpu.CompilerParams(dimension_semantics=("arbitrary",), vmem_limit_bytes=VMEM_LIMIT),
    )(x1, mod, n2pre, w1_early, w2_early)
    return outs[:3], outs[3:]


def _mlp_fwd_late(tt, r_all, x1, h2, f_early, tgt, mod, n2post, w1_late, w2_late):
    t_len = x1.shape[0]
    nt = t_len // tt
    n_late = N_DEV - FC_EARLY

    def body(r_all_ref, x1_ref, h2_ref, fe_ref, tgt_ref, mod_ref, n2post_ref, w1_ref, w2_ref,
             r_ref, df_ref, dy_ref, red_ref):
        i = pl.program_id(0)

        @pl.when(i == 0)
        def _():
            red_ref[...] = jnp.zeros_like(red_ref)

        x1v = x1_ref[...]
        gate2 = mod_ref[5:6, :]
        h2 = h2_ref[...]
        f = fe_ref[...]
        for j in range(n_late):
            cols = slice(j * FF_BLK, (j + 1) * FF_BLK)
            ra = jnp.maximum(_dot(h2, w1_ref[j]), 0.0)
            r = (ra * ra).astype(BF16)
            r_ref[:, cols] = r
            f = f + _dot(r, w2_ref[j])
        rf = _rstd(f)
        fhat = f * rf
        nf = fhat * n2post_ref[...]
        err = (x1v + gate2 * nf) - tgt_ref[...]
        dy = err * (1.0 / D)
        dy_ref[...] = dy
        dnf = dy * gate2
        df_ref[...] = _rms_bwd(dnf * n2post_ref[...], fhat, rf).astype(BF16)
        red_ref[0:1, :] += _colsum(dy * nf)
        red_ref[1:2, :] += _colsum(dnf * fhat)
        red_ref[2:3, :] += _colsum(0.5 * jnp.mean(err * err, axis=-1, keepdims=True)) * jnp.ones((1, D), F32)

    tile = lambda w: pl.BlockSpec((tt, w), lambda i: (i, 0))
    return pl.pallas_call(
        body, name="mlp_fwd_late", grid=(nt,),
        out_shape=(jax.ShapeDtypeStruct((t_len, D_FF), BF16), jax.ShapeDtypeStruct((t_len, D), BF16),
                   jax.ShapeDtypeStruct((t_len, D), F32), jax.ShapeDtypeStruct((8, D), F32)),
        in_specs=[pl.BlockSpec(memory_space=pl.ANY), tile(D), tile(D), tile(D), tile(D), _full((8, D)),
                  _full((1, D)), _resident((n_late, D, FF_BLK)), _resident((n_late, FF_BLK, D))],
        out_specs=(pl.BlockSpec((tt, n_late * FF_BLK), lambda i: (i, FC_EARLY // n_late)), tile(D), tile(D),
                   _full((8, D))),
        input_output_aliases={0: 0},
        compiler_params=pltpu.CompilerParams(dimension_semantics=("arbitrary",), vmem_limit_bytes=VMEM_LIMIT),
    )(r_all, x1, h2, f_early, tgt, mod, n2post, w1_late, w2_late)


def _mlp_bwd(tt, df, r, dy, x1, mix, mod, n2pre, n1post, w1_early, w2_early, w1_late, w2_late):
    t_len = x1.shape[0]
    nt = t_len // tt
    n_late = N_DEV - FC_EARLY

    def body(df_ref, r_ref, dy_ref, x1_ref, mix_ref, mod_ref, n2pre_ref, n1post_ref,
             w1e_ref, w2e_ref, w1l_ref, w2l_ref, da_ref, dmix_ref, dx1_ref, red_ref, dh2_acc):
        i = pl.program_id(0)

        @pl.when(i == 0)
        def _():
            red_ref[...] = jnp.zeros_like(red_ref)

        dfv = df_ref[...]
        for j in range(N_DEV):
            cols = slice(j * FF_BLK, (j + 1) * FF_BLK)
            if j < FC_EARLY:
                w1, w2 = w1e_ref[j], w2e_ref[j]
            else:
                w1, w2 = w1l_ref[j - FC_EARLY], w2l_ref[j - FC_EARLY]
            dr = _dot_nt(dfv, w2)
            da = (dr * (2.0 * jnp.sqrt(r_ref[:, cols].astype(F32)))).astype(BF16)
            da_ref[:, cols] = da
            contrib = _dot_nt(da, w1)
            if j == 0:
                dh2_acc[...] = contrib
            else:
                dh2_acc[...] += contrib
        dh2 = dh2_acc[...]
        gate1, scale2 = mod_ref[2:3, :], mod_ref[4:5, :]
        x1v = x1_ref[...]
        r2 = _rstd(x1v)
        xhat = x1v * r2
        n2 = xhat * n2pre_ref[...]
        dn2 = dh2 * (1.0 + scale2)
        dx1 = dy_ref[...] + _rms_bwd(dn2 * n2pre_ref[...], xhat, r2)
        dx1_ref[...] = dx1
        mixv = mix_ref[...]
        rm = _rstd(mixv)
        mhat = mixv * rm
        dnm = dx1 * gate1
        dmix_ref[...] = _rms_bwd(dnm * n1post_ref[...], mhat, rm).astype(BF16)
        red_ref[0:1, :] += _colsum(dh2)
        red_ref[1:2, :] += _colsum(dh2 * n2)
        red_ref[2:3, :] += _colsum(dn2 * xhat)
        red_ref[3:4, :] += _colsum(dx1 * (mhat * n1post_ref[...]))
        red_ref[4:5, :] += _colsum(dnm * mhat)

    tile = lambda w: pl.BlockSpec((tt, w), lambda i: (i, 0))
    return pl.pallas_call(
        body, name="mlp_bwd", grid=(nt,),
        out_shape=(jax.ShapeDtypeStruct((t_len, D_FF), BF16),
                   jax.ShapeDtypeStruct((t_len, D), BF16), jax.ShapeDtypeStruct((t_len, D), F32),
                   jax.ShapeDtypeStruct((8, D), F32)),
        in_specs=[tile(D), tile(D_FF), tile(D), tile(D), tile(D),
                  _full((8, D)), _full((1, D)), _full((1, D)),
                  _resident((FC_EARLY, D, FF_BLK)), _resident((FC_EARLY, FF_BLK, D)),
                  _resident((n_late, D, FF_BLK)), _resident((n_late, FF_BLK, D))],
        out_specs=(tile(D_FF), tile(D), tile(D), _full((8, D))),
        scratch_shapes=[pltpu.VMEM((tt, D), F32)],
        compiler_params=pltpu.CompilerParams(dimension_semantics=("arbitrary",), vmem_limit_bytes=VMEM_LIMIT),
    )(df, r, dy, x1, mix, mod, n2pre, n1post, w1_early, w2_early, w1_late, w2_late)


def _mlp_wgrad(tt, r, da, df, h2):
    t_len = df.shape[0]
    nt = t_len // tt

    def relation(j):
        return jnp.where(j < 4, 2 * j + 1, jnp.where(j == 4, 6, jnp.where(j == N_DEV - 1, 0, 2 * (j - 4))))

    def body(r_ref, da_ref, df_ref, h2_ref, own1_ref, own2_ref, out1_ref, out2_ref, diag1_ref, diag2_ref,
             acc1, acc2, snd1, snd2, sib1, sib2, dsnd1, dsnd2, send_sems, recv_sems):
        j, t = pl.program_id(0), pl.program_id(1)
        rows = pl.ds(pl.multiple_of(t * tt, tt), tt)
        x, y, c = _place()
        accs, snds, sibs = (acc1, acc2), (snd1, snd2), (sib1, sib2)
        dsnds, diags = (dsnd1, dsnd2), (diag1_ref, diag2_ref)

        def to_sibling(a, jj):
            return pltpu.make_async_remote_copy(
                src_ref=snds[a].at[jj % 2], dst_ref=sibs[a].at[jj],
                send_sem=send_sems.at[4 * a + jj], recv_sem=recv_sems.at[4 * a + jj],
                device_id=(x, y, 1 - c), device_id_type=MESH)

        def to_diagonal(a):
            return pltpu.make_async_remote_copy(
                src_ref=dsnds[a], dst_ref=diags[a], send_sem=send_sems.at[8 + a], recv_sem=recv_sems.at[8 + a],
                device_id=_peer(x, y, c, 6), device_id_type=MESH)

        @pl.when(t == 0)
        def _():
            acc2[...] = jnp.zeros_like(acc2)
            acc1[...] = jnp.zeros_like(acc1)

        acc2[...] += _dot_tn(r_ref[...], df_ref[rows, :])
        acc1[...] += _dot_tn(h2_ref[rows, :], da_ref[...])

        @pl.when((t == nt - 1) & (j < 4))
        def _():
            for a in range(2):
                @pl.when(j >= 2)
                def _():
                    to_sibling(a, j - 2).wait_send()

                snds[a][j % 2] = accs[a][...].astype(BF16)
                to_sibling(a, j).start()

        @pl.when((t == nt - 1) & (j >= 4))
        def _():
            jj = relation(j) // 2
            for a, (own_ref, out_ref) in enumerate(((own1_ref, out1_ref), (own2_ref, out2_ref))):
                to_sibling(a, jj).wait_recv()

                @pl.when(j == 4)
                def _():
                    dsnds[a][...] = (accs[a][...] + sibs[a][jj].astype(F32)).astype(BF16)
                    to_diagonal(a).start()

                @pl.when((j > 4) & (j < N_DEV - 1))
                def _():
                    out_ref[0] = (accs[a][...] + sibs[a][jj].astype(F32)).astype(BF16)

                @pl.when(j == N_DEV - 1)
                def _():
                    own_ref[...] = accs[a][...] + sibs[a][jj].astype(F32)
                    to_sibling(a, 2).wait_send()
                    to_sibling(a, 3).wait_send()
                    to_diagonal(a).wait_recv()
                    to_diagonal(a).wait_send()

    blk = pl.BlockSpec((tt, FF_BLK), lambda j, t: (t, relation(j)))
    chip = lambda j, t: (jnp.clip(j - 5, 0, 1), 0, 0)
    hbm = pl.BlockSpec(memory_space=pl.ANY)
    return pl.pallas_call(
        body, name="mlp_wgrad", grid=(N_DEV, nt),
        out_shape=(jax.ShapeDtypeStruct((D, FF_BLK), F32), jax.ShapeDtypeStruct((FF_BLK, D), F32),
                   jax.ShapeDtypeStruct((2, D, FF_BLK), BF16), jax.ShapeDtypeStruct((2, FF_BLK, D), BF16),
                   jax.ShapeDtypeStruct((D, FF_BLK), BF16), jax.ShapeDtypeStruct((FF_BLK, D), BF16)),
        in_specs=[blk, blk, _resident((t_len, D)), _resident((t_len, D))],
        out_specs=(_full((D, FF_BLK)), _full((FF_BLK, D)),
                   pl.BlockSpec((1, D, FF_BLK), chip), pl.BlockSpec((1, FF_BLK, D), chip), hbm, hbm),
        scratch_shapes=[pltpu.VMEM((D, FF_BLK), F32), pltpu.VMEM((FF_BLK, D), F32),
                        pltpu.VMEM((2, D, FF_BLK), BF16), pltpu.VMEM((2, FF_BLK, D), BF16),
                        pltpu.VMEM((4, D, FF_BLK), BF16), pltpu.VMEM((4, FF_BLK, D), BF16),
                        pltpu.VMEM((D, FF_BLK), BF16), pltpu.VMEM((FF_BLK, D), BF16),
                        pltpu.SemaphoreType.DMA((10,)), pltpu.SemaphoreType.DMA((10,))],
        compiler_params=pltpu.CompilerParams(dimension_semantics=("arbitrary", "arbitrary"),
                                             vmem_limit_bytes=VMEM_LIMIT),
    )(r, da, df, h2)


def _acc_rows(ref, row0, k, val):
    half = CHUNK // 2
    ref[row0:row0 + half, k * GROUP:(k + 1) * GROUP] += val[:half, :]
    ref[row0:row0 + half, D_A + k * GROUP:D_A + (k + 1) * GROUP] += val[half:, :]


def _attn_bwd(tt, dmix, dx1, x, z, cat, mod, n1pre, w_in_t, w_out, w_sp, bs_rows, ln_g, ln_b, w_pool, b_pool,
              pool_scale, red_fwd, red_bwd, chip_sums):
    t_len = x.shape[0]
    nt = t_len // tt
    hb = tt // HALO
    n_sums = len(chip_sums)

    def body(dmix_ref, dx1_ref, x_ref, z_ref, zprev_ref, cat_ref, mod_ref, n1pre_ref, win_ref, wout_ref, wsp_ref,
             bs_ref, lng_ref, lnb_ref, wp_ref, bp_ref, ps_ref, redf_ref, redb_ref, *rest):
        sum_out = rest[:n_sums]
        gx_ref, gwin_ref, gwout_ref, small_ref = rest[n_sums:n_sums + 4]
        sum_in = rest[n_sums + 4:2 * n_sums + 4]
        carry, acc_in, acc_out, dz_scr, bs_acc, send_sems, recv_sems = rest[2 * n_sums + 4:]
        s = pl.program_id(0)
        i = nt - 1 - s
        px, py, pc = _place()

        def chip_copy(a, r):
            return pltpu.make_async_remote_copy(
                src_ref=sum_out[a].at[r], dst_ref=sum_in[a].at[r],
                send_sem=send_sems.at[2 * a + r], recv_sem=recv_sems.at[2 * a + r],
                device_id=_peer(px, py, pc, 2 * (r + 1)), device_id_type=MESH)

        @pl.when(s == 0)
        def _():
            for a in range(n_sums):
                for r in range(2):
                    chip_copy(a, r).start()
            carry[...] = jnp.zeros_like(carry)
            acc_in[...] = jnp.zeros_like(acc_in)
            acc_out[...] = jnp.zeros_like(acc_out)
            bs_acc[...] = jnp.zeros_like(bs_acc)
            small_ref[...] = jnp.zeros_like(small_ref)
            small_ref[ROW_DMOD + 2:ROW_DMOD + 3, :] = redb_ref[3:4, :]
            small_ref[ROW_DMOD + 3:ROW_DMOD + 5, :] = redb_ref[0:2, :]
            small_ref[ROW_DMOD + 5:ROW_DMOD + 6, :] = redf_ref[0:1, :]
            small_ref[ROW_N1POST:ROW_N1POST + 1, :] = redb_ref[4:5, :]
            small_ref[ROW_N2PRE:ROW_N2PRE + 1, :] = redb_ref[2:3, :]
            small_ref[ROW_N2POST:ROW_N2POST + 1, :] = redf_ref[1:2, :]
            small_ref[ROW_LOSS:ROW_LOSS + 1, :] = redf_ref[2:3, :]

        dmixv = dmix_ref[...]
        dcat = _dot_nt(dmixv, wout_ref[...])
        acc_out[...] += _dot_tn(cat_ref[...], dmixv)

        z = z_ref[...]
        t_g, ga = _gelu_parts(z[:, :2 * D_A])
        u, vr = ga[:, :D_A], ga[:, D_A:]
        dv0 = vr - jnp.mean(vr, axis=-1, keepdims=True)
        rv = lax.rsqrt(jnp.mean(dv0 * dv0, axis=-1, keepdims=True) + EPS)
        vhat = dv0 * rv
        vb = (vhat * lng_ref[...] + lnb_ref[...]).astype(BF16)
        mask = _tril_mask()
        wc = [(wsp_ref[h] * mask).astype(BF16) for h in range(N_HEADS)]

        dya = dcat[:, :D_A]
        for h in range(N_HEADS):
            cols = slice(h * GROUP, (h + 1) * GROUP)
            bs_sum = jnp.zeros((CHUNK, GROUP), F32)
            ws_sum = jnp.zeros((CHUNK, CHUNK), F32)
            for ch in range(tt // CHUNK):
                rows = slice(ch * CHUNK, (ch + 1) * CHUNK)
                v_ch = vb[rows, cols]
                mixed = _dot(wc[h], v_ch) + bs_ref[:, cols]
                dy_ch = dya[rows, cols]
                dz_scr[rows, cols] = dy_ch * mixed
                dmixed = dy_ch * u[rows, cols]
                dmb = dmixed.astype(BF16)
                dz_scr[rows, D_A + h * GROUP:D_A + (h + 1) * GROUP] = _dot_tn(wc[h], dmb)
                bs_sum = bs_sum + dmixed
                ws_sum = ws_sum + _dot_nt(dmb, v_ch)
            _acc_rows(bs_acc, 0, h, bs_sum)
            _acc_rows(small_ref, ROW_WS, h, ws_sum)

        dvl = dz_scr[:, D_A:2 * D_A]
        dvhat = dvl * lng_ref[...]
        dvr = rv * (dvhat - jnp.mean(dvhat, axis=-1, keepdims=True)
                    - vhat * jnp.mean(dvhat * vhat, axis=-1, keepdims=True))
        small_ref[ROW_LN:ROW_LN + 1, 0:D_A] += _colsum(dvl * vhat)
        small_ref[ROW_LN:ROW_LN + 1, D_A:D] += _colsum(dvl)
        dga = jnp.concatenate([dz_scr[:, :D_A], dvr], axis=1)
        dza = dga * _gelu_grad(z[:, :2 * D_A], t_g)

        zb = z[:, 2 * D_A:]
        halo_prev = jnp.where(i == 0, 0.0, zprev_ref[...])
        diff = _pool_diff(zb, halo_prev, i * tt)
        dyb = dcat[:, D_A:]
        inv = _inv_counts(i * tt, tt)
        scaled, ddiffs = [], []
        for g in range(len(WINDOWS)):
            cols = slice(g * GROUP, (g + 1) * GROUP)
            db = diff[g].astype(BF16)
            wpg = wp_ref[g].astype(BF16)
            pre = _dot(db, wpg) + bp_ref[:, cols]
            small_ref[ROW_POOL:ROW_POOL + 1, cols] += _colsum(dyb[:, cols] * pre)
            dpre = dyb[:, cols] * ps_ref[:, cols]
            small_ref[ROW_POOL:ROW_POOL + 1, D_B + g * GROUP:D_B + (g + 1) * GROUP] += _colsum(dpre)
            dpb = dpre.astype(BF16)
            _acc_rows(small_ref, ROW_WP, g, _dot_tn(db, dpb))
            ddiff = _dot_nt(dpb, wpg)
            ddiffs.append(ddiff)
            scaled.append(ddiff * inv[g])
        scaled_all = jnp.concatenate(scaled, axis=1)
        ext = jnp.concatenate([scaled_all, carry[...]], axis=0)
        n_ext = tt + HALO
        s2 = ext + pltpu.roll(ext, n_ext - 1, 0)
        t4 = s2[:, GROUP:]
        s4 = t4 + pltpu.roll(t4, n_ext - 2, 0)
        t8 = s4[:, GROUP:]
        s8 = t8 + pltpu.roll(t8, n_ext - 4, 0)
        t16 = s8[:, GROUP:]
        s16 = t16 + pltpu.roll(t16, n_ext - 8, 0)
        back = [s2[:, :GROUP], s4[:, :GROUP], s8[:, :GROUP], s16]
        carry[...] = scaled_all[:HALO, :]
        dzb = jnp.concatenate([back[g][:tt, :] - ddiffs[g] for g in range(len(WINDOWS))], axis=1)

        dzv = jnp.concatenate([dza, dzb], axis=1).astype(BF16)
        dh1 = _dot(dzv, win_ref[...])
        xv = x_ref[...]
        r1 = _rstd(xv)
        xhat = xv * r1
        shift1, scale1 = mod_ref[0:1, :], mod_ref[1:2, :]
        n1 = xhat * n1pre_ref[...]
        h1 = (n1 * (1.0 + scale1) + shift1).astype(BF16)
        acc_in[...] += _dot_tn(dzv, h1)
        dn1 = dh1 * (1.0 + scale1)
        gx_ref[...] = dx1_ref[...] + _rms_bwd(dn1 * n1pre_ref[...], xhat, r1)
        small_ref[ROW_DMOD:ROW_DMOD + 1, :] += _colsum(dh1)
        small_ref[ROW_DMOD + 1:ROW_DMOD + 2, :] += _colsum(dh1 * n1)
        small_ref[ROW_N1PRE:ROW_N1PRE + 1, :] += _colsum(dn1 * xhat)

        @pl.when(s == nt - 1)
        def _():
            gwin_ref[...] = acc_in[...].astype(BF16)
            gwout_ref[...] = acc_out[...].astype(BF16)
            bs = _unfold(bs_acc[...])
            for h in range(N_HEADS):
                small_ref[ROW_BS + h:ROW_BS + h + 1, 0:GROUP] = jnp.sum(
                    bs[:, h * GROUP:(h + 1) * GROUP].T, axis=0, keepdims=True)
            for a in range(n_sums):
                for r in range(2):
                    chip_copy(a, r).wait_recv()
                    chip_copy(a, r).wait_send()

    rev = lambda w: pl.BlockSpec((tt, w), lambda s: (nt - 1 - s, 0))
    zprev = pl.BlockSpec((HALO, D_B), lambda s: (jnp.maximum((nt - 1 - s) * hb - 1, 0), 2))
    hbm = pl.BlockSpec(memory_space=pl.ANY)
    outs = pl.pallas_call(
        body, name="attn_bwd", grid=(nt,),
        out_shape=tuple([jax.ShapeDtypeStruct((t_len, D), F32), jax.ShapeDtypeStruct((D_Z, D), BF16),
                         jax.ShapeDtypeStruct((D, D), BF16), jax.ShapeDtypeStruct((SMALL_ROWS, D), F32)]
                        + [jax.ShapeDtypeStruct(cs.shape, cs.dtype) for cs in chip_sums]),
        in_specs=[rev(D), rev(D), rev(D), rev(D_Z), zprev, rev(D), _full((8, D)), _full((1, D)),
                  _resident((D_Z, D)), _resident((D, D)), _full((N_HEADS, CHUNK, CHUNK)), _full((CHUNK, D_A)),
                  _full((1, D_A)), _full((1, D_A)), _full((len(WINDOWS), GROUP, GROUP)), _full((1, D_B)),
                  _full((1, D_B)), _full((8, D)), _full((8, D))] + [_resident(cs.shape) for cs in chip_sums],
        out_specs=tuple([rev(D), _resident((D_Z, D)), _resident((D, D)), _full((SMALL_ROWS, D))] + [hbm] * n_sums),
        scratch_shapes=[pltpu.VMEM((HALO, D_B), F32), pltpu.VMEM((D_Z, D), F32), pltpu.VMEM((D, D), F32),
                        pltpu.VMEM((tt, 2 * D_A), F32), pltpu.VMEM((CHUNK // 2, D), F32),
                        pltpu.SemaphoreType.DMA((2 * n_sums,)), pltpu.SemaphoreType.DMA((2 * n_sums,))],
        compiler_params=pltpu.CompilerParams(dimension_semantics=("arbitrary",), vmem_limit_bytes=VMEM_LIMIT),
    )(dmix, dx1, x, z, z, cat, mod, n1pre, w_in_t, w_out, w_sp, bs_rows, ln_g, ln_b, w_pool, b_pool, pool_scale,
      red_fwd, red_bwd, *chip_sums)
    return outs[:4], outs[4:]


def _adam(w, g, m, v):
    m2 = ADAM_B1 * m + (1.0 - ADAM_B1) * g
    v2 = ADAM_B2 * v + (1.0 - ADAM_B2) * (g * g)
    m_hat = m2 / (1.0 - ADAM_B1 ** ADAM_STEP)
    v_hat = v2 / (1.0 - ADAM_B2 ** ADAM_STEP)
    delta = -ADAM_LR * (m_hat / (jnp.sqrt(v_hat) + ADAM_EPS) + ADAM_WD * w)
    return delta, m2, v2


def _adamw_shard(name, rb, w, g, m, v):
    rows, cols = w.shape

    def body(w_ref, g_ref, m_ref, v_ref, d_ref, m2_ref, v2_ref):
        d_ref[...], m2_ref[...], v2_ref[...] = _adam(w_ref[...], g_ref[...], m_ref[...], v_ref[...])

    blk = pl.BlockSpec((rb, cols), lambda i: (i, 0))
    shp = jax.ShapeDtypeStruct((rows, cols), F32)
    return pl.pallas_call(
        body, name=name, grid=(rows // rb,), out_shape=(shp, shp, shp),
        in_specs=[blk] * 4, out_specs=(blk, blk, blk),
        compiler_params=pltpu.CompilerParams(dimension_semantics=("arbitrary",)),
    )(w, g, m, v)


def _adamw_ada(rb, w, sc, dmod_cols, m, v):
    rows, cols = w.shape

    def body(w_ref, sc_ref, dm_ref, m_ref, v_ref, g_ref, d_ref, m2_ref, v2_ref):
        g = _dot_tn(sc_ref[...].astype(BF16), dm_ref[...].astype(BF16))
        g_ref[...] = g
        d_ref[...], m2_ref[...], v2_ref[...] = _adam(w_ref[...], g, m_ref[...], v_ref[...])

    blk = pl.BlockSpec((rb, cols), lambda i: (i, 0))
    shp = jax.ShapeDtypeStruct((rows, cols), F32)
    return pl.pallas_call(
        body, name="adamw_ada", grid=(rows // rb,), out_shape=(shp, shp, shp, shp),
        in_specs=[blk, pl.BlockSpec((N_DEV, rb), lambda i: (0, i)), _full((N_DEV, cols)), blk, blk],
        out_specs=(blk, blk, blk, blk),
        compiler_params=pltpu.CompilerParams(dimension_semantics=("arbitrary",)),
    )(w, sc, dmod_cols, m, v)


def _unfold(acc_rows):
    return jnp.concatenate([acc_rows[:, :D_A], acc_rows[:, D_A:]], axis=0)


def _adamw_small(total, params):
    n = len(params)
    flat = [a for p in params for a in p]

    def body(*refs):
        s_ref = refs[0]
        p_refs = refs[1:1 + 3 * n]
        loss_ref = refs[1 + 3 * n]
        o_refs = refs[2 + 3 * n:]
        d_b_ada = s_ref[0:6, :]
        for b in range(1, N_DEV):
            d_b_ada = d_b_ada + s_ref[8 * b:8 * b + 6, :]
        tot = s_ref[PACK_SHIFT:PACK_ROWS, :]
        loss_ref[...] = jnp.broadcast_to(tot[ROW_LOSS:ROW_LOSS + 1, 0:GROUP], (8, GROUP))
        mask = _tril_mask()
        ws = _unfold(tot[ROW_WS:ROW_WS + 64, :])
        wp = _unfold(tot[ROW_WP:ROW_WP + 64, :])
        grads = [
            d_b_ada,
            tot[ROW_N1PRE:ROW_N1PRE + 1, :], tot[ROW_N1POST:ROW_N1POST + 1, :],
            tot[ROW_N2PRE:ROW_N2PRE + 1, :], tot[ROW_N2POST:ROW_N2POST + 1, :],
            tot[ROW_LN:ROW_LN + 1, :D_A], tot[ROW_LN:ROW_LN + 1, D_A:],
            tot[ROW_POOL:ROW_POOL + 1, :D_B], tot[ROW_POOL:ROW_POOL + 1, D_B:],
            tot[ROW_BS:ROW_BS + N_HEADS, 0:GROUP],
            jnp.stack([ws[:, h * GROUP:(h + 1) * GROUP] * mask for h in range(N_HEADS)]),
            jnp.stack([wp[:, g * GROUP:(g + 1) * GROUP] for g in range(len(WINDOWS))]),
        ]
        for k in range(n):
            w_ref, m_ref, v_ref = p_refs[3 * k:3 * k + 3]
            g = grads[k]
            o_refs[4 * k][...] = g
            o_refs[4 * k + 1][...], o_refs[4 * k + 2][...], o_refs[4 * k + 3][...] = _adam(
                w_ref[...], g, m_ref[...], v_ref[...])

    vm = pl.BlockSpec(memory_space=pltpu.VMEM)
    out_shape = [jax.ShapeDtypeStruct((8, GROUP), F32)]
    for w, _, _ in params:
        out_shape += [jax.ShapeDtypeStruct(w.shape, F32)] * 4
    return pl.pallas_call(
        body, name="adamw_small", out_shape=tuple(out_shape),
        in_specs=[vm] * (1 + 3 * n), out_specs=tuple([vm] * len(out_shape)),
    )(total, *flat)


TT_ATTN_FWD = 512
TT_MLP_FWD = 512
TT_MLP = 256
TT_WGRAD = 2048
TT_ATTN_BWD = 512


def kernel(x, c, w_ada, b_ada, norm1_pre, norm1_post, w_in, w_spatial, b_spatial, ln_v_gain, ln_v_bias, w_pool, b_pool, pool_scale, w_out, norm2_pre, norm2_post, w_fc1, w_fc2, loss_target, m_w_ada, m_b_ada, m_norm1_pre, m_norm1_post, m_w_in, m_w_spatial, m_b_spatial, m_ln_v_gain, m_ln_v_bias, m_w_pool, m_b_pool, m_pool_scale, m_w_out, m_norm2_pre, m_norm2_post, m_w_fc1, m_w_fc2, v_w_ada, v_b_ada, v_norm1_pre, v_norm1_post, v_w_in, v_w_spatial, v_b_spatial, v_ln_v_gain, v_ln_v_bias, v_w_pool, v_b_pool, v_pool_scale, v_w_out, v_norm2_pre, v_norm2_post, v_w_fc1, v_w_fc2):
    t_len = x.shape[1]
    me = 4 * lax.axis_index("x") + 2 * lax.axis_index("y") + lax.axis_index("c")
    ada_cols = w_ada.shape[1]
    tt = lambda want: min(want, t_len)

    x2 = x.reshape(t_len, D)
    tgt = loss_target.reshape(t_len, D)
    row = lambda a: a.reshape(1, -1)

    b_my = lax.dynamic_slice_in_dim(b_ada, me * ada_cols, ada_cols).reshape(1, ada_cols)
    modp, sc, (g_in, g_out), fc_shards = _fwd_comm(jnp.broadcast_to(c, (8, D)), w_ada, b_my,
                                                   [w_in.T, w_out], [w_fc1, w_fc2])
    mod = jnp.concatenate([modp.reshape(6, D), jnp.zeros((2, D), F32)], axis=0)
    w_in_t = g_in.reshape(D_Z, D)
    w_out_all = g_out.reshape(D, D)

    bs_rows = jnp.repeat(b_spatial.T, GROUP, axis=1)
    attn_consts = (w_spatial, bs_rows, row(ln_v_gain), row(ln_v_bias), w_pool, row(b_pool), row(pool_scale))

    (z, cat, mix, x1), (w1_early, w2_early) = _attn_fwd(
        tt(TT_ATTN_FWD), x2, mod, row(norm1_pre), row(norm1_post), w_in_t, w_out_all, *attn_consts, fc_shards)
    (r_early, h2, f_early), (w1_late, w2_late) = _mlp_fwd_early(
        tt(TT_MLP_FWD), x1, mod, row(norm2_pre), w1_early, w2_early)
    r, df, dy, red_fwd = _mlp_fwd_late(tt(TT_MLP_FWD), r_early, x1, h2, f_early, tgt, mod, row(norm2_post),
                                       w1_late, w2_late)
    da, dmix, dx1, red_bwd = _mlp_bwd(tt(TT_MLP), df, r, dy, x1, mix, mod, row(norm2_pre), row(norm1_post),
                                      w1_early, w2_early, w1_late, w2_late)
    own_w1, own_w2, sums_w1, sums_w2, diag_w1, diag_w2 = _mlp_wgrad(tt(TT_WGRAD), r, da, df, h2)
    (grad_x, p_in, p_out, small), (arr_w1, arr_w2) = _attn_bwd(
        tt(TT_ATTN_BWD), dmix, dx1, x2, z, cat, mod, row(norm1_pre), w_in_t, w_out_all, *attn_consts,
        red_fwd, red_bwd, [sums_w1, sums_w2])
    (grad_in_t, grad_out, total), ((grad_w1, d_w1, m_w1, v_w1), (grad_w2, d_w2, m_w2, v_w2)) = _tail_comm(
        [p_in.reshape(N_DEV, D_Z // N_DEV, D), p_out.reshape(N_DEV, D // N_DEV, D)], small, 64,
        [(w_fc1, own_w1, arr_w1, diag_w1, m_w_fc1, v_w_fc1), (w_fc2, own_w2, arr_w2, diag_w2, m_w_fc2, v_w_fc2)])

    d_out, m_out, v_out = _adamw_shard("adamw_out", 128, w_out, grad_out, m_w_out, v_w_out)
    d_in_t, m_in_t, v_in_t = _adamw_shard("adamw_in", D_Z // N_DEV, w_in.T, grad_in_t, m_w_in.T, v_w_in.T)
    dmod_all = total[0:TABLE_ROWS, :].reshape(N_DEV, 8, D)[:, :6, :].reshape(N_DEV, 6 * D)
    dmod_cols = lax.dynamic_slice_in_dim(dmod_all, me * ada_cols, ada_cols, axis=1)
    grad_ada, d_ada, m_ada, v_ada = _adamw_ada(256, w_ada, sc, dmod_cols, m_w_ada, v_w_ada)

    six = lambda a: a.reshape(6, D)
    small_params = [
        (six(b_ada), six(m_b_ada), six(v_b_ada)),
        (row(norm1_pre), row(m_norm1_pre), row(v_norm1_pre)),
        (row(norm1_post), row(m_norm1_post), row(v_norm1_post)),
        (row(norm2_pre), row(m_norm2_pre), row(v_norm2_pre)),
        (row(norm2_post), row(m_norm2_post), row(v_norm2_post)),
        (row(ln_v_gain), row(m_ln_v_gain), row(v_ln_v_gain)),
        (row(ln_v_bias), row(m_ln_v_bias), row(v_ln_v_bias)),
        (row(pool_scale), row(m_pool_scale), row(v_pool_scale)),
        (row(b_pool), row(m_b_pool), row(v_b_pool)),
        (b_spatial, m_b_spatial, v_b_spatial),
        (w_spatial, m_w_spatial, v_w_spatial),
        (w_pool, m_w_pool, v_w_pool),
    ]
    outs = _adamw_small(total, small_params)
    loss = outs[0][0, 0]
    names = ["b_ada", "norm1_pre", "norm1_post", "norm2_pre", "norm2_post", "ln_v_gain", "ln_v_bias", "pool_scale",
             "b_pool", "b_spatial", "w_spatial", "w_pool"]
    shapes = dict(b_ada=b_ada.shape, norm1_pre=norm1_pre.shape, norm1_post=norm1_post.shape,
                  norm2_pre=norm2_pre.shape, norm2_post=norm2_post.shape, ln_v_gain=ln_v_gain.shape,
                  ln_v_bias=ln_v_bias.shape, pool_scale=pool_scale.shape, b_pool=b_pool.shape,
                  b_spatial=b_spatial.shape, w_spatial=w_spatial.shape, w_pool=w_pool.shape)
    res = {}
    for k, nm in enumerate(names):
        res[nm] = tuple(o.reshape(shapes[nm]) for o in outs[1 + 4 * k:5 + 4 * k])
    res["w_ada"] = (grad_ada, d_ada, m_ada, v_ada)
    res["w_in"] = (grad_in_t.T, d_in_t.T, m_in_t.T, v_in_t.T)
    res["w_out"] = (grad_out, d_out, m_out, v_out)
    res["w_fc1"] = (grad_w1, d_w1, m_w1, v_w1)
    res["w_fc2"] = (grad_w2, d_w2, m_w2, v_w2)

    order = ["w_ada", "b_ada", "norm1_pre", "norm1_post", "w_in", "w_spatial", "b_spatial", "ln_v_gain", "ln_v_bias",
             "w_pool", "b_pool", "pool_scale", "w_out", "norm2_pre", "norm2_post", "w_fc1", "w_fc2"]
    return (loss, grad_x.reshape(x.shape),
            *[res[nm][0] for nm in order], *[res[nm][1] for nm in order],
            *[res[nm][2] for nm in order], *[res[nm][3] for nm in order])
```

```python
import functools

import jax
import jax.numpy as jnp
from jax import lax
from jax.experimental import pallas as pl
from jax.experimental.pallas import tpu as pltpu

F32 = jnp.float32
BF16 = jnp.bfloat16
MESH = pl.DeviceIdType.MESH

N_DEV = 8
D = 1024
D_A = 512
D_B = 512
D_Z = 2 * D_A + D_B
N_HEADS = 4
CHUNK = 128
WINDOWS = (2, 4, 8, 16)
GROUP = 128
D_FF = 4096
FF_BLK = D_FF // N_DEV
HALO = 16
EPS = 1e-6
VMEM_LIMIT = 60 * 1024 * 1024

ADAM_LR = 0.001
ADAM_B1 = 0.9
ADAM_B2 = 0.999
ADAM_EPS = 1e-08
ADAM_WD = 0.01
ADAM_STEP = 10

ROW_DMOD = 0
ROW_N1PRE, ROW_N1POST, ROW_N2PRE, ROW_N2POST = 8, 9, 10, 11
ROW_LN = 12
ROW_POOL = 13
ROW_LOSS = 14
ROW_BS = 16
ROW_WS = 24
ROW_WP = 88
SMALL_ROWS = 152
TABLE_ROWS = 8 * N_DEV
PACK_SHIFT = TABLE_ROWS - 8
PACK_ROWS = SMALL_ROWS + PACK_SHIFT
PACK_HALF = PACK_ROWS // 2


def _dot(a, b):
    return jnp.dot(a, b, preferred_element_type=F32)


def _dot_nt(a, b):
    return lax.dot_general(a, b, (((1,), (1,)), ((), ())), preferred_element_type=F32)


def _dot_tn(a, b):
    return lax.dot_general(a, b, (((0,), (0,)), ((), ())), preferred_element_type=F32)


def _rstd(v):
    return lax.rsqrt(jnp.mean(v * v, axis=-1, keepdims=True) + EPS)


def _rms_bwd(d_hat, hat, rstd):
    return rstd * (d_hat - hat * jnp.mean(d_hat * hat, axis=-1, keepdims=True))


_K0 = 0.7978845608028654
_K1 = 0.044715


def _gelu_parts(v):
    t = jnp.tanh(_K0 * (v + _K1 * (v * v * v)))
    return t, v * (0.5 * (1.0 + t))


def _gelu_grad(v, t):
    return 0.5 * (1.0 + t) + (0.5 * v) * (1.0 - t * t) * (_K0 * (1.0 + (3.0 * _K1) * (v * v)))


def _colsum(v):
    return jnp.sum(v, axis=0, keepdims=True)


def _full(shape):
    n = len(shape)
    return pl.BlockSpec(shape, lambda *_: (0,) * n)


def _resident(shape):
    n = len(shape)
    return pl.BlockSpec(shape, lambda *_: (0,) * n, pipeline_mode=pl.Buffered(1))


def _place():
    x, y, c = lax.axis_index("x"), lax.axis_index("y"), lax.axis_index("c")
    return x, y, c


def _flip(v, bit):
    return 1 - v if bit else v


def _peer(x, y, c, k):
    return (_flip(x, (k >> 2) & 1), _flip(y, (k >> 1) & 1), _flip(c, k & 1))


def _index(p):
    return 4 * p[0] + 2 * p[1] + p[2]


def _two_level_gather_begin(x, y, c, out_refs, send_sems, recv_sems):
    me = (x, y, c)
    sibling = (x, y, 1 - c)
    chips = [(1 - x, y), (x, 1 - y), (1 - x, 1 - y)]

    def copy(a, k, block, to):
        ref = out_refs[a].at[_index(block)]
        return pltpu.make_async_remote_copy(
            src_ref=ref, dst_ref=ref, send_sem=send_sems.at[7 * a + k], recv_sem=recv_sems.at[7 * a + k],
            device_id=to, device_id_type=MESH)

    first = []
    for a in range(len(out_refs)):
        first.append(copy(a, 0, me, sibling))
        first += [copy(a, 1 + j, me, (*chip, c)) for j, chip in enumerate(chips)]
    for cp in first:
        cp.start()
    return copy, first, me, sibling, chips


def _two_level_gather_finish(c, n, begun):
    copy, first, me, sibling, chips = begun
    passed = []
    for a in range(n):
        for j, chip in enumerate(chips):
            copy(a, 1 + j, (*chip, c), me).wait_recv()
            fwd = copy(a, 4 + j, (*chip, c), sibling)
            fwd.start()
            passed.append(fwd)
    for a in range(n):
        copy(a, 0, sibling, me).wait_recv()
        for j, chip in enumerate(chips):
            copy(a, 4 + j, (*chip, 1 - c), me).wait_recv()
    for cp in first + passed:
        cp.wait_send()


def _fwd_comm(c8, w_ada, b_my, gathered, kept):
    ncol = w_ada.shape[1]
    n_g, n_k = len(gathered), len(kept)

    def body(c_ref, w_ref, b_ref, *rest):
        g_in, k_in = rest[:n_g], rest[n_g:n_g + n_k]
        modp_ref, sc_ref = rest[n_g + n_k:n_g + n_k + 2]
        g_out = rest[n_g + n_k + 2:2 * n_g + n_k + 2]
        k_out = rest[2 * n_g + n_k + 2:2 * n_g + 2 * n_k + 2]
        cg, mg, part, send_sems, recv_sems, g_send, g_recv = rest[2 * n_g + 2 * n_k + 2:]
        x, y, c = _place()
        me = _index((x, y, c))
        for a in range(n_g):
            g_out[a][me] = g_in[a][...].astype(BF16)
        begun = _two_level_gather_begin(x, y, c, g_out, g_send, g_recv)
        for a in range(n_k):
            k_out[a][...] = k_in[a][...].astype(BF16)

        def c_copy(k):
            p = _peer(x, y, c, k)
            return pltpu.make_async_remote_copy(
                src_ref=c_ref, dst_ref=cg.at[me], send_sem=send_sems.at[k - 1], recv_sem=recv_sems.at[k - 1],
                device_id=p, device_id_type=MESH)

        def c_arrival(k):
            p = _peer(x, y, c, k)
            return pltpu.make_async_remote_copy(
                src_ref=c_ref, dst_ref=cg.at[_index(p)], send_sem=send_sems.at[k - 1], recv_sem=recv_sems.at[k - 1],
                device_id=p, device_id_type=MESH)

        def m_copy(k):
            p = _peer(x, y, c, k)
            return pltpu.make_async_remote_copy(
                src_ref=part, dst_ref=mg.at[me], send_sem=send_sems.at[6 + k], recv_sem=recv_sems.at[6 + k],
                device_id=p, device_id_type=MESH)

        def m_arrival(k):
            p = _peer(x, y, c, k)
            return pltpu.make_async_remote_copy(
                src_ref=part, dst_ref=mg.at[_index(p)], send_sem=send_sems.at[6 + k], recv_sem=recv_sems.at[6 + k],
                device_id=p, device_id_type=MESH)

        for k in range(1, N_DEV):
            c_copy(k).start()
        cg[me] = c_ref[...]
        for k in range(1, N_DEV):
            c_arrival(k).wait_recv()
        c_all = jnp.concatenate([cg[j, 0:1, :] for j in range(N_DEV)], axis=0)
        sc = c_all * jax.nn.sigmoid(c_all)
        sc_ref[...] = sc
        part[...] = _dot(sc.astype(BF16), w_ref[...].astype(BF16)) + b_ref[...]
        for k in range(1, N_DEV):
            m_copy(k).start()
        mg[me] = part[...]
        for k in range(1, N_DEV):
            m_arrival(k).wait_recv()
        for j in range(N_DEV):
            modp_ref[j:j + 1, :] = mg[j, pl.ds(me, 1), :]
        _two_level_gather_finish(c, n_g, begun)
        for k in range(1, N_DEV):
            c_copy(k).wait_send()
            m_copy(k).wait_send()

    vm = pl.BlockSpec(memory_space=pltpu.VMEM)
    outs = pl.pallas_call(
        body, name="fwd_comm",
        out_shape=tuple([jax.ShapeDtypeStruct((N_DEV, ncol), F32), jax.ShapeDtypeStruct((N_DEV, D), F32)]
                        + [jax.ShapeDtypeStruct((N_DEV,) + s.shape, BF16) for s in gathered]
                        + [jax.ShapeDtypeStruct(s.shape, BF16) for s in kept]),
        in_specs=[vm] * (3 + n_g + n_k), out_specs=tuple([vm] * (2 + n_g + n_k)),
        scratch_shapes=[
            pltpu.VMEM((N_DEV, 8, D), F32),
            pltpu.VMEM((N_DEV, N_DEV, ncol), F32),
            pltpu.VMEM((N_DEV, ncol), F32),
            pltpu.SemaphoreType.DMA((2 * (N_DEV - 1),)),
            pltpu.SemaphoreType.DMA((2 * (N_DEV - 1),)),
            pltpu.SemaphoreType.DMA((7 * n_g,)),
            pltpu.SemaphoreType.DMA((7 * n_g,)),
        ],
        compiler_params=pltpu.CompilerParams(vmem_limit_bytes=VMEM_LIMIT),
    )(c8, w_ada, b_my, *gathered, *kept)
    return outs[0], outs[1], outs[2:2 + n_g], outs[2 + n_g:]


FC_EARLY = 6
WGRAD_ORDER = (7, 6, 1, 3, 5, 2, 4, 0)


class _Copies:
    def __init__(self, entries, send_sems, recv_sems):
        self.place = _place()
        self.entries, self.send_sems, self.recv_sems = entries, send_sems, recv_sems

    def _copy(self, i, arrival=False):
        src, dst, rel = self.entries[i]
        return pltpu.make_async_remote_copy(
            src_ref=dst if arrival else src, dst_ref=dst, send_sem=self.send_sems.at[i],
            recv_sem=self.recv_sems.at[i], device_id=_peer(*self.place, rel), device_id_type=MESH)

    def start(self, *which):
        for i in which:
            self._copy(i).start()

    def wait_recv(self, *which):
        for i in which:
            self._copy(i, arrival=True).wait_recv()

    def wait_send(self, *which):
        for i in which:
            self._copy(i).wait_send()


TAIL_STEPS = 8


def _tail_comm(parts, small, row_chunk, fc):
    n, n_fc = len(parts), len(fc)

    def body(*refs):
        p_refs, small_ref = refs[:n], refs[n]
        fc_in = refs[n + 1:n + 1 + 6 * n_fc]
        outs = refs[n + 1 + 6 * n_fc:]
        g_refs, total_ref = outs[:n], outs[n]
        fc_out = outs[n + 1:n + 1 + 4 * n_fc]
        scr = outs[n + 1 + 4 * n_fc:]
        from_sib = scr[0:n]
        chip_out = scr[n:2 * n]
        chip_in = scr[2 * n:3 * n]
        pack, pack_sib, halves, total_scr = scr[3 * n:3 * n + 4]
        send_a, recv_a, send_b, recv_b, send_s, recv_s = scr[3 * n + 4:]
        step = pl.program_id(0)
        x, y, c = _place()
        me = _index((x, y, c))
        sibling = (x, y, 1 - c)
        my_chip = 2 * x + y
        others = [(1 - x, y), (x, 1 - y), (1 - x, 1 - y)]
        my_half = pl.ds(pl.multiple_of(PACK_HALF * c, 8), PACK_HALF)

        def pack_to_sibling():
            return pltpu.make_async_remote_copy(
                src_ref=pack, dst_ref=pack_sib, send_sem=send_s.at[0], recv_sem=recv_s.at[0],
                device_id=sibling, device_id_type=MESH)

        def half_to_chip(r):
            return pltpu.make_async_remote_copy(
                src_ref=halves.at[my_chip], dst_ref=halves.at[my_chip],
                send_sem=send_s.at[1 + r], recv_sem=recv_s.at[1 + r],
                device_id=(*others[r], c), device_id_type=MESH)

        def half_from_chip(r):
            k = 2 * others[r][0] + others[r][1]
            return pltpu.make_async_remote_copy(
                src_ref=halves.at[k], dst_ref=halves.at[k], send_sem=send_s.at[1 + r], recv_sem=recv_s.at[1 + r],
                device_id=(*others[r], c), device_id_type=MESH)

        def total_to_sibling():
            return pltpu.make_async_remote_copy(
                src_ref=total_scr.at[my_half], dst_ref=total_scr.at[my_half],
                send_sem=send_s.at[4], recv_sem=recv_s.at[4], device_id=sibling, device_id_type=MESH)

        def total_from_sibling():
            sib_half = pl.ds(pl.multiple_of(PACK_HALF * (1 - c), 8), PACK_HALF)
            return pltpu.make_async_remote_copy(
                src_ref=total_scr.at[sib_half], dst_ref=total_scr.at[sib_half],
                send_sem=send_s.at[4], recv_sem=recv_s.at[4], device_id=sibling, device_id_type=MESH)

        def to_sibling(a, k):
            return pltpu.make_async_remote_copy(
                src_ref=p_refs[a].at[2 * k + (1 - c)], dst_ref=from_sib[a].at[k],
                send_sem=send_a.at[a], recv_sem=recv_a.at[a], device_id=sibling, device_id_type=MESH)

        def all_from_sibling(a):
            return pltpu.make_async_remote_copy(
                src_ref=from_sib[a], dst_ref=from_sib[a], send_sem=send_a.at[a], recv_sem=recv_a.at[a],
                device_id=sibling, device_id_type=MESH)

        def to_chip(a, r):
            return pltpu.make_async_remote_copy(
                src_ref=chip_out[a].at[r], dst_ref=chip_in[a].at[r],
                send_sem=send_b.at[3 * a + r], recv_sem=recv_b.at[3 * a + r],
                device_id=(*others[r], c), device_id_type=MESH)

        @pl.when(step == 0)
        def _():
            pack[0:TABLE_ROWS, :] = jnp.zeros((TABLE_ROWS, D), F32)
            pack[pl.ds(pl.multiple_of(8 * me, 8), 8), :] = small_ref[0:8, :]
            pack[TABLE_ROWS:PACK_ROWS, :] = small_ref[8:SMALL_ROWS, :]
            pack_to_sibling().start()
            for a in range(n):
                for k in range(4):
                    to_sibling(a, k).start()

        @pl.when(step == 1)
        def _():
            pack_to_sibling().wait_recv()
            halves[my_chip] = pack[my_half, :] + pack_sib[my_half, :]
            for r in range(3):
                half_to_chip(r).start()
            for a in range(n):
                all_from_sibling(a).wait_recv()
                rows = p_refs[a].shape[1]
                for r in range(3):
                    k = 2 * others[r][0] + others[r][1]
                    for s in range(0, rows, row_chunk):
                        sl = pl.ds(s, row_chunk)
                        chip_out[a][r, sl, :] = (p_refs[a][2 * k + c, sl, :].astype(F32)
                                                 + from_sib[a][k, sl, :].astype(F32)).astype(BF16)
                    to_chip(a, r).start()
                for s in range(0, rows, row_chunk):
                    sl = pl.ds(s, row_chunk)
                    g_refs[a][sl, :] = (p_refs[a][2 * my_chip + c, sl, :].astype(F32)
                                        + from_sib[a][my_chip, sl, :].astype(F32))

        for k in range(n_fc):
            w_ref, own_ref, arr_ref, diag_ref, m_ref, v_ref = fc_in[6 * k:6 * k + 6]
            g = own_ref[...]
            for r in range(2):
                g = g + arr_ref[r].astype(F32)
            g = g + diag_ref[...].astype(F32)
            fc_out[4 * k][...] = g
            fc_out[4 * k + 1][...], fc_out[4 * k + 2][...], fc_out[4 * k + 3][...] = _adam(
                w_ref[...], g, m_ref[...], v_ref[...])

        @pl.when(step == TAIL_STEPS - 1)
        def _():
            for r in range(3):
                half_from_chip(r).wait_recv()
            total_scr[my_half, :] = ((halves[0] + halves[1]) + halves[2]) + halves[3]
            total_to_sibling().start()
            for a in range(n):
                rows = p_refs[a].shape[1]
                for r in range(3):
                    to_chip(a, r).wait_recv()
                    for s in range(0, rows, row_chunk):
                        sl = pl.ds(s, row_chunk)
                        g_refs[a][sl, :] = g_refs[a][sl, :] + chip_in[a][r, sl, :].astype(F32)
            total_from_sibling().wait_recv()
            total_ref[...] = total_scr[...]
            for a in range(n):
                all_from_sibling(a).wait_send()
                for r in range(3):
                    to_chip(a, r).wait_send()
            pack_to_sibling().wait_send()
            for r in range(3):
                half_to_chip(r).wait_send()
            total_to_sibling().wait_send()

    fc_specs_in, fc_specs_out, fc_shapes, fc_args = [], [], [], []
    for w, own, arrived, diagonal, m, v in fc:
        rows, cols = w.shape
        blk = pl.BlockSpec((rows // TAIL_STEPS, cols), lambda i: (i, 0))
        fc_specs_in += [blk, blk, pl.BlockSpec((2, rows // TAIL_STEPS, cols), lambda i: (0, i, 0)), blk, blk, blk]
        fc_specs_out += [blk] * 4
        fc_shapes += [jax.ShapeDtypeStruct((rows, cols), F32)] * 4
        fc_args += [w, own, arrived, diagonal, m, v]
    outs = pl.pallas_call(
        body, name="tail_comm", grid=(TAIL_STEPS,),
        out_shape=tuple([jax.ShapeDtypeStruct(p.shape[1:], F32) for p in parts]
                        + [jax.ShapeDtypeStruct((PACK_ROWS, D), F32)] + fc_shapes),
        in_specs=[_resident(p.shape) for p in parts] + [_resident(small.shape)] + fc_specs_in,
        out_specs=tuple([_full(p.shape[1:]) for p in parts] + [_full((PACK_ROWS, D))] + fc_specs_out),
        scratch_shapes=(
            [pltpu.VMEM((4,) + p.shape[1:], BF16) for p in parts]
            + [pltpu.VMEM((3,) + p.shape[1:], BF16) for p in parts]
            + [pltpu.VMEM((3,) + p.shape[1:], BF16) for p in parts]
            + [pltpu.VMEM((PACK_ROWS, D), F32), pltpu.VMEM((PACK_ROWS, D), F32),
               pltpu.VMEM((4, PACK_HALF, D), F32), pltpu.VMEM((PACK_ROWS, D), F32)]
            + [pltpu.SemaphoreType.DMA((n,)), pltpu.SemaphoreType.DMA((n,)),
               pltpu.SemaphoreType.DMA((3 * n,)), pltpu.SemaphoreType.DMA((3 * n,)),
               pltpu.SemaphoreType.DMA((5,)), pltpu.SemaphoreType.DMA((5,))]),
        compiler_params=pltpu.CompilerParams(dimension_semantics=("arbitrary",), vmem_limit_bytes=VMEM_LIMIT),
    )(*parts, small, *fc_args)
    return outs[:n + 1], [outs[n + 1 + 4 * k:n + 5 + 4 * k] for k in range(n_fc)]


def _tril_mask():
    row = lax.broadcasted_iota(jnp.int32, (CHUNK, CHUNK), 0)
    col = lax.broadcasted_iota(jnp.int32, (CHUNK, CHUNK), 1)
    return (col <= row).astype(F32)


def _window_sums(ext):
    s2 = ext + pltpu.roll(ext, 1, 0)
    t4 = s2[:, GROUP:]
    s4 = t4 + pltpu.roll(t4, 2, 0)
    t8 = s4[:, GROUP:]
    s8 = t8 + pltpu.roll(t8, 4, 0)
    t16 = s8[:, GROUP:]
    s16 = t16 + pltpu.roll(t16, 8, 0)
    return [s2[:, :GROUP], s4[:, :GROUP], s8[:, :GROUP], s16]


def _inv_counts(first_pos, rows):
    pos = first_pos + lax.broadcasted_iota(jnp.int32, (rows, 1), 0)
    return [1.0 / jnp.minimum(pos + 1, w).astype(F32) for w in WINDOWS]


def _pool_diff(zb, halo, first_pos):
    tt = zb.shape[0]
    sums = _window_sums(jnp.concatenate([halo, zb], axis=0))
    inv = _inv_counts(first_pos, tt)
    return [sums[g][HALO:, :] * inv[g] - zb[:, g * GROUP:(g + 1) * GROUP] for g in range(len(WINDOWS))]


def _attn_fwd(tt, x, mod, n1pre, n1post, w_in_t, w_out, w_sp, bs_rows, ln_g, ln_b, w_pool, b_pool, pool_scale,
              fc_shards):
    t_len = x.shape[0]
    nt = t_len // tt

    def body(x_ref, mod_ref, n1pre_ref, n1post_ref, win_ref, wout_ref, wsp_ref, bs_ref, lng_ref, lnb_ref,
             wp_ref, bp_ref, ps_ref, w1_ref, w2_ref, z_ref, cat_ref, mix_ref, x1_ref, e1_ref, e2_ref,
             carry, land1, land2, send_sems, recv_sems, local_sems):
        i = pl.program_id(0)
        copies = _Copies(
            [(w1_ref, e1_ref.at[1], 1), (w2_ref, e2_ref.at[1], 1),
             (w1_ref, land1.at[0], 2), (w2_ref, land2.at[0], 2),
             (w1_ref, land1.at[1], 4), (w2_ref, land2.at[1], 4),
             (land1.at[0], e1_ref.at[3], 1), (land2.at[0], e2_ref.at[3], 1),
             (land1.at[1], e1_ref.at[5], 1), (land2.at[1], e2_ref.at[5], 1)],
            send_sems, recv_sems)
        keep = [pltpu.make_async_copy(w1_ref, e1_ref.at[0], local_sems.at[0]),
                pltpu.make_async_copy(w2_ref, e2_ref.at[0], local_sems.at[1]),
                pltpu.make_async_copy(land1.at[0], e1_ref.at[2], local_sems.at[2]),
                pltpu.make_async_copy(land1.at[1], e1_ref.at[4], local_sems.at[3]),
                pltpu.make_async_copy(land2.at[0], e2_ref.at[2], local_sems.at[4]),
                pltpu.make_async_copy(land2.at[1], e2_ref.at[4], local_sems.at[5])]

        @pl.when(i == 0)
        def _():
            copies.start(2, 4, 3, 5, 0, 1)
            keep[0].start()
            keep[1].start()
            carry[...] = jnp.zeros_like(carry)

        @pl.when(i == nt // 2)
        def _():
            copies.wait_recv(2, 4)
            copies.start(6, 8)
            keep[2].start()
            keep[3].start()

        @pl.when(i == nt - 1)
        def _():
            copies.wait_recv(3, 5)
            copies.start(7, 9)
            keep[4].start()
            keep[5].start()

        xv = x_ref[...]
        shift1, scale1, gate1 = mod_ref[0:1, :], mod_ref[1:2, :], mod_ref[2:3, :]
        h1 = (xv * _rstd(xv) * n1pre_ref[...]) * (1.0 + scale1) + shift1
        z = _dot_nt(h1.astype(BF16), win_ref[...])
        z_ref[...] = z

        _, ga = _gelu_parts(z[:, :2 * D_A])
        u, vr = ga[:, :D_A], ga[:, D_A:]
        dv = vr - jnp.mean(vr, axis=-1, keepdims=True)
        v = (dv * lax.rsqrt(jnp.mean(dv * dv, axis=-1, keepdims=True) + EPS)) * lng_ref[...] + lnb_ref[...]
        vb = v.astype(BF16)
        mask = _tril_mask()
        wc = [(wsp_ref[h] * mask).astype(BF16) for h in range(N_HEADS)]
        for ch in range(tt // CHUNK):
            rows = slice(ch * CHUNK, (ch + 1) * CHUNK)
            for h in range(N_HEADS):
                cols = slice(h * GROUP, (h + 1) * GROUP)
                mixed = _dot(wc[h], vb[rows, cols]) + bs_ref[:, cols]
                cat_ref[rows, cols] = (u[rows, cols] * mixed).astype(BF16)

        zb = z[:, 2 * D_A:]
        diff = _pool_diff(zb, carry[...], i * tt)
        carry[...] = zb[tt - HALO:, :]
        for g in range(len(WINDOWS)):
            cols = slice(g * GROUP, (g + 1) * GROUP)
            pre = _dot(diff[g].astype(BF16), wp_ref[g].astype(BF16)) + bp_ref[:, cols]
            cat_ref[:, D_A + g * GROUP:D_A + (g + 1) * GROUP] = (pre * ps_ref[:, cols]).astype(BF16)

        mix = _dot(cat_ref[...], wout_ref[...])
        mix_ref[...] = mix
        x1_ref[...] = xv + gate1 * (mix * _rstd(mix) * n1post_ref[...])

        @pl.when(i == nt - 1)
        def _():
            copies.wait_recv(0, 1, 6, 7, 8, 9)
            copies.wait_send(*range(10))
            for cp in keep:
                cp.wait()

    tile = lambda w: pl.BlockSpec((tt, w), lambda i: (i, 0))
    hbm = pl.BlockSpec(memory_space=pl.ANY)
    outs = pl.pallas_call(
        body, name="attn_fwd", grid=(nt,),
        out_shape=tuple([jax.ShapeDtypeStruct((t_len, D_Z), F32), jax.ShapeDtypeStruct((t_len, D), BF16),
                         jax.ShapeDtypeStruct((t_len, D), F32), jax.ShapeDtypeStruct((t_len, D), F32)]
                        + [jax.ShapeDtypeStruct((FC_EARLY,) + s.shape, BF16) for s in fc_shards]),
        in_specs=[tile(D), _full((8, D)), _full((1, D)), _full((1, D)), _resident((D_Z, D)), _resident((D, D)),
                  _full((N_HEADS, CHUNK, CHUNK)), _full((CHUNK, D_A)), _full((1, D_A)), _full((1, D_A)),
                  _full((len(WINDOWS), GROUP, GROUP)), _full((1, D_B)), _full((1, D_B)),
                  _resident(fc_shards[0].shape), _resident(fc_shards[1].shape)],
        out_specs=(tile(D_Z), tile(D), tile(D), tile(D), hbm, hbm),
        scratch_shapes=[pltpu.VMEM((HALO, D_B), F32),
                        pltpu.VMEM((2,) + fc_shards[0].shape, BF16), pltpu.VMEM((2,) + fc_shards[1].shape, BF16),
                        pltpu.SemaphoreType.DMA((10,)), pltpu.SemaphoreType.DMA((10,)),
                        pltpu.SemaphoreType.DMA((6,))],
        compiler_params=pltpu.CompilerParams(dimension_semantics=("arbitrary",), vmem_limit_bytes=VMEM_LIMIT),
    )(x, mod, n1pre, n1post, w_in_t, w_out, w_sp, bs_rows, ln_g, ln_b, w_pool, b_pool, pool_scale, *fc_shards)
    return outs[:4], outs[4:]


def _mlp_fwd_early(tt, x1, mod, n2pre, w1_early, w2_early):
    t_len = x1.shape[0]
    nt = t_len // tt
    n_late = N_DEV - FC_EARLY

    def body(x1_ref, mod_ref, n2pre_ref, w1_ref, w2_ref, r_ref, h2_ref, f_ref, l1_ref, l2_ref,
             land1, land2, send_sems, recv_sems, local_sems):
        i = pl.program_id(0)
        copies = _Copies(
            [(w1_ref.at[2], land1, 4), (w2_ref.at[4], land2, 2),
             (land1, l1_ref.at[1], 1), (land2, l2_ref.at[1], 1)],
            send_sems, recv_sems)
        keep = [pltpu.make_async_copy(land1, l1_ref.at[0], local_sems.at[0]),
                pltpu.make_async_copy(land2, l2_ref.at[0], local_sems.at[1])]

        @pl.when(i == 0)
        def _():
            copies.start(0, 1)

        @pl.when(i == nt // 2)
        def _():
            copies.wait_recv(0, 1)
            copies.start(2, 3)
            for cp in keep:
                cp.start()

        x1v = x1_ref[...]
        shift2, scale2 = mod_ref[3:4, :], mod_ref[4:5, :]
        h2 = ((x1v * _rstd(x1v) * n2pre_ref[...]) * (1.0 + scale2) + shift2).astype(BF16)
        h2_ref[...] = h2
        for j in range(FC_EARLY):
            cols = slice(j * FF_BLK, (j + 1) * FF_BLK)
            ra = jnp.maximum(_dot(h2, w1_ref[j]), 0.0)
            r = (ra * ra).astype(BF16)
            r_ref[:, cols] = r
            contrib = _dot(r, w2_ref[j])
            if j == 0:
                f_ref[...] = contrib
            else:
                f_ref[...] += contrib

        @pl.when(i == nt - 1)
        def _():
            copies.wait_recv(2, 3)
            copies.wait_send(0, 1, 2, 3)
            for cp in keep:
                cp.wait()

    tile = lambda w: pl.BlockSpec((tt, w), lambda i: (i, 0))
    hbm = pl.BlockSpec(memory_space=pl.ANY)
    outs = pl.pallas_call(
        body, name="mlp_fwd_early", grid=(nt,),
        out_shape=(jax.ShapeDtypeStruct((t_len, D_FF), BF16),
                   jax.ShapeDtypeStruct((t_len, D), BF16), jax.ShapeDtypeStruct((t_len, D), F32),
                   jax.ShapeDtypeStruct((n_late,) + w1_early.shape[1:], BF16),
                   jax.ShapeDtypeStruct((n_late,) + w2_early.shape[1:], BF16)),
        in_specs=[tile(D), _full((8, D)), _full((1, D)),
                  _resident((FC_EARLY, D, FF_BLK)), _resident((FC_EARLY, FF_BLK, D))],
        out_specs=(tile(FC_EARLY * FF_BLK), tile(D), tile(D), hbm, hbm),
        scratch_shapes=[pltpu.VMEM(w1_early.shape[1:], BF16), pltpu.VMEM(w2_early.shape[1:], BF16),
                        pltpu.SemaphoreType.DMA((4,)), pltpu.SemaphoreType.DMA((4,)),
                        pltpu.SemaphoreType.DMA((2,))],
        compiler_params=pltpu.CompilerParams(dimension_semantics=("arbitrary",), vmem_limit_bytes=VMEM_LIMIT),
    )(x1, mod, n2pre, w1_early, w2_early)
    return outs[:3], outs[3:]


def _mlp_fwd_late(tt, r_all, x1, h2, f_early, tgt, mod, n2post, w1_late, w2_late):
    t_len = x1.shape[0]
    nt = t_len // tt
    n_late = N_DEV - FC_EARLY

    def body(r_all_ref, x1_ref, h2_ref, fe_ref, tgt_ref, mod_ref, n2post_ref, w1_ref, w2_ref,
             r_ref, df_ref, dy_ref, red_ref):
        i = pl.program_id(0)

        @pl.when(i == 0)
        def _():
            red_ref[...] = jnp.zeros_like(red_ref)

        x1v = x1_ref[...]
        gate2 = mod_ref[5:6, :]
        h2 = h2_ref[...]
        f = fe_ref[...]
        for j in range(n_late):
            cols = slice(j * FF_BLK, (j + 1) * FF_BLK)
            ra = jnp.maximum(_dot(h2, w1_ref[j]), 0.0)
            r = (ra * ra).astype(BF16)
            r_ref[:, cols] = r
            f = f + _dot(r, w2_ref[j])
        rf = _rstd(f)
        fhat = f * rf
        nf = fhat * n2post_ref[...]
        err = (x1v + gate2 * nf) - tgt_ref[...]
        dy = err * (1.0 / D)
        dy_ref[...] = dy
        dnf = dy * gate2
        df_ref[...] = _rms_bwd(dnf * n2post_ref[...], fhat, rf).astype(BF16)
        red_ref[0:1, :] += _colsum(dy * nf)
        red_ref[1:2, :] += _colsum(dnf * fhat)
        red_ref[2:3, :] += _colsum(0.5 * jnp.mean(err * err, axis=-1, keepdims=True)) * jnp.ones((1, D), F32)

    tile = lambda w: pl.BlockSpec((tt, w), lambda i: (i, 0))
    return pl.pallas_call(
        body, name="mlp_fwd_late", grid=(nt,),
        out_shape=(jax.ShapeDtypeStruct((t_len, D_FF), BF16), jax.ShapeDtypeStruct((t_len, D), BF16),
                   jax.ShapeDtypeStruct((t_len, D), F32), jax.ShapeDtypeStruct((8, D), F32)),
        in_specs=[pl.BlockSpec(memory_space=pl.ANY), tile(D), tile(D), tile(D), tile(D), _full((8, D)),
                  _full((1, D)), _resident((n_late, D, FF_BLK)), _resident((n_late, FF_BLK, D))],
        out_specs=(pl.BlockSpec((tt, n_late * FF_BLK), lambda i: (i, FC_EARLY // n_late)), tile(D), tile(D),
                   _full((8, D))),
        input_output_aliases={0: 0},
        compiler_params=pltpu.CompilerParams(dimension_semantics=("arbitrary",), vmem_limit_bytes=VMEM_LIMIT),
    )(r_all, x1, h2, f_early, tgt, mod, n2post, w1_late, w2_late)


def _mlp_bwd(tt, df, r, dy, x1, mix, mod, n2pre, n1post, w1_early, w2_early, w1_late, w2_late):
    t_len = x1.shape[0]
    nt = t_len // tt
    n_late = N_DEV - FC_EARLY

    def body(df_ref, r_ref, dy_ref, x1_ref, mix_ref, mod_ref, n2pre_ref, n1post_ref,
             w1e_ref, w2e_ref, w1l_ref, w2l_ref, da_ref, dmix_ref, dx1_ref, red_ref, dh2_acc):
        i = pl.program_id(0)

        @pl.when(i == 0)
        def _():
            red_ref[...] = jnp.zeros_like(red_ref)

        dfv = df_ref[...]
        for j in range(N_DEV):
            cols = slice(j * FF_BLK, (j + 1) * FF_BLK)
            if j < FC_EARLY:
                w1, w2 = w1e_ref[j], w2e_ref[j]
            else:
                w1, w2 = w1l_ref[j - FC_EARLY], w2l_ref[j - FC_EARLY]
            dr = _dot_nt(dfv, w2)
            da = (dr * (2.0 * jnp.sqrt(r_ref[:, cols].astype(F32)))).astype(BF16)
            da_ref[:, cols] = da
            contrib = _dot_nt(da, w1)
            if j == 0:
                dh2_acc[...] = contrib
            else:
                dh2_acc[...] += contrib
        dh2 = dh2_acc[...]
        gate1, scale2 = mod_ref[2:3, :], mod_ref[4:5, :]
        x1v = x1_ref[...]
        r2 = _rstd(x1v)
        xhat = x1v * r2
        n2 = xhat * n2pre_ref[...]
        dn2 = dh2 * (1.0 + scale2)
        dx1 = dy_ref[...] + _rms_bwd(dn2 * n2pre_ref[...], xhat, r2)
        dx1_ref[...] = dx1
        mixv = mix_ref[...]
        rm = _rstd(mixv)
        mhat = mixv * rm
        dnm = dx1 * gate1
        dmix_ref[...] = _rms_bwd(dnm * n1post_ref[...], mhat, rm).astype(BF16)
        red_ref[0:1, :] += _colsum(dh2)
        red_ref[1:2, :] += _colsum(dh2 * n2)
        red_ref[2:3, :] += _colsum(dn2 * xhat)
        red_ref[3:4, :] += _colsum(dx1 * (mhat * n1post_ref[...]))
        red_ref[4:5, :] += _colsum(dnm * mhat)

    tile = lambda w: pl.BlockSpec((tt, w), lambda i: (i, 0))
    return pl.pallas_call(
        body, name="mlp_bwd", grid=(nt,),
        out_shape=(jax.ShapeDtypeStruct((t_len, D_FF), BF16),
                   jax.ShapeDtypeStruct((t_len, D), BF16), jax.ShapeDtypeStruct((t_len, D), F32),
                   jax.ShapeDtypeStruct((8, D), F32)),
        in_specs=[tile(D), tile(D_FF), tile(D), tile(D), tile(D),
                  _full((8, D)), _full((1, D)), _full((1, D)),
                  _resident((FC_EARLY, D, FF_BLK)), _resident((FC_EARLY, FF_BLK, D)),
                  _resident((n_late, D, FF_BLK)), _resident((n_late, FF_BLK, D))],
        out_specs=(tile(D_FF), tile(D), tile(D), _full((8, D))),
        scratch_shapes=[pltpu.VMEM((tt, D), F32)],
        compiler_params=pltpu.CompilerParams(dimension_semantics=("arbitrary",), vmem_limit_bytes=VMEM_LIMIT),
    )(df, r, dy, x1, mix, mod, n2pre, n1post, w1_early, w2_early, w1_late, w2_late)


def _mlp_wgrad(tt, r, da, df, h2):
    t_len = df.shape[0]
    nt = t_len // tt
    odd_steps = [j for j, rel in enumerate(WGRAD_ORDER) if rel % 2]

    def relation(j):
        rel = jnp.int32(WGRAD_ORDER[-1])
        for step in range(N_DEV - 2, -1, -1):
            rel = jnp.where(j == step, WGRAD_ORDER[step], rel)
        return rel

    def body(r_ref, da_ref, df_ref, h2_ref, own1_ref, own2_ref, out1_ref, out2_ref, diag1_ref, diag2_ref,
             acc1, acc2, snd1, snd2, sib1, sib2, dsnd1, dsnd2, send_sems, recv_sems):
        j, t = pl.program_id(0), pl.program_id(1)
        rows = pl.ds(pl.multiple_of(t * tt, tt), tt)
        x, y, c = _place()
        accs, snds, sibs = (acc1, acc2), (snd1, snd2), (sib1, sib2)
        dsnds, diags = (dsnd1, dsnd2), (diag1_ref, diag2_ref)

        def to_sibling(a, jj, buf=0):
            return pltpu.make_async_remote_copy(
                src_ref=snds[a].at[buf], dst_ref=sibs[a].at[jj],
                send_sem=send_sems.at[4 * a + jj], recv_sem=recv_sems.at[4 * a + jj],
                device_id=(x, y, 1 - c), device_id_type=MESH)

        def to_diagonal(a):
            return pltpu.make_async_remote_copy(
                src_ref=dsnds[a], dst_ref=diags[a], send_sem=send_sems.at[8 + a], recv_sem=recv_sems.at[8 + a],
                device_id=_peer(x, y, c, 6), device_id_type=MESH)

        @pl.when(t == 0)
        def _():
            acc2[...] = jnp.zeros_like(acc2)
            acc1[...] = jnp.zeros_like(acc1)

        acc2[...] += _dot_tn(r_ref[...], df_ref[rows, :])
        acc1[...] += _dot_tn(h2_ref[rows, :], da_ref[...])

        for step, rel in enumerate(WGRAD_ORDER):
            jj = rel // 2

            @pl.when((t == nt - 1) & (j == step))
            def _():
                for a, (own_ref, out_ref) in enumerate(((own1_ref, out1_ref), (own2_ref, out2_ref))):
                    if rel % 2:
                        q = odd_steps.index(step)
                        if q >= 2:
                            to_sibling(a, WGRAD_ORDER[odd_steps[q - 2]] // 2).wait_send()
                        snds[a][q % 2] = accs[a][...].astype(BF16)
                        to_sibling(a, jj, q % 2).start()
                        continue
                    to_sibling(a, jj).wait_recv()
                    chip_sum = accs[a][...] + sibs[a][jj].astype(F32)
                    if rel == 6:
                        dsnds[a][...] = chip_sum.astype(BF16)
                        to_diagonal(a).start()
                    elif rel == 0:
                        own_ref[...] = chip_sum
                    else:
                        out_ref[0] = chip_sum.astype(BF16)
                    if step == N_DEV - 1:
                        for q in (2, 3):
                            to_sibling(a, WGRAD_ORDER[odd_steps[q]] // 2).wait_send()
                        to_diagonal(a).wait_recv()
                        to_diagonal(a).wait_send()

    assert WGRAD_ORDER[-1] == 0 and WGRAD_ORDER[-3:-1] == (2, 4)
    blk = pl.BlockSpec((tt, FF_BLK), lambda j, t: (t, relation(j)))
    chip = lambda j, t: (jnp.clip(j - 5, 0, 1), 0, 0)
    hbm = pl.BlockSpec(memory_space=pl.ANY)
    return pl.pallas_call(
        body, name="mlp_wgrad", grid=(N_DEV, nt),
        out_shape=(jax.ShapeDtypeStruct((D, FF_BLK), F32), jax.ShapeDtypeStruct((FF_BLK, D), F32),
                   jax.ShapeDtypeStruct((2, D, FF_BLK), BF16), jax.ShapeDtypeStruct((2, FF_BLK, D), BF16),
                   jax.ShapeDtypeStruct((D, FF_BLK), BF16), jax.ShapeDtypeStruct((FF_BLK, D), BF16)),
        in_specs=[blk, blk, _resident((t_len, D)), _resident((t_len, D))],
        out_specs=(_full((D, FF_BLK)), _full((FF_BLK, D)),
                   pl.BlockSpec((1, D, FF_BLK), chip), pl.BlockSpec((1, FF_BLK, D), chip), hbm, hbm),
        scratch_shapes=[pltpu.VMEM((D, FF_BLK), F32), pltpu.VMEM((FF_BLK, D), F32),
                        pltpu.VMEM((2, D, FF_BLK), BF16), pltpu.VMEM((2, FF_BLK, D), BF16),
                        pltpu.VMEM((4, D, FF_BLK), BF16), pltpu.VMEM((4, FF_BLK, D), BF16),
                        pltpu.VMEM((D, FF_BLK), BF16), pltpu.VMEM((FF_BLK, D), BF16),
                        pltpu.SemaphoreType.DMA((10,)), pltpu.SemaphoreType.DMA((10,))],
        compiler_params=pltpu.CompilerParams(dimension_semantics=("arbitrary", "arbitrary"),
                                             vmem_limit_bytes=VMEM_LIMIT),
    )(r, da, df, h2)


def _acc_rows(ref, row0, k, val):
    half = CHUNK // 2
    ref[row0:row0 + half, k * GROUP:(k + 1) * GROUP] += val[:half, :]
    ref[row0:row0 + half, D_A + k * GROUP:D_A + (k + 1) * GROUP] += val[half:, :]


def _attn_bwd(tt, dmix, dx1, x, z, cat, mod, n1pre, w_in_t, w_out, w_sp, bs_rows, ln_g, ln_b, w_pool, b_pool,
              pool_scale, red_fwd, red_bwd, chip_sums):
    t_len = x.shape[0]
    nt = t_len // tt
    hb = tt // HALO
    n_sums = len(chip_sums)

    def body(dmix_ref, dx1_ref, x_ref, z_ref, zprev_ref, cat_ref, mod_ref, n1pre_ref, win_ref, wout_ref, wsp_ref,
             bs_ref, lng_ref, lnb_ref, wp_ref, bp_ref, ps_ref, redf_ref, redb_ref, *rest):
        sum_out = rest[:n_sums]
        gx_ref, gwin_ref, gwout_ref, small_ref = rest[n_sums:n_sums + 4]
        sum_in = rest[n_sums + 4:2 * n_sums + 4]
        carry, acc_in, acc_out, dz_scr, bs_acc, send_sems, recv_sems = rest[2 * n_sums + 4:]
        s = pl.program_id(0)
        i = nt - 1 - s
        px, py, pc = _place()

        def chip_copy(a, r):
            return pltpu.make_async_remote_copy(
                src_ref=sum_out[a].at[r], dst_ref=sum_in[a].at[r],
                send_sem=send_sems.at[2 * a + r], recv_sem=recv_sems.at[2 * a + r],
                device_id=_peer(px, py, pc, 2 * (r + 1)), device_id_type=MESH)

        @pl.when(s == 0)
        def _():
            for a in range(n_sums):
                for r in range(2):
                    chip_copy(a, r).start()
            carry[...] = jnp.zeros_like(carry)
            acc_in[...] = jnp.zeros_like(acc_in)
            acc_out[...] = jnp.zeros_like(acc_out)
            bs_acc[...] = jnp.zeros_like(bs_acc)
            small_ref[...] = jnp.zeros_like(small_ref)
            small_ref[ROW_DMOD + 2:ROW_DMOD + 3, :] = redb_ref[3:4, :]
            small_ref[ROW_DMOD + 3:ROW_DMOD + 5, :] = redb_ref[0:2, :]
            small_ref[ROW_DMOD + 5:ROW_DMOD + 6, :] = redf_ref[0:1, :]
            small_ref[ROW_N1POST:ROW_N1POST + 1, :] = redb_ref[4:5, :]
            small_ref[ROW_N2PRE:ROW_N2PRE + 1, :] = redb_ref[2:3, :]
            small_ref[ROW_N2POST:ROW_N2POST + 1, :] = redf_ref[1:2, :]
            small_ref[ROW_LOSS:ROW_LOSS + 1, :] = redf_ref[2:3, :]

        dmixv = dmix_ref[...]
        dcat = _dot_nt(dmixv, wout_ref[...])
        acc_out[...] += _dot_tn(cat_ref[...], dmixv)

        z = z_ref[...]
        t_g, ga = _gelu_parts(z[:, :2 * D_A])
        u, vr = ga[:, :D_A], ga[:, D_A:]
        dv0 = vr - jnp.mean(vr, axis=-1, keepdims=True)
        rv = lax.rsqrt(jnp.mean(dv0 * dv0, axis=-1, keepdims=True) + EPS)
        vhat = dv0 * rv
        vb = (vhat * lng_ref[...] + lnb_ref[...]).astype(BF16)
        mask = _tril_mask()
        wc = [(wsp_ref[h] * mask).astype(BF16) for h in range(N_HEADS)]

        dya = dcat[:, :D_A]
        for h in range(N_HEADS):
            cols = slice(h * GROUP, (h + 1) * GROUP)
            bs_sum = jnp.zeros((CHUNK, GROUP), F32)
            ws_sum = jnp.zeros((CHUNK, CHUNK), F32)
            for ch in range(tt // CHUNK):
                rows = slice(ch * CHUNK, (ch + 1) * CHUNK)
                v_ch = vb[rows, cols]
                mixed = _dot(wc[h], v_ch) + bs_ref[:, cols]
                dy_ch = dya[rows, cols]
                dz_scr[rows, cols] = dy_ch * mixed
                dmixed = dy_ch * u[rows, cols]
                dmb = dmixed.astype(BF16)
                dz_scr[rows, D_A + h * GROUP:D_A + (h + 1) * GROUP] = _dot_tn(wc[h], dmb)
                bs_sum = bs_sum + dmixed
                ws_sum = ws_sum + _dot_nt(dmb, v_ch)
            _acc_rows(bs_acc, 0, h, bs_sum)
            _acc_rows(small_ref, ROW_WS, h, ws_sum)

        dvl = dz_scr[:, D_A:2 * D_A]
        dvhat = dvl * lng_ref[...]
        dvr = rv * (dvhat - jnp.mean(dvhat, axis=-1, keepdims=True)
                    - vhat * jnp.mean(dvhat * vhat, axis=-1, keepdims=True))
        small_ref[ROW_LN:ROW_LN + 1, 0:D_A] += _colsum(dvl * vhat)
        small_ref[ROW_LN:ROW_LN + 1, D_A:D] += _colsum(dvl)
        dga = jnp.concatenate([dz_scr[:, :D_A], dvr], axis=1)
        dza = dga * _gelu_grad(z[:, :2 * D_A], t_g)

        zb = z[:, 2 * D_A:]
        halo_prev = jnp.where(i == 0, 0.0, zprev_ref[...])
        diff = _pool_diff(zb, halo_prev, i * tt)
        dyb = dcat[:, D_A:]
        inv = _inv_counts(i * tt, tt)
        scaled, ddiffs = [], []
        for g in range(len(WINDOWS)):
            cols = slice(g * GROUP, (g + 1) * GROUP)
            db = diff[g].astype(BF16)
            wpg = wp_ref[g].astype(BF16)
            pre = _dot(db, wpg) + bp_ref[:, cols]
            small_ref[ROW_POOL:ROW_POOL + 1, cols] += _colsum(dyb[:, cols] * pre)
            dpre = dyb[:, cols] * ps_ref[:, cols]
            small_ref[ROW_POOL:ROW_POOL + 1, D_B + g * GROUP:D_B + (g + 1) * GROUP] += _colsum(dpre)
            dpb = dpre.astype(BF16)
            _acc_rows(small_ref, ROW_WP, g, _dot_tn(db, dpb))
            ddiff = _dot_nt(dpb, wpg)
            ddiffs.append(ddiff)
            scaled.append(ddiff * inv[g])
        scaled_all = jnp.concatenate(scaled, axis=1)
        ext = jnp.concatenate([scaled_all, carry[...]], axis=0)
        n_ext = tt + HALO
        s2 = ext + pltpu.roll(ext, n_ext - 1, 0)
        t4 = s2[:, GROUP:]
        s4 = t4 + pltpu.roll(t4, n_ext - 2, 0)
        t8 = s4[:, GROUP:]
        s8 = t8 + pltpu.roll(t8, n_ext - 4, 0)
        t16 = s8[:, GROUP:]
        s16 = t16 + pltpu.roll(t16, n_ext - 8, 0)
        back = [s2[:, :GROUP], s4[:, :GROUP], s8[:, :GROUP], s16]
        carry[...] = scaled_all[:HALO, :]
        dzb = jnp.concatenate([back[g][:tt, :] - ddiffs[g] for g in range(len(WINDOWS))], axis=1)

        dzv = jnp.concatenate([dza, dzb], axis=1).astype(BF16)
        dh1 = _dot(dzv, win_ref[...])
        xv = x_ref[...]
        r1 = _rstd(xv)
        xhat = xv * r1
        shift1, scale1 = mod_ref[0:1, :], mod_ref[1:2, :]
        n1 = xhat * n1pre_ref[...]
        h1 = (n1 * (1.0 + scale1) + shift1).astype(BF16)
        acc_in[...] += _dot_tn(dzv, h1)
        dn1 = dh1 * (1.0 + scale1)
        gx_ref[...] = dx1_ref[...] + _rms_bwd(dn1 * n1pre_ref[...], xhat, r1)
        small_ref[ROW_DMOD:ROW_DMOD + 1, :] += _colsum(dh1)
        small_ref[ROW_DMOD + 1:ROW_DMOD + 2, :] += _colsum(dh1 * n1)
        small_ref[ROW_N1PRE:ROW_N1PRE + 1, :] += _colsum(dn1 * xhat)

        @pl.when(s == nt - 1)
        def _():
            gwin_ref[...] = acc_in[...].astype(BF16)
            gwout_ref[...] = acc_out[...].astype(BF16)
            bs = _unfold(bs_acc[...])
            for h in range(N_HEADS):
                small_ref[ROW_BS + h:ROW_BS + h + 1, 0:GROUP] = jnp.sum(
                    bs[:, h * GROUP:(h + 1) * GROUP].T, axis=0, keepdims=True)
            for a in range(n_sums):
                for r in range(2):
                    chip_copy(a, r).wait_recv()
                    chip_copy(a, r).wait_send()

    rev = lambda w: pl.BlockSpec((tt, w), lambda s: (nt - 1 - s, 0))
    zprev = pl.BlockSpec((HALO, D_B), lambda s: (jnp.maximum((nt - 1 - s) * hb - 1, 0), 2))
    hbm = pl.BlockSpec(memory_space=pl.ANY)
    outs = pl.pallas_call(
        body, name="attn_bwd", grid=(nt,),
        out_shape=tuple([jax.ShapeDtypeStruct((t_len, D), F32), jax.ShapeDtypeStruct((D_Z, D), BF16),
                         jax.ShapeDtypeStruct((D, D), BF16), jax.ShapeDtypeStruct((SMALL_ROWS, D), F32)]
                        + [jax.ShapeDtypeStruct(cs.shape, cs.dtype) for cs in chip_sums]),
        in_specs=[rev(D), rev(D), rev(D), rev(D_Z), zprev, rev(D), _full((8, D)), _full((1, D)),
                  _resident((D_Z, D)), _resident((D, D)), _full((N_HEADS, CHUNK, CHUNK)), _full((CHUNK, D_A)),
                  _full((1, D_A)), _full((1, D_A)), _full((len(WINDOWS), GROUP, GROUP)), _full((1, D_B)),
                  _full((1, D_B)), _full((8, D)), _full((8, D))] + [_resident(cs.shape) for cs in chip_sums],
        out_specs=tuple([rev(D), _resident((D_Z, D)), _resident((D, D)), _full((SMALL_ROWS, D))] + [hbm] * n_sums),
        scratch_shapes=[pltpu.VMEM((HALO, D_B), F32), pltpu.VMEM((D_Z, D), F32), pltpu.VMEM((D, D), F32),
                        pltpu.VMEM((tt, 2 * D_A), F32), pltpu.VMEM((CHUNK // 2, D), F32),
                        pltpu.SemaphoreType.DMA((2 * n_sums,)), pltpu.SemaphoreType.DMA((2 * n_sums,))],
        compiler_params=pltpu.CompilerParams(dimension_semantics=("arbitrary",), vmem_limit_bytes=VMEM_LIMIT),
    )(dmix, dx1, x, z, z, cat, mod, n1pre, w_in_t, w_out, w_sp, bs_rows, ln_g, ln_b, w_pool, b_pool, pool_scale,
      red_fwd, red_bwd, *chip_sums)
    return outs[:4], outs[4:]


def _adam(w, g, m, v):
    m2 = ADAM_B1 * m + (1.0 - ADAM_B1) * g
    v2 = ADAM_B2 * v + (1.0 - ADAM_B2) * (g * g)
    m_hat = m2 / (1.0 - ADAM_B1 ** ADAM_STEP)
    v_hat = v2 / (1.0 - ADAM_B2 ** ADAM_STEP)
    delta = -ADAM_LR * (m_hat / (jnp.sqrt(v_hat) + ADAM_EPS) + ADAM_WD * w)
    return delta, m2, v2


def _adamw_shard(name, rb, w, g, m, v):
    rows, cols = w.shape

    def body(w_ref, g_ref, m_ref, v_ref, d_ref, m2_ref, v2_ref):
        d_ref[...], m2_ref[...], v2_ref[...] = _adam(w_ref[...], g_ref[...], m_ref[...], v_ref[...])

    blk = pl.BlockSpec((rb, cols), lambda i: (i, 0))
    shp = jax.ShapeDtypeStruct((rows, cols), F32)
    return pl.pallas_call(
        body, name=name, grid=(rows // rb,), out_shape=(shp, shp, shp),
        in_specs=[blk] * 4, out_specs=(blk, blk, blk),
        compiler_params=pltpu.CompilerParams(dimension_semantics=("arbitrary",)),
    )(w, g, m, v)


def _adamw_ada(rb, w, sc, dmod_cols, m, v):
    rows, cols = w.shape

    def body(w_ref, sc_ref, dm_ref, m_ref, v_ref, g_ref, d_ref, m2_ref, v2_ref):
        g = _dot_tn(sc_ref[...].astype(BF16), dm_ref[...].astype(BF16))
        g_ref[...] = g
        d_ref[...], m2_ref[...], v2_ref[...] = _adam(w_ref[...], g, m_ref[...], v_ref[...])

    blk = pl.BlockSpec((rb, cols), lambda i: (i, 0))
    shp = jax.ShapeDtypeStruct((rows, cols), F32)
    return pl.pallas_call(
        body, name="adamw_ada", grid=(rows // rb,), out_shape=(shp, shp, shp, shp),
        in_specs=[blk, pl.BlockSpec((N_DEV, rb), lambda i: (0, i)), _full((N_DEV, cols)), blk, blk],
        out_specs=(blk, blk, blk, blk),
        compiler_params=pltpu.CompilerParams(dimension_semantics=("arbitrary",)),
    )(w, sc, dmod_cols, m, v)


def _unfold(acc_rows):
    return jnp.concatenate([acc_rows[:, :D_A], acc_rows[:, D_A:]], axis=0)


def _adamw_small(total, params):
    n = len(params)
    flat = [a for p in params for a in p]

    def body(*refs):
        s_ref = refs[0]
        p_refs = refs[1:1 + 3 * n]
        loss_ref = refs[1 + 3 * n]
        o_refs = refs[2 + 3 * n:]
        d_b_ada = s_ref[0:6, :]
        for b in range(1, N_DEV):
            d_b_ada = d_b_ada + s_ref[8 * b:8 * b + 6, :]
        tot = s_ref[PACK_SHIFT:PACK_ROWS, :]
        loss_ref[...] = jnp.broadcast_to(tot[ROW_LOSS:ROW_LOSS + 1, 0:GROUP], (8, GROUP))
        mask = _tril_mask()
        ws = _unfold(tot[ROW_WS:ROW_WS + 64, :])
        wp = _unfold(tot[ROW_WP:ROW_WP + 64, :])
        grads = [
            d_b_ada,
            tot[ROW_N1PRE:ROW_N1PRE + 1, :], tot[ROW_N1POST:ROW_N1POST + 1, :],
            tot[ROW_N2PRE:ROW_N2PRE + 1, :], tot[ROW_N2POST:ROW_N2POST + 1, :],
            tot[ROW_LN:ROW_LN + 1, :D_A], tot[ROW_LN:ROW_LN + 1, D_A:],
            tot[ROW_POOL:ROW_POOL + 1, :D_B], tot[ROW_POOL:ROW_POOL + 1, D_B:],
            tot[ROW_BS:ROW_BS + N_HEADS, 0:GROUP],
            jnp.stack([ws[:, h * GROUP:(h + 1) * GROUP] * mask for h in range(N_HEADS)]),
            jnp.stack([wp[:, g * GROUP:(g + 1) * GROUP] for g in range(len(WINDOWS))]),
        ]
        for k in range(n):
            w_ref, m_ref, v_ref = p_refs[3 * k:3 * k + 3]
            g = grads[k]
            o_refs[4 * k][...] = g
            o_refs[4 * k + 1][...], o_refs[4 * k + 2][...], o_refs[4 * k + 3][...] = _adam(
                w_ref[...], g, m_ref[...], v_ref[...])

    vm = pl.BlockSpec(memory_space=pltpu.VMEM)
    out_shape = [jax.ShapeDtypeStruct((8, GROUP), F32)]
    for w, _, _ in params:
        out_shape += [jax.ShapeDtypeStruct(w.shape, F32)] * 4
    return pl.pallas_call(
        body, name="adamw_small", out_shape=tuple(out_shape),
        in_specs=[vm] * (1 + 3 * n), out_specs=tuple([vm] * len(out_shape)),
    )(total, *flat)


TT_ATTN_FWD = 512
TT_MLP_FWD = 512
TT_MLP = 256
TT_WGRAD = 2048
TT_ATTN_BWD = 512


def kernel(x, c, w_ada, b_ada, norm1_pre, norm1_post, w_in, w_spatial, b_spatial, ln_v_gain, ln_v_bias, w_pool, b_pool, pool_scale, w_out, norm2_pre, norm2_post, w_fc1, w_fc2, loss_target, m_w_ada, m_b_ada, m_norm1_pre, m_norm1_post, m_w_in, m_w_spatial, m_b_spatial, m_ln_v_gain, m_ln_v_bias, m_w_pool, m_b_pool, m_pool_scale, m_w_out, m_norm2_pre, m_norm2_post, m_w_fc1, m_w_fc2, v_w_ada, v_b_ada, v_norm1_pre, v_norm1_post, v_w_in, v_w_spatial, v_b_spatial, v_ln_v_gain, v_ln_v_bias, v_w_pool, v_b_pool, v_pool_scale, v_w_out, v_norm2_pre, v_norm2_post, v_w_fc1, v_w_fc2):
    t_len = x.shape[1]
    me = 4 * lax.axis_index("x") + 2 * lax.axis_index("y") + lax.axis_index("c")
    ada_cols = w_ada.shape[1]
    tt = lambda want: min(want, t_len)

    x2 = x.reshape(t_len, D)
    tgt = loss_target.reshape(t_len, D)
    row = lambda a: a.reshape(1, -1)

    b_my = lax.dynamic_slice_in_dim(b_ada, me * ada_cols, ada_cols).reshape(1, ada_cols)
    modp, sc, (g_in, g_out), fc_shards = _fwd_comm(jnp.broadcast_to(c, (8, D)), w_ada, b_my,
                                                   [w_in.T, w_out], [w_fc1, w_fc2])
    mod = jnp.concatenate([modp.reshape(6, D), jnp.zeros((2, D), F32)], axis=0)
    w_in_t = g_in.reshape(D_Z, D)
    w_out_all = g_out.reshape(D, D)

    bs_rows = jnp.repeat(b_spatial.T, GROUP, axis=1)
    attn_consts = (w_spatial, bs_rows, row(ln_v_gain), row(ln_v_bias), w_pool, row(b_pool), row(pool_scale))

    (z, cat, mix, x1), (w1_early, w2_early) = _attn_fwd(
        tt(TT_ATTN_FWD), x2, mod, row(norm1_pre), row(norm1_post), w_in_t, w_out_all, *attn_consts, fc_shards)
    (r_early, h2, f_early), (w1_late, w2_late) = _mlp_fwd_early(
        tt(TT_MLP_FWD), x1, mod, row(norm2_pre), w1_early, w2_early)
    r, df, dy, red_fwd = _mlp_fwd_late(tt(TT_MLP_FWD), r_early, x1, h2, f_early, tgt, mod, row(norm2_post),
                                       w1_late, w2_late)
    da, dmix, dx1, red_bwd = _mlp_bwd(tt(TT_MLP), df, r, dy, x1, mix, mod, row(norm2_pre), row(norm1_post),
                                      w1_early, w2_early, w1_late, w2_late)
    own_w1, own_w2, sums_w1, sums_w2, diag_w1, diag_w2 = _mlp_wgrad(tt(TT_WGRAD), r, da, df, h2)
    (grad_x, p_in, p_out, small), (arr_w1, arr_w2) = _attn_bwd(
        tt(TT_ATTN_BWD), dmix, dx1, x2, z, cat, mod, row(norm1_pre), w_in_t, w_out_all, *attn_consts,
        red_fwd, red_bwd, [sums_w1, sums_w2])
    (grad_in_t, grad_out, total), ((grad_w1, d_w1, m_w1, v_w1), (grad_w2, d_w2, m_w2, v_w2)) = _tail_comm(
        [p_in.reshape(N_DEV, D_Z // N_DEV, D), p_out.reshape(N_DEV, D // N_DEV, D)], small, 64,
        [(w_fc1, own_w1, arr_w1, diag_w1, m_w_fc1, v_w_fc1), (w_fc2, own_w2, arr_w2, diag_w2, m_w_fc2, v_w_fc2)])

    d_out, m_out, v_out = _adamw_shard("adamw_out", 128, w_out, grad_out, m_w_out, v_w_out)
    d_in_t, m_in_t, v_in_t = _adamw_shard("adamw_in", D_Z // N_DEV, w_in.T, grad_in_t, m_w_in.T, v_w_in.T)
    dmod_all = total[0:TABLE_ROWS, :].reshape(N_DEV, 8, D)[:, :6, :].reshape(N_DEV, 6 * D)
    dmod_cols = lax.dynamic_slice_in_dim(dmod_all, me * ada_cols, ada_cols, axis=1)
    grad_ada, d_ada, m_ada, v_ada = _adamw_ada(256, w_ada, sc, dmod_cols, m_w_ada, v_w_ada)

    six = lambda a: a.reshape(6, D)
    small_params = [
        (six(b_ada), six(m_b_ada), six(v_b_ada)),
        (row(norm1_pre), row(m_norm1_pre), row(v_norm1_pre)),
        (row(norm1_post), row(m_norm1_post), row(v_norm1_post)),
        (row(norm2_pre), row(m_norm2_pre), row(v_norm2_pre)),
        (row(norm2_post), row(m_norm2_post), row(v_norm2_post)),
        (row(ln_v_gain), row(m_ln_v_gain), row(v_ln_v_gain)),
        (row(ln_v_bias), row(m_ln_v_bias), row(v_ln_v_bias)),
        (row(pool_scale), row(m_pool_scale), row(v_pool_scale)),
        (row(b_pool), row(m_b_pool), row(v_b_pool)),
        (b_spatial, m_b_spatial, v_b_spatial),
        (w_spatial, m_w_spatial, v_w_spatial),
        (w_pool, m_w_pool, v_w_pool),
    ]
    outs = _adamw_small(total, small_params)
    loss = outs[0][0, 0]
    names = ["b_ada", "norm1_pre", "norm1_post", "norm2_pre", "norm2_post", "ln_v_gain", "ln_v_bias", "pool_scale",
             "b_pool", "b_spatial", "w_spatial", "w_pool"]
    shapes = dict(b_ada=b_ada.shape, norm1_pre=norm1_pre.shape, norm1_post=norm1_post.shape,
                  norm2_pre=norm2_pre.shape, norm2_post=norm2_post.shape, ln_v_gain=ln_v_gain.shape,
                  ln_v_bias=ln_v_bias.shape, pool_scale=pool_scale.shape, b_pool=b_pool.shape,
                  b_spatial=b_spatial.shape, w_spatial=w_spatial.shape, w_pool=w_pool.shape)
    res = {}
    for k, nm in enumerate(names):
        res[nm] = tuple(o.reshape(shapes[nm]) for o in outs[1 + 4 * k:5 + 4 * k])
    res["w_ada"] = (grad_ada, d_ada, m_ada, v_ada)
    res["w_in"] = (grad_in_t.T, d_in_t.T, m_in_t.T, v_in_t.T)
    res["w_out"] = (grad_out, d_out, m_out, v_out)
    res["w_fc1"] = (grad_w1, d_w1, m_w1, v_w1)
    res["w_fc2"] = (grad_w2, d_w2, m_w2, v_w2)

    order = ["w_ada", "b_ada", "norm1_pre", "norm1_post", "w_in", "w_spatial", "b_spatial", "ln_v_gain", "ln_v_bias",
             "w_pool", "b_pool", "pool_scale", "w_out", "norm2_pre", "norm2_post", "w_fc1", "w_fc2"]
    return (loss, grad_x.reshape(x.shape),
            *[res[nm][0] for nm in order], *[res[nm][1] for nm in order],
            *[res[nm][2] for nm in order], *[res[nm][3] for nm in order])
```

```python
import functools

import jax
import jax.numpy as jnp
from jax import lax
from jax.experimental import pallas as pl
from jax.experimental.pallas import tpu as pltpu

F32 = jnp.float32
BF16 = jnp.bfloat16
MESH = pl.DeviceIdType.MESH

N_DEV = 8
D = 1024
D_A = 512
D_B = 512
D_Z = 2 * D_A + D_B
N_HEADS = 4
CHUNK = 128
WINDOWS = (2, 4, 8, 16)
GROUP = 128
D_FF = 4096
FF_BLK = D_FF // N_DEV
HALO = 16
EPS = 1e-6
VMEM_LIMIT = 60 * 1024 * 1024

ADAM_LR = 0.001
ADAM_B1 = 0.9
ADAM_B2 = 0.999
ADAM_EPS = 1e-08
ADAM_WD = 0.01
ADAM_STEP = 10

ROW_DMOD = 0
ROW_N1PRE, ROW_N1POST, ROW_N2PRE, ROW_N2POST = 8, 9, 10, 11
ROW_LN = 12
ROW_POOL = 13
ROW_LOSS = 14
ROW_BS = 16
ROW_WS = 24
ROW_WP = 88
SMALL_ROWS = 152
TABLE_ROWS = 8 * N_DEV
PACK_SHIFT = TABLE_ROWS - 8
PACK_ROWS = SMALL_ROWS + PACK_SHIFT
PACK_HALF = PACK_ROWS // 2


def _dot(a, b):
    return jnp.dot(a, b, preferred_element_type=F32)


def _dot_nt(a, b):
    return lax.dot_general(a, b, (((1,), (1,)), ((), ())), preferred_element_type=F32)


def _dot_tn(a, b):
    return lax.dot_general(a, b, (((0,), (0,)), ((), ())), preferred_element_type=F32)


def _rstd(v):
    return lax.rsqrt(jnp.mean(v * v, axis=-1, keepdims=True) + EPS)


def _rms_bwd(d_hat, hat, rstd):
    return rstd * (d_hat - hat * jnp.mean(d_hat * hat, axis=-1, keepdims=True))


_K0 = 0.7978845608028654
_K1 = 0.044715


def _gelu_parts(v):
    t = jnp.tanh(_K0 * (v + _K1 * (v * v * v)))
    return t, v * (0.5 * (1.0 + t))


def _gelu_grad(v, t):
    return 0.5 * (1.0 + t) + (0.5 * v) * (1.0 - t * t) * (_K0 * (1.0 + (3.0 * _K1) * (v * v)))


def _colsum(v):
    return jnp.sum(v, axis=0, keepdims=True)


def _full(shape):
    n = len(shape)
    return pl.BlockSpec(shape, lambda *_: (0,) * n)


def _resident(shape):
    n = len(shape)
    return pl.BlockSpec(shape, lambda *_: (0,) * n, pipeline_mode=pl.Buffered(1))


def _place():
    x, y, c = lax.axis_index("x"), lax.axis_index("y"), lax.axis_index("c")
    return x, y, c


def _flip(v, bit):
    return 1 - v if bit else v


def _peer(x, y, c, k):
    return (_flip(x, (k >> 2) & 1), _flip(y, (k >> 1) & 1), _flip(c, k & 1))


def _index(p):
    return 4 * p[0] + 2 * p[1] + p[2]


def _two_level_gather_begin(x, y, c, out_refs, send_sems, recv_sems):
    me = (x, y, c)
    sibling = (x, y, 1 - c)
    chips = [(1 - x, y), (x, 1 - y), (1 - x, 1 - y)]

    def copy(a, k, block, to):
        ref = out_refs[a].at[_index(block)]
        return pltpu.make_async_remote_copy(
            src_ref=ref, dst_ref=ref, send_sem=send_sems.at[7 * a + k], recv_sem=recv_sems.at[7 * a + k],
            device_id=to, device_id_type=MESH)

    first = []
    for a in range(len(out_refs)):
        first.append(copy(a, 0, me, sibling))
        first += [copy(a, 1 + j, me, (*chip, c)) for j, chip in enumerate(chips)]
    for cp in first:
        cp.start()
    return copy, first, me, sibling, chips


def _two_level_gather_finish(c, n, begun):
    copy, first, me, sibling, chips = begun
    passed = []
    for a in range(n):
        for j, chip in enumerate(chips):
            copy(a, 1 + j, (*chip, c), me).wait_recv()
            fwd = copy(a, 4 + j, (*chip, c), sibling)
            fwd.start()
            passed.append(fwd)
    for a in range(n):
        copy(a, 0, sibling, me).wait_recv()
        for j, chip in enumerate(chips):
            copy(a, 4 + j, (*chip, 1 - c), me).wait_recv()
    for cp in first + passed:
        cp.wait_send()


def _fwd_comm(c8, w_ada, b_my, gathered, kept):
    ncol = w_ada.shape[1]
    n_g, n_k = len(gathered), len(kept)

    def body(c_ref, w_ref, b_ref, *rest):
        g_in, k_in = rest[:n_g], rest[n_g:n_g + n_k]
        modp_ref, sc_ref = rest[n_g + n_k:n_g + n_k + 2]
        g_out = rest[n_g + n_k + 2:2 * n_g + n_k + 2]
        k_out = rest[2 * n_g + n_k + 2:2 * n_g + 2 * n_k + 2]
        cg, mg, part, send_sems, recv_sems, g_send, g_recv = rest[2 * n_g + 2 * n_k + 2:]
        x, y, c = _place()
        me = _index((x, y, c))
        for a in range(n_g):
            g_out[a][me] = g_in[a][...].astype(BF16)
        begun = _two_level_gather_begin(x, y, c, g_out, g_send, g_recv)
        for a in range(n_k):
            k_out[a][...] = k_in[a][...].astype(BF16)

        def c_copy(k):
            p = _peer(x, y, c, k)
            return pltpu.make_async_remote_copy(
                src_ref=c_ref, dst_ref=cg.at[me], send_sem=send_sems.at[k - 1], recv_sem=recv_sems.at[k - 1],
                device_id=p, device_id_type=MESH)

        def c_arrival(k):
            p = _peer(x, y, c, k)
            return pltpu.make_async_remote_copy(
                src_ref=c_ref, dst_ref=cg.at[_index(p)], send_sem=send_sems.at[k - 1], recv_sem=recv_sems.at[k - 1],
                device_id=p, device_id_type=MESH)

        def m_copy(k):
            p = _peer(x, y, c, k)
            return pltpu.make_async_remote_copy(
                src_ref=part, dst_ref=mg.at[me], send_sem=send_sems.at[6 + k], recv_sem=recv_sems.at[6 + k],
                device_id=p, device_id_type=MESH)

        def m_arrival(k):
            p = _peer(x, y, c, k)
            return pltpu.make_async_remote_copy(
                src_ref=part, dst_ref=mg.at[_index(p)], send_sem=send_sems.at[6 + k], recv_sem=recv_sems.at[6 + k],
                device_id=p, device_id_type=MESH)

        for k in range(1, N_DEV):
            c_copy(k).start()
        cg[me] = c_ref[...]
        for k in range(1, N_DEV):
            c_arrival(k).wait_recv()
        c_all = jnp.concatenate([cg[j, 0:1, :] for j in range(N_DEV)], axis=0)
        sc = c_all * jax.nn.sigmoid(c_all)
        sc_ref[...] = sc
        part[...] = _dot(sc.astype(BF16), w_ref[...].astype(BF16)) + b_ref[...]
        for k in range(1, N_DEV):
            m_copy(k).start()
        mg[me] = part[...]
        for k in range(1, N_DEV):
            m_arrival(k).wait_recv()
        for j in range(N_DEV):
            modp_ref[j:j + 1, :] = mg[j, pl.ds(me, 1), :]
        _two_level_gather_finish(c, n_g, begun)
        for k in range(1, N_DEV):
            c_copy(k).wait_send()
            m_copy(k).wait_send()

    vm = pl.BlockSpec(memory_space=pltpu.VMEM)
    outs = pl.pallas_call(
        body, name="fwd_comm",
        out_shape=tuple([jax.ShapeDtypeStruct((N_DEV, ncol), F32), jax.ShapeDtypeStruct((N_DEV, D), F32)]
                        + [jax.ShapeDtypeStruct((N_DEV,) + s.shape, BF16) for s in gathered]
                        + [jax.ShapeDtypeStruct(s.shape, BF16) for s in kept]),
        in_specs=[vm] * (3 + n_g + n_k), out_specs=tuple([vm] * (2 + n_g + n_k)),
        scratch_shapes=[
            pltpu.VMEM((N_DEV, 8, D), F32),
            pltpu.VMEM((N_DEV, N_DEV, ncol), F32),
            pltpu.VMEM((N_DEV, ncol), F32),
            pltpu.SemaphoreType.DMA((2 * (N_DEV - 1),)),
            pltpu.SemaphoreType.DMA((2 * (N_DEV - 1),)),
            pltpu.SemaphoreType.DMA((7 * n_g,)),
            pltpu.SemaphoreType.DMA((7 * n_g,)),
        ],
        compiler_params=pltpu.CompilerParams(vmem_limit_bytes=VMEM_LIMIT),
    )(c8, w_ada, b_my, *gathered, *kept)
    return outs[0], outs[1], outs[2:2 + n_g], outs[2 + n_g:]


FC_EARLY = 6
WGRAD_ORDER = (7, 6, 1, 3, 5, 2, 4, 0)


class _Copies:
    def __init__(self, entries, send_sems, recv_sems):
        self.place = _place()
        self.entries, self.send_sems, self.recv_sems = entries, send_sems, recv_sems

    def _copy(self, i, arrival=False):
        src, dst, rel = self.entries[i]
        return pltpu.make_async_remote_copy(
            src_ref=dst if arrival else src, dst_ref=dst, send_sem=self.send_sems.at[i],
            recv_sem=self.recv_sems.at[i], device_id=_peer(*self.place, rel), device_id_type=MESH)

    def start(self, *which):
        for i in which:
            self._copy(i).start()

    def wait_recv(self, *which):
        for i in which:
            self._copy(i, arrival=True).wait_recv()

    def wait_send(self, *which):
        for i in which:
            self._copy(i).wait_send()


TAIL_STEPS = 8


def _tail_comm(parts, small, row_chunk, fc):
    n, n_fc = len(parts), len(fc)

    def body(*refs):
        p_refs, small_ref = refs[:n], refs[n]
        fc_in = refs[n + 1:n + 1 + 6 * n_fc]
        outs = refs[n + 1 + 6 * n_fc:]
        g_refs, total_ref = outs[:n], outs[n]
        fc_out = outs[n + 1:n + 1 + 4 * n_fc]
        scr = outs[n + 1 + 4 * n_fc:]
        from_sib = scr[0:n]
        chip_out = scr[n:2 * n]
        chip_in = scr[2 * n:3 * n]
        pack, pack_sib, halves, total_scr = scr[3 * n:3 * n + 4]
        send_a, recv_a, send_b, recv_b, send_s, recv_s = scr[3 * n + 4:]
        step = pl.program_id(0)
        x, y, c = _place()
        me = _index((x, y, c))
        sibling = (x, y, 1 - c)
        my_chip = 2 * x + y
        others = [(1 - x, y), (x, 1 - y), (1 - x, 1 - y)]
        my_half = pl.ds(pl.multiple_of(PACK_HALF * c, 8), PACK_HALF)

        def pack_to_sibling():
            return pltpu.make_async_remote_copy(
                src_ref=pack, dst_ref=pack_sib, send_sem=send_s.at[0], recv_sem=recv_s.at[0],
                device_id=sibling, device_id_type=MESH)

        def half_to_chip(r):
            return pltpu.make_async_remote_copy(
                src_ref=halves.at[my_chip], dst_ref=halves.at[my_chip],
                send_sem=send_s.at[1 + r], recv_sem=recv_s.at[1 + r],
                device_id=(*others[r], c), device_id_type=MESH)

        def half_from_chip(r):
            k = 2 * others[r][0] + others[r][1]
            return pltpu.make_async_remote_copy(
                src_ref=halves.at[k], dst_ref=halves.at[k], send_sem=send_s.at[1 + r], recv_sem=recv_s.at[1 + r],
                device_id=(*others[r], c), device_id_type=MESH)

        def total_to_sibling():
            return pltpu.make_async_remote_copy(
                src_ref=total_scr.at[my_half], dst_ref=total_scr.at[my_half],
                send_sem=send_s.at[4], recv_sem=recv_s.at[4], device_id=sibling, device_id_type=MESH)

        def total_from_sibling():
            sib_half = pl.ds(pl.multiple_of(PACK_HALF * (1 - c), 8), PACK_HALF)
            return pltpu.make_async_remote_copy(
                src_ref=total_scr.at[sib_half], dst_ref=total_scr.at[sib_half],
                send_sem=send_s.at[4], recv_sem=recv_s.at[4], device_id=sibling, device_id_type=MESH)

        def to_sibling(a, k):
            return pltpu.make_async_remote_copy(
                src_ref=p_refs[a].at[2 * k + (1 - c)], dst_ref=from_sib[a].at[k],
                send_sem=send_a.at[a], recv_sem=recv_a.at[a], device_id=sibling, device_id_type=MESH)

        def all_from_sibling(a):
            return pltpu.make_async_remote_copy(
                src_ref=from_sib[a], dst_ref=from_sib[a], send_sem=send_a.at[a], recv_sem=recv_a.at[a],
                device_id=sibling, device_id_type=MESH)

        def to_chip(a, r):
            return pltpu.make_async_remote_copy(
                src_ref=chip_out[a].at[r], dst_ref=chip_in[a].at[r],
                send_sem=send_b.at[3 * a + r], recv_sem=recv_b.at[3 * a + r],
                device_id=(*others[r], c), device_id_type=MESH)

        @pl.when(step == 0)
        def _():
            pack[0:TABLE_ROWS, :] = jnp.zeros((TABLE_ROWS, D), F32)
            pack[pl.ds(pl.multiple_of(8 * me, 8), 8), :] = small_ref[0:8, :]
            pack[TABLE_ROWS:PACK_ROWS, :] = small_ref[8:SMALL_ROWS, :]
            pack_to_sibling().start()
            for a in range(n):
                for k in range(4):
                    to_sibling(a, k).start()

        @pl.when(step == 1)
        def _():
            pack_to_sibling().wait_recv()
            halves[my_chip] = pack[my_half, :] + pack_sib[my_half, :]
            for r in range(3):
                half_to_chip(r).start()
            for a in range(n):
                all_from_sibling(a).wait_recv()
                rows = p_refs[a].shape[1]
                for r in range(3):
                    k = 2 * others[r][0] + others[r][1]
                    for s in range(0, rows, row_chunk):
                        sl = pl.ds(s, row_chunk)
                        chip_out[a][r, sl, :] = (p_refs[a][2 * k + c, sl, :].astype(F32)
                                                 + from_sib[a][k, sl, :].astype(F32)).astype(BF16)
                    to_chip(a, r).start()
                for s in range(0, rows, row_chunk):
                    sl = pl.ds(s, row_chunk)
                    g_refs[a][sl, :] = (p_refs[a][2 * my_chip + c, sl, :].astype(F32)
                                        + from_sib[a][my_chip, sl, :].astype(F32))

        for k in range(n_fc):
            w_ref, own_ref, arr_ref, diag_ref, m_ref, v_ref = fc_in[6 * k:6 * k + 6]
            g = own_ref[...]
            for r in range(2):
                g = g + arr_ref[r].astype(F32)
            g = g + diag_ref[...].astype(F32)
            fc_out[4 * k][...] = g
            fc_out[4 * k + 1][...], fc_out[4 * k + 2][...], fc_out[4 * k + 3][...] = _adam(
                w_ref[...], g, m_ref[...], v_ref[...])

        @pl.when(step == TAIL_STEPS - 1)
        def _():
            for r in range(3):
                half_from_chip(r).wait_recv()
            total_scr[my_half, :] = ((halves[0] + halves[1]) + halves[2]) + halves[3]
            total_to_sibling().start()
            for a in range(n):
                rows = p_refs[a].shape[1]
                for r in range(3):
                    to_chip(a, r).wait_recv()
                    for s in range(0, rows, row_chunk):
                        sl = pl.ds(s, row_chunk)
                        g_refs[a][sl, :] = g_refs[a][sl, :] + chip_in[a][r, sl, :].astype(F32)
            total_from_sibling().wait_recv()
            total_ref[...] = total_scr[...]
            for a in range(n):
                all_from_sibling(a).wait_send()
                for r in range(3):
                    to_chip(a, r).wait_send()
            pack_to_sibling().wait_send()
            for r in range(3):
                half_to_chip(r).wait_send()
            total_to_sibling().wait_send()

    fc_specs_in, fc_specs_out, fc_shapes, fc_args = [], [], [], []
    for w, own, arrived, diagonal, m, v in fc:
        rows, cols = w.shape
        blk = pl.BlockSpec((rows // TAIL_STEPS, cols), lambda i: (i, 0))
        fc_specs_in += [blk, blk, pl.BlockSpec((2, rows // TAIL_STEPS, cols), lambda i: (0, i, 0)), blk, blk, blk]
        fc_specs_out += [blk] * 4
        fc_shapes += [jax.ShapeDtypeStruct((rows, cols), F32)] * 4
        fc_args += [w, own, arrived, diagonal, m, v]
    outs = pl.pallas_call(
        body, name="tail_comm", grid=(TAIL_STEPS,),
        out_shape=tuple([jax.ShapeDtypeStruct(p.shape[1:], F32) for p in parts]
                        + [jax.ShapeDtypeStruct((PACK_ROWS, D), F32)] + fc_shapes),
        in_specs=[_resident(p.shape) for p in parts] + [_resident(small.shape)] + fc_specs_in,
        out_specs=tuple([_full(p.shape[1:]) for p in parts] + [_full((PACK_ROWS, D))] + fc_specs_out),
        scratch_shapes=(
            [pltpu.VMEM((4,) + p.shape[1:], BF16) for p in parts]
            + [pltpu.VMEM((3,) + p.shape[1:], BF16) for p in parts]
            + [pltpu.VMEM((3,) + p.shape[1:], BF16) for p in parts]
            + [pltpu.VMEM((PACK_ROWS, D), F32), pltpu.VMEM((PACK_ROWS, D), F32),
               pltpu.VMEM((4, PACK_HALF, D), F32), pltpu.VMEM((PACK_ROWS, D), F32)]
            + [pltpu.SemaphoreType.DMA((n,)), pltpu.SemaphoreType.DMA((n,)),
               pltpu.SemaphoreType.DMA((3 * n,)), pltpu.SemaphoreType.DMA((3 * n,)),
               pltpu.SemaphoreType.DMA((5,)), pltpu.SemaphoreType.DMA((5,))]),
        compiler_params=pltpu.CompilerParams(dimension_semantics=("arbitrary",), vmem_limit_bytes=VMEM_LIMIT),
    )(*parts, small, *fc_args)
    return outs[:n + 1], [outs[n + 1 + 4 * k:n + 5 + 4 * k] for k in range(n_fc)]


def _tril_mask():
    row = lax.broadcasted_iota(jnp.int32, (CHUNK, CHUNK), 0)
    col = lax.broadcasted_iota(jnp.int32, (CHUNK, CHUNK), 1)
    return (col <= row).astype(F32)


def _window_sums(ext):
    s2 = ext + pltpu.roll(ext, 1, 0)
    t4 = s2[:, GROUP:]
    s4 = t4 + pltpu.roll(t4, 2, 0)
    t8 = s4[:, GROUP:]
    s8 = t8 + pltpu.roll(t8, 4, 0)
    t16 = s8[:, GROUP:]
    s16 = t16 + pltpu.roll(t16, 8, 0)
    return [s2[:, :GROUP], s4[:, :GROUP], s8[:, :GROUP], s16]


def _inv_counts(first_pos, rows):
    pos = first_pos + lax.broadcasted_iota(jnp.int32, (rows, 1), 0)
    return [1.0 / jnp.minimum(pos + 1, w).astype(F32) for w in WINDOWS]


def _pool_diff(zb, halo, first_pos):
    tt = zb.shape[0]
    sums = _window_sums(jnp.concatenate([halo, zb], axis=0))
    inv = _inv_counts(first_pos, tt)
    return [sums[g][HALO:, :] * inv[g] - zb[:, g * GROUP:(g + 1) * GROUP] for g in range(len(WINDOWS))]


def _attn_fwd(tt, x, mod, n1pre, n1post, w_in_t, w_out, w_sp, bs_rows, ln_g, ln_b, w_pool, b_pool, pool_scale,
              fc_shards):
    t_len = x.shape[0]
    nt = t_len // tt

    def body(x_ref, mod_ref, n1pre_ref, n1post_ref, win_ref, wout_ref, wsp_ref, bs_ref, lng_ref, lnb_ref,
             wp_ref, bp_ref, ps_ref, w1_ref, w2_ref, z_ref, cat_ref, mix_ref, x1_ref, e1_ref, e2_ref,
             carry, land1, land2, send_sems, recv_sems, local_sems):
        i = pl.program_id(0)
        copies = _Copies(
            [(w1_ref, e1_ref.at[1], 1), (w2_ref, e2_ref.at[1], 1),
             (w1_ref, land1.at[0], 2), (w2_ref, land2.at[0], 2),
             (w1_ref, land1.at[1], 4), (w2_ref, land2.at[1], 4),
             (land1.at[0], e1_ref.at[3], 1), (land2.at[0], e2_ref.at[3], 1),
             (land1.at[1], e1_ref.at[5], 1), (land2.at[1], e2_ref.at[5], 1)],
            send_sems, recv_sems)
        keep = [pltpu.make_async_copy(w1_ref, e1_ref.at[0], local_sems.at[0]),
                pltpu.make_async_copy(w2_ref, e2_ref.at[0], local_sems.at[1]),
                pltpu.make_async_copy(land1.at[0], e1_ref.at[2], local_sems.at[2]),
                pltpu.make_async_copy(land1.at[1], e1_ref.at[4], local_sems.at[3]),
                pltpu.make_async_copy(land2.at[0], e2_ref.at[2], local_sems.at[4]),
                pltpu.make_async_copy(land2.at[1], e2_ref.at[4], local_sems.at[5])]

        @pl.when(i == 0)
        def _():
            copies.start(2, 4, 3, 5, 0, 1)
            keep[0].start()
            keep[1].start()
            carry[...] = jnp.zeros_like(carry)

        @pl.when(i == nt // 2)
        def _():
            copies.wait_recv(2, 4)
            copies.start(6, 8)
            keep[2].start()
            keep[3].start()

        @pl.when(i == nt - 1)
        def _():
            copies.wait_recv(3, 5)
            copies.start(7, 9)
            keep[4].start()
            keep[5].start()

        xv = x_ref[...]
        shift1, scale1, gate1 = mod_ref[0:1, :], mod_ref[1:2, :], mod_ref[2:3, :]
        h1 = (xv * _rstd(xv) * n1pre_ref[...]) * (1.0 + scale1) + shift1
        z = _dot_nt(h1.astype(BF16), win_ref[...])
        z_ref[...] = z

        _, ga = _gelu_parts(z[:, :2 * D_A])
        u, vr = ga[:, :D_A], ga[:, D_A:]
        dv = vr - jnp.mean(vr, axis=-1, keepdims=True)
        v = (dv * lax.rsqrt(jnp.mean(dv * dv, axis=-1, keepdims=True) + EPS)) * lng_ref[...] + lnb_ref[...]
        vb = v.astype(BF16)
        mask = _tril_mask()
        wc = [(wsp_ref[h] * mask).astype(BF16) for h in range(N_HEADS)]
        for ch in range(tt // CHUNK):
            rows = slice(ch * CHUNK, (ch + 1) * CHUNK)
            for h in range(N_HEADS):
                cols = slice(h * GROUP, (h + 1) * GROUP)
                mixed = _dot(wc[h], vb[rows, cols]) + bs_ref[:, cols]
                cat_ref[rows, cols] = (u[rows, cols] * mixed).astype(BF16)

        zb = z[:, 2 * D_A:]
        diff = _pool_diff(zb, carry[...], i * tt)
        carry[...] = zb[tt - HALO:, :]
        for g in range(len(WINDOWS)):
            cols = slice(g * GROUP, (g + 1) * GROUP)
            pre = _dot(diff[g].astype(BF16), wp_ref[g].astype(BF16)) + bp_ref[:, cols]
            cat_ref[:, D_A + g * GROUP:D_A + (g + 1) * GROUP] = (pre * ps_ref[:, cols]).astype(BF16)

        mix = _dot(cat_ref[...], wout_ref[...])
        mix_ref[...] = mix
        x1_ref[...] = xv + gate1 * (mix * _rstd(mix) * n1post_ref[...])

        @pl.when(i == nt - 1)
        def _():
            copies.wait_recv(0, 1, 6, 7, 8, 9)
            copies.wait_send(*range(10))
            for cp in keep:
                cp.wait()

    tile = lambda w: pl.BlockSpec((tt, w), lambda i: (i, 0))
    hbm = pl.BlockSpec(memory_space=pl.ANY)
    outs = pl.pallas_call(
        body, name="attn_fwd", grid=(nt,),
        out_shape=tuple([jax.ShapeDtypeStruct((t_len, D_Z), F32), jax.ShapeDtypeStruct((t_len, D), BF16),
                         jax.ShapeDtypeStruct((t_len, D), F32), jax.ShapeDtypeStruct((t_len, D), F32)]
                        + [jax.ShapeDtypeStruct((FC_EARLY,) + s.shape, BF16) for s in fc_shards]),
        in_specs=[tile(D), _full((8, D)), _full((1, D)), _full((1, D)), _resident((D_Z, D)), _resident((D, D)),
                  _full((N_HEADS, CHUNK, CHUNK)), _full((CHUNK, D_A)), _full((1, D_A)), _full((1, D_A)),
                  _full((len(WINDOWS), GROUP, GROUP)), _full((1, D_B)), _full((1, D_B)),
                  _resident(fc_shards[0].shape), _resident(fc_shards[1].shape)],
        out_specs=(tile(D_Z), tile(D), tile(D), tile(D), hbm, hbm),
        scratch_shapes=[pltpu.VMEM((HALO, D_B), F32),
                        pltpu.VMEM((2,) + fc_shards[0].shape, BF16), pltpu.VMEM((2,) + fc_shards[1].shape, BF16),
                        pltpu.SemaphoreType.DMA((10,)), pltpu.SemaphoreType.DMA((10,)),
                        pltpu.SemaphoreType.DMA((6,))],
        compiler_params=pltpu.CompilerParams(dimension_semantics=("arbitrary",), vmem_limit_bytes=VMEM_LIMIT),
    )(x, mod, n1pre, n1post, w_in_t, w_out, w_sp, bs_rows, ln_g, ln_b, w_pool, b_pool, pool_scale, *fc_shards)
    return outs[:4], outs[4:]


def _mlp_fwd_early(tt, x1, mod, n2pre, w1_early, w2_early):
    t_len = x1.shape[0]
    nt = t_len // tt
    n_late = N_DEV - FC_EARLY

    def body(x1_ref, mod_ref, n2pre_ref, w1_ref, w2_ref, r_ref, h2_ref, f_ref, l1_ref, l2_ref,
             land1, land2, send_sems, recv_sems, local_sems):
        i = pl.program_id(0)
        copies = _Copies(
            [(w1_ref.at[2], land1, 4), (w2_ref.at[4], land2, 2),
             (land1, l1_ref.at[1], 1), (land2, l2_ref.at[1], 1)],
            send_sems, recv_sems)
        keep = [pltpu.make_async_copy(land1, l1_ref.at[0], local_sems.at[0]),
                pltpu.make_async_copy(land2, l2_ref.at[0], local_sems.at[1])]

        @pl.when(i == 0)
        def _():
            copies.start(0, 1)

        @pl.when(i == nt // 2)
        def _():
            copies.wait_recv(0, 1)
            copies.start(2, 3)
            for cp in keep:
                cp.start()

        x1v = x1_ref[...]
        shift2, scale2 = mod_ref[3:4, :], mod_ref[4:5, :]
        h2 = ((x1v * _rstd(x1v) * n2pre_ref[...]) * (1.0 + scale2) + shift2).astype(BF16)
        h2_ref[...] = h2
        for j in range(FC_EARLY):
            cols = slice(j * FF_BLK, (j + 1) * FF_BLK)
            ra = jnp.maximum(_dot(h2, w1_ref[j]), 0.0)
            r = (ra * ra).astype(BF16)
            r_ref[:, cols] = r
            contrib = _dot(r, w2_ref[j])
            if j == 0:
                f_ref[...] = contrib
            else:
                f_ref[...] += contrib

        @pl.when(i == nt - 1)
        def _():
            copies.wait_recv(2, 3)
            copies.wait_send(0, 1, 2, 3)
            for cp in keep:
                cp.wait()

    tile = lambda w: pl.BlockSpec((tt, w), lambda i: (i, 0))
    hbm = pl.BlockSpec(memory_space=pl.ANY)
    outs = pl.pallas_call(
        body, name="mlp_fwd_early", grid=(nt,),
        out_shape=(jax.ShapeDtypeStruct((t_len, D_FF), BF16),
                   jax.ShapeDtypeStruct((t_len, D), BF16), jax.ShapeDtypeStruct((t_len, D), F32),
                   jax.ShapeDtypeStruct((n_late,) + w1_early.shape[1:], BF16),
                   jax.ShapeDtypeStruct((n_late,) + w2_early.shape[1:], BF16)),
        in_specs=[tile(D), _full((8, D)), _full((1, D)),
                  _resident((FC_EARLY, D, FF_BLK)), _resident((FC_EARLY, FF_BLK, D))],
        out_specs=(tile(FC_EARLY * FF_BLK), tile(D), tile(D), hbm, hbm),
        scratch_shapes=[pltpu.VMEM(w1_early.shape[1:], BF16), pltpu.VMEM(w2_early.shape[1:], BF16),
                        pltpu.SemaphoreType.DMA((4,)), pltpu.SemaphoreType.DMA((4,)),
                        pltpu.SemaphoreType.DMA((2,))],
        compiler_params=pltpu.CompilerParams(dimension_semantics=("arbitrary",), vmem_limit_bytes=VMEM_LIMIT),
    )(x1, mod, n2pre, w1_early, w2_early)
    return outs[:3], outs[3:]


def _mlp_late_bwd(tt, r_all, x1, h2, f_early, tgt, mix, mod, n2pre, n2post, n1post,
                  w1_early, w2_early, w1_late, w2_late):
    t_len = x1.shape[0]
    nt = t_len // tt
    n_late = N_DEV - FC_EARLY
    late_cols = n_late * FF_BLK

    def body(r_all_ref, re_ref, x1_ref, h2_ref, fe_ref, tgt_ref, mix_ref, mod_ref, n2pre_ref, n2post_ref,
             n1post_ref, w1e_ref, w2e_ref, w1l_ref, w2l_ref,
             rl_ref, df_ref, da_ref, dmix_ref, dx1_ref, redf_ref, redb_ref, dh2_acc):
        i = pl.program_id(0)

        @pl.when(i == 0)
        def _():
            redf_ref[...] = jnp.zeros_like(redf_ref)
            redb_ref[...] = jnp.zeros_like(redb_ref)

        x1v = x1_ref[...]
        gate1, scale2, gate2 = mod_ref[2:3, :], mod_ref[4:5, :], mod_ref[5:6, :]
        h2 = h2_ref[...]
        f = fe_ref[...]
        for j in range(n_late):
            cols = slice(j * FF_BLK, (j + 1) * FF_BLK)
            ra = jnp.maximum(_dot(h2, w1l_ref[j]), 0.0)
            r = (ra * ra).astype(BF16)
            rl_ref[:, cols] = r
            f = f + _dot(r, w2l_ref[j])
        rf = _rstd(f)
        fhat = f * rf
        nf = fhat * n2post_ref[...]
        err = (x1v + gate2 * nf) - tgt_ref[...]
        dy = err * (1.0 / D)
        dnf = dy * gate2
        dfv = _rms_bwd(dnf * n2post_ref[...], fhat, rf).astype(BF16)
        df_ref[...] = dfv
        redf_ref[0:1, :] += _colsum(dy * nf)
        redf_ref[1:2, :] += _colsum(dnf * fhat)
        redf_ref[2:3, :] += _colsum(0.5 * jnp.mean(err * err, axis=-1, keepdims=True)) * jnp.ones((1, D), F32)

        for j in range(N_DEV):
            cols = slice(j * FF_BLK, (j + 1) * FF_BLK)
            if j < FC_EARLY:
                w1, w2, r = w1e_ref[j], w2e_ref[j], re_ref[:, cols]
            else:
                jl = j - FC_EARLY
                w1, w2, r = w1l_ref[jl], w2l_ref[jl], rl_ref[:, jl * FF_BLK:(jl + 1) * FF_BLK]
            dr = _dot_nt(dfv, w2)
            da = (dr * (2.0 * jnp.sqrt(r.astype(F32)))).astype(BF16)
            da_ref[:, cols] = da
            contrib = _dot_nt(da, w1)
            if j == 0:
                dh2_acc[...] = contrib
            else:
                dh2_acc[...] += contrib
        dh2 = dh2_acc[...]
        r2 = _rstd(x1v)
        xhat = x1v * r2
        n2 = xhat * n2pre_ref[...]
        dn2 = dh2 * (1.0 + scale2)
        dx1 = dy + _rms_bwd(dn2 * n2pre_ref[...], xhat, r2)
        dx1_ref[...] = dx1
        mixv = mix_ref[...]
        rm = _rstd(mixv)
        mhat = mixv * rm
        dnm = dx1 * gate1
        dmix_ref[...] = _rms_bwd(dnm * n1post_ref[...], mhat, rm).astype(BF16)
        redb_ref[0:1, :] += _colsum(dh2)
        redb_ref[1:2, :] += _colsum(dh2 * n2)
        redb_ref[2:3, :] += _colsum(dn2 * xhat)
        redb_ref[3:4, :] += _colsum(dx1 * (mhat * n1post_ref[...]))
        redb_ref[4:5, :] += _colsum(dnm * mhat)

    tile = lambda w: pl.BlockSpec((tt, w), lambda i: (i, 0))
    return pl.pallas_call(
        body, name="mlp_late_bwd", grid=(nt,),
        out_shape=(jax.ShapeDtypeStruct((t_len, D_FF), BF16), jax.ShapeDtypeStruct((t_len, D), BF16),
                   jax.ShapeDtypeStruct((t_len, D_FF), BF16), jax.ShapeDtypeStruct((t_len, D), BF16),
                   jax.ShapeDtypeStruct((t_len, D), F32), jax.ShapeDtypeStruct((8, D), F32),
                   jax.ShapeDtypeStruct((8, D), F32)),
        in_specs=[pl.BlockSpec(memory_space=pl.ANY), tile(FC_EARLY * FF_BLK), tile(D), tile(D), tile(D), tile(D),
                  tile(D), _full((8, D)), _full((1, D)), _full((1, D)), _full((1, D)),
                  _resident((FC_EARLY, D, FF_BLK)), _resident((FC_EARLY, FF_BLK, D)),
                  _resident((n_late, D, FF_BLK)), _resident((n_late, FF_BLK, D))],
        out_specs=(pl.BlockSpec((tt, late_cols), lambda i: (i, FC_EARLY // n_late)), tile(D), tile(D_FF), tile(D),
                   tile(D), _full((8, D)), _full((8, D))),
        input_output_aliases={0: 0},
        scratch_shapes=[pltpu.VMEM((tt, D), F32)],
        compiler_params=pltpu.CompilerParams(dimension_semantics=("arbitrary",), vmem_limit_bytes=VMEM_LIMIT),
    )(r_all, r_all, x1, h2, f_early, tgt, mix, mod, n2pre, n2post, n1post, w1_early, w2_early, w1_late, w2_late)


def _mlp_wgrad(tt, r, da, df, h2):
    t_len = df.shape[0]
    nt = t_len // tt
    odd_steps = [j for j, rel in enumerate(WGRAD_ORDER) if rel % 2]

    def relation(j):
        rel = jnp.int32(WGRAD_ORDER[-1])
        for step in range(N_DEV - 2, -1, -1):
            rel = jnp.where(j == step, WGRAD_ORDER[step], rel)
        return rel

    def body(r_ref, da_ref, df_ref, h2_ref, own1_ref, own2_ref, out1_ref, out2_ref, diag1_ref, diag2_ref,
             acc1, acc2, snd1, snd2, sib1, sib2, dsnd1, dsnd2, send_sems, recv_sems):
        j, t = pl.program_id(0), pl.program_id(1)
        rows = pl.ds(pl.multiple_of(t * tt, tt), tt)
        x, y, c = _place()
        accs, snds, sibs = (acc1, acc2), (snd1, snd2), (sib1, sib2)
        dsnds, diags = (dsnd1, dsnd2), (diag1_ref, diag2_ref)

        def to_sibling(a, jj, buf=0):
            return pltpu.make_async_remote_copy(
                src_ref=snds[a].at[buf], dst_ref=sibs[a].at[jj],
                send_sem=send_sems.at[4 * a + jj], recv_sem=recv_sems.at[4 * a + jj],
                device_id=(x, y, 1 - c), device_id_type=MESH)

        def to_diagonal(a):
            return pltpu.make_async_remote_copy(
                src_ref=dsnds[a], dst_ref=diags[a], send_sem=send_sems.at[8 + a], recv_sem=recv_sems.at[8 + a],
                device_id=_peer(x, y, c, 6), device_id_type=MESH)

        @pl.when(t == 0)
        def _():
            acc2[...] = jnp.zeros_like(acc2)
            acc1[...] = jnp.zeros_like(acc1)

        acc2[...] += _dot_tn(r_ref[...], df_ref[rows, :])
        acc1[...] += _dot_tn(h2_ref[rows, :], da_ref[...])

        for step, rel in enumerate(WGRAD_ORDER):
            jj = rel // 2

            @pl.when((t == nt - 1) & (j == step))
            def _():
                for a, (own_ref, out_ref) in enumerate(((own1_ref, out1_ref), (own2_ref, out2_ref))):
                    if rel % 2:
                        q = odd_steps.index(step)
                        if q >= 2:
                            to_sibling(a, WGRAD_ORDER[odd_steps[q - 2]] // 2).wait_send()
                        snds[a][q % 2] = accs[a][...].astype(BF16)
                        to_sibling(a, jj, q % 2).start()
                        continue
                    to_sibling(a, jj).wait_recv()
                    chip_sum = accs[a][...] + sibs[a][jj].astype(F32)
                    if rel == 6:
                        dsnds[a][...] = chip_sum.astype(BF16)
                        to_diagonal(a).start()
                    elif rel == 0:
                        own_ref[...] = chip_sum
                    else:
                        out_ref[0] = chip_sum.astype(BF16)
                    if step == N_DEV - 1:
                        for q in (2, 3):
                            to_sibling(a, WGRAD_ORDER[odd_steps[q]] // 2).wait_send()
                        to_diagonal(a).wait_recv()
                        to_diagonal(a).wait_send()

    assert WGRAD_ORDER[-1] == 0 and WGRAD_ORDER[-3:-1] == (2, 4)
    blk = pl.BlockSpec((tt, FF_BLK), lambda j, t: (t, relation(j)))
    chip = lambda j, t: (jnp.clip(j - 5, 0, 1), 0, 0)
    hbm = pl.BlockSpec(memory_space=pl.ANY)
    return pl.pallas_call(
        body, name="mlp_wgrad", grid=(N_DEV, nt),
        out_shape=(jax.ShapeDtypeStruct((D, FF_BLK), F32), jax.ShapeDtypeStruct((FF_BLK, D), F32),
                   jax.ShapeDtypeStruct((2, D, FF_BLK), BF16), jax.ShapeDtypeStruct((2, FF_BLK, D), BF16),
                   jax.ShapeDtypeStruct((D, FF_BLK), BF16), jax.ShapeDtypeStruct((FF_BLK, D), BF16)),
        in_specs=[blk, blk, _resident((t_len, D)), _resident((t_len, D))],
        out_specs=(_full((D, FF_BLK)), _full((FF_BLK, D)),
                   pl.BlockSpec((1, D, FF_BLK), chip), pl.BlockSpec((1, FF_BLK, D), chip), hbm, hbm),
        scratch_shapes=[pltpu.VMEM((D, FF_BLK), F32), pltpu.VMEM((FF_BLK, D), F32),
                        pltpu.VMEM((2, D, FF_BLK), BF16), pltpu.VMEM((2, FF_BLK, D), BF16),
                        pltpu.VMEM((4, D, FF_BLK), BF16), pltpu.VMEM((4, FF_BLK, D), BF16),
                        pltpu.VMEM((D, FF_BLK), BF16), pltpu.VMEM((FF_BLK, D), BF16),
                        pltpu.SemaphoreType.DMA((10,)), pltpu.SemaphoreType.DMA((10,))],
        compiler_params=pltpu.CompilerParams(dimension_semantics=("arbitrary", "arbitrary"),
                                             vmem_limit_bytes=VMEM_LIMIT),
    )(r, da, df, h2)


def _acc_rows(ref, row0, k, val):
    half = CHUNK // 2
    ref[row0:row0 + half, k * GROUP:(k + 1) * GROUP] += val[:half, :]
    ref[row0:row0 + half, D_A + k * GROUP:D_A + (k + 1) * GROUP] += val[half:, :]


def _attn_bwd(tt, dmix, dx1, x, z, cat, mod, n1pre, w_in_t, w_out, w_sp, bs_rows, ln_g, ln_b, w_pool, b_pool,
              pool_scale, red_fwd, red_bwd, chip_sums):
    t_len = x.shape[0]
    nt = t_len // tt
    hb = tt // HALO
    n_sums = len(chip_sums)

    def body(dmix_ref, dx1_ref, x_ref, z_ref, zprev_ref, cat_ref, mod_ref, n1pre_ref, win_ref, wout_ref, wsp_ref,
             bs_ref, lng_ref, lnb_ref, wp_ref, bp_ref, ps_ref, redf_ref, redb_ref, *rest):
        sum_out = rest[:n_sums]
        gx_ref, gwin_ref, gwout_ref, small_ref = rest[n_sums:n_sums + 4]
        sum_in = rest[n_sums + 4:2 * n_sums + 4]
        carry, acc_in, acc_out, dz_scr, bs_acc, send_sems, recv_sems = rest[2 * n_sums + 4:]
        s = pl.program_id(0)
        i = nt - 1 - s
        px, py, pc = _place()

        def chip_copy(a, r):
            return pltpu.make_async_remote_copy(
                src_ref=sum_out[a].at[r], dst_ref=sum_in[a].at[r],
                send_sem=send_sems.at[2 * a + r], recv_sem=recv_sems.at[2 * a + r],
                device_id=_peer(px, py, pc, 2 * (r + 1)), device_id_type=MESH)

        @pl.when(s == 0)
        def _():
            for a in range(n_sums):
                for r in range(2):
                    chip_copy(a, r).start()
            carry[...] = jnp.zeros_like(carry)
            acc_in[...] = jnp.zeros_like(acc_in)
            acc_out[...] = jnp.zeros_like(acc_out)
            bs_acc[...] = jnp.zeros_like(bs_acc)
            small_ref[...] = jnp.zeros_like(small_ref)
            small_ref[ROW_DMOD + 2:ROW_DMOD + 3, :] = redb_ref[3:4, :]
            small_ref[ROW_DMOD + 3:ROW_DMOD + 5, :] = redb_ref[0:2, :]
            small_ref[ROW_DMOD + 5:ROW_DMOD + 6, :] = redf_ref[0:1, :]
            small_ref[ROW_N1POST:ROW_N1POST + 1, :] = redb_ref[4:5, :]
            small_ref[ROW_N2PRE:ROW_N2PRE + 1, :] = redb_ref[2:3, :]
            small_ref[ROW_N2POST:ROW_N2POST + 1, :] = redf_ref[1:2, :]
            small_ref[ROW_LOSS:ROW_LOSS + 1, :] = redf_ref[2:3, :]

        dmixv = dmix_ref[...]
        dcat = _dot_nt(dmixv, wout_ref[...])
        acc_out[...] += _dot_tn(cat_ref[...], dmixv)

        z = z_ref[...]
        t_g, ga = _gelu_parts(z[:, :2 * D_A])
        u, vr = ga[:, :D_A], ga[:, D_A:]
        dv0 = vr - jnp.mean(vr, axis=-1, keepdims=True)
        rv = lax.rsqrt(jnp.mean(dv0 * dv0, axis=-1, keepdims=True) + EPS)
        vhat = dv0 * rv
        vb = (vhat * lng_ref[...] + lnb_ref[...]).astype(BF16)
        mask = _tril_mask()
        wc = [(wsp_ref[h] * mask).astype(BF16) for h in range(N_HEADS)]

        dya = dcat[:, :D_A]
        for h in range(N_HEADS):
            cols = slice(h * GROUP, (h + 1) * GROUP)
            bs_sum = jnp.zeros((CHUNK, GROUP), F32)
            ws_sum = jnp.zeros((CHUNK, CHUNK), F32)
            for ch in range(tt // CHUNK):
                rows = slice(ch * CHUNK, (ch + 1) * CHUNK)
                v_ch = vb[rows, cols]
                mixed = _dot(wc[h], v_ch) + bs_ref[:, cols]
                dy_ch = dya[rows, cols]
                dz_scr[rows, cols] = dy_ch * mixed
                dmixed = dy_ch * u[rows, cols]
                dmb = dmixed.astype(BF16)
                dz_scr[rows, D_A + h * GROUP:D_A + (h + 1) * GROUP] = _dot_tn(wc[h], dmb)
                bs_sum = bs_sum + dmixed
                ws_sum = ws_sum + _dot_nt(dmb, v_ch)
            _acc_rows(bs_acc, 0, h, bs_sum)
            _acc_rows(small_ref, ROW_WS, h, ws_sum)

        dvl = dz_scr[:, D_A:2 * D_A]
        dvhat = dvl * lng_ref[...]
        dvr = rv * (dvhat - jnp.mean(dvhat, axis=-1, keepdims=True)
                    - vhat * jnp.mean(dvhat * vhat, axis=-1, keepdims=True))
        small_ref[ROW_LN:ROW_LN + 1, 0:D_A] += _colsum(dvl * vhat)
        small_ref[ROW_LN:ROW_LN + 1, D_A:D] += _colsum(dvl)
        dga = jnp.concatenate([dz_scr[:, :D_A], dvr], axis=1)
        dza = dga * _gelu_grad(z[:, :2 * D_A], t_g)

        zb = z[:, 2 * D_A:]
        halo_prev = jnp.where(i == 0, 0.0, zprev_ref[...])
        diff = _pool_diff(zb, halo_prev, i * tt)
        dyb = dcat[:, D_A:]
        inv = _inv_counts(i * tt, tt)
        scaled, ddiffs = [], []
        for g in range(len(WINDOWS)):
            cols = slice(g * GROUP, (g + 1) * GROUP)
            db = diff[g].astype(BF16)
            wpg = wp_ref[g].astype(BF16)
            pre = _dot(db, wpg) + bp_ref[:, cols]
            small_ref[ROW_POOL:ROW_POOL + 1, cols] += _colsum(dyb[:, cols] * pre)
            dpre = dyb[:, cols] * ps_ref[:, cols]
            small_ref[ROW_POOL:ROW_POOL + 1, D_B + g * GROUP:D_B + (g + 1) * GROUP] += _colsum(dpre)
            dpb = dpre.astype(BF16)
            _acc_rows(small_ref, ROW_WP, g, _dot_tn(db, dpb))
            ddiff = _dot_nt(dpb, wpg)
            ddiffs.append(ddiff)
            scaled.append(ddiff * inv[g])
        scaled_all = jnp.concatenate(scaled, axis=1)
        ext = jnp.concatenate([scaled_all, carry[...]], axis=0)
        n_ext = tt + HALO
        s2 = ext + pltpu.roll(ext, n_ext - 1, 0)
        t4 = s2[:, GROUP:]
        s4 = t4 + pltpu.roll(t4, n_ext - 2, 0)
        t8 = s4[:, GROUP:]
        s8 = t8 + pltpu.roll(t8, n_ext - 4, 0)
        t16 = s8[:, GROUP:]
        s16 = t16 + pltpu.roll(t16, n_ext - 8, 0)
        back = [s2[:, :GROUP], s4[:, :GROUP], s8[:, :GROUP], s16]
        carry[...] = scaled_all[:HALO, :]
        dzb = jnp.concatenate([back[g][:tt, :] - ddiffs[g] for g in range(len(WINDOWS))], axis=1)

        dzv = jnp.concatenate([dza, dzb], axis=1).astype(BF16)
        dh1 = _dot(dzv, win_ref[...])
        xv = x_ref[...]
        r1 = _rstd(xv)
        xhat = xv * r1
        shift1, scale1 = mod_ref[0:1, :], mod_ref[1:2, :]
        n1 = xhat * n1pre_ref[...]
        h1 = (n1 * (1.0 + scale1) + shift1).astype(BF16)
        acc_in[...] += _dot_tn(dzv, h1)
        dn1 = dh1 * (1.0 + scale1)
        gx_ref[...] = dx1_ref[...] + _rms_bwd(dn1 * n1pre_ref[...], xhat, r1)
        small_ref[ROW_DMOD:ROW_DMOD + 1, :] += _colsum(dh1)
        small_ref[ROW_DMOD + 1:ROW_DMOD + 2, :] += _colsum(dh1 * n1)
        small_ref[ROW_N1PRE:ROW_N1PRE + 1, :] += _colsum(dn1 * xhat)

        @pl.when(s == nt - 1)
        def _():
            gwin_ref[...] = acc_in[...].astype(BF16)
            gwout_ref[...] = acc_out[...].astype(BF16)
            bs = _unfold(bs_acc[...])
            for h in range(N_HEADS):
                small_ref[ROW_BS + h:ROW_BS + h + 1, 0:GROUP] = jnp.sum(
                    bs[:, h * GROUP:(h + 1) * GROUP].T, axis=0, keepdims=True)
            for a in range(n_sums):
                for r in range(2):
                    chip_copy(a, r).wait_recv()
                    chip_copy(a, r).wait_send()

    rev = lambda w: pl.BlockSpec((tt, w), lambda s: (nt - 1 - s, 0))
    zprev = pl.BlockSpec((HALO, D_B), lambda s: (jnp.maximum((nt - 1 - s) * hb - 1, 0), 2))
    hbm = pl.BlockSpec(memory_space=pl.ANY)
    outs = pl.pallas_call(
        body, name="attn_bwd", grid=(nt,),
        out_shape=tuple([jax.ShapeDtypeStruct((t_len, D), F32), jax.ShapeDtypeStruct((D_Z, D), BF16),
                         jax.ShapeDtypeStruct((D, D), BF16), jax.ShapeDtypeStruct((SMALL_ROWS, D), F32)]
                        + [jax.ShapeDtypeStruct(cs.shape, cs.dtype) for cs in chip_sums]),
        in_specs=[rev(D), rev(D), rev(D), rev(D_Z), zprev, rev(D), _full((8, D)), _full((1, D)),
                  _resident((D_Z, D)), _resident((D, D)), _full((N_HEADS, CHUNK, CHUNK)), _full((CHUNK, D_A)),
                  _full((1, D_A)), _full((1, D_A)), _full((len(WINDOWS), GROUP, GROUP)), _full((1, D_B)),
                  _full((1, D_B)), _full((8, D)), _full((8, D))] + [_resident(cs.shape) for cs in chip_sums],
        out_specs=tuple([rev(D), _resident((D_Z, D)), _resident((D, D)), _full((SMALL_ROWS, D))] + [hbm] * n_sums),
        scratch_shapes=[pltpu.VMEM((HALO, D_B), F32), pltpu.VMEM((D_Z, D), F32), pltpu.VMEM((D, D), F32),
                        pltpu.VMEM((tt, 2 * D_A), F32), pltpu.VMEM((CHUNK // 2, D), F32),
                        pltpu.SemaphoreType.DMA((2 * n_sums,)), pltpu.SemaphoreType.DMA((2 * n_sums,))],
        compiler_params=pltpu.CompilerParams(dimension_semantics=("arbitrary",), vmem_limit_bytes=VMEM_LIMIT),
    )(dmix, dx1, x, z, z, cat, mod, n1pre, w_in_t, w_out, w_sp, bs_rows, ln_g, ln_b, w_pool, b_pool, pool_scale,
      red_fwd, red_bwd, *chip_sums)
    return outs[:4], outs[4:]


def _adam(w, g, m, v):
    m2 = ADAM_B1 * m + (1.0 - ADAM_B1) * g
    v2 = ADAM_B2 * v + (1.0 - ADAM_B2) * (g * g)
    m_hat = m2 / (1.0 - ADAM_B1 ** ADAM_STEP)
    v_hat = v2 / (1.0 - ADAM_B2 ** ADAM_STEP)
    delta = -ADAM_LR * (m_hat / (jnp.sqrt(v_hat) + ADAM_EPS) + ADAM_WD * w)
    return delta, m2, v2


def _adamw_shard(name, rb, w, g, m, v):
    rows, cols = w.shape

    def body(w_ref, g_ref, m_ref, v_ref, d_ref, m2_ref, v2_ref):
        d_ref[...], m2_ref[...], v2_ref[...] = _adam(w_ref[...], g_ref[...], m_ref[...], v_ref[...])

    blk = pl.BlockSpec((rb, cols), lambda i: (i, 0))
    shp = jax.ShapeDtypeStruct((rows, cols), F32)
    return pl.pallas_call(
        body, name=name, grid=(rows // rb,), out_shape=(shp, shp, shp),
        in_specs=[blk] * 4, out_specs=(blk, blk, blk),
        compiler_params=pltpu.CompilerParams(dimension_semantics=("arbitrary",)),
    )(w, g, m, v)


def _adamw_ada(rb, w, sc, dmod_cols, m, v):
    rows, cols = w.shape

    def body(w_ref, sc_ref, dm_ref, m_ref, v_ref, g_ref, d_ref, m2_ref, v2_ref):
        g = _dot_tn(sc_ref[...].astype(BF16), dm_ref[...].astype(BF16))
        g_ref[...] = g
        d_ref[...], m2_ref[...], v2_ref[...] = _adam(w_ref[...], g, m_ref[...], v_ref[...])

    blk = pl.BlockSpec((rb, cols), lambda i: (i, 0))
    shp = jax.ShapeDtypeStruct((rows, cols), F32)
    return pl.pallas_call(
        body, name="adamw_ada", grid=(rows // rb,), out_shape=(shp, shp, shp, shp),
        in_specs=[blk, pl.BlockSpec((N_DEV, rb), lambda i: (0, i)), _full((N_DEV, cols)), blk, blk],
        out_specs=(blk, blk, blk, blk),
        compiler_params=pltpu.CompilerParams(dimension_semantics=("arbitrary",)),
    )(w, sc, dmod_cols, m, v)


def _unfold(acc_rows):
    return jnp.concatenate([acc_rows[:, :D_A], acc_rows[:, D_A:]], axis=0)


def _adamw_small(total, params):
    n = len(params)
    flat = [a for p in params for a in p]

    def body(*refs):
        s_ref = refs[0]
        p_refs = refs[1:1 + 3 * n]
        loss_ref = refs[1 + 3 * n]
        o_refs = refs[2 + 3 * n:]
        d_b_ada = s_ref[0:6, :]
        for b in range(1, N_DEV):
            d_b_ada = d_b_ada + s_ref[8 * b:8 * b + 6, :]
        tot = s_ref[PACK_SHIFT:PACK_ROWS, :]
        loss_ref[...] = jnp.broadcast_to(tot[ROW_LOSS:ROW_LOSS + 1, 0:GROUP], (8, GROUP))
        mask = _tril_mask()
        ws = _unfold(tot[ROW_WS:ROW_WS + 64, :])
        wp = _unfold(tot[ROW_WP:ROW_WP + 64, :])
        grads = [
            d_b_ada,
            tot[ROW_N1PRE:ROW_N1PRE + 1, :], tot[ROW_N1POST:ROW_N1POST + 1, :],
            tot[ROW_N2PRE:ROW_N2PRE + 1, :], tot[ROW_N2POST:ROW_N2POST + 1, :],
            tot[ROW_LN:ROW_LN + 1, :D_A], tot[ROW_LN:ROW_LN + 1, D_A:],
            tot[ROW_POOL:ROW_POOL + 1, :D_B], tot[ROW_POOL:ROW_POOL + 1, D_B:],
            tot[ROW_BS:ROW_BS + N_HEADS, 0:GROUP],
            jnp.stack([ws[:, h * GROUP:(h + 1) * GROUP] * mask for h in range(N_HEADS)]),
            jnp.stack([wp[:, g * GROUP:(g + 1) * GROUP] for g in range(len(WINDOWS))]),
        ]
        for k in range(n):
            w_ref, m_ref, v_ref = p_refs[3 * k:3 * k + 3]
            g = grads[k]
            o_refs[4 * k][...] = g
            o_refs[4 * k + 1][...], o_refs[4 * k + 2][...], o_refs[4 * k + 3][...] = _adam(
                w_ref[...], g, m_ref[...], v_ref[...])

    vm = pl.BlockSpec(memory_space=pltpu.VMEM)
    out_shape = [jax.ShapeDtypeStruct((8, GROUP), F32)]
    for w, _, _ in params:
        out_shape += [jax.ShapeDtypeStruct(w.shape, F32)] * 4
    return pl.pallas_call(
        body, name="adamw_small", out_shape=tuple(out_shape),
        in_specs=[vm] * (1 + 3 * n), out_specs=tuple([vm] * len(out_shape)),
    )(total, *flat)


TT_ATTN_FWD = 512
TT_MLP_FWD = 512
TT_MLP = 256
TT_WGRAD = 2048
TT_ATTN_BWD = 512


def kernel(x, c, w_ada, b_ada, norm1_pre, norm1_post, w_in, w_spatial, b_spatial, ln_v_gain, ln_v_bias, w_pool, b_pool, pool_scale, w_out, norm2_pre, norm2_post, w_fc1, w_fc2, loss_target, m_w_ada, m_b_ada, m_norm1_pre, m_norm1_post, m_w_in, m_w_spatial, m_b_spatial, m_ln_v_gain, m_ln_v_bias, m_w_pool, m_b_pool, m_pool_scale, m_w_out, m_norm2_pre, m_norm2_post, m_w_fc1, m_w_fc2, v_w_ada, v_b_ada, v_norm1_pre, v_norm1_post, v_w_in, v_w_spatial, v_b_spatial, v_ln_v_gain, v_ln_v_bias, v_w_pool, v_b_pool, v_pool_scale, v_w_out, v_norm2_pre, v_norm2_post, v_w_fc1, v_w_fc2):
    t_len = x.shape[1]
    me = 4 * lax.axis_index("x") + 2 * lax.axis_index("y") + lax.axis_index("c")
    ada_cols = w_ada.shape[1]
    tt = lambda want: min(want, t_len)

    x2 = x.reshape(t_len, D)
    tgt = loss_target.reshape(t_len, D)
    row = lambda a: a.reshape(1, -1)

    b_my = lax.dynamic_slice_in_dim(b_ada, me * ada_cols, ada_cols).reshape(1, ada_cols)
    modp, sc, (g_in, g_out), fc_shards = _fwd_comm(jnp.broadcast_to(c, (8, D)), w_ada, b_my,
                                                   [w_in.T, w_out], [w_fc1, w_fc2])
    mod = jnp.concatenate([modp.reshape(6, D), jnp.zeros((2, D), F32)], axis=0)
    w_in_t = g_in.reshape(D_Z, D)
    w_out_all = g_out.reshape(D, D)

    bs_rows = jnp.repeat(b_spatial.T, GROUP, axis=1)
    attn_consts = (w_spatial, bs_rows, row(ln_v_gain), row(ln_v_bias), w_pool, row(b_pool), row(pool_scale))

    (z, cat, mix, x1), (w1_early, w2_early) = _attn_fwd(
        tt(TT_ATTN_FWD), x2, mod, row(norm1_pre), row(norm1_post), w_in_t, w_out_all, *attn_consts, fc_shards)
    (r_early, h2, f_early), (w1_late, w2_late) = _mlp_fwd_early(
        tt(TT_MLP_FWD), x1, mod, row(norm2_pre), w1_early, w2_early)
    r, df, da, dmix, dx1, red_fwd, red_bwd = _mlp_late_bwd(
        tt(TT_MLP), r_early, x1, h2, f_early, tgt, mix, mod, row(norm2_pre), row(norm2_post), row(norm1_post),
        w1_early, w2_early, w1_late, w2_late)
    own_w1, own_w2, sums_w1, sums_w2, diag_w1, diag_w2 = _mlp_wgrad(tt(TT_WGRAD), r, da, df, h2)
    (grad_x, p_in, p_out, small), (arr_w1, arr_w2) = _attn_bwd(
        tt(TT_ATTN_BWD), dmix, dx1, x2, z, cat, mod, row(norm1_pre), w_in_t, w_out_all, *attn_consts,
        red_fwd, red_bwd, [sums_w1, sums_w2])
    (grad_in_t, grad_out, total), ((grad_w1, d_w1, m_w1, v_w1), (grad_w2, d_w2, m_w2, v_w2)) = _tail_comm(
        [p_in.reshape(N_DEV, D_Z // N_DEV, D), p_out.reshape(N_DEV, D // N_DEV, D)], small, 64,
        [(w_fc1, own_w1, arr_w1, diag_w1, m_w_fc1, v_w_fc1), (w_fc2, own_w2, arr_w2, diag_w2, m_w_fc2, v_w_fc2)])

    d_out, m_out, v_out = _adamw_shard("adamw_out", 128, w_out, grad_out, m_w_out, v_w_out)
    d_in_t, m_in_t, v_in_t = _adamw_shard("adamw_in", D_Z // N_DEV, w_in.T, grad_in_t, m_w_in.T, v_w_in.T)
    dmod_all = total[0:TABLE_ROWS, :].reshape(N_DEV, 8, D)[:, :6, :].reshape(N_DEV, 6 * D)
    dmod_cols = lax.dynamic_slice_in_dim(dmod_all, me * ada_cols, ada_cols, axis=1)
    grad_ada, d_ada, m_ada, v_ada = _adamw_ada(256, w_ada, sc, dmod_cols, m_w_ada, v_w_ada)

    six = lambda a: a.reshape(6, D)
    small_params = [
        (six(b_ada), six(m_b_ada), six(v_b_ada)),
        (row(norm1_pre), row(m_norm1_pre), row(v_norm1_pre)),
        (row(norm1_post), row(m_norm1_post), row(v_norm1_post)),
        (row(norm2_pre), row(m_norm2_pre), row(v_norm2_pre)),
        (row(norm2_post), row(m_norm2_post), row(v_norm2_post)),
        (row(ln_v_gain), row(m_ln_v_gain), row(v_ln_v_gain)),
        (row(ln_v_bias), row(m_ln_v_bias), row(v_ln_v_bias)),
        (row(pool_scale), row(m_pool_scale), row(v_pool_scale)),
        (row(b_pool), row(m_b_pool), row(v_b_pool)),
        (b_spatial, m_b_spatial, v_b_spatial),
        (w_spatial, m_w_spatial, v_w_spatial),
        (w_pool, m_w_pool, v_w_pool),
    ]
    outs = _adamw_small(total, small_params)
    loss = outs[0][0, 0]
    names = ["b_ada", "norm1_pre", "norm1_post", "norm2_pre", "norm2_post", "ln_v_gain", "ln_v_bias", "pool_scale",
             "b_pool", "b_spatial", "w_spatial", "w_pool"]
    shapes = dict(b_ada=b_ada.shape, norm1_pre=norm1_pre.shape, norm1_post=norm1_post.shape,
                  norm2_pre=norm2_pre.shape, norm2_post=norm2_post.shape, ln_v_gain=ln_v_gain.shape,
                  ln_v_bias=ln_v_bias.shape, pool_scale=pool_scale.shape, b_pool=b_pool.shape,
                  b_spatial=b_spatial.shape, w_spatial=w_spatial.shape, w_pool=w_pool.shape)
    res = {}
    for k, nm in enumerate(names):
        res[nm] = tuple(o.reshape(shapes[nm]) for o in outs[1 + 4 * k:5 + 4 * k])
    res["w_ada"] = (grad_ada, d_ada, m_ada, v_ada)
    res["w_in"] = (grad_in_t.T, d_in_t.T, m_in_t.T, v_in_t.T)
    res["w_out"] = (grad_out, d_out, m_out, v_out)
    res["w_fc1"] = (grad_w1, d_w1, m_w1, v_w1)
    res["w_fc2"] = (grad_w2, d_w2, m_w2, v_w2)

    order = ["w_ada", "b_ada", "norm1_pre", "norm1_post", "w_in", "w_spatial", "b_spatial", "ln_v_gain", "ln_v_bias",
             "w_pool", "b_pool", "pool_scale", "w_out", "norm2_pre", "norm2_post", "w_fc1", "w_fc2"]
    return (loss, grad_x.reshape(x.shape),
            *[res[nm][0] for nm in order], *[res[nm][1] for nm in order],
            *[res[nm][2] for nm in order], *[res[nm][3] for nm in order])
```

```python
import functools

import jax
import jax.numpy as jnp
from jax import lax
from jax.experimental import pallas as pl
from jax.experimental.pallas import tpu as pltpu

F32 = jnp.float32
BF16 = jnp.bfloat16
MESH = pl.DeviceIdType.MESH

N_DEV = 8
D = 1024
D_A = 512
D_B = 512
D_Z = 2 * D_A + D_B
N_HEADS = 4
CHUNK = 128
WINDOWS = (2, 4, 8, 16)
GROUP = 128
D_FF = 4096
FF_BLK = D_FF // N_DEV
HALO = 16
EPS = 1e-6
VMEM_LIMIT = 60 * 1024 * 1024

ADAM_LR = 0.001
ADAM_B1 = 0.9
ADAM_B2 = 0.999
ADAM_EPS = 1e-08
ADAM_WD = 0.01
ADAM_STEP = 10

ROW_DMOD = 0
ROW_N1PRE, ROW_N1POST, ROW_N2PRE, ROW_N2POST = 8, 9, 10, 11
ROW_LN = 12
ROW_POOL = 13
ROW_LOSS = 14
ROW_BS = 16
ROW_WS = 24
ROW_WP = 88
SMALL_ROWS = 152
TABLE_ROWS = 8 * N_DEV
PACK_SHIFT = TABLE_ROWS - 8
PACK_ROWS = SMALL_ROWS + PACK_SHIFT
PACK_HALF = PACK_ROWS // 2


def _dot(a, b):
    return jnp.dot(a, b, preferred_element_type=F32)


def _dot_nt(a, b):
    return lax.dot_general(a, b, (((1,), (1,)), ((), ())), preferred_element_type=F32)


def _dot_tn(a, b):
    return lax.dot_general(a, b, (((0,), (0,)), ((), ())), preferred_element_type=F32)


def _rstd(v):
    return lax.rsqrt(jnp.mean(v * v, axis=-1, keepdims=True) + EPS)


def _rms_bwd(d_hat, hat, rstd):
    return rstd * (d_hat - hat * jnp.mean(d_hat * hat, axis=-1, keepdims=True))


_K0 = 0.7978845608028654
_K1 = 0.044715


def _gelu_parts(v):
    t = jnp.tanh(_K0 * (v + _K1 * (v * v * v)))
    return t, v * (0.5 * (1.0 + t))


def _gelu_grad(v, t):
    return 0.5 * (1.0 + t) + (0.5 * v) * (1.0 - t * t) * (_K0 * (1.0 + (3.0 * _K1) * (v * v)))


def _colsum(v):
    return jnp.sum(v, axis=0, keepdims=True)


def _full(shape):
    n = len(shape)
    return pl.BlockSpec(shape, lambda *_: (0,) * n)


def _resident(shape):
    n = len(shape)
    return pl.BlockSpec(shape, lambda *_: (0,) * n, pipeline_mode=pl.Buffered(1))


def _place():
    x, y, c = lax.axis_index("x"), lax.axis_index("y"), lax.axis_index("c")
    return x, y, c


def _flip(v, bit):
    return 1 - v if bit else v


def _peer(x, y, c, k):
    return (_flip(x, (k >> 2) & 1), _flip(y, (k >> 1) & 1), _flip(c, k & 1))


def _index(p):
    return 4 * p[0] + 2 * p[1] + p[2]


def _two_level_gather_begin(x, y, c, out_refs, send_sems, recv_sems):
    me = (x, y, c)
    sibling = (x, y, 1 - c)
    chips = [(1 - x, y), (x, 1 - y), (1 - x, 1 - y)]

    def copy(a, k, block, to):
        ref = out_refs[a].at[_index(block)]
        return pltpu.make_async_remote_copy(
            src_ref=ref, dst_ref=ref, send_sem=send_sems.at[7 * a + k], recv_sem=recv_sems.at[7 * a + k],
            device_id=to, device_id_type=MESH)

    first = []
    for a in range(len(out_refs)):
        first.append(copy(a, 0, me, sibling))
        first += [copy(a, 1 + j, me, (*chip, c)) for j, chip in enumerate(chips)]
    for cp in first:
        cp.start()
    return copy, first, me, sibling, chips


def _two_level_gather_finish(c, n, begun):
    copy, first, me, sibling, chips = begun
    passed = []
    for a in range(n):
        for j, chip in enumerate(chips):
            copy(a, 1 + j, (*chip, c), me).wait_recv()
            fwd = copy(a, 4 + j, (*chip, c), sibling)
            fwd.start()
            passed.append(fwd)
    for a in range(n):
        copy(a, 0, sibling, me).wait_recv()
        for j, chip in enumerate(chips):
            copy(a, 4 + j, (*chip, 1 - c), me).wait_recv()
    for cp in first + passed:
        cp.wait_send()


def _fwd_comm(c8, w_ada, b_my, gathered, kept):
    ncol = w_ada.shape[1]
    n_g, n_k = len(gathered), len(kept)

    def body(c_ref, w_ref, b_ref, *rest):
        g_in, k_in = rest[:n_g], rest[n_g:n_g + n_k]
        modp_ref, sc_ref = rest[n_g + n_k:n_g + n_k + 2]
        g_out = rest[n_g + n_k + 2:2 * n_g + n_k + 2]
        k_out = rest[2 * n_g + n_k + 2:2 * n_g + 2 * n_k + 2]
        cg, mg, part, send_sems, recv_sems, g_send, g_recv = rest[2 * n_g + 2 * n_k + 2:]
        x, y, c = _place()
        me = _index((x, y, c))
        for a in range(n_g):
            g_out[a][me] = g_in[a][...].astype(BF16)
        begun = _two_level_gather_begin(x, y, c, g_out, g_send, g_recv)
        for a in range(n_k):
            k_out[a][...] = k_in[a][...].astype(BF16)

        def c_copy(k):
            p = _peer(x, y, c, k)
            return pltpu.make_async_remote_copy(
                src_ref=c_ref, dst_ref=cg.at[me], send_sem=send_sems.at[k - 1], recv_sem=recv_sems.at[k - 1],
                device_id=p, device_id_type=MESH)

        def c_arrival(k):
            p = _peer(x, y, c, k)
            return pltpu.make_async_remote_copy(
                src_ref=c_ref, dst_ref=cg.at[_index(p)], send_sem=send_sems.at[k - 1], recv_sem=recv_sems.at[k - 1],
                device_id=p, device_id_type=MESH)

        def m_copy(k):
            p = _peer(x, y, c, k)
            return pltpu.make_async_remote_copy(
                src_ref=part, dst_ref=mg.at[me], send_sem=send_sems.at[6 + k], recv_sem=recv_sems.at[6 + k],
                device_id=p, device_id_type=MESH)

        def m_arrival(k):
            p = _peer(x, y, c, k)
            return pltpu.make_async_remote_copy(
                src_ref=part, dst_ref=mg.at[_index(p)], send_sem=send_sems.at[6 + k], recv_sem=recv_sems.at[6 + k],
                device_id=p, device_id_type=MESH)

        for k in range(1, N_DEV):
            c_copy(k).start()
        cg[me] = c_ref[...]
        for k in range(1, N_DEV):
            c_arrival(k).wait_recv()
        c_all = jnp.concatenate([cg[j, 0:1, :] for j in range(N_DEV)], axis=0)
        sc = c_all * jax.nn.sigmoid(c_all)
        sc_ref[...] = sc
        part[...] = _dot(sc.astype(BF16), w_ref[...].astype(BF16)) + b_ref[...]
        for k in range(1, N_DEV):
            m_copy(k).start()
        mg[me] = part[...]
        for k in range(1, N_DEV):
            m_arrival(k).wait_recv()
        for j in range(N_DEV):
            modp_ref[j:j + 1, :] = mg[j, pl.ds(me, 1), :]
        _two_level_gather_finish(c, n_g, begun)
        for k in range(1, N_DEV):
            c_copy(k).wait_send()
            m_copy(k).wait_send()

    vm = pl.BlockSpec(memory_space=pltpu.VMEM)
    outs = pl.pallas_call(
        body, name="fwd_comm",
        out_shape=tuple([jax.ShapeDtypeStruct((N_DEV, ncol), F32), jax.ShapeDtypeStruct((N_DEV, D), F32)]
                        + [jax.ShapeDtypeStruct((N_DEV,) + s.shape, BF16) for s in gathered]
                        + [jax.ShapeDtypeStruct(s.shape, BF16) for s in kept]),
        in_specs=[vm] * (3 + n_g + n_k), out_specs=tuple([vm] * (2 + n_g + n_k)),
        scratch_shapes=[
            pltpu.VMEM((N_DEV, 8, D), F32),
            pltpu.VMEM((N_DEV, N_DEV, ncol), F32),
            pltpu.VMEM((N_DEV, ncol), F32),
            pltpu.SemaphoreType.DMA((2 * (N_DEV - 1),)),
            pltpu.SemaphoreType.DMA((2 * (N_DEV - 1),)),
            pltpu.SemaphoreType.DMA((7 * n_g,)),
            pltpu.SemaphoreType.DMA((7 * n_g,)),
        ],
        compiler_params=pltpu.CompilerParams(vmem_limit_bytes=VMEM_LIMIT),
    )(c8, w_ada, b_my, *gathered, *kept)
    return outs[0], outs[1], outs[2:2 + n_g], outs[2 + n_g:]


FC_EARLY = 6
WGRAD_ORDER = (7, 6, 1, 3, 5, 2, 4, 0)


class _Copies:
    def __init__(self, entries, send_sems, recv_sems):
        self.place = _place()
        self.entries, self.send_sems, self.recv_sems = entries, send_sems, recv_sems

    def _copy(self, i, arrival=False):
        src, dst, rel = self.entries[i]
        return pltpu.make_async_remote_copy(
            src_ref=dst if arrival else src, dst_ref=dst, send_sem=self.send_sems.at[i],
            recv_sem=self.recv_sems.at[i], device_id=_peer(*self.place, rel), device_id_type=MESH)

    def start(self, *which):
        for i in which:
            self._copy(i).start()

    def wait_recv(self, *which):
        for i in which:
            self._copy(i, arrival=True).wait_recv()

    def wait_send(self, *which):
        for i in which:
            self._copy(i).wait_send()


TAIL_STEPS = 8


def _tail_comm(parts, small, row_chunk, fc):
    n, n_fc = len(parts), len(fc)

    def body(*refs):
        p_refs, small_ref = refs[:n], refs[n]
        fc_in = refs[n + 1:n + 1 + 6 * n_fc]
        outs = refs[n + 1 + 6 * n_fc:]
        g_refs, total_ref = outs[:n], outs[n]
        fc_out = outs[n + 1:n + 1 + 4 * n_fc]
        scr = outs[n + 1 + 4 * n_fc:]
        from_sib = scr[0:n]
        chip_out = scr[n:2 * n]
        chip_in = scr[2 * n:3 * n]
        pack, pack_sib, halves, total_scr = scr[3 * n:3 * n + 4]
        send_a, recv_a, send_b, recv_b, send_s, recv_s = scr[3 * n + 4:]
        step = pl.program_id(0)
        x, y, c = _place()
        me = _index((x, y, c))
        sibling = (x, y, 1 - c)
        my_chip = 2 * x + y
        others = [(1 - x, y), (x, 1 - y), (1 - x, 1 - y)]
        my_half = pl.ds(pl.multiple_of(PACK_HALF * c, 8), PACK_HALF)

        def pack_to_sibling():
            return pltpu.make_async_remote_copy(
                src_ref=pack, dst_ref=pack_sib, send_sem=send_s.at[0], recv_sem=recv_s.at[0],
                device_id=sibling, device_id_type=MESH)

        def half_to_chip(r):
            return pltpu.make_async_remote_copy(
                src_ref=halves.at[my_chip], dst_ref=halves.at[my_chip],
                send_sem=send_s.at[1 + r], recv_sem=recv_s.at[1 + r],
                device_id=(*others[r], c), device_id_type=MESH)

        def half_from_chip(r):
            k = 2 * others[r][0] + others[r][1]
            return pltpu.make_async_remote_copy(
                src_ref=halves.at[k], dst_ref=halves.at[k], send_sem=send_s.at[1 + r], recv_sem=recv_s.at[1 + r],
                device_id=(*others[r], c), device_id_type=MESH)

        def total_to_sibling():
            return pltpu.make_async_remote_copy(
                src_ref=total_scr.at[my_half], dst_ref=total_scr.at[my_half],
                send_sem=send_s.at[4], recv_sem=recv_s.at[4], device_id=sibling, device_id_type=MESH)

        def total_from_sibling():
            sib_half = pl.ds(pl.multiple_of(PACK_HALF * (1 - c), 8), PACK_HALF)
            return pltpu.make_async_remote_copy(
                src_ref=total_scr.at[sib_half], dst_ref=total_scr.at[sib_half],
                send_sem=send_s.at[4], recv_sem=recv_s.at[4], device_id=sibling, device_id_type=MESH)

        def to_sibling(a, k):
            return pltpu.make_async_remote_copy(
                src_ref=p_refs[a].at[2 * k + (1 - c)], dst_ref=from_sib[a].at[k],
                send_sem=send_a.at[a], recv_sem=recv_a.at[a], device_id=sibling, device_id_type=MESH)

        def all_from_sibling(a):
            return pltpu.make_async_remote_copy(
                src_ref=from_sib[a], dst_ref=from_sib[a], send_sem=send_a.at[a], recv_sem=recv_a.at[a],
                device_id=sibling, device_id_type=MESH)

        def to_chip(a, r):
            return pltpu.make_async_remote_copy(
                src_ref=chip_out[a].at[r], dst_ref=chip_in[a].at[r],
                send_sem=send_b.at[3 * a + r], recv_sem=recv_b.at[3 * a + r],
                device_id=(*others[r], c), device_id_type=MESH)

        @pl.when(step == 0)
        def _():
            pack[0:TABLE_ROWS, :] = jnp.zeros((TABLE_ROWS, D), F32)
            pack[pl.ds(pl.multiple_of(8 * me, 8), 8), :] = small_ref[0:8, :]
            pack[TABLE_ROWS:PACK_ROWS, :] = small_ref[8:SMALL_ROWS, :]
            pack_to_sibling().start()
            for a in range(n):
                for k in range(4):
                    to_sibling(a, k).start()

        @pl.when(step == 1)
        def _():
            pack_to_sibling().wait_recv()
            halves[my_chip] = pack[my_half, :] + pack_sib[my_half, :]
            for r in range(3):
                half_to_chip(r).start()
            for a in range(n):
                all_from_sibling(a).wait_recv()
                rows = p_refs[a].shape[1]
                for r in range(3):
                    k = 2 * others[r][0] + others[r][1]
                    for s in range(0, rows, row_chunk):
                        sl = pl.ds(s, row_chunk)
                        chip_out[a][r, sl, :] = (p_refs[a][2 * k + c, sl, :].astype(F32)
                                                 + from_sib[a][k, sl, :].astype(F32)).astype(BF16)
                    to_chip(a, r).start()
                for s in range(0, rows, row_chunk):
                    sl = pl.ds(s, row_chunk)
                    g_refs[a][sl, :] = (p_refs[a][2 * my_chip + c, sl, :].astype(F32)
                                        + from_sib[a][my_chip, sl, :].astype(F32))

        for k in range(n_fc):
            w_ref, own_ref, arr_ref, diag_ref, m_ref, v_ref = fc_in[6 * k:6 * k + 6]
            g = own_ref[...]
            for r in range(2):
                g = g + arr_ref[r].astype(F32)
            g = g + diag_ref[...].astype(F32)
            fc_out[4 * k][...] = g
            fc_out[4 * k + 1][...], fc_out[4 * k + 2][...], fc_out[4 * k + 3][...] = _adam(
                w_ref[...], g, m_ref[...], v_ref[...])

        @pl.when(step == TAIL_STEPS - 1)
        def _():
            for r in range(3):
                half_from_chip(r).wait_recv()
            total_scr[my_half, :] = ((halves[0] + halves[1]) + halves[2]) + halves[3]
            total_to_sibling().start()
            for a in range(n):
                rows = p_refs[a].shape[1]
                for r in range(3):
                    to_chip(a, r).wait_recv()
                    for s in range(0, rows, row_chunk):
                        sl = pl.ds(s, row_chunk)
                        g_refs[a][sl, :] = g_refs[a][sl, :] + chip_in[a][r, sl, :].astype(F32)
            total_from_sibling().wait_recv()
            total_ref[...] = total_scr[...]
            for a in range(n):
                all_from_sibling(a).wait_send()
                for r in range(3):
                    to_chip(a, r).wait_send()
            pack_to_sibling().wait_send()
            for r in range(3):
                half_to_chip(r).wait_send()
            total_to_sibling().wait_send()

    fc_specs_in, fc_specs_out, fc_shapes, fc_args = [], [], [], []
    for w, own, arrived, diagonal, m, v in fc:
        rows, cols = w.shape
        blk = pl.BlockSpec((rows // TAIL_STEPS, cols), lambda i: (i, 0))
        fc_specs_in += [blk, blk, pl.BlockSpec((2, rows // TAIL_STEPS, cols), lambda i: (0, i, 0)), blk, blk, blk]
        fc_specs_out += [blk] * 4
        fc_shapes += [jax.ShapeDtypeStruct((rows, cols), F32)] * 4
        fc_args += [w, own, arrived, diagonal, m, v]
    outs = pl.pallas_call(
        body, name="tail_comm", grid=(TAIL_STEPS,),
        out_shape=tuple([jax.ShapeDtypeStruct(p.shape[1:], F32) for p in parts]
                        + [jax.ShapeDtypeStruct((PACK_ROWS, D), F32)] + fc_shapes),
        in_specs=[_resident(p.shape) for p in parts] + [_resident(small.shape)] + fc_specs_in,
        out_specs=tuple([_full(p.shape[1:]) for p in parts] + [_full((PACK_ROWS, D))] + fc_specs_out),
        scratch_shapes=(
            [pltpu.VMEM((4,) + p.shape[1:], BF16) for p in parts]
            + [pltpu.VMEM((3,) + p.shape[1:], BF16) for p in parts]
            + [pltpu.VMEM((3,) + p.shape[1:], BF16) for p in parts]
            + [pltpu.VMEM((PACK_ROWS, D), F32), pltpu.VMEM((PACK_ROWS, D), F32),
               pltpu.VMEM((4, PACK_HALF, D), F32), pltpu.VMEM((PACK_ROWS, D), F32)]
            + [pltpu.SemaphoreType.DMA((n,)), pltpu.SemaphoreType.DMA((n,)),
               pltpu.SemaphoreType.DMA((3 * n,)), pltpu.SemaphoreType.DMA((3 * n,)),
               pltpu.SemaphoreType.DMA((5,)), pltpu.SemaphoreType.DMA((5,))]),
        compiler_params=pltpu.CompilerParams(dimension_semantics=("arbitrary",), vmem_limit_bytes=VMEM_LIMIT),
    )(*parts, small, *fc_args)
    return outs[:n + 1], [outs[n + 1 + 4 * k:n + 5 + 4 * k] for k in range(n_fc)]


def _tril_mask():
    row = lax.broadcasted_iota(jnp.int32, (CHUNK, CHUNK), 0)
    col = lax.broadcasted_iota(jnp.int32, (CHUNK, CHUNK), 1)
    return (col <= row).astype(F32)


def _window_sums(ext):
    s2 = ext + pltpu.roll(ext, 1, 0)
    t4 = s2[:, GROUP:]
    s4 = t4 + pltpu.roll(t4, 2, 0)
    t8 = s4[:, GROUP:]
    s8 = t8 + pltpu.roll(t8, 4, 0)
    t16 = s8[:, GROUP:]
    s16 = t16 + pltpu.roll(t16, 8, 0)
    return [s2[:, :GROUP], s4[:, :GROUP], s8[:, :GROUP], s16]


def _inv_counts(first_pos, rows):
    pos = first_pos + lax.broadcasted_iota(jnp.int32, (rows, 1), 0)
    return [1.0 / jnp.minimum(pos + 1, w).astype(F32) for w in WINDOWS]


def _pool_diff(zb, halo, first_pos):
    tt = zb.shape[0]
    sums = _window_sums(jnp.concatenate([halo, zb], axis=0))
    inv = _inv_counts(first_pos, tt)
    return [sums[g][HALO:, :] * inv[g] - zb[:, g * GROUP:(g + 1) * GROUP] for g in range(len(WINDOWS))]


def _attn_fwd(tt, x, mod, n1pre, n1post, w_in_t, w_out, w_sp, bs_rows, ln_g, ln_b, w_pool, b_pool, pool_scale,
              fc_shards):
    t_len = x.shape[0]
    nt = t_len // tt

    def body(x_ref, mod_ref, n1pre_ref, n1post_ref, win_ref, wout_ref, wsp_ref, bs_ref, lng_ref, lnb_ref,
             wp_ref, bp_ref, ps_ref, w1_ref, w2_ref, z_ref, cat_ref, mix_ref, x1_ref, e1_ref, e2_ref,
             carry, land1, land2, send_sems, recv_sems, local_sems):
        i = pl.program_id(0)
        copies = _Copies(
            [(w1_ref, e1_ref.at[1], 1), (w2_ref, e2_ref.at[1], 1),
             (w1_ref, land1.at[0], 2), (w2_ref, land2.at[0], 2),
             (w1_ref, land1.at[1], 4), (w2_ref, land2.at[1], 4),
             (land1.at[0], e1_ref.at[3], 1), (land2.at[0], e2_ref.at[3], 1),
             (land1.at[1], e1_ref.at[5], 1), (land2.at[1], e2_ref.at[5], 1)],
            send_sems, recv_sems)
        keep = [pltpu.make_async_copy(w1_ref, e1_ref.at[0], local_sems.at[0]),
                pltpu.make_async_copy(w2_ref, e2_ref.at[0], local_sems.at[1]),
                pltpu.make_async_copy(land1.at[0], e1_ref.at[2], local_sems.at[2]),
                pltpu.make_async_copy(land1.at[1], e1_ref.at[4], local_sems.at[3]),
                pltpu.make_async_copy(land2.at[0], e2_ref.at[2], local_sems.at[4]),
                pltpu.make_async_copy(land2.at[1], e2_ref.at[4], local_sems.at[5])]

        @pl.when(i == 0)
        def _():
            copies.start(2, 4, 3, 5, 0, 1)
            keep[0].start()
            keep[1].start()
            carry[...] = jnp.zeros_like(carry)

        @pl.when(i == nt // 2)
        def _():
            copies.wait_recv(2, 4)
            copies.start(6, 8)
            keep[2].start()
            keep[3].start()

        @pl.when(i == nt - 1)
        def _():
            copies.wait_recv(3, 5)
            copies.start(7, 9)
            keep[4].start()
            keep[5].start()

        xv = x_ref[...]
        shift1, scale1, gate1 = mod_ref[0:1, :], mod_ref[1:2, :], mod_ref[2:3, :]
        h1 = (xv * _rstd(xv) * n1pre_ref[...]) * (1.0 + scale1) + shift1
        z = _dot_nt(h1.astype(BF16), win_ref[...])
        z_ref[...] = z

        _, ga = _gelu_parts(z[:, :2 * D_A])
        u, vr = ga[:, :D_A], ga[:, D_A:]
        dv = vr - jnp.mean(vr, axis=-1, keepdims=True)
        v = (dv * lax.rsqrt(jnp.mean(dv * dv, axis=-1, keepdims=True) + EPS)) * lng_ref[...] + lnb_ref[...]
        vb = v.astype(BF16)
        mask = _tril_mask()
        wc = [(wsp_ref[h] * mask).astype(BF16) for h in range(N_HEADS)]
        for ch in range(tt // CHUNK):
            rows = slice(ch * CHUNK, (ch + 1) * CHUNK)
            for h in range(N_HEADS):
                cols = slice(h * GROUP, (h + 1) * GROUP)
                mixed = _dot(wc[h], vb[rows, cols]) + bs_ref[:, cols]
                cat_ref[rows, cols] = (u[rows, cols] * mixed).astype(BF16)

        zb = z[:, 2 * D_A:]
        diff = _pool_diff(zb, carry[...], i * tt)
        carry[...] = zb[tt - HALO:, :]
        for g in range(len(WINDOWS)):
            cols = slice(g * GROUP, (g + 1) * GROUP)
            pre = _dot(diff[g].astype(BF16), wp_ref[g].astype(BF16)) + bp_ref[:, cols]
            cat_ref[:, D_A + g * GROUP:D_A + (g + 1) * GROUP] = (pre * ps_ref[:, cols]).astype(BF16)

        mix = _dot(cat_ref[...], wout_ref[...])
        mix_ref[...] = mix
        x1_ref[...] = xv + gate1 * (mix * _rstd(mix) * n1post_ref[...])

        @pl.when(i == nt - 1)
        def _():
            copies.wait_recv(0, 1, 6, 7, 8, 9)
            copies.wait_send(*range(10))
            for cp in keep:
                cp.wait()

    tile = lambda w: pl.BlockSpec((tt, w), lambda i: (i, 0))
    hbm = pl.BlockSpec(memory_space=pl.ANY)
    outs = pl.pallas_call(
        body, name="attn_fwd", grid=(nt,),
        out_shape=tuple([jax.ShapeDtypeStruct((t_len, D_Z), F32), jax.ShapeDtypeStruct((t_len, D), BF16),
                         jax.ShapeDtypeStruct((t_len, D), F32), jax.ShapeDtypeStruct((t_len, D), F32)]
                        + [jax.ShapeDtypeStruct((FC_EARLY,) + s.shape, BF16) for s in fc_shards]),
        in_specs=[tile(D), _full((8, D)), _full((1, D)), _full((1, D)), _resident((D_Z, D)), _resident((D, D)),
                  _full((N_HEADS, CHUNK, CHUNK)), _full((CHUNK, D_A)), _full((1, D_A)), _full((1, D_A)),
                  _full((len(WINDOWS), GROUP, GROUP)), _full((1, D_B)), _full((1, D_B)),
                  _resident(fc_shards[0].shape), _resident(fc_shards[1].shape)],
        out_specs=(tile(D_Z), tile(D), tile(D), tile(D), hbm, hbm),
        scratch_shapes=[pltpu.VMEM((HALO, D_B), F32),
                        pltpu.VMEM((2,) + fc_shards[0].shape, BF16), pltpu.VMEM((2,) + fc_shards[1].shape, BF16),
                        pltpu.SemaphoreType.DMA((10,)), pltpu.SemaphoreType.DMA((10,)),
                        pltpu.SemaphoreType.DMA((6,))],
        compiler_params=pltpu.CompilerParams(dimension_semantics=("arbitrary",), vmem_limit_bytes=VMEM_LIMIT),
    )(x, mod, n1pre, n1post, w_in_t, w_out, w_sp, bs_rows, ln_g, ln_b, w_pool, b_pool, pool_scale, *fc_shards)
    return outs[:4], outs[4:]


def _mlp_fwd_early(tt, x1, mod, n2pre, w1_early, w2_early):
    t_len = x1.shape[0]
    nt = t_len // tt
    n_late = N_DEV - FC_EARLY

    def body(x1_ref, mod_ref, n2pre_ref, w1_ref, w2_ref, r_ref, h2_ref, f_ref, l1_ref, l2_ref,
             land1, land2, send_sems, recv_sems, local_sems):
        i = pl.program_id(0)
        copies = _Copies(
            [(w1_ref.at[2], land1, 4), (w2_ref.at[4], land2, 2),
             (land1, l1_ref.at[1], 1), (land2, l2_ref.at[1], 1)],
            send_sems, recv_sems)
        keep = [pltpu.make_async_copy(land1, l1_ref.at[0], local_sems.at[0]),
                pltpu.make_async_copy(land2, l2_ref.at[0], local_sems.at[1])]

        @pl.when(i == 0)
        def _():
            copies.start(0, 1)

        @pl.when(i == nt // 2)
        def _():
            copies.wait_recv(0, 1)
            copies.start(2, 3)
            for cp in keep:
                cp.start()

        x1v = x1_ref[...]
        shift2, scale2 = mod_ref[3:4, :], mod_ref[4:5, :]
        h2 = ((x1v * _rstd(x1v) * n2pre_ref[...]) * (1.0 + scale2) + shift2).astype(BF16)
        h2_ref[...] = h2
        for j in range(FC_EARLY):
            cols = slice(j * FF_BLK, (j + 1) * FF_BLK)
            ra = jnp.maximum(_dot(h2, w1_ref[j]), 0.0)
            r = (ra * ra).astype(BF16)
            r_ref[:, cols] = r
            contrib = _dot(r, w2_ref[j])
            if j == 0:
                f_ref[...] = contrib
            else:
                f_ref[...] += contrib

        @pl.when(i == nt - 1)
        def _():
            copies.wait_recv(2, 3)
            copies.wait_send(0, 1, 2, 3)
            for cp in keep:
                cp.wait()

    tile = lambda w: pl.BlockSpec((tt, w), lambda i: (i, 0))
    hbm = pl.BlockSpec(memory_space=pl.ANY)
    outs = pl.pallas_call(
        body, name="mlp_fwd_early", grid=(nt,),
        out_shape=(jax.ShapeDtypeStruct((t_len, FC_EARLY * FF_BLK), BF16),
                   jax.ShapeDtypeStruct((t_len, D), BF16), jax.ShapeDtypeStruct((t_len, D), F32),
                   jax.ShapeDtypeStruct((n_late,) + w1_early.shape[1:], BF16),
                   jax.ShapeDtypeStruct((n_late,) + w2_early.shape[1:], BF16)),
        in_specs=[tile(D), _full((8, D)), _full((1, D)),
                  _resident((FC_EARLY, D, FF_BLK)), _resident((FC_EARLY, FF_BLK, D))],
        out_specs=(tile(FC_EARLY * FF_BLK), tile(D), tile(D), hbm, hbm),
        scratch_shapes=[pltpu.VMEM(w1_early.shape[1:], BF16), pltpu.VMEM(w2_early.shape[1:], BF16),
                        pltpu.SemaphoreType.DMA((4,)), pltpu.SemaphoreType.DMA((4,)),
                        pltpu.SemaphoreType.DMA((2,))],
        compiler_params=pltpu.CompilerParams(dimension_semantics=("arbitrary",), vmem_limit_bytes=VMEM_LIMIT),
    )(x1, mod, n2pre, w1_early, w2_early)
    return outs[:3], outs[3:]


def _mlp_late_bwd(tt, r_early, x1, h2, f_early, tgt, mix, mod, n2pre, n2post, n1post,
                  w1_early, w2_early, w1_late, w2_late):
    t_len = x1.shape[0]
    nt = t_len // tt
    n_late = N_DEV - FC_EARLY
    late_cols = n_late * FF_BLK

    def body(re_ref, x1_ref, h2_ref, fe_ref, tgt_ref, mix_ref, mod_ref, n2pre_ref, n2post_ref,
             n1post_ref, w1e_ref, w2e_ref, w1l_ref, w2l_ref,
             rl_ref, df_ref, da_ref, dmix_ref, dx1_ref, redf_ref, redb_ref, dh2_acc):
        i = pl.program_id(0)

        @pl.when(i == 0)
        def _():
            redf_ref[...] = jnp.zeros_like(redf_ref)
            redb_ref[...] = jnp.zeros_like(redb_ref)

        x1v = x1_ref[...]
        gate1, scale2, gate2 = mod_ref[2:3, :], mod_ref[4:5, :], mod_ref[5:6, :]
        h2 = h2_ref[...]
        f = fe_ref[...]
        for j in range(n_late):
            cols = slice(j * FF_BLK, (j + 1) * FF_BLK)
            ra = jnp.maximum(_dot(h2, w1l_ref[j]), 0.0)
            r = (ra * ra).astype(BF16)
            rl_ref[:, cols] = r
            f = f + _dot(r, w2l_ref[j])
        rf = _rstd(f)
        fhat = f * rf
        nf = fhat * n2post_ref[...]
        err = (x1v + gate2 * nf) - tgt_ref[...]
        dy = err * (1.0 / D)
        dnf = dy * gate2
        dfv = _rms_bwd(dnf * n2post_ref[...], fhat, rf).astype(BF16)
        df_ref[...] = dfv
        redf_ref[0:1, :] += _colsum(dy * nf)
        redf_ref[1:2, :] += _colsum(dnf * fhat)
        redf_ref[2:3, :] += _colsum(0.5 * jnp.mean(err * err, axis=-1, keepdims=True)) * jnp.ones((1, D), F32)

        for j in range(N_DEV):
            cols = slice(j * FF_BLK, (j + 1) * FF_BLK)
            if j < FC_EARLY:
                w1, w2, r = w1e_ref[j], w2e_ref[j], re_ref[:, cols]
            else:
                jl = j - FC_EARLY
                w1, w2, r = w1l_ref[jl], w2l_ref[jl], rl_ref[:, jl * FF_BLK:(jl + 1) * FF_BLK]
            dr = _dot_nt(dfv, w2)
            da = (dr * (2.0 * jnp.sqrt(r.astype(F32)))).astype(BF16)
            da_ref[:, cols] = da
            contrib = _dot_nt(da, w1)
            if j == 0:
                dh2_acc[...] = contrib
            else:
                dh2_acc[...] += contrib
        dh2 = dh2_acc[...]
        r2 = _rstd(x1v)
        xhat = x1v * r2
        n2 = xhat * n2pre_ref[...]
        dn2 = dh2 * (1.0 + scale2)
        dx1 = dy + _rms_bwd(dn2 * n2pre_ref[...], xhat, r2)
        dx1_ref[...] = dx1
        mixv = mix_ref[...]
        rm = _rstd(mixv)
        mhat = mixv * rm
        dnm = dx1 * gate1
        dmix_ref[...] = _rms_bwd(dnm * n1post_ref[...], mhat, rm).astype(BF16)
        redb_ref[0:1, :] += _colsum(dh2)
        redb_ref[1:2, :] += _colsum(dh2 * n2)
        redb_ref[2:3, :] += _colsum(dn2 * xhat)
        redb_ref[3:4, :] += _colsum(dx1 * (mhat * n1post_ref[...]))
        redb_ref[4:5, :] += _colsum(dnm * mhat)

    tile = lambda w: pl.BlockSpec((tt, w), lambda i: (i, 0))
    return pl.pallas_call(
        body, name="mlp_late_bwd", grid=(nt,),
        out_shape=(jax.ShapeDtypeStruct((t_len, late_cols), BF16), jax.ShapeDtypeStruct((t_len, D), BF16),
                   jax.ShapeDtypeStruct((t_len, D_FF), BF16), jax.ShapeDtypeStruct((t_len, D), BF16),
                   jax.ShapeDtypeStruct((t_len, D), F32), jax.ShapeDtypeStruct((8, D), F32),
                   jax.ShapeDtypeStruct((8, D), F32)),
        in_specs=[tile(FC_EARLY * FF_BLK), tile(D), tile(D), tile(D), tile(D),
                  tile(D), _full((8, D)), _full((1, D)), _full((1, D)), _full((1, D)),
                  _resident((FC_EARLY, D, FF_BLK)), _resident((FC_EARLY, FF_BLK, D)),
                  _resident((n_late, D, FF_BLK)), _resident((n_late, FF_BLK, D))],
        out_specs=(tile(late_cols), tile(D), tile(D_FF), tile(D), tile(D), _full((8, D)), _full((8, D))),
        scratch_shapes=[pltpu.VMEM((tt, D), F32)],
        compiler_params=pltpu.CompilerParams(dimension_semantics=("arbitrary",), vmem_limit_bytes=VMEM_LIMIT),
    )(r_early, x1, h2, f_early, tgt, mix, mod, n2pre, n2post, n1post, w1_early, w2_early, w1_late, w2_late)


def _mlp_wgrad(tt, r_early, r_late, da, df, h2):
    t_len = df.shape[0]
    nt = t_len // tt
    odd_steps = [j for j, rel in enumerate(WGRAD_ORDER) if rel % 2]

    def relation(j):
        rel = jnp.int32(WGRAD_ORDER[-1])
        for step in range(N_DEV - 2, -1, -1):
            rel = jnp.where(j == step, WGRAD_ORDER[step], rel)
        return rel

    def body(re_ref, rl_ref, da_ref, df_ref, h2_ref, own1_ref, own2_ref, out1_ref, out2_ref, diag1_ref, diag2_ref,
             acc1, acc2, snd1, snd2, sib1, sib2, dsnd1, dsnd2, send_sems, recv_sems):
        j, t = pl.program_id(0), pl.program_id(1)
        rows = pl.ds(pl.multiple_of(t * tt, tt), tt)
        x, y, c = _place()
        accs, snds, sibs = (acc1, acc2), (snd1, snd2), (sib1, sib2)
        dsnds, diags = (dsnd1, dsnd2), (diag1_ref, diag2_ref)

        def to_sibling(a, jj, buf=0):
            return pltpu.make_async_remote_copy(
                src_ref=snds[a].at[buf], dst_ref=sibs[a].at[jj],
                send_sem=send_sems.at[4 * a + jj], recv_sem=recv_sems.at[4 * a + jj],
                device_id=(x, y, 1 - c), device_id_type=MESH)

        def to_diagonal(a):
            return pltpu.make_async_remote_copy(
                src_ref=dsnds[a], dst_ref=diags[a], send_sem=send_sems.at[8 + a], recv_sem=recv_sems.at[8 + a],
                device_id=_peer(x, y, c, 6), device_id_type=MESH)

        @pl.when(t == 0)
        def _():
            acc2[...] = jnp.zeros_like(acc2)
            acc1[...] = jnp.zeros_like(acc1)

        @pl.when(relation(j) < FC_EARLY)
        def _():
            acc2[...] += _dot_tn(re_ref[...], df_ref[rows, :])

        @pl.when(relation(j) >= FC_EARLY)
        def _():
            acc2[...] += _dot_tn(rl_ref[...], df_ref[rows, :])

        acc1[...] += _dot_tn(h2_ref[rows, :], da_ref[...])

        for step, rel in enumerate(WGRAD_ORDER):
            jj = rel // 2

            @pl.when((t == nt - 1) & (j == step))
            def _():
                for a, (own_ref, out_ref) in enumerate(((own1_ref, out1_ref), (own2_ref, out2_ref))):
                    if rel % 2:
                        q = odd_steps.index(step)
                        if q >= 2:
                            to_sibling(a, WGRAD_ORDER[odd_steps[q - 2]] // 2).wait_send()
                        snds[a][q % 2] = accs[a][...].astype(BF16)
                        to_sibling(a, jj, q % 2).start()
                        continue
                    to_sibling(a, jj).wait_recv()
                    chip_sum = accs[a][...] + sibs[a][jj].astype(F32)
                    if rel == 6:
                        dsnds[a][...] = chip_sum.astype(BF16)
                        to_diagonal(a).start()
                    elif rel == 0:
                        own_ref[...] = chip_sum
                    else:
                        out_ref[0] = chip_sum.astype(BF16)
                    if step == N_DEV - 1:
                        for q in (2, 3):
                            to_sibling(a, WGRAD_ORDER[odd_steps[q]] // 2).wait_send()
                        to_diagonal(a).wait_recv()
                        to_diagonal(a).wait_send()

    assert WGRAD_ORDER[-1] == 0 and WGRAD_ORDER[-3:-1] == (2, 4)
    blk = pl.BlockSpec((tt, FF_BLK), lambda j, t: (t, relation(j)))
    early = lambda j, t: (jnp.where(relation(j) < FC_EARLY, t, 0), jnp.where(relation(j) < FC_EARLY, relation(j), 0))
    late = lambda j, t: (jnp.where(relation(j) < FC_EARLY, 0, t), jnp.maximum(relation(j) - FC_EARLY, 0))
    chip = lambda j, t: (jnp.clip(j - 5, 0, 1), 0, 0)
    hbm = pl.BlockSpec(memory_space=pl.ANY)
    return pl.pallas_call(
        body, name="mlp_wgrad", grid=(N_DEV, nt),
        out_shape=(jax.ShapeDtypeStruct((D, FF_BLK), F32), jax.ShapeDtypeStruct((FF_BLK, D), F32),
                   jax.ShapeDtypeStruct((2, D, FF_BLK), BF16), jax.ShapeDtypeStruct((2, FF_BLK, D), BF16),
                   jax.ShapeDtypeStruct((D, FF_BLK), BF16), jax.ShapeDtypeStruct((FF_BLK, D), BF16)),
        in_specs=[pl.BlockSpec((tt, FF_BLK), early), pl.BlockSpec((tt, FF_BLK), late), blk,
                  _resident((t_len, D)), _resident((t_len, D))],
        out_specs=(_full((D, FF_BLK)), _full((FF_BLK, D)),
                   pl.BlockSpec((1, D, FF_BLK), chip), pl.BlockSpec((1, FF_BLK, D), chip), hbm, hbm),
        scratch_shapes=[pltpu.VMEM((D, FF_BLK), F32), pltpu.VMEM((FF_BLK, D), F32),
                        pltpu.VMEM((2, D, FF_BLK), BF16), pltpu.VMEM((2, FF_BLK, D), BF16),
                        pltpu.VMEM((4, D, FF_BLK), BF16), pltpu.VMEM((4, FF_BLK, D), BF16),
                        pltpu.VMEM((D, FF_BLK), BF16), pltpu.VMEM((FF_BLK, D), BF16),
                        pltpu.SemaphoreType.DMA((10,)), pltpu.SemaphoreType.DMA((10,))],
        compiler_params=pltpu.CompilerParams(dimension_semantics=("arbitrary", "arbitrary"),
                                             vmem_limit_bytes=VMEM_LIMIT),
    )(r_early, r_late, da, df, h2)


def _acc_rows(ref, row0, k, val):
    half = CHUNK // 2
    ref[row0:row0 + half, k * GROUP:(k + 1) * GROUP] += val[:half, :]
    ref[row0:row0 + half, D_A + k * GROUP:D_A + (k + 1) * GROUP] += val[half:, :]


def _attn_bwd(tt, dmix, dx1, x, z, cat, mod, n1pre, w_in_t, w_out, w_sp, bs_rows, ln_g, ln_b, w_pool, b_pool,
              pool_scale, red_fwd, red_bwd, chip_sums):
    t_len = x.shape[0]
    nt = t_len // tt
    hb = tt // HALO
    n_sums = len(chip_sums)

    def body(dmix_ref, dx1_ref, x_ref, z_ref, zprev_ref, cat_ref, mod_ref, n1pre_ref, win_ref, wout_ref, wsp_ref,
             bs_ref, lng_ref, lnb_ref, wp_ref, bp_ref, ps_ref, redf_ref, redb_ref, *rest):
        sum_out = rest[:n_sums]
        gx_ref, gwin_ref, gwout_ref, small_ref = rest[n_sums:n_sums + 4]
        sum_in = rest[n_sums + 4:2 * n_sums + 4]
        carry, acc_in, acc_out, dz_scr, bs_acc, send_sems, recv_sems = rest[2 * n_sums + 4:]
        s = pl.program_id(0)
        i = nt - 1 - s
        px, py, pc = _place()

        def chip_copy(a, r):
            return pltpu.make_async_remote_copy(
                src_ref=sum_out[a].at[r], dst_ref=sum_in[a].at[r],
                send_sem=send_sems.at[2 * a + r], recv_sem=recv_sems.at[2 * a + r],
                device_id=_peer(px, py, pc, 2 * (r + 1)), device_id_type=MESH)

        @pl.when(s == 0)
        def _():
            for a in range(n_sums):
                for r in range(2):
                    chip_copy(a, r).start()
            carry[...] = jnp.zeros_like(carry)
            acc_in[...] = jnp.zeros_like(acc_in)
            acc_out[...] = jnp.zeros_like(acc_out)
            bs_acc[...] = jnp.zeros_like(bs_acc)
            small_ref[...] = jnp.zeros_like(small_ref)
            small_ref[ROW_DMOD + 2:ROW_DMOD + 3, :] = redb_ref[3:4, :]
            small_ref[ROW_DMOD + 3:ROW_DMOD + 5, :] = redb_ref[0:2, :]
            small_ref[ROW_DMOD + 5:ROW_DMOD + 6, :] = redf_ref[0:1, :]
            small_ref[ROW_N1POST:ROW_N1POST + 1, :] = redb_ref[4:5, :]
            small_ref[ROW_N2PRE:ROW_N2PRE + 1, :] = redb_ref[2:3, :]
            small_ref[ROW_N2POST:ROW_N2POST + 1, :] = redf_ref[1:2, :]
            small_ref[ROW_LOSS:ROW_LOSS + 1, :] = redf_ref[2:3, :]

        dmixv = dmix_ref[...]
        dcat = _dot_nt(dmixv, wout_ref[...])
        acc_out[...] += _dot_tn(cat_ref[...], dmixv)

        z = z_ref[...]
        t_g, ga = _gelu_parts(z[:, :2 * D_A])
        u, vr = ga[:, :D_A], ga[:, D_A:]
        dv0 = vr - jnp.mean(vr, axis=-1, keepdims=True)
        rv = lax.rsqrt(jnp.mean(dv0 * dv0, axis=-1, keepdims=True) + EPS)
        vhat = dv0 * rv
        vb = (vhat * lng_ref[...] + lnb_ref[...]).astype(BF16)
        mask = _tril_mask()
        wc = [(wsp_ref[h] * mask).astype(BF16) for h in range(N_HEADS)]

        dya = dcat[:, :D_A]
        for h in range(N_HEADS):
            cols = slice(h * GROUP, (h + 1) * GROUP)
            bs_sum = jnp.zeros((CHUNK, GROUP), F32)
            ws_sum = jnp.zeros((CHUNK, CHUNK), F32)
            for ch in range(tt // CHUNK):
                rows = slice(ch * CHUNK, (ch + 1) * CHUNK)
                v_ch = vb[rows, cols]
                mixed = _dot(wc[h], v_ch) + bs_ref[:, cols]
                dy_ch = dya[rows, cols]
                dz_scr[rows, cols] = dy_ch * mixed
                dmixed = dy_ch * u[rows, cols]
                dmb = dmixed.astype(BF16)
                dz_scr[rows, D_A + h * GROUP:D_A + (h + 1) * GROUP] = _dot_tn(wc[h], dmb)
                bs_sum = bs_sum + dmixed
                ws_sum = ws_sum + _dot_nt(dmb, v_ch)
            _acc_rows(bs_acc, 0, h, bs_sum)
            _acc_rows(small_ref, ROW_WS, h, ws_sum)

        dvl = dz_scr[:, D_A:2 * D_A]
        dvhat = dvl * lng_ref[...]
        dvr = rv * (dvhat - jnp.mean(dvhat, axis=-1, keepdims=True)
                    - vhat * jnp.mean(dvhat * vhat, axis=-1, keepdims=True))
        small_ref[ROW_LN:ROW_LN + 1, 0:D_A] += _colsum(dvl * vhat)
        small_ref[ROW_LN:ROW_LN + 1, D_A:D] += _colsum(dvl)
        dga = jnp.concatenate([dz_scr[:, :D_A], dvr], axis=1)
        dza = dga * _gelu_grad(z[:, :2 * D_A], t_g)

        zb = z[:, 2 * D_A:]
        halo_prev = jnp.where(i == 0, 0.0, zprev_ref[...])
        diff = _pool_diff(zb, halo_prev, i * tt)
        dyb = dcat[:, D_A:]
        inv = _inv_counts(i * tt, tt)
        scaled, ddiffs = [], []
        for g in range(len(WINDOWS)):
            cols = slice(g * GROUP, (g + 1) * GROUP)
            db = diff[g].astype(BF16)
            wpg = wp_ref[g].astype(BF16)
            pre = _dot(db, wpg) + bp_ref[:, cols]
            small_ref[ROW_POOL:ROW_POOL + 1, cols] += _colsum(dyb[:, cols] * pre)
            dpre = dyb[:, cols] * ps_ref[:, cols]
            small_ref[ROW_POOL:ROW_POOL + 1, D_B + g * GROUP:D_B + (g + 1) * GROUP] += _colsum(dpre)
            dpb = dpre.astype(BF16)
            _acc_rows(small_ref, ROW_WP, g, _dot_tn(db, dpb))
            ddiff = _dot_nt(dpb, wpg)
            ddiffs.append(ddiff)
            scaled.append(ddiff * inv[g])
        scaled_all = jnp.concatenate(scaled, axis=1)
        ext = jnp.concatenate([scaled_all, carry[...]], axis=0)
        n_ext = tt + HALO
        s2 = ext + pltpu.roll(ext, n_ext - 1, 0)
        t4 = s2[:, GROUP:]
        s4 = t4 + pltpu.roll(t4, n_ext - 2, 0)
        t8 = s4[:, GROUP:]
        s8 = t8 + pltpu.roll(t8, n_ext - 4, 0)
        t16 = s8[:, GROUP:]
        s16 = t16 + pltpu.roll(t16, n_ext - 8, 0)
        back = [s2[:, :GROUP], s4[:, :GROUP], s8[:, :GROUP], s16]
        carry[...] = scaled_all[:HALO, :]
        dzb = jnp.concatenate([back[g][:tt, :] - ddiffs[g] for g in range(len(WINDOWS))], axis=1)

        dzv = jnp.concatenate([dza, dzb], axis=1).astype(BF16)
        dh1 = _dot(dzv, win_ref[...])
        xv = x_ref[...]
        r1 = _rstd(xv)
        xhat = xv * r1
        shift1, scale1 = mod_ref[0:1, :], mod_ref[1:2, :]
        n1 = xhat * n1pre_ref[...]
        h1 = (n1 * (1.0 + scale1) + shift1).astype(BF16)
        acc_in[...] += _dot_tn(dzv, h1)
        dn1 = dh1 * (1.0 + scale1)
        gx_ref[...] = dx1_ref[...] + _rms_bwd(dn1 * n1pre_ref[...], xhat, r1)
        small_ref[ROW_DMOD:ROW_DMOD + 1, :] += _colsum(dh1)
        small_ref[ROW_DMOD + 1:ROW_DMOD + 2, :] += _colsum(dh1 * n1)
        small_ref[ROW_N1PRE:ROW_N1PRE + 1, :] += _colsum(dn1 * xhat)

        @pl.when(s == nt - 1)
        def _():
            gwin_ref[...] = acc_in[...].astype(BF16)
            gwout_ref[...] = acc_out[...].astype(BF16)
            bs = _unfold(bs_acc[...])
            for h in range(N_HEADS):
                small_ref[ROW_BS + h:ROW_BS + h + 1, 0:GROUP] = jnp.sum(
                    bs[:, h * GROUP:(h + 1) * GROUP].T, axis=0, keepdims=True)
            for a in range(n_sums):
                for r in range(2):
                    chip_copy(a, r).wait_recv()
                    chip_copy(a, r).wait_send()

    rev = lambda w: pl.BlockSpec((tt, w), lambda s: (nt - 1 - s, 0))
    zprev = pl.BlockSpec((HALO, D_B), lambda s: (jnp.maximum((nt - 1 - s) * hb - 1, 0), 2))
    hbm = pl.BlockSpec(memory_space=pl.ANY)
    outs = pl.pallas_call(
        body, name="attn_bwd", grid=(nt,),
        out_shape=tuple([jax.ShapeDtypeStruct((t_len, D), F32), jax.ShapeDtypeStruct((D_Z, D), BF16),
                         jax.ShapeDtypeStruct((D, D), BF16), jax.ShapeDtypeStruct((SMALL_ROWS, D), F32)]
                        + [jax.ShapeDtypeStruct(cs.shape, cs.dtype) for cs in chip_sums]),
        in_specs=[rev(D), rev(D), rev(D), rev(D_Z), zprev, rev(D), _full((8, D)), _full((1, D)),
                  _resident((D_Z, D)), _resident((D, D)), _full((N_HEADS, CHUNK, CHUNK)), _full((CHUNK, D_A)),
                  _full((1, D_A)), _full((1, D_A)), _full((len(WINDOWS), GROUP, GROUP)), _full((1, D_B)),
                  _full((1, D_B)), _full((8, D)), _full((8, D))] + [_resident(cs.shape) for cs in chip_sums],
        out_specs=tuple([rev(D), _resident((D_Z, D)), _resident((D, D)), _full((SMALL_ROWS, D))] + [hbm] * n_sums),
        scratch_shapes=[pltpu.VMEM((HALO, D_B), F32), pltpu.VMEM((D_Z, D), F32), pltpu.VMEM((D, D), F32),
                        pltpu.VMEM((tt, 2 * D_A), F32), pltpu.VMEM((CHUNK // 2, D), F32),
                        pltpu.SemaphoreType.DMA((2 * n_sums,)), pltpu.SemaphoreType.DMA((2 * n_sums,))],
        compiler_params=pltpu.CompilerParams(dimension_semantics=("arbitrary",), vmem_limit_bytes=VMEM_LIMIT),
    )(dmix, dx1, x, z, z, cat, mod, n1pre, w_in_t, w_out, w_sp, bs_rows, ln_g, ln_b, w_pool, b_pool, pool_scale,
      red_fwd, red_bwd, *chip_sums)
    return outs[:4], outs[4:]


def _adam(w, g, m, v):
    m2 = ADAM_B1 * m + (1.0 - ADAM_B1) * g
    v2 = ADAM_B2 * v + (1.0 - ADAM_B2) * (g * g)
    m_hat = m2 / (1.0 - ADAM_B1 ** ADAM_STEP)
    v_hat = v2 / (1.0 - ADAM_B2 ** ADAM_STEP)
    delta = -ADAM_LR * (m_hat / (jnp.sqrt(v_hat) + ADAM_EPS) + ADAM_WD * w)
    return delta, m2, v2


def _adamw_shard(name, rb, w, g, m, v):
    rows, cols = w.shape

    def body(w_ref, g_ref, m_ref, v_ref, d_ref, m2_ref, v2_ref):
        d_ref[...], m2_ref[...], v2_ref[...] = _adam(w_ref[...], g_ref[...], m_ref[...], v_ref[...])

    blk = pl.BlockSpec((rb, cols), lambda i: (i, 0))
    shp = jax.ShapeDtypeStruct((rows, cols), F32)
    return pl.pallas_call(
        body, name=name, grid=(rows // rb,), out_shape=(shp, shp, shp),
        in_specs=[blk] * 4, out_specs=(blk, blk, blk),
        compiler_params=pltpu.CompilerParams(dimension_semantics=("arbitrary",)),
    )(w, g, m, v)


def _adamw_ada(rb, w, sc, dmod_cols, m, v):
    rows, cols = w.shape

    def body(w_ref, sc_ref, dm_ref, m_ref, v_ref, g_ref, d_ref, m2_ref, v2_ref):
        g = _dot_tn(sc_ref[...].astype(BF16), dm_ref[...].astype(BF16))
        g_ref[...] = g
        d_ref[...], m2_ref[...], v2_ref[...] = _adam(w_ref[...], g, m_ref[...], v_ref[...])

    blk = pl.BlockSpec((rb, cols), lambda i: (i, 0))
    shp = jax.ShapeDtypeStruct((rows, cols), F32)
    return pl.pallas_call(
        body, name="adamw_ada", grid=(rows // rb,), out_shape=(shp, shp, shp, shp),
        in_specs=[blk, pl.BlockSpec((N_DEV, rb), lambda i: (0, i)), _full((N_DEV, cols)), blk, blk],
        out_specs=(blk, blk, blk, blk),
        compiler_params=pltpu.CompilerParams(dimension_semantics=("arbitrary",)),
    )(w, sc, dmod_cols, m, v)


def _unfold(acc_rows):
    return jnp.concatenate([acc_rows[:, :D_A], acc_rows[:, D_A:]], axis=0)


def _adamw_small(total, params):
    n = len(params)
    flat = [a for p in params for a in p]

    def body(*refs):
        s_ref = refs[0]
        p_refs = refs[1:1 + 3 * n]
        loss_ref = refs[1 + 3 * n]
        o_refs = refs[2 + 3 * n:]
        d_b_ada = s_ref[0:6, :]
        for b in range(1, N_DEV):
            d_b_ada = d_b_ada + s_ref[8 * b:8 * b + 6, :]
        tot = s_ref[PACK_SHIFT:PACK_ROWS, :]
        loss_ref[...] = jnp.broadcast_to(tot[ROW_LOSS:ROW_LOSS + 1, 0:GROUP], (8, GROUP))
        mask = _tril_mask()
        ws = _unfold(tot[ROW_WS:ROW_WS + 64, :])
        wp = _unfold(tot[ROW_WP:ROW_WP + 64, :])
        grads = [
            d_b_ada,
            tot[ROW_N1PRE:ROW_N1PRE + 1, :], tot[ROW_N1POST:ROW_N1POST + 1, :],
            tot[ROW_N2PRE:ROW_N2PRE + 1, :], tot[ROW_N2POST:ROW_N2POST + 1, :],
            tot[ROW_LN:ROW_LN + 1, :D_A], tot[ROW_LN:ROW_LN + 1, D_A:],
            tot[ROW_POOL:ROW_POOL + 1, :D_B], tot[ROW_POOL:ROW_POOL + 1, D_B:],
            tot[ROW_BS:ROW_BS + N_HEADS, 0:GROUP],
            jnp.stack([ws[:, h * GROUP:(h + 1) * GROUP] * mask for h in range(N_HEADS)]),
            jnp.stack([wp[:, g * GROUP:(g + 1) * GROUP] for g in range(len(WINDOWS))]),
        ]
        for k in range(n):
            w_ref, m_ref, v_ref = p_refs[3 * k:3 * k + 3]
            g = grads[k]
            o_refs[4 * k][...] = g
            o_refs[4 * k + 1][...], o_refs[4 * k + 2][...], o_refs[4 * k + 3][...] = _adam(
                w_ref[...], g, m_ref[...], v_ref[...])

    vm = pl.BlockSpec(memory_space=pltpu.VMEM)
    out_shape = [jax.ShapeDtypeStruct((8, GROUP), F32)]
    for w, _, _ in params:
        out_shape += [jax.ShapeDtypeStruct(w.shape, F32)] * 4
    return pl.pallas_call(
        body, name="adamw_small", out_shape=tuple(out_shape),
        in_specs=[vm] * (1 + 3 * n), out_specs=tuple([vm] * len(out_shape)),
    )(total, *flat)


TT_ATTN_FWD = 512
TT_MLP_FWD = 512
TT_MLP = 256
TT_WGRAD = 2048
TT_ATTN_BWD = 512


def kernel(x, c, w_ada, b_ada, norm1_pre, norm1_post, w_in, w_spatial, b_spatial, ln_v_gain, ln_v_bias, w_pool, b_pool, pool_scale, w_out, norm2_pre, norm2_post, w_fc1, w_fc2, loss_target, m_w_ada, m_b_ada, m_norm1_pre, m_norm1_post, m_w_in, m_w_spatial, m_b_spatial, m_ln_v_gain, m_ln_v_bias, m_w_pool, m_b_pool, m_pool_scale, m_w_out, m_norm2_pre, m_norm2_post, m_w_fc1, m_w_fc2, v_w_ada, v_b_ada, v_norm1_pre, v_norm1_post, v_w_in, v_w_spatial, v_b_spatial, v_ln_v_gain, v_ln_v_bias, v_w_pool, v_b_pool, v_pool_scale, v_w_out, v_norm2_pre, v_norm2_post, v_w_fc1, v_w_fc2):
    t_len = x.shape[1]
    me = 4 * lax.axis_index("x") + 2 * lax.axis_index("y") + lax.axis_index("c")
    ada_cols = w_ada.shape[1]
    tt = lambda want: min(want, t_len)

    x2 = x.reshape(t_len, D)
    tgt = loss_target.reshape(t_len, D)
    row = lambda a: a.reshape(1, -1)

    b_my = lax.dynamic_slice_in_dim(b_ada, me * ada_cols, ada_cols).reshape(1, ada_cols)
    modp, sc, (g_in, g_out), fc_shards = _fwd_comm(jnp.broadcast_to(c, (8, D)), w_ada, b_my,
                                                   [w_in.T, w_out], [w_fc1, w_fc2])
    mod = jnp.concatenate([modp.reshape(6, D), jnp.zeros((2, D), F32)], axis=0)
    w_in_t = g_in.reshape(D_Z, D)
    w_out_all = g_out.reshape(D, D)

    bs_rows = jnp.repeat(b_spatial.T, GROUP, axis=1)
    attn_consts = (w_spatial, bs_rows, row(ln_v_gain), row(ln_v_bias), w_pool, row(b_pool), row(pool_scale))

    (z, cat, mix, x1), (w1_early, w2_early) = _attn_fwd(
        tt(TT_ATTN_FWD), x2, mod, row(norm1_pre), row(norm1_post), w_in_t, w_out_all, *attn_consts, fc_shards)
    (r_early, h2, f_early), (w1_late, w2_late) = _mlp_fwd_early(
        tt(TT_MLP_FWD), x1, mod, row(norm2_pre), w1_early, w2_early)
    r_late, df, da, dmix, dx1, red_fwd, red_bwd = _mlp_late_bwd(
        tt(TT_MLP), r_early, x1, h2, f_early, tgt, mix, mod, row(norm2_pre), row(norm2_post), row(norm1_post),
        w1_early, w2_early, w1_late, w2_late)
    own_w1, own_w2, sums_w1, sums_w2, diag_w1, diag_w2 = _mlp_wgrad(tt(TT_WGRAD), r_early, r_late, da, df, h2)
    (grad_x, p_in, p_out, small), (arr_w1, arr_w2) = _attn_bwd(
        tt(TT_ATTN_BWD), dmix, dx1, x2, z, cat, mod, row(norm1_pre), w_in_t, w_out_all, *attn_consts,
        red_fwd, red_bwd, [sums_w1, sums_w2])
    (grad_in_t, grad_out, total), ((grad_w1, d_w1, m_w1, v_w1), (grad_w2, d_w2, m_w2, v_w2)) = _tail_comm(
        [p_in.reshape(N_DEV, D_Z // N_DEV, D), p_out.reshape(N_DEV, D // N_DEV, D)], small, 64,
        [(w_fc1, own_w1, arr_w1, diag_w1, m_w_fc1, v_w_fc1), (w_fc2, own_w2, arr_w2, diag_w2, m_w_fc2, v_w_fc2)])

    d_out, m_out, v_out = _adamw_shard("adamw_out", 128, w_out, grad_out, m_w_out, v_w_out)
    d_in_t, m_in_t, v_in_t = _adamw_shard("adamw_in", D_Z // N_DEV, w_in.T, grad_in_t, m_w_in.T, v_w_in.T)
    dmod_all = total[0:TABLE_ROWS, :].reshape(N_DEV, 8, D)[:, :6, :].reshape(N_DEV, 6 * D)
    dmod_cols = lax.dynamic_slice_in_dim(dmod_all, me * ada_cols, ada_cols, axis=1)
    grad_ada, d_ada, m_ada, v_ada = _adamw_ada(256, w_ada, sc, dmod_cols, m_w_ada, v_w_ada)

    six = lambda a: a.reshape(6, D)
    small_params = [
        (six(b_ada), six(m_b_ada), six(v_b_ada)),
        (row(norm1_pre), row(m_norm1_pre), row(v_norm1_pre)),
        (row(norm1_post), row(m_norm1_post), row(v_norm1_post)),
        (row(norm2_pre), row(m_norm2_pre), row(v_norm2_pre)),
        (row(norm2_post), row(m_norm2_post), row(v_norm2_post)),
        (row(ln_v_gain), row(m_ln_v_gain), row(v_ln_v_gain)),
        (row(ln_v_bias), row(m_ln_v_bias), row(v_ln_v_bias)),
        (row(pool_scale), row(m_pool_scale), row(v_pool_scale)),
        (row(b_pool), row(m_b_pool), row(v_b_pool)),
        (b_spatial, m_b_spatial, v_b_spatial),
        (w_spatial, m_w_spatial, v_w_spatial),
        (w_pool, m_w_pool, v_w_pool),
    ]
    outs = _adamw_small(total, small_params)
    loss = outs[0][0, 0]
    names = ["b_ada", "norm1_pre", "norm1_post", "norm2_pre", "norm2_post", "ln_v_gain", "ln_v_bias", "pool_scale",
             "b_pool", "b_spatial", "w_spatial", "w_pool"]
    shapes = dict(b_ada=b_ada.shape, norm1_pre=norm1_pre.shape, norm1_post=norm1_post.shape,
                  norm2_pre=norm2_pre.shape, norm2_post=norm2_post.shape, ln_v_gain=ln_v_gain.shape,
                  ln_v_bias=ln_v_bias.shape, pool_scale=pool_scale.shape, b_pool=b_pool.shape,
                  b_spatial=b_spatial.shape, w_spatial=w_spatial.shape, w_pool=w_pool.shape)
    res = {}
    for k, nm in enumerate(names):
        res[nm] = tuple(o.reshape(shapes[nm]) for o in outs[1 + 4 * k:5 + 4 * k])
    res["w_ada"] = (grad_ada, d_ada, m_ada, v_ada)
    res["w_in"] = (grad_in_t.T, d_in_t.T, m_in_t.T, v_in_t.T)
    res["w_out"] = (grad_out, d_out, m_out, v_out)
    res["w_fc1"] = (grad_w1, d_w1, m_w1, v_w1)
    res["w_fc2"] = (grad_w2, d_w2, m_w2, v_w2)

    order = ["w_ada", "b_ada", "norm1_pre", "norm1_post", "w_in", "w_spatial", "b_spatial", "ln_v_gain", "ln_v_bias",
             "w_pool", "b_pool", "pool_scale", "w_out", "norm2_pre", "norm2_post", "w_fc1", "w_fc2"]
    return (loss, grad_x.reshape(x.shape),
            *[res[nm][0] for nm in order], *[res[nm][1] for nm in order],
            *[res[nm][2] for nm in order], *[res[nm][3] for nm in order])
```

```python
import functools

import jax
import jax.numpy as jnp
from jax import lax
from jax.experimental import pallas as pl
from jax.experimental.pallas import tpu as pltpu

F32 = jnp.float32
BF16 = jnp.bfloat16
MESH = pl.DeviceIdType.MESH

N_DEV = 8
D = 1024
D_A = 512
D_B = 512
D_Z = 2 * D_A + D_B
N_HEADS = 4
CHUNK = 128
WINDOWS = (2, 4, 8, 16)
GROUP = 128
D_FF = 4096
FF_BLK = D_FF // N_DEV
HALO = 16
EPS = 1e-6
VMEM_LIMIT = 60 * 1024 * 1024

ADAM_LR = 0.001
ADAM_B1 = 0.9
ADAM_B2 = 0.999
ADAM_EPS = 1e-08
ADAM_WD = 0.01
ADAM_STEP = 10

ROW_DMOD = 0
ROW_N1PRE, ROW_N1POST, ROW_N2PRE, ROW_N2POST = 8, 9, 10, 11
ROW_LN = 12
ROW_POOL = 13
ROW_LOSS = 14
ROW_BS = 16
ROW_WS = 24
ROW_WP = 88
SMALL_ROWS = 152
TABLE_ROWS = 8 * N_DEV
PACK_SHIFT = TABLE_ROWS - 8
PACK_ROWS = SMALL_ROWS + PACK_SHIFT
PACK_HALF = PACK_ROWS // 2


def _dot(a, b):
    return jnp.dot(a, b, preferred_element_type=F32)


def _dot_nt(a, b):
    return lax.dot_general(a, b, (((1,), (1,)), ((), ())), preferred_element_type=F32)


def _dot_tn(a, b):
    return lax.dot_general(a, b, (((0,), (0,)), ((), ())), preferred_element_type=F32)


def _rstd(v):
    return lax.rsqrt(jnp.mean(v * v, axis=-1, keepdims=True) + EPS)


def _rms_bwd(d_hat, hat, rstd):
    return rstd * (d_hat - hat * jnp.mean(d_hat * hat, axis=-1, keepdims=True))


_K0 = 0.7978845608028654
_K1 = 0.044715


def _gelu_parts(v):
    t = jnp.tanh(_K0 * (v + _K1 * (v * v * v)))
    return t, v * (0.5 * (1.0 + t))


def _gelu_grad(v, t):
    return 0.5 * (1.0 + t) + (0.5 * v) * (1.0 - t * t) * (_K0 * (1.0 + (3.0 * _K1) * (v * v)))


def _colsum(v):
    return jnp.sum(v, axis=0, keepdims=True)


def _full(shape):
    n = len(shape)
    return pl.BlockSpec(shape, lambda *_: (0,) * n)


def _resident(shape):
    n = len(shape)
    return pl.BlockSpec(shape, lambda *_: (0,) * n, pipeline_mode=pl.Buffered(1))


def _place():
    x, y, c = lax.axis_index("x"), lax.axis_index("y"), lax.axis_index("c")
    return x, y, c


def _flip(v, bit):
    return 1 - v if bit else v


def _peer(x, y, c, k):
    return (_flip(x, (k >> 2) & 1), _flip(y, (k >> 1) & 1), _flip(c, k & 1))


def _index(p):
    return 4 * p[0] + 2 * p[1] + p[2]


def _two_level_gather_begin(x, y, c, out_refs, send_sems, recv_sems):
    me = (x, y, c)
    sibling = (x, y, 1 - c)
    chips = [(1 - x, y), (x, 1 - y), (1 - x, 1 - y)]

    def copy(a, k, block, to):
        ref = out_refs[a].at[_index(block)]
        return pltpu.make_async_remote_copy(
            src_ref=ref, dst_ref=ref, send_sem=send_sems.at[7 * a + k], recv_sem=recv_sems.at[7 * a + k],
            device_id=to, device_id_type=MESH)

    first = []
    for a in range(len(out_refs)):
        first.append(copy(a, 0, me, sibling))
        first += [copy(a, 1 + j, me, (*chip, c)) for j, chip in enumerate(chips)]
    for cp in first:
        cp.start()
    return copy, first, me, sibling, chips


def _two_level_gather_finish(c, n, begun):
    copy, first, me, sibling, chips = begun
    passed = []
    for a in range(n):
        for j, chip in enumerate(chips):
            copy(a, 1 + j, (*chip, c), me).wait_recv()
            fwd = copy(a, 4 + j, (*chip, c), sibling)
            fwd.start()
            passed.append(fwd)
    for a in range(n):
        copy(a, 0, sibling, me).wait_recv()
        for j, chip in enumerate(chips):
            copy(a, 4 + j, (*chip, 1 - c), me).wait_recv()
    for cp in first + passed:
        cp.wait_send()


def _fwd_comm(c8, w_ada, b_my, gathered, kept):
    ncol = w_ada.shape[1]
    n_g, n_k = len(gathered), len(kept)

    def body(c_ref, w_ref, b_ref, *rest):
        g_in, k_in = rest[:n_g], rest[n_g:n_g + n_k]
        modp_ref, sc_ref = rest[n_g + n_k:n_g + n_k + 2]
        g_out = rest[n_g + n_k + 2:2 * n_g + n_k + 2]
        k_out = rest[2 * n_g + n_k + 2:2 * n_g + 2 * n_k + 2]
        cg, mg, part, send_sems, recv_sems, g_send, g_recv = rest[2 * n_g + 2 * n_k + 2:]
        x, y, c = _place()
        me = _index((x, y, c))
        for a in range(n_g):
            g_out[a][me] = g_in[a][...].astype(BF16)
        begun = _two_level_gather_begin(x, y, c, g_out, g_send, g_recv)
        for a in range(n_k):
            k_out[a][...] = k_in[a][...].astype(BF16)

        def c_copy(k):
            p = _peer(x, y, c, k)
            return pltpu.make_async_remote_copy(
                src_ref=c_ref, dst_ref=cg.at[me], send_sem=send_sems.at[k - 1], recv_sem=recv_sems.at[k - 1],
                device_id=p, device_id_type=MESH)

        def c_arrival(k):
            p = _peer(x, y, c, k)
            return pltpu.make_async_remote_copy(
                src_ref=c_ref, dst_ref=cg.at[_index(p)], send_sem=send_sems.at[k - 1], recv_sem=recv_sems.at[k - 1],
                device_id=p, device_id_type=MESH)

        def m_copy(k):
            p = _peer(x, y, c, k)
            return pltpu.make_async_remote_copy(
                src_ref=part, dst_ref=mg.at[me], send_sem=send_sems.at[6 + k], recv_sem=recv_sems.at[6 + k],
                device_id=p, device_id_type=MESH)

        def m_arrival(k):
            p = _peer(x, y, c, k)
            return pltpu.make_async_remote_copy(
                src_ref=part, dst_ref=mg.at[_index(p)], send_sem=send_sems.at[6 + k], recv_sem=recv_sems.at[6 + k],
                device_id=p, device_id_type=MESH)

        for k in range(1, N_DEV):
            c_copy(k).start()
        cg[me] = c_ref[...]
        for k in range(1, N_DEV):
            c_arrival(k).wait_recv()
        c_all = jnp.concatenate([cg[j, 0:1, :] for j in range(N_DEV)], axis=0)
        sc = c_all * jax.nn.sigmoid(c_all)
        sc_ref[...] = sc
        part[...] = _dot(sc.astype(BF16), w_ref[...].astype(BF16)) + b_ref[...]
        for k in range(1, N_DEV):
            m_copy(k).start()
        mg[me] = part[...]
        for k in range(1, N_DEV):
            m_arrival(k).wait_recv()
        for j in range(N_DEV):
            modp_ref[j:j + 1, :] = mg[j, pl.ds(me, 1), :]
        _two_level_gather_finish(c, n_g, begun)
        for k in range(1, N_DEV):
            c_copy(k).wait_send()
            m_copy(k).wait_send()

    vm = pl.BlockSpec(memory_space=pltpu.VMEM)
    outs = pl.pallas_call(
        body, name="fwd_comm",
        out_shape=tuple([jax.ShapeDtypeStruct((N_DEV, ncol), F32), jax.ShapeDtypeStruct((N_DEV, D), F32)]
                        + [jax.ShapeDtypeStruct((N_DEV,) + s.shape, BF16) for s in gathered]
                        + [jax.ShapeDtypeStruct(s.shape, BF16) for s in kept]),
        in_specs=[vm] * (3 + n_g + n_k), out_specs=tuple([vm] * (2 + n_g + n_k)),
        scratch_shapes=[
            pltpu.VMEM((N_DEV, 8, D), F32),
            pltpu.VMEM((N_DEV, N_DEV, ncol), F32),
            pltpu.VMEM((N_DEV, ncol), F32),
            pltpu.SemaphoreType.DMA((2 * (N_DEV - 1),)),
            pltpu.SemaphoreType.DMA((2 * (N_DEV - 1),)),
            pltpu.SemaphoreType.DMA((7 * n_g,)),
            pltpu.SemaphoreType.DMA((7 * n_g,)),
        ],
        compiler_params=pltpu.CompilerParams(vmem_limit_bytes=VMEM_LIMIT),
    )(c8, w_ada, b_my, *gathered, *kept)
    return outs[0], outs[1], outs[2:2 + n_g], outs[2 + n_g:]


FC_EARLY = 6
WGRAD_ORDER = (7, 6, 1, 3, 5, 2, 4, 0)


class _Copies:
    def __init__(self, entries, send_sems, recv_sems):
        self.place = _place()
        self.entries, self.send_sems, self.recv_sems = entries, send_sems, recv_sems

    def _copy(self, i, arrival=False):
        src, dst, rel = self.entries[i]
        return pltpu.make_async_remote_copy(
            src_ref=dst if arrival else src, dst_ref=dst, send_sem=self.send_sems.at[i],
            recv_sem=self.recv_sems.at[i], device_id=_peer(*self.place, rel), device_id_type=MESH)

    def start(self, *which):
        for i in which:
            self._copy(i).start()

    def wait_recv(self, *which):
        for i in which:
            self._copy(i, arrival=True).wait_recv()

    def wait_send(self, *which):
        for i in which:
            self._copy(i).wait_send()


TAIL_STEPS = 8


def _tail_comm(parts, small, row_chunk, fc):
    n, n_fc = len(parts), len(fc)

    def body(*refs):
        p_refs, small_ref = refs[:n], refs[n]
        fc_in = refs[n + 1:n + 1 + 6 * n_fc]
        outs = refs[n + 1 + 6 * n_fc:]
        g_refs, total_ref = outs[:n], outs[n]
        fc_out = outs[n + 1:n + 1 + 4 * n_fc]
        scr = outs[n + 1 + 4 * n_fc:]
        from_sib = scr[0:n]
        chip_out = scr[n:2 * n]
        chip_in = scr[2 * n:3 * n]
        pack, pack_sib, halves, total_scr = scr[3 * n:3 * n + 4]
        send_a, recv_a, send_b, recv_b, send_s, recv_s = scr[3 * n + 4:]
        step = pl.program_id(0)
        x, y, c = _place()
        me = _index((x, y, c))
        sibling = (x, y, 1 - c)
        my_chip = 2 * x + y
        others = [(1 - x, y), (x, 1 - y), (1 - x, 1 - y)]
        my_half = pl.ds(pl.multiple_of(PACK_HALF * c, 8), PACK_HALF)

        def pack_to_sibling():
            return pltpu.make_async_remote_copy(
                src_ref=pack, dst_ref=pack_sib, send_sem=send_s.at[0], recv_sem=recv_s.at[0],
                device_id=sibling, device_id_type=MESH)

        def half_to_chip(r):
            return pltpu.make_async_remote_copy(
                src_ref=halves.at[my_chip], dst_ref=halves.at[my_chip],
                send_sem=send_s.at[1 + r], recv_sem=recv_s.at[1 + r],
                device_id=(*others[r], c), device_id_type=MESH)

        def half_from_chip(r):
            k = 2 * others[r][0] + others[r][1]
            return pltpu.make_async_remote_copy(
                src_ref=halves.at[k], dst_ref=halves.at[k], send_sem=send_s.at[1 + r], recv_sem=recv_s.at[1 + r],
                device_id=(*others[r], c), device_id_type=MESH)

        def total_to_sibling():
            return pltpu.make_async_remote_copy(
                src_ref=total_scr.at[my_half], dst_ref=total_scr.at[my_half],
                send_sem=send_s.at[4], recv_sem=recv_s.at[4], device_id=sibling, device_id_type=MESH)

        def total_from_sibling():
            sib_half = pl.ds(pl.multiple_of(PACK_HALF * (1 - c), 8), PACK_HALF)
            return pltpu.make_async_remote_copy(
                src_ref=total_scr.at[sib_half], dst_ref=total_scr.at[sib_half],
                send_sem=send_s.at[4], recv_sem=recv_s.at[4], device_id=sibling, device_id_type=MESH)

        def to_sibling(a, k):
            return pltpu.make_async_remote_copy(
                src_ref=p_refs[a].at[2 * k + (1 - c)], dst_ref=from_sib[a].at[k],
                send_sem=send_a.at[a], recv_sem=recv_a.at[a], device_id=sibling, device_id_type=MESH)

        def all_from_sibling(a):
            return pltpu.make_async_remote_copy(
                src_ref=from_sib[a], dst_ref=from_sib[a], send_sem=send_a.at[a], recv_sem=recv_a.at[a],
                device_id=sibling, device_id_type=MESH)

        def to_chip(a, r):
            return pltpu.make_async_remote_copy(
                src_ref=chip_out[a].at[r], dst_ref=chip_in[a].at[r],
                send_sem=send_b.at[3 * a + r], recv_sem=recv_b.at[3 * a + r],
                device_id=(*others[r], c), device_id_type=MESH)

        @pl.when(step == 0)
        def _():
            pack[0:TABLE_ROWS, :] = jnp.zeros((TABLE_ROWS, D), F32)
            pack[pl.ds(pl.multiple_of(8 * me, 8), 8), :] = small_ref[0:8, :]
            pack[TABLE_ROWS:PACK_ROWS, :] = small_ref[8:SMALL_ROWS, :]
            pack_to_sibling().start()
            for a in range(n):
                for k in range(4):
                    to_sibling(a, k).start()

        @pl.when(step == 1)
        def _():
            pack_to_sibling().wait_recv()
            halves[my_chip] = pack[my_half, :] + pack_sib[my_half, :]
            for r in range(3):
                half_to_chip(r).start()
            for a in range(n):
                all_from_sibling(a).wait_recv()
                rows = p_refs[a].shape[1]
                for r in range(3):
                    k = 2 * others[r][0] + others[r][1]
                    for s in range(0, rows, row_chunk):
                        sl = pl.ds(s, row_chunk)
                        chip_out[a][r, sl, :] = (p_refs[a][2 * k + c, sl, :].astype(F32)
                                                 + from_sib[a][k, sl, :].astype(F32)).astype(BF16)
                    to_chip(a, r).start()
                for s in range(0, rows, row_chunk):
                    sl = pl.ds(s, row_chunk)
                    g_refs[a][sl, :] = (p_refs[a][2 * my_chip + c, sl, :].astype(F32)
                                        + from_sib[a][my_chip, sl, :].astype(F32))

        for k in range(n_fc):
            w_ref, own_ref, arr_ref, diag_ref, m_ref, v_ref = fc_in[6 * k:6 * k + 6]
            g = own_ref[...]
            for r in range(2):
                g = g + arr_ref[r].astype(F32)
            g = g + diag_ref[...].astype(F32)
            fc_out[4 * k][...] = g
            fc_out[4 * k + 1][...], fc_out[4 * k + 2][...], fc_out[4 * k + 3][...] = _adam(
                w_ref[...], g, m_ref[...], v_ref[...])

        @pl.when(step == TAIL_STEPS - 1)
        def _():
            for r in range(3):
                half_from_chip(r).wait_recv()
            total_scr[my_half, :] = ((halves[0] + halves[1]) + halves[2]) + halves[3]
            total_to_sibling().start()
            for a in range(n):
                rows = p_refs[a].shape[1]
                for r in range(3):
                    to_chip(a, r).wait_recv()
                    for s in range(0, rows, row_chunk):
                        sl = pl.ds(s, row_chunk)
                        g_refs[a][sl, :] = g_refs[a][sl, :] + chip_in[a][r, sl, :].astype(F32)
            total_from_sibling().wait_recv()
            total_ref[...] = total_scr[...]
            for a in range(n):
                all_from_sibling(a).wait_send()
                for r in range(3):
                    to_chip(a, r).wait_send()
            pack_to_sibling().wait_send()
            for r in range(3):
                half_to_chip(r).wait_send()
            total_to_sibling().wait_send()

    fc_specs_in, fc_specs_out, fc_shapes, fc_args = [], [], [], []
    for w, own, arrived, diagonal, m, v in fc:
        rows, cols = w.shape
        blk = pl.BlockSpec((rows // TAIL_STEPS, cols), lambda i: (i, 0))
        fc_specs_in += [blk, blk, pl.BlockSpec((2, rows // TAIL_STEPS, cols), lambda i: (0, i, 0)), blk, blk, blk]
        fc_specs_out += [blk] * 4
        fc_shapes += [jax.ShapeDtypeStruct((rows, cols), F32)] * 4
        fc_args += [w, own, arrived, diagonal, m, v]
    outs = pl.pallas_call(
        body, name="tail_comm", grid=(TAIL_STEPS,),
        out_shape=tuple([jax.ShapeDtypeStruct(p.shape[1:], F32) for p in parts]
                        + [jax.ShapeDtypeStruct((PACK_ROWS, D), F32)] + fc_shapes),
        in_specs=[_resident(p.shape) for p in parts] + [_resident(small.shape)] + fc_specs_in,
        out_specs=tuple([_full(p.shape[1:]) for p in parts] + [_full((PACK_ROWS, D))] + fc_specs_out),
        scratch_shapes=(
            [pltpu.VMEM((4,) + p.shape[1:], BF16) for p in parts]
            + [pltpu.VMEM((3,) + p.shape[1:], BF16) for p in parts]
            + [pltpu.VMEM((3,) + p.shape[1:], BF16) for p in parts]
            + [pltpu.VMEM((PACK_ROWS, D), F32), pltpu.VMEM((PACK_ROWS, D), F32),
               pltpu.VMEM((4, PACK_HALF, D), F32), pltpu.VMEM((PACK_ROWS, D), F32)]
            + [pltpu.SemaphoreType.DMA((n,)), pltpu.SemaphoreType.DMA((n,)),
               pltpu.SemaphoreType.DMA((3 * n,)), pltpu.SemaphoreType.DMA((3 * n,)),
               pltpu.SemaphoreType.DMA((5,)), pltpu.SemaphoreType.DMA((5,))]),
        compiler_params=pltpu.CompilerParams(dimension_semantics=("arbitrary",), vmem_limit_bytes=VMEM_LIMIT),
    )(*parts, small, *fc_args)
    return outs[:n + 1], [outs[n + 1 + 4 * k:n + 5 + 4 * k] for k in range(n_fc)]


def _tril_mask():
    row = lax.broadcasted_iota(jnp.int32, (CHUNK, CHUNK), 0)
    col = lax.broadcasted_iota(jnp.int32, (CHUNK, CHUNK), 1)
    return (col <= row).astype(F32)


def _window_sums(ext):
    s2 = ext + pltpu.roll(ext, 1, 0)
    t4 = s2[:, GROUP:]
    s4 = t4 + pltpu.roll(t4, 2, 0)
    t8 = s4[:, GROUP:]
    s8 = t8 + pltpu.roll(t8, 4, 0)
    t16 = s8[:, GROUP:]
    s16 = t16 + pltpu.roll(t16, 8, 0)
    return [s2[:, :GROUP], s4[:, :GROUP], s8[:, :GROUP], s16]


def _inv_counts(first_pos, rows):
    pos = first_pos + lax.broadcasted_iota(jnp.int32, (rows, 1), 0)
    return [1.0 / jnp.minimum(pos + 1, w).astype(F32) for w in WINDOWS]


def _pool_diff(zb, halo, first_pos):
    tt = zb.shape[0]
    sums = _window_sums(jnp.concatenate([halo, zb], axis=0))
    inv = _inv_counts(first_pos, tt)
    return [sums[g][HALO:, :] * inv[g] - zb[:, g * GROUP:(g + 1) * GROUP] for g in range(len(WINDOWS))]


def _attn_fwd(tt, x, mod, n1pre, n1post, w_in_t, w_out, w_sp, bs_rows, ln_g, ln_b, w_pool, b_pool, pool_scale,
              fc_shards):
    t_len = x.shape[0]
    nt = t_len // tt

    def body(x_ref, mod_ref, n1pre_ref, n1post_ref, win_ref, wout_ref, wsp_ref, bs_ref, lng_ref, lnb_ref,
             wp_ref, bp_ref, ps_ref, w1_ref, w2_ref, z_ref, cat_ref, mix_ref, x1_ref, e1_ref, e2_ref,
             carry, land1, land2, send_sems, recv_sems, local_sems):
        i = pl.program_id(0)
        copies = _Copies(
            [(w1_ref, e1_ref.at[1], 1), (w2_ref, e2_ref.at[1], 1),
             (w1_ref, land1.at[0], 2), (w2_ref, land2.at[0], 2),
             (w1_ref, land1.at[1], 4), (w2_ref, land2.at[1], 4),
             (land1.at[0], e1_ref.at[3], 1), (land2.at[0], e2_ref.at[3], 1),
             (land1.at[1], e1_ref.at[5], 1), (land2.at[1], e2_ref.at[5], 1)],
            send_sems, recv_sems)
        keep = [pltpu.make_async_copy(w1_ref, e1_ref.at[0], local_sems.at[0]),
                pltpu.make_async_copy(w2_ref, e2_ref.at[0], local_sems.at[1]),
                pltpu.make_async_copy(land1.at[0], e1_ref.at[2], local_sems.at[2]),
                pltpu.make_async_copy(land1.at[1], e1_ref.at[4], local_sems.at[3]),
                pltpu.make_async_copy(land2.at[0], e2_ref.at[2], local_sems.at[4]),
                pltpu.make_async_copy(land2.at[1], e2_ref.at[4], local_sems.at[5])]

        @pl.when(i == 0)
        def _():
            copies.start(2, 4, 3, 5, 0, 1)
            keep[0].start()
            keep[1].start()
            carry[...] = jnp.zeros_like(carry)

        @pl.when(i == nt // 2)
        def _():
            copies.wait_recv(2, 4)
            copies.start(6, 8)
            keep[2].start()
            keep[3].start()

        @pl.when(i == nt - 1)
        def _():
            copies.wait_recv(3, 5)
            copies.start(7, 9)
            keep[4].start()
            keep[5].start()

        xv = x_ref[...]
        shift1, scale1, gate1 = mod_ref[0:1, :], mod_ref[1:2, :], mod_ref[2:3, :]
        h1 = (xv * _rstd(xv) * n1pre_ref[...]) * (1.0 + scale1) + shift1
        z = _dot_nt(h1.astype(BF16), win_ref[...])
        z_ref[...] = z

        _, ga = _gelu_parts(z[:, :2 * D_A])
        u, vr = ga[:, :D_A], ga[:, D_A:]
        dv = vr - jnp.mean(vr, axis=-1, keepdims=True)
        v = (dv * lax.rsqrt(jnp.mean(dv * dv, axis=-1, keepdims=True) + EPS)) * lng_ref[...] + lnb_ref[...]
        vb = v.astype(BF16)
        mask = _tril_mask()
        wc = [(wsp_ref[h] * mask).astype(BF16) for h in range(N_HEADS)]
        for ch in range(tt // CHUNK):
            rows = slice(ch * CHUNK, (ch + 1) * CHUNK)
            for h in range(N_HEADS):
                cols = slice(h * GROUP, (h + 1) * GROUP)
                mixed = _dot(wc[h], vb[rows, cols]) + bs_ref[:, cols]
                cat_ref[rows, cols] = (u[rows, cols] * mixed).astype(BF16)

        zb = z[:, 2 * D_A:]
        diff = _pool_diff(zb, carry[...], i * tt)
        carry[...] = zb[tt - HALO:, :]
        for g in range(len(WINDOWS)):
            cols = slice(g * GROUP, (g + 1) * GROUP)
            pre = _dot(diff[g].astype(BF16), wp_ref[g].astype(BF16)) + bp_ref[:, cols]
            cat_ref[:, D_A + g * GROUP:D_A + (g + 1) * GROUP] = (pre * ps_ref[:, cols]).astype(BF16)

        mix = _dot(cat_ref[...], wout_ref[...])
        mix_ref[...] = mix
        x1_ref[...] = xv + gate1 * (mix * _rstd(mix) * n1post_ref[...])

        @pl.when(i == nt - 1)
        def _():
            copies.wait_recv(0, 1, 6, 7, 8, 9)
            copies.wait_send(*range(10))
            for cp in keep:
                cp.wait()

    tile = lambda w: pl.BlockSpec((tt, w), lambda i: (i, 0))
    hbm = pl.BlockSpec(memory_space=pl.ANY)
    outs = pl.pallas_call(
        body, name="attn_fwd", grid=(nt,),
        out_shape=tuple([jax.ShapeDtypeStruct((t_len, D_Z), F32), jax.ShapeDtypeStruct((t_len, D), BF16),
                         jax.ShapeDtypeStruct((t_len, D), F32), jax.ShapeDtypeStruct((t_len, D), F32)]
                        + [jax.ShapeDtypeStruct((FC_EARLY,) + s.shape, BF16) for s in fc_shards]),
        in_specs=[tile(D), _full((8, D)), _full((1, D)), _full((1, D)), _resident((D_Z, D)), _resident((D, D)),
                  _full((N_HEADS, CHUNK, CHUNK)), _full((CHUNK, D_A)), _full((1, D_A)), _full((1, D_A)),
                  _full((len(WINDOWS), GROUP, GROUP)), _full((1, D_B)), _full((1, D_B)),
                  _resident(fc_shards[0].shape), _resident(fc_shards[1].shape)],
        out_specs=(tile(D_Z), tile(D), tile(D), tile(D), hbm, hbm),
        scratch_shapes=[pltpu.VMEM((HALO, D_B), F32),
                        pltpu.VMEM((2,) + fc_shards[0].shape, BF16), pltpu.VMEM((2,) + fc_shards[1].shape, BF16),
                        pltpu.SemaphoreType.DMA((10,)), pltpu.SemaphoreType.DMA((10,)),
                        pltpu.SemaphoreType.DMA((6,))],
        compiler_params=pltpu.CompilerParams(dimension_semantics=("arbitrary",), vmem_limit_bytes=VMEM_LIMIT),
    )(x, mod, n1pre, n1post, w_in_t, w_out, w_sp, bs_rows, ln_g, ln_b, w_pool, b_pool, pool_scale, *fc_shards)
    return outs[:4], outs[4:]


def _mlp_fwd_early(tt, x1, mod, n2pre, w1_early, w2_early):
    t_len = x1.shape[0]
    nt = t_len // tt
    n_late = N_DEV - FC_EARLY

    def body(x1_ref, mod_ref, n2pre_ref, w1_ref, w2_ref, r_ref, h2_ref, f_ref, l1_ref, l2_ref,
             land1, land2, send_sems, recv_sems, local_sems):
        i = pl.program_id(0)
        copies = _Copies(
            [(w1_ref.at[2], land1, 4), (w2_ref.at[4], land2, 2),
             (land1, l1_ref.at[1], 1), (land2, l2_ref.at[1], 1)],
            send_sems, recv_sems)
        keep = [pltpu.make_async_copy(land1, l1_ref.at[0], local_sems.at[0]),
                pltpu.make_async_copy(land2, l2_ref.at[0], local_sems.at[1])]

        @pl.when(i == 0)
        def _():
            copies.start(0, 1)

        @pl.when(i == nt // 2)
        def _():
            copies.wait_recv(0, 1)
            copies.start(2, 3)
            for cp in keep:
                cp.start()

        x1v = x1_ref[...]
        shift2, scale2 = mod_ref[3:4, :], mod_ref[4:5, :]
        h2 = ((x1v * _rstd(x1v) * n2pre_ref[...]) * (1.0 + scale2) + shift2).astype(BF16)
        h2_ref[...] = h2
        for j in range(FC_EARLY):
            cols = slice(j * FF_BLK, (j + 1) * FF_BLK)
            ra = jnp.maximum(_dot(h2, w1_ref[j]), 0.0)
            r = (ra * ra).astype(BF16)
            r_ref[:, cols] = r
            contrib = _dot(r, w2_ref[j])
            if j == 0:
                f_ref[...] = contrib
            else:
                f_ref[...] += contrib

        @pl.when(i == nt - 1)
        def _():
            copies.wait_recv(2, 3)
            copies.wait_send(0, 1, 2, 3)
            for cp in keep:
                cp.wait()

    tile = lambda w: pl.BlockSpec((tt, w), lambda i: (i, 0))
    hbm = pl.BlockSpec(memory_space=pl.ANY)
    outs = pl.pallas_call(
        body, name="mlp_fwd_early", grid=(nt,),
        out_shape=(jax.ShapeDtypeStruct((t_len, FC_EARLY * FF_BLK), BF16),
                   jax.ShapeDtypeStruct((t_len, D), BF16), jax.ShapeDtypeStruct((t_len, D), F32),
                   jax.ShapeDtypeStruct((n_late,) + w1_early.shape[1:], BF16),
                   jax.ShapeDtypeStruct((n_late,) + w2_early.shape[1:], BF16)),
        in_specs=[tile(D), _full((8, D)), _full((1, D)),
                  _resident((FC_EARLY, D, FF_BLK)), _resident((FC_EARLY, FF_BLK, D))],
        out_specs=(tile(FC_EARLY * FF_BLK), tile(D), tile(D), hbm, hbm),
        scratch_shapes=[pltpu.VMEM(w1_early.shape[1:], BF16), pltpu.VMEM(w2_early.shape[1:], BF16),
                        pltpu.SemaphoreType.DMA((4,)), pltpu.SemaphoreType.DMA((4,)),
                        pltpu.SemaphoreType.DMA((2,))],
        compiler_params=pltpu.CompilerParams(dimension_semantics=("arbitrary",), vmem_limit_bytes=VMEM_LIMIT),
    )(x1, mod, n2pre, w1_early, w2_early)
    return outs[:3], outs[3:]


def _mlp_late_bwd(tt, r_early, x1, h2, f_early, tgt, mix, mod, n2pre, n2post, n1post,
                  w1_early, w2_early, w1_late, w2_late):
    t_len = x1.shape[0]
    nt = t_len // tt
    n_late = N_DEV - FC_EARLY
    late_cols = n_late * FF_BLK
    sub_rows = min(tt, MLP_SUB_ROWS)

    def body(re_ref, x1_ref, h2_ref, fe_ref, tgt_ref, mix_ref, mod_ref, n2pre_ref, n2post_ref,
             n1post_ref, w1e_ref, w2e_ref, w1l_ref, w2l_ref,
             rl_ref, df_ref, da_ref, dmix_ref, dx1_ref, redf_ref, redb_ref, dh2_acc):
        i = pl.program_id(0)

        @pl.when(i == 0)
        def _():
            redf_ref[...] = jnp.zeros_like(redf_ref)
            redb_ref[...] = jnp.zeros_like(redb_ref)

        gate1, scale2, gate2 = mod_ref[2:3, :], mod_ref[4:5, :], mod_ref[5:6, :]
        for sub in range(tt // sub_rows):
            rows = slice(sub * sub_rows, (sub + 1) * sub_rows)
            x1v = x1_ref[rows, :]
            h2 = h2_ref[rows, :]
            f = fe_ref[rows, :]
            for j in range(n_late):
                cols = slice(j * FF_BLK, (j + 1) * FF_BLK)
                ra = jnp.maximum(_dot(h2, w1l_ref[j]), 0.0)
                r = (ra * ra).astype(BF16)
                rl_ref[rows, cols] = r
                f = f + _dot(r, w2l_ref[j])
            rf = _rstd(f)
            fhat = f * rf
            nf = fhat * n2post_ref[...]
            err = (x1v + gate2 * nf) - tgt_ref[rows, :]
            dy = err * (1.0 / D)
            dnf = dy * gate2
            dfv = _rms_bwd(dnf * n2post_ref[...], fhat, rf).astype(BF16)
            df_ref[rows, :] = dfv
            redf_ref[0:1, :] += _colsum(dy * nf)
            redf_ref[1:2, :] += _colsum(dnf * fhat)
            redf_ref[2:3, :] += _colsum(0.5 * jnp.mean(err * err, axis=-1, keepdims=True)) * jnp.ones((1, D), F32)

            for j in range(N_DEV):
                cols = slice(j * FF_BLK, (j + 1) * FF_BLK)
                if j < FC_EARLY:
                    w1, w2, r = w1e_ref[j], w2e_ref[j], re_ref[rows, cols]
                else:
                    jl = j - FC_EARLY
                    w1, w2, r = w1l_ref[jl], w2l_ref[jl], rl_ref[rows, jl * FF_BLK:(jl + 1) * FF_BLK]
                dr = _dot_nt(dfv, w2)
                da = (dr * (2.0 * jnp.sqrt(r.astype(F32)))).astype(BF16)
                da_ref[rows, cols] = da
                contrib = _dot_nt(da, w1)
                if j == 0:
                    dh2_acc[rows, :] = contrib
                else:
                    dh2_acc[rows, :] += contrib
            dh2 = dh2_acc[rows, :]
            r2 = _rstd(x1v)
            xhat = x1v * r2
            n2 = xhat * n2pre_ref[...]
            dn2 = dh2 * (1.0 + scale2)
            dx1 = dy + _rms_bwd(dn2 * n2pre_ref[...], xhat, r2)
            dx1_ref[rows, :] = dx1
            mixv = mix_ref[rows, :]
            rm = _rstd(mixv)
            mhat = mixv * rm
            dnm = dx1 * gate1
            dmix_ref[rows, :] = _rms_bwd(dnm * n1post_ref[...], mhat, rm).astype(BF16)
            redb_ref[0:1, :] += _colsum(dh2)
            redb_ref[1:2, :] += _colsum(dh2 * n2)
            redb_ref[2:3, :] += _colsum(dn2 * xhat)
            redb_ref[3:4, :] += _colsum(dx1 * (mhat * n1post_ref[...]))
            redb_ref[4:5, :] += _colsum(dnm * mhat)

    tile = lambda w: pl.BlockSpec((tt, w), lambda i: (i, 0))
    return pl.pallas_call(
        body, name="mlp_late_bwd", grid=(nt,),
        out_shape=(jax.ShapeDtypeStruct((t_len, late_cols), BF16), jax.ShapeDtypeStruct((t_len, D), BF16),
                   jax.ShapeDtypeStruct((t_len, D_FF), BF16), jax.ShapeDtypeStruct((t_len, D), BF16),
                   jax.ShapeDtypeStruct((t_len, D), F32), jax.ShapeDtypeStruct((8, D), F32),
                   jax.ShapeDtypeStruct((8, D), F32)),
        in_specs=[tile(FC_EARLY * FF_BLK), tile(D), tile(D), tile(D), tile(D),
                  tile(D), _full((8, D)), _full((1, D)), _full((1, D)), _full((1, D)),
                  _resident((FC_EARLY, D, FF_BLK)), _resident((FC_EARLY, FF_BLK, D)),
                  _resident((n_late, D, FF_BLK)), _resident((n_late, FF_BLK, D))],
        out_specs=(tile(late_cols), tile(D), tile(D_FF), tile(D), tile(D), _full((8, D)), _full((8, D))),
        scratch_shapes=[pltpu.VMEM((tt, D), F32)],
        compiler_params=pltpu.CompilerParams(dimension_semantics=("arbitrary",), vmem_limit_bytes=VMEM_LIMIT),
    )(r_early, x1, h2, f_early, tgt, mix, mod, n2pre, n2post, n1post, w1_early, w2_early, w1_late, w2_late)


def _mlp_wgrad(tt, r_early, r_late, da, df, h2):
    t_len = df.shape[0]
    nt = t_len // tt
    odd_steps = [j for j, rel in enumerate(WGRAD_ORDER) if rel % 2]

    def relation(j):
        rel = jnp.int32(WGRAD_ORDER[-1])
        for step in range(N_DEV - 2, -1, -1):
            rel = jnp.where(j == step, WGRAD_ORDER[step], rel)
        return rel

    def body(re_ref, rl_ref, da_ref, df_ref, h2_ref, own1_ref, own2_ref, out1_ref, out2_ref, diag1_ref, diag2_ref,
             acc1, acc2, snd1, snd2, sib1, sib2, dsnd1, dsnd2, send_sems, recv_sems):
        j, t = pl.program_id(0), pl.program_id(1)
        rows = pl.ds(pl.multiple_of(t * tt, tt), tt)
        x, y, c = _place()
        accs, snds, sibs = (acc1, acc2), (snd1, snd2), (sib1, sib2)
        dsnds, diags = (dsnd1, dsnd2), (diag1_ref, diag2_ref)

        def to_sibling(a, jj, buf=0):
            return pltpu.make_async_remote_copy(
                src_ref=snds[a].at[buf], dst_ref=sibs[a].at[jj],
                send_sem=send_sems.at[4 * a + jj], recv_sem=recv_sems.at[4 * a + jj],
                device_id=(x, y, 1 - c), device_id_type=MESH)

        def to_diagonal(a):
            return pltpu.make_async_remote_copy(
                src_ref=dsnds[a], dst_ref=diags[a], send_sem=send_sems.at[8 + a], recv_sem=recv_sems.at[8 + a],
                device_id=_peer(x, y, c, 6), device_id_type=MESH)

        @pl.when(t == 0)
        def _():
            acc2[...] = jnp.zeros_like(acc2)
            acc1[...] = jnp.zeros_like(acc1)

        @pl.when(relation(j) < FC_EARLY)
        def _():
            acc2[...] += _dot_tn(re_ref[...], df_ref[rows, :])

        @pl.when(relation(j) >= FC_EARLY)
        def _():
            acc2[...] += _dot_tn(rl_ref[...], df_ref[rows, :])

        acc1[...] += _dot_tn(h2_ref[rows, :], da_ref[...])

        for step, rel in enumerate(WGRAD_ORDER):
            jj = rel // 2

            @pl.when((t == nt - 1) & (j == step))
            def _():
                for a, (own_ref, out_ref) in enumerate(((own1_ref, out1_ref), (own2_ref, out2_ref))):
                    if rel % 2:
                        q = odd_steps.index(step)
                        if q >= 2:
                            to_sibling(a, WGRAD_ORDER[odd_steps[q - 2]] // 2).wait_send()
                        snds[a][q % 2] = accs[a][...].astype(BF16)
                        to_sibling(a, jj, q % 2).start()
                        continue
                    to_sibling(a, jj).wait_recv()
                    chip_sum = accs[a][...] + sibs[a][jj].astype(F32)
                    if rel == 6:
                        dsnds[a][...] = chip_sum.astype(BF16)
                        to_diagonal(a).start()
                    elif rel == 0:
                        own_ref[...] = chip_sum
                    else:
                        out_ref[0] = chip_sum.astype(BF16)
                    if step == N_DEV - 1:
                        for q in (2, 3):
                            to_sibling(a, WGRAD_ORDER[odd_steps[q]] // 2).wait_send()
                        to_diagonal(a).wait_recv()
                        to_diagonal(a).wait_send()

    assert WGRAD_ORDER[-1] == 0 and WGRAD_ORDER[-3:-1] == (2, 4)
    blk = pl.BlockSpec((tt, FF_BLK), lambda j, t: (t, relation(j)))
    early = lambda j, t: (jnp.where(relation(j) < FC_EARLY, t, 0), jnp.where(relation(j) < FC_EARLY, relation(j), 0))
    late = lambda j, t: (jnp.where(relation(j) < FC_EARLY, 0, t), jnp.maximum(relation(j) - FC_EARLY, 0))
    chip = lambda j, t: (jnp.clip(j - 5, 0, 1), 0, 0)
    hbm = pl.BlockSpec(memory_space=pl.ANY)
    return pl.pallas_call(
        body, name="mlp_wgrad", grid=(N_DEV, nt),
        out_shape=(jax.ShapeDtypeStruct((D, FF_BLK), F32), jax.ShapeDtypeStruct((FF_BLK, D), F32),
                   jax.ShapeDtypeStruct((2, D, FF_BLK), BF16), jax.ShapeDtypeStruct((2, FF_BLK, D), BF16),
                   jax.ShapeDtypeStruct((D, FF_BLK), BF16), jax.ShapeDtypeStruct((FF_BLK, D), BF16)),
        in_specs=[pl.BlockSpec((tt, FF_BLK), early), pl.BlockSpec((tt, FF_BLK), late), blk,
                  _resident((t_len, D)), _resident((t_len, D))],
        out_specs=(_full((D, FF_BLK)), _full((FF_BLK, D)),
                   pl.BlockSpec((1, D, FF_BLK), chip), pl.BlockSpec((1, FF_BLK, D), chip), hbm, hbm),
        scratch_shapes=[pltpu.VMEM((D, FF_BLK), F32), pltpu.VMEM((FF_BLK, D), F32),
                        pltpu.VMEM((2, D, FF_BLK), BF16), pltpu.VMEM((2, FF_BLK, D), BF16),
                        pltpu.VMEM((4, D, FF_BLK), BF16), pltpu.VMEM((4, FF_BLK, D), BF16),
                        pltpu.VMEM((D, FF_BLK), BF16), pltpu.VMEM((FF_BLK, D), BF16),
                        pltpu.SemaphoreType.DMA((10,)), pltpu.SemaphoreType.DMA((10,))],
        compiler_params=pltpu.CompilerParams(dimension_semantics=("arbitrary", "arbitrary"),
                                             vmem_limit_bytes=VMEM_LIMIT),
    )(r_early, r_late, da, df, h2)


def _acc_rows(ref, row0, k, val):
    half = CHUNK // 2
    ref[row0:row0 + half, k * GROUP:(k + 1) * GROUP] += val[:half, :]
    ref[row0:row0 + half, D_A + k * GROUP:D_A + (k + 1) * GROUP] += val[half:, :]


def _attn_bwd(tt, dmix, dx1, x, z, cat, mod, n1pre, w_in_t, w_out, w_sp, bs_rows, ln_g, ln_b, w_pool, b_pool,
              pool_scale, red_fwd, red_bwd, chip_sums):
    t_len = x.shape[0]
    nt = t_len // tt
    hb = tt // HALO
    n_sums = len(chip_sums)

    def body(dmix_ref, dx1_ref, x_ref, z_ref, zprev_ref, cat_ref, mod_ref, n1pre_ref, win_ref, wout_ref, wsp_ref,
             bs_ref, lng_ref, lnb_ref, wp_ref, bp_ref, ps_ref, redf_ref, redb_ref, *rest):
        sum_out = rest[:n_sums]
        gx_ref, gwin_ref, gwout_ref, small_ref = rest[n_sums:n_sums + 4]
        sum_in = rest[n_sums + 4:2 * n_sums + 4]
        carry, acc_in, acc_out, dz_scr, bs_acc, send_sems, recv_sems = rest[2 * n_sums + 4:]
        s = pl.program_id(0)
        i = nt - 1 - s
        px, py, pc = _place()

        def chip_copy(a, r):
            return pltpu.make_async_remote_copy(
                src_ref=sum_out[a].at[r], dst_ref=sum_in[a].at[r],
                send_sem=send_sems.at[2 * a + r], recv_sem=recv_sems.at[2 * a + r],
                device_id=_peer(px, py, pc, 2 * (r + 1)), device_id_type=MESH)

        @pl.when(s == 0)
        def _():
            for a in range(n_sums):
                for r in range(2):
                    chip_copy(a, r).start()
            carry[...] = jnp.zeros_like(carry)
            acc_in[...] = jnp.zeros_like(acc_in)
            acc_out[...] = jnp.zeros_like(acc_out)
            bs_acc[...] = jnp.zeros_like(bs_acc)
            small_ref[...] = jnp.zeros_like(small_ref)
            small_ref[ROW_DMOD + 2:ROW_DMOD + 3, :] = redb_ref[3:4, :]
            small_ref[ROW_DMOD + 3:ROW_DMOD + 5, :] = redb_ref[0:2, :]
            small_ref[ROW_DMOD + 5:ROW_DMOD + 6, :] = redf_ref[0:1, :]
            small_ref[ROW_N1POST:ROW_N1POST + 1, :] = redb_ref[4:5, :]
            small_ref[ROW_N2PRE:ROW_N2PRE + 1, :] = redb_ref[2:3, :]
            small_ref[ROW_N2POST:ROW_N2POST + 1, :] = redf_ref[1:2, :]
            small_ref[ROW_LOSS:ROW_LOSS + 1, :] = redf_ref[2:3, :]

        dmixv = dmix_ref[...]
        dcat = _dot_nt(dmixv, wout_ref[...])
        acc_out[...] += _dot_tn(cat_ref[...], dmixv)

        z = z_ref[...]
        t_g, ga = _gelu_parts(z[:, :2 * D_A])
        u, vr = ga[:, :D_A], ga[:, D_A:]
        dv0 = vr - jnp.mean(vr, axis=-1, keepdims=True)
        rv = lax.rsqrt(jnp.mean(dv0 * dv0, axis=-1, keepdims=True) + EPS)
        vhat = dv0 * rv
        vb = (vhat * lng_ref[...] + lnb_ref[...]).astype(BF16)
        mask = _tril_mask()
        wc = [(wsp_ref[h] * mask).astype(BF16) for h in range(N_HEADS)]

        dya = dcat[:, :D_A]
        for h in range(N_HEADS):
            cols = slice(h * GROUP, (h + 1) * GROUP)
            bs_sum = jnp.zeros((CHUNK, GROUP), F32)
            ws_sum = jnp.zeros((CHUNK, CHUNK), F32)
            for ch in range(tt // CHUNK):
                rows = slice(ch * CHUNK, (ch + 1) * CHUNK)
                v_ch = vb[rows, cols]
                mixed = _dot(wc[h], v_ch) + bs_ref[:, cols]
                dy_ch = dya[rows, cols]
                dz_scr[rows, cols] = dy_ch * mixed
                dmixed = dy_ch * u[rows, cols]
                dmb = dmixed.astype(BF16)
                dz_scr[rows, D_A + h * GROUP:D_A + (h + 1) * GROUP] = _dot_tn(wc[h], dmb)
                bs_sum = bs_sum + dmixed
                ws_sum = ws_sum + _dot_nt(dmb, v_ch)
            _acc_rows(bs_acc, 0, h, bs_sum)
            _acc_rows(small_ref, ROW_WS, h, ws_sum)

        dvl = dz_scr[:, D_A:2 * D_A]
        dvhat = dvl * lng_ref[...]
        dvr = rv * (dvhat - jnp.mean(dvhat, axis=-1, keepdims=True)
                    - vhat * jnp.mean(dvhat * vhat, axis=-1, keepdims=True))
        small_ref[ROW_LN:ROW_LN + 1, 0:D_A] += _colsum(dvl * vhat)
        small_ref[ROW_LN:ROW_LN + 1, D_A:D] += _colsum(dvl)
        dga = jnp.concatenate([dz_scr[:, :D_A], dvr], axis=1)
        dza = dga * _gelu_grad(z[:, :2 * D_A], t_g)

        zb = z[:, 2 * D_A:]
        halo_prev = jnp.where(i == 0, 0.0, zprev_ref[...])
        diff = _pool_diff(zb, halo_prev, i * tt)
        dyb = dcat[:, D_A:]
        inv = _inv_counts(i * tt, tt)
        scaled, ddiffs = [], []
        for g in range(len(WINDOWS)):
            cols = slice(g * GROUP, (g + 1) * GROUP)
            db = diff[g].astype(BF16)
            wpg = wp_ref[g].astype(BF16)
            pre = _dot(db, wpg) + bp_ref[:, cols]
            small_ref[ROW_POOL:ROW_POOL + 1, cols] += _colsum(dyb[:, cols] * pre)
            dpre = dyb[:, cols] * ps_ref[:, cols]
            small_ref[ROW_POOL:ROW_POOL + 1, D_B + g * GROUP:D_B + (g + 1) * GROUP] += _colsum(dpre)
            dpb = dpre.astype(BF16)
            _acc_rows(small_ref, ROW_WP, g, _dot_tn(db, dpb))
            ddiff = _dot_nt(dpb, wpg)
            ddiffs.append(ddiff)
            scaled.append(ddiff * inv[g])
        scaled_all = jnp.concatenate(scaled, axis=1)
        ext = jnp.concatenate([scaled_all, carry[...]], axis=0)
        n_ext = tt + HALO
        s2 = ext + pltpu.roll(ext, n_ext - 1, 0)
        t4 = s2[:, GROUP:]
        s4 = t4 + pltpu.roll(t4, n_ext - 2, 0)
        t8 = s4[:, GROUP:]
        s8 = t8 + pltpu.roll(t8, n_ext - 4, 0)
        t16 = s8[:, GROUP:]
        s16 = t16 + pltpu.roll(t16, n_ext - 8, 0)
        back = [s2[:, :GROUP], s4[:, :GROUP], s8[:, :GROUP], s16]
        carry[...] = scaled_all[:HALO, :]
        dzb = jnp.concatenate([back[g][:tt, :] - ddiffs[g] for g in range(len(WINDOWS))], axis=1)

        dzv = jnp.concatenate([dza, dzb], axis=1).astype(BF16)
        dh1 = _dot(dzv, win_ref[...])
        xv = x_ref[...]
        r1 = _rstd(xv)
        xhat = xv * r1
        shift1, scale1 = mod_ref[0:1, :], mod_ref[1:2, :]
        n1 = xhat * n1pre_ref[...]
        h1 = (n1 * (1.0 + scale1) + shift1).astype(BF16)
        acc_in[...] += _dot_tn(dzv, h1)
        dn1 = dh1 * (1.0 + scale1)
        gx_ref[...] = dx1_ref[...] + _rms_bwd(dn1 * n1pre_ref[...], xhat, r1)
        small_ref[ROW_DMOD:ROW_DMOD + 1, :] += _colsum(dh1)
        small_ref[ROW_DMOD + 1:ROW_DMOD + 2, :] += _colsum(dh1 * n1)
        small_ref[ROW_N1PRE:ROW_N1PRE + 1, :] += _colsum(dn1 * xhat)

        @pl.when(s == nt - 1)
        def _():
            gwin_ref[...] = acc_in[...].astype(BF16)
            gwout_ref[...] = acc_out[...].astype(BF16)
            bs = _unfold(bs_acc[...])
            for h in range(N_HEADS):
                small_ref[ROW_BS + h:ROW_BS + h + 1, 0:GROUP] = jnp.sum(
                    bs[:, h * GROUP:(h + 1) * GROUP].T, axis=0, keepdims=True)
            for a in range(n_sums):
                for r in range(2):
                    chip_copy(a, r).wait_recv()
                    chip_copy(a, r).wait_send()

    rev = lambda w: pl.BlockSpec((tt, w), lambda s: (nt - 1 - s, 0))
    zprev = pl.BlockSpec((HALO, D_B), lambda s: (jnp.maximum((nt - 1 - s) * hb - 1, 0), 2))
    hbm = pl.BlockSpec(memory_space=pl.ANY)
    outs = pl.pallas_call(
        body, name="attn_bwd", grid=(nt,),
        out_shape=tuple([jax.ShapeDtypeStruct((t_len, D), F32), jax.ShapeDtypeStruct((D_Z, D), BF16),
                         jax.ShapeDtypeStruct((D, D), BF16), jax.ShapeDtypeStruct((SMALL_ROWS, D), F32)]
                        + [jax.ShapeDtypeStruct(cs.shape, cs.dtype) for cs in chip_sums]),
        in_specs=[rev(D), rev(D), rev(D), rev(D_Z), zprev, rev(D), _full((8, D)), _full((1, D)),
                  _resident((D_Z, D)), _resident((D, D)), _full((N_HEADS, CHUNK, CHUNK)), _full((CHUNK, D_A)),
                  _full((1, D_A)), _full((1, D_A)), _full((len(WINDOWS), GROUP, GROUP)), _full((1, D_B)),
                  _full((1, D_B)), _full((8, D)), _full((8, D))] + [_resident(cs.shape) for cs in chip_sums],
        out_specs=tuple([rev(D), _resident((D_Z, D)), _resident((D, D)), _full((SMALL_ROWS, D))] + [hbm] * n_sums),
        scratch_shapes=[pltpu.VMEM((HALO, D_B), F32), pltpu.VMEM((D_Z, D), F32), pltpu.VMEM((D, D), F32),
                        pltpu.VMEM((tt, 2 * D_A), F32), pltpu.VMEM((CHUNK // 2, D), F32),
                        pltpu.SemaphoreType.DMA((2 * n_sums,)), pltpu.SemaphoreType.DMA((2 * n_sums,))],
        compiler_params=pltpu.CompilerParams(dimension_semantics=("arbitrary",), vmem_limit_bytes=VMEM_LIMIT),
    )(dmix, dx1, x, z, z, cat, mod, n1pre, w_in_t, w_out, w_sp, bs_rows, ln_g, ln_b, w_pool, b_pool, pool_scale,
      red_fwd, red_bwd, *chip_sums)
    return outs[:4], outs[4:]


def _adam(w, g, m, v):
    m2 = ADAM_B1 * m + (1.0 - ADAM_B1) * g
    v2 = ADAM_B2 * v + (1.0 - ADAM_B2) * (g * g)
    m_hat = m2 / (1.0 - ADAM_B1 ** ADAM_STEP)
    v_hat = v2 / (1.0 - ADAM_B2 ** ADAM_STEP)
    delta = -ADAM_LR * (m_hat / (jnp.sqrt(v_hat) + ADAM_EPS) + ADAM_WD * w)
    return delta, m2, v2


def _adamw_shard(name, rb, w, g, m, v):
    rows, cols = w.shape

    def body(w_ref, g_ref, m_ref, v_ref, d_ref, m2_ref, v2_ref):
        d_ref[...], m2_ref[...], v2_ref[...] = _adam(w_ref[...], g_ref[...], m_ref[...], v_ref[...])

    blk = pl.BlockSpec((rb, cols), lambda i: (i, 0))
    shp = jax.ShapeDtypeStruct((rows, cols), F32)
    return pl.pallas_call(
        body, name=name, grid=(rows // rb,), out_shape=(shp, shp, shp),
        in_specs=[blk] * 4, out_specs=(blk, blk, blk),
        compiler_params=pltpu.CompilerParams(dimension_semantics=("arbitrary",)),
    )(w, g, m, v)


def _adamw_ada(rb, w, sc, dmod_cols, m, v):
    rows, cols = w.shape

    def body(w_ref, sc_ref, dm_ref, m_ref, v_ref, g_ref, d_ref, m2_ref, v2_ref):
        g = _dot_tn(sc_ref[...].astype(BF16), dm_ref[...].astype(BF16))
        g_ref[...] = g
        d_ref[...], m2_ref[...], v2_ref[...] = _adam(w_ref[...], g, m_ref[...], v_ref[...])

    blk = pl.BlockSpec((rb, cols), lambda i: (i, 0))
    shp = jax.ShapeDtypeStruct((rows, cols), F32)
    return pl.pallas_call(
        body, name="adamw_ada", grid=(rows // rb,), out_shape=(shp, shp, shp, shp),
        in_specs=[blk, pl.BlockSpec((N_DEV, rb), lambda i: (0, i)), _full((N_DEV, cols)), blk, blk],
        out_specs=(blk, blk, blk, blk),
        compiler_params=pltpu.CompilerParams(dimension_semantics=("arbitrary",)),
    )(w, sc, dmod_cols, m, v)


def _unfold(acc_rows):
    return jnp.concatenate([acc_rows[:, :D_A], acc_rows[:, D_A:]], axis=0)


def _adamw_small(total, params):
    n = len(params)
    flat = [a for p in params for a in p]

    def body(*refs):
        s_ref = refs[0]
        p_refs = refs[1:1 + 3 * n]
        loss_ref = refs[1 + 3 * n]
        o_refs = refs[2 + 3 * n:]
        d_b_ada = s_ref[0:6, :]
        for b in range(1, N_DEV):
            d_b_ada = d_b_ada + s_ref[8 * b:8 * b + 6, :]
        tot = s_ref[PACK_SHIFT:PACK_ROWS, :]
        loss_ref[...] = jnp.broadcast_to(tot[ROW_LOSS:ROW_LOSS + 1, 0:GROUP], (8, GROUP))
        mask = _tril_mask()
        ws = _unfold(tot[ROW_WS:ROW_WS + 64, :])
        wp = _unfold(tot[ROW_WP:ROW_WP + 64, :])
        grads = [
            d_b_ada,
            tot[ROW_N1PRE:ROW_N1PRE + 1, :], tot[ROW_N1POST:ROW_N1POST + 1, :],
            tot[ROW_N2PRE:ROW_N2PRE + 1, :], tot[ROW_N2POST:ROW_N2POST + 1, :],
            tot[ROW_LN:ROW_LN + 1, :D_A], tot[ROW_LN:ROW_LN + 1, D_A:],
            tot[ROW_POOL:ROW_POOL + 1, :D_B], tot[ROW_POOL:ROW_POOL + 1, D_B:],
            tot[ROW_BS:ROW_BS + N_HEADS, 0:GROUP],
            jnp.stack([ws[:, h * GROUP:(h + 1) * GROUP] * mask for h in range(N_HEADS)]),
            jnp.stack([wp[:, g * GROUP:(g + 1) * GROUP] for g in range(len(WINDOWS))]),
        ]
        for k in range(n):
            w_ref, m_ref, v_ref = p_refs[3 * k:3 * k + 3]
            g = grads[k]
            o_refs[4 * k][...] = g
            o_refs[4 * k + 1][...], o_refs[4 * k + 2][...], o_refs[4 * k + 3][...] = _adam(
                w_ref[...], g, m_ref[...], v_ref[...])

    vm = pl.BlockSpec(memory_space=pltpu.VMEM)
    out_shape = [jax.ShapeDtypeStruct((8, GROUP), F32)]
    for w, _, _ in params:
        out_shape += [jax.ShapeDtypeStruct(w.shape, F32)] * 4
    return pl.pallas_call(
        body, name="adamw_small", out_shape=tuple(out_shape),
        in_specs=[vm] * (1 + 3 * n), out_specs=tuple([vm] * len(out_shape)),
    )(total, *flat)


TT_ATTN_FWD = 512
TT_MLP_FWD = 512
TT_MLP = 256
MLP_SUB_ROWS = 256
TT_WGRAD = 2048
TT_ATTN_BWD = 512


def kernel(x, c, w_ada, b_ada, norm1_pre, norm1_post, w_in, w_spatial, b_spatial, ln_v_gain, ln_v_bias, w_pool, b_pool, pool_scale, w_out, norm2_pre, norm2_post, w_fc1, w_fc2, loss_target, m_w_ada, m_b_ada, m_norm1_pre, m_norm1_post, m_w_in, m_w_spatial, m_b_spatial, m_ln_v_gain, m_ln_v_bias, m_w_pool, m_b_pool, m_pool_scale, m_w_out, m_norm2_pre, m_norm2_post, m_w_fc1, m_w_fc2, v_w_ada, v_b_ada, v_norm1_pre, v_norm1_post, v_w_in, v_w_spatial, v_b_spatial, v_ln_v_gain, v_ln_v_bias, v_w_pool, v_b_pool, v_pool_scale, v_w_out, v_norm2_pre, v_norm2_post, v_w_fc1, v_w_fc2):
    t_len = x.shape[1]
    me = 4 * lax.axis_index("x") + 2 * lax.axis_index("y") + lax.axis_index("c")
    ada_cols = w_ada.shape[1]
    tt = lambda want: min(want, t_len)

    x2 = x.reshape(t_len, D)
    tgt = loss_target.reshape(t_len, D)
    row = lambda a: a.reshape(1, -1)

    b_my = lax.dynamic_slice_in_dim(b_ada, me * ada_cols, ada_cols).reshape(1, ada_cols)
    modp, sc, (g_in, g_out), fc_shards = _fwd_comm(jnp.broadcast_to(c, (8, D)), w_ada, b_my,
                                                   [w_in.T, w_out], [w_fc1, w_fc2])
    mod = jnp.concatenate([modp.reshape(6, D), jnp.zeros((2, D), F32)], axis=0)
    w_in_t = g_in.reshape(D_Z, D)
    w_out_all = g_out.reshape(D, D)

    bs_rows = jnp.repeat(b_spatial.T, GROUP, axis=1)
    attn_consts = (w_spatial, bs_rows, row(ln_v_gain), row(ln_v_bias), w_pool, row(b_pool), row(pool_scale))

    (z, cat, mix, x1), (w1_early, w2_early) = _attn_fwd(
        tt(TT_ATTN_FWD), x2, mod, row(norm1_pre), row(norm1_post), w_in_t, w_out_all, *attn_consts, fc_shards)
    (r_early, h2, f_early), (w1_late, w2_late) = _mlp_fwd_early(
        tt(TT_MLP_FWD), x1, mod, row(norm2_pre), w1_early, w2_early)
    r_late, df, da, dmix, dx1, red_fwd, red_bwd = _mlp_late_bwd(
        tt(TT_MLP), r_early, x1, h2, f_early, tgt, mix, mod, row(norm2_pre), row(norm2_post), row(norm1_post),
        w1_early, w2_early, w1_late, w2_late)
    own_w1, own_w2, sums_w1, sums_w2, diag_w1, diag_w2 = _mlp_wgrad(tt(TT_WGRAD), r_early, r_late, da, df, h2)
    (grad_x, p_in, p_out, small), (arr_w1, arr_w2) = _attn_bwd(
        tt(TT_ATTN_BWD), dmix, dx1, x2, z, cat, mod, row(norm1_pre), w_in_t, w_out_all, *attn_consts,
        red_fwd, red_bwd, [sums_w1, sums_w2])
    (grad_in_t, grad_out, total), ((grad_w1, d_w1, m_w1, v_w1), (grad_w2, d_w2, m_w2, v_w2)) = _tail_comm(
        [p_in.reshape(N_DEV, D_Z // N_DEV, D), p_out.reshape(N_DEV, D // N_DEV, D)], small, 64,
        [(w_fc1, own_w1, arr_w1, diag_w1, m_w_fc1, v_w_fc1), (w_fc2, own_w2, arr_w2, diag_w2, m_w_fc2, v_w_fc2)])

    d_out, m_out, v_out = _adamw_shard("adamw_out", 128, w_out, grad_out, m_w_out, v_w_out)
    grad_in = grad_in_t.T
    d_in, m_in, v_in = _adamw_shard("adamw_in", 256, w_in, grad_in, m_w_in, v_w_in)
    dmod_all = total[0:TABLE_ROWS, :].reshape(N_DEV, 8, D)[:, :6, :].reshape(N_DEV, 6 * D)
    dmod_cols = lax.dynamic_slice_in_dim(dmod_all, me * ada_cols, ada_cols, axis=1)
    grad_ada, d_ada, m_ada, v_ada = _adamw_ada(256, w_ada, sc, dmod_cols, m_w_ada, v_w_ada)

    six = lambda a: a.reshape(6, D)
    small_params = [
        (six(b_ada), six(m_b_ada), six(v_b_ada)),
        (row(norm1_pre), row(m_norm1_pre), row(v_norm1_pre)),
        (row(norm1_post), row(m_norm1_post), row(v_norm1_post)),
        (row(norm2_pre), row(m_norm2_pre), row(v_norm2_pre)),
        (row(norm2_post), row(m_norm2_post), row(v_norm2_post)),
        (row(ln_v_gain), row(m_ln_v_gain), row(v_ln_v_gain)),
        (row(ln_v_bias), row(m_ln_v_bias), row(v_ln_v_bias)),
        (row(pool_scale), row(m_pool_scale), row(v_pool_scale)),
        (row(b_pool), row(m_b_pool), row(v_b_pool)),
        (b_spatial, m_b_spatial, v_b_spatial),
        (w_spatial, m_w_spatial, v_w_spatial),
        (w_pool, m_w_pool, v_w_pool),
    ]
    outs = _adamw_small(total, small_params)
    loss = outs[0][0, 0]
    names = ["b_ada", "norm1_pre", "norm1_post", "norm2_pre", "norm2_post", "ln_v_gain", "ln_v_bias", "pool_scale",
             "b_pool", "b_spatial", "w_spatial", "w_pool"]
    shapes = dict(b_ada=b_ada.shape, norm1_pre=norm1_pre.shape, norm1_post=norm1_post.shape,
                  norm2_pre=norm2_pre.shape, norm2_post=norm2_post.shape, ln_v_gain=ln_v_gain.shape,
                  ln_v_bias=ln_v_bias.shape, pool_scale=pool_scale.shape, b_pool=b_pool.shape,
                  b_spatial=b_spatial.shape, w_spatial=w_spatial.shape, w_pool=w_pool.shape)
    res = {}
    for k, nm in enumerate(names):
        res[nm] = tuple(o.reshape(shapes[nm]) for o in outs[1 + 4 * k:5 + 4 * k])
    res["w_ada"] = (grad_ada, d_ada, m_ada, v_ada)
    res["w_in"] = (grad_in, d_in, m_in, v_in)
    res["w_out"] = (grad_out, d_out, m_out, v_out)
    res["w_fc1"] = (grad_w1, d_w1, m_w1, v_w1)
    res["w_fc2"] = (grad_w2, d_w2, m_w2, v_w2)

    order = ["w_ada", "b_ada", "norm1_pre", "norm1_post", "w_in", "w_spatial", "b_spatial", "ln_v_gain", "ln_v_bias",
             "w_pool", "b_pool", "pool_scale", "w_out", "norm2_pre", "norm2_post", "w_fc1", "w_fc2"]
    return (loss, grad_x.reshape(x.shape),
            *[res[nm][0] for nm in order], *[res[nm][1] for nm in order],
            *[res[nm][2] for nm in order], *[res[nm][3] for nm in order])
```

```python
import functools

import jax
import jax.numpy as jnp
from jax import lax
from jax.experimental import pallas as pl
from jax.experimental.pallas import tpu as pltpu

F32 = jnp.float32
BF16 = jnp.bfloat16
MESH = pl.DeviceIdType.MESH

N_DEV = 8
D = 1024
D_A = 512
D_B = 512
D_Z = 2 * D_A + D_B
N_HEADS = 4
CHUNK = 128
WINDOWS = (2, 4, 8, 16)
GROUP = 128
D_FF = 4096
FF_BLK = D_FF // N_DEV
HALO = 16
EPS = 1e-6
VMEM_LIMIT = 60 * 1024 * 1024

ADAM_LR = 0.001
ADAM_B1 = 0.9
ADAM_B2 = 0.999
ADAM_EPS = 1e-08
ADAM_WD = 0.01
ADAM_STEP = 10

ROW_DMOD = 0
ROW_N1PRE, ROW_N1POST, ROW_N2PRE, ROW_N2POST = 8, 9, 10, 11
ROW_LN = 12
ROW_POOL = 13
ROW_LOSS = 14
ROW_BS = 16
ROW_WS = 24
ROW_WP = 88
SMALL_ROWS = 152
TABLE_ROWS = 8 * N_DEV
PACK_SHIFT = TABLE_ROWS - 8
PACK_ROWS = SMALL_ROWS + PACK_SHIFT
PACK_HALF = PACK_ROWS // 2


def _dot(a, b):
    return jnp.dot(a, b, preferred_element_type=F32)


def _dot_nt(a, b):
    return lax.dot_general(a, b, (((1,), (1,)), ((), ())), preferred_element_type=F32)


def _dot_tn(a, b):
    return lax.dot_general(a, b, (((0,), (0,)), ((), ())), preferred_element_type=F32)


def _rstd(v):
    return lax.rsqrt(jnp.mean(v * v, axis=-1, keepdims=True) + EPS)


def _rms_bwd(d_hat, hat, rstd):
    return rstd * (d_hat - hat * jnp.mean(d_hat * hat, axis=-1, keepdims=True))


_K0 = 0.7978845608028654
_K1 = 0.044715


def _gelu_parts(v):
    t = jnp.tanh(_K0 * (v + _K1 * (v * v * v)))
    return t, v * (0.5 * (1.0 + t))


def _gelu_grad(v, t):
    return 0.5 * (1.0 + t) + (0.5 * v) * (1.0 - t * t) * (_K0 * (1.0 + (3.0 * _K1) * (v * v)))


def _colsum(v):
    return jnp.sum(v, axis=0, keepdims=True)


def _full(shape):
    n = len(shape)
    return pl.BlockSpec(shape, lambda *_: (0,) * n)


def _resident(shape):
    n = len(shape)
    return pl.BlockSpec(shape, lambda *_: (0,) * n, pipeline_mode=pl.Buffered(1))


def _place():
    x, y, c = lax.axis_index("x"), lax.axis_index("y"), lax.axis_index("c")
    return x, y, c


def _flip(v, bit):
    return 1 - v if bit else v


def _peer(x, y, c, k):
    return (_flip(x, (k >> 2) & 1), _flip(y, (k >> 1) & 1), _flip(c, k & 1))


def _index(p):
    return 4 * p[0] + 2 * p[1] + p[2]


def _two_level_gather_begin(x, y, c, out_refs, send_sems, recv_sems):
    me = (x, y, c)
    sibling = (x, y, 1 - c)
    chips = [(1 - x, y), (x, 1 - y), (1 - x, 1 - y)]

    def copy(a, k, block, to):
        ref = out_refs[a].at[_index(block)]
        return pltpu.make_async_remote_copy(
            src_ref=ref, dst_ref=ref, send_sem=send_sems.at[7 * a + k], recv_sem=recv_sems.at[7 * a + k],
            device_id=to, device_id_type=MESH)

    first = []
    for a in range(len(out_refs)):
        first.append(copy(a, 0, me, sibling))
        first += [copy(a, 1 + j, me, (*chip, c)) for j, chip in enumerate(chips)]
    for cp in first:
        cp.start()
    return copy, first, me, sibling, chips


def _two_level_gather_finish(c, n, begun):
    copy, first, me, sibling, chips = begun
    passed = []
    for a in range(n):
        for j, chip in enumerate(chips):
            copy(a, 1 + j, (*chip, c), me).wait_recv()
            fwd = copy(a, 4 + j, (*chip, c), sibling)
            fwd.start()
            passed.append(fwd)
    for a in range(n):
        copy(a, 0, sibling, me).wait_recv()
        for j, chip in enumerate(chips):
            copy(a, 4 + j, (*chip, 1 - c), me).wait_recv()
    for cp in first + passed:
        cp.wait_send()


def _fwd_comm(c8, w_ada, b_my, gathered, kept):
    ncol = w_ada.shape[1]
    n_g, n_k = len(gathered), len(kept)

    def body(c_ref, w_ref, b_ref, *rest):
        g_in, k_in = rest[:n_g], rest[n_g:n_g + n_k]
        modp_ref, sc_ref = rest[n_g + n_k:n_g + n_k + 2]
        g_out = rest[n_g + n_k + 2:2 * n_g + n_k + 2]
        k_out = rest[2 * n_g + n_k + 2:2 * n_g + 2 * n_k + 2]
        cg, mg, part, send_sems, recv_sems, g_send, g_recv = rest[2 * n_g + 2 * n_k + 2:]
        x, y, c = _place()
        me = _index((x, y, c))
        for a in range(n_g):
            g_out[a][me] = g_in[a][...].astype(BF16)
        begun = _two_level_gather_begin(x, y, c, g_out, g_send, g_recv)
        for a in range(n_k):
            k_out[a][...] = k_in[a][...].astype(BF16)

        def c_copy(k):
            p = _peer(x, y, c, k)
            return pltpu.make_async_remote_copy(
                src_ref=c_ref, dst_ref=cg.at[me], send_sem=send_sems.at[k - 1], recv_sem=recv_sems.at[k - 1],
                device_id=p, device_id_type=MESH)

        def c_arrival(k):
            p = _peer(x, y, c, k)
            return pltpu.make_async_remote_copy(
                src_ref=c_ref, dst_ref=cg.at[_index(p)], send_sem=send_sems.at[k - 1], recv_sem=recv_sems.at[k - 1],
                device_id=p, device_id_type=MESH)

        def m_copy(k):
            p = _peer(x, y, c, k)
            return pltpu.make_async_remote_copy(
                src_ref=part, dst_ref=mg.at[me], send_sem=send_sems.at[6 + k], recv_sem=recv_sems.at[6 + k],
                device_id=p, device_id_type=MESH)

        def m_arrival(k):
            p = _peer(x, y, c, k)
            return pltpu.make_async_remote_copy(
                src_ref=part, dst_ref=mg.at[_index(p)], send_sem=send_sems.at[6 + k], recv_sem=recv_sems.at[6 + k],
                device_id=p, device_id_type=MESH)

        for k in range(1, N_DEV):
            c_copy(k).start()
        cg[me] = c_ref[...]
        for k in range(1, N_DEV):
            c_arrival(k).wait_recv()
        c_all = jnp.concatenate([cg[j, 0:1, :] for j in range(N_DEV)], axis=0)
        sc = c_all * jax.nn.sigmoid(c_all)
        sc_ref[...] = sc
        part[...] = _dot(sc.astype(BF16), w_ref[...].astype(BF16)) + b_ref[...]
        for k in range(1, N_DEV):
            m_copy(k).start()
        mg[me] = part[...]
        for k in range(1, N_DEV):
            m_arrival(k).wait_recv()
        for j in range(N_DEV):
            modp_ref[j:j + 1, :] = mg[j, pl.ds(me, 1), :]
        _two_level_gather_finish(c, n_g, begun)
        for k in range(1, N_DEV):
            c_copy(k).wait_send()
            m_copy(k).wait_send()

    vm = pl.BlockSpec(memory_space=pltpu.VMEM)
    outs = pl.pallas_call(
        body, name="fwd_comm",
        out_shape=tuple([jax.ShapeDtypeStruct((N_DEV, ncol), F32), jax.ShapeDtypeStruct((N_DEV, D), F32)]
                        + [jax.ShapeDtypeStruct((N_DEV,) + s.shape, BF16) for s in gathered]
                        + [jax.ShapeDtypeStruct(s.shape, BF16) for s in kept]),
        in_specs=[vm] * (3 + n_g + n_k), out_specs=tuple([vm] * (2 + n_g + n_k)),
        scratch_shapes=[
            pltpu.VMEM((N_DEV, 8, D), F32),
            pltpu.VMEM((N_DEV, N_DEV, ncol), F32),
            pltpu.VMEM((N_DEV, ncol), F32),
            pltpu.SemaphoreType.DMA((2 * (N_DEV - 1),)),
            pltpu.SemaphoreType.DMA((2 * (N_DEV - 1),)),
            pltpu.SemaphoreType.DMA((7 * n_g,)),
            pltpu.SemaphoreType.DMA((7 * n_g,)),
        ],
        compiler_params=pltpu.CompilerParams(vmem_limit_bytes=VMEM_LIMIT),
    )(c8, w_ada, b_my, *gathered, *kept)
    return outs[0], outs[1], outs[2:2 + n_g], outs[2 + n_g:]


FC_EARLY = 6
WGRAD_ORDER = (7, 6, 1, 3, 5, 2, 4, 0)


class _Copies:
    def __init__(self, entries, send_sems, recv_sems):
        self.place = _place()
        self.entries, self.send_sems, self.recv_sems = entries, send_sems, recv_sems

    def _copy(self, i, arrival=False):
        src, dst, rel = self.entries[i]
        return pltpu.make_async_remote_copy(
            src_ref=dst if arrival else src, dst_ref=dst, send_sem=self.send_sems.at[i],
            recv_sem=self.recv_sems.at[i], device_id=_peer(*self.place, rel), device_id_type=MESH)

    def start(self, *which):
        for i in which:
            self._copy(i).start()

    def wait_recv(self, *which):
        for i in which:
            self._copy(i, arrival=True).wait_recv()

    def wait_send(self, *which):
        for i in which:
            self._copy(i).wait_send()


TAIL_STEPS = 8


def _tail_comm(parts, small, row_chunk, fc):
    n, n_fc = len(parts), len(fc)

    def body(*refs):
        p_refs, small_ref = refs[:n], refs[n]
        fc_in = refs[n + 1:n + 1 + 6 * n_fc]
        outs = refs[n + 1 + 6 * n_fc:]
        g_refs, total_ref = outs[:n], outs[n]
        fc_out = outs[n + 1:n + 1 + 4 * n_fc]
        scr = outs[n + 1 + 4 * n_fc:]
        from_sib = scr[0:n]
        chip_out = scr[n:2 * n]
        chip_in = scr[2 * n:3 * n]
        pack, pack_sib, halves, total_scr = scr[3 * n:3 * n + 4]
        send_a, recv_a, send_b, recv_b, send_s, recv_s = scr[3 * n + 4:]
        step = pl.program_id(0)
        x, y, c = _place()
        me = _index((x, y, c))
        sibling = (x, y, 1 - c)
        my_chip = 2 * x + y
        others = [(1 - x, y), (x, 1 - y), (1 - x, 1 - y)]
        my_half = pl.ds(pl.multiple_of(PACK_HALF * c, 8), PACK_HALF)

        def pack_to_sibling():
            return pltpu.make_async_remote_copy(
                src_ref=pack, dst_ref=pack_sib, send_sem=send_s.at[0], recv_sem=recv_s.at[0],
                device_id=sibling, device_id_type=MESH)

        def half_to_chip(r):
            return pltpu.make_async_remote_copy(
                src_ref=halves.at[my_chip], dst_ref=halves.at[my_chip],
                send_sem=send_s.at[1 + r], recv_sem=recv_s.at[1 + r],
                device_id=(*others[r], c), device_id_type=MESH)

        def half_from_chip(r):
            k = 2 * others[r][0] + others[r][1]
            return pltpu.make_async_remote_copy(
                src_ref=halves.at[k], dst_ref=halves.at[k], send_sem=send_s.at[1 + r], recv_sem=recv_s.at[1 + r],
                device_id=(*others[r], c), device_id_type=MESH)

        def total_to_sibling():
            return pltpu.make_async_remote_copy(
                src_ref=total_scr.at[my_half], dst_ref=total_scr.at[my_half],
                send_sem=send_s.at[4], recv_sem=recv_s.at[4], device_id=sibling, device_id_type=MESH)

        def total_from_sibling():
            sib_half = pl.ds(pl.multiple_of(PACK_HALF * (1 - c), 8), PACK_HALF)
            return pltpu.make_async_remote_copy(
                src_ref=total_scr.at[sib_half], dst_ref=total_scr.at[sib_half],
                send_sem=send_s.at[4], recv_sem=recv_s.at[4], device_id=sibling, device_id_type=MESH)

        def to_sibling(a, k):
            return pltpu.make_async_remote_copy(
                src_ref=p_refs[a].at[2 * k + (1 - c)], dst_ref=from_sib[a].at[k],
                send_sem=send_a.at[a], recv_sem=recv_a.at[a], device_id=sibling, device_id_type=MESH)

        def all_from_sibling(a):
            return pltpu.make_async_remote_copy(
                src_ref=from_sib[a], dst_ref=from_sib[a], send_sem=send_a.at[a], recv_sem=recv_a.at[a],
                device_id=sibling, device_id_type=MESH)

        def to_chip(a, r):
            return pltpu.make_async_remote_copy(
                src_ref=chip_out[a].at[r], dst_ref=chip_in[a].at[r],
                send_sem=send_b.at[3 * a + r], recv_sem=recv_b.at[3 * a + r],
                device_id=(*others[r], c), device_id_type=MESH)

        @pl.when(step == 0)
        def _():
            pack[0:TABLE_ROWS, :] = jnp.zeros((TABLE_ROWS, D), F32)
            pack[pl.ds(pl.multiple_of(8 * me, 8), 8), :] = small_ref[0:8, :]
            pack[TABLE_ROWS:PACK_ROWS, :] = small_ref[8:SMALL_ROWS, :]
            pack_to_sibling().start()
            for a in range(n):
                for k in range(4):
                    to_sibling(a, k).start()

        @pl.when(step == 1)
        def _():
            pack_to_sibling().wait_recv()
            halves[my_chip] = pack[my_half, :] + pack_sib[my_half, :]
            for r in range(3):
                half_to_chip(r).start()
            for a in range(n):
                all_from_sibling(a).wait_recv()
                rows = p_refs[a].shape[1]
                for r in range(3):
                    k = 2 * others[r][0] + others[r][1]
                    for s in range(0, rows, row_chunk):
                        sl = pl.ds(s, row_chunk)
                        chip_out[a][r, sl, :] = (p_refs[a][2 * k + c, sl, :].astype(F32)
                                                 + from_sib[a][k, sl, :].astype(F32)).astype(BF16)
                    to_chip(a, r).start()
                for s in range(0, rows, row_chunk):
                    sl = pl.ds(s, row_chunk)
                    g_refs[a][sl, :] = (p_refs[a][2 * my_chip + c, sl, :].astype(F32)
                                        + from_sib[a][my_chip, sl, :].astype(F32))

        for k in range(n_fc):
            w_ref, own_ref, arr_ref, diag_ref, m_ref, v_ref = fc_in[6 * k:6 * k + 6]
            g = own_ref[...]
            for r in range(2):
                g = g + arr_ref[r].astype(F32)
            g = g + diag_ref[...].astype(F32)
            fc_out[4 * k][...] = g
            fc_out[4 * k + 1][...], fc_out[4 * k + 2][...], fc_out[4 * k + 3][...] = _adam(
                w_ref[...], g, m_ref[...], v_ref[...])

        @pl.when(step == TAIL_STEPS - 1)
        def _():
            for r in range(3):
                half_from_chip(r).wait_recv()
            total_scr[my_half, :] = ((halves[0] + halves[1]) + halves[2]) + halves[3]
            total_to_sibling().start()
            for a in range(n):
                rows = p_refs[a].shape[1]
                for r in range(3):
                    to_chip(a, r).wait_recv()
                    for s in range(0, rows, row_chunk):
                        sl = pl.ds(s, row_chunk)
                        g_refs[a][sl, :] = g_refs[a][sl, :] + chip_in[a][r, sl, :].astype(F32)
            total_from_sibling().wait_recv()
            total_ref[...] = total_scr[...]
            for a in range(n):
                all_from_sibling(a).wait_send()
                for r in range(3):
                    to_chip(a, r).wait_send()
            pack_to_sibling().wait_send()
            for r in range(3):
                half_to_chip(r).wait_send()
            total_to_sibling().wait_send()

    fc_specs_in, fc_specs_out, fc_shapes, fc_args = [], [], [], []
    for w, own, arrived, diagonal, m, v in fc:
        rows, cols = w.shape
        blk = pl.BlockSpec((rows // TAIL_STEPS, cols), lambda i: (i, 0))
        fc_specs_in += [blk, blk, pl.BlockSpec((2, rows // TAIL_STEPS, cols), lambda i: (0, i, 0)), blk, blk, blk]
        fc_specs_out += [blk] * 4
        fc_shapes += [jax.ShapeDtypeStruct((rows, cols), F32)] * 4
        fc_args += [w, own, arrived, diagonal, m, v]
    outs = pl.pallas_call(
        body, name="tail_comm", grid=(TAIL_STEPS,),
        out_shape=tuple([jax.ShapeDtypeStruct(p.shape[1:], F32) for p in parts]
                        + [jax.ShapeDtypeStruct((PACK_ROWS, D), F32)] + fc_shapes),
        in_specs=[_resident(p.shape) for p in parts] + [_resident(small.shape)] + fc_specs_in,
        out_specs=tuple([_full(p.shape[1:]) for p in parts] + [_full((PACK_ROWS, D))] + fc_specs_out),
        scratch_shapes=(
            [pltpu.VMEM((4,) + p.shape[1:], BF16) for p in parts]
            + [pltpu.VMEM((3,) + p.shape[1:], BF16) for p in parts]
            + [pltpu.VMEM((3,) + p.shape[1:], BF16) for p in parts]
            + [pltpu.VMEM((PACK_ROWS, D), F32), pltpu.VMEM((PACK_ROWS, D), F32),
               pltpu.VMEM((4, PACK_HALF, D), F32), pltpu.VMEM((PACK_ROWS, D), F32)]
            + [pltpu.SemaphoreType.DMA((n,)), pltpu.SemaphoreType.DMA((n,)),
               pltpu.SemaphoreType.DMA((3 * n,)), pltpu.SemaphoreType.DMA((3 * n,)),
               pltpu.SemaphoreType.DMA((5,)), pltpu.SemaphoreType.DMA((5,))]),
        compiler_params=pltpu.CompilerParams(dimension_semantics=("arbitrary",), vmem_limit_bytes=VMEM_LIMIT),
    )(*parts, small, *fc_args)
    return outs[:n + 1], [outs[n + 1 + 4 * k:n + 5 + 4 * k] for k in range(n_fc)]


def _tril_mask():
    row = lax.broadcasted_iota(jnp.int32, (CHUNK, CHUNK), 0)
    col = lax.broadcasted_iota(jnp.int32, (CHUNK, CHUNK), 1)
    return (col <= row).astype(F32)


def _window_sums(ext):
    s2 = ext + pltpu.roll(ext, 1, 0)
    t4 = s2[:, GROUP:]
    s4 = t4 + pltpu.roll(t4, 2, 0)
    t8 = s4[:, GROUP:]
    s8 = t8 + pltpu.roll(t8, 4, 0)
    t16 = s8[:, GROUP:]
    s16 = t16 + pltpu.roll(t16, 8, 0)
    return [s2[:, :GROUP], s4[:, :GROUP], s8[:, :GROUP], s16]


def _inv_counts(first_pos, rows):
    pos = first_pos + lax.broadcasted_iota(jnp.int32, (rows, 1), 0)
    return [1.0 / jnp.minimum(pos + 1, w).astype(F32) for w in WINDOWS]


def _pool_diff(zb, halo, first_pos):
    tt = zb.shape[0]
    sums = _window_sums(jnp.concatenate([halo, zb], axis=0))
    inv = _inv_counts(first_pos, tt)
    return [sums[g][HALO:, :] * inv[g] - zb[:, g * GROUP:(g + 1) * GROUP] for g in range(len(WINDOWS))]


def _attn_fwd(tt, x, mod, n1pre, n1post, w_in_t, w_out, w_sp, bs_rows, ln_g, ln_b, w_pool, b_pool, pool_scale,
              fc_shards):
    t_len = x.shape[0]
    nt = t_len // tt

    def body(x_ref, mod_ref, n1pre_ref, n1post_ref, win_ref, wout_ref, wsp_ref, bs_ref, lng_ref, lnb_ref,
             wp_ref, bp_ref, ps_ref, w1_ref, w2_ref, z_ref, cat_ref, mix_ref, x1_ref, e1_ref, e2_ref,
             carry, land1, land2, send_sems, recv_sems, local_sems):
        i = pl.program_id(0)
        copies = _Copies(
            [(w1_ref, e1_ref.at[1], 1), (w2_ref, e2_ref.at[1], 1),
             (w1_ref, land1.at[0], 2), (w2_ref, land2.at[0], 2),
             (w1_ref, land1.at[1], 4), (w2_ref, land2.at[1], 4),
             (land1.at[0], e1_ref.at[3], 1), (land2.at[0], e2_ref.at[3], 1),
             (land1.at[1], e1_ref.at[5], 1), (land2.at[1], e2_ref.at[5], 1)],
            send_sems, recv_sems)
        keep = [pltpu.make_async_copy(w1_ref, e1_ref.at[0], local_sems.at[0]),
                pltpu.make_async_copy(w2_ref, e2_ref.at[0], local_sems.at[1]),
                pltpu.make_async_copy(land1.at[0], e1_ref.at[2], local_sems.at[2]),
                pltpu.make_async_copy(land1.at[1], e1_ref.at[4], local_sems.at[3]),
                pltpu.make_async_copy(land2.at[0], e2_ref.at[2], local_sems.at[4]),
                pltpu.make_async_copy(land2.at[1], e2_ref.at[4], local_sems.at[5])]

        @pl.when(i == 0)
        def _():
            copies.start(2, 4, 3, 5, 0, 1)
            keep[0].start()
            keep[1].start()
            carry[...] = jnp.zeros_like(carry)

        @pl.when(i == nt // 2)
        def _():
            copies.wait_recv(2, 4)
            copies.start(6, 8)
            keep[2].start()
            keep[3].start()

        @pl.when(i == nt - 1)
        def _():
            copies.wait_recv(3, 5)
            copies.start(7, 9)
            keep[4].start()
            keep[5].start()

        xv = x_ref[...]
        shift1, scale1, gate1 = mod_ref[0:1, :], mod_ref[1:2, :], mod_ref[2:3, :]
        h1 = (xv * _rstd(xv) * n1pre_ref[...]) * (1.0 + scale1) + shift1
        z = _dot_nt(h1.astype(BF16), win_ref[...])
        z_ref[...] = z

        _, ga = _gelu_parts(z[:, :2 * D_A])
        u, vr = ga[:, :D_A], ga[:, D_A:]
        dv = vr - jnp.mean(vr, axis=-1, keepdims=True)
        v = (dv * lax.rsqrt(jnp.mean(dv * dv, axis=-1, keepdims=True) + EPS)) * lng_ref[...] + lnb_ref[...]
        vb = v.astype(BF16)
        mask = _tril_mask()
        wc = [(wsp_ref[h] * mask).astype(BF16) for h in range(N_HEADS)]
        for ch in range(tt // CHUNK):
            rows = slice(ch * CHUNK, (ch + 1) * CHUNK)
            for h in range(N_HEADS):
                cols = slice(h * GROUP, (h + 1) * GROUP)
                mixed = _dot(wc[h], vb[rows, cols]) + bs_ref[:, cols]
                cat_ref[rows, cols] = (u[rows, cols] * mixed).astype(BF16)

        zb = z[:, 2 * D_A:]
        diff = _pool_diff(zb, carry[...], i * tt)
        carry[...] = zb[tt - HALO:, :]
        for g in range(len(WINDOWS)):
            cols = slice(g * GROUP, (g + 1) * GROUP)
            pre = _dot(diff[g].astype(BF16), wp_ref[g].astype(BF16)) + bp_ref[:, cols]
            cat_ref[:, D_A + g * GROUP:D_A + (g + 1) * GROUP] = (pre * ps_ref[:, cols]).astype(BF16)

        mix = _dot(cat_ref[...], wout_ref[...])
        mix_ref[...] = mix
        x1_ref[...] = xv + gate1 * (mix * _rstd(mix) * n1post_ref[...])

        @pl.when(i == nt - 1)
        def _():
            copies.wait_recv(0, 1, 6, 7, 8, 9)
            copies.wait_send(*range(10))
            for cp in keep:
                cp.wait()

    tile = lambda w: pl.BlockSpec((tt, w), lambda i: (i, 0))
    hbm = pl.BlockSpec(memory_space=pl.ANY)
    outs = pl.pallas_call(
        body, name="attn_fwd", grid=(nt,),
        out_shape=tuple([jax.ShapeDtypeStruct((t_len, D_Z), F32), jax.ShapeDtypeStruct((t_len, D), BF16),
                         jax.ShapeDtypeStruct((t_len, D), F32), jax.ShapeDtypeStruct((t_len, D), F32)]
                        + [jax.ShapeDtypeStruct((FC_EARLY,) + s.shape, BF16) for s in fc_shards]),
        in_specs=[tile(D), _full((8, D)), _full((1, D)), _full((1, D)), _resident((D_Z, D)), _resident((D, D)),
                  _full((N_HEADS, CHUNK, CHUNK)), _full((CHUNK, D_A)), _full((1, D_A)), _full((1, D_A)),
                  _full((len(WINDOWS), GROUP, GROUP)), _full((1, D_B)), _full((1, D_B)),
                  _resident(fc_shards[0].shape), _resident(fc_shards[1].shape)],
        out_specs=(tile(D_Z), tile(D), tile(D), tile(D), hbm, hbm),
        scratch_shapes=[pltpu.VMEM((HALO, D_B), F32),
                        pltpu.VMEM((2,) + fc_shards[0].shape, BF16), pltpu.VMEM((2,) + fc_shards[1].shape, BF16),
                        pltpu.SemaphoreType.DMA((10,)), pltpu.SemaphoreType.DMA((10,)),
                        pltpu.SemaphoreType.DMA((6,))],
        compiler_params=pltpu.CompilerParams(dimension_semantics=("arbitrary",), vmem_limit_bytes=VMEM_LIMIT),
    )(x, mod, n1pre, n1post, w_in_t, w_out, w_sp, bs_rows, ln_g, ln_b, w_pool, b_pool, pool_scale, *fc_shards)
    return outs[:4], outs[4:]


def _mlp_fwd_early(tt, x1, mod, n2pre, w1_early, w2_early):
    t_len = x1.shape[0]
    nt = t_len // tt
    n_late = N_DEV - FC_EARLY

    def body(x1_ref, mod_ref, n2pre_ref, w1_ref, w2_ref, r_ref, h2_ref, f_ref, l1_ref, l2_ref,
             land1, land2, send_sems, recv_sems, local_sems):
        i = pl.program_id(0)
        copies = _Copies(
            [(w1_ref.at[2], land1, 4), (w2_ref.at[4], land2, 2),
             (land1, l1_ref.at[1], 1), (land2, l2_ref.at[1], 1)],
            send_sems, recv_sems)
        keep = [pltpu.make_async_copy(land1, l1_ref.at[0], local_sems.at[0]),
                pltpu.make_async_copy(land2, l2_ref.at[0], local_sems.at[1])]

        @pl.when(i == 0)
        def _():
            copies.start(0, 1)

        @pl.when(i == nt // 2)
        def _():
            copies.wait_recv(0, 1)
            copies.start(2, 3)
            for cp in keep:
                cp.start()

        x1v = x1_ref[...]
        shift2, scale2 = mod_ref[3:4, :], mod_ref[4:5, :]
        h2 = ((x1v * _rstd(x1v) * n2pre_ref[...]) * (1.0 + scale2) + shift2).astype(BF16)
        h2_ref[...] = h2
        for j in range(FC_EARLY):
            cols = slice(j * FF_BLK, (j + 1) * FF_BLK)
            ra = jnp.maximum(_dot(h2, w1_ref[j]), 0.0)
            r = (ra * ra).astype(BF16)
            r_ref[:, cols] = r
            contrib = _dot(r, w2_ref[j])
            if j == 0:
                f_ref[...] = contrib
            else:
                f_ref[...] += contrib

        @pl.when(i == nt - 1)
        def _():
            copies.wait_recv(2, 3)
            copies.wait_send(0, 1, 2, 3)
            for cp in keep:
                cp.wait()

    tile = lambda w: pl.BlockSpec((tt, w), lambda i: (i, 0))
    hbm = pl.BlockSpec(memory_space=pl.ANY)
    outs = pl.pallas_call(
        body, name="mlp_fwd_early", grid=(nt,),
        out_shape=(jax.ShapeDtypeStruct((t_len, FC_EARLY * FF_BLK), BF16),
                   jax.ShapeDtypeStruct((t_len, D), BF16), jax.ShapeDtypeStruct((t_len, D), F32),
                   jax.ShapeDtypeStruct((n_late,) + w1_early.shape[1:], BF16),
                   jax.ShapeDtypeStruct((n_late,) + w2_early.shape[1:], BF16)),
        in_specs=[tile(D), _full((8, D)), _full((1, D)),
                  _resident((FC_EARLY, D, FF_BLK)), _resident((FC_EARLY, FF_BLK, D))],
        out_specs=(tile(FC_EARLY * FF_BLK), tile(D), tile(D), hbm, hbm),
        scratch_shapes=[pltpu.VMEM(w1_early.shape[1:], BF16), pltpu.VMEM(w2_early.shape[1:], BF16),
                        pltpu.SemaphoreType.DMA((4,)), pltpu.SemaphoreType.DMA((4,)),
                        pltpu.SemaphoreType.DMA((2,))],
        compiler_params=pltpu.CompilerParams(dimension_semantics=("arbitrary",), vmem_limit_bytes=VMEM_LIMIT),
    )(x1, mod, n2pre, w1_early, w2_early)
    return outs[:3], outs[3:]


def _mlp_late_bwd(tt, r_early, x1, h2, f_early, tgt, mix, mod, n2pre, n2post, n1post,
                  w1_early, w2_early, w1_late, w2_late):
    t_len = x1.shape[0]
    nt = t_len // tt
    n_late = N_DEV - FC_EARLY
    late_cols = n_late * FF_BLK

    def body(re_ref, x1_ref, h2_ref, fe_ref, tgt_ref, mix_ref, mod_ref, n2pre_ref, n2post_ref,
             n1post_ref, w1e_ref, w2e_ref, w1l_ref, w2l_ref,
             rl_ref, df_ref, da_ref, dmix_ref, dx1_ref, redf_ref, redb_ref, dh2_acc):
        i = pl.program_id(0)

        @pl.when(i == 0)
        def _():
            redf_ref[...] = jnp.zeros_like(redf_ref)
            redb_ref[...] = jnp.zeros_like(redb_ref)

        x1v = x1_ref[...]
        gate1, scale2, gate2 = mod_ref[2:3, :], mod_ref[4:5, :], mod_ref[5:6, :]
        h2 = h2_ref[...]
        f = fe_ref[...]
        for j in range(n_late):
            cols = slice(j * FF_BLK, (j + 1) * FF_BLK)
            ra = jnp.maximum(_dot(h2, w1l_ref[j]), 0.0)
            r = (ra * ra).astype(BF16)
            rl_ref[:, cols] = r
            f = f + _dot(r, w2l_ref[j])
        rf = _rstd(f)
        fhat = f * rf
        nf = fhat * n2post_ref[...]
        err = (x1v + gate2 * nf) - tgt_ref[...]
        dy = err * (1.0 / D)
        dnf = dy * gate2
        dfv = _rms_bwd(dnf * n2post_ref[...], fhat, rf).astype(BF16)
        df_ref[...] = dfv
        redf_ref[0:1, :] += _colsum(dy * nf)
        redf_ref[1:2, :] += _colsum(dnf * fhat)
        redf_ref[2:3, :] += _colsum(0.5 * jnp.mean(err * err, axis=-1, keepdims=True)) * jnp.ones((1, D), F32)

        for j in range(N_DEV):
            cols = slice(j * FF_BLK, (j + 1) * FF_BLK)
            if j < FC_EARLY:
                w1, w2, r = w1e_ref[j], w2e_ref[j], re_ref[:, cols]
            else:
                jl = j - FC_EARLY
                w1, w2, r = w1l_ref[jl], w2l_ref[jl], rl_ref[:, jl * FF_BLK:(jl + 1) * FF_BLK]
            dr = _dot_nt(dfv, w2)
            da = (dr * (2.0 * jnp.sqrt(r.astype(F32)))).astype(BF16)
            da_ref[:, cols] = da
            contrib = _dot_nt(da, w1)
            if j == 0:
                dh2_acc[...] = contrib
            else:
                dh2_acc[...] += contrib
        dh2 = dh2_acc[...]
        r2 = _rstd(x1v)
        xhat = x1v * r2
        n2 = xhat * n2pre_ref[...]
        dn2 = dh2 * (1.0 + scale2)
        dx1 = dy + _rms_bwd(dn2 * n2pre_ref[...], xhat, r2)
        dx1_ref[...] = dx1
        mixv = mix_ref[...]
        rm = _rstd(mixv)
        mhat = mixv * rm
        dnm = dx1 * gate1
        dmix_ref[...] = _rms_bwd(dnm * n1post_ref[...], mhat, rm).astype(BF16)
        redb_ref[0:1, :] += _colsum(dh2)
        redb_ref[1:2, :] += _colsum(dh2 * n2)
        redb_ref[2:3, :] += _colsum(dn2 * xhat)
        redb_ref[3:4, :] += _colsum(dx1 * (mhat * n1post_ref[...]))
        redb_ref[4:5, :] += _colsum(dnm * mhat)

    tile = lambda w: pl.BlockSpec((tt, w), lambda i: (i, 0))
    return pl.pallas_call(
        body, name="mlp_late_bwd", grid=(nt,),
        out_shape=(jax.ShapeDtypeStruct((t_len, late_cols), BF16), jax.ShapeDtypeStruct((t_len, D), BF16),
                   jax.ShapeDtypeStruct((t_len, D_FF), BF16), jax.ShapeDtypeStruct((t_len, D), BF16),
                   jax.ShapeDtypeStruct((t_len, D), F32), jax.ShapeDtypeStruct((8, D), F32),
                   jax.ShapeDtypeStruct((8, D), F32)),
        in_specs=[tile(FC_EARLY * FF_BLK), tile(D), tile(D), tile(D), tile(D),
                  tile(D), _full((8, D)), _full((1, D)), _full((1, D)), _full((1, D)),
                  _resident((FC_EARLY, D, FF_BLK)), _resident((FC_EARLY, FF_BLK, D)),
                  _resident((n_late, D, FF_BLK)), _resident((n_late, FF_BLK, D))],
        out_specs=(tile(late_cols), tile(D), tile(D_FF), tile(D), tile(D), _full((8, D)), _full((8, D))),
        scratch_shapes=[pltpu.VMEM((tt, D), F32)],
        compiler_params=pltpu.CompilerParams(dimension_semantics=("arbitrary",), vmem_limit_bytes=VMEM_LIMIT),
    )(r_early, x1, h2, f_early, tgt, mix, mod, n2pre, n2post, n1post, w1_early, w2_early, w1_late, w2_late)


def _mlp_wgrad(tt, r_early, r_late, da, df, h2):
    t_len = df.shape[0]
    nt = t_len // tt
    odd_steps = [j for j, rel in enumerate(WGRAD_ORDER) if rel % 2]

    def relation(j):
        rel = jnp.int32(WGRAD_ORDER[-1])
        for step in range(N_DEV - 2, -1, -1):
            rel = jnp.where(j == step, WGRAD_ORDER[step], rel)
        return rel

    def body(re_ref, rl_ref, da_ref, df_ref, h2_ref, own1_ref, own2_ref, out1_ref, out2_ref, diag1_ref, diag2_ref,
             acc1, acc2, snd1, snd2, sib1, sib2, dsnd1, dsnd2, send_sems, recv_sems):
        j, t = pl.program_id(0), pl.program_id(1)
        rows = pl.ds(pl.multiple_of(t * tt, tt), tt)
        x, y, c = _place()
        accs, snds, sibs = (acc1, acc2), (snd1, snd2), (sib1, sib2)
        dsnds, diags = (dsnd1, dsnd2), (diag1_ref, diag2_ref)

        def to_sibling(a, jj, buf=0):
            return pltpu.make_async_remote_copy(
                src_ref=snds[a].at[buf], dst_ref=sibs[a].at[jj],
                send_sem=send_sems.at[4 * a + jj], recv_sem=recv_sems.at[4 * a + jj],
                device_id=(x, y, 1 - c), device_id_type=MESH)

        def to_diagonal(a):
            return pltpu.make_async_remote_copy(
                src_ref=dsnds[a], dst_ref=diags[a], send_sem=send_sems.at[8 + a], recv_sem=recv_sems.at[8 + a],
                device_id=_peer(x, y, c, 6), device_id_type=MESH)

        @pl.when(t == 0)
        def _():
            acc2[...] = jnp.zeros_like(acc2)
            acc1[...] = jnp.zeros_like(acc1)

        for r_ref, mine in ((re_ref, relation(j) < FC_EARLY), (rl_ref, relation(j) >= FC_EARLY)):
            @pl.when(mine)
            def _():
                acc2[...] += _dot_tn(r_ref[...], df_ref[rows, :])
                acc1[...] += _dot_tn(h2_ref[rows, :], da_ref[...])

        for step, rel in enumerate(WGRAD_ORDER):
            jj = rel // 2

            @pl.when((t == nt - 1) & (j == step))
            def _():
                for a, (own_ref, out_ref) in enumerate(((own1_ref, out1_ref), (own2_ref, out2_ref))):
                    if rel % 2:
                        q = odd_steps.index(step)
                        if q >= 2:
                            to_sibling(a, WGRAD_ORDER[odd_steps[q - 2]] // 2).wait_send()
                        snds[a][q % 2] = accs[a][...].astype(BF16)
                        to_sibling(a, jj, q % 2).start()
                        continue
                    to_sibling(a, jj).wait_recv()
                    chip_sum = accs[a][...] + sibs[a][jj].astype(F32)
                    if rel == 6:
                        dsnds[a][...] = chip_sum.astype(BF16)
                        to_diagonal(a).start()
                    elif rel == 0:
                        own_ref[...] = chip_sum
                    else:
                        out_ref[0] = chip_sum.astype(BF16)
                    if step == N_DEV - 1:
                        for q in (2, 3):
                            to_sibling(a, WGRAD_ORDER[odd_steps[q]] // 2).wait_send()
                        to_diagonal(a).wait_recv()
                        to_diagonal(a).wait_send()

    assert WGRAD_ORDER[-1] == 0 and WGRAD_ORDER[-3:-1] == (2, 4)
    blk = pl.BlockSpec((tt, FF_BLK), lambda j, t: (t, relation(j)))
    early = lambda j, t: (jnp.where(relation(j) < FC_EARLY, t, 0), jnp.where(relation(j) < FC_EARLY, relation(j), 0))
    late = lambda j, t: (jnp.where(relation(j) < FC_EARLY, 0, t), jnp.maximum(relation(j) - FC_EARLY, 0))
    chip = lambda j, t: (jnp.clip(j - 5, 0, 1), 0, 0)
    hbm = pl.BlockSpec(memory_space=pl.ANY)
    return pl.pallas_call(
        body, name="mlp_wgrad", grid=(N_DEV, nt),
        out_shape=(jax.ShapeDtypeStruct((D, FF_BLK), F32), jax.ShapeDtypeStruct((FF_BLK, D), F32),
                   jax.ShapeDtypeStruct((2, D, FF_BLK), BF16), jax.ShapeDtypeStruct((2, FF_BLK, D), BF16),
                   jax.ShapeDtypeStruct((D, FF_BLK), BF16), jax.ShapeDtypeStruct((FF_BLK, D), BF16)),
        in_specs=[pl.BlockSpec((tt, FF_BLK), early), pl.BlockSpec((tt, FF_BLK), late), blk,
                  _resident((t_len, D)), _resident((t_len, D))],
        out_specs=(_full((D, FF_BLK)), _full((FF_BLK, D)),
                   pl.BlockSpec((1, D, FF_BLK), chip), pl.BlockSpec((1, FF_BLK, D), chip), hbm, hbm),
        scratch_shapes=[pltpu.VMEM((D, FF_BLK), F32), pltpu.VMEM((FF_BLK, D), F32),
                        pltpu.VMEM((2, D, FF_BLK), BF16), pltpu.VMEM((2, FF_BLK, D), BF16),
                        pltpu.VMEM((4, D, FF_BLK), BF16), pltpu.VMEM((4, FF_BLK, D), BF16),
                        pltpu.VMEM((D, FF_BLK), BF16), pltpu.VMEM((FF_BLK, D), BF16),
                        pltpu.SemaphoreType.DMA((10,)), pltpu.SemaphoreType.DMA((10,))],
        compiler_params=pltpu.CompilerParams(dimension_semantics=("arbitrary", "arbitrary"),
                                             vmem_limit_bytes=VMEM_LIMIT),
    )(r_early, r_late, da, df, h2)


def _acc_rows(ref, row0, k, val):
    half = CHUNK // 2
    ref[row0:row0 + half, k * GROUP:(k + 1) * GROUP] += val[:half, :]
    ref[row0:row0 + half, D_A + k * GROUP:D_A + (k + 1) * GROUP] += val[half:, :]


def _attn_bwd(tt, dmix, dx1, x, z, cat, mod, n1pre, w_in_t, w_out, w_sp, bs_rows, ln_g, ln_b, w_pool, b_pool,
              pool_scale, red_fwd, red_bwd, chip_sums):
    t_len = x.shape[0]
    nt = t_len // tt
    hb = tt // HALO
    n_sums = len(chip_sums)

    def body(dmix_ref, dx1_ref, x_ref, z_ref, zprev_ref, cat_ref, mod_ref, n1pre_ref, win_ref, wout_ref, wsp_ref,
             bs_ref, lng_ref, lnb_ref, wp_ref, bp_ref, ps_ref, redf_ref, redb_ref, *rest):
        sum_out = rest[:n_sums]
        gx_ref, gwin_ref, gwout_ref, small_ref = rest[n_sums:n_sums + 4]
        sum_in = rest[n_sums + 4:2 * n_sums + 4]
        carry, acc_in, acc_out, dz_scr, bs_acc, send_sems, recv_sems = rest[2 * n_sums + 4:]
        s = pl.program_id(0)
        i = nt - 1 - s
        px, py, pc = _place()

        def chip_copy(a, r):
            return pltpu.make_async_remote_copy(
                src_ref=sum_out[a].at[r], dst_ref=sum_in[a].at[r],
                send_sem=send_sems.at[2 * a + r], recv_sem=recv_sems.at[2 * a + r],
                device_id=_peer(px, py, pc, 2 * (r + 1)), device_id_type=MESH)

        @pl.when(s == 0)
        def _():
            for a in range(n_sums):
                for r in range(2):
                    chip_copy(a, r).start()
            carry[...] = jnp.zeros_like(carry)
            acc_in[...] = jnp.zeros_like(acc_in)
            acc_out[...] = jnp.zeros_like(acc_out)
            bs_acc[...] = jnp.zeros_like(bs_acc)
            small_ref[...] = jnp.zeros_like(small_ref)
            small_ref[ROW_DMOD + 2:ROW_DMOD + 3, :] = redb_ref[3:4, :]
            small_ref[ROW_DMOD + 3:ROW_DMOD + 5, :] = redb_ref[0:2, :]
            small_ref[ROW_DMOD + 5:ROW_DMOD + 6, :] = redf_ref[0:1, :]
            small_ref[ROW_N1POST:ROW_N1POST + 1, :] = redb_ref[4:5, :]
            small_ref[ROW_N2PRE:ROW_N2PRE + 1, :] = redb_ref[2:3, :]
            small_ref[ROW_N2POST:ROW_N2POST + 1, :] = redf_ref[1:2, :]
            small_ref[ROW_LOSS:ROW_LOSS + 1, :] = redf_ref[2:3, :]

        dmixv = dmix_ref[...]
        dcat = _dot_nt(dmixv, wout_ref[...])
        acc_out[...] += _dot_tn(cat_ref[...], dmixv)

        z = z_ref[...]
        t_g, ga = _gelu_parts(z[:, :2 * D_A])
        u, vr = ga[:, :D_A], ga[:, D_A:]
        dv0 = vr - jnp.mean(vr, axis=-1, keepdims=True)
        rv = lax.rsqrt(jnp.mean(dv0 * dv0, axis=-1, keepdims=True) + EPS)
        vhat = dv0 * rv
        vb = (vhat * lng_ref[...] + lnb_ref[...]).astype(BF16)
        mask = _tril_mask()
        wc = [(wsp_ref[h] * mask).astype(BF16) for h in range(N_HEADS)]

        dya = dcat[:, :D_A]
        for h in range(N_HEADS):
            cols = slice(h * GROUP, (h + 1) * GROUP)
            bs_sum = jnp.zeros((CHUNK, GROUP), F32)
            ws_sum = jnp.zeros((CHUNK, CHUNK), F32)
            for ch in range(tt // CHUNK):
                rows = slice(ch * CHUNK, (ch + 1) * CHUNK)
                v_ch = vb[rows, cols]
                mixed = _dot(wc[h], v_ch) + bs_ref[:, cols]
                dy_ch = dya[rows, cols]
                dz_scr[rows, cols] = dy_ch * mixed
                dmixed = dy_ch * u[rows, cols]
                dmb = dmixed.astype(BF16)
                dz_scr[rows, D_A + h * GROUP:D_A + (h + 1) * GROUP] = _dot_tn(wc[h], dmb)
                bs_sum = bs_sum + dmixed
                ws_sum = ws_sum + _dot_nt(dmb, v_ch)
            _acc_rows(bs_acc, 0, h, bs_sum)
            _acc_rows(small_ref, ROW_WS, h, ws_sum)

        dvl = dz_scr[:, D_A:2 * D_A]
        dvhat = dvl * lng_ref[...]
        dvr = rv * (dvhat - jnp.mean(dvhat, axis=-1, keepdims=True)
                    - vhat * jnp.mean(dvhat * vhat, axis=-1, keepdims=True))
        small_ref[ROW_LN:ROW_LN + 1, 0:D_A] += _colsum(dvl * vhat)
        small_ref[ROW_LN:ROW_LN + 1, D_A:D] += _colsum(dvl)
        dga = jnp.concatenate([dz_scr[:, :D_A], dvr], axis=1)
        dza = dga * _gelu_grad(z[:, :2 * D_A], t_g)

        zb = z[:, 2 * D_A:]
        halo_prev = jnp.where(i == 0, 0.0, zprev_ref[...])
        diff = _pool_diff(zb, halo_prev, i * tt)
        dyb = dcat[:, D_A:]
        inv = _inv_counts(i * tt, tt)
        scaled, ddiffs = [], []
        for g in range(len(WINDOWS)):
            cols = slice(g * GROUP, (g + 1) * GROUP)
            db = diff[g].astype(BF16)
            wpg = wp_ref[g].astype(BF16)
            pre = _dot(db, wpg) + bp_ref[:, cols]
            small_ref[ROW_POOL:ROW_POOL + 1, cols] += _colsum(dyb[:, cols] * pre)
            dpre = dyb[:, cols] * ps_ref[:, cols]
            small_ref[ROW_POOL:ROW_POOL + 1, D_B + g * GROUP:D_B + (g + 1) * GROUP] += _colsum(dpre)
            dpb = dpre.astype(BF16)
            _acc_rows(small_ref, ROW_WP, g, _dot_tn(db, dpb))
            ddiff = _dot_nt(dpb, wpg)
            ddiffs.append(ddiff)
            scaled.append(ddiff * inv[g])
        scaled_all = jnp.concatenate(scaled, axis=1)
        ext = jnp.concatenate([scaled_all, carry[...]], axis=0)
        n_ext = tt + HALO
        s2 = ext + pltpu.roll(ext, n_ext - 1, 0)
        t4 = s2[:, GROUP:]
        s4 = t4 + pltpu.roll(t4, n_ext - 2, 0)
        t8 = s4[:, GROUP:]
        s8 = t8 + pltpu.roll(t8, n_ext - 4, 0)
        t16 = s8[:, GROUP:]
        s16 = t16 + pltpu.roll(t16, n_ext - 8, 0)
        back = [s2[:, :GROUP], s4[:, :GROUP], s8[:, :GROUP], s16]
        carry[...] = scaled_all[:HALO, :]
        dzb = jnp.concatenate([back[g][:tt, :] - ddiffs[g] for g in range(len(WINDOWS))], axis=1)

        dzv = jnp.concatenate([dza, dzb], axis=1).astype(BF16)
        dh1 = _dot(dzv, win_ref[...])
        xv = x_ref[...]
        r1 = _rstd(xv)
        xhat = xv * r1
        shift1, scale1 = mod_ref[0:1, :], mod_ref[1:2, :]
        n1 = xhat * n1pre_ref[...]
        h1 = (n1 * (1.0 + scale1) + shift1).astype(BF16)
        acc_in[...] += _dot_tn(dzv, h1)
        dn1 = dh1 * (1.0 + scale1)
        gx_ref[...] = dx1_ref[...] + _rms_bwd(dn1 * n1pre_ref[...], xhat, r1)
        small_ref[ROW_DMOD:ROW_DMOD + 1, :] += _colsum(dh1)
        small_ref[ROW_DMOD + 1:ROW_DMOD + 2, :] += _colsum(dh1 * n1)
        small_ref[ROW_N1PRE:ROW_N1PRE + 1, :] += _colsum(dn1 * xhat)

        @pl.when(s == nt - 1)
        def _():
            gwin_ref[...] = acc_in[...].astype(BF16)
            gwout_ref[...] = acc_out[...].astype(BF16)
            bs = _unfold(bs_acc[...])
            for h in range(N_HEADS):
                small_ref[ROW_BS + h:ROW_BS + h + 1, 0:GROUP] = jnp.sum(
                    bs[:, h * GROUP:(h + 1) * GROUP].T, axis=0, keepdims=True)
            for a in range(n_sums):
                for r in range(2):
                    chip_copy(a, r).wait_recv()
                    chip_copy(a, r).wait_send()

    rev = lambda w: pl.BlockSpec((tt, w), lambda s: (nt - 1 - s, 0))
    zprev = pl.BlockSpec((HALO, D_B), lambda s: (jnp.maximum((nt - 1 - s) * hb - 1, 0), 2))
    hbm = pl.BlockSpec(memory_space=pl.ANY)
    outs = pl.pallas_call(
        body, name="attn_bwd", grid=(nt,),
        out_shape=tuple([jax.ShapeDtypeStruct((t_len, D), F32), jax.ShapeDtypeStruct((D_Z, D), BF16),
                         jax.ShapeDtypeStruct((D, D), BF16), jax.ShapeDtypeStruct((SMALL_ROWS, D), F32)]
                        + [jax.ShapeDtypeStruct(cs.shape, cs.dtype) for cs in chip_sums]),
        in_specs=[rev(D), rev(D), rev(D), rev(D_Z), zprev, rev(D), _full((8, D)), _full((1, D)),
                  _resident((D_Z, D)), _resident((D, D)), _full((N_HEADS, CHUNK, CHUNK)), _full((CHUNK, D_A)),
                  _full((1, D_A)), _full((1, D_A)), _full((len(WINDOWS), GROUP, GROUP)), _full((1, D_B)),
                  _full((1, D_B)), _full((8, D)), _full((8, D))] + [_resident(cs.shape) for cs in chip_sums],
        out_specs=tuple([rev(D), _resident((D_Z, D)), _resident((D, D)), _full((SMALL_ROWS, D))] + [hbm] * n_sums),
        scratch_shapes=[pltpu.VMEM((HALO, D_B), F32), pltpu.VMEM((D_Z, D), F32), pltpu.VMEM((D, D), F32),
                        pltpu.VMEM((tt, 2 * D_A), F32), pltpu.VMEM((CHUNK // 2, D), F32),
                        pltpu.SemaphoreType.DMA((2 * n_sums,)), pltpu.SemaphoreType.DMA((2 * n_sums,))],
        compiler_params=pltpu.CompilerParams(dimension_semantics=("arbitrary",), vmem_limit_bytes=VMEM_LIMIT),
    )(dmix, dx1, x, z, z, cat, mod, n1pre, w_in_t, w_out, w_sp, bs_rows, ln_g, ln_b, w_pool, b_pool, pool_scale,
      red_fwd, red_bwd, *chip_sums)
    return outs[:4], outs[4:]


def _adam(w, g, m, v):
    m2 = ADAM_B1 * m + (1.0 - ADAM_B1) * g
    v2 = ADAM_B2 * v + (1.0 - ADAM_B2) * (g * g)
    m_hat = m2 / (1.0 - ADAM_B1 ** ADAM_STEP)
    v_hat = v2 / (1.0 - ADAM_B2 ** ADAM_STEP)
    delta = -ADAM_LR * (m_hat / (jnp.sqrt(v_hat) + ADAM_EPS) + ADAM_WD * w)
    return delta, m2, v2


def _adamw_shard(name, rb, w, g, m, v):
    rows, cols = w.shape

    def body(w_ref, g_ref, m_ref, v_ref, d_ref, m2_ref, v2_ref):
        d_ref[...], m2_ref[...], v2_ref[...] = _adam(w_ref[...], g_ref[...], m_ref[...], v_ref[...])

    blk = pl.BlockSpec((rb, cols), lambda i: (i, 0))
    shp = jax.ShapeDtypeStruct((rows, cols), F32)
    return pl.pallas_call(
        body, name=name, grid=(rows // rb,), out_shape=(shp, shp, shp),
        in_specs=[blk] * 4, out_specs=(blk, blk, blk),
        compiler_params=pltpu.CompilerParams(dimension_semantics=("arbitrary",)),
    )(w, g, m, v)


def _adamw_ada(rb, w, sc, dmod_cols, m, v):
    rows, cols = w.shape

    def body(w_ref, sc_ref, dm_ref, m_ref, v_ref, g_ref, d_ref, m2_ref, v2_ref):
        g = _dot_tn(sc_ref[...].astype(BF16), dm_ref[...].astype(BF16))
        g_ref[...] = g
        d_ref[...], m2_ref[...], v2_ref[...] = _adam(w_ref[...], g, m_ref[...], v_ref[...])

    blk = pl.BlockSpec((rb, cols), lambda i: (i, 0))
    shp = jax.ShapeDtypeStruct((rows, cols), F32)
    return pl.pallas_call(
        body, name="adamw_ada", grid=(rows // rb,), out_shape=(shp, shp, shp, shp),
        in_specs=[blk, pl.BlockSpec((N_DEV, rb), lambda i: (0, i)), _full((N_DEV, cols)), blk, blk],
        out_specs=(blk, blk, blk, blk),
        compiler_params=pltpu.CompilerParams(dimension_semantics=("arbitrary",)),
    )(w, sc, dmod_cols, m, v)


def _unfold(acc_rows):
    return jnp.concatenate([acc_rows[:, :D_A], acc_rows[:, D_A:]], axis=0)


def _adamw_small(total, params):
    n = len(params)
    flat = [a for p in params for a in p]

    def body(*refs):
        s_ref = refs[0]
        p_refs = refs[1:1 + 3 * n]
        loss_ref = refs[1 + 3 * n]
        o_refs = refs[2 + 3 * n:]
        d_b_ada = s_ref[0:6, :]
        for b in range(1, N_DEV):
            d_b_ada = d_b_ada + s_ref[8 * b:8 * b + 6, :]
        tot = s_ref[PACK_SHIFT:PACK_ROWS, :]
        loss_ref[...] = jnp.broadcast_to(tot[ROW_LOSS:ROW_LOSS + 1, 0:GROUP], (8, GROUP))
        mask = _tril_mask()
        ws = _unfold(tot[ROW_WS:ROW_WS + 64, :])
        wp = _unfold(tot[ROW_WP:ROW_WP + 64, :])
        grads = [
            d_b_ada,
            tot[ROW_N1PRE:ROW_N1PRE + 1, :], tot[ROW_N1POST:ROW_N1POST + 1, :],
            tot[ROW_N2PRE:ROW_N2PRE + 1, :], tot[ROW_N2POST:ROW_N2POST + 1, :],
            tot[ROW_LN:ROW_LN + 1, :D_A], tot[ROW_LN:ROW_LN + 1, D_A:],
            tot[ROW_POOL:ROW_POOL + 1, :D_B], tot[ROW_POOL:ROW_POOL + 1, D_B:],
            tot[ROW_BS:ROW_BS + N_HEADS, 0:GROUP],
            jnp.stack([ws[:, h * GROUP:(h + 1) * GROUP] * mask for h in range(N_HEADS)]),
            jnp.stack([wp[:, g * GROUP:(g + 1) * GROUP] for g in range(len(WINDOWS))]),
        ]
        for k in range(n):
            w_ref, m_ref, v_ref = p_refs[3 * k:3 * k + 3]
            g = grads[k]
            o_refs[4 * k][...] = g
            o_refs[4 * k + 1][...], o_refs[4 * k + 2][...], o_refs[4 * k + 3][...] = _adam(
                w_ref[...], g, m_ref[...], v_ref[...])

    vm = pl.BlockSpec(memory_space=pltpu.VMEM)
    out_shape = [jax.ShapeDtypeStruct((8, GROUP), F32)]
    for w, _, _ in params:
        out_shape += [jax.ShapeDtypeStruct(w.shape, F32)] * 4
    return pl.pallas_call(
        body, name="adamw_small", out_shape=tuple(out_shape),
        in_specs=[vm] * (1 + 3 * n), out_specs=tuple([vm] * len(out_shape)),
    )(total, *flat)


TT_ATTN_FWD = 512
TT_MLP_FWD = 512
TT_MLP = 256
TT_WGRAD = 2048
TT_ATTN_BWD = 512


def kernel(x, c, w_ada, b_ada, norm1_pre, norm1_post, w_in, w_spatial, b_spatial, ln_v_gain, ln_v_bias, w_pool, b_pool, pool_scale, w_out, norm2_pre, norm2_post, w_fc1, w_fc2, loss_target, m_w_ada, m_b_ada, m_norm1_pre, m_norm1_post, m_w_in, m_w_spatial, m_b_spatial, m_ln_v_gain, m_ln_v_bias, m_w_pool, m_b_pool, m_pool_scale, m_w_out, m_norm2_pre, m_norm2_post, m_w_fc1, m_w_fc2, v_w_ada, v_b_ada, v_norm1_pre, v_norm1_post, v_w_in, v_w_spatial, v_b_spatial, v_ln_v_gain, v_ln_v_bias, v_w_pool, v_b_pool, v_pool_scale, v_w_out, v_norm2_pre, v_norm2_post, v_w_fc1, v_w_fc2):
    t_len = x.shape[1]
    me = 4 * lax.axis_index("x") + 2 * lax.axis_index("y") + lax.axis_index("c")
    ada_cols = w_ada.shape[1]
    tt = lambda want: min(want, t_len)

    x2 = x.reshape(t_len, D)
    tgt = loss_target.reshape(t_len, D)
    row = lambda a: a.reshape(1, -1)

    b_my = lax.dynamic_slice_in_dim(b_ada, me * ada_cols, ada_cols).reshape(1, ada_cols)
    modp, sc, (g_in, g_out), fc_shards = _fwd_comm(jnp.broadcast_to(c, (8, D)), w_ada, b_my,
                                                   [w_in.T, w_out], [w_fc1, w_fc2])
    mod = jnp.concatenate([modp.reshape(6, D), jnp.zeros((2, D), F32)], axis=0)
    w_in_t = g_in.reshape(D_Z, D)
    w_out_all = g_out.reshape(D, D)

    bs_rows = jnp.repeat(b_spatial.T, GROUP, axis=1)
    attn_consts = (w_spatial, bs_rows, row(ln_v_gain), row(ln_v_bias), w_pool, row(b_pool), row(pool_scale))

    (z, cat, mix, x1), (w1_early, w2_early) = _attn_fwd(
        tt(TT_ATTN_FWD), x2, mod, row(norm1_pre), row(norm1_post), w_in_t, w_out_all, *attn_consts, fc_shards)
    (r_early, h2, f_early), (w1_late, w2_late) = _mlp_fwd_early(
        tt(TT_MLP_FWD), x1, mod, row(norm2_pre), w1_early, w2_early)
    r_late, df, da, dmix, dx1, red_fwd, red_bwd = _mlp_late_bwd(
        tt(TT_MLP), r_early, x1, h2, f_early, tgt, mix, mod, row(norm2_pre), row(norm2_post), row(norm1_post),
        w1_early, w2_early, w1_late, w2_late)
    own_w1, own_w2, sums_w1, sums_w2, diag_w1, diag_w2 = _mlp_wgrad(tt(TT_WGRAD), r_early, r_late, da, df, h2)
    (grad_x, p_in, p_out, small), (arr_w1, arr_w2) = _attn_bwd(
        tt(TT_ATTN_BWD), dmix, dx1, x2, z, cat, mod, row(norm1_pre), w_in_t, w_out_all, *attn_consts,
        red_fwd, red_bwd, [sums_w1, sums_w2])
    (grad_in_t, grad_out, total), ((grad_w1, d_w1, m_w1, v_w1), (grad_w2, d_w2, m_w2, v_w2)) = _tail_comm(
        [p_in.reshape(N_DEV, D_Z // N_DEV, D), p_out.reshape(N_DEV, D // N_DEV, D)], small, 64,
        [(w_fc1, own_w1, arr_w1, diag_w1, m_w_fc1, v_w_fc1), (w_fc2, own_w2, arr_w2, diag_w2, m_w_fc2, v_w_fc2)])

    d_out, m_out, v_out = _adamw_shard("adamw_out", 128, w_out, grad_out, m_w_out, v_w_out)
    d_in_t, m_in_t, v_in_t = _adamw_shard("adamw_in", D_Z // N_DEV, w_in.T, grad_in_t, m_w_in.T, v_w_in.T)
    dmod_all = total[0:TABLE_ROWS, :].reshape(N_DEV, 8, D)[:, :6, :].reshape(N_DEV, 6 * D)
    dmod_cols = lax.dynamic_slice_in_dim(dmod_all, me * ada_cols, ada_cols, axis=1)
    grad_ada, d_ada, m_ada, v_ada = _adamw_ada(256, w_ada, sc, dmod_cols, m_w_ada, v_w_ada)

    six = lambda a: a.reshape(6, D)
    small_params = [
        (six(b_ada), six(m_b_ada), six(v_b_ada)),
        (row(norm1_pre), row(m_norm1_pre), row(v_norm1_pre)),
        (row(norm1_post), row(m_norm1_post), row(v_norm1_post)),
        (row(norm2_pre), row(m_norm2_pre), row(v_norm2_pre)),
        (row(norm2_post), row(m_norm2_post), row(v_norm2_post)),
        (row(ln_v_gain), row(m_ln_v_gain), row(v_ln_v_gain)),
        (row(ln_v_bias), row(m_ln_v_bias), row(v_ln_v_bias)),
        (row(pool_scale), row(m_pool_scale), row(v_pool_scale)),
        (row(b_pool), row(m_b_pool), row(v_b_pool)),
        (b_spatial, m_b_spatial, v_b_spatial),
        (w_spatial, m_w_spatial, v_w_spatial),
        (w_pool, m_w_pool, v_w_pool),
    ]
    outs = _adamw_small(total, small_params)
    loss = outs[0][0, 0]
    names = ["b_ada", "norm1_pre", "norm1_post", "norm2_pre", "norm2_post", "ln_v_gain", "ln_v_bias", "pool_scale",
             "b_pool", "b_spatial", "w_spatial", "w_pool"]
    shapes = dict(b_ada=b_ada.shape, norm1_pre=norm1_pre.shape, norm1_post=norm1_post.shape,
                  norm2_pre=norm2_pre.shape, norm2_post=norm2_post.shape, ln_v_gain=ln_v_gain.shape,
                  ln_v_bias=ln_v_bias.shape, pool_scale=pool_scale.shape, b_pool=b_pool.shape,
                  b_spatial=b_spatial.shape, w_spatial=w_spatial.shape, w_pool=w_pool.shape)
    res = {}
    for k, nm in enumerate(names):
        res[nm] = tuple(o.reshape(shapes[nm]) for o in outs[1 + 4 * k:5 + 4 * k])
    res["w_ada"] = (grad_ada, d_ada, m_ada, v_ada)
    res["w_in"] = (grad_in_t.T, d_in_t.T, m_in_t.T, v_in_t.T)
    res["w_out"] = (grad_out, d_out, m_out, v_out)
    res["w_fc1"] = (grad_w1, d_w1, m_w1, v_w1)
    res["w_fc2"] = (grad_w2, d_w2, m_w2, v_w2)

    order = ["w_ada", "b_ada", "norm1_pre", "norm1_post", "w_in", "w_spatial", "b_spatial", "ln_v_gain", "ln_v_bias",
             "w_pool", "b_pool", "pool_scale", "w_out", "norm2_pre", "norm2_post", "w_fc1", "w_fc2"]
    return (loss, grad_x.reshape(x.shape),
            *[res[nm][0] for nm in order], *[res[nm][1] for nm in order],
            *[res[nm][2] for nm in order], *[res[nm][3] for nm in order])
```

```python
import functools

import jax
import jax.numpy as jnp
from jax import lax
from jax.experimental import pallas as pl
from jax.experimental.pallas import tpu as pltpu

F32 = jnp.float32
BF16 = jnp.bfloat16
MESH = pl.DeviceIdType.MESH

N_DEV = 8
D = 1024
D_A = 512
D_B = 512
D_Z = 2 * D_A + D_B
N_HEADS = 4
CHUNK = 128
WINDOWS = (2, 4, 8, 16)
GROUP = 128
D_FF = 4096
FF_BLK = D_FF // N_DEV
HALO = 16
EPS = 1e-6
VMEM_LIMIT = 60 * 1024 * 1024

ADAM_LR = 0.001
ADAM_B1 = 0.9
ADAM_B2 = 0.999
ADAM_EPS = 1e-08
ADAM_WD = 0.01
ADAM_STEP = 10

ROW_DMOD = 0
ROW_N1PRE, ROW_N1POST, ROW_N2PRE, ROW_N2POST = 8, 9, 10, 11
ROW_LN = 12
ROW_POOL = 13
ROW_LOSS = 14
ROW_BS = 16
ROW_WS = 24
ROW_WP = 88
SMALL_ROWS = 152
TABLE_ROWS = 8 * N_DEV
PACK_SHIFT = TABLE_ROWS - 8
PACK_ROWS = SMALL_ROWS + PACK_SHIFT
PACK_HALF = PACK_ROWS // 2


def _dot(a, b):
    return jnp.dot(a, b, preferred_element_type=F32)


def _dot_nt(a, b):
    return lax.dot_general(a, b, (((1,), (1,)), ((), ())), preferred_element_type=F32)


def _dot_tn(a, b):
    return lax.dot_general(a, b, (((0,), (0,)), ((), ())), preferred_element_type=F32)


def _rstd(v):
    return lax.rsqrt(jnp.mean(v * v, axis=-1, keepdims=True) + EPS)


def _rms_bwd(d_hat, hat, rstd):
    return rstd * (d_hat - hat * jnp.mean(d_hat * hat, axis=-1, keepdims=True))


_K0 = 0.7978845608028654
_K1 = 0.044715


def _gelu_parts(v):
    t = jnp.tanh(_K0 * (v + _K1 * (v * v * v)))
    return t, v * (0.5 * (1.0 + t))


def _gelu_grad(v, t):
    return 0.5 * (1.0 + t) + (0.5 * v) * (1.0 - t * t) * (_K0 * (1.0 + (3.0 * _K1) * (v * v)))


def _colsum(v):
    return jnp.sum(v, axis=0, keepdims=True)


def _full(shape):
    n = len(shape)
    return pl.BlockSpec(shape, lambda *_: (0,) * n)


def _resident(shape):
    n = len(shape)
    return pl.BlockSpec(shape, lambda *_: (0,) * n, pipeline_mode=pl.Buffered(1))


def _place():
    x, y, c = lax.axis_index("x"), lax.axis_index("y"), lax.axis_index("c")
    return x, y, c


def _flip(v, bit):
    return 1 - v if bit else v


def _peer(x, y, c, k):
    return (_flip(x, (k >> 2) & 1), _flip(y, (k >> 1) & 1), _flip(c, k & 1))


def _index(p):
    return 4 * p[0] + 2 * p[1] + p[2]


def _two_level_gather_begin(x, y, c, out_refs, send_sems, recv_sems):
    me = (x, y, c)
    sibling = (x, y, 1 - c)
    chips = [(1 - x, y), (x, 1 - y), (1 - x, 1 - y)]

    def copy(a, k, block, to):
        ref = out_refs[a].at[_index(block)]
        return pltpu.make_async_remote_copy(
            src_ref=ref, dst_ref=ref, send_sem=send_sems.at[7 * a + k], recv_sem=recv_sems.at[7 * a + k],
            device_id=to, device_id_type=MESH)

    first = []
    for a in range(len(out_refs)):
        first.append(copy(a, 0, me, sibling))
        first += [copy(a, 1 + j, me, (*chip, c)) for j, chip in enumerate(chips)]
    for cp in first:
        cp.start()
    return copy, first, me, sibling, chips


def _two_level_gather_finish(c, n, begun):
    copy, first, me, sibling, chips = begun
    passed = []
    for a in range(n):
        for j, chip in enumerate(chips):
            copy(a, 1 + j, (*chip, c), me).wait_recv()
            fwd = copy(a, 4 + j, (*chip, c), sibling)
            fwd.start()
            passed.append(fwd)
    for a in range(n):
        copy(a, 0, sibling, me).wait_recv()
        for j, chip in enumerate(chips):
            copy(a, 4 + j, (*chip, 1 - c), me).wait_recv()
    for cp in first + passed:
        cp.wait_send()


def _fwd_comm(c8, w_ada, b_my, gathered, kept):
    ncol = w_ada.shape[1]
    n_g, n_k = len(gathered), len(kept)

    def body(c_ref, w_ref, b_ref, *rest):
        g_in, k_in = rest[:n_g], rest[n_g:n_g + n_k]
        modp_ref, sc_ref = rest[n_g + n_k:n_g + n_k + 2]
        g_out = rest[n_g + n_k + 2:2 * n_g + n_k + 2]
        k_out = rest[2 * n_g + n_k + 2:2 * n_g + 2 * n_k + 2]
        cg, mg, part, send_sems, recv_sems, g_send, g_recv = rest[2 * n_g + 2 * n_k + 2:]
        x, y, c = _place()
        me = _index((x, y, c))
        for a in range(n_g):
            g_out[a][me] = g_in[a][...].astype(BF16)
        begun = _two_level_gather_begin(x, y, c, g_out, g_send, g_recv)
        for a in range(n_k):
            k_out[a][...] = k_in[a][...].astype(BF16)

        def c_copy(k):
            p = _peer(x, y, c, k)
            return pltpu.make_async_remote_copy(
                src_ref=c_ref, dst_ref=cg.at[me], send_sem=send_sems.at[k - 1], recv_sem=recv_sems.at[k - 1],
                device_id=p, device_id_type=MESH)

        def c_arrival(k):
            p = _peer(x, y, c, k)
            return pltpu.make_async_remote_copy(
                src_ref=c_ref, dst_ref=cg.at[_index(p)], send_sem=send_sems.at[k - 1], recv_sem=recv_sems.at[k - 1],
                device_id=p, device_id_type=MESH)

        def m_copy(k):
            p = _peer(x, y, c, k)
            return pltpu.make_async_remote_copy(
                src_ref=part, dst_ref=mg.at[me], send_sem=send_sems.at[6 + k], recv_sem=recv_sems.at[6 + k],
                device_id=p, device_id_type=MESH)

        def m_arrival(k):
            p = _peer(x, y, c, k)
            return pltpu.make_async_remote_copy(
                src_ref=part, dst_ref=mg.at[_index(p)], send_sem=send_sems.at[6 + k], recv_sem=recv_sems.at[6 + k],
                device_id=p, device_id_type=MESH)

        for k in range(1, N_DEV):
            c_copy(k).start()
        cg[me] = c_ref[...]
        for k in range(1, N_DEV):
            c_arrival(k).wait_recv()
        c_all = jnp.concatenate([cg[j, 0:1, :] for j in range(N_DEV)], axis=0)
        sc = c_all * jax.nn.sigmoid(c_all)
        sc_ref[...] = sc
        part[...] = _dot(sc.astype(BF16), w_ref[...].astype(BF16)) + b_ref[...]
        for k in range(1, N_DEV):
            m_copy(k).start()
        mg[me] = part[...]
        for k in range(1, N_DEV):
            m_arrival(k).wait_recv()
        for j in range(N_DEV):
            modp_ref[j:j + 1, :] = mg[j, pl.ds(me, 1), :]
        _two_level_gather_finish(c, n_g, begun)
        for k in range(1, N_DEV):
            c_copy(k).wait_send()
            m_copy(k).wait_send()

    vm = pl.BlockSpec(memory_space=pltpu.VMEM)
    outs = pl.pallas_call(
        body, name="fwd_comm",
        out_shape=tuple([jax.ShapeDtypeStruct((N_DEV, ncol), F32), jax.ShapeDtypeStruct((N_DEV, D), F32)]
                        + [jax.ShapeDtypeStruct((N_DEV,) + s.shape, BF16) for s in gathered]
                        + [jax.ShapeDtypeStruct(s.shape, BF16) for s in kept]),
        in_specs=[vm] * (3 + n_g + n_k), out_specs=tuple([vm] * (2 + n_g + n_k)),
        scratch_shapes=[
            pltpu.VMEM((N_DEV, 8, D), F32),
            pltpu.VMEM((N_DEV, N_DEV, ncol), F32),
            pltpu.VMEM((N_DEV, ncol), F32),
            pltpu.SemaphoreType.DMA((2 * (N_DEV - 1),)),
            pltpu.SemaphoreType.DMA((2 * (N_DEV - 1),)),
            pltpu.SemaphoreType.DMA((7 * n_g,)),
            pltpu.SemaphoreType.DMA((7 * n_g,)),
        ],
        compiler_params=pltpu.CompilerParams(vmem_limit_bytes=VMEM_LIMIT),
    )(c8, w_ada, b_my, *gathered, *kept)
    return outs[0], outs[1], outs[2:2 + n_g], outs[2 + n_g:]


FC_EARLY = 6
WGRAD_ORDER = (7, 6, 1, 3, 5, 2, 4, 0)


class _Copies:
    def __init__(self, entries, send_sems, recv_sems):
        self.place = _place()
        self.entries, self.send_sems, self.recv_sems = entries, send_sems, recv_sems

    def _copy(self, i, arrival=False):
        src, dst, rel = self.entries[i]
        return pltpu.make_async_remote_copy(
            src_ref=dst if arrival else src, dst_ref=dst, send_sem=self.send_sems.at[i],
            recv_sem=self.recv_sems.at[i], device_id=_peer(*self.place, rel), device_id_type=MESH)

    def start(self, *which):
        for i in which:
            self._copy(i).start()

    def wait_recv(self, *which):
        for i in which:
            self._copy(i, arrival=True).wait_recv()

    def wait_send(self, *which):
        for i in which:
            self._copy(i).wait_send()


TAIL_STEPS = 8


def _tail_comm(parts, small, row_chunk, fc):
    n, n_fc = len(parts), len(fc)

    def body(*refs):
        p_refs, small_ref = refs[:n], refs[n]
        fc_in = refs[n + 1:n + 1 + 6 * n_fc]
        outs = refs[n + 1 + 6 * n_fc:]
        g_refs, total_ref = outs[:n], outs[n]
        fc_out = outs[n + 1:n + 1 + 4 * n_fc]
        scr = outs[n + 1 + 4 * n_fc:]
        from_sib = scr[0:n]
        chip_out = scr[n:2 * n]
        chip_in = scr[2 * n:3 * n]
        pack, pack_sib, halves, total_scr = scr[3 * n:3 * n + 4]
        send_a, recv_a, send_b, recv_b, send_s, recv_s = scr[3 * n + 4:]
        step = pl.program_id(0)
        x, y, c = _place()
        me = _index((x, y, c))
        sibling = (x, y, 1 - c)
        my_chip = 2 * x + y
        others = [(1 - x, y), (x, 1 - y), (1 - x, 1 - y)]
        my_half = pl.ds(pl.multiple_of(PACK_HALF * c, 8), PACK_HALF)

        def pack_to_sibling():
            return pltpu.make_async_remote_copy(
                src_ref=pack, dst_ref=pack_sib, send_sem=send_s.at[0], recv_sem=recv_s.at[0],
                device_id=sibling, device_id_type=MESH)

        def half_to_chip(r):
            return pltpu.make_async_remote_copy(
                src_ref=halves.at[my_chip], dst_ref=halves.at[my_chip],
                send_sem=send_s.at[1 + r], recv_sem=recv_s.at[1 + r],
                device_id=(*others[r], c), device_id_type=MESH)

        def half_from_chip(r):
            k = 2 * others[r][0] + others[r][1]
            return pltpu.make_async_remote_copy(
                src_ref=halves.at[k], dst_ref=halves.at[k], send_sem=send_s.at[1 + r], recv_sem=recv_s.at[1 + r],
                device_id=(*others[r], c), device_id_type=MESH)

        def total_to_sibling():
            return pltpu.make_async_remote_copy(
                src_ref=total_scr.at[my_half], dst_ref=total_scr.at[my_half],
                send_sem=send_s.at[4], recv_sem=recv_s.at[4], device_id=sibling, device_id_type=MESH)

        def total_from_sibling():
            sib_half = pl.ds(pl.multiple_of(PACK_HALF * (1 - c), 8), PACK_HALF)
            return pltpu.make_async_remote_copy(
                src_ref=total_scr.at[sib_half], dst_ref=total_scr.at[sib_half],
                send_sem=send_s.at[4], recv_sem=recv_s.at[4], device_id=sibling, device_id_type=MESH)

        def to_sibling(a, k):
            return pltpu.make_async_remote_copy(
                src_ref=p_refs[a].at[2 * k + (1 - c)], dst_ref=from_sib[a].at[k],
                send_sem=send_a.at[a], recv_sem=recv_a.at[a], device_id=sibling, device_id_type=MESH)

        def all_from_sibling(a):
            return pltpu.make_async_remote_copy(
                src_ref=from_sib[a], dst_ref=from_sib[a], send_sem=send_a.at[a], recv_sem=recv_a.at[a],
                device_id=sibling, device_id_type=MESH)

        def to_chip(a, r):
            return pltpu.make_async_remote_copy(
                src_ref=chip_out[a].at[r], dst_ref=chip_in[a].at[r],
                send_sem=send_b.at[3 * a + r], recv_sem=recv_b.at[3 * a + r],
                device_id=(*others[r], c), device_id_type=MESH)

        @pl.when(step == 0)
        def _():
            pack[0:TABLE_ROWS, :] = jnp.zeros((TABLE_ROWS, D), F32)
            pack[pl.ds(pl.multiple_of(8 * me, 8), 8), :] = small_ref[0:8, :]
            pack[TABLE_ROWS:PACK_ROWS, :] = small_ref[8:SMALL_ROWS, :]
            pack_to_sibling().start()
            for a in range(n):
                for k in range(4):
                    to_sibling(a, k).start()

        @pl.when(step == 1)
        def _():
            pack_to_sibling().wait_recv()
            halves[my_chip] = pack[my_half, :] + pack_sib[my_half, :]
            for r in range(3):
                half_to_chip(r).start()
            for a in range(n):
                all_from_sibling(a).wait_recv()
                rows = p_refs[a].shape[1]
                for r in range(3):
                    k = 2 * others[r][0] + others[r][1]
                    for s in range(0, rows, row_chunk):
                        sl = pl.ds(s, row_chunk)
                        chip_out[a][r, sl, :] = (p_refs[a][2 * k + c, sl, :].astype(F32)
                                                 + from_sib[a][k, sl, :].astype(F32)).astype(BF16)
                    to_chip(a, r).start()
                for s in range(0, rows, row_chunk):
                    sl = pl.ds(s, row_chunk)
                    g_refs[a][sl, :] = (p_refs[a][2 * my_chip + c, sl, :].astype(F32)
                                        + from_sib[a][my_chip, sl, :].astype(F32))

        for k in range(n_fc):
            w_ref, own_ref, arr_ref, diag_ref, m_ref, v_ref = fc_in[6 * k:6 * k + 6]
            g = own_ref[...]
            for r in range(2):
                g = g + arr_ref[r].astype(F32)
            g = g + diag_ref[...].astype(F32)
            fc_out[4 * k][...] = g
            fc_out[4 * k + 1][...], fc_out[4 * k + 2][...], fc_out[4 * k + 3][...] = _adam(
                w_ref[...], g, m_ref[...], v_ref[...])

        @pl.when(step == TAIL_STEPS - 1)
        def _():
            for r in range(3):
                half_from_chip(r).wait_recv()
            total_scr[my_half, :] = ((halves[0] + halves[1]) + halves[2]) + halves[3]
            total_to_sibling().start()
            for a in range(n):
                rows = p_refs[a].shape[1]
                for r in range(3):
                    to_chip(a, r).wait_recv()
                    for s in range(0, rows, row_chunk):
                        sl = pl.ds(s, row_chunk)
                        g_refs[a][sl, :] = g_refs[a][sl, :] + chip_in[a][r, sl, :].astype(F32)
            total_from_sibling().wait_recv()
            total_ref[...] = total_scr[...]
            for a in range(n):
                all_from_sibling(a).wait_send()
                for r in range(3):
                    to_chip(a, r).wait_send()
            pack_to_sibling().wait_send()
            for r in range(3):
                half_to_chip(r).wait_send()
            total_to_sibling().wait_send()

    fc_specs_in, fc_specs_out, fc_shapes, fc_args = [], [], [], []
    for w, own, arrived, diagonal, m, v in fc:
        rows, cols = w.shape
        blk = pl.BlockSpec((rows // TAIL_STEPS, cols), lambda i: (i, 0))
        fc_specs_in += [blk, blk, pl.BlockSpec((2, rows // TAIL_STEPS, cols), lambda i: (0, i, 0)), blk, blk, blk]
        fc_specs_out += [blk] * 4
        fc_shapes += [jax.ShapeDtypeStruct((rows, cols), F32)] * 4
        fc_args += [w, own, arrived, diagonal, m, v]
    outs = pl.pallas_call(
        body, name="tail_comm", grid=(TAIL_STEPS,),
        out_shape=tuple([jax.ShapeDtypeStruct(p.shape[1:], F32) for p in parts]
                        + [jax.ShapeDtypeStruct((PACK_ROWS, D), F32)] + fc_shapes),
        in_specs=[_resident(p.shape) for p in parts] + [_resident(small.shape)] + fc_specs_in,
        out_specs=tuple([_full(p.shape[1:]) for p in parts] + [_full((PACK_ROWS, D))] + fc_specs_out),
        scratch_shapes=(
            [pltpu.VMEM((4,) + p.shape[1:], BF16) for p in parts]
            + [pltpu.VMEM((3,) + p.shape[1:], BF16) for p in parts]
            + [pltpu.VMEM((3,) + p.shape[1:], BF16) for p in parts]
            + [pltpu.VMEM((PACK_ROWS, D), F32), pltpu.VMEM((PACK_ROWS, D), F32),
               pltpu.VMEM((4, PACK_HALF, D), F32), pltpu.VMEM((PACK_ROWS, D), F32)]
            + [pltpu.SemaphoreType.DMA((n,)), pltpu.SemaphoreType.DMA((n,)),
               pltpu.SemaphoreType.DMA((3 * n,)), pltpu.SemaphoreType.DMA((3 * n,)),
               pltpu.SemaphoreType.DMA((5,)), pltpu.SemaphoreType.DMA((5,))]),
        compiler_params=pltpu.CompilerParams(dimension_semantics=("arbitrary",), vmem_limit_bytes=VMEM_LIMIT),
    )(*parts, small, *fc_args)
    return outs[:n + 1], [outs[n + 1 + 4 * k:n + 5 + 4 * k] for k in range(n_fc)]


def _tril_mask():
    row = lax.broadcasted_iota(jnp.int32, (CHUNK, CHUNK), 0)
    col = lax.broadcasted_iota(jnp.int32, (CHUNK, CHUNK), 1)
    return (col <= row).astype(F32)


def _window_sums(ext):
    s2 = ext + pltpu.roll(ext, 1, 0)
    t4 = s2[:, GROUP:]
    s4 = t4 + pltpu.roll(t4, 2, 0)
    t8 = s4[:, GROUP:]
    s8 = t8 + pltpu.roll(t8, 4, 0)
    t16 = s8[:, GROUP:]
    s16 = t16 + pltpu.roll(t16, 8, 0)
    return [s2[:, :GROUP], s4[:, :GROUP], s8[:, :GROUP], s16]


def _inv_counts(first_pos, rows):
    pos = first_pos + lax.broadcasted_iota(jnp.int32, (rows, 1), 0)
    return [1.0 / jnp.minimum(pos + 1, w).astype(F32) for w in WINDOWS]


def _pool_diff(zb, halo, first_pos):
    tt = zb.shape[0]
    sums = _window_sums(jnp.concatenate([halo, zb], axis=0))
    inv = _inv_counts(first_pos, tt)
    return [sums[g][HALO:, :] * inv[g] - zb[:, g * GROUP:(g + 1) * GROUP] for g in range(len(WINDOWS))]


def _attn_fwd(tt, x, mod, n1pre, n1post, w_in_t, w_out, w_sp, bs_rows, ln_g, ln_b, w_pool, b_pool, pool_scale,
              fc_shards):
    t_len = x.shape[0]
    nt = t_len // tt

    def body(x_ref, mod_ref, n1pre_ref, n1post_ref, win_ref, wout_ref, wsp_ref, bs_ref, lng_ref, lnb_ref,
             wp_ref, bp_ref, ps_ref, w1_ref, w2_ref, z_ref, cat_ref, mix_ref, x1_ref, e1_ref, e2_ref,
             carry, land1, land2, send_sems, recv_sems, local_sems):
        i = pl.program_id(0)
        copies = _Copies(
            [(w1_ref, e1_ref.at[1], 1), (w2_ref, e2_ref.at[1], 1),
             (w1_ref, land1.at[0], 2), (w2_ref, land2.at[0], 2),
             (w1_ref, land1.at[1], 4), (w2_ref, land2.at[1], 4),
             (land1.at[0], e1_ref.at[3], 1), (land2.at[0], e2_ref.at[3], 1),
             (land1.at[1], e1_ref.at[5], 1), (land2.at[1], e2_ref.at[5], 1)],
            send_sems, recv_sems)
        keep = [pltpu.make_async_copy(w1_ref, e1_ref.at[0], local_sems.at[0]),
                pltpu.make_async_copy(w2_ref, e2_ref.at[0], local_sems.at[1]),
                pltpu.make_async_copy(land1.at[0], e1_ref.at[2], local_sems.at[2]),
                pltpu.make_async_copy(land1.at[1], e1_ref.at[4], local_sems.at[3]),
                pltpu.make_async_copy(land2.at[0], e2_ref.at[2], local_sems.at[4]),
                pltpu.make_async_copy(land2.at[1], e2_ref.at[4], local_sems.at[5])]

        @pl.when(i == 0)
        def _():
            copies.start(2, 4, 3, 5, 0, 1)
            keep[0].start()
            keep[1].start()
            carry[...] = jnp.zeros_like(carry)

        @pl.when(i == nt // 2)
        def _():
            copies.wait_recv(2, 4)
            copies.start(6, 8)
            keep[2].start()
            keep[3].start()

        @pl.when(i == nt - 1)
        def _():
            copies.wait_recv(3, 5)
            copies.start(7, 9)
            keep[4].start()
            keep[5].start()

        xv = x_ref[...]
        shift1, scale1, gate1 = mod_ref[0:1, :], mod_ref[1:2, :], mod_ref[2:3, :]
        h1 = (xv * _rstd(xv)) * (n1pre_ref[...] * (1.0 + scale1)) + shift1
        z = _dot_nt(h1.astype(BF16), win_ref[...])
        z_ref[...] = z

        _, ga = _gelu_parts(z[:, :2 * D_A])
        u, vr = ga[:, :D_A], ga[:, D_A:]
        dv = vr - jnp.mean(vr, axis=-1, keepdims=True)
        v = (dv * lax.rsqrt(jnp.mean(dv * dv, axis=-1, keepdims=True) + EPS)) * lng_ref[...] + lnb_ref[...]
        vb = v.astype(BF16)
        mask = _tril_mask()
        wc = [(wsp_ref[h] * mask).astype(BF16) for h in range(N_HEADS)]
        for ch in range(tt // CHUNK):
            rows = slice(ch * CHUNK, (ch + 1) * CHUNK)
            for h in range(N_HEADS):
                cols = slice(h * GROUP, (h + 1) * GROUP)
                mixed = _dot(wc[h], vb[rows, cols]) + bs_ref[:, cols]
                cat_ref[rows, cols] = (u[rows, cols] * mixed).astype(BF16)

        zb = z[:, 2 * D_A:]
        diff = _pool_diff(zb, carry[...], i * tt)
        carry[...] = zb[tt - HALO:, :]
        for g in range(len(WINDOWS)):
            cols = slice(g * GROUP, (g + 1) * GROUP)
            pre = _dot(diff[g].astype(BF16), wp_ref[g].astype(BF16)) + bp_ref[:, cols]
            cat_ref[:, D_A + g * GROUP:D_A + (g + 1) * GROUP] = (pre * ps_ref[:, cols]).astype(BF16)

        mix = _dot(cat_ref[...], wout_ref[...])
        mix_ref[...] = mix
        x1_ref[...] = xv + (mix * _rstd(mix)) * (gate1 * n1post_ref[...])

        @pl.when(i == nt - 1)
        def _():
            copies.wait_recv(0, 1, 6, 7, 8, 9)
            copies.wait_send(*range(10))
            for cp in keep:
                cp.wait()

    tile = lambda w: pl.BlockSpec((tt, w), lambda i: (i, 0))
    hbm = pl.BlockSpec(memory_space=pl.ANY)
    outs = pl.pallas_call(
        body, name="attn_fwd", grid=(nt,),
        out_shape=tuple([jax.ShapeDtypeStruct((t_len, D_Z), F32), jax.ShapeDtypeStruct((t_len, D), BF16),
                         jax.ShapeDtypeStruct((t_len, D), F32), jax.ShapeDtypeStruct((t_len, D), F32)]
                        + [jax.ShapeDtypeStruct((FC_EARLY,) + s.shape, BF16) for s in fc_shards]),
        in_specs=[tile(D), _full((8, D)), _full((1, D)), _full((1, D)), _resident((D_Z, D)), _resident((D, D)),
                  _full((N_HEADS, CHUNK, CHUNK)), _full((CHUNK, D_A)), _full((1, D_A)), _full((1, D_A)),
                  _full((len(WINDOWS), GROUP, GROUP)), _full((1, D_B)), _full((1, D_B)),
                  _resident(fc_shards[0].shape), _resident(fc_shards[1].shape)],
        out_specs=(tile(D_Z), tile(D), tile(D), tile(D), hbm, hbm),
        scratch_shapes=[pltpu.VMEM((HALO, D_B), F32),
                        pltpu.VMEM((2,) + fc_shards[0].shape, BF16), pltpu.VMEM((2,) + fc_shards[1].shape, BF16),
                        pltpu.SemaphoreType.DMA((10,)), pltpu.SemaphoreType.DMA((10,)),
                        pltpu.SemaphoreType.DMA((6,))],
        compiler_params=pltpu.CompilerParams(dimension_semantics=("arbitrary",), vmem_limit_bytes=VMEM_LIMIT),
    )(x, mod, n1pre, n1post, w_in_t, w_out, w_sp, bs_rows, ln_g, ln_b, w_pool, b_pool, pool_scale, *fc_shards)
    return outs[:4], outs[4:]


def _mlp_fwd_early(tt, x1, mod, n2pre, w1_early, w2_early):
    t_len = x1.shape[0]
    nt = t_len // tt
    n_late = N_DEV - FC_EARLY

    def body(x1_ref, mod_ref, n2pre_ref, w1_ref, w2_ref, r_ref, h2_ref, f_ref, l1_ref, l2_ref,
             land1, land2, send_sems, recv_sems, local_sems):
        i = pl.program_id(0)
        copies = _Copies(
            [(w1_ref.at[2], land1, 4), (w2_ref.at[4], land2, 2),
             (land1, l1_ref.at[1], 1), (land2, l2_ref.at[1], 1)],
            send_sems, recv_sems)
        keep = [pltpu.make_async_copy(land1, l1_ref.at[0], local_sems.at[0]),
                pltpu.make_async_copy(land2, l2_ref.at[0], local_sems.at[1])]

        @pl.when(i == 0)
        def _():
            copies.start(0, 1)

        @pl.when(i == nt // 2)
        def _():
            copies.wait_recv(0, 1)
            copies.start(2, 3)
            for cp in keep:
                cp.start()

        x1v = x1_ref[...]
        shift2, scale2 = mod_ref[3:4, :], mod_ref[4:5, :]
        h2 = ((x1v * _rstd(x1v)) * (n2pre_ref[...] * (1.0 + scale2)) + shift2).astype(BF16)
        h2_ref[...] = h2
        for j in range(FC_EARLY):
            cols = slice(j * FF_BLK, (j + 1) * FF_BLK)
            ra = jnp.maximum(_dot(h2, w1_ref[j]), 0.0)
            r = (ra * ra).astype(BF16)
            r_ref[:, cols] = r
            contrib = _dot(r, w2_ref[j])
            if j == 0:
                f_ref[...] = contrib
            else:
                f_ref[...] += contrib

        @pl.when(i == nt - 1)
        def _():
            copies.wait_recv(2, 3)
            copies.wait_send(0, 1, 2, 3)
            for cp in keep:
                cp.wait()

    tile = lambda w: pl.BlockSpec((tt, w), lambda i: (i, 0))
    hbm = pl.BlockSpec(memory_space=pl.ANY)
    outs = pl.pallas_call(
        body, name="mlp_fwd_early", grid=(nt,),
        out_shape=(jax.ShapeDtypeStruct((t_len, FC_EARLY * FF_BLK), BF16),
                   jax.ShapeDtypeStruct((t_len, D), BF16), jax.ShapeDtypeStruct((t_len, D), F32),
                   jax.ShapeDtypeStruct((n_late,) + w1_early.shape[1:], BF16),
                   jax.ShapeDtypeStruct((n_late,) + w2_early.shape[1:], BF16)),
        in_specs=[tile(D), _full((8, D)), _full((1, D)),
                  _resident((FC_EARLY, D, FF_BLK)), _resident((FC_EARLY, FF_BLK, D))],
        out_specs=(tile(FC_EARLY * FF_BLK), tile(D), tile(D), hbm, hbm),
        scratch_shapes=[pltpu.VMEM(w1_early.shape[1:], BF16), pltpu.VMEM(w2_early.shape[1:], BF16),
                        pltpu.SemaphoreType.DMA((4,)), pltpu.SemaphoreType.DMA((4,)),
                        pltpu.SemaphoreType.DMA((2,))],
        compiler_params=pltpu.CompilerParams(dimension_semantics=("arbitrary",), vmem_limit_bytes=VMEM_LIMIT),
    )(x1, mod, n2pre, w1_early, w2_early)
    return outs[:3], outs[3:]


def _mlp_late_bwd(tt, r_early, x1, h2, f_early, tgt, mix, mod, n2pre, n2post, n1post,
                  w1_early, w2_early, w1_late, w2_late):
    t_len = x1.shape[0]
    nt = t_len // tt
    n_late = N_DEV - FC_EARLY
    late_cols = n_late * FF_BLK

    def body(re_ref, x1_ref, h2_ref, fe_ref, tgt_ref, mix_ref, mod_ref, n2pre_ref, n2post_ref,
             n1post_ref, w1e_ref, w2e_ref, w1l_ref, w2l_ref,
             rl_ref, df_ref, da_ref, dmix_ref, dx1_ref, redf_ref, redb_ref, dh2_acc):
        i = pl.program_id(0)

        @pl.when(i == 0)
        def _():
            redf_ref[...] = jnp.zeros_like(redf_ref)
            redb_ref[...] = jnp.zeros_like(redb_ref)

        x1v = x1_ref[...]
        gate1, scale2, gate2 = mod_ref[2:3, :], mod_ref[4:5, :], mod_ref[5:6, :]
        h2 = h2_ref[...]
        f = fe_ref[...]
        for j in range(n_late):
            cols = slice(j * FF_BLK, (j + 1) * FF_BLK)
            ra = jnp.maximum(_dot(h2, w1l_ref[j]), 0.0)
            r = (ra * ra).astype(BF16)
            rl_ref[:, cols] = r
            f = f + _dot(r, w2l_ref[j])
        post2 = n2post_ref[...]
        gate_post2 = gate2 * post2
        rf = _rstd(f)
        fhat = f * rf
        err = (x1v + fhat * gate_post2) - tgt_ref[...]
        dy = err * (1.0 / D)
        dfv = _rms_bwd(dy * gate_post2, fhat, rf).astype(BF16)
        df_ref[...] = dfv
        sum_f = _colsum(dy * fhat)
        redf_ref[0:1, :] += post2 * sum_f
        redf_ref[1:2, :] += gate2 * sum_f
        redf_ref[2:3, :] += _colsum(0.5 * jnp.mean(err * err, axis=-1, keepdims=True)) * jnp.ones((1, D), F32)

        for j in range(N_DEV):
            cols = slice(j * FF_BLK, (j + 1) * FF_BLK)
            if j < FC_EARLY:
                w1, w2, r = w1e_ref[j], w2e_ref[j], re_ref[:, cols]
            else:
                jl = j - FC_EARLY
                w1, w2, r = w1l_ref[jl], w2l_ref[jl], rl_ref[:, jl * FF_BLK:(jl + 1) * FF_BLK]
            dr = _dot_nt(dfv, w2)
            da = (dr * (2.0 * jnp.sqrt(r.astype(F32)))).astype(BF16)
            da_ref[:, cols] = da
            contrib = _dot_nt(da, w1)
            if j == 0:
                dh2_acc[...] = contrib
            else:
                dh2_acc[...] += contrib
        dh2 = dh2_acc[...]
        pre2, post1 = n2pre_ref[...], n1post_ref[...]
        r2 = _rstd(x1v)
        xhat = x1v * r2
        dx1 = dy + _rms_bwd(dh2 * (pre2 * (1.0 + scale2)), xhat, r2)
        dx1_ref[...] = dx1
        mixv = mix_ref[...]
        rm = _rstd(mixv)
        mhat = mixv * rm
        dmix_ref[...] = _rms_bwd(dx1 * (gate1 * post1), mhat, rm).astype(BF16)
        sum_h, sum_m = _colsum(dh2 * xhat), _colsum(dx1 * mhat)
        redb_ref[0:1, :] += _colsum(dh2)
        redb_ref[1:2, :] += pre2 * sum_h
        redb_ref[2:3, :] += (1.0 + scale2) * sum_h
        redb_ref[3:4, :] += post1 * sum_m
        redb_ref[4:5, :] += gate1 * sum_m

    tile = lambda w: pl.BlockSpec((tt, w), lambda i: (i, 0))
    return pl.pallas_call(
        body, name="mlp_late_bwd", grid=(nt,),
        out_shape=(jax.ShapeDtypeStruct((t_len, late_cols), BF16), jax.ShapeDtypeStruct((t_len, D), BF16),
                   jax.ShapeDtypeStruct((t_len, D_FF), BF16), jax.ShapeDtypeStruct((t_len, D), BF16),
                   jax.ShapeDtypeStruct((t_len, D), F32), jax.ShapeDtypeStruct((8, D), F32),
                   jax.ShapeDtypeStruct((8, D), F32)),
        in_specs=[tile(FC_EARLY * FF_BLK), tile(D), tile(D), tile(D), tile(D),
                  tile(D), _full((8, D)), _full((1, D)), _full((1, D)), _full((1, D)),
                  _resident((FC_EARLY, D, FF_BLK)), _resident((FC_EARLY, FF_BLK, D)),
                  _resident((n_late, D, FF_BLK)), _resident((n_late, FF_BLK, D))],
        out_specs=(tile(late_cols), tile(D), tile(D_FF), tile(D), tile(D), _full((8, D)), _full((8, D))),
        scratch_shapes=[pltpu.VMEM((tt, D), F32)],
        compiler_params=pltpu.CompilerParams(dimension_semantics=("arbitrary",), vmem_limit_bytes=VMEM_LIMIT),
    )(r_early, x1, h2, f_early, tgt, mix, mod, n2pre, n2post, n1post, w1_early, w2_early, w1_late, w2_late)


def _mlp_wgrad(tt, r_early, r_late, da, df, h2):
    t_len = df.shape[0]
    nt = t_len // tt
    odd_steps = [j for j, rel in enumerate(WGRAD_ORDER) if rel % 2]

    def relation(j):
        rel = jnp.int32(WGRAD_ORDER[-1])
        for step in range(N_DEV - 2, -1, -1):
            rel = jnp.where(j == step, WGRAD_ORDER[step], rel)
        return rel

    def body(re_ref, rl_ref, da_ref, df_ref, h2_ref, own1_ref, own2_ref, out1_ref, out2_ref, diag1_ref, diag2_ref,
             acc1, acc2, snd1, snd2, sib1, sib2, dsnd1, dsnd2, send_sems, recv_sems):
        j, t = pl.program_id(0), pl.program_id(1)
        rows = pl.ds(pl.multiple_of(t * tt, tt), tt)
        x, y, c = _place()
        accs, snds, sibs = (acc1, acc2), (snd1, snd2), (sib1, sib2)
        dsnds, diags = (dsnd1, dsnd2), (diag1_ref, diag2_ref)

        def to_sibling(a, jj, buf=0):
            return pltpu.make_async_remote_copy(
                src_ref=snds[a].at[buf], dst_ref=sibs[a].at[jj],
                send_sem=send_sems.at[4 * a + jj], recv_sem=recv_sems.at[4 * a + jj],
                device_id=(x, y, 1 - c), device_id_type=MESH)

        def to_diagonal(a):
            return pltpu.make_async_remote_copy(
                src_ref=dsnds[a], dst_ref=diags[a], send_sem=send_sems.at[8 + a], recv_sem=recv_sems.at[8 + a],
                device_id=_peer(x, y, c, 6), device_id_type=MESH)

        @pl.when(t == 0)
        def _():
            acc2[...] = jnp.zeros_like(acc2)
            acc1[...] = jnp.zeros_like(acc1)

        for r_ref, mine in ((re_ref, relation(j) < FC_EARLY), (rl_ref, relation(j) >= FC_EARLY)):
            @pl.when(mine)
            def _():
                acc2[...] += _dot_tn(r_ref[...], df_ref[rows, :])
                acc1[...] += _dot_tn(h2_ref[rows, :], da_ref[...])

        for step, rel in enumerate(WGRAD_ORDER):
            jj = rel // 2

            @pl.when((t == nt - 1) & (j == step))
            def _():
                for a, (own_ref, out_ref) in enumerate(((own1_ref, out1_ref), (own2_ref, out2_ref))):
                    if rel % 2:
                        q = odd_steps.index(step)
                        if q >= 2:
                            to_sibling(a, WGRAD_ORDER[odd_steps[q - 2]] // 2).wait_send()
                        snds[a][q % 2] = accs[a][...].astype(BF16)
                        to_sibling(a, jj, q % 2).start()
                        continue
                    to_sibling(a, jj).wait_recv()
                    chip_sum = accs[a][...] + sibs[a][jj].astype(F32)
                    if rel == 6:
                        dsnds[a][...] = chip_sum.astype(BF16)
                        to_diagonal(a).start()
                    elif rel == 0:
                        own_ref[...] = chip_sum
                    else:
                        out_ref[0] = chip_sum.astype(BF16)
                    if step == N_DEV - 1:
                        for q in (2, 3):
                            to_sibling(a, WGRAD_ORDER[odd_steps[q]] // 2).wait_send()
                        to_diagonal(a).wait_recv()
                        to_diagonal(a).wait_send()

    assert WGRAD_ORDER[-1] == 0 and WGRAD_ORDER[-3:-1] == (2, 4)
    blk = pl.BlockSpec((tt, FF_BLK), lambda j, t: (t, relation(j)))
    early = lambda j, t: (jnp.where(relation(j) < FC_EARLY, t, 0), jnp.where(relation(j) < FC_EARLY, relation(j), 0))
    late = lambda j, t: (jnp.where(relation(j) < FC_EARLY, 0, t), jnp.maximum(relation(j) - FC_EARLY, 0))
    chip = lambda j, t: (jnp.clip(j - 5, 0, 1), 0, 0)
    hbm = pl.BlockSpec(memory_space=pl.ANY)
    return pl.pallas_call(
        body, name="mlp_wgrad", grid=(N_DEV, nt),
        out_shape=(jax.ShapeDtypeStruct((D, FF_BLK), F32), jax.ShapeDtypeStruct((FF_BLK, D), F32),
                   jax.ShapeDtypeStruct((2, D, FF_BLK), BF16), jax.ShapeDtypeStruct((2, FF_BLK, D), BF16),
                   jax.ShapeDtypeStruct((D, FF_BLK), BF16), jax.ShapeDtypeStruct((FF_BLK, D), BF16)),
        in_specs=[pl.BlockSpec((tt, FF_BLK), early), pl.BlockSpec((tt, FF_BLK), late), blk,
                  _resident((t_len, D)), _resident((t_len, D))],
        out_specs=(_full((D, FF_BLK)), _full((FF_BLK, D)),
                   pl.BlockSpec((1, D, FF_BLK), chip), pl.BlockSpec((1, FF_BLK, D), chip), hbm, hbm),
        scratch_shapes=[pltpu.VMEM((D, FF_BLK), F32), pltpu.VMEM((FF_BLK, D), F32),
                        pltpu.VMEM((2, D, FF_BLK), BF16), pltpu.VMEM((2, FF_BLK, D), BF16),
                        pltpu.VMEM((4, D, FF_BLK), BF16), pltpu.VMEM((4, FF_BLK, D), BF16),
                        pltpu.VMEM((D, FF_BLK), BF16), pltpu.VMEM((FF_BLK, D), BF16),
                        pltpu.SemaphoreType.DMA((10,)), pltpu.SemaphoreType.DMA((10,))],
        compiler_params=pltpu.CompilerParams(dimension_semantics=("arbitrary", "arbitrary"),
                                             vmem_limit_bytes=VMEM_LIMIT),
    )(r_early, r_late, da, df, h2)


def _acc_rows(ref, row0, k, val):
    half = CHUNK // 2
    ref[row0:row0 + half, k * GROUP:(k + 1) * GROUP] += val[:half, :]
    ref[row0:row0 + half, D_A + k * GROUP:D_A + (k + 1) * GROUP] += val[half:, :]


def _attn_bwd(tt, dmix, dx1, x, z, cat, mod, n1pre, w_in_t, w_out, w_sp, bs_rows, ln_g, ln_b, w_pool, b_pool,
              pool_scale, red_fwd, red_bwd, chip_sums):
    t_len = x.shape[0]
    nt = t_len // tt
    hb = tt // HALO
    n_sums = len(chip_sums)

    def body(dmix_ref, dx1_ref, x_ref, z_ref, zprev_ref, cat_ref, mod_ref, n1pre_ref, win_ref, wout_ref, wsp_ref,
             bs_ref, lng_ref, lnb_ref, wp_ref, bp_ref, ps_ref, redf_ref, redb_ref, *rest):
        sum_out = rest[:n_sums]
        gx_ref, gwin_ref, gwout_ref, small_ref = rest[n_sums:n_sums + 4]
        sum_in = rest[n_sums + 4:2 * n_sums + 4]
        carry, acc_in, acc_out, dz_scr, bs_acc, send_sems, recv_sems = rest[2 * n_sums + 4:]
        s = pl.program_id(0)
        i = nt - 1 - s
        px, py, pc = _place()

        def chip_copy(a, r):
            return pltpu.make_async_remote_copy(
                src_ref=sum_out[a].at[r], dst_ref=sum_in[a].at[r],
                send_sem=send_sems.at[2 * a + r], recv_sem=recv_sems.at[2 * a + r],
                device_id=_peer(px, py, pc, 2 * (r + 1)), device_id_type=MESH)

        @pl.when(s == 0)
        def _():
            for a in range(n_sums):
                for r in range(2):
                    chip_copy(a, r).start()
            carry[...] = jnp.zeros_like(carry)
            acc_in[...] = jnp.zeros_like(acc_in)
            acc_out[...] = jnp.zeros_like(acc_out)
            bs_acc[...] = jnp.zeros_like(bs_acc)
            small_ref[...] = jnp.zeros_like(small_ref)
            small_ref[ROW_DMOD + 2:ROW_DMOD + 3, :] = redb_ref[3:4, :]
            small_ref[ROW_DMOD + 3:ROW_DMOD + 5, :] = redb_ref[0:2, :]
            small_ref[ROW_DMOD + 5:ROW_DMOD + 6, :] = redf_ref[0:1, :]
            small_ref[ROW_N1POST:ROW_N1POST + 1, :] = redb_ref[4:5, :]
            small_ref[ROW_N2PRE:ROW_N2PRE + 1, :] = redb_ref[2:3, :]
            small_ref[ROW_N2POST:ROW_N2POST + 1, :] = redf_ref[1:2, :]
            small_ref[ROW_LOSS:ROW_LOSS + 1, :] = redf_ref[2:3, :]

        dmixv = dmix_ref[...]
        dcat = _dot_nt(dmixv, wout_ref[...])
        acc_out[...] += _dot_tn(cat_ref[...], dmixv)

        z = z_ref[...]
        t_g, ga = _gelu_parts(z[:, :2 * D_A])
        u, vr = ga[:, :D_A], ga[:, D_A:]
        dv0 = vr - jnp.mean(vr, axis=-1, keepdims=True)
        rv = lax.rsqrt(jnp.mean(dv0 * dv0, axis=-1, keepdims=True) + EPS)
        vhat = dv0 * rv
        vb = (vhat * lng_ref[...] + lnb_ref[...]).astype(BF16)
        mask = _tril_mask()
        wc = [(wsp_ref[h] * mask).astype(BF16) for h in range(N_HEADS)]

        dya = dcat[:, :D_A]
        for h in range(N_HEADS):
            cols = slice(h * GROUP, (h + 1) * GROUP)
            bs_sum = jnp.zeros((CHUNK, GROUP), F32)
            ws_sum = jnp.zeros((CHUNK, CHUNK), F32)
            for ch in range(tt // CHUNK):
                rows = slice(ch * CHUNK, (ch + 1) * CHUNK)
                v_ch = vb[rows, cols]
                mixed = _dot(wc[h], v_ch) + bs_ref[:, cols]
                dy_ch = dya[rows, cols]
                dz_scr[rows, cols] = dy_ch * mixed
                dmixed = dy_ch * u[rows, cols]
                dmb = dmixed.astype(BF16)
                dz_scr[rows, D_A + h * GROUP:D_A + (h + 1) * GROUP] = _dot_tn(wc[h], dmb)
                bs_sum = bs_sum + dmixed
                ws_sum = ws_sum + _dot_nt(dmb, v_ch)
            _acc_rows(bs_acc, 0, h, bs_sum)
            _acc_rows(small_ref, ROW_WS, h, ws_sum)

        dvl = dz_scr[:, D_A:2 * D_A]
        dvhat = dvl * lng_ref[...]
        dvr = rv * (dvhat - jnp.mean(dvhat, axis=-1, keepdims=True)
                    - vhat * jnp.mean(dvhat * vhat, axis=-1, keepdims=True))
        small_ref[ROW_LN:ROW_LN + 1, 0:D_A] += _colsum(dvl * vhat)
        small_ref[ROW_LN:ROW_LN + 1, D_A:D] += _colsum(dvl)
        dga = jnp.concatenate([dz_scr[:, :D_A], dvr], axis=1)
        dza = dga * _gelu_grad(z[:, :2 * D_A], t_g)

        zb = z[:, 2 * D_A:]
        halo_prev = jnp.where(i == 0, 0.0, zprev_ref[...])
        diff = _pool_diff(zb, halo_prev, i * tt)
        dyb = dcat[:, D_A:]
        inv = _inv_counts(i * tt, tt)
        scaled, ddiffs = [], []
        for g in range(len(WINDOWS)):
            cols = slice(g * GROUP, (g + 1) * GROUP)
            db = diff[g].astype(BF16)
            wpg = wp_ref[g].astype(BF16)
            pre = _dot(db, wpg) + bp_ref[:, cols]
            small_ref[ROW_POOL:ROW_POOL + 1, cols] += _colsum(dyb[:, cols] * pre)
            dpre = dyb[:, cols] * ps_ref[:, cols]
            small_ref[ROW_POOL:ROW_POOL + 1, D_B + g * GROUP:D_B + (g + 1) * GROUP] += _colsum(dpre)
            dpb = dpre.astype(BF16)
            _acc_rows(small_ref, ROW_WP, g, _dot_tn(db, dpb))
            ddiff = _dot_nt(dpb, wpg)
            ddiffs.append(ddiff)
            scaled.append(ddiff * inv[g])
        scaled_all = jnp.concatenate(scaled, axis=1)
        ext = jnp.concatenate([scaled_all, carry[...]], axis=0)
        n_ext = tt + HALO
        s2 = ext + pltpu.roll(ext, n_ext - 1, 0)
        t4 = s2[:, GROUP:]
        s4 = t4 + pltpu.roll(t4, n_ext - 2, 0)
        t8 = s4[:, GROUP:]
        s8 = t8 + pltpu.roll(t8, n_ext - 4, 0)
        t16 = s8[:, GROUP:]
        s16 = t16 + pltpu.roll(t16, n_ext - 8, 0)
        back = [s2[:, :GROUP], s4[:, :GROUP], s8[:, :GROUP], s16]
        carry[...] = scaled_all[:HALO, :]
        dzb = jnp.concatenate([back[g][:tt, :] - ddiffs[g] for g in range(len(WINDOWS))], axis=1)

        dzv = jnp.concatenate([dza, dzb], axis=1).astype(BF16)
        dh1 = _dot(dzv, win_ref[...])
        xv = x_ref[...]
        r1 = _rstd(xv)
        xhat = xv * r1
        shift1, scale1 = mod_ref[0:1, :], mod_ref[1:2, :]
        pre1 = n1pre_ref[...]
        gain1 = pre1 * (1.0 + scale1)
        h1 = (xhat * gain1 + shift1).astype(BF16)
        acc_in[...] += _dot_tn(dzv, h1)
        gx_ref[...] = dx1_ref[...] + _rms_bwd(dh1 * gain1, xhat, r1)
        sum_h = _colsum(dh1 * xhat)
        small_ref[ROW_DMOD:ROW_DMOD + 1, :] += _colsum(dh1)
        small_ref[ROW_DMOD + 1:ROW_DMOD + 2, :] += pre1 * sum_h
        small_ref[ROW_N1PRE:ROW_N1PRE + 1, :] += (1.0 + scale1) * sum_h

        @pl.when(s == nt - 1)
        def _():
            gwin_ref[...] = acc_in[...].astype(BF16)
            gwout_ref[...] = acc_out[...].astype(BF16)
            bs = _unfold(bs_acc[...])
            for h in range(N_HEADS):
                small_ref[ROW_BS + h:ROW_BS + h + 1, 0:GROUP] = jnp.sum(
                    bs[:, h * GROUP:(h + 1) * GROUP].T, axis=0, keepdims=True)
            for a in range(n_sums):
                for r in range(2):
                    chip_copy(a, r).wait_recv()
                    chip_copy(a, r).wait_send()

    rev = lambda w: pl.BlockSpec((tt, w), lambda s: (nt - 1 - s, 0))
    zprev = pl.BlockSpec((HALO, D_B), lambda s: (jnp.maximum((nt - 1 - s) * hb - 1, 0), 2))
    hbm = pl.BlockSpec(memory_space=pl.ANY)
    outs = pl.pallas_call(
        body, name="attn_bwd", grid=(nt,),
        out_shape=tuple([jax.ShapeDtypeStruct((t_len, D), F32), jax.ShapeDtypeStruct((D_Z, D), BF16),
                         jax.ShapeDtypeStruct((D, D), BF16), jax.ShapeDtypeStruct((SMALL_ROWS, D), F32)]
                        + [jax.ShapeDtypeStruct(cs.shape, cs.dtype) for cs in chip_sums]),
        in_specs=[rev(D), rev(D), rev(D), rev(D_Z), zprev, rev(D), _full((8, D)), _full((1, D)),
                  _resident((D_Z, D)), _resident((D, D)), _full((N_HEADS, CHUNK, CHUNK)), _full((CHUNK, D_A)),
                  _full((1, D_A)), _full((1, D_A)), _full((len(WINDOWS), GROUP, GROUP)), _full((1, D_B)),
                  _full((1, D_B)), _full((8, D)), _full((8, D))] + [_resident(cs.shape) for cs in chip_sums],
        out_specs=tuple([rev(D), _resident((D_Z, D)), _resident((D, D)), _full((SMALL_ROWS, D))] + [hbm] * n_sums),
        scratch_shapes=[pltpu.VMEM((HALO, D_B), F32), pltpu.VMEM((D_Z, D), F32), pltpu.VMEM((D, D), F32),
                        pltpu.VMEM((tt, 2 * D_A), F32), pltpu.VMEM((CHUNK // 2, D), F32),
                        pltpu.SemaphoreType.DMA((2 * n_sums,)), pltpu.SemaphoreType.DMA((2 * n_sums,))],
        compiler_params=pltpu.CompilerParams(dimension_semantics=("arbitrary",), vmem_limit_bytes=VMEM_LIMIT),
    )(dmix, dx1, x, z, z, cat, mod, n1pre, w_in_t, w_out, w_sp, bs_rows, ln_g, ln_b, w_pool, b_pool, pool_scale,
      red_fwd, red_bwd, *chip_sums)
    return outs[:4], outs[4:]


def _adam(w, g, m, v):
    m2 = ADAM_B1 * m + (1.0 - ADAM_B1) * g
    v2 = ADAM_B2 * v + (1.0 - ADAM_B2) * (g * g)
    m_hat = m2 / (1.0 - ADAM_B1 ** ADAM_STEP)
    v_hat = v2 / (1.0 - ADAM_B2 ** ADAM_STEP)
    delta = -ADAM_LR * (m_hat / (jnp.sqrt(v_hat) + ADAM_EPS) + ADAM_WD * w)
    return delta, m2, v2


def _adamw_shard(name, rb, w, g, m, v):
    rows, cols = w.shape

    def body(w_ref, g_ref, m_ref, v_ref, d_ref, m2_ref, v2_ref):
        d_ref[...], m2_ref[...], v2_ref[...] = _adam(w_ref[...], g_ref[...], m_ref[...], v_ref[...])

    blk = pl.BlockSpec((rb, cols), lambda i: (i, 0))
    shp = jax.ShapeDtypeStruct((rows, cols), F32)
    return pl.pallas_call(
        body, name=name, grid=(rows // rb,), out_shape=(shp, shp, shp),
        in_specs=[blk] * 4, out_specs=(blk, blk, blk),
        compiler_params=pltpu.CompilerParams(dimension_semantics=("arbitrary",)),
    )(w, g, m, v)


def _adamw_ada(rb, w, sc, dmod_cols, m, v):
    rows, cols = w.shape

    def body(w_ref, sc_ref, dm_ref, m_ref, v_ref, g_ref, d_ref, m2_ref, v2_ref):
        g = _dot_tn(sc_ref[...].astype(BF16), dm_ref[...].astype(BF16))
        g_ref[...] = g
        d_ref[...], m2_ref[...], v2_ref[...] = _adam(w_ref[...], g, m_ref[...], v_ref[...])

    blk = pl.BlockSpec((rb, cols), lambda i: (i, 0))
    shp = jax.ShapeDtypeStruct((rows, cols), F32)
    return pl.pallas_call(
        body, name="adamw_ada", grid=(rows // rb,), out_shape=(shp, shp, shp, shp),
        in_specs=[blk, pl.BlockSpec((N_DEV, rb), lambda i: (0, i)), _full((N_DEV, cols)), blk, blk],
        out_specs=(blk, blk, blk, blk),
        compiler_params=pltpu.CompilerParams(dimension_semantics=("arbitrary",)),
    )(w, sc, dmod_cols, m, v)


def _unfold(acc_rows):
    return jnp.concatenate([acc_rows[:, :D_A], acc_rows[:, D_A:]], axis=0)


def _adamw_small(total, params):
    n = len(params)
    flat = [a for p in params for a in p]

    def body(*refs):
        s_ref = refs[0]
        p_refs = refs[1:1 + 3 * n]
        loss_ref = refs[1 + 3 * n]
        o_refs = refs[2 + 3 * n:]
        d_b_ada = s_ref[0:6, :]
        for b in range(1, N_DEV):
            d_b_ada = d_b_ada + s_ref[8 * b:8 * b + 6, :]
        tot = s_ref[PACK_SHIFT:PACK_ROWS, :]
        loss_ref[...] = jnp.broadcast_to(tot[ROW_LOSS:ROW_LOSS + 1, 0:GROUP], (8, GROUP))
        mask = _tril_mask()
        ws = _unfold(tot[ROW_WS:ROW_WS + 64, :])
        wp = _unfold(tot[ROW_WP:ROW_WP + 64, :])
        grads = [
            d_b_ada,
            tot[ROW_N1PRE:ROW_N1PRE + 1, :], tot[ROW_N1POST:ROW_N1POST + 1, :],
            tot[ROW_N2PRE:ROW_N2PRE + 1, :], tot[ROW_N2POST:ROW_N2POST + 1, :],
            tot[ROW_LN:ROW_LN + 1, :D_A], tot[ROW_LN:ROW_LN + 1, D_A:],
            tot[ROW_POOL:ROW_POOL + 1, :D_B], tot[ROW_POOL:ROW_POOL + 1, D_B:],
            tot[ROW_BS:ROW_BS + N_HEADS, 0:GROUP],
            jnp.stack([ws[:, h * GROUP:(h + 1) * GROUP] * mask for h in range(N_HEADS)]),
            jnp.stack([wp[:, g * GROUP:(g + 1) * GROUP] for g in range(len(WINDOWS))]),
        ]
        for k in range(n):
            w_ref, m_ref, v_ref = p_refs[3 * k:3 * k + 3]
            g = grads[k]
            o_refs[4 * k][...] = g
            o_refs[4 * k + 1][...], o_refs[4 * k + 2][...], o_refs[4 * k + 3][...] = _adam(
                w_ref[...], g, m_ref[...], v_ref[...])

    vm = pl.BlockSpec(memory_space=pltpu.VMEM)
    out_shape = [jax.ShapeDtypeStruct((8, GROUP), F32)]
    for w, _, _ in params:
        out_shape += [jax.ShapeDtypeStruct(w.shape, F32)] * 4
    return pl.pallas_call(
        body, name="adamw_small", out_shape=tuple(out_shape),
        in_specs=[vm] * (1 + 3 * n), out_specs=tuple([vm] * len(out_shape)),
    )(total, *flat)


TT_ATTN_FWD = 512
TT_MLP_FWD = 512
TT_MLP = 256
TT_WGRAD = 2048
TT_ATTN_BWD = 512


def kernel(x, c, w_ada, b_ada, norm1_pre, norm1_post, w_in, w_spatial, b_spatial, ln_v_gain, ln_v_bias, w_pool, b_pool, pool_scale, w_out, norm2_pre, norm2_post, w_fc1, w_fc2, loss_target, m_w_ada, m_b_ada, m_norm1_pre, m_norm1_post, m_w_in, m_w_spatial, m_b_spatial, m_ln_v_gain, m_ln_v_bias, m_w_pool, m_b_pool, m_pool_scale, m_w_out, m_norm2_pre, m_norm2_post, m_w_fc1, m_w_fc2, v_w_ada, v_b_ada, v_norm1_pre, v_norm1_post, v_w_in, v_w_spatial, v_b_spatial, v_ln_v_gain, v_ln_v_bias, v_w_pool, v_b_pool, v_pool_scale, v_w_out, v_norm2_pre, v_norm2_post, v_w_fc1, v_w_fc2):
    t_len = x.shape[1]
    me = 4 * lax.axis_index("x") + 2 * lax.axis_index("y") + lax.axis_index("c")
    ada_cols = w_ada.shape[1]
    tt = lambda want: min(want, t_len)

    x2 = x.reshape(t_len, D)
    tgt = loss_target.reshape(t_len, D)
    row = lambda a: a.reshape(1, -1)

    b_my = lax.dynamic_slice_in_dim(b_ada, me * ada_cols, ada_cols).reshape(1, ada_cols)
    modp, sc, (g_in, g_out), fc_shards = _fwd_comm(jnp.broadcast_to(c, (8, D)), w_ada, b_my,
                                                   [w_in.T, w_out], [w_fc1, w_fc2])
    mod = jnp.concatenate([modp.reshape(6, D), jnp.zeros((2, D), F32)], axis=0)
    w_in_t = g_in.reshape(D_Z, D)
    w_out_all = g_out.reshape(D, D)

    bs_rows = jnp.repeat(b_spatial.T, GROUP, axis=1)
    attn_consts = (w_spatial, bs_rows, row(ln_v_gain), row(ln_v_bias), w_pool, row(b_pool), row(pool_scale))

    (z, cat, mix, x1), (w1_early, w2_early) = _attn_fwd(
        tt(TT_ATTN_FWD), x2, mod, row(norm1_pre), row(norm1_post), w_in_t, w_out_all, *attn_consts, fc_shards)
    (r_early, h2, f_early), (w1_late, w2_late) = _mlp_fwd_early(
        tt(TT_MLP_FWD), x1, mod, row(norm2_pre), w1_early, w2_early)
    r_late, df, da, dmix, dx1, red_fwd, red_bwd = _mlp_late_bwd(
        tt(TT_MLP), r_early, x1, h2, f_early, tgt, mix, mod, row(norm2_pre), row(norm2_post), row(norm1_post),
        w1_early, w2_early, w1_late, w2_late)
    own_w1, own_w2, sums_w1, sums_w2, diag_w1, diag_w2 = _mlp_wgrad(tt(TT_WGRAD), r_early, r_late, da, df, h2)
    (grad_x, p_in, p_out, small), (arr_w1, arr_w2) = _attn_bwd(
        tt(TT_ATTN_BWD), dmix, dx1, x2, z, cat, mod, row(norm1_pre), w_in_t, w_out_all, *attn_consts,
        red_fwd, red_bwd, [sums_w1, sums_w2])
    (grad_in_t, grad_out, total), ((grad_w1, d_w1, m_w1, v_w1), (grad_w2, d_w2, m_w2, v_w2)) = _tail_comm(
        [p_in.reshape(N_DEV, D_Z // N_DEV, D), p_out.reshape(N_DEV, D // N_DEV, D)], small, 64,
        [(w_fc1, own_w1, arr_w1, diag_w1, m_w_fc1, v_w_fc1), (w_fc2, own_w2, arr_w2, diag_w2, m_w_fc2, v_w_fc2)])

    d_out, m_out, v_out = _adamw_shard("adamw_out", 128, w_out, grad_out, m_w_out, v_w_out)
    d_in_t, m_in_t, v_in_t = _adamw_shard("adamw_in", D_Z // N_DEV, w_in.T, grad_in_t, m_w_in.T, v_w_in.T)
    dmod_all = total[0:TABLE_ROWS, :].reshape(N_DEV, 8, D)[:, :6, :].reshape(N_DEV, 6 * D)
    dmod_cols = lax.dynamic_slice_in_dim(dmod_all, me * ada_cols, ada_cols, axis=1)
    grad_ada, d_ada, m_ada, v_ada = _adamw_ada(256, w_ada, sc, dmod_cols, m_w_ada, v_w_ada)

    six = lambda a: a.reshape(6, D)
    small_params = [
        (six(b_ada), six(m_b_ada), six(v_b_ada)),
        (row(norm1_pre), row(m_norm1_pre), row(v_norm1_pre)),
        (row(norm1_post), row(m_norm1_post), row(v_norm1_post)),
        (row(norm2_pre), row(m_norm2_pre), row(v_norm2_pre)),
        (row(norm2_post), row(m_norm2_post), row(v_norm2_post)),
        (row(ln_v_gain), row(m_ln_v_gain), row(v_ln_v_gain)),
        (row(ln_v_bias), row(m_ln_v_bias), row(v_ln_v_bias)),
        (row(pool_scale), row(m_pool_scale), row(v_pool_scale)),
        (row(b_pool), row(m_b_pool), row(v_b_pool)),
        (b_spatial, m_b_spatial, v_b_spatial),
        (w_spatial, m_w_spatial, v_w_spatial),
        (w_pool, m_w_pool, v_w_pool),
    ]
    outs = _adamw_small(total, small_params)
    loss = outs[0][0, 0]
    names = ["b_ada", "norm1_pre", "norm1_post", "norm2_pre", "norm2_post", "ln_v_gain", "ln_v_bias", "pool_scale",
             "b_pool", "b_spatial", "w_spatial", "w_pool"]
    shapes = dict(b_ada=b_ada.shape, norm1_pre=norm1_pre.shape, norm1_post=norm1_post.shape,
                  norm2_pre=norm2_pre.shape, norm2_post=norm2_post.shape, ln_v_gain=ln_v_gain.shape,
                  ln_v_bias=ln_v_bias.shape, pool_scale=pool_scale.shape, b_pool=b_pool.shape,
                  b_spatial=b_spatial.shape, w_spatial=w_spatial.shape, w_pool=w_pool.shape)
    res = {}
    for k, nm in enumerate(names):
        res[nm] = tuple(o.reshape(shapes[nm]) for o in outs[1 + 4 * k:5 + 4 * k])
    res["w_ada"] = (grad_ada, d_ada, m_ada, v_ada)
    res["w_in"] = (grad_in_t.T, d_in_t.T, m_in_t.T, v_in_t.T)
    res["w_out"] = (grad_out, d_out, m_out, v_out)
    res["w_fc1"] = (grad_w1, d_w1, m_w1, v_w1)
    res["w_fc2"] = (grad_w2, d_w2, m_w2, v_w2)

    order = ["w_ada", "b_ada", "norm1_pre", "norm1_post", "w_in", "w_spatial", "b_spatial", "ln_v_gain", "ln_v_bias",
             "w_pool", "b_pool", "pool_scale", "w_out", "norm2_pre", "norm2_post", "w_fc1", "w_fc2"]
    return (loss, grad_x.reshape(x.shape),
            *[res[nm][0] for nm in order], *[res[nm][1] for nm in order],
            *[res[nm][2] for nm in order], *[res[nm][3] for nm in order])
```

```python
import functools

import jax
import jax.numpy as jnp
from jax import lax
from jax.experimental import pallas as pl
from jax.experimental.pallas import tpu as pltpu

F32 = jnp.float32
BF16 = jnp.bfloat16
MESH = pl.DeviceIdType.MESH

N_DEV = 8
D = 1024
D_A = 512
D_B = 512
D_Z = 2 * D_A + D_B
N_HEADS = 4
CHUNK = 128
WINDOWS = (2, 4, 8, 16)
GROUP = 128
D_FF = 4096
FF_BLK = D_FF // N_DEV
HALO = 16
EPS = 1e-6
VMEM_LIMIT = 60 * 1024 * 1024

ADAM_LR = 0.001
ADAM_B1 = 0.9
ADAM_B2 = 0.999
ADAM_EPS = 1e-08
ADAM_WD = 0.01
ADAM_STEP = 10

ROW_DMOD = 0
ROW_N1PRE, ROW_N1POST, ROW_N2PRE, ROW_N2POST = 8, 9, 10, 11
ROW_LN = 12
ROW_POOL = 13
ROW_LOSS = 14
ROW_BS = 16
ROW_WS = 24
ROW_WP = 88
SMALL_ROWS = 152
TABLE_ROWS = 8 * N_DEV
PACK_SHIFT = TABLE_ROWS - 8
PACK_ROWS = SMALL_ROWS + PACK_SHIFT
PACK_HALF = PACK_ROWS // 2


def _dot(a, b):
    return jnp.dot(a, b, preferred_element_type=F32)


def _dot_nt(a, b):
    return lax.dot_general(a, b, (((1,), (1,)), ((), ())), preferred_element_type=F32)


def _dot_tn(a, b):
    return lax.dot_general(a, b, (((0,), (0,)), ((), ())), preferred_element_type=F32)


def _rstd(v):
    return lax.rsqrt(jnp.mean(v * v, axis=-1, keepdims=True) + EPS)


def _rms_bwd(d_hat, hat, rstd):
    return rstd * (d_hat - hat * jnp.mean(d_hat * hat, axis=-1, keepdims=True))


def _rms_bwd_gained(g, gain, hat, rstd):
    g_hat = g * hat
    d_v = rstd * (g * gain - hat * jnp.mean(g_hat * gain, axis=-1, keepdims=True))
    return d_v, _colsum(g_hat)


_K0 = 0.7978845608028654
_K1 = 0.044715


def _gelu_parts(v):
    t = jnp.tanh(_K0 * (v + _K1 * (v * v * v)))
    return t, v * (0.5 * (1.0 + t))


def _gelu_grad(v, t):
    return 0.5 * (1.0 + t) + (0.5 * v) * (1.0 - t * t) * (_K0 * (1.0 + (3.0 * _K1) * (v * v)))


def _colsum(v):
    return jnp.sum(v, axis=0, keepdims=True)


def _full(shape):
    n = len(shape)
    return pl.BlockSpec(shape, lambda *_: (0,) * n)


def _resident(shape):
    n = len(shape)
    return pl.BlockSpec(shape, lambda *_: (0,) * n, pipeline_mode=pl.Buffered(1))


def _place():
    x, y, c = lax.axis_index("x"), lax.axis_index("y"), lax.axis_index("c")
    return x, y, c


def _flip(v, bit):
    return 1 - v if bit else v


def _peer(x, y, c, k):
    return (_flip(x, (k >> 2) & 1), _flip(y, (k >> 1) & 1), _flip(c, k & 1))


def _index(p):
    return 4 * p[0] + 2 * p[1] + p[2]


def _two_level_gather_begin(x, y, c, out_refs, send_sems, recv_sems):
    me = (x, y, c)
    sibling = (x, y, 1 - c)
    chips = [(1 - x, y), (x, 1 - y), (1 - x, 1 - y)]

    def copy(a, k, block, to):
        ref = out_refs[a].at[_index(block)]
        return pltpu.make_async_remote_copy(
            src_ref=ref, dst_ref=ref, send_sem=send_sems.at[7 * a + k], recv_sem=recv_sems.at[7 * a + k],
            device_id=to, device_id_type=MESH)

    first = []
    for a in range(len(out_refs)):
        first.append(copy(a, 0, me, sibling))
        first += [copy(a, 1 + j, me, (*chip, c)) for j, chip in enumerate(chips)]
    for cp in first:
        cp.start()
    return copy, first, me, sibling, chips


def _two_level_gather_finish(c, n, begun):
    copy, first, me, sibling, chips = begun
    passed = []
    for a in range(n):
        for j, chip in enumerate(chips):
            copy(a, 1 + j, (*chip, c), me).wait_recv()
            fwd = copy(a, 4 + j, (*chip, c), sibling)
            fwd.start()
            passed.append(fwd)
    for a in range(n):
        copy(a, 0, sibling, me).wait_recv()
        for j, chip in enumerate(chips):
            copy(a, 4 + j, (*chip, 1 - c), me).wait_recv()
    for cp in first + passed:
        cp.wait_send()


def _fwd_comm(c8, w_ada, b_my, gathered, kept):
    ncol = w_ada.shape[1]
    n_g, n_k = len(gathered), len(kept)

    def body(c_ref, w_ref, b_ref, *rest):
        g_in, k_in = rest[:n_g], rest[n_g:n_g + n_k]
        modp_ref, sc_ref = rest[n_g + n_k:n_g + n_k + 2]
        g_out = rest[n_g + n_k + 2:2 * n_g + n_k + 2]
        k_out = rest[2 * n_g + n_k + 2:2 * n_g + 2 * n_k + 2]
        cg, mg, part, send_sems, recv_sems, g_send, g_recv = rest[2 * n_g + 2 * n_k + 2:]
        x, y, c = _place()
        me = _index((x, y, c))
        for a in range(n_g):
            g_out[a][me] = g_in[a][...].astype(BF16)
        begun = _two_level_gather_begin(x, y, c, g_out, g_send, g_recv)
        for a in range(n_k):
            k_out[a][...] = k_in[a][...].astype(BF16)

        def c_copy(k):
            p = _peer(x, y, c, k)
            return pltpu.make_async_remote_copy(
                src_ref=c_ref, dst_ref=cg.at[me], send_sem=send_sems.at[k - 1], recv_sem=recv_sems.at[k - 1],
                device_id=p, device_id_type=MESH)

        def c_arrival(k):
            p = _peer(x, y, c, k)
            return pltpu.make_async_remote_copy(
                src_ref=c_ref, dst_ref=cg.at[_index(p)], send_sem=send_sems.at[k - 1], recv_sem=recv_sems.at[k - 1],
                device_id=p, device_id_type=MESH)

        def m_copy(k):
            p = _peer(x, y, c, k)
            return pltpu.make_async_remote_copy(
                src_ref=part, dst_ref=mg.at[me], send_sem=send_sems.at[6 + k], recv_sem=recv_sems.at[6 + k],
                device_id=p, device_id_type=MESH)

        def m_arrival(k):
            p = _peer(x, y, c, k)
            return pltpu.make_async_remote_copy(
                src_ref=part, dst_ref=mg.at[_index(p)], send_sem=send_sems.at[6 + k], recv_sem=recv_sems.at[6 + k],
                device_id=p, device_id_type=MESH)

        for k in range(1, N_DEV):
            c_copy(k).start()
        cg[me] = c_ref[...]
        for k in range(1, N_DEV):
            c_arrival(k).wait_recv()
        c_all = jnp.concatenate([cg[j, 0:1, :] for j in range(N_DEV)], axis=0)
        sc = c_all * jax.nn.sigmoid(c_all)
        sc_ref[...] = sc
        part[...] = _dot(sc.astype(BF16), w_ref[...].astype(BF16)) + b_ref[...]
        for k in range(1, N_DEV):
            m_copy(k).start()
        mg[me] = part[...]
        for k in range(1, N_DEV):
            m_arrival(k).wait_recv()
        for j in range(N_DEV):
            modp_ref[j:j + 1, :] = mg[j, pl.ds(me, 1), :]
        _two_level_gather_finish(c, n_g, begun)
        for k in range(1, N_DEV):
            c_copy(k).wait_send()
            m_copy(k).wait_send()

    vm = pl.BlockSpec(memory_space=pltpu.VMEM)
    outs = pl.pallas_call(
        body, name="fwd_comm",
        out_shape=tuple([jax.ShapeDtypeStruct((N_DEV, ncol), F32), jax.ShapeDtypeStruct((N_DEV, D), F32)]
                        + [jax.ShapeDtypeStruct((N_DEV,) + s.shape, BF16) for s in gathered]
                        + [jax.ShapeDtypeStruct(s.shape, BF16) for s in kept]),
        in_specs=[vm] * (3 + n_g + n_k), out_specs=tuple([vm] * (2 + n_g + n_k)),
        scratch_shapes=[
            pltpu.VMEM((N_DEV, 8, D), F32),
            pltpu.VMEM((N_DEV, N_DEV, ncol), F32),
            pltpu.VMEM((N_DEV, ncol), F32),
            pltpu.SemaphoreType.DMA((2 * (N_DEV - 1),)),
            pltpu.SemaphoreType.DMA((2 * (N_DEV - 1),)),
            pltpu.SemaphoreType.DMA((7 * n_g,)),
            pltpu.SemaphoreType.DMA((7 * n_g,)),
        ],
        compiler_params=pltpu.CompilerParams(vmem_limit_bytes=VMEM_LIMIT),
    )(c8, w_ada, b_my, *gathered, *kept)
    return outs[0], outs[1], outs[2:2 + n_g], outs[2 + n_g:]


FC_EARLY = 6
WGRAD_ORDER = (7, 6, 1, 3, 5, 2, 4, 0)


class _Copies:
    def __init__(self, entries, send_sems, recv_sems):
        self.place = _place()
        self.entries, self.send_sems, self.recv_sems = entries, send_sems, recv_sems

    def _copy(self, i, arrival=False):
        src, dst, rel = self.entries[i]
        return pltpu.make_async_remote_copy(
            src_ref=dst if arrival else src, dst_ref=dst, send_sem=self.send_sems.at[i],
            recv_sem=self.recv_sems.at[i], device_id=_peer(*self.place, rel), device_id_type=MESH)

    def start(self, *which):
        for i in which:
            self._copy(i).start()

    def wait_recv(self, *which):
        for i in which:
            self._copy(i, arrival=True).wait_recv()

    def wait_send(self, *which):
        for i in which:
            self._copy(i).wait_send()


TAIL_STEPS = 8


def _tail_comm(parts, small, row_chunk, fc):
    n, n_fc = len(parts), len(fc)

    def body(*refs):
        p_refs, small_ref = refs[:n], refs[n]
        fc_in = refs[n + 1:n + 1 + 6 * n_fc]
        outs = refs[n + 1 + 6 * n_fc:]
        g_refs, total_ref = outs[:n], outs[n]
        fc_out = outs[n + 1:n + 1 + 4 * n_fc]
        scr = outs[n + 1 + 4 * n_fc:]
        from_sib = scr[0:n]
        chip_out = scr[n:2 * n]
        chip_in = scr[2 * n:3 * n]
        pack, pack_sib, halves, total_scr = scr[3 * n:3 * n + 4]
        send_a, recv_a, send_b, recv_b, send_s, recv_s = scr[3 * n + 4:]
        step = pl.program_id(0)
        x, y, c = _place()
        me = _index((x, y, c))
        sibling = (x, y, 1 - c)
        my_chip = 2 * x + y
        others = [(1 - x, y), (x, 1 - y), (1 - x, 1 - y)]
        my_half = pl.ds(pl.multiple_of(PACK_HALF * c, 8), PACK_HALF)

        def pack_to_sibling():
            return pltpu.make_async_remote_copy(
                src_ref=pack, dst_ref=pack_sib, send_sem=send_s.at[0], recv_sem=recv_s.at[0],
                device_id=sibling, device_id_type=MESH)

        def half_to_chip(r):
            return pltpu.make_async_remote_copy(
                src_ref=halves.at[my_chip], dst_ref=halves.at[my_chip],
                send_sem=send_s.at[1 + r], recv_sem=recv_s.at[1 + r],
                device_id=(*others[r], c), device_id_type=MESH)

        def half_from_chip(r):
            k = 2 * others[r][0] + others[r][1]
            return pltpu.make_async_remote_copy(
                src_ref=halves.at[k], dst_ref=halves.at[k], send_sem=send_s.at[1 + r], recv_sem=recv_s.at[1 + r],
                device_id=(*others[r], c), device_id_type=MESH)

        def total_to_sibling():
            return pltpu.make_async_remote_copy(
                src_ref=total_scr.at[my_half], dst_ref=total_scr.at[my_half],
                send_sem=send_s.at[4], recv_sem=recv_s.at[4], device_id=sibling, device_id_type=MESH)

        def total_from_sibling():
            sib_half = pl.ds(pl.multiple_of(PACK_HALF * (1 - c), 8), PACK_HALF)
            return pltpu.make_async_remote_copy(
                src_ref=total_scr.at[sib_half], dst_ref=total_scr.at[sib_half],
                send_sem=send_s.at[4], recv_sem=recv_s.at[4], device_id=sibling, device_id_type=MESH)

        def to_sibling(a, k):
            return pltpu.make_async_remote_copy(
                src_ref=p_refs[a].at[2 * k + (1 - c)], dst_ref=from_sib[a].at[k],
                send_sem=send_a.at[a], recv_sem=recv_a.at[a], device_id=sibling, device_id_type=MESH)

        def all_from_sibling(a):
            return pltpu.make_async_remote_copy(
                src_ref=from_sib[a], dst_ref=from_sib[a], send_sem=send_a.at[a], recv_sem=recv_a.at[a],
                device_id=sibling, device_id_type=MESH)

        def to_chip(a, r):
            return pltpu.make_async_remote_copy(
                src_ref=chip_out[a].at[r], dst_ref=chip_in[a].at[r],
                send_sem=send_b.at[3 * a + r], recv_sem=recv_b.at[3 * a + r],
                device_id=(*others[r], c), device_id_type=MESH)

        @pl.when(step == 0)
        def _():
            pack[0:TABLE_ROWS, :] = jnp.zeros((TABLE_ROWS, D), F32)
            pack[pl.ds(pl.multiple_of(8 * me, 8), 8), :] = small_ref[0:8, :]
            pack[TABLE_ROWS:PACK_ROWS, :] = small_ref[8:SMALL_ROWS, :]
            pack_to_sibling().start()
            for a in range(n):
                for k in range(4):
                    to_sibling(a, k).start()

        @pl.when(step == 1)
        def _():
            pack_to_sibling().wait_recv()
            halves[my_chip] = pack[my_half, :] + pack_sib[my_half, :]
            for r in range(3):
                half_to_chip(r).start()
            for a in range(n):
                all_from_sibling(a).wait_recv()
                rows = p_refs[a].shape[1]
                for r in range(3):
                    k = 2 * others[r][0] + others[r][1]
                    for s in range(0, rows, row_chunk):
                        sl = pl.ds(s, row_chunk)
                        chip_out[a][r, sl, :] = (p_refs[a][2 * k + c, sl, :].astype(F32)
                                                 + from_sib[a][k, sl, :].astype(F32)).astype(BF16)
                    to_chip(a, r).start()
                for s in range(0, rows, row_chunk):
                    sl = pl.ds(s, row_chunk)
                    g_refs[a][sl, :] = (p_refs[a][2 * my_chip + c, sl, :].astype(F32)
                                        + from_sib[a][my_chip, sl, :].astype(F32))

        for k in range(n_fc):
            w_ref, own_ref, arr_ref, diag_ref, m_ref, v_ref = fc_in[6 * k:6 * k + 6]
            g = own_ref[...]
            for r in range(2):
                g = g + arr_ref[r].astype(F32)
            g = g + diag_ref[...].astype(F32)
            fc_out[4 * k][...] = g
            fc_out[4 * k + 1][...], fc_out[4 * k + 2][...], fc_out[4 * k + 3][...] = _adam(
                w_ref[...], g, m_ref[...], v_ref[...])

        @pl.when(step == TAIL_STEPS - 1)
        def _():
            for r in range(3):
                half_from_chip(r).wait_recv()
            total_scr[my_half, :] = ((halves[0] + halves[1]) + halves[2]) + halves[3]
            total_to_sibling().start()
            for a in range(n):
                rows = p_refs[a].shape[1]
                for r in range(3):
                    to_chip(a, r).wait_recv()
                    for s in range(0, rows, row_chunk):
                        sl = pl.ds(s, row_chunk)
                        g_refs[a][sl, :] = g_refs[a][sl, :] + chip_in[a][r, sl, :].astype(F32)
            total_from_sibling().wait_recv()
            total_ref[...] = total_scr[...]
            for a in range(n):
                all_from_sibling(a).wait_send()
                for r in range(3):
                    to_chip(a, r).wait_send()
            pack_to_sibling().wait_send()
            for r in range(3):
                half_to_chip(r).wait_send()
            total_to_sibling().wait_send()

    fc_specs_in, fc_specs_out, fc_shapes, fc_args = [], [], [], []
    for w, own, arrived, diagonal, m, v in fc:
        rows, cols = w.shape
        blk = pl.BlockSpec((rows // TAIL_STEPS, cols), lambda i: (i, 0))
        fc_specs_in += [blk, blk, pl.BlockSpec((2, rows // TAIL_STEPS, cols), lambda i: (0, i, 0)), blk, blk, blk]
        fc_specs_out += [blk] * 4
        fc_shapes += [jax.ShapeDtypeStruct((rows, cols), F32)] * 4
        fc_args += [w, own, arrived, diagonal, m, v]
    outs = pl.pallas_call(
        body, name="tail_comm", grid=(TAIL_STEPS,),
        out_shape=tuple([jax.ShapeDtypeStruct(p.shape[1:], F32) for p in parts]
                        + [jax.ShapeDtypeStruct((PACK_ROWS, D), F32)] + fc_shapes),
        in_specs=[_resident(p.shape) for p in parts] + [_resident(small.shape)] + fc_specs_in,
        out_specs=tuple([_full(p.shape[1:]) for p in parts] + [_full((PACK_ROWS, D))] + fc_specs_out),
        scratch_shapes=(
            [pltpu.VMEM((4,) + p.shape[1:], BF16) for p in parts]
            + [pltpu.VMEM((3,) + p.shape[1:], BF16) for p in parts]
            + [pltpu.VMEM((3,) + p.shape[1:], BF16) for p in parts]
            + [pltpu.VMEM((PACK_ROWS, D), F32), pltpu.VMEM((PACK_ROWS, D), F32),
               pltpu.VMEM((4, PACK_HALF, D), F32), pltpu.VMEM((PACK_ROWS, D), F32)]
            + [pltpu.SemaphoreType.DMA((n,)), pltpu.SemaphoreType.DMA((n,)),
               pltpu.SemaphoreType.DMA((3 * n,)), pltpu.SemaphoreType.DMA((3 * n,)),
               pltpu.SemaphoreType.DMA((5,)), pltpu.SemaphoreType.DMA((5,))]),
        compiler_params=pltpu.CompilerParams(dimension_semantics=("arbitrary",), vmem_limit_bytes=VMEM_LIMIT),
    )(*parts, small, *fc_args)
    return outs[:n + 1], [outs[n + 1 + 4 * k:n + 5 + 4 * k] for k in range(n_fc)]


def _tril_mask():
    row = lax.broadcasted_iota(jnp.int32, (CHUNK, CHUNK), 0)
    col = lax.broadcasted_iota(jnp.int32, (CHUNK, CHUNK), 1)
    return (col <= row).astype(F32)


def _window_sums(ext):
    s2 = ext + pltpu.roll(ext, 1, 0)
    t4 = s2[:, GROUP:]
    s4 = t4 + pltpu.roll(t4, 2, 0)
    t8 = s4[:, GROUP:]
    s8 = t8 + pltpu.roll(t8, 4, 0)
    t16 = s8[:, GROUP:]
    s16 = t16 + pltpu.roll(t16, 8, 0)
    return [s2[:, :GROUP], s4[:, :GROUP], s8[:, :GROUP], s16]


def _inv_counts(first_pos, rows):
    pos = first_pos + lax.broadcasted_iota(jnp.int32, (rows, 1), 0)
    return [1.0 / jnp.minimum(pos + 1, w).astype(F32) for w in WINDOWS]


def _pool_diff(zb, halo, first_pos):
    tt = zb.shape[0]
    sums = _window_sums(jnp.concatenate([halo, zb], axis=0))
    inv = _inv_counts(first_pos, tt)
    return [sums[g][HALO:, :] * inv[g] - zb[:, g * GROUP:(g + 1) * GROUP] for g in range(len(WINDOWS))]


def _attn_fwd(tt, x, mod, n1pre, n1post, w_in_t, w_out, w_sp, bs_rows, ln_g, ln_b, w_pool, b_pool, pool_scale,
              fc_shards):
    t_len = x.shape[0]
    nt = t_len // tt

    def body(x_ref, mod_ref, n1pre_ref, n1post_ref, win_ref, wout_ref, wsp_ref, bs_ref, lng_ref, lnb_ref,
             wp_ref, bp_ref, ps_ref, w1_ref, w2_ref, z_ref, cat_ref, mix_ref, x1_ref, e1_ref, e2_ref,
             carry, land1, land2, send_sems, recv_sems, local_sems):
        i = pl.program_id(0)
        copies = _Copies(
            [(w1_ref, e1_ref.at[1], 1), (w2_ref, e2_ref.at[1], 1),
             (w1_ref, land1.at[0], 2), (w2_ref, land2.at[0], 2),
             (w1_ref, land1.at[1], 4), (w2_ref, land2.at[1], 4),
             (land1.at[0], e1_ref.at[3], 1), (land2.at[0], e2_ref.at[3], 1),
             (land1.at[1], e1_ref.at[5], 1), (land2.at[1], e2_ref.at[5], 1)],
            send_sems, recv_sems)
        keep = [pltpu.make_async_copy(w1_ref, e1_ref.at[0], local_sems.at[0]),
                pltpu.make_async_copy(w2_ref, e2_ref.at[0], local_sems.at[1]),
                pltpu.make_async_copy(land1.at[0], e1_ref.at[2], local_sems.at[2]),
                pltpu.make_async_copy(land1.at[1], e1_ref.at[4], local_sems.at[3]),
                pltpu.make_async_copy(land2.at[0], e2_ref.at[2], local_sems.at[4]),
                pltpu.make_async_copy(land2.at[1], e2_ref.at[4], local_sems.at[5])]

        @pl.when(i == 0)
        def _():
            copies.start(2, 4, 3, 5, 0, 1)
            keep[0].start()
            keep[1].start()
            carry[...] = jnp.zeros_like(carry)

        @pl.when(i == nt // 2)
        def _():
            copies.wait_recv(2, 4)
            copies.start(6, 8)
            keep[2].start()
            keep[3].start()

        @pl.when(i == nt - 1)
        def _():
            copies.wait_recv(3, 5)
            copies.start(7, 9)
            keep[4].start()
            keep[5].start()

        xv = x_ref[...]
        shift1, scale1, gate1 = mod_ref[0:1, :], mod_ref[1:2, :], mod_ref[2:3, :]
        h1 = (xv * _rstd(xv)) * (n1pre_ref[...] * (1.0 + scale1)) + shift1
        z = _dot_nt(h1.astype(BF16), win_ref[...])
        z_ref[...] = z

        _, ga = _gelu_parts(z[:, :2 * D_A])
        u, vr = ga[:, :D_A], ga[:, D_A:]
        dv = vr - jnp.mean(vr, axis=-1, keepdims=True)
        v = (dv * lax.rsqrt(jnp.mean(dv * dv, axis=-1, keepdims=True) + EPS)) * lng_ref[...] + lnb_ref[...]
        vb = v.astype(BF16)
        mask = _tril_mask()
        wc = [(wsp_ref[h] * mask).astype(BF16) for h in range(N_HEADS)]
        for ch in range(tt // CHUNK):
            rows = slice(ch * CHUNK, (ch + 1) * CHUNK)
            for h in range(N_HEADS):
                cols = slice(h * GROUP, (h + 1) * GROUP)
                mixed = _dot(wc[h], vb[rows, cols]) + bs_ref[:, cols]
                cat_ref[rows, cols] = (u[rows, cols] * mixed).astype(BF16)

        zb = z[:, 2 * D_A:]
        diff = _pool_diff(zb, carry[...], i * tt)
        carry[...] = zb[tt - HALO:, :]
        for g in range(len(WINDOWS)):
            cols = slice(g * GROUP, (g + 1) * GROUP)
            pre = _dot(diff[g].astype(BF16), wp_ref[g].astype(BF16)) + bp_ref[:, cols]
            cat_ref[:, D_A + g * GROUP:D_A + (g + 1) * GROUP] = (pre * ps_ref[:, cols]).astype(BF16)

        mix = _dot(cat_ref[...], wout_ref[...])
        mix_ref[...] = mix
        x1_ref[...] = xv + (mix * _rstd(mix)) * (gate1 * n1post_ref[...])

        @pl.when(i == nt - 1)
        def _():
            copies.wait_recv(0, 1, 6, 7, 8, 9)
            copies.wait_send(*range(10))
            for cp in keep:
                cp.wait()

    tile = lambda w: pl.BlockSpec((tt, w), lambda i: (i, 0))
    hbm = pl.BlockSpec(memory_space=pl.ANY)
    outs = pl.pallas_call(
        body, name="attn_fwd", grid=(nt,),
        out_shape=tuple([jax.ShapeDtypeStruct((t_len, D_Z), F32), jax.ShapeDtypeStruct((t_len, D), BF16),
                         jax.ShapeDtypeStruct((t_len, D), F32), jax.ShapeDtypeStruct((t_len, D), F32)]
                        + [jax.ShapeDtypeStruct((FC_EARLY,) + s.shape, BF16) for s in fc_shards]),
        in_specs=[tile(D), _full((8, D)), _full((1, D)), _full((1, D)), _resident((D_Z, D)), _resident((D, D)),
                  _full((N_HEADS, CHUNK, CHUNK)), _full((CHUNK, D_A)), _full((1, D_A)), _full((1, D_A)),
                  _full((len(WINDOWS), GROUP, GROUP)), _full((1, D_B)), _full((1, D_B)),
                  _resident(fc_shards[0].shape), _resident(fc_shards[1].shape)],
        out_specs=(tile(D_Z), tile(D), tile(D), tile(D), hbm, hbm),
        scratch_shapes=[pltpu.VMEM((HALO, D_B), F32),
                        pltpu.VMEM((2,) + fc_shards[0].shape, BF16), pltpu.VMEM((2,) + fc_shards[1].shape, BF16),
                        pltpu.SemaphoreType.DMA((10,)), pltpu.SemaphoreType.DMA((10,)),
                        pltpu.SemaphoreType.DMA((6,))],
        compiler_params=pltpu.CompilerParams(dimension_semantics=("arbitrary",), vmem_limit_bytes=VMEM_LIMIT),
    )(x, mod, n1pre, n1post, w_in_t, w_out, w_sp, bs_rows, ln_g, ln_b, w_pool, b_pool, pool_scale, *fc_shards)
    return outs[:4], outs[4:]


def _mlp_fwd_early(tt, x1, mod, n2pre, w1_early, w2_early):
    t_len = x1.shape[0]
    nt = t_len // tt
    n_late = N_DEV - FC_EARLY

    def body(x1_ref, mod_ref, n2pre_ref, w1_ref, w2_ref, r_ref, h2_ref, f_ref, l1_ref, l2_ref,
             land1, land2, send_sems, recv_sems, local_sems):
        i = pl.program_id(0)
        copies = _Copies(
            [(w1_ref.at[2], land1, 4), (w2_ref.at[4], land2, 2),
             (land1, l1_ref.at[1], 1), (land2, l2_ref.at[1], 1)],
            send_sems, recv_sems)
        keep = [pltpu.make_async_copy(land1, l1_ref.at[0], local_sems.at[0]),
                pltpu.make_async_copy(land2, l2_ref.at[0], local_sems.at[1])]

        @pl.when(i == 0)
        def _():
            copies.start(0, 1)

        @pl.when(i == nt // 2)
        def _():
            copies.wait_recv(0, 1)
            copies.start(2, 3)
            for cp in keep:
                cp.start()

        x1v = x1_ref[...]
        shift2, scale2 = mod_ref[3:4, :], mod_ref[4:5, :]
        h2 = ((x1v * _rstd(x1v)) * (n2pre_ref[...] * (1.0 + scale2)) + shift2).astype(BF16)
        h2_ref[...] = h2
        for j in range(FC_EARLY):
            cols = slice(j * FF_BLK, (j + 1) * FF_BLK)
            ra = jnp.maximum(_dot(h2, w1_ref[j]), 0.0)
            r = (ra * ra).astype(BF16)
            r_ref[:, cols] = r
            contrib = _dot(r, w2_ref[j])
            if j == 0:
                f_ref[...] = contrib
            else:
                f_ref[...] += contrib

        @pl.when(i == nt - 1)
        def _():
            copies.wait_recv(2, 3)
            copies.wait_send(0, 1, 2, 3)
            for cp in keep:
                cp.wait()

    tile = lambda w: pl.BlockSpec((tt, w), lambda i: (i, 0))
    hbm = pl.BlockSpec(memory_space=pl.ANY)
    outs = pl.pallas_call(
        body, name="mlp_fwd_early", grid=(nt,),
        out_shape=(jax.ShapeDtypeStruct((t_len, FC_EARLY * FF_BLK), BF16),
                   jax.ShapeDtypeStruct((t_len, D), BF16), jax.ShapeDtypeStruct((t_len, D), F32),
                   jax.ShapeDtypeStruct((n_late,) + w1_early.shape[1:], BF16),
                   jax.ShapeDtypeStruct((n_late,) + w2_early.shape[1:], BF16)),
        in_specs=[tile(D), _full((8, D)), _full((1, D)),
                  _resident((FC_EARLY, D, FF_BLK)), _resident((FC_EARLY, FF_BLK, D))],
        out_specs=(tile(FC_EARLY * FF_BLK), tile(D), tile(D), hbm, hbm),
        scratch_shapes=[pltpu.VMEM(w1_early.shape[1:], BF16), pltpu.VMEM(w2_early.shape[1:], BF16),
                        pltpu.SemaphoreType.DMA((4,)), pltpu.SemaphoreType.DMA((4,)),
                        pltpu.SemaphoreType.DMA((2,))],
        compiler_params=pltpu.CompilerParams(dimension_semantics=("arbitrary",), vmem_limit_bytes=VMEM_LIMIT),
    )(x1, mod, n2pre, w1_early, w2_early)
    return outs[:3], outs[3:]


def _mlp_late_bwd(tt, r_early, x1, h2, f_early, tgt, mix, mod, n2pre, n2post, n1post,
                  w1_early, w2_early, w1_late, w2_late):
    t_len = x1.shape[0]
    nt = t_len // tt
    n_late = N_DEV - FC_EARLY
    late_cols = n_late * FF_BLK

    def body(re_ref, x1_ref, h2_ref, fe_ref, tgt_ref, mix_ref, mod_ref, n2pre_ref, n2post_ref,
             n1post_ref, w1e_ref, w2e_ref, w1l_ref, w2l_ref,
             rl_ref, df_ref, da_ref, dmix_ref, dx1_ref, redf_ref, redb_ref, dh2_acc):
        i = pl.program_id(0)

        @pl.when(i == 0)
        def _():
            redf_ref[...] = jnp.zeros_like(redf_ref)
            redb_ref[...] = jnp.zeros_like(redb_ref)

        x1v = x1_ref[...]
        gate1, scale2, gate2 = mod_ref[2:3, :], mod_ref[4:5, :], mod_ref[5:6, :]
        h2 = h2_ref[...]
        f = fe_ref[...]
        for j in range(n_late):
            cols = slice(j * FF_BLK, (j + 1) * FF_BLK)
            ra = jnp.maximum(_dot(h2, w1l_ref[j]), 0.0)
            r = (ra * ra).astype(BF16)
            rl_ref[:, cols] = r
            f = f + _dot(r, w2l_ref[j])
        post2 = n2post_ref[...]
        gate_post2 = gate2 * post2
        rf = _rstd(f)
        fhat = f * rf
        err = (x1v + fhat * gate_post2) - tgt_ref[...]
        dy = err * (1.0 / D)
        d_f, sum_f = _rms_bwd_gained(dy, gate_post2, fhat, rf)
        dfv = d_f.astype(BF16)
        df_ref[...] = dfv
        redf_ref[0:1, :] += post2 * sum_f
        redf_ref[1:2, :] += gate2 * sum_f
        redf_ref[2:3, :] += _colsum(err * err)

        for j in range(N_DEV):
            cols = slice(j * FF_BLK, (j + 1) * FF_BLK)
            if j < FC_EARLY:
                w1, w2, r = w1e_ref[j], w2e_ref[j], re_ref[:, cols]
            else:
                jl = j - FC_EARLY
                w1, w2, r = w1l_ref[jl], w2l_ref[jl], rl_ref[:, jl * FF_BLK:(jl + 1) * FF_BLK]
            dr = _dot_nt(dfv, w2)
            da = (dr * (2.0 * jnp.sqrt(r.astype(F32)))).astype(BF16)
            da_ref[:, cols] = da
            contrib = _dot_nt(da, w1)
            if j == 0:
                dh2_acc[...] = contrib
            else:
                dh2_acc[...] += contrib
        dh2 = dh2_acc[...]
        pre2, post1 = n2pre_ref[...], n1post_ref[...]
        r2 = _rstd(x1v)
        xhat = x1v * r2
        d_x1, sum_h = _rms_bwd_gained(dh2, pre2 * (1.0 + scale2), xhat, r2)
        dx1 = dy + d_x1
        dx1_ref[...] = dx1
        mixv = mix_ref[...]
        rm = _rstd(mixv)
        mhat = mixv * rm
        d_mix, sum_m = _rms_bwd_gained(dx1, gate1 * post1, mhat, rm)
        dmix_ref[...] = d_mix.astype(BF16)
        redb_ref[0:1, :] += _colsum(dh2)
        redb_ref[1:2, :] += pre2 * sum_h
        redb_ref[2:3, :] += (1.0 + scale2) * sum_h
        redb_ref[3:4, :] += post1 * sum_m
        redb_ref[4:5, :] += gate1 * sum_m

    tile = lambda w: pl.BlockSpec((tt, w), lambda i: (i, 0))
    return pl.pallas_call(
        body, name="mlp_late_bwd", grid=(nt,),
        out_shape=(jax.ShapeDtypeStruct((t_len, late_cols), BF16), jax.ShapeDtypeStruct((t_len, D), BF16),
                   jax.ShapeDtypeStruct((t_len, D_FF), BF16), jax.ShapeDtypeStruct((t_len, D), BF16),
                   jax.ShapeDtypeStruct((t_len, D), F32), jax.ShapeDtypeStruct((8, D), F32),
                   jax.ShapeDtypeStruct((8, D), F32)),
        in_specs=[tile(FC_EARLY * FF_BLK), tile(D), tile(D), tile(D), tile(D),
                  tile(D), _full((8, D)), _full((1, D)), _full((1, D)), _full((1, D)),
                  _resident((FC_EARLY, D, FF_BLK)), _resident((FC_EARLY, FF_BLK, D)),
                  _resident((n_late, D, FF_BLK)), _resident((n_late, FF_BLK, D))],
        out_specs=(tile(late_cols), tile(D), tile(D_FF), tile(D), tile(D), _full((8, D)), _full((8, D))),
        scratch_shapes=[pltpu.VMEM((tt, D), F32)],
        compiler_params=pltpu.CompilerParams(dimension_semantics=("arbitrary",), vmem_limit_bytes=VMEM_LIMIT),
    )(r_early, x1, h2, f_early, tgt, mix, mod, n2pre, n2post, n1post, w1_early, w2_early, w1_late, w2_late)


def _mlp_wgrad(tt, r_early, r_late, da, df, h2):
    t_len = df.shape[0]
    nt = t_len // tt
    odd_steps = [j for j, rel in enumerate(WGRAD_ORDER) if rel % 2]

    def relation(j):
        rel = jnp.int32(WGRAD_ORDER[-1])
        for step in range(N_DEV - 2, -1, -1):
            rel = jnp.where(j == step, WGRAD_ORDER[step], rel)
        return rel

    def body(re_ref, rl_ref, da_ref, df_ref, h2_ref, own1_ref, own2_ref, out1_ref, out2_ref, diag1_ref, diag2_ref,
             acc1, acc2, snd1, snd2, sib1, sib2, dsnd1, dsnd2, send_sems, recv_sems):
        j, t = pl.program_id(0), pl.program_id(1)
        rows = pl.ds(pl.multiple_of(t * tt, tt), tt)
        x, y, c = _place()
        accs, snds, sibs = (acc1, acc2), (snd1, snd2), (sib1, sib2)
        dsnds, diags = (dsnd1, dsnd2), (diag1_ref, diag2_ref)

        def to_sibling(a, jj, buf=0):
            return pltpu.make_async_remote_copy(
                src_ref=snds[a].at[buf], dst_ref=sibs[a].at[jj],
                send_sem=send_sems.at[4 * a + jj], recv_sem=recv_sems.at[4 * a + jj],
                device_id=(x, y, 1 - c), device_id_type=MESH)

        def to_diagonal(a):
            return pltpu.make_async_remote_copy(
                src_ref=dsnds[a], dst_ref=diags[a], send_sem=send_sems.at[8 + a], recv_sem=recv_sems.at[8 + a],
                device_id=_peer(x, y, c, 6), device_id_type=MESH)

        @pl.when(t == 0)
        def _():
            acc2[...] = jnp.zeros_like(acc2)
            acc1[...] = jnp.zeros_like(acc1)

        for r_ref, mine in ((re_ref, relation(j) < FC_EARLY), (rl_ref, relation(j) >= FC_EARLY)):
            @pl.when(mine)
            def _():
                acc2[...] += _dot_tn(r_ref[...], df_ref[rows, :])
                acc1[...] += _dot_tn(h2_ref[rows, :], da_ref[...])

        for step, rel in enumerate(WGRAD_ORDER):
            jj = rel // 2

            @pl.when((t == nt - 1) & (j == step))
            def _():
                for a, (own_ref, out_ref) in enumerate(((own1_ref, out1_ref), (own2_ref, out2_ref))):
                    if rel % 2:
                        q = odd_steps.index(step)
                        if q >= 2:
                            to_sibling(a, WGRAD_ORDER[odd_steps[q - 2]] // 2).wait_send()
                        snds[a][q % 2] = accs[a][...].astype(BF16)
                        to_sibling(a, jj, q % 2).start()
                        continue
                    to_sibling(a, jj).wait_recv()
                    chip_sum = accs[a][...] + sibs[a][jj].astype(F32)
                    if rel == 6:
                        dsnds[a][...] = chip_sum.astype(BF16)
                        to_diagonal(a).start()
                    elif rel == 0:
                        own_ref[...] = chip_sum
                    else:
                        out_ref[0] = chip_sum.astype(BF16)
                    if step == N_DEV - 1:
                        for q in (2, 3):
                            to_sibling(a, WGRAD_ORDER[odd_steps[q]] // 2).wait_send()
                        to_diagonal(a).wait_recv()
                        to_diagonal(a).wait_send()

    assert WGRAD_ORDER[-1] == 0 and WGRAD_ORDER[-3:-1] == (2, 4)
    blk = pl.BlockSpec((tt, FF_BLK), lambda j, t: (t, relation(j)))
    early = lambda j, t: (jnp.where(relation(j) < FC_EARLY, t, 0), jnp.where(relation(j) < FC_EARLY, relation(j), 0))
    late = lambda j, t: (jnp.where(relation(j) < FC_EARLY, 0, t), jnp.maximum(relation(j) - FC_EARLY, 0))
    chip = lambda j, t: (jnp.clip(j - 5, 0, 1), 0, 0)
    hbm = pl.BlockSpec(memory_space=pl.ANY)
    return pl.pallas_call(
        body, name="mlp_wgrad", grid=(N_DEV, nt),
        out_shape=(jax.ShapeDtypeStruct((D, FF_BLK), F32), jax.ShapeDtypeStruct((FF_BLK, D), F32),
                   jax.ShapeDtypeStruct((2, D, FF_BLK), BF16), jax.ShapeDtypeStruct((2, FF_BLK, D), BF16),
                   jax.ShapeDtypeStruct((D, FF_BLK), BF16), jax.ShapeDtypeStruct((FF_BLK, D), BF16)),
        in_specs=[pl.BlockSpec((tt, FF_BLK), early), pl.BlockSpec((tt, FF_BLK), late), blk,
                  _resident((t_len, D)), _resident((t_len, D))],
        out_specs=(_full((D, FF_BLK)), _full((FF_BLK, D)),
                   pl.BlockSpec((1, D, FF_BLK), chip), pl.BlockSpec((1, FF_BLK, D), chip), hbm, hbm),
        scratch_shapes=[pltpu.VMEM((D, FF_BLK), F32), pltpu.VMEM((FF_BLK, D), F32),
                        pltpu.VMEM((2, D, FF_BLK), BF16), pltpu.VMEM((2, FF_BLK, D), BF16),
                        pltpu.VMEM((4, D, FF_BLK), BF16), pltpu.VMEM((4, FF_BLK, D), BF16),
                        pltpu.VMEM((D, FF_BLK), BF16), pltpu.VMEM((FF_BLK, D), BF16),
                        pltpu.SemaphoreType.DMA((10,)), pltpu.SemaphoreType.DMA((10,))],
        compiler_params=pltpu.CompilerParams(dimension_semantics=("arbitrary", "arbitrary"),
                                             vmem_limit_bytes=VMEM_LIMIT),
    )(r_early, r_late, da, df, h2)


def _acc_rows(ref, row0, k, val):
    half = CHUNK // 2
    ref[row0:row0 + half, k * GROUP:(k + 1) * GROUP] += val[:half, :]
    ref[row0:row0 + half, D_A + k * GROUP:D_A + (k + 1) * GROUP] += val[half:, :]


def _attn_bwd(tt, dmix, dx1, x, z, cat, mod, n1pre, w_in_t, w_out, w_sp, bs_rows, ln_g, ln_b, w_pool, b_pool,
              pool_scale, red_fwd, red_bwd, chip_sums):
    t_len = x.shape[0]
    nt = t_len // tt
    hb = tt // HALO
    n_sums = len(chip_sums)

    def body(dmix_ref, dx1_ref, x_ref, z_ref, zprev_ref, cat_ref, mod_ref, n1pre_ref, win_ref, wout_ref, wsp_ref,
             bs_ref, lng_ref, lnb_ref, wp_ref, bp_ref, ps_ref, redf_ref, redb_ref, *rest):
        sum_out = rest[:n_sums]
        gx_ref, gwin_ref, gwout_ref, small_ref = rest[n_sums:n_sums + 4]
        sum_in = rest[n_sums + 4:2 * n_sums + 4]
        carry, acc_in, acc_out, dz_scr, bs_acc, send_sems, recv_sems = rest[2 * n_sums + 4:]
        s = pl.program_id(0)
        i = nt - 1 - s
        px, py, pc = _place()

        def chip_copy(a, r):
            return pltpu.make_async_remote_copy(
                src_ref=sum_out[a].at[r], dst_ref=sum_in[a].at[r],
                send_sem=send_sems.at[2 * a + r], recv_sem=recv_sems.at[2 * a + r],
                device_id=_peer(px, py, pc, 2 * (r + 1)), device_id_type=MESH)

        @pl.when(s == 0)
        def _():
            for a in range(n_sums):
                for r in range(2):
                    chip_copy(a, r).start()
            carry[...] = jnp.zeros_like(carry)
            acc_in[...] = jnp.zeros_like(acc_in)
            acc_out[...] = jnp.zeros_like(acc_out)
            bs_acc[...] = jnp.zeros_like(bs_acc)
            small_ref[...] = jnp.zeros_like(small_ref)
            small_ref[ROW_DMOD + 2:ROW_DMOD + 3, :] = redb_ref[3:4, :]
            small_ref[ROW_DMOD + 3:ROW_DMOD + 5, :] = redb_ref[0:2, :]
            small_ref[ROW_DMOD + 5:ROW_DMOD + 6, :] = redf_ref[0:1, :]
            small_ref[ROW_N1POST:ROW_N1POST + 1, :] = redb_ref[4:5, :]
            small_ref[ROW_N2PRE:ROW_N2PRE + 1, :] = redb_ref[2:3, :]
            small_ref[ROW_N2POST:ROW_N2POST + 1, :] = redf_ref[1:2, :]
            small_ref[ROW_LOSS:ROW_LOSS + 1, :] = redf_ref[2:3, :]

        dmixv = dmix_ref[...]
        dcat = _dot_nt(dmixv, wout_ref[...])
        acc_out[...] += _dot_tn(cat_ref[...], dmixv)

        z = z_ref[...]
        t_g, ga = _gelu_parts(z[:, :2 * D_A])
        u, vr = ga[:, :D_A], ga[:, D_A:]
        dv0 = vr - jnp.mean(vr, axis=-1, keepdims=True)
        rv = lax.rsqrt(jnp.mean(dv0 * dv0, axis=-1, keepdims=True) + EPS)
        vhat = dv0 * rv
        vb = (vhat * lng_ref[...] + lnb_ref[...]).astype(BF16)
        mask = _tril_mask()
        wc = [(wsp_ref[h] * mask).astype(BF16) for h in range(N_HEADS)]

        dya = dcat[:, :D_A]
        for h in range(N_HEADS):
            cols = slice(h * GROUP, (h + 1) * GROUP)
            bs_sum = jnp.zeros((CHUNK, GROUP), F32)
            ws_sum = jnp.zeros((CHUNK, CHUNK), F32)
            for ch in range(tt // CHUNK):
                rows = slice(ch * CHUNK, (ch + 1) * CHUNK)
                v_ch = vb[rows, cols]
                mixed = _dot(wc[h], v_ch) + bs_ref[:, cols]
                dy_ch = dya[rows, cols]
                dz_scr[rows, cols] = dy_ch * mixed
                dmixed = dy_ch * u[rows, cols]
                dmb = dmixed.astype(BF16)
                dz_scr[rows, D_A + h * GROUP:D_A + (h + 1) * GROUP] = _dot_tn(wc[h], dmb)
                bs_sum = bs_sum + dmixed
                ws_sum = ws_sum + _dot_nt(dmb, v_ch)
            _acc_rows(bs_acc, 0, h, bs_sum)
            _acc_rows(small_ref, ROW_WS, h, ws_sum)

        dvl = dz_scr[:, D_A:2 * D_A]
        dvhat = dvl * lng_ref[...]
        dvl_vhat = dvl * vhat
        dvr = rv * (dvhat - jnp.mean(dvhat, axis=-1, keepdims=True)
                    - vhat * jnp.mean(dvl_vhat * lng_ref[...], axis=-1, keepdims=True))
        small_ref[ROW_LN:ROW_LN + 1, 0:D_A] += _colsum(dvl_vhat)
        small_ref[ROW_LN:ROW_LN + 1, D_A:D] += _colsum(dvl)
        dga = jnp.concatenate([dz_scr[:, :D_A], dvr], axis=1)
        dza = dga * _gelu_grad(z[:, :2 * D_A], t_g)

        zb = z[:, 2 * D_A:]
        halo_prev = jnp.where(i == 0, 0.0, zprev_ref[...])
        diff = _pool_diff(zb, halo_prev, i * tt)
        dyb = dcat[:, D_A:]
        inv = _inv_counts(i * tt, tt)
        scaled, ddiffs = [], []
        for g in range(len(WINDOWS)):
            cols = slice(g * GROUP, (g + 1) * GROUP)
            db = diff[g].astype(BF16)
            wpg = wp_ref[g].astype(BF16)
            pre = _dot(db, wpg) + bp_ref[:, cols]
            small_ref[ROW_POOL:ROW_POOL + 1, cols] += _colsum(dyb[:, cols] * pre)
            dpre = dyb[:, cols] * ps_ref[:, cols]
            small_ref[ROW_POOL:ROW_POOL + 1, D_B + g * GROUP:D_B + (g + 1) * GROUP] += _colsum(dpre)
            dpb = dpre.astype(BF16)
            _acc_rows(small_ref, ROW_WP, g, _dot_tn(db, dpb))
            ddiff = _dot_nt(dpb, wpg)
            ddiffs.append(ddiff)
            scaled.append(ddiff * inv[g])
        scaled_all = jnp.concatenate(scaled, axis=1)
        ext = jnp.concatenate([scaled_all, carry[...]], axis=0)
        n_ext = tt + HALO
        s2 = ext + pltpu.roll(ext, n_ext - 1, 0)
        t4 = s2[:, GROUP:]
        s4 = t4 + pltpu.roll(t4, n_ext - 2, 0)
        t8 = s4[:, GROUP:]
        s8 = t8 + pltpu.roll(t8, n_ext - 4, 0)
        t16 = s8[:, GROUP:]
        s16 = t16 + pltpu.roll(t16, n_ext - 8, 0)
        back = [s2[:, :GROUP], s4[:, :GROUP], s8[:, :GROUP], s16]
        carry[...] = scaled_all[:HALO, :]
        dzb = jnp.concatenate([back[g][:tt, :] - ddiffs[g] for g in range(len(WINDOWS))], axis=1)

        dzv = jnp.concatenate([dza, dzb], axis=1).astype(BF16)
        dh1 = _dot(dzv, win_ref[...])
        xv = x_ref[...]
        r1 = _rstd(xv)
        xhat = xv * r1
        shift1, scale1 = mod_ref[0:1, :], mod_ref[1:2, :]
        pre1 = n1pre_ref[...]
        gain1 = pre1 * (1.0 + scale1)
        h1 = (xhat * gain1 + shift1).astype(BF16)
        acc_in[...] += _dot_tn(dzv, h1)
        d_x, sum_h = _rms_bwd_gained(dh1, gain1, xhat, r1)
        gx_ref[...] = dx1_ref[...] + d_x
        small_ref[ROW_DMOD:ROW_DMOD + 1, :] += _colsum(dh1)
        small_ref[ROW_DMOD + 1:ROW_DMOD + 2, :] += pre1 * sum_h
        small_ref[ROW_N1PRE:ROW_N1PRE + 1, :] += (1.0 + scale1) * sum_h

        @pl.when(s == nt - 1)
        def _():
            gwin_ref[...] = acc_in[...].astype(BF16)
            gwout_ref[...] = acc_out[...].astype(BF16)
            bs = _unfold(bs_acc[...])
            for h in range(N_HEADS):
                small_ref[ROW_BS + h:ROW_BS + h + 1, 0:GROUP] = jnp.sum(
                    bs[:, h * GROUP:(h + 1) * GROUP].T, axis=0, keepdims=True)
            for a in range(n_sums):
                for r in range(2):
                    chip_copy(a, r).wait_recv()
                    chip_copy(a, r).wait_send()

    rev = lambda w: pl.BlockSpec((tt, w), lambda s: (nt - 1 - s, 0))
    zprev = pl.BlockSpec((HALO, D_B), lambda s: (jnp.maximum((nt - 1 - s) * hb - 1, 0), 2))
    hbm = pl.BlockSpec(memory_space=pl.ANY)
    outs = pl.pallas_call(
        body, name="attn_bwd", grid=(nt,),
        out_shape=tuple([jax.ShapeDtypeStruct((t_len, D), F32), jax.ShapeDtypeStruct((D_Z, D), BF16),
                         jax.ShapeDtypeStruct((D, D), BF16), jax.ShapeDtypeStruct((SMALL_ROWS, D), F32)]
                        + [jax.ShapeDtypeStruct(cs.shape, cs.dtype) for cs in chip_sums]),
        in_specs=[rev(D), rev(D), rev(D), rev(D_Z), zprev, rev(D), _full((8, D)), _full((1, D)),
                  _resident((D_Z, D)), _resident((D, D)), _full((N_HEADS, CHUNK, CHUNK)), _full((CHUNK, D_A)),
                  _full((1, D_A)), _full((1, D_A)), _full((len(WINDOWS), GROUP, GROUP)), _full((1, D_B)),
                  _full((1, D_B)), _full((8, D)), _full((8, D))] + [_resident(cs.shape) for cs in chip_sums],
        out_specs=tuple([rev(D), _resident((D_Z, D)), _resident((D, D)), _full((SMALL_ROWS, D))] + [hbm] * n_sums),
        scratch_shapes=[pltpu.VMEM((HALO, D_B), F32), pltpu.VMEM((D_Z, D), F32), pltpu.VMEM((D, D), F32),
                        pltpu.VMEM((tt, 2 * D_A), F32), pltpu.VMEM((CHUNK // 2, D), F32),
                        pltpu.SemaphoreType.DMA((2 * n_sums,)), pltpu.SemaphoreType.DMA((2 * n_sums,))],
        compiler_params=pltpu.CompilerParams(dimension_semantics=("arbitrary",), vmem_limit_bytes=VMEM_LIMIT),
    )(dmix, dx1, x, z, z, cat, mod, n1pre, w_in_t, w_out, w_sp, bs_rows, ln_g, ln_b, w_pool, b_pool, pool_scale,
      red_fwd, red_bwd, *chip_sums)
    return outs[:4], outs[4:]


def _adam(w, g, m, v):
    m2 = ADAM_B1 * m + (1.0 - ADAM_B1) * g
    v2 = ADAM_B2 * v + (1.0 - ADAM_B2) * (g * g)
    m_hat = m2 / (1.0 - ADAM_B1 ** ADAM_STEP)
    v_hat = v2 / (1.0 - ADAM_B2 ** ADAM_STEP)
    delta = -ADAM_LR * (m_hat / (jnp.sqrt(v_hat) + ADAM_EPS) + ADAM_WD * w)
    return delta, m2, v2


def _adamw_shard(name, rb, w, g, m, v):
    rows, cols = w.shape

    def body(w_ref, g_ref, m_ref, v_ref, d_ref, m2_ref, v2_ref):
        d_ref[...], m2_ref[...], v2_ref[...] = _adam(w_ref[...], g_ref[...], m_ref[...], v_ref[...])

    blk = pl.BlockSpec((rb, cols), lambda i: (i, 0))
    shp = jax.ShapeDtypeStruct((rows, cols), F32)
    return pl.pallas_call(
        body, name=name, grid=(rows // rb,), out_shape=(shp, shp, shp),
        in_specs=[blk] * 4, out_specs=(blk, blk, blk),
        compiler_params=pltpu.CompilerParams(dimension_semantics=("arbitrary",)),
    )(w, g, m, v)


def _adamw_ada(rb, w, sc, dmod_cols, m, v):
    rows, cols = w.shape

    def body(w_ref, sc_ref, dm_ref, m_ref, v_ref, g_ref, d_ref, m2_ref, v2_ref):
        g = _dot_tn(sc_ref[...].astype(BF16), dm_ref[...].astype(BF16))
        g_ref[...] = g
        d_ref[...], m2_ref[...], v2_ref[...] = _adam(w_ref[...], g, m_ref[...], v_ref[...])

    blk = pl.BlockSpec((rb, cols), lambda i: (i, 0))
    shp = jax.ShapeDtypeStruct((rows, cols), F32)
    return pl.pallas_call(
        body, name="adamw_ada", grid=(rows // rb,), out_shape=(shp, shp, shp, shp),
        in_specs=[blk, pl.BlockSpec((N_DEV, rb), lambda i: (0, i)), _full((N_DEV, cols)), blk, blk],
        out_specs=(blk, blk, blk, blk),
        compiler_params=pltpu.CompilerParams(dimension_semantics=("arbitrary",)),
    )(w, sc, dmod_cols, m, v)


def _unfold(acc_rows):
    return jnp.concatenate([acc_rows[:, :D_A], acc_rows[:, D_A:]], axis=0)


def _adamw_small(total, params):
    n = len(params)
    flat = [a for p in params for a in p]

    def body(*refs):
        s_ref = refs[0]
        p_refs = refs[1:1 + 3 * n]
        loss_ref = refs[1 + 3 * n]
        o_refs = refs[2 + 3 * n:]
        d_b_ada = s_ref[0:6, :]
        for b in range(1, N_DEV):
            d_b_ada = d_b_ada + s_ref[8 * b:8 * b + 6, :]
        tot = s_ref[PACK_SHIFT:PACK_ROWS, :]
        loss = jnp.sum(tot[ROW_LOSS:ROW_LOSS + 1, :], axis=-1, keepdims=True) * (0.5 / D)
        loss_ref[...] = jnp.broadcast_to(loss, (8, GROUP))
        mask = _tril_mask()
        ws = _unfold(tot[ROW_WS:ROW_WS + 64, :])
        wp = _unfold(tot[ROW_WP:ROW_WP + 64, :])
        grads = [
            d_b_ada,
            tot[ROW_N1PRE:ROW_N1PRE + 1, :], tot[ROW_N1POST:ROW_N1POST + 1, :],
            tot[ROW_N2PRE:ROW_N2PRE + 1, :], tot[ROW_N2POST:ROW_N2POST + 1, :],
            tot[ROW_LN:ROW_LN + 1, :D_A], tot[ROW_LN:ROW_LN + 1, D_A:],
            tot[ROW_POOL:ROW_POOL + 1, :D_B], tot[ROW_POOL:ROW_POOL + 1, D_B:],
            tot[ROW_BS:ROW_BS + N_HEADS, 0:GROUP],
            jnp.stack([ws[:, h * GROUP:(h + 1) * GROUP] * mask for h in range(N_HEADS)]),
            jnp.stack([wp[:, g * GROUP:(g + 1) * GROUP] for g in range(len(WINDOWS))]),
        ]
        for k in range(n):
            w_ref, m_ref, v_ref = p_refs[3 * k:3 * k + 3]
            g = grads[k]
            o_refs[4 * k][...] = g
            o_refs[4 * k + 1][...], o_refs[4 * k + 2][...], o_refs[4 * k + 3][...] = _adam(
                w_ref[...], g, m_ref[...], v_ref[...])

    vm = pl.BlockSpec(memory_space=pltpu.VMEM)
    out_shape = [jax.ShapeDtypeStruct((8, GROUP), F32)]
    for w, _, _ in params:
        out_shape += [jax.ShapeDtypeStruct(w.shape, F32)] * 4
    return pl.pallas_call(
        body, name="adamw_small", out_shape=tuple(out_shape),
        in_specs=[vm] * (1 + 3 * n), out_specs=tuple([vm] * len(out_shape)),
    )(total, *flat)


TT_ATTN_FWD = 512
TT_MLP_FWD = 512
TT_MLP = 256
TT_WGRAD = 2048
TT_ATTN_BWD = 512


def kernel(x, c, w_ada, b_ada, norm1_pre, norm1_post, w_in, w_spatial, b_spatial, ln_v_gain, ln_v_bias, w_pool, b_pool, pool_scale, w_out, norm2_pre, norm2_post, w_fc1, w_fc2, loss_target, m_w_ada, m_b_ada, m_norm1_pre, m_norm1_post, m_w_in, m_w_spatial, m_b_spatial, m_ln_v_gain, m_ln_v_bias, m_w_pool, m_b_pool, m_pool_scale, m_w_out, m_norm2_pre, m_norm2_post, m_w_fc1, m_w_fc2, v_w_ada, v_b_ada, v_norm1_pre, v_norm1_post, v_w_in, v_w_spatial, v_b_spatial, v_ln_v_gain, v_ln_v_bias, v_w_pool, v_b_pool, v_pool_scale, v_w_out, v_norm2_pre, v_norm2_post, v_w_fc1, v_w_fc2):
    t_len = x.shape[1]
    me = 4 * lax.axis_index("x") + 2 * lax.axis_index("y") + lax.axis_index("c")
    ada_cols = w_ada.shape[1]
    tt = lambda want: min(want, t_len)

    x2 = x.reshape(t_len, D)
    tgt = loss_target.reshape(t_len, D)
    row = lambda a: a.reshape(1, -1)

    b_my = lax.dynamic_slice_in_dim(b_ada, me * ada_cols, ada_cols).reshape(1, ada_cols)
    modp, sc, (g_in, g_out), fc_shards = _fwd_comm(jnp.broadcast_to(c, (8, D)), w_ada, b_my,
                                                   [w_in.T, w_out], [w_fc1, w_fc2])
    mod = jnp.concatenate([modp.reshape(6, D), jnp.zeros((2, D), F32)], axis=0)
    w_in_t = g_in.reshape(D_Z, D)
    w_out_all = g_out.reshape(D, D)

    bs_rows = jnp.repeat(b_spatial.T, GROUP, axis=1)
    attn_consts = (w_spatial, bs_rows, row(ln_v_gain), row(ln_v_bias), w_pool, row(b_pool), row(pool_scale))

    (z, cat, mix, x1), (w1_early, w2_early) = _attn_fwd(
        tt(TT_ATTN_FWD), x2, mod, row(norm1_pre), row(norm1_post), w_in_t, w_out_all, *attn_consts, fc_shards)
    (r_early, h2, f_early), (w1_late, w2_late) = _mlp_fwd_early(
        tt(TT_MLP_FWD), x1, mod, row(norm2_pre), w1_early, w2_early)
    r_late, df, da, dmix, dx1, red_fwd, red_bwd = _mlp_late_bwd(
        tt(TT_MLP), r_early, x1, h2, f_early, tgt, mix, mod, row(norm2_pre), row(norm2_post), row(norm1_post),
        w1_early, w2_early, w1_late, w2_late)
    own_w1, own_w2, sums_w1, sums_w2, diag_w1, diag_w2 = _mlp_wgrad(tt(TT_WGRAD), r_early, r_late, da, df, h2)
    (grad_x, p_in, p_out, small), (arr_w1, arr_w2) = _attn_bwd(
        tt(TT_ATTN_BWD), dmix, dx1, x2, z, cat, mod, row(norm1_pre), w_in_t, w_out_all, *attn_consts,
        red_fwd, red_bwd, [sums_w1, sums_w2])
    (grad_in_t, grad_out, total), ((grad_w1, d_w1, m_w1, v_w1), (grad_w2, d_w2, m_w2, v_w2)) = _tail_comm(
        [p_in.reshape(N_DEV, D_Z // N_DEV, D), p_out.reshape(N_DEV, D // N_DEV, D)], small, 64,
        [(w_fc1, own_w1, arr_w1, diag_w1, m_w_fc1, v_w_fc1), (w_fc2, own_w2, arr_w2, diag_w2, m_w_fc2, v_w_fc2)])

    d_out, m_out, v_out = _adamw_shard("adamw_out", 128, w_out, grad_out, m_w_out, v_w_out)
    d_in_t, m_in_t, v_in_t = _adamw_shard("adamw_in", D_Z // N_DEV, w_in.T, grad_in_t, m_w_in.T, v_w_in.T)
    dmod_all = total[0:TABLE_ROWS, :].reshape(N_DEV, 8, D)[:, :6, :].reshape(N_DEV, 6 * D)
    dmod_cols = lax.dynamic_slice_in_dim(dmod_all, me * ada_cols, ada_cols, axis=1)
    grad_ada, d_ada, m_ada, v_ada = _adamw_ada(256, w_ada, sc, dmod_cols, m_w_ada, v_w_ada)

    six = lambda a: a.reshape(6, D)
    small_params = [
        (six(b_ada), six(m_b_ada), six(v_b_ada)),
        (row(norm1_pre), row(m_norm1_pre), row(v_norm1_pre)),
        (row(norm1_post), row(m_norm1_post), row(v_norm1_post)),
        (row(norm2_pre), row(m_norm2_pre), row(v_norm2_pre)),
        (row(norm2_post), row(m_norm2_post), row(v_norm2_post)),
        (row(ln_v_gain), row(m_ln_v_gain), row(v_ln_v_gain)),
        (row(ln_v_bias), row(m_ln_v_bias), row(v_ln_v_bias)),
        (row(pool_scale), row(m_pool_scale), row(v_pool_scale)),
        (row(b_pool), row(m_b_pool), row(v_b_pool)),
        (b_spatial, m_b_spatial, v_b_spatial),
        (w_spatial, m_w_spatial, v_w_spatial),
        (w_pool, m_w_pool, v_w_pool),
    ]
    outs = _adamw_small(total, small_params)
    loss = outs[0][0, 0]
    names = ["b_ada", "norm1_pre", "norm1_post", "norm2_pre", "norm2_post", "ln_v_gain", "ln_v_bias", "pool_scale",
             "b_pool", "b_spatial", "w_spatial", "w_pool"]
    shapes = dict(b_ada=b_ada.shape, norm1_pre=norm1_pre.shape, norm1_post=norm1_post.shape,
                  norm2_pre=norm2_pre.shape, norm2_post=norm2_post.shape, ln_v_gain=ln_v_gain.shape,
                  ln_v_bias=ln_v_bias.shape, pool_scale=pool_scale.shape, b_pool=b_pool.shape,
                  b_spatial=b_spatial.shape, w_spatial=w_spatial.shape, w_pool=w_pool.shape)
    res = {}
    for k, nm in enumerate(names):
        res[nm] = tuple(o.reshape(shapes[nm]) for o in outs[1 + 4 * k:5 + 4 * k])
    res["w_ada"] = (grad_ada, d_ada, m_ada, v_ada)
    res["w_in"] = (grad_in_t.T, d_in_t.T, m_in_t.T, v_in_t.T)
    res["w_out"] = (grad_out, d_out, m_out, v_out)
    res["w_fc1"] = (grad_w1, d_w1, m_w1, v_w1)
    res["w_fc2"] = (grad_w2, d_w2, m_w2, v_w2)

    order = ["w_ada", "b_ada", "norm1_pre", "norm1_post", "w_in", "w_spatial", "b_spatial", "ln_v_gain", "ln_v_bias",
             "w_pool", "b_pool", "pool_scale", "w_out", "norm2_pre", "norm2_post", "w_fc1", "w_fc2"]
    return (loss, grad_x.reshape(x.shape),
            *[res[nm][0] for nm in order], *[res[nm][1] for nm in order],
            *[res[nm][2] for nm in order], *[res[nm][3] for nm in order])
```

```python
import functools

import jax
import jax.numpy as jnp
from jax import lax
from jax.experimental import pallas as pl
from jax.experimental.pallas import tpu as pltpu

F32 = jnp.float32
BF16 = jnp.bfloat16
MESH = pl.DeviceIdType.MESH

N_DEV = 8
D = 1024
D_A = 512
D_B = 512
D_Z = 2 * D_A + D_B
N_HEADS = 4
CHUNK = 128
WINDOWS = (2, 4, 8, 16)
GROUP = 128
D_FF = 4096
FF_BLK = D_FF // N_DEV
HALO = 16
EPS = 1e-6
VMEM_LIMIT = 60 * 1024 * 1024

ADAM_LR = 0.001
ADAM_B1 = 0.9
ADAM_B2 = 0.999
ADAM_EPS = 1e-08
ADAM_WD = 0.01
ADAM_STEP = 10

ROW_DMOD = 0
ROW_N1PRE, ROW_N1POST, ROW_N2PRE, ROW_N2POST = 8, 9, 10, 11
ROW_LN = 12
ROW_POOL = 13
ROW_LOSS = 14
ROW_BS = 16
ROW_WS = 24
ROW_WP = 88
SMALL_ROWS = 152
TABLE_ROWS = 8 * N_DEV
PACK_SHIFT = TABLE_ROWS - 8
PACK_ROWS = SMALL_ROWS + PACK_SHIFT
PACK_HALF = PACK_ROWS // 2


def _dot(a, b):
    return jnp.dot(a, b, preferred_element_type=F32)


def _dot_nt(a, b):
    return lax.dot_general(a, b, (((1,), (1,)), ((), ())), preferred_element_type=F32)


def _dot_tn(a, b):
    return lax.dot_general(a, b, (((0,), (0,)), ((), ())), preferred_element_type=F32)


def _rstd(v):
    return lax.rsqrt(jnp.mean(v * v, axis=-1, keepdims=True) + EPS)


def _rms_bwd(d_hat, hat, rstd):
    return rstd * (d_hat - hat * jnp.mean(d_hat * hat, axis=-1, keepdims=True))


def _rms_bwd_gained(g, gain, hat, rstd):
    g_hat = g * hat
    d_v = rstd * (g * gain - hat * jnp.mean(g_hat * gain, axis=-1, keepdims=True))
    return d_v, _colsum(g_hat)


_K0 = 0.7978845608028654
_K1 = 0.044715


def _gelu_parts(v):
    t = jnp.tanh(_K0 * (v + _K1 * (v * v * v)))
    return t, v * (0.5 * (1.0 + t))


def _gelu_grad(v, t):
    return 0.5 * (1.0 + t) + (0.5 * v) * (1.0 - t * t) * (_K0 * (1.0 + (3.0 * _K1) * (v * v)))


def _colsum(v):
    return jnp.sum(v, axis=0, keepdims=True)


def _full(shape):
    n = len(shape)
    return pl.BlockSpec(shape, lambda *_: (0,) * n)


def _resident(shape):
    n = len(shape)
    return pl.BlockSpec(shape, lambda *_: (0,) * n, pipeline_mode=pl.Buffered(1))


def _place():
    x, y, c = lax.axis_index("x"), lax.axis_index("y"), lax.axis_index("c")
    return x, y, c


def _flip(v, bit):
    return 1 - v if bit else v


def _peer(x, y, c, k):
    return (_flip(x, (k >> 2) & 1), _flip(y, (k >> 1) & 1), _flip(c, k & 1))


def _index(p):
    return 4 * p[0] + 2 * p[1] + p[2]


def _two_level_gather_begin(x, y, c, out_refs, send_sems, recv_sems):
    me = (x, y, c)
    sibling = (x, y, 1 - c)
    chips = [(1 - x, y), (x, 1 - y), (1 - x, 1 - y)]

    def copy(a, k, block, to):
        ref = out_refs[a].at[_index(block)]
        return pltpu.make_async_remote_copy(
            src_ref=ref, dst_ref=ref, send_sem=send_sems.at[7 * a + k], recv_sem=recv_sems.at[7 * a + k],
            device_id=to, device_id_type=MESH)

    first = []
    for a in range(len(out_refs)):
        first.append(copy(a, 0, me, sibling))
        first += [copy(a, 1 + j, me, (*chip, c)) for j, chip in enumerate(chips)]
    for cp in first:
        cp.start()
    return copy, first, me, sibling, chips


def _two_level_gather_finish(c, n, begun):
    copy, first, me, sibling, chips = begun
    passed = []
    for a in range(n):
        for j, chip in enumerate(chips):
            copy(a, 1 + j, (*chip, c), me).wait_recv()
            fwd = copy(a, 4 + j, (*chip, c), sibling)
            fwd.start()
            passed.append(fwd)
    for a in range(n):
        copy(a, 0, sibling, me).wait_recv()
        for j, chip in enumerate(chips):
            copy(a, 4 + j, (*chip, 1 - c), me).wait_recv()
    for cp in first + passed:
        cp.wait_send()


def _fwd_comm(c8, w_ada, b_my, gathered, kept):
    ncol = w_ada.shape[1]
    n_g, n_k = len(gathered), len(kept)

    def body(c_ref, w_ref, b_ref, *rest):
        g_in, k_in = rest[:n_g], rest[n_g:n_g + n_k]
        modp_ref, sc_ref = rest[n_g + n_k:n_g + n_k + 2]
        g_out = rest[n_g + n_k + 2:2 * n_g + n_k + 2]
        k_out = rest[2 * n_g + n_k + 2:2 * n_g + 2 * n_k + 2]
        cg, mg, part, send_sems, recv_sems, g_send, g_recv = rest[2 * n_g + 2 * n_k + 2:]
        x, y, c = _place()
        me = _index((x, y, c))
        for a in range(n_g):
            g_out[a][me] = g_in[a][...].astype(BF16)
        begun = _two_level_gather_begin(x, y, c, g_out, g_send, g_recv)
        for a in range(n_k):
            k_out[a][...] = k_in[a][...].astype(BF16)

        def c_copy(k):
            p = _peer(x, y, c, k)
            return pltpu.make_async_remote_copy(
                src_ref=c_ref, dst_ref=cg.at[me], send_sem=send_sems.at[k - 1], recv_sem=recv_sems.at[k - 1],
                device_id=p, device_id_type=MESH)

        def c_arrival(k):
            p = _peer(x, y, c, k)
            return pltpu.make_async_remote_copy(
                src_ref=c_ref, dst_ref=cg.at[_index(p)], send_sem=send_sems.at[k - 1], recv_sem=recv_sems.at[k - 1],
                device_id=p, device_id_type=MESH)

        def m_copy(k):
            p = _peer(x, y, c, k)
            return pltpu.make_async_remote_copy(
                src_ref=part, dst_ref=mg.at[me], send_sem=send_sems.at[6 + k], recv_sem=recv_sems.at[6 + k],
                device_id=p, device_id_type=MESH)

        def m_arrival(k):
            p = _peer(x, y, c, k)
            return pltpu.make_async_remote_copy(
                src_ref=part, dst_ref=mg.at[_index(p)], send_sem=send_sems.at[6 + k], recv_sem=recv_sems.at[6 + k],
                device_id=p, device_id_type=MESH)

        for k in range(1, N_DEV):
            c_copy(k).start()
        cg[me] = c_ref[...]
        for k in range(1, N_DEV):
            c_arrival(k).wait_recv()
        c_all = jnp.concatenate([cg[j, 0:1, :] for j in range(N_DEV)], axis=0)
        sc = c_all * jax.nn.sigmoid(c_all)
        sc_ref[...] = sc
        part[...] = _dot(sc.astype(BF16), w_ref[...].astype(BF16)) + b_ref[...]
        for k in range(1, N_DEV):
            m_copy(k).start()
        mg[me] = part[...]
        for k in range(1, N_DEV):
            m_arrival(k).wait_recv()
        for j in range(N_DEV):
            modp_ref[j:j + 1, :] = mg[j, pl.ds(me, 1), :]
        _two_level_gather_finish(c, n_g, begun)
        for k in range(1, N_DEV):
            c_copy(k).wait_send()
            m_copy(k).wait_send()

    vm = pl.BlockSpec(memory_space=pltpu.VMEM)
    outs = pl.pallas_call(
        body, name="fwd_comm",
        out_shape=tuple([jax.ShapeDtypeStruct((N_DEV, ncol), F32), jax.ShapeDtypeStruct((N_DEV, D), F32)]
                        + [jax.ShapeDtypeStruct((N_DEV,) + s.shape, BF16) for s in gathered]
                        + [jax.ShapeDtypeStruct(s.shape, BF16) for s in kept]),
        in_specs=[vm] * (3 + n_g + n_k), out_specs=tuple([vm] * (2 + n_g + n_k)),
        scratch_shapes=[
            pltpu.VMEM((N_DEV, 8, D), F32),
            pltpu.VMEM((N_DEV, N_DEV, ncol), F32),
            pltpu.VMEM((N_DEV, ncol), F32),
            pltpu.SemaphoreType.DMA((2 * (N_DEV - 1),)),
            pltpu.SemaphoreType.DMA((2 * (N_DEV - 1),)),
            pltpu.SemaphoreType.DMA((7 * n_g,)),
            pltpu.SemaphoreType.DMA((7 * n_g,)),
        ],
        compiler_params=pltpu.CompilerParams(vmem_limit_bytes=VMEM_LIMIT),
    )(c8, w_ada, b_my, *gathered, *kept)
    return outs[0], outs[1], outs[2:2 + n_g], outs[2 + n_g:]


FC_EARLY = 6
FC_HEAD = 2
R_HEAD_COLS = (FC_EARLY - FC_HEAD) * FF_BLK
WGRAD_ORDER = (7, 6, 1, 3, 5, 2, 4, 0)


def _early_col(j):
    return R_HEAD_COLS + j * FF_BLK if j < FC_HEAD else (j - FC_HEAD) * FF_BLK


class _Copies:
    def __init__(self, entries, send_sems, recv_sems):
        self.place = _place()
        self.entries, self.send_sems, self.recv_sems = entries, send_sems, recv_sems

    def _copy(self, i, arrival=False):
        src, dst, rel = self.entries[i]
        return pltpu.make_async_remote_copy(
            src_ref=dst if arrival else src, dst_ref=dst, send_sem=self.send_sems.at[i],
            recv_sem=self.recv_sems.at[i], device_id=_peer(*self.place, rel), device_id_type=MESH)

    def start(self, *which):
        for i in which:
            self._copy(i).start()

    def wait_recv(self, *which):
        for i in which:
            self._copy(i, arrival=True).wait_recv()

    def wait_send(self, *which):
        for i in which:
            self._copy(i).wait_send()


TAIL_STEPS = 8


def _tail_comm(parts, small, row_chunk, fc):
    n, n_fc = len(parts), len(fc)

    def body(*refs):
        p_refs, small_ref = refs[:n], refs[n]
        fc_in = refs[n + 1:n + 1 + 6 * n_fc]
        outs = refs[n + 1 + 6 * n_fc:]
        g_refs, total_ref = outs[:n], outs[n]
        fc_out = outs[n + 1:n + 1 + 4 * n_fc]
        scr = outs[n + 1 + 4 * n_fc:]
        from_sib = scr[0:n]
        chip_out = scr[n:2 * n]
        chip_in = scr[2 * n:3 * n]
        pack, pack_sib, halves, total_scr = scr[3 * n:3 * n + 4]
        send_a, recv_a, send_b, recv_b, send_s, recv_s = scr[3 * n + 4:]
        step = pl.program_id(0)
        x, y, c = _place()
        me = _index((x, y, c))
        sibling = (x, y, 1 - c)
        my_chip = 2 * x + y
        others = [(1 - x, y), (x, 1 - y), (1 - x, 1 - y)]
        my_half = pl.ds(pl.multiple_of(PACK_HALF * c, 8), PACK_HALF)

        def pack_to_sibling():
            return pltpu.make_async_remote_copy(
                src_ref=pack, dst_ref=pack_sib, send_sem=send_s.at[0], recv_sem=recv_s.at[0],
                device_id=sibling, device_id_type=MESH)

        def half_to_chip(r):
            return pltpu.make_async_remote_copy(
                src_ref=halves.at[my_chip], dst_ref=halves.at[my_chip],
                send_sem=send_s.at[1 + r], recv_sem=recv_s.at[1 + r],
                device_id=(*others[r], c), device_id_type=MESH)

        def half_from_chip(r):
            k = 2 * others[r][0] + others[r][1]
            return pltpu.make_async_remote_copy(
                src_ref=halves.at[k], dst_ref=halves.at[k], send_sem=send_s.at[1 + r], recv_sem=recv_s.at[1 + r],
                device_id=(*others[r], c), device_id_type=MESH)

        def total_to_sibling():
            return pltpu.make_async_remote_copy(
                src_ref=total_scr.at[my_half], dst_ref=total_scr.at[my_half],
                send_sem=send_s.at[4], recv_sem=recv_s.at[4], device_id=sibling, device_id_type=MESH)

        def total_from_sibling():
            sib_half = pl.ds(pl.multiple_of(PACK_HALF * (1 - c), 8), PACK_HALF)
            return pltpu.make_async_remote_copy(
                src_ref=total_scr.at[sib_half], dst_ref=total_scr.at[sib_half],
                send_sem=send_s.at[4], recv_sem=recv_s.at[4], device_id=sibling, device_id_type=MESH)

        def to_sibling(a, k):
            return pltpu.make_async_remote_copy(
                src_ref=p_refs[a].at[2 * k + (1 - c)], dst_ref=from_sib[a].at[k],
                send_sem=send_a.at[a], recv_sem=recv_a.at[a], device_id=sibling, device_id_type=MESH)

        def all_from_sibling(a):
            return pltpu.make_async_remote_copy(
                src_ref=from_sib[a], dst_ref=from_sib[a], send_sem=send_a.at[a], recv_sem=recv_a.at[a],
                device_id=sibling, device_id_type=MESH)

        def to_chip(a, r):
            return pltpu.make_async_remote_copy(
                src_ref=chip_out[a].at[r], dst_ref=chip_in[a].at[r],
                send_sem=send_b.at[3 * a + r], recv_sem=recv_b.at[3 * a + r],
                device_id=(*others[r], c), device_id_type=MESH)

        @pl.when(step == 0)
        def _():
            pack[0:TABLE_ROWS, :] = jnp.zeros((TABLE_ROWS, D), F32)
            pack[pl.ds(pl.multiple_of(8 * me, 8), 8), :] = small_ref[0:8, :]
            pack[TABLE_ROWS:PACK_ROWS, :] = small_ref[8:SMALL_ROWS, :]
            pack_to_sibling().start()
            for a in range(n):
                for k in range(4):
                    to_sibling(a, k).start()

        @pl.when(step == 1)
        def _():
            pack_to_sibling().wait_recv()
            halves[my_chip] = pack[my_half, :] + pack_sib[my_half, :]
            for r in range(3):
                half_to_chip(r).start()
            for a in range(n):
                all_from_sibling(a).wait_recv()
                rows = p_refs[a].shape[1]
                for r in range(3):
                    k = 2 * others[r][0] + others[r][1]
                    for s in range(0, rows, row_chunk):
                        sl = pl.ds(s, row_chunk)
                        chip_out[a][r, sl, :] = (p_refs[a][2 * k + c, sl, :].astype(F32)
                                                 + from_sib[a][k, sl, :].astype(F32)).astype(BF16)
                    to_chip(a, r).start()
                for s in range(0, rows, row_chunk):
                    sl = pl.ds(s, row_chunk)
                    g_refs[a][sl, :] = (p_refs[a][2 * my_chip + c, sl, :].astype(F32)
                                        + from_sib[a][my_chip, sl, :].astype(F32))

        for k in range(n_fc):
            w_ref, own_ref, arr_ref, diag_ref, m_ref, v_ref = fc_in[6 * k:6 * k + 6]
            g = own_ref[...]
            for r in range(2):
                g = g + arr_ref[r].astype(F32)
            g = g + diag_ref[...].astype(F32)
            fc_out[4 * k][...] = g
            fc_out[4 * k + 1][...], fc_out[4 * k + 2][...], fc_out[4 * k + 3][...] = _adam(
                w_ref[...], g, m_ref[...], v_ref[...])

        @pl.when(step == TAIL_STEPS - 1)
        def _():
            for r in range(3):
                half_from_chip(r).wait_recv()
            total_scr[my_half, :] = ((halves[0] + halves[1]) + halves[2]) + halves[3]
            total_to_sibling().start()
            for a in range(n):
                rows = p_refs[a].shape[1]
                for r in range(3):
                    to_chip(a, r).wait_recv()
                    for s in range(0, rows, row_chunk):
                        sl = pl.ds(s, row_chunk)
                        g_refs[a][sl, :] = g_refs[a][sl, :] + chip_in[a][r, sl, :].astype(F32)
            total_from_sibling().wait_recv()
            total_ref[...] = total_scr[...]
            for a in range(n):
                all_from_sibling(a).wait_send()
                for r in range(3):
                    to_chip(a, r).wait_send()
            pack_to_sibling().wait_send()
            for r in range(3):
                half_to_chip(r).wait_send()
            total_to_sibling().wait_send()

    fc_specs_in, fc_specs_out, fc_shapes, fc_args = [], [], [], []
    for w, own, arrived, diagonal, m, v in fc:
        rows, cols = w.shape
        blk = pl.BlockSpec((rows // TAIL_STEPS, cols), lambda i: (i, 0))
        fc_specs_in += [blk, blk, pl.BlockSpec((2, rows // TAIL_STEPS, cols), lambda i: (0, i, 0)), blk, blk, blk]
        fc_specs_out += [blk] * 4
        fc_shapes += [jax.ShapeDtypeStruct((rows, cols), F32)] * 4
        fc_args += [w, own, arrived, diagonal, m, v]
    outs = pl.pallas_call(
        body, name="tail_comm", grid=(TAIL_STEPS,),
        out_shape=tuple([jax.ShapeDtypeStruct(p.shape[1:], F32) for p in parts]
                        + [jax.ShapeDtypeStruct((PACK_ROWS, D), F32)] + fc_shapes),
        in_specs=[_resident(p.shape) for p in parts] + [_resident(small.shape)] + fc_specs_in,
        out_specs=tuple([_full(p.shape[1:]) for p in parts] + [_full((PACK_ROWS, D))] + fc_specs_out),
        scratch_shapes=(
            [pltpu.VMEM((4,) + p.shape[1:], BF16) for p in parts]
            + [pltpu.VMEM((3,) + p.shape[1:], BF16) for p in parts]
            + [pltpu.VMEM((3,) + p.shape[1:], BF16) for p in parts]
            + [pltpu.VMEM((PACK_ROWS, D), F32), pltpu.VMEM((PACK_ROWS, D), F32),
               pltpu.VMEM((4, PACK_HALF, D), F32), pltpu.VMEM((PACK_ROWS, D), F32)]
            + [pltpu.SemaphoreType.DMA((n,)), pltpu.SemaphoreType.DMA((n,)),
               pltpu.SemaphoreType.DMA((3 * n,)), pltpu.SemaphoreType.DMA((3 * n,)),
               pltpu.SemaphoreType.DMA((5,)), pltpu.SemaphoreType.DMA((5,))]),
        compiler_params=pltpu.CompilerParams(dimension_semantics=("arbitrary",), vmem_limit_bytes=VMEM_LIMIT),
    )(*parts, small, *fc_args)
    return outs[:n + 1], [outs[n + 1 + 4 * k:n + 5 + 4 * k] for k in range(n_fc)]


def _tril_mask():
    row = lax.broadcasted_iota(jnp.int32, (CHUNK, CHUNK), 0)
    col = lax.broadcasted_iota(jnp.int32, (CHUNK, CHUNK), 1)
    return (col <= row).astype(F32)


def _window_sums(ext):
    s2 = ext + pltpu.roll(ext, 1, 0)
    t4 = s2[:, GROUP:]
    s4 = t4 + pltpu.roll(t4, 2, 0)
    t8 = s4[:, GROUP:]
    s8 = t8 + pltpu.roll(t8, 4, 0)
    t16 = s8[:, GROUP:]
    s16 = t16 + pltpu.roll(t16, 8, 0)
    return [s2[:, :GROUP], s4[:, :GROUP], s8[:, :GROUP], s16]


def _inv_counts(first_pos, rows):
    pos = first_pos + lax.broadcasted_iota(jnp.int32, (rows, 1), 0)
    return [1.0 / jnp.minimum(pos + 1, w).astype(F32) for w in WINDOWS]


def _pool_diff(zb, halo, first_pos):
    tt = zb.shape[0]
    sums = _window_sums(jnp.concatenate([halo, zb], axis=0))
    inv = _inv_counts(first_pos, tt)
    return [sums[g][HALO:, :] * inv[g] - zb[:, g * GROUP:(g + 1) * GROUP] for g in range(len(WINDOWS))]


def _attn_fwd(tt, x, mod, n1pre, n1post, w_in_t, w_out, w_sp, bs_rows, ln_g, ln_b, w_pool, b_pool, pool_scale,
              fc_shards, n2pre):
    t_len = x.shape[0]
    nt = t_len // tt

    def body(x_ref, mod_ref, n1pre_ref, n1post_ref, win_ref, wout_ref, wsp_ref, bs_ref, lng_ref, lnb_ref,
             wp_ref, bp_ref, ps_ref, w1_ref, w2_ref, n2pre_ref,
             z_ref, cat_ref, mix_ref, x1_ref, h2_ref, r_ref, f_ref, e1_ref, e2_ref,
             carry, land1, land2, sib1, sib2, send_sems, recv_sems, local_sems):
        i = pl.program_id(0)
        copies = _Copies(
            [(w1_ref, sib1, 1), (w2_ref, sib2, 1),
             (w1_ref, land1.at[0], 2), (w2_ref, land2.at[0], 2),
             (w1_ref, land1.at[1], 4), (w2_ref, land2.at[1], 4),
             (land1.at[0], e1_ref.at[3], 1), (land2.at[0], e2_ref.at[3], 1),
             (land1.at[1], e1_ref.at[5], 1), (land2.at[1], e2_ref.at[5], 1)],
            send_sems, recv_sems)
        keep = [pltpu.make_async_copy(w1_ref, e1_ref.at[0], local_sems.at[0]),
                pltpu.make_async_copy(w2_ref, e2_ref.at[0], local_sems.at[1]),
                pltpu.make_async_copy(land1.at[0], e1_ref.at[2], local_sems.at[2]),
                pltpu.make_async_copy(land1.at[1], e1_ref.at[4], local_sems.at[3]),
                pltpu.make_async_copy(land2.at[0], e2_ref.at[2], local_sems.at[4]),
                pltpu.make_async_copy(land2.at[1], e2_ref.at[4], local_sems.at[5]),
                pltpu.make_async_copy(sib1, e1_ref.at[1], local_sems.at[6]),
                pltpu.make_async_copy(sib2, e2_ref.at[1], local_sems.at[7])]

        @pl.when(i == 0)
        def _():
            copies.start(0, 1, 2, 4, 3, 5)
            keep[0].start()
            keep[1].start()
            carry[...] = jnp.zeros_like(carry)

        @pl.when(i == nt // 2)
        def _():
            copies.wait_recv(2, 4)
            copies.start(6, 8)
            keep[2].start()
            keep[3].start()

        @pl.when(i == nt - 1)
        def _():
            copies.wait_recv(3, 5)
            copies.start(7, 9)
            keep[4].start()
            keep[5].start()

        xv = x_ref[...]
        shift1, scale1, gate1 = mod_ref[0:1, :], mod_ref[1:2, :], mod_ref[2:3, :]
        h1 = (xv * _rstd(xv)) * (n1pre_ref[...] * (1.0 + scale1)) + shift1
        z = _dot_nt(h1.astype(BF16), win_ref[...])
        z_ref[...] = z

        _, ga = _gelu_parts(z[:, :2 * D_A])
        u, vr = ga[:, :D_A], ga[:, D_A:]
        dv = vr - jnp.mean(vr, axis=-1, keepdims=True)
        v = (dv * lax.rsqrt(jnp.mean(dv * dv, axis=-1, keepdims=True) + EPS)) * lng_ref[...] + lnb_ref[...]
        vb = v.astype(BF16)
        mask = _tril_mask()
        wc = [(wsp_ref[h] * mask).astype(BF16) for h in range(N_HEADS)]
        for ch in range(tt // CHUNK):
            rows = slice(ch * CHUNK, (ch + 1) * CHUNK)
            for h in range(N_HEADS):
                cols = slice(h * GROUP, (h + 1) * GROUP)
                mixed = _dot(wc[h], vb[rows, cols]) + bs_ref[:, cols]
                cat_ref[rows, cols] = (u[rows, cols] * mixed).astype(BF16)

        zb = z[:, 2 * D_A:]
        diff = _pool_diff(zb, carry[...], i * tt)
        carry[...] = zb[tt - HALO:, :]
        for g in range(len(WINDOWS)):
            cols = slice(g * GROUP, (g + 1) * GROUP)
            pre = _dot(diff[g].astype(BF16), wp_ref[g].astype(BF16)) + bp_ref[:, cols]
            cat_ref[:, D_A + g * GROUP:D_A + (g + 1) * GROUP] = (pre * ps_ref[:, cols]).astype(BF16)

        mix = _dot(cat_ref[...], wout_ref[...])
        mix_ref[...] = mix
        x1v = xv + (mix * _rstd(mix)) * (gate1 * n1post_ref[...])
        x1_ref[...] = x1v

        @pl.when(i == 0)
        def _():
            copies.wait_recv(0, 1)
            keep[6].start()
            keep[7].start()

        shift2, scale2 = mod_ref[3:4, :], mod_ref[4:5, :]
        h2 = ((x1v * _rstd(x1v)) * (n2pre_ref[...] * (1.0 + scale2)) + shift2).astype(BF16)
        h2_ref[...] = h2
        for j, (w1, w2) in enumerate(((w1_ref, w2_ref), (sib1, sib2))):
            ra = jnp.maximum(_dot(h2, w1[...]), 0.0)
            r = (ra * ra).astype(BF16)
            r_ref[:, j * FF_BLK:(j + 1) * FF_BLK] = r
            if j == 0:
                f_ref[...] = _dot(r, w2[...])
            else:
                f_ref[...] += _dot(r, w2[...])

        @pl.when(i == nt - 1)
        def _():
            copies.wait_recv(6, 7, 8, 9)
            copies.wait_send(*range(10))
            for cp in keep:
                cp.wait()

    tile = lambda w: pl.BlockSpec((tt, w), lambda i: (i, 0))
    hbm = pl.BlockSpec(memory_space=pl.ANY)
    outs = pl.pallas_call(
        body, name="attn_fwd", grid=(nt,),
        out_shape=tuple([jax.ShapeDtypeStruct((t_len, D_Z), F32), jax.ShapeDtypeStruct((t_len, D), BF16),
                         jax.ShapeDtypeStruct((t_len, D), F32), jax.ShapeDtypeStruct((t_len, D), F32),
                         jax.ShapeDtypeStruct((t_len, D), BF16),
                         jax.ShapeDtypeStruct((t_len, FC_EARLY * FF_BLK), BF16),
                         jax.ShapeDtypeStruct((t_len, D), F32)]
                        + [jax.ShapeDtypeStruct((FC_EARLY,) + s.shape, BF16) for s in fc_shards]),
        in_specs=[tile(D), _full((8, D)), _full((1, D)), _full((1, D)), _resident((D_Z, D)), _resident((D, D)),
                  _full((N_HEADS, CHUNK, CHUNK)), _full((CHUNK, D_A)), _full((1, D_A)), _full((1, D_A)),
                  _full((len(WINDOWS), GROUP, GROUP)), _full((1, D_B)), _full((1, D_B)),
                  _resident(fc_shards[0].shape), _resident(fc_shards[1].shape), _full((1, D))],
        out_specs=(tile(D_Z), tile(D), tile(D), tile(D), tile(D),
                   pl.BlockSpec((tt, FC_HEAD * FF_BLK), lambda i: (i, R_HEAD_COLS // (FC_HEAD * FF_BLK))), tile(D),
                   hbm, hbm),
        scratch_shapes=[pltpu.VMEM((HALO, D_B), F32),
                        pltpu.VMEM((2,) + fc_shards[0].shape, BF16), pltpu.VMEM((2,) + fc_shards[1].shape, BF16),
                        pltpu.VMEM(fc_shards[0].shape, BF16), pltpu.VMEM(fc_shards[1].shape, BF16),
                        pltpu.SemaphoreType.DMA((10,)), pltpu.SemaphoreType.DMA((10,)),
                        pltpu.SemaphoreType.DMA((8,))],
        compiler_params=pltpu.CompilerParams(dimension_semantics=("arbitrary",), vmem_limit_bytes=VMEM_LIMIT),
    )(x, mod, n1pre, n1post, w_in_t, w_out, w_sp, bs_rows, ln_g, ln_b, w_pool, b_pool, pool_scale, *fc_shards,
      n2pre)
    return outs[:7], outs[7:]


def _mlp_fwd_early(tt, r_begun, h2, f_head, w1_early, w2_early):
    t_len = h2.shape[0]
    nt = t_len // tt
    n_late = N_DEV - FC_EARLY

    def body(r_begun_ref, h2_ref, fh_ref, w1_ref, w2_ref, r_ref, f_ref, l1_ref, l2_ref,
             land1, land2, send_sems, recv_sems, local_sems):
        i = pl.program_id(0)
        copies = _Copies(
            [(w1_ref.at[2], land1, 4), (w2_ref.at[4], land2, 2),
             (land1, l1_ref.at[1], 1), (land2, l2_ref.at[1], 1)],
            send_sems, recv_sems)
        keep = [pltpu.make_async_copy(land1, l1_ref.at[0], local_sems.at[0]),
                pltpu.make_async_copy(land2, l2_ref.at[0], local_sems.at[1])]

        @pl.when(i == 0)
        def _():
            copies.start(0, 1)

        @pl.when(i == nt // 2)
        def _():
            copies.wait_recv(0, 1)
            copies.start(2, 3)
            for cp in keep:
                cp.start()

        h2 = h2_ref[...]
        f_ref[...] = fh_ref[...]
        for j in range(FC_HEAD, FC_EARLY):
            ra = jnp.maximum(_dot(h2, w1_ref[j]), 0.0)
            r = (ra * ra).astype(BF16)
            r_ref[:, _early_col(j):_early_col(j) + FF_BLK] = r
            f_ref[...] += _dot(r, w2_ref[j])

        @pl.when(i == nt - 1)
        def _():
            copies.wait_recv(2, 3)
            copies.wait_send(0, 1, 2, 3)
            for cp in keep:
                cp.wait()

    tile = lambda w: pl.BlockSpec((tt, w), lambda i: (i, 0))
    hbm = pl.BlockSpec(memory_space=pl.ANY)
    outs = pl.pallas_call(
        body, name="mlp_fwd_early", grid=(nt,),
        out_shape=(jax.ShapeDtypeStruct((t_len, FC_EARLY * FF_BLK), BF16), jax.ShapeDtypeStruct((t_len, D), F32),
                   jax.ShapeDtypeStruct((n_late,) + w1_early.shape[1:], BF16),
                   jax.ShapeDtypeStruct((n_late,) + w2_early.shape[1:], BF16)),
        in_specs=[hbm, tile(D), tile(D), _resident((FC_EARLY, D, FF_BLK)), _resident((FC_EARLY, FF_BLK, D))],
        out_specs=(tile(R_HEAD_COLS), tile(D), hbm, hbm),
        input_output_aliases={0: 0},
        scratch_shapes=[pltpu.VMEM(w1_early.shape[1:], BF16), pltpu.VMEM(w2_early.shape[1:], BF16),
                        pltpu.SemaphoreType.DMA((4,)), pltpu.SemaphoreType.DMA((4,)),
                        pltpu.SemaphoreType.DMA((2,))],
        compiler_params=pltpu.CompilerParams(dimension_semantics=("arbitrary",), vmem_limit_bytes=VMEM_LIMIT),
    )(r_begun, h2, f_head, w1_early, w2_early)
    return outs[:2], outs[2:]


def _mlp_late_bwd(tt, r_early, x1, h2, f_early, tgt, mix, mod, n2pre, n2post, n1post,
                  w1_early, w2_early, w1_late, w2_late):
    t_len = x1.shape[0]
    nt = t_len // tt
    n_late = N_DEV - FC_EARLY
    late_cols = n_late * FF_BLK

    def body(re_ref, x1_ref, h2_ref, fe_ref, tgt_ref, mix_ref, mod_ref, n2pre_ref, n2post_ref,
             n1post_ref, w1e_ref, w2e_ref, w1l_ref, w2l_ref,
             rl_ref, df_ref, da_ref, dmix_ref, dx1_ref, redf_ref, redb_ref, dh2_acc):
        i = pl.program_id(0)

        @pl.when(i == 0)
        def _():
            redf_ref[...] = jnp.zeros_like(redf_ref)
            redb_ref[...] = jnp.zeros_like(redb_ref)

        x1v = x1_ref[...]
        gate1, scale2, gate2 = mod_ref[2:3, :], mod_ref[4:5, :], mod_ref[5:6, :]
        h2 = h2_ref[...]
        f = fe_ref[...]
        for j in range(n_late):
            cols = slice(j * FF_BLK, (j + 1) * FF_BLK)
            ra = jnp.maximum(_dot(h2, w1l_ref[j]), 0.0)
            r = (ra * ra).astype(BF16)
            rl_ref[:, cols] = r
            f = f + _dot(r, w2l_ref[j])
        post2 = n2post_ref[...]
        gate_post2 = gate2 * post2
        rf = _rstd(f)
        fhat = f * rf
        err = (x1v + fhat * gate_post2) - tgt_ref[...]
        dy = err * (1.0 / D)
        d_f, sum_f = _rms_bwd_gained(dy, gate_post2, fhat, rf)
        dfv = d_f.astype(BF16)
        df_ref[...] = dfv
        redf_ref[0:1, :] += post2 * sum_f
        redf_ref[1:2, :] += gate2 * sum_f
        redf_ref[2:3, :] += _colsum(err * err)

        for j in range(N_DEV):
            cols = slice(j * FF_BLK, (j + 1) * FF_BLK)
            if j < FC_EARLY:
                w1, w2, r = w1e_ref[j], w2e_ref[j], re_ref[:, _early_col(j):_early_col(j) + FF_BLK]
            else:
                jl = j - FC_EARLY
                w1, w2, r = w1l_ref[jl], w2l_ref[jl], rl_ref[:, jl * FF_BLK:(jl + 1) * FF_BLK]
            dr = _dot_nt(dfv, w2)
            da = (dr * (2.0 * jnp.sqrt(r.astype(F32)))).astype(BF16)
            da_ref[:, cols] = da
            contrib = _dot_nt(da, w1)
            if j == 0:
                dh2_acc[...] = contrib
            else:
                dh2_acc[...] += contrib
        dh2 = dh2_acc[...]
        pre2, post1 = n2pre_ref[...], n1post_ref[...]
        r2 = _rstd(x1v)
        xhat = x1v * r2
        d_x1, sum_h = _rms_bwd_gained(dh2, pre2 * (1.0 + scale2), xhat, r2)
        dx1 = dy + d_x1
        dx1_ref[...] = dx1
        mixv = mix_ref[...]
        rm = _rstd(mixv)
        mhat = mixv * rm
        d_mix, sum_m = _rms_bwd_gained(dx1, gate1 * post1, mhat, rm)
        dmix_ref[...] = d_mix.astype(BF16)
        redb_ref[0:1, :] += _colsum(dh2)
        redb_ref[1:2, :] += pre2 * sum_h
        redb_ref[2:3, :] += (1.0 + scale2) * sum_h
        redb_ref[3:4, :] += post1 * sum_m
        redb_ref[4:5, :] += gate1 * sum_m

    tile = lambda w: pl.BlockSpec((tt, w), lambda i: (i, 0))
    return pl.pallas_call(
        body, name="mlp_late_bwd", grid=(nt,),
        out_shape=(jax.ShapeDtypeStruct((t_len, late_cols), BF16), jax.ShapeDtypeStruct((t_len, D), BF16),
                   jax.ShapeDtypeStruct((t_len, D_FF), BF16), jax.ShapeDtypeStruct((t_len, D), BF16),
                   jax.ShapeDtypeStruct((t_len, D), F32), jax.ShapeDtypeStruct((8, D), F32),
                   jax.ShapeDtypeStruct((8, D), F32)),
        in_specs=[tile(FC_EARLY * FF_BLK), tile(D), tile(D), tile(D), tile(D),
                  tile(D), _full((8, D)), _full((1, D)), _full((1, D)), _full((1, D)),
                  _resident((FC_EARLY, D, FF_BLK)), _resident((FC_EARLY, FF_BLK, D)),
                  _resident((n_late, D, FF_BLK)), _resident((n_late, FF_BLK, D))],
        out_specs=(tile(late_cols), tile(D), tile(D_FF), tile(D), tile(D), _full((8, D)), _full((8, D))),
        scratch_shapes=[pltpu.VMEM((tt, D), F32)],
        compiler_params=pltpu.CompilerParams(dimension_semantics=("arbitrary",), vmem_limit_bytes=VMEM_LIMIT),
    )(r_early, x1, h2, f_early, tgt, mix, mod, n2pre, n2post, n1post, w1_early, w2_early, w1_late, w2_late)


def _mlp_wgrad(tt, r_early, r_late, da, df, h2):
    t_len = df.shape[0]
    nt = t_len // tt
    odd_steps = [j for j, rel in enumerate(WGRAD_ORDER) if rel % 2]

    def relation(j):
        rel = jnp.int32(WGRAD_ORDER[-1])
        for step in range(N_DEV - 2, -1, -1):
            rel = jnp.where(j == step, WGRAD_ORDER[step], rel)
        return rel

    def body(re_ref, rl_ref, da_ref, df_ref, h2_ref, own1_ref, own2_ref, out1_ref, out2_ref, diag1_ref, diag2_ref,
             acc1, acc2, snd1, snd2, sib1, sib2, dsnd1, dsnd2, send_sems, recv_sems):
        j, t = pl.program_id(0), pl.program_id(1)
        rows = pl.ds(pl.multiple_of(t * tt, tt), tt)
        x, y, c = _place()
        accs, snds, sibs = (acc1, acc2), (snd1, snd2), (sib1, sib2)
        dsnds, diags = (dsnd1, dsnd2), (diag1_ref, diag2_ref)

        def to_sibling(a, jj, buf=0):
            return pltpu.make_async_remote_copy(
                src_ref=snds[a].at[buf], dst_ref=sibs[a].at[jj],
                send_sem=send_sems.at[4 * a + jj], recv_sem=recv_sems.at[4 * a + jj],
                device_id=(x, y, 1 - c), device_id_type=MESH)

        def to_diagonal(a):
            return pltpu.make_async_remote_copy(
                src_ref=dsnds[a], dst_ref=diags[a], send_sem=send_sems.at[8 + a], recv_sem=recv_sems.at[8 + a],
                device_id=_peer(x, y, c, 6), device_id_type=MESH)

        @pl.when(t == 0)
        def _():
            acc2[...] = jnp.zeros_like(acc2)
            acc1[...] = jnp.zeros_like(acc1)

        for r_ref, mine in ((re_ref, relation(j) < FC_EARLY), (rl_ref, relation(j) >= FC_EARLY)):
            @pl.when(mine)
            def _():
                acc2[...] += _dot_tn(r_ref[...], df_ref[rows, :])
                acc1[...] += _dot_tn(h2_ref[rows, :], da_ref[...])

        for step, rel in enumerate(WGRAD_ORDER):
            jj = rel // 2

            @pl.when((t == nt - 1) & (j == step))
            def _():
                for a, (own_ref, out_ref) in enumerate(((own1_ref, out1_ref), (own2_ref, out2_ref))):
                    if rel % 2:
                        q = odd_steps.index(step)
                        if q >= 2:
                            to_sibling(a, WGRAD_ORDER[odd_steps[q - 2]] // 2).wait_send()
                        snds[a][q % 2] = accs[a][...].astype(BF16)
                        to_sibling(a, jj, q % 2).start()
                        continue
                    to_sibling(a, jj).wait_recv()
                    chip_sum = accs[a][...] + sibs[a][jj].astype(F32)
                    if rel == 6:
                        dsnds[a][...] = chip_sum.astype(BF16)
                        to_diagonal(a).start()
                    elif rel == 0:
                        own_ref[...] = chip_sum
                    else:
                        out_ref[0] = chip_sum.astype(BF16)
                    if step == N_DEV - 1:
                        for q in (2, 3):
                            to_sibling(a, WGRAD_ORDER[odd_steps[q]] // 2).wait_send()
                        to_diagonal(a).wait_recv()
                        to_diagonal(a).wait_send()

    assert WGRAD_ORDER[-1] == 0 and WGRAD_ORDER[-3:-1] == (2, 4)
    blk = pl.BlockSpec((tt, FF_BLK), lambda j, t: (t, relation(j)))
    early_block = lambda rel: jnp.where(rel < FC_HEAD, rel + FC_EARLY - FC_HEAD, rel - FC_HEAD)
    early = lambda j, t: (jnp.where(relation(j) < FC_EARLY, t, 0),
                          jnp.where(relation(j) < FC_EARLY, early_block(relation(j)), 0))
    late = lambda j, t: (jnp.where(relation(j) < FC_EARLY, 0, t), jnp.maximum(relation(j) - FC_EARLY, 0))
    chip = lambda j, t: (jnp.clip(j - 5, 0, 1), 0, 0)
    hbm = pl.BlockSpec(memory_space=pl.ANY)
    return pl.pallas_call(
        body, name="mlp_wgrad", grid=(N_DEV, nt),
        out_shape=(jax.ShapeDtypeStruct((D, FF_BLK), F32), jax.ShapeDtypeStruct((FF_BLK, D), F32),
                   jax.ShapeDtypeStruct((2, D, FF_BLK), BF16), jax.ShapeDtypeStruct((2, FF_BLK, D), BF16),
                   jax.ShapeDtypeStruct((D, FF_BLK), BF16), jax.ShapeDtypeStruct((FF_BLK, D), BF16)),
        in_specs=[pl.BlockSpec((tt, FF_BLK), early), pl.BlockSpec((tt, FF_BLK), late), blk,
                  _resident((t_len, D)), _resident((t_len, D))],
        out_specs=(_full((D, FF_BLK)), _full((FF_BLK, D)),
                   pl.BlockSpec((1, D, FF_BLK), chip), pl.BlockSpec((1, FF_BLK, D), chip), hbm, hbm),
        scratch_shapes=[pltpu.VMEM((D, FF_BLK), F32), pltpu.VMEM((FF_BLK, D), F32),
                        pltpu.VMEM((2, D, FF_BLK), BF16), pltpu.VMEM((2, FF_BLK, D), BF16),
                        pltpu.VMEM((4, D, FF_BLK), BF16), pltpu.VMEM((4, FF_BLK, D), BF16),
                        pltpu.VMEM((D, FF_BLK), BF16), pltpu.VMEM((FF_BLK, D), BF16),
                        pltpu.SemaphoreType.DMA((10,)), pltpu.SemaphoreType.DMA((10,))],
        compiler_params=pltpu.CompilerParams(dimension_semantics=("arbitrary", "arbitrary"),
                                             vmem_limit_bytes=VMEM_LIMIT),
    )(r_early, r_late, da, df, h2)


def _acc_rows(ref, row0, k, val):
    half = CHUNK // 2
    ref[row0:row0 + half, k * GROUP:(k + 1) * GROUP] += val[:half, :]
    ref[row0:row0 + half, D_A + k * GROUP:D_A + (k + 1) * GROUP] += val[half:, :]


def _attn_bwd(tt, dmix, dx1, x, z, cat, mod, n1pre, w_in_t, w_out, w_sp, bs_rows, ln_g, ln_b, w_pool, b_pool,
              pool_scale, red_fwd, red_bwd, chip_sums):
    t_len = x.shape[0]
    nt = t_len // tt
    hb = tt // HALO
    n_sums = len(chip_sums)

    def body(dmix_ref, dx1_ref, x_ref, z_ref, zprev_ref, cat_ref, mod_ref, n1pre_ref, win_ref, wout_ref, wsp_ref,
             bs_ref, lng_ref, lnb_ref, wp_ref, bp_ref, ps_ref, redf_ref, redb_ref, *rest):
        sum_out = rest[:n_sums]
        gx_ref, gwin_ref, gwout_ref, small_ref = rest[n_sums:n_sums + 4]
        sum_in = rest[n_sums + 4:2 * n_sums + 4]
        carry, acc_in, acc_out, dz_scr, bs_acc, send_sems, recv_sems = rest[2 * n_sums + 4:]
        s = pl.program_id(0)
        i = nt - 1 - s
        px, py, pc = _place()

        def chip_copy(a, r):
            return pltpu.make_async_remote_copy(
                src_ref=sum_out[a].at[r], dst_ref=sum_in[a].at[r],
                send_sem=send_sems.at[2 * a + r], recv_sem=recv_sems.at[2 * a + r],
                device_id=_peer(px, py, pc, 2 * (r + 1)), device_id_type=MESH)

        @pl.when(s == 0)
        def _():
            for a in range(n_sums):
                for r in range(2):
                    chip_copy(a, r).start()
            carry[...] = jnp.zeros_like(carry)
            acc_in[...] = jnp.zeros_like(acc_in)
            acc_out[...] = jnp.zeros_like(acc_out)
            bs_acc[...] = jnp.zeros_like(bs_acc)
            small_ref[...] = jnp.zeros_like(small_ref)
            small_ref[ROW_DMOD + 2:ROW_DMOD + 3, :] = redb_ref[3:4, :]
            small_ref[ROW_DMOD + 3:ROW_DMOD + 5, :] = redb_ref[0:2, :]
            small_ref[ROW_DMOD + 5:ROW_DMOD + 6, :] = redf_ref[0:1, :]
            small_ref[ROW_N1POST:ROW_N1POST + 1, :] = redb_ref[4:5, :]
            small_ref[ROW_N2PRE:ROW_N2PRE + 1, :] = redb_ref[2:3, :]
            small_ref[ROW_N2POST:ROW_N2POST + 1, :] = redf_ref[1:2, :]
            small_ref[ROW_LOSS:ROW_LOSS + 1, :] = redf_ref[2:3, :]

        dmixv = dmix_ref[...]
        dcat = _dot_nt(dmixv, wout_ref[...])
        acc_out[...] += _dot_tn(cat_ref[...], dmixv)

        z = z_ref[...]
        t_g, ga = _gelu_parts(z[:, :2 * D_A])
        u, vr = ga[:, :D_A], ga[:, D_A:]
        dv0 = vr - jnp.mean(vr, axis=-1, keepdims=True)
        rv = lax.rsqrt(jnp.mean(dv0 * dv0, axis=-1, keepdims=True) + EPS)
        vhat = dv0 * rv
        vb = (vhat * lng_ref[...] + lnb_ref[...]).astype(BF16)
        mask = _tril_mask()
        wc = [(wsp_ref[h] * mask).astype(BF16) for h in range(N_HEADS)]

        dya = dcat[:, :D_A]
        for h in range(N_HEADS):
            cols = slice(h * GROUP, (h + 1) * GROUP)
            bs_sum = jnp.zeros((CHUNK, GROUP), F32)
            ws_sum = jnp.zeros((CHUNK, CHUNK), F32)
            for ch in range(tt // CHUNK):
                rows = slice(ch * CHUNK, (ch + 1) * CHUNK)
                v_ch = vb[rows, cols]
                mixed = _dot(wc[h], v_ch) + bs_ref[:, cols]
                dy_ch = dya[rows, cols]
                dz_scr[rows, cols] = dy_ch * mixed
                dmixed = dy_ch * u[rows, cols]
                dmb = dmixed.astype(BF16)
                dz_scr[rows, D_A + h * GROUP:D_A + (h + 1) * GROUP] = _dot_tn(wc[h], dmb)
                bs_sum = bs_sum + dmixed
                ws_sum = ws_sum + _dot_nt(dmb, v_ch)
            _acc_rows(bs_acc, 0, h, bs_sum)
            _acc_rows(small_ref, ROW_WS, h, ws_sum)

        dvl = dz_scr[:, D_A:2 * D_A]
        dvhat = dvl * lng_ref[...]
        dvl_vhat = dvl * vhat
        dvr = rv * (dvhat - jnp.mean(dvhat, axis=-1, keepdims=True)
                    - vhat * jnp.mean(dvl_vhat * lng_ref[...], axis=-1, keepdims=True))
        small_ref[ROW_LN:ROW_LN + 1, 0:D_A] += _colsum(dvl_vhat)
        small_ref[ROW_LN:ROW_LN + 1, D_A:D] += _colsum(dvl)
        dga = jnp.concatenate([dz_scr[:, :D_A], dvr], axis=1)
        dza = dga * _gelu_grad(z[:, :2 * D_A], t_g)

        zb = z[:, 2 * D_A:]
        halo_prev = jnp.where(i == 0, 0.0, zprev_ref[...])
        diff = _pool_diff(zb, halo_prev, i * tt)
        dyb = dcat[:, D_A:]
        inv = _inv_counts(i * tt, tt)
        scaled, ddiffs = [], []
        for g in range(len(WINDOWS)):
            cols = slice(g * GROUP, (g + 1) * GROUP)
            db = diff[g].astype(BF16)
            wpg = wp_ref[g].astype(BF16)
            pre = _dot(db, wpg) + bp_ref[:, cols]
            small_ref[ROW_POOL:ROW_POOL + 1, cols] += _colsum(dyb[:, cols] * pre)
            dpre = dyb[:, cols] * ps_ref[:, cols]
            small_ref[ROW_POOL:ROW_POOL + 1, D_B + g * GROUP:D_B + (g + 1) * GROUP] += _colsum(dpre)
            dpb = dpre.astype(BF16)
            _acc_rows(small_ref, ROW_WP, g, _dot_tn(db, dpb))
            ddiff = _dot_nt(dpb, wpg)
            ddiffs.append(ddiff)
            scaled.append(ddiff * inv[g])
        scaled_all = jnp.concatenate(scaled, axis=1)
        ext = jnp.concatenate([scaled_all, carry[...]], axis=0)
        n_ext = tt + HALO
        s2 = ext + pltpu.roll(ext, n_ext - 1, 0)
        t4 = s2[:, GROUP:]
        s4 = t4 + pltpu.roll(t4, n_ext - 2, 0)
        t8 = s4[:, GROUP:]
        s8 = t8 + pltpu.roll(t8, n_ext - 4, 0)
        t16 = s8[:, GROUP:]
        s16 = t16 + pltpu.roll(t16, n_ext - 8, 0)
        back = [s2[:, :GROUP], s4[:, :GROUP], s8[:, :GROUP], s16]
        carry[...] = scaled_all[:HALO, :]
        dzb = jnp.concatenate([back[g][:tt, :] - ddiffs[g] for g in range(len(WINDOWS))], axis=1)

        dzv = jnp.concatenate([dza, dzb], axis=1).astype(BF16)
        dh1 = _dot(dzv, win_ref[...])
        xv = x_ref[...]
        r1 = _rstd(xv)
        xhat = xv * r1
        shift1, scale1 = mod_ref[0:1, :], mod_ref[1:2, :]
        pre1 = n1pre_ref[...]
        gain1 = pre1 * (1.0 + scale1)
        h1 = (xhat * gain1 + shift1).astype(BF16)
        acc_in[...] += _dot_tn(dzv, h1)
        d_x, sum_h = _rms_bwd_gained(dh1, gain1, xhat, r1)
        gx_ref[...] = dx1_ref[...] + d_x
        small_ref[ROW_DMOD:ROW_DMOD + 1, :] += _colsum(dh1)
        small_ref[ROW_DMOD + 1:ROW_DMOD + 2, :] += pre1 * sum_h
        small_ref[ROW_N1PRE:ROW_N1PRE + 1, :] += (1.0 + scale1) * sum_h

        @pl.when(s == nt - 1)
        def _():
            gwin_ref[...] = acc_in[...].astype(BF16)
            gwout_ref[...] = acc_out[...].astype(BF16)
            bs = _unfold(bs_acc[...])
            for h in range(N_HEADS):
                small_ref[ROW_BS + h:ROW_BS + h + 1, 0:GROUP] = jnp.sum(
                    bs[:, h * GROUP:(h + 1) * GROUP].T, axis=0, keepdims=True)
            for a in range(n_sums):
                for r in range(2):
                    chip_copy(a, r).wait_recv()
                    chip_copy(a, r).wait_send()

    rev = lambda w: pl.BlockSpec((tt, w), lambda s: (nt - 1 - s, 0))
    zprev = pl.BlockSpec((HALO, D_B), lambda s: (jnp.maximum((nt - 1 - s) * hb - 1, 0), 2))
    hbm = pl.BlockSpec(memory_space=pl.ANY)
    outs = pl.pallas_call(
        body, name="attn_bwd", grid=(nt,),
        out_shape=tuple([jax.ShapeDtypeStruct((t_len, D), F32), jax.ShapeDtypeStruct((D_Z, D), BF16),
                         jax.ShapeDtypeStruct((D, D), BF16), jax.ShapeDtypeStruct((SMALL_ROWS, D), F32)]
                        + [jax.ShapeDtypeStruct(cs.shape, cs.dtype) for cs in chip_sums]),
        in_specs=[rev(D), rev(D), rev(D), rev(D_Z), zprev, rev(D), _full((8, D)), _full((1, D)),
                  _resident((D_Z, D)), _resident((D, D)), _full((N_HEADS, CHUNK, CHUNK)), _full((CHUNK, D_A)),
                  _full((1, D_A)), _full((1, D_A)), _full((len(WINDOWS), GROUP, GROUP)), _full((1, D_B)),
                  _full((1, D_B)), _full((8, D)), _full((8, D))] + [_resident(cs.shape) for cs in chip_sums],
        out_specs=tuple([rev(D), _resident((D_Z, D)), _resident((D, D)), _full((SMALL_ROWS, D))] + [hbm] * n_sums),
        scratch_shapes=[pltpu.VMEM((HALO, D_B), F32), pltpu.VMEM((D_Z, D), F32), pltpu.VMEM((D, D), F32),
                        pltpu.VMEM((tt, 2 * D_A), F32), pltpu.VMEM((CHUNK // 2, D), F32),
                        pltpu.SemaphoreType.DMA((2 * n_sums,)), pltpu.SemaphoreType.DMA((2 * n_sums,))],
        compiler_params=pltpu.CompilerParams(dimension_semantics=("arbitrary",), vmem_limit_bytes=VMEM_LIMIT),
    )(dmix, dx1, x, z, z, cat, mod, n1pre, w_in_t, w_out, w_sp, bs_rows, ln_g, ln_b, w_pool, b_pool, pool_scale,
      red_fwd, red_bwd, *chip_sums)
    return outs[:4], outs[4:]


def _adam(w, g, m, v):
    m2 = ADAM_B1 * m + (1.0 - ADAM_B1) * g
    v2 = ADAM_B2 * v + (1.0 - ADAM_B2) * (g * g)
    m_hat = m2 / (1.0 - ADAM_B1 ** ADAM_STEP)
    v_hat = v2 / (1.0 - ADAM_B2 ** ADAM_STEP)
    delta = -ADAM_LR * (m_hat / (jnp.sqrt(v_hat) + ADAM_EPS) + ADAM_WD * w)
    return delta, m2, v2


def _adamw_shard(name, rb, w, g, m, v):
    rows, cols = w.shape

    def body(w_ref, g_ref, m_ref, v_ref, d_ref, m2_ref, v2_ref):
        d_ref[...], m2_ref[...], v2_ref[...] = _adam(w_ref[...], g_ref[...], m_ref[...], v_ref[...])

    blk = pl.BlockSpec((rb, cols), lambda i: (i, 0))
    shp = jax.ShapeDtypeStruct((rows, cols), F32)
    return pl.pallas_call(
        body, name=name, grid=(rows // rb,), out_shape=(shp, shp, shp),
        in_specs=[blk] * 4, out_specs=(blk, blk, blk),
        compiler_params=pltpu.CompilerParams(dimension_semantics=("arbitrary",)),
    )(w, g, m, v)


def _adamw_ada(rb, w, sc, dmod_cols, m, v):
    rows, cols = w.shape

    def body(w_ref, sc_ref, dm_ref, m_ref, v_ref, g_ref, d_ref, m2_ref, v2_ref):
        g = _dot_tn(sc_ref[...].astype(BF16), dm_ref[...].astype(BF16))
        g_ref[...] = g
        d_ref[...], m2_ref[...], v2_ref[...] = _adam(w_ref[...], g, m_ref[...], v_ref[...])

    blk = pl.BlockSpec((rb, cols), lambda i: (i, 0))
    shp = jax.ShapeDtypeStruct((rows, cols), F32)
    return pl.pallas_call(
        body, name="adamw_ada", grid=(rows // rb,), out_shape=(shp, shp, shp, shp),
        in_specs=[blk, pl.BlockSpec((N_DEV, rb), lambda i: (0, i)), _full((N_DEV, cols)), blk, blk],
        out_specs=(blk, blk, blk, blk),
        compiler_params=pltpu.CompilerParams(dimension_semantics=("arbitrary",)),
    )(w, sc, dmod_cols, m, v)


def _unfold(acc_rows):
    return jnp.concatenate([acc_rows[:, :D_A], acc_rows[:, D_A:]], axis=0)


def _adamw_small(total, params):
    n = len(params)
    flat = [a for p in params for a in p]

    def body(*refs):
        s_ref = refs[0]
        p_refs = refs[1:1 + 3 * n]
        loss_ref = refs[1 + 3 * n]
        o_refs = refs[2 + 3 * n:]
        d_b_ada = s_ref[0:6, :]
        for b in range(1, N_DEV):
            d_b_ada = d_b_ada + s_ref[8 * b:8 * b + 6, :]
        tot = s_ref[PACK_SHIFT:PACK_ROWS, :]
        loss = jnp.sum(tot[ROW_LOSS:ROW_LOSS + 1, :], axis=-1, keepdims=True) * (0.5 / D)
        loss_ref[...] = jnp.broadcast_to(loss, (8, GROUP))
        mask = _tril_mask()
        ws = _unfold(tot[ROW_WS:ROW_WS + 64, :])
        wp = _unfold(tot[ROW_WP:ROW_WP + 64, :])
        grads = [
            d_b_ada,
            tot[ROW_N1PRE:ROW_N1PRE + 1, :], tot[ROW_N1POST:ROW_N1POST + 1, :],
            tot[ROW_N2PRE:ROW_N2PRE + 1, :], tot[ROW_N2POST:ROW_N2POST + 1, :],
            tot[ROW_LN:ROW_LN + 1, :D_A], tot[ROW_LN:ROW_LN + 1, D_A:],
            tot[ROW_POOL:ROW_POOL + 1, :D_B], tot[ROW_POOL:ROW_POOL + 1, D_B:],
            tot[ROW_BS:ROW_BS + N_HEADS, 0:GROUP],
            jnp.stack([ws[:, h * GROUP:(h + 1) * GROUP] * mask for h in range(N_HEADS)]),
            jnp.stack([wp[:, g * GROUP:(g + 1) * GROUP] for g in range(len(WINDOWS))]),
        ]
        for k in range(n):
            w_ref, m_ref, v_ref = p_refs[3 * k:3 * k + 3]
            g = grads[k]
            o_refs[4 * k][...] = g
            o_refs[4 * k + 1][...], o_refs[4 * k + 2][...], o_refs[4 * k + 3][...] = _adam(
                w_ref[...], g, m_ref[...], v_ref[...])

    vm = pl.BlockSpec(memory_space=pltpu.VMEM)
    out_shape = [jax.ShapeDtypeStruct((8, GROUP), F32)]
    for w, _, _ in params:
        out_shape += [jax.ShapeDtypeStruct(w.shape, F32)] * 4
    return pl.pallas_call(
        body, name="adamw_small", out_shape=tuple(out_shape),
        in_specs=[vm] * (1 + 3 * n), out_specs=tuple([vm] * len(out_shape)),
    )(total, *flat)


TT_ATTN_FWD = 512
TT_MLP_FWD = 512
TT_MLP = 256
TT_WGRAD = 2048
TT_ATTN_BWD = 512


def kernel(x, c, w_ada, b_ada, norm1_pre, norm1_post, w_in, w_spatial, b_spatial, ln_v_gain, ln_v_bias, w_pool, b_pool, pool_scale, w_out, norm2_pre, norm2_post, w_fc1, w_fc2, loss_target, m_w_ada, m_b_ada, m_norm1_pre, m_norm1_post, m_w_in, m_w_spatial, m_b_spatial, m_ln_v_gain, m_ln_v_bias, m_w_pool, m_b_pool, m_pool_scale, m_w_out, m_norm2_pre, m_norm2_post, m_w_fc1, m_w_fc2, v_w_ada, v_b_ada, v_norm1_pre, v_norm1_post, v_w_in, v_w_spatial, v_b_spatial, v_ln_v_gain, v_ln_v_bias, v_w_pool, v_b_pool, v_pool_scale, v_w_out, v_norm2_pre, v_norm2_post, v_w_fc1, v_w_fc2):
    t_len = x.shape[1]
    me = 4 * lax.axis_index("x") + 2 * lax.axis_index("y") + lax.axis_index("c")
    ada_cols = w_ada.shape[1]
    tt = lambda want: min(want, t_len)

    x2 = x.reshape(t_len, D)
    tgt = loss_target.reshape(t_len, D)
    row = lambda a: a.reshape(1, -1)

    b_my = lax.dynamic_slice_in_dim(b_ada, me * ada_cols, ada_cols).reshape(1, ada_cols)
    modp, sc, (g_in, g_out), fc_shards = _fwd_comm(jnp.broadcast_to(c, (8, D)), w_ada, b_my,
                                                   [w_in.T, w_out], [w_fc1, w_fc2])
    mod = jnp.concatenate([modp.reshape(6, D), jnp.zeros((2, D), F32)], axis=0)
    w_in_t = g_in.reshape(D_Z, D)
    w_out_all = g_out.reshape(D, D)

    bs_rows = jnp.repeat(b_spatial.T, GROUP, axis=1)
    attn_consts = (w_spatial, bs_rows, row(ln_v_gain), row(ln_v_bias), w_pool, row(b_pool), row(pool_scale))

    (z, cat, mix, x1, h2, r_begun, f_head), (w1_early, w2_early) = _attn_fwd(
        tt(TT_ATTN_FWD), x2, mod, row(norm1_pre), row(norm1_post), w_in_t, w_out_all, *attn_consts, fc_shards,
        row(norm2_pre))
    (r_early, f_early), (w1_late, w2_late) = _mlp_fwd_early(
        tt(TT_MLP_FWD), r_begun, h2, f_head, w1_early, w2_early)
    r_late, df, da, dmix, dx1, red_fwd, red_bwd = _mlp_late_bwd(
        tt(TT_MLP), r_early, x1, h2, f_early, tgt, mix, mod, row(norm2_pre), row(norm2_post), row(norm1_post),
        w1_early, w2_early, w1_late, w2_late)
    own_w1, own_w2, sums_w1, sums_w2, diag_w1, diag_w2 = _mlp_wgrad(tt(TT_WGRAD), r_early, r_late, da, df, h2)
    (grad_x, p_in, p_out, small), (arr_w1, arr_w2) = _attn_bwd(
        tt(TT_ATTN_BWD), dmix, dx1, x2, z, cat, mod, row(norm1_pre), w_in_t, w_out_all, *attn_consts,
        red_fwd, red_bwd, [sums_w1, sums_w2])
    (grad_in_t, grad_out, total), ((grad_w1, d_w1, m_w1, v_w1), (grad_w2, d_w2, m_w2, v_w2)) = _tail_comm(
        [p_in.reshape(N_DEV, D_Z // N_DEV, D), p_out.reshape(N_DEV, D // N_DEV, D)], small, 64,
        [(w_fc1, own_w1, arr_w1, diag_w1, m_w_fc1, v_w_fc1), (w_fc2, own_w2, arr_w2, diag_w2, m_w_fc2, v_w_fc2)])

    d_out, m_out, v_out = _adamw_shard("adamw_out", 128, w_out, grad_out, m_w_out, v_w_out)
    d_in_t, m_in_t, v_in_t = _adamw_shard("adamw_in", D_Z // N_DEV, w_in.T, grad_in_t, m_w_in.T, v_w_in.T)
    dmod_all = total[0:TABLE_ROWS, :].reshape(N_DEV, 8, D)[:, :6, :].reshape(N_DEV, 6 * D)
    dmod_cols = lax.dynamic_slice_in_dim(dmod_all, me * ada_cols, ada_cols, axis=1)
    grad_ada, d_ada, m_ada, v_ada = _adamw_ada(256, w_ada, sc, dmod_cols, m_w_ada, v_w_ada)

    six = lambda a: a.reshape(6, D)
    small_params = [
        (six(b_ada), six(m_b_ada), six(v_b_ada)),
        (row(norm1_pre), row(m_norm1_pre), row(v_norm1_pre)),
        (row(norm1_post), row(m_norm1_post), row(v_norm1_post)),
        (row(norm2_pre), row(m_norm2_pre), row(v_norm2_pre)),
        (row(norm2_post), row(m_norm2_post), row(v_norm2_post)),
        (row(ln_v_gain), row(m_ln_v_gain), row(v_ln_v_gain)),
        (row(ln_v_bias), row(m_ln_v_bias), row(v_ln_v_bias)),
        (row(pool_scale), row(m_pool_scale), row(v_pool_scale)),
        (row(b_pool), row(m_b_pool), row(v_b_pool)),
        (b_spatial, m_b_spatial, v_b_spatial),
        (w_spatial, m_w_spatial, v_w_spatial),
        (w_pool, m_w_pool, v_w_pool),
    ]
    outs = _adamw_small(total, small_params)
    loss = outs[0][0, 0]
    names = ["b_ada", "norm1_pre", "norm1_post", "norm2_pre", "norm2_post", "ln_v_gain", "ln_v_bias", "pool_scale",
             "b_pool", "b_spatial", "w_spatial", "w_pool"]
    shapes = dict(b_ada=b_ada.shape, norm1_pre=norm1_pre.shape, norm1_post=norm1_post.shape,
                  norm2_pre=norm2_pre.shape, norm2_post=norm2_post.shape, ln_v_gain=ln_v_gain.shape,
                  ln_v_bias=ln_v_bias.shape, pool_scale=pool_scale.shape, b_pool=b_pool.shape,
                  b_spatial=b_spatial.shape, w_spatial=w_spatial.shape, w_pool=w_pool.shape)
    res = {}
    for k, nm in enumerate(names):
        res[nm] = tuple(o.reshape(shapes[nm]) for o in outs[1 + 4 * k:5 + 4 * k])
    res["w_ada"] = (grad_ada, d_ada, m_ada, v_ada)
    res["w_in"] = (grad_in_t.T, d_in_t.T, m_in_t.T, v_in_t.T)
    res["w_out"] = (grad_out, d_out, m_out, v_out)
    res["w_fc1"] = (grad_w1, d_w1, m_w1, v_w1)
    res["w_fc2"] = (grad_w2, d_w2, m_w2, v_w2)

    order = ["w_ada", "b_ada", "norm1_pre", "norm1_post", "w_in", "w_spatial", "b_spatial", "ln_v_gain", "ln_v_bias",
             "w_pool", "b_pool", "pool_scale", "w_out", "norm2_pre", "norm2_post", "w_fc1", "w_fc2"]
    return (loss, grad_x.reshape(x.shape),
            *[res[nm][0] for nm in order], *[res[nm][1] for nm in order],
            *[res[nm][2] for nm in order], *[res[nm][3] for nm in order])
```

```python
import functools

import jax
import jax.numpy as jnp
from jax import lax
from jax.experimental import pallas as pl
from jax.experimental.pallas import tpu as pltpu

F32 = jnp.float32
BF16 = jnp.bfloat16
MESH = pl.DeviceIdType.MESH

N_DEV = 8
D = 1024
D_A = 512
D_B = 512
D_Z = 2 * D_A + D_B
N_HEADS = 4
CHUNK = 128
WINDOWS = (2, 4, 8, 16)
GROUP = 128
D_FF = 4096
FF_BLK = D_FF // N_DEV
HALO = 16
EPS = 1e-6
VMEM_LIMIT = 60 * 1024 * 1024

ADAM_LR = 0.001
ADAM_B1 = 0.9
ADAM_B2 = 0.999
ADAM_EPS = 1e-08
ADAM_WD = 0.01
ADAM_STEP = 10

ROW_DMOD = 0
ROW_N1PRE, ROW_N1POST, ROW_N2PRE, ROW_N2POST = 8, 9, 10, 11
ROW_LN = 12
ROW_POOL = 13
ROW_LOSS = 14
ROW_BS = 16
ROW_WS = 24
ROW_WP = 88
SMALL_ROWS = 152
TABLE_ROWS = 8 * N_DEV
PACK_SHIFT = TABLE_ROWS - 8
PACK_ROWS = SMALL_ROWS + PACK_SHIFT
PACK_HALF = PACK_ROWS // 2


def _dot(a, b):
    return jnp.dot(a, b, preferred_element_type=F32)


def _dot_nt(a, b):
    return lax.dot_general(a, b, (((1,), (1,)), ((), ())), preferred_element_type=F32)


def _dot_tn(a, b):
    return lax.dot_general(a, b, (((0,), (0,)), ((), ())), preferred_element_type=F32)


def _rstd(v):
    return lax.rsqrt(jnp.mean(v * v, axis=-1, keepdims=True) + EPS)


def _rms_bwd(d_hat, hat, rstd):
    return rstd * (d_hat - hat * jnp.mean(d_hat * hat, axis=-1, keepdims=True))


def _rms_bwd_gained(g, gain, hat, rstd):
    g_hat = g * hat
    d_v = rstd * (g * gain - hat * jnp.mean(g_hat * gain, axis=-1, keepdims=True))
    return d_v, _colsum(g_hat)


_K0 = 0.7978845608028654
_K1 = 0.044715


def _gelu_parts(v):
    t = jnp.tanh(_K0 * (v + _K1 * (v * v * v)))
    return t, v * (0.5 * (1.0 + t))


def _gelu_grad(v, t):
    return 0.5 * (1.0 + t) + (0.5 * v) * (1.0 - t * t) * (_K0 * (1.0 + (3.0 * _K1) * (v * v)))


def _colsum(v):
    return jnp.sum(v, axis=0, keepdims=True)


def _full(shape):
    n = len(shape)
    return pl.BlockSpec(shape, lambda *_: (0,) * n)


def _resident(shape):
    n = len(shape)
    return pl.BlockSpec(shape, lambda *_: (0,) * n, pipeline_mode=pl.Buffered(1))


def _place():
    x, y, c = lax.axis_index("x"), lax.axis_index("y"), lax.axis_index("c")
    return x, y, c


def _flip(v, bit):
    return 1 - v if bit else v


def _peer(x, y, c, k):
    return (_flip(x, (k >> 2) & 1), _flip(y, (k >> 1) & 1), _flip(c, k & 1))


def _index(p):
    return 4 * p[0] + 2 * p[1] + p[2]


def _two_level_gather_begin(x, y, c, out_refs, send_sems, recv_sems):
    me = (x, y, c)
    sibling = (x, y, 1 - c)
    chips = [(1 - x, y), (x, 1 - y), (1 - x, 1 - y)]

    def copy(a, k, block, to):
        ref = out_refs[a].at[_index(block)]
        return pltpu.make_async_remote_copy(
            src_ref=ref, dst_ref=ref, send_sem=send_sems.at[7 * a + k], recv_sem=recv_sems.at[7 * a + k],
            device_id=to, device_id_type=MESH)

    first = []
    for a in range(len(out_refs)):
        first.append(copy(a, 0, me, sibling))
        first += [copy(a, 1 + j, me, (*chip, c)) for j, chip in enumerate(chips)]
    for cp in first:
        cp.start()
    return copy, first, me, sibling, chips


def _two_level_gather_finish(c, n, begun):
    copy, first, me, sibling, chips = begun
    passed = []
    for a in range(n):
        for j, chip in enumerate(chips):
            copy(a, 1 + j, (*chip, c), me).wait_recv()
            fwd = copy(a, 4 + j, (*chip, c), sibling)
            fwd.start()
            passed.append(fwd)
    for a in range(n):
        copy(a, 0, sibling, me).wait_recv()
        for j, chip in enumerate(chips):
            copy(a, 4 + j, (*chip, 1 - c), me).wait_recv()
    for cp in first + passed:
        cp.wait_send()


def _fwd_comm(c8, w_ada, b_my, gathered, kept):
    ncol = w_ada.shape[1]
    n_g, n_k = len(gathered), len(kept)

    def body(c_ref, w_ref, b_ref, *rest):
        g_in, k_in = rest[:n_g], rest[n_g:n_g + n_k]
        modp_ref, sc_ref = rest[n_g + n_k:n_g + n_k + 2]
        g_out = rest[n_g + n_k + 2:2 * n_g + n_k + 2]
        k_out = rest[2 * n_g + n_k + 2:2 * n_g + 2 * n_k + 2]
        cg, mg, part, send_sems, recv_sems, g_send, g_recv = rest[2 * n_g + 2 * n_k + 2:]
        x, y, c = _place()
        me = _index((x, y, c))
        for a in range(n_g):
            g_out[a][me] = g_in[a][...].astype(BF16)
        begun = _two_level_gather_begin(x, y, c, g_out, g_send, g_recv)
        for a in range(n_k):
            k_out[a][...] = k_in[a][...].astype(BF16)

        def c_copy(k):
            p = _peer(x, y, c, k)
            return pltpu.make_async_remote_copy(
                src_ref=c_ref, dst_ref=cg.at[me], send_sem=send_sems.at[k - 1], recv_sem=recv_sems.at[k - 1],
                device_id=p, device_id_type=MESH)

        def c_arrival(k):
            p = _peer(x, y, c, k)
            return pltpu.make_async_remote_copy(
                src_ref=c_ref, dst_ref=cg.at[_index(p)], send_sem=send_sems.at[k - 1], recv_sem=recv_sems.at[k - 1],
                device_id=p, device_id_type=MESH)

        def m_copy(k):
            p = _peer(x, y, c, k)
            return pltpu.make_async_remote_copy(
                src_ref=part, dst_ref=mg.at[me], send_sem=send_sems.at[6 + k], recv_sem=recv_sems.at[6 + k],
                device_id=p, device_id_type=MESH)

        def m_arrival(k):
            p = _peer(x, y, c, k)
            return pltpu.make_async_remote_copy(
                src_ref=part, dst_ref=mg.at[_index(p)], send_sem=send_sems.at[6 + k], recv_sem=recv_sems.at[6 + k],
                device_id=p, device_id_type=MESH)

        for k in range(1, N_DEV):
            c_copy(k).start()
        cg[me] = c_ref[...]
        for k in range(1, N_DEV):
            c_arrival(k).wait_recv()
        c_all = jnp.concatenate([cg[j, 0:1, :] for j in range(N_DEV)], axis=0)
        sc = c_all * jax.nn.sigmoid(c_all)
        sc_ref[...] = sc
        part[...] = _dot(sc.astype(BF16), w_ref[...].astype(BF16)) + b_ref[...]
        for k in range(1, N_DEV):
            m_copy(k).start()
        mg[me] = part[...]
        for k in range(1, N_DEV):
            m_arrival(k).wait_recv()
        for j in range(N_DEV):
            modp_ref[j:j + 1, :] = mg[j, pl.ds(me, 1), :]
        _two_level_gather_finish(c, n_g, begun)
        for k in range(1, N_DEV):
            c_copy(k).wait_send()
            m_copy(k).wait_send()

    vm = pl.BlockSpec(memory_space=pltpu.VMEM)
    outs = pl.pallas_call(
        body, name="fwd_comm",
        out_shape=tuple([jax.ShapeDtypeStruct((N_DEV, ncol), F32), jax.ShapeDtypeStruct((N_DEV, D), F32)]
                        + [jax.ShapeDtypeStruct((N_DEV,) + s.shape, BF16) for s in gathered]
                        + [jax.ShapeDtypeStruct(s.shape, BF16) for s in kept]),
        in_specs=[vm] * (3 + n_g + n_k), out_specs=tuple([vm] * (2 + n_g + n_k)),
        scratch_shapes=[
            pltpu.VMEM((N_DEV, 8, D), F32),
            pltpu.VMEM((N_DEV, N_DEV, ncol), F32),
            pltpu.VMEM((N_DEV, ncol), F32),
            pltpu.SemaphoreType.DMA((2 * (N_DEV - 1),)),
            pltpu.SemaphoreType.DMA((2 * (N_DEV - 1),)),
            pltpu.SemaphoreType.DMA((7 * n_g,)),
            pltpu.SemaphoreType.DMA((7 * n_g,)),
        ],
        compiler_params=pltpu.CompilerParams(vmem_limit_bytes=VMEM_LIMIT),
    )(c8, w_ada, b_my, *gathered, *kept)
    return outs[0], outs[1], outs[2:2 + n_g], outs[2 + n_g:]


FC_EARLY = 6
FC_HEAD = 2
R_HEAD_COLS = (FC_EARLY - FC_HEAD) * FF_BLK
WGRAD_ORDER = (7, 6, 1, 3, 5, 2, 4, 0)


def _early_col(j):
    return R_HEAD_COLS + j * FF_BLK if j < FC_HEAD else (j - FC_HEAD) * FF_BLK


class _Copies:
    def __init__(self, entries, send_sems, recv_sems):
        self.place = _place()
        self.entries, self.send_sems, self.recv_sems = entries, send_sems, recv_sems

    def _copy(self, i, arrival=False):
        src, dst, rel = self.entries[i]
        return pltpu.make_async_remote_copy(
            src_ref=dst if arrival else src, dst_ref=dst, send_sem=self.send_sems.at[i],
            recv_sem=self.recv_sems.at[i], device_id=_peer(*self.place, rel), device_id_type=MESH)

    def start(self, *which):
        for i in which:
            self._copy(i).start()

    def wait_recv(self, *which):
        for i in which:
            self._copy(i, arrival=True).wait_recv()

    def wait_send(self, *which):
        for i in which:
            self._copy(i).wait_send()


TAIL_STEPS = 8


def _tail_comm(parts, small, row_chunk, fc):
    n, n_fc = len(parts), len(fc)

    def body(*refs):
        p_refs, small_ref = refs[:n], refs[n]
        fc_in = refs[n + 1:n + 1 + 6 * n_fc]
        outs = refs[n + 1 + 6 * n_fc:]
        g_refs, total_ref = outs[:n], outs[n]
        fc_out = outs[n + 1:n + 1 + 4 * n_fc]
        scr = outs[n + 1 + 4 * n_fc:]
        from_sib = scr[0:n]
        chip_out = scr[n:2 * n]
        chip_in = scr[2 * n:3 * n]
        pack, pack_sib, halves, total_scr = scr[3 * n:3 * n + 4]
        send_a, recv_a, send_b, recv_b, send_s, recv_s = scr[3 * n + 4:]
        step = pl.program_id(0)
        x, y, c = _place()
        me = _index((x, y, c))
        sibling = (x, y, 1 - c)
        my_chip = 2 * x + y
        others = [(1 - x, y), (x, 1 - y), (1 - x, 1 - y)]
        my_half = pl.ds(pl.multiple_of(PACK_HALF * c, 8), PACK_HALF)

        def pack_to_sibling():
            return pltpu.make_async_remote_copy(
                src_ref=pack, dst_ref=pack_sib, send_sem=send_s.at[0], recv_sem=recv_s.at[0],
                device_id=sibling, device_id_type=MESH)

        def half_to_chip(r):
            return pltpu.make_async_remote_copy(
                src_ref=halves.at[my_chip], dst_ref=halves.at[my_chip],
                send_sem=send_s.at[1 + r], recv_sem=recv_s.at[1 + r],
                device_id=(*others[r], c), device_id_type=MESH)

        def half_from_chip(r):
            k = 2 * others[r][0] + others[r][1]
            return pltpu.make_async_remote_copy(
                src_ref=halves.at[k], dst_ref=halves.at[k], send_sem=send_s.at[1 + r], recv_sem=recv_s.at[1 + r],
                device_id=(*others[r], c), device_id_type=MESH)

        def total_to_sibling():
            return pltpu.make_async_remote_copy(
                src_ref=total_scr.at[my_half], dst_ref=total_scr.at[my_half],
                send_sem=send_s.at[4], recv_sem=recv_s.at[4], device_id=sibling, device_id_type=MESH)

        def total_from_sibling():
            sib_half = pl.ds(pl.multiple_of(PACK_HALF * (1 - c), 8), PACK_HALF)
            return pltpu.make_async_remote_copy(
                src_ref=total_scr.at[sib_half], dst_ref=total_scr.at[sib_half],
                send_sem=send_s.at[4], recv_sem=recv_s.at[4], device_id=sibling, device_id_type=MESH)

        def to_sibling(a, k):
            return pltpu.make_async_remote_copy(
                src_ref=p_refs[a].at[2 * k + (1 - c)], dst_ref=from_sib[a].at[k],
                send_sem=send_a.at[a], recv_sem=recv_a.at[a], device_id=sibling, device_id_type=MESH)

        def all_from_sibling(a):
            return pltpu.make_async_remote_copy(
                src_ref=from_sib[a], dst_ref=from_sib[a], send_sem=send_a.at[a], recv_sem=recv_a.at[a],
                device_id=sibling, device_id_type=MESH)

        def to_chip(a, r):
            return pltpu.make_async_remote_copy(
                src_ref=chip_out[a].at[r], dst_ref=chip_in[a].at[r],
                send_sem=send_b.at[3 * a + r], recv_sem=recv_b.at[3 * a + r],
                device_id=(*others[r], c), device_id_type=MESH)

        @pl.when(step == 0)
        def _():
            pack[0:TABLE_ROWS, :] = jnp.zeros((TABLE_ROWS, D), F32)
            pack[pl.ds(pl.multiple_of(8 * me, 8), 8), :] = small_ref[0:8, :]
            pack[TABLE_ROWS:PACK_ROWS, :] = small_ref[8:SMALL_ROWS, :]
            pack_to_sibling().start()
            for a in range(n):
                for k in range(4):
                    to_sibling(a, k).start()

        @pl.when(step == 1)
        def _():
            pack_to_sibling().wait_recv()
            halves[my_chip] = pack[my_half, :] + pack_sib[my_half, :]
            for r in range(3):
                half_to_chip(r).start()
            for a in range(n):
                all_from_sibling(a).wait_recv()
                rows = p_refs[a].shape[1]
                for r in range(3):
                    k = 2 * others[r][0] + others[r][1]
                    for s in range(0, rows, row_chunk):
                        sl = pl.ds(s, row_chunk)
                        chip_out[a][r, sl, :] = (p_refs[a][2 * k + c, sl, :].astype(F32)
                                                 + from_sib[a][k, sl, :].astype(F32)).astype(BF16)
                    to_chip(a, r).start()
                for s in range(0, rows, row_chunk):
                    sl = pl.ds(s, row_chunk)
                    g_refs[a][sl, :] = (p_refs[a][2 * my_chip + c, sl, :].astype(F32)
                                        + from_sib[a][my_chip, sl, :].astype(F32))

        for k in range(n_fc):
            w_ref, own_ref, arr_ref, diag_ref, m_ref, v_ref = fc_in[6 * k:6 * k + 6]
            g = own_ref[...]
            for r in range(2):
                g = g + arr_ref[r].astype(F32)
            g = g + diag_ref[...].astype(F32)
            fc_out[4 * k][...] = g
            fc_out[4 * k + 1][...], fc_out[4 * k + 2][...], fc_out[4 * k + 3][...] = _adam(
                w_ref[...], g, m_ref[...], v_ref[...])

        @pl.when(step == TAIL_STEPS - 1)
        def _():
            for r in range(3):
                half_from_chip(r).wait_recv()
            total_scr[my_half, :] = ((halves[0] + halves[1]) + halves[2]) + halves[3]
            total_to_sibling().start()
            for a in range(n):
                rows = p_refs[a].shape[1]
                for r in range(3):
                    to_chip(a, r).wait_recv()
                    for s in range(0, rows, row_chunk):
                        sl = pl.ds(s, row_chunk)
                        g_refs[a][sl, :] = g_refs[a][sl, :] + chip_in[a][r, sl, :].astype(F32)
            total_from_sibling().wait_recv()
            total_ref[...] = total_scr[...]
            for a in range(n):
                all_from_sibling(a).wait_send()
                for r in range(3):
                    to_chip(a, r).wait_send()
            pack_to_sibling().wait_send()
            for r in range(3):
                half_to_chip(r).wait_send()
            total_to_sibling().wait_send()

    fc_specs_in, fc_specs_out, fc_shapes, fc_args = [], [], [], []
    for w, own, arrived, diagonal, m, v in fc:
        rows, cols = w.shape
        blk = pl.BlockSpec((rows // TAIL_STEPS, cols), lambda i: (i, 0))
        fc_specs_in += [blk, blk, pl.BlockSpec((2, rows // TAIL_STEPS, cols), lambda i: (0, i, 0)), blk, blk, blk]
        fc_specs_out += [blk] * 4
        fc_shapes += [jax.ShapeDtypeStruct((rows, cols), F32)] * 4
        fc_args += [w, own, arrived, diagonal, m, v]
    outs = pl.pallas_call(
        body, name="tail_comm", grid=(TAIL_STEPS,),
        out_shape=tuple([jax.ShapeDtypeStruct(p.shape[1:], F32) for p in parts]
                        + [jax.ShapeDtypeStruct((PACK_ROWS, D), F32)] + fc_shapes),
        in_specs=[_resident(p.shape) for p in parts] + [_resident(small.shape)] + fc_specs_in,
        out_specs=tuple([_full(p.shape[1:]) for p in parts] + [_full((PACK_ROWS, D))] + fc_specs_out),
        scratch_shapes=(
            [pltpu.VMEM((4,) + p.shape[1:], BF16) for p in parts]
            + [pltpu.VMEM((3,) + p.shape[1:], BF16) for p in parts]
            + [pltpu.VMEM((3,) + p.shape[1:], BF16) for p in parts]
            + [pltpu.VMEM((PACK_ROWS, D), F32), pltpu.VMEM((PACK_ROWS, D), F32),
               pltpu.VMEM((4, PACK_HALF, D), F32), pltpu.VMEM((PACK_ROWS, D), F32)]
            + [pltpu.SemaphoreType.DMA((n,)), pltpu.SemaphoreType.DMA((n,)),
               pltpu.SemaphoreType.DMA((3 * n,)), pltpu.SemaphoreType.DMA((3 * n,)),
               pltpu.SemaphoreType.DMA((5,)), pltpu.SemaphoreType.DMA((5,))]),
        compiler_params=pltpu.CompilerParams(dimension_semantics=("arbitrary",), vmem_limit_bytes=VMEM_LIMIT),
    )(*parts, small, *fc_args)
    return outs[:n + 1], [outs[n + 1 + 4 * k:n + 5 + 4 * k] for k in range(n_fc)]


def _tril_mask():
    row = lax.broadcasted_iota(jnp.int32, (CHUNK, CHUNK), 0)
    col = lax.broadcasted_iota(jnp.int32, (CHUNK, CHUNK), 1)
    return (col <= row).astype(F32)


def _window_sums(ext):
    s2 = ext + pltpu.roll(ext, 1, 0)
    t4 = s2[:, GROUP:]
    s4 = t4 + pltpu.roll(t4, 2, 0)
    t8 = s4[:, GROUP:]
    s8 = t8 + pltpu.roll(t8, 4, 0)
    t16 = s8[:, GROUP:]
    s16 = t16 + pltpu.roll(t16, 8, 0)
    return [s2[:, :GROUP], s4[:, :GROUP], s8[:, :GROUP], s16]


def _inv_counts(first_pos, rows):
    pos = first_pos + lax.broadcasted_iota(jnp.int32, (rows, 1), 0)
    return [1.0 / jnp.minimum(pos + 1, w).astype(F32) for w in WINDOWS]


def _pool_diff(zb, halo, first_pos):
    tt = zb.shape[0]
    sums = _window_sums(jnp.concatenate([halo, zb], axis=0))
    inv = _inv_counts(first_pos, tt)
    return [sums[g][HALO:, :] * inv[g] - zb[:, g * GROUP:(g + 1) * GROUP] for g in range(len(WINDOWS))]


def _attn_fwd(tt, x, mod, n1pre, n1post, w_in_t, w_out, w_sp, bs_rows, ln_g, ln_b, w_pool, b_pool, pool_scale,
              fc_shards, n2pre):
    t_len = x.shape[0]
    nt = t_len // tt

    def body(x_ref, mod_ref, n1pre_ref, n1post_ref, win_ref, wout_ref, wsp_ref, bs_ref, lng_ref, lnb_ref,
             wp_ref, bp_ref, ps_ref, w1_ref, w2_ref, n2pre_ref,
             z_ref, cat_ref, mix_ref, x1_ref, h2_ref, r_ref, f_ref, e1_ref, e2_ref,
             carry, land1, land2, sib1, sib2, send_sems, recv_sems, local_sems):
        i = pl.program_id(0)
        copies = _Copies(
            [(w1_ref, sib1, 1), (w2_ref, sib2, 1),
             (w1_ref, land1.at[0], 2), (w2_ref, land2.at[0], 2),
             (w1_ref, land1.at[1], 4), (w2_ref, land2.at[1], 4),
             (land1.at[0], e1_ref.at[3], 1), (land2.at[0], e2_ref.at[3], 1),
             (land1.at[1], e1_ref.at[5], 1), (land2.at[1], e2_ref.at[5], 1)],
            send_sems, recv_sems)
        keep = [pltpu.make_async_copy(w1_ref, e1_ref.at[0], local_sems.at[0]),
                pltpu.make_async_copy(w2_ref, e2_ref.at[0], local_sems.at[1]),
                pltpu.make_async_copy(land1.at[0], e1_ref.at[2], local_sems.at[2]),
                pltpu.make_async_copy(land1.at[1], e1_ref.at[4], local_sems.at[3]),
                pltpu.make_async_copy(land2.at[0], e2_ref.at[2], local_sems.at[4]),
                pltpu.make_async_copy(land2.at[1], e2_ref.at[4], local_sems.at[5]),
                pltpu.make_async_copy(sib1, e1_ref.at[1], local_sems.at[6]),
                pltpu.make_async_copy(sib2, e2_ref.at[1], local_sems.at[7])]

        @pl.when(i == 0)
        def _():
            copies.start(0, 1, 2, 4, 3, 5)
            keep[0].start()
            keep[1].start()
            carry[...] = jnp.zeros_like(carry)

        @pl.when(i == nt // 2)
        def _():
            copies.wait_recv(2, 4)
            copies.start(6, 8)
            keep[2].start()
            keep[3].start()

        @pl.when(i == nt - 1)
        def _():
            copies.wait_recv(3, 5)
            copies.start(7, 9)
            keep[4].start()
            keep[5].start()

        xv = x_ref[...]
        shift1, scale1, gate1 = mod_ref[0:1, :], mod_ref[1:2, :], mod_ref[2:3, :]
        h1 = (xv * _rstd(xv)) * (n1pre_ref[...] * (1.0 + scale1)) + shift1
        z = _dot_nt(h1.astype(BF16), win_ref[...])
        z_ref[...] = z

        _, ga = _gelu_parts(z[:, :2 * D_A])
        u, vr = ga[:, :D_A], ga[:, D_A:]
        dv = vr - jnp.mean(vr, axis=-1, keepdims=True)
        v = (dv * lax.rsqrt(jnp.mean(dv * dv, axis=-1, keepdims=True) + EPS)) * lng_ref[...] + lnb_ref[...]
        vb = v.astype(BF16)
        mask = _tril_mask()
        wc = [(wsp_ref[h] * mask).astype(BF16) for h in range(N_HEADS)]
        for ch in range(tt // CHUNK):
            rows = slice(ch * CHUNK, (ch + 1) * CHUNK)
            for h in range(N_HEADS):
                cols = slice(h * GROUP, (h + 1) * GROUP)
                mixed = _dot(wc[h], vb[rows, cols]) + bs_ref[:, cols]
                cat_ref[rows, cols] = (u[rows, cols] * mixed).astype(BF16)

        zb = z[:, 2 * D_A:]
        diff = _pool_diff(zb, carry[...], i * tt)
        carry[...] = zb[tt - HALO:, :]
        for g in range(len(WINDOWS)):
            cols = slice(g * GROUP, (g + 1) * GROUP)
            pre = _dot(diff[g].astype(BF16), wp_ref[g].astype(BF16)) + bp_ref[:, cols]
            cat_ref[:, D_A + g * GROUP:D_A + (g + 1) * GROUP] = (pre * ps_ref[:, cols]).astype(BF16)

        mix = _dot(cat_ref[...], wout_ref[...])
        mix_ref[...] = mix
        x1v = xv + (mix * _rstd(mix)) * (gate1 * n1post_ref[...])
        x1_ref[...] = x1v

        @pl.when(i == 0)
        def _():
            copies.wait_recv(0, 1)
            keep[6].start()
            keep[7].start()

        shift2, scale2 = mod_ref[3:4, :], mod_ref[4:5, :]
        h2 = ((x1v * _rstd(x1v)) * (n2pre_ref[...] * (1.0 + scale2)) + shift2).astype(BF16)
        h2_ref[...] = h2
        for j, (w1, w2) in enumerate(((w1_ref, w2_ref), (sib1, sib2))):
            ra = jnp.maximum(_dot(h2, w1[...]), 0.0)
            r = (ra * ra).astype(BF16)
            r_ref[:, j * FF_BLK:(j + 1) * FF_BLK] = r
            if j == 0:
                f_ref[...] = _dot(r, w2[...])
            else:
                f_ref[...] += _dot(r, w2[...])

        @pl.when(i == nt - 1)
        def _():
            copies.wait_recv(6, 7, 8, 9)
            copies.wait_send(*range(10))
            for cp in keep:
                cp.wait()

    tile = lambda w: pl.BlockSpec((tt, w), lambda i: (i, 0))
    hbm = pl.BlockSpec(memory_space=pl.ANY)
    outs = pl.pallas_call(
        body, name="attn_fwd", grid=(nt,),
        out_shape=tuple([jax.ShapeDtypeStruct((t_len, D_Z), F32), jax.ShapeDtypeStruct((t_len, D), BF16),
                         jax.ShapeDtypeStruct((t_len, D), F32), jax.ShapeDtypeStruct((t_len, D), F32),
                         jax.ShapeDtypeStruct((t_len, D), BF16),
                         jax.ShapeDtypeStruct((t_len, FC_EARLY * FF_BLK), BF16),
                         jax.ShapeDtypeStruct((t_len, D), F32)]
                        + [jax.ShapeDtypeStruct((FC_EARLY,) + s.shape, BF16) for s in fc_shards]),
        in_specs=[tile(D), _full((8, D)), _full((1, D)), _full((1, D)), _resident((D_Z, D)), _resident((D, D)),
                  _full((N_HEADS, CHUNK, CHUNK)), _full((CHUNK, D_A)), _full((1, D_A)), _full((1, D_A)),
                  _full((len(WINDOWS), GROUP, GROUP)), _full((1, D_B)), _full((1, D_B)),
                  _resident(fc_shards[0].shape), _resident(fc_shards[1].shape), _full((1, D))],
        out_specs=(tile(D_Z), tile(D), tile(D), tile(D), tile(D),
                   pl.BlockSpec((tt, FC_HEAD * FF_BLK), lambda i: (i, R_HEAD_COLS // (FC_HEAD * FF_BLK))), tile(D),
                   hbm, hbm),
        scratch_shapes=[pltpu.VMEM((HALO, D_B), F32),
                        pltpu.VMEM((2,) + fc_shards[0].shape, BF16), pltpu.VMEM((2,) + fc_shards[1].shape, BF16),
                        pltpu.VMEM(fc_shards[0].shape, BF16), pltpu.VMEM(fc_shards[1].shape, BF16),
                        pltpu.SemaphoreType.DMA((10,)), pltpu.SemaphoreType.DMA((10,)),
                        pltpu.SemaphoreType.DMA((8,))],
        compiler_params=pltpu.CompilerParams(dimension_semantics=("arbitrary",), vmem_limit_bytes=VMEM_LIMIT),
    )(x, mod, n1pre, n1post, w_in_t, w_out, w_sp, bs_rows, ln_g, ln_b, w_pool, b_pool, pool_scale, *fc_shards,
      n2pre)
    return outs[:7], outs[7:]


def _mlp_fwd_early(tt, r_begun, h2, f_head, w1_early, w2_early):
    t_len = h2.shape[0]
    nt = t_len // tt
    n_late = N_DEV - FC_EARLY

    def body(r_begun_ref, h2_ref, fh_ref, w1_ref, w2_ref, r_ref, f_ref, l1_ref, l2_ref,
             land1, land2, send_sems, recv_sems, local_sems):
        i = pl.program_id(0)
        copies = _Copies(
            [(w1_ref.at[2], land1, 4), (w2_ref.at[4], land2, 2),
             (land1, l1_ref.at[1], 1), (land2, l2_ref.at[1], 1)],
            send_sems, recv_sems)
        keep = [pltpu.make_async_copy(land1, l1_ref.at[0], local_sems.at[0]),
                pltpu.make_async_copy(land2, l2_ref.at[0], local_sems.at[1])]

        @pl.when(i == 0)
        def _():
            copies.start(0, 1)

        @pl.when(i == nt - 1)
        def _():
            copies.wait_recv(0, 1)
            copies.start(2, 3)
            for cp in keep:
                cp.start()

        h2 = h2_ref[...]
        f_ref[...] = fh_ref[...]
        for j in range(FC_HEAD, FC_EARLY):
            ra = jnp.maximum(_dot(h2, w1_ref[j]), 0.0)
            r = (ra * ra).astype(BF16)
            r_ref[:, _early_col(j):_early_col(j) + FF_BLK] = r
            f_ref[...] += _dot(r, w2_ref[j])

        @pl.when(i == nt - 1)
        def _():
            copies.wait_recv(2, 3)
            copies.wait_send(0, 1, 2, 3)
            for cp in keep:
                cp.wait()

    tile = lambda w: pl.BlockSpec((tt, w), lambda i: (i, 0))
    hbm = pl.BlockSpec(memory_space=pl.ANY)
    outs = pl.pallas_call(
        body, name="mlp_fwd_early", grid=(nt,),
        out_shape=(jax.ShapeDtypeStruct((t_len, FC_EARLY * FF_BLK), BF16), jax.ShapeDtypeStruct((t_len, D), F32),
                   jax.ShapeDtypeStruct((n_late,) + w1_early.shape[1:], BF16),
                   jax.ShapeDtypeStruct((n_late,) + w2_early.shape[1:], BF16)),
        in_specs=[hbm, tile(D), tile(D), _resident((FC_EARLY, D, FF_BLK)), _resident((FC_EARLY, FF_BLK, D))],
        out_specs=(tile(R_HEAD_COLS), tile(D), hbm, hbm),
        input_output_aliases={0: 0},
        scratch_shapes=[pltpu.VMEM(w1_early.shape[1:], BF16), pltpu.VMEM(w2_early.shape[1:], BF16),
                        pltpu.SemaphoreType.DMA((4,)), pltpu.SemaphoreType.DMA((4,)),
                        pltpu.SemaphoreType.DMA((2,))],
        compiler_params=pltpu.CompilerParams(dimension_semantics=("arbitrary",), vmem_limit_bytes=VMEM_LIMIT),
    )(r_begun, h2, f_head, w1_early, w2_early)
    return outs[:2], outs[2:]


def _mlp_late_bwd(tt, r_early, x1, h2, f_early, tgt, mix, mod, n2pre, n2post, n1post,
                  w1_early, w2_early, w1_late, w2_late):
    t_len = x1.shape[0]
    nt = t_len // tt
    n_late = N_DEV - FC_EARLY
    late_cols = n_late * FF_BLK

    def body(re_ref, x1_ref, h2_ref, fe_ref, tgt_ref, mix_ref, mod_ref, n2pre_ref, n2post_ref,
             n1post_ref, w1e_ref, w2e_ref, w1l_ref, w2l_ref,
             rl_ref, df_ref, da_ref, dmix_ref, dx1_ref, redf_ref, redb_ref, dh2_acc):
        i = pl.program_id(0)

        @pl.when(i == 0)
        def _():
            redf_ref[...] = jnp.zeros_like(redf_ref)
            redb_ref[...] = jnp.zeros_like(redb_ref)

        x1v = x1_ref[...]
        gate1, scale2, gate2 = mod_ref[2:3, :], mod_ref[4:5, :], mod_ref[5:6, :]
        h2 = h2_ref[...]
        f = fe_ref[...]
        for j in range(n_late):
            cols = slice(j * FF_BLK, (j + 1) * FF_BLK)
            ra = jnp.maximum(_dot(h2, w1l_ref[j]), 0.0)
            r = (ra * ra).astype(BF16)
            rl_ref[:, cols] = r
            f = f + _dot(r, w2l_ref[j])
        post2 = n2post_ref[...]
        gate_post2 = gate2 * post2
        rf = _rstd(f)
        fhat = f * rf
        err = (x1v + fhat * gate_post2) - tgt_ref[...]
        dy = err * (1.0 / D)
        d_f, sum_f = _rms_bwd_gained(dy, gate_post2, fhat, rf)
        dfv = d_f.astype(BF16)
        df_ref[...] = dfv
        redf_ref[0:1, :] += post2 * sum_f
        redf_ref[1:2, :] += gate2 * sum_f
        redf_ref[2:3, :] += _colsum(err * err)

        for j in range(N_DEV):
            cols = slice(j * FF_BLK, (j + 1) * FF_BLK)
            if j < FC_EARLY:
                w1, w2, r = w1e_ref[j], w2e_ref[j], re_ref[:, _early_col(j):_early_col(j) + FF_BLK]
            else:
                jl = j - FC_EARLY
                w1, w2, r = w1l_ref[jl], w2l_ref[jl], rl_ref[:, jl * FF_BLK:(jl + 1) * FF_BLK]
            dr = _dot_nt(dfv, w2)
            da = (dr * (2.0 * jnp.sqrt(r.astype(F32)))).astype(BF16)
            da_ref[:, cols] = da
            contrib = _dot_nt(da, w1)
            if j == 0:
                dh2_acc[...] = contrib
            else:
                dh2_acc[...] += contrib
        dh2 = dh2_acc[...]
        pre2, post1 = n2pre_ref[...], n1post_ref[...]
        r2 = _rstd(x1v)
        xhat = x1v * r2
        d_x1, sum_h = _rms_bwd_gained(dh2, pre2 * (1.0 + scale2), xhat, r2)
        dx1 = dy + d_x1
        dx1_ref[...] = dx1
        mixv = mix_ref[...]
        rm = _rstd(mixv)
        mhat = mixv * rm
        d_mix, sum_m = _rms_bwd_gained(dx1, gate1 * post1, mhat, rm)
        dmix_ref[...] = d_mix.astype(BF16)
        redb_ref[0:1, :] += _colsum(dh2)
        redb_ref[1:2, :] += pre2 * sum_h
        redb_ref[2:3, :] += (1.0 + scale2) * sum_h
        redb_ref[3:4, :] += post1 * sum_m
        redb_ref[4:5, :] += gate1 * sum_m

    tile = lambda w: pl.BlockSpec((tt, w), lambda i: (i, 0))
    return pl.pallas_call(
        body, name="mlp_late_bwd", grid=(nt,),
        out_shape=(jax.ShapeDtypeStruct((t_len, late_cols), BF16), jax.ShapeDtypeStruct((t_len, D), BF16),
                   jax.ShapeDtypeStruct((t_len, D_FF), BF16), jax.ShapeDtypeStruct((t_len, D), BF16),
                   jax.ShapeDtypeStruct((t_len, D), F32), jax.ShapeDtypeStruct((8, D), F32),
                   jax.ShapeDtypeStruct((8, D), F32)),
        in_specs=[tile(FC_EARLY * FF_BLK), tile(D), tile(D), tile(D), tile(D),
                  tile(D), _full((8, D)), _full((1, D)), _full((1, D)), _full((1, D)),
                  _resident((FC_EARLY, D, FF_BLK)), _resident((FC_EARLY, FF_BLK, D)),
                  _resident((n_late, D, FF_BLK)), _resident((n_late, FF_BLK, D))],
        out_specs=(tile(late_cols), tile(D), tile(D_FF), tile(D), tile(D), _full((8, D)), _full((8, D))),
        scratch_shapes=[pltpu.VMEM((tt, D), F32)],
        compiler_params=pltpu.CompilerParams(dimension_semantics=("arbitrary",), vmem_limit_bytes=VMEM_LIMIT),
    )(r_early, x1, h2, f_early, tgt, mix, mod, n2pre, n2post, n1post, w1_early, w2_early, w1_late, w2_late)


def _mlp_wgrad(tt, r_early, r_late, da, df, h2):
    t_len = df.shape[0]
    nt = t_len // tt
    odd_steps = [j for j, rel in enumerate(WGRAD_ORDER) if rel % 2]

    def relation(j):
        rel = jnp.int32(WGRAD_ORDER[-1])
        for step in range(N_DEV - 2, -1, -1):
            rel = jnp.where(j == step, WGRAD_ORDER[step], rel)
        return rel

    def body(re_ref, rl_ref, da_ref, df_ref, h2_ref, own1_ref, own2_ref, out1_ref, out2_ref, diag1_ref, diag2_ref,
             acc1, acc2, snd1, snd2, sib1, sib2, dsnd1, dsnd2, send_sems, recv_sems):
        j, t = pl.program_id(0), pl.program_id(1)
        rows = pl.ds(pl.multiple_of(t * tt, tt), tt)
        x, y, c = _place()
        accs, snds, sibs = (acc1, acc2), (snd1, snd2), (sib1, sib2)
        dsnds, diags = (dsnd1, dsnd2), (diag1_ref, diag2_ref)

        def to_sibling(a, jj, buf=0):
            return pltpu.make_async_remote_copy(
                src_ref=snds[a].at[buf], dst_ref=sibs[a].at[jj],
                send_sem=send_sems.at[4 * a + jj], recv_sem=recv_sems.at[4 * a + jj],
                device_id=(x, y, 1 - c), device_id_type=MESH)

        def to_diagonal(a):
            return pltpu.make_async_remote_copy(
                src_ref=dsnds[a], dst_ref=diags[a], send_sem=send_sems.at[8 + a], recv_sem=recv_sems.at[8 + a],
                device_id=_peer(x, y, c, 6), device_id_type=MESH)

        @pl.when(t == 0)
        def _():
            acc2[...] = jnp.zeros_like(acc2)
            acc1[...] = jnp.zeros_like(acc1)

        for r_ref, mine in ((re_ref, relation(j) < FC_EARLY), (rl_ref, relation(j) >= FC_EARLY)):
            @pl.when(mine)
            def _():
                acc2[...] += _dot_tn(r_ref[...], df_ref[rows, :])
                acc1[...] += _dot_tn(h2_ref[rows, :], da_ref[...])

        for step, rel in enumerate(WGRAD_ORDER):
            jj = rel // 2

            @pl.when((t == nt - 1) & (j == step))
            def _():
                for a, (own_ref, out_ref) in enumerate(((own1_ref, out1_ref), (own2_ref, out2_ref))):
                    if rel % 2:
                        q = odd_steps.index(step)
                        if q >= 2:
                            to_sibling(a, WGRAD_ORDER[odd_steps[q - 2]] // 2).wait_send()
                        snds[a][q % 2] = accs[a][...].astype(BF16)
                        to_sibling(a, jj, q % 2).start()
                        continue
                    to_sibling(a, jj).wait_recv()
                    chip_sum = accs[a][...] + sibs[a][jj].astype(F32)
                    if rel == 6:
                        dsnds[a][...] = chip_sum.astype(BF16)
                        to_diagonal(a).start()
                    elif rel == 0:
                        own_ref[...] = chip_sum
                    else:
                        out_ref[0] = chip_sum.astype(BF16)
                    if step == N_DEV - 1:
                        for q in (2, 3):
                            to_sibling(a, WGRAD_ORDER[odd_steps[q]] // 2).wait_send()
                        to_diagonal(a).wait_recv()
                        to_diagonal(a).wait_send()

    assert WGRAD_ORDER[-1] == 0 and WGRAD_ORDER[-3:-1] == (2, 4)
    blk = pl.BlockSpec((tt, FF_BLK), lambda j, t: (t, relation(j)))
    early_block = lambda rel: jnp.where(rel < FC_HEAD, rel + FC_EARLY - FC_HEAD, rel - FC_HEAD)
    early = lambda j, t: (jnp.where(relation(j) < FC_EARLY, t, 0),
                          jnp.where(relation(j) < FC_EARLY, early_block(relation(j)), 0))
    late = lambda j, t: (jnp.where(relation(j) < FC_EARLY, 0, t), jnp.maximum(relation(j) - FC_EARLY, 0))
    chip = lambda j, t: (jnp.clip(j - 5, 0, 1), 0, 0)
    hbm = pl.BlockSpec(memory_space=pl.ANY)
    return pl.pallas_call(
        body, name="mlp_wgrad", grid=(N_DEV, nt),
        out_shape=(jax.ShapeDtypeStruct((D, FF_BLK), F32), jax.ShapeDtypeStruct((FF_BLK, D), F32),
                   jax.ShapeDtypeStruct((2, D, FF_BLK), BF16), jax.ShapeDtypeStruct((2, FF_BLK, D), BF16),
                   jax.ShapeDtypeStruct((D, FF_BLK), BF16), jax.ShapeDtypeStruct((FF_BLK, D), BF16)),
        in_specs=[pl.BlockSpec((tt, FF_BLK), early), pl.BlockSpec((tt, FF_BLK), late), blk,
                  _resident((t_len, D)), _resident((t_len, D))],
        out_specs=(_full((D, FF_BLK)), _full((FF_BLK, D)),
                   pl.BlockSpec((1, D, FF_BLK), chip), pl.BlockSpec((1, FF_BLK, D), chip), hbm, hbm),
        scratch_shapes=[pltpu.VMEM((D, FF_BLK), F32), pltpu.VMEM((FF_BLK, D), F32),
                        pltpu.VMEM((2, D, FF_BLK), BF16), pltpu.VMEM((2, FF_BLK, D), BF16),
                        pltpu.VMEM((4, D, FF_BLK), BF16), pltpu.VMEM((4, FF_BLK, D), BF16),
                        pltpu.VMEM((D, FF_BLK), BF16), pltpu.VMEM((FF_BLK, D), BF16),
                        pltpu.SemaphoreType.DMA((10,)), pltpu.SemaphoreType.DMA((10,))],
        compiler_params=pltpu.CompilerParams(dimension_semantics=("arbitrary", "arbitrary"),
                                             vmem_limit_bytes=VMEM_LIMIT),
    )(r_early, r_late, da, df, h2)


def _acc_rows(ref, row0, k, val):
    half = CHUNK // 2
    ref[row0:row0 + half, k * GROUP:(k + 1) * GROUP] += val[:half, :]
    ref[row0:row0 + half, D_A + k * GROUP:D_A + (k + 1) * GROUP] += val[half:, :]


def _attn_bwd(tt, dmix, dx1, x, z, cat, mod, n1pre, w_in_t, w_out, w_sp, bs_rows, ln_g, ln_b, w_pool, b_pool,
              pool_scale, red_fwd, red_bwd, chip_sums):
    t_len = x.shape[0]
    nt = t_len // tt
    hb = tt // HALO
    n_sums = len(chip_sums)

    def body(dmix_ref, dx1_ref, x_ref, z_ref, zprev_ref, cat_ref, mod_ref, n1pre_ref, win_ref, wout_ref, wsp_ref,
             bs_ref, lng_ref, lnb_ref, wp_ref, bp_ref, ps_ref, redf_ref, redb_ref, *rest):
        sum_out = rest[:n_sums]
        gx_ref, gwin_ref, gwout_ref, small_ref = rest[n_sums:n_sums + 4]
        sum_in = rest[n_sums + 4:2 * n_sums + 4]
        carry, acc_in, acc_out, dz_scr, bs_acc, send_sems, recv_sems = rest[2 * n_sums + 4:]
        s = pl.program_id(0)
        i = nt - 1 - s
        px, py, pc = _place()

        def chip_copy(a, r):
            return pltpu.make_async_remote_copy(
                src_ref=sum_out[a].at[r], dst_ref=sum_in[a].at[r],
                send_sem=send_sems.at[2 * a + r], recv_sem=recv_sems.at[2 * a + r],
                device_id=_peer(px, py, pc, 2 * (r + 1)), device_id_type=MESH)

        @pl.when(s == 0)
        def _():
            for a in range(n_sums):
                for r in range(2):
                    chip_copy(a, r).start()
            carry[...] = jnp.zeros_like(carry)
            acc_in[...] = jnp.zeros_like(acc_in)
            acc_out[...] = jnp.zeros_like(acc_out)
            bs_acc[...] = jnp.zeros_like(bs_acc)
            small_ref[...] = jnp.zeros_like(small_ref)
            small_ref[ROW_DMOD + 2:ROW_DMOD + 3, :] = redb_ref[3:4, :]
            small_ref[ROW_DMOD + 3:ROW_DMOD + 5, :] = redb_ref[0:2, :]
            small_ref[ROW_DMOD + 5:ROW_DMOD + 6, :] = redf_ref[0:1, :]
            small_ref[ROW_N1POST:ROW_N1POST + 1, :] = redb_ref[4:5, :]
            small_ref[ROW_N2PRE:ROW_N2PRE + 1, :] = redb_ref[2:3, :]
            small_ref[ROW_N2POST:ROW_N2POST + 1, :] = redf_ref[1:2, :]
            small_ref[ROW_LOSS:ROW_LOSS + 1, :] = redf_ref[2:3, :]

        dmixv = dmix_ref[...]
        dcat = _dot_nt(dmixv, wout_ref[...])
        acc_out[...] += _dot_tn(cat_ref[...], dmixv)

        z = z_ref[...]
        t_g, ga = _gelu_parts(z[:, :2 * D_A])
        u, vr = ga[:, :D_A], ga[:, D_A:]
        dv0 = vr - jnp.mean(vr, axis=-1, keepdims=True)
        rv = lax.rsqrt(jnp.mean(dv0 * dv0, axis=-1, keepdims=True) + EPS)
        vhat = dv0 * rv
        vb = (vhat * lng_ref[...] + lnb_ref[...]).astype(BF16)
        mask = _tril_mask()
        wc = [(wsp_ref[h] * mask).astype(BF16) for h in range(N_HEADS)]

        dya = dcat[:, :D_A]
        for h in range(N_HEADS):
            cols = slice(h * GROUP, (h + 1) * GROUP)
            bs_sum = jnp.zeros((CHUNK, GROUP), F32)
            ws_sum = jnp.zeros((CHUNK, CHUNK), F32)
            for ch in range(tt // CHUNK):
                rows = slice(ch * CHUNK, (ch + 1) * CHUNK)
                v_ch = vb[rows, cols]
                mixed = _dot(wc[h], v_ch) + bs_ref[:, cols]
                dy_ch = dya[rows, cols]
                dz_scr[rows, cols] = dy_ch * mixed
                dmixed = dy_ch * u[rows, cols]
                dmb = dmixed.astype(BF16)
                dz_scr[rows, D_A + h * GROUP:D_A + (h + 1) * GROUP] = _dot_tn(wc[h], dmb)
                bs_sum = bs_sum + dmixed
                ws_sum = ws_sum + _dot_nt(dmb, v_ch)
            _acc_rows(bs_acc, 0, h, bs_sum)
            _acc_rows(small_ref, ROW_WS, h, ws_sum)

        dvl = dz_scr[:, D_A:2 * D_A]
        dvhat = dvl * lng_ref[...]
        dvl_vhat = dvl * vhat
        dvr = rv * (dvhat - jnp.mean(dvhat, axis=-1, keepdims=True)
                    - vhat * jnp.mean(dvl_vhat * lng_ref[...], axis=-1, keepdims=True))
        small_ref[ROW_LN:ROW_LN + 1, 0:D_A] += _colsum(dvl_vhat)
        small_ref[ROW_LN:ROW_LN + 1, D_A:D] += _colsum(dvl)
        dga = jnp.concatenate([dz_scr[:, :D_A], dvr], axis=1)
        dza = dga * _gelu_grad(z[:, :2 * D_A], t_g)

        zb = z[:, 2 * D_A:]
        halo_prev = jnp.where(i == 0, 0.0, zprev_ref[...])
        diff = _pool_diff(zb, halo_prev, i * tt)
        dyb = dcat[:, D_A:]
        inv = _inv_counts(i * tt, tt)
        scaled, ddiffs = [], []
        for g in range(len(WINDOWS)):
            cols = slice(g * GROUP, (g + 1) * GROUP)
            db = diff[g].astype(BF16)
            wpg = wp_ref[g].astype(BF16)
            pre = _dot(db, wpg) + bp_ref[:, cols]
            small_ref[ROW_POOL:ROW_POOL + 1, cols] += _colsum(dyb[:, cols] * pre)
            dpre = dyb[:, cols] * ps_ref[:, cols]
            small_ref[ROW_POOL:ROW_POOL + 1, D_B + g * GROUP:D_B + (g + 1) * GROUP] += _colsum(dpre)
            dpb = dpre.astype(BF16)
            _acc_rows(small_ref, ROW_WP, g, _dot_tn(db, dpb))
            ddiff = _dot_nt(dpb, wpg)
            ddiffs.append(ddiff)
            scaled.append(ddiff * inv[g])
        scaled_all = jnp.concatenate(scaled, axis=1)
        ext = jnp.concatenate([scaled_all, carry[...]], axis=0)
        n_ext = tt + HALO
        s2 = ext + pltpu.roll(ext, n_ext - 1, 0)
        t4 = s2[:, GROUP:]
        s4 = t4 + pltpu.roll(t4, n_ext - 2, 0)
        t8 = s4[:, GROUP:]
        s8 = t8 + pltpu.roll(t8, n_ext - 4, 0)
        t16 = s8[:, GROUP:]
        s16 = t16 + pltpu.roll(t16, n_ext - 8, 0)
        back = [s2[:, :GROUP], s4[:, :GROUP], s8[:, :GROUP], s16]
        carry[...] = scaled_all[:HALO, :]
        dzb = jnp.concatenate([back[g][:tt, :] - ddiffs[g] for g in range(len(WINDOWS))], axis=1)

        dzv = jnp.concatenate([dza, dzb], axis=1).astype(BF16)
        dh1 = _dot(dzv, win_ref[...])
        xv = x_ref[...]
        r1 = _rstd(xv)
        xhat = xv * r1
        shift1, scale1 = mod_ref[0:1, :], mod_ref[1:2, :]
        pre1 = n1pre_ref[...]
        gain1 = pre1 * (1.0 + scale1)
        h1 = (xhat * gain1 + shift1).astype(BF16)
        acc_in[...] += _dot_tn(dzv, h1)
        d_x, sum_h = _rms_bwd_gained(dh1, gain1, xhat, r1)
        gx_ref[...] = dx1_ref[...] + d_x
        small_ref[ROW_DMOD:ROW_DMOD + 1, :] += _colsum(dh1)
        small_ref[ROW_DMOD + 1:ROW_DMOD + 2, :] += pre1 * sum_h
        small_ref[ROW_N1PRE:ROW_N1PRE + 1, :] += (1.0 + scale1) * sum_h

        @pl.when(s == nt - 1)
        def _():
            gwin_ref[...] = acc_in[...].astype(BF16)
            gwout_ref[...] = acc_out[...].astype(BF16)
            bs = _unfold(bs_acc[...])
            for h in range(N_HEADS):
                small_ref[ROW_BS + h:ROW_BS + h + 1, 0:GROUP] = jnp.sum(
                    bs[:, h * GROUP:(h + 1) * GROUP].T, axis=0, keepdims=True)
            for a in range(n_sums):
                for r in range(2):
                    chip_copy(a, r).wait_recv()
                    chip_copy(a, r).wait_send()

    rev = lambda w: pl.BlockSpec((tt, w), lambda s: (nt - 1 - s, 0))
    zprev = pl.BlockSpec((HALO, D_B), lambda s: (jnp.maximum((nt - 1 - s) * hb - 1, 0), 2))
    hbm = pl.BlockSpec(memory_space=pl.ANY)
    outs = pl.pallas_call(
        body, name="attn_bwd", grid=(nt,),
        out_shape=tuple([jax.ShapeDtypeStruct((t_len, D), F32), jax.ShapeDtypeStruct((D_Z, D), BF16),
                         jax.ShapeDtypeStruct((D, D), BF16), jax.ShapeDtypeStruct((SMALL_ROWS, D), F32)]
                        + [jax.ShapeDtypeStruct(cs.shape, cs.dtype) for cs in chip_sums]),
        in_specs=[rev(D), rev(D), rev(D), rev(D_Z), zprev, rev(D), _full((8, D)), _full((1, D)),
                  _resident((D_Z, D)), _resident((D, D)), _full((N_HEADS, CHUNK, CHUNK)), _full((CHUNK, D_A)),
                  _full((1, D_A)), _full((1, D_A)), _full((len(WINDOWS), GROUP, GROUP)), _full((1, D_B)),
                  _full((1, D_B)), _full((8, D)), _full((8, D))] + [_resident(cs.shape) for cs in chip_sums],
        out_specs=tuple([rev(D), _resident((D_Z, D)), _resident((D, D)), _full((SMALL_ROWS, D))] + [hbm] * n_sums),
        scratch_shapes=[pltpu.VMEM((HALO, D_B), F32), pltpu.VMEM((D_Z, D), F32), pltpu.VMEM((D, D), F32),
                        pltpu.VMEM((tt, 2 * D_A), F32), pltpu.VMEM((CHUNK // 2, D), F32),
                        pltpu.SemaphoreType.DMA((2 * n_sums,)), pltpu.SemaphoreType.DMA((2 * n_sums,))],
        compiler_params=pltpu.CompilerParams(dimension_semantics=("arbitrary",), vmem_limit_bytes=VMEM_LIMIT),
    )(dmix, dx1, x, z, z, cat, mod, n1pre, w_in_t, w_out, w_sp, bs_rows, ln_g, ln_b, w_pool, b_pool, pool_scale,
      red_fwd, red_bwd, *chip_sums)
    return outs[:4], outs[4:]


def _adam(w, g, m, v):
    m2 = ADAM_B1 * m + (1.0 - ADAM_B1) * g
    v2 = ADAM_B2 * v + (1.0 - ADAM_B2) * (g * g)
    m_hat = m2 / (1.0 - ADAM_B1 ** ADAM_STEP)
    v_hat = v2 / (1.0 - ADAM_B2 ** ADAM_STEP)
    delta = -ADAM_LR * (m_hat / (jnp.sqrt(v_hat) + ADAM_EPS) + ADAM_WD * w)
    return delta, m2, v2


def _adamw_shard(name, rb, w, g, m, v):
    rows, cols = w.shape

    def body(w_ref, g_ref, m_ref, v_ref, d_ref, m2_ref, v2_ref):
        d_ref[...], m2_ref[...], v2_ref[...] = _adam(w_ref[...], g_ref[...], m_ref[...], v_ref[...])

    blk = pl.BlockSpec((rb, cols), lambda i: (i, 0))
    shp = jax.ShapeDtypeStruct((rows, cols), F32)
    return pl.pallas_call(
        body, name=name, grid=(rows // rb,), out_shape=(shp, shp, shp),
        in_specs=[blk] * 4, out_specs=(blk, blk, blk),
        compiler_params=pltpu.CompilerParams(dimension_semantics=("arbitrary",)),
    )(w, g, m, v)


def _adamw_ada(rb, w, sc, dmod_cols, m, v):
    rows, cols = w.shape

    def body(w_ref, sc_ref, dm_ref, m_ref, v_ref, g_ref, d_ref, m2_ref, v2_ref):
        g = _dot_tn(sc_ref[...].astype(BF16), dm_ref[...].astype(BF16))
        g_ref[...] = g
        d_ref[...], m2_ref[...], v2_ref[...] = _adam(w_ref[...], g, m_ref[...], v_ref[...])

    blk = pl.BlockSpec((rb, cols), lambda i: (i, 0))
    shp = jax.ShapeDtypeStruct((rows, cols), F32)
    return pl.pallas_call(
        body, name="adamw_ada", grid=(rows // rb,), out_shape=(shp, shp, shp, shp),
        in_specs=[blk, pl.BlockSpec((N_DEV, rb), lambda i: (0, i)), _full((N_DEV, cols)), blk, blk],
        out_specs=(blk, blk, blk, blk),
        compiler_params=pltpu.CompilerParams(dimension_semantics=("arbitrary",)),
    )(w, sc, dmod_cols, m, v)


def _unfold(acc_rows):
    return jnp.concatenate([acc_rows[:, :D_A], acc_rows[:, D_A:]], axis=0)


def _adamw_small(total, params):
    n = len(params)
    flat = [a for p in params for a in p]

    def body(*refs):
        s_ref = refs[0]
        p_refs = refs[1:1 + 3 * n]
        loss_ref = refs[1 + 3 * n]
        o_refs = refs[2 + 3 * n:]
        d_b_ada = s_ref[0:6, :]
        for b in range(1, N_DEV):
            d_b_ada = d_b_ada + s_ref[8 * b:8 * b + 6, :]
        tot = s_ref[PACK_SHIFT:PACK_ROWS, :]
        loss = jnp.sum(tot[ROW_LOSS:ROW_LOSS + 1, :], axis=-1, keepdims=True) * (0.5 / D)
        loss_ref[...] = jnp.broadcast_to(loss, (8, GROUP))
        mask = _tril_mask()
        ws = _unfold(tot[ROW_WS:ROW_WS + 64, :])
        wp = _unfold(tot[ROW_WP:ROW_WP + 64, :])
        grads = [
            d_b_ada,
            tot[ROW_N1PRE:ROW_N1PRE + 1, :], tot[ROW_N1POST:ROW_N1POST + 1, :],
            tot[ROW_N2PRE:ROW_N2PRE + 1, :], tot[ROW_N2POST:ROW_N2POST + 1, :],
            tot[ROW_LN:ROW_LN + 1, :D_A], tot[ROW_LN:ROW_LN + 1, D_A:],
            tot[ROW_POOL:ROW_POOL + 1, :D_B], tot[ROW_POOL:ROW_POOL + 1, D_B:],
            tot[ROW_BS:ROW_BS + N_HEADS, 0:GROUP],
            jnp.stack([ws[:, h * GROUP:(h + 1) * GROUP] * mask for h in range(N_HEADS)]),
            jnp.stack([wp[:, g * GROUP:(g + 1) * GROUP] for g in range(len(WINDOWS))]),
        ]
        for k in range(n):
            w_ref, m_ref, v_ref = p_refs[3 * k:3 * k + 3]
            g = grads[k]
            o_refs[4 * k][...] = g
            o_refs[4 * k + 1][...], o_refs[4 * k + 2][...], o_refs[4 * k + 3][...] = _adam(
                w_ref[...], g, m_ref[...], v_ref[...])

    vm = pl.BlockSpec(memory_space=pltpu.VMEM)
    out_shape = [jax.ShapeDtypeStruct((8, GROUP), F32)]
    for w, _, _ in params:
        out_shape += [jax.ShapeDtypeStruct(w.shape, F32)] * 4
    return pl.pallas_call(
        body, name="adamw_small", out_shape=tuple(out_shape),
        in_specs=[vm] * (1 + 3 * n), out_specs=tuple([vm] * len(out_shape)),
    )(total, *flat)


TT_ATTN_FWD = 512
TT_MLP_FWD = 512
TT_MLP = 256
TT_WGRAD = 2048
TT_ATTN_BWD = 512


def kernel(x, c, w_ada, b_ada, norm1_pre, norm1_post, w_in, w_spatial, b_spatial, ln_v_gain, ln_v_bias, w_pool, b_pool, pool_scale, w_out, norm2_pre, norm2_post, w_fc1, w_fc2, loss_target, m_w_ada, m_b_ada, m_norm1_pre, m_norm1_post, m_w_in, m_w_spatial, m_b_spatial, m_ln_v_gain, m_ln_v_bias, m_w_pool, m_b_pool, m_pool_scale, m_w_out, m_norm2_pre, m_norm2_post, m_w_fc1, m_w_fc2, v_w_ada, v_b_ada, v_norm1_pre, v_norm1_post, v_w_in, v_w_spatial, v_b_spatial, v_ln_v_gain, v_ln_v_bias, v_w_pool, v_b_pool, v_pool_scale, v_w_out, v_norm2_pre, v_norm2_post, v_w_fc1, v_w_fc2):
    t_len = x.shape[1]
    me = 4 * lax.axis_index("x") + 2 * lax.axis_index("y") + lax.axis_index("c")
    ada_cols = w_ada.shape[1]
    tt = lambda want: min(want, t_len)

    x2 = x.reshape(t_len, D)
    tgt = loss_target.reshape(t_len, D)
    row = lambda a: a.reshape(1, -1)

    b_my = lax.dynamic_slice_in_dim(b_ada, me * ada_cols, ada_cols).reshape(1, ada_cols)
    modp, sc, (g_in, g_out), fc_shards = _fwd_comm(jnp.broadcast_to(c, (8, D)), w_ada, b_my,
                                                   [w_in.T, w_out], [w_fc1, w_fc2])
    mod = jnp.concatenate([modp.reshape(6, D), jnp.zeros((2, D), F32)], axis=0)
    w_in_t = g_in.reshape(D_Z, D)
    w_out_all = g_out.reshape(D, D)

    bs_rows = jnp.repeat(b_spatial.T, GROUP, axis=1)
    attn_consts = (w_spatial, bs_rows, row(ln_v_gain), row(ln_v_bias), w_pool, row(b_pool), row(pool_scale))

    (z, cat, mix, x1, h2, r_begun, f_head), (w1_early, w2_early) = _attn_fwd(
        tt(TT_ATTN_FWD), x2, mod, row(norm1_pre), row(norm1_post), w_in_t, w_out_all, *attn_consts, fc_shards,
        row(norm2_pre))
    (r_early, f_early), (w1_late, w2_late) = _mlp_fwd_early(
        tt(TT_MLP_FWD), r_begun, h2, f_head, w1_early, w2_early)
    r_late, df, da, dmix, dx1, red_fwd, red_bwd = _mlp_late_bwd(
        tt(TT_MLP), r_early, x1, h2, f_early, tgt, mix, mod, row(norm2_pre), row(norm2_post), row(norm1_post),
        w1_early, w2_early, w1_late, w2_late)
    own_w1, own_w2, sums_w1, sums_w2, diag_w1, diag_w2 = _mlp_wgrad(tt(TT_WGRAD), r_early, r_late, da, df, h2)
    (grad_x, p_in, p_out, small), (arr_w1, arr_w2) = _attn_bwd(
        tt(TT_ATTN_BWD), dmix, dx1, x2, z, cat, mod, row(norm1_pre), w_in_t, w_out_all, *attn_consts,
        red_fwd, red_bwd, [sums_w1, sums_w2])
    (grad_in_t, grad_out, total), ((grad_w1, d_w1, m_w1, v_w1), (grad_w2, d_w2, m_w2, v_w2)) = _tail_comm(
        [p_in.reshape(N_DEV, D_Z // N_DEV, D), p_out.reshape(N_DEV, D // N_DEV, D)], small, 64,
        [(w_fc1, own_w1, arr_w1, diag_w1, m_w_fc1, v_w_fc1), (w_fc2, own_w2, arr_w2, diag_w2, m_w_fc2, v_w_fc2)])

    d_out, m_out, v_out = _adamw_shard("adamw_out", 128, w_out, grad_out, m_w_out, v_w_out)
    d_in_t, m_in_t, v_in_t = _adamw_shard("adamw_in", D_Z // N_DEV, w_in.T, grad_in_t, m_w_in.T, v_w_in.T)
    dmod_all = total[0:TABLE_ROWS, :].reshape(N_DEV, 8, D)[:, :6, :].reshape(N_DEV, 6 * D)
    dmod_cols = lax.dynamic_slice_in_dim(dmod_all, me * ada_cols, ada_cols, axis=1)
    grad_ada, d_ada, m_ada, v_ada = _adamw_ada(256, w_ada, sc, dmod_cols, m_w_ada, v_w_ada)

    six = lambda a: a.reshape(6, D)
    small_params = [
        (six(b_ada), six(m_b_ada), six(v_b_ada)),
        (row(norm1_pre), row(m_norm1_pre), row(v_norm1_pre)),
        (row(norm1_post), row(m_norm1_post), row(v_norm1_post)),
        (row(norm2_pre), row(m_norm2_pre), row(v_norm2_pre)),
        (row(norm2_post), row(m_norm2_post), row(v_norm2_post)),
        (row(ln_v_gain), row(m_ln_v_gain), row(v_ln_v_gain)),
        (row(ln_v_bias), row(m_ln_v_bias), row(v_ln_v_bias)),
        (row(pool_scale), row(m_pool_scale), row(v_pool_scale)),
        (row(b_pool), row(m_b_pool), row(v_b_pool)),
        (b_spatial, m_b_spatial, v_b_spatial),
        (w_spatial, m_w_spatial, v_w_spatial),
        (w_pool, m_w_pool, v_w_pool),
    ]
    outs = _adamw_small(total, small_params)
    loss = outs[0][0, 0]
    names = ["b_ada", "norm1_pre", "norm1_post", "norm2_pre", "norm2_post", "ln_v_gain", "ln_v_bias", "pool_scale",
             "b_pool", "b_spatial", "w_spatial", "w_pool"]
    shapes = dict(b_ada=b_ada.shape, norm1_pre=norm1_pre.shape, norm1_post=norm1_post.shape,
                  norm2_pre=norm2_pre.shape, norm2_post=norm2_post.shape, ln_v_gain=ln_v_gain.shape,
                  ln_v_bias=ln_v_bias.shape, pool_scale=pool_scale.shape, b_pool=b_pool.shape,
                  b_spatial=b_spatial.shape, w_spatial=w_spatial.shape, w_pool=w_pool.shape)
    res = {}
    for k, nm in enumerate(names):
        res[nm] = tuple(o.reshape(shapes[nm]) for o in outs[1 + 4 * k:5 + 4 * k])
    res["w_ada"] = (grad_ada, d_ada, m_ada, v_ada)
    res["w_in"] = (grad_in_t.T, d_in_t.T, m_in_t.T, v_in_t.T)
    res["w_out"] = (grad_out, d_out, m_out, v_out)
    res["w_fc1"] = (grad_w1, d_w1, m_w1, v_w1)
    res["w_fc2"] = (grad_w2, d_w2, m_w2, v_w2)

    order = ["w_ada", "b_ada", "norm1_pre", "norm1_post", "w_in", "w_spatial", "b_spatial", "ln_v_gain", "ln_v_bias",
             "w_pool", "b_pool", "pool_scale", "w_out", "norm2_pre", "norm2_post", "w_fc1", "w_fc2"]
    return (loss, grad_x.reshape(x.shape),
            *[res[nm][0] for nm in order], *[res[nm][1] for nm in order],
            *[res[nm][2] for nm in order], *[res[nm][3] for nm in order])
```

```python
import functools

import jax
import jax.numpy as jnp
from jax import lax
from jax.experimental import pallas as pl
from jax.experimental.pallas import tpu as pltpu

F32 = jnp.float32
BF16 = jnp.bfloat16
MESH = pl.DeviceIdType.MESH

N_DEV = 8
D = 1024
D_A = 512
D_B = 512
D_Z = 2 * D_A + D_B
N_HEADS = 4
CHUNK = 128
WINDOWS = (2, 4, 8, 16)
GROUP = 128
D_FF = 4096
FF_BLK = D_FF // N_DEV
HALO = 16
EPS = 1e-6
VMEM_LIMIT = 60 * 1024 * 1024

ADAM_LR = 0.001
ADAM_B1 = 0.9
ADAM_B2 = 0.999
ADAM_EPS = 1e-08
ADAM_WD = 0.01
ADAM_STEP = 10

ROW_DMOD = 0
ROW_N1PRE, ROW_N1POST, ROW_N2PRE, ROW_N2POST = 8, 9, 10, 11
ROW_LN = 12
ROW_POOL = 13
ROW_LOSS = 14
ROW_BS = 16
ROW_WS = 24
ROW_WP = 88
SMALL_ROWS = 152
PACK_FINE = 40
PACK_HALF = PACK_FINE + 64
PACK_ROWS = 2 * PACK_HALF
PK_WS = PACK_FINE
PK_TABLE_B = PACK_HALF
PK_MISC = PK_TABLE_B + 24
PK_BS = PK_MISC + 8
PK_WP = PK_BS + 8


def _table_row(b):
    if isinstance(b, int):
        return 8 * b if 8 * b < PACK_FINE else 8 * b + PK_TABLE_B - PACK_FINE
    return 8 * b + jnp.where(8 * b < PACK_FINE, 0, PK_TABLE_B - PACK_FINE)


def _dot(a, b):
    return jnp.dot(a, b, preferred_element_type=F32)


def _dot_nt(a, b):
    return lax.dot_general(a, b, (((1,), (1,)), ((), ())), preferred_element_type=F32)


def _dot_tn(a, b):
    return lax.dot_general(a, b, (((0,), (0,)), ((), ())), preferred_element_type=F32)


def _rstd(v):
    return lax.rsqrt(jnp.mean(v * v, axis=-1, keepdims=True) + EPS)


def _rms_bwd(d_hat, hat, rstd):
    return rstd * (d_hat - hat * jnp.mean(d_hat * hat, axis=-1, keepdims=True))


def _rms_bwd_gained(g, gain, hat, rstd):
    g_hat = g * hat
    d_v = rstd * (g * gain - hat * jnp.mean(g_hat * gain, axis=-1, keepdims=True))
    return d_v, _colsum(g_hat)


_K0 = 0.7978845608028654
_K1 = 0.044715


def _gelu_parts(v):
    t = jnp.tanh(v * (_K0 + (_K0 * _K1) * (v * v)))
    return t, v * (0.5 + 0.5 * t)


def _gelu_grad(v, t):
    return (0.5 + 0.5 * t) + (0.5 * v) * (1.0 - t * t) * (_K0 + (3.0 * _K0 * _K1) * (v * v))


def _colsum(v):
    return jnp.sum(v, axis=0, keepdims=True)


def _full(shape):
    n = len(shape)
    return pl.BlockSpec(shape, lambda *_: (0,) * n)


def _resident(shape):
    n = len(shape)
    return pl.BlockSpec(shape, lambda *_: (0,) * n, pipeline_mode=pl.Buffered(1))


def _place():
    x, y, c = lax.axis_index("x"), lax.axis_index("y"), lax.axis_index("c")
    return x, y, c


def _flip(v, bit):
    return 1 - v if bit else v


def _peer(x, y, c, k):
    return (_flip(x, (k >> 2) & 1), _flip(y, (k >> 1) & 1), _flip(c, k & 1))


def _index(p):
    return 4 * p[0] + 2 * p[1] + p[2]


def _two_level_gather_begin(x, y, c, out_refs, send_sems, recv_sems):
    me = (x, y, c)
    sibling = (x, y, 1 - c)
    chips = [(1 - x, y), (x, 1 - y), (1 - x, 1 - y)]

    def copy(a, k, block, to):
        ref = out_refs[a].at[_index(block)]
        return pltpu.make_async_remote_copy(
            src_ref=ref, dst_ref=ref, send_sem=send_sems.at[7 * a + k], recv_sem=recv_sems.at[7 * a + k],
            device_id=to, device_id_type=MESH)

    first = []
    for a in range(len(out_refs)):
        first.append(copy(a, 0, me, sibling))
        first += [copy(a, 1 + j, me, (*chip, c)) for j, chip in enumerate(chips)]
    for cp in first:
        cp.start()
    return copy, first, me, sibling, chips


def _two_level_gather_finish(c, n, begun):
    copy, first, me, sibling, chips = begun
    passed = []
    for a in range(n):
        for j, chip in enumerate(chips):
            copy(a, 1 + j, (*chip, c), me).wait_recv()
            fwd = copy(a, 4 + j, (*chip, c), sibling)
            fwd.start()
            passed.append(fwd)
    for a in range(n):
        copy(a, 0, sibling, me).wait_recv()
        for j, chip in enumerate(chips):
            copy(a, 4 + j, (*chip, 1 - c), me).wait_recv()
    for cp in first + passed:
        cp.wait_send()


def _fwd_comm(c8, w_ada, b_my, gathered, kept):
    ncol = w_ada.shape[1]
    n_g, n_k = len(gathered), len(kept)

    def body(c_ref, w_ref, b_ref, *rest):
        g_in, k_in = rest[:n_g], rest[n_g:n_g + n_k]
        modp_ref, sc_ref = rest[n_g + n_k:n_g + n_k + 2]
        g_out = rest[n_g + n_k + 2:2 * n_g + n_k + 2]
        k_out = rest[2 * n_g + n_k + 2:2 * n_g + 2 * n_k + 2]
        cg, mg, part, send_sems, recv_sems, g_send, g_recv = rest[2 * n_g + 2 * n_k + 2:]
        x, y, c = _place()
        me = _index((x, y, c))
        for a in range(n_g):
            g_out[a][me] = g_in[a][...].astype(BF16)
        begun = _two_level_gather_begin(x, y, c, g_out, g_send, g_recv)
        for a in range(n_k):
            k_out[a][...] = k_in[a][...].astype(BF16)

        def c_copy(k):
            p = _peer(x, y, c, k)
            return pltpu.make_async_remote_copy(
                src_ref=c_ref, dst_ref=cg.at[me], send_sem=send_sems.at[k - 1], recv_sem=recv_sems.at[k - 1],
                device_id=p, device_id_type=MESH)

        def c_arrival(k):
            p = _peer(x, y, c, k)
            return pltpu.make_async_remote_copy(
                src_ref=c_ref, dst_ref=cg.at[_index(p)], send_sem=send_sems.at[k - 1], recv_sem=recv_sems.at[k - 1],
                device_id=p, device_id_type=MESH)

        def m_copy(k):
            p = _peer(x, y, c, k)
            return pltpu.make_async_remote_copy(
                src_ref=part, dst_ref=mg.at[me], send_sem=send_sems.at[6 + k], recv_sem=recv_sems.at[6 + k],
                device_id=p, device_id_type=MESH)

        def m_arrival(k):
            p = _peer(x, y, c, k)
            return pltpu.make_async_remote_copy(
                src_ref=part, dst_ref=mg.at[_index(p)], send_sem=send_sems.at[6 + k], recv_sem=recv_sems.at[6 + k],
                device_id=p, device_id_type=MESH)

        for k in range(1, N_DEV):
            c_copy(k).start()
        cg[me] = c_ref[...]
        for k in range(1, N_DEV):
            c_arrival(k).wait_recv()
        c_all = jnp.concatenate([cg[j, 0:1, :] for j in range(N_DEV)], axis=0)
        sc = c_all * jax.nn.sigmoid(c_all)
        sc_ref[...] = sc
        part[...] = _dot(sc.astype(BF16), w_ref[...].astype(BF16)) + b_ref[...]
        for k in range(1, N_DEV):
            m_copy(k).start()
        mg[me] = part[...]
        for k in range(1, N_DEV):
            m_arrival(k).wait_recv()
        for j in range(N_DEV):
            modp_ref[j:j + 1, :] = mg[j, pl.ds(me, 1), :]
        _two_level_gather_finish(c, n_g, begun)
        for k in range(1, N_DEV):
            c_copy(k).wait_send()
            m_copy(k).wait_send()

    vm = pl.BlockSpec(memory_space=pltpu.VMEM)
    outs = pl.pallas_call(
        body, name="fwd_comm",
        out_shape=tuple([jax.ShapeDtypeStruct((N_DEV, ncol), F32), jax.ShapeDtypeStruct((N_DEV, D), F32)]
                        + [jax.ShapeDtypeStruct((N_DEV,) + s.shape, BF16) for s in gathered]
                        + [jax.ShapeDtypeStruct(s.shape, BF16) for s in kept]),
        in_specs=[vm] * (3 + n_g + n_k), out_specs=tuple([vm] * (2 + n_g + n_k)),
        scratch_shapes=[
            pltpu.VMEM((N_DEV, 8, D), F32),
            pltpu.VMEM((N_DEV, N_DEV, ncol), F32),
            pltpu.VMEM((N_DEV, ncol), F32),
            pltpu.SemaphoreType.DMA((2 * (N_DEV - 1),)),
            pltpu.SemaphoreType.DMA((2 * (N_DEV - 1),)),
            pltpu.SemaphoreType.DMA((7 * n_g,)),
            pltpu.SemaphoreType.DMA((7 * n_g,)),
        ],
        compiler_params=pltpu.CompilerParams(vmem_limit_bytes=VMEM_LIMIT),
    )(c8, w_ada, b_my, *gathered, *kept)
    return outs[0], outs[1], outs[2:2 + n_g], outs[2 + n_g:]


FC_EARLY = 6
FC_HEAD = 2
R_HEAD_COLS = (FC_EARLY - FC_HEAD) * FF_BLK
WGRAD_ORDER = (7, 6, 1, 3, 5, 2, 4, 0)


def _early_col(j):
    return R_HEAD_COLS + j * FF_BLK if j < FC_HEAD else (j - FC_HEAD) * FF_BLK


class _Copies:
    def __init__(self, entries, send_sems, recv_sems):
        self.place = _place()
        self.entries, self.send_sems, self.recv_sems = entries, send_sems, recv_sems

    def _copy(self, i, arrival=False):
        src, dst, rel = self.entries[i]
        return pltpu.make_async_remote_copy(
            src_ref=dst if arrival else src, dst_ref=dst, send_sem=self.send_sems.at[i],
            recv_sem=self.recv_sems.at[i], device_id=_peer(*self.place, rel), device_id_type=MESH)

    def start(self, *which):
        for i in which:
            self._copy(i).start()

    def wait_recv(self, *which):
        for i in which:
            self._copy(i, arrival=True).wait_recv()

    def wait_send(self, *which):
        for i in which:
            self._copy(i).wait_send()


TAIL_STEPS = 8


def _tail_comm(parts, small, row_chunk, fc):
    n, n_fc = len(parts), len(fc)

    def body(*refs):
        p_refs, small_ref = refs[:n], refs[n]
        fc_in = refs[n + 1:n + 1 + 6 * n_fc]
        outs = refs[n + 1 + 6 * n_fc:]
        g_refs, total_ref = outs[:n], outs[n]
        fc_out = outs[n + 1:n + 1 + 4 * n_fc]
        scr = outs[n + 1 + 4 * n_fc:]
        from_sib = scr[0:n]
        chip_out = scr[n:2 * n]
        chip_in = scr[2 * n:3 * n]
        pack, pack_sib, fine, bulk, total_scr = scr[3 * n:3 * n + 5]
        send_a, recv_a, send_b, recv_b, send_s, recv_s = scr[3 * n + 5:]
        step = pl.program_id(0)
        x, y, c = _place()
        me = _index((x, y, c))
        sibling = (x, y, 1 - c)
        my_chip = 2 * x + y
        others = [(1 - x, y), (x, 1 - y), (1 - x, 1 - y)]
        my_half = pl.ds(pl.multiple_of(PACK_HALF * c, 8), PACK_HALF)

        def pack_to_sibling():
            return pltpu.make_async_remote_copy(
                src_ref=pack, dst_ref=pack_sib, send_sem=send_s.at[0], recv_sem=recv_s.at[0],
                device_id=sibling, device_id_type=MESH)

        def half_to_chip(r, part):
            buf = (fine, bulk)[part]
            return pltpu.make_async_remote_copy(
                src_ref=buf.at[my_chip], dst_ref=buf.at[my_chip],
                send_sem=send_s.at[1 + 3 * part + r], recv_sem=recv_s.at[1 + 3 * part + r],
                device_id=(*others[r], c), device_id_type=MESH)

        def half_from_chip(r, part):
            k = 2 * others[r][0] + others[r][1]
            buf = (fine, bulk)[part]
            return pltpu.make_async_remote_copy(
                src_ref=buf.at[k], dst_ref=buf.at[k],
                send_sem=send_s.at[1 + 3 * part + r], recv_sem=recv_s.at[1 + 3 * part + r],
                device_id=(*others[r], c), device_id_type=MESH)

        def total_to_sibling():
            return pltpu.make_async_remote_copy(
                src_ref=total_scr.at[my_half], dst_ref=total_scr.at[my_half],
                send_sem=send_s.at[7], recv_sem=recv_s.at[7], device_id=sibling, device_id_type=MESH)

        def total_from_sibling():
            sib_half = pl.ds(pl.multiple_of(PACK_HALF * (1 - c), 8), PACK_HALF)
            return pltpu.make_async_remote_copy(
                src_ref=total_scr.at[sib_half], dst_ref=total_scr.at[sib_half],
                send_sem=send_s.at[7], recv_sem=recv_s.at[7], device_id=sibling, device_id_type=MESH)

        def to_sibling(a, k):
            return pltpu.make_async_remote_copy(
                src_ref=p_refs[a].at[2 * k + (1 - c)], dst_ref=from_sib[a].at[k],
                send_sem=send_a.at[a], recv_sem=recv_a.at[a], device_id=sibling, device_id_type=MESH)

        def all_from_sibling(a):
            return pltpu.make_async_remote_copy(
                src_ref=from_sib[a], dst_ref=from_sib[a], send_sem=send_a.at[a], recv_sem=recv_a.at[a],
                device_id=sibling, device_id_type=MESH)

        def to_chip(a, r):
            return pltpu.make_async_remote_copy(
                src_ref=chip_out[a].at[r], dst_ref=chip_in[a].at[r],
                send_sem=send_b.at[3 * a + r], recv_sem=recv_b.at[3 * a + r],
                device_id=(*others[r], c), device_id_type=MESH)

        @pl.when(step == 0)
        def _():
            pack[0:PACK_FINE, :] = jnp.zeros((PACK_FINE, D), F32)
            pack[PK_TABLE_B:PK_MISC, :] = jnp.zeros((PK_MISC - PK_TABLE_B, D), F32)
            pack[pl.ds(pl.multiple_of(_table_row(me), 8), 8), :] = small_ref[0:8, :]
            pack[PK_WS:PK_WS + 64, :] = small_ref[ROW_WS:ROW_WS + 64, :]
            pack[PK_MISC:PK_MISC + 8, :] = small_ref[ROW_N1PRE:ROW_N1PRE + 8, :]
            pack[PK_BS:PK_BS + 8, :] = small_ref[ROW_BS:ROW_BS + 8, :]
            pack[PK_WP:PK_WP + 64, :] = small_ref[ROW_WP:ROW_WP + 64, :]
            pack_to_sibling().start()
            for a in range(n):
                for k in range(4):
                    to_sibling(a, k).start()

        @pl.when(step == 1)
        def _():
            pack_to_sibling().wait_recv()
            chip_sum = pack[my_half, :] + pack_sib[my_half, :]
            fine[my_chip] = chip_sum[:PACK_FINE, :]
            bulk[my_chip] = chip_sum[PACK_FINE:, :].astype(BF16)
            for r in range(3):
                half_to_chip(r, 0).start()
                half_to_chip(r, 1).start()
            for a in range(n):
                all_from_sibling(a).wait_recv()
                rows = p_refs[a].shape[1]
                for r in range(3):
                    k = 2 * others[r][0] + others[r][1]
                    for s in range(0, rows, row_chunk):
                        sl = pl.ds(s, row_chunk)
                        chip_out[a][r, sl, :] = (p_refs[a][2 * k + c, sl, :].astype(F32)
                                                 + from_sib[a][k, sl, :].astype(F32)).astype(BF16)
                    to_chip(a, r).start()
                for s in range(0, rows, row_chunk):
                    sl = pl.ds(s, row_chunk)
                    g_refs[a][sl, :] = (p_refs[a][2 * my_chip + c, sl, :].astype(F32)
                                        + from_sib[a][my_chip, sl, :].astype(F32))

        for k in range(n_fc):
            w_ref, own_ref, arr_ref, diag_ref, m_ref, v_ref = fc_in[6 * k:6 * k + 6]
            g = own_ref[...]
            for r in range(2):
                g = g + arr_ref[r].astype(F32)
            g = g + diag_ref[...].astype(F32)
            fc_out[4 * k][...] = g
            fc_out[4 * k + 1][...], fc_out[4 * k + 2][...], fc_out[4 * k + 3][...] = _adam(
                w_ref[...], g, m_ref[...], v_ref[...])

        @pl.when(step == TAIL_STEPS - 1)
        def _():
            for r in range(3):
                half_from_chip(r, 0).wait_recv()
                half_from_chip(r, 1).wait_recv()
            half_start = pl.multiple_of(PACK_HALF * c, 8)
            total_scr[pl.ds(half_start, PACK_FINE), :] = ((fine[0] + fine[1]) + fine[2]) + fine[3]
            total_scr[pl.ds(half_start + PACK_FINE, PACK_HALF - PACK_FINE), :] = (
                (bulk[0].astype(F32) + bulk[1].astype(F32)) + bulk[2].astype(F32)) + bulk[3].astype(F32)
            total_to_sibling().start()
            for a in range(n):
                rows = p_refs[a].shape[1]
                for r in range(3):
                    to_chip(a, r).wait_recv()
                    for s in range(0, rows, row_chunk):
                        sl = pl.ds(s, row_chunk)
                        g_refs[a][sl, :] = g_refs[a][sl, :] + chip_in[a][r, sl, :].astype(F32)
            total_from_sibling().wait_recv()
            total_ref[...] = total_scr[...]
            for a in range(n):
                all_from_sibling(a).wait_send()
                for r in range(3):
                    to_chip(a, r).wait_send()
            pack_to_sibling().wait_send()
            for r in range(3):
                half_to_chip(r, 0).wait_send()
                half_to_chip(r, 1).wait_send()
            total_to_sibling().wait_send()

    fc_specs_in, fc_specs_out, fc_shapes, fc_args = [], [], [], []
    for w, own, arrived, diagonal, m, v in fc:
        rows, cols = w.shape
        blk = pl.BlockSpec((rows // TAIL_STEPS, cols), lambda i: (i, 0))
        fc_specs_in += [blk, blk, pl.BlockSpec((2, rows // TAIL_STEPS, cols), lambda i: (0, i, 0)), blk, blk, blk]
        fc_specs_out += [blk] * 4
        fc_shapes += [jax.ShapeDtypeStruct((rows, cols), F32)] * 4
        fc_args += [w, own, arrived, diagonal, m, v]
    outs = pl.pallas_call(
        body, name="tail_comm", grid=(TAIL_STEPS,),
        out_shape=tuple([jax.ShapeDtypeStruct(p.shape[1:], F32) for p in parts]
                        + [jax.ShapeDtypeStruct((PACK_ROWS, D), F32)] + fc_shapes),
        in_specs=[_resident(p.shape) for p in parts] + [_resident(small.shape)] + fc_specs_in,
        out_specs=tuple([_full(p.shape[1:]) for p in parts] + [_full((PACK_ROWS, D))] + fc_specs_out),
        scratch_shapes=(
            [pltpu.VMEM((4,) + p.shape[1:], BF16) for p in parts]
            + [pltpu.VMEM((3,) + p.shape[1:], BF16) for p in parts]
            + [pltpu.VMEM((3,) + p.shape[1:], BF16) for p in parts]
            + [pltpu.VMEM((PACK_ROWS, D), F32), pltpu.VMEM((PACK_ROWS, D), F32),
               pltpu.VMEM((4, PACK_FINE, D), F32), pltpu.VMEM((4, PACK_HALF - PACK_FINE, D), BF16),
               pltpu.VMEM((PACK_ROWS, D), F32)]
            + [pltpu.SemaphoreType.DMA((n,)), pltpu.SemaphoreType.DMA((n,)),
               pltpu.SemaphoreType.DMA((3 * n,)), pltpu.SemaphoreType.DMA((3 * n,)),
               pltpu.SemaphoreType.DMA((8,)), pltpu.SemaphoreType.DMA((8,))]),
        compiler_params=pltpu.CompilerParams(dimension_semantics=("arbitrary",), vmem_limit_bytes=VMEM_LIMIT),
    )(*parts, small, *fc_args)
    return outs[:n + 1], [outs[n + 1 + 4 * k:n + 5 + 4 * k] for k in range(n_fc)]


def _tril_mask():
    row = lax.broadcasted_iota(jnp.int32, (CHUNK, CHUNK), 0)
    col = lax.broadcasted_iota(jnp.int32, (CHUNK, CHUNK), 1)
    return (col <= row).astype(F32)


def _window_sums(ext):
    s2 = ext + pltpu.roll(ext, 1, 0)
    t4 = s2[:, GROUP:]
    s4 = t4 + pltpu.roll(t4, 2, 0)
    t8 = s4[:, GROUP:]
    s8 = t8 + pltpu.roll(t8, 4, 0)
    t16 = s8[:, GROUP:]
    s16 = t16 + pltpu.roll(t16, 8, 0)
    return [s2[:, :GROUP], s4[:, :GROUP], s8[:, :GROUP], s16]


def _inv_counts(first_pos, rows):
    pos = first_pos + lax.broadcasted_iota(jnp.int32, (rows, 1), 0)
    return [1.0 / jnp.minimum(pos + 1, w).astype(F32) for w in WINDOWS]


def _pool_diff(zb, halo, first_pos):
    tt = zb.shape[0]
    sums = _window_sums(jnp.concatenate([halo, zb], axis=0))
    inv = _inv_counts(first_pos, tt)
    return [sums[g][HALO:, :] * inv[g] - zb[:, g * GROUP:(g + 1) * GROUP] for g in range(len(WINDOWS))]


def _attn_fwd(tt, x, mod, n1pre, n1post, w_in_t, w_out, w_sp, bs_rows, ln_g, ln_b, w_pool, b_pool, pool_scale,
              fc_shards, n2pre):
    t_len = x.shape[0]
    nt = t_len // tt

    def body(x_ref, mod_ref, n1pre_ref, n1post_ref, win_ref, wout_ref, wsp_ref, bs_ref, lng_ref, lnb_ref,
             wp_ref, bp_ref, ps_ref, w1_ref, w2_ref, n2pre_ref,
             z_ref, cat_ref, mix_ref, x1_ref, h2_ref, r_ref, f_ref, e1_ref, e2_ref,
             carry, land1, land2, sib1, sib2, send_sems, recv_sems, local_sems):
        i = pl.program_id(0)
        copies = _Copies(
            [(w1_ref, sib1, 1), (w2_ref, sib2, 1),
             (w1_ref, land1.at[0], 2), (w2_ref, land2.at[0], 2),
             (w1_ref, land1.at[1], 4), (w2_ref, land2.at[1], 4),
             (land1.at[0], e1_ref.at[3], 1), (land2.at[0], e2_ref.at[3], 1),
             (land1.at[1], e1_ref.at[5], 1), (land2.at[1], e2_ref.at[5], 1)],
            send_sems, recv_sems)
        keep = [pltpu.make_async_copy(w1_ref, e1_ref.at[0], local_sems.at[0]),
                pltpu.make_async_copy(w2_ref, e2_ref.at[0], local_sems.at[1]),
                pltpu.make_async_copy(land1.at[0], e1_ref.at[2], local_sems.at[2]),
                pltpu.make_async_copy(land1.at[1], e1_ref.at[4], local_sems.at[3]),
                pltpu.make_async_copy(land2.at[0], e2_ref.at[2], local_sems.at[4]),
                pltpu.make_async_copy(land2.at[1], e2_ref.at[4], local_sems.at[5]),
                pltpu.make_async_copy(sib1, e1_ref.at[1], local_sems.at[6]),
                pltpu.make_async_copy(sib2, e2_ref.at[1], local_sems.at[7])]

        @pl.when(i == 0)
        def _():
            copies.start(0, 1, 2, 4, 3, 5)
            keep[0].start()
            keep[1].start()
            carry[...] = jnp.zeros_like(carry)

        @pl.when(i == nt // 2)
        def _():
            copies.wait_recv(2, 4)
            copies.start(6, 8)
            keep[2].start()
            keep[3].start()

        @pl.when(i == nt - 1)
        def _():
            copies.wait_recv(3, 5)
            copies.start(7, 9)
            keep[4].start()
            keep[5].start()

        xv = x_ref[...]
        shift1, scale1, gate1 = mod_ref[0:1, :], mod_ref[1:2, :], mod_ref[2:3, :]
        h1 = (xv * _rstd(xv)) * (n1pre_ref[...] * (1.0 + scale1)) + shift1
        z = _dot_nt(h1.astype(BF16), win_ref[...])
        z_ref[...] = z

        _, ga = _gelu_parts(z[:, :2 * D_A])
        u, vr = ga[:, :D_A], ga[:, D_A:]
        dv = vr - jnp.mean(vr, axis=-1, keepdims=True)
        v = (dv * lax.rsqrt(jnp.mean(dv * dv, axis=-1, keepdims=True) + EPS)) * lng_ref[...] + lnb_ref[...]
        vb = v.astype(BF16)
        mask = _tril_mask()
        wc = [(wsp_ref[h] * mask).astype(BF16) for h in range(N_HEADS)]
        for ch in range(tt // CHUNK):
            rows = slice(ch * CHUNK, (ch + 1) * CHUNK)
            for h in range(N_HEADS):
                cols = slice(h * GROUP, (h + 1) * GROUP)
                mixed = _dot(wc[h], vb[rows, cols]) + bs_ref[:, cols]
                cat_ref[rows, cols] = (u[rows, cols] * mixed).astype(BF16)

        zb = z[:, 2 * D_A:]
        diff = _pool_diff(zb, carry[...], i * tt)
        carry[...] = zb[tt - HALO:, :]
        for g in range(len(WINDOWS)):
            cols = slice(g * GROUP, (g + 1) * GROUP)
            pre = _dot(diff[g].astype(BF16), wp_ref[g].astype(BF16)) + bp_ref[:, cols]
            cat_ref[:, D_A + g * GROUP:D_A + (g + 1) * GROUP] = (pre * ps_ref[:, cols]).astype(BF16)

        mix = _dot(cat_ref[...], wout_ref[...])
        mix_ref[...] = mix
        x1v = xv + (mix * _rstd(mix)) * (gate1 * n1post_ref[...])
        x1_ref[...] = x1v

        @pl.when(i == 0)
        def _():
            copies.wait_recv(0, 1)
            keep[6].start()
            keep[7].start()

        shift2, scale2 = mod_ref[3:4, :], mod_ref[4:5, :]
        h2 = ((x1v * _rstd(x1v)) * (n2pre_ref[...] * (1.0 + scale2)) + shift2).astype(BF16)
        h2_ref[...] = h2
        for j, (w1, w2) in enumerate(((w1_ref, w2_ref), (sib1, sib2))):
            ra = jnp.maximum(_dot(h2, w1[...]), 0.0)
            r = (ra * ra).astype(BF16)
            r_ref[:, j * FF_BLK:(j + 1) * FF_BLK] = r
            if j == 0:
                f_ref[...] = _dot(r, w2[...])
            else:
                f_ref[...] += _dot(r, w2[...])

        @pl.when(i == nt - 1)
        def _():
            copies.wait_recv(6, 7, 8, 9)
            copies.wait_send(*range(10))
            for cp in keep:
                cp.wait()

    tile = lambda w: pl.BlockSpec((tt, w), lambda i: (i, 0))
    hbm = pl.BlockSpec(memory_space=pl.ANY)
    outs = pl.pallas_call(
        body, name="attn_fwd", grid=(nt,),
        out_shape=tuple([jax.ShapeDtypeStruct((t_len, D_Z), F32), jax.ShapeDtypeStruct((t_len, D), BF16),
                         jax.ShapeDtypeStruct((t_len, D), F32), jax.ShapeDtypeStruct((t_len, D), F32),
                         jax.ShapeDtypeStruct((t_len, D), BF16),
                         jax.ShapeDtypeStruct((t_len, FC_EARLY * FF_BLK), BF16),
                         jax.ShapeDtypeStruct((t_len, D), F32)]
                        + [jax.ShapeDtypeStruct((FC_EARLY,) + s.shape, BF16) for s in fc_shards]),
        in_specs=[tile(D), _full((8, D)), _full((1, D)), _full((1, D)), _resident((D_Z, D)), _resident((D, D)),
                  _full((N_HEADS, CHUNK, CHUNK)), _full((CHUNK, D_A)), _full((1, D_A)), _full((1, D_A)),
                  _full((len(WINDOWS), GROUP, GROUP)), _full((1, D_B)), _full((1, D_B)),
                  _resident(fc_shards[0].shape), _resident(fc_shards[1].shape), _full((1, D))],
        out_specs=(tile(D_Z), tile(D), tile(D), tile(D), tile(D),
                   pl.BlockSpec((tt, FC_HEAD * FF_BLK), lambda i: (i, R_HEAD_COLS // (FC_HEAD * FF_BLK))), tile(D),
                   hbm, hbm),
        scratch_shapes=[pltpu.VMEM((HALO, D_B), F32),
                        pltpu.VMEM((2,) + fc_shards[0].shape, BF16), pltpu.VMEM((2,) + fc_shards[1].shape, BF16),
                        pltpu.VMEM(fc_shards[0].shape, BF16), pltpu.VMEM(fc_shards[1].shape, BF16),
                        pltpu.SemaphoreType.DMA((10,)), pltpu.SemaphoreType.DMA((10,)),
                        pltpu.SemaphoreType.DMA((8,))],
        compiler_params=pltpu.CompilerParams(dimension_semantics=("arbitrary",), vmem_limit_bytes=VMEM_LIMIT),
    )(x, mod, n1pre, n1post, w_in_t, w_out, w_sp, bs_rows, ln_g, ln_b, w_pool, b_pool, pool_scale, *fc_shards,
      n2pre)
    return outs[:7], outs[7:]


def _mlp_fwd_early(tt, r_begun, h2, f_head, w1_early, w2_early):
    t_len = h2.shape[0]
    nt = t_len // tt
    n_late = N_DEV - FC_EARLY

    def body(r_begun_ref, h2_ref, fh_ref, w1_ref, w2_ref, r_ref, f_ref, l1_ref, l2_ref,
             land1, land2, send_sems, recv_sems, local_sems):
        i = pl.program_id(0)
        copies = _Copies(
            [(w1_ref.at[2], land1, 4), (w2_ref.at[4], land2, 2),
             (land1, l1_ref.at[1], 1), (land2, l2_ref.at[1], 1)],
            send_sems, recv_sems)
        keep = [pltpu.make_async_copy(land1, l1_ref.at[0], local_sems.at[0]),
                pltpu.make_async_copy(land2, l2_ref.at[0], local_sems.at[1])]

        @pl.when(i == 0)
        def _():
            copies.start(0, 1)

        @pl.when(i == nt - 1)
        def _():
            copies.wait_recv(0, 1)
            copies.start(2, 3)
            for cp in keep:
                cp.start()

        h2 = h2_ref[...]
        f_ref[...] = fh_ref[...]
        for j in range(FC_HEAD, FC_EARLY):
            ra = jnp.maximum(_dot(h2, w1_ref[j]), 0.0)
            r = (ra * ra).astype(BF16)
            r_ref[:, _early_col(j):_early_col(j) + FF_BLK] = r
            f_ref[...] += _dot(r, w2_ref[j])

        @pl.when(i == nt - 1)
        def _():
            copies.wait_recv(2, 3)
            copies.wait_send(0, 1, 2, 3)
            for cp in keep:
                cp.wait()

    tile = lambda w: pl.BlockSpec((tt, w), lambda i: (i, 0))
    hbm = pl.BlockSpec(memory_space=pl.ANY)
    outs = pl.pallas_call(
        body, name="mlp_fwd_early", grid=(nt,),
        out_shape=(jax.ShapeDtypeStruct((t_len, FC_EARLY * FF_BLK), BF16), jax.ShapeDtypeStruct((t_len, D), F32),
                   jax.ShapeDtypeStruct((n_late,) + w1_early.shape[1:], BF16),
                   jax.ShapeDtypeStruct((n_late,) + w2_early.shape[1:], BF16)),
        in_specs=[hbm, tile(D), tile(D), _resident((FC_EARLY, D, FF_BLK)), _resident((FC_EARLY, FF_BLK, D))],
        out_specs=(tile(R_HEAD_COLS), tile(D), hbm, hbm),
        input_output_aliases={0: 0},
        scratch_shapes=[pltpu.VMEM(w1_early.shape[1:], BF16), pltpu.VMEM(w2_early.shape[1:], BF16),
                        pltpu.SemaphoreType.DMA((4,)), pltpu.SemaphoreType.DMA((4,)),
                        pltpu.SemaphoreType.DMA((2,))],
        compiler_params=pltpu.CompilerParams(dimension_semantics=("arbitrary",), vmem_limit_bytes=VMEM_LIMIT),
    )(r_begun, h2, f_head, w1_early, w2_early)
    return outs[:2], outs[2:]


def _mlp_late_bwd(tt, r_early, x1, h2, f_early, tgt, mix, mod, n2pre, n2post, n1post,
                  w1_early, w2_early, w1_late, w2_late):
    t_len = x1.shape[0]
    nt = t_len // tt
    n_late = N_DEV - FC_EARLY
    late_cols = n_late * FF_BLK

    def body(re_ref, x1_ref, h2_ref, fe_ref, tgt_ref, mix_ref, mod_ref, n2pre_ref, n2post_ref,
             n1post_ref, w1e_ref, w2e_ref, w1l_ref, w2l_ref,
             rl_ref, df_ref, da_ref, dmix_ref, dx1_ref, redf_ref, redb_ref, dh2_acc):
        i = pl.program_id(0)

        @pl.when(i == 0)
        def _():
            redf_ref[...] = jnp.zeros_like(redf_ref)
            redb_ref[...] = jnp.zeros_like(redb_ref)

        x1v = x1_ref[...]
        gate1, scale2, gate2 = mod_ref[2:3, :], mod_ref[4:5, :], mod_ref[5:6, :]
        h2 = h2_ref[...]
        f = fe_ref[...]
        for j in range(n_late):
            cols = slice(j * FF_BLK, (j + 1) * FF_BLK)
            ra = jnp.maximum(_dot(h2, w1l_ref[j]), 0.0)
            r = (ra * ra).astype(BF16)
            rl_ref[:, cols] = r
            f = f + _dot(r, w2l_ref[j])
        post2 = n2post_ref[...]
        gate_post2 = gate2 * post2
        rf = _rstd(f)
        fhat = f * rf
        err = (x1v + fhat * gate_post2) - tgt_ref[...]
        dy = err * (1.0 / D)
        d_f, sum_f = _rms_bwd_gained(dy, gate_post2, fhat, rf)
        dfv = d_f.astype(BF16)
        df_ref[...] = dfv
        redf_ref[0:1, :] += post2 * sum_f
        redf_ref[1:2, :] += gate2 * sum_f
        redf_ref[2:3, :] += _colsum(err * err)

        for j in range(N_DEV):
            cols = slice(j * FF_BLK, (j + 1) * FF_BLK)
            if j < FC_EARLY:
                w1, w2, r = w1e_ref[j], w2e_ref[j], re_ref[:, _early_col(j):_early_col(j) + FF_BLK]
            else:
                jl = j - FC_EARLY
                w1, w2, r = w1l_ref[jl], w2l_ref[jl], rl_ref[:, jl * FF_BLK:(jl + 1) * FF_BLK]
            dr = _dot_nt(dfv, w2)
            da = (dr * (2.0 * jnp.sqrt(r.astype(F32)))).astype(BF16)
            da_ref[:, cols] = da
            contrib = _dot_nt(da, w1)
            if j == 0:
                dh2_acc[...] = contrib
            else:
                dh2_acc[...] += contrib
        dh2 = dh2_acc[...]
        pre2, post1 = n2pre_ref[...], n1post_ref[...]
        r2 = _rstd(x1v)
        xhat = x1v * r2
        d_x1, sum_h = _rms_bwd_gained(dh2, pre2 * (1.0 + scale2), xhat, r2)
        dx1 = dy + d_x1
        dx1_ref[...] = dx1
        mixv = mix_ref[...]
        rm = _rstd(mixv)
        mhat = mixv * rm
        d_mix, sum_m = _rms_bwd_gained(dx1, gate1 * post1, mhat, rm)
        dmix_ref[...] = d_mix.astype(BF16)
        redb_ref[0:1, :] += _colsum(dh2)
        redb_ref[1:2, :] += pre2 * sum_h
        redb_ref[2:3, :] += (1.0 + scale2) * sum_h
        redb_ref[3:4, :] += post1 * sum_m
        redb_ref[4:5, :] += gate1 * sum_m

    tile = lambda w: pl.BlockSpec((tt, w), lambda i: (i, 0))
    return pl.pallas_call(
        body, name="mlp_late_bwd", grid=(nt,),
        out_shape=(jax.ShapeDtypeStruct((t_len, late_cols), BF16), jax.ShapeDtypeStruct((t_len, D), BF16),
                   jax.ShapeDtypeStruct((t_len, D_FF), BF16), jax.ShapeDtypeStruct((t_len, D), BF16),
                   jax.ShapeDtypeStruct((t_len, D), F32), jax.ShapeDtypeStruct((8, D), F32),
                   jax.ShapeDtypeStruct((8, D), F32)),
        in_specs=[tile(FC_EARLY * FF_BLK), tile(D), tile(D), tile(D), tile(D),
                  tile(D), _full((8, D)), _full((1, D)), _full((1, D)), _full((1, D)),
                  _resident((FC_EARLY, D, FF_BLK)), _resident((FC_EARLY, FF_BLK, D)),
                  _resident((n_late, D, FF_BLK)), _resident((n_late, FF_BLK, D))],
        out_specs=(tile(late_cols), tile(D), tile(D_FF), tile(D), tile(D), _full((8, D)), _full((8, D))),
        scratch_shapes=[pltpu.VMEM((tt, D), F32)],
        compiler_params=pltpu.CompilerParams(dimension_semantics=("arbitrary",), vmem_limit_bytes=VMEM_LIMIT),
    )(r_early, x1, h2, f_early, tgt, mix, mod, n2pre, n2post, n1post, w1_early, w2_early, w1_late, w2_late)


def _mlp_wgrad(tt, r_early, r_late, da, df, h2):
    t_len = df.shape[0]
    nt = t_len // tt
    odd_steps = [j for j, rel in enumerate(WGRAD_ORDER) if rel % 2]

    def relation(j):
        rel = jnp.int32(WGRAD_ORDER[-1])
        for step in range(N_DEV - 2, -1, -1):
            rel = jnp.where(j == step, WGRAD_ORDER[step], rel)
        return rel

    def body(re_ref, rl_ref, da_ref, df_ref, h2_ref, own1_ref, own2_ref, out1_ref, out2_ref, diag1_ref, diag2_ref,
             acc1, acc2, snd1, snd2, sib1, sib2, dsnd1, dsnd2, send_sems, recv_sems):
        j, t = pl.program_id(0), pl.program_id(1)
        rows = pl.ds(pl.multiple_of(t * tt, tt), tt)
        x, y, c = _place()
        accs, snds, sibs = (acc1, acc2), (snd1, snd2), (sib1, sib2)
        dsnds, diags = (dsnd1, dsnd2), (diag1_ref, diag2_ref)

        def to_sibling(a, jj, buf=0):
            return pltpu.make_async_remote_copy(
                src_ref=snds[a].at[buf], dst_ref=sibs[a].at[jj],
                send_sem=send_sems.at[4 * a + jj], recv_sem=recv_sems.at[4 * a + jj],
                device_id=(x, y, 1 - c), device_id_type=MESH)

        def to_diagonal(a):
            return pltpu.make_async_remote_copy(
                src_ref=dsnds[a], dst_ref=diags[a], send_sem=send_sems.at[8 + a], recv_sem=recv_sems.at[8 + a],
                device_id=_peer(x, y, c, 6), device_id_type=MESH)

        @pl.when(t == 0)
        def _():
            acc2[...] = jnp.zeros_like(acc2)
            acc1[...] = jnp.zeros_like(acc1)

        for r_ref, mine in ((re_ref, relation(j) < FC_EARLY), (rl_ref, relation(j) >= FC_EARLY)):
            @pl.when(mine)
            def _():
                acc2[...] += _dot_tn(r_ref[...], df_ref[rows, :])
                acc1[...] += _dot_tn(h2_ref[rows, :], da_ref[...])

        for step, rel in enumerate(WGRAD_ORDER):
            jj = rel // 2

            @pl.when((t == nt - 1) & (j == step))
            def _():
                for a, (own_ref, out_ref) in enumerate(((own1_ref, out1_ref), (own2_ref, out2_ref))):
                    if rel % 2:
                        q = odd_steps.index(step)
                        if q >= 2:
                            to_sibling(a, WGRAD_ORDER[odd_steps[q - 2]] // 2).wait_send()
                        snds[a][q % 2] = accs[a][...].astype(BF16)
                        to_sibling(a, jj, q % 2).start()
                        continue
                    to_sibling(a, jj).wait_recv()
                    chip_sum = accs[a][...] + sibs[a][jj].astype(F32)
                    if rel == 6:
                        dsnds[a][...] = chip_sum.astype(BF16)
                        to_diagonal(a).start()
                    elif rel == 0:
                        own_ref[...] = chip_sum
                    else:
                        out_ref[0] = chip_sum.astype(BF16)
                    if step == N_DEV - 1:
                        for q in (2, 3):
                            to_sibling(a, WGRAD_ORDER[odd_steps[q]] // 2).wait_send()
                        to_diagonal(a).wait_recv()
                        to_diagonal(a).wait_send()

    assert WGRAD_ORDER[-1] == 0 and WGRAD_ORDER[-3:-1] == (2, 4)
    blk = pl.BlockSpec((tt, FF_BLK), lambda j, t: (t, relation(j)))
    early_block = lambda rel: jnp.where(rel < FC_HEAD, rel + FC_EARLY - FC_HEAD, rel - FC_HEAD)
    early = lambda j, t: (jnp.where(relation(j) < FC_EARLY, t, 0),
                          jnp.where(relation(j) < FC_EARLY, early_block(relation(j)), 0))
    late = lambda j, t: (jnp.where(relation(j) < FC_EARLY, 0, t), jnp.maximum(relation(j) - FC_EARLY, 0))
    chip = lambda j, t: (jnp.clip(j - 5, 0, 1), 0, 0)
    hbm = pl.BlockSpec(memory_space=pl.ANY)
    return pl.pallas_call(
        body, name="mlp_wgrad", grid=(N_DEV, nt),
        out_shape=(jax.ShapeDtypeStruct((D, FF_BLK), F32), jax.ShapeDtypeStruct((FF_BLK, D), F32),
                   jax.ShapeDtypeStruct((2, D, FF_BLK), BF16), jax.ShapeDtypeStruct((2, FF_BLK, D), BF16),
                   jax.ShapeDtypeStruct((D, FF_BLK), BF16), jax.ShapeDtypeStruct((FF_BLK, D), BF16)),
        in_specs=[pl.BlockSpec((tt, FF_BLK), early), pl.BlockSpec((tt, FF_BLK), late), blk,
                  _resident((t_len, D)), _resident((t_len, D))],
        out_specs=(_full((D, FF_BLK)), _full((FF_BLK, D)),
                   pl.BlockSpec((1, D, FF_BLK), chip), pl.BlockSpec((1, FF_BLK, D), chip), hbm, hbm),
        scratch_shapes=[pltpu.VMEM((D, FF_BLK), F32), pltpu.VMEM((FF_BLK, D), F32),
                        pltpu.VMEM((2, D, FF_BLK), BF16), pltpu.VMEM((2, FF_BLK, D), BF16),
                        pltpu.VMEM((4, D, FF_BLK), BF16), pltpu.VMEM((4, FF_BLK, D), BF16),
                        pltpu.VMEM((D, FF_BLK), BF16), pltpu.VMEM((FF_BLK, D), BF16),
                        pltpu.SemaphoreType.DMA((10,)), pltpu.SemaphoreType.DMA((10,))],
        compiler_params=pltpu.CompilerParams(dimension_semantics=("arbitrary", "arbitrary"),
                                             vmem_limit_bytes=VMEM_LIMIT),
    )(r_early, r_late, da, df, h2)


def _acc_rows(ref, row0, k, val):
    half = CHUNK // 2
    ref[row0:row0 + half, k * GROUP:(k + 1) * GROUP] += val[:half, :]
    ref[row0:row0 + half, D_A + k * GROUP:D_A + (k + 1) * GROUP] += val[half:, :]


def _attn_bwd(tt, dmix, dx1, x, z, cat, mod, n1pre, w_in_t, w_out, w_sp, bs_rows, ln_g, ln_b, w_pool, b_pool,
              pool_scale, red_fwd, red_bwd, chip_sums):
    t_len = x.shape[0]
    nt = t_len // tt
    hb = tt // HALO
    n_sums = len(chip_sums)

    def body(dmix_ref, dx1_ref, x_ref, z_ref, zprev_ref, cat_ref, mod_ref, n1pre_ref, win_ref, wout_ref, wsp_ref,
             bs_ref, lng_ref, lnb_ref, wp_ref, bp_ref, ps_ref, redf_ref, redb_ref, *rest):
        sum_out = rest[:n_sums]
        gx_ref, gwin_ref, gwout_ref, small_ref = rest[n_sums:n_sums + 4]
        sum_in = rest[n_sums + 4:2 * n_sums + 4]
        carry, acc_in, acc_out, dz_scr, bs_acc, send_sems, recv_sems = rest[2 * n_sums + 4:]
        s = pl.program_id(0)
        i = nt - 1 - s
        px, py, pc = _place()

        def chip_copy(a, r):
            return pltpu.make_async_remote_copy(
                src_ref=sum_out[a].at[r], dst_ref=sum_in[a].at[r],
                send_sem=send_sems.at[2 * a + r], recv_sem=recv_sems.at[2 * a + r],
                device_id=_peer(px, py, pc, 2 * (r + 1)), device_id_type=MESH)

        @pl.when(s == 0)
        def _():
            for a in range(n_sums):
                for r in range(2):
                    chip_copy(a, r).start()
            carry[...] = jnp.zeros_like(carry)
            acc_in[...] = jnp.zeros_like(acc_in)
            acc_out[...] = jnp.zeros_like(acc_out)
            bs_acc[...] = jnp.zeros_like(bs_acc)
            small_ref[...] = jnp.zeros_like(small_ref)
            small_ref[ROW_DMOD + 2:ROW_DMOD + 3, :] = redb_ref[3:4, :]
            small_ref[ROW_DMOD + 3:ROW_DMOD + 5, :] = redb_ref[0:2, :]
            small_ref[ROW_DMOD + 5:ROW_DMOD + 6, :] = redf_ref[0:1, :]
            small_ref[ROW_N1POST:ROW_N1POST + 1, :] = redb_ref[4:5, :]
            small_ref[ROW_N2PRE:ROW_N2PRE + 1, :] = redb_ref[2:3, :]
            small_ref[ROW_N2POST:ROW_N2POST + 1, :] = redf_ref[1:2, :]
            small_ref[ROW_LOSS:ROW_LOSS + 1, :] = redf_ref[2:3, :]

        dmixv = dmix_ref[...]
        dcat = _dot_nt(dmixv, wout_ref[...])
        acc_out[...] += _dot_tn(cat_ref[...], dmixv)

        z = z_ref[...]
        t_g, ga = _gelu_parts(z[:, :2 * D_A])
        u, vr = ga[:, :D_A], ga[:, D_A:]
        dv0 = vr - jnp.mean(vr, axis=-1, keepdims=True)
        rv = lax.rsqrt(jnp.mean(dv0 * dv0, axis=-1, keepdims=True) + EPS)
        vhat = dv0 * rv
        vb = (vhat * lng_ref[...] + lnb_ref[...]).astype(BF16)
        mask = _tril_mask()
        wc = [(wsp_ref[h] * mask).astype(BF16) for h in range(N_HEADS)]

        dya = dcat[:, :D_A]
        for h in range(N_HEADS):
            cols = slice(h * GROUP, (h + 1) * GROUP)
            bs_sum = jnp.zeros((CHUNK, GROUP), F32)
            ws_sum = jnp.zeros((CHUNK, CHUNK), F32)
            for ch in range(tt // CHUNK):
                rows = slice(ch * CHUNK, (ch + 1) * CHUNK)
                v_ch = vb[rows, cols]
                mixed = _dot(wc[h], v_ch) + bs_ref[:, cols]
                dy_ch = dya[rows, cols]
                dz_scr[rows, cols] = dy_ch * mixed
                dmixed = dy_ch * u[rows, cols]
                dmb = dmixed.astype(BF16)
                dz_scr[rows, D_A + h * GROUP:D_A + (h + 1) * GROUP] = _dot_tn(wc[h], dmb)
                bs_sum = bs_sum + dmixed
                ws_sum = ws_sum + _dot_nt(dmb, v_ch)
            _acc_rows(bs_acc, 0, h, bs_sum)
            _acc_rows(small_ref, ROW_WS, h, ws_sum)

        dvl = dz_scr[:, D_A:2 * D_A]
        dvhat = dvl * lng_ref[...]
        dvl_vhat = dvl * vhat
        dvr = rv * (dvhat - jnp.mean(dvhat, axis=-1, keepdims=True)
                    - vhat * jnp.mean(dvl_vhat * lng_ref[...], axis=-1, keepdims=True))
        small_ref[ROW_LN:ROW_LN + 1, 0:D_A] += _colsum(dvl_vhat)
        small_ref[ROW_LN:ROW_LN + 1, D_A:D] += _colsum(dvl)
        dga = jnp.concatenate([dz_scr[:, :D_A], dvr], axis=1)
        dza = dga * _gelu_grad(z[:, :2 * D_A], t_g)

        zb = z[:, 2 * D_A:]
        halo_prev = jnp.where(i == 0, 0.0, zprev_ref[...])
        diff = _pool_diff(zb, halo_prev, i * tt)
        dyb = dcat[:, D_A:]
        inv = _inv_counts(i * tt, tt)
        scaled, ddiffs = [], []
        for g in range(len(WINDOWS)):
            cols = slice(g * GROUP, (g + 1) * GROUP)
            db = diff[g].astype(BF16)
            wpg = wp_ref[g].astype(BF16)
            pre = _dot(db, wpg) + bp_ref[:, cols]
            small_ref[ROW_POOL:ROW_POOL + 1, cols] += _colsum(dyb[:, cols] * pre)
            dpre = dyb[:, cols] * ps_ref[:, cols]
            small_ref[ROW_POOL:ROW_POOL + 1, D_B + g * GROUP:D_B + (g + 1) * GROUP] += _colsum(dpre)
            dpb = dpre.astype(BF16)
            _acc_rows(small_ref, ROW_WP, g, _dot_tn(db, dpb))
            ddiff = _dot_nt(dpb, wpg)
            ddiffs.append(ddiff)
            scaled.append(ddiff * inv[g])
        scaled_all = jnp.concatenate(scaled, axis=1)
        ext = jnp.concatenate([scaled_all, carry[...]], axis=0)
        n_ext = tt + HALO
        s2 = ext + pltpu.roll(ext, n_ext - 1, 0)
        t4 = s2[:, GROUP:]
        s4 = t4 + pltpu.roll(t4, n_ext - 2, 0)
        t8 = s4[:, GROUP:]
        s8 = t8 + pltpu.roll(t8, n_ext - 4, 0)
        t16 = s8[:, GROUP:]
        s16 = t16 + pltpu.roll(t16, n_ext - 8, 0)
        back = [s2[:, :GROUP], s4[:, :GROUP], s8[:, :GROUP], s16]
        carry[...] = scaled_all[:HALO, :]
        dzb = jnp.concatenate([back[g][:tt, :] - ddiffs[g] for g in range(len(WINDOWS))], axis=1)

        dzv = jnp.concatenate([dza, dzb], axis=1).astype(BF16)
        dh1 = _dot(dzv, win_ref[...])
        xv = x_ref[...]
        r1 = _rstd(xv)
        xhat = xv * r1
        shift1, scale1 = mod_ref[0:1, :], mod_ref[1:2, :]
        pre1 = n1pre_ref[...]
        gain1 = pre1 * (1.0 + scale1)
        h1 = (xhat * gain1 + shift1).astype(BF16)
        acc_in[...] += _dot_tn(dzv, h1)
        d_x, sum_h = _rms_bwd_gained(dh1, gain1, xhat, r1)
        gx_ref[...] = dx1_ref[...] + d_x
        small_ref[ROW_DMOD:ROW_DMOD + 1, :] += _colsum(dh1)
        small_ref[ROW_DMOD + 1:ROW_DMOD + 2, :] += pre1 * sum_h
        small_ref[ROW_N1PRE:ROW_N1PRE + 1, :] += (1.0 + scale1) * sum_h

        @pl.when(s == nt - 1)
        def _():
            gwin_ref[...] = acc_in[...].astype(BF16)
            gwout_ref[...] = acc_out[...].astype(BF16)
            bs = _unfold(bs_acc[...])
            for h in range(N_HEADS):
                small_ref[ROW_BS + h:ROW_BS + h + 1, 0:GROUP] = jnp.sum(
                    bs[:, h * GROUP:(h + 1) * GROUP].T, axis=0, keepdims=True)
            for a in range(n_sums):
                for r in range(2):
                    chip_copy(a, r).wait_recv()
                    chip_copy(a, r).wait_send()

    rev = lambda w: pl.BlockSpec((tt, w), lambda s: (nt - 1 - s, 0))
    zprev = pl.BlockSpec((HALO, D_B), lambda s: (jnp.maximum((nt - 1 - s) * hb - 1, 0), 2))
    hbm = pl.BlockSpec(memory_space=pl.ANY)
    outs = pl.pallas_call(
        body, name="attn_bwd", grid=(nt,),
        out_shape=tuple([jax.ShapeDtypeStruct((t_len, D), F32), jax.ShapeDtypeStruct((D_Z, D), BF16),
                         jax.ShapeDtypeStruct((D, D), BF16), jax.ShapeDtypeStruct((SMALL_ROWS, D), F32)]
                        + [jax.ShapeDtypeStruct(cs.shape, cs.dtype) for cs in chip_sums]),
        in_specs=[rev(D), rev(D), rev(D), rev(D_Z), zprev, rev(D), _full((8, D)), _full((1, D)),
                  _resident((D_Z, D)), _resident((D, D)), _full((N_HEADS, CHUNK, CHUNK)), _full((CHUNK, D_A)),
                  _full((1, D_A)), _full((1, D_A)), _full((len(WINDOWS), GROUP, GROUP)), _full((1, D_B)),
                  _full((1, D_B)), _full((8, D)), _full((8, D))] + [_resident(cs.shape) for cs in chip_sums],
        out_specs=tuple([rev(D), _resident((D_Z, D)), _resident((D, D)), _full((SMALL_ROWS, D))] + [hbm] * n_sums),
        scratch_shapes=[pltpu.VMEM((HALO, D_B), F32), pltpu.VMEM((D_Z, D), F32), pltpu.VMEM((D, D), F32),
                        pltpu.VMEM((tt, 2 * D_A), F32), pltpu.VMEM((CHUNK // 2, D), F32),
                        pltpu.SemaphoreType.DMA((2 * n_sums,)), pltpu.SemaphoreType.DMA((2 * n_sums,))],
        compiler_params=pltpu.CompilerParams(dimension_semantics=("arbitrary",), vmem_limit_bytes=VMEM_LIMIT),
    )(dmix, dx1, x, z, z, cat, mod, n1pre, w_in_t, w_out, w_sp, bs_rows, ln_g, ln_b, w_pool, b_pool, pool_scale,
      red_fwd, red_bwd, *chip_sums)
    return outs[:4], outs[4:]


def _adam(w, g, m, v):
    m2 = ADAM_B1 * m + (1.0 - ADAM_B1) * g
    v2 = ADAM_B2 * v + (1.0 - ADAM_B2) * (g * g)
    m_hat = m2 / (1.0 - ADAM_B1 ** ADAM_STEP)
    v_hat = v2 / (1.0 - ADAM_B2 ** ADAM_STEP)
    delta = -ADAM_LR * (m_hat / (jnp.sqrt(v_hat) + ADAM_EPS) + ADAM_WD * w)
    return delta, m2, v2


def _adamw_shard(name, rb, w, g, m, v):
    rows, cols = w.shape

    def body(w_ref, g_ref, m_ref, v_ref, d_ref, m2_ref, v2_ref):
        d_ref[...], m2_ref[...], v2_ref[...] = _adam(w_ref[...], g_ref[...], m_ref[...], v_ref[...])

    blk = pl.BlockSpec((rb, cols), lambda i: (i, 0))
    shp = jax.ShapeDtypeStruct((rows, cols), F32)
    return pl.pallas_call(
        body, name=name, grid=(rows // rb,), out_shape=(shp, shp, shp),
        in_specs=[blk] * 4, out_specs=(blk, blk, blk),
        compiler_params=pltpu.CompilerParams(dimension_semantics=("arbitrary",)),
    )(w, g, m, v)


def _adamw_ada(rb, w, sc, dmod_cols, m, v):
    rows, cols = w.shape

    def body(w_ref, sc_ref, dm_ref, m_ref, v_ref, g_ref, d_ref, m2_ref, v2_ref):
        g = _dot_tn(sc_ref[...].astype(BF16), dm_ref[...].astype(BF16))
        g_ref[...] = g
        d_ref[...], m2_ref[...], v2_ref[...] = _adam(w_ref[...], g, m_ref[...], v_ref[...])

    blk = pl.BlockSpec((rb, cols), lambda i: (i, 0))
    shp = jax.ShapeDtypeStruct((rows, cols), F32)
    return pl.pallas_call(
        body, name="adamw_ada", grid=(rows // rb,), out_shape=(shp, shp, shp, shp),
        in_specs=[blk, pl.BlockSpec((N_DEV, rb), lambda i: (0, i)), _full((N_DEV, cols)), blk, blk],
        out_specs=(blk, blk, blk, blk),
        compiler_params=pltpu.CompilerParams(dimension_semantics=("arbitrary",)),
    )(w, sc, dmod_cols, m, v)


def _unfold(acc_rows):
    return jnp.concatenate([acc_rows[:, :D_A], acc_rows[:, D_A:]], axis=0)


def _adamw_small(total, params):
    n = len(params)
    flat = [a for p in params for a in p]

    def body(*refs):
        s_ref = refs[0]
        p_refs = refs[1:1 + 3 * n]
        loss_ref = refs[1 + 3 * n]
        o_refs = refs[2 + 3 * n:]
        d_b_ada = s_ref[0:6, :]
        for b in range(1, N_DEV):
            d_b_ada = d_b_ada + s_ref[_table_row(b):_table_row(b) + 6, :]
        misc = lambda r: s_ref[PK_MISC + r - ROW_N1PRE:PK_MISC + r - ROW_N1PRE + 1, :]
        loss = jnp.sum(misc(ROW_LOSS), axis=-1, keepdims=True) * (0.5 / D)
        loss_ref[...] = jnp.broadcast_to(loss, (8, GROUP))
        mask = _tril_mask()
        ws = _unfold(s_ref[PK_WS:PK_WS + 64, :])
        wp = _unfold(s_ref[PK_WP:PK_WP + 64, :])
        grads = [
            d_b_ada,
            misc(ROW_N1PRE), misc(ROW_N1POST), misc(ROW_N2PRE), misc(ROW_N2POST),
            misc(ROW_LN)[:, :D_A], misc(ROW_LN)[:, D_A:],
            misc(ROW_POOL)[:, :D_B], misc(ROW_POOL)[:, D_B:],
            s_ref[PK_BS:PK_BS + N_HEADS, 0:GROUP],
            jnp.stack([ws[:, h * GROUP:(h + 1) * GROUP] * mask for h in range(N_HEADS)]),
            jnp.stack([wp[:, g * GROUP:(g + 1) * GROUP] for g in range(len(WINDOWS))]),
        ]
        for k in range(n):
            w_ref, m_ref, v_ref = p_refs[3 * k:3 * k + 3]
            g = grads[k]
            o_refs[4 * k][...] = g
            o_refs[4 * k + 1][...], o_refs[4 * k + 2][...], o_refs[4 * k + 3][...] = _adam(
                w_ref[...], g, m_ref[...], v_ref[...])

    vm = pl.BlockSpec(memory_space=pltpu.VMEM)
    out_shape = [jax.ShapeDtypeStruct((8, GROUP), F32)]
    for w, _, _ in params:
        out_shape += [jax.ShapeDtypeStruct(w.shape, F32)] * 4
    return pl.pallas_call(
        body, name="adamw_small", out_shape=tuple(out_shape),
        in_specs=[vm] * (1 + 3 * n), out_specs=tuple([vm] * len(out_shape)),
    )(total, *flat)


TT_ATTN_FWD = 512
TT_MLP_FWD = 512
TT_MLP = 256
TT_WGRAD = 2048
TT_ATTN_BWD = 512


def kernel(x, c, w_ada, b_ada, norm1_pre, norm1_post, w_in, w_spatial, b_spatial, ln_v_gain, ln_v_bias, w_pool, b_pool, pool_scale, w_out, norm2_pre, norm2_post, w_fc1, w_fc2, loss_target, m_w_ada, m_b_ada, m_norm1_pre, m_norm1_post, m_w_in, m_w_spatial, m_b_spatial, m_ln_v_gain, m_ln_v_bias, m_w_pool, m_b_pool, m_pool_scale, m_w_out, m_norm2_pre, m_norm2_post, m_w_fc1, m_w_fc2, v_w_ada, v_b_ada, v_norm1_pre, v_norm1_post, v_w_in, v_w_spatial, v_b_spatial, v_ln_v_gain, v_ln_v_bias, v_w_pool, v_b_pool, v_pool_scale, v_w_out, v_norm2_pre, v_norm2_post, v_w_fc1, v_w_fc2):
    t_len = x.shape[1]
    me = 4 * lax.axis_index("x") + 2 * lax.axis_index("y") + lax.axis_index("c")
    ada_cols = w_ada.shape[1]
    tt = lambda want: min(want, t_len)

    x2 = x.reshape(t_len, D)
    tgt = loss_target.reshape(t_len, D)
    row = lambda a: a.reshape(1, -1)

    b_my = lax.dynamic_slice_in_dim(b_ada, me * ada_cols, ada_cols).reshape(1, ada_cols)
    modp, sc, (g_in, g_out), fc_shards = _fwd_comm(jnp.broadcast_to(c, (8, D)), w_ada, b_my,
                                                   [w_in.T, w_out], [w_fc1, w_fc2])
    mod = jnp.concatenate([modp.reshape(6, D), jnp.zeros((2, D), F32)], axis=0)
    w_in_t = g_in.reshape(D_Z, D)
    w_out_all = g_out.reshape(D, D)

    bs_rows = jnp.repeat(b_spatial.T, GROUP, axis=1)
    attn_consts = (w_spatial, bs_rows, row(ln_v_gain), row(ln_v_bias), w_pool, row(b_pool), row(pool_scale))

    (z, cat, mix, x1, h2, r_begun, f_head), (w1_early, w2_early) = _attn_fwd(
        tt(TT_ATTN_FWD), x2, mod, row(norm1_pre), row(norm1_post), w_in_t, w_out_all, *attn_consts, fc_shards,
        row(norm2_pre))
    (r_early, f_early), (w1_late, w2_late) = _mlp_fwd_early(
        tt(TT_MLP_FWD), r_begun, h2, f_head, w1_early, w2_early)
    r_late, df, da, dmix, dx1, red_fwd, red_bwd = _mlp_late_bwd(
        tt(TT_MLP), r_early, x1, h2, f_early, tgt, mix, mod, row(norm2_pre), row(norm2_post), row(norm1_post),
        w1_early, w2_early, w1_late, w2_late)
    own_w1, own_w2, sums_w1, sums_w2, diag_w1, diag_w2 = _mlp_wgrad(tt(TT_WGRAD), r_early, r_late, da, df, h2)
    (grad_x, p_in, p_out, small), (arr_w1, arr_w2) = _attn_bwd(
        tt(TT_ATTN_BWD), dmix, dx1, x2, z, cat, mod, row(norm1_pre), w_in_t, w_out_all, *attn_consts,
        red_fwd, red_bwd, [sums_w1, sums_w2])
    (grad_in_t, grad_out, total), ((grad_w1, d_w1, m_w1, v_w1), (grad_w2, d_w2, m_w2, v_w2)) = _tail_comm(
        [p_in.reshape(N_DEV, D_Z // N_DEV, D), p_out.reshape(N_DEV, D // N_DEV, D)], small, 64,
        [(w_fc1, own_w1, arr_w1, diag_w1, m_w_fc1, v_w_fc1), (w_fc2, own_w2, arr_w2, diag_w2, m_w_fc2, v_w_fc2)])

    d_out, m_out, v_out = _adamw_shard("adamw_out", 128, w_out, grad_out, m_w_out, v_w_out)
    d_in_t, m_in_t, v_in_t = _adamw_shard("adamw_in", D_Z // N_DEV, w_in.T, grad_in_t, m_w_in.T, v_w_in.T)
    table = jnp.concatenate([total[0:PACK_FINE, :], total[PK_TABLE_B:PK_MISC, :]], axis=0)
    dmod_all = table.reshape(N_DEV, 8, D)[:, :6, :].reshape(N_DEV, 6 * D)
    dmod_cols = lax.dynamic_slice_in_dim(dmod_all, me * ada_cols, ada_cols, axis=1)
    grad_ada, d_ada, m_ada, v_ada = _adamw_ada(256, w_ada, sc, dmod_cols, m_w_ada, v_w_ada)

    six = lambda a: a.reshape(6, D)
    small_params = [
        (six(b_ada), six(m_b_ada), six(v_b_ada)),
        (row(norm1_pre), row(m_norm1_pre), row(v_norm1_pre)),
        (row(norm1_post), row(m_norm1_post), row(v_norm1_post)),
        (row(norm2_pre), row(m_norm2_pre), row(v_norm2_pre)),
        (row(norm2_post), row(m_norm2_post), row(v_norm2_post)),
        (row(ln_v_gain), row(m_ln_v_gain), row(v_ln_v_gain)),
        (row(ln_v_bias), row(m_ln_v_bias), row(v_ln_v_bias)),
        (row(pool_scale), row(m_pool_scale), row(v_pool_scale)),
        (row(b_pool), row(m_b_pool), row(v_b_pool)),
        (b_spatial, m_b_spatial, v_b_spatial),
        (w_spatial, m_w_spatial, v_w_spatial),
        (w_pool, m_w_pool, v_w_pool),
    ]
    outs = _adamw_small(total, small_params)
    loss = outs[0][0, 0]
    names = ["b_ada", "norm1_pre", "norm1_post", "norm2_pre", "norm2_post", "ln_v_gain", "ln_v_bias", "pool_scale",
             "b_pool", "b_spatial", "w_spatial", "w_pool"]
    shapes = dict(b_ada=b_ada.shape, norm1_pre=norm1_pre.shape, norm1_post=norm1_post.shape,
                  norm2_pre=norm2_pre.shape, norm2_post=norm2_post.shape, ln_v_gain=ln_v_gain.shape,
                  ln_v_bias=ln_v_bias.shape, pool_scale=pool_scale.shape, b_pool=b_pool.shape,
                  b_spatial=b_spatial.shape, w_spatial=w_spatial.shape, w_pool=w_pool.shape)
    res = {}
    for k, nm in enumerate(names):
        res[nm] = tuple(o.reshape(shapes[nm]) for o in outs[1 + 4 * k:5 + 4 * k])
    res["w_ada"] = (grad_ada, d_ada, m_ada, v_ada)
    res["w_in"] = (grad_in_t.T, d_in_t.T, m_in_t.T, v_in_t.T)
    res["w_out"] = (grad_out, d_out, m_out, v_out)
    res["w_fc1"] = (grad_w1, d_w1, m_w1, v_w1)
    res["w_fc2"] = (grad_w2, d_w2, m_w2, v_w2)

    order = ["w_ada", "b_ada", "norm1_pre", "norm1_post", "w_in", "w_spatial", "b_spatial", "ln_v_gain", "ln_v_bias",
             "w_pool", "b_pool", "pool_scale", "w_out", "norm2_pre", "norm2_post", "w_fc1", "w_fc2"]
    return (loss, grad_x.reshape(x.shape),
            *[res[nm][0] for nm in order], *[res[nm][1] for nm in order],
            *[res[nm][2] for nm in order], *[res[nm][3] for nm in order])
```

```python
import functools

import jax
import jax.numpy as jnp
from jax import lax
from jax.experimental import pallas as pl
from jax.experimental.pallas import tpu as pltpu

F32 = jnp.float32
BF16 = jnp.bfloat16
MESH = pl.DeviceIdType.MESH

N_DEV = 8
D = 1024
D_A = 512
D_B = 512
D_Z = 2 * D_A + D_B
N_HEADS = 4
CHUNK = 128
WINDOWS = (2, 4, 8, 16)
GROUP = 128
D_FF = 4096
FF_BLK = D_FF // N_DEV
HALO = 16
EPS = 1e-6
VMEM_LIMIT = 60 * 1024 * 1024

ADAM_LR = 0.001
ADAM_B1 = 0.9
ADAM_B2 = 0.999
ADAM_EPS = 1e-08
ADAM_WD = 0.01
ADAM_STEP = 10

ROW_DMOD = 0
ROW_N1PRE, ROW_N1POST, ROW_N2PRE, ROW_N2POST = 8, 9, 10, 11
ROW_LN = 12
ROW_POOL = 13
ROW_LOSS = 14
ROW_BS = 16
ROW_WS = 24
ROW_WP = 88
SMALL_ROWS = 152
PACK_FINE = 40
PACK_HALF = PACK_FINE + 64
PACK_ROWS = 2 * PACK_HALF
PK_WS = PACK_FINE
PK_TABLE_B = PACK_HALF
PK_MISC = PK_TABLE_B + 24
PK_BS = PK_MISC + 8
PK_WP = PK_BS + 8


def _table_row(b):
    if isinstance(b, int):
        return 8 * b if 8 * b < PACK_FINE else 8 * b + PK_TABLE_B - PACK_FINE
    return 8 * b + jnp.where(8 * b < PACK_FINE, 0, PK_TABLE_B - PACK_FINE)


def _dot(a, b):
    return jnp.dot(a, b, preferred_element_type=F32)


def _dot_nt(a, b):
    return lax.dot_general(a, b, (((1,), (1,)), ((), ())), preferred_element_type=F32)


def _dot_tn(a, b):
    return lax.dot_general(a, b, (((0,), (0,)), ((), ())), preferred_element_type=F32)


def _rstd(v):
    return lax.rsqrt(jnp.mean(v * v, axis=-1, keepdims=True) + EPS)


def _rms_bwd(d_hat, hat, rstd):
    return rstd * (d_hat - hat * jnp.mean(d_hat * hat, axis=-1, keepdims=True))


def _rms_bwd_gained(g, gain, hat, rstd):
    g_hat = g * hat
    d_v = rstd * (g * gain - hat * jnp.mean(g_hat * gain, axis=-1, keepdims=True))
    return d_v, _colsum(g_hat)


_K0 = 0.7978845608028654
_K1 = 0.044715


def _gelu_parts(v):
    t = jnp.tanh(v * (_K0 + (_K0 * _K1) * (v * v)))
    return t, v * (0.5 + 0.5 * t)


def _gelu_grad(v, t):
    return (0.5 + 0.5 * t) + (0.5 * v) * (1.0 - t * t) * (_K0 + (3.0 * _K0 * _K1) * (v * v))


def _colsum(v):
    return jnp.sum(v, axis=0, keepdims=True)


def _full(shape):
    n = len(shape)
    return pl.BlockSpec(shape, lambda *_: (0,) * n)


def _resident(shape):
    n = len(shape)
    return pl.BlockSpec(shape, lambda *_: (0,) * n, pipeline_mode=pl.Buffered(1))


def _place():
    x, y, c = lax.axis_index("x"), lax.axis_index("y"), lax.axis_index("c")
    return x, y, c


def _flip(v, bit):
    return 1 - v if bit else v


def _peer(x, y, c, k):
    return (_flip(x, (k >> 2) & 1), _flip(y, (k >> 1) & 1), _flip(c, k & 1))


def _index(p):
    return 4 * p[0] + 2 * p[1] + p[2]


def _two_level_gather_begin(x, y, c, out_refs, send_sems, recv_sems):
    me = (x, y, c)
    sibling = (x, y, 1 - c)
    chips = [(1 - x, y), (x, 1 - y), (1 - x, 1 - y)]

    def copy(a, k, block, to):
        ref = out_refs[a].at[_index(block)]
        return pltpu.make_async_remote_copy(
            src_ref=ref, dst_ref=ref, send_sem=send_sems.at[7 * a + k], recv_sem=recv_sems.at[7 * a + k],
            device_id=to, device_id_type=MESH)

    first = []
    for a in range(len(out_refs)):
        first.append(copy(a, 0, me, sibling))
        first += [copy(a, 1 + j, me, (*chip, c)) for j, chip in enumerate(chips)]
    for cp in first:
        cp.start()
    return copy, first, me, sibling, chips


def _two_level_gather_finish(c, n, begun):
    copy, first, me, sibling, chips = begun
    passed = []
    for a in range(n):
        for j, chip in enumerate(chips):
            copy(a, 1 + j, (*chip, c), me).wait_recv()
            fwd = copy(a, 4 + j, (*chip, c), sibling)
            fwd.start()
            passed.append(fwd)
    for a in range(n):
        copy(a, 0, sibling, me).wait_recv()
        for j, chip in enumerate(chips):
            copy(a, 4 + j, (*chip, 1 - c), me).wait_recv()
    for cp in first + passed:
        cp.wait_send()


def _fwd_comm(c8, w_ada, b_my, gathered, kept):
    ncol = w_ada.shape[1]
    n_g, n_k = len(gathered), len(kept)

    def body(c_ref, w_ref, b_ref, *rest):
        g_in, k_in = rest[:n_g], rest[n_g:n_g + n_k]
        modp_ref, sc_ref = rest[n_g + n_k:n_g + n_k + 2]
        g_out = rest[n_g + n_k + 2:2 * n_g + n_k + 2]
        k_out = rest[2 * n_g + n_k + 2:2 * n_g + 2 * n_k + 2]
        cg, mg, part, send_sems, recv_sems, g_send, g_recv = rest[2 * n_g + 2 * n_k + 2:]
        x, y, c = _place()
        me = _index((x, y, c))
        for a in range(n_g):
            g_out[a][me] = g_in[a][...].astype(BF16)
        begun = _two_level_gather_begin(x, y, c, g_out, g_send, g_recv)
        for a in range(n_k):
            k_out[a][...] = k_in[a][...].astype(BF16)

        def c_copy(k):
            p = _peer(x, y, c, k)
            return pltpu.make_async_remote_copy(
                src_ref=c_ref, dst_ref=cg.at[me], send_sem=send_sems.at[k - 1], recv_sem=recv_sems.at[k - 1],
                device_id=p, device_id_type=MESH)

        def c_arrival(k):
            p = _peer(x, y, c, k)
            return pltpu.make_async_remote_copy(
                src_ref=c_ref, dst_ref=cg.at[_index(p)], send_sem=send_sems.at[k - 1], recv_sem=recv_sems.at[k - 1],
                device_id=p, device_id_type=MESH)

        def m_copy(k):
            p = _peer(x, y, c, k)
            return pltpu.make_async_remote_copy(
                src_ref=part, dst_ref=mg.at[me], send_sem=send_sems.at[6 + k], recv_sem=recv_sems.at[6 + k],
                device_id=p, device_id_type=MESH)

        def m_arrival(k):
            p = _peer(x, y, c, k)
            return pltpu.make_async_remote_copy(
                src_ref=part, dst_ref=mg.at[_index(p)], send_sem=send_sems.at[6 + k], recv_sem=recv_sems.at[6 + k],
                device_id=p, device_id_type=MESH)

        for k in range(1, N_DEV):
            c_copy(k).start()
        cg[me] = c_ref[...]
        for k in range(1, N_DEV):
            c_arrival(k).wait_recv()
        c_all = jnp.concatenate([cg[j, 0:1, :] for j in range(N_DEV)], axis=0)
        sc = c_all * jax.nn.sigmoid(c_all)
        sc_ref[...] = sc
        part[...] = _dot(sc.astype(BF16), w_ref[...].astype(BF16)) + b_ref[...]
        for k in range(1, N_DEV):
            m_copy(k).start()
        mg[me] = part[...]
        for k in range(1, N_DEV):
            m_arrival(k).wait_recv()
        for j in range(N_DEV):
            modp_ref[j:j + 1, :] = mg[j, pl.ds(me, 1), :]
        _two_level_gather_finish(c, n_g, begun)
        for k in range(1, N_DEV):
            c_copy(k).wait_send()
            m_copy(k).wait_send()

    vm = pl.BlockSpec(memory_space=pltpu.VMEM)
    outs = pl.pallas_call(
        body, name="fwd_comm",
        out_shape=tuple([jax.ShapeDtypeStruct((N_DEV, ncol), F32), jax.ShapeDtypeStruct((N_DEV, D), F32)]
                        + [jax.ShapeDtypeStruct((N_DEV,) + s.shape, BF16) for s in gathered]
                        + [jax.ShapeDtypeStruct(s.shape, BF16) for s in kept]),
        in_specs=[vm] * (3 + n_g + n_k), out_specs=tuple([vm] * (2 + n_g + n_k)),
        scratch_shapes=[
            pltpu.VMEM((N_DEV, 8, D), F32),
            pltpu.VMEM((N_DEV, N_DEV, ncol), F32),
            pltpu.VMEM((N_DEV, ncol), F32),
            pltpu.SemaphoreType.DMA((2 * (N_DEV - 1),)),
            pltpu.SemaphoreType.DMA((2 * (N_DEV - 1),)),
            pltpu.SemaphoreType.DMA((7 * n_g,)),
            pltpu.SemaphoreType.DMA((7 * n_g,)),
        ],
        compiler_params=pltpu.CompilerParams(vmem_limit_bytes=VMEM_LIMIT),
    )(c8, w_ada, b_my, *gathered, *kept)
    return outs[0], outs[1], outs[2:2 + n_g], outs[2 + n_g:]


FC_EARLY = 6
FC_HEAD = 2
R_HEAD_COLS = (FC_EARLY - FC_HEAD) * FF_BLK
WGRAD_ORDER = (7, 6, 1, 3, 5, 2, 4, 0)


def _early_col(j):
    return R_HEAD_COLS + j * FF_BLK if j < FC_HEAD else (j - FC_HEAD) * FF_BLK


class _Copies:
    def __init__(self, entries, send_sems, recv_sems):
        self.place = _place()
        self.entries, self.send_sems, self.recv_sems = entries, send_sems, recv_sems

    def _copy(self, i, arrival=False):
        src, dst, rel = self.entries[i]
        return pltpu.make_async_remote_copy(
            src_ref=dst if arrival else src, dst_ref=dst, send_sem=self.send_sems.at[i],
            recv_sem=self.recv_sems.at[i], device_id=_peer(*self.place, rel), device_id_type=MESH)

    def start(self, *which):
        for i in which:
            self._copy(i).start()

    def wait_recv(self, *which):
        for i in which:
            self._copy(i, arrival=True).wait_recv()

    def wait_send(self, *which):
        for i in which:
            self._copy(i).wait_send()


TAIL_STEPS = 8


def _tail_comm(parts, small, row_chunk, fc):
    n, n_fc = len(parts), len(fc)

    def body(*refs):
        p_refs, small_ref = refs[:n], refs[n]
        fc_in = refs[n + 1:n + 1 + 6 * n_fc]
        outs = refs[n + 1 + 6 * n_fc:]
        g_refs, total_ref = outs[:n], outs[n]
        fc_out = outs[n + 1:n + 1 + 4 * n_fc]
        scr = outs[n + 1 + 4 * n_fc:]
        from_sib = scr[0:n]
        chip_out = scr[n:2 * n]
        chip_in = scr[2 * n:3 * n]
        pack, pack_sib, fine, bulk, total_scr = scr[3 * n:3 * n + 5]
        send_a, recv_a, send_b, recv_b, send_s, recv_s = scr[3 * n + 5:]
        step = pl.program_id(0)
        x, y, c = _place()
        me = _index((x, y, c))
        sibling = (x, y, 1 - c)
        my_chip = 2 * x + y
        others = [(1 - x, y), (x, 1 - y), (1 - x, 1 - y)]
        my_half = pl.ds(pl.multiple_of(PACK_HALF * c, 8), PACK_HALF)

        def pack_to_sibling():
            return pltpu.make_async_remote_copy(
                src_ref=pack, dst_ref=pack_sib, send_sem=send_s.at[0], recv_sem=recv_s.at[0],
                device_id=sibling, device_id_type=MESH)

        def half_to_chip(r, part):
            buf = (fine, bulk)[part]
            return pltpu.make_async_remote_copy(
                src_ref=buf.at[my_chip], dst_ref=buf.at[my_chip],
                send_sem=send_s.at[1 + 3 * part + r], recv_sem=recv_s.at[1 + 3 * part + r],
                device_id=(*others[r], c), device_id_type=MESH)

        def half_from_chip(r, part):
            k = 2 * others[r][0] + others[r][1]
            buf = (fine, bulk)[part]
            return pltpu.make_async_remote_copy(
                src_ref=buf.at[k], dst_ref=buf.at[k],
                send_sem=send_s.at[1 + 3 * part + r], recv_sem=recv_s.at[1 + 3 * part + r],
                device_id=(*others[r], c), device_id_type=MESH)

        def total_to_sibling():
            return pltpu.make_async_remote_copy(
                src_ref=total_scr.at[my_half], dst_ref=total_scr.at[my_half],
                send_sem=send_s.at[7], recv_sem=recv_s.at[7], device_id=sibling, device_id_type=MESH)

        def total_from_sibling():
            sib_half = pl.ds(pl.multiple_of(PACK_HALF * (1 - c), 8), PACK_HALF)
            return pltpu.make_async_remote_copy(
                src_ref=total_scr.at[sib_half], dst_ref=total_scr.at[sib_half],
                send_sem=send_s.at[7], recv_sem=recv_s.at[7], device_id=sibling, device_id_type=MESH)

        def to_sibling(a, k):
            return pltpu.make_async_remote_copy(
                src_ref=p_refs[a].at[2 * k + (1 - c)], dst_ref=from_sib[a].at[k],
                send_sem=send_a.at[a], recv_sem=recv_a.at[a], device_id=sibling, device_id_type=MESH)

        def all_from_sibling(a):
            return pltpu.make_async_remote_copy(
                src_ref=from_sib[a], dst_ref=from_sib[a], send_sem=send_a.at[a], recv_sem=recv_a.at[a],
                device_id=sibling, device_id_type=MESH)

        def to_chip(a, r):
            return pltpu.make_async_remote_copy(
                src_ref=chip_out[a].at[r], dst_ref=chip_in[a].at[r],
                send_sem=send_b.at[3 * a + r], recv_sem=recv_b.at[3 * a + r],
                device_id=(*others[r], c), device_id_type=MESH)

        @pl.when(step == 0)
        def _():
            pack[0:PACK_FINE, :] = jnp.zeros((PACK_FINE, D), F32)
            pack[PK_TABLE_B:PK_MISC, :] = jnp.zeros((PK_MISC - PK_TABLE_B, D), F32)
            pack[pl.ds(pl.multiple_of(_table_row(me), 8), 8), :] = small_ref[0:8, :]
            pack[PK_WS:PK_WS + 64, :] = small_ref[ROW_WS:ROW_WS + 64, :]
            pack[PK_MISC:PK_MISC + 8, :] = small_ref[ROW_N1PRE:ROW_N1PRE + 8, :]
            pack[PK_BS:PK_BS + 8, :] = small_ref[ROW_BS:ROW_BS + 8, :]
            pack[PK_WP:PK_WP + 64, :] = small_ref[ROW_WP:ROW_WP + 64, :]
            pack_to_sibling().start()
            for a in range(n):
                for k in range(4):
                    to_sibling(a, k).start()

        @pl.when(step == 1)
        def _():
            pack_to_sibling().wait_recv()
            chip_sum = pack[my_half, :] + pack_sib[my_half, :]
            fine[my_chip] = chip_sum[:PACK_FINE, :]
            bulk[my_chip] = chip_sum[PACK_FINE:, :].astype(BF16)
            for r in range(3):
                half_to_chip(r, 0).start()
                half_to_chip(r, 1).start()
            for a in range(n):
                all_from_sibling(a).wait_recv()
                rows = p_refs[a].shape[1]
                for r in range(3):
                    k = 2 * others[r][0] + others[r][1]
                    for s in range(0, rows, row_chunk):
                        sl = pl.ds(s, row_chunk)
                        chip_out[a][r, sl, :] = (p_refs[a][2 * k + c, sl, :].astype(F32)
                                                 + from_sib[a][k, sl, :].astype(F32)).astype(BF16)
                    to_chip(a, r).start()
                for s in range(0, rows, row_chunk):
                    sl = pl.ds(s, row_chunk)
                    g_refs[a][sl, :] = (p_refs[a][2 * my_chip + c, sl, :].astype(F32)
                                        + from_sib[a][my_chip, sl, :].astype(F32))

        for k in range(n_fc):
            w_ref, own_ref, arr_ref, diag_ref, m_ref, v_ref = fc_in[6 * k:6 * k + 6]
            g = own_ref[...]
            for r in range(2):
                g = g + arr_ref[r].astype(F32)
            g = g + diag_ref[...].astype(F32)
            fc_out[4 * k][...] = g
            fc_out[4 * k + 1][...], fc_out[4 * k + 2][...], fc_out[4 * k + 3][...] = _adam(
                w_ref[...], g, m_ref[...], v_ref[...])

        @pl.when(step == TAIL_STEPS - 1)
        def _():
            for r in range(3):
                half_from_chip(r, 0).wait_recv()
                half_from_chip(r, 1).wait_recv()
            half_start = pl.multiple_of(PACK_HALF * c, 8)
            total_scr[pl.ds(half_start, PACK_FINE), :] = ((fine[0] + fine[1]) + fine[2]) + fine[3]
            total_scr[pl.ds(half_start + PACK_FINE, PACK_HALF - PACK_FINE), :] = (
                (bulk[0].astype(F32) + bulk[1].astype(F32)) + bulk[2].astype(F32)) + bulk[3].astype(F32)
            total_to_sibling().start()
            for a in range(n):
                rows = p_refs[a].shape[1]
                for r in range(3):
                    to_chip(a, r).wait_recv()
                    for s in range(0, rows, row_chunk):
                        sl = pl.ds(s, row_chunk)
                        g_refs[a][sl, :] = g_refs[a][sl, :] + chip_in[a][r, sl, :].astype(F32)
            total_from_sibling().wait_recv()
            total_ref[...] = total_scr[...]
            for a in range(n):
                all_from_sibling(a).wait_send()
                for r in range(3):
                    to_chip(a, r).wait_send()
            pack_to_sibling().wait_send()
            for r in range(3):
                half_to_chip(r, 0).wait_send()
                half_to_chip(r, 1).wait_send()
            total_to_sibling().wait_send()

    fc_specs_in, fc_specs_out, fc_shapes, fc_args = [], [], [], []
    for w, own, arrived, diagonal, m, v in fc:
        rows, cols = w.shape
        blk = pl.BlockSpec((rows // TAIL_STEPS, cols), lambda i: (i, 0))
        fc_specs_in += [blk, blk, pl.BlockSpec((2, rows // TAIL_STEPS, cols), lambda i: (0, i, 0)), blk, blk, blk]
        fc_specs_out += [blk] * 4
        fc_shapes += [jax.ShapeDtypeStruct((rows, cols), F32)] * 4
        fc_args += [w, own, arrived, diagonal, m, v]
    outs = pl.pallas_call(
        body, name="tail_comm", grid=(TAIL_STEPS,),
        out_shape=tuple([jax.ShapeDtypeStruct(p.shape[1:], F32) for p in parts]
                        + [jax.ShapeDtypeStruct((PACK_ROWS, D), F32)] + fc_shapes),
        in_specs=[_resident(p.shape) for p in parts] + [_resident(small.shape)] + fc_specs_in,
        out_specs=tuple([_full(p.shape[1:]) for p in parts] + [_full((PACK_ROWS, D))] + fc_specs_out),
        scratch_shapes=(
            [pltpu.VMEM((4,) + p.shape[1:], BF16) for p in parts]
            + [pltpu.VMEM((3,) + p.shape[1:], BF16) for p in parts]
            + [pltpu.VMEM((3,) + p.shape[1:], BF16) for p in parts]
            + [pltpu.VMEM((PACK_ROWS, D), F32), pltpu.VMEM((PACK_ROWS, D), F32),
               pltpu.VMEM((4, PACK_FINE, D), F32), pltpu.VMEM((4, PACK_HALF - PACK_FINE, D), BF16),
               pltpu.VMEM((PACK_ROWS, D), F32)]
            + [pltpu.SemaphoreType.DMA((n,)), pltpu.SemaphoreType.DMA((n,)),
               pltpu.SemaphoreType.DMA((3 * n,)), pltpu.SemaphoreType.DMA((3 * n,)),
               pltpu.SemaphoreType.DMA((8,)), pltpu.SemaphoreType.DMA((8,))]),
        compiler_params=pltpu.CompilerParams(dimension_semantics=("arbitrary",), vmem_limit_bytes=VMEM_LIMIT),
    )(*parts, small, *fc_args)
    return outs[:n + 1], [outs[n + 1 + 4 * k:n + 5 + 4 * k] for k in range(n_fc)]


def _tril_mask():
    row = lax.broadcasted_iota(jnp.int32, (CHUNK, CHUNK), 0)
    col = lax.broadcasted_iota(jnp.int32, (CHUNK, CHUNK), 1)
    return (col <= row).astype(F32)


def _window_sums(ext):
    s2 = ext + pltpu.roll(ext, 1, 0)
    t4 = s2[:, GROUP:]
    s4 = t4 + pltpu.roll(t4, 2, 0)
    t8 = s4[:, GROUP:]
    s8 = t8 + pltpu.roll(t8, 4, 0)
    t16 = s8[:, GROUP:]
    s16 = t16 + pltpu.roll(t16, 8, 0)
    return [s2[:, :GROUP], s4[:, :GROUP], s8[:, :GROUP], s16]


def _inv_counts(first_pos, rows):
    pos = first_pos + lax.broadcasted_iota(jnp.int32, (rows, 1), 0)
    return [1.0 / jnp.minimum(pos + 1, w).astype(F32) for w in WINDOWS]


def _pool_diff(zb, halo, first_pos):
    tt = zb.shape[0]
    sums = _window_sums(jnp.concatenate([halo, zb], axis=0))
    inv = _inv_counts(first_pos, tt)
    return [sums[g][HALO:, :] * inv[g] - zb[:, g * GROUP:(g + 1) * GROUP] for g in range(len(WINDOWS))]


def _attn_fwd(tt, x, mod, n1pre, n1post, w_in_t, w_out_shard, w_sp, bs_rows, ln_g, ln_b, w_pool, b_pool, pool_scale,
              fc_shards, n2pre):
    t_len = x.shape[0]
    nt = t_len // tt
    wo_rows = w_out_shard.shape[0]

    def body(x_ref, xb_ref, mod_ref, n1pre_ref, n1post_ref, win_ref, wo_ref, wsp_ref, bs_ref, lng_ref, lnb_ref,
             wp_ref, bp_ref, ps_ref, w1_ref, w2_ref, n2pre_ref,
             z_ref, cat_ref, mix_ref, x1_ref, h2_ref, r_ref, f_ref, e1_ref, e2_ref, wout_ref,
             carry, land1, land2, sib1, sib2, cat_keep, wout_scr, send_sems, recv_sems, local_sems,
             wo_send, wo_recv):
        i = pl.program_id(0)
        px, py, pc = _place()

        def wo_block(rel):
            start = pl.multiple_of(wo_rows * _index(_peer(px, py, pc, rel)), wo_rows)
            return wout_scr.at[pl.ds(start, wo_rows), :]

        wo_copies = _Copies(
            [(wo_ref, wo_block(0), 1), (wo_ref, wo_block(0), 2), (wo_ref, wo_block(0), 4), (wo_ref, wo_block(0), 6),
             (wo_block(2), wo_block(2), 1), (wo_block(4), wo_block(4), 1), (wo_block(6), wo_block(6), 1)],
            wo_send, wo_recv)
        wo_keep = pltpu.make_async_copy(wout_scr, wout_ref, local_sems.at[8])
        copies = _Copies(
            [(w1_ref, sib1, 1), (w2_ref, sib2, 1),
             (w1_ref, land1.at[0], 2), (w2_ref, land2.at[0], 2),
             (w1_ref, land1.at[1], 4), (w2_ref, land2.at[1], 4),
             (land1.at[0], e1_ref.at[3], 1), (land2.at[0], e2_ref.at[3], 1),
             (land1.at[1], e1_ref.at[5], 1), (land2.at[1], e2_ref.at[5], 1)],
            send_sems, recv_sems)
        keep = [pltpu.make_async_copy(w1_ref, e1_ref.at[0], local_sems.at[0]),
                pltpu.make_async_copy(w2_ref, e2_ref.at[0], local_sems.at[1]),
                pltpu.make_async_copy(land1.at[0], e1_ref.at[2], local_sems.at[2]),
                pltpu.make_async_copy(land1.at[1], e1_ref.at[4], local_sems.at[3]),
                pltpu.make_async_copy(land2.at[0], e2_ref.at[2], local_sems.at[4]),
                pltpu.make_async_copy(land2.at[1], e2_ref.at[4], local_sems.at[5]),
                pltpu.make_async_copy(sib1, e1_ref.at[1], local_sems.at[6]),
                pltpu.make_async_copy(sib2, e2_ref.at[1], local_sems.at[7])]

        @pl.when(i == 0)
        def _():
            wo_copies.start(0, 1, 2, 3)
            copies.start(0, 1, 2, 4, 3, 5)
            keep[0].start()
            keep[1].start()
            own = pl.multiple_of(wo_rows * _index((px, py, pc)), wo_rows)
            wout_scr[pl.ds(own, wo_rows), :] = wo_ref[...]
            carry[...] = jnp.zeros_like(carry)

        @pl.when(i == nt // 2)
        def _():
            copies.wait_recv(2, 4)
            copies.start(6, 8)
            keep[2].start()
            keep[3].start()

        @pl.when(i == nt - 1)
        def _():
            copies.wait_recv(3, 5)
            copies.start(7, 9)
            keep[4].start()
            keep[5].start()

        shift1, scale1, gate1 = mod_ref[0:1, :], mod_ref[1:2, :], mod_ref[2:3, :]

        @pl.when(i < nt)
        def _():
            xv = x_ref[...]
            h1 = (xv * _rstd(xv)) * (n1pre_ref[...] * (1.0 + scale1)) + shift1
            z = _dot_nt(h1.astype(BF16), win_ref[...])
            z_ref[...] = z

            _, ga = _gelu_parts(z[:, :2 * D_A])
            u, vr = ga[:, :D_A], ga[:, D_A:]
            dv = vr - jnp.mean(vr, axis=-1, keepdims=True)
            v = (dv * lax.rsqrt(jnp.mean(dv * dv, axis=-1, keepdims=True) + EPS)) * lng_ref[...] + lnb_ref[...]
            vb = v.astype(BF16)
            mask = _tril_mask()
            wc = [(wsp_ref[h] * mask).astype(BF16) for h in range(N_HEADS)]
            for ch in range(tt // CHUNK):
                rows = slice(ch * CHUNK, (ch + 1) * CHUNK)
                for h in range(N_HEADS):
                    cols = slice(h * GROUP, (h + 1) * GROUP)
                    mixed = _dot(wc[h], vb[rows, cols]) + bs_ref[:, cols]
                    cat_ref[rows, cols] = (u[rows, cols] * mixed).astype(BF16)

            zb = z[:, 2 * D_A:]
            diff = _pool_diff(zb, carry[...], i * tt)
            carry[...] = zb[tt - HALO:, :]
            for g in range(len(WINDOWS)):
                cols = slice(g * GROUP, (g + 1) * GROUP)
                pre = _dot(diff[g].astype(BF16), wp_ref[g].astype(BF16)) + bp_ref[:, cols]
                cat_ref[:, D_A + g * GROUP:D_A + (g + 1) * GROUP] = (pre * ps_ref[:, cols]).astype(BF16)
            cat_keep[i % (ATTN_LAG + 1)] = cat_ref[...]

        @pl.when(i == 0)
        def _():
            copies.wait_recv(0, 1)
            keep[6].start()
            keep[7].start()

        @pl.when(i == 1)
        def _():
            wo_copies.wait_recv(1, 2, 3)
            wo_copies.start(4, 5, 6)

        @pl.when(i == ATTN_LAG)
        def _():
            wo_copies.wait_recv(0, 4, 5, 6)
            wo_keep.start()

        @pl.when(i >= ATTN_LAG)
        def _():
            xv = xb_ref[...]
            mix = _dot(cat_keep[(i - ATTN_LAG) % (ATTN_LAG + 1)], wout_scr[...])
            mix_ref[...] = mix
            x1v = xv + (mix * _rstd(mix)) * (gate1 * n1post_ref[...])
            x1_ref[...] = x1v
            shift2, scale2 = mod_ref[3:4, :], mod_ref[4:5, :]
            h2 = ((x1v * _rstd(x1v)) * (n2pre_ref[...] * (1.0 + scale2)) + shift2).astype(BF16)
            h2_ref[...] = h2
            for j, (w1, w2) in enumerate(((w1_ref, w2_ref), (sib1, sib2))):
                ra = jnp.maximum(_dot(h2, w1[...]), 0.0)
                r = (ra * ra).astype(BF16)
                r_ref[:, j * FF_BLK:(j + 1) * FF_BLK] = r
                if j == 0:
                    f_ref[...] = _dot(r, w2[...])
                else:
                    f_ref[...] += _dot(r, w2[...])

        @pl.when(i == nt - 1 + ATTN_LAG)
        def _():
            copies.wait_recv(6, 7, 8, 9)
            copies.wait_send(*range(10))
            wo_copies.wait_send(*range(7))
            for cp in keep:
                cp.wait()
            wo_keep.wait()

    first = lambda w: pl.BlockSpec((tt, w), lambda i: (jnp.minimum(i, nt - 1), 0))
    second = lambda w: pl.BlockSpec((tt, w), lambda i: (jnp.maximum(i - ATTN_LAG, 0), 0))
    r_head = pl.BlockSpec((tt, FC_HEAD * FF_BLK),
                          lambda i: (jnp.maximum(i - ATTN_LAG, 0), R_HEAD_COLS // (FC_HEAD * FF_BLK)))
    hbm = pl.BlockSpec(memory_space=pl.ANY)
    outs = pl.pallas_call(
        body, name="attn_fwd", grid=(nt + ATTN_LAG,),
        out_shape=tuple([jax.ShapeDtypeStruct((t_len, D_Z), F32), jax.ShapeDtypeStruct((t_len, D), BF16),
                         jax.ShapeDtypeStruct((t_len, D), F32), jax.ShapeDtypeStruct((t_len, D), F32),
                         jax.ShapeDtypeStruct((t_len, D), BF16),
                         jax.ShapeDtypeStruct((t_len, FC_EARLY * FF_BLK), BF16),
                         jax.ShapeDtypeStruct((t_len, D), F32)]
                        + [jax.ShapeDtypeStruct((FC_EARLY,) + s.shape, BF16) for s in fc_shards]
                        + [jax.ShapeDtypeStruct((D, D), BF16)]),
        in_specs=[first(D), second(D), _full((8, D)), _full((1, D)), _full((1, D)), _resident((D_Z, D)),
                  _resident(w_out_shard.shape),
                  _full((N_HEADS, CHUNK, CHUNK)), _full((CHUNK, D_A)), _full((1, D_A)), _full((1, D_A)),
                  _full((len(WINDOWS), GROUP, GROUP)), _full((1, D_B)), _full((1, D_B)),
                  _resident(fc_shards[0].shape), _resident(fc_shards[1].shape), _full((1, D))],
        out_specs=(first(D_Z), first(D), second(D), second(D), second(D), r_head, second(D), hbm, hbm, hbm),
        scratch_shapes=[pltpu.VMEM((HALO, D_B), F32),
                        pltpu.VMEM((2,) + fc_shards[0].shape, BF16), pltpu.VMEM((2,) + fc_shards[1].shape, BF16),
                        pltpu.VMEM(fc_shards[0].shape, BF16), pltpu.VMEM(fc_shards[1].shape, BF16),
                        pltpu.VMEM((ATTN_LAG + 1, tt, D), BF16), pltpu.VMEM((D, D), BF16),
                        pltpu.SemaphoreType.DMA((10,)), pltpu.SemaphoreType.DMA((10,)),
                        pltpu.SemaphoreType.DMA((9,)),
                        pltpu.SemaphoreType.DMA((7,)), pltpu.SemaphoreType.DMA((7,))],
        compiler_params=pltpu.CompilerParams(dimension_semantics=("arbitrary",), vmem_limit_bytes=VMEM_LIMIT),
    )(x, x, mod, n1pre, n1post, w_in_t, w_out_shard, w_sp, bs_rows, ln_g, ln_b, w_pool, b_pool, pool_scale,
      *fc_shards, n2pre)
    return outs[:7], outs[7:9], outs[9]


def _mlp_fwd_early(tt, r_begun, h2, f_head, w1_early, w2_early):
    t_len = h2.shape[0]
    nt = t_len // tt
    n_late = N_DEV - FC_EARLY

    def body(r_begun_ref, h2_ref, fh_ref, w1_ref, w2_ref, r_ref, f_ref, l1_ref, l2_ref,
             land1, land2, send_sems, recv_sems, local_sems):
        i = pl.program_id(0)
        copies = _Copies(
            [(w1_ref.at[2], land1, 4), (w2_ref.at[4], land2, 2),
             (land1, l1_ref.at[1], 1), (land2, l2_ref.at[1], 1)],
            send_sems, recv_sems)
        keep = [pltpu.make_async_copy(land1, l1_ref.at[0], local_sems.at[0]),
                pltpu.make_async_copy(land2, l2_ref.at[0], local_sems.at[1])]

        @pl.when(i == 0)
        def _():
            copies.start(0, 1)

        @pl.when(i == nt - 1)
        def _():
            copies.wait_recv(0, 1)
            copies.start(2, 3)
            for cp in keep:
                cp.start()

        h2 = h2_ref[...]
        f_ref[...] = fh_ref[...]
        for j in range(FC_HEAD, FC_EARLY):
            ra = jnp.maximum(_dot(h2, w1_ref[j]), 0.0)
            r = (ra * ra).astype(BF16)
            r_ref[:, _early_col(j):_early_col(j) + FF_BLK] = r
            f_ref[...] += _dot(r, w2_ref[j])

        @pl.when(i == nt - 1)
        def _():
            copies.wait_recv(2, 3)
            copies.wait_send(0, 1, 2, 3)
            for cp in keep:
                cp.wait()

    tile = lambda w: pl.BlockSpec((tt, w), lambda i: (i, 0))
    hbm = pl.BlockSpec(memory_space=pl.ANY)
    outs = pl.pallas_call(
        body, name="mlp_fwd_early", grid=(nt,),
        out_shape=(jax.ShapeDtypeStruct((t_len, FC_EARLY * FF_BLK), BF16), jax.ShapeDtypeStruct((t_len, D), F32),
                   jax.ShapeDtypeStruct((n_late,) + w1_early.shape[1:], BF16),
                   jax.ShapeDtypeStruct((n_late,) + w2_early.shape[1:], BF16)),
        in_specs=[hbm, tile(D), tile(D), _resident((FC_EARLY, D, FF_BLK)), _resident((FC_EARLY, FF_BLK, D))],
        out_specs=(tile(R_HEAD_COLS), tile(D), hbm, hbm),
        input_output_aliases={0: 0},
        scratch_shapes=[pltpu.VMEM(w1_early.shape[1:], BF16), pltpu.VMEM(w2_early.shape[1:], BF16),
                        pltpu.SemaphoreType.DMA((4,)), pltpu.SemaphoreType.DMA((4,)),
                        pltpu.SemaphoreType.DMA((2,))],
        compiler_params=pltpu.CompilerParams(dimension_semantics=("arbitrary",), vmem_limit_bytes=VMEM_LIMIT),
    )(r_begun, h2, f_head, w1_early, w2_early)
    return outs[:2], outs[2:]


def _mlp_late_bwd(tt, r_early, x1, h2, f_early, tgt, mix, mod, n2pre, n2post, n1post,
                  w1_early, w2_early, w1_late, w2_late):
    t_len = x1.shape[0]
    nt = t_len // tt
    n_late = N_DEV - FC_EARLY
    late_cols = n_late * FF_BLK

    def body(re_ref, x1_ref, h2_ref, fe_ref, tgt_ref, mix_ref, mod_ref, n2pre_ref, n2post_ref,
             n1post_ref, w1e_ref, w2e_ref, w1l_ref, w2l_ref,
             rl_ref, df_ref, da_ref, dmix_ref, dx1_ref, redf_ref, redb_ref, dh2_acc):
        i = pl.program_id(0)

        @pl.when(i == 0)
        def _():
            redf_ref[...] = jnp.zeros_like(redf_ref)
            redb_ref[...] = jnp.zeros_like(redb_ref)

        x1v = x1_ref[...]
        gate1, scale2, gate2 = mod_ref[2:3, :], mod_ref[4:5, :], mod_ref[5:6, :]
        h2 = h2_ref[...]
        f = fe_ref[...]
        for j in range(n_late):
            cols = slice(j * FF_BLK, (j + 1) * FF_BLK)
            ra = jnp.maximum(_dot(h2, w1l_ref[j]), 0.0)
            r = (ra * ra).astype(BF16)
            rl_ref[:, cols] = r
            f = f + _dot(r, w2l_ref[j])
        post2 = n2post_ref[...]
        gate_post2 = gate2 * post2
        rf = _rstd(f)
        fhat = f * rf
        err = (x1v + fhat * gate_post2) - tgt_ref[...]
        dy = err * (1.0 / D)
        d_f, sum_f = _rms_bwd_gained(dy, gate_post2, fhat, rf)
        dfv = d_f.astype(BF16)
        df_ref[...] = dfv
        redf_ref[0:1, :] += post2 * sum_f
        redf_ref[1:2, :] += gate2 * sum_f
        redf_ref[2:3, :] += _colsum(err * err)

        for j in range(N_DEV):
            cols = slice(j * FF_BLK, (j + 1) * FF_BLK)
            if j < FC_EARLY:
                w1, w2, r = w1e_ref[j], w2e_ref[j], re_ref[:, _early_col(j):_early_col(j) + FF_BLK]
            else:
                jl = j - FC_EARLY
                w1, w2, r = w1l_ref[jl], w2l_ref[jl], rl_ref[:, jl * FF_BLK:(jl + 1) * FF_BLK]
            dr = _dot_nt(dfv, w2)
            da = (dr * (2.0 * jnp.sqrt(r.astype(F32)))).astype(BF16)
            da_ref[:, cols] = da
            contrib = _dot_nt(da, w1)
            if j == 0:
                dh2_acc[...] = contrib
            else:
                dh2_acc[...] += contrib
        dh2 = dh2_acc[...]
        pre2, post1 = n2pre_ref[...], n1post_ref[...]
        r2 = _rstd(x1v)
        xhat = x1v * r2
        d_x1, sum_h = _rms_bwd_gained(dh2, pre2 * (1.0 + scale2), xhat, r2)
        dx1 = dy + d_x1
        dx1_ref[...] = dx1
        mixv = mix_ref[...]
        rm = _rstd(mixv)
        mhat = mixv * rm
        d_mix, sum_m = _rms_bwd_gained(dx1, gate1 * post1, mhat, rm)
        dmix_ref[...] = d_mix.astype(BF16)
        redb_ref[0:1, :] += _colsum(dh2)
        redb_ref[1:2, :] += pre2 * sum_h
        redb_ref[2:3, :] += (1.0 + scale2) * sum_h
        redb_ref[3:4, :] += post1 * sum_m
        redb_ref[4:5, :] += gate1 * sum_m

    tile = lambda w: pl.BlockSpec((tt, w), lambda i: (i, 0))
    return pl.pallas_call(
        body, name="mlp_late_bwd", grid=(nt,),
        out_shape=(jax.ShapeDtypeStruct((t_len, late_cols), BF16), jax.ShapeDtypeStruct((t_len, D), BF16),
                   jax.ShapeDtypeStruct((t_len, D_FF), BF16), jax.ShapeDtypeStruct((t_len, D), BF16),
                   jax.ShapeDtypeStruct((t_len, D), F32), jax.ShapeDtypeStruct((8, D), F32),
                   jax.ShapeDtypeStruct((8, D), F32)),
        in_specs=[tile(FC_EARLY * FF_BLK), tile(D), tile(D), tile(D), tile(D),
                  tile(D), _full((8, D)), _full((1, D)), _full((1, D)), _full((1, D)),
                  _resident((FC_EARLY, D, FF_BLK)), _resident((FC_EARLY, FF_BLK, D)),
                  _resident((n_late, D, FF_BLK)), _resident((n_late, FF_BLK, D))],
        out_specs=(tile(late_cols), tile(D), tile(D_FF), tile(D), tile(D), _full((8, D)), _full((8, D))),
        scratch_shapes=[pltpu.VMEM((tt, D), F32)],
        compiler_params=pltpu.CompilerParams(dimension_semantics=("arbitrary",), vmem_limit_bytes=VMEM_LIMIT),
    )(r_early, x1, h2, f_early, tgt, mix, mod, n2pre, n2post, n1post, w1_early, w2_early, w1_late, w2_late)


def _mlp_wgrad(tt, r_early, r_late, da, df, h2):
    t_len = df.shape[0]
    nt = t_len // tt
    odd_steps = [j for j, rel in enumerate(WGRAD_ORDER) if rel % 2]

    def relation(j):
        rel = jnp.int32(WGRAD_ORDER[-1])
        for step in range(N_DEV - 2, -1, -1):
            rel = jnp.where(j == step, WGRAD_ORDER[step], rel)
        return rel

    def body(re_ref, rl_ref, da_ref, df_ref, h2_ref, own1_ref, own2_ref, out1_ref, out2_ref, diag1_ref, diag2_ref,
             acc1, acc2, snd1, snd2, sib1, sib2, dsnd1, dsnd2, send_sems, recv_sems):
        j, t = pl.program_id(0), pl.program_id(1)
        rows = pl.ds(pl.multiple_of(t * tt, tt), tt)
        x, y, c = _place()
        accs, snds, sibs = (acc1, acc2), (snd1, snd2), (sib1, sib2)
        dsnds, diags = (dsnd1, dsnd2), (diag1_ref, diag2_ref)

        def to_sibling(a, jj, buf=0):
            return pltpu.make_async_remote_copy(
                src_ref=snds[a].at[buf], dst_ref=sibs[a].at[jj],
                send_sem=send_sems.at[4 * a + jj], recv_sem=recv_sems.at[4 * a + jj],
                device_id=(x, y, 1 - c), device_id_type=MESH)

        def to_diagonal(a):
            return pltpu.make_async_remote_copy(
                src_ref=dsnds[a], dst_ref=diags[a], send_sem=send_sems.at[8 + a], recv_sem=recv_sems.at[8 + a],
                device_id=_peer(x, y, c, 6), device_id_type=MESH)

        @pl.when(t == 0)
        def _():
            acc2[...] = jnp.zeros_like(acc2)
            acc1[...] = jnp.zeros_like(acc1)

        for r_ref, mine in ((re_ref, relation(j) < FC_EARLY), (rl_ref, relation(j) >= FC_EARLY)):
            @pl.when(mine)
            def _():
                acc2[...] += _dot_tn(r_ref[...], df_ref[rows, :])
                acc1[...] += _dot_tn(h2_ref[rows, :], da_ref[...])

        for step, rel in enumerate(WGRAD_ORDER):
            jj = rel // 2

            @pl.when((t == nt - 1) & (j == step))
            def _():
                for a, (own_ref, out_ref) in enumerate(((own1_ref, out1_ref), (own2_ref, out2_ref))):
                    if rel % 2:
                        q = odd_steps.index(step)
                        if q >= 2:
                            to_sibling(a, WGRAD_ORDER[odd_steps[q - 2]] // 2).wait_send()
                        snds[a][q % 2] = accs[a][...].astype(BF16)
                        to_sibling(a, jj, q % 2).start()
                        continue
                    to_sibling(a, jj).wait_recv()
                    chip_sum = accs[a][...] + sibs[a][jj].astype(F32)
                    if rel == 6:
                        dsnds[a][...] = chip_sum.astype(BF16)
                        to_diagonal(a).start()
                    elif rel == 0:
                        own_ref[...] = chip_sum
                    else:
                        out_ref[0] = chip_sum.astype(BF16)
                    if step == N_DEV - 1:
                        for q in (2, 3):
                            to_sibling(a, WGRAD_ORDER[odd_steps[q]] // 2).wait_send()
                        to_diagonal(a).wait_recv()
                        to_diagonal(a).wait_send()

    assert WGRAD_ORDER[-1] == 0 and WGRAD_ORDER[-3:-1] == (2, 4)
    blk = pl.BlockSpec((tt, FF_BLK), lambda j, t: (t, relation(j)))
    early_block = lambda rel: jnp.where(rel < FC_HEAD, rel + FC_EARLY - FC_HEAD, rel - FC_HEAD)
    early = lambda j, t: (jnp.where(relation(j) < FC_EARLY, t, 0),
                          jnp.where(relation(j) < FC_EARLY, early_block(relation(j)), 0))
    late = lambda j, t: (jnp.where(relation(j) < FC_EARLY, 0, t), jnp.maximum(relation(j) - FC_EARLY, 0))
    chip = lambda j, t: (jnp.clip(j - 5, 0, 1), 0, 0)
    hbm = pl.BlockSpec(memory_space=pl.ANY)
    return pl.pallas_call(
        body, name="mlp_wgrad", grid=(N_DEV, nt),
        out_shape=(jax.ShapeDtypeStruct((D, FF_BLK), F32), jax.ShapeDtypeStruct((FF_BLK, D), F32),
                   jax.ShapeDtypeStruct((2, D, FF_BLK), BF16), jax.ShapeDtypeStruct((2, FF_BLK, D), BF16),
                   jax.ShapeDtypeStruct((D, FF_BLK), BF16), jax.ShapeDtypeStruct((FF_BLK, D), BF16)),
        in_specs=[pl.BlockSpec((tt, FF_BLK), early), pl.BlockSpec((tt, FF_BLK), late), blk,
                  _resident((t_len, D)), _resident((t_len, D))],
        out_specs=(_full((D, FF_BLK)), _full((FF_BLK, D)),
                   pl.BlockSpec((1, D, FF_BLK), chip), pl.BlockSpec((1, FF_BLK, D), chip), hbm, hbm),
        scratch_shapes=[pltpu.VMEM((D, FF_BLK), F32), pltpu.VMEM((FF_BLK, D), F32),
                        pltpu.VMEM((2, D, FF_BLK), BF16), pltpu.VMEM((2, FF_BLK, D), BF16),
                        pltpu.VMEM((4, D, FF_BLK), BF16), pltpu.VMEM((4, FF_BLK, D), BF16),
                        pltpu.VMEM((D, FF_BLK), BF16), pltpu.VMEM((FF_BLK, D), BF16),
                        pltpu.SemaphoreType.DMA((10,)), pltpu.SemaphoreType.DMA((10,))],
        compiler_params=pltpu.CompilerParams(dimension_semantics=("arbitrary", "arbitrary"),
                                             vmem_limit_bytes=VMEM_LIMIT),
    )(r_early, r_late, da, df, h2)


def _acc_rows(ref, row0, k, val):
    half = CHUNK // 2
    ref[row0:row0 + half, k * GROUP:(k + 1) * GROUP] += val[:half, :]
    ref[row0:row0 + half, D_A + k * GROUP:D_A + (k + 1) * GROUP] += val[half:, :]


def _attn_bwd(tt, dmix, dx1, x, z, cat, mod, n1pre, w_in_t, w_out, w_sp, bs_rows, ln_g, ln_b, w_pool, b_pool,
              pool_scale, red_fwd, red_bwd, chip_sums):
    t_len = x.shape[0]
    nt = t_len // tt
    hb = tt // HALO
    n_sums = len(chip_sums)

    def body(dmix_ref, dx1_ref, x_ref, z_ref, zprev_ref, cat_ref, mod_ref, n1pre_ref, win_ref, wout_ref, wsp_ref,
             bs_ref, lng_ref, lnb_ref, wp_ref, bp_ref, ps_ref, redf_ref, redb_ref, *rest):
        sum_out = rest[:n_sums]
        gx_ref, gwin_ref, gwout_ref, small_ref = rest[n_sums:n_sums + 4]
        sum_in = rest[n_sums + 4:2 * n_sums + 4]
        carry, acc_in, acc_out, dz_scr, bs_acc, send_sems, recv_sems = rest[2 * n_sums + 4:]
        s = pl.program_id(0)
        i = nt - 1 - s
        px, py, pc = _place()

        def chip_copy(a, r):
            return pltpu.make_async_remote_copy(
                src_ref=sum_out[a].at[r], dst_ref=sum_in[a].at[r],
                send_sem=send_sems.at[2 * a + r], recv_sem=recv_sems.at[2 * a + r],
                device_id=_peer(px, py, pc, 2 * (r + 1)), device_id_type=MESH)

        @pl.when(s == 0)
        def _():
            for a in range(n_sums):
                for r in range(2):
                    chip_copy(a, r).start()
            carry[...] = jnp.zeros_like(carry)
            acc_in[...] = jnp.zeros_like(acc_in)
            acc_out[...] = jnp.zeros_like(acc_out)
            bs_acc[...] = jnp.zeros_like(bs_acc)
            small_ref[...] = jnp.zeros_like(small_ref)
            small_ref[ROW_DMOD + 2:ROW_DMOD + 3, :] = redb_ref[3:4, :]
            small_ref[ROW_DMOD + 3:ROW_DMOD + 5, :] = redb_ref[0:2, :]
            small_ref[ROW_DMOD + 5:ROW_DMOD + 6, :] = redf_ref[0:1, :]
            small_ref[ROW_N1POST:ROW_N1POST + 1, :] = redb_ref[4:5, :]
            small_ref[ROW_N2PRE:ROW_N2PRE + 1, :] = redb_ref[2:3, :]
            small_ref[ROW_N2POST:ROW_N2POST + 1, :] = redf_ref[1:2, :]
            small_ref[ROW_LOSS:ROW_LOSS + 1, :] = redf_ref[2:3, :]

        dmixv = dmix_ref[...]
        dcat = _dot_nt(dmixv, wout_ref[...])
        acc_out[...] += _dot_tn(cat_ref[...], dmixv)

        z = z_ref[...]
        t_g, ga = _gelu_parts(z[:, :2 * D_A])
        u, vr = ga[:, :D_A], ga[:, D_A:]
        dv0 = vr - jnp.mean(vr, axis=-1, keepdims=True)
        rv = lax.rsqrt(jnp.mean(dv0 * dv0, axis=-1, keepdims=True) + EPS)
        vhat = dv0 * rv
        vb = (vhat * lng_ref[...] + lnb_ref[...]).astype(BF16)
        mask = _tril_mask()
        wc = [(wsp_ref[h] * mask).astype(BF16) for h in range(N_HEADS)]

        dya = dcat[:, :D_A]
        for h in range(N_HEADS):
            cols = slice(h * GROUP, (h + 1) * GROUP)
            bs_sum = jnp.zeros((CHUNK, GROUP), F32)
            ws_sum = jnp.zeros((CHUNK, CHUNK), F32)
            for ch in range(tt // CHUNK):
                rows = slice(ch * CHUNK, (ch + 1) * CHUNK)
                v_ch = vb[rows, cols]
                mixed = _dot(wc[h], v_ch) + bs_ref[:, cols]
                dy_ch = dya[rows, cols]
                dz_scr[rows, cols] = dy_ch * mixed
                dmixed = dy_ch * u[rows, cols]
                dmb = dmixed.astype(BF16)
                dz_scr[rows, D_A + h * GROUP:D_A + (h + 1) * GROUP] = _dot_tn(wc[h], dmb)
                bs_sum = bs_sum + dmixed
                ws_sum = ws_sum + _dot_nt(dmb, v_ch)
            _acc_rows(bs_acc, 0, h, bs_sum)
            _acc_rows(small_ref, ROW_WS, h, ws_sum)

        dvl = dz_scr[:, D_A:2 * D_A]
        dvhat = dvl * lng_ref[...]
        dvl_vhat = dvl * vhat
        dvr = rv * (dvhat - jnp.mean(dvhat, axis=-1, keepdims=True)
                    - vhat * jnp.mean(dvl_vhat * lng_ref[...], axis=-1, keepdims=True))
        small_ref[ROW_LN:ROW_LN + 1, 0:D_A] += _colsum(dvl_vhat)
        small_ref[ROW_LN:ROW_LN + 1, D_A:D] += _colsum(dvl)
        dga = jnp.concatenate([dz_scr[:, :D_A], dvr], axis=1)
        dza = dga * _gelu_grad(z[:, :2 * D_A], t_g)

        zb = z[:, 2 * D_A:]
        halo_prev = jnp.where(i == 0, 0.0, zprev_ref[...])
        diff = _pool_diff(zb, halo_prev, i * tt)
        dyb = dcat[:, D_A:]
        inv = _inv_counts(i * tt, tt)
        scaled, ddiffs = [], []
        for g in range(len(WINDOWS)):
            cols = slice(g * GROUP, (g + 1) * GROUP)
            db = diff[g].astype(BF16)
            wpg = wp_ref[g].astype(BF16)
            pre = _dot(db, wpg) + bp_ref[:, cols]
            small_ref[ROW_POOL:ROW_POOL + 1, cols] += _colsum(dyb[:, cols] * pre)
            dpre = dyb[:, cols] * ps_ref[:, cols]
            small_ref[ROW_POOL:ROW_POOL + 1, D_B + g * GROUP:D_B + (g + 1) * GROUP] += _colsum(dpre)
            dpb = dpre.astype(BF16)
            _acc_rows(small_ref, ROW_WP, g, _dot_tn(db, dpb))
            ddiff = _dot_nt(dpb, wpg)
            ddiffs.append(ddiff)
            scaled.append(ddiff * inv[g])
        scaled_all = jnp.concatenate(scaled, axis=1)
        ext = jnp.concatenate([scaled_all, carry[...]], axis=0)
        n_ext = tt + HALO
        s2 = ext + pltpu.roll(ext, n_ext - 1, 0)
        t4 = s2[:, GROUP:]
        s4 = t4 + pltpu.roll(t4, n_ext - 2, 0)
        t8 = s4[:, GROUP:]
        s8 = t8 + pltpu.roll(t8, n_ext - 4, 0)
        t16 = s8[:, GROUP:]
        s16 = t16 + pltpu.roll(t16, n_ext - 8, 0)
        back = [s2[:, :GROUP], s4[:, :GROUP], s8[:, :GROUP], s16]
        carry[...] = scaled_all[:HALO, :]
        dzb = jnp.concatenate([back[g][:tt, :] - ddiffs[g] for g in range(len(WINDOWS))], axis=1)

        dzv = jnp.concatenate([dza, dzb], axis=1).astype(BF16)
        dh1 = _dot(dzv, win_ref[...])
        xv = x_ref[...]
        r1 = _rstd(xv)
        xhat = xv * r1
        shift1, scale1 = mod_ref[0:1, :], mod_ref[1:2, :]
        pre1 = n1pre_ref[...]
        gain1 = pre1 * (1.0 + scale1)
        h1 = (xhat * gain1 + shift1).astype(BF16)
        acc_in[...] += _dot_tn(dzv, h1)
        d_x, sum_h = _rms_bwd_gained(dh1, gain1, xhat, r1)
        gx_ref[...] = dx1_ref[...] + d_x
        small_ref[ROW_DMOD:ROW_DMOD + 1, :] += _colsum(dh1)
        small_ref[ROW_DMOD + 1:ROW_DMOD + 2, :] += pre1 * sum_h
        small_ref[ROW_N1PRE:ROW_N1PRE + 1, :] += (1.0 + scale1) * sum_h

        @pl.when(s == nt - 1)
        def _():
            gwin_ref[...] = acc_in[...].astype(BF16)
            gwout_ref[...] = acc_out[...].astype(BF16)
            bs = _unfold(bs_acc[...])
            for h in range(N_HEADS):
                small_ref[ROW_BS + h:ROW_BS + h + 1, 0:GROUP] = jnp.sum(
                    bs[:, h * GROUP:(h + 1) * GROUP].T, axis=0, keepdims=True)
            for a in range(n_sums):
                for r in range(2):
                    chip_copy(a, r).wait_recv()
                    chip_copy(a, r).wait_send()

    rev = lambda w: pl.BlockSpec((tt, w), lambda s: (nt - 1 - s, 0))
    zprev = pl.BlockSpec((HALO, D_B), lambda s: (jnp.maximum((nt - 1 - s) * hb - 1, 0), 2))
    hbm = pl.BlockSpec(memory_space=pl.ANY)
    outs = pl.pallas_call(
        body, name="attn_bwd", grid=(nt,),
        out_shape=tuple([jax.ShapeDtypeStruct((t_len, D), F32), jax.ShapeDtypeStruct((D_Z, D), BF16),
                         jax.ShapeDtypeStruct((D, D), BF16), jax.ShapeDtypeStruct((SMALL_ROWS, D), F32)]
                        + [jax.ShapeDtypeStruct(cs.shape, cs.dtype) for cs in chip_sums]),
        in_specs=[rev(D), rev(D), rev(D), rev(D_Z), zprev, rev(D), _full((8, D)), _full((1, D)),
                  _resident((D_Z, D)), _resident((D, D)), _full((N_HEADS, CHUNK, CHUNK)), _full((CHUNK, D_A)),
                  _full((1, D_A)), _full((1, D_A)), _full((len(WINDOWS), GROUP, GROUP)), _full((1, D_B)),
                  _full((1, D_B)), _full((8, D)), _full((8, D))] + [_resident(cs.shape) for cs in chip_sums],
        out_specs=tuple([rev(D), _resident((D_Z, D)), _resident((D, D)), _full((SMALL_ROWS, D))] + [hbm] * n_sums),
        scratch_shapes=[pltpu.VMEM((HALO, D_B), F32), pltpu.VMEM((D_Z, D), F32), pltpu.VMEM((D, D), F32),
                        pltpu.VMEM((tt, 2 * D_A), F32), pltpu.VMEM((CHUNK // 2, D), F32),
                        pltpu.SemaphoreType.DMA((2 * n_sums,)), pltpu.SemaphoreType.DMA((2 * n_sums,))],
        compiler_params=pltpu.CompilerParams(dimension_semantics=("arbitrary",), vmem_limit_bytes=VMEM_LIMIT),
    )(dmix, dx1, x, z, z, cat, mod, n1pre, w_in_t, w_out, w_sp, bs_rows, ln_g, ln_b, w_pool, b_pool, pool_scale,
      red_fwd, red_bwd, *chip_sums)
    return outs[:4], outs[4:]


def _adam(w, g, m, v):
    m2 = ADAM_B1 * m + (1.0 - ADAM_B1) * g
    v2 = ADAM_B2 * v + (1.0 - ADAM_B2) * (g * g)
    m_hat = m2 / (1.0 - ADAM_B1 ** ADAM_STEP)
    v_hat = v2 / (1.0 - ADAM_B2 ** ADAM_STEP)
    delta = -ADAM_LR * (m_hat / (jnp.sqrt(v_hat) + ADAM_EPS) + ADAM_WD * w)
    return delta, m2, v2


def _adamw_shard(name, rb, w, g, m, v):
    rows, cols = w.shape

    def body(w_ref, g_ref, m_ref, v_ref, d_ref, m2_ref, v2_ref):
        d_ref[...], m2_ref[...], v2_ref[...] = _adam(w_ref[...], g_ref[...], m_ref[...], v_ref[...])

    blk = pl.BlockSpec((rb, cols), lambda i: (i, 0))
    shp = jax.ShapeDtypeStruct((rows, cols), F32)
    return pl.pallas_call(
        body, name=name, grid=(rows // rb,), out_shape=(shp, shp, shp),
        in_specs=[blk] * 4, out_specs=(blk, blk, blk),
        compiler_params=pltpu.CompilerParams(dimension_semantics=("arbitrary",)),
    )(w, g, m, v)


def _adamw_ada(rb, w, sc, dmod_cols, m, v):
    rows, cols = w.shape

    def body(w_ref, sc_ref, dm_ref, m_ref, v_ref, g_ref, d_ref, m2_ref, v2_ref):
        g = _dot_tn(sc_ref[...].astype(BF16), dm_ref[...].astype(BF16))
        g_ref[...] = g
        d_ref[...], m2_ref[...], v2_ref[...] = _adam(w_ref[...], g, m_ref[...], v_ref[...])

    blk = pl.BlockSpec((rb, cols), lambda i: (i, 0))
    shp = jax.ShapeDtypeStruct((rows, cols), F32)
    return pl.pallas_call(
        body, name="adamw_ada", grid=(rows // rb,), out_shape=(shp, shp, shp, shp),
        in_specs=[blk, pl.BlockSpec((N_DEV, rb), lambda i: (0, i)), _full((N_DEV, cols)), blk, blk],
        out_specs=(blk, blk, blk, blk),
        compiler_params=pltpu.CompilerParams(dimension_semantics=("arbitrary",)),
    )(w, sc, dmod_cols, m, v)


def _unfold(acc_rows):
    return jnp.concatenate([acc_rows[:, :D_A], acc_rows[:, D_A:]], axis=0)


def _adamw_small(total, params):
    n = len(params)
    flat = [a for p in params for a in p]

    def body(*refs):
        s_ref = refs[0]
        p_refs = refs[1:1 + 3 * n]
        loss_ref = refs[1 + 3 * n]
        o_refs = refs[2 + 3 * n:]
        d_b_ada = s_ref[0:6, :]
        for b in range(1, N_DEV):
            d_b_ada = d_b_ada + s_ref[_table_row(b):_table_row(b) + 6, :]
        misc = lambda r: s_ref[PK_MISC + r - ROW_N1PRE:PK_MISC + r - ROW_N1PRE + 1, :]
        loss = jnp.sum(misc(ROW_LOSS), axis=-1, keepdims=True) * (0.5 / D)
        loss_ref[...] = jnp.broadcast_to(loss, (8, GROUP))
        mask = _tril_mask()
        ws = _unfold(s_ref[PK_WS:PK_WS + 64, :])
        wp = _unfold(s_ref[PK_WP:PK_WP + 64, :])
        grads = [
            d_b_ada,
            misc(ROW_N1PRE), misc(ROW_N1POST), misc(ROW_N2PRE), misc(ROW_N2POST),
            misc(ROW_LN)[:, :D_A], misc(ROW_LN)[:, D_A:],
            misc(ROW_POOL)[:, :D_B], misc(ROW_POOL)[:, D_B:],
            s_ref[PK_BS:PK_BS + N_HEADS, 0:GROUP],
            jnp.stack([ws[:, h * GROUP:(h + 1) * GROUP] * mask for h in range(N_HEADS)]),
            jnp.stack([wp[:, g * GROUP:(g + 1) * GROUP] for g in range(len(WINDOWS))]),
        ]
        for k in range(n):
            w_ref, m_ref, v_ref = p_refs[3 * k:3 * k + 3]
            g = grads[k]
            o_refs[4 * k][...] = g
            o_refs[4 * k + 1][...], o_refs[4 * k + 2][...], o_refs[4 * k + 3][...] = _adam(
                w_ref[...], g, m_ref[...], v_ref[...])

    vm = pl.BlockSpec(memory_space=pltpu.VMEM)
    out_shape = [jax.ShapeDtypeStruct((8, GROUP), F32)]
    for w, _, _ in params:
        out_shape += [jax.ShapeDtypeStruct(w.shape, F32)] * 4
    return pl.pallas_call(
        body, name="adamw_small", out_shape=tuple(out_shape),
        in_specs=[vm] * (1 + 3 * n), out_specs=tuple([vm] * len(out_shape)),
    )(total, *flat)


TT_ATTN_FWD = 512
ATTN_LAG = 2
TT_MLP_FWD = 512
TT_MLP = 256
TT_WGRAD = 2048
TT_ATTN_BWD = 512


def kernel(x, c, w_ada, b_ada, norm1_pre, norm1_post, w_in, w_spatial, b_spatial, ln_v_gain, ln_v_bias, w_pool, b_pool, pool_scale, w_out, norm2_pre, norm2_post, w_fc1, w_fc2, loss_target, m_w_ada, m_b_ada, m_norm1_pre, m_norm1_post, m_w_in, m_w_spatial, m_b_spatial, m_ln_v_gain, m_ln_v_bias, m_w_pool, m_b_pool, m_pool_scale, m_w_out, m_norm2_pre, m_norm2_post, m_w_fc1, m_w_fc2, v_w_ada, v_b_ada, v_norm1_pre, v_norm1_post, v_w_in, v_w_spatial, v_b_spatial, v_ln_v_gain, v_ln_v_bias, v_w_pool, v_b_pool, v_pool_scale, v_w_out, v_norm2_pre, v_norm2_post, v_w_fc1, v_w_fc2):
    t_len = x.shape[1]
    me = 4 * lax.axis_index("x") + 2 * lax.axis_index("y") + lax.axis_index("c")
    ada_cols = w_ada.shape[1]
    tt = lambda want: min(want, t_len)

    x2 = x.reshape(t_len, D)
    tgt = loss_target.reshape(t_len, D)
    row = lambda a: a.reshape(1, -1)

    b_my = lax.dynamic_slice_in_dim(b_ada, me * ada_cols, ada_cols).reshape(1, ada_cols)
    modp, sc, (g_in,), (w1_shard, w2_shard, w_out_shard) = _fwd_comm(
        jnp.broadcast_to(c, (8, D)), w_ada, b_my, [w_in.T], [w_fc1, w_fc2, w_out])
    fc_shards = (w1_shard, w2_shard)
    mod = jnp.concatenate([modp.reshape(6, D), jnp.zeros((2, D), F32)], axis=0)
    w_in_t = g_in.reshape(D_Z, D)

    bs_rows = jnp.repeat(b_spatial.T, GROUP, axis=1)
    attn_consts = (w_spatial, bs_rows, row(ln_v_gain), row(ln_v_bias), w_pool, row(b_pool), row(pool_scale))

    (z, cat, mix, x1, h2, r_begun, f_head), (w1_early, w2_early), w_out_all = _attn_fwd(
        tt(TT_ATTN_FWD), x2, mod, row(norm1_pre), row(norm1_post), w_in_t, w_out_shard, *attn_consts, fc_shards,
        row(norm2_pre))
    (r_early, f_early), (w1_late, w2_late) = _mlp_fwd_early(
        tt(TT_MLP_FWD), r_begun, h2, f_head, w1_early, w2_early)
    r_late, df, da, dmix, dx1, red_fwd, red_bwd = _mlp_late_bwd(
        tt(TT_MLP), r_early, x1, h2, f_early, tgt, mix, mod, row(norm2_pre), row(norm2_post), row(norm1_post),
        w1_early, w2_early, w1_late, w2_late)
    own_w1, own_w2, sums_w1, sums_w2, diag_w1, diag_w2 = _mlp_wgrad(tt(TT_WGRAD), r_early, r_late, da, df, h2)
    (grad_x, p_in, p_out, small), (arr_w1, arr_w2) = _attn_bwd(
        tt(TT_ATTN_BWD), dmix, dx1, x2, z, cat, mod, row(norm1_pre), w_in_t, w_out_all, *attn_consts,
        red_fwd, red_bwd, [sums_w1, sums_w2])
    (grad_in_t, grad_out, total), ((grad_w1, d_w1, m_w1, v_w1), (grad_w2, d_w2, m_w2, v_w2)) = _tail_comm(
        [p_in.reshape(N_DEV, D_Z // N_DEV, D), p_out.reshape(N_DEV, D // N_DEV, D)], small, 64,
        [(w_fc1, own_w1, arr_w1, diag_w1, m_w_fc1, v_w_fc1), (w_fc2, own_w2, arr_w2, diag_w2, m_w_fc2, v_w_fc2)])

    d_out, m_out, v_out = _adamw_shard("adamw_out", 128, w_out, grad_out, m_w_out, v_w_out)
    d_in_t, m_in_t, v_in_t = _adamw_shard("adamw_in", D_Z // N_DEV, w_in.T, grad_in_t, m_w_in.T, v_w_in.T)
    table = jnp.concatenate([total[0:PACK_FINE, :], total[PK_TABLE_B:PK_MISC, :]], axis=0)
    dmod_all = table.reshape(N_DEV, 8, D)[:, :6, :].reshape(N_DEV, 6 * D)
    dmod_cols = lax.dynamic_slice_in_dim(dmod_all, me * ada_cols, ada_cols, axis=1)
    grad_ada, d_ada, m_ada, v_ada = _adamw_ada(256, w_ada, sc, dmod_cols, m_w_ada, v_w_ada)

    six = lambda a: a.reshape(6, D)
    small_params = [
        (six(b_ada), six(m_b_ada), six(v_b_ada)),
        (row(norm1_pre), row(m_norm1_pre), row(v_norm1_pre)),
        (row(norm1_post), row(m_norm1_post), row(v_norm1_post)),
        (row(norm2_pre), row(m_norm2_pre), row(v_norm2_pre)),
        (row(norm2_post), row(m_norm2_post), row(v_norm2_post)),
        (row(ln_v_gain), row(m_ln_v_gain), row(v_ln_v_gain)),
        (row(ln_v_bias), row(m_ln_v_bias), row(v_ln_v_bias)),
        (row(pool_scale), row(m_pool_scale), row(v_pool_scale)),
        (row(b_pool), row(m_b_pool), row(v_b_pool)),
        (b_spatial, m_b_spatial, v_b_spatial),
        (w_spatial, m_w_spatial, v_w_spatial),
        (w_pool, m_w_pool, v_w_pool),
    ]
    outs = _adamw_small(total, small_params)
    loss = outs[0][0, 0]
    names = ["b_ada", "norm1_pre", "norm1_post", "norm2_pre", "norm2_post", "ln_v_gain", "ln_v_bias", "pool_scale",
             "b_pool", "b_spatial", "w_spatial", "w_pool"]
    shapes = dict(b_ada=b_ada.shape, norm1_pre=norm1_pre.shape, norm1_post=norm1_post.shape,
                  norm2_pre=norm2_pre.shape, norm2_post=norm2_post.shape, ln_v_gain=ln_v_gain.shape,
                  ln_v_bias=ln_v_bias.shape, pool_scale=pool_scale.shape, b_pool=b_pool.shape,
                  b_spatial=b_spatial.shape, w_spatial=w_spatial.shape, w_pool=w_pool.shape)
    res = {}
    for k, nm in enumerate(names):
        res[nm] = tuple(o.reshape(shapes[nm]) for o in outs[1 + 4 * k:5 + 4 * k])
    res["w_ada"] = (grad_ada, d_ada, m_ada, v_ada)
    res["w_in"] = (grad_in_t.T, d_in_t.T, m_in_t.T, v_in_t.T)
    res["w_out"] = (grad_out, d_out, m_out, v_out)
    res["w_fc1"] = (grad_w1, d_w1, m_w1, v_w1)
    res["w_fc2"] = (grad_w2, d_w2, m_w2, v_w2)

    order = ["w_ada", "b_ada", "norm1_pre", "norm1_post", "w_in", "w_spatial", "b_spatial", "ln_v_gain", "ln_v_bias",
             "w_pool", "b_pool", "pool_scale", "w_out", "norm2_pre", "norm2_post", "w_fc1", "w_fc2"]
    return (loss, grad_x.reshape(x.shape),
            *[res[nm][0] for nm in order], *[res[nm][1] for nm in order],
            *[res[nm][2] for nm in order], *[res[nm][3] for nm in order])
```

```python
import functools

import jax
import jax.numpy as jnp
from jax import lax
from jax.experimental import pallas as pl
from jax.experimental.pallas import tpu as pltpu

F32 = jnp.float32
BF16 = jnp.bfloat16
MESH = pl.DeviceIdType.MESH

N_DEV = 8
D = 1024
D_A = 512
D_B = 512
D_Z = 2 * D_A + D_B
N_HEADS = 4
CHUNK = 128
WINDOWS = (2, 4, 8, 16)
GROUP = 128
D_FF = 4096
FF_BLK = D_FF // N_DEV
HALO = 16
EPS = 1e-6
VMEM_LIMIT = 60 * 1024 * 1024

ADAM_LR = 0.001
ADAM_B1 = 0.9
ADAM_B2 = 0.999
ADAM_EPS = 1e-08
ADAM_WD = 0.01
ADAM_STEP = 10

ROW_DMOD = 0
ROW_N1PRE, ROW_N1POST, ROW_N2PRE, ROW_N2POST = 8, 9, 10, 11
ROW_LN = 12
ROW_POOL = 13
ROW_LOSS = 14
ROW_BS = 16
ROW_WS = 24
ROW_WP = 88
SMALL_ROWS = 152
PACK_FINE = 40
PACK_HALF = PACK_FINE + 64
PACK_ROWS = 2 * PACK_HALF
PK_WS = PACK_FINE
PK_TABLE_B = PACK_HALF
PK_MISC = PK_TABLE_B + 24
PK_BS = PK_MISC + 8
PK_WP = PK_BS + 8


def _table_row(b):
    if isinstance(b, int):
        return 8 * b if 8 * b < PACK_FINE else 8 * b + PK_TABLE_B - PACK_FINE
    return 8 * b + jnp.where(8 * b < PACK_FINE, 0, PK_TABLE_B - PACK_FINE)


def _dot(a, b):
    return jnp.dot(a, b, preferred_element_type=F32)


def _dot_nt(a, b):
    return lax.dot_general(a, b, (((1,), (1,)), ((), ())), preferred_element_type=F32)


def _dot_tn(a, b):
    return lax.dot_general(a, b, (((0,), (0,)), ((), ())), preferred_element_type=F32)


def _rstd(v):
    return lax.rsqrt(jnp.mean(v * v, axis=-1, keepdims=True) + EPS)


def _rms_bwd(d_hat, hat, rstd):
    return rstd * (d_hat - hat * jnp.mean(d_hat * hat, axis=-1, keepdims=True))


def _rms_bwd_gained(g, gain, hat, rstd):
    g_hat = g * hat
    d_v = rstd * (g * gain - hat * jnp.mean(g_hat * gain, axis=-1, keepdims=True))
    return d_v, _colsum(g_hat)


_K0 = 0.7978845608028654
_K1 = 0.044715


def _gelu_parts(v):
    t = jnp.tanh(v * (_K0 + (_K0 * _K1) * (v * v)))
    return t, v * (0.5 + 0.5 * t)


def _gelu_grad(v, t):
    return (0.5 + 0.5 * t) + (0.5 * v) * (1.0 - t * t) * (_K0 + (3.0 * _K0 * _K1) * (v * v))


def _colsum(v):
    return jnp.sum(v, axis=0, keepdims=True)


def _full(shape):
    n = len(shape)
    return pl.BlockSpec(shape, lambda *_: (0,) * n)


def _resident(shape):
    n = len(shape)
    return pl.BlockSpec(shape, lambda *_: (0,) * n, pipeline_mode=pl.Buffered(1))


def _place():
    x, y, c = lax.axis_index("x"), lax.axis_index("y"), lax.axis_index("c")
    return x, y, c


def _flip(v, bit):
    return 1 - v if bit else v


def _peer(x, y, c, k):
    return (_flip(x, (k >> 2) & 1), _flip(y, (k >> 1) & 1), _flip(c, k & 1))


def _index(p):
    return 4 * p[0] + 2 * p[1] + p[2]


def _two_level_gather_begin(x, y, c, out_refs, send_sems, recv_sems):
    me = (x, y, c)
    sibling = (x, y, 1 - c)
    chips = [(1 - x, y), (x, 1 - y), (1 - x, 1 - y)]

    def copy(a, k, block, to):
        ref = out_refs[a].at[_index(block)]
        return pltpu.make_async_remote_copy(
            src_ref=ref, dst_ref=ref, send_sem=send_sems.at[7 * a + k], recv_sem=recv_sems.at[7 * a + k],
            device_id=to, device_id_type=MESH)

    first = []
    for a in range(len(out_refs)):
        first.append(copy(a, 0, me, sibling))
        first += [copy(a, 1 + j, me, (*chip, c)) for j, chip in enumerate(chips)]
    for cp in first:
        cp.start()
    return copy, first, me, sibling, chips


def _two_level_gather_finish(c, n, begun):
    copy, first, me, sibling, chips = begun
    passed = []
    for a in range(n):
        for j, chip in enumerate(chips):
            copy(a, 1 + j, (*chip, c), me).wait_recv()
            fwd = copy(a, 4 + j, (*chip, c), sibling)
            fwd.start()
            passed.append(fwd)
    for a in range(n):
        copy(a, 0, sibling, me).wait_recv()
        for j, chip in enumerate(chips):
            copy(a, 4 + j, (*chip, 1 - c), me).wait_recv()
    for cp in first + passed:
        cp.wait_send()


def _fwd_comm(c8, w_ada, b_my, gathered, kept):
    ncol = w_ada.shape[1]
    n_g, n_k = len(gathered), len(kept)

    def body(c_ref, w_ref, b_ref, *rest):
        g_in, k_in = rest[:n_g], rest[n_g:n_g + n_k]
        modp_ref, sc_ref = rest[n_g + n_k:n_g + n_k + 2]
        g_out = rest[n_g + n_k + 2:2 * n_g + n_k + 2]
        k_out = rest[2 * n_g + n_k + 2:2 * n_g + 2 * n_k + 2]
        cg, mg, part, send_sems, recv_sems, g_send, g_recv = rest[2 * n_g + 2 * n_k + 2:]
        x, y, c = _place()
        me = _index((x, y, c))
        for a in range(n_g):
            g_out[a][me] = g_in[a][...].astype(BF16)
        begun = _two_level_gather_begin(x, y, c, g_out, g_send, g_recv)
        for a in range(n_k):
            k_out[a][...] = k_in[a][...].astype(BF16)

        def c_copy(k):
            p = _peer(x, y, c, k)
            return pltpu.make_async_remote_copy(
                src_ref=c_ref, dst_ref=cg.at[me], send_sem=send_sems.at[k - 1], recv_sem=recv_sems.at[k - 1],
                device_id=p, device_id_type=MESH)

        def c_arrival(k):
            p = _peer(x, y, c, k)
            return pltpu.make_async_remote_copy(
                src_ref=c_ref, dst_ref=cg.at[_index(p)], send_sem=send_sems.at[k - 1], recv_sem=recv_sems.at[k - 1],
                device_id=p, device_id_type=MESH)

        def m_copy(k):
            p = _peer(x, y, c, k)
            return pltpu.make_async_remote_copy(
                src_ref=part, dst_ref=mg.at[me], send_sem=send_sems.at[6 + k], recv_sem=recv_sems.at[6 + k],
                device_id=p, device_id_type=MESH)

        def m_arrival(k):
            p = _peer(x, y, c, k)
            return pltpu.make_async_remote_copy(
                src_ref=part, dst_ref=mg.at[_index(p)], send_sem=send_sems.at[6 + k], recv_sem=recv_sems.at[6 + k],
                device_id=p, device_id_type=MESH)

        for k in range(1, N_DEV):
            c_copy(k).start()
        cg[me] = c_ref[...]
        for k in range(1, N_DEV):
            c_arrival(k).wait_recv()
        c_all = jnp.concatenate([cg[j, 0:1, :] for j in range(N_DEV)], axis=0)
        sc = c_all * jax.nn.sigmoid(c_all)
        sc_ref[...] = sc
        part[...] = _dot(sc.astype(BF16), w_ref[...].astype(BF16)) + b_ref[...]
        for k in range(1, N_DEV):
            m_copy(k).start()
        mg[me] = part[...]
        for k in range(1, N_DEV):
            m_arrival(k).wait_recv()
        for j in range(N_DEV):
            modp_ref[j:j + 1, :] = mg[j, pl.ds(me, 1), :]
        _two_level_gather_finish(c, n_g, begun)
        for k in range(1, N_DEV):
            c_copy(k).wait_send()
            m_copy(k).wait_send()

    vm = pl.BlockSpec(memory_space=pltpu.VMEM)
    outs = pl.pallas_call(
        body, name="fwd_comm",
        out_shape=tuple([jax.ShapeDtypeStruct((N_DEV, ncol), F32), jax.ShapeDtypeStruct((N_DEV, D), F32)]
                        + [jax.ShapeDtypeStruct((N_DEV,) + s.shape, BF16) for s in gathered]
                        + [jax.ShapeDtypeStruct(s.shape, BF16) for s in kept]),
        in_specs=[vm] * (3 + n_g + n_k), out_specs=tuple([vm] * (2 + n_g + n_k)),
        scratch_shapes=[
            pltpu.VMEM((N_DEV, 8, D), F32),
            pltpu.VMEM((N_DEV, N_DEV, ncol), F32),
            pltpu.VMEM((N_DEV, ncol), F32),
            pltpu.SemaphoreType.DMA((2 * (N_DEV - 1),)),
            pltpu.SemaphoreType.DMA((2 * (N_DEV - 1),)),
            pltpu.SemaphoreType.DMA((7 * n_g,)),
            pltpu.SemaphoreType.DMA((7 * n_g,)),
        ],
        compiler_params=pltpu.CompilerParams(vmem_limit_bytes=VMEM_LIMIT),
    )(c8, w_ada, b_my, *gathered, *kept)
    return outs[0], outs[1], outs[2:2 + n_g], outs[2 + n_g:]


FC_EARLY = 6
FC_HEAD = 2
R_HEAD_COLS = (FC_EARLY - FC_HEAD) * FF_BLK
WGRAD_ORDER = (7, 6, 1, 3, 5, 2, 4, 0)


def _early_col(j):
    return R_HEAD_COLS + j * FF_BLK if j < FC_HEAD else (j - FC_HEAD) * FF_BLK


class _Copies:
    def __init__(self, entries, send_sems, recv_sems):
        self.place = _place()
        self.entries, self.send_sems, self.recv_sems = entries, send_sems, recv_sems

    def _copy(self, i, arrival=False):
        src, dst, rel = self.entries[i]
        return pltpu.make_async_remote_copy(
            src_ref=dst if arrival else src, dst_ref=dst, send_sem=self.send_sems.at[i],
            recv_sem=self.recv_sems.at[i], device_id=_peer(*self.place, rel), device_id_type=MESH)

    def start(self, *which):
        for i in which:
            self._copy(i).start()

    def wait_recv(self, *which):
        for i in which:
            self._copy(i, arrival=True).wait_recv()

    def wait_send(self, *which):
        for i in which:
            self._copy(i).wait_send()


TAIL_STEPS = 8


def _tail_comm(parts, small, row_chunk, fc):
    n, n_fc = len(parts), len(fc)

    def body(*refs):
        p_refs, small_ref = refs[:n], refs[n]
        fc_in = refs[n + 1:n + 1 + 6 * n_fc]
        outs = refs[n + 1 + 6 * n_fc:]
        g_refs, total_ref = outs[:n], outs[n]
        fc_out = outs[n + 1:n + 1 + 4 * n_fc]
        scr = outs[n + 1 + 4 * n_fc:]
        from_sib = scr[0:n]
        chip_out = scr[n:2 * n]
        chip_in = scr[2 * n:3 * n]
        pack, pack_sib, fine, bulk, total_scr = scr[3 * n:3 * n + 5]
        send_a, recv_a, send_b, recv_b, send_s, recv_s = scr[3 * n + 5:]
        step = pl.program_id(0)
        x, y, c = _place()
        me = _index((x, y, c))
        sibling = (x, y, 1 - c)
        my_chip = 2 * x + y
        others = [(1 - x, y), (x, 1 - y), (1 - x, 1 - y)]
        my_half = pl.ds(pl.multiple_of(PACK_HALF * c, 8), PACK_HALF)

        def pack_to_sibling():
            return pltpu.make_async_remote_copy(
                src_ref=pack, dst_ref=pack_sib, send_sem=send_s.at[0], recv_sem=recv_s.at[0],
                device_id=sibling, device_id_type=MESH)

        def half_to_chip(r, part):
            buf = (fine, bulk)[part]
            return pltpu.make_async_remote_copy(
                src_ref=buf.at[my_chip], dst_ref=buf.at[my_chip],
                send_sem=send_s.at[1 + 3 * part + r], recv_sem=recv_s.at[1 + 3 * part + r],
                device_id=(*others[r], c), device_id_type=MESH)

        def half_from_chip(r, part):
            k = 2 * others[r][0] + others[r][1]
            buf = (fine, bulk)[part]
            return pltpu.make_async_remote_copy(
                src_ref=buf.at[k], dst_ref=buf.at[k],
                send_sem=send_s.at[1 + 3 * part + r], recv_sem=recv_s.at[1 + 3 * part + r],
                device_id=(*others[r], c), device_id_type=MESH)

        def total_to_sibling():
            return pltpu.make_async_remote_copy(
                src_ref=total_scr.at[my_half], dst_ref=total_scr.at[my_half],
                send_sem=send_s.at[7], recv_sem=recv_s.at[7], device_id=sibling, device_id_type=MESH)

        def total_from_sibling():
            sib_half = pl.ds(pl.multiple_of(PACK_HALF * (1 - c), 8), PACK_HALF)
            return pltpu.make_async_remote_copy(
                src_ref=total_scr.at[sib_half], dst_ref=total_scr.at[sib_half],
                send_sem=send_s.at[7], recv_sem=recv_s.at[7], device_id=sibling, device_id_type=MESH)

        def to_sibling(a, k):
            return pltpu.make_async_remote_copy(
                src_ref=p_refs[a].at[2 * k + (1 - c)], dst_ref=from_sib[a].at[k],
                send_sem=send_a.at[a], recv_sem=recv_a.at[a], device_id=sibling, device_id_type=MESH)

        def all_from_sibling(a):
            return pltpu.make_async_remote_copy(
                src_ref=from_sib[a], dst_ref=from_sib[a], send_sem=send_a.at[a], recv_sem=recv_a.at[a],
                device_id=sibling, device_id_type=MESH)

        def to_chip(a, r):
            return pltpu.make_async_remote_copy(
                src_ref=chip_out[a].at[r], dst_ref=chip_in[a].at[r],
                send_sem=send_b.at[3 * a + r], recv_sem=recv_b.at[3 * a + r],
                device_id=(*others[r], c), device_id_type=MESH)

        @pl.when(step == 0)
        def _():
            pack[0:PACK_FINE, :] = jnp.zeros((PACK_FINE, D), F32)
            pack[PK_TABLE_B:PK_MISC, :] = jnp.zeros((PK_MISC - PK_TABLE_B, D), F32)
            pack[pl.ds(pl.multiple_of(_table_row(me), 8), 8), :] = small_ref[0:8, :]
            pack[PK_WS:PK_WS + 64, :] = small_ref[ROW_WS:ROW_WS + 64, :]
            pack[PK_MISC:PK_MISC + 8, :] = small_ref[ROW_N1PRE:ROW_N1PRE + 8, :]
            pack[PK_BS:PK_BS + 8, :] = small_ref[ROW_BS:ROW_BS + 8, :]
            pack[PK_WP:PK_WP + 64, :] = small_ref[ROW_WP:ROW_WP + 64, :]
            pack_to_sibling().start()
            for a in range(n):
                for k in range(4):
                    to_sibling(a, k).start()

        @pl.when(step == 1)
        def _():
            pack_to_sibling().wait_recv()
            chip_sum = pack[my_half, :] + pack_sib[my_half, :]
            fine[my_chip] = chip_sum[:PACK_FINE, :]
            bulk[my_chip] = chip_sum[PACK_FINE:, :].astype(BF16)
            for r in range(3):
                half_to_chip(r, 0).start()
                half_to_chip(r, 1).start()
            for a in range(n):
                all_from_sibling(a).wait_recv()
                rows = p_refs[a].shape[1]
                for r in range(3):
                    k = 2 * others[r][0] + others[r][1]
                    for s in range(0, rows, row_chunk):
                        sl = pl.ds(s, row_chunk)
                        chip_out[a][r, sl, :] = (p_refs[a][2 * k + c, sl, :].astype(F32)
                                                 + from_sib[a][k, sl, :].astype(F32)).astype(BF16)
                    to_chip(a, r).start()
                for s in range(0, rows, row_chunk):
                    sl = pl.ds(s, row_chunk)
                    g_refs[a][sl, :] = (p_refs[a][2 * my_chip + c, sl, :].astype(F32)
                                        + from_sib[a][my_chip, sl, :].astype(F32))

        for k in range(n_fc):
            w_ref, own_ref, arr_ref, diag_ref, m_ref, v_ref = fc_in[6 * k:6 * k + 6]
            g = own_ref[...]
            for r in range(2):
                g = g + arr_ref[r].astype(F32)
            g = g + diag_ref[...].astype(F32)
            fc_out[4 * k][...] = g
            fc_out[4 * k + 1][...], fc_out[4 * k + 2][...], fc_out[4 * k + 3][...] = _adam(
                w_ref[...], g, m_ref[...], v_ref[...])

        @pl.when(step == TAIL_STEPS - 1)
        def _():
            for r in range(3):
                half_from_chip(r, 0).wait_recv()
                half_from_chip(r, 1).wait_recv()
            half_start = pl.multiple_of(PACK_HALF * c, 8)
            total_scr[pl.ds(half_start, PACK_FINE), :] = ((fine[0] + fine[1]) + fine[2]) + fine[3]
            total_scr[pl.ds(half_start + PACK_FINE, PACK_HALF - PACK_FINE), :] = (
                (bulk[0].astype(F32) + bulk[1].astype(F32)) + bulk[2].astype(F32)) + bulk[3].astype(F32)
            total_to_sibling().start()
            for a in range(n):
                rows = p_refs[a].shape[1]
                for r in range(3):
                    to_chip(a, r).wait_recv()
                    for s in range(0, rows, row_chunk):
                        sl = pl.ds(s, row_chunk)
                        g_refs[a][sl, :] = g_refs[a][sl, :] + chip_in[a][r, sl, :].astype(F32)
            total_from_sibling().wait_recv()
            total_ref[...] = total_scr[...]
            for a in range(n):
                all_from_sibling(a).wait_send()
                for r in range(3):
                    to_chip(a, r).wait_send()
            pack_to_sibling().wait_send()
            for r in range(3):
                half_to_chip(r, 0).wait_send()
                half_to_chip(r, 1).wait_send()
            total_to_sibling().wait_send()

    fc_specs_in, fc_specs_out, fc_shapes, fc_args = [], [], [], []
    for w, own, arrived, diagonal, m, v in fc:
        rows, cols = w.shape
        blk = pl.BlockSpec((rows // TAIL_STEPS, cols), lambda i: (i, 0))
        fc_specs_in += [blk, blk, pl.BlockSpec((2, rows // TAIL_STEPS, cols), lambda i: (0, i, 0)), blk, blk, blk]
        fc_specs_out += [blk] * 4
        fc_shapes += [jax.ShapeDtypeStruct((rows, cols), F32)] * 4
        fc_args += [w, own, arrived, diagonal, m, v]
    outs = pl.pallas_call(
        body, name="tail_comm", grid=(TAIL_STEPS,),
        out_shape=tuple([jax.ShapeDtypeStruct(p.shape[1:], F32) for p in parts]
                        + [jax.ShapeDtypeStruct((PACK_ROWS, D), F32)] + fc_shapes),
        in_specs=[_resident(p.shape) for p in parts] + [_resident(small.shape)] + fc_specs_in,
        out_specs=tuple([_full(p.shape[1:]) for p in parts] + [_full((PACK_ROWS, D))] + fc_specs_out),
        scratch_shapes=(
            [pltpu.VMEM((4,) + p.shape[1:], BF16) for p in parts]
            + [pltpu.VMEM((3,) + p.shape[1:], BF16) for p in parts]
            + [pltpu.VMEM((3,) + p.shape[1:], BF16) for p in parts]
            + [pltpu.VMEM((PACK_ROWS, D), F32), pltpu.VMEM((PACK_ROWS, D), F32),
               pltpu.VMEM((4, PACK_FINE, D), F32), pltpu.VMEM((4, PACK_HALF - PACK_FINE, D), BF16),
               pltpu.VMEM((PACK_ROWS, D), F32)]
            + [pltpu.SemaphoreType.DMA((n,)), pltpu.SemaphoreType.DMA((n,)),
               pltpu.SemaphoreType.DMA((3 * n,)), pltpu.SemaphoreType.DMA((3 * n,)),
               pltpu.SemaphoreType.DMA((8,)), pltpu.SemaphoreType.DMA((8,))]),
        compiler_params=pltpu.CompilerParams(dimension_semantics=("arbitrary",), vmem_limit_bytes=VMEM_LIMIT),
    )(*parts, small, *fc_args)
    return outs[:n + 1], [outs[n + 1 + 4 * k:n + 5 + 4 * k] for k in range(n_fc)]


def _tril_mask():
    row = lax.broadcasted_iota(jnp.int32, (CHUNK, CHUNK), 0)
    col = lax.broadcasted_iota(jnp.int32, (CHUNK, CHUNK), 1)
    return (col <= row).astype(F32)


def _window_sums(ext):
    s2 = ext + pltpu.roll(ext, 1, 0)
    t4 = s2[:, GROUP:]
    s4 = t4 + pltpu.roll(t4, 2, 0)
    t8 = s4[:, GROUP:]
    s8 = t8 + pltpu.roll(t8, 4, 0)
    t16 = s8[:, GROUP:]
    s16 = t16 + pltpu.roll(t16, 8, 0)
    return [s2[:, :GROUP], s4[:, :GROUP], s8[:, :GROUP], s16]


def _inv_counts(first_pos, rows):
    pos = first_pos + lax.broadcasted_iota(jnp.int32, (rows, 1), 0)
    return [1.0 / jnp.minimum(pos + 1, w).astype(F32) for w in WINDOWS]


def _pool_diff(zb, halo, first_pos):
    tt = zb.shape[0]
    sums = _window_sums(jnp.concatenate([halo, zb], axis=0))
    inv = _inv_counts(first_pos, tt)
    return [sums[g][HALO:, :] * inv[g] - zb[:, g * GROUP:(g + 1) * GROUP] for g in range(len(WINDOWS))]


def _attn_fwd(tt, x, mod, n1pre, n1post, w_in_t, w_out_shard, w_sp, bs_rows, ln_g, ln_b, w_pool, b_pool, pool_scale,
              fc_shards, n2pre):
    t_len = x.shape[0]
    nt = t_len // tt
    wo_rows = w_out_shard.shape[0]

    def body(x_ref, xb_ref, mod_ref, n1pre_ref, n1post_ref, win_ref, wo_ref, wsp_ref, bs_ref, lng_ref, lnb_ref,
             wp_ref, bp_ref, ps_ref, w1_ref, w2_ref, n2pre_ref,
             z_ref, cat_ref, mix_ref, x1_ref, h2_ref, r_ref, f_ref, e1_ref, e2_ref, wout_ref,
             carry, land1, land2, sib1, sib2, cat_keep, wout_scr, send_sems, recv_sems, local_sems,
             wo_send, wo_recv):
        i = pl.program_id(0)
        px, py, pc = _place()

        def wo_block(rel):
            start = pl.multiple_of(wo_rows * _index(_peer(px, py, pc, rel)), wo_rows)
            return wout_scr.at[pl.ds(start, wo_rows), :]

        wo_copies = _Copies(
            [(wo_ref, wo_block(0), 1), (wo_ref, wo_block(0), 2), (wo_ref, wo_block(0), 4), (wo_ref, wo_block(0), 6),
             (wo_block(2), wo_block(2), 1), (wo_block(4), wo_block(4), 1), (wo_block(6), wo_block(6), 1)],
            wo_send, wo_recv)
        wo_keep = pltpu.make_async_copy(wout_scr, wout_ref, local_sems.at[8])
        copies = _Copies(
            [(w1_ref, sib1, 1), (w2_ref, sib2, 1),
             (w1_ref, land1.at[0], 2), (w2_ref, land2.at[0], 2),
             (w1_ref, land1.at[1], 4), (w2_ref, land2.at[1], 4),
             (land1.at[0], e1_ref.at[3], 1), (land2.at[0], e2_ref.at[3], 1),
             (land1.at[1], e1_ref.at[5], 1), (land2.at[1], e2_ref.at[5], 1)],
            send_sems, recv_sems)
        keep = [pltpu.make_async_copy(w1_ref, e1_ref.at[0], local_sems.at[0]),
                pltpu.make_async_copy(w2_ref, e2_ref.at[0], local_sems.at[1]),
                pltpu.make_async_copy(land1.at[0], e1_ref.at[2], local_sems.at[2]),
                pltpu.make_async_copy(land1.at[1], e1_ref.at[4], local_sems.at[3]),
                pltpu.make_async_copy(land2.at[0], e2_ref.at[2], local_sems.at[4]),
                pltpu.make_async_copy(land2.at[1], e2_ref.at[4], local_sems.at[5]),
                pltpu.make_async_copy(sib1, e1_ref.at[1], local_sems.at[6]),
                pltpu.make_async_copy(sib2, e2_ref.at[1], local_sems.at[7])]

        @pl.when(i == 0)
        def _():
            wo_copies.start(0, 1, 2, 3)
            copies.start(0, 1, 2, 4, 3, 5)
            keep[0].start()
            keep[1].start()
            own = pl.multiple_of(wo_rows * _index((px, py, pc)), wo_rows)
            wout_scr[pl.ds(own, wo_rows), :] = wo_ref[...]
            carry[...] = jnp.zeros_like(carry)

        @pl.when(i == nt // 2 + ATTN_LAG)
        def _():
            copies.wait_recv(2, 4)
            copies.start(6, 8)
            keep[2].start()
            keep[3].start()

        @pl.when(i == nt - 1 + ATTN_LAG)
        def _():
            copies.wait_recv(3, 5)
            copies.start(7, 9)
            keep[4].start()
            keep[5].start()

        shift1, scale1, gate1 = mod_ref[0:1, :], mod_ref[1:2, :], mod_ref[2:3, :]

        @pl.when(i < nt)
        def _():
            xv = x_ref[...]
            h1 = (xv * _rstd(xv)) * (n1pre_ref[...] * (1.0 + scale1)) + shift1
            z = _dot_nt(h1.astype(BF16), win_ref[...])
            z_ref[...] = z

            _, ga = _gelu_parts(z[:, :2 * D_A])
            u, vr = ga[:, :D_A], ga[:, D_A:]
            dv = vr - jnp.mean(vr, axis=-1, keepdims=True)
            v = (dv * lax.rsqrt(jnp.mean(dv * dv, axis=-1, keepdims=True) + EPS)) * lng_ref[...] + lnb_ref[...]
            vb = v.astype(BF16)
            mask = _tril_mask()
            wc = [(wsp_ref[h] * mask).astype(BF16) for h in range(N_HEADS)]
            for ch in range(tt // CHUNK):
                rows = slice(ch * CHUNK, (ch + 1) * CHUNK)
                for h in range(N_HEADS):
                    cols = slice(h * GROUP, (h + 1) * GROUP)
                    mixed = _dot(wc[h], vb[rows, cols]) + bs_ref[:, cols]
                    cat_ref[rows, cols] = (u[rows, cols] * mixed).astype(BF16)

            zb = z[:, 2 * D_A:]
            diff = _pool_diff(zb, carry[...], i * tt)
            carry[...] = zb[tt - HALO:, :]
            for g in range(len(WINDOWS)):
                cols = slice(g * GROUP, (g + 1) * GROUP)
                pre = _dot(diff[g].astype(BF16), wp_ref[g].astype(BF16)) + bp_ref[:, cols]
                cat_ref[:, D_A + g * GROUP:D_A + (g + 1) * GROUP] = (pre * ps_ref[:, cols]).astype(BF16)
            cat_keep[i % (ATTN_LAG + 1)] = cat_ref[...]

        @pl.when(i == 0)
        def _():
            copies.wait_recv(0, 1)
            keep[6].start()
            keep[7].start()

        @pl.when(i == 1)
        def _():
            wo_copies.wait_recv(1, 2, 3)
            wo_copies.start(4, 5, 6)

        @pl.when(i == ATTN_LAG)
        def _():
            wo_copies.wait_recv(0, 4, 5, 6)
            wo_keep.start()

        @pl.when(i >= ATTN_LAG)
        def _():
            xv = xb_ref[...]
            mix = _dot(cat_keep[(i - ATTN_LAG) % (ATTN_LAG + 1)], wout_scr[...])
            mix_ref[...] = mix
            x1v = xv + (mix * _rstd(mix)) * (gate1 * n1post_ref[...])
            x1_ref[...] = x1v
            shift2, scale2 = mod_ref[3:4, :], mod_ref[4:5, :]
            h2 = ((x1v * _rstd(x1v)) * (n2pre_ref[...] * (1.0 + scale2)) + shift2).astype(BF16)
            h2_ref[...] = h2
            for j, (w1, w2) in enumerate(((w1_ref, w2_ref), (sib1, sib2))):
                ra = jnp.maximum(_dot(h2, w1[...]), 0.0)
                r = (ra * ra).astype(BF16)
                r_ref[:, j * FF_BLK:(j + 1) * FF_BLK] = r
                if j == 0:
                    f_ref[...] = _dot(r, w2[...])
                else:
                    f_ref[...] += _dot(r, w2[...])

        @pl.when(i == nt - 1 + ATTN_LAG)
        def _():
            copies.wait_recv(6, 7, 8, 9)
            copies.wait_send(*range(10))
            wo_copies.wait_send(*range(7))
            for cp in keep:
                cp.wait()
            wo_keep.wait()

    first = lambda w: pl.BlockSpec((tt, w), lambda i: (jnp.minimum(i, nt - 1), 0))
    second = lambda w: pl.BlockSpec((tt, w), lambda i: (jnp.maximum(i - ATTN_LAG, 0), 0))
    r_head = pl.BlockSpec((tt, FC_HEAD * FF_BLK),
                          lambda i: (jnp.maximum(i - ATTN_LAG, 0), R_HEAD_COLS // (FC_HEAD * FF_BLK)))
    hbm = pl.BlockSpec(memory_space=pl.ANY)
    outs = pl.pallas_call(
        body, name="attn_fwd", grid=(nt + ATTN_LAG,),
        out_shape=tuple([jax.ShapeDtypeStruct((t_len, D_Z), F32), jax.ShapeDtypeStruct((t_len, D), BF16),
                         jax.ShapeDtypeStruct((t_len, D), F32), jax.ShapeDtypeStruct((t_len, D), F32),
                         jax.ShapeDtypeStruct((t_len, D), BF16),
                         jax.ShapeDtypeStruct((t_len, FC_EARLY * FF_BLK), BF16),
                         jax.ShapeDtypeStruct((t_len, D), F32)]
                        + [jax.ShapeDtypeStruct((FC_EARLY,) + s.shape, BF16) for s in fc_shards]
                        + [jax.ShapeDtypeStruct((D, D), BF16)]),
        in_specs=[first(D), second(D), _full((8, D)), _full((1, D)), _full((1, D)), _resident((D_Z, D)),
                  _resident(w_out_shard.shape),
                  _full((N_HEADS, CHUNK, CHUNK)), _full((CHUNK, D_A)), _full((1, D_A)), _full((1, D_A)),
                  _full((len(WINDOWS), GROUP, GROUP)), _full((1, D_B)), _full((1, D_B)),
                  _resident(fc_shards[0].shape), _resident(fc_shards[1].shape), _full((1, D))],
        out_specs=(first(D_Z), first(D), second(D), second(D), second(D), r_head, second(D), hbm, hbm, hbm),
        scratch_shapes=[pltpu.VMEM((HALO, D_B), F32),
                        pltpu.VMEM((2,) + fc_shards[0].shape, BF16), pltpu.VMEM((2,) + fc_shards[1].shape, BF16),
                        pltpu.VMEM(fc_shards[0].shape, BF16), pltpu.VMEM(fc_shards[1].shape, BF16),
                        pltpu.VMEM((ATTN_LAG + 1, tt, D), BF16), pltpu.VMEM((D, D), BF16),
                        pltpu.SemaphoreType.DMA((10,)), pltpu.SemaphoreType.DMA((10,)),
                        pltpu.SemaphoreType.DMA((9,)),
                        pltpu.SemaphoreType.DMA((7,)), pltpu.SemaphoreType.DMA((7,))],
        compiler_params=pltpu.CompilerParams(dimension_semantics=("arbitrary",), vmem_limit_bytes=VMEM_LIMIT),
    )(x, x, mod, n1pre, n1post, w_in_t, w_out_shard, w_sp, bs_rows, ln_g, ln_b, w_pool, b_pool, pool_scale,
      *fc_shards, n2pre)
    return outs[:7], outs[7:9], outs[9]


def _mlp_fwd_early(tt, r_begun, h2, f_head, w1_early, w2_early):
    t_len = h2.shape[0]
    nt = t_len // tt
    n_late = N_DEV - FC_EARLY

    def body(r_begun_ref, h2_ref, fh_ref, w1_ref, w2_ref, r_ref, f_ref, l1_ref, l2_ref,
             land1, land2, send_sems, recv_sems, local_sems):
        i = pl.program_id(0)
        copies = _Copies(
            [(w1_ref.at[2], land1, 4), (w2_ref.at[4], land2, 2),
             (land1, l1_ref.at[1], 1), (land2, l2_ref.at[1], 1)],
            send_sems, recv_sems)
        keep = [pltpu.make_async_copy(land1, l1_ref.at[0], local_sems.at[0]),
                pltpu.make_async_copy(land2, l2_ref.at[0], local_sems.at[1])]

        @pl.when(i == 0)
        def _():
            copies.start(0, 1)

        @pl.when(i == nt - 1)
        def _():
            copies.wait_recv(0, 1)
            copies.start(2, 3)
            for cp in keep:
                cp.start()

        h2 = h2_ref[...]
        f_ref[...] = fh_ref[...]
        for j in range(FC_HEAD, FC_EARLY):
            ra = jnp.maximum(_dot(h2, w1_ref[j]), 0.0)
            r = (ra * ra).astype(BF16)
            r_ref[:, _early_col(j):_early_col(j) + FF_BLK] = r
            f_ref[...] += _dot(r, w2_ref[j])

        @pl.when(i == nt - 1)
        def _():
            copies.wait_recv(2, 3)
            copies.wait_send(0, 1, 2, 3)
            for cp in keep:
                cp.wait()

    tile = lambda w: pl.BlockSpec((tt, w), lambda i: (i, 0))
    hbm = pl.BlockSpec(memory_space=pl.ANY)
    outs = pl.pallas_call(
        body, name="mlp_fwd_early", grid=(nt,),
        out_shape=(jax.ShapeDtypeStruct((t_len, FC_EARLY * FF_BLK), BF16), jax.ShapeDtypeStruct((t_len, D), F32),
                   jax.ShapeDtypeStruct((n_late,) + w1_early.shape[1:], BF16),
                   jax.ShapeDtypeStruct((n_late,) + w2_early.shape[1:], BF16)),
        in_specs=[hbm, tile(D), tile(D), _resident((FC_EARLY, D, FF_BLK)), _resident((FC_EARLY, FF_BLK, D))],
        out_specs=(tile(R_HEAD_COLS), tile(D), hbm, hbm),
        input_output_aliases={0: 0},
        scratch_shapes=[pltpu.VMEM(w1_early.shape[1:], BF16), pltpu.VMEM(w2_early.shape[1:], BF16),
                        pltpu.SemaphoreType.DMA((4,)), pltpu.SemaphoreType.DMA((4,)),
                        pltpu.SemaphoreType.DMA((2,))],
        compiler_params=pltpu.CompilerParams(dimension_semantics=("arbitrary",), vmem_limit_bytes=VMEM_LIMIT),
    )(r_begun, h2, f_head, w1_early, w2_early)
    return outs[:2], outs[2:]


def _mlp_late_bwd(tt, r_early, x1, h2, f_early, tgt, mix, mod, n2pre, n2post, n1post,
                  w1_early, w2_early, w1_late, w2_late):
    t_len = x1.shape[0]
    nt = t_len // tt
    n_late = N_DEV - FC_EARLY
    late_cols = n_late * FF_BLK

    def body(re_ref, x1_ref, h2_ref, fe_ref, tgt_ref, mix_ref, mod_ref, n2pre_ref, n2post_ref,
             n1post_ref, w1e_ref, w2e_ref, w1l_ref, w2l_ref,
             rl_ref, df_ref, da_ref, dmix_ref, dx1_ref, redf_ref, redb_ref, dh2_acc):
        i = pl.program_id(0)

        @pl.when(i == 0)
        def _():
            redf_ref[...] = jnp.zeros_like(redf_ref)
            redb_ref[...] = jnp.zeros_like(redb_ref)

        x1v = x1_ref[...]
        gate1, scale2, gate2 = mod_ref[2:3, :], mod_ref[4:5, :], mod_ref[5:6, :]
        h2 = h2_ref[...]
        f = fe_ref[...]
        for j in range(n_late):
            cols = slice(j * FF_BLK, (j + 1) * FF_BLK)
            ra = jnp.maximum(_dot(h2, w1l_ref[j]), 0.0)
            r = (ra * ra).astype(BF16)
            rl_ref[:, cols] = r
            f = f + _dot(r, w2l_ref[j])
        post2 = n2post_ref[...]
        gate_post2 = gate2 * post2
        rf = _rstd(f)
        fhat = f * rf
        err = (x1v + fhat * gate_post2) - tgt_ref[...]
        dy = err * (1.0 / D)
        d_f, sum_f = _rms_bwd_gained(dy, gate_post2, fhat, rf)
        dfv = d_f.astype(BF16)
        df_ref[...] = dfv
        redf_ref[0:1, :] += post2 * sum_f
        redf_ref[1:2, :] += gate2 * sum_f
        redf_ref[2:3, :] += _colsum(err * err)

        for j in range(N_DEV):
            cols = slice(j * FF_BLK, (j + 1) * FF_BLK)
            if j < FC_EARLY:
                w1, w2, r = w1e_ref[j], w2e_ref[j], re_ref[:, _early_col(j):_early_col(j) + FF_BLK]
            else:
                jl = j - FC_EARLY
                w1, w2, r = w1l_ref[jl], w2l_ref[jl], rl_ref[:, jl * FF_BLK:(jl + 1) * FF_BLK]
            dr = _dot_nt(dfv, w2)
            da = (dr * (2.0 * jnp.sqrt(r.astype(F32)))).astype(BF16)
            da_ref[:, cols] = da
            contrib = _dot_nt(da, w1)
            if j == 0:
                dh2_acc[...] = contrib
            else:
                dh2_acc[...] += contrib
        dh2 = dh2_acc[...]
        pre2, post1 = n2pre_ref[...], n1post_ref[...]
        r2 = _rstd(x1v)
        xhat = x1v * r2
        d_x1, sum_h = _rms_bwd_gained(dh2, pre2 * (1.0 + scale2), xhat, r2)
        dx1 = dy + d_x1
        dx1_ref[...] = dx1
        mixv = mix_ref[...]
        rm = _rstd(mixv)
        mhat = mixv * rm
        d_mix, sum_m = _rms_bwd_gained(dx1, gate1 * post1, mhat, rm)
        dmix_ref[...] = d_mix.astype(BF16)
        redb_ref[0:1, :] += _colsum(dh2)
        redb_ref[1:2, :] += pre2 * sum_h
        redb_ref[2:3, :] += (1.0 + scale2) * sum_h
        redb_ref[3:4, :] += post1 * sum_m
        redb_ref[4:5, :] += gate1 * sum_m

    tile = lambda w: pl.BlockSpec((tt, w), lambda i: (i, 0))
    return pl.pallas_call(
        body, name="mlp_late_bwd", grid=(nt,),
        out_shape=(jax.ShapeDtypeStruct((t_len, late_cols), BF16), jax.ShapeDtypeStruct((t_len, D), BF16),
                   jax.ShapeDtypeStruct((t_len, D_FF), BF16), jax.ShapeDtypeStruct((t_len, D), BF16),
                   jax.ShapeDtypeStruct((t_len, D), F32), jax.ShapeDtypeStruct((8, D), F32),
                   jax.ShapeDtypeStruct((8, D), F32)),
        in_specs=[tile(FC_EARLY * FF_BLK), tile(D), tile(D), tile(D), tile(D),
                  tile(D), _full((8, D)), _full((1, D)), _full((1, D)), _full((1, D)),
                  _resident((FC_EARLY, D, FF_BLK)), _resident((FC_EARLY, FF_BLK, D)),
                  _resident((n_late, D, FF_BLK)), _resident((n_late, FF_BLK, D))],
        out_specs=(tile(late_cols), tile(D), tile(D_FF), tile(D), tile(D), _full((8, D)), _full((8, D))),
        scratch_shapes=[pltpu.VMEM((tt, D), F32)],
        compiler_params=pltpu.CompilerParams(dimension_semantics=("arbitrary",), vmem_limit_bytes=VMEM_LIMIT),
    )(r_early, x1, h2, f_early, tgt, mix, mod, n2pre, n2post, n1post, w1_early, w2_early, w1_late, w2_late)


def _mlp_wgrad(tt, r_early, r_late, da, df, h2):
    t_len = df.shape[0]
    nt = t_len // tt
    odd_steps = [j for j, rel in enumerate(WGRAD_ORDER) if rel % 2]

    def relation(j):
        rel = jnp.int32(WGRAD_ORDER[-1])
        for step in range(N_DEV - 2, -1, -1):
            rel = jnp.where(j == step, WGRAD_ORDER[step], rel)
        return rel

    def body(re_ref, rl_ref, da_ref, df_ref, h2_ref, own1_ref, own2_ref, out1_ref, out2_ref, diag1_ref, diag2_ref,
             acc1, acc2, snd1, snd2, sib1, sib2, dsnd1, dsnd2, send_sems, recv_sems):
        j, t = pl.program_id(0), pl.program_id(1)
        rows = pl.ds(pl.multiple_of(t * tt, tt), tt)
        x, y, c = _place()
        accs, snds, sibs = (acc1, acc2), (snd1, snd2), (sib1, sib2)
        dsnds, diags = (dsnd1, dsnd2), (diag1_ref, diag2_ref)

        def to_sibling(a, jj, buf=0):
            return pltpu.make_async_remote_copy(
                src_ref=snds[a].at[buf], dst_ref=sibs[a].at[jj],
                send_sem=send_sems.at[4 * a + jj], recv_sem=recv_sems.at[4 * a + jj],
                device_id=(x, y, 1 - c), device_id_type=MESH)

        def to_diagonal(a):
            return pltpu.make_async_remote_copy(
                src_ref=dsnds[a], dst_ref=diags[a], send_sem=send_sems.at[8 + a], recv_sem=recv_sems.at[8 + a],
                device_id=_peer(x, y, c, 6), device_id_type=MESH)

        @pl.when(t == 0)
        def _():
            acc2[...] = jnp.zeros_like(acc2)
            acc1[...] = jnp.zeros_like(acc1)

        for r_ref, mine in ((re_ref, relation(j) < FC_EARLY), (rl_ref, relation(j) >= FC_EARLY)):
            @pl.when(mine)
            def _():
                acc2[...] += _dot_tn(r_ref[...], df_ref[rows, :])
                acc1[...] += _dot_tn(h2_ref[rows, :], da_ref[...])

        for step, rel in enumerate(WGRAD_ORDER):
            jj = rel // 2

            @pl.when((t == nt - 1) & (j == step))
            def _():
                for a, (own_ref, out_ref) in enumerate(((own1_ref, out1_ref), (own2_ref, out2_ref))):
                    if rel % 2:
                        q = odd_steps.index(step)
                        if q >= 2:
                            to_sibling(a, WGRAD_ORDER[odd_steps[q - 2]] // 2).wait_send()
                        snds[a][q % 2] = accs[a][...].astype(BF16)
                        to_sibling(a, jj, q % 2).start()
                        continue
                    to_sibling(a, jj).wait_recv()
                    chip_sum = accs[a][...] + sibs[a][jj].astype(F32)
                    if rel == 6:
                        dsnds[a][...] = chip_sum.astype(BF16)
                        to_diagonal(a).start()
                    elif rel == 0:
                        own_ref[...] = chip_sum
                    else:
                        out_ref[0] = chip_sum.astype(BF16)
                    if step == N_DEV - 1:
                        for q in (2, 3):
                            to_sibling(a, WGRAD_ORDER[odd_steps[q]] // 2).wait_send()
                        to_diagonal(a).wait_recv()
                        to_diagonal(a).wait_send()

    assert WGRAD_ORDER[-1] == 0 and WGRAD_ORDER[-3:-1] == (2, 4)
    blk = pl.BlockSpec((tt, FF_BLK), lambda j, t: (t, relation(j)))
    early_block = lambda rel: jnp.where(rel < FC_HEAD, rel + FC_EARLY - FC_HEAD, rel - FC_HEAD)
    early = lambda j, t: (jnp.where(relation(j) < FC_EARLY, t, 0),
                          jnp.where(relation(j) < FC_EARLY, early_block(relation(j)), 0))
    late = lambda j, t: (jnp.where(relation(j) < FC_EARLY, 0, t), jnp.maximum(relation(j) - FC_EARLY, 0))
    chip = lambda j, t: (jnp.clip(j - 5, 0, 1), 0, 0)
    hbm = pl.BlockSpec(memory_space=pl.ANY)
    return pl.pallas_call(
        body, name="mlp_wgrad", grid=(N_DEV, nt),
        out_shape=(jax.ShapeDtypeStruct((D, FF_BLK), F32), jax.ShapeDtypeStruct((FF_BLK, D), F32),
                   jax.ShapeDtypeStruct((2, D, FF_BLK), BF16), jax.ShapeDtypeStruct((2, FF_BLK, D), BF16),
                   jax.ShapeDtypeStruct((D, FF_BLK), BF16), jax.ShapeDtypeStruct((FF_BLK, D), BF16)),
        in_specs=[pl.BlockSpec((tt, FF_BLK), early), pl.BlockSpec((tt, FF_BLK), late), blk,
                  _resident((t_len, D)), _resident((t_len, D))],
        out_specs=(_full((D, FF_BLK)), _full((FF_BLK, D)),
                   pl.BlockSpec((1, D, FF_BLK), chip), pl.BlockSpec((1, FF_BLK, D), chip), hbm, hbm),
        scratch_shapes=[pltpu.VMEM((D, FF_BLK), F32), pltpu.VMEM((FF_BLK, D), F32),
                        pltpu.VMEM((2, D, FF_BLK), BF16), pltpu.VMEM((2, FF_BLK, D), BF16),
                        pltpu.VMEM((4, D, FF_BLK), BF16), pltpu.VMEM((4, FF_BLK, D), BF16),
                        pltpu.VMEM((D, FF_BLK), BF16), pltpu.VMEM((FF_BLK, D), BF16),
                        pltpu.SemaphoreType.DMA((10,)), pltpu.SemaphoreType.DMA((10,))],
        compiler_params=pltpu.CompilerParams(dimension_semantics=("arbitrary", "arbitrary"),
                                             vmem_limit_bytes=VMEM_LIMIT),
    )(r_early, r_late, da, df, h2)


def _acc_rows(ref, row0, k, val):
    half = CHUNK // 2
    ref[row0:row0 + half, k * GROUP:(k + 1) * GROUP] += val[:half, :]
    ref[row0:row0 + half, D_A + k * GROUP:D_A + (k + 1) * GROUP] += val[half:, :]


def _attn_bwd(tt, dmix, dx1, x, z, cat, mod, n1pre, w_in_t, w_out, w_sp, bs_rows, ln_g, ln_b, w_pool, b_pool,
              pool_scale, red_fwd, red_bwd, chip_sums):
    t_len = x.shape[0]
    nt = t_len // tt
    hb = tt // HALO
    n_sums = len(chip_sums)

    def body(dmix_ref, dx1_ref, x_ref, z_ref, zprev_ref, cat_ref, mod_ref, n1pre_ref, win_ref, wout_ref, wsp_ref,
             bs_ref, lng_ref, lnb_ref, wp_ref, bp_ref, ps_ref, redf_ref, redb_ref, *rest):
        sum_out = rest[:n_sums]
        gx_ref, gwin_ref, gwout_ref, small_ref = rest[n_sums:n_sums + 4]
        sum_in = rest[n_sums + 4:2 * n_sums + 4]
        carry, acc_in, acc_out, dz_scr, bs_acc, send_sems, recv_sems = rest[2 * n_sums + 4:]
        s = pl.program_id(0)
        i = nt - 1 - s
        px, py, pc = _place()

        def chip_copy(a, r):
            return pltpu.make_async_remote_copy(
                src_ref=sum_out[a].at[r], dst_ref=sum_in[a].at[r],
                send_sem=send_sems.at[2 * a + r], recv_sem=recv_sems.at[2 * a + r],
                device_id=_peer(px, py, pc, 2 * (r + 1)), device_id_type=MESH)

        @pl.when(s == 0)
        def _():
            for a in range(n_sums):
                for r in range(2):
                    chip_copy(a, r).start()
            carry[...] = jnp.zeros_like(carry)
            acc_in[...] = jnp.zeros_like(acc_in)
            acc_out[...] = jnp.zeros_like(acc_out)
            bs_acc[...] = jnp.zeros_like(bs_acc)
            small_ref[...] = jnp.zeros_like(small_ref)
            small_ref[ROW_DMOD + 2:ROW_DMOD + 3, :] = redb_ref[3:4, :]
            small_ref[ROW_DMOD + 3:ROW_DMOD + 5, :] = redb_ref[0:2, :]
            small_ref[ROW_DMOD + 5:ROW_DMOD + 6, :] = redf_ref[0:1, :]
            small_ref[ROW_N1POST:ROW_N1POST + 1, :] = redb_ref[4:5, :]
            small_ref[ROW_N2PRE:ROW_N2PRE + 1, :] = redb_ref[2:3, :]
            small_ref[ROW_N2POST:ROW_N2POST + 1, :] = redf_ref[1:2, :]
            small_ref[ROW_LOSS:ROW_LOSS + 1, :] = redf_ref[2:3, :]

        dmixv = dmix_ref[...]
        dcat = _dot_nt(dmixv, wout_ref[...])
        acc_out[...] += _dot_tn(cat_ref[...], dmixv)

        z = z_ref[...]
        t_g, ga = _gelu_parts(z[:, :2 * D_A])
        u, vr = ga[:, :D_A], ga[:, D_A:]
        dv0 = vr - jnp.mean(vr, axis=-1, keepdims=True)
        rv = lax.rsqrt(jnp.mean(dv0 * dv0, axis=-1, keepdims=True) + EPS)
        vhat = dv0 * rv
        vb = (vhat * lng_ref[...] + lnb_ref[...]).astype(BF16)
        mask = _tril_mask()
        wc = [(wsp_ref[h] * mask).astype(BF16) for h in range(N_HEADS)]

        dya = dcat[:, :D_A]
        for h in range(N_HEADS):
            cols = slice(h * GROUP, (h + 1) * GROUP)
            bs_sum = jnp.zeros((CHUNK, GROUP), F32)
            ws_sum = jnp.zeros((CHUNK, CHUNK), F32)
            for ch in range(tt // CHUNK):
                rows = slice(ch * CHUNK, (ch + 1) * CHUNK)
                v_ch = vb[rows, cols]
                mixed = _dot(wc[h], v_ch) + bs_ref[:, cols]
                dy_ch = dya[rows, cols]
                dz_scr[rows, cols] = dy_ch * mixed
                dmixed = dy_ch * u[rows, cols]
                dmb = dmixed.astype(BF16)
                dz_scr[rows, D_A + h * GROUP:D_A + (h + 1) * GROUP] = _dot_tn(wc[h], dmb)
                bs_sum = bs_sum + dmixed
                ws_sum = ws_sum + _dot_nt(dmb, v_ch)
            _acc_rows(bs_acc, 0, h, bs_sum)
            _acc_rows(small_ref, ROW_WS, h, ws_sum)

        dvl = dz_scr[:, D_A:2 * D_A]
        dvhat = dvl * lng_ref[...]
        dvl_vhat = dvl * vhat
        dvr = rv * (dvhat - jnp.mean(dvhat, axis=-1, keepdims=True)
                    - vhat * jnp.mean(dvl_vhat * lng_ref[...], axis=-1, keepdims=True))
        small_ref[ROW_LN:ROW_LN + 1, 0:D_A] += _colsum(dvl_vhat)
        small_ref[ROW_LN:ROW_LN + 1, D_A:D] += _colsum(dvl)
        dga = jnp.concatenate([dz_scr[:, :D_A], dvr], axis=1)
        dza = dga * _gelu_grad(z[:, :2 * D_A], t_g)

        zb = z[:, 2 * D_A:]
        halo_prev = jnp.where(i == 0, 0.0, zprev_ref[...])
        diff = _pool_diff(zb, halo_prev, i * tt)
        dyb = dcat[:, D_A:]
        inv = _inv_counts(i * tt, tt)
        scaled, ddiffs = [], []
        for g in range(len(WINDOWS)):
            cols = slice(g * GROUP, (g + 1) * GROUP)
            db = diff[g].astype(BF16)
            wpg = wp_ref[g].astype(BF16)
            pre = _dot(db, wpg) + bp_ref[:, cols]
            small_ref[ROW_POOL:ROW_POOL + 1, cols] += _colsum(dyb[:, cols] * pre)
            dpre = dyb[:, cols] * ps_ref[:, cols]
            small_ref[ROW_POOL:ROW_POOL + 1, D_B + g * GROUP:D_B + (g + 1) * GROUP] += _colsum(dpre)
            dpb = dpre.astype(BF16)
            _acc_rows(small_ref, ROW_WP, g, _dot_tn(db, dpb))
            ddiff = _dot_nt(dpb, wpg)
            ddiffs.append(ddiff)
            scaled.append(ddiff * inv[g])
        scaled_all = jnp.concatenate(scaled, axis=1)
        ext = jnp.concatenate([scaled_all, carry[...]], axis=0)
        n_ext = tt + HALO
        s2 = ext + pltpu.roll(ext, n_ext - 1, 0)
        t4 = s2[:, GROUP:]
        s4 = t4 + pltpu.roll(t4, n_ext - 2, 0)
        t8 = s4[:, GROUP:]
        s8 = t8 + pltpu.roll(t8, n_ext - 4, 0)
        t16 = s8[:, GROUP:]
        s16 = t16 + pltpu.roll(t16, n_ext - 8, 0)
        back = [s2[:, :GROUP], s4[:, :GROUP], s8[:, :GROUP], s16]
        carry[...] = scaled_all[:HALO, :]
        dzb = jnp.concatenate([back[g][:tt, :] - ddiffs[g] for g in range(len(WINDOWS))], axis=1)

        dzv = jnp.concatenate([dza, dzb], axis=1).astype(BF16)
        dh1 = _dot(dzv, win_ref[...])
        xv = x_ref[...]
        r1 = _rstd(xv)
        xhat = xv * r1
        shift1, scale1 = mod_ref[0:1, :], mod_ref[1:2, :]
        pre1 = n1pre_ref[...]
        gain1 = pre1 * (1.0 + scale1)
        h1 = (xhat * gain1 + shift1).astype(BF16)
        acc_in[...] += _dot_tn(dzv, h1)
        d_x, sum_h = _rms_bwd_gained(dh1, gain1, xhat, r1)
        gx_ref[...] = dx1_ref[...] + d_x
        small_ref[ROW_DMOD:ROW_DMOD + 1, :] += _colsum(dh1)
        small_ref[ROW_DMOD + 1:ROW_DMOD + 2, :] += pre1 * sum_h
        small_ref[ROW_N1PRE:ROW_N1PRE + 1, :] += (1.0 + scale1) * sum_h

        @pl.when(s == nt - 1)
        def _():
            gwin_ref[...] = acc_in[...].astype(BF16)
            gwout_ref[...] = acc_out[...].astype(BF16)
            bs = _unfold(bs_acc[...])
            for h in range(N_HEADS):
                small_ref[ROW_BS + h:ROW_BS + h + 1, 0:GROUP] = jnp.sum(
                    bs[:, h * GROUP:(h + 1) * GROUP].T, axis=0, keepdims=True)
            for a in range(n_sums):
                for r in range(2):
                    chip_copy(a, r).wait_recv()
                    chip_copy(a, r).wait_send()

    rev = lambda w: pl.BlockSpec((tt, w), lambda s: (nt - 1 - s, 0))
    zprev = pl.BlockSpec((HALO, D_B), lambda s: (jnp.maximum((nt - 1 - s) * hb - 1, 0), 2))
    hbm = pl.BlockSpec(memory_space=pl.ANY)
    outs = pl.pallas_call(
        body, name="attn_bwd", grid=(nt,),
        out_shape=tuple([jax.ShapeDtypeStruct((t_len, D), F32), jax.ShapeDtypeStruct((D_Z, D), BF16),
                         jax.ShapeDtypeStruct((D, D), BF16), jax.ShapeDtypeStruct((SMALL_ROWS, D), F32)]
                        + [jax.ShapeDtypeStruct(cs.shape, cs.dtype) for cs in chip_sums]),
        in_specs=[rev(D), rev(D), rev(D), rev(D_Z), zprev, rev(D), _full((8, D)), _full((1, D)),
                  _resident((D_Z, D)), _resident((D, D)), _full((N_HEADS, CHUNK, CHUNK)), _full((CHUNK, D_A)),
                  _full((1, D_A)), _full((1, D_A)), _full((len(WINDOWS), GROUP, GROUP)), _full((1, D_B)),
                  _full((1, D_B)), _full((8, D)), _full((8, D))] + [_resident(cs.shape) for cs in chip_sums],
        out_specs=tuple([rev(D), _resident((D_Z, D)), _resident((D, D)), _full((SMALL_ROWS, D))] + [hbm] * n_sums),
        scratch_shapes=[pltpu.VMEM((HALO, D_B), F32), pltpu.VMEM((D_Z, D), F32), pltpu.VMEM((D, D), F32),
                        pltpu.VMEM((tt, 2 * D_A), F32), pltpu.VMEM((CHUNK // 2, D), F32),
                        pltpu.SemaphoreType.DMA((2 * n_sums,)), pltpu.SemaphoreType.DMA((2 * n_sums,))],
        compiler_params=pltpu.CompilerParams(dimension_semantics=("arbitrary",), vmem_limit_bytes=VMEM_LIMIT),
    )(dmix, dx1, x, z, z, cat, mod, n1pre, w_in_t, w_out, w_sp, bs_rows, ln_g, ln_b, w_pool, b_pool, pool_scale,
      red_fwd, red_bwd, *chip_sums)
    return outs[:4], outs[4:]


def _adam(w, g, m, v):
    m2 = ADAM_B1 * m + (1.0 - ADAM_B1) * g
    v2 = ADAM_B2 * v + (1.0 - ADAM_B2) * (g * g)
    m_hat = m2 / (1.0 - ADAM_B1 ** ADAM_STEP)
    v_hat = v2 / (1.0 - ADAM_B2 ** ADAM_STEP)
    delta = -ADAM_LR * (m_hat / (jnp.sqrt(v_hat) + ADAM_EPS) + ADAM_WD * w)
    return delta, m2, v2


def _adamw_shard(name, rb, w, g, m, v):
    rows, cols = w.shape

    def body(w_ref, g_ref, m_ref, v_ref, d_ref, m2_ref, v2_ref):
        d_ref[...], m2_ref[...], v2_ref[...] = _adam(w_ref[...], g_ref[...], m_ref[...], v_ref[...])

    blk = pl.BlockSpec((rb, cols), lambda i: (i, 0))
    shp = jax.ShapeDtypeStruct((rows, cols), F32)
    return pl.pallas_call(
        body, name=name, grid=(rows // rb,), out_shape=(shp, shp, shp),
        in_specs=[blk] * 4, out_specs=(blk, blk, blk),
        compiler_params=pltpu.CompilerParams(dimension_semantics=("arbitrary",)),
    )(w, g, m, v)


def _adamw_ada(rb, w, sc, dmod_cols, m, v):
    rows, cols = w.shape

    def body(w_ref, sc_ref, dm_ref, m_ref, v_ref, g_ref, d_ref, m2_ref, v2_ref):
        g = _dot_tn(sc_ref[...].astype(BF16), dm_ref[...].astype(BF16))
        g_ref[...] = g
        d_ref[...], m2_ref[...], v2_ref[...] = _adam(w_ref[...], g, m_ref[...], v_ref[...])

    blk = pl.BlockSpec((rb, cols), lambda i: (i, 0))
    shp = jax.ShapeDtypeStruct((rows, cols), F32)
    return pl.pallas_call(
        body, name="adamw_ada", grid=(rows // rb,), out_shape=(shp, shp, shp, shp),
        in_specs=[blk, pl.BlockSpec((N_DEV, rb), lambda i: (0, i)), _full((N_DEV, cols)), blk, blk],
        out_specs=(blk, blk, blk, blk),
        compiler_params=pltpu.CompilerParams(dimension_semantics=("arbitrary",)),
    )(w, sc, dmod_cols, m, v)


def _unfold(acc_rows):
    return jnp.concatenate([acc_rows[:, :D_A], acc_rows[:, D_A:]], axis=0)


def _adamw_small(total, params):
    n = len(params)
    flat = [a for p in params for a in p]

    def body(*refs):
        s_ref = refs[0]
        p_refs = refs[1:1 + 3 * n]
        loss_ref = refs[1 + 3 * n]
        o_refs = refs[2 + 3 * n:]
        d_b_ada = s_ref[0:6, :]
        for b in range(1, N_DEV):
            d_b_ada = d_b_ada + s_ref[_table_row(b):_table_row(b) + 6, :]
        misc = lambda r: s_ref[PK_MISC + r - ROW_N1PRE:PK_MISC + r - ROW_N1PRE + 1, :]
        loss = jnp.sum(misc(ROW_LOSS), axis=-1, keepdims=True) * (0.5 / D)
        loss_ref[...] = jnp.broadcast_to(loss, (8, GROUP))
        mask = _tril_mask()
        ws = _unfold(s_ref[PK_WS:PK_WS + 64, :])
        wp = _unfold(s_ref[PK_WP:PK_WP + 64, :])
        grads = [
            d_b_ada,
            misc(ROW_N1PRE), misc(ROW_N1POST), misc(ROW_N2PRE), misc(ROW_N2POST),
            misc(ROW_LN)[:, :D_A], misc(ROW_LN)[:, D_A:],
            misc(ROW_POOL)[:, :D_B], misc(ROW_POOL)[:, D_B:],
            s_ref[PK_BS:PK_BS + N_HEADS, 0:GROUP],
            jnp.stack([ws[:, h * GROUP:(h + 1) * GROUP] * mask for h in range(N_HEADS)]),
            jnp.stack([wp[:, g * GROUP:(g + 1) * GROUP] for g in range(len(WINDOWS))]),
        ]
        for k in range(n):
            w_ref, m_ref, v_ref = p_refs[3 * k:3 * k + 3]
            g = grads[k]
            o_refs[4 * k][...] = g
            o_refs[4 * k + 1][...], o_refs[4 * k + 2][...], o_refs[4 * k + 3][...] = _adam(
                w_ref[...], g, m_ref[...], v_ref[...])

    vm = pl.BlockSpec(memory_space=pltpu.VMEM)
    out_shape = [jax.ShapeDtypeStruct((8, GROUP), F32)]
    for w, _, _ in params:
        out_shape += [jax.ShapeDtypeStruct(w.shape, F32)] * 4
    return pl.pallas_call(
        body, name="adamw_small", out_shape=tuple(out_shape),
        in_specs=[vm] * (1 + 3 * n), out_specs=tuple([vm] * len(out_shape)),
    )(total, *flat)


TT_ATTN_FWD = 512
ATTN_LAG = 2
TT_MLP_FWD = 512
TT_MLP = 256
TT_WGRAD = 2048
TT_ATTN_BWD = 512


def kernel(x, c, w_ada, b_ada, norm1_pre, norm1_post, w_in, w_spatial, b_spatial, ln_v_gain, ln_v_bias, w_pool, b_pool, pool_scale, w_out, norm2_pre, norm2_post, w_fc1, w_fc2, loss_target, m_w_ada, m_b_ada, m_norm1_pre, m_norm1_post, m_w_in, m_w_spatial, m_b_spatial, m_ln_v_gain, m_ln_v_bias, m_w_pool, m_b_pool, m_pool_scale, m_w_out, m_norm2_pre, m_norm2_post, m_w_fc1, m_w_fc2, v_w_ada, v_b_ada, v_norm1_pre, v_norm1_post, v_w_in, v_w_spatial, v_b_spatial, v_ln_v_gain, v_ln_v_bias, v_w_pool, v_b_pool, v_pool_scale, v_w_out, v_norm2_pre, v_norm2_post, v_w_fc1, v_w_fc2):
    t_len = x.shape[1]
    me = 4 * lax.axis_index("x") + 2 * lax.axis_index("y") + lax.axis_index("c")
    ada_cols = w_ada.shape[1]
    tt = lambda want: min(want, t_len)

    x2 = x.reshape(t_len, D)
    tgt = loss_target.reshape(t_len, D)
    row = lambda a: a.reshape(1, -1)

    b_my = lax.dynamic_slice_in_dim(b_ada, me * ada_cols, ada_cols).reshape(1, ada_cols)
    modp, sc, (g_in,), (w1_shard, w2_shard, w_out_shard) = _fwd_comm(
        jnp.broadcast_to(c, (8, D)), w_ada, b_my, [w_in.T], [w_fc1, w_fc2, w_out])
    fc_shards = (w1_shard, w2_shard)
    mod = jnp.concatenate([modp.reshape(6, D), jnp.zeros((2, D), F32)], axis=0)
    w_in_t = g_in.reshape(D_Z, D)

    bs_rows = jnp.repeat(b_spatial.T, GROUP, axis=1)
    attn_consts = (w_spatial, bs_rows, row(ln_v_gain), row(ln_v_bias), w_pool, row(b_pool), row(pool_scale))

    (z, cat, mix, x1, h2, r_begun, f_head), (w1_early, w2_early), w_out_all = _attn_fwd(
        tt(TT_ATTN_FWD), x2, mod, row(norm1_pre), row(norm1_post), w_in_t, w_out_shard, *attn_consts, fc_shards,
        row(norm2_pre))
    (r_early, f_early), (w1_late, w2_late) = _mlp_fwd_early(
        tt(TT_MLP_FWD), r_begun, h2, f_head, w1_early, w2_early)
    r_late, df, da, dmix, dx1, red_fwd, red_bwd = _mlp_late_bwd(
        tt(TT_MLP), r_early, x1, h2, f_early, tgt, mix, mod, row(norm2_pre), row(norm2_post), row(norm1_post),
        w1_early, w2_early, w1_late, w2_late)
    own_w1, own_w2, sums_w1, sums_w2, diag_w1, diag_w2 = _mlp_wgrad(tt(TT_WGRAD), r_early, r_late, da, df, h2)
    (grad_x, p_in, p_out, small), (arr_w1, arr_w2) = _attn_bwd(
        tt(TT_ATTN_BWD), dmix, dx1, x2, z, cat, mod, row(norm1_pre), w_in_t, w_out_all, *attn_consts,
        red_fwd, red_bwd, [sums_w1, sums_w2])
    (grad_in_t, grad_out, total), ((grad_w1, d_w1, m_w1, v_w1), (grad_w2, d_w2, m_w2, v_w2)) = _tail_comm(
        [p_in.reshape(N_DEV, D_Z // N_DEV, D), p_out.reshape(N_DEV, D // N_DEV, D)], small, 64,
        [(w_fc1, own_w1, arr_w1, diag_w1, m_w_fc1, v_w_fc1), (w_fc2, own_w2, arr_w2, diag_w2, m_w_fc2, v_w_fc2)])

    d_out, m_out, v_out = _adamw_shard("adamw_out", 128, w_out, grad_out, m_w_out, v_w_out)
    d_in_t, m_in_t, v_in_t = _adamw_shard("adamw_in", D_Z // N_DEV, w_in.T, grad_in_t, m_w_in.T, v_w_in.T)
    table = jnp.concatenate([total[0:PACK_FINE, :], total[PK_TABLE_B:PK_MISC, :]], axis=0)
    dmod_all = table.reshape(N_DEV, 8, D)[:, :6, :].reshape(N_DEV, 6 * D)
    dmod_cols = lax.dynamic_slice_in_dim(dmod_all, me * ada_cols, ada_cols, axis=1)
    grad_ada, d_ada, m_ada, v_ada = _adamw_ada(256, w_ada, sc, dmod_cols, m_w_ada, v_w_ada)

    six = lambda a: a.reshape(6, D)
    small_params = [
        (six(b_ada), six(m_b_ada), six(v_b_ada)),
        (row(norm1_pre), row(m_norm1_pre), row(v_norm1_pre)),
        (row(norm1_post), row(m_norm1_post), row(v_norm1_post)),
        (row(norm2_pre), row(m_norm2_pre), row(v_norm2_pre)),
        (row(norm2_post), row(m_norm2_post), row(v_norm2_post)),
        (row(ln_v_gain), row(m_ln_v_gain), row(v_ln_v_gain)),
        (row(ln_v_bias), row(m_ln_v_bias), row(v_ln_v_bias)),
        (row(pool_scale), row(m_pool_scale), row(v_pool_scale)),
        (row(b_pool), row(m_b_pool), row(v_b_pool)),
        (b_spatial, m_b_spatial, v_b_spatial),
        (w_spatial, m_w_spatial, v_w_spatial),
        (w_pool, m_w_pool, v_w_pool),
    ]
    outs = _adamw_small(total, small_params)
    loss = outs[0][0, 0]
    names = ["b_ada", "norm1_pre", "norm1_post", "norm2_pre", "norm2_post", "ln_v_gain", "ln_v_bias", "pool_scale",
             "b_pool", "b_spatial", "w_spatial", "w_pool"]
    shapes = dict(b_ada=b_ada.shape, norm1_pre=norm1_pre.shape, norm1_post=norm1_post.shape,
                  norm2_pre=norm2_pre.shape, norm2_post=norm2_post.shape, ln_v_gain=ln_v_gain.shape,
                  ln_v_bias=ln_v_bias.shape, pool_scale=pool_scale.shape, b_pool=b_pool.shape,
                  b_spatial=b_spatial.shape, w_spatial=w_spatial.shape, w_pool=w_pool.shape)
    res = {}
    for k, nm in enumerate(names):
        res[nm] = tuple(o.reshape(shapes[nm]) for o in outs[1 + 4 * k:5 + 4 * k])
    res["w_ada"] = (grad_ada, d_ada, m_ada, v_ada)
    res["w_in"] = (grad_in_t.T, d_in_t.T, m_in_t.T, v_in_t.T)
    res["w_out"] = (grad_out, d_out, m_out, v_out)
    res["w_fc1"] = (grad_w1, d_w1, m_w1, v_w1)
    res["w_fc2"] = (grad_w2, d_w2, m_w2, v_w2)

    order = ["w_ada", "b_ada", "norm1_pre", "norm1_post", "w_in", "w_spatial", "b_spatial", "ln_v_gain", "ln_v_bias",
             "w_pool", "b_pool", "pool_scale", "w_out", "norm2_pre", "norm2_post", "w_fc1", "w_fc2"]
    return (loss, grad_x.reshape(x.shape),
            *[res[nm][0] for nm in order], *[res[nm][1] for nm in order],
            *[res[nm][2] for nm in order], *[res[nm][3] for nm in order])
```

```python
import functools

import jax
import jax.numpy as jnp
from jax import lax
from jax.experimental import pallas as pl
from jax.experimental.pallas import tpu as pltpu

F32 = jnp.float32
BF16 = jnp.bfloat16
MESH = pl.DeviceIdType.MESH

N_DEV = 8
D = 1024
D_A = 512
D_B = 512
D_Z = 2 * D_A + D_B
N_HEADS = 4
CHUNK = 128
WINDOWS = (2, 4, 8, 16)
GROUP = 128
D_FF = 4096
FF_BLK = D_FF // N_DEV
HALO = 16
EPS = 1e-6
VMEM_LIMIT = 60 * 1024 * 1024

ADAM_LR = 0.001
ADAM_B1 = 0.9
ADAM_B2 = 0.999
ADAM_EPS = 1e-08
ADAM_WD = 0.01
ADAM_STEP = 10

ROW_DMOD = 0
ROW_N1PRE, ROW_N1POST, ROW_N2PRE, ROW_N2POST = 8, 9, 10, 11
ROW_LN = 12
ROW_POOL = 13
ROW_LOSS = 14
ROW_BS = 16
ROW_WS = 24
ROW_WP = 88
SMALL_ROWS = 152
PACK_FINE = 40
PACK_HALF = PACK_FINE + 64
PACK_ROWS = 2 * PACK_HALF
PK_WS = PACK_FINE
PK_TABLE_B = PACK_HALF
PK_MISC = PK_TABLE_B + 24
PK_BS = PK_MISC + 8
PK_WP = PK_BS + 8


def _table_row(b):
    if isinstance(b, int):
        return 8 * b if 8 * b < PACK_FINE else 8 * b + PK_TABLE_B - PACK_FINE
    return 8 * b + jnp.where(8 * b < PACK_FINE, 0, PK_TABLE_B - PACK_FINE)


def _dot(a, b):
    return jnp.dot(a, b, preferred_element_type=F32)


def _dot_nt(a, b):
    return lax.dot_general(a, b, (((1,), (1,)), ((), ())), preferred_element_type=F32)


def _dot_tn(a, b):
    return lax.dot_general(a, b, (((0,), (0,)), ((), ())), preferred_element_type=F32)


def _rstd(v):
    return lax.rsqrt(jnp.mean(v * v, axis=-1, keepdims=True) + EPS)


def _rms_bwd(d_hat, hat, rstd):
    return rstd * (d_hat - hat * jnp.mean(d_hat * hat, axis=-1, keepdims=True))


def _rms_bwd_gained(g, gain, hat, rstd):
    g_hat = g * hat
    d_v = rstd * (g * gain - hat * jnp.mean(g_hat * gain, axis=-1, keepdims=True))
    return d_v, _colsum(g_hat)


_K0 = 0.7978845608028654
_K1 = 0.044715


def _gelu_parts(v):
    t = jnp.tanh(v * (_K0 + (_K0 * _K1) * (v * v)))
    return t, v * (0.5 + 0.5 * t)


def _gelu_grad(v, t):
    return (0.5 + 0.5 * t) + (0.5 * v) * (1.0 - t * t) * (_K0 + (3.0 * _K0 * _K1) * (v * v))


def _colsum(v):
    return jnp.sum(v, axis=0, keepdims=True)


def _full(shape):
    n = len(shape)
    return pl.BlockSpec(shape, lambda *_: (0,) * n)


def _resident(shape):
    n = len(shape)
    return pl.BlockSpec(shape, lambda *_: (0,) * n, pipeline_mode=pl.Buffered(1))


def _place():
    x, y, c = lax.axis_index("x"), lax.axis_index("y"), lax.axis_index("c")
    return x, y, c


def _flip(v, bit):
    return 1 - v if bit else v


def _peer(x, y, c, k):
    return (_flip(x, (k >> 2) & 1), _flip(y, (k >> 1) & 1), _flip(c, k & 1))


def _index(p):
    return 4 * p[0] + 2 * p[1] + p[2]


def _cast_shards(shards):
    def body(*refs):
        for src, dst in zip(refs[:len(shards)], refs[len(shards):]):
            dst[...] = src[...].astype(BF16)

    vm = pl.BlockSpec(memory_space=pltpu.VMEM)
    return pl.pallas_call(
        body, name="cast_shards", out_shape=tuple(jax.ShapeDtypeStruct(s.shape, BF16) for s in shards),
        in_specs=[vm] * len(shards), out_specs=tuple([vm] * len(shards)),
    )(*shards)


FC_EARLY = 6
FC_HEAD = 2
R_HEAD_COLS = (FC_EARLY - FC_HEAD) * FF_BLK
WGRAD_ORDER = (7, 6, 1, 3, 5, 2, 4, 0)


def _early_col(j):
    return R_HEAD_COLS + j * FF_BLK if j < FC_HEAD else (j - FC_HEAD) * FF_BLK


class _Copies:
    def __init__(self, entries, send_sems, recv_sems):
        self.place = _place()
        self.entries, self.send_sems, self.recv_sems = entries, send_sems, recv_sems

    def _copy(self, i, arrival=False):
        src, dst, rel = self.entries[i]
        return pltpu.make_async_remote_copy(
            src_ref=dst if arrival else src, dst_ref=dst, send_sem=self.send_sems.at[i],
            recv_sem=self.recv_sems.at[i], device_id=_peer(*self.place, rel), device_id_type=MESH)

    def start(self, *which):
        for i in which:
            self._copy(i).start()

    def wait_recv(self, *which):
        for i in which:
            self._copy(i, arrival=True).wait_recv()

    def wait_send(self, *which):
        for i in which:
            self._copy(i).wait_send()


TAIL_STEPS = 8


def _tail_comm(parts, small, row_chunk, fc):
    n, n_fc = len(parts), len(fc)

    def body(*refs):
        p_refs, small_ref = refs[:n], refs[n]
        fc_in = refs[n + 1:n + 1 + 6 * n_fc]
        outs = refs[n + 1 + 6 * n_fc:]
        g_refs, total_ref = outs[:n], outs[n]
        fc_out = outs[n + 1:n + 1 + 4 * n_fc]
        scr = outs[n + 1 + 4 * n_fc:]
        from_sib = scr[0:n]
        chip_out = scr[n:2 * n]
        chip_in = scr[2 * n:3 * n]
        pack, pack_sib, fine, bulk, total_scr = scr[3 * n:3 * n + 5]
        send_a, recv_a, send_b, recv_b, send_s, recv_s = scr[3 * n + 5:]
        step = pl.program_id(0)
        x, y, c = _place()
        me = _index((x, y, c))
        sibling = (x, y, 1 - c)
        my_chip = 2 * x + y
        others = [(1 - x, y), (x, 1 - y), (1 - x, 1 - y)]
        my_half = pl.ds(pl.multiple_of(PACK_HALF * c, 8), PACK_HALF)

        def pack_to_sibling():
            return pltpu.make_async_remote_copy(
                src_ref=pack, dst_ref=pack_sib, send_sem=send_s.at[0], recv_sem=recv_s.at[0],
                device_id=sibling, device_id_type=MESH)

        def half_to_chip(r, part):
            buf = (fine, bulk)[part]
            return pltpu.make_async_remote_copy(
                src_ref=buf.at[my_chip], dst_ref=buf.at[my_chip],
                send_sem=send_s.at[1 + 3 * part + r], recv_sem=recv_s.at[1 + 3 * part + r],
                device_id=(*others[r], c), device_id_type=MESH)

        def half_from_chip(r, part):
            k = 2 * others[r][0] + others[r][1]
            buf = (fine, bulk)[part]
            return pltpu.make_async_remote_copy(
                src_ref=buf.at[k], dst_ref=buf.at[k],
                send_sem=send_s.at[1 + 3 * part + r], recv_sem=recv_s.at[1 + 3 * part + r],
                device_id=(*others[r], c), device_id_type=MESH)

        def total_to_sibling():
            return pltpu.make_async_remote_copy(
                src_ref=total_scr.at[my_half], dst_ref=total_scr.at[my_half],
                send_sem=send_s.at[7], recv_sem=recv_s.at[7], device_id=sibling, device_id_type=MESH)

        def total_from_sibling():
            sib_half = pl.ds(pl.multiple_of(PACK_HALF * (1 - c), 8), PACK_HALF)
            return pltpu.make_async_remote_copy(
                src_ref=total_scr.at[sib_half], dst_ref=total_scr.at[sib_half],
                send_sem=send_s.at[7], recv_sem=recv_s.at[7], device_id=sibling, device_id_type=MESH)

        def to_sibling(a, k):
            return pltpu.make_async_remote_copy(
                src_ref=p_refs[a].at[2 * k + (1 - c)], dst_ref=from_sib[a].at[k],
                send_sem=send_a.at[a], recv_sem=recv_a.at[a], device_id=sibling, device_id_type=MESH)

        def all_from_sibling(a):
            return pltpu.make_async_remote_copy(
                src_ref=from_sib[a], dst_ref=from_sib[a], send_sem=send_a.at[a], recv_sem=recv_a.at[a],
                device_id=sibling, device_id_type=MESH)

        def to_chip(a, r):
            return pltpu.make_async_remote_copy(
                src_ref=chip_out[a].at[r], dst_ref=chip_in[a].at[r],
                send_sem=send_b.at[3 * a + r], recv_sem=recv_b.at[3 * a + r],
                device_id=(*others[r], c), device_id_type=MESH)

        @pl.when(step == 0)
        def _():
            pack[0:PACK_FINE, :] = jnp.zeros((PACK_FINE, D), F32)
            pack[PK_TABLE_B:PK_MISC, :] = jnp.zeros((PK_MISC - PK_TABLE_B, D), F32)
            pack[pl.ds(pl.multiple_of(_table_row(me), 8), 8), :] = small_ref[0:8, :]
            pack[PK_WS:PK_WS + 64, :] = small_ref[ROW_WS:ROW_WS + 64, :]
            pack[PK_MISC:PK_MISC + 8, :] = small_ref[ROW_N1PRE:ROW_N1PRE + 8, :]
            pack[PK_BS:PK_BS + 8, :] = small_ref[ROW_BS:ROW_BS + 8, :]
            pack[PK_WP:PK_WP + 64, :] = small_ref[ROW_WP:ROW_WP + 64, :]
            pack_to_sibling().start()
            for a in range(n):
                for k in range(4):
                    to_sibling(a, k).start()

        @pl.when(step == 1)
        def _():
            pack_to_sibling().wait_recv()
            chip_sum = pack[my_half, :] + pack_sib[my_half, :]
            fine[my_chip] = chip_sum[:PACK_FINE, :]
            bulk[my_chip] = chip_sum[PACK_FINE:, :].astype(BF16)
            for r in range(3):
                half_to_chip(r, 0).start()
                half_to_chip(r, 1).start()
            for a in range(n):
                all_from_sibling(a).wait_recv()
                rows = p_refs[a].shape[1]
                for r in range(3):
                    k = 2 * others[r][0] + others[r][1]
                    for s in range(0, rows, row_chunk):
                        sl = pl.ds(s, row_chunk)
                        chip_out[a][r, sl, :] = (p_refs[a][2 * k + c, sl, :].astype(F32)
                                                 + from_sib[a][k, sl, :].astype(F32)).astype(BF16)
                    to_chip(a, r).start()
                for s in range(0, rows, row_chunk):
                    sl = pl.ds(s, row_chunk)
                    g_refs[a][sl, :] = (p_refs[a][2 * my_chip + c, sl, :].astype(F32)
                                        + from_sib[a][my_chip, sl, :].astype(F32))

        for k in range(n_fc):
            w_ref, own_ref, arr_ref, diag_ref, m_ref, v_ref = fc_in[6 * k:6 * k + 6]
            g = own_ref[...]
            for r in range(2):
                g = g + arr_ref[r].astype(F32)
            g = g + diag_ref[...].astype(F32)
            fc_out[4 * k][...] = g
            fc_out[4 * k + 1][...], fc_out[4 * k + 2][...], fc_out[4 * k + 3][...] = _adam(
                w_ref[...], g, m_ref[...], v_ref[...])

        @pl.when(step == TAIL_STEPS - 1)
        def _():
            for r in range(3):
                half_from_chip(r, 0).wait_recv()
                half_from_chip(r, 1).wait_recv()
            half_start = pl.multiple_of(PACK_HALF * c, 8)
            total_scr[pl.ds(half_start, PACK_FINE), :] = ((fine[0] + fine[1]) + fine[2]) + fine[3]
            total_scr[pl.ds(half_start + PACK_FINE, PACK_HALF - PACK_FINE), :] = (
                (bulk[0].astype(F32) + bulk[1].astype(F32)) + bulk[2].astype(F32)) + bulk[3].astype(F32)
            total_to_sibling().start()
            for a in range(n):
                rows = p_refs[a].shape[1]
                for r in range(3):
                    to_chip(a, r).wait_recv()
                    for s in range(0, rows, row_chunk):
                        sl = pl.ds(s, row_chunk)
                        g_refs[a][sl, :] = g_refs[a][sl, :] + chip_in[a][r, sl, :].astype(F32)
            total_from_sibling().wait_recv()
            total_ref[...] = total_scr[...]
            for a in range(n):
                all_from_sibling(a).wait_send()
                for r in range(3):
                    to_chip(a, r).wait_send()
            pack_to_sibling().wait_send()
            for r in range(3):
                half_to_chip(r, 0).wait_send()
                half_to_chip(r, 1).wait_send()
            total_to_sibling().wait_send()

    fc_specs_in, fc_specs_out, fc_shapes, fc_args = [], [], [], []
    for w, own, arrived, diagonal, m, v in fc:
        rows, cols = w.shape
        blk = pl.BlockSpec((rows // TAIL_STEPS, cols), lambda i: (i, 0))
        fc_specs_in += [blk, blk, pl.BlockSpec((2, rows // TAIL_STEPS, cols), lambda i: (0, i, 0)), blk, blk, blk]
        fc_specs_out += [blk] * 4
        fc_shapes += [jax.ShapeDtypeStruct((rows, cols), F32)] * 4
        fc_args += [w, own, arrived, diagonal, m, v]
    outs = pl.pallas_call(
        body, name="tail_comm", grid=(TAIL_STEPS,),
        out_shape=tuple([jax.ShapeDtypeStruct(p.shape[1:], F32) for p in parts]
                        + [jax.ShapeDtypeStruct((PACK_ROWS, D), F32)] + fc_shapes),
        in_specs=[_resident(p.shape) for p in parts] + [_resident(small.shape)] + fc_specs_in,
        out_specs=tuple([_full(p.shape[1:]) for p in parts] + [_full((PACK_ROWS, D))] + fc_specs_out),
        scratch_shapes=(
            [pltpu.VMEM((4,) + p.shape[1:], BF16) for p in parts]
            + [pltpu.VMEM((3,) + p.shape[1:], BF16) for p in parts]
            + [pltpu.VMEM((3,) + p.shape[1:], BF16) for p in parts]
            + [pltpu.VMEM((PACK_ROWS, D), F32), pltpu.VMEM((PACK_ROWS, D), F32),
               pltpu.VMEM((4, PACK_FINE, D), F32), pltpu.VMEM((4, PACK_HALF - PACK_FINE, D), BF16),
               pltpu.VMEM((PACK_ROWS, D), F32)]
            + [pltpu.SemaphoreType.DMA((n,)), pltpu.SemaphoreType.DMA((n,)),
               pltpu.SemaphoreType.DMA((3 * n,)), pltpu.SemaphoreType.DMA((3 * n,)),
               pltpu.SemaphoreType.DMA((8,)), pltpu.SemaphoreType.DMA((8,))]),
        compiler_params=pltpu.CompilerParams(dimension_semantics=("arbitrary",), vmem_limit_bytes=VMEM_LIMIT),
    )(*parts, small, *fc_args)
    return outs[:n + 1], [outs[n + 1 + 4 * k:n + 5 + 4 * k] for k in range(n_fc)]


def _tril_mask():
    row = lax.broadcasted_iota(jnp.int32, (CHUNK, CHUNK), 0)
    col = lax.broadcasted_iota(jnp.int32, (CHUNK, CHUNK), 1)
    return (col <= row).astype(F32)


def _window_sums(ext):
    s2 = ext + pltpu.roll(ext, 1, 0)
    t4 = s2[:, GROUP:]
    s4 = t4 + pltpu.roll(t4, 2, 0)
    t8 = s4[:, GROUP:]
    s8 = t8 + pltpu.roll(t8, 4, 0)
    t16 = s8[:, GROUP:]
    s16 = t16 + pltpu.roll(t16, 8, 0)
    return [s2[:, :GROUP], s4[:, :GROUP], s8[:, :GROUP], s16]


def _inv_counts(first_pos, rows):
    pos = first_pos + lax.broadcasted_iota(jnp.int32, (rows, 1), 0)
    return [1.0 / jnp.minimum(pos + 1, w).astype(F32) for w in WINDOWS]


def _pool_diff(zb, halo, first_pos):
    tt = zb.shape[0]
    sums = _window_sums(jnp.concatenate([halo, zb], axis=0))
    inv = _inv_counts(first_pos, tt)
    return [sums[g][HALO:, :] * inv[g] - zb[:, g * GROUP:(g + 1) * GROUP] for g in range(len(WINDOWS))]


def _row_blocks(scr, rows, place):
    def block(rel):
        start = pl.multiple_of(rows * _index(_peer(*place, rel)), rows)
        return scr.at[pl.ds(start, rows), :]

    def entries(shard_ref):
        return ([(shard_ref, block(0), rel) for rel in (1, 2, 4, 6)]
                + [(block(rel), block(rel), 1) for rel in (2, 4, 6)])
    return block, entries


def _attn_fwd(tt, x, c8, w_ada, b_my, n1pre, n1post, w_in_shard, w_out_shard, w_sp, bs_rows, ln_g, ln_b, w_pool,
              b_pool, pool_scale, fc_shards, n2pre):
    t_len = x.shape[0]
    nt = t_len // tt
    ncol = w_ada.shape[1]

    def body(x_ref, xb_ref, c_ref, wada_ref, b_ref, n1pre_ref, n1post_ref, wi_ref, wo_ref, wsp_ref, bs_ref,
             lng_ref, lnb_ref, wp_ref, bp_ref, ps_ref, w1_ref, w2_ref, n2pre_ref,
             z_ref, cat_ref, mix_ref, x1_ref, h2_ref, r_ref, f_ref, mod_out, sc_out, e1_ref, e2_ref, wout_ref,
             win_out, carry, land1, land2, sib1, sib2, cat_keep, wout_scr, win_ref, mod_ref, cg, mg, part,
             send_sems, recv_sems, local_sems, wo_send, wo_recv, wi_send, wi_recv, ada_send, ada_recv):
        i = pl.program_id(0)
        place = px, py, pc = _place()
        me = _index(place)
        wo_block, wo_entries = _row_blocks(wout_scr, w_out_shard.shape[0], place)
        wi_block, wi_entries = _row_blocks(win_ref, w_in_shard.shape[0], place)
        wo_copies = _Copies(wo_entries(wo_ref), wo_send, wo_recv)
        wi_copies = _Copies(wi_entries(wi_ref), wi_send, wi_recv)
        wo_keep = pltpu.make_async_copy(wout_scr, wout_ref, local_sems.at[8])
        wi_keep = pltpu.make_async_copy(win_ref, win_out, local_sems.at[9])
        ada = _Copies([(c_ref, cg.at[me], k) for k in range(1, N_DEV)]
                      + [(part, mg.at[me], k) for k in range(1, N_DEV)], ada_send, ada_recv)
        copies = _Copies(
            [(w1_ref, sib1, 1), (w2_ref, sib2, 1),
             (w1_ref, land1.at[0], 2), (w2_ref, land2.at[0], 2),
             (w1_ref, land1.at[1], 4), (w2_ref, land2.at[1], 4),
             (land1.at[0], e1_ref.at[3], 1), (land2.at[0], e2_ref.at[3], 1),
             (land1.at[1], e1_ref.at[5], 1), (land2.at[1], e2_ref.at[5], 1)],
            send_sems, recv_sems)
        keep = [pltpu.make_async_copy(w1_ref, e1_ref.at[0], local_sems.at[0]),
                pltpu.make_async_copy(w2_ref, e2_ref.at[0], local_sems.at[1]),
                pltpu.make_async_copy(land1.at[0], e1_ref.at[2], local_sems.at[2]),
                pltpu.make_async_copy(land1.at[1], e1_ref.at[4], local_sems.at[3]),
                pltpu.make_async_copy(land2.at[0], e2_ref.at[2], local_sems.at[4]),
                pltpu.make_async_copy(land2.at[1], e2_ref.at[4], local_sems.at[5]),
                pltpu.make_async_copy(sib1, e1_ref.at[1], local_sems.at[6]),
                pltpu.make_async_copy(sib2, e2_ref.at[1], local_sems.at[7])]

        @pl.when(i == 0)
        def _():
            ada.start(*range(N_DEV - 1))
            wi_copies.start(0, 1, 2, 3)
            wo_copies.start(0, 1, 2, 3)
            copies.start(0, 1, 2, 4, 3, 5)
            keep[0].start()
            keep[1].start()
            cg[me] = c_ref[...]
            wi_rows, wo_rows = w_in_shard.shape[0], w_out_shard.shape[0]
            win_ref[pl.ds(pl.multiple_of(wi_rows * me, wi_rows), wi_rows), :] = wi_ref[...]
            wout_scr[pl.ds(pl.multiple_of(wo_rows * me, wo_rows), wo_rows), :] = wo_ref[...]
            carry[...] = jnp.zeros_like(carry)

            ada.wait_recv(*range(N_DEV - 1))
            c_all = jnp.concatenate([cg[j, 0:1, :] for j in range(N_DEV)], axis=0)
            sc = c_all * jax.nn.sigmoid(c_all)
            sc_out[...] = sc
            part[...] = _dot(sc.astype(BF16), wada_ref[...].astype(BF16)) + b_ref[...]
            ada.start(*range(N_DEV - 1, 2 * (N_DEV - 1)))
            mg[me] = part[...]
            ada.wait_recv(*range(N_DEV - 1, 2 * (N_DEV - 1)))
            mod_ref[...] = jnp.zeros_like(mod_ref)
            for j in range(N_DEV):
                for m in range(6):
                    lo, hi = max(ncol * j, D * m), min(ncol * (j + 1), D * (m + 1))
                    if lo < hi:
                        mod_ref[m:m + 1, lo - D * m:hi - D * m] = mg[j, pl.ds(me, 1), lo - ncol * j:hi - ncol * j]
            mod_out[...] = mod_ref[...]

            wi_copies.wait_recv(1, 2, 3)
            wi_copies.start(4, 5, 6)
            wi_copies.wait_recv(0, 4, 5, 6)
            wi_keep.start()

        @pl.when(i == nt // 2 + ATTN_LAG)
        def _():
            copies.wait_recv(2, 4)
            copies.start(6, 8)
            keep[2].start()
            keep[3].start()

        @pl.when(i == nt - 1 + ATTN_LAG)
        def _():
            copies.wait_recv(3, 5)
            copies.start(7, 9)
            keep[4].start()
            keep[5].start()

        shift1, scale1, gate1 = mod_ref[0:1, :], mod_ref[1:2, :], mod_ref[2:3, :]

        @pl.when(i < nt)
        def _():
            xv = x_ref[...]
            h1 = (xv * _rstd(xv)) * (n1pre_ref[...] * (1.0 + scale1)) + shift1
            z = _dot_nt(h1.astype(BF16), win_ref[...])
            z_ref[...] = z

            _, ga = _gelu_parts(z[:, :2 * D_A])
            u, vr = ga[:, :D_A], ga[:, D_A:]
            dv = vr - jnp.mean(vr, axis=-1, keepdims=True)
            v = (dv * lax.rsqrt(jnp.mean(dv * dv, axis=-1, keepdims=True) + EPS)) * lng_ref[...] + lnb_ref[...]
            vb = v.astype(BF16)
            mask = _tril_mask()
            wc = [(wsp_ref[h] * mask).astype(BF16) for h in range(N_HEADS)]
            for ch in range(tt // CHUNK):
                rows = slice(ch * CHUNK, (ch + 1) * CHUNK)
                for h in range(N_HEADS):
                    cols = slice(h * GROUP, (h + 1) * GROUP)
                    mixed = _dot(wc[h], vb[rows, cols]) + bs_ref[:, cols]
                    cat_ref[rows, cols] = (u[rows, cols] * mixed).astype(BF16)

            zb = z[:, 2 * D_A:]
            diff = _pool_diff(zb, carry[...], i * tt)
            carry[...] = zb[tt - HALO:, :]
            for g in range(len(WINDOWS)):
                cols = slice(g * GROUP, (g + 1) * GROUP)
                pre = _dot(diff[g].astype(BF16), wp_ref[g].astype(BF16)) + bp_ref[:, cols]
                cat_ref[:, D_A + g * GROUP:D_A + (g + 1) * GROUP] = (pre * ps_ref[:, cols]).astype(BF16)
            cat_keep[i % (ATTN_LAG + 1)] = cat_ref[...]

        @pl.when(i == 0)
        def _():
            copies.wait_recv(0, 1)
            keep[6].start()
            keep[7].start()

        @pl.when(i == 1)
        def _():
            wo_copies.wait_recv(1, 2, 3)
            wo_copies.start(4, 5, 6)

        @pl.when(i == ATTN_LAG)
        def _():
            wo_copies.wait_recv(0, 4, 5, 6)
            wo_keep.start()

        @pl.when(i >= ATTN_LAG)
        def _():
            xv = xb_ref[...]
            mix = _dot(cat_keep[(i - ATTN_LAG) % (ATTN_LAG + 1)], wout_scr[...])
            mix_ref[...] = mix
            x1v = xv + (mix * _rstd(mix)) * (gate1 * n1post_ref[...])
            x1_ref[...] = x1v
            shift2, scale2 = mod_ref[3:4, :], mod_ref[4:5, :]
            h2 = ((x1v * _rstd(x1v)) * (n2pre_ref[...] * (1.0 + scale2)) + shift2).astype(BF16)
            h2_ref[...] = h2
            for j, (w1, w2) in enumerate(((w1_ref, w2_ref), (sib1, sib2))):
                ra = jnp.maximum(_dot(h2, w1[...]), 0.0)
                r = (ra * ra).astype(BF16)
                r_ref[:, j * FF_BLK:(j + 1) * FF_BLK] = r
                if j == 0:
                    f_ref[...] = _dot(r, w2[...])
                else:
                    f_ref[...] += _dot(r, w2[...])

        @pl.when(i == nt - 1 + ATTN_LAG)
        def _():
            copies.wait_recv(6, 7, 8, 9)
            copies.wait_send(*range(10))
            wo_copies.wait_send(*range(7))
            wi_copies.wait_send(*range(7))
            ada.wait_send(*range(2 * (N_DEV - 1)))
            for cp in keep:
                cp.wait()
            wo_keep.wait()
            wi_keep.wait()

    first = lambda w: pl.BlockSpec((tt, w), lambda i: (jnp.minimum(i, nt - 1), 0))
    second = lambda w: pl.BlockSpec((tt, w), lambda i: (jnp.maximum(i - ATTN_LAG, 0), 0))
    r_head = pl.BlockSpec((tt, FC_HEAD * FF_BLK),
                          lambda i: (jnp.maximum(i - ATTN_LAG, 0), R_HEAD_COLS // (FC_HEAD * FF_BLK)))
    hbm = pl.BlockSpec(memory_space=pl.ANY)
    outs = pl.pallas_call(
        body, name="attn_fwd", grid=(nt + ATTN_LAG,),
        out_shape=tuple([jax.ShapeDtypeStruct((t_len, D_Z), F32), jax.ShapeDtypeStruct((t_len, D), BF16),
                         jax.ShapeDtypeStruct((t_len, D), F32), jax.ShapeDtypeStruct((t_len, D), F32),
                         jax.ShapeDtypeStruct((t_len, D), BF16),
                         jax.ShapeDtypeStruct((t_len, FC_EARLY * FF_BLK), BF16),
                         jax.ShapeDtypeStruct((t_len, D), F32)]
                        + [jax.ShapeDtypeStruct((8, D), F32), jax.ShapeDtypeStruct((N_DEV, D), F32)]
                        + [jax.ShapeDtypeStruct((FC_EARLY,) + s.shape, BF16) for s in fc_shards]
                        + [jax.ShapeDtypeStruct((D, D), BF16), jax.ShapeDtypeStruct((D_Z, D), BF16)]),
        in_specs=[first(D), second(D), _full((8, D)), _resident(w_ada.shape), _full((1, ncol)), _full((1, D)),
                  _full((1, D)), _resident(w_in_shard.shape), _resident(w_out_shard.shape),
                  _full((N_HEADS, CHUNK, CHUNK)), _full((CHUNK, D_A)), _full((1, D_A)), _full((1, D_A)),
                  _full((len(WINDOWS), GROUP, GROUP)), _full((1, D_B)), _full((1, D_B)),
                  _resident(fc_shards[0].shape), _resident(fc_shards[1].shape), _full((1, D))],
        out_specs=(first(D_Z), first(D), second(D), second(D), second(D), r_head, second(D),
                   _full((8, D)), _full((N_DEV, D)), hbm, hbm, hbm, hbm),
        scratch_shapes=[pltpu.VMEM((HALO, D_B), F32),
                        pltpu.VMEM((2,) + fc_shards[0].shape, BF16), pltpu.VMEM((2,) + fc_shards[1].shape, BF16),
                        pltpu.VMEM(fc_shards[0].shape, BF16), pltpu.VMEM(fc_shards[1].shape, BF16),
                        pltpu.VMEM((ATTN_LAG + 1, tt, D), BF16), pltpu.VMEM((D, D), BF16),
                        pltpu.VMEM((D_Z, D), BF16), pltpu.VMEM((8, D), F32),
                        pltpu.VMEM((N_DEV, 8, D), F32), pltpu.VMEM((N_DEV, N_DEV, ncol), F32),
                        pltpu.VMEM((N_DEV, ncol), F32),
                        pltpu.SemaphoreType.DMA((10,)), pltpu.SemaphoreType.DMA((10,)),
                        pltpu.SemaphoreType.DMA((10,)),
                        pltpu.SemaphoreType.DMA((7,)), pltpu.SemaphoreType.DMA((7,)),
                        pltpu.SemaphoreType.DMA((7,)), pltpu.SemaphoreType.DMA((7,)),
                        pltpu.SemaphoreType.DMA((2 * (N_DEV - 1),)), pltpu.SemaphoreType.DMA((2 * (N_DEV - 1),))],
        compiler_params=pltpu.CompilerParams(dimension_semantics=("arbitrary",), vmem_limit_bytes=VMEM_LIMIT),
    )(x, x, c8, w_ada, b_my, n1pre, n1post, w_in_shard, w_out_shard, w_sp, bs_rows, ln_g, ln_b, w_pool, b_pool,
      pool_scale, *fc_shards, n2pre)
    return outs[:9], outs[9:11], outs[11], outs[12]


def _mlp_fwd_early(tt, r_begun, h2, f_head, w1_early, w2_early):
    t_len = h2.shape[0]
    nt = t_len // tt
    n_late = N_DEV - FC_EARLY

    def body(r_begun_ref, h2_ref, fh_ref, w1_ref, w2_ref, r_ref, f_ref, l1_ref, l2_ref,
             land1, land2, send_sems, recv_sems, local_sems):
        i = pl.program_id(0)
        copies = _Copies(
            [(w1_ref.at[2], land1, 4), (w2_ref.at[4], land2, 2),
             (land1, l1_ref.at[1], 1), (land2, l2_ref.at[1], 1)],
            send_sems, recv_sems)
        keep = [pltpu.make_async_copy(land1, l1_ref.at[0], local_sems.at[0]),
                pltpu.make_async_copy(land2, l2_ref.at[0], local_sems.at[1])]

        @pl.when(i == 0)
        def _():
            copies.start(0, 1)

        @pl.when(i == nt - 1)
        def _():
            copies.wait_recv(0, 1)
            copies.start(2, 3)
            for cp in keep:
                cp.start()

        h2 = h2_ref[...]
        f_ref[...] = fh_ref[...]
        for j in range(FC_HEAD, FC_EARLY):
            ra = jnp.maximum(_dot(h2, w1_ref[j]), 0.0)
            r = (ra * ra).astype(BF16)
            r_ref[:, _early_col(j):_early_col(j) + FF_BLK] = r
            f_ref[...] += _dot(r, w2_ref[j])

        @pl.when(i == nt - 1)
        def _():
            copies.wait_recv(2, 3)
            copies.wait_send(0, 1, 2, 3)
            for cp in keep:
                cp.wait()

    tile = lambda w: pl.BlockSpec((tt, w), lambda i: (i, 0))
    hbm = pl.BlockSpec(memory_space=pl.ANY)
    outs = pl.pallas_call(
        body, name="mlp_fwd_early", grid=(nt,),
        out_shape=(jax.ShapeDtypeStruct((t_len, FC_EARLY * FF_BLK), BF16), jax.ShapeDtypeStruct((t_len, D), F32),
                   jax.ShapeDtypeStruct((n_late,) + w1_early.shape[1:], BF16),
                   jax.ShapeDtypeStruct((n_late,) + w2_early.shape[1:], BF16)),
        in_specs=[hbm, tile(D), tile(D), _resident((FC_EARLY, D, FF_BLK)), _resident((FC_EARLY, FF_BLK, D))],
        out_specs=(tile(R_HEAD_COLS), tile(D), hbm, hbm),
        input_output_aliases={0: 0},
        scratch_shapes=[pltpu.VMEM(w1_early.shape[1:], BF16), pltpu.VMEM(w2_early.shape[1:], BF16),
                        pltpu.SemaphoreType.DMA((4,)), pltpu.SemaphoreType.DMA((4,)),
                        pltpu.SemaphoreType.DMA((2,))],
        compiler_params=pltpu.CompilerParams(dimension_semantics=("arbitrary",), vmem_limit_bytes=VMEM_LIMIT),
    )(r_begun, h2, f_head, w1_early, w2_early)
    return outs[:2], outs[2:]


def _mlp_late_bwd(tt, r_early, x1, h2, f_early, tgt, mix, mod, n2pre, n2post, n1post,
                  w1_early, w2_early, w1_late, w2_late):
    t_len = x1.shape[0]
    nt = t_len // tt
    n_late = N_DEV - FC_EARLY
    late_cols = n_late * FF_BLK

    def body(re_ref, x1_ref, h2_ref, fe_ref, tgt_ref, mix_ref, mod_ref, n2pre_ref, n2post_ref,
             n1post_ref, w1e_ref, w2e_ref, w1l_ref, w2l_ref,
             rl_ref, df_ref, da_ref, dmix_ref, dx1_ref, redf_ref, redb_ref, dh2_acc):
        i = pl.program_id(0)

        @pl.when(i == 0)
        def _():
            redf_ref[...] = jnp.zeros_like(redf_ref)
            redb_ref[...] = jnp.zeros_like(redb_ref)

        x1v = x1_ref[...]
        gate1, scale2, gate2 = mod_ref[2:3, :], mod_ref[4:5, :], mod_ref[5:6, :]
        h2 = h2_ref[...]
        f = fe_ref[...]
        for j in range(n_late):
            cols = slice(j * FF_BLK, (j + 1) * FF_BLK)
            ra = jnp.maximum(_dot(h2, w1l_ref[j]), 0.0)
            r = (ra * ra).astype(BF16)
            rl_ref[:, cols] = r
            f = f + _dot(r, w2l_ref[j])
        post2 = n2post_ref[...]
        gate_post2 = gate2 * post2
        rf = _rstd(f)
        fhat = f * rf
        err = (x1v + fhat * gate_post2) - tgt_ref[...]
        dy = err * (1.0 / D)
        d_f, sum_f = _rms_bwd_gained(dy, gate_post2, fhat, rf)
        dfv = d_f.astype(BF16)
        df_ref[...] = dfv
        redf_ref[0:1, :] += post2 * sum_f
        redf_ref[1:2, :] += gate2 * sum_f
        redf_ref[2:3, :] += _colsum(err * err)

        for j in range(N_DEV):
            cols = slice(j * FF_BLK, (j + 1) * FF_BLK)
            if j < FC_EARLY:
                w1, w2, r = w1e_ref[j], w2e_ref[j], re_ref[:, _early_col(j):_early_col(j) + FF_BLK]
            else:
                jl = j - FC_EARLY
                w1, w2, r = w1l_ref[jl], w2l_ref[jl], rl_ref[:, jl * FF_BLK:(jl + 1) * FF_BLK]
            dr = _dot_nt(dfv, w2)
            da = (dr * (2.0 * jnp.sqrt(r.astype(F32)))).astype(BF16)
            da_ref[:, cols] = da
            contrib = _dot_nt(da, w1)
            if j == 0:
                dh2_acc[...] = contrib
            else:
                dh2_acc[...] += contrib
        dh2 = dh2_acc[...]
        pre2, post1 = n2pre_ref[...], n1post_ref[...]
        r2 = _rstd(x1v)
        xhat = x1v * r2
        d_x1, sum_h = _rms_bwd_gained(dh2, pre2 * (1.0 + scale2), xhat, r2)
        dx1 = dy + d_x1
        dx1_ref[...] = dx1
        mixv = mix_ref[...]
        rm = _rstd(mixv)
        mhat = mixv * rm
        d_mix, sum_m = _rms_bwd_gained(dx1, gate1 * post1, mhat, rm)
        dmix_ref[...] = d_mix.astype(BF16)
        redb_ref[0:1, :] += _colsum(dh2)
        redb_ref[1:2, :] += pre2 * sum_h
        redb_ref[2:3, :] += (1.0 + scale2) * sum_h
        redb_ref[3:4, :] += post1 * sum_m
        redb_ref[4:5, :] += gate1 * sum_m

    tile = lambda w: pl.BlockSpec((tt, w), lambda i: (i, 0))
    return pl.pallas_call(
        body, name="mlp_late_bwd", grid=(nt,),
        out_shape=(jax.ShapeDtypeStruct((t_len, late_cols), BF16), jax.ShapeDtypeStruct((t_len, D), BF16),
                   jax.ShapeDtypeStruct((t_len, D_FF), BF16), jax.ShapeDtypeStruct((t_len, D), BF16),
                   jax.ShapeDtypeStruct((t_len, D), F32), jax.ShapeDtypeStruct((8, D), F32),
                   jax.ShapeDtypeStruct((8, D), F32)),
        in_specs=[tile(FC_EARLY * FF_BLK), tile(D), tile(D), tile(D), tile(D),
                  tile(D), _full((8, D)), _full((1, D)), _full((1, D)), _full((1, D)),
                  _resident((FC_EARLY, D, FF_BLK)), _resident((FC_EARLY, FF_BLK, D)),
                  _resident((n_late, D, FF_BLK)), _resident((n_late, FF_BLK, D))],
        out_specs=(tile(late_cols), tile(D), tile(D_FF), tile(D), tile(D), _full((8, D)), _full((8, D))),
        scratch_shapes=[pltpu.VMEM((tt, D), F32)],
        compiler_params=pltpu.CompilerParams(dimension_semantics=("arbitrary",), vmem_limit_bytes=VMEM_LIMIT),
    )(r_early, x1, h2, f_early, tgt, mix, mod, n2pre, n2post, n1post, w1_early, w2_early, w1_late, w2_late)


def _mlp_wgrad(tt, r_early, r_late, da, df, h2):
    t_len = df.shape[0]
    nt = t_len // tt
    odd_steps = [j for j, rel in enumerate(WGRAD_ORDER) if rel % 2]

    def relation(j):
        rel = jnp.int32(WGRAD_ORDER[-1])
        for step in range(N_DEV - 2, -1, -1):
            rel = jnp.where(j == step, WGRAD_ORDER[step], rel)
        return rel

    def body(re_ref, rl_ref, da_ref, df_ref, h2_ref, own1_ref, own2_ref, out1_ref, out2_ref, diag1_ref, diag2_ref,
             acc1, acc2, snd1, snd2, sib1, sib2, dsnd1, dsnd2, send_sems, recv_sems):
        j, t = pl.program_id(0), pl.program_id(1)
        rows = pl.ds(pl.multiple_of(t * tt, tt), tt)
        x, y, c = _place()
        accs, snds, sibs = (acc1, acc2), (snd1, snd2), (sib1, sib2)
        dsnds, diags = (dsnd1, dsnd2), (diag1_ref, diag2_ref)

        def to_sibling(a, jj, buf=0):
            return pltpu.make_async_remote_copy(
                src_ref=snds[a].at[buf], dst_ref=sibs[a].at[jj],
                send_sem=send_sems.at[4 * a + jj], recv_sem=recv_sems.at[4 * a + jj],
                device_id=(x, y, 1 - c), device_id_type=MESH)

        def to_diagonal(a):
            return pltpu.make_async_remote_copy(
                src_ref=dsnds[a], dst_ref=diags[a], send_sem=send_sems.at[8 + a], recv_sem=recv_sems.at[8 + a],
                device_id=_peer(x, y, c, 6), device_id_type=MESH)

        @pl.when(t == 0)
        def _():
            acc2[...] = jnp.zeros_like(acc2)
            acc1[...] = jnp.zeros_like(acc1)

        for r_ref, mine in ((re_ref, relation(j) < FC_EARLY), (rl_ref, relation(j) >= FC_EARLY)):
            @pl.when(mine)
            def _():
                acc2[...] += _dot_tn(r_ref[...], df_ref[rows, :])
                acc1[...] += _dot_tn(h2_ref[rows, :], da_ref[...])

        for step, rel in enumerate(WGRAD_ORDER):
            jj = rel // 2

            @pl.when((t == nt - 1) & (j == step))
            def _():
                for a, (own_ref, out_ref) in enumerate(((own1_ref, out1_ref), (own2_ref, out2_ref))):
                    if rel % 2:
                        q = odd_steps.index(step)
                        if q >= 2:
                            to_sibling(a, WGRAD_ORDER[odd_steps[q - 2]] // 2).wait_send()
                        snds[a][q % 2] = accs[a][...].astype(BF16)
                        to_sibling(a, jj, q % 2).start()
                        continue
                    to_sibling(a, jj).wait_recv()
                    chip_sum = accs[a][...] + sibs[a][jj].astype(F32)
                    if rel == 6:
                        dsnds[a][...] = chip_sum.astype(BF16)
                        to_diagonal(a).start()
                    elif rel == 0:
                        own_ref[...] = chip_sum
                    else:
                        out_ref[0] = chip_sum.astype(BF16)
                    if step == N_DEV - 1:
                        for q in (2, 3):
                            to_sibling(a, WGRAD_ORDER[odd_steps[q]] // 2).wait_send()
                        to_diagonal(a).wait_recv()
                        to_diagonal(a).wait_send()

    assert WGRAD_ORDER[-1] == 0 and WGRAD_ORDER[-3:-1] == (2, 4)
    blk = pl.BlockSpec((tt, FF_BLK), lambda j, t: (t, relation(j)))
    early_block = lambda rel: jnp.where(rel < FC_HEAD, rel + FC_EARLY - FC_HEAD, rel - FC_HEAD)
    early = lambda j, t: (jnp.where(relation(j) < FC_EARLY, t, 0),
                          jnp.where(relation(j) < FC_EARLY, early_block(relation(j)), 0))
    late = lambda j, t: (jnp.where(relation(j) < FC_EARLY, 0, t), jnp.maximum(relation(j) - FC_EARLY, 0))
    chip = lambda j, t: (jnp.clip(j - 5, 0, 1), 0, 0)
    hbm = pl.BlockSpec(memory_space=pl.ANY)
    return pl.pallas_call(
        body, name="mlp_wgrad", grid=(N_DEV, nt),
        out_shape=(jax.ShapeDtypeStruct((D, FF_BLK), F32), jax.ShapeDtypeStruct((FF_BLK, D), F32),
                   jax.ShapeDtypeStruct((2, D, FF_BLK), BF16), jax.ShapeDtypeStruct((2, FF_BLK, D), BF16),
                   jax.ShapeDtypeStruct((D, FF_BLK), BF16), jax.ShapeDtypeStruct((FF_BLK, D), BF16)),
        in_specs=[pl.BlockSpec((tt, FF_BLK), early), pl.BlockSpec((tt, FF_BLK), late), blk,
                  _resident((t_len, D)), _resident((t_len, D))],
        out_specs=(_full((D, FF_BLK)), _full((FF_BLK, D)),
                   pl.BlockSpec((1, D, FF_BLK), chip), pl.BlockSpec((1, FF_BLK, D), chip), hbm, hbm),
        scratch_shapes=[pltpu.VMEM((D, FF_BLK), F32), pltpu.VMEM((FF_BLK, D), F32),
                        pltpu.VMEM((2, D, FF_BLK), BF16), pltpu.VMEM((2, FF_BLK, D), BF16),
                        pltpu.VMEM((4, D, FF_BLK), BF16), pltpu.VMEM((4, FF_BLK, D), BF16),
                        pltpu.VMEM((D, FF_BLK), BF16), pltpu.VMEM((FF_BLK, D), BF16),
                        pltpu.SemaphoreType.DMA((10,)), pltpu.SemaphoreType.DMA((10,))],
        compiler_params=pltpu.CompilerParams(dimension_semantics=("arbitrary", "arbitrary"),
                                             vmem_limit_bytes=VMEM_LIMIT),
    )(r_early, r_late, da, df, h2)


def _acc_rows(ref, row0, k, val):
    half = CHUNK // 2
    ref[row0:row0 + half, k * GROUP:(k + 1) * GROUP] += val[:half, :]
    ref[row0:row0 + half, D_A + k * GROUP:D_A + (k + 1) * GROUP] += val[half:, :]


def _attn_bwd(tt, dmix, dx1, x, z, cat, mod, n1pre, w_in_t, w_out, w_sp, bs_rows, ln_g, ln_b, w_pool, b_pool,
              pool_scale, red_fwd, red_bwd, chip_sums):
    t_len = x.shape[0]
    nt = t_len // tt
    hb = tt // HALO
    n_sums = len(chip_sums)

    def body(dmix_ref, dx1_ref, x_ref, z_ref, zprev_ref, cat_ref, mod_ref, n1pre_ref, win_ref, wout_ref, wsp_ref,
             bs_ref, lng_ref, lnb_ref, wp_ref, bp_ref, ps_ref, redf_ref, redb_ref, *rest):
        sum_out = rest[:n_sums]
        gx_ref, gwin_ref, gwout_ref, small_ref = rest[n_sums:n_sums + 4]
        sum_in = rest[n_sums + 4:2 * n_sums + 4]
        carry, acc_in, acc_out, dz_scr, bs_acc, send_sems, recv_sems = rest[2 * n_sums + 4:]
        s = pl.program_id(0)
        i = nt - 1 - s
        px, py, pc = _place()

        def chip_copy(a, r):
            return pltpu.make_async_remote_copy(
                src_ref=sum_out[a].at[r], dst_ref=sum_in[a].at[r],
                send_sem=send_sems.at[2 * a + r], recv_sem=recv_sems.at[2 * a + r],
                device_id=_peer(px, py, pc, 2 * (r + 1)), device_id_type=MESH)

        @pl.when(s == 0)
        def _():
            for a in range(n_sums):
                for r in range(2):
                    chip_copy(a, r).start()
            carry[...] = jnp.zeros_like(carry)
            acc_in[...] = jnp.zeros_like(acc_in)
            acc_out[...] = jnp.zeros_like(acc_out)
            bs_acc[...] = jnp.zeros_like(bs_acc)
            small_ref[...] = jnp.zeros_like(small_ref)
            small_ref[ROW_DMOD + 2:ROW_DMOD + 3, :] = redb_ref[3:4, :]
            small_ref[ROW_DMOD + 3:ROW_DMOD + 5, :] = redb_ref[0:2, :]
            small_ref[ROW_DMOD + 5:ROW_DMOD + 6, :] = redf_ref[0:1, :]
            small_ref[ROW_N1POST:ROW_N1POST + 1, :] = redb_ref[4:5, :]
            small_ref[ROW_N2PRE:ROW_N2PRE + 1, :] = redb_ref[2:3, :]
            small_ref[ROW_N2POST:ROW_N2POST + 1, :] = redf_ref[1:2, :]
            small_ref[ROW_LOSS:ROW_LOSS + 1, :] = redf_ref[2:3, :]

        dmixv = dmix_ref[...]
        dcat = _dot_nt(dmixv, wout_ref[...])
        acc_out[...] += _dot_tn(cat_ref[...], dmixv)

        z = z_ref[...]
        t_g, ga = _gelu_parts(z[:, :2 * D_A])
        u, vr = ga[:, :D_A], ga[:, D_A:]
        dv0 = vr - jnp.mean(vr, axis=-1, keepdims=True)
        rv = lax.rsqrt(jnp.mean(dv0 * dv0, axis=-1, keepdims=True) + EPS)
        vhat = dv0 * rv
        vb = (vhat * lng_ref[...] + lnb_ref[...]).astype(BF16)
        mask = _tril_mask()
        wc = [(wsp_ref[h] * mask).astype(BF16) for h in range(N_HEADS)]

        dya = dcat[:, :D_A]
        for h in range(N_HEADS):
            cols = slice(h * GROUP, (h + 1) * GROUP)
            bs_sum = jnp.zeros((CHUNK, GROUP), F32)
            ws_sum = jnp.zeros((CHUNK, CHUNK), F32)
            for ch in range(tt // CHUNK):
                rows = slice(ch * CHUNK, (ch + 1) * CHUNK)
                v_ch = vb[rows, cols]
                mixed = _dot(wc[h], v_ch) + bs_ref[:, cols]
                dy_ch = dya[rows, cols]
                dz_scr[rows, cols] = dy_ch * mixed
                dmixed = dy_ch * u[rows, cols]
                dmb = dmixed.astype(BF16)
                dz_scr[rows, D_A + h * GROUP:D_A + (h + 1) * GROUP] = _dot_tn(wc[h], dmb)
                bs_sum = bs_sum + dmixed
                ws_sum = ws_sum + _dot_nt(dmb, v_ch)
            _acc_rows(bs_acc, 0, h, bs_sum)
            _acc_rows(small_ref, ROW_WS, h, ws_sum)

        dvl = dz_scr[:, D_A:2 * D_A]
        dvhat = dvl * lng_ref[...]
        dvl_vhat = dvl * vhat
        dvr = rv * (dvhat - jnp.mean(dvhat, axis=-1, keepdims=True)
                    - vhat * jnp.mean(dvl_vhat * lng_ref[...], axis=-1, keepdims=True))
        small_ref[ROW_LN:ROW_LN + 1, 0:D_A] += _colsum(dvl_vhat)
        small_ref[ROW_LN:ROW_LN + 1, D_A:D] += _colsum(dvl)
        dga = jnp.concatenate([dz_scr[:, :D_A], dvr], axis=1)
        dza = dga * _gelu_grad(z[:, :2 * D_A], t_g)

        zb = z[:, 2 * D_A:]
        halo_prev = jnp.where(i == 0, 0.0, zprev_ref[...])
        diff = _pool_diff(zb, halo_prev, i * tt)
        dyb = dcat[:, D_A:]
        inv = _inv_counts(i * tt, tt)
        scaled, ddiffs = [], []
        for g in range(len(WINDOWS)):
            cols = slice(g * GROUP, (g + 1) * GROUP)
            db = diff[g].astype(BF16)
            wpg = wp_ref[g].astype(BF16)
            pre = _dot(db, wpg) + bp_ref[:, cols]
            small_ref[ROW_POOL:ROW_POOL + 1, cols] += _colsum(dyb[:, cols] * pre)
            dpre = dyb[:, cols] * ps_ref[:, cols]
            small_ref[ROW_POOL:ROW_POOL + 1, D_B + g * GROUP:D_B + (g + 1) * GROUP] += _colsum(dpre)
            dpb = dpre.astype(BF16)
            _acc_rows(small_ref, ROW_WP, g, _dot_tn(db, dpb))
            ddiff = _dot_nt(dpb, wpg)
            ddiffs.append(ddiff)
            scaled.append(ddiff * inv[g])
        scaled_all = jnp.concatenate(scaled, axis=1)
        ext = jnp.concatenate([scaled_all, carry[...]], axis=0)
        n_ext = tt + HALO
        s2 = ext + pltpu.roll(ext, n_ext - 1, 0)
        t4 = s2[:, GROUP:]
        s4 = t4 + pltpu.roll(t4, n_ext - 2, 0)
        t8 = s4[:, GROUP:]
        s8 = t8 + pltpu.roll(t8, n_ext - 4, 0)
        t16 = s8[:, GROUP:]
        s16 = t16 + pltpu.roll(t16, n_ext - 8, 0)
        back = [s2[:, :GROUP], s4[:, :GROUP], s8[:, :GROUP], s16]
        carry[...] = scaled_all[:HALO, :]
        dzb = jnp.concatenate([back[g][:tt, :] - ddiffs[g] for g in range(len(WINDOWS))], axis=1)

        dzv = jnp.concatenate([dza, dzb], axis=1).astype(BF16)
        dh1 = _dot(dzv, win_ref[...])
        xv = x_ref[...]
        r1 = _rstd(xv)
        xhat = xv * r1
        shift1, scale1 = mod_ref[0:1, :], mod_ref[1:2, :]
        pre1 = n1pre_ref[...]
        gain1 = pre1 * (1.0 + scale1)
        h1 = (xhat * gain1 + shift1).astype(BF16)
        acc_in[...] += _dot_tn(dzv, h1)
        d_x, sum_h = _rms_bwd_gained(dh1, gain1, xhat, r1)
        gx_ref[...] = dx1_ref[...] + d_x
        small_ref[ROW_DMOD:ROW_DMOD + 1, :] += _colsum(dh1)
        small_ref[ROW_DMOD + 1:ROW_DMOD + 2, :] += pre1 * sum_h
        small_ref[ROW_N1PRE:ROW_N1PRE + 1, :] += (1.0 + scale1) * sum_h

        @pl.when(s == nt - 1)
        def _():
            gwin_ref[...] = acc_in[...].astype(BF16)
            gwout_ref[...] = acc_out[...].astype(BF16)
            bs = _unfold(bs_acc[...])
            for h in range(N_HEADS):
                small_ref[ROW_BS + h:ROW_BS + h + 1, 0:GROUP] = jnp.sum(
                    bs[:, h * GROUP:(h + 1) * GROUP].T, axis=0, keepdims=True)
            for a in range(n_sums):
                for r in range(2):
                    chip_copy(a, r).wait_recv()
                    chip_copy(a, r).wait_send()

    rev = lambda w: pl.BlockSpec((tt, w), lambda s: (nt - 1 - s, 0))
    zprev = pl.BlockSpec((HALO, D_B), lambda s: (jnp.maximum((nt - 1 - s) * hb - 1, 0), 2))
    hbm = pl.BlockSpec(memory_space=pl.ANY)
    outs = pl.pallas_call(
        body, name="attn_bwd", grid=(nt,),
        out_shape=tuple([jax.ShapeDtypeStruct((t_len, D), F32), jax.ShapeDtypeStruct((D_Z, D), BF16),
                         jax.ShapeDtypeStruct((D, D), BF16), jax.ShapeDtypeStruct((SMALL_ROWS, D), F32)]
                        + [jax.ShapeDtypeStruct(cs.shape, cs.dtype) for cs in chip_sums]),
        in_specs=[rev(D), rev(D), rev(D), rev(D_Z), zprev, rev(D), _full((8, D)), _full((1, D)),
                  _resident((D_Z, D)), _resident((D, D)), _full((N_HEADS, CHUNK, CHUNK)), _full((CHUNK, D_A)),
                  _full((1, D_A)), _full((1, D_A)), _full((len(WINDOWS), GROUP, GROUP)), _full((1, D_B)),
                  _full((1, D_B)), _full((8, D)), _full((8, D))] + [_resident(cs.shape) for cs in chip_sums],
        out_specs=tuple([rev(D), _resident((D_Z, D)), _resident((D, D)), _full((SMALL_ROWS, D))] + [hbm] * n_sums),
        scratch_shapes=[pltpu.VMEM((HALO, D_B), F32), pltpu.VMEM((D_Z, D), F32), pltpu.VMEM((D, D), F32),
                        pltpu.VMEM((tt, 2 * D_A), F32), pltpu.VMEM((CHUNK // 2, D), F32),
                        pltpu.SemaphoreType.DMA((2 * n_sums,)), pltpu.SemaphoreType.DMA((2 * n_sums,))],
        compiler_params=pltpu.CompilerParams(dimension_semantics=("arbitrary",), vmem_limit_bytes=VMEM_LIMIT),
    )(dmix, dx1, x, z, z, cat, mod, n1pre, w_in_t, w_out, w_sp, bs_rows, ln_g, ln_b, w_pool, b_pool, pool_scale,
      red_fwd, red_bwd, *chip_sums)
    return outs[:4], outs[4:]


def _adam(w, g, m, v):
    m2 = ADAM_B1 * m + (1.0 - ADAM_B1) * g
    v2 = ADAM_B2 * v + (1.0 - ADAM_B2) * (g * g)
    m_hat = m2 / (1.0 - ADAM_B1 ** ADAM_STEP)
    v_hat = v2 / (1.0 - ADAM_B2 ** ADAM_STEP)
    delta = -ADAM_LR * (m_hat / (jnp.sqrt(v_hat) + ADAM_EPS) + ADAM_WD * w)
    return delta, m2, v2


def _adamw_shard(name, rb, w, g, m, v):
    rows, cols = w.shape

    def body(w_ref, g_ref, m_ref, v_ref, d_ref, m2_ref, v2_ref):
        d_ref[...], m2_ref[...], v2_ref[...] = _adam(w_ref[...], g_ref[...], m_ref[...], v_ref[...])

    blk = pl.BlockSpec((rb, cols), lambda i: (i, 0))
    shp = jax.ShapeDtypeStruct((rows, cols), F32)
    return pl.pallas_call(
        body, name=name, grid=(rows // rb,), out_shape=(shp, shp, shp),
        in_specs=[blk] * 4, out_specs=(blk, blk, blk),
        compiler_params=pltpu.CompilerParams(dimension_semantics=("arbitrary",)),
    )(w, g, m, v)


def _adamw_ada(rb, w, sc, dmod_cols, m, v):
    rows, cols = w.shape

    def body(w_ref, sc_ref, dm_ref, m_ref, v_ref, g_ref, d_ref, m2_ref, v2_ref):
        g = _dot_tn(sc_ref[...].astype(BF16), dm_ref[...].astype(BF16))
        g_ref[...] = g
        d_ref[...], m2_ref[...], v2_ref[...] = _adam(w_ref[...], g, m_ref[...], v_ref[...])

    blk = pl.BlockSpec((rb, cols), lambda i: (i, 0))
    shp = jax.ShapeDtypeStruct((rows, cols), F32)
    return pl.pallas_call(
        body, name="adamw_ada", grid=(rows // rb,), out_shape=(shp, shp, shp, shp),
        in_specs=[blk, pl.BlockSpec((N_DEV, rb), lambda i: (0, i)), _full((N_DEV, cols)), blk, blk],
        out_specs=(blk, blk, blk, blk),
        compiler_params=pltpu.CompilerParams(dimension_semantics=("arbitrary",)),
    )(w, sc, dmod_cols, m, v)


def _unfold(acc_rows):
    return jnp.concatenate([acc_rows[:, :D_A], acc_rows[:, D_A:]], axis=0)


def _adamw_small(total, params):
    n = len(params)
    flat = [a for p in params for a in p]

    def body(*refs):
        s_ref = refs[0]
        p_refs = refs[1:1 + 3 * n]
        loss_ref = refs[1 + 3 * n]
        o_refs = refs[2 + 3 * n:]
        d_b_ada = s_ref[0:6, :]
        for b in range(1, N_DEV):
            d_b_ada = d_b_ada + s_ref[_table_row(b):_table_row(b) + 6, :]
        misc = lambda r: s_ref[PK_MISC + r - ROW_N1PRE:PK_MISC + r - ROW_N1PRE + 1, :]
        loss = jnp.sum(misc(ROW_LOSS), axis=-1, keepdims=True) * (0.5 / D)
        loss_ref[...] = jnp.broadcast_to(loss, (8, GROUP))
        mask = _tril_mask()
        ws = _unfold(s_ref[PK_WS:PK_WS + 64, :])
        wp = _unfold(s_ref[PK_WP:PK_WP + 64, :])
        grads = [
            d_b_ada,
            misc(ROW_N1PRE), misc(ROW_N1POST), misc(ROW_N2PRE), misc(ROW_N2POST),
            misc(ROW_LN)[:, :D_A], misc(ROW_LN)[:, D_A:],
            misc(ROW_POOL)[:, :D_B], misc(ROW_POOL)[:, D_B:],
            s_ref[PK_BS:PK_BS + N_HEADS, 0:GROUP],
            jnp.stack([ws[:, h * GROUP:(h + 1) * GROUP] * mask for h in range(N_HEADS)]),
            jnp.stack([wp[:, g * GROUP:(g + 1) * GROUP] for g in range(len(WINDOWS))]),
        ]
        for k in range(n):
            w_ref, m_ref, v_ref = p_refs[3 * k:3 * k + 3]
            g = grads[k]
            o_refs[4 * k][...] = g
            o_refs[4 * k + 1][...], o_refs[4 * k + 2][...], o_refs[4 * k + 3][...] = _adam(
                w_ref[...], g, m_ref[...], v_ref[...])

    vm = pl.BlockSpec(memory_space=pltpu.VMEM)
    out_shape = [jax.ShapeDtypeStruct((8, GROUP), F32)]
    for w, _, _ in params:
        out_shape += [jax.ShapeDtypeStruct(w.shape, F32)] * 4
    return pl.pallas_call(
        body, name="adamw_small", out_shape=tuple(out_shape),
        in_specs=[vm] * (1 + 3 * n), out_specs=tuple([vm] * len(out_shape)),
    )(total, *flat)


TT_ATTN_FWD = 512
ATTN_LAG = 2
TT_MLP_FWD = 512
TT_MLP = 256
TT_WGRAD = 2048
TT_ATTN_BWD = 512


def kernel(x, c, w_ada, b_ada, norm1_pre, norm1_post, w_in, w_spatial, b_spatial, ln_v_gain, ln_v_bias, w_pool, b_pool, pool_scale, w_out, norm2_pre, norm2_post, w_fc1, w_fc2, loss_target, m_w_ada, m_b_ada, m_norm1_pre, m_norm1_post, m_w_in, m_w_spatial, m_b_spatial, m_ln_v_gain, m_ln_v_bias, m_w_pool, m_b_pool, m_pool_scale, m_w_out, m_norm2_pre, m_norm2_post, m_w_fc1, m_w_fc2, v_w_ada, v_b_ada, v_norm1_pre, v_norm1_post, v_w_in, v_w_spatial, v_b_spatial, v_ln_v_gain, v_ln_v_bias, v_w_pool, v_b_pool, v_pool_scale, v_w_out, v_norm2_pre, v_norm2_post, v_w_fc1, v_w_fc2):
    t_len = x.shape[1]
    me = 4 * lax.axis_index("x") + 2 * lax.axis_index("y") + lax.axis_index("c")
    ada_cols = w_ada.shape[1]
    tt = lambda want: min(want, t_len)

    x2 = x.reshape(t_len, D)
    tgt = loss_target.reshape(t_len, D)
    row = lambda a: a.reshape(1, -1)

    b_my = lax.dynamic_slice_in_dim(b_ada, me * ada_cols, ada_cols).reshape(1, ada_cols)
    w_in_shard, w_out_shard, w1_shard, w2_shard = _cast_shards([w_in.T, w_out, w_fc1, w_fc2])

    bs_rows = jnp.repeat(b_spatial.T, GROUP, axis=1)
    attn_consts = (w_spatial, bs_rows, row(ln_v_gain), row(ln_v_bias), w_pool, row(b_pool), row(pool_scale))

    (z, cat, mix, x1, h2, r_begun, f_head, mod, sc), (w1_early, w2_early), w_out_all, w_in_t = _attn_fwd(
        tt(TT_ATTN_FWD), x2, jnp.broadcast_to(c, (8, D)), w_ada, b_my, row(norm1_pre), row(norm1_post),
        w_in_shard, w_out_shard, *attn_consts, (w1_shard, w2_shard), row(norm2_pre))
    (r_early, f_early), (w1_late, w2_late) = _mlp_fwd_early(
        tt(TT_MLP_FWD), r_begun, h2, f_head, w1_early, w2_early)
    r_late, df, da, dmix, dx1, red_fwd, red_bwd = _mlp_late_bwd(
        tt(TT_MLP), r_early, x1, h2, f_early, tgt, mix, mod, row(norm2_pre), row(norm2_post), row(norm1_post),
        w1_early, w2_early, w1_late, w2_late)
    own_w1, own_w2, sums_w1, sums_w2, diag_w1, diag_w2 = _mlp_wgrad(tt(TT_WGRAD), r_early, r_late, da, df, h2)
    (grad_x, p_in, p_out, small), (arr_w1, arr_w2) = _attn_bwd(
        tt(TT_ATTN_BWD), dmix, dx1, x2, z, cat, mod, row(norm1_pre), w_in_t, w_out_all, *attn_consts,
        red_fwd, red_bwd, [sums_w1, sums_w2])
    (grad_in_t, grad_out, total), ((grad_w1, d_w1, m_w1, v_w1), (grad_w2, d_w2, m_w2, v_w2)) = _tail_comm(
        [p_in.reshape(N_DEV, D_Z // N_DEV, D), p_out.reshape(N_DEV, D // N_DEV, D)], small, 64,
        [(w_fc1, own_w1, arr_w1, diag_w1, m_w_fc1, v_w_fc1), (w_fc2, own_w2, arr_w2, diag_w2, m_w_fc2, v_w_fc2)])

    d_out, m_out, v_out = _adamw_shard("adamw_out", 128, w_out, grad_out, m_w_out, v_w_out)
    d_in_t, m_in_t, v_in_t = _adamw_shard("adamw_in", D_Z // N_DEV, w_in.T, grad_in_t, m_w_in.T, v_w_in.T)
    table = jnp.concatenate([total[0:PACK_FINE, :], total[PK_TABLE_B:PK_MISC, :]], axis=0)
    dmod_all = table.reshape(N_DEV, 8, D)[:, :6, :].reshape(N_DEV, 6 * D)
    dmod_cols = lax.dynamic_slice_in_dim(dmod_all, me * ada_cols, ada_cols, axis=1)
    grad_ada, d_ada, m_ada, v_ada = _adamw_ada(256, w_ada, sc, dmod_cols, m_w_ada, v_w_ada)

    six = lambda a: a.reshape(6, D)
    small_params = [
        (six(b_ada), six(m_b_ada), six(v_b_ada)),
        (row(norm1_pre), row(m_norm1_pre), row(v_norm1_pre)),
        (row(norm1_post), row(m_norm1_post), row(v_norm1_post)),
        (row(norm2_pre), row(m_norm2_pre), row(v_norm2_pre)),
        (row(norm2_post), row(m_norm2_post), row(v_norm2_post)),
        (row(ln_v_gain), row(m_ln_v_gain), row(v_ln_v_gain)),
        (row(ln_v_bias), row(m_ln_v_bias), row(v_ln_v_bias)),
        (row(pool_scale), row(m_pool_scale), row(v_pool_scale)),
        (row(b_pool), row(m_b_pool), row(v_b_pool)),
        (b_spatial, m_b_spatial, v_b_spatial),
        (w_spatial, m_w_spatial, v_w_spatial),
        (w_pool, m_w_pool, v_w_pool),
    ]
    outs = _adamw_small(total, small_params)
    loss = outs[0][0, 0]
    names = ["b_ada", "norm1_pre", "norm1_post", "norm2_pre", "norm2_post", "ln_v_gain", "ln_v_bias", "pool_scale",
             "b_pool", "b_spatial", "w_spatial", "w_pool"]
    shapes = dict(b_ada=b_ada.shape, norm1_pre=norm1_pre.shape, norm1_post=norm1_post.shape,
                  norm2_pre=norm2_pre.shape, norm2_post=norm2_post.shape, ln_v_gain=ln_v_gain.shape,
                  ln_v_bias=ln_v_bias.shape, pool_scale=pool_scale.shape, b_pool=b_pool.shape,
                  b_spatial=b_spatial.shape, w_spatial=w_spatial.shape, w_pool=w_pool.shape)
    res = {}
    for k, nm in enumerate(names):
        res[nm] = tuple(o.reshape(shapes[nm]) for o in outs[1 + 4 * k:5 + 4 * k])
    res["w_ada"] = (grad_ada, d_ada, m_ada, v_ada)
    res["w_in"] = (grad_in_t.T, d_in_t.T, m_in_t.T, v_in_t.T)
    res["w_out"] = (grad_out, d_out, m_out, v_out)
    res["w_fc1"] = (grad_w1, d_w1, m_w1, v_w1)
    res["w_fc2"] = (grad_w2, d_w2, m_w2, v_w2)

    order = ["w_ada", "b_ada", "norm1_pre", "norm1_post", "w_in", "w_spatial", "b_spatial", "ln_v_gain", "ln_v_bias",
             "w_pool", "b_pool", "pool_scale", "w_out", "norm2_pre", "norm2_post", "w_fc1", "w_fc2"]
    return (loss, grad_x.reshape(x.shape),
            *[res[nm][0] for nm in order], *[res[nm][1] for nm in order],
            *[res[nm][2] for nm in order], *[res[nm][3] for nm in order])
```

```python
import functools

import jax
import jax.numpy as jnp
from jax import lax
from jax.experimental import pallas as pl
from jax.experimental.pallas import tpu as pltpu

F32 = jnp.float32
BF16 = jnp.bfloat16
MESH = pl.DeviceIdType.MESH

N_DEV = 8
D = 1024
D_A = 512
D_B = 512
D_Z = 2 * D_A + D_B
N_HEADS = 4
CHUNK = 128
WINDOWS = (2, 4, 8, 16)
GROUP = 128
D_FF = 4096
FF_BLK = D_FF // N_DEV
HALO = 16
EPS = 1e-6
VMEM_LIMIT = 60 * 1024 * 1024

ADAM_LR = 0.001
ADAM_B1 = 0.9
ADAM_B2 = 0.999
ADAM_EPS = 1e-08
ADAM_WD = 0.01
ADAM_STEP = 10

ROW_DMOD = 0
ROW_N1PRE, ROW_N1POST, ROW_N2PRE, ROW_N2POST = 8, 9, 10, 11
ROW_LN = 12
ROW_POOL = 13
ROW_LOSS = 14
ROW_BS = 16
ROW_WS = 24
ROW_WP = 88
SMALL_ROWS = 152
PACK_FINE = 40
PACK_HALF = PACK_FINE + 64
PACK_ROWS = 2 * PACK_HALF
PK_WS = PACK_FINE
PK_TABLE_B = PACK_HALF
PK_MISC = PK_TABLE_B + 24
PK_BS = PK_MISC + 8
PK_WP = PK_BS + 8


def _table_row(b):
    if isinstance(b, int):
        return 8 * b if 8 * b < PACK_FINE else 8 * b + PK_TABLE_B - PACK_FINE
    return 8 * b + jnp.where(8 * b < PACK_FINE, 0, PK_TABLE_B - PACK_FINE)


def _dot(a, b):
    return jnp.dot(a, b, preferred_element_type=F32)


def _dot_nt(a, b):
    return lax.dot_general(a, b, (((1,), (1,)), ((), ())), preferred_element_type=F32)


def _dot_tn(a, b):
    return lax.dot_general(a, b, (((0,), (0,)), ((), ())), preferred_element_type=F32)


def _rstd(v):
    return lax.rsqrt(jnp.mean(v * v, axis=-1, keepdims=True) + EPS)


def _rms_bwd(d_hat, hat, rstd):
    return rstd * (d_hat - hat * jnp.mean(d_hat * hat, axis=-1, keepdims=True))


def _rms_bwd_gained(g, gain, hat, rstd):
    g_hat = g * hat
    d_v = rstd * (g * gain - hat * jnp.mean(g_hat * gain, axis=-1, keepdims=True))
    return d_v, _colsum(g_hat)


_K0 = 0.7978845608028654
_K1 = 0.044715


def _gelu_parts(v):
    t = jnp.tanh(v * (_K0 + (_K0 * _K1) * (v * v)))
    return t, v * (0.5 + 0.5 * t)


def _gelu_grad(v, t):
    return (0.5 + 0.5 * t) + (0.5 * v) * (1.0 - t * t) * (_K0 + (3.0 * _K0 * _K1) * (v * v))


def _colsum(v):
    return jnp.sum(v, axis=0, keepdims=True)


def _full(shape):
    n = len(shape)
    return pl.BlockSpec(shape, lambda *_: (0,) * n)


def _resident(shape):
    n = len(shape)
    return pl.BlockSpec(shape, lambda *_: (0,) * n, pipeline_mode=pl.Buffered(1))


def _place():
    x, y, c = lax.axis_index("x"), lax.axis_index("y"), lax.axis_index("c")
    return x, y, c


def _flip(v, bit):
    return 1 - v if bit else v


def _peer(x, y, c, k):
    return (_flip(x, (k >> 2) & 1), _flip(y, (k >> 1) & 1), _flip(c, k & 1))


def _index(p):
    return 4 * p[0] + 2 * p[1] + p[2]


def _cast_shards(shards):
    def body(*refs):
        for src, dst in zip(refs[:len(shards)], refs[len(shards):]):
            dst[...] = src[...].astype(BF16)

    vm = pl.BlockSpec(memory_space=pltpu.VMEM)
    return pl.pallas_call(
        body, name="cast_shards", out_shape=tuple(jax.ShapeDtypeStruct(s.shape, BF16) for s in shards),
        in_specs=[vm] * len(shards), out_specs=tuple([vm] * len(shards)),
    )(*shards)


FC_EARLY = 6
FC_HEAD = 2
R_HEAD_COLS = (FC_EARLY - FC_HEAD) * FF_BLK
WGRAD_ORDER = (7, 6, 1, 3, 5, 2, 4, 0)


def _early_col(j):
    return R_HEAD_COLS + j * FF_BLK if j < FC_HEAD else (j - FC_HEAD) * FF_BLK


class _Copies:
    def __init__(self, entries, send_sems, recv_sems):
        self.place = _place()
        self.entries, self.send_sems, self.recv_sems = entries, send_sems, recv_sems

    def _copy(self, i, arrival=False):
        src, dst, rel = self.entries[i]
        return pltpu.make_async_remote_copy(
            src_ref=dst if arrival else src, dst_ref=dst, send_sem=self.send_sems.at[i],
            recv_sem=self.recv_sems.at[i], device_id=_peer(*self.place, rel), device_id_type=MESH)

    def start(self, *which):
        for i in which:
            self._copy(i).start()

    def wait_recv(self, *which):
        for i in which:
            self._copy(i, arrival=True).wait_recv()

    def wait_send(self, *which):
        for i in which:
            self._copy(i).wait_send()


TAIL_STEPS = 8


def _tail_comm(parts, small, row_chunk, fc):
    n, n_fc = len(parts), len(fc)

    def body(*refs):
        p_refs, small_ref = refs[:n], refs[n]
        fc_in = refs[n + 1:n + 1 + 6 * n_fc]
        outs = refs[n + 1 + 6 * n_fc:]
        g_refs, total_ref = outs[:n], outs[n]
        fc_out = outs[n + 1:n + 1 + 4 * n_fc]
        scr = outs[n + 1 + 4 * n_fc:]
        from_sib = scr[0:n]
        chip_out = scr[n:2 * n]
        chip_in = scr[2 * n:3 * n]
        pack, pack_sib, fine, bulk, total_scr = scr[3 * n:3 * n + 5]
        send_a, recv_a, send_b, recv_b, send_s, recv_s = scr[3 * n + 5:]
        step = pl.program_id(0)
        x, y, c = _place()
        me = _index((x, y, c))
        sibling = (x, y, 1 - c)
        my_chip = 2 * x + y
        others = [(1 - x, y), (x, 1 - y), (1 - x, 1 - y)]
        my_half = pl.ds(pl.multiple_of(PACK_HALF * c, 8), PACK_HALF)

        def pack_to_sibling():
            return pltpu.make_async_remote_copy(
                src_ref=pack, dst_ref=pack_sib, send_sem=send_s.at[0], recv_sem=recv_s.at[0],
                device_id=sibling, device_id_type=MESH)

        def half_to_chip(r, part):
            buf = (fine, bulk)[part]
            return pltpu.make_async_remote_copy(
                src_ref=buf.at[my_chip], dst_ref=buf.at[my_chip],
                send_sem=send_s.at[1 + 3 * part + r], recv_sem=recv_s.at[1 + 3 * part + r],
                device_id=(*others[r], c), device_id_type=MESH)

        def half_from_chip(r, part):
            k = 2 * others[r][0] + others[r][1]
            buf = (fine, bulk)[part]
            return pltpu.make_async_remote_copy(
                src_ref=buf.at[k], dst_ref=buf.at[k],
                send_sem=send_s.at[1 + 3 * part + r], recv_sem=recv_s.at[1 + 3 * part + r],
                device_id=(*others[r], c), device_id_type=MESH)

        def total_to_sibling():
            return pltpu.make_async_remote_copy(
                src_ref=total_scr.at[my_half], dst_ref=total_scr.at[my_half],
                send_sem=send_s.at[7], recv_sem=recv_s.at[7], device_id=sibling, device_id_type=MESH)

        def total_from_sibling():
            sib_half = pl.ds(pl.multiple_of(PACK_HALF * (1 - c), 8), PACK_HALF)
            return pltpu.make_async_remote_copy(
                src_ref=total_scr.at[sib_half], dst_ref=total_scr.at[sib_half],
                send_sem=send_s.at[7], recv_sem=recv_s.at[7], device_id=sibling, device_id_type=MESH)

        def to_sibling(a, k):
            return pltpu.make_async_remote_copy(
                src_ref=p_refs[a].at[2 * k + (1 - c)], dst_ref=from_sib[a].at[k],
                send_sem=send_a.at[a], recv_sem=recv_a.at[a], device_id=sibling, device_id_type=MESH)

        def all_from_sibling(a):
            return pltpu.make_async_remote_copy(
                src_ref=from_sib[a], dst_ref=from_sib[a], send_sem=send_a.at[a], recv_sem=recv_a.at[a],
                device_id=sibling, device_id_type=MESH)

        def to_chip(a, r):
            return pltpu.make_async_remote_copy(
                src_ref=chip_out[a].at[r], dst_ref=chip_in[a].at[r],
                send_sem=send_b.at[3 * a + r], recv_sem=recv_b.at[3 * a + r],
                device_id=(*others[r], c), device_id_type=MESH)

        @pl.when(step == 0)
        def _():
            pack[0:PACK_FINE, :] = jnp.zeros((PACK_FINE, D), F32)
            pack[PK_TABLE_B:PK_MISC, :] = jnp.zeros((PK_MISC - PK_TABLE_B, D), F32)
            pack[pl.ds(pl.multiple_of(_table_row(me), 8), 8), :] = small_ref[0:8, :]
            pack[PK_WS:PK_WS + 64, :] = small_ref[ROW_WS:ROW_WS + 64, :]
            pack[PK_MISC:PK_MISC + 8, :] = small_ref[ROW_N1PRE:ROW_N1PRE + 8, :]
            pack[PK_BS:PK_BS + 8, :] = small_ref[ROW_BS:ROW_BS + 8, :]
            pack[PK_WP:PK_WP + 64, :] = small_ref[ROW_WP:ROW_WP + 64, :]
            pack_to_sibling().start()
            for a in range(n):
                for k in range(4):
                    to_sibling(a, k).start()

        @pl.when(step == 1)
        def _():
            pack_to_sibling().wait_recv()
            chip_sum = pack[my_half, :] + pack_sib[my_half, :]
            fine[my_chip] = chip_sum[:PACK_FINE, :]
            bulk[my_chip] = chip_sum[PACK_FINE:, :].astype(BF16)
            for r in range(3):
                half_to_chip(r, 0).start()
                half_to_chip(r, 1).start()
            for a in range(n):
                all_from_sibling(a).wait_recv()
                rows = p_refs[a].shape[1]
                for r in range(3):
                    k = 2 * others[r][0] + others[r][1]
                    for s in range(0, rows, row_chunk):
                        sl = pl.ds(s, row_chunk)
                        chip_out[a][r, sl, :] = (p_refs[a][2 * k + c, sl, :].astype(F32)
                                                 + from_sib[a][k, sl, :].astype(F32)).astype(BF16)
                    to_chip(a, r).start()
                for s in range(0, rows, row_chunk):
                    sl = pl.ds(s, row_chunk)
                    g_refs[a][sl, :] = (p_refs[a][2 * my_chip + c, sl, :].astype(F32)
                                        + from_sib[a][my_chip, sl, :].astype(F32))

        for k in range(n_fc):
            w_ref, own_ref, arr_ref, diag_ref, m_ref, v_ref = fc_in[6 * k:6 * k + 6]
            g = own_ref[...]
            for r in range(2):
                g = g + arr_ref[r].astype(F32)
            g = g + diag_ref[...].astype(F32)
            fc_out[4 * k][...] = g
            fc_out[4 * k + 1][...], fc_out[4 * k + 2][...], fc_out[4 * k + 3][...] = _adam(
                w_ref[...], g, m_ref[...], v_ref[...])

        @pl.when(step == TAIL_STEPS - 1)
        def _():
            for r in range(3):
                half_from_chip(r, 0).wait_recv()
                half_from_chip(r, 1).wait_recv()
            half_start = pl.multiple_of(PACK_HALF * c, 8)
            total_scr[pl.ds(half_start, PACK_FINE), :] = ((fine[0] + fine[1]) + fine[2]) + fine[3]
            total_scr[pl.ds(half_start + PACK_FINE, PACK_HALF - PACK_FINE), :] = (
                (bulk[0].astype(F32) + bulk[1].astype(F32)) + bulk[2].astype(F32)) + bulk[3].astype(F32)
            total_to_sibling().start()
            for a in range(n):
                rows = p_refs[a].shape[1]
                for r in range(3):
                    to_chip(a, r).wait_recv()
                    for s in range(0, rows, row_chunk):
                        sl = pl.ds(s, row_chunk)
                        g_refs[a][sl, :] = g_refs[a][sl, :] + chip_in[a][r, sl, :].astype(F32)
            total_from_sibling().wait_recv()
            total_ref[...] = total_scr[...]
            for a in range(n):
                all_from_sibling(a).wait_send()
                for r in range(3):
                    to_chip(a, r).wait_send()
            pack_to_sibling().wait_send()
            for r in range(3):
                half_to_chip(r, 0).wait_send()
                half_to_chip(r, 1).wait_send()
            total_to_sibling().wait_send()

    fc_specs_in, fc_specs_out, fc_shapes, fc_args = [], [], [], []
    for w, own, arrived, diagonal, m, v in fc:
        rows, cols = w.shape
        blk = pl.BlockSpec((rows // TAIL_STEPS, cols), lambda i: (i, 0))
        fc_specs_in += [blk, blk, pl.BlockSpec((2, rows // TAIL_STEPS, cols), lambda i: (0, i, 0)), blk, blk, blk]
        fc_specs_out += [blk] * 4
        fc_shapes += [jax.ShapeDtypeStruct((rows, cols), F32)] * 4
        fc_args += [w, own, arrived, diagonal, m, v]
    outs = pl.pallas_call(
        body, name="tail_comm", grid=(TAIL_STEPS,),
        out_shape=tuple([jax.ShapeDtypeStruct(p.shape[1:], F32) for p in parts]
                        + [jax.ShapeDtypeStruct((PACK_ROWS, D), F32)] + fc_shapes),
        in_specs=[_resident(p.shape) for p in parts] + [_resident(small.shape)] + fc_specs_in,
        out_specs=tuple([_full(p.shape[1:]) for p in parts] + [_full((PACK_ROWS, D))] + fc_specs_out),
        scratch_shapes=(
            [pltpu.VMEM((4,) + p.shape[1:], BF16) for p in parts]
            + [pltpu.VMEM((3,) + p.shape[1:], BF16) for p in parts]
            + [pltpu.VMEM((3,) + p.shape[1:], BF16) for p in parts]
            + [pltpu.VMEM((PACK_ROWS, D), F32), pltpu.VMEM((PACK_ROWS, D), F32),
               pltpu.VMEM((4, PACK_FINE, D), F32), pltpu.VMEM((4, PACK_HALF - PACK_FINE, D), BF16),
               pltpu.VMEM((PACK_ROWS, D), F32)]
            + [pltpu.SemaphoreType.DMA((n,)), pltpu.SemaphoreType.DMA((n,)),
               pltpu.SemaphoreType.DMA((3 * n,)), pltpu.SemaphoreType.DMA((3 * n,)),
               pltpu.SemaphoreType.DMA((8,)), pltpu.SemaphoreType.DMA((8,))]),
        compiler_params=pltpu.CompilerParams(dimension_semantics=("arbitrary",), vmem_limit_bytes=VMEM_LIMIT),
    )(*parts, small, *fc_args)
    return outs[:n + 1], [outs[n + 1 + 4 * k:n + 5 + 4 * k] for k in range(n_fc)]


def _tril_mask():
    row = lax.broadcasted_iota(jnp.int32, (CHUNK, CHUNK), 0)
    col = lax.broadcasted_iota(jnp.int32, (CHUNK, CHUNK), 1)
    return (col <= row).astype(F32)


def _window_sums(ext):
    s2 = ext + pltpu.roll(ext, 1, 0)
    t4 = s2[:, GROUP:]
    s4 = t4 + pltpu.roll(t4, 2, 0)
    t8 = s4[:, GROUP:]
    s8 = t8 + pltpu.roll(t8, 4, 0)
    t16 = s8[:, GROUP:]
    s16 = t16 + pltpu.roll(t16, 8, 0)
    return [s2[:, :GROUP], s4[:, :GROUP], s8[:, :GROUP], s16]


def _inv_counts(first_pos, rows):
    pos = first_pos + lax.broadcasted_iota(jnp.int32, (rows, 1), 0)
    return [1.0 / jnp.minimum(pos + 1, w).astype(F32) for w in WINDOWS]


def _pool_diff(zb, halo, first_pos):
    tt = zb.shape[0]
    sums = _window_sums(jnp.concatenate([halo, zb], axis=0))
    inv = _inv_counts(first_pos, tt)
    return [sums[g][HALO:, :] * inv[g] - zb[:, g * GROUP:(g + 1) * GROUP] for g in range(len(WINDOWS))]


def _row_blocks(scr, rows, place):
    def block(rel):
        start = pl.multiple_of(rows * _index(_peer(*place, rel)), rows)
        return scr.at[pl.ds(start, rows), :]

    def entries(shard_ref):
        return ([(shard_ref, block(0), rel) for rel in (1, 2, 4, 6)]
                + [(block(rel), block(rel), 1) for rel in (2, 4, 6)])
    return block, entries


def _attn_fwd(tt, x, c8, w_ada, b_my, n1pre, n1post, w_in_shard, w_out_shard, w_sp, bs_rows, ln_g, ln_b, w_pool,
              b_pool, pool_scale, fc_shards, n2pre):
    t_len = x.shape[0]
    nt = t_len // tt
    ncol = w_ada.shape[1]

    def body(x_ref, xb_ref, c_ref, wada_ref, b_ref, n1pre_ref, n1post_ref, wi_ref, wo_ref, wsp_ref, bs_ref,
             lng_ref, lnb_ref, wp_ref, bp_ref, ps_ref, w1_ref, w2_ref, n2pre_ref,
             z_ref, cat_ref, mix_ref, x1_ref, h2_ref, r_ref, f_ref, mod_out, sc_out, e1_ref, e2_ref, wout_ref,
             win_out, carry, land1, land2, sib1, sib2, cat_keep, wout_scr, win_ref, mod_ref, cg, mg, part,
             send_sems, recv_sems, local_sems, wo_send, wo_recv, wi_send, wi_recv, ada_send, ada_recv):
        i = pl.program_id(0)
        place = px, py, pc = _place()
        me = _index(place)
        wo_block, wo_entries = _row_blocks(wout_scr, w_out_shard.shape[0], place)
        wi_block, wi_entries = _row_blocks(win_ref, w_in_shard.shape[0], place)
        wo_copies = _Copies(wo_entries(wo_ref), wo_send, wo_recv)
        wi_copies = _Copies(wi_entries(wi_ref), wi_send, wi_recv)
        wo_keep = pltpu.make_async_copy(wout_scr, wout_ref, local_sems.at[8])
        wi_keep = pltpu.make_async_copy(win_ref, win_out, local_sems.at[9])
        ada = _Copies([(c_ref, cg.at[me], k) for k in range(1, N_DEV)]
                      + [(part, mg.at[me], k) for k in range(1, N_DEV)], ada_send, ada_recv)
        copies = _Copies(
            [(w1_ref, sib1, 1), (w2_ref, sib2, 1),
             (w1_ref, land1.at[0], 2), (w2_ref, land2.at[0], 2),
             (w1_ref, land1.at[1], 4), (w2_ref, land2.at[1], 4),
             (land1.at[0], e1_ref.at[3], 1), (land2.at[0], e2_ref.at[3], 1),
             (land1.at[1], e1_ref.at[5], 1), (land2.at[1], e2_ref.at[5], 1)],
            send_sems, recv_sems)
        keep = [pltpu.make_async_copy(w1_ref, e1_ref.at[0], local_sems.at[0]),
                pltpu.make_async_copy(w2_ref, e2_ref.at[0], local_sems.at[1]),
                pltpu.make_async_copy(land1.at[0], e1_ref.at[2], local_sems.at[2]),
                pltpu.make_async_copy(land1.at[1], e1_ref.at[4], local_sems.at[3]),
                pltpu.make_async_copy(land2.at[0], e2_ref.at[2], local_sems.at[4]),
                pltpu.make_async_copy(land2.at[1], e2_ref.at[4], local_sems.at[5]),
                pltpu.make_async_copy(sib1, e1_ref.at[1], local_sems.at[6]),
                pltpu.make_async_copy(sib2, e2_ref.at[1], local_sems.at[7])]

        @pl.when(i == 0)
        def _():
            ada.start(*range(N_DEV - 1))
            cg[me] = c_ref[...]
            wi_rows, wo_rows = w_in_shard.shape[0], w_out_shard.shape[0]
            win_ref[pl.ds(pl.multiple_of(wi_rows * me, wi_rows), wi_rows), :] = wi_ref[...]
            wout_scr[pl.ds(pl.multiple_of(wo_rows * me, wo_rows), wo_rows), :] = wo_ref[...]
            carry[...] = jnp.zeros_like(carry)

            ada.wait_recv(*range(N_DEV - 1))
            c_all = jnp.concatenate([cg[j, 0:1, :] for j in range(N_DEV)], axis=0)
            sc = c_all * jax.nn.sigmoid(c_all)
            sc_out[...] = sc
            part[...] = _dot(sc.astype(BF16), wada_ref[...].astype(BF16)) + b_ref[...]
            ada.start(*range(N_DEV - 1, 2 * (N_DEV - 1)))
            wi_copies.start(0, 1, 2, 3)
            wo_copies.start(0, 1, 2, 3)
            copies.start(0, 1, 2, 4, 3, 5)
            keep[0].start()
            keep[1].start()
            mg[me] = part[...]
            ada.wait_recv(*range(N_DEV - 1, 2 * (N_DEV - 1)))
            mod_ref[...] = jnp.zeros_like(mod_ref)
            for j in range(N_DEV):
                for m in range(6):
                    lo, hi = max(ncol * j, D * m), min(ncol * (j + 1), D * (m + 1))
                    if lo < hi:
                        mod_ref[m:m + 1, lo - D * m:hi - D * m] = mg[j, pl.ds(me, 1), lo - ncol * j:hi - ncol * j]
            mod_out[...] = mod_ref[...]

            wi_copies.wait_recv(1, 2, 3)
            wi_copies.start(4, 5, 6)
            wi_copies.wait_recv(0, 4, 5, 6)
            wi_keep.start()

        @pl.when(i == nt // 2 + ATTN_LAG)
        def _():
            copies.wait_recv(2, 4)
            copies.start(6, 8)
            keep[2].start()
            keep[3].start()

        @pl.when(i == nt - 1 + ATTN_LAG)
        def _():
            copies.wait_recv(3, 5)
            copies.start(7, 9)
            keep[4].start()
            keep[5].start()

        shift1, scale1, gate1 = mod_ref[0:1, :], mod_ref[1:2, :], mod_ref[2:3, :]

        @pl.when(i < nt)
        def _():
            xv = x_ref[...]
            h1 = (xv * _rstd(xv)) * (n1pre_ref[...] * (1.0 + scale1)) + shift1
            z = _dot_nt(h1.astype(BF16), win_ref[...])
            z_ref[...] = z

            _, ga = _gelu_parts(z[:, :2 * D_A])
            u, vr = ga[:, :D_A], ga[:, D_A:]
            dv = vr - jnp.mean(vr, axis=-1, keepdims=True)
            v = (dv * lax.rsqrt(jnp.mean(dv * dv, axis=-1, keepdims=True) + EPS)) * lng_ref[...] + lnb_ref[...]
            vb = v.astype(BF16)
            mask = _tril_mask()
            wc = [(wsp_ref[h] * mask).astype(BF16) for h in range(N_HEADS)]
            for ch in range(tt // CHUNK):
                rows = slice(ch * CHUNK, (ch + 1) * CHUNK)
                for h in range(N_HEADS):
                    cols = slice(h * GROUP, (h + 1) * GROUP)
                    mixed = _dot(wc[h], vb[rows, cols]) + bs_ref[:, cols]
                    cat_ref[rows, cols] = (u[rows, cols] * mixed).astype(BF16)

            zb = z[:, 2 * D_A:]
            diff = _pool_diff(zb, carry[...], i * tt)
            carry[...] = zb[tt - HALO:, :]
            for g in range(len(WINDOWS)):
                cols = slice(g * GROUP, (g + 1) * GROUP)
                pre = _dot(diff[g].astype(BF16), wp_ref[g].astype(BF16)) + bp_ref[:, cols]
                cat_ref[:, D_A + g * GROUP:D_A + (g + 1) * GROUP] = (pre * ps_ref[:, cols]).astype(BF16)
            cat_keep[i % (ATTN_LAG + 1)] = cat_ref[...]

        @pl.when(i == 0)
        def _():
            copies.wait_recv(0, 1)
            keep[6].start()
            keep[7].start()

        @pl.when(i == 1)
        def _():
            wo_copies.wait_recv(1, 2, 3)
            wo_copies.start(4, 5, 6)

        @pl.when(i == ATTN_LAG)
        def _():
            wo_copies.wait_recv(0, 4, 5, 6)
            wo_keep.start()

        @pl.when(i >= ATTN_LAG)
        def _():
            xv = xb_ref[...]
            mix = _dot(cat_keep[(i - ATTN_LAG) % (ATTN_LAG + 1)], wout_scr[...])
            mix_ref[...] = mix
            x1v = xv + (mix * _rstd(mix)) * (gate1 * n1post_ref[...])
            x1_ref[...] = x1v
            shift2, scale2 = mod_ref[3:4, :], mod_ref[4:5, :]
            h2 = ((x1v * _rstd(x1v)) * (n2pre_ref[...] * (1.0 + scale2)) + shift2).astype(BF16)
            h2_ref[...] = h2
            for j, (w1, w2) in enumerate(((w1_ref, w2_ref), (sib1, sib2))):
                ra = jnp.maximum(_dot(h2, w1[...]), 0.0)
                r = (ra * ra).astype(BF16)
                r_ref[:, j * FF_BLK:(j + 1) * FF_BLK] = r
                if j == 0:
                    f_ref[...] = _dot(r, w2[...])
                else:
                    f_ref[...] += _dot(r, w2[...])

        @pl.when(i == nt - 1 + ATTN_LAG)
        def _():
            copies.wait_recv(6, 7, 8, 9)
            copies.wait_send(*range(10))
            wo_copies.wait_send(*range(7))
            wi_copies.wait_send(*range(7))
            ada.wait_send(*range(2 * (N_DEV - 1)))
            for cp in keep:
                cp.wait()
            wo_keep.wait()
            wi_keep.wait()

    first = lambda w: pl.BlockSpec((tt, w), lambda i: (jnp.minimum(i, nt - 1), 0))
    second = lambda w: pl.BlockSpec((tt, w), lambda i: (jnp.maximum(i - ATTN_LAG, 0), 0))
    r_head = pl.BlockSpec((tt, FC_HEAD * FF_BLK),
                          lambda i: (jnp.maximum(i - ATTN_LAG, 0), R_HEAD_COLS // (FC_HEAD * FF_BLK)))
    hbm = pl.BlockSpec(memory_space=pl.ANY)
    outs = pl.pallas_call(
        body, name="attn_fwd", grid=(nt + ATTN_LAG,),
        out_shape=tuple([jax.ShapeDtypeStruct((t_len, D_Z), F32), jax.ShapeDtypeStruct((t_len, D), BF16),
                         jax.ShapeDtypeStruct((t_len, D), F32), jax.ShapeDtypeStruct((t_len, D), F32),
                         jax.ShapeDtypeStruct((t_len, D), BF16),
                         jax.ShapeDtypeStruct((t_len, FC_EARLY * FF_BLK), BF16),
                         jax.ShapeDtypeStruct((t_len, D), F32)]
                        + [jax.ShapeDtypeStruct((8, D), F32), jax.ShapeDtypeStruct((N_DEV, D), F32)]
                        + [jax.ShapeDtypeStruct((FC_EARLY,) + s.shape, BF16) for s in fc_shards]
                        + [jax.ShapeDtypeStruct((D, D), BF16), jax.ShapeDtypeStruct((D_Z, D), BF16)]),
        in_specs=[first(D), second(D), _full((8, D)), _resident(w_ada.shape), _full((1, ncol)), _full((1, D)),
                  _full((1, D)), _resident(w_in_shard.shape), _resident(w_out_shard.shape),
                  _full((N_HEADS, CHUNK, CHUNK)), _full((CHUNK, D_A)), _full((1, D_A)), _full((1, D_A)),
                  _full((len(WINDOWS), GROUP, GROUP)), _full((1, D_B)), _full((1, D_B)),
                  _resident(fc_shards[0].shape), _resident(fc_shards[1].shape), _full((1, D))],
        out_specs=(first(D_Z), first(D), second(D), second(D), second(D), r_head, second(D),
                   _full((8, D)), _full((N_DEV, D)), hbm, hbm, hbm, hbm),
        scratch_shapes=[pltpu.VMEM((HALO, D_B), F32),
                        pltpu.VMEM((2,) + fc_shards[0].shape, BF16), pltpu.VMEM((2,) + fc_shards[1].shape, BF16),
                        pltpu.VMEM(fc_shards[0].shape, BF16), pltpu.VMEM(fc_shards[1].shape, BF16),
                        pltpu.VMEM((ATTN_LAG + 1, tt, D), BF16), pltpu.VMEM((D, D), BF16),
                        pltpu.VMEM((D_Z, D), BF16), pltpu.VMEM((8, D), F32),
                        pltpu.VMEM((N_DEV, 8, D), F32), pltpu.VMEM((N_DEV, N_DEV, ncol), F32),
                        pltpu.VMEM((N_DEV, ncol), F32),
                        pltpu.SemaphoreType.DMA((10,)), pltpu.SemaphoreType.DMA((10,)),
                        pltpu.SemaphoreType.DMA((10,)),
                        pltpu.SemaphoreType.DMA((7,)), pltpu.SemaphoreType.DMA((7,)),
                        pltpu.SemaphoreType.DMA((7,)), pltpu.SemaphoreType.DMA((7,)),
                        pltpu.SemaphoreType.DMA((2 * (N_DEV - 1),)), pltpu.SemaphoreType.DMA((2 * (N_DEV - 1),))],
        compiler_params=pltpu.CompilerParams(dimension_semantics=("arbitrary",), vmem_limit_bytes=VMEM_LIMIT),
    )(x, x, c8, w_ada, b_my, n1pre, n1post, w_in_shard, w_out_shard, w_sp, bs_rows, ln_g, ln_b, w_pool, b_pool,
      pool_scale, *fc_shards, n2pre)
    return outs[:9], outs[9:11], outs[11], outs[12]


def _mlp_fwd_early(tt, r_begun, h2, f_head, w1_early, w2_early):
    t_len = h2.shape[0]
    nt = t_len // tt
    n_late = N_DEV - FC_EARLY

    def body(r_begun_ref, h2_ref, fh_ref, w1_ref, w2_ref, r_ref, f_ref, l1_ref, l2_ref,
             land1, land2, send_sems, recv_sems, local_sems):
        i = pl.program_id(0)
        copies = _Copies(
            [(w1_ref.at[2], land1, 4), (w2_ref.at[4], land2, 2),
             (land1, l1_ref.at[1], 1), (land2, l2_ref.at[1], 1)],
            send_sems, recv_sems)
        keep = [pltpu.make_async_copy(land1, l1_ref.at[0], local_sems.at[0]),
                pltpu.make_async_copy(land2, l2_ref.at[0], local_sems.at[1])]

        @pl.when(i == 0)
        def _():
            copies.start(0, 1)

        @pl.when(i == nt - 1)
        def _():
            copies.wait_recv(0, 1)
            copies.start(2, 3)
            for cp in keep:
                cp.start()

        h2 = h2_ref[...]
        f_ref[...] = fh_ref[...]
        for j in range(FC_HEAD, FC_EARLY):
            ra = jnp.maximum(_dot(h2, w1_ref[j]), 0.0)
            r = (ra * ra).astype(BF16)
            r_ref[:, _early_col(j):_early_col(j) + FF_BLK] = r
            f_ref[...] += _dot(r, w2_ref[j])

        @pl.when(i == nt - 1)
        def _():
            copies.wait_recv(2, 3)
            copies.wait_send(0, 1, 2, 3)
            for cp in keep:
                cp.wait()

    tile = lambda w: pl.BlockSpec((tt, w), lambda i: (i, 0))
    hbm = pl.BlockSpec(memory_space=pl.ANY)
    outs = pl.pallas_call(
        body, name="mlp_fwd_early", grid=(nt,),
        out_shape=(jax.ShapeDtypeStruct((t_len, FC_EARLY * FF_BLK), BF16), jax.ShapeDtypeStruct((t_len, D), F32),
                   jax.ShapeDtypeStruct((n_late,) + w1_early.shape[1:], BF16),
                   jax.ShapeDtypeStruct((n_late,) + w2_early.shape[1:], BF16)),
        in_specs=[hbm, tile(D), tile(D), _resident((FC_EARLY, D, FF_BLK)), _resident((FC_EARLY, FF_BLK, D))],
        out_specs=(tile(R_HEAD_COLS), tile(D), hbm, hbm),
        input_output_aliases={0: 0},
        scratch_shapes=[pltpu.VMEM(w1_early.shape[1:], BF16), pltpu.VMEM(w2_early.shape[1:], BF16),
                        pltpu.SemaphoreType.DMA((4,)), pltpu.SemaphoreType.DMA((4,)),
                        pltpu.SemaphoreType.DMA((2,))],
        compiler_params=pltpu.CompilerParams(dimension_semantics=("arbitrary",), vmem_limit_bytes=VMEM_LIMIT),
    )(r_begun, h2, f_head, w1_early, w2_early)
    return outs[:2], outs[2:]


def _mlp_late_bwd(tt, r_early, x1, h2, f_early, tgt, mix, mod, n2pre, n2post, n1post,
                  w1_early, w2_early, w1_late, w2_late):
    t_len = x1.shape[0]
    nt = t_len // tt
    n_late = N_DEV - FC_EARLY
    late_cols = n_late * FF_BLK

    def body(re_ref, x1_ref, h2_ref, fe_ref, tgt_ref, mix_ref, mod_ref, n2pre_ref, n2post_ref,
             n1post_ref, w1e_ref, w2e_ref, w1l_ref, w2l_ref,
             rl_ref, df_ref, da_ref, dmix_ref, dx1_ref, redf_ref, redb_ref, dh2_acc):
        i = pl.program_id(0)

        @pl.when(i == 0)
        def _():
            redf_ref[...] = jnp.zeros_like(redf_ref)
            redb_ref[...] = jnp.zeros_like(redb_ref)

        x1v = x1_ref[...]
        gate1, scale2, gate2 = mod_ref[2:3, :], mod_ref[4:5, :], mod_ref[5:6, :]
        h2 = h2_ref[...]
        f = fe_ref[...]
        for j in range(n_late):
            cols = slice(j * FF_BLK, (j + 1) * FF_BLK)
            ra = jnp.maximum(_dot(h2, w1l_ref[j]), 0.0)
            r = (ra * ra).astype(BF16)
            rl_ref[:, cols] = r
            f = f + _dot(r, w2l_ref[j])
        post2 = n2post_ref[...]
        gate_post2 = gate2 * post2
        rf = _rstd(f)
        fhat = f * rf
        err = (x1v + fhat * gate_post2) - tgt_ref[...]
        dy = err * (1.0 / D)
        d_f, sum_f = _rms_bwd_gained(dy, gate_post2, fhat, rf)
        dfv = d_f.astype(BF16)
        df_ref[...] = dfv
        redf_ref[0:1, :] += post2 * sum_f
        redf_ref[1:2, :] += gate2 * sum_f
        redf_ref[2:3, :] += _colsum(err * err)

        for j in range(N_DEV):
            cols = slice(j * FF_BLK, (j + 1) * FF_BLK)
            if j < FC_EARLY:
                w1, w2, r = w1e_ref[j], w2e_ref[j], re_ref[:, _early_col(j):_early_col(j) + FF_BLK]
            else:
                jl = j - FC_EARLY
                w1, w2, r = w1l_ref[jl], w2l_ref[jl], rl_ref[:, jl * FF_BLK:(jl + 1) * FF_BLK]
            dr = _dot_nt(dfv, w2)
            da = (dr * (2.0 * jnp.sqrt(r.astype(F32)))).astype(BF16)
            da_ref[:, cols] = da
            contrib = _dot_nt(da, w1)
            if j == 0:
                dh2_acc[...] = contrib
            else:
                dh2_acc[...] += contrib
        dh2 = dh2_acc[...]
        pre2, post1 = n2pre_ref[...], n1post_ref[...]
        r2 = _rstd(x1v)
        xhat = x1v * r2
        d_x1, sum_h = _rms_bwd_gained(dh2, pre2 * (1.0 + scale2), xhat, r2)
        dx1 = dy + d_x1
        dx1_ref[...] = dx1
        mixv = mix_ref[...]
        rm = _rstd(mixv)
        mhat = mixv * rm
        d_mix, sum_m = _rms_bwd_gained(dx1, gate1 * post1, mhat, rm)
        dmix_ref[...] = d_mix.astype(BF16)
        redb_ref[0:1, :] += _colsum(dh2)
        redb_ref[1:2, :] += pre2 * sum_h
        redb_ref[2:3, :] += (1.0 + scale2) * sum_h
        redb_ref[3:4, :] += post1 * sum_m
        redb_ref[4:5, :] += gate1 * sum_m

    tile = lambda w: pl.BlockSpec((tt, w), lambda i: (i, 0))
    return pl.pallas_call(
        body, name="mlp_late_bwd", grid=(nt,),
        out_shape=(jax.ShapeDtypeStruct((t_len, late_cols), BF16), jax.ShapeDtypeStruct((t_len, D), BF16),
                   jax.ShapeDtypeStruct((t_len, D_FF), BF16), jax.ShapeDtypeStruct((t_len, D), BF16),
                   jax.ShapeDtypeStruct((t_len, D), F32), jax.ShapeDtypeStruct((8, D), F32),
                   jax.ShapeDtypeStruct((8, D), F32)),
        in_specs=[tile(FC_EARLY * FF_BLK), tile(D), tile(D), tile(D), tile(D),
                  tile(D), _full((8, D)), _full((1, D)), _full((1, D)), _full((1, D)),
                  _resident((FC_EARLY, D, FF_BLK)), _resident((FC_EARLY, FF_BLK, D)),
                  _resident((n_late, D, FF_BLK)), _resident((n_late, FF_BLK, D))],
        out_specs=(tile(late_cols), tile(D), tile(D_FF), tile(D), tile(D), _full((8, D)), _full((8, D))),
        scratch_shapes=[pltpu.VMEM((tt, D), F32)],
        compiler_params=pltpu.CompilerParams(dimension_semantics=("arbitrary",), vmem_limit_bytes=VMEM_LIMIT),
    )(r_early, x1, h2, f_early, tgt, mix, mod, n2pre, n2post, n1post, w1_early, w2_early, w1_late, w2_late)


def _mlp_wgrad(tt, r_early, r_late, da, df, h2):
    t_len = df.shape[0]
    nt = t_len // tt
    odd_steps = [j for j, rel in enumerate(WGRAD_ORDER) if rel % 2]

    def relation(j):
        rel = jnp.int32(WGRAD_ORDER[-1])
        for step in range(N_DEV - 2, -1, -1):
            rel = jnp.where(j == step, WGRAD_ORDER[step], rel)
        return rel

    def body(re_ref, rl_ref, da_ref, df_ref, h2_ref, own1_ref, own2_ref, out1_ref, out2_ref, diag1_ref, diag2_ref,
             acc1, acc2, snd1, snd2, sib1, sib2, dsnd1, dsnd2, send_sems, recv_sems):
        j, t = pl.program_id(0), pl.program_id(1)
        rows = pl.ds(pl.multiple_of(t * tt, tt), tt)
        x, y, c = _place()
        accs, snds, sibs = (acc1, acc2), (snd1, snd2), (sib1, sib2)
        dsnds, diags = (dsnd1, dsnd2), (diag1_ref, diag2_ref)

        def to_sibling(a, jj, buf=0):
            return pltpu.make_async_remote_copy(
                src_ref=snds[a].at[buf], dst_ref=sibs[a].at[jj],
                send_sem=send_sems.at[4 * a + jj], recv_sem=recv_sems.at[4 * a + jj],
                device_id=(x, y, 1 - c), device_id_type=MESH)

        def to_diagonal(a):
            return pltpu.make_async_remote_copy(
                src_ref=dsnds[a], dst_ref=diags[a], send_sem=send_sems.at[8 + a], recv_sem=recv_sems.at[8 + a],
                device_id=_peer(x, y, c, 6), device_id_type=MESH)

        @pl.when(t == 0)
        def _():
            acc2[...] = jnp.zeros_like(acc2)
            acc1[...] = jnp.zeros_like(acc1)

        for r_ref, mine in ((re_ref, relation(j) < FC_EARLY), (rl_ref, relation(j) >= FC_EARLY)):
            @pl.when(mine)
            def _():
                acc2[...] += _dot_tn(r_ref[...], df_ref[rows, :])
                acc1[...] += _dot_tn(h2_ref[rows, :], da_ref[...])

        for step, rel in enumerate(WGRAD_ORDER):
            jj = rel // 2

            @pl.when((t == nt - 1) & (j == step))
            def _():
                for a, (own_ref, out_ref) in enumerate(((own1_ref, out1_ref), (own2_ref, out2_ref))):
                    if rel % 2:
                        q = odd_steps.index(step)
                        if q >= 2:
                            to_sibling(a, WGRAD_ORDER[odd_steps[q - 2]] // 2).wait_send()
                        snds[a][q % 2] = accs[a][...].astype(BF16)
                        to_sibling(a, jj, q % 2).start()
                        continue
                    to_sibling(a, jj).wait_recv()
                    chip_sum = accs[a][...] + sibs[a][jj].astype(F32)
                    if rel == 6:
                        dsnds[a][...] = chip_sum.astype(BF16)
                        to_diagonal(a).start()
                    elif rel == 0:
                        own_ref[...] = chip_sum
                    else:
                        out_ref[0] = chip_sum.astype(BF16)
                    if step == N_DEV - 1:
                        for q in (2, 3):
                            to_sibling(a, WGRAD_ORDER[odd_steps[q]] // 2).wait_send()
                        to_diagonal(a).wait_recv()
                        to_diagonal(a).wait_send()

    assert WGRAD_ORDER[-1] == 0 and WGRAD_ORDER[-3:-1] == (2, 4)
    blk = pl.BlockSpec((tt, FF_BLK), lambda j, t: (t, relation(j)))
    early_block = lambda rel: jnp.where(rel < FC_HEAD, rel + FC_EARLY - FC_HEAD, rel - FC_HEAD)
    early = lambda j, t: (jnp.where(relation(j) < FC_EARLY, t, 0),
                          jnp.where(relation(j) < FC_EARLY, early_block(relation(j)), 0))
    late = lambda j, t: (jnp.where(relation(j) < FC_EARLY, 0, t), jnp.maximum(relation(j) - FC_EARLY, 0))
    chip = lambda j, t: (jnp.clip(j - 5, 0, 1), 0, 0)
    hbm = pl.BlockSpec(memory_space=pl.ANY)
    return pl.pallas_call(
        body, name="mlp_wgrad", grid=(N_DEV, nt),
        out_shape=(jax.ShapeDtypeStruct((D, FF_BLK), F32), jax.ShapeDtypeStruct((FF_BLK, D), F32),
                   jax.ShapeDtypeStruct((2, D, FF_BLK), BF16), jax.ShapeDtypeStruct((2, FF_BLK, D), BF16),
                   jax.ShapeDtypeStruct((D, FF_BLK), BF16), jax.ShapeDtypeStruct((FF_BLK, D), BF16)),
        in_specs=[pl.BlockSpec((tt, FF_BLK), early), pl.BlockSpec((tt, FF_BLK), late), blk,
                  _resident((t_len, D)), _resident((t_len, D))],
        out_specs=(_full((D, FF_BLK)), _full((FF_BLK, D)),
                   pl.BlockSpec((1, D, FF_BLK), chip), pl.BlockSpec((1, FF_BLK, D), chip), hbm, hbm),
        scratch_shapes=[pltpu.VMEM((D, FF_BLK), F32), pltpu.VMEM((FF_BLK, D), F32),
                        pltpu.VMEM((2, D, FF_BLK), BF16), pltpu.VMEM((2, FF_BLK, D), BF16),
                        pltpu.VMEM((4, D, FF_BLK), BF16), pltpu.VMEM((4, FF_BLK, D), BF16),
                        pltpu.VMEM((D, FF_BLK), BF16), pltpu.VMEM((FF_BLK, D), BF16),
                        pltpu.SemaphoreType.DMA((10,)), pltpu.SemaphoreType.DMA((10,))],
        compiler_params=pltpu.CompilerParams(dimension_semantics=("arbitrary", "arbitrary"),
                                             vmem_limit_bytes=VMEM_LIMIT),
    )(r_early, r_late, da, df, h2)


def _acc_rows(ref, row0, k, val):
    half = CHUNK // 2
    ref[row0:row0 + half, k * GROUP:(k + 1) * GROUP] += val[:half, :]
    ref[row0:row0 + half, D_A + k * GROUP:D_A + (k + 1) * GROUP] += val[half:, :]


def _attn_bwd(tt, dmix, dx1, x, z, cat, mod, n1pre, w_in_t, w_out, w_sp, bs_rows, ln_g, ln_b, w_pool, b_pool,
              pool_scale, red_fwd, red_bwd, chip_sums):
    t_len = x.shape[0]
    nt = t_len // tt
    hb = tt // HALO
    n_sums = len(chip_sums)

    def body(dmix_ref, dx1_ref, x_ref, z_ref, zprev_ref, cat_ref, mod_ref, n1pre_ref, win_ref, wout_ref, wsp_ref,
             bs_ref, lng_ref, lnb_ref, wp_ref, bp_ref, ps_ref, redf_ref, redb_ref, *rest):
        sum_out = rest[:n_sums]
        gx_ref, gwin_ref, gwout_ref, small_ref = rest[n_sums:n_sums + 4]
        sum_in = rest[n_sums + 4:2 * n_sums + 4]
        carry, acc_in, acc_out, dz_scr, bs_acc, send_sems, recv_sems = rest[2 * n_sums + 4:]
        s = pl.program_id(0)
        i = nt - 1 - s
        px, py, pc = _place()

        def chip_copy(a, r):
            return pltpu.make_async_remote_copy(
                src_ref=sum_out[a].at[r], dst_ref=sum_in[a].at[r],
                send_sem=send_sems.at[2 * a + r], recv_sem=recv_sems.at[2 * a + r],
                device_id=_peer(px, py, pc, 2 * (r + 1)), device_id_type=MESH)

        @pl.when(s == 0)
        def _():
            for a in range(n_sums):
                for r in range(2):
                    chip_copy(a, r).start()
            carry[...] = jnp.zeros_like(carry)
            acc_in[...] = jnp.zeros_like(acc_in)
            acc_out[...] = jnp.zeros_like(acc_out)
            bs_acc[...] = jnp.zeros_like(bs_acc)
            small_ref[...] = jnp.zeros_like(small_ref)
            small_ref[ROW_DMOD + 2:ROW_DMOD + 3, :] = redb_ref[3:4, :]
            small_ref[ROW_DMOD + 3:ROW_DMOD + 5, :] = redb_ref[0:2, :]
            small_ref[ROW_DMOD + 5:ROW_DMOD + 6, :] = redf_ref[0:1, :]
            small_ref[ROW_N1POST:ROW_N1POST + 1, :] = redb_ref[4:5, :]
            small_ref[ROW_N2PRE:ROW_N2PRE + 1, :] = redb_ref[2:3, :]
            small_ref[ROW_N2POST:ROW_N2POST + 1, :] = redf_ref[1:2, :]
            small_ref[ROW_LOSS:ROW_LOSS + 1, :] = redf_ref[2:3, :]

        dmixv = dmix_ref[...]
        dcat = _dot_nt(dmixv, wout_ref[...])
        acc_out[...] += _dot_tn(cat_ref[...], dmixv)

        z = z_ref[...]
        t_g, ga = _gelu_parts(z[:, :2 * D_A])
        u, vr = ga[:, :D_A], ga[:, D_A:]
        dv0 = vr - jnp.mean(vr, axis=-1, keepdims=True)
        rv = lax.rsqrt(jnp.mean(dv0 * dv0, axis=-1, keepdims=True) + EPS)
        vhat = dv0 * rv
        vb = (vhat * lng_ref[...] + lnb_ref[...]).astype(BF16)
        mask = _tril_mask()
        wc = [(wsp_ref[h] * mask).astype(BF16) for h in range(N_HEADS)]

        dya = dcat[:, :D_A]
        for h in range(N_HEADS):
            cols = slice(h * GROUP, (h + 1) * GROUP)
            bs_sum = jnp.zeros((CHUNK, GROUP), F32)
            ws_sum = jnp.zeros((CHUNK, CHUNK), F32)
            for ch in range(tt // CHUNK):
                rows = slice(ch * CHUNK, (ch + 1) * CHUNK)
                v_ch = vb[rows, cols]
                mixed = _dot(wc[h], v_ch) + bs_ref[:, cols]
                dy_ch = dya[rows, cols]
                dz_scr[rows, cols] = dy_ch * mixed
                dmixed = dy_ch * u[rows, cols]
                dmb = dmixed.astype(BF16)
                dz_scr[rows, D_A + h * GROUP:D_A + (h + 1) * GROUP] = _dot_tn(wc[h], dmb)
                bs_sum = bs_sum + dmixed
                ws_sum = ws_sum + _dot_nt(dmb, v_ch)
            _acc_rows(bs_acc, 0, h, bs_sum)
            _acc_rows(small_ref, ROW_WS, h, ws_sum)

        dvl = dz_scr[:, D_A:2 * D_A]
        dvhat = dvl * lng_ref[...]
        dvl_vhat = dvl * vhat
        dvr = rv * (dvhat - jnp.mean(dvhat, axis=-1, keepdims=True)
                    - vhat * jnp.mean(dvl_vhat * lng_ref[...], axis=-1, keepdims=True))
        small_ref[ROW_LN:ROW_LN + 1, 0:D_A] += _colsum(dvl_vhat)
        small_ref[ROW_LN:ROW_LN + 1, D_A:D] += _colsum(dvl)
        dga = jnp.concatenate([dz_scr[:, :D_A], dvr], axis=1)
        dza = dga * _gelu_grad(z[:, :2 * D_A], t_g)

        zb = z[:, 2 * D_A:]
        halo_prev = jnp.where(i == 0, 0.0, zprev_ref[...])
        diff = _pool_diff(zb, halo_prev, i * tt)
        dyb = dcat[:, D_A:]
        inv = _inv_counts(i * tt, tt)
        scaled, ddiffs = [], []
        for g in range(len(WINDOWS)):
            cols = slice(g * GROUP, (g + 1) * GROUP)
            db = diff[g].astype(BF16)
            wpg = wp_ref[g].astype(BF16)
            pre = _dot(db, wpg) + bp_ref[:, cols]
            small_ref[ROW_POOL:ROW_POOL + 1, cols] += _colsum(dyb[:, cols] * pre)
            dpre = dyb[:, cols] * ps_ref[:, cols]
            small_ref[ROW_POOL:ROW_POOL + 1, D_B + g * GROUP:D_B + (g + 1) * GROUP] += _colsum(dpre)
            dpb = dpre.astype(BF16)
            _acc_rows(small_ref, ROW_WP, g, _dot_tn(db, dpb))
            ddiff = _dot_nt(dpb, wpg)
            ddiffs.append(ddiff)
            scaled.append(ddiff * inv[g])
        scaled_all = jnp.concatenate(scaled, axis=1)
        ext = jnp.concatenate([scaled_all, carry[...]], axis=0)
        n_ext = tt + HALO
        s2 = ext + pltpu.roll(ext, n_ext - 1, 0)
        t4 = s2[:, GROUP:]
        s4 = t4 + pltpu.roll(t4, n_ext - 2, 0)
        t8 = s4[:, GROUP:]
        s8 = t8 + pltpu.roll(t8, n_ext - 4, 0)
        t16 = s8[:, GROUP:]
        s16 = t16 + pltpu.roll(t16, n_ext - 8, 0)
        back = [s2[:, :GROUP], s4[:, :GROUP], s8[:, :GROUP], s16]
        carry[...] = scaled_all[:HALO, :]
        dzb = jnp.concatenate([back[g][:tt, :] - ddiffs[g] for g in range(len(WINDOWS))], axis=1)

        dzv = jnp.concatenate([dza, dzb], axis=1).astype(BF16)
        dh1 = _dot(dzv, win_ref[...])
        xv = x_ref[...]
        r1 = _rstd(xv)
        xhat = xv * r1
        shift1, scale1 = mod_ref[0:1, :], mod_ref[1:2, :]
        pre1 = n1pre_ref[...]
        gain1 = pre1 * (1.0 + scale1)
        h1 = (xhat * gain1 + shift1).astype(BF16)
        acc_in[...] += _dot_tn(dzv, h1)
        d_x, sum_h = _rms_bwd_gained(dh1, gain1, xhat, r1)
        gx_ref[...] = dx1_ref[...] + d_x
        small_ref[ROW_DMOD:ROW_DMOD + 1, :] += _colsum(dh1)
        small_ref[ROW_DMOD + 1:ROW_DMOD + 2, :] += pre1 * sum_h
        small_ref[ROW_N1PRE:ROW_N1PRE + 1, :] += (1.0 + scale1) * sum_h

        @pl.when(s == nt - 1)
        def _():
            gwin_ref[...] = acc_in[...].astype(BF16)
            gwout_ref[...] = acc_out[...].astype(BF16)
            bs = _unfold(bs_acc[...])
            for h in range(N_HEADS):
                small_ref[ROW_BS + h:ROW_BS + h + 1, 0:GROUP] = jnp.sum(
                    bs[:, h * GROUP:(h + 1) * GROUP].T, axis=0, keepdims=True)
            for a in range(n_sums):
                for r in range(2):
                    chip_copy(a, r).wait_recv()
                    chip_copy(a, r).wait_send()

    rev = lambda w: pl.BlockSpec((tt, w), lambda s: (nt - 1 - s, 0))
    zprev = pl.BlockSpec((HALO, D_B), lambda s: (jnp.maximum((nt - 1 - s) * hb - 1, 0), 2))
    hbm = pl.BlockSpec(memory_space=pl.ANY)
    outs = pl.pallas_call(
        body, name="attn_bwd", grid=(nt,),
        out_shape=tuple([jax.ShapeDtypeStruct((t_len, D), F32), jax.ShapeDtypeStruct((D_Z, D), BF16),
                         jax.ShapeDtypeStruct((D, D), BF16), jax.ShapeDtypeStruct((SMALL_ROWS, D), F32)]
                        + [jax.ShapeDtypeStruct(cs.shape, cs.dtype) for cs in chip_sums]),
        in_specs=[rev(D), rev(D), rev(D), rev(D_Z), zprev, rev(D), _full((8, D)), _full((1, D)),
                  _resident((D_Z, D)), _resident((D, D)), _full((N_HEADS, CHUNK, CHUNK)), _full((CHUNK, D_A)),
                  _full((1, D_A)), _full((1, D_A)), _full((len(WINDOWS), GROUP, GROUP)), _full((1, D_B)),
                  _full((1, D_B)), _full((8, D)), _full((8, D))] + [_resident(cs.shape) for cs in chip_sums],
        out_specs=tuple([rev(D), _resident((D_Z, D)), _resident((D, D)), _full((SMALL_ROWS, D))] + [hbm] * n_sums),
        scratch_shapes=[pltpu.VMEM((HALO, D_B), F32), pltpu.VMEM((D_Z, D), F32), pltpu.VMEM((D, D), F32),
                        pltpu.VMEM((tt, 2 * D_A), F32), pltpu.VMEM((CHUNK // 2, D), F32),
                        pltpu.SemaphoreType.DMA((2 * n_sums,)), pltpu.SemaphoreType.DMA((2 * n_sums,))],
        compiler_params=pltpu.CompilerParams(dimension_semantics=("arbitrary",), vmem_limit_bytes=VMEM_LIMIT),
    )(dmix, dx1, x, z, z, cat, mod, n1pre, w_in_t, w_out, w_sp, bs_rows, ln_g, ln_b, w_pool, b_pool, pool_scale,
      red_fwd, red_bwd, *chip_sums)
    return outs[:4], outs[4:]


def _adam(w, g, m, v):
    m2 = ADAM_B1 * m + (1.0 - ADAM_B1) * g
    v2 = ADAM_B2 * v + (1.0 - ADAM_B2) * (g * g)
    m_hat = m2 / (1.0 - ADAM_B1 ** ADAM_STEP)
    v_hat = v2 / (1.0 - ADAM_B2 ** ADAM_STEP)
    delta = -ADAM_LR * (m_hat / (jnp.sqrt(v_hat) + ADAM_EPS) + ADAM_WD * w)
    return delta, m2, v2


def _adamw_shard(name, rb, w, g, m, v):
    rows, cols = w.shape

    def body(w_ref, g_ref, m_ref, v_ref, d_ref, m2_ref, v2_ref):
        d_ref[...], m2_ref[...], v2_ref[...] = _adam(w_ref[...], g_ref[...], m_ref[...], v_ref[...])

    blk = pl.BlockSpec((rb, cols), lambda i: (i, 0))
    shp = jax.ShapeDtypeStruct((rows, cols), F32)
    return pl.pallas_call(
        body, name=name, grid=(rows // rb,), out_shape=(shp, shp, shp),
        in_specs=[blk] * 4, out_specs=(blk, blk, blk),
        compiler_params=pltpu.CompilerParams(dimension_semantics=("arbitrary",)),
    )(w, g, m, v)


def _adamw_ada(rb, w, sc, dmod_cols, m, v):
    rows, cols = w.shape

    def body(w_ref, sc_ref, dm_ref, m_ref, v_ref, g_ref, d_ref, m2_ref, v2_ref):
        g = _dot_tn(sc_ref[...].astype(BF16), dm_ref[...].astype(BF16))
        g_ref[...] = g
        d_ref[...], m2_ref[...], v2_ref[...] = _adam(w_ref[...], g, m_ref[...], v_ref[...])

    blk = pl.BlockSpec((rb, cols), lambda i: (i, 0))
    shp = jax.ShapeDtypeStruct((rows, cols), F32)
    return pl.pallas_call(
        body, name="adamw_ada", grid=(rows // rb,), out_shape=(shp, shp, shp, shp),
        in_specs=[blk, pl.BlockSpec((N_DEV, rb), lambda i: (0, i)), _full((N_DEV, cols)), blk, blk],
        out_specs=(blk, blk, blk, blk),
        compiler_params=pltpu.CompilerParams(dimension_semantics=("arbitrary",)),
    )(w, sc, dmod_cols, m, v)


def _unfold(acc_rows):
    return jnp.concatenate([acc_rows[:, :D_A], acc_rows[:, D_A:]], axis=0)


def _adamw_small(total, params):
    n = len(params)
    flat = [a for p in params for a in p]

    def body(*refs):
        s_ref = refs[0]
        p_refs = refs[1:1 + 3 * n]
        loss_ref = refs[1 + 3 * n]
        o_refs = refs[2 + 3 * n:]
        d_b_ada = s_ref[0:6, :]
        for b in range(1, N_DEV):
            d_b_ada = d_b_ada + s_ref[_table_row(b):_table_row(b) + 6, :]
        misc = lambda r: s_ref[PK_MISC + r - ROW_N1PRE:PK_MISC + r - ROW_N1PRE + 1, :]
        loss = jnp.sum(misc(ROW_LOSS), axis=-1, keepdims=True) * (0.5 / D)
        loss_ref[...] = jnp.broadcast_to(loss, (8, GROUP))
        mask = _tril_mask()
        ws = _unfold(s_ref[PK_WS:PK_WS + 64, :])
        wp = _unfold(s_ref[PK_WP:PK_WP + 64, :])
        grads = [
            d_b_ada,
            misc(ROW_N1PRE), misc(ROW_N1POST), misc(ROW_N2PRE), misc(ROW_N2POST),
            misc(ROW_LN)[:, :D_A], misc(ROW_LN)[:, D_A:],
            misc(ROW_POOL)[:, :D_B], misc(ROW_POOL)[:, D_B:],
            s_ref[PK_BS:PK_BS + N_HEADS, 0:GROUP],
            jnp.stack([ws[:, h * GROUP:(h + 1) * GROUP] * mask for h in range(N_HEADS)]),
            jnp.stack([wp[:, g * GROUP:(g + 1) * GROUP] for g in range(len(WINDOWS))]),
        ]
        for k in range(n):
            w_ref, m_ref, v_ref = p_refs[3 * k:3 * k + 3]
            g = grads[k]
            o_refs[4 * k][...] = g
            o_refs[4 * k + 1][...], o_refs[4 * k + 2][...], o_refs[4 * k + 3][...] = _adam(
                w_ref[...], g, m_ref[...], v_ref[...])

    vm = pl.BlockSpec(memory_space=pltpu.VMEM)
    out_shape = [jax.ShapeDtypeStruct((8, GROUP), F32)]
    for w, _, _ in params:
        out_shape += [jax.ShapeDtypeStruct(w.shape, F32)] * 4
    return pl.pallas_call(
        body, name="adamw_small", out_shape=tuple(out_shape),
        in_specs=[vm] * (1 + 3 * n), out_specs=tuple([vm] * len(out_shape)),
    )(total, *flat)


TT_ATTN_FWD = 512
ATTN_LAG = 2
TT_MLP_FWD = 512
TT_MLP = 256
TT_WGRAD = 2048
TT_ATTN_BWD = 512


def kernel(x, c, w_ada, b_ada, norm1_pre, norm1_post, w_in, w_spatial, b_spatial, ln_v_gain, ln_v_bias, w_pool, b_pool, pool_scale, w_out, norm2_pre, norm2_post, w_fc1, w_fc2, loss_target, m_w_ada, m_b_ada, m_norm1_pre, m_norm1_post, m_w_in, m_w_spatial, m_b_spatial, m_ln_v_gain, m_ln_v_bias, m_w_pool, m_b_pool, m_pool_scale, m_w_out, m_norm2_pre, m_norm2_post, m_w_fc1, m_w_fc2, v_w_ada, v_b_ada, v_norm1_pre, v_norm1_post, v_w_in, v_w_spatial, v_b_spatial, v_ln_v_gain, v_ln_v_bias, v_w_pool, v_b_pool, v_pool_scale, v_w_out, v_norm2_pre, v_norm2_post, v_w_fc1, v_w_fc2):
    t_len = x.shape[1]
    me = 4 * lax.axis_index("x") + 2 * lax.axis_index("y") + lax.axis_index("c")
    ada_cols = w_ada.shape[1]
    tt = lambda want: min(want, t_len)

    x2 = x.reshape(t_len, D)
    tgt = loss_target.reshape(t_len, D)
    row = lambda a: a.reshape(1, -1)

    b_my = lax.dynamic_slice_in_dim(b_ada, me * ada_cols, ada_cols).reshape(1, ada_cols)
    w_in_shard, w_out_shard, w1_shard, w2_shard = _cast_shards([w_in.T, w_out, w_fc1, w_fc2])

    bs_rows = jnp.repeat(b_spatial.T, GROUP, axis=1)
    attn_consts = (w_spatial, bs_rows, row(ln_v_gain), row(ln_v_bias), w_pool, row(b_pool), row(pool_scale))

    (z, cat, mix, x1, h2, r_begun, f_head, mod, sc), (w1_early, w2_early), w_out_all, w_in_t = _attn_fwd(
        tt(TT_ATTN_FWD), x2, jnp.broadcast_to(c, (8, D)), w_ada, b_my, row(norm1_pre), row(norm1_post),
        w_in_shard, w_out_shard, *attn_consts, (w1_shard, w2_shard), row(norm2_pre))
    (r_early, f_early), (w1_late, w2_late) = _mlp_fwd_early(
        tt(TT_MLP_FWD), r_begun, h2, f_head, w1_early, w2_early)
    r_late, df, da, dmix, dx1, red_fwd, red_bwd = _mlp_late_bwd(
        tt(TT_MLP), r_early, x1, h2, f_early, tgt, mix, mod, row(norm2_pre), row(norm2_post), row(norm1_post),
        w1_early, w2_early, w1_late, w2_late)
    own_w1, own_w2, sums_w1, sums_w2, diag_w1, diag_w2 = _mlp_wgrad(tt(TT_WGRAD), r_early, r_late, da, df, h2)
    (grad_x, p_in, p_out, small), (arr_w1, arr_w2) = _attn_bwd(
        tt(TT_ATTN_BWD), dmix, dx1, x2, z, cat, mod, row(norm1_pre), w_in_t, w_out_all, *attn_consts,
        red_fwd, red_bwd, [sums_w1, sums_w2])
    (grad_in_t, grad_out, total), ((grad_w1, d_w1, m_w1, v_w1), (grad_w2, d_w2, m_w2, v_w2)) = _tail_comm(
        [p_in.reshape(N_DEV, D_Z // N_DEV, D), p_out.reshape(N_DEV, D // N_DEV, D)], small, 64,
        [(w_fc1, own_w1, arr_w1, diag_w1, m_w_fc1, v_w_fc1), (w_fc2, own_w2, arr_w2, diag_w2, m_w_fc2, v_w_fc2)])

    d_out, m_out, v_out = _adamw_shard("adamw_out", 128, w_out, grad_out, m_w_out, v_w_out)
    d_in_t, m_in_t, v_in_t = _adamw_shard("adamw_in", D_Z // N_DEV, w_in.T, grad_in_t, m_w_in.T, v_w_in.T)
    table = jnp.concatenate([total[0:PACK_FINE, :], total[PK_TABLE_B:PK_MISC, :]], axis=0)
    dmod_all = table.reshape(N_DEV, 8, D)[:, :6, :].reshape(N_DEV, 6 * D)
    dmod_cols = lax.dynamic_slice_in_dim(dmod_all, me * ada_cols, ada_cols, axis=1)
    grad_ada, d_ada, m_ada, v_ada = _adamw_ada(256, w_ada, sc, dmod_cols, m_w_ada, v_w_ada)

    six = lambda a: a.reshape(6, D)
    small_params = [
        (six(b_ada), six(m_b_ada), six(v_b_ada)),
        (row(norm1_pre), row(m_norm1_pre), row(v_norm1_pre)),
        (row(norm1_post), row(m_norm1_post), row(v_norm1_post)),
        (row(norm2_pre), row(m_norm2_pre), row(v_norm2_pre)),
        (row(norm2_post), row(m_norm2_post), row(v_norm2_post)),
        (row(ln_v_gain), row(m_ln_v_gain), row(v_ln_v_gain)),
        (row(ln_v_bias), row(m_ln_v_bias), row(v_ln_v_bias)),
        (row(pool_scale), row(m_pool_scale), row(v_pool_scale)),
        (row(b_pool), row(m_b_pool), row(v_b_pool)),
        (b_spatial, m_b_spatial, v_b_spatial),
        (w_spatial, m_w_spatial, v_w_spatial),
        (w_pool, m_w_pool, v_w_pool),
    ]
    outs = _adamw_small(total, small_params)
    loss = outs[0][0, 0]
    names = ["b_ada", "norm1_pre", "norm1_post", "norm2_pre", "norm2_post", "ln_v_gain", "ln_v_bias", "pool_scale",
             "b_pool", "b_spatial", "w_spatial", "w_pool"]
    shapes = dict(b_ada=b_ada.shape, norm1_pre=norm1_pre.shape, norm1_post=norm1_post.shape,
                  norm2_pre=norm2_pre.shape, norm2_post=norm2_post.shape, ln_v_gain=ln_v_gain.shape,
                  ln_v_bias=ln_v_bias.shape, pool_scale=pool_scale.shape, b_pool=b_pool.shape,
                  b_spatial=b_spatial.shape, w_spatial=w_spatial.shape, w_pool=w_pool.shape)
    res = {}
    for k, nm in enumerate(names):
        res[nm] = tuple(o.reshape(shapes[nm]) for o in outs[1 + 4 * k:5 + 4 * k])
    res["w_ada"] = (grad_ada, d_ada, m_ada, v_ada)
    res["w_in"] = (grad_in_t.T, d_in_t.T, m_in_t.T, v_in_t.T)
    res["w_out"] = (grad_out, d_out, m_out, v_out)
    res["w_fc1"] = (grad_w1, d_w1, m_w1, v_w1)
    res["w_fc2"] = (grad_w2, d_w2, m_w2, v_w2)

    order = ["w_ada", "b_ada", "norm1_pre", "norm1_post", "w_in", "w_spatial", "b_spatial", "ln_v_gain", "ln_v_bias",
             "w_pool", "b_pool", "pool_scale", "w_out", "norm2_pre", "norm2_post", "w_fc1", "w_fc2"]
    return (loss, grad_x.reshape(x.shape),
            *[res[nm][0] for nm in order], *[res[nm][1] for nm in order],
            *[res[nm][2] for nm in order], *[res[nm][3] for nm in order])
```

```python
import functools

import jax
import jax.numpy as jnp
from jax import lax
from jax.experimental import pallas as pl
from jax.experimental.pallas import tpu as pltpu

F32 = jnp.float32
BF16 = jnp.bfloat16
MESH = pl.DeviceIdType.MESH

N_DEV = 8
D = 1024
D_A = 512
D_B = 512
D_Z = 2 * D_A + D_B
N_HEADS = 4
CHUNK = 128
WINDOWS = (2, 4, 8, 16)
GROUP = 128
D_FF = 4096
FF_BLK = D_FF // N_DEV
HALO = 16
EPS = 1e-6
VMEM_LIMIT = 60 * 1024 * 1024

ADAM_LR = 0.001
ADAM_B1 = 0.9
ADAM_B2 = 0.999
ADAM_EPS = 1e-08
ADAM_WD = 0.01
ADAM_STEP = 10

ROW_DMOD = 0
ROW_N1PRE, ROW_N1POST, ROW_N2PRE, ROW_N2POST = 8, 9, 10, 11
ROW_LN = 12
ROW_POOL = 13
ROW_LOSS = 14
ROW_BS = 16
ROW_WS = 24
ROW_WP = 88
SMALL_ROWS = 152
PACK_FINE = 40
PACK_HALF = PACK_FINE + 64
PACK_ROWS = 2 * PACK_HALF
PK_WS = PACK_FINE
PK_TABLE_B = PACK_HALF
PK_MISC = PK_TABLE_B + 24
PK_BS = PK_MISC + 8
PK_WP = PK_BS + 8


def _table_row(b):
    if isinstance(b, int):
        return 8 * b if 8 * b < PACK_FINE else 8 * b + PK_TABLE_B - PACK_FINE
    return 8 * b + jnp.where(8 * b < PACK_FINE, 0, PK_TABLE_B - PACK_FINE)


def _dot(a, b):
    return jnp.dot(a, b, preferred_element_type=F32)


def _dot_nt(a, b):
    return lax.dot_general(a, b, (((1,), (1,)), ((), ())), preferred_element_type=F32)


def _dot_tn(a, b):
    return lax.dot_general(a, b, (((0,), (0,)), ((), ())), preferred_element_type=F32)


def _rstd(v):
    return lax.rsqrt(jnp.mean(v * v, axis=-1, keepdims=True) + EPS)


def _rms_bwd(d_hat, hat, rstd):
    return rstd * (d_hat - hat * jnp.mean(d_hat * hat, axis=-1, keepdims=True))


def _rms_bwd_gained(g, gain, hat, rstd):
    g_hat = g * hat
    d_v = rstd * (g * gain - hat * jnp.mean(g_hat * gain, axis=-1, keepdims=True))
    return d_v, _colsum(g_hat)


_K0 = 0.7978845608028654
_K1 = 0.044715


def _gelu_parts(v):
    t = jnp.tanh(v * (_K0 + (_K0 * _K1) * (v * v)))
    return t, v * (0.5 + 0.5 * t)


def _gelu_grad(v, t):
    return (0.5 + 0.5 * t) + (0.5 * v) * (1.0 - t * t) * (_K0 + (3.0 * _K0 * _K1) * (v * v))


def _colsum(v):
    return jnp.sum(v, axis=0, keepdims=True)


def _full(shape):
    n = len(shape)
    return pl.BlockSpec(shape, lambda *_: (0,) * n)


def _resident(shape):
    n = len(shape)
    return pl.BlockSpec(shape, lambda *_: (0,) * n, pipeline_mode=pl.Buffered(1))


def _place():
    x, y, c = lax.axis_index("x"), lax.axis_index("y"), lax.axis_index("c")
    return x, y, c


def _flip(v, bit):
    return 1 - v if bit else v


def _peer(x, y, c, k):
    return (_flip(x, (k >> 2) & 1), _flip(y, (k >> 1) & 1), _flip(c, k & 1))


def _index(p):
    return 4 * p[0] + 2 * p[1] + p[2]


def _cast_shards(shards):
    def body(*refs):
        for src, dst in zip(refs[:len(shards)], refs[len(shards):]):
            dst[...] = src[...].astype(BF16)

    vm = pl.BlockSpec(memory_space=pltpu.VMEM)
    return pl.pallas_call(
        body, name="cast_shards", out_shape=tuple(jax.ShapeDtypeStruct(s.shape, BF16) for s in shards),
        in_specs=[vm] * len(shards), out_specs=tuple([vm] * len(shards)),
    )(*shards)


FC_EARLY = 6
FC_HEAD = 2
R_HEAD_COLS = (FC_EARLY - FC_HEAD) * FF_BLK
WGRAD_ORDER = (7, 6, 1, 3, 5, 2, 4, 0)


def _early_col(j):
    return R_HEAD_COLS + j * FF_BLK if j < FC_HEAD else (j - FC_HEAD) * FF_BLK


class _Copies:
    def __init__(self, entries, send_sems, recv_sems):
        self.place = _place()
        self.entries, self.send_sems, self.recv_sems = entries, send_sems, recv_sems

    def _copy(self, i, arrival=False):
        src, dst, rel = self.entries[i]
        return pltpu.make_async_remote_copy(
            src_ref=dst if arrival else src, dst_ref=dst, send_sem=self.send_sems.at[i],
            recv_sem=self.recv_sems.at[i], device_id=_peer(*self.place, rel), device_id_type=MESH)

    def start(self, *which):
        for i in which:
            self._copy(i).start()

    def wait_recv(self, *which):
        for i in which:
            self._copy(i, arrival=True).wait_recv()

    def wait_send(self, *which):
        for i in which:
            self._copy(i).wait_send()


TAIL_STEPS = 8


def _tail_comm(parts, small, row_chunk, fc):
    n, n_fc = len(parts), len(fc)

    def body(*refs):
        p_refs, small_ref = refs[:n], refs[n]
        fc_in = refs[n + 1:n + 1 + 6 * n_fc]
        outs = refs[n + 1 + 6 * n_fc:]
        g_refs, total_ref = outs[:n], outs[n]
        fc_out = outs[n + 1:n + 1 + 4 * n_fc]
        scr = outs[n + 1 + 4 * n_fc:]
        from_sib = scr[0:n]
        chip_out = scr[n:2 * n]
        chip_in = scr[2 * n:3 * n]
        pack, pack_sib, fine, bulk, total_scr = scr[3 * n:3 * n + 5]
        send_a, recv_a, send_b, recv_b, send_s, recv_s = scr[3 * n + 5:]
        step = pl.program_id(0)
        x, y, c = _place()
        me = _index((x, y, c))
        sibling = (x, y, 1 - c)
        my_chip = 2 * x + y
        others = [(1 - x, y), (x, 1 - y), (1 - x, 1 - y)]
        my_half = pl.ds(pl.multiple_of(PACK_HALF * c, 8), PACK_HALF)

        def pack_to_sibling():
            return pltpu.make_async_remote_copy(
                src_ref=pack, dst_ref=pack_sib, send_sem=send_s.at[0], recv_sem=recv_s.at[0],
                device_id=sibling, device_id_type=MESH)

        def half_to_chip(r, part):
            buf = (fine, bulk)[part]
            return pltpu.make_async_remote_copy(
                src_ref=buf.at[my_chip], dst_ref=buf.at[my_chip],
                send_sem=send_s.at[1 + 3 * part + r], recv_sem=recv_s.at[1 + 3 * part + r],
                device_id=(*others[r], c), device_id_type=MESH)

        def half_from_chip(r, part):
            k = 2 * others[r][0] + others[r][1]
            buf = (fine, bulk)[part]
            return pltpu.make_async_remote_copy(
                src_ref=buf.at[k], dst_ref=buf.at[k],
                send_sem=send_s.at[1 + 3 * part + r], recv_sem=recv_s.at[1 + 3 * part + r],
                device_id=(*others[r], c), device_id_type=MESH)

        def total_to_sibling():
            return pltpu.make_async_remote_copy(
                src_ref=total_scr.at[my_half], dst_ref=total_scr.at[my_half],
                send_sem=send_s.at[7], recv_sem=recv_s.at[7], device_id=sibling, device_id_type=MESH)

        def total_from_sibling():
            sib_half = pl.ds(pl.multiple_of(PACK_HALF * (1 - c), 8), PACK_HALF)
            return pltpu.make_async_remote_copy(
                src_ref=total_scr.at[sib_half], dst_ref=total_scr.at[sib_half],
                send_sem=send_s.at[7], recv_sem=recv_s.at[7], device_id=sibling, device_id_type=MESH)

        def to_sibling(a, k):
            return pltpu.make_async_remote_copy(
                src_ref=p_refs[a].at[2 * k + (1 - c)], dst_ref=from_sib[a].at[k],
                send_sem=send_a.at[a], recv_sem=recv_a.at[a], device_id=sibling, device_id_type=MESH)

        def all_from_sibling(a):
            return pltpu.make_async_remote_copy(
                src_ref=from_sib[a], dst_ref=from_sib[a], send_sem=send_a.at[a], recv_sem=recv_a.at[a],
                device_id=sibling, device_id_type=MESH)

        def to_chip(a, r):
            return pltpu.make_async_remote_copy(
                src_ref=chip_out[a].at[r], dst_ref=chip_in[a].at[r],
                send_sem=send_b.at[3 * a + r], recv_sem=recv_b.at[3 * a + r],
                device_id=(*others[r], c), device_id_type=MESH)

        @pl.when(step == 0)
        def _():
            pack[0:PACK_FINE, :] = jnp.zeros((PACK_FINE, D), F32)
            pack[PK_TABLE_B:PK_MISC, :] = jnp.zeros((PK_MISC - PK_TABLE_B, D), F32)
            pack[pl.ds(pl.multiple_of(_table_row(me), 8), 8), :] = small_ref[0:8, :]
            pack[PK_WS:PK_WS + 64, :] = small_ref[ROW_WS:ROW_WS + 64, :]
            pack[PK_MISC:PK_MISC + 8, :] = small_ref[ROW_N1PRE:ROW_N1PRE + 8, :]
            pack[PK_BS:PK_BS + 8, :] = small_ref[ROW_BS:ROW_BS + 8, :]
            pack[PK_WP:PK_WP + 64, :] = small_ref[ROW_WP:ROW_WP + 64, :]
            pack_to_sibling().start()
            for a in range(n):
                for k in range(4):
                    to_sibling(a, k).start()

        @pl.when(step == 1)
        def _():
            pack_to_sibling().wait_recv()
            chip_sum = pack[my_half, :] + pack_sib[my_half, :]
            fine[my_chip] = chip_sum[:PACK_FINE, :]
            bulk[my_chip] = chip_sum[PACK_FINE:, :].astype(BF16)
            for r in range(3):
                half_to_chip(r, 0).start()
                half_to_chip(r, 1).start()
            for a in range(n):
                all_from_sibling(a).wait_recv()
                rows = p_refs[a].shape[1]
                for r in range(3):
                    k = 2 * others[r][0] + others[r][1]
                    for s in range(0, rows, row_chunk):
                        sl = pl.ds(s, row_chunk)
                        chip_out[a][r, sl, :] = (p_refs[a][2 * k + c, sl, :].astype(F32)
                                                 + from_sib[a][k, sl, :].astype(F32)).astype(BF16)
                    to_chip(a, r).start()
                for s in range(0, rows, row_chunk):
                    sl = pl.ds(s, row_chunk)
                    g_refs[a][sl, :] = (p_refs[a][2 * my_chip + c, sl, :].astype(F32)
                                        + from_sib[a][my_chip, sl, :].astype(F32))

        for k in range(n_fc):
            w_ref, own_ref, arr_ref, diag_ref, m_ref, v_ref = fc_in[6 * k:6 * k + 6]
            g = own_ref[...]
            for r in range(2):
                g = g + arr_ref[r].astype(F32)
            g = g + diag_ref[...].astype(F32)
            fc_out[4 * k][...] = g
            fc_out[4 * k + 1][...], fc_out[4 * k + 2][...], fc_out[4 * k + 3][...] = _adam(
                w_ref[...], g, m_ref[...], v_ref[...])

        @pl.when(step == TAIL_STEPS - 1)
        def _():
            for r in range(3):
                half_from_chip(r, 0).wait_recv()
                half_from_chip(r, 1).wait_recv()
            half_start = pl.multiple_of(PACK_HALF * c, 8)
            total_scr[pl.ds(half_start, PACK_FINE), :] = ((fine[0] + fine[1]) + fine[2]) + fine[3]
            total_scr[pl.ds(half_start + PACK_FINE, PACK_HALF - PACK_FINE), :] = (
                (bulk[0].astype(F32) + bulk[1].astype(F32)) + bulk[2].astype(F32)) + bulk[3].astype(F32)
            total_to_sibling().start()
            for a in range(n):
                rows = p_refs[a].shape[1]
                for r in range(3):
                    to_chip(a, r).wait_recv()
                    for s in range(0, rows, row_chunk):
                        sl = pl.ds(s, row_chunk)
                        g_refs[a][sl, :] = g_refs[a][sl, :] + chip_in[a][r, sl, :].astype(F32)
            total_from_sibling().wait_recv()
            total_ref[...] = total_scr[...]
            for a in range(n):
                all_from_sibling(a).wait_send()
                for r in range(3):
                    to_chip(a, r).wait_send()
            pack_to_sibling().wait_send()
            for r in range(3):
                half_to_chip(r, 0).wait_send()
                half_to_chip(r, 1).wait_send()
            total_to_sibling().wait_send()

    fc_specs_in, fc_specs_out, fc_shapes, fc_args = [], [], [], []
    for w, own, arrived, diagonal, m, v in fc:
        rows, cols = w.shape
        blk = pl.BlockSpec((rows // TAIL_STEPS, cols), lambda i: (i, 0))
        fc_specs_in += [blk, blk, pl.BlockSpec((2, rows // TAIL_STEPS, cols), lambda i: (0, i, 0)), blk, blk, blk]
        fc_specs_out += [blk] * 4
        fc_shapes += [jax.ShapeDtypeStruct((rows, cols), F32)] * 4
        fc_args += [w, own, arrived, diagonal, m, v]
    outs = pl.pallas_call(
        body, name="tail_comm", grid=(TAIL_STEPS,),
        out_shape=tuple([jax.ShapeDtypeStruct(p.shape[1:], F32) for p in parts]
                        + [jax.ShapeDtypeStruct((PACK_ROWS, D), F32)] + fc_shapes),
        in_specs=[_resident(p.shape) for p in parts] + [_resident(small.shape)] + fc_specs_in,
        out_specs=tuple([_full(p.shape[1:]) for p in parts] + [_full((PACK_ROWS, D))] + fc_specs_out),
        scratch_shapes=(
            [pltpu.VMEM((4,) + p.shape[1:], BF16) for p in parts]
            + [pltpu.VMEM((3,) + p.shape[1:], BF16) for p in parts]
            + [pltpu.VMEM((3,) + p.shape[1:], BF16) for p in parts]
            + [pltpu.VMEM((PACK_ROWS, D), F32), pltpu.VMEM((PACK_ROWS, D), F32),
               pltpu.VMEM((4, PACK_FINE, D), F32), pltpu.VMEM((4, PACK_HALF - PACK_FINE, D), BF16),
               pltpu.VMEM((PACK_ROWS, D), F32)]
            + [pltpu.SemaphoreType.DMA((n,)), pltpu.SemaphoreType.DMA((n,)),
               pltpu.SemaphoreType.DMA((3 * n,)), pltpu.SemaphoreType.DMA((3 * n,)),
               pltpu.SemaphoreType.DMA((8,)), pltpu.SemaphoreType.DMA((8,))]),
        compiler_params=pltpu.CompilerParams(dimension_semantics=("arbitrary",), vmem_limit_bytes=VMEM_LIMIT),
    )(*parts, small, *fc_args)
    return outs[:n + 1], [outs[n + 1 + 4 * k:n + 5 + 4 * k] for k in range(n_fc)]


def _tril_mask():
    row = lax.broadcasted_iota(jnp.int32, (CHUNK, CHUNK), 0)
    col = lax.broadcasted_iota(jnp.int32, (CHUNK, CHUNK), 1)
    return (col <= row).astype(F32)


def _window_sums(ext):
    s2 = ext + pltpu.roll(ext, 1, 0)
    t4 = s2[:, GROUP:]
    s4 = t4 + pltpu.roll(t4, 2, 0)
    t8 = s4[:, GROUP:]
    s8 = t8 + pltpu.roll(t8, 4, 0)
    t16 = s8[:, GROUP:]
    s16 = t16 + pltpu.roll(t16, 8, 0)
    return [s2[:, :GROUP], s4[:, :GROUP], s8[:, :GROUP], s16]


def _inv_counts(first_pos, rows):
    pos = first_pos + lax.broadcasted_iota(jnp.int32, (rows, 1), 0)
    return [1.0 / jnp.minimum(pos + 1, w).astype(F32) for w in WINDOWS]


def _pool_diff(zb, halo, first_pos):
    tt = zb.shape[0]
    sums = _window_sums(jnp.concatenate([halo, zb], axis=0))
    inv = _inv_counts(first_pos, tt)
    return [sums[g][HALO:, :] * inv[g] - zb[:, g * GROUP:(g + 1) * GROUP] for g in range(len(WINDOWS))]


def _row_blocks(scr, rows, place):
    def block(rel):
        start = pl.multiple_of(rows * _index(_peer(*place, rel)), rows)
        return scr.at[pl.ds(start, rows), :]

    def entries(shard_ref):
        return ([(shard_ref, block(0), rel) for rel in (1, 2, 4, 6)]
                + [(block(rel), block(rel), 1) for rel in (2, 4, 6)])
    return block, entries


def _attn_fwd(tt, x, c8, w_ada, b_my, n1pre, n1post, w_in_shard, w_out_shard, w_sp, bs_rows, ln_g, ln_b, w_pool,
              b_pool, pool_scale, fc_shards, n2pre):
    t_len = x.shape[0]
    nt = t_len // tt
    ncol = w_ada.shape[1]

    def body(x_ref, xb_ref, c_ref, wada_ref, b_ref, n1pre_ref, n1post_ref, wi_ref, wo_ref, wsp_ref, bs_ref,
             lng_ref, lnb_ref, wp_ref, bp_ref, ps_ref, w1_ref, w2_ref, n2pre_ref,
             z_ref, cat_ref, mix_ref, x1_ref, h2_ref, r_ref, f_ref, mod_out, sc_out, e1_ref, e2_ref, wout_ref,
             win_out, carry, land1, land2, sib1, sib2, cat_keep, wout_scr, win_ref, mod_ref, cg, mg, part,
             send_sems, recv_sems, local_sems, wo_send, wo_recv, wi_send, wi_recv, ada_send, ada_recv):
        i = pl.program_id(0)
        place = px, py, pc = _place()
        me = _index(place)
        wo_block, wo_entries = _row_blocks(wout_scr, w_out_shard.shape[0], place)
        wi_block, wi_entries = _row_blocks(win_ref, w_in_shard.shape[0], place)
        wo_copies = _Copies(wo_entries(wo_ref), wo_send, wo_recv)
        wi_copies = _Copies(wi_entries(wi_ref), wi_send, wi_recv)
        wo_keep = pltpu.make_async_copy(wout_scr, wout_ref, local_sems.at[8])
        wi_keep = pltpu.make_async_copy(win_ref, win_out, local_sems.at[9])
        ada = _Copies([(c_ref, cg.at[me], k) for k in range(1, N_DEV)]
                      + [(part, mg.at[me], k) for k in range(1, N_DEV)], ada_send, ada_recv)
        copies = _Copies(
            [(w1_ref, sib1, 1), (w2_ref, sib2, 1),
             (w1_ref, land1.at[0], 2), (w2_ref, land2.at[0], 2),
             (w1_ref, land1.at[1], 4), (w2_ref, land2.at[1], 4),
             (land1.at[0], e1_ref.at[3], 1), (land2.at[0], e2_ref.at[3], 1),
             (land1.at[1], e1_ref.at[5], 1), (land2.at[1], e2_ref.at[5], 1)],
            send_sems, recv_sems)
        keep = [pltpu.make_async_copy(w1_ref, e1_ref.at[0], local_sems.at[0]),
                pltpu.make_async_copy(w2_ref, e2_ref.at[0], local_sems.at[1]),
                pltpu.make_async_copy(land1.at[0], e1_ref.at[2], local_sems.at[2]),
                pltpu.make_async_copy(land1.at[1], e1_ref.at[4], local_sems.at[3]),
                pltpu.make_async_copy(land2.at[0], e2_ref.at[2], local_sems.at[4]),
                pltpu.make_async_copy(land2.at[1], e2_ref.at[4], local_sems.at[5]),
                pltpu.make_async_copy(sib1, e1_ref.at[1], local_sems.at[6]),
                pltpu.make_async_copy(sib2, e2_ref.at[1], local_sems.at[7])]

        @pl.when(i == 0)
        def _():
            ada.start(*range(N_DEV - 1))
            wi_copies.start(0, 1, 2, 3)
            cg[me] = c_ref[...]
            wi_rows, wo_rows = w_in_shard.shape[0], w_out_shard.shape[0]
            win_ref[pl.ds(pl.multiple_of(wi_rows * me, wi_rows), wi_rows), :] = wi_ref[...]
            wout_scr[pl.ds(pl.multiple_of(wo_rows * me, wo_rows), wo_rows), :] = wo_ref[...]
            carry[...] = jnp.zeros_like(carry)

            ada.wait_recv(*range(N_DEV - 1))
            c_all = jnp.concatenate([cg[j, 0:1, :] for j in range(N_DEV)], axis=0)
            sc = c_all * jax.nn.sigmoid(c_all)
            sc_out[...] = sc
            part[...] = _dot(sc.astype(BF16), wada_ref[...].astype(BF16)) + b_ref[...]
            ada.start(*range(N_DEV - 1, 2 * (N_DEV - 1)))
            wo_copies.start(0, 1, 2, 3)
            copies.start(0, 1, 2, 4, 3, 5)
            keep[0].start()
            keep[1].start()
            mg[me] = part[...]
            ada.wait_recv(*range(N_DEV - 1, 2 * (N_DEV - 1)))
            mod_ref[...] = jnp.zeros_like(mod_ref)
            for j in range(N_DEV):
                for m in range(6):
                    lo, hi = max(ncol * j, D * m), min(ncol * (j + 1), D * (m + 1))
                    if lo < hi:
                        mod_ref[m:m + 1, lo - D * m:hi - D * m] = mg[j, pl.ds(me, 1), lo - ncol * j:hi - ncol * j]
            mod_out[...] = mod_ref[...]

            wi_copies.wait_recv(1, 2, 3)
            wi_copies.start(4, 5, 6)
            wi_copies.wait_recv(0, 4, 5, 6)
            wi_keep.start()

        @pl.when(i == nt // 2 + ATTN_LAG)
        def _():
            copies.wait_recv(2, 4)
            copies.start(6, 8)
            keep[2].start()
            keep[3].start()

        @pl.when(i == nt - 1 + ATTN_LAG)
        def _():
            copies.wait_recv(3, 5)
            copies.start(7, 9)
            keep[4].start()
            keep[5].start()

        shift1, scale1, gate1 = mod_ref[0:1, :], mod_ref[1:2, :], mod_ref[2:3, :]

        @pl.when(i < nt)
        def _():
            xv = x_ref[...]
            h1 = (xv * _rstd(xv)) * (n1pre_ref[...] * (1.0 + scale1)) + shift1
            z = _dot_nt(h1.astype(BF16), win_ref[...])
            z_ref[...] = z

            _, ga = _gelu_parts(z[:, :2 * D_A])
            u, vr = ga[:, :D_A], ga[:, D_A:]
            dv = vr - jnp.mean(vr, axis=-1, keepdims=True)
            v = (dv * lax.rsqrt(jnp.mean(dv * dv, axis=-1, keepdims=True) + EPS)) * lng_ref[...] + lnb_ref[...]
            vb = v.astype(BF16)
            mask = _tril_mask()
            wc = [(wsp_ref[h] * mask).astype(BF16) for h in range(N_HEADS)]
            for ch in range(tt // CHUNK):
                rows = slice(ch * CHUNK, (ch + 1) * CHUNK)
                for h in range(N_HEADS):
                    cols = slice(h * GROUP, (h + 1) * GROUP)
                    mixed = _dot(wc[h], vb[rows, cols]) + bs_ref[:, cols]
                    cat_ref[rows, cols] = (u[rows, cols] * mixed).astype(BF16)

            zb = z[:, 2 * D_A:]
            diff = _pool_diff(zb, carry[...], i * tt)
            carry[...] = zb[tt - HALO:, :]
            for g in range(len(WINDOWS)):
                cols = slice(g * GROUP, (g + 1) * GROUP)
                pre = _dot(diff[g].astype(BF16), wp_ref[g].astype(BF16)) + bp_ref[:, cols]
                cat_ref[:, D_A + g * GROUP:D_A + (g + 1) * GROUP] = (pre * ps_ref[:, cols]).astype(BF16)
            cat_keep[i % (ATTN_LAG + 1)] = cat_ref[...]

        @pl.when(i == 0)
        def _():
            copies.wait_recv(0, 1)
            keep[6].start()
            keep[7].start()

        @pl.when(i == 1)
        def _():
            wo_copies.wait_recv(1, 2, 3)
            wo_copies.start(4, 5, 6)

        @pl.when(i == ATTN_LAG)
        def _():
            wo_copies.wait_recv(0, 4, 5, 6)
            wo_keep.start()

        @pl.when(i >= ATTN_LAG)
        def _():
            xv = xb_ref[...]
            mix = _dot(cat_keep[(i - ATTN_LAG) % (ATTN_LAG + 1)], wout_scr[...])
            mix_ref[...] = mix
            x1v = xv + (mix * _rstd(mix)) * (gate1 * n1post_ref[...])
            x1_ref[...] = x1v
            shift2, scale2 = mod_ref[3:4, :], mod_ref[4:5, :]
            h2 = ((x1v * _rstd(x1v)) * (n2pre_ref[...] * (1.0 + scale2)) + shift2).astype(BF16)
            h2_ref[...] = h2
            for j, (w1, w2) in enumerate(((w1_ref, w2_ref), (sib1, sib2))):
                ra = jnp.maximum(_dot(h2, w1[...]), 0.0)
                r = (ra * ra).astype(BF16)
                r_ref[:, j * FF_BLK:(j + 1) * FF_BLK] = r
                if j == 0:
                    f_ref[...] = _dot(r, w2[...])
                else:
                    f_ref[...] += _dot(r, w2[...])

        @pl.when(i == nt - 1 + ATTN_LAG)
        def _():
            copies.wait_recv(6, 7, 8, 9)
            copies.wait_send(*range(10))
            wo_copies.wait_send(*range(7))
            wi_copies.wait_send(*range(7))
            ada.wait_send(*range(2 * (N_DEV - 1)))
            for cp in keep:
                cp.wait()
            wo_keep.wait()
            wi_keep.wait()

    first = lambda w: pl.BlockSpec((tt, w), lambda i: (jnp.minimum(i, nt - 1), 0))
    second = lambda w: pl.BlockSpec((tt, w), lambda i: (jnp.maximum(i - ATTN_LAG, 0), 0))
    r_head = pl.BlockSpec((tt, FC_HEAD * FF_BLK),
                          lambda i: (jnp.maximum(i - ATTN_LAG, 0), R_HEAD_COLS // (FC_HEAD * FF_BLK)))
    hbm = pl.BlockSpec(memory_space=pl.ANY)
    outs = pl.pallas_call(
        body, name="attn_fwd", grid=(nt + ATTN_LAG,),
        out_shape=tuple([jax.ShapeDtypeStruct((t_len, D_Z), F32), jax.ShapeDtypeStruct((t_len, D), BF16),
                         jax.ShapeDtypeStruct((t_len, D), F32), jax.ShapeDtypeStruct((t_len, D), F32),
                         jax.ShapeDtypeStruct((t_len, D), BF16),
                         jax.ShapeDtypeStruct((t_len, FC_EARLY * FF_BLK), BF16),
                         jax.ShapeDtypeStruct((t_len, D), F32)]
                        + [jax.ShapeDtypeStruct((8, D), F32), jax.ShapeDtypeStruct((N_DEV, D), F32)]
                        + [jax.ShapeDtypeStruct((FC_EARLY,) + s.shape, BF16) for s in fc_shards]
                        + [jax.ShapeDtypeStruct((D, D), BF16), jax.ShapeDtypeStruct((D_Z, D), BF16)]),
        in_specs=[first(D), second(D), _full((8, D)), _resident(w_ada.shape), _full((1, ncol)), _full((1, D)),
                  _full((1, D)), _resident(w_in_shard.shape), _resident(w_out_shard.shape),
                  _full((N_HEADS, CHUNK, CHUNK)), _full((CHUNK, D_A)), _full((1, D_A)), _full((1, D_A)),
                  _full((len(WINDOWS), GROUP, GROUP)), _full((1, D_B)), _full((1, D_B)),
                  _resident(fc_shards[0].shape), _resident(fc_shards[1].shape), _full((1, D))],
        out_specs=(first(D_Z), first(D), second(D), second(D), second(D), r_head, second(D),
                   _full((8, D)), _full((N_DEV, D)), hbm, hbm, hbm, hbm),
        scratch_shapes=[pltpu.VMEM((HALO, D_B), F32),
                        pltpu.VMEM((2,) + fc_shards[0].shape, BF16), pltpu.VMEM((2,) + fc_shards[1].shape, BF16),
                        pltpu.VMEM(fc_shards[0].shape, BF16), pltpu.VMEM(fc_shards[1].shape, BF16),
                        pltpu.VMEM((ATTN_LAG + 1, tt, D), BF16), pltpu.VMEM((D, D), BF16),
                        pltpu.VMEM((D_Z, D), BF16), pltpu.VMEM((8, D), F32),
                        pltpu.VMEM((N_DEV, 8, D), F32), pltpu.VMEM((N_DEV, N_DEV, ncol), F32),
                        pltpu.VMEM((N_DEV, ncol), F32),
                        pltpu.SemaphoreType.DMA((10,)), pltpu.SemaphoreType.DMA((10,)),
                        pltpu.SemaphoreType.DMA((10,)),
                        pltpu.SemaphoreType.DMA((7,)), pltpu.SemaphoreType.DMA((7,)),
                        pltpu.SemaphoreType.DMA((7,)), pltpu.SemaphoreType.DMA((7,)),
                        pltpu.SemaphoreType.DMA((2 * (N_DEV - 1),)), pltpu.SemaphoreType.DMA((2 * (N_DEV - 1),))],
        compiler_params=pltpu.CompilerParams(dimension_semantics=("arbitrary",), vmem_limit_bytes=VMEM_LIMIT),
    )(x, x, c8, w_ada, b_my, n1pre, n1post, w_in_shard, w_out_shard, w_sp, bs_rows, ln_g, ln_b, w_pool, b_pool,
      pool_scale, *fc_shards, n2pre)
    return outs[:9], outs[9:11], outs[11], outs[12]


def _mlp_fwd_early(tt, r_begun, h2, f_head, w1_early, w2_early):
    t_len = h2.shape[0]
    nt = t_len // tt
    n_late = N_DEV - FC_EARLY

    def body(r_begun_ref, h2_ref, fh_ref, w1_ref, w2_ref, r_ref, f_ref, l1_ref, l2_ref,
             land1, land2, send_sems, recv_sems, local_sems):
        i = pl.program_id(0)
        copies = _Copies(
            [(w1_ref.at[2], land1, 4), (w2_ref.at[4], land2, 2),
             (land1, l1_ref.at[1], 1), (land2, l2_ref.at[1], 1)],
            send_sems, recv_sems)
        keep = [pltpu.make_async_copy(land1, l1_ref.at[0], local_sems.at[0]),
                pltpu.make_async_copy(land2, l2_ref.at[0], local_sems.at[1])]

        @pl.when(i == 0)
        def _():
            copies.start(0, 1)

        @pl.when(i == nt - 1)
        def _():
            copies.wait_recv(0, 1)
            copies.start(2, 3)
            for cp in keep:
                cp.start()

        h2 = h2_ref[...]
        f_ref[...] = fh_ref[...]
        for j in range(FC_HEAD, FC_EARLY):
            ra = jnp.maximum(_dot(h2, w1_ref[j]), 0.0)
            r = (ra * ra).astype(BF16)
            r_ref[:, _early_col(j):_early_col(j) + FF_BLK] = r
            f_ref[...] += _dot(r, w2_ref[j])

        @pl.when(i == nt - 1)
        def _():
            copies.wait_recv(2, 3)
            copies.wait_send(0, 1, 2, 3)
            for cp in keep:
                cp.wait()

    tile = lambda w: pl.BlockSpec((tt, w), lambda i: (i, 0))
    hbm = pl.BlockSpec(memory_space=pl.ANY)
    outs = pl.pallas_call(
        body, name="mlp_fwd_early", grid=(nt,),
        out_shape=(jax.ShapeDtypeStruct((t_len, FC_EARLY * FF_BLK), BF16), jax.ShapeDtypeStruct((t_len, D), F32),
                   jax.ShapeDtypeStruct((n_late,) + w1_early.shape[1:], BF16),
                   jax.ShapeDtypeStruct((n_late,) + w2_early.shape[1:], BF16)),
        in_specs=[hbm, tile(D), tile(D), _resident((FC_EARLY, D, FF_BLK)), _resident((FC_EARLY, FF_BLK, D))],
        out_specs=(tile(R_HEAD_COLS), tile(D), hbm, hbm),
        input_output_aliases={0: 0},
        scratch_shapes=[pltpu.VMEM(w1_early.shape[1:], BF16), pltpu.VMEM(w2_early.shape[1:], BF16),
                        pltpu.SemaphoreType.DMA((4,)), pltpu.SemaphoreType.DMA((4,)),
                        pltpu.SemaphoreType.DMA((2,))],
        compiler_params=pltpu.CompilerParams(dimension_semantics=("arbitrary",), vmem_limit_bytes=VMEM_LIMIT),
    )(r_begun, h2, f_head, w1_early, w2_early)
    return outs[:2], outs[2:]


def _mlp_late_bwd(tt, r_early, x1, h2, f_early, tgt, mix, mod, n2pre, n2post, n1post,
                  w1_early, w2_early, w1_late, w2_late):
    t_len = x1.shape[0]
    nt = t_len // tt
    n_late = N_DEV - FC_EARLY
    late_cols = n_late * FF_BLK

    def body(re_ref, x1_ref, h2_ref, fe_ref, tgt_ref, mix_ref, mod_ref, n2pre_ref, n2post_ref,
             n1post_ref, w1e_ref, w2e_ref, w1l_ref, w2l_ref,
             rl_ref, df_ref, da_ref, dmix_ref, dx1_ref, redf_ref, redb_ref, dh2_acc):
        i = pl.program_id(0)

        @pl.when(i == 0)
        def _():
            redf_ref[...] = jnp.zeros_like(redf_ref)
            redb_ref[...] = jnp.zeros_like(redb_ref)

        x1v = x1_ref[...]
        gate1, scale2, gate2 = mod_ref[2:3, :], mod_ref[4:5, :], mod_ref[5:6, :]
        h2 = h2_ref[...]
        f = fe_ref[...]
        for j in range(n_late):
            cols = slice(j * FF_BLK, (j + 1) * FF_BLK)
            ra = jnp.maximum(_dot(h2, w1l_ref[j]), 0.0)
            r = (ra * ra).astype(BF16)
            rl_ref[:, cols] = r
            f = f + _dot(r, w2l_ref[j])
        post2 = n2post_ref[...]
        gate_post2 = gate2 * post2
        rf = _rstd(f)
        fhat = f * rf
        err = (x1v + fhat * gate_post2) - tgt_ref[...]
        dy = err * (1.0 / D)
        d_f, sum_f = _rms_bwd_gained(dy, gate_post2, fhat, rf)
        dfv = d_f.astype(BF16)
        df_ref[...] = dfv
        redf_ref[0:1, :] += post2 * sum_f
        redf_ref[1:2, :] += gate2 * sum_f
        redf_ref[2:3, :] += _colsum(err * err)

        for j in range(N_DEV):
            cols = slice(j * FF_BLK, (j + 1) * FF_BLK)
            if j < FC_EARLY:
                w1, w2, r = w1e_ref[j], w2e_ref[j], re_ref[:, _early_col(j):_early_col(j) + FF_BLK]
            else:
                jl = j - FC_EARLY
                w1, w2, r = w1l_ref[jl], w2l_ref[jl], rl_ref[:, jl * FF_BLK:(jl + 1) * FF_BLK]
            dr = _dot_nt(dfv, w2)
            da = (dr * (2.0 * jnp.sqrt(r.astype(F32)))).astype(BF16)
            da_ref[:, cols] = da
            contrib = _dot_nt(da, w1)
            if j == 0:
                dh2_acc[...] = contrib
            else:
                dh2_acc[...] += contrib
        dh2 = dh2_acc[...]
        pre2, post1 = n2pre_ref[...], n1post_ref[...]
        r2 = _rstd(x1v)
        xhat = x1v * r2
        d_x1, sum_h = _rms_bwd_gained(dh2, pre2 * (1.0 + scale2), xhat, r2)
        dx1 = dy + d_x1
        dx1_ref[...] = dx1
        mixv = mix_ref[...]
        rm = _rstd(mixv)
        mhat = mixv * rm
        d_mix, sum_m = _rms_bwd_gained(dx1, gate1 * post1, mhat, rm)
        dmix_ref[...] = d_mix.astype(BF16)
        redb_ref[0:1, :] += _colsum(dh2)
        redb_ref[1:2, :] += pre2 * sum_h
        redb_ref[2:3, :] += (1.0 + scale2) * sum_h
        redb_ref[3:4, :] += post1 * sum_m
        redb_ref[4:5, :] += gate1 * sum_m

    tile = lambda w: pl.BlockSpec((tt, w), lambda i: (i, 0))
    return pl.pallas_call(
        body, name="mlp_late_bwd", grid=(nt,),
        out_shape=(jax.ShapeDtypeStruct((t_len, late_cols), BF16), jax.ShapeDtypeStruct((t_len, D), BF16),
                   jax.ShapeDtypeStruct((t_len, D_FF), BF16), jax.ShapeDtypeStruct((t_len, D), BF16),
                   jax.ShapeDtypeStruct((t_len, D), F32), jax.ShapeDtypeStruct((8, D), F32),
                   jax.ShapeDtypeStruct((8, D), F32)),
        in_specs=[tile(FC_EARLY * FF_BLK), tile(D), tile(D), tile(D), tile(D),
                  tile(D), _full((8, D)), _full((1, D)), _full((1, D)), _full((1, D)),
                  _resident((FC_EARLY, D, FF_BLK)), _resident((FC_EARLY, FF_BLK, D)),
                  _resident((n_late, D, FF_BLK)), _resident((n_late, FF_BLK, D))],
        out_specs=(tile(late_cols), tile(D), tile(D_FF), tile(D), tile(D), _full((8, D)), _full((8, D))),
        scratch_shapes=[pltpu.VMEM((tt, D), F32)],
        compiler_params=pltpu.CompilerParams(dimension_semantics=("arbitrary",), vmem_limit_bytes=VMEM_LIMIT),
    )(r_early, x1, h2, f_early, tgt, mix, mod, n2pre, n2post, n1post, w1_early, w2_early, w1_late, w2_late)


def _mlp_wgrad(tt, r_early, r_late, da, df, h2):
    t_len = df.shape[0]
    nt = t_len // tt
    odd_steps = [j for j, rel in enumerate(WGRAD_ORDER) if rel % 2]

    def relation(j):
        rel = jnp.int32(WGRAD_ORDER[-1])
        for step in range(N_DEV - 2, -1, -1):
            rel = jnp.where(j == step, WGRAD_ORDER[step], rel)
        return rel

    def body(re_ref, rl_ref, da_ref, df_ref, h2_ref, own1_ref, own2_ref, out1_ref, out2_ref, diag1_ref, diag2_ref,
             acc1, acc2, snd1, snd2, sib1, sib2, dsnd1, dsnd2, send_sems, recv_sems):
        j, t = pl.program_id(0), pl.program_id(1)
        rows = pl.ds(pl.multiple_of(t * tt, tt), tt)
        x, y, c = _place()
        accs, snds, sibs = (acc1, acc2), (snd1, snd2), (sib1, sib2)
        dsnds, diags = (dsnd1, dsnd2), (diag1_ref, diag2_ref)

        def to_sibling(a, jj, buf=0):
            return pltpu.make_async_remote_copy(
                src_ref=snds[a].at[buf], dst_ref=sibs[a].at[jj],
                send_sem=send_sems.at[4 * a + jj], recv_sem=recv_sems.at[4 * a + jj],
                device_id=(x, y, 1 - c), device_id_type=MESH)

        def to_diagonal(a):
            return pltpu.make_async_remote_copy(
                src_ref=dsnds[a], dst_ref=diags[a], send_sem=send_sems.at[8 + a], recv_sem=recv_sems.at[8 + a],
                device_id=_peer(x, y, c, 6), device_id_type=MESH)

        @pl.when(t == 0)
        def _():
            acc2[...] = jnp.zeros_like(acc2)
            acc1[...] = jnp.zeros_like(acc1)

        for r_ref, mine in ((re_ref, relation(j) < FC_EARLY), (rl_ref, relation(j) >= FC_EARLY)):
            @pl.when(mine)
            def _():
                acc2[...] += _dot_tn(r_ref[...], df_ref[rows, :])
                acc1[...] += _dot_tn(h2_ref[rows, :], da_ref[...])

        for step, rel in enumerate(WGRAD_ORDER):
            jj = rel // 2

            @pl.when((t == nt - 1) & (j == step))
            def _():
                for a, (own_ref, out_ref) in enumerate(((own1_ref, out1_ref), (own2_ref, out2_ref))):
                    if rel % 2:
                        q = odd_steps.index(step)
                        if q >= 2:
                            to_sibling(a, WGRAD_ORDER[odd_steps[q - 2]] // 2).wait_send()
                        snds[a][q % 2] = accs[a][...].astype(BF16)
                        to_sibling(a, jj, q % 2).start()
                        continue
                    to_sibling(a, jj).wait_recv()
                    chip_sum = accs[a][...] + sibs[a][jj].astype(F32)
                    if rel == 6:
                        dsnds[a][...] = chip_sum.astype(BF16)
                        to_diagonal(a).start()
                    elif rel == 0:
                        own_ref[...] = chip_sum
                    else:
                        out_ref[0] = chip_sum.astype(BF16)
                    if step == N_DEV - 1:
                        for q in (2, 3):
                            to_sibling(a, WGRAD_ORDER[odd_steps[q]] // 2).wait_send()
                        to_diagonal(a).wait_recv()
                        to_diagonal(a).wait_send()

    assert WGRAD_ORDER[-1] == 0 and WGRAD_ORDER[-3:-1] == (2, 4)
    blk = pl.BlockSpec((tt, FF_BLK), lambda j, t: (t, relation(j)))
    early_block = lambda rel: jnp.where(rel < FC_HEAD, rel + FC_EARLY - FC_HEAD, rel - FC_HEAD)
    early = lambda j, t: (jnp.where(relation(j) < FC_EARLY, t, 0),
                          jnp.where(relation(j) < FC_EARLY, early_block(relation(j)), 0))
    late = lambda j, t: (jnp.where(relation(j) < FC_EARLY, 0, t), jnp.maximum(relation(j) - FC_EARLY, 0))
    chip = lambda j, t: (jnp.clip(j - 5, 0, 1), 0, 0)
    hbm = pl.BlockSpec(memory_space=pl.ANY)
    return pl.pallas_call(
        body, name="mlp_wgrad", grid=(N_DEV, nt),
        out_shape=(jax.ShapeDtypeStruct((D, FF_BLK), F32), jax.ShapeDtypeStruct((FF_BLK, D), F32),
                   jax.ShapeDtypeStruct((2, D, FF_BLK), BF16), jax.ShapeDtypeStruct((2, FF_BLK, D), BF16),
                   jax.ShapeDtypeStruct((D, FF_BLK), BF16), jax.ShapeDtypeStruct((FF_BLK, D), BF16)),
        in_specs=[pl.BlockSpec((tt, FF_BLK), early), pl.BlockSpec((tt, FF_BLK), late), blk,
                  _resident((t_len, D)), _resident((t_len, D))],
        out_specs=(_full((D, FF_BLK)), _full((FF_BLK, D)),
                   pl.BlockSpec((1, D, FF_BLK), chip), pl.BlockSpec((1, FF_BLK, D), chip), hbm, hbm),
        scratch_shapes=[pltpu.VMEM((D, FF_BLK), F32), pltpu.VMEM((FF_BLK, D), F32),
                        pltpu.VMEM((2, D, FF_BLK), BF16), pltpu.VMEM((2, FF_BLK, D), BF16),
                        pltpu.VMEM((4, D, FF_BLK), BF16), pltpu.VMEM((4, FF_BLK, D), BF16),
                        pltpu.VMEM((D, FF_BLK), BF16), pltpu.VMEM((FF_BLK, D), BF16),
                        pltpu.SemaphoreType.DMA((10,)), pltpu.SemaphoreType.DMA((10,))],
        compiler_params=pltpu.CompilerParams(dimension_semantics=("arbitrary", "arbitrary"),
                                             vmem_limit_bytes=VMEM_LIMIT),
    )(r_early, r_late, da, df, h2)


def _acc_rows(ref, row0, k, val):
    half = CHUNK // 2
    ref[row0:row0 + half, k * GROUP:(k + 1) * GROUP] += val[:half, :]
    ref[row0:row0 + half, D_A + k * GROUP:D_A + (k + 1) * GROUP] += val[half:, :]


def _attn_bwd(tt, dmix, dx1, x, z, cat, mod, n1pre, w_in_t, w_out, w_sp, bs_rows, ln_g, ln_b, w_pool, b_pool,
              pool_scale, red_fwd, red_bwd, chip_sums):
    t_len = x.shape[0]
    nt = t_len // tt
    hb = tt // HALO
    n_sums = len(chip_sums)

    def body(dmix_ref, dx1_ref, x_ref, z_ref, zprev_ref, cat_ref, mod_ref, n1pre_ref, win_ref, wout_ref, wsp_ref,
             bs_ref, lng_ref, lnb_ref, wp_ref, bp_ref, ps_ref, redf_ref, redb_ref, *rest):
        sum_out = rest[:n_sums]
        gx_ref, gwin_ref, gwout_ref, small_ref = rest[n_sums:n_sums + 4]
        sum_in = rest[n_sums + 4:2 * n_sums + 4]
        carry, acc_in, acc_out, dz_scr, bs_acc, send_sems, recv_sems = rest[2 * n_sums + 4:]
        s = pl.program_id(0)
        i = nt - 1 - s
        px, py, pc = _place()

        def chip_copy(a, r):
            return pltpu.make_async_remote_copy(
                src_ref=sum_out[a].at[r], dst_ref=sum_in[a].at[r],
                send_sem=send_sems.at[2 * a + r], recv_sem=recv_sems.at[2 * a + r],
                device_id=_peer(px, py, pc, 2 * (r + 1)), device_id_type=MESH)

        @pl.when(s == 0)
        def _():
            for a in range(n_sums):
                for r in range(2):
                    chip_copy(a, r).start()
            carry[...] = jnp.zeros_like(carry)
            acc_in[...] = jnp.zeros_like(acc_in)
            acc_out[...] = jnp.zeros_like(acc_out)
            bs_acc[...] = jnp.zeros_like(bs_acc)
            small_ref[...] = jnp.zeros_like(small_ref)
            small_ref[ROW_DMOD + 2:ROW_DMOD + 3, :] = redb_ref[3:4, :]
            small_ref[ROW_DMOD + 3:ROW_DMOD + 5, :] = redb_ref[0:2, :]
            small_ref[ROW_DMOD + 5:ROW_DMOD + 6, :] = redf_ref[0:1, :]
            small_ref[ROW_N1POST:ROW_N1POST + 1, :] = redb_ref[4:5, :]
            small_ref[ROW_N2PRE:ROW_N2PRE + 1, :] = redb_ref[2:3, :]
            small_ref[ROW_N2POST:ROW_N2POST + 1, :] = redf_ref[1:2, :]
            small_ref[ROW_LOSS:ROW_LOSS + 1, :] = redf_ref[2:3, :]

        dmixv = dmix_ref[...]
        dcat = _dot_nt(dmixv, wout_ref[...])
        acc_out[...] += _dot_tn(cat_ref[...], dmixv)

        z = z_ref[...]
        t_g, ga = _gelu_parts(z[:, :2 * D_A])
        u, vr = ga[:, :D_A], ga[:, D_A:]
        dv0 = vr - jnp.mean(vr, axis=-1, keepdims=True)
        rv = lax.rsqrt(jnp.mean(dv0 * dv0, axis=-1, keepdims=True) + EPS)
        vhat = dv0 * rv
        vb = (vhat * lng_ref[...] + lnb_ref[...]).astype(BF16)
        mask = _tril_mask()
        wc = [(wsp_ref[h] * mask).astype(BF16) for h in range(N_HEADS)]

        dya = dcat[:, :D_A]
        for h in range(N_HEADS):
            cols = slice(h * GROUP, (h + 1) * GROUP)
            bs_sum = jnp.zeros((CHUNK, GROUP), F32)
            ws_sum = jnp.zeros((CHUNK, CHUNK), F32)
            for ch in range(tt // CHUNK):
                rows = slice(ch * CHUNK, (ch + 1) * CHUNK)
                v_ch = vb[rows, cols]
                mixed = _dot(wc[h], v_ch) + bs_ref[:, cols]
                dy_ch = dya[rows, cols]
                dz_scr[rows, cols] = dy_ch * mixed
                dmixed = dy_ch * u[rows, cols]
                dmb = dmixed.astype(BF16)
                dz_scr[rows, D_A + h * GROUP:D_A + (h + 1) * GROUP] = _dot_tn(wc[h], dmb)
                bs_sum = bs_sum + dmixed
                ws_sum = ws_sum + _dot_nt(dmb, v_ch)
            _acc_rows(bs_acc, 0, h, bs_sum)
            _acc_rows(small_ref, ROW_WS, h, ws_sum)

        dvl = dz_scr[:, D_A:2 * D_A]
        dvhat = dvl * lng_ref[...]
        dvl_vhat = dvl * vhat
        dvr = rv * (dvhat - jnp.mean(dvhat, axis=-1, keepdims=True)
                    - vhat * jnp.mean(dvl_vhat * lng_ref[...], axis=-1, keepdims=True))
        small_ref[ROW_LN:ROW_LN + 1, 0:D_A] += _colsum(dvl_vhat)
        small_ref[ROW_LN:ROW_LN + 1, D_A:D] += _colsum(dvl)
        dga = jnp.concatenate([dz_scr[:, :D_A], dvr], axis=1)
        dza = dga * _gelu_grad(z[:, :2 * D_A], t_g)

        zb = z[:, 2 * D_A:]
        halo_prev = jnp.where(i == 0, 0.0, zprev_ref[...])
        diff = _pool_diff(zb, halo_prev, i * tt)
        dyb = dcat[:, D_A:]
        inv = _inv_counts(i * tt, tt)
        scaled, ddiffs = [], []
        for g in range(len(WINDOWS)):
            cols = slice(g * GROUP, (g + 1) * GROUP)
            db = diff[g].astype(BF16)
            wpg = wp_ref[g].astype(BF16)
            pre = _dot(db, wpg) + bp_ref[:, cols]
            small_ref[ROW_POOL:ROW_POOL + 1, cols] += _colsum(dyb[:, cols] * pre)
            dpre = dyb[:, cols] * ps_ref[:, cols]
            small_ref[ROW_POOL:ROW_POOL + 1, D_B + g * GROUP:D_B + (g + 1) * GROUP] += _colsum(dpre)
            dpb = dpre.astype(BF16)
            _acc_rows(small_ref, ROW_WP, g, _dot_tn(db, dpb))
            ddiff = _dot_nt(dpb, wpg)
            ddiffs.append(ddiff)
            scaled.append(ddiff * inv[g])
        scaled_all = jnp.concatenate(scaled, axis=1)
        ext = jnp.concatenate([scaled_all, carry[...]], axis=0)
        n_ext = tt + HALO
        s2 = ext + pltpu.roll(ext, n_ext - 1, 0)
        t4 = s2[:, GROUP:]
        s4 = t4 + pltpu.roll(t4, n_ext - 2, 0)
        t8 = s4[:, GROUP:]
        s8 = t8 + pltpu.roll(t8, n_ext - 4, 0)
        t16 = s8[:, GROUP:]
        s16 = t16 + pltpu.roll(t16, n_ext - 8, 0)
        back = [s2[:, :GROUP], s4[:, :GROUP], s8[:, :GROUP], s16]
        carry[...] = scaled_all[:HALO, :]
        dzb = jnp.concatenate([back[g][:tt, :] - ddiffs[g] for g in range(len(WINDOWS))], axis=1)

        dzv = jnp.concatenate([dza, dzb], axis=1).astype(BF16)
        dh1 = _dot(dzv, win_ref[...])
        xv = x_ref[...]
        r1 = _rstd(xv)
        xhat = xv * r1
        shift1, scale1 = mod_ref[0:1, :], mod_ref[1:2, :]
        pre1 = n1pre_ref[...]
        gain1 = pre1 * (1.0 + scale1)
        h1 = (xhat * gain1 + shift1).astype(BF16)
        acc_in[...] += _dot_tn(dzv, h1)
        d_x, sum_h = _rms_bwd_gained(dh1, gain1, xhat, r1)
        gx_ref[...] = dx1_ref[...] + d_x
        small_ref[ROW_DMOD:ROW_DMOD + 1, :] += _colsum(dh1)
        small_ref[ROW_DMOD + 1:ROW_DMOD + 2, :] += pre1 * sum_h
        small_ref[ROW_N1PRE:ROW_N1PRE + 1, :] += (1.0 + scale1) * sum_h

        @pl.when(s == nt - 1)
        def _():
            gwin_ref[...] = acc_in[...].astype(BF16)
            gwout_ref[...] = acc_out[...].astype(BF16)
            bs = _unfold(bs_acc[...])
            for h in range(N_HEADS):
                small_ref[ROW_BS + h:ROW_BS + h + 1, 0:GROUP] = jnp.sum(
                    bs[:, h * GROUP:(h + 1) * GROUP].T, axis=0, keepdims=True)
            for a in range(n_sums):
                for r in range(2):
                    chip_copy(a, r).wait_recv()
                    chip_copy(a, r).wait_send()

    rev = lambda w: pl.BlockSpec((tt, w), lambda s: (nt - 1 - s, 0))
    zprev = pl.BlockSpec((HALO, D_B), lambda s: (jnp.maximum((nt - 1 - s) * hb - 1, 0), 2))
    hbm = pl.BlockSpec(memory_space=pl.ANY)
    outs = pl.pallas_call(
        body, name="attn_bwd", grid=(nt,),
        out_shape=tuple([jax.ShapeDtypeStruct((t_len, D), F32), jax.ShapeDtypeStruct((D_Z, D), BF16),
                         jax.ShapeDtypeStruct((D, D), BF16), jax.ShapeDtypeStruct((SMALL_ROWS, D), F32)]
                        + [jax.ShapeDtypeStruct(cs.shape, cs.dtype) for cs in chip_sums]),
        in_specs=[rev(D), rev(D), rev(D), rev(D_Z), zprev, rev(D), _full((8, D)), _full((1, D)),
                  _resident((D_Z, D)), _resident((D, D)), _full((N_HEADS, CHUNK, CHUNK)), _full((CHUNK, D_A)),
                  _full((1, D_A)), _full((1, D_A)), _full((len(WINDOWS), GROUP, GROUP)), _full((1, D_B)),
                  _full((1, D_B)), _full((8, D)), _full((8, D))] + [_resident(cs.shape) for cs in chip_sums],
        out_specs=tuple([rev(D), _resident((D_Z, D)), _resident((D, D)), _full((SMALL_ROWS, D))] + [hbm] * n_sums),
        scratch_shapes=[pltpu.VMEM((HALO, D_B), F32), pltpu.VMEM((D_Z, D), F32), pltpu.VMEM((D, D), F32),
                        pltpu.VMEM((tt, 2 * D_A), F32), pltpu.VMEM((CHUNK // 2, D), F32),
                        pltpu.SemaphoreType.DMA((2 * n_sums,)), pltpu.SemaphoreType.DMA((2 * n_sums,))],
        compiler_params=pltpu.CompilerParams(dimension_semantics=("arbitrary",), vmem_limit_bytes=VMEM_LIMIT),
    )(dmix, dx1, x, z, z, cat, mod, n1pre, w_in_t, w_out, w_sp, bs_rows, ln_g, ln_b, w_pool, b_pool, pool_scale,
      red_fwd, red_bwd, *chip_sums)
    return outs[:4], outs[4:]


def _adam(w, g, m, v):
    m2 = ADAM_B1 * m + (1.0 - ADAM_B1) * g
    v2 = ADAM_B2 * v + (1.0 - ADAM_B2) * (g * g)
    m_hat = m2 / (1.0 - ADAM_B1 ** ADAM_STEP)
    v_hat = v2 / (1.0 - ADAM_B2 ** ADAM_STEP)
    delta = -ADAM_LR * (m_hat / (jnp.sqrt(v_hat) + ADAM_EPS) + ADAM_WD * w)
    return delta, m2, v2


def _adamw_shard(name, rb, w, g, m, v):
    rows, cols = w.shape

    def body(w_ref, g_ref, m_ref, v_ref, d_ref, m2_ref, v2_ref):
        d_ref[...], m2_ref[...], v2_ref[...] = _adam(w_ref[...], g_ref[...], m_ref[...], v_ref[...])

    blk = pl.BlockSpec((rb, cols), lambda i: (i, 0))
    shp = jax.ShapeDtypeStruct((rows, cols), F32)
    return pl.pallas_call(
        body, name=name, grid=(rows // rb,), out_shape=(shp, shp, shp),
        in_specs=[blk] * 4, out_specs=(blk, blk, blk),
        compiler_params=pltpu.CompilerParams(dimension_semantics=("arbitrary",)),
    )(w, g, m, v)


def _adamw_ada(rb, w, sc, dmod_cols, m, v):
    rows, cols = w.shape

    def body(w_ref, sc_ref, dm_ref, m_ref, v_ref, g_ref, d_ref, m2_ref, v2_ref):
        g = _dot_tn(sc_ref[...].astype(BF16), dm_ref[...].astype(BF16))
        g_ref[...] = g
        d_ref[...], m2_ref[...], v2_ref[...] = _adam(w_ref[...], g, m_ref[...], v_ref[...])

    blk = pl.BlockSpec((rb, cols), lambda i: (i, 0))
    shp = jax.ShapeDtypeStruct((rows, cols), F32)
    return pl.pallas_call(
        body, name="adamw_ada", grid=(rows // rb,), out_shape=(shp, shp, shp, shp),
        in_specs=[blk, pl.BlockSpec((N_DEV, rb), lambda i: (0, i)), _full((N_DEV, cols)), blk, blk],
        out_specs=(blk, blk, blk, blk),
        compiler_params=pltpu.CompilerParams(dimension_semantics=("arbitrary",)),
    )(w, sc, dmod_cols, m, v)


def _unfold(acc_rows):
    return jnp.concatenate([acc_rows[:, :D_A], acc_rows[:, D_A:]], axis=0)


def _adamw_small(total, params):
    n = len(params)
    flat = [a for p in params for a in p]

    def body(*refs):
        s_ref = refs[0]
        p_refs = refs[1:1 + 3 * n]
        loss_ref = refs[1 + 3 * n]
        o_refs = refs[2 + 3 * n:]
        d_b_ada = s_ref[0:6, :]
        for b in range(1, N_DEV):
            d_b_ada = d_b_ada + s_ref[_table_row(b):_table_row(b) + 6, :]
        misc = lambda r: s_ref[PK_MISC + r - ROW_N1PRE:PK_MISC + r - ROW_N1PRE + 1, :]
        loss = jnp.sum(misc(ROW_LOSS), axis=-1, keepdims=True) * (0.5 / D)
        loss_ref[...] = jnp.broadcast_to(loss, (8, GROUP))
        mask = _tril_mask()
        ws = _unfold(s_ref[PK_WS:PK_WS + 64, :])
        wp = _unfold(s_ref[PK_WP:PK_WP + 64, :])
        grads = [
            d_b_ada,
            misc(ROW_N1PRE), misc(ROW_N1POST), misc(ROW_N2PRE), misc(ROW_N2POST),
            misc(ROW_LN)[:, :D_A], misc(ROW_LN)[:, D_A:],
            misc(ROW_POOL)[:, :D_B], misc(ROW_POOL)[:, D_B:],
            s_ref[PK_BS:PK_BS + N_HEADS, 0:GROUP],
            jnp.stack([ws[:, h * GROUP:(h + 1) * GROUP] * mask for h in range(N_HEADS)]),
            jnp.stack([wp[:, g * GROUP:(g + 1) * GROUP] for g in range(len(WINDOWS))]),
        ]
        for k in range(n):
            w_ref, m_ref, v_ref = p_refs[3 * k:3 * k + 3]
            g = grads[k]
            o_refs[4 * k][...] = g
            o_refs[4 * k + 1][...], o_refs[4 * k + 2][...], o_refs[4 * k + 3][...] = _adam(
                w_ref[...], g, m_ref[...], v_ref[...])

    vm = pl.BlockSpec(memory_space=pltpu.VMEM)
    out_shape = [jax.ShapeDtypeStruct((8, GROUP), F32)]
    for w, _, _ in params:
        out_shape += [jax.ShapeDtypeStruct(w.shape, F32)] * 4
    return pl.pallas_call(
        body, name="adamw_small", out_shape=tuple(out_shape),
        in_specs=[vm] * (1 + 3 * n), out_specs=tuple([vm] * len(out_shape)),
    )(total, *flat)


TT_ATTN_FWD = 512
ATTN_LAG = 2
TT_MLP_FWD = 512
TT_MLP = 256
TT_WGRAD = 2048
TT_ATTN_BWD = 512


def kernel(x, c, w_ada, b_ada, norm1_pre, norm1_post, w_in, w_spatial, b_spatial, ln_v_gain, ln_v_bias, w_pool, b_pool, pool_scale, w_out, norm2_pre, norm2_post, w_fc1, w_fc2, loss_target, m_w_ada, m_b_ada, m_norm1_pre, m_norm1_post, m_w_in, m_w_spatial, m_b_spatial, m_ln_v_gain, m_ln_v_bias, m_w_pool, m_b_pool, m_pool_scale, m_w_out, m_norm2_pre, m_norm2_post, m_w_fc1, m_w_fc2, v_w_ada, v_b_ada, v_norm1_pre, v_norm1_post, v_w_in, v_w_spatial, v_b_spatial, v_ln_v_gain, v_ln_v_bias, v_w_pool, v_b_pool, v_pool_scale, v_w_out, v_norm2_pre, v_norm2_post, v_w_fc1, v_w_fc2):
    t_len = x.shape[1]
    me = 4 * lax.axis_index("x") + 2 * lax.axis_index("y") + lax.axis_index("c")
    ada_cols = w_ada.shape[1]
    tt = lambda want: min(want, t_len)

    x2 = x.reshape(t_len, D)
    tgt = loss_target.reshape(t_len, D)
    row = lambda a: a.reshape(1, -1)

    b_my = lax.dynamic_slice_in_dim(b_ada, me * ada_cols, ada_cols).reshape(1, ada_cols)
    w_in_shard, w_out_shard, w1_shard, w2_shard = _cast_shards([w_in.T, w_out, w_fc1, w_fc2])

    bs_rows = jnp.repeat(b_spatial.T, GROUP, axis=1)
    attn_consts = (w_spatial, bs_rows, row(ln_v_gain), row(ln_v_bias), w_pool, row(b_pool), row(pool_scale))

    (z, cat, mix, x1, h2, r_begun, f_head, mod, sc), (w1_early, w2_early), w_out_all, w_in_t = _attn_fwd(
        tt(TT_ATTN_FWD), x2, jnp.broadcast_to(c, (8, D)), w_ada, b_my, row(norm1_pre), row(norm1_post),
        w_in_shard, w_out_shard, *attn_consts, (w1_shard, w2_shard), row(norm2_pre))
    (r_early, f_early), (w1_late, w2_late) = _mlp_fwd_early(
        tt(TT_MLP_FWD), r_begun, h2, f_head, w1_early, w2_early)
    r_late, df, da, dmix, dx1, red_fwd, red_bwd = _mlp_late_bwd(
        tt(TT_MLP), r_early, x1, h2, f_early, tgt, mix, mod, row(norm2_pre), row(norm2_post), row(norm1_post),
        w1_early, w2_early, w1_late, w2_late)
    own_w1, own_w2, sums_w1, sums_w2, diag_w1, diag_w2 = _mlp_wgrad(tt(TT_WGRAD), r_early, r_late, da, df, h2)
    (grad_x, p_in, p_out, small), (arr_w1, arr_w2) = _attn_bwd(
        tt(TT_ATTN_BWD), dmix, dx1, x2, z, cat, mod, row(norm1_pre), w_in_t, w_out_all, *attn_consts,
        red_fwd, red_bwd, [sums_w1, sums_w2])
    (grad_in_t, grad_out, total), ((grad_w1, d_w1, m_w1, v_w1), (grad_w2, d_w2, m_w2, v_w2)) = _tail_comm(
        [p_in.reshape(N_DEV, D_Z // N_DEV, D), p_out.reshape(N_DEV, D // N_DEV, D)], small, 64,
        [(w_fc1, own_w1, arr_w1, diag_w1, m_w_fc1, v_w_fc1), (w_fc2, own_w2, arr_w2, diag_w2, m_w_fc2, v_w_fc2)])

    d_out, m_out, v_out = _adamw_shard("adamw_out", 128, w_out, grad_out, m_w_out, v_w_out)
    d_in_t, m_in_t, v_in_t = _adamw_shard("adamw_in", D_Z // N_DEV, w_in.T, grad_in_t, m_w_in.T, v_w_in.T)
    table = jnp.concatenate([total[0:PACK_FINE, :], total[PK_TABLE_B:PK_MISC, :]], axis=0)
    dmod_all = table.reshape(N_DEV, 8, D)[:, :6, :].reshape(N_DEV, 6 * D)
    dmod_cols = lax.dynamic_slice_in_dim(dmod_all, me * ada_cols, ada_cols, axis=1)
    grad_ada, d_ada, m_ada, v_ada = _adamw_ada(256, w_ada, sc, dmod_cols, m_w_ada, v_w_ada)

    six = lambda a: a.reshape(6, D)
    small_params = [
        (six(b_ada), six(m_b_ada), six(v_b_ada)),
        (row(norm1_pre), row(m_norm1_pre), row(v_norm1_pre)),
        (row(norm1_post), row(m_norm1_post), row(v_norm1_post)),
        (row(norm2_pre), row(m_norm2_pre), row(v_norm2_pre)),
        (row(norm2_post), row(m_norm2_post), row(v_norm2_post)),
        (row(ln_v_gain), row(m_ln_v_gain), row(v_ln_v_gain)),
        (row(ln_v_bias), row(m_ln_v_bias), row(v_ln_v_bias)),
        (row(pool_scale), row(m_pool_scale), row(v_pool_scale)),
        (row(b_pool), row(m_b_pool), row(v_b_pool)),
        (b_spatial, m_b_spatial, v_b_spatial),
        (w_spatial, m_w_spatial, v_w_spatial),
        (w_pool, m_w_pool, v_w_pool),
    ]
    outs = _adamw_small(total, small_params)
    loss = outs[0][0, 0]
    names = ["b_ada", "norm1_pre", "norm1_post", "norm2_pre", "norm2_post", "ln_v_gain", "ln_v_bias", "pool_scale",
             "b_pool", "b_spatial", "w_spatial", "w_pool"]
    shapes = dict(b_ada=b_ada.shape, norm1_pre=norm1_pre.shape, norm1_post=norm1_post.shape,
                  norm2_pre=norm2_pre.shape, norm2_post=norm2_post.shape, ln_v_gain=ln_v_gain.shape,
                  ln_v_bias=ln_v_bias.shape, pool_scale=pool_scale.shape, b_pool=b_pool.shape,
                  b_spatial=b_spatial.shape, w_spatial=w_spatial.shape, w_pool=w_pool.shape)
    res = {}
    for k, nm in enumerate(names):
        res[nm] = tuple(o.reshape(shapes[nm]) for o in outs[1 + 4 * k:5 + 4 * k])
    res["w_ada"] = (grad_ada, d_ada, m_ada, v_ada)
    res["w_in"] = (grad_in_t.T, d_in_t.T, m_in_t.T, v_in_t.T)
    res["w_out"] = (grad_out, d_out, m_out, v_out)
    res["w_fc1"] = (grad_w1, d_w1, m_w1, v_w1)
    res["w_fc2"] = (grad_w2, d_w2, m_w2, v_w2)

    order = ["w_ada", "b_ada", "norm1_pre", "norm1_post", "w_in", "w_spatial", "b_spatial", "ln_v_gain", "ln_v_bias",
             "w_pool", "b_pool", "pool_scale", "w_out", "norm2_pre", "norm2_post", "w_fc1", "w_fc2"]
    return (loss, grad_x.reshape(x.shape),
            *[res[nm][0] for nm in order], *[res[nm][1] for nm in order],
            *[res[nm][2] for nm in order], *[res[nm][3] for nm in order])
```

```python
import functools

import jax
import jax.numpy as jnp
from jax import lax
from jax.experimental import pallas as pl
from jax.experimental.pallas import tpu as pltpu

F32 = jnp.float32
BF16 = jnp.bfloat16
MESH = pl.DeviceIdType.MESH

N_DEV = 8
D = 1024
D_A = 512
D_B = 512
D_Z = 2 * D_A + D_B
N_HEADS = 4
CHUNK = 128
WINDOWS = (2, 4, 8, 16)
GROUP = 128
D_FF = 4096
FF_BLK = D_FF // N_DEV
HALO = 16
EPS = 1e-6
VMEM_LIMIT = 60 * 1024 * 1024

ADAM_LR = 0.001
ADAM_B1 = 0.9
ADAM_B2 = 0.999
ADAM_EPS = 1e-08
ADAM_WD = 0.01
ADAM_STEP = 10

ROW_DMOD = 0
ROW_N1PRE, ROW_N1POST, ROW_N2PRE, ROW_N2POST = 8, 9, 10, 11
ROW_LN = 12
ROW_POOL = 13
ROW_LOSS = 14
ROW_BS = 16
ROW_WS = 24
ROW_WP = 88
SMALL_ROWS = 152
PACK_FINE = 40
PACK_HALF = PACK_FINE + 64
PACK_ROWS = 2 * PACK_HALF
PK_WS = PACK_FINE
PK_TABLE_B = PACK_HALF
PK_MISC = PK_TABLE_B + 24
PK_BS = PK_MISC + 8
PK_WP = PK_BS + 8


def _table_row(b):
    if isinstance(b, int):
        return 8 * b if 8 * b < PACK_FINE else 8 * b + PK_TABLE_B - PACK_FINE
    return 8 * b + jnp.where(8 * b < PACK_FINE, 0, PK_TABLE_B - PACK_FINE)


def _dot(a, b):
    return jnp.dot(a, b, preferred_element_type=F32)


def _dot_nt(a, b):
    return lax.dot_general(a, b, (((1,), (1,)), ((), ())), preferred_element_type=F32)


def _dot_tn(a, b):
    return lax.dot_general(a, b, (((0,), (0,)), ((), ())), preferred_element_type=F32)


def _rstd(v):
    return lax.rsqrt(jnp.mean(v * v, axis=-1, keepdims=True) + EPS)


def _rms_bwd(d_hat, hat, rstd):
    return rstd * (d_hat - hat * jnp.mean(d_hat * hat, axis=-1, keepdims=True))


def _rms_bwd_gained(g, gain, hat, rstd):
    g_hat = g * hat
    d_v = rstd * (g * gain - hat * jnp.mean(g_hat * gain, axis=-1, keepdims=True))
    return d_v, _colsum(g_hat)


_K0 = 0.7978845608028654
_K1 = 0.044715


def _gelu_parts(v):
    t = jnp.tanh(v * (_K0 + (_K0 * _K1) * (v * v)))
    return t, v * (0.5 + 0.5 * t)


def _gelu_grad(v, t):
    return (0.5 + 0.5 * t) + (0.5 * v) * (1.0 - t * t) * (_K0 + (3.0 * _K0 * _K1) * (v * v))


def _colsum(v):
    return jnp.sum(v, axis=0, keepdims=True)


def _full(shape):
    n = len(shape)
    return pl.BlockSpec(shape, lambda *_: (0,) * n)


def _resident(shape):
    n = len(shape)
    return pl.BlockSpec(shape, lambda *_: (0,) * n, pipeline_mode=pl.Buffered(1))


def _place():
    x, y, c = lax.axis_index("x"), lax.axis_index("y"), lax.axis_index("c")
    return x, y, c


def _flip(v, bit):
    return 1 - v if bit else v


def _peer(x, y, c, k):
    return (_flip(x, (k >> 2) & 1), _flip(y, (k >> 1) & 1), _flip(c, k & 1))


def _index(p):
    return 4 * p[0] + 2 * p[1] + p[2]


def _cast_shards(shards):
    def body(*refs):
        for src, dst in zip(refs[:len(shards)], refs[len(shards):]):
            dst[...] = src[...].astype(BF16)

    vm = pl.BlockSpec(memory_space=pltpu.VMEM)
    return pl.pallas_call(
        body, name="cast_shards", out_shape=tuple(jax.ShapeDtypeStruct(s.shape, BF16) for s in shards),
        in_specs=[vm] * len(shards), out_specs=tuple([vm] * len(shards)),
    )(*shards)


FC_EARLY = 6
FC_HEAD = 2
R_HEAD_COLS = (FC_EARLY - FC_HEAD) * FF_BLK
WGRAD_ORDER = (7, 6, 1, 3, 5, 2, 4, 0)


def _early_col(j):
    return R_HEAD_COLS + j * FF_BLK if j < FC_HEAD else (j - FC_HEAD) * FF_BLK


class _Copies:
    def __init__(self, entries, send_sems, recv_sems):
        self.place = _place()
        self.entries, self.send_sems, self.recv_sems = entries, send_sems, recv_sems

    def _copy(self, i, arrival=False):
        src, dst, rel = self.entries[i]
        return pltpu.make_async_remote_copy(
            src_ref=dst if arrival else src, dst_ref=dst, send_sem=self.send_sems.at[i],
            recv_sem=self.recv_sems.at[i], device_id=_peer(*self.place, rel), device_id_type=MESH)

    def start(self, *which):
        for i in which:
            self._copy(i).start()

    def wait_recv(self, *which):
        for i in which:
            self._copy(i, arrival=True).wait_recv()

    def wait_send(self, *which):
        for i in which:
            self._copy(i).wait_send()


TAIL_STEPS = 4


def _tail_comm(parts, small, row_chunk, fc):
    n, n_fc = len(parts), len(fc)

    def body(*refs):
        p_refs, small_ref = refs[:n], refs[n]
        fc_in = refs[n + 1:n + 1 + 6 * n_fc]
        outs = refs[n + 1 + 6 * n_fc:]
        g_refs, total_ref = outs[:n], outs[n]
        fc_out = outs[n + 1:n + 1 + 4 * n_fc]
        scr = outs[n + 1 + 4 * n_fc:]
        from_sib = scr[0:n]
        chip_out = scr[n:2 * n]
        chip_in = scr[2 * n:3 * n]
        pack, pack_sib, fine, bulk, total_scr = scr[3 * n:3 * n + 5]
        send_a, recv_a, send_b, recv_b, send_s, recv_s = scr[3 * n + 5:]
        step = pl.program_id(0)
        x, y, c = _place()
        me = _index((x, y, c))
        sibling = (x, y, 1 - c)
        my_chip = 2 * x + y
        others = [(1 - x, y), (x, 1 - y), (1 - x, 1 - y)]
        my_half = pl.ds(pl.multiple_of(PACK_HALF * c, 8), PACK_HALF)

        def pack_to_sibling():
            return pltpu.make_async_remote_copy(
                src_ref=pack, dst_ref=pack_sib, send_sem=send_s.at[0], recv_sem=recv_s.at[0],
                device_id=sibling, device_id_type=MESH)

        def half_to_chip(r, part):
            buf = (fine, bulk)[part]
            return pltpu.make_async_remote_copy(
                src_ref=buf.at[my_chip], dst_ref=buf.at[my_chip],
                send_sem=send_s.at[1 + 3 * part + r], recv_sem=recv_s.at[1 + 3 * part + r],
                device_id=(*others[r], c), device_id_type=MESH)

        def half_from_chip(r, part):
            k = 2 * others[r][0] + others[r][1]
            buf = (fine, bulk)[part]
            return pltpu.make_async_remote_copy(
                src_ref=buf.at[k], dst_ref=buf.at[k],
                send_sem=send_s.at[1 + 3 * part + r], recv_sem=recv_s.at[1 + 3 * part + r],
                device_id=(*others[r], c), device_id_type=MESH)

        def total_to_sibling():
            return pltpu.make_async_remote_copy(
                src_ref=total_scr.at[my_half], dst_ref=total_scr.at[my_half],
                send_sem=send_s.at[7], recv_sem=recv_s.at[7], device_id=sibling, device_id_type=MESH)

        def total_from_sibling():
            sib_half = pl.ds(pl.multiple_of(PACK_HALF * (1 - c), 8), PACK_HALF)
            return pltpu.make_async_remote_copy(
                src_ref=total_scr.at[sib_half], dst_ref=total_scr.at[sib_half],
                send_sem=send_s.at[7], recv_sem=recv_s.at[7], device_id=sibling, device_id_type=MESH)

        def to_sibling(a, k):
            return pltpu.make_async_remote_copy(
                src_ref=p_refs[a].at[2 * k + (1 - c)], dst_ref=from_sib[a].at[k],
                send_sem=send_a.at[a], recv_sem=recv_a.at[a], device_id=sibling, device_id_type=MESH)

        def all_from_sibling(a):
            return pltpu.make_async_remote_copy(
                src_ref=from_sib[a], dst_ref=from_sib[a], send_sem=send_a.at[a], recv_sem=recv_a.at[a],
                device_id=sibling, device_id_type=MESH)

        def to_chip(a, r):
            return pltpu.make_async_remote_copy(
                src_ref=chip_out[a].at[r], dst_ref=chip_in[a].at[r],
                send_sem=send_b.at[3 * a + r], recv_sem=recv_b.at[3 * a + r],
                device_id=(*others[r], c), device_id_type=MESH)

        @pl.when(step == 0)
        def _():
            pack[0:PACK_FINE, :] = jnp.zeros((PACK_FINE, D), F32)
            pack[PK_TABLE_B:PK_MISC, :] = jnp.zeros((PK_MISC - PK_TABLE_B, D), F32)
            pack[pl.ds(pl.multiple_of(_table_row(me), 8), 8), :] = small_ref[0:8, :]
            pack[PK_WS:PK_WS + 64, :] = small_ref[ROW_WS:ROW_WS + 64, :]
            pack[PK_MISC:PK_MISC + 8, :] = small_ref[ROW_N1PRE:ROW_N1PRE + 8, :]
            pack[PK_BS:PK_BS + 8, :] = small_ref[ROW_BS:ROW_BS + 8, :]
            pack[PK_WP:PK_WP + 64, :] = small_ref[ROW_WP:ROW_WP + 64, :]
            pack_to_sibling().start()
            for a in range(n):
                for k in range(4):
                    to_sibling(a, k).start()

        @pl.when(step == 1)
        def _():
            pack_to_sibling().wait_recv()
            chip_sum = pack[my_half, :] + pack_sib[my_half, :]
            fine[my_chip] = chip_sum[:PACK_FINE, :]
            bulk[my_chip] = chip_sum[PACK_FINE:, :].astype(BF16)
            for r in range(3):
                half_to_chip(r, 0).start()
                half_to_chip(r, 1).start()
            for a in range(n):
                all_from_sibling(a).wait_recv()
                rows = p_refs[a].shape[1]
                for r in range(3):
                    k = 2 * others[r][0] + others[r][1]
                    for s in range(0, rows, row_chunk):
                        sl = pl.ds(s, row_chunk)
                        chip_out[a][r, sl, :] = (p_refs[a][2 * k + c, sl, :].astype(F32)
                                                 + from_sib[a][k, sl, :].astype(F32)).astype(BF16)
                    to_chip(a, r).start()
                for s in range(0, rows, row_chunk):
                    sl = pl.ds(s, row_chunk)
                    g_refs[a][sl, :] = (p_refs[a][2 * my_chip + c, sl, :].astype(F32)
                                        + from_sib[a][my_chip, sl, :].astype(F32))

        for k in range(n_fc):
            w_ref, own_ref, arr_ref, diag_ref, m_ref, v_ref = fc_in[6 * k:6 * k + 6]
            g = own_ref[...]
            for r in range(2):
                g = g + arr_ref[r].astype(F32)
            g = g + diag_ref[...].astype(F32)
            fc_out[4 * k][...] = g
            fc_out[4 * k + 1][...], fc_out[4 * k + 2][...], fc_out[4 * k + 3][...] = _adam(
                w_ref[...], g, m_ref[...], v_ref[...])

        @pl.when(step == TAIL_STEPS - 1)
        def _():
            for r in range(3):
                half_from_chip(r, 0).wait_recv()
                half_from_chip(r, 1).wait_recv()
            half_start = pl.multiple_of(PACK_HALF * c, 8)
            total_scr[pl.ds(half_start, PACK_FINE), :] = ((fine[0] + fine[1]) + fine[2]) + fine[3]
            total_scr[pl.ds(half_start + PACK_FINE, PACK_HALF - PACK_FINE), :] = (
                (bulk[0].astype(F32) + bulk[1].astype(F32)) + bulk[2].astype(F32)) + bulk[3].astype(F32)
            total_to_sibling().start()
            for a in range(n):
                rows = p_refs[a].shape[1]
                for r in range(3):
                    to_chip(a, r).wait_recv()
                    for s in range(0, rows, row_chunk):
                        sl = pl.ds(s, row_chunk)
                        g_refs[a][sl, :] = g_refs[a][sl, :] + chip_in[a][r, sl, :].astype(F32)
            total_from_sibling().wait_recv()
            total_ref[...] = total_scr[...]
            for a in range(n):
                all_from_sibling(a).wait_send()
                for r in range(3):
                    to_chip(a, r).wait_send()
            pack_to_sibling().wait_send()
            for r in range(3):
                half_to_chip(r, 0).wait_send()
                half_to_chip(r, 1).wait_send()
            total_to_sibling().wait_send()

    fc_specs_in, fc_specs_out, fc_shapes, fc_args = [], [], [], []
    for w, own, arrived, diagonal, m, v in fc:
        rows, cols = w.shape
        blk = pl.BlockSpec((rows // TAIL_STEPS, cols), lambda i: (i, 0))
        fc_specs_in += [blk, blk, pl.BlockSpec((2, rows // TAIL_STEPS, cols), lambda i: (0, i, 0)), blk, blk, blk]
        fc_specs_out += [blk] * 4
        fc_shapes += [jax.ShapeDtypeStruct((rows, cols), F32)] * 4
        fc_args += [w, own, arrived, diagonal, m, v]
    outs = pl.pallas_call(
        body, name="tail_comm", grid=(TAIL_STEPS,),
        out_shape=tuple([jax.ShapeDtypeStruct(p.shape[1:], F32) for p in parts]
                        + [jax.ShapeDtypeStruct((PACK_ROWS, D), F32)] + fc_shapes),
        in_specs=[_resident(p.shape) for p in parts] + [_resident(small.shape)] + fc_specs_in,
        out_specs=tuple([_full(p.shape[1:]) for p in parts] + [_full((PACK_ROWS, D))] + fc_specs_out),
        scratch_shapes=(
            [pltpu.VMEM((4,) + p.shape[1:], BF16) for p in parts]
            + [pltpu.VMEM((3,) + p.shape[1:], BF16) for p in parts]
            + [pltpu.VMEM((3,) + p.shape[1:], BF16) for p in parts]
            + [pltpu.VMEM((PACK_ROWS, D), F32), pltpu.VMEM((PACK_ROWS, D), F32),
               pltpu.VMEM((4, PACK_FINE, D), F32), pltpu.VMEM((4, PACK_HALF - PACK_FINE, D), BF16),
               pltpu.VMEM((PACK_ROWS, D), F32)]
            + [pltpu.SemaphoreType.DMA((n,)), pltpu.SemaphoreType.DMA((n,)),
               pltpu.SemaphoreType.DMA((3 * n,)), pltpu.SemaphoreType.DMA((3 * n,)),
               pltpu.SemaphoreType.DMA((8,)), pltpu.SemaphoreType.DMA((8,))]),
        compiler_params=pltpu.CompilerParams(dimension_semantics=("arbitrary",), vmem_limit_bytes=VMEM_LIMIT),
    )(*parts, small, *fc_args)
    return outs[:n + 1], [outs[n + 1 + 4 * k:n + 5 + 4 * k] for k in range(n_fc)]


def _tril_mask():
    row = lax.broadcasted_iota(jnp.int32, (CHUNK, CHUNK), 0)
    col = lax.broadcasted_iota(jnp.int32, (CHUNK, CHUNK), 1)
    return (col <= row).astype(F32)


def _window_sums(ext):
    s2 = ext + pltpu.roll(ext, 1, 0)
    t4 = s2[:, GROUP:]
    s4 = t4 + pltpu.roll(t4, 2, 0)
    t8 = s4[:, GROUP:]
    s8 = t8 + pltpu.roll(t8, 4, 0)
    t16 = s8[:, GROUP:]
    s16 = t16 + pltpu.roll(t16, 8, 0)
    return [s2[:, :GROUP], s4[:, :GROUP], s8[:, :GROUP], s16]


def _inv_counts(first_pos, rows):
    pos = first_pos + lax.broadcasted_iota(jnp.int32, (rows, 1), 0)
    return [1.0 / jnp.minimum(pos + 1, w).astype(F32) for w in WINDOWS]


def _pool_diff(zb, halo, first_pos):
    tt = zb.shape[0]
    sums = _window_sums(jnp.concatenate([halo, zb], axis=0))
    inv = _inv_counts(first_pos, tt)
    return [sums[g][HALO:, :] * inv[g] - zb[:, g * GROUP:(g + 1) * GROUP] for g in range(len(WINDOWS))]


def _row_blocks(scr, rows, place):
    def block(rel):
        start = pl.multiple_of(rows * _index(_peer(*place, rel)), rows)
        return scr.at[pl.ds(start, rows), :]

    def entries(shard_ref):
        return ([(shard_ref, block(0), rel) for rel in (1, 2, 4, 6)]
                + [(block(rel), block(rel), 1) for rel in (2, 4, 6)])
    return block, entries


def _attn_fwd(tt, x, c8, w_ada, b_my, n1pre, n1post, w_in_shard, w_out_shard, w_sp, bs_rows, ln_g, ln_b, w_pool,
              b_pool, pool_scale, fc_shards, n2pre):
    t_len = x.shape[0]
    nt = t_len // tt
    ncol = w_ada.shape[1]

    def body(x_ref, xb_ref, c_ref, wada_ref, b_ref, n1pre_ref, n1post_ref, wi_ref, wo_ref, wsp_ref, bs_ref,
             lng_ref, lnb_ref, wp_ref, bp_ref, ps_ref, w1_ref, w2_ref, n2pre_ref,
             z_ref, cat_ref, mix_ref, x1_ref, h2_ref, r_ref, f_ref, mod_out, sc_out, e1_ref, e2_ref, wout_ref,
             win_out, carry, land1, land2, sib1, sib2, cat_keep, wout_scr, win_ref, mod_ref, cg, mg, part,
             send_sems, recv_sems, local_sems, wo_send, wo_recv, wi_send, wi_recv, ada_send, ada_recv):
        i = pl.program_id(0)
        place = px, py, pc = _place()
        me = _index(place)
        wo_block, wo_entries = _row_blocks(wout_scr, w_out_shard.shape[0], place)
        wi_block, wi_entries = _row_blocks(win_ref, w_in_shard.shape[0], place)
        wo_copies = _Copies(wo_entries(wo_ref), wo_send, wo_recv)
        wi_copies = _Copies(wi_entries(wi_ref), wi_send, wi_recv)
        wo_keep = pltpu.make_async_copy(wout_scr, wout_ref, local_sems.at[8])
        wi_keep = pltpu.make_async_copy(win_ref, win_out, local_sems.at[9])
        ada = _Copies([(c_ref, cg.at[me], k) for k in range(1, N_DEV)]
                      + [(part, mg.at[me], k) for k in range(1, N_DEV)], ada_send, ada_recv)
        copies = _Copies(
            [(w1_ref, sib1, 1), (w2_ref, sib2, 1),
             (w1_ref, land1.at[0], 2), (w2_ref, land2.at[0], 2),
             (w1_ref, land1.at[1], 4), (w2_ref, land2.at[1], 4),
             (land1.at[0], e1_ref.at[3], 1), (land2.at[0], e2_ref.at[3], 1),
             (land1.at[1], e1_ref.at[5], 1), (land2.at[1], e2_ref.at[5], 1)],
            send_sems, recv_sems)
        keep = [pltpu.make_async_copy(w1_ref, e1_ref.at[0], local_sems.at[0]),
                pltpu.make_async_copy(w2_ref, e2_ref.at[0], local_sems.at[1]),
                pltpu.make_async_copy(land1.at[0], e1_ref.at[2], local_sems.at[2]),
                pltpu.make_async_copy(land1.at[1], e1_ref.at[4], local_sems.at[3]),
                pltpu.make_async_copy(land2.at[0], e2_ref.at[2], local_sems.at[4]),
                pltpu.make_async_copy(land2.at[1], e2_ref.at[4], local_sems.at[5]),
                pltpu.make_async_copy(sib1, e1_ref.at[1], local_sems.at[6]),
                pltpu.make_async_copy(sib2, e2_ref.at[1], local_sems.at[7])]

        @pl.when(i == 0)
        def _():
            ada.start(*range(N_DEV - 1))
            wi_copies.start(0, 1, 2, 3)
            cg[me] = c_ref[...]
            wi_rows, wo_rows = w_in_shard.shape[0], w_out_shard.shape[0]
            win_ref[pl.ds(pl.multiple_of(wi_rows * me, wi_rows), wi_rows), :] = wi_ref[...]
            wout_scr[pl.ds(pl.multiple_of(wo_rows * me, wo_rows), wo_rows), :] = wo_ref[...]
            carry[...] = jnp.zeros_like(carry)

            ada.wait_recv(*range(N_DEV - 1))
            c_all = jnp.concatenate([cg[j, 0:1, :] for j in range(N_DEV)], axis=0)
            sc = c_all * jax.nn.sigmoid(c_all)
            sc_out[...] = sc
            part[...] = _dot(sc.astype(BF16), wada_ref[...].astype(BF16)) + b_ref[...]
            ada.start(*range(N_DEV - 1, 2 * (N_DEV - 1)))
            wo_copies.start(0, 1, 2, 3)
            copies.start(0, 1, 2, 4, 3, 5)
            keep[0].start()
            keep[1].start()
            mg[me] = part[...]
            ada.wait_recv(*range(N_DEV - 1, 2 * (N_DEV - 1)))
            mod_ref[...] = jnp.zeros_like(mod_ref)
            for j in range(N_DEV):
                for m in range(6):
                    lo, hi = max(ncol * j, D * m), min(ncol * (j + 1), D * (m + 1))
                    if lo < hi:
                        mod_ref[m:m + 1, lo - D * m:hi - D * m] = mg[j, pl.ds(me, 1), lo - ncol * j:hi - ncol * j]
            mod_out[...] = mod_ref[...]

            wi_copies.wait_recv(1, 2, 3)
            wi_copies.start(4, 5, 6)
            wi_copies.wait_recv(0, 4, 5, 6)
            wi_keep.start()

        @pl.when(i == nt // 2 + ATTN_LAG)
        def _():
            copies.wait_recv(2, 4)
            copies.start(6, 8)
            keep[2].start()
            keep[3].start()

        @pl.when(i == nt - 1 + ATTN_LAG)
        def _():
            copies.wait_recv(3, 5)
            copies.start(7, 9)
            keep[4].start()
            keep[5].start()

        shift1, scale1, gate1 = mod_ref[0:1, :], mod_ref[1:2, :], mod_ref[2:3, :]

        @pl.when(i < nt)
        def _():
            xv = x_ref[...]
            h1 = (xv * _rstd(xv)) * (n1pre_ref[...] * (1.0 + scale1)) + shift1
            z = _dot_nt(h1.astype(BF16), win_ref[...])
            z_ref[...] = z

            _, ga = _gelu_parts(z[:, :2 * D_A])
            u, vr = ga[:, :D_A], ga[:, D_A:]
            dv = vr - jnp.mean(vr, axis=-1, keepdims=True)
            v = (dv * lax.rsqrt(jnp.mean(dv * dv, axis=-1, keepdims=True) + EPS)) * lng_ref[...] + lnb_ref[...]
            vb = v.astype(BF16)
            mask = _tril_mask()
            wc = [(wsp_ref[h] * mask).astype(BF16) for h in range(N_HEADS)]
            for ch in range(tt // CHUNK):
                rows = slice(ch * CHUNK, (ch + 1) * CHUNK)
                for h in range(N_HEADS):
                    cols = slice(h * GROUP, (h + 1) * GROUP)
                    mixed = _dot(wc[h], vb[rows, cols]) + bs_ref[:, cols]
                    cat_ref[rows, cols] = (u[rows, cols] * mixed).astype(BF16)

            zb = z[:, 2 * D_A:]
            diff = _pool_diff(zb, carry[...], i * tt)
            carry[...] = zb[tt - HALO:, :]
            for g in range(len(WINDOWS)):
                cols = slice(g * GROUP, (g + 1) * GROUP)
                pre = _dot(diff[g].astype(BF16), wp_ref[g].astype(BF16)) + bp_ref[:, cols]
                cat_ref[:, D_A + g * GROUP:D_A + (g + 1) * GROUP] = (pre * ps_ref[:, cols]).astype(BF16)
            cat_keep[i % (ATTN_LAG + 1)] = cat_ref[...]

        @pl.when(i == 0)
        def _():
            copies.wait_recv(0, 1)
            keep[6].start()
            keep[7].start()

        @pl.when(i == 1)
        def _():
            wo_copies.wait_recv(1, 2, 3)
            wo_copies.start(4, 5, 6)

        @pl.when(i == ATTN_LAG)
        def _():
            wo_copies.wait_recv(0, 4, 5, 6)
            wo_keep.start()

        @pl.when(i >= ATTN_LAG)
        def _():
            xv = xb_ref[...]
            mix = _dot(cat_keep[(i - ATTN_LAG) % (ATTN_LAG + 1)], wout_scr[...])
            mix_ref[...] = mix
            x1v = xv + (mix * _rstd(mix)) * (gate1 * n1post_ref[...])
            x1_ref[...] = x1v
            shift2, scale2 = mod_ref[3:4, :], mod_ref[4:5, :]
            h2 = ((x1v * _rstd(x1v)) * (n2pre_ref[...] * (1.0 + scale2)) + shift2).astype(BF16)
            h2_ref[...] = h2
            for j, (w1, w2) in enumerate(((w1_ref, w2_ref), (sib1, sib2))):
                ra = jnp.maximum(_dot(h2, w1[...]), 0.0)
                r = (ra * ra).astype(BF16)
                r_ref[:, j * FF_BLK:(j + 1) * FF_BLK] = r
                if j == 0:
                    f_ref[...] = _dot(r, w2[...])
                else:
                    f_ref[...] += _dot(r, w2[...])

        @pl.when(i == nt - 1 + ATTN_LAG)
        def _():
            copies.wait_recv(6, 7, 8, 9)
            copies.wait_send(*range(10))
            wo_copies.wait_send(*range(7))
            wi_copies.wait_send(*range(7))
            ada.wait_send(*range(2 * (N_DEV - 1)))
            for cp in keep:
                cp.wait()
            wo_keep.wait()
            wi_keep.wait()

    first = lambda w: pl.BlockSpec((tt, w), lambda i: (jnp.minimum(i, nt - 1), 0))
    second = lambda w: pl.BlockSpec((tt, w), lambda i: (jnp.maximum(i - ATTN_LAG, 0), 0))
    r_head = pl.BlockSpec((tt, FC_HEAD * FF_BLK),
                          lambda i: (jnp.maximum(i - ATTN_LAG, 0), R_HEAD_COLS // (FC_HEAD * FF_BLK)))
    hbm = pl.BlockSpec(memory_space=pl.ANY)
    outs = pl.pallas_call(
        body, name="attn_fwd", grid=(nt + ATTN_LAG,),
        out_shape=tuple([jax.ShapeDtypeStruct((t_len, D_Z), F32), jax.ShapeDtypeStruct((t_len, D), BF16),
                         jax.ShapeDtypeStruct((t_len, D), F32), jax.ShapeDtypeStruct((t_len, D), F32),
                         jax.ShapeDtypeStruct((t_len, D), BF16),
                         jax.ShapeDtypeStruct((t_len, FC_EARLY * FF_BLK), BF16),
                         jax.ShapeDtypeStruct((t_len, D), F32)]
                        + [jax.ShapeDtypeStruct((8, D), F32), jax.ShapeDtypeStruct((N_DEV, D), F32)]
                        + [jax.ShapeDtypeStruct((FC_EARLY,) + s.shape, BF16) for s in fc_shards]
                        + [jax.ShapeDtypeStruct((D, D), BF16), jax.ShapeDtypeStruct((D_Z, D), BF16)]),
        in_specs=[first(D), second(D), _full((8, D)), _resident(w_ada.shape), _full((1, ncol)), _full((1, D)),
                  _full((1, D)), _resident(w_in_shard.shape), _resident(w_out_shard.shape),
                  _full((N_HEADS, CHUNK, CHUNK)), _full((CHUNK, D_A)), _full((1, D_A)), _full((1, D_A)),
                  _full((len(WINDOWS), GROUP, GROUP)), _full((1, D_B)), _full((1, D_B)),
                  _resident(fc_shards[0].shape), _resident(fc_shards[1].shape), _full((1, D))],
        out_specs=(first(D_Z), first(D), second(D), second(D), second(D), r_head, second(D),
                   _full((8, D)), _full((N_DEV, D)), hbm, hbm, hbm, hbm),
        scratch_shapes=[pltpu.VMEM((HALO, D_B), F32),
                        pltpu.VMEM((2,) + fc_shards[0].shape, BF16), pltpu.VMEM((2,) + fc_shards[1].shape, BF16),
                        pltpu.VMEM(fc_shards[0].shape, BF16), pltpu.VMEM(fc_shards[1].shape, BF16),
                        pltpu.VMEM((ATTN_LAG + 1, tt, D), BF16), pltpu.VMEM((D, D), BF16),
                        pltpu.VMEM((D_Z, D), BF16), pltpu.VMEM((8, D), F32),
                        pltpu.VMEM((N_DEV, 8, D), F32), pltpu.VMEM((N_DEV, N_DEV, ncol), F32),
                        pltpu.VMEM((N_DEV, ncol), F32),
                        pltpu.SemaphoreType.DMA((10,)), pltpu.SemaphoreType.DMA((10,)),
                        pltpu.SemaphoreType.DMA((10,)),
                        pltpu.SemaphoreType.DMA((7,)), pltpu.SemaphoreType.DMA((7,)),
                        pltpu.SemaphoreType.DMA((7,)), pltpu.SemaphoreType.DMA((7,)),
                        pltpu.SemaphoreType.DMA((2 * (N_DEV - 1),)), pltpu.SemaphoreType.DMA((2 * (N_DEV - 1),))],
        compiler_params=pltpu.CompilerParams(dimension_semantics=("arbitrary",), vmem_limit_bytes=VMEM_LIMIT),
    )(x, x, c8, w_ada, b_my, n1pre, n1post, w_in_shard, w_out_shard, w_sp, bs_rows, ln_g, ln_b, w_pool, b_pool,
      pool_scale, *fc_shards, n2pre)
    return outs[:9], outs[9:11], outs[11], outs[12]


def _mlp_fwd_early(tt, r_begun, h2, f_head, w1_early, w2_early):
    t_len = h2.shape[0]
    nt = t_len // tt
    n_late = N_DEV - FC_EARLY

    def body(r_begun_ref, h2_ref, fh_ref, w1_ref, w2_ref, r_ref, f_ref, l1_ref, l2_ref,
             land1, land2, send_sems, recv_sems, local_sems):
        i = pl.program_id(0)
        copies = _Copies(
            [(w1_ref.at[2], land1, 4), (w2_ref.at[4], land2, 2),
             (land1, l1_ref.at[1], 1), (land2, l2_ref.at[1], 1)],
            send_sems, recv_sems)
        keep = [pltpu.make_async_copy(land1, l1_ref.at[0], local_sems.at[0]),
                pltpu.make_async_copy(land2, l2_ref.at[0], local_sems.at[1])]

        @pl.when(i == 0)
        def _():
            copies.start(0, 1)

        @pl.when(i == nt - 1)
        def _():
            copies.wait_recv(0, 1)
            copies.start(2, 3)
            for cp in keep:
                cp.start()

        h2 = h2_ref[...]
        f_ref[...] = fh_ref[...]
        for j in range(FC_HEAD, FC_EARLY):
            ra = jnp.maximum(_dot(h2, w1_ref[j]), 0.0)
            r = (ra * ra).astype(BF16)
            r_ref[:, _early_col(j):_early_col(j) + FF_BLK] = r
            f_ref[...] += _dot(r, w2_ref[j])

        @pl.when(i == nt - 1)
        def _():
            copies.wait_recv(2, 3)
            copies.wait_send(0, 1, 2, 3)
            for cp in keep:
                cp.wait()

    tile = lambda w: pl.BlockSpec((tt, w), lambda i: (i, 0))
    hbm = pl.BlockSpec(memory_space=pl.ANY)
    outs = pl.pallas_call(
        body, name="mlp_fwd_early", grid=(nt,),
        out_shape=(jax.ShapeDtypeStruct((t_len, FC_EARLY * FF_BLK), BF16), jax.ShapeDtypeStruct((t_len, D), F32),
                   jax.ShapeDtypeStruct((n_late,) + w1_early.shape[1:], BF16),
                   jax.ShapeDtypeStruct((n_late,) + w2_early.shape[1:], BF16)),
        in_specs=[hbm, tile(D), tile(D), _resident((FC_EARLY, D, FF_BLK)), _resident((FC_EARLY, FF_BLK, D))],
        out_specs=(tile(R_HEAD_COLS), tile(D), hbm, hbm),
        input_output_aliases={0: 0},
        scratch_shapes=[pltpu.VMEM(w1_early.shape[1:], BF16), pltpu.VMEM(w2_early.shape[1:], BF16),
                        pltpu.SemaphoreType.DMA((4,)), pltpu.SemaphoreType.DMA((4,)),
                        pltpu.SemaphoreType.DMA((2,))],
        compiler_params=pltpu.CompilerParams(dimension_semantics=("arbitrary",), vmem_limit_bytes=VMEM_LIMIT),
    )(r_begun, h2, f_head, w1_early, w2_early)
    return outs[:2], outs[2:]


def _mlp_late_bwd(tt, r_early, x1, h2, f_early, tgt, mix, mod, n2pre, n2post, n1post,
                  w1_early, w2_early, w1_late, w2_late):
    t_len = x1.shape[0]
    nt = t_len // tt
    n_late = N_DEV - FC_EARLY
    late_cols = n_late * FF_BLK

    def body(re_ref, x1_ref, h2_ref, fe_ref, tgt_ref, mix_ref, mod_ref, n2pre_ref, n2post_ref,
             n1post_ref, w1e_ref, w2e_ref, w1l_ref, w2l_ref,
             rl_ref, df_ref, da_ref, dmix_ref, dx1_ref, redf_ref, redb_ref, dh2_acc):
        i = pl.program_id(0)

        @pl.when(i == 0)
        def _():
            redf_ref[...] = jnp.zeros_like(redf_ref)
            redb_ref[...] = jnp.zeros_like(redb_ref)

        x1v = x1_ref[...]
        gate1, scale2, gate2 = mod_ref[2:3, :], mod_ref[4:5, :], mod_ref[5:6, :]
        h2 = h2_ref[...]
        f = fe_ref[...]
        for j in range(n_late):
            cols = slice(j * FF_BLK, (j + 1) * FF_BLK)
            ra = jnp.maximum(_dot(h2, w1l_ref[j]), 0.0)
            r = (ra * ra).astype(BF16)
            rl_ref[:, cols] = r
            f = f + _dot(r, w2l_ref[j])
        post2 = n2post_ref[...]
        gate_post2 = gate2 * post2
        rf = _rstd(f)
        fhat = f * rf
        err = (x1v + fhat * gate_post2) - tgt_ref[...]
        dy = err * (1.0 / D)
        d_f, sum_f = _rms_bwd_gained(dy, gate_post2, fhat, rf)
        dfv = d_f.astype(BF16)
        df_ref[...] = dfv
        redf_ref[0:1, :] += post2 * sum_f
        redf_ref[1:2, :] += gate2 * sum_f
        redf_ref[2:3, :] += _colsum(err * err)

        for j in range(N_DEV):
            cols = slice(j * FF_BLK, (j + 1) * FF_BLK)
            if j < FC_EARLY:
                w1, w2, r = w1e_ref[j], w2e_ref[j], re_ref[:, _early_col(j):_early_col(j) + FF_BLK]
            else:
                jl = j - FC_EARLY
                w1, w2, r = w1l_ref[jl], w2l_ref[jl], rl_ref[:, jl * FF_BLK:(jl + 1) * FF_BLK]
            dr = _dot_nt(dfv, w2)
            da = (dr * (2.0 * jnp.sqrt(r.astype(F32)))).astype(BF16)
            da_ref[:, cols] = da
            contrib = _dot_nt(da, w1)
            if j == 0:
                dh2_acc[...] = contrib
            else:
                dh2_acc[...] += contrib
        dh2 = dh2_acc[...]
        pre2, post1 = n2pre_ref[...], n1post_ref[...]
        r2 = _rstd(x1v)
        xhat = x1v * r2
        d_x1, sum_h = _rms_bwd_gained(dh2, pre2 * (1.0 + scale2), xhat, r2)
        dx1 = dy + d_x1
        dx1_ref[...] = dx1
        mixv = mix_ref[...]
        rm = _rstd(mixv)
        mhat = mixv * rm
        d_mix, sum_m = _rms_bwd_gained(dx1, gate1 * post1, mhat, rm)
        dmix_ref[...] = d_mix.astype(BF16)
        redb_ref[0:1, :] += _colsum(dh2)
        redb_ref[1:2, :] += pre2 * sum_h
        redb_ref[2:3, :] += (1.0 + scale2) * sum_h
        redb_ref[3:4, :] += post1 * sum_m
        redb_ref[4:5, :] += gate1 * sum_m

    tile = lambda w: pl.BlockSpec((tt, w), lambda i: (i, 0))
    return pl.pallas_call(
        body, name="mlp_late_bwd", grid=(nt,),
        out_shape=(jax.ShapeDtypeStruct((t_len, late_cols), BF16), jax.ShapeDtypeStruct((t_len, D), BF16),
                   jax.ShapeDtypeStruct((t_len, D_FF), BF16), jax.ShapeDtypeStruct((t_len, D), BF16),
                   jax.ShapeDtypeStruct((t_len, D), F32), jax.ShapeDtypeStruct((8, D), F32),
                   jax.ShapeDtypeStruct((8, D), F32)),
        in_specs=[tile(FC_EARLY * FF_BLK), tile(D), tile(D), tile(D), tile(D),
                  tile(D), _full((8, D)), _full((1, D)), _full((1, D)), _full((1, D)),
                  _resident((FC_EARLY, D, FF_BLK)), _resident((FC_EARLY, FF_BLK, D)),
                  _resident((n_late, D, FF_BLK)), _resident((n_late, FF_BLK, D))],
        out_specs=(tile(late_cols), tile(D), tile(D_FF), tile(D), tile(D), _full((8, D)), _full((8, D))),
        scratch_shapes=[pltpu.VMEM((tt, D), F32)],
        compiler_params=pltpu.CompilerParams(dimension_semantics=("arbitrary",), vmem_limit_bytes=VMEM_LIMIT),
    )(r_early, x1, h2, f_early, tgt, mix, mod, n2pre, n2post, n1post, w1_early, w2_early, w1_late, w2_late)


def _mlp_wgrad(tt, r_early, r_late, da, df, h2):
    t_len = df.shape[0]
    nt = t_len // tt
    odd_steps = [j for j, rel in enumerate(WGRAD_ORDER) if rel % 2]

    def relation(j):
        rel = jnp.int32(WGRAD_ORDER[-1])
        for step in range(N_DEV - 2, -1, -1):
            rel = jnp.where(j == step, WGRAD_ORDER[step], rel)
        return rel

    def body(re_ref, rl_ref, da_ref, df_ref, h2_ref, own1_ref, own2_ref, out1_ref, out2_ref, diag1_ref, diag2_ref,
             acc1, acc2, snd1, snd2, sib1, sib2, dsnd1, dsnd2, send_sems, recv_sems):
        j, t = pl.program_id(0), pl.program_id(1)
        rows = pl.ds(pl.multiple_of(t * tt, tt), tt)
        x, y, c = _place()
        accs, snds, sibs = (acc1, acc2), (snd1, snd2), (sib1, sib2)
        dsnds, diags = (dsnd1, dsnd2), (diag1_ref, diag2_ref)

        def to_sibling(a, jj, buf=0):
            return pltpu.make_async_remote_copy(
                src_ref=snds[a].at[buf], dst_ref=sibs[a].at[jj],
                send_sem=send_sems.at[4 * a + jj], recv_sem=recv_sems.at[4 * a + jj],
                device_id=(x, y, 1 - c), device_id_type=MESH)

        def to_diagonal(a):
            return pltpu.make_async_remote_copy(
                src_ref=dsnds[a], dst_ref=diags[a], send_sem=send_sems.at[8 + a], recv_sem=recv_sems.at[8 + a],
                device_id=_peer(x, y, c, 6), device_id_type=MESH)

        @pl.when(t == 0)
        def _():
            acc2[...] = jnp.zeros_like(acc2)
            acc1[...] = jnp.zeros_like(acc1)

        for r_ref, mine in ((re_ref, relation(j) < FC_EARLY), (rl_ref, relation(j) >= FC_EARLY)):
            @pl.when(mine)
            def _():
                acc2[...] += _dot_tn(r_ref[...], df_ref[rows, :])
                acc1[...] += _dot_tn(h2_ref[rows, :], da_ref[...])

        for step, rel in enumerate(WGRAD_ORDER):
            jj = rel // 2

            @pl.when((t == nt - 1) & (j == step))
            def _():
                for a, (own_ref, out_ref) in enumerate(((own1_ref, out1_ref), (own2_ref, out2_ref))):
                    if rel % 2:
                        q = odd_steps.index(step)
                        if q >= 2:
                            to_sibling(a, WGRAD_ORDER[odd_steps[q - 2]] // 2).wait_send()
                        snds[a][q % 2] = accs[a][...].astype(BF16)
                        to_sibling(a, jj, q % 2).start()
                        continue
                    to_sibling(a, jj).wait_recv()
                    chip_sum = accs[a][...] + sibs[a][jj].astype(F32)
                    if rel == 6:
                        dsnds[a][...] = chip_sum.astype(BF16)
                        to_diagonal(a).start()
                    elif rel == 0:
                        own_ref[...] = chip_sum
                    else:
                        out_ref[0] = chip_sum.astype(BF16)
                    if step == N_DEV - 1:
                        for q in (2, 3):
                            to_sibling(a, WGRAD_ORDER[odd_steps[q]] // 2).wait_send()
                        to_diagonal(a).wait_recv()
                        to_diagonal(a).wait_send()

    assert WGRAD_ORDER[-1] == 0 and WGRAD_ORDER[-3:-1] == (2, 4)
    blk = pl.BlockSpec((tt, FF_BLK), lambda j, t: (t, relation(j)))
    early_block = lambda rel: jnp.where(rel < FC_HEAD, rel + FC_EARLY - FC_HEAD, rel - FC_HEAD)
    early = lambda j, t: (jnp.where(relation(j) < FC_EARLY, t, 0),
                          jnp.where(relation(j) < FC_EARLY, early_block(relation(j)), 0))
    late = lambda j, t: (jnp.where(relation(j) < FC_EARLY, 0, t), jnp.maximum(relation(j) - FC_EARLY, 0))
    chip = lambda j, t: (jnp.clip(j - 5, 0, 1), 0, 0)
    hbm = pl.BlockSpec(memory_space=pl.ANY)
    return pl.pallas_call(
        body, name="mlp_wgrad", grid=(N_DEV, nt),
        out_shape=(jax.ShapeDtypeStruct((D, FF_BLK), F32), jax.ShapeDtypeStruct((FF_BLK, D), F32),
                   jax.ShapeDtypeStruct((2, D, FF_BLK), BF16), jax.ShapeDtypeStruct((2, FF_BLK, D), BF16),
                   jax.ShapeDtypeStruct((D, FF_BLK), BF16), jax.ShapeDtypeStruct((FF_BLK, D), BF16)),
        in_specs=[pl.BlockSpec((tt, FF_BLK), early), pl.BlockSpec((tt, FF_BLK), late), blk,
                  _resident((t_len, D)), _resident((t_len, D))],
        out_specs=(_full((D, FF_BLK)), _full((FF_BLK, D)),
                   pl.BlockSpec((1, D, FF_BLK), chip), pl.BlockSpec((1, FF_BLK, D), chip), hbm, hbm),
        scratch_shapes=[pltpu.VMEM((D, FF_BLK), F32), pltpu.VMEM((FF_BLK, D), F32),
                        pltpu.VMEM((2, D, FF_BLK), BF16), pltpu.VMEM((2, FF_BLK, D), BF16),
                        pltpu.VMEM((4, D, FF_BLK), BF16), pltpu.VMEM((4, FF_BLK, D), BF16),
                        pltpu.VMEM((D, FF_BLK), BF16), pltpu.VMEM((FF_BLK, D), BF16),
                        pltpu.SemaphoreType.DMA((10,)), pltpu.SemaphoreType.DMA((10,))],
        compiler_params=pltpu.CompilerParams(dimension_semantics=("arbitrary", "arbitrary"),
                                             vmem_limit_bytes=VMEM_LIMIT),
    )(r_early, r_late, da, df, h2)


def _acc_rows(ref, row0, k, val):
    half = CHUNK // 2
    ref[row0:row0 + half, k * GROUP:(k + 1) * GROUP] += val[:half, :]
    ref[row0:row0 + half, D_A + k * GROUP:D_A + (k + 1) * GROUP] += val[half:, :]


def _attn_bwd(tt, dmix, dx1, x, z, cat, mod, n1pre, w_in_t, w_out, w_sp, bs_rows, ln_g, ln_b, w_pool, b_pool,
              pool_scale, red_fwd, red_bwd, chip_sums):
    t_len = x.shape[0]
    nt = t_len // tt
    hb = tt // HALO
    n_sums = len(chip_sums)

    def body(dmix_ref, dx1_ref, x_ref, z_ref, zprev_ref, cat_ref, mod_ref, n1pre_ref, win_ref, wout_ref, wsp_ref,
             bs_ref, lng_ref, lnb_ref, wp_ref, bp_ref, ps_ref, redf_ref, redb_ref, *rest):
        sum_out = rest[:n_sums]
        gx_ref, gwin_ref, gwout_ref, small_ref = rest[n_sums:n_sums + 4]
        sum_in = rest[n_sums + 4:2 * n_sums + 4]
        carry, acc_in, acc_out, dz_scr, bs_acc, send_sems, recv_sems = rest[2 * n_sums + 4:]
        s = pl.program_id(0)
        i = nt - 1 - s
        px, py, pc = _place()

        def chip_copy(a, r):
            return pltpu.make_async_remote_copy(
                src_ref=sum_out[a].at[r], dst_ref=sum_in[a].at[r],
                send_sem=send_sems.at[2 * a + r], recv_sem=recv_sems.at[2 * a + r],
                device_id=_peer(px, py, pc, 2 * (r + 1)), device_id_type=MESH)

        @pl.when(s == 0)
        def _():
            for a in range(n_sums):
                for r in range(2):
                    chip_copy(a, r).start()
            carry[...] = jnp.zeros_like(carry)
            acc_in[...] = jnp.zeros_like(acc_in)
            acc_out[...] = jnp.zeros_like(acc_out)
            bs_acc[...] = jnp.zeros_like(bs_acc)
            small_ref[...] = jnp.zeros_like(small_ref)
            small_ref[ROW_DMOD + 2:ROW_DMOD + 3, :] = redb_ref[3:4, :]
            small_ref[ROW_DMOD + 3:ROW_DMOD + 5, :] = redb_ref[0:2, :]
            small_ref[ROW_DMOD + 5:ROW_DMOD + 6, :] = redf_ref[0:1, :]
            small_ref[ROW_N1POST:ROW_N1POST + 1, :] = redb_ref[4:5, :]
            small_ref[ROW_N2PRE:ROW_N2PRE + 1, :] = redb_ref[2:3, :]
            small_ref[ROW_N2POST:ROW_N2POST + 1, :] = redf_ref[1:2, :]
            small_ref[ROW_LOSS:ROW_LOSS + 1, :] = redf_ref[2:3, :]

        dmixv = dmix_ref[...]
        dcat = _dot_nt(dmixv, wout_ref[...])
        acc_out[...] += _dot_tn(cat_ref[...], dmixv)

        z = z_ref[...]
        t_g, ga = _gelu_parts(z[:, :2 * D_A])
        u, vr = ga[:, :D_A], ga[:, D_A:]
        dv0 = vr - jnp.mean(vr, axis=-1, keepdims=True)
        rv = lax.rsqrt(jnp.mean(dv0 * dv0, axis=-1, keepdims=True) + EPS)
        vhat = dv0 * rv
        vb = (vhat * lng_ref[...] + lnb_ref[...]).astype(BF16)
        mask = _tril_mask()
        wc = [(wsp_ref[h] * mask).astype(BF16) for h in range(N_HEADS)]

        dya = dcat[:, :D_A]
        for h in range(N_HEADS):
            cols = slice(h * GROUP, (h + 1) * GROUP)
            bs_sum = jnp.zeros((CHUNK, GROUP), F32)
            ws_sum = jnp.zeros((CHUNK, CHUNK), F32)
            for ch in range(tt // CHUNK):
                rows = slice(ch * CHUNK, (ch + 1) * CHUNK)
                v_ch = vb[rows, cols]
                mixed = _dot(wc[h], v_ch) + bs_ref[:, cols]
                dy_ch = dya[rows, cols]
                dz_scr[rows, cols] = dy_ch * mixed
                dmixed = dy_ch * u[rows, cols]
                dmb = dmixed.astype(BF16)
                dz_scr[rows, D_A + h * GROUP:D_A + (h + 1) * GROUP] = _dot_tn(wc[h], dmb)
                bs_sum = bs_sum + dmixed
                ws_sum = ws_sum + _dot_nt(dmb, v_ch)
            _acc_rows(bs_acc, 0, h, bs_sum)
            _acc_rows(small_ref, ROW_WS, h, ws_sum)

        dvl = dz_scr[:, D_A:2 * D_A]
        dvhat = dvl * lng_ref[...]
        dvl_vhat = dvl * vhat
        dvr = rv * (dvhat - jnp.mean(dvhat, axis=-1, keepdims=True)
                    - vhat * jnp.mean(dvl_vhat * lng_ref[...], axis=-1, keepdims=True))
        small_ref[ROW_LN:ROW_LN + 1, 0:D_A] += _colsum(dvl_vhat)
        small_ref[ROW_LN:ROW_LN + 1, D_A:D] += _colsum(dvl)
        dga = jnp.concatenate([dz_scr[:, :D_A], dvr], axis=1)
        dza = dga * _gelu_grad(z[:, :2 * D_A], t_g)

        zb = z[:, 2 * D_A:]
        halo_prev = jnp.where(i == 0, 0.0, zprev_ref[...])
        diff = _pool_diff(zb, halo_prev, i * tt)
        dyb = dcat[:, D_A:]
        inv = _inv_counts(i * tt, tt)
        scaled, ddiffs = [], []
        for g in range(len(WINDOWS)):
            cols = slice(g * GROUP, (g + 1) * GROUP)
            db = diff[g].astype(BF16)
            wpg = wp_ref[g].astype(BF16)
            pre = _dot(db, wpg) + bp_ref[:, cols]
            small_ref[ROW_POOL:ROW_POOL + 1, cols] += _colsum(dyb[:, cols] * pre)
            dpre = dyb[:, cols] * ps_ref[:, cols]
            small_ref[ROW_POOL:ROW_POOL + 1, D_B + g * GROUP:D_B + (g + 1) * GROUP] += _colsum(dpre)
            dpb = dpre.astype(BF16)
            _acc_rows(small_ref, ROW_WP, g, _dot_tn(db, dpb))
            ddiff = _dot_nt(dpb, wpg)
            ddiffs.append(ddiff)
            scaled.append(ddiff * inv[g])
        scaled_all = jnp.concatenate(scaled, axis=1)
        ext = jnp.concatenate([scaled_all, carry[...]], axis=0)
        n_ext = tt + HALO
        s2 = ext + pltpu.roll(ext, n_ext - 1, 0)
        t4 = s2[:, GROUP:]
        s4 = t4 + pltpu.roll(t4, n_ext - 2, 0)
        t8 = s4[:, GROUP:]
        s8 = t8 + pltpu.roll(t8, n_ext - 4, 0)
        t16 = s8[:, GROUP:]
        s16 = t16 + pltpu.roll(t16, n_ext - 8, 0)
        back = [s2[:, :GROUP], s4[:, :GROUP], s8[:, :GROUP], s16]
        carry[...] = scaled_all[:HALO, :]
        dzb = jnp.concatenate([back[g][:tt, :] - ddiffs[g] for g in range(len(WINDOWS))], axis=1)

        dzv = jnp.concatenate([dza, dzb], axis=1).astype(BF16)
        dh1 = _dot(dzv, win_ref[...])
        xv = x_ref[...]
        r1 = _rstd(xv)
        xhat = xv * r1
        shift1, scale1 = mod_ref[0:1, :], mod_ref[1:2, :]
        pre1 = n1pre_ref[...]
        gain1 = pre1 * (1.0 + scale1)
        h1 = (xhat * gain1 + shift1).astype(BF16)
        acc_in[...] += _dot_tn(dzv, h1)
        d_x, sum_h = _rms_bwd_gained(dh1, gain1, xhat, r1)
        gx_ref[...] = dx1_ref[...] + d_x
        small_ref[ROW_DMOD:ROW_DMOD + 1, :] += _colsum(dh1)
        small_ref[ROW_DMOD + 1:ROW_DMOD + 2, :] += pre1 * sum_h
        small_ref[ROW_N1PRE:ROW_N1PRE + 1, :] += (1.0 + scale1) * sum_h

        @pl.when(s == nt - 1)
        def _():
            gwin_ref[...] = acc_in[...].astype(BF16)
            gwout_ref[...] = acc_out[...].astype(BF16)
            bs = _unfold(bs_acc[...])
            for h in range(N_HEADS):
                small_ref[ROW_BS + h:ROW_BS + h + 1, 0:GROUP] = jnp.sum(
                    bs[:, h * GROUP:(h + 1) * GROUP].T, axis=0, keepdims=True)
            for a in range(n_sums):
                for r in range(2):
                    chip_copy(a, r).wait_recv()
                    chip_copy(a, r).wait_send()

    rev = lambda w: pl.BlockSpec((tt, w), lambda s: (nt - 1 - s, 0))
    zprev = pl.BlockSpec((HALO, D_B), lambda s: (jnp.maximum((nt - 1 - s) * hb - 1, 0), 2))
    hbm = pl.BlockSpec(memory_space=pl.ANY)
    outs = pl.pallas_call(
        body, name="attn_bwd", grid=(nt,),
        out_shape=tuple([jax.ShapeDtypeStruct((t_len, D), F32), jax.ShapeDtypeStruct((D_Z, D), BF16),
                         jax.ShapeDtypeStruct((D, D), BF16), jax.ShapeDtypeStruct((SMALL_ROWS, D), F32)]
                        + [jax.ShapeDtypeStruct(cs.shape, cs.dtype) for cs in chip_sums]),
        in_specs=[rev(D), rev(D), rev(D), rev(D_Z), zprev, rev(D), _full((8, D)), _full((1, D)),
                  _resident((D_Z, D)), _resident((D, D)), _full((N_HEADS, CHUNK, CHUNK)), _full((CHUNK, D_A)),
                  _full((1, D_A)), _full((1, D_A)), _full((len(WINDOWS), GROUP, GROUP)), _full((1, D_B)),
                  _full((1, D_B)), _full((8, D)), _full((8, D))] + [_resident(cs.shape) for cs in chip_sums],
        out_specs=tuple([rev(D), _resident((D_Z, D)), _resident((D, D)), _full((SMALL_ROWS, D))] + [hbm] * n_sums),
        scratch_shapes=[pltpu.VMEM((HALO, D_B), F32), pltpu.VMEM((D_Z, D), F32), pltpu.VMEM((D, D), F32),
                        pltpu.VMEM((tt, 2 * D_A), F32), pltpu.VMEM((CHUNK // 2, D), F32),
                        pltpu.SemaphoreType.DMA((2 * n_sums,)), pltpu.SemaphoreType.DMA((2 * n_sums,))],
        compiler_params=pltpu.CompilerParams(dimension_semantics=("arbitrary",), vmem_limit_bytes=VMEM_LIMIT),
    )(dmix, dx1, x, z, z, cat, mod, n1pre, w_in_t, w_out, w_sp, bs_rows, ln_g, ln_b, w_pool, b_pool, pool_scale,
      red_fwd, red_bwd, *chip_sums)
    return outs[:4], outs[4:]


def _adam(w, g, m, v):
    m2 = ADAM_B1 * m + (1.0 - ADAM_B1) * g
    v2 = ADAM_B2 * v + (1.0 - ADAM_B2) * (g * g)
    m_hat = m2 / (1.0 - ADAM_B1 ** ADAM_STEP)
    v_hat = v2 / (1.0 - ADAM_B2 ** ADAM_STEP)
    delta = -ADAM_LR * (m_hat / (jnp.sqrt(v_hat) + ADAM_EPS) + ADAM_WD * w)
    return delta, m2, v2


def _adamw_shard(name, rb, w, g, m, v):
    rows, cols = w.shape

    def body(w_ref, g_ref, m_ref, v_ref, d_ref, m2_ref, v2_ref):
        d_ref[...], m2_ref[...], v2_ref[...] = _adam(w_ref[...], g_ref[...], m_ref[...], v_ref[...])

    blk = pl.BlockSpec((rb, cols), lambda i: (i, 0))
    shp = jax.ShapeDtypeStruct((rows, cols), F32)
    return pl.pallas_call(
        body, name=name, grid=(rows // rb,), out_shape=(shp, shp, shp),
        in_specs=[blk] * 4, out_specs=(blk, blk, blk),
        compiler_params=pltpu.CompilerParams(dimension_semantics=("arbitrary",)),
    )(w, g, m, v)


def _adamw_ada(rb, w, sc, dmod_cols, m, v):
    rows, cols = w.shape

    def body(w_ref, sc_ref, dm_ref, m_ref, v_ref, g_ref, d_ref, m2_ref, v2_ref):
        g = _dot_tn(sc_ref[...].astype(BF16), dm_ref[...].astype(BF16))
        g_ref[...] = g
        d_ref[...], m2_ref[...], v2_ref[...] = _adam(w_ref[...], g, m_ref[...], v_ref[...])

    blk = pl.BlockSpec((rb, cols), lambda i: (i, 0))
    shp = jax.ShapeDtypeStruct((rows, cols), F32)
    return pl.pallas_call(
        body, name="adamw_ada", grid=(rows // rb,), out_shape=(shp, shp, shp, shp),
        in_specs=[blk, pl.BlockSpec((N_DEV, rb), lambda i: (0, i)), _full((N_DEV, cols)), blk, blk],
        out_specs=(blk, blk, blk, blk),
        compiler_params=pltpu.CompilerParams(dimension_semantics=("arbitrary",)),
    )(w, sc, dmod_cols, m, v)


def _unfold(acc_rows):
    return jnp.concatenate([acc_rows[:, :D_A], acc_rows[:, D_A:]], axis=0)


def _adamw_small(total, params):
    n = len(params)
    flat = [a for p in params for a in p]

    def body(*refs):
        s_ref = refs[0]
        p_refs = refs[1:1 + 3 * n]
        loss_ref = refs[1 + 3 * n]
        o_refs = refs[2 + 3 * n:]
        d_b_ada = s_ref[0:6, :]
        for b in range(1, N_DEV):
            d_b_ada = d_b_ada + s_ref[_table_row(b):_table_row(b) + 6, :]
        misc = lambda r: s_ref[PK_MISC + r - ROW_N1PRE:PK_MISC + r - ROW_N1PRE + 1, :]
        loss = jnp.sum(misc(ROW_LOSS), axis=-1, keepdims=True) * (0.5 / D)
        loss_ref[...] = jnp.broadcast_to(loss, (8, GROUP))
        mask = _tril_mask()
        ws = _unfold(s_ref[PK_WS:PK_WS + 64, :])
        wp = _unfold(s_ref[PK_WP:PK_WP + 64, :])
        grads = [
            d_b_ada,
            misc(ROW_N1PRE), misc(ROW_N1POST), misc(ROW_N2PRE), misc(ROW_N2POST),
            misc(ROW_LN)[:, :D_A], misc(ROW_LN)[:, D_A:],
            misc(ROW_POOL)[:, :D_B], misc(ROW_POOL)[:, D_B:],
            s_ref[PK_BS:PK_BS + N_HEADS, 0:GROUP],
            jnp.stack([ws[:, h * GROUP:(h + 1) * GROUP] * mask for h in range(N_HEADS)]),
            jnp.stack([wp[:, g * GROUP:(g + 1) * GROUP] for g in range(len(WINDOWS))]),
        ]
        for k in range(n):
            w_ref, m_ref, v_ref = p_refs[3 * k:3 * k + 3]
            g = grads[k]
            o_refs[4 * k][...] = g
            o_refs[4 * k + 1][...], o_refs[4 * k + 2][...], o_refs[4 * k + 3][...] = _adam(
                w_ref[...], g, m_ref[...], v_ref[...])

    vm = pl.BlockSpec(memory_space=pltpu.VMEM)
    out_shape = [jax.ShapeDtypeStruct((8, GROUP), F32)]
    for w, _, _ in params:
        out_shape += [jax.ShapeDtypeStruct(w.shape, F32)] * 4
    return pl.pallas_call(
        body, name="adamw_small", out_shape=tuple(out_shape),
        in_specs=[vm] * (1 + 3 * n), out_specs=tuple([vm] * len(out_shape)),
    )(total, *flat)


TT_ATTN_FWD = 512
ATTN_LAG = 2
TT_MLP_FWD = 1024
TT_MLP = 256
TT_WGRAD = 2048
TT_ATTN_BWD = 512


def kernel(x, c, w_ada, b_ada, norm1_pre, norm1_post, w_in, w_spatial, b_spatial, ln_v_gain, ln_v_bias, w_pool, b_pool, pool_scale, w_out, norm2_pre, norm2_post, w_fc1, w_fc2, loss_target, m_w_ada, m_b_ada, m_norm1_pre, m_norm1_post, m_w_in, m_w_spatial, m_b_spatial, m_ln_v_gain, m_ln_v_bias, m_w_pool, m_b_pool, m_pool_scale, m_w_out, m_norm2_pre, m_norm2_post, m_w_fc1, m_w_fc2, v_w_ada, v_b_ada, v_norm1_pre, v_norm1_post, v_w_in, v_w_spatial, v_b_spatial, v_ln_v_gain, v_ln_v_bias, v_w_pool, v_b_pool, v_pool_scale, v_w_out, v_norm2_pre, v_norm2_post, v_w_fc1, v_w_fc2):
    t_len = x.shape[1]
    me = 4 * lax.axis_index("x") + 2 * lax.axis_index("y") + lax.axis_index("c")
    ada_cols = w_ada.shape[1]
    tt = lambda want: min(want, t_len)

    x2 = x.reshape(t_len, D)
    tgt = loss_target.reshape(t_len, D)
    row = lambda a: a.reshape(1, -1)

    b_my = lax.dynamic_slice_in_dim(b_ada, me * ada_cols, ada_cols).reshape(1, ada_cols)
    w_in_shard, w_out_shard, w1_shard, w2_shard = _cast_shards([w_in.T, w_out, w_fc1, w_fc2])

    bs_rows = jnp.repeat(b_spatial.T, GROUP, axis=1)
    attn_consts = (w_spatial, bs_rows, row(ln_v_gain), row(ln_v_bias), w_pool, row(b_pool), row(pool_scale))

    (z, cat, mix, x1, h2, r_begun, f_head, mod, sc), (w1_early, w2_early), w_out_all, w_in_t = _attn_fwd(
        tt(TT_ATTN_FWD), x2, jnp.broadcast_to(c, (8, D)), w_ada, b_my, row(norm1_pre), row(norm1_post),
        w_in_shard, w_out_shard, *attn_consts, (w1_shard, w2_shard), row(norm2_pre))
    (r_early, f_early), (w1_late, w2_late) = _mlp_fwd_early(
        tt(TT_MLP_FWD), r_begun, h2, f_head, w1_early, w2_early)
    r_late, df, da, dmix, dx1, red_fwd, red_bwd = _mlp_late_bwd(
        tt(TT_MLP), r_early, x1, h2, f_early, tgt, mix, mod, row(norm2_pre), row(norm2_post), row(norm1_post),
        w1_early, w2_early, w1_late, w2_late)
    own_w1, own_w2, sums_w1, sums_w2, diag_w1, diag_w2 = _mlp_wgrad(tt(TT_WGRAD), r_early, r_late, da, df, h2)
    (grad_x, p_in, p_out, small), (arr_w1, arr_w2) = _attn_bwd(
        tt(TT_ATTN_BWD), dmix, dx1, x2, z, cat, mod, row(norm1_pre), w_in_t, w_out_all, *attn_consts,
        red_fwd, red_bwd, [sums_w1, sums_w2])
    (grad_in_t, grad_out, total), ((grad_w1, d_w1, m_w1, v_w1), (grad_w2, d_w2, m_w2, v_w2)) = _tail_comm(
        [p_in.reshape(N_DEV, D_Z // N_DEV, D), p_out.reshape(N_DEV, D // N_DEV, D)], small, 64,
        [(w_fc1, own_w1, arr_w1, diag_w1, m_w_fc1, v_w_fc1), (w_fc2, own_w2, arr_w2, diag_w2, m_w_fc2, v_w_fc2)])

    d_out, m_out, v_out = _adamw_shard("adamw_out", 128, w_out, grad_out, m_w_out, v_w_out)
    d_in_t, m_in_t, v_in_t = _adamw_shard("adamw_in", D_Z // N_DEV, w_in.T, grad_in_t, m_w_in.T, v_w_in.T)
    table = jnp.concatenate([total[0:PACK_FINE, :], total[PK_TABLE_B:PK_MISC, :]], axis=0)
    dmod_all = table.reshape(N_DEV, 8, D)[:, :6, :].reshape(N_DEV, 6 * D)
    dmod_cols = lax.dynamic_slice_in_dim(dmod_all, me * ada_cols, ada_cols, axis=1)
    grad_ada, d_ada, m_ada, v_ada = _adamw_ada(256, w_ada, sc, dmod_cols, m_w_ada, v_w_ada)

    six = lambda a: a.reshape(6, D)
    small_params = [
        (six(b_ada), six(m_b_ada), six(v_b_ada)),
        (row(norm1_pre), row(m_norm1_pre), row(v_norm1_pre)),
        (row(norm1_post), row(m_norm1_post), row(v_norm1_post)),
        (row(norm2_pre), row(m_norm2_pre), row(v_norm2_pre)),
        (row(norm2_post), row(m_norm2_post), row(v_norm2_post)),
        (row(ln_v_gain), row(m_ln_v_gain), row(v_ln_v_gain)),
        (row(ln_v_bias), row(m_ln_v_bias), row(v_ln_v_bias)),
        (row(pool_scale), row(m_pool_scale), row(v_pool_scale)),
        (row(b_pool), row(m_b_pool), row(v_b_pool)),
        (b_spatial, m_b_spatial, v_b_spatial),
        (w_spatial, m_w_spatial, v_w_spatial),
        (w_pool, m_w_pool, v_w_pool),
    ]
    outs = _adamw_small(total, small_params)
    loss = outs[0][0, 0]
    names = ["b_ada", "norm1_pre", "norm1_post", "norm2_pre", "norm2_post", "ln_v_gain", "ln_v_bias", "pool_scale",
             "b_pool", "b_spatial", "w_spatial", "w_pool"]
    shapes = dict(b_ada=b_ada.shape, norm1_pre=norm1_pre.shape, norm1_post=norm1_post.shape,
                  norm2_pre=norm2_pre.shape, norm2_post=norm2_post.shape, ln_v_gain=ln_v_gain.shape,
                  ln_v_bias=ln_v_bias.shape, pool_scale=pool_scale.shape, b_pool=b_pool.shape,
                  b_spatial=b_spatial.shape, w_spatial=w_spatial.shape, w_pool=w_pool.shape)
    res = {}
    for k, nm in enumerate(names):
        res[nm] = tuple(o.reshape(shapes[nm]) for o in outs[1 + 4 * k:5 + 4 * k])
    res["w_ada"] = (grad_ada, d_ada, m_ada, v_ada)
    res["w_in"] = (grad_in_t.T, d_in_t.T, m_in_t.T, v_in_t.T)
    res["w_out"] = (grad_out, d_out, m_out, v_out)
    res["w_fc1"] = (grad_w1, d_w1, m_w1, v_w1)
    res["w_fc2"] = (grad_w2, d_w2, m_w2, v_w2)

    order = ["w_ada", "b_ada", "norm1_pre", "norm1_post", "w_in", "w_spatial", "b_spatial", "ln_v_gain", "ln_v_bias",
             "w_pool", "b_pool", "pool_scale", "w_out", "norm2_pre", "norm2_post", "w_fc1", "w_fc2"]
    return (loss, grad_x.reshape(x.shape),
            *[res[nm][0] for nm in order], *[res[nm][1] for nm in order],
            *[res[nm][2] for nm in order], *[res[nm][3] for nm in order])
```

```python
import functools

import jax
import jax.numpy as jnp
from jax import lax
from jax.experimental import pallas as pl
from jax.experimental.pallas import tpu as pltpu

F32 = jnp.float32
BF16 = jnp.bfloat16
MESH = pl.DeviceIdType.MESH

N_DEV = 8
D = 1024
D_A = 512
D_B = 512
D_Z = 2 * D_A + D_B
N_HEADS = 4
CHUNK = 128
WINDOWS = (2, 4, 8, 16)
GROUP = 128
D_FF = 4096
FF_BLK = D_FF // N_DEV
HALO = 16
EPS = 1e-6
VMEM_LIMIT = 60 * 1024 * 1024

ADAM_LR = 0.001
ADAM_B1 = 0.9
ADAM_B2 = 0.999
ADAM_EPS = 1e-08
ADAM_WD = 0.01
ADAM_STEP = 10

ROW_DMOD = 0
ROW_N1PRE, ROW_N1POST, ROW_N2PRE, ROW_N2POST = 8, 9, 10, 11
ROW_LN = 12
ROW_POOL = 13
ROW_LOSS = 14
ROW_BS = 16
ROW_WS = 24
ROW_WP = 88
SMALL_ROWS = 152
PACK_FINE = 40
PACK_HALF = PACK_FINE + 64
PACK_ROWS = 2 * PACK_HALF
PK_WS = PACK_FINE
PK_TABLE_B = PACK_HALF
PK_MISC = PK_TABLE_B + 24
PK_BS = PK_MISC + 8
PK_WP = PK_BS + 8


def _table_row(b):
    if isinstance(b, int):
        return 8 * b if 8 * b < PACK_FINE else 8 * b + PK_TABLE_B - PACK_FINE
    return 8 * b + jnp.where(8 * b < PACK_FINE, 0, PK_TABLE_B - PACK_FINE)


def _dot(a, b):
    return jnp.dot(a, b, preferred_element_type=F32)


def _dot_nt(a, b):
    return lax.dot_general(a, b, (((1,), (1,)), ((), ())), preferred_element_type=F32)


def _dot_tn(a, b):
    return lax.dot_general(a, b, (((0,), (0,)), ((), ())), preferred_element_type=F32)


def _rstd(v):
    return lax.rsqrt(jnp.mean(v * v, axis=-1, keepdims=True) + EPS)


def _rms_bwd(d_hat, hat, rstd):
    return rstd * (d_hat - hat * jnp.mean(d_hat * hat, axis=-1, keepdims=True))


def _rms_bwd_gained(g, gain, hat, rstd):
    g_hat = g * hat
    d_v = rstd * (g * gain - hat * jnp.mean(g_hat * gain, axis=-1, keepdims=True))
    return d_v, _colsum(g_hat)


_K0 = 0.7978845608028654
_K1 = 0.044715


def _gelu_parts(v):
    t = jnp.tanh(v * (_K0 + (_K0 * _K1) * (v * v)))
    return t, v * (0.5 + 0.5 * t)


def _gelu_grad(v, t):
    return (0.5 + 0.5 * t) + (0.5 * v) * (1.0 - t * t) * (_K0 + (3.0 * _K0 * _K1) * (v * v))


def _colsum(v):
    return jnp.sum(v, axis=0, keepdims=True)


def _full(shape):
    n = len(shape)
    return pl.BlockSpec(shape, lambda *_: (0,) * n)


def _resident(shape):
    n = len(shape)
    return pl.BlockSpec(shape, lambda *_: (0,) * n, pipeline_mode=pl.Buffered(1))


def _place():
    x, y, c = lax.axis_index("x"), lax.axis_index("y"), lax.axis_index("c")
    return x, y, c


def _flip(v, bit):
    return 1 - v if bit else v


def _peer(x, y, c, k):
    return (_flip(x, (k >> 2) & 1), _flip(y, (k >> 1) & 1), _flip(c, k & 1))


def _index(p):
    return 4 * p[0] + 2 * p[1] + p[2]


def _cast_shards(shards):
    def body(*refs):
        for src, dst in zip(refs[:len(shards)], refs[len(shards):]):
            dst[...] = src[...].astype(BF16)

    vm = pl.BlockSpec(memory_space=pltpu.VMEM)
    return pl.pallas_call(
        body, name="cast_shards", out_shape=tuple(jax.ShapeDtypeStruct(s.shape, BF16) for s in shards),
        in_specs=[vm] * len(shards), out_specs=tuple([vm] * len(shards)),
    )(*shards)


FC_EARLY = 6
FC_HEAD = 2
R_HEAD_COLS = (FC_EARLY - FC_HEAD) * FF_BLK
WGRAD_ORDER = (7, 6, 1, 3, 5, 2, 4, 0)


def _early_col(j):
    return R_HEAD_COLS + j * FF_BLK if j < FC_HEAD else (j - FC_HEAD) * FF_BLK


class _Copies:
    def __init__(self, entries, send_sems, recv_sems):
        self.place = _place()
        self.entries, self.send_sems, self.recv_sems = entries, send_sems, recv_sems

    def _copy(self, i, arrival=False):
        src, dst, rel = self.entries[i]
        return pltpu.make_async_remote_copy(
            src_ref=dst if arrival else src, dst_ref=dst, send_sem=self.send_sems.at[i],
            recv_sem=self.recv_sems.at[i], device_id=_peer(*self.place, rel), device_id_type=MESH)

    def start(self, *which):
        for i in which:
            self._copy(i).start()

    def wait_recv(self, *which):
        for i in which:
            self._copy(i, arrival=True).wait_recv()

    def wait_send(self, *which):
        for i in which:
            self._copy(i).wait_send()


TAIL_STEPS = 3


def _tail_comm(parts, small, row_chunk):
    n = len(parts)

    def body(*refs):
        p_refs, small_ref = refs[:n], refs[n]
        outs = refs[n + 1:]
        g_refs, total_ref = outs[:n], outs[n]
        scr = outs[n + 1:]
        from_sib = scr[0:n]
        chip_out = scr[n:2 * n]
        chip_in = scr[2 * n:3 * n]
        pack, pack_sib, fine, bulk, total_scr = scr[3 * n:3 * n + 5]
        send_a, recv_a, send_b, recv_b, send_s, recv_s = scr[3 * n + 5:]
        step = pl.program_id(0)
        x, y, c = _place()
        me = _index((x, y, c))
        sibling = (x, y, 1 - c)
        my_chip = 2 * x + y
        others = [(1 - x, y), (x, 1 - y), (1 - x, 1 - y)]
        my_half = pl.ds(pl.multiple_of(PACK_HALF * c, 8), PACK_HALF)

        def pack_to_sibling():
            return pltpu.make_async_remote_copy(
                src_ref=pack, dst_ref=pack_sib, send_sem=send_s.at[0], recv_sem=recv_s.at[0],
                device_id=sibling, device_id_type=MESH)

        def half_to_chip(r, part):
            buf = (fine, bulk)[part]
            return pltpu.make_async_remote_copy(
                src_ref=buf.at[my_chip], dst_ref=buf.at[my_chip],
                send_sem=send_s.at[1 + 3 * part + r], recv_sem=recv_s.at[1 + 3 * part + r],
                device_id=(*others[r], c), device_id_type=MESH)

        def half_from_chip(r, part):
            k = 2 * others[r][0] + others[r][1]
            buf = (fine, bulk)[part]
            return pltpu.make_async_remote_copy(
                src_ref=buf.at[k], dst_ref=buf.at[k],
                send_sem=send_s.at[1 + 3 * part + r], recv_sem=recv_s.at[1 + 3 * part + r],
                device_id=(*others[r], c), device_id_type=MESH)

        def total_to_sibling():
            return pltpu.make_async_remote_copy(
                src_ref=total_scr.at[my_half], dst_ref=total_scr.at[my_half],
                send_sem=send_s.at[7], recv_sem=recv_s.at[7], device_id=sibling, device_id_type=MESH)

        def total_from_sibling():
            sib_half = pl.ds(pl.multiple_of(PACK_HALF * (1 - c), 8), PACK_HALF)
            return pltpu.make_async_remote_copy(
                src_ref=total_scr.at[sib_half], dst_ref=total_scr.at[sib_half],
                send_sem=send_s.at[7], recv_sem=recv_s.at[7], device_id=sibling, device_id_type=MESH)

        def to_sibling(a, k):
            return pltpu.make_async_remote_copy(
                src_ref=p_refs[a].at[2 * k + (1 - c)], dst_ref=from_sib[a].at[k],
                send_sem=send_a.at[a], recv_sem=recv_a.at[a], device_id=sibling, device_id_type=MESH)

        def all_from_sibling(a):
            return pltpu.make_async_remote_copy(
                src_ref=from_sib[a], dst_ref=from_sib[a], send_sem=send_a.at[a], recv_sem=recv_a.at[a],
                device_id=sibling, device_id_type=MESH)

        def to_chip(a, r):
            return pltpu.make_async_remote_copy(
                src_ref=chip_out[a].at[r], dst_ref=chip_in[a].at[r],
                send_sem=send_b.at[3 * a + r], recv_sem=recv_b.at[3 * a + r],
                device_id=(*others[r], c), device_id_type=MESH)

        @pl.when(step == 0)
        def _():
            pack[0:PACK_FINE, :] = jnp.zeros((PACK_FINE, D), F32)
            pack[PK_TABLE_B:PK_MISC, :] = jnp.zeros((PK_MISC - PK_TABLE_B, D), F32)
            pack[pl.ds(pl.multiple_of(_table_row(me), 8), 8), :] = small_ref[0:8, :]
            pack[PK_WS:PK_WS + 64, :] = small_ref[ROW_WS:ROW_WS + 64, :]
            pack[PK_MISC:PK_MISC + 8, :] = small_ref[ROW_N1PRE:ROW_N1PRE + 8, :]
            pack[PK_BS:PK_BS + 8, :] = small_ref[ROW_BS:ROW_BS + 8, :]
            pack[PK_WP:PK_WP + 64, :] = small_ref[ROW_WP:ROW_WP + 64, :]
            pack_to_sibling().start()
            for a in range(n):
                for k in range(4):
                    to_sibling(a, k).start()

        @pl.when(step == 1)
        def _():
            pack_to_sibling().wait_recv()
            chip_sum = pack[my_half, :] + pack_sib[my_half, :]
            fine[my_chip] = chip_sum[:PACK_FINE, :]
            bulk[my_chip] = chip_sum[PACK_FINE:, :].astype(BF16)
            for r in range(3):
                half_to_chip(r, 0).start()
                half_to_chip(r, 1).start()
            for a in range(n):
                all_from_sibling(a).wait_recv()
                rows = p_refs[a].shape[1]
                for r in range(3):
                    k = 2 * others[r][0] + others[r][1]
                    for s in range(0, rows, row_chunk):
                        sl = pl.ds(s, row_chunk)
                        chip_out[a][r, sl, :] = (p_refs[a][2 * k + c, sl, :].astype(F32)
                                                 + from_sib[a][k, sl, :].astype(F32)).astype(BF16)
                    to_chip(a, r).start()
                for s in range(0, rows, row_chunk):
                    sl = pl.ds(s, row_chunk)
                    g_refs[a][sl, :] = (p_refs[a][2 * my_chip + c, sl, :].astype(F32)
                                        + from_sib[a][my_chip, sl, :].astype(F32))

        @pl.when(step == TAIL_STEPS - 1)
        def _():
            for r in range(3):
                half_from_chip(r, 0).wait_recv()
                half_from_chip(r, 1).wait_recv()
            half_start = pl.multiple_of(PACK_HALF * c, 8)
            total_scr[pl.ds(half_start, PACK_FINE), :] = ((fine[0] + fine[1]) + fine[2]) + fine[3]
            total_scr[pl.ds(half_start + PACK_FINE, PACK_HALF - PACK_FINE), :] = (
                (bulk[0].astype(F32) + bulk[1].astype(F32)) + bulk[2].astype(F32)) + bulk[3].astype(F32)
            total_to_sibling().start()
            for a in range(n):
                rows = p_refs[a].shape[1]
                for r in range(3):
                    to_chip(a, r).wait_recv()
                    for s in range(0, rows, row_chunk):
                        sl = pl.ds(s, row_chunk)
                        g_refs[a][sl, :] = g_refs[a][sl, :] + chip_in[a][r, sl, :].astype(F32)
            total_from_sibling().wait_recv()
            total_ref[...] = total_scr[...]
            for a in range(n):
                all_from_sibling(a).wait_send()
                for r in range(3):
                    to_chip(a, r).wait_send()
            pack_to_sibling().wait_send()
            for r in range(3):
                half_to_chip(r, 0).wait_send()
                half_to_chip(r, 1).wait_send()
            total_to_sibling().wait_send()

    return pl.pallas_call(
        body, name="tail_comm", grid=(TAIL_STEPS,),
        out_shape=tuple([jax.ShapeDtypeStruct(p.shape[1:], F32) for p in parts]
                        + [jax.ShapeDtypeStruct((PACK_ROWS, D), F32)]),
        in_specs=[_resident(p.shape) for p in parts] + [_resident(small.shape)],
        out_specs=tuple([_full(p.shape[1:]) for p in parts] + [_full((PACK_ROWS, D))]),
        scratch_shapes=(
            [pltpu.VMEM((4,) + p.shape[1:], BF16) for p in parts]
            + [pltpu.VMEM((3,) + p.shape[1:], BF16) for p in parts]
            + [pltpu.VMEM((3,) + p.shape[1:], BF16) for p in parts]
            + [pltpu.VMEM((PACK_ROWS, D), F32), pltpu.VMEM((PACK_ROWS, D), F32),
               pltpu.VMEM((4, PACK_FINE, D), F32), pltpu.VMEM((4, PACK_HALF - PACK_FINE, D), BF16),
               pltpu.VMEM((PACK_ROWS, D), F32)]
            + [pltpu.SemaphoreType.DMA((n,)), pltpu.SemaphoreType.DMA((n,)),
               pltpu.SemaphoreType.DMA((3 * n,)), pltpu.SemaphoreType.DMA((3 * n,)),
               pltpu.SemaphoreType.DMA((8,)), pltpu.SemaphoreType.DMA((8,))]),
        compiler_params=pltpu.CompilerParams(dimension_semantics=("arbitrary",), vmem_limit_bytes=VMEM_LIMIT),
    )(*parts, small)


def _tril_mask():
    row = lax.broadcasted_iota(jnp.int32, (CHUNK, CHUNK), 0)
    col = lax.broadcasted_iota(jnp.int32, (CHUNK, CHUNK), 1)
    return (col <= row).astype(F32)


def _window_sums(ext):
    s2 = ext + pltpu.roll(ext, 1, 0)
    t4 = s2[:, GROUP:]
    s4 = t4 + pltpu.roll(t4, 2, 0)
    t8 = s4[:, GROUP:]
    s8 = t8 + pltpu.roll(t8, 4, 0)
    t16 = s8[:, GROUP:]
    s16 = t16 + pltpu.roll(t16, 8, 0)
    return [s2[:, :GROUP], s4[:, :GROUP], s8[:, :GROUP], s16]


def _inv_counts(first_pos, rows):
    pos = first_pos + lax.broadcasted_iota(jnp.int32, (rows, 1), 0)
    return [1.0 / jnp.minimum(pos + 1, w).astype(F32) for w in WINDOWS]


def _pool_diff(zb, halo, first_pos):
    tt = zb.shape[0]
    sums = _window_sums(jnp.concatenate([halo, zb], axis=0))
    inv = _inv_counts(first_pos, tt)
    return [sums[g][HALO:, :] * inv[g] - zb[:, g * GROUP:(g + 1) * GROUP] for g in range(len(WINDOWS))]


def _row_blocks(scr, rows, place):
    def block(rel):
        start = pl.multiple_of(rows * _index(_peer(*place, rel)), rows)
        return scr.at[pl.ds(start, rows), :]

    def entries(shard_ref):
        return ([(shard_ref, block(0), rel) for rel in (1, 2, 4, 6)]
                + [(block(rel), block(rel), 1) for rel in (2, 4, 6)])
    return block, entries


def _attn_fwd(tt, x, c8, w_ada, b_my, n1pre, n1post, w_in_shard, w_out_shard, w_sp, bs_rows, ln_g, ln_b, w_pool,
              b_pool, pool_scale, fc_shards, n2pre):
    t_len = x.shape[0]
    nt = t_len // tt
    ncol = w_ada.shape[1]

    def body(x_ref, xb_ref, c_ref, wada_ref, b_ref, n1pre_ref, n1post_ref, wi_ref, wo_ref, wsp_ref, bs_ref,
             lng_ref, lnb_ref, wp_ref, bp_ref, ps_ref, w1_ref, w2_ref, n2pre_ref,
             z_ref, cat_ref, mix_ref, x1_ref, h2_ref, r_ref, f_ref, mod_out, sc_out, e1_ref, e2_ref, wout_ref,
             win_out, carry, land1, land2, sib1, sib2, cat_keep, wout_scr, win_ref, mod_ref, cg, mg, part,
             send_sems, recv_sems, local_sems, wo_send, wo_recv, wi_send, wi_recv, ada_send, ada_recv):
        i = pl.program_id(0)
        place = px, py, pc = _place()
        me = _index(place)
        wo_block, wo_entries = _row_blocks(wout_scr, w_out_shard.shape[0], place)
        wi_block, wi_entries = _row_blocks(win_ref, w_in_shard.shape[0], place)
        wo_copies = _Copies(wo_entries(wo_ref), wo_send, wo_recv)
        wi_copies = _Copies(wi_entries(wi_ref), wi_send, wi_recv)
        wo_keep = pltpu.make_async_copy(wout_scr, wout_ref, local_sems.at[8])
        wi_keep = pltpu.make_async_copy(win_ref, win_out, local_sems.at[9])
        ada = _Copies([(c_ref, cg.at[me], k) for k in range(1, N_DEV)]
                      + [(part, mg.at[me], k) for k in range(1, N_DEV)], ada_send, ada_recv)
        copies = _Copies(
            [(w1_ref, sib1, 1), (w2_ref, sib2, 1),
             (w1_ref, land1.at[0], 2), (w2_ref, land2.at[0], 2),
             (w1_ref, land1.at[1], 4), (w2_ref, land2.at[1], 4),
             (land1.at[0], e1_ref.at[3], 1), (land2.at[0], e2_ref.at[3], 1),
             (land1.at[1], e1_ref.at[5], 1), (land2.at[1], e2_ref.at[5], 1)],
            send_sems, recv_sems)
        keep = [pltpu.make_async_copy(w1_ref, e1_ref.at[0], local_sems.at[0]),
                pltpu.make_async_copy(w2_ref, e2_ref.at[0], local_sems.at[1]),
                pltpu.make_async_copy(land1.at[0], e1_ref.at[2], local_sems.at[2]),
                pltpu.make_async_copy(land1.at[1], e1_ref.at[4], local_sems.at[3]),
                pltpu.make_async_copy(land2.at[0], e2_ref.at[2], local_sems.at[4]),
                pltpu.make_async_copy(land2.at[1], e2_ref.at[4], local_sems.at[5]),
                pltpu.make_async_copy(sib1, e1_ref.at[1], local_sems.at[6]),
                pltpu.make_async_copy(sib2, e2_ref.at[1], local_sems.at[7])]

        @pl.when(i == 0)
        def _():
            ada.start(*range(N_DEV - 1))
            wi_copies.start(0, 1, 2, 3)
            cg[me] = c_ref[...]
            wi_rows, wo_rows = w_in_shard.shape[0], w_out_shard.shape[0]
            win_ref[pl.ds(pl.multiple_of(wi_rows * me, wi_rows), wi_rows), :] = wi_ref[...]
            wout_scr[pl.ds(pl.multiple_of(wo_rows * me, wo_rows), wo_rows), :] = wo_ref[...]
            carry[...] = jnp.zeros_like(carry)

            ada.wait_recv(*range(N_DEV - 1))
            c_all = jnp.concatenate([cg[j, 0:1, :] for j in range(N_DEV)], axis=0)
            sc = c_all * jax.nn.sigmoid(c_all)
            sc_out[...] = sc
            part[...] = _dot(sc.astype(BF16), wada_ref[...].astype(BF16)) + b_ref[...]
            ada.start(*range(N_DEV - 1, 2 * (N_DEV - 1)))
            wo_copies.start(0, 1, 2, 3)
            copies.start(0, 1, 2, 4, 3, 5)
            keep[0].start()
            keep[1].start()
            mg[me] = part[...]
            ada.wait_recv(*range(N_DEV - 1, 2 * (N_DEV - 1)))
            mod_ref[...] = jnp.zeros_like(mod_ref)
            for j in range(N_DEV):
                for m in range(6):
                    lo, hi = max(ncol * j, D * m), min(ncol * (j + 1), D * (m + 1))
                    if lo < hi:
                        mod_ref[m:m + 1, lo - D * m:hi - D * m] = mg[j, pl.ds(me, 1), lo - ncol * j:hi - ncol * j]
            mod_out[...] = mod_ref[...]

            wi_copies.wait_recv(1, 2, 3)
            wi_copies.start(4, 5, 6)
            wi_copies.wait_recv(0, 4, 5, 6)
            wi_keep.start()

        @pl.when(i == nt // 2 + ATTN_LAG)
        def _():
            copies.wait_recv(2, 4)
            copies.start(6, 8)
            keep[2].start()
            keep[3].start()

        @pl.when(i == nt - 1 + ATTN_LAG)
        def _():
            copies.wait_recv(3, 5)
            copies.start(7, 9)
            keep[4].start()
            keep[5].start()

        shift1, scale1, gate1 = mod_ref[0:1, :], mod_ref[1:2, :], mod_ref[2:3, :]

        @pl.when(i < nt)
        def _():
            xv = x_ref[...]
            h1 = (xv * _rstd(xv)) * (n1pre_ref[...] * (1.0 + scale1)) + shift1
            z = _dot_nt(h1.astype(BF16), win_ref[...])
            z_ref[...] = z

            _, ga = _gelu_parts(z[:, :2 * D_A])
            u, vr = ga[:, :D_A], ga[:, D_A:]
            dv = vr - jnp.mean(vr, axis=-1, keepdims=True)
            v = (dv * lax.rsqrt(jnp.mean(dv * dv, axis=-1, keepdims=True) + EPS)) * lng_ref[...] + lnb_ref[...]
            vb = v.astype(BF16)
            mask = _tril_mask()
            wc = [(wsp_ref[h] * mask).astype(BF16) for h in range(N_HEADS)]
            for ch in range(tt // CHUNK):
                rows = slice(ch * CHUNK, (ch + 1) * CHUNK)
                for h in range(N_HEADS):
                    cols = slice(h * GROUP, (h + 1) * GROUP)
                    mixed = _dot(wc[h], vb[rows, cols]) + bs_ref[:, cols]
                    cat_ref[rows, cols] = (u[rows, cols] * mixed).astype(BF16)

            zb = z[:, 2 * D_A:]
            diff = _pool_diff(zb, carry[...], i * tt)
            carry[...] = zb[tt - HALO:, :]
            for g in range(len(WINDOWS)):
                cols = slice(g * GROUP, (g + 1) * GROUP)
                pre = _dot(diff[g].astype(BF16), wp_ref[g].astype(BF16)) + bp_ref[:, cols]
                cat_ref[:, D_A + g * GROUP:D_A + (g + 1) * GROUP] = (pre * ps_ref[:, cols]).astype(BF16)
            cat_keep[i % (ATTN_LAG + 1)] = cat_ref[...]

        @pl.when(i == 0)
        def _():
            copies.wait_recv(0, 1)
            keep[6].start()
            keep[7].start()

        @pl.when(i == 1)
        def _():
            wo_copies.wait_recv(1, 2, 3)
            wo_copies.start(4, 5, 6)

        @pl.when(i == ATTN_LAG)
        def _():
            wo_copies.wait_recv(0, 4, 5, 6)
            wo_keep.start()

        @pl.when(i >= ATTN_LAG)
        def _():
            xv = xb_ref[...]
            mix = _dot(cat_keep[(i - ATTN_LAG) % (ATTN_LAG + 1)], wout_scr[...])
            mix_ref[...] = mix
            x1v = xv + (mix * _rstd(mix)) * (gate1 * n1post_ref[...])
            x1_ref[...] = x1v
            shift2, scale2 = mod_ref[3:4, :], mod_ref[4:5, :]
            h2 = ((x1v * _rstd(x1v)) * (n2pre_ref[...] * (1.0 + scale2)) + shift2).astype(BF16)
            h2_ref[...] = h2
            for j, (w1, w2) in enumerate(((w1_ref, w2_ref), (sib1, sib2))):
                ra = jnp.maximum(_dot(h2, w1[...]), 0.0)
                r = (ra * ra).astype(BF16)
                r_ref[:, j * FF_BLK:(j + 1) * FF_BLK] = r
                if j == 0:
                    f_ref[...] = _dot(r, w2[...])
                else:
                    f_ref[...] += _dot(r, w2[...])

        @pl.when(i == nt - 1 + ATTN_LAG)
        def _():
            copies.wait_recv(6, 7, 8, 9)
            copies.wait_send(*range(10))
            wo_copies.wait_send(*range(7))
            wi_copies.wait_send(*range(7))
            ada.wait_send(*range(2 * (N_DEV - 1)))
            for cp in keep:
                cp.wait()
            wo_keep.wait()
            wi_keep.wait()

    first = lambda w: pl.BlockSpec((tt, w), lambda i: (jnp.minimum(i, nt - 1), 0))
    second = lambda w: pl.BlockSpec((tt, w), lambda i: (jnp.maximum(i - ATTN_LAG, 0), 0))
    r_head = pl.BlockSpec((tt, FC_HEAD * FF_BLK),
                          lambda i: (jnp.maximum(i - ATTN_LAG, 0), R_HEAD_COLS // (FC_HEAD * FF_BLK)))
    hbm = pl.BlockSpec(memory_space=pl.ANY)
    outs = pl.pallas_call(
        body, name="attn_fwd", grid=(nt + ATTN_LAG,),
        out_shape=tuple([jax.ShapeDtypeStruct((t_len, D_Z), F32), jax.ShapeDtypeStruct((t_len, D), BF16),
                         jax.ShapeDtypeStruct((t_len, D), F32), jax.ShapeDtypeStruct((t_len, D), F32),
                         jax.ShapeDtypeStruct((t_len, D), BF16),
                         jax.ShapeDtypeStruct((t_len, FC_EARLY * FF_BLK), BF16),
                         jax.ShapeDtypeStruct((t_len, D), F32)]
                        + [jax.ShapeDtypeStruct((8, D), F32), jax.ShapeDtypeStruct((N_DEV, D), F32)]
                        + [jax.ShapeDtypeStruct((FC_EARLY,) + s.shape, BF16) for s in fc_shards]
                        + [jax.ShapeDtypeStruct((D, D), BF16), jax.ShapeDtypeStruct((D_Z, D), BF16)]),
        in_specs=[first(D), second(D), _full((8, D)), _resident(w_ada.shape), _full((1, ncol)), _full((1, D)),
                  _full((1, D)), _resident(w_in_shard.shape), _resident(w_out_shard.shape),
                  _full((N_HEADS, CHUNK, CHUNK)), _full((CHUNK, D_A)), _full((1, D_A)), _full((1, D_A)),
                  _full((len(WINDOWS), GROUP, GROUP)), _full((1, D_B)), _full((1, D_B)),
                  _resident(fc_shards[0].shape), _resident(fc_shards[1].shape), _full((1, D))],
        out_specs=(first(D_Z), first(D), second(D), second(D), second(D), r_head, second(D),
                   _full((8, D)), _full((N_DEV, D)), hbm, hbm, hbm, hbm),
        scratch_shapes=[pltpu.VMEM((HALO, D_B), F32),
                        pltpu.VMEM((2,) + fc_shards[0].shape, BF16), pltpu.VMEM((2,) + fc_shards[1].shape, BF16),
                        pltpu.VMEM(fc_shards[0].shape, BF16), pltpu.VMEM(fc_shards[1].shape, BF16),
                        pltpu.VMEM((ATTN_LAG + 1, tt, D), BF16), pltpu.VMEM((D, D), BF16),
                        pltpu.VMEM((D_Z, D), BF16), pltpu.VMEM((8, D), F32),
                        pltpu.VMEM((N_DEV, 8, D), F32), pltpu.VMEM((N_DEV, N_DEV, ncol), F32),
                        pltpu.VMEM((N_DEV, ncol), F32),
                        pltpu.SemaphoreType.DMA((10,)), pltpu.SemaphoreType.DMA((10,)),
                        pltpu.SemaphoreType.DMA((10,)),
                        pltpu.SemaphoreType.DMA((7,)), pltpu.SemaphoreType.DMA((7,)),
                        pltpu.SemaphoreType.DMA((7,)), pltpu.SemaphoreType.DMA((7,)),
                        pltpu.SemaphoreType.DMA((2 * (N_DEV - 1),)), pltpu.SemaphoreType.DMA((2 * (N_DEV - 1),))],
        compiler_params=pltpu.CompilerParams(dimension_semantics=("arbitrary",), vmem_limit_bytes=VMEM_LIMIT),
    )(x, x, c8, w_ada, b_my, n1pre, n1post, w_in_shard, w_out_shard, w_sp, bs_rows, ln_g, ln_b, w_pool, b_pool,
      pool_scale, *fc_shards, n2pre)
    return outs[:9], outs[9:11], outs[11], outs[12]


def _mlp_fwd_early(tt, r_begun, h2, f_head, w1_early, w2_early):
    t_len = h2.shape[0]
    nt = t_len // tt
    n_late = N_DEV - FC_EARLY

    def body(r_begun_ref, h2_ref, fh_ref, w1_ref, w2_ref, r_ref, f_ref, l1_ref, l2_ref,
             land1, land2, send_sems, recv_sems, local_sems):
        i = pl.program_id(0)
        copies = _Copies(
            [(w1_ref.at[2], land1, 4), (w2_ref.at[4], land2, 2),
             (land1, l1_ref.at[1], 1), (land2, l2_ref.at[1], 1)],
            send_sems, recv_sems)
        keep = [pltpu.make_async_copy(land1, l1_ref.at[0], local_sems.at[0]),
                pltpu.make_async_copy(land2, l2_ref.at[0], local_sems.at[1])]

        @pl.when(i == 0)
        def _():
            copies.start(0, 1)

        @pl.when(i == nt - 1)
        def _():
            copies.wait_recv(0, 1)
            copies.start(2, 3)
            for cp in keep:
                cp.start()

        h2 = h2_ref[...]
        f_ref[...] = fh_ref[...]
        for j in range(FC_HEAD, FC_EARLY):
            ra = jnp.maximum(_dot(h2, w1_ref[j]), 0.0)
            r = (ra * ra).astype(BF16)
            r_ref[:, _early_col(j):_early_col(j) + FF_BLK] = r
            f_ref[...] += _dot(r, w2_ref[j])

        @pl.when(i == nt - 1)
        def _():
            copies.wait_recv(2, 3)
            copies.wait_send(0, 1, 2, 3)
            for cp in keep:
                cp.wait()

    tile = lambda w: pl.BlockSpec((tt, w), lambda i: (i, 0))
    hbm = pl.BlockSpec(memory_space=pl.ANY)
    outs = pl.pallas_call(
        body, name="mlp_fwd_early", grid=(nt,),
        out_shape=(jax.ShapeDtypeStruct((t_len, FC_EARLY * FF_BLK), BF16), jax.ShapeDtypeStruct((t_len, D), F32),
                   jax.ShapeDtypeStruct((n_late,) + w1_early.shape[1:], BF16),
                   jax.ShapeDtypeStruct((n_late,) + w2_early.shape[1:], BF16)),
        in_specs=[hbm, tile(D), tile(D), _resident((FC_EARLY, D, FF_BLK)), _resident((FC_EARLY, FF_BLK, D))],
        out_specs=(tile(R_HEAD_COLS), tile(D), hbm, hbm),
        input_output_aliases={0: 0},
        scratch_shapes=[pltpu.VMEM(w1_early.shape[1:], BF16), pltpu.VMEM(w2_early.shape[1:], BF16),
                        pltpu.SemaphoreType.DMA((4,)), pltpu.SemaphoreType.DMA((4,)),
                        pltpu.SemaphoreType.DMA((2,))],
        compiler_params=pltpu.CompilerParams(dimension_semantics=("arbitrary",), vmem_limit_bytes=VMEM_LIMIT),
    )(r_begun, h2, f_head, w1_early, w2_early)
    return outs[:2], outs[2:]


def _mlp_late_bwd(tt, r_early, x1, h2, f_early, tgt, mix, mod, n2pre, n2post, n1post,
                  w1_early, w2_early, w1_late, w2_late):
    t_len = x1.shape[0]
    nt = t_len // tt
    n_late = N_DEV - FC_EARLY
    late_cols = n_late * FF_BLK

    def body(re_ref, x1_ref, h2_ref, fe_ref, tgt_ref, mix_ref, mod_ref, n2pre_ref, n2post_ref,
             n1post_ref, w1e_ref, w2e_ref, w1l_ref, w2l_ref,
             rl_ref, df_ref, da_ref, dmix_ref, dx1_ref, redf_ref, redb_ref, dh2_acc):
        i = pl.program_id(0)

        @pl.when(i == 0)
        def _():
            redf_ref[...] = jnp.zeros_like(redf_ref)
            redb_ref[...] = jnp.zeros_like(redb_ref)

        x1v = x1_ref[...]
        gate1, scale2, gate2 = mod_ref[2:3, :], mod_ref[4:5, :], mod_ref[5:6, :]
        h2 = h2_ref[...]
        f = fe_ref[...]
        for j in range(n_late):
            cols = slice(j * FF_BLK, (j + 1) * FF_BLK)
            ra = jnp.maximum(_dot(h2, w1l_ref[j]), 0.0)
            r = (ra * ra).astype(BF16)
            rl_ref[:, cols] = r
            f = f + _dot(r, w2l_ref[j])
        post2 = n2post_ref[...]
        gate_post2 = gate2 * post2
        rf = _rstd(f)
        fhat = f * rf
        err = (x1v + fhat * gate_post2) - tgt_ref[...]
        dy = err * (1.0 / D)
        d_f, sum_f = _rms_bwd_gained(dy, gate_post2, fhat, rf)
        dfv = d_f.astype(BF16)
        df_ref[...] = dfv
        redf_ref[0:1, :] += post2 * sum_f
        redf_ref[1:2, :] += gate2 * sum_f
        redf_ref[2:3, :] += _colsum(err * err)

        for j in range(N_DEV):
            cols = slice(j * FF_BLK, (j + 1) * FF_BLK)
            if j < FC_EARLY:
                w1, w2, r = w1e_ref[j], w2e_ref[j], re_ref[:, _early_col(j):_early_col(j) + FF_BLK]
            else:
                jl = j - FC_EARLY
                w1, w2, r = w1l_ref[jl], w2l_ref[jl], rl_ref[:, jl * FF_BLK:(jl + 1) * FF_BLK]
            dr = _dot_nt(dfv, w2)
            da = (dr * (2.0 * jnp.sqrt(r.astype(F32)))).astype(BF16)
            da_ref[:, cols] = da
            contrib = _dot_nt(da, w1)
            if j == 0:
                dh2_acc[...] = contrib
            else:
                dh2_acc[...] += contrib
        dh2 = dh2_acc[...]
        pre2, post1 = n2pre_ref[...], n1post_ref[...]
        r2 = _rstd(x1v)
        xhat = x1v * r2
        d_x1, sum_h = _rms_bwd_gained(dh2, pre2 * (1.0 + scale2), xhat, r2)
        dx1 = dy + d_x1
        dx1_ref[...] = dx1
        mixv = mix_ref[...]
        rm = _rstd(mixv)
        mhat = mixv * rm
        d_mix, sum_m = _rms_bwd_gained(dx1, gate1 * post1, mhat, rm)
        dmix_ref[...] = d_mix.astype(BF16)
        redb_ref[0:1, :] += _colsum(dh2)
        redb_ref[1:2, :] += pre2 * sum_h
        redb_ref[2:3, :] += (1.0 + scale2) * sum_h
        redb_ref[3:4, :] += post1 * sum_m
        redb_ref[4:5, :] += gate1 * sum_m

    tile = lambda w: pl.BlockSpec((tt, w), lambda i: (i, 0))
    return pl.pallas_call(
        body, name="mlp_late_bwd", grid=(nt,),
        out_shape=(jax.ShapeDtypeStruct((t_len, late_cols), BF16), jax.ShapeDtypeStruct((t_len, D), BF16),
                   jax.ShapeDtypeStruct((t_len, D_FF), BF16), jax.ShapeDtypeStruct((t_len, D), BF16),
                   jax.ShapeDtypeStruct((t_len, D), F32), jax.ShapeDtypeStruct((8, D), F32),
                   jax.ShapeDtypeStruct((8, D), F32)),
        in_specs=[tile(FC_EARLY * FF_BLK), tile(D), tile(D), tile(D), tile(D),
                  tile(D), _full((8, D)), _full((1, D)), _full((1, D)), _full((1, D)),
                  _resident((FC_EARLY, D, FF_BLK)), _resident((FC_EARLY, FF_BLK, D)),
                  _resident((n_late, D, FF_BLK)), _resident((n_late, FF_BLK, D))],
        out_specs=(tile(late_cols), tile(D), tile(D_FF), tile(D), tile(D), _full((8, D)), _full((8, D))),
        scratch_shapes=[pltpu.VMEM((tt, D), F32)],
        compiler_params=pltpu.CompilerParams(dimension_semantics=("arbitrary",), vmem_limit_bytes=VMEM_LIMIT),
    )(r_early, x1, h2, f_early, tgt, mix, mod, n2pre, n2post, n1post, w1_early, w2_early, w1_late, w2_late)


def _mlp_wgrad(tt, r_early, r_late, da, df, h2):
    t_len = df.shape[0]
    nt = t_len // tt
    odd_steps = [j for j, rel in enumerate(WGRAD_ORDER) if rel % 2]

    def relation(j):
        rel = jnp.int32(WGRAD_ORDER[-1])
        for step in range(N_DEV - 2, -1, -1):
            rel = jnp.where(j == step, WGRAD_ORDER[step], rel)
        return rel

    def body(re_ref, rl_ref, da_ref, df_ref, h2_ref, own1_ref, own2_ref, out1_ref, out2_ref, diag1_ref, diag2_ref,
             acc1, acc2, snd1, snd2, sib1, sib2, dsnd1, dsnd2, send_sems, recv_sems):
        j, t = pl.program_id(0), pl.program_id(1)
        rows = pl.ds(pl.multiple_of(t * tt, tt), tt)
        x, y, c = _place()
        accs, snds, sibs = (acc1, acc2), (snd1, snd2), (sib1, sib2)
        dsnds, diags = (dsnd1, dsnd2), (diag1_ref, diag2_ref)

        def to_sibling(a, jj, buf=0):
            return pltpu.make_async_remote_copy(
                src_ref=snds[a].at[buf], dst_ref=sibs[a].at[jj],
                send_sem=send_sems.at[4 * a + jj], recv_sem=recv_sems.at[4 * a + jj],
                device_id=(x, y, 1 - c), device_id_type=MESH)

        def to_diagonal(a):
            return pltpu.make_async_remote_copy(
                src_ref=dsnds[a], dst_ref=diags[a], send_sem=send_sems.at[8 + a], recv_sem=recv_sems.at[8 + a],
                device_id=_peer(x, y, c, 6), device_id_type=MESH)

        @pl.when(t == 0)
        def _():
            acc2[...] = jnp.zeros_like(acc2)
            acc1[...] = jnp.zeros_like(acc1)

        for r_ref, mine in ((re_ref, relation(j) < FC_EARLY), (rl_ref, relation(j) >= FC_EARLY)):
            @pl.when(mine)
            def _():
                acc2[...] += _dot_tn(r_ref[...], df_ref[rows, :])
                acc1[...] += _dot_tn(h2_ref[rows, :], da_ref[...])

        for step, rel in enumerate(WGRAD_ORDER):
            jj = rel // 2

            @pl.when((t == nt - 1) & (j == step))
            def _():
                for a, (own_ref, out_ref) in enumerate(((own1_ref, out1_ref), (own2_ref, out2_ref))):
                    if rel % 2:
                        q = odd_steps.index(step)
                        if q >= 2:
                            to_sibling(a, WGRAD_ORDER[odd_steps[q - 2]] // 2).wait_send()
                        snds[a][q % 2] = accs[a][...].astype(BF16)
                        to_sibling(a, jj, q % 2).start()
                        continue
                    to_sibling(a, jj).wait_recv()
                    chip_sum = accs[a][...] + sibs[a][jj].astype(F32)
                    if rel == 6:
                        dsnds[a][...] = chip_sum.astype(BF16)
                        to_diagonal(a).start()
                    elif rel == 0:
                        own_ref[...] = chip_sum
                    else:
                        out_ref[0] = chip_sum.astype(BF16)
                    if step == N_DEV - 1:
                        for q in (2, 3):
                            to_sibling(a, WGRAD_ORDER[odd_steps[q]] // 2).wait_send()
                        to_diagonal(a).wait_recv()
                        to_diagonal(a).wait_send()

    assert WGRAD_ORDER[-1] == 0 and WGRAD_ORDER[-3:-1] == (2, 4)
    blk = pl.BlockSpec((tt, FF_BLK), lambda j, t: (t, relation(j)))
    early_block = lambda rel: jnp.where(rel < FC_HEAD, rel + FC_EARLY - FC_HEAD, rel - FC_HEAD)
    early = lambda j, t: (jnp.where(relation(j) < FC_EARLY, t, 0),
                          jnp.where(relation(j) < FC_EARLY, early_block(relation(j)), 0))
    late = lambda j, t: (jnp.where(relation(j) < FC_EARLY, 0, t), jnp.maximum(relation(j) - FC_EARLY, 0))
    chip = lambda j, t: (jnp.clip(j - 5, 0, 1), 0, 0)
    hbm = pl.BlockSpec(memory_space=pl.ANY)
    return pl.pallas_call(
        body, name="mlp_wgrad", grid=(N_DEV, nt),
        out_shape=(jax.ShapeDtypeStruct((D, FF_BLK), F32), jax.ShapeDtypeStruct((FF_BLK, D), F32),
                   jax.ShapeDtypeStruct((2, D, FF_BLK), BF16), jax.ShapeDtypeStruct((2, FF_BLK, D), BF16),
                   jax.ShapeDtypeStruct((D, FF_BLK), BF16), jax.ShapeDtypeStruct((FF_BLK, D), BF16)),
        in_specs=[pl.BlockSpec((tt, FF_BLK), early), pl.BlockSpec((tt, FF_BLK), late), blk,
                  _resident((t_len, D)), _resident((t_len, D))],
        out_specs=(_full((D, FF_BLK)), _full((FF_BLK, D)),
                   pl.BlockSpec((1, D, FF_BLK), chip), pl.BlockSpec((1, FF_BLK, D), chip), hbm, hbm),
        scratch_shapes=[pltpu.VMEM((D, FF_BLK), F32), pltpu.VMEM((FF_BLK, D), F32),
                        pltpu.VMEM((2, D, FF_BLK), BF16), pltpu.VMEM((2, FF_BLK, D), BF16),
                        pltpu.VMEM((4, D, FF_BLK), BF16), pltpu.VMEM((4, FF_BLK, D), BF16),
                        pltpu.VMEM((D, FF_BLK), BF16), pltpu.VMEM((FF_BLK, D), BF16),
                        pltpu.SemaphoreType.DMA((10,)), pltpu.SemaphoreType.DMA((10,))],
        compiler_params=pltpu.CompilerParams(dimension_semantics=("arbitrary", "arbitrary"),
                                             vmem_limit_bytes=VMEM_LIMIT),
    )(r_early, r_late, da, df, h2)


def _acc_rows(ref, row0, k, val):
    half = CHUNK // 2
    ref[row0:row0 + half, k * GROUP:(k + 1) * GROUP] += val[:half, :]
    ref[row0:row0 + half, D_A + k * GROUP:D_A + (k + 1) * GROUP] += val[half:, :]


def _attn_bwd(tt, dmix, dx1, x, z, cat, mod, n1pre, w_in_t, w_out, w_sp, bs_rows, ln_g, ln_b, w_pool, b_pool,
              pool_scale, red_fwd, red_bwd, chip_sums):
    t_len = x.shape[0]
    nt = t_len // tt
    hb = tt // HALO
    n_sums = len(chip_sums)

    def body(dmix_ref, dx1_ref, x_ref, z_ref, zprev_ref, cat_ref, mod_ref, n1pre_ref, win_ref, wout_ref, wsp_ref,
             bs_ref, lng_ref, lnb_ref, wp_ref, bp_ref, ps_ref, redf_ref, redb_ref, *rest):
        sum_out = rest[:n_sums]
        gx_ref, gwin_ref, gwout_ref, small_ref = rest[n_sums:n_sums + 4]
        sum_in = rest[n_sums + 4:2 * n_sums + 4]
        carry, acc_in, acc_out, dz_scr, bs_acc, send_sems, recv_sems = rest[2 * n_sums + 4:]
        s = pl.program_id(0)
        i = nt - 1 - s
        px, py, pc = _place()

        def chip_copy(a, r):
            return pltpu.make_async_remote_copy(
                src_ref=sum_out[a].at[r], dst_ref=sum_in[a].at[r],
                send_sem=send_sems.at[2 * a + r], recv_sem=recv_sems.at[2 * a + r],
                device_id=_peer(px, py, pc, 2 * (r + 1)), device_id_type=MESH)

        @pl.when(s == 0)
        def _():
            for a in range(n_sums):
                for r in range(2):
                    chip_copy(a, r).start()
            carry[...] = jnp.zeros_like(carry)
            acc_in[...] = jnp.zeros_like(acc_in)
            acc_out[...] = jnp.zeros_like(acc_out)
            bs_acc[...] = jnp.zeros_like(bs_acc)
            small_ref[...] = jnp.zeros_like(small_ref)
            small_ref[ROW_DMOD + 2:ROW_DMOD + 3, :] = redb_ref[3:4, :]
            small_ref[ROW_DMOD + 3:ROW_DMOD + 5, :] = redb_ref[0:2, :]
            small_ref[ROW_DMOD + 5:ROW_DMOD + 6, :] = redf_ref[0:1, :]
            small_ref[ROW_N1POST:ROW_N1POST + 1, :] = redb_ref[4:5, :]
            small_ref[ROW_N2PRE:ROW_N2PRE + 1, :] = redb_ref[2:3, :]
            small_ref[ROW_N2POST:ROW_N2POST + 1, :] = redf_ref[1:2, :]
            small_ref[ROW_LOSS:ROW_LOSS + 1, :] = redf_ref[2:3, :]

        dmixv = dmix_ref[...]
        dcat = _dot_nt(dmixv, wout_ref[...])
        acc_out[...] += _dot_tn(cat_ref[...], dmixv)

        z = z_ref[...]
        t_g, ga = _gelu_parts(z[:, :2 * D_A])
        u, vr = ga[:, :D_A], ga[:, D_A:]
        dv0 = vr - jnp.mean(vr, axis=-1, keepdims=True)
        rv = lax.rsqrt(jnp.mean(dv0 * dv0, axis=-1, keepdims=True) + EPS)
        vhat = dv0 * rv
        vb = (vhat * lng_ref[...] + lnb_ref[...]).astype(BF16)
        mask = _tril_mask()
        wc = [(wsp_ref[h] * mask).astype(BF16) for h in range(N_HEADS)]

        dya = dcat[:, :D_A]
        for h in range(N_HEADS):
            cols = slice(h * GROUP, (h + 1) * GROUP)
            bs_sum = jnp.zeros((CHUNK, GROUP), F32)
            ws_sum = jnp.zeros((CHUNK, CHUNK), F32)
            for ch in range(tt // CHUNK):
                rows = slice(ch * CHUNK, (ch + 1) * CHUNK)
                v_ch = vb[rows, cols]
                mixed = _dot(wc[h], v_ch) + bs_ref[:, cols]
                dy_ch = dya[rows, cols]
                dz_scr[rows, cols] = dy_ch * mixed
                dmixed = dy_ch * u[rows, cols]
                dmb = dmixed.astype(BF16)
                dz_scr[rows, D_A + h * GROUP:D_A + (h + 1) * GROUP] = _dot_tn(wc[h], dmb)
                bs_sum = bs_sum + dmixed
                ws_sum = ws_sum + _dot_nt(dmb, v_ch)
            _acc_rows(bs_acc, 0, h, bs_sum)
            _acc_rows(small_ref, ROW_WS, h, ws_sum)

        dvl = dz_scr[:, D_A:2 * D_A]
        dvhat = dvl * lng_ref[...]
        dvl_vhat = dvl * vhat
        dvr = rv * (dvhat - jnp.mean(dvhat, axis=-1, keepdims=True)
                    - vhat * jnp.mean(dvl_vhat * lng_ref[...], axis=-1, keepdims=True))
        small_ref[ROW_LN:ROW_LN + 1, 0:D_A] += _colsum(dvl_vhat)
        small_ref[ROW_LN:ROW_LN + 1, D_A:D] += _colsum(dvl)
        dga = jnp.concatenate([dz_scr[:, :D_A], dvr], axis=1)
        dza = dga * _gelu_grad(z[:, :2 * D_A], t_g)

        zb = z[:, 2 * D_A:]
        halo_prev = jnp.where(i == 0, 0.0, zprev_ref[...])
        diff = _pool_diff(zb, halo_prev, i * tt)
        dyb = dcat[:, D_A:]
        inv = _inv_counts(i * tt, tt)
        scaled, ddiffs = [], []
        for g in range(len(WINDOWS)):
            cols = slice(g * GROUP, (g + 1) * GROUP)
            db = diff[g].astype(BF16)
            wpg = wp_ref[g].astype(BF16)
            pre = _dot(db, wpg) + bp_ref[:, cols]
            small_ref[ROW_POOL:ROW_POOL + 1, cols] += _colsum(dyb[:, cols] * pre)
            dpre = dyb[:, cols] * ps_ref[:, cols]
            small_ref[ROW_POOL:ROW_POOL + 1, D_B + g * GROUP:D_B + (g + 1) * GROUP] += _colsum(dpre)
            dpb = dpre.astype(BF16)
            _acc_rows(small_ref, ROW_WP, g, _dot_tn(db, dpb))
            ddiff = _dot_nt(dpb, wpg)
            ddiffs.append(ddiff)
            scaled.append(ddiff * inv[g])
        scaled_all = jnp.concatenate(scaled, axis=1)
        ext = jnp.concatenate([scaled_all, carry[...]], axis=0)
        n_ext = tt + HALO
        s2 = ext + pltpu.roll(ext, n_ext - 1, 0)
        t4 = s2[:, GROUP:]
        s4 = t4 + pltpu.roll(t4, n_ext - 2, 0)
        t8 = s4[:, GROUP:]
        s8 = t8 + pltpu.roll(t8, n_ext - 4, 0)
        t16 = s8[:, GROUP:]
        s16 = t16 + pltpu.roll(t16, n_ext - 8, 0)
        back = [s2[:, :GROUP], s4[:, :GROUP], s8[:, :GROUP], s16]
        carry[...] = scaled_all[:HALO, :]
        dzb = jnp.concatenate([back[g][:tt, :] - ddiffs[g] for g in range(len(WINDOWS))], axis=1)

        dzv = jnp.concatenate([dza, dzb], axis=1).astype(BF16)
        dh1 = _dot(dzv, win_ref[...])
        xv = x_ref[...]
        r1 = _rstd(xv)
        xhat = xv * r1
        shift1, scale1 = mod_ref[0:1, :], mod_ref[1:2, :]
        pre1 = n1pre_ref[...]
        gain1 = pre1 * (1.0 + scale1)
        h1 = (xhat * gain1 + shift1).astype(BF16)
        acc_in[...] += _dot_tn(dzv, h1)
        d_x, sum_h = _rms_bwd_gained(dh1, gain1, xhat, r1)
        gx_ref[...] = dx1_ref[...] + d_x
        small_ref[ROW_DMOD:ROW_DMOD + 1, :] += _colsum(dh1)
        small_ref[ROW_DMOD + 1:ROW_DMOD + 2, :] += pre1 * sum_h
        small_ref[ROW_N1PRE:ROW_N1PRE + 1, :] += (1.0 + scale1) * sum_h

        @pl.when(s == nt - 1)
        def _():
            gwin_ref[...] = acc_in[...].astype(BF16)
            gwout_ref[...] = acc_out[...].astype(BF16)
            bs = _unfold(bs_acc[...])
            for h in range(N_HEADS):
                small_ref[ROW_BS + h:ROW_BS + h + 1, 0:GROUP] = jnp.sum(
                    bs[:, h * GROUP:(h + 1) * GROUP].T, axis=0, keepdims=True)
            for a in range(n_sums):
                for r in range(2):
                    chip_copy(a, r).wait_recv()
                    chip_copy(a, r).wait_send()

    rev = lambda w: pl.BlockSpec((tt, w), lambda s: (nt - 1 - s, 0))
    zprev = pl.BlockSpec((HALO, D_B), lambda s: (jnp.maximum((nt - 1 - s) * hb - 1, 0), 2))
    hbm = pl.BlockSpec(memory_space=pl.ANY)
    outs = pl.pallas_call(
        body, name="attn_bwd", grid=(nt,),
        out_shape=tuple([jax.ShapeDtypeStruct((t_len, D), F32), jax.ShapeDtypeStruct((D_Z, D), BF16),
                         jax.ShapeDtypeStruct((D, D), BF16), jax.ShapeDtypeStruct((SMALL_ROWS, D), F32)]
                        + [jax.ShapeDtypeStruct(cs.shape, cs.dtype) for cs in chip_sums]),
        in_specs=[rev(D), rev(D), rev(D), rev(D_Z), zprev, rev(D), _full((8, D)), _full((1, D)),
                  _resident((D_Z, D)), _resident((D, D)), _full((N_HEADS, CHUNK, CHUNK)), _full((CHUNK, D_A)),
                  _full((1, D_A)), _full((1, D_A)), _full((len(WINDOWS), GROUP, GROUP)), _full((1, D_B)),
                  _full((1, D_B)), _full((8, D)), _full((8, D))] + [_resident(cs.shape) for cs in chip_sums],
        out_specs=tuple([rev(D), _resident((D_Z, D)), _resident((D, D)), _full((SMALL_ROWS, D))] + [hbm] * n_sums),
        scratch_shapes=[pltpu.VMEM((HALO, D_B), F32), pltpu.VMEM((D_Z, D), F32), pltpu.VMEM((D, D), F32),
                        pltpu.VMEM((tt, 2 * D_A), F32), pltpu.VMEM((CHUNK // 2, D), F32),
                        pltpu.SemaphoreType.DMA((2 * n_sums,)), pltpu.SemaphoreType.DMA((2 * n_sums,))],
        compiler_params=pltpu.CompilerParams(dimension_semantics=("arbitrary",), vmem_limit_bytes=VMEM_LIMIT),
    )(dmix, dx1, x, z, z, cat, mod, n1pre, w_in_t, w_out, w_sp, bs_rows, ln_g, ln_b, w_pool, b_pool, pool_scale,
      red_fwd, red_bwd, *chip_sums)
    return outs[:4], outs[4:]


def _adam(w, g, m, v):
    m2 = ADAM_B1 * m + (1.0 - ADAM_B1) * g
    v2 = ADAM_B2 * v + (1.0 - ADAM_B2) * (g * g)
    m_hat = m2 / (1.0 - ADAM_B1 ** ADAM_STEP)
    v_hat = v2 / (1.0 - ADAM_B2 ** ADAM_STEP)
    delta = -ADAM_LR * (m_hat / (jnp.sqrt(v_hat) + ADAM_EPS) + ADAM_WD * w)
    return delta, m2, v2


def _adamw_shard(name, rb, w, g, m, v):
    rows, cols = w.shape

    def body(w_ref, g_ref, m_ref, v_ref, d_ref, m2_ref, v2_ref):
        d_ref[...], m2_ref[...], v2_ref[...] = _adam(w_ref[...], g_ref[...], m_ref[...], v_ref[...])

    blk = pl.BlockSpec((rb, cols), lambda i: (i, 0))
    shp = jax.ShapeDtypeStruct((rows, cols), F32)
    return pl.pallas_call(
        body, name=name, grid=(rows // rb,), out_shape=(shp, shp, shp),
        in_specs=[blk] * 4, out_specs=(blk, blk, blk),
        compiler_params=pltpu.CompilerParams(dimension_semantics=("arbitrary",)),
    )(w, g, m, v)


def _adamw_fc(steps, fc):
    n_fc = len(fc)

    def body(*refs):
        ins, outs = refs[:6 * n_fc], refs[6 * n_fc:]
        for k in range(n_fc):
            w_ref, own_ref, arr_ref, diag_ref, m_ref, v_ref = ins[6 * k:6 * k + 6]
            g = ((own_ref[...] + arr_ref[0].astype(F32)) + arr_ref[1].astype(F32)) + diag_ref[...].astype(F32)
            outs[4 * k][...] = g
            outs[4 * k + 1][...], outs[4 * k + 2][...], outs[4 * k + 3][...] = _adam(
                w_ref[...], g, m_ref[...], v_ref[...])

    specs_in, specs_out, shapes, args = [], [], [], []
    for w, own, arrived, diagonal, m, v in fc:
        rows, cols = w.shape
        blk = pl.BlockSpec((rows // steps, cols), lambda i: (i, 0))
        specs_in += [blk, blk, pl.BlockSpec((2, rows // steps, cols), lambda i: (0, i, 0)), blk, blk, blk]
        specs_out += [blk] * 4
        shapes += [jax.ShapeDtypeStruct((rows, cols), F32)] * 4
        args += [w, own, arrived, diagonal, m, v]
    outs = pl.pallas_call(
        body, name="adamw_fc", grid=(steps,), out_shape=tuple(shapes), in_specs=specs_in, out_specs=tuple(specs_out),
        compiler_params=pltpu.CompilerParams(dimension_semantics=("arbitrary",)),
    )(*args)
    return [outs[4 * k:4 * k + 4] for k in range(n_fc)]


def _adamw_ada(rb, w, sc, dmod_cols, m, v):
    rows, cols = w.shape

    def body(w_ref, sc_ref, dm_ref, m_ref, v_ref, g_ref, d_ref, m2_ref, v2_ref):
        g = _dot_tn(sc_ref[...].astype(BF16), dm_ref[...].astype(BF16))
        g_ref[...] = g
        d_ref[...], m2_ref[...], v2_ref[...] = _adam(w_ref[...], g, m_ref[...], v_ref[...])

    blk = pl.BlockSpec((rb, cols), lambda i: (i, 0))
    shp = jax.ShapeDtypeStruct((rows, cols), F32)
    return pl.pallas_call(
        body, name="adamw_ada", grid=(rows // rb,), out_shape=(shp, shp, shp, shp),
        in_specs=[blk, pl.BlockSpec((N_DEV, rb), lambda i: (0, i)), _full((N_DEV, cols)), blk, blk],
        out_specs=(blk, blk, blk, blk),
        compiler_params=pltpu.CompilerParams(dimension_semantics=("arbitrary",)),
    )(w, sc, dmod_cols, m, v)


def _unfold(acc_rows):
    return jnp.concatenate([acc_rows[:, :D_A], acc_rows[:, D_A:]], axis=0)


def _adamw_small(total, params):
    n = len(params)
    flat = [a for p in params for a in p]

    def body(*refs):
        s_ref = refs[0]
        p_refs = refs[1:1 + 3 * n]
        loss_ref = refs[1 + 3 * n]
        o_refs = refs[2 + 3 * n:]
        d_b_ada = s_ref[0:6, :]
        for b in range(1, N_DEV):
            d_b_ada = d_b_ada + s_ref[_table_row(b):_table_row(b) + 6, :]
        misc = lambda r: s_ref[PK_MISC + r - ROW_N1PRE:PK_MISC + r - ROW_N1PRE + 1, :]
        loss = jnp.sum(misc(ROW_LOSS), axis=-1, keepdims=True) * (0.5 / D)
        loss_ref[...] = loss
        mask = _tril_mask()
        ws = _unfold(s_ref[PK_WS:PK_WS + 64, :])
        wp = _unfold(s_ref[PK_WP:PK_WP + 64, :])
        grads = [
            d_b_ada,
            misc(ROW_N1PRE), misc(ROW_N1POST), misc(ROW_N2PRE), misc(ROW_N2POST),
            misc(ROW_LN)[:, :D_A], misc(ROW_LN)[:, D_A:],
            misc(ROW_POOL)[:, :D_B], misc(ROW_POOL)[:, D_B:],
            s_ref[PK_BS:PK_BS + N_HEADS, 0:GROUP],
            jnp.stack([ws[:, h * GROUP:(h + 1) * GROUP] * mask for h in range(N_HEADS)]),
            jnp.stack([wp[:, g * GROUP:(g + 1) * GROUP] for g in range(len(WINDOWS))]),
        ]
        for k in range(n):
            w_ref, m_ref, v_ref = p_refs[3 * k:3 * k + 3]
            g = grads[k]
            if k == 0:
                for j in range(6):
                    o_refs[0][j] = g[j:j + 1, :]
                    o_refs[1][j], o_refs[2][j], o_refs[3][j] = _adam(w_ref[j], g[j:j + 1, :], m_ref[j], v_ref[j])
                continue
            o_refs[4 * k][...] = g
            o_refs[4 * k + 1][...], o_refs[4 * k + 2][...], o_refs[4 * k + 3][...] = _adam(
                w_ref[...], g, m_ref[...], v_ref[...])

    vm = pl.BlockSpec(memory_space=pltpu.VMEM)
    out_shape = [jax.ShapeDtypeStruct((1, 1), F32)]
    for w, _, _ in params:
        out_shape += [jax.ShapeDtypeStruct(w.shape, F32)] * 4
    return pl.pallas_call(
        body, name="adamw_small", out_shape=tuple(out_shape),
        in_specs=[vm] * (1 + 3 * n), out_specs=tuple([vm] * len(out_shape)),
    )(total, *flat)


TT_ATTN_FWD = 512
ATTN_LAG = 2
TT_MLP_FWD = 512
TT_MLP = 256
TT_WGRAD = 2048
TT_ATTN_BWD = 512


def kernel(x, c, w_ada, b_ada, norm1_pre, norm1_post, w_in, w_spatial, b_spatial, ln_v_gain, ln_v_bias, w_pool, b_pool, pool_scale, w_out, norm2_pre, norm2_post, w_fc1, w_fc2, loss_target, m_w_ada, m_b_ada, m_norm1_pre, m_norm1_post, m_w_in, m_w_spatial, m_b_spatial, m_ln_v_gain, m_ln_v_bias, m_w_pool, m_b_pool, m_pool_scale, m_w_out, m_norm2_pre, m_norm2_post, m_w_fc1, m_w_fc2, v_w_ada, v_b_ada, v_norm1_pre, v_norm1_post, v_w_in, v_w_spatial, v_b_spatial, v_ln_v_gain, v_ln_v_bias, v_w_pool, v_b_pool, v_pool_scale, v_w_out, v_norm2_pre, v_norm2_post, v_w_fc1, v_w_fc2):
    t_len = x.shape[1]
    me = 4 * lax.axis_index("x") + 2 * lax.axis_index("y") + lax.axis_index("c")
    ada_cols = w_ada.shape[1]
    tt = lambda want: min(want, t_len)

    x2 = x.reshape(t_len, D)
    tgt = loss_target.reshape(t_len, D)
    row = lambda a: a.reshape(1, -1)

    b_my = lax.dynamic_slice_in_dim(b_ada, me * ada_cols, ada_cols).reshape(1, ada_cols)
    w_in_shard, w_out_shard, w1_shard, w2_shard = _cast_shards([w_in.T, w_out, w_fc1, w_fc2])

    bs_rows = jnp.repeat(b_spatial.T, GROUP, axis=1)
    attn_consts = (w_spatial, bs_rows, row(ln_v_gain), row(ln_v_bias), w_pool, row(b_pool), row(pool_scale))

    (z, cat, mix, x1, h2, r_begun, f_head, mod, sc), (w1_early, w2_early), w_out_all, w_in_t = _attn_fwd(
        tt(TT_ATTN_FWD), x2, jnp.broadcast_to(c, (8, D)), w_ada, b_my, row(norm1_pre), row(norm1_post),
        w_in_shard, w_out_shard, *attn_consts, (w1_shard, w2_shard), row(norm2_pre))
    (r_early, f_early), (w1_late, w2_late) = _mlp_fwd_early(
        tt(TT_MLP_FWD), r_begun, h2, f_head, w1_early, w2_early)
    r_late, df, da, dmix, dx1, red_fwd, red_bwd = _mlp_late_bwd(
        tt(TT_MLP), r_early, x1, h2, f_early, tgt, mix, mod, row(norm2_pre), row(norm2_post), row(norm1_post),
        w1_early, w2_early, w1_late, w2_late)
    own_w1, own_w2, sums_w1, sums_w2, diag_w1, diag_w2 = _mlp_wgrad(tt(TT_WGRAD), r_early, r_late, da, df, h2)
    (grad_x, p_in, p_out, small), (arr_w1, arr_w2) = _attn_bwd(
        tt(TT_ATTN_BWD), dmix, dx1, x2, z, cat, mod, row(norm1_pre), w_in_t, w_out_all, *attn_consts,
        red_fwd, red_bwd, [sums_w1, sums_w2])
    (grad_w1, d_w1, m_w1, v_w1), (grad_w2, d_w2, m_w2, v_w2) = _adamw_fc(
        8, [(w_fc1, own_w1, arr_w1, diag_w1, m_w_fc1, v_w_fc1), (w_fc2, own_w2, arr_w2, diag_w2, m_w_fc2, v_w_fc2)])
    grad_in_t, grad_out, total = _tail_comm(
        [p_in.reshape(N_DEV, D_Z // N_DEV, D), p_out.reshape(N_DEV, D // N_DEV, D)], small, 64)

    d_out, m_out, v_out = _adamw_shard("adamw_out", 128, w_out, grad_out, m_w_out, v_w_out)
    d_in_t, m_in_t, v_in_t = _adamw_shard("adamw_in", D_Z // N_DEV, w_in.T, grad_in_t, m_w_in.T, v_w_in.T)
    table = jnp.concatenate([total[0:PACK_FINE, :], total[PK_TABLE_B:PK_MISC, :]], axis=0)
    dmod_all = table.reshape(N_DEV, 8, D)[:, :6, :].reshape(N_DEV, 6 * D)
    dmod_cols = lax.dynamic_slice_in_dim(dmod_all, me * ada_cols, ada_cols, axis=1)
    grad_ada, d_ada, m_ada, v_ada = _adamw_ada(256, w_ada, sc, dmod_cols, m_w_ada, v_w_ada)

    six = lambda a: a.reshape(6, 1, D)
    small_params = [
        (six(b_ada), six(m_b_ada), six(v_b_ada)),
        (row(norm1_pre), row(m_norm1_pre), row(v_norm1_pre)),
        (row(norm1_post), row(m_norm1_post), row(v_norm1_post)),
        (row(norm2_pre), row(m_norm2_pre), row(v_norm2_pre)),
        (row(norm2_post), row(m_norm2_post), row(v_norm2_post)),
        (row(ln_v_gain), row(m_ln_v_gain), row(v_ln_v_gain)),
        (row(ln_v_bias), row(m_ln_v_bias), row(v_ln_v_bias)),
        (row(pool_scale), row(m_pool_scale), row(v_pool_scale)),
        (row(b_pool), row(m_b_pool), row(v_b_pool)),
        (b_spatial, m_b_spatial, v_b_spatial),
        (w_spatial, m_w_spatial, v_w_spatial),
        (w_pool, m_w_pool, v_w_pool),
    ]
    outs = _adamw_small(total, small_params)
    loss = outs[0].reshape(())
    names = ["b_ada", "norm1_pre", "norm1_post", "norm2_pre", "norm2_post", "ln_v_gain", "ln_v_bias", "pool_scale",
             "b_pool", "b_spatial", "w_spatial", "w_pool"]
    shapes = dict(b_ada=b_ada.shape, norm1_pre=norm1_pre.shape, norm1_post=norm1_post.shape,
                  norm2_pre=norm2_pre.shape, norm2_post=norm2_post.shape, ln_v_gain=ln_v_gain.shape,
                  ln_v_bias=ln_v_bias.shape, pool_scale=pool_scale.shape, b_pool=b_pool.shape,
                  b_spatial=b_spatial.shape, w_spatial=w_spatial.shape, w_pool=w_pool.shape)
    res = {}
    for k, nm in enumerate(names):
        res[nm] = tuple(o.reshape(shapes[nm]) for o in outs[1 + 4 * k:5 + 4 * k])
    res["w_ada"] = (grad_ada, d_ada, m_ada, v_ada)
    res["w_in"] = (grad_in_t.T, d_in_t.T, m_in_t.T, v_in_t.T)
    res["w_out"] = (grad_out, d_out, m_out, v_out)
    res["w_fc1"] = (grad_w1, d_w1, m_w1, v_w1)
    res["w_fc2"] = (grad_w2, d_w2, m_w2, v_w2)

    order = ["w_ada", "b_ada", "norm1_pre", "norm1_post", "w_in", "w_spatial", "b_spatial", "ln_v_gain", "ln_v_bias",
             "w_pool", "b_pool", "pool_scale", "w_out", "norm2_pre", "norm2_post", "w_fc1", "w_fc2"]
    return (loss, grad_x.reshape(x.shape),
            *[res[nm][0] for nm in order], *[res[nm][1] for nm in order],
            *[res[nm][2] for nm in order], *[res[nm][3] for nm in order])
```

```python
import functools

import jax
import jax.numpy as jnp
from jax import lax
from jax.experimental import pallas as pl
from jax.experimental.pallas import tpu as pltpu

F32 = jnp.float32
BF16 = jnp.bfloat16
MESH = pl.DeviceIdType.MESH

N_DEV = 8
D = 1024
D_A = 512
D_B = 512
D_Z = 2 * D_A + D_B
N_HEADS = 4
CHUNK = 128
WINDOWS = (2, 4, 8, 16)
GROUP = 128
D_FF = 4096
FF_BLK = D_FF // N_DEV
HALO = 16
EPS = 1e-6
VMEM_LIMIT = 60 * 1024 * 1024

ADAM_LR = 0.001
ADAM_B1 = 0.9
ADAM_B2 = 0.999
ADAM_EPS = 1e-08
ADAM_WD = 0.01
ADAM_STEP = 10

ROW_DMOD = 0
ROW_N1PRE, ROW_N1POST, ROW_N2PRE, ROW_N2POST = 8, 9, 10, 11
ROW_LN = 12
ROW_POOL = 13
ROW_LOSS = 14
ROW_BS = 16
ROW_WS = 24
ROW_WP = 88
SMALL_ROWS = 152
PACK_FINE = 40
PACK_HALF = PACK_FINE + 64
PACK_ROWS = 2 * PACK_HALF
PK_WS = PACK_FINE
PK_TABLE_B = PACK_HALF
PK_MISC = PK_TABLE_B + 24
PK_BS = PK_MISC + 8
PK_WP = PK_BS + 8


def _table_row(b):
    if isinstance(b, int):
        return 8 * b if 8 * b < PACK_FINE else 8 * b + PK_TABLE_B - PACK_FINE
    return 8 * b + jnp.where(8 * b < PACK_FINE, 0, PK_TABLE_B - PACK_FINE)


def _dot(a, b):
    return jnp.dot(a, b, preferred_element_type=F32)


def _dot_nt(a, b):
    return lax.dot_general(a, b, (((1,), (1,)), ((), ())), preferred_element_type=F32)


def _dot_tn(a, b):
    return lax.dot_general(a, b, (((0,), (0,)), ((), ())), preferred_element_type=F32)


def _rstd(v):
    return lax.rsqrt(jnp.mean(v * v, axis=-1, keepdims=True) + EPS)


def _rms_bwd(d_hat, hat, rstd):
    return rstd * (d_hat - hat * jnp.mean(d_hat * hat, axis=-1, keepdims=True))


def _rms_bwd_gained(g, gain, hat, rstd):
    g_hat = g * hat
    d_v = rstd * (g * gain - hat * jnp.mean(g_hat * gain, axis=-1, keepdims=True))
    return d_v, _colsum(g_hat)


_K0 = 0.7978845608028654
_K1 = 0.044715


def _gelu_parts(v):
    t = jnp.tanh(v * (_K0 + (_K0 * _K1) * (v * v)))
    return t, v * (0.5 + 0.5 * t)


def _gelu_grad(v, t):
    return (0.5 + 0.5 * t) + (0.5 * v) * (1.0 - t * t) * (_K0 + (3.0 * _K0 * _K1) * (v * v))


def _colsum(v):
    return jnp.sum(v, axis=0, keepdims=True)


def _full(shape):
    n = len(shape)
    return pl.BlockSpec(shape, lambda *_: (0,) * n)


def _resident(shape):
    n = len(shape)
    return pl.BlockSpec(shape, lambda *_: (0,) * n, pipeline_mode=pl.Buffered(1))


def _place():
    x, y, c = lax.axis_index("x"), lax.axis_index("y"), lax.axis_index("c")
    return x, y, c


def _flip(v, bit):
    return 1 - v if bit else v


def _peer(x, y, c, k):
    return (_flip(x, (k >> 2) & 1), _flip(y, (k >> 1) & 1), _flip(c, k & 1))


def _index(p):
    return 4 * p[0] + 2 * p[1] + p[2]


def _cast_shards(shards):
    def body(*refs):
        for src, dst in zip(refs[:len(shards)], refs[len(shards):]):
            dst[...] = src[...].astype(BF16)

    vm = pl.BlockSpec(memory_space=pltpu.VMEM)
    return pl.pallas_call(
        body, name="cast_shards", out_shape=tuple(jax.ShapeDtypeStruct(s.shape, BF16) for s in shards),
        in_specs=[vm] * len(shards), out_specs=tuple([vm] * len(shards)),
    )(*shards)


FC_EARLY = 6
FC_HEAD = 2
R_HEAD_COLS = (FC_EARLY - FC_HEAD) * FF_BLK
WGRAD_ORDER = (7, 6, 1, 3, 5, 2, 4, 0)


def _early_col(j):
    return R_HEAD_COLS + j * FF_BLK if j < FC_HEAD else (j - FC_HEAD) * FF_BLK


class _Copies:
    def __init__(self, entries, send_sems, recv_sems):
        self.place = _place()
        self.entries, self.send_sems, self.recv_sems = entries, send_sems, recv_sems

    def _copy(self, i, arrival=False):
        src, dst, rel = self.entries[i]
        return pltpu.make_async_remote_copy(
            src_ref=dst if arrival else src, dst_ref=dst, send_sem=self.send_sems.at[i],
            recv_sem=self.recv_sems.at[i], device_id=_peer(*self.place, rel), device_id_type=MESH)

    def start(self, *which):
        for i in which:
            self._copy(i).start()

    def wait_recv(self, *which):
        for i in which:
            self._copy(i, arrival=True).wait_recv()

    def wait_send(self, *which):
        for i in which:
            self._copy(i).wait_send()


TAIL_STEPS = 3


def _tail_comm(parts, small, head, row_chunk):
    n = len(parts)

    def body(*refs):
        p_refs, small_ref, head_ref = refs[:n], refs[n], refs[n + 1]
        outs = refs[n + 2:]
        g_refs, total_ref = outs[:n], outs[n]
        scr = outs[n + 1:]
        from_sib = scr[0:n]
        chip_out = scr[n:2 * n]
        chip_in = scr[2 * n:3 * n]
        pack, pack_sib, fine, bulk, total_scr = scr[3 * n:3 * n + 5]
        send_a, recv_a, send_b, recv_b, send_s, recv_s = scr[3 * n + 5:]
        step = pl.program_id(0)
        x, y, c = _place()
        me = _index((x, y, c))
        sibling = (x, y, 1 - c)
        my_chip = 2 * x + y
        others = [(1 - x, y), (x, 1 - y), (1 - x, 1 - y)]
        my_half = pl.ds(pl.multiple_of(PACK_HALF * c, 8), PACK_HALF)

        def pack_to_sibling():
            return pltpu.make_async_remote_copy(
                src_ref=pack, dst_ref=pack_sib, send_sem=send_s.at[0], recv_sem=recv_s.at[0],
                device_id=sibling, device_id_type=MESH)

        def half_to_chip(r, part):
            buf = (fine, bulk)[part]
            return pltpu.make_async_remote_copy(
                src_ref=buf.at[my_chip], dst_ref=buf.at[my_chip],
                send_sem=send_s.at[1 + 3 * part + r], recv_sem=recv_s.at[1 + 3 * part + r],
                device_id=(*others[r], c), device_id_type=MESH)

        def half_from_chip(r, part):
            k = 2 * others[r][0] + others[r][1]
            buf = (fine, bulk)[part]
            return pltpu.make_async_remote_copy(
                src_ref=buf.at[k], dst_ref=buf.at[k],
                send_sem=send_s.at[1 + 3 * part + r], recv_sem=recv_s.at[1 + 3 * part + r],
                device_id=(*others[r], c), device_id_type=MESH)

        def total_to_sibling():
            return pltpu.make_async_remote_copy(
                src_ref=total_scr.at[my_half], dst_ref=total_scr.at[my_half],
                send_sem=send_s.at[7], recv_sem=recv_s.at[7], device_id=sibling, device_id_type=MESH)

        def total_from_sibling():
            sib_half = pl.ds(pl.multiple_of(PACK_HALF * (1 - c), 8), PACK_HALF)
            return pltpu.make_async_remote_copy(
                src_ref=total_scr.at[sib_half], dst_ref=total_scr.at[sib_half],
                send_sem=send_s.at[7], recv_sem=recv_s.at[7], device_id=sibling, device_id_type=MESH)

        def to_sibling(a, k):
            return pltpu.make_async_remote_copy(
                src_ref=p_refs[a].at[2 * k + (1 - c)], dst_ref=from_sib[a].at[k],
                send_sem=send_a.at[a], recv_sem=recv_a.at[a], device_id=sibling, device_id_type=MESH)

        def all_from_sibling(a):
            return pltpu.make_async_remote_copy(
                src_ref=from_sib[a], dst_ref=from_sib[a], send_sem=send_a.at[a], recv_sem=recv_a.at[a],
                device_id=sibling, device_id_type=MESH)

        def to_chip(a, r):
            return pltpu.make_async_remote_copy(
                src_ref=chip_out[a].at[r], dst_ref=chip_in[a].at[r],
                send_sem=send_b.at[3 * a + r], recv_sem=recv_b.at[3 * a + r],
                device_id=(*others[r], c), device_id_type=MESH)

        @pl.when(step == 0)
        def _():
            pack[0:PACK_FINE, :] = jnp.zeros((PACK_FINE, D), F32)
            pack[PK_TABLE_B:PK_MISC, :] = jnp.zeros((PK_MISC - PK_TABLE_B, D), F32)
            pack[pl.ds(pl.multiple_of(_table_row(me), 8), 8), :] = small_ref[0:8, :] + head_ref[0:8, :]
            pack[PK_WS:PK_WS + 64, :] = small_ref[ROW_WS:ROW_WS + 64, :]
            pack[PK_MISC:PK_MISC + 8, :] = small_ref[ROW_N1PRE:ROW_N1PRE + 8, :] + head_ref[8:16, :]
            pack[PK_BS:PK_BS + 8, :] = small_ref[ROW_BS:ROW_BS + 8, :]
            pack[PK_WP:PK_WP + 64, :] = small_ref[ROW_WP:ROW_WP + 64, :]
            pack_to_sibling().start()
            for a in range(n):
                for k in range(4):
                    to_sibling(a, k).start()

        @pl.when(step == 1)
        def _():
            pack_to_sibling().wait_recv()
            chip_sum = pack[my_half, :] + pack_sib[my_half, :]
            fine[my_chip] = chip_sum[:PACK_FINE, :]
            bulk[my_chip] = chip_sum[PACK_FINE:, :].astype(BF16)
            for r in range(3):
                half_to_chip(r, 0).start()
                half_to_chip(r, 1).start()
            for a in range(n):
                all_from_sibling(a).wait_recv()
                rows = p_refs[a].shape[1]
                for r in range(3):
                    k = 2 * others[r][0] + others[r][1]
                    for s in range(0, rows, row_chunk):
                        sl = pl.ds(s, row_chunk)
                        chip_out[a][r, sl, :] = (p_refs[a][2 * k + c, sl, :].astype(F32)
                                                 + from_sib[a][k, sl, :].astype(F32)).astype(BF16)
                    to_chip(a, r).start()
                for s in range(0, rows, row_chunk):
                    sl = pl.ds(s, row_chunk)
                    g_refs[a][sl, :] = (p_refs[a][2 * my_chip + c, sl, :].astype(F32)
                                        + from_sib[a][my_chip, sl, :].astype(F32))

        @pl.when(step == TAIL_STEPS - 1)
        def _():
            for r in range(3):
                half_from_chip(r, 0).wait_recv()
                half_from_chip(r, 1).wait_recv()
            half_start = pl.multiple_of(PACK_HALF * c, 8)
            total_scr[pl.ds(half_start, PACK_FINE), :] = ((fine[0] + fine[1]) + fine[2]) + fine[3]
            total_scr[pl.ds(half_start + PACK_FINE, PACK_HALF - PACK_FINE), :] = (
                (bulk[0].astype(F32) + bulk[1].astype(F32)) + bulk[2].astype(F32)) + bulk[3].astype(F32)
            total_to_sibling().start()
            for a in range(n):
                rows = p_refs[a].shape[1]
                for r in range(3):
                    to_chip(a, r).wait_recv()
                    for s in range(0, rows, row_chunk):
                        sl = pl.ds(s, row_chunk)
                        g_refs[a][sl, :] = g_refs[a][sl, :] + chip_in[a][r, sl, :].astype(F32)
            total_from_sibling().wait_recv()
            total_ref[...] = total_scr[...]
            for a in range(n):
                all_from_sibling(a).wait_send()
                for r in range(3):
                    to_chip(a, r).wait_send()
            pack_to_sibling().wait_send()
            for r in range(3):
                half_to_chip(r, 0).wait_send()
                half_to_chip(r, 1).wait_send()
            total_to_sibling().wait_send()

    return pl.pallas_call(
        body, name="tail_comm", grid=(TAIL_STEPS,),
        out_shape=tuple([jax.ShapeDtypeStruct(p.shape[1:], F32) for p in parts]
                        + [jax.ShapeDtypeStruct((PACK_ROWS, D), F32)]),
        in_specs=[_resident(p.shape) for p in parts] + [_resident(small.shape), _resident(head.shape)],
        out_specs=tuple([_full(p.shape[1:]) for p in parts] + [_full((PACK_ROWS, D))]),
        scratch_shapes=(
            [pltpu.VMEM((4,) + p.shape[1:], BF16) for p in parts]
            + [pltpu.VMEM((3,) + p.shape[1:], BF16) for p in parts]
            + [pltpu.VMEM((3,) + p.shape[1:], BF16) for p in parts]
            + [pltpu.VMEM((PACK_ROWS, D), F32), pltpu.VMEM((PACK_ROWS, D), F32),
               pltpu.VMEM((4, PACK_FINE, D), F32), pltpu.VMEM((4, PACK_HALF - PACK_FINE, D), BF16),
               pltpu.VMEM((PACK_ROWS, D), F32)]
            + [pltpu.SemaphoreType.DMA((n,)), pltpu.SemaphoreType.DMA((n,)),
               pltpu.SemaphoreType.DMA((3 * n,)), pltpu.SemaphoreType.DMA((3 * n,)),
               pltpu.SemaphoreType.DMA((8,)), pltpu.SemaphoreType.DMA((8,))]),
        compiler_params=pltpu.CompilerParams(dimension_semantics=("arbitrary",), vmem_limit_bytes=VMEM_LIMIT),
    )(*parts, small, head)


def _tril_mask():
    row = lax.broadcasted_iota(jnp.int32, (CHUNK, CHUNK), 0)
    col = lax.broadcasted_iota(jnp.int32, (CHUNK, CHUNK), 1)
    return (col <= row).astype(F32)


def _window_sums(ext):
    s2 = ext + pltpu.roll(ext, 1, 0)
    t4 = s2[:, GROUP:]
    s4 = t4 + pltpu.roll(t4, 2, 0)
    t8 = s4[:, GROUP:]
    s8 = t8 + pltpu.roll(t8, 4, 0)
    t16 = s8[:, GROUP:]
    s16 = t16 + pltpu.roll(t16, 8, 0)
    return [s2[:, :GROUP], s4[:, :GROUP], s8[:, :GROUP], s16]


def _inv_counts(first_pos, rows):
    pos = first_pos + lax.broadcasted_iota(jnp.int32, (rows, 1), 0)
    return [1.0 / jnp.minimum(pos + 1, w).astype(F32) for w in WINDOWS]


def _pool_diff(zb, halo, first_pos):
    tt = zb.shape[0]
    sums = _window_sums(jnp.concatenate([halo, zb], axis=0))
    inv = _inv_counts(first_pos, tt)
    return [sums[g][HALO:, :] * inv[g] - zb[:, g * GROUP:(g + 1) * GROUP] for g in range(len(WINDOWS))]


def _row_blocks(scr, rows, place):
    def block(rel):
        start = pl.multiple_of(rows * _index(_peer(*place, rel)), rows)
        return scr.at[pl.ds(start, rows), :]

    def entries(shard_ref):
        return ([(shard_ref, block(0), rel) for rel in (1, 2, 4, 6)]
                + [(block(rel), block(rel), 1) for rel in (2, 4, 6)])
    return block, entries


def _attn_fwd(tt, x, c8, w_ada, b_my, n1pre, n1post, w_in_shard, w_out_shard, w_sp, bs_rows, ln_g, ln_b, w_pool,
              b_pool, pool_scale, fc_shards, n2pre):
    t_len = x.shape[0]
    nt = t_len // tt
    ncol = w_ada.shape[1]

    def body(x_ref, xb_ref, c_ref, wada_ref, b_ref, n1pre_ref, n1post_ref, wi_ref, wo_ref, wsp_ref, bs_ref,
             lng_ref, lnb_ref, wp_ref, bp_ref, ps_ref, w1_ref, w2_ref, n2pre_ref,
             z_ref, cat_ref, mix_ref, x1_ref, h2_ref, r_ref, f_ref, mod_out, sc_out, e1_ref, e2_ref, wout_ref,
             win_out, carry, land1, land2, sib1, sib2, cat_keep, wout_scr, win_ref, mod_ref, cg, mg, part,
             send_sems, recv_sems, local_sems, wo_send, wo_recv, wi_send, wi_recv, ada_send, ada_recv):
        i = pl.program_id(0)
        place = px, py, pc = _place()
        me = _index(place)
        wo_block, wo_entries = _row_blocks(wout_scr, w_out_shard.shape[0], place)
        wi_block, wi_entries = _row_blocks(win_ref, w_in_shard.shape[0], place)
        wo_copies = _Copies(wo_entries(wo_ref), wo_send, wo_recv)
        wi_copies = _Copies(wi_entries(wi_ref), wi_send, wi_recv)
        wo_keep = pltpu.make_async_copy(wout_scr, wout_ref, local_sems.at[8])
        wi_keep = pltpu.make_async_copy(win_ref, win_out, local_sems.at[9])
        ada = _Copies([(c_ref, cg.at[me], k) for k in range(1, N_DEV)]
                      + [(part, mg.at[me], k) for k in range(1, N_DEV)], ada_send, ada_recv)
        copies = _Copies(
            [(w1_ref, sib1, 1), (w2_ref, sib2, 1),
             (w1_ref, land1.at[0], 2), (w2_ref, land2.at[0], 2),
             (w1_ref, land1.at[1], 4), (w2_ref, land2.at[1], 4),
             (land1.at[0], e1_ref.at[3], 1), (land2.at[0], e2_ref.at[3], 1),
             (land1.at[1], e1_ref.at[5], 1), (land2.at[1], e2_ref.at[5], 1)],
            send_sems, recv_sems)
        keep = [pltpu.make_async_copy(w1_ref, e1_ref.at[0], local_sems.at[0]),
                pltpu.make_async_copy(w2_ref, e2_ref.at[0], local_sems.at[1]),
                pltpu.make_async_copy(land1.at[0], e1_ref.at[2], local_sems.at[2]),
                pltpu.make_async_copy(land1.at[1], e1_ref.at[4], local_sems.at[3]),
                pltpu.make_async_copy(land2.at[0], e2_ref.at[2], local_sems.at[4]),
                pltpu.make_async_copy(land2.at[1], e2_ref.at[4], local_sems.at[5]),
                pltpu.make_async_copy(sib1, e1_ref.at[1], local_sems.at[6]),
                pltpu.make_async_copy(sib2, e2_ref.at[1], local_sems.at[7])]

        @pl.when(i == 0)
        def _():
            ada.start(*range(N_DEV - 1))
            wi_copies.start(0, 1, 2, 3)
            cg[me] = c_ref[...]
            wi_rows, wo_rows = w_in_shard.shape[0], w_out_shard.shape[0]
            win_ref[pl.ds(pl.multiple_of(wi_rows * me, wi_rows), wi_rows), :] = wi_ref[...]
            wout_scr[pl.ds(pl.multiple_of(wo_rows * me, wo_rows), wo_rows), :] = wo_ref[...]
            carry[...] = jnp.zeros_like(carry)

            ada.wait_recv(*range(N_DEV - 1))
            c_all = jnp.concatenate([cg[j, 0:1, :] for j in range(N_DEV)], axis=0)
            sc = c_all * jax.nn.sigmoid(c_all)
            sc_out[...] = sc
            part[...] = _dot(sc.astype(BF16), wada_ref[...].astype(BF16)) + b_ref[...]
            ada.start(*range(N_DEV - 1, 2 * (N_DEV - 1)))
            wo_copies.start(0, 1, 2, 3)
            copies.start(0, 1, 2, 4, 3, 5)
            keep[0].start()
            keep[1].start()
            mg[me] = part[...]
            ada.wait_recv(*range(N_DEV - 1, 2 * (N_DEV - 1)))
            mod_ref[...] = jnp.zeros_like(mod_ref)
            for j in range(N_DEV):
                for m in range(6):
                    lo, hi = max(ncol * j, D * m), min(ncol * (j + 1), D * (m + 1))
                    if lo < hi:
                        mod_ref[m:m + 1, lo - D * m:hi - D * m] = mg[j, pl.ds(me, 1), lo - ncol * j:hi - ncol * j]
            mod_out[...] = mod_ref[...]

            wi_copies.wait_recv(1, 2, 3)
            wi_copies.start(4, 5, 6)
            wi_copies.wait_recv(0, 4, 5, 6)
            wi_keep.start()

        @pl.when(i == nt // 2 + ATTN_LAG)
        def _():
            copies.wait_recv(2, 4)
            copies.start(6, 8)
            keep[2].start()
            keep[3].start()

        @pl.when(i == nt - 1 + ATTN_LAG)
        def _():
            copies.wait_recv(3, 5)
            copies.start(7, 9)
            keep[4].start()
            keep[5].start()

        shift1, scale1, gate1 = mod_ref[0:1, :], mod_ref[1:2, :], mod_ref[2:3, :]

        @pl.when(i < nt)
        def _():
            xv = x_ref[...]
            h1 = (xv * _rstd(xv)) * (n1pre_ref[...] * (1.0 + scale1)) + shift1
            z = _dot_nt(h1.astype(BF16), win_ref[...])
            z_ref[...] = z

            _, ga = _gelu_parts(z[:, :2 * D_A])
            u, vr = ga[:, :D_A], ga[:, D_A:]
            dv = vr - jnp.mean(vr, axis=-1, keepdims=True)
            v = (dv * lax.rsqrt(jnp.mean(dv * dv, axis=-1, keepdims=True) + EPS)) * lng_ref[...] + lnb_ref[...]
            vb = v.astype(BF16)
            mask = _tril_mask()
            wc = [(wsp_ref[h] * mask).astype(BF16) for h in range(N_HEADS)]
            for ch in range(tt // CHUNK):
                rows = slice(ch * CHUNK, (ch + 1) * CHUNK)
                for h in range(N_HEADS):
                    cols = slice(h * GROUP, (h + 1) * GROUP)
                    mixed = _dot(wc[h], vb[rows, cols]) + bs_ref[:, cols]
                    cat_ref[rows, cols] = (u[rows, cols] * mixed).astype(BF16)

            zb = z[:, 2 * D_A:]
            diff = _pool_diff(zb, carry[...], i * tt)
            carry[...] = zb[tt - HALO:, :]
            for g in range(len(WINDOWS)):
                cols = slice(g * GROUP, (g + 1) * GROUP)
                pre = _dot(diff[g].astype(BF16), wp_ref[g].astype(BF16)) + bp_ref[:, cols]
                cat_ref[:, D_A + g * GROUP:D_A + (g + 1) * GROUP] = (pre * ps_ref[:, cols]).astype(BF16)
            cat_keep[i % (ATTN_LAG + 1)] = cat_ref[...]

        @pl.when(i == 0)
        def _():
            copies.wait_recv(0, 1)
            keep[6].start()
            keep[7].start()

        @pl.when(i == 1)
        def _():
            wo_copies.wait_recv(1, 2, 3)
            wo_copies.start(4, 5, 6)

        @pl.when(i == ATTN_LAG)
        def _():
            wo_copies.wait_recv(0, 4, 5, 6)
            wo_keep.start()

        @pl.when(i >= ATTN_LAG)
        def _():
            xv = xb_ref[...]
            mix = _dot(cat_keep[(i - ATTN_LAG) % (ATTN_LAG + 1)], wout_scr[...])
            mix_ref[...] = mix
            x1v = xv + (mix * _rstd(mix)) * (gate1 * n1post_ref[...])
            x1_ref[...] = x1v
            shift2, scale2 = mod_ref[3:4, :], mod_ref[4:5, :]
            h2 = ((x1v * _rstd(x1v)) * (n2pre_ref[...] * (1.0 + scale2)) + shift2).astype(BF16)
            h2_ref[...] = h2
            for j, (w1, w2) in enumerate(((w1_ref, w2_ref), (sib1, sib2))):
                ra = jnp.maximum(_dot(h2, w1[...]), 0.0)
                r = (ra * ra).astype(BF16)
                r_ref[:, j * FF_BLK:(j + 1) * FF_BLK] = r
                if j == 0:
                    f_ref[...] = _dot(r, w2[...])
                else:
                    f_ref[...] += _dot(r, w2[...])

        @pl.when(i == nt - 1 + ATTN_LAG)
        def _():
            copies.wait_recv(6, 7, 8, 9)
            copies.wait_send(*range(10))
            wo_copies.wait_send(*range(7))
            wi_copies.wait_send(*range(7))
            ada.wait_send(*range(2 * (N_DEV - 1)))
            for cp in keep:
                cp.wait()
            wo_keep.wait()
            wi_keep.wait()

    first = lambda w: pl.BlockSpec((tt, w), lambda i: (jnp.minimum(i, nt - 1), 0))
    second = lambda w: pl.BlockSpec((tt, w), lambda i: (jnp.maximum(i - ATTN_LAG, 0), 0))
    r_head = pl.BlockSpec((tt, FC_HEAD * FF_BLK),
                          lambda i: (jnp.maximum(i - ATTN_LAG, 0), R_HEAD_COLS // (FC_HEAD * FF_BLK)))
    hbm = pl.BlockSpec(memory_space=pl.ANY)
    outs = pl.pallas_call(
        body, name="attn_fwd", grid=(nt + ATTN_LAG,),
        out_shape=tuple([jax.ShapeDtypeStruct((t_len, D_Z), F32), jax.ShapeDtypeStruct((t_len, D), BF16),
                         jax.ShapeDtypeStruct((t_len, D), F32), jax.ShapeDtypeStruct((t_len, D), F32),
                         jax.ShapeDtypeStruct((t_len, D), BF16),
                         jax.ShapeDtypeStruct((t_len, FC_EARLY * FF_BLK), BF16),
                         jax.ShapeDtypeStruct((t_len, D), F32)]
                        + [jax.ShapeDtypeStruct((8, D), F32), jax.ShapeDtypeStruct((N_DEV, D), F32)]
                        + [jax.ShapeDtypeStruct((FC_EARLY,) + s.shape, BF16) for s in fc_shards]
                        + [jax.ShapeDtypeStruct((D, D), BF16), jax.ShapeDtypeStruct((D_Z, D), BF16)]),
        in_specs=[first(D), second(D), _full((8, D)), _resident(w_ada.shape), _full((1, ncol)), _full((1, D)),
                  _full((1, D)), _resident(w_in_shard.shape), _resident(w_out_shard.shape),
                  _full((N_HEADS, CHUNK, CHUNK)), _full((CHUNK, D_A)), _full((1, D_A)), _full((1, D_A)),
                  _full((len(WINDOWS), GROUP, GROUP)), _full((1, D_B)), _full((1, D_B)),
                  _resident(fc_shards[0].shape), _resident(fc_shards[1].shape), _full((1, D))],
        out_specs=(first(D_Z), first(D), second(D), second(D), second(D), r_head, second(D),
                   _full((8, D)), _full((N_DEV, D)), hbm, hbm, hbm, hbm),
        scratch_shapes=[pltpu.VMEM((HALO, D_B), F32),
                        pltpu.VMEM((2,) + fc_shards[0].shape, BF16), pltpu.VMEM((2,) + fc_shards[1].shape, BF16),
                        pltpu.VMEM(fc_shards[0].shape, BF16), pltpu.VMEM(fc_shards[1].shape, BF16),
                        pltpu.VMEM((ATTN_LAG + 1, tt, D), BF16), pltpu.VMEM((D, D), BF16),
                        pltpu.VMEM((D_Z, D), BF16), pltpu.VMEM((8, D), F32),
                        pltpu.VMEM((N_DEV, 8, D), F32), pltpu.VMEM((N_DEV, N_DEV, ncol), F32),
                        pltpu.VMEM((N_DEV, ncol), F32),
                        pltpu.SemaphoreType.DMA((10,)), pltpu.SemaphoreType.DMA((10,)),
                        pltpu.SemaphoreType.DMA((10,)),
                        pltpu.SemaphoreType.DMA((7,)), pltpu.SemaphoreType.DMA((7,)),
                        pltpu.SemaphoreType.DMA((7,)), pltpu.SemaphoreType.DMA((7,)),
                        pltpu.SemaphoreType.DMA((2 * (N_DEV - 1),)), pltpu.SemaphoreType.DMA((2 * (N_DEV - 1),))],
        compiler_params=pltpu.CompilerParams(dimension_semantics=("arbitrary",), vmem_limit_bytes=VMEM_LIMIT),
    )(x, x, c8, w_ada, b_my, n1pre, n1post, w_in_shard, w_out_shard, w_sp, bs_rows, ln_g, ln_b, w_pool, b_pool,
      pool_scale, *fc_shards, n2pre)
    return outs[:9], outs[9:11], outs[11], outs[12]


def _mlp_fwd_early(tt, r_begun, h2, f_head, w1_early, w2_early):
    t_len = h2.shape[0]
    nt = t_len // tt
    n_late = N_DEV - FC_EARLY

    def body(r_begun_ref, h2_ref, fh_ref, w1_ref, w2_ref, r_ref, f_ref, l1_ref, l2_ref,
             land1, land2, send_sems, recv_sems, local_sems):
        i = pl.program_id(0)
        copies = _Copies(
            [(w1_ref.at[2], land1, 4), (w2_ref.at[4], land2, 2),
             (land1, l1_ref.at[1], 1), (land2, l2_ref.at[1], 1)],
            send_sems, recv_sems)
        keep = [pltpu.make_async_copy(land1, l1_ref.at[0], local_sems.at[0]),
                pltpu.make_async_copy(land2, l2_ref.at[0], local_sems.at[1])]

        @pl.when(i == 0)
        def _():
            copies.start(0, 1)

        @pl.when(i == nt - 1)
        def _():
            copies.wait_recv(0, 1)
            copies.start(2, 3)
            for cp in keep:
                cp.start()

        h2 = h2_ref[...]
        f_ref[...] = fh_ref[...]
        for j in range(FC_HEAD, FC_EARLY):
            ra = jnp.maximum(_dot(h2, w1_ref[j]), 0.0)
            r = (ra * ra).astype(BF16)
            r_ref[:, _early_col(j):_early_col(j) + FF_BLK] = r
            f_ref[...] += _dot(r, w2_ref[j])

        @pl.when(i == nt - 1)
        def _():
            copies.wait_recv(2, 3)
            copies.wait_send(0, 1, 2, 3)
            for cp in keep:
                cp.wait()

    tile = lambda w: pl.BlockSpec((tt, w), lambda i: (i, 0))
    hbm = pl.BlockSpec(memory_space=pl.ANY)
    outs = pl.pallas_call(
        body, name="mlp_fwd_early", grid=(nt,),
        out_shape=(jax.ShapeDtypeStruct((t_len, FC_EARLY * FF_BLK), BF16), jax.ShapeDtypeStruct((t_len, D), F32),
                   jax.ShapeDtypeStruct((n_late,) + w1_early.shape[1:], BF16),
                   jax.ShapeDtypeStruct((n_late,) + w2_early.shape[1:], BF16)),
        in_specs=[hbm, tile(D), tile(D), _resident((FC_EARLY, D, FF_BLK)), _resident((FC_EARLY, FF_BLK, D))],
        out_specs=(tile(R_HEAD_COLS), tile(D), hbm, hbm),
        input_output_aliases={0: 0},
        scratch_shapes=[pltpu.VMEM(w1_early.shape[1:], BF16), pltpu.VMEM(w2_early.shape[1:], BF16),
                        pltpu.SemaphoreType.DMA((4,)), pltpu.SemaphoreType.DMA((4,)),
                        pltpu.SemaphoreType.DMA((2,))],
        compiler_params=pltpu.CompilerParams(dimension_semantics=("arbitrary",), vmem_limit_bytes=VMEM_LIMIT),
    )(r_begun, h2, f_head, w1_early, w2_early)
    return outs[:2], outs[2:]


def _mlp_late_bwd(tt, r_early, x1, h2, f_early, tgt, mix, mod, n2pre, n2post, n1post,
                  w1_early, w2_early, w1_late, w2_late):
    t_len = x1.shape[0]
    nt = t_len // tt
    n_late = N_DEV - FC_EARLY
    late_cols = n_late * FF_BLK

    def body(re_ref, x1_ref, h2_ref, fe_ref, tgt_ref, mix_ref, mod_ref, n2pre_ref, n2post_ref,
             n1post_ref, w1e_ref, w2e_ref, w1l_ref, w2l_ref,
             rl_ref, df_ref, da_ref, dmix_ref, dx1_ref, redf_ref, redb_ref, dh2_acc):
        i = pl.program_id(0)

        @pl.when(i == 0)
        def _():
            redf_ref[...] = jnp.zeros_like(redf_ref)
            redb_ref[...] = jnp.zeros_like(redb_ref)

        x1v = x1_ref[...]
        gate1, scale2, gate2 = mod_ref[2:3, :], mod_ref[4:5, :], mod_ref[5:6, :]
        h2 = h2_ref[...]
        f = fe_ref[...]
        for j in range(n_late):
            cols = slice(j * FF_BLK, (j + 1) * FF_BLK)
            ra = jnp.maximum(_dot(h2, w1l_ref[j]), 0.0)
            r = (ra * ra).astype(BF16)
            rl_ref[:, cols] = r
            f = f + _dot(r, w2l_ref[j])
        post2 = n2post_ref[...]
        gate_post2 = gate2 * post2
        rf = _rstd(f)
        fhat = f * rf
        err = (x1v + fhat * gate_post2) - tgt_ref[...]
        dy = err * (1.0 / D)
        d_f, sum_f = _rms_bwd_gained(dy, gate_post2, fhat, rf)
        dfv = d_f.astype(BF16)
        df_ref[...] = dfv
        redf_ref[0:1, :] += post2 * sum_f
        redf_ref[1:2, :] += gate2 * sum_f
        redf_ref[2:3, :] += _colsum(err * err)

        for j in range(N_DEV):
            cols = slice(j * FF_BLK, (j + 1) * FF_BLK)
            if j < FC_EARLY:
                w1, w2, r = w1e_ref[j], w2e_ref[j], re_ref[:, _early_col(j):_early_col(j) + FF_BLK]
            else:
                jl = j - FC_EARLY
                w1, w2, r = w1l_ref[jl], w2l_ref[jl], rl_ref[:, jl * FF_BLK:(jl + 1) * FF_BLK]
            dr = _dot_nt(dfv, w2)
            da = (dr * (2.0 * jnp.sqrt(r.astype(F32)))).astype(BF16)
            da_ref[:, cols] = da
            contrib = _dot_nt(da, w1)
            if j == 0:
                dh2_acc[...] = contrib
            else:
                dh2_acc[...] += contrib
        dh2 = dh2_acc[...]
        pre2, post1 = n2pre_ref[...], n1post_ref[...]
        r2 = _rstd(x1v)
        xhat = x1v * r2
        d_x1, sum_h = _rms_bwd_gained(dh2, pre2 * (1.0 + scale2), xhat, r2)
        dx1 = dy + d_x1
        dx1_ref[...] = dx1
        mixv = mix_ref[...]
        rm = _rstd(mixv)
        mhat = mixv * rm
        d_mix, sum_m = _rms_bwd_gained(dx1, gate1 * post1, mhat, rm)
        dmix_ref[...] = d_mix.astype(BF16)
        redb_ref[0:1, :] += _colsum(dh2)
        redb_ref[1:2, :] += pre2 * sum_h
        redb_ref[2:3, :] += (1.0 + scale2) * sum_h
        redb_ref[3:4, :] += post1 * sum_m
        redb_ref[4:5, :] += gate1 * sum_m

    tile = lambda w: pl.BlockSpec((tt, w), lambda i: (i, 0))
    return pl.pallas_call(
        body, name="mlp_late_bwd", grid=(nt,),
        out_shape=(jax.ShapeDtypeStruct((t_len, late_cols), BF16), jax.ShapeDtypeStruct((t_len, D), BF16),
                   jax.ShapeDtypeStruct((t_len, D_FF), BF16), jax.ShapeDtypeStruct((t_len, D), BF16),
                   jax.ShapeDtypeStruct((t_len, D), F32), jax.ShapeDtypeStruct((8, D), F32),
                   jax.ShapeDtypeStruct((8, D), F32)),
        in_specs=[tile(FC_EARLY * FF_BLK), tile(D), tile(D), tile(D), tile(D),
                  tile(D), _full((8, D)), _full((1, D)), _full((1, D)), _full((1, D)),
                  _resident((FC_EARLY, D, FF_BLK)), _resident((FC_EARLY, FF_BLK, D)),
                  _resident((n_late, D, FF_BLK)), _resident((n_late, FF_BLK, D))],
        out_specs=(tile(late_cols), tile(D), tile(D_FF), tile(D), tile(D), _full((8, D)), _full((8, D))),
        scratch_shapes=[pltpu.VMEM((tt, D), F32)],
        compiler_params=pltpu.CompilerParams(dimension_semantics=("arbitrary",), vmem_limit_bytes=VMEM_LIMIT),
    )(r_early, x1, h2, f_early, tgt, mix, mod, n2pre, n2post, n1post, w1_early, w2_early, w1_late, w2_late)


def _mlp_wgrad(tt, r_early, r_late, da, df, h2):
    t_len = df.shape[0]
    nt = t_len // tt
    odd_steps = [j for j, rel in enumerate(WGRAD_ORDER) if rel % 2]

    def relation(j):
        rel = jnp.int32(WGRAD_ORDER[-1])
        for step in range(N_DEV - 2, -1, -1):
            rel = jnp.where(j == step, WGRAD_ORDER[step], rel)
        return rel

    def body(re_ref, rl_ref, da_ref, df_ref, h2_ref, own1_ref, own2_ref, out1_ref, out2_ref, diag1_ref, diag2_ref,
             acc1, acc2, snd1, snd2, sib1, sib2, dsnd1, dsnd2, send_sems, recv_sems):
        j, t = pl.program_id(0), pl.program_id(1)
        rows = pl.ds(pl.multiple_of(t * tt, tt), tt)
        x, y, c = _place()
        accs, snds, sibs = (acc1, acc2), (snd1, snd2), (sib1, sib2)
        dsnds, diags = (dsnd1, dsnd2), (diag1_ref, diag2_ref)

        def to_sibling(a, jj, buf=0):
            return pltpu.make_async_remote_copy(
                src_ref=snds[a].at[buf], dst_ref=sibs[a].at[jj],
                send_sem=send_sems.at[4 * a + jj], recv_sem=recv_sems.at[4 * a + jj],
                device_id=(x, y, 1 - c), device_id_type=MESH)

        def to_diagonal(a):
            return pltpu.make_async_remote_copy(
                src_ref=dsnds[a], dst_ref=diags[a], send_sem=send_sems.at[8 + a], recv_sem=recv_sems.at[8 + a],
                device_id=_peer(x, y, c, 6), device_id_type=MESH)

        @pl.when(t == 0)
        def _():
            acc2[...] = jnp.zeros_like(acc2)
            acc1[...] = jnp.zeros_like(acc1)

        for r_ref, mine in ((re_ref, relation(j) < FC_EARLY), (rl_ref, relation(j) >= FC_EARLY)):
            @pl.when(mine)
            def _():
                acc2[...] += _dot_tn(r_ref[...], df_ref[rows, :])
                acc1[...] += _dot_tn(h2_ref[rows, :], da_ref[...])

        for step, rel in enumerate(WGRAD_ORDER):
            jj = rel // 2

            @pl.when((t == nt - 1) & (j == step))
            def _():
                for a, (own_ref, out_ref) in enumerate(((own1_ref, out1_ref), (own2_ref, out2_ref))):
                    if rel % 2:
                        q = odd_steps.index(step)
                        if q >= 2:
                            to_sibling(a, WGRAD_ORDER[odd_steps[q - 2]] // 2).wait_send()
                        snds[a][q % 2] = accs[a][...].astype(BF16)
                        to_sibling(a, jj, q % 2).start()
                        continue
                    to_sibling(a, jj).wait_recv()
                    chip_sum = accs[a][...] + sibs[a][jj].astype(F32)
                    if rel == 6:
                        dsnds[a][...] = chip_sum.astype(BF16)
                        to_diagonal(a).start()
                    elif rel == 0:
                        own_ref[...] = chip_sum
                    else:
                        out_ref[0] = chip_sum.astype(BF16)
                    if step == N_DEV - 1:
                        for q in (2, 3):
                            to_sibling(a, WGRAD_ORDER[odd_steps[q]] // 2).wait_send()
                        to_diagonal(a).wait_recv()
                        to_diagonal(a).wait_send()

    assert WGRAD_ORDER[-1] == 0 and WGRAD_ORDER[-3:-1] == (2, 4)
    blk = pl.BlockSpec((tt, FF_BLK), lambda j, t: (t, relation(j)))
    early_block = lambda rel: jnp.where(rel < FC_HEAD, rel + FC_EARLY - FC_HEAD, rel - FC_HEAD)
    early = lambda j, t: (jnp.where(relation(j) < FC_EARLY, t, 0),
                          jnp.where(relation(j) < FC_EARLY, early_block(relation(j)), 0))
    late = lambda j, t: (jnp.where(relation(j) < FC_EARLY, 0, t), jnp.maximum(relation(j) - FC_EARLY, 0))
    chip = lambda j, t: (jnp.clip(j - 5, 0, 1), 0, 0)
    hbm = pl.BlockSpec(memory_space=pl.ANY)
    return pl.pallas_call(
        body, name="mlp_wgrad", grid=(N_DEV, nt),
        out_shape=(jax.ShapeDtypeStruct((D, FF_BLK), F32), jax.ShapeDtypeStruct((FF_BLK, D), F32),
                   jax.ShapeDtypeStruct((2, D, FF_BLK), BF16), jax.ShapeDtypeStruct((2, FF_BLK, D), BF16),
                   jax.ShapeDtypeStruct((D, FF_BLK), BF16), jax.ShapeDtypeStruct((FF_BLK, D), BF16)),
        in_specs=[pl.BlockSpec((tt, FF_BLK), early), pl.BlockSpec((tt, FF_BLK), late), blk,
                  _resident((t_len, D)), _resident((t_len, D))],
        out_specs=(_full((D, FF_BLK)), _full((FF_BLK, D)),
                   pl.BlockSpec((1, D, FF_BLK), chip), pl.BlockSpec((1, FF_BLK, D), chip), hbm, hbm),
        scratch_shapes=[pltpu.VMEM((D, FF_BLK), F32), pltpu.VMEM((FF_BLK, D), F32),
                        pltpu.VMEM((2, D, FF_BLK), BF16), pltpu.VMEM((2, FF_BLK, D), BF16),
                        pltpu.VMEM((4, D, FF_BLK), BF16), pltpu.VMEM((4, FF_BLK, D), BF16),
                        pltpu.VMEM((D, FF_BLK), BF16), pltpu.VMEM((FF_BLK, D), BF16),
                        pltpu.SemaphoreType.DMA((10,)), pltpu.SemaphoreType.DMA((10,))],
        compiler_params=pltpu.CompilerParams(dimension_semantics=("arbitrary", "arbitrary"),
                                             vmem_limit_bytes=VMEM_LIMIT),
    )(r_early, r_late, da, df, h2)


def _acc_rows(ref, row0, k, val):
    half = CHUNK // 2
    ref[row0:row0 + half, k * GROUP:(k + 1) * GROUP] += val[:half, :]
    ref[row0:row0 + half, D_A + k * GROUP:D_A + (k + 1) * GROUP] += val[half:, :]


def _attn_bwd(tt, dmix, z, cat, w_out, w_sp, bs_rows, ln_g, ln_b, w_pool, b_pool, pool_scale, red_fwd, red_bwd,
              chip_sums):
    t_len = z.shape[0]
    nt = t_len // tt
    hb = tt // HALO
    n_sums = len(chip_sums)

    def body(dmix_ref, z_ref, zprev_ref, cat_ref, wout_ref, wsp_ref,
             bs_ref, lng_ref, lnb_ref, wp_ref, bp_ref, ps_ref, redf_ref, redb_ref, *rest):
        sum_out = rest[:n_sums]
        dz_ref, gwout_ref, small_ref = rest[n_sums:n_sums + 3]
        sum_in = rest[n_sums + 3:2 * n_sums + 3]
        carry, acc_out, dz_scr, bs_acc, send_sems, recv_sems = rest[2 * n_sums + 3:]
        s = pl.program_id(0)
        i = nt - 1 - s
        px, py, pc = _place()

        def chip_copy(a, r):
            return pltpu.make_async_remote_copy(
                src_ref=sum_out[a].at[r], dst_ref=sum_in[a].at[r],
                send_sem=send_sems.at[2 * a + r], recv_sem=recv_sems.at[2 * a + r],
                device_id=_peer(px, py, pc, 2 * (r + 1)), device_id_type=MESH)

        @pl.when(s == 0)
        def _():
            for a in range(n_sums):
                for r in range(2):
                    chip_copy(a, r).start()
            carry[...] = jnp.zeros_like(carry)
            acc_out[...] = jnp.zeros_like(acc_out)
            bs_acc[...] = jnp.zeros_like(bs_acc)
            small_ref[...] = jnp.zeros_like(small_ref)
            small_ref[ROW_DMOD + 2:ROW_DMOD + 3, :] = redb_ref[3:4, :]
            small_ref[ROW_DMOD + 3:ROW_DMOD + 5, :] = redb_ref[0:2, :]
            small_ref[ROW_DMOD + 5:ROW_DMOD + 6, :] = redf_ref[0:1, :]
            small_ref[ROW_N1POST:ROW_N1POST + 1, :] = redb_ref[4:5, :]
            small_ref[ROW_N2PRE:ROW_N2PRE + 1, :] = redb_ref[2:3, :]
            small_ref[ROW_N2POST:ROW_N2POST + 1, :] = redf_ref[1:2, :]
            small_ref[ROW_LOSS:ROW_LOSS + 1, :] = redf_ref[2:3, :]

        dmixv = dmix_ref[...]
        dcat = _dot_nt(dmixv, wout_ref[...])
        acc_out[...] += _dot_tn(cat_ref[...], dmixv)

        z = z_ref[...]
        t_g, ga = _gelu_parts(z[:, :2 * D_A])
        u, vr = ga[:, :D_A], ga[:, D_A:]
        dv0 = vr - jnp.mean(vr, axis=-1, keepdims=True)
        rv = lax.rsqrt(jnp.mean(dv0 * dv0, axis=-1, keepdims=True) + EPS)
        vhat = dv0 * rv
        vb = (vhat * lng_ref[...] + lnb_ref[...]).astype(BF16)
        mask = _tril_mask()
        wc = [(wsp_ref[h] * mask).astype(BF16) for h in range(N_HEADS)]

        dya = dcat[:, :D_A]
        for h in range(N_HEADS):
            cols = slice(h * GROUP, (h + 1) * GROUP)
            bs_sum = jnp.zeros((CHUNK, GROUP), F32)
            ws_sum = jnp.zeros((CHUNK, CHUNK), F32)
            for ch in range(tt // CHUNK):
                rows = slice(ch * CHUNK, (ch + 1) * CHUNK)
                v_ch = vb[rows, cols]
                mixed = _dot(wc[h], v_ch) + bs_ref[:, cols]
                dy_ch = dya[rows, cols]
                dz_scr[rows, cols] = dy_ch * mixed
                dmixed = dy_ch * u[rows, cols]
                dmb = dmixed.astype(BF16)
                dz_scr[rows, D_A + h * GROUP:D_A + (h + 1) * GROUP] = _dot_tn(wc[h], dmb)
                bs_sum = bs_sum + dmixed
                ws_sum = ws_sum + _dot_nt(dmb, v_ch)
            _acc_rows(bs_acc, 0, h, bs_sum)
            _acc_rows(small_ref, ROW_WS, h, ws_sum)

        dvl = dz_scr[:, D_A:2 * D_A]
        dvhat = dvl * lng_ref[...]
        dvl_vhat = dvl * vhat
        dvr = rv * (dvhat - jnp.mean(dvhat, axis=-1, keepdims=True)
                    - vhat * jnp.mean(dvl_vhat * lng_ref[...], axis=-1, keepdims=True))
        small_ref[ROW_LN:ROW_LN + 1, 0:D_A] += _colsum(dvl_vhat)
        small_ref[ROW_LN:ROW_LN + 1, D_A:D] += _colsum(dvl)
        dga = jnp.concatenate([dz_scr[:, :D_A], dvr], axis=1)
        dza = dga * _gelu_grad(z[:, :2 * D_A], t_g)

        zb = z[:, 2 * D_A:]
        halo_prev = jnp.where(i == 0, 0.0, zprev_ref[...])
        diff = _pool_diff(zb, halo_prev, i * tt)
        dyb = dcat[:, D_A:]
        inv = _inv_counts(i * tt, tt)
        scaled, ddiffs = [], []
        for g in range(len(WINDOWS)):
            cols = slice(g * GROUP, (g + 1) * GROUP)
            db = diff[g].astype(BF16)
            wpg = wp_ref[g].astype(BF16)
            pre = _dot(db, wpg) + bp_ref[:, cols]
            small_ref[ROW_POOL:ROW_POOL + 1, cols] += _colsum(dyb[:, cols] * pre)
            dpre = dyb[:, cols] * ps_ref[:, cols]
            small_ref[ROW_POOL:ROW_POOL + 1, D_B + g * GROUP:D_B + (g + 1) * GROUP] += _colsum(dpre)
            dpb = dpre.astype(BF16)
            _acc_rows(small_ref, ROW_WP, g, _dot_tn(db, dpb))
            ddiff = _dot_nt(dpb, wpg)
            ddiffs.append(ddiff)
            scaled.append(ddiff * inv[g])
        scaled_all = jnp.concatenate(scaled, axis=1)
        ext = jnp.concatenate([scaled_all, carry[...]], axis=0)
        n_ext = tt + HALO
        s2 = ext + pltpu.roll(ext, n_ext - 1, 0)
        t4 = s2[:, GROUP:]
        s4 = t4 + pltpu.roll(t4, n_ext - 2, 0)
        t8 = s4[:, GROUP:]
        s8 = t8 + pltpu.roll(t8, n_ext - 4, 0)
        t16 = s8[:, GROUP:]
        s16 = t16 + pltpu.roll(t16, n_ext - 8, 0)
        back = [s2[:, :GROUP], s4[:, :GROUP], s8[:, :GROUP], s16]
        carry[...] = scaled_all[:HALO, :]
        dzb = jnp.concatenate([back[g][:tt, :] - ddiffs[g] for g in range(len(WINDOWS))], axis=1)

        dz_ref[:, :2 * D_A] = dza.astype(BF16)
        dz_ref[:, 2 * D_A:] = dzb.astype(BF16)

        @pl.when(s == nt - 1)
        def _():
            gwout_ref[...] = acc_out[...].astype(BF16)
            bs = _unfold(bs_acc[...])
            for h in range(N_HEADS):
                small_ref[ROW_BS + h:ROW_BS + h + 1, 0:GROUP] = jnp.sum(
                    bs[:, h * GROUP:(h + 1) * GROUP].T, axis=0, keepdims=True)
            for a in range(n_sums):
                for r in range(2):
                    chip_copy(a, r).wait_recv()
                    chip_copy(a, r).wait_send()

    rev = lambda w: pl.BlockSpec((tt, w), lambda s: (nt - 1 - s, 0))
    zprev = pl.BlockSpec((HALO, D_B), lambda s: (jnp.maximum((nt - 1 - s) * hb - 1, 0), 2))
    hbm = pl.BlockSpec(memory_space=pl.ANY)
    outs = pl.pallas_call(
        body, name="attn_bwd", grid=(nt,),
        out_shape=tuple([jax.ShapeDtypeStruct((t_len, D_Z), BF16),
                         jax.ShapeDtypeStruct((D, D), BF16), jax.ShapeDtypeStruct((SMALL_ROWS, D), F32)]
                        + [jax.ShapeDtypeStruct(cs.shape, cs.dtype) for cs in chip_sums]),
        in_specs=[rev(D), rev(D_Z), zprev, rev(D),
                  _resident((D, D)), _full((N_HEADS, CHUNK, CHUNK)), _full((CHUNK, D_A)),
                  _full((1, D_A)), _full((1, D_A)), _full((len(WINDOWS), GROUP, GROUP)), _full((1, D_B)),
                  _full((1, D_B)), _full((8, D)), _full((8, D))] + [_resident(cs.shape) for cs in chip_sums],
        out_specs=tuple([rev(D_Z), _resident((D, D)), _full((SMALL_ROWS, D))] + [hbm] * n_sums),
        scratch_shapes=[pltpu.VMEM((HALO, D_B), F32), pltpu.VMEM((D, D), F32),
                        pltpu.VMEM((tt, 2 * D_A), F32), pltpu.VMEM((CHUNK // 2, D), F32),
                        pltpu.SemaphoreType.DMA((2 * n_sums,)), pltpu.SemaphoreType.DMA((2 * n_sums,))],
        compiler_params=pltpu.CompilerParams(dimension_semantics=("arbitrary",), vmem_limit_bytes=VMEM_LIMIT),
    )(dmix, z, z, cat, w_out, w_sp, bs_rows, ln_g, ln_b, w_pool, b_pool, pool_scale,
      red_fwd, red_bwd, *chip_sums)
    return outs[:3], outs[3:]


def _in_proj_bwd(tt, dz, dx1, x, mod, n1pre, w_in_t):
    t_len = x.shape[0]
    nt = t_len // tt

    def body(dz_ref, dx1_ref, x_ref, mod_ref, n1pre_ref, win_ref, gx_ref, gwin_ref, sums_ref, acc_in):
        i = pl.program_id(0)

        @pl.when(i == 0)
        def _():
            acc_in[...] = jnp.zeros_like(acc_in)
            sums_ref[...] = jnp.zeros_like(sums_ref)

        dzv = dz_ref[...]
        dh1 = _dot(dzv, win_ref[...])
        xv = x_ref[...]
        r1 = _rstd(xv)
        xhat = xv * r1
        shift1, scale1 = mod_ref[0:1, :], mod_ref[1:2, :]
        pre1 = n1pre_ref[...]
        gain1 = pre1 * (1.0 + scale1)
        h1 = (xhat * gain1 + shift1).astype(BF16)
        acc_in[...] += _dot_tn(dzv, h1)
        d_x, sum_h = _rms_bwd_gained(dh1, gain1, xhat, r1)
        gx_ref[...] = dx1_ref[...] + d_x
        sums_ref[ROW_DMOD:ROW_DMOD + 1, :] += _colsum(dh1)
        sums_ref[ROW_DMOD + 1:ROW_DMOD + 2, :] += pre1 * sum_h
        sums_ref[ROW_N1PRE:ROW_N1PRE + 1, :] += (1.0 + scale1) * sum_h

        @pl.when(i == nt - 1)
        def _():
            gwin_ref[...] = acc_in[...].astype(BF16)

    tile = lambda w: pl.BlockSpec((tt, w), lambda i: (i, 0))
    return pl.pallas_call(
        body, name="in_proj_bwd", grid=(nt,),
        out_shape=(jax.ShapeDtypeStruct((t_len, D), F32), jax.ShapeDtypeStruct((D_Z, D), BF16),
                   jax.ShapeDtypeStruct((16, D), F32)),
        in_specs=[tile(D_Z), tile(D), tile(D), _full((8, D)), _full((1, D)), _resident((D_Z, D))],
        out_specs=(tile(D), _resident((D_Z, D)), _full((16, D))),
        scratch_shapes=[pltpu.VMEM((D_Z, D), F32)],
        compiler_params=pltpu.CompilerParams(dimension_semantics=("arbitrary",), vmem_limit_bytes=VMEM_LIMIT),
    )(dz, dx1, x, mod, n1pre, w_in_t)


def _adam(w, g, m, v):
    m2 = ADAM_B1 * m + (1.0 - ADAM_B1) * g
    v2 = ADAM_B2 * v + (1.0 - ADAM_B2) * (g * g)
    m_hat = m2 / (1.0 - ADAM_B1 ** ADAM_STEP)
    v_hat = v2 / (1.0 - ADAM_B2 ** ADAM_STEP)
    delta = -ADAM_LR * (m_hat / (jnp.sqrt(v_hat) + ADAM_EPS) + ADAM_WD * w)
    return delta, m2, v2


def _adamw_shard(name, rb, w, g, m, v):
    rows, cols = w.shape

    def body(w_ref, g_ref, m_ref, v_ref, d_ref, m2_ref, v2_ref):
        d_ref[...], m2_ref[...], v2_ref[...] = _adam(w_ref[...], g_ref[...], m_ref[...], v_ref[...])

    blk = pl.BlockSpec((rb, cols), lambda i: (i, 0))
    shp = jax.ShapeDtypeStruct((rows, cols), F32)
    return pl.pallas_call(
        body, name=name, grid=(rows // rb,), out_shape=(shp, shp, shp),
        in_specs=[blk] * 4, out_specs=(blk, blk, blk),
        compiler_params=pltpu.CompilerParams(dimension_semantics=("arbitrary",)),
    )(w, g, m, v)


def _adamw_fc(steps, fc):
    n_fc = len(fc)

    def body(*refs):
        ins, outs = refs[:6 * n_fc], refs[6 * n_fc:]
        for k in range(n_fc):
            w_ref, own_ref, arr_ref, diag_ref, m_ref, v_ref = ins[6 * k:6 * k + 6]
            g = ((own_ref[...] + arr_ref[0].astype(F32)) + arr_ref[1].astype(F32)) + diag_ref[...].astype(F32)
            outs[4 * k][...] = g
            outs[4 * k + 1][...], outs[4 * k + 2][...], outs[4 * k + 3][...] = _adam(
                w_ref[...], g, m_ref[...], v_ref[...])

    specs_in, specs_out, shapes, args = [], [], [], []
    for w, own, arrived, diagonal, m, v in fc:
        rows, cols = w.shape
        blk = pl.BlockSpec((rows // steps, cols), lambda i: (i, 0))
        specs_in += [blk, blk, pl.BlockSpec((2, rows // steps, cols), lambda i: (0, i, 0)), blk, blk, blk]
        specs_out += [blk] * 4
        shapes += [jax.ShapeDtypeStruct((rows, cols), F32)] * 4
        args += [w, own, arrived, diagonal, m, v]
    outs = pl.pallas_call(
        body, name="adamw_fc", grid=(steps,), out_shape=tuple(shapes), in_specs=specs_in, out_specs=tuple(specs_out),
        compiler_params=pltpu.CompilerParams(dimension_semantics=("arbitrary",), vmem_limit_bytes=VMEM_LIMIT),
    )(*args)
    return [outs[4 * k:4 * k + 4] for k in range(n_fc)]


def _adamw_ada(rb, w, sc, dmod_cols, m, v):
    rows, cols = w.shape

    def body(w_ref, sc_ref, dm_ref, m_ref, v_ref, g_ref, d_ref, m2_ref, v2_ref):
        g = _dot_tn(sc_ref[...].astype(BF16), dm_ref[...].astype(BF16))
        g_ref[...] = g
        d_ref[...], m2_ref[...], v2_ref[...] = _adam(w_ref[...], g, m_ref[...], v_ref[...])

    blk = pl.BlockSpec((rb, cols), lambda i: (i, 0))
    shp = jax.ShapeDtypeStruct((rows, cols), F32)
    return pl.pallas_call(
        body, name="adamw_ada", grid=(rows // rb,), out_shape=(shp, shp, shp, shp),
        in_specs=[blk, pl.BlockSpec((N_DEV, rb), lambda i: (0, i)), _full((N_DEV, cols)), blk, blk],
        out_specs=(blk, blk, blk, blk),
        compiler_params=pltpu.CompilerParams(dimension_semantics=("arbitrary",)),
    )(w, sc, dmod_cols, m, v)


def _unfold(acc_rows):
    return jnp.concatenate([acc_rows[:, :D_A], acc_rows[:, D_A:]], axis=0)


def _adamw_small(total, params):
    n = len(params)
    flat = [a for p in params for a in p]

    def body(*refs):
        s_ref = refs[0]
        p_refs = refs[1:1 + 3 * n]
        loss_ref = refs[1 + 3 * n]
        o_refs = refs[2 + 3 * n:]
        d_b_ada = s_ref[0:6, :]
        for b in range(1, N_DEV):
            d_b_ada = d_b_ada + s_ref[_table_row(b):_table_row(b) + 6, :]
        misc = lambda r: s_ref[PK_MISC + r - ROW_N1PRE:PK_MISC + r - ROW_N1PRE + 1, :]
        loss = jnp.sum(misc(ROW_LOSS), axis=-1, keepdims=True) * (0.5 / D)
        loss_ref[...] = loss
        mask = _tril_mask()
        ws = _unfold(s_ref[PK_WS:PK_WS + 64, :])
        wp = _unfold(s_ref[PK_WP:PK_WP + 64, :])
        grads = [
            d_b_ada,
            misc(ROW_N1PRE), misc(ROW_N1POST), misc(ROW_N2PRE), misc(ROW_N2POST),
            misc(ROW_LN)[:, :D_A], misc(ROW_LN)[:, D_A:],
            misc(ROW_POOL)[:, :D_B], misc(ROW_POOL)[:, D_B:],
            s_ref[PK_BS:PK_BS + N_HEADS, 0:GROUP],
            jnp.stack([ws[:, h * GROUP:(h + 1) * GROUP] * mask for h in range(N_HEADS)]),
            jnp.stack([wp[:, g * GROUP:(g + 1) * GROUP] for g in range(len(WINDOWS))]),
        ]
        for k in range(n):
            w_ref, m_ref, v_ref = p_refs[3 * k:3 * k + 3]
            g = grads[k]
            if k == 0:
                for j in range(6):
                    o_refs[0][j] = g[j:j + 1, :]
                    o_refs[1][j], o_refs[2][j], o_refs[3][j] = _adam(w_ref[j], g[j:j + 1, :], m_ref[j], v_ref[j])
                continue
            o_refs[4 * k][...] = g
            o_refs[4 * k + 1][...], o_refs[4 * k + 2][...], o_refs[4 * k + 3][...] = _adam(
                w_ref[...], g, m_ref[...], v_ref[...])

    vm = pl.BlockSpec(memory_space=pltpu.VMEM)
    out_shape = [jax.ShapeDtypeStruct((1, 1), F32)]
    for w, _, _ in params:
        out_shape += [jax.ShapeDtypeStruct(w.shape, F32)] * 4
    return pl.pallas_call(
        body, name="adamw_small", out_shape=tuple(out_shape),
        in_specs=[vm] * (1 + 3 * n), out_specs=tuple([vm] * len(out_shape)),
    )(total, *flat)


TT_ATTN_FWD = 512
ATTN_LAG = 2
TT_MLP_FWD = 512
TT_MLP = 256
TT_WGRAD = 2048
TT_ATTN_BWD = 512
TT_IN_PROJ_BWD = 512


def kernel(x, c, w_ada, b_ada, norm1_pre, norm1_post, w_in, w_spatial, b_spatial, ln_v_gain, ln_v_bias, w_pool, b_pool, pool_scale, w_out, norm2_pre, norm2_post, w_fc1, w_fc2, loss_target, m_w_ada, m_b_ada, m_norm1_pre, m_norm1_post, m_w_in, m_w_spatial, m_b_spatial, m_ln_v_gain, m_ln_v_bias, m_w_pool, m_b_pool, m_pool_scale, m_w_out, m_norm2_pre, m_norm2_post, m_w_fc1, m_w_fc2, v_w_ada, v_b_ada, v_norm1_pre, v_norm1_post, v_w_in, v_w_spatial, v_b_spatial, v_ln_v_gain, v_ln_v_bias, v_w_pool, v_b_pool, v_pool_scale, v_w_out, v_norm2_pre, v_norm2_post, v_w_fc1, v_w_fc2):
    t_len = x.shape[1]
    me = 4 * lax.axis_index("x") + 2 * lax.axis_index("y") + lax.axis_index("c")
    ada_cols = w_ada.shape[1]
    tt = lambda want: min(want, t_len)

    x2 = x.reshape(t_len, D)
    tgt = loss_target.reshape(t_len, D)
    row = lambda a: a.reshape(1, -1)

    b_my = lax.dynamic_slice_in_dim(b_ada, me * ada_cols, ada_cols).reshape(1, ada_cols)
    w_in_shard, w_out_shard, w1_shard, w2_shard = _cast_shards([w_in.T, w_out, w_fc1, w_fc2])

    bs_rows = jnp.repeat(b_spatial.T, GROUP, axis=1)
    attn_consts = (w_spatial, bs_rows, row(ln_v_gain), row(ln_v_bias), w_pool, row(b_pool), row(pool_scale))

    (z, cat, mix, x1, h2, r_begun, f_head, mod, sc), (w1_early, w2_early), w_out_all, w_in_t = _attn_fwd(
        tt(TT_ATTN_FWD), x2, jnp.broadcast_to(c, (8, D)), w_ada, b_my, row(norm1_pre), row(norm1_post),
        w_in_shard, w_out_shard, *attn_consts, (w1_shard, w2_shard), row(norm2_pre))
    (r_early, f_early), (w1_late, w2_late) = _mlp_fwd_early(
        tt(TT_MLP_FWD), r_begun, h2, f_head, w1_early, w2_early)
    r_late, df, da, dmix, dx1, red_fwd, red_bwd = _mlp_late_bwd(
        tt(TT_MLP), r_early, x1, h2, f_early, tgt, mix, mod, row(norm2_pre), row(norm2_post), row(norm1_post),
        w1_early, w2_early, w1_late, w2_late)
    own_w1, own_w2, sums_w1, sums_w2, diag_w1, diag_w2 = _mlp_wgrad(tt(TT_WGRAD), r_early, r_late, da, df, h2)
    (dz, p_out, small), (arr_w1, arr_w2) = _attn_bwd(
        tt(TT_ATTN_BWD), dmix, z, cat, w_out_all, *attn_consts, red_fwd, red_bwd, [sums_w1, sums_w2])
    grad_x, p_in, small_head = _in_proj_bwd(tt(TT_IN_PROJ_BWD), dz, dx1, x2, mod, row(norm1_pre), w_in_t)
    (grad_w1, d_w1, m_w1, v_w1), (grad_w2, d_w2, m_w2, v_w2) = _adamw_fc(
        4, [(w_fc1, own_w1, arr_w1, diag_w1, m_w_fc1, v_w_fc1), (w_fc2, own_w2, arr_w2, diag_w2, m_w_fc2, v_w_fc2)])
    grad_in_t, grad_out, total = _tail_comm(
        [p_in.reshape(N_DEV, D_Z // N_DEV, D), p_out.reshape(N_DEV, D // N_DEV, D)], small, small_head, 64)

    d_out, m_out, v_out = _adamw_shard("adamw_out", 128, w_out, grad_out, m_w_out, v_w_out)
    d_in_t, m_in_t, v_in_t = _adamw_shard("adamw_in", D_Z // N_DEV, w_in.T, grad_in_t, m_w_in.T, v_w_in.T)
    table = jnp.concatenate([total[0:PACK_FINE, :], total[PK_TABLE_B:PK_MISC, :]], axis=0)
    dmod_all = table.reshape(N_DEV, 8, D)[:, :6, :].reshape(N_DEV, 6 * D)
    dmod_cols = lax.dynamic_slice_in_dim(dmod_all, me * ada_cols, ada_cols, axis=1)
    grad_ada, d_ada, m_ada, v_ada = _adamw_ada(256, w_ada, sc, dmod_cols, m_w_ada, v_w_ada)

    six = lambda a: a.reshape(6, 1, D)
    small_params = [
        (six(b_ada), six(m_b_ada), six(v_b_ada)),
        (row(norm1_pre), row(m_norm1_pre), row(v_norm1_pre)),
        (row(norm1_post), row(m_norm1_post), row(v_norm1_post)),
        (row(norm2_pre), row(m_norm2_pre), row(v_norm2_pre)),
        (row(norm2_post), row(m_norm2_post), row(v_norm2_post)),
        (row(ln_v_gain), row(m_ln_v_gain), row(v_ln_v_gain)),
        (row(ln_v_bias), row(m_ln_v_bias), row(v_ln_v_bias)),
        (row(pool_scale), row(m_pool_scale), row(v_pool_scale)),
        (row(b_pool), row(m_b_pool), row(v_b_pool)),
        (b_spatial, m_b_spatial, v_b_spatial),
        (w_spatial, m_w_spatial, v_w_spatial),
        (w_pool, m_w_pool, v_w_pool),
    ]
    outs = _adamw_small(total, small_params)
    loss = outs[0].reshape(())
    names = ["b_ada", "norm1_pre", "norm1_post", "norm2_pre", "norm2_post", "ln_v_gain", "ln_v_bias", "pool_scale",
             "b_pool", "b_spatial", "w_spatial", "w_pool"]
    shapes = dict(b_ada=b_ada.shape, norm1_pre=norm1_pre.shape, norm1_post=norm1_post.shape,
                  norm2_pre=norm2_pre.shape, norm2_post=norm2_post.shape, ln_v_gain=ln_v_gain.shape,
                  ln_v_bias=ln_v_bias.shape, pool_scale=pool_scale.shape, b_pool=b_pool.shape,
                  b_spatial=b_spatial.shape, w_spatial=w_spatial.shape, w_pool=w_pool.shape)
    res = {}
    for k, nm in enumerate(names):
        res[nm] = tuple(o.reshape(shapes[nm]) for o in outs[1 + 4 * k:5 + 4 * k])
    res["w_ada"] = (grad_ada, d_ada, m_ada, v_ada)
    res["w_in"] = (grad_in_t.T, d_in_t.T, m_in_t.T, v_in_t.T)
    res["w_out"] = (grad_out, d_out, m_out, v_out)
    res["w_fc1"] = (grad_w1, d_w1, m_w1, v_w1)
    res["w_fc2"] = (grad_w2, d_w2, m_w2, v_w2)

    order = ["w_ada", "b_ada", "norm1_pre", "norm1_post", "w_in", "w_spatial", "b_spatial", "ln_v_gain", "ln_v_bias",
             "w_pool", "b_pool", "pool_scale", "w_out", "norm2_pre", "norm2_post", "w_fc1", "w_fc2"]
    return (loss, grad_x.reshape(x.shape),
            *[res[nm][0] for nm in order], *[res[nm][1] for nm in order],
            *[res[nm][2] for nm in order], *[res[nm][3] for nm in order])
```

```python
import functools

import jax
import jax.numpy as jnp
from jax import lax
from jax.experimental import pallas as pl
from jax.experimental.pallas import tpu as pltpu

F32 = jnp.float32
BF16 = jnp.bfloat16
MESH = pl.DeviceIdType.MESH

N_DEV = 8
D = 1024
D_A = 512
D_B = 512
D_Z = 2 * D_A + D_B
N_HEADS = 4
CHUNK = 128
WINDOWS = (2, 4, 8, 16)
GROUP = 128
D_FF = 4096
FF_BLK = D_FF // N_DEV
HALO = 16
EPS = 1e-6
VMEM_LIMIT = 60 * 1024 * 1024

ADAM_LR = 0.001
ADAM_B1 = 0.9
ADAM_B2 = 0.999
ADAM_EPS = 1e-08
ADAM_WD = 0.01
ADAM_STEP = 10

ROW_DMOD = 0
ROW_N1PRE, ROW_N1POST, ROW_N2PRE, ROW_N2POST = 8, 9, 10, 11
ROW_LN = 12
ROW_POOL = 13
ROW_LOSS = 14
ROW_BS = 16
ROW_WS = 24
ROW_WP = 88
SMALL_ROWS = 152
PACK_FINE = 40
PACK_HALF = PACK_FINE + 64
PACK_ROWS = 2 * PACK_HALF
PK_WS = PACK_FINE
PK_TABLE_B = PACK_HALF
PK_MISC = PK_TABLE_B + 24
PK_BS = PK_MISC + 8
PK_WP = PK_BS + 8


def _table_row(b):
    if isinstance(b, int):
        return 8 * b if 8 * b < PACK_FINE else 8 * b + PK_TABLE_B - PACK_FINE
    return 8 * b + jnp.where(8 * b < PACK_FINE, 0, PK_TABLE_B - PACK_FINE)


def _dot(a, b):
    return jnp.dot(a, b, preferred_element_type=F32)


def _dot_nt(a, b):
    return lax.dot_general(a, b, (((1,), (1,)), ((), ())), preferred_element_type=F32)


def _dot_tn(a, b):
    return lax.dot_general(a, b, (((0,), (0,)), ((), ())), preferred_element_type=F32)


def _rstd(v):
    return lax.rsqrt(jnp.mean(v * v, axis=-1, keepdims=True) + EPS)


def _rms_bwd(d_hat, hat, rstd):
    return rstd * (d_hat - hat * jnp.mean(d_hat * hat, axis=-1, keepdims=True))


def _rms_bwd_gained(g, gain, hat, rstd):
    g_hat = g * hat
    d_v = rstd * (g * gain - hat * jnp.mean(g_hat * gain, axis=-1, keepdims=True))
    return d_v, _colsum(g_hat)


_K0 = 0.7978845608028654
_K1 = 0.044715


def _gelu_parts(v):
    t = jnp.tanh(v * (_K0 + (_K0 * _K1) * (v * v)))
    return t, v * (0.5 + 0.5 * t)


def _gelu_grad(v, t):
    return (0.5 + 0.5 * t) + (0.5 * v) * (1.0 - t * t) * (_K0 + (3.0 * _K0 * _K1) * (v * v))


def _colsum(v):
    return jnp.sum(v, axis=0, keepdims=True)


def _full(shape):
    n = len(shape)
    return pl.BlockSpec(shape, lambda *_: (0,) * n)


def _resident(shape):
    n = len(shape)
    return pl.BlockSpec(shape, lambda *_: (0,) * n, pipeline_mode=pl.Buffered(1))


def _place():
    x, y, c = lax.axis_index("x"), lax.axis_index("y"), lax.axis_index("c")
    return x, y, c


def _flip(v, bit):
    return 1 - v if bit else v


def _peer(x, y, c, k):
    return (_flip(x, (k >> 2) & 1), _flip(y, (k >> 1) & 1), _flip(c, k & 1))


def _index(p):
    return 4 * p[0] + 2 * p[1] + p[2]


def _cast_shards(shards):
    def body(*refs):
        for src, dst in zip(refs[:len(shards)], refs[len(shards):]):
            dst[...] = src[...].astype(BF16)

    vm = pl.BlockSpec(memory_space=pltpu.VMEM)
    return pl.pallas_call(
        body, name="cast_shards", out_shape=tuple(jax.ShapeDtypeStruct(s.shape, BF16) for s in shards),
        in_specs=[vm] * len(shards), out_specs=tuple([vm] * len(shards)),
    )(*shards)


FC_EARLY = 6
FC_HEAD = 2
R_HEAD_COLS = (FC_EARLY - FC_HEAD) * FF_BLK
WGRAD_ORDER = (7, 6, 1, 3, 5, 2, 4, 0)


def _early_col(j):
    return R_HEAD_COLS + j * FF_BLK if j < FC_HEAD else (j - FC_HEAD) * FF_BLK


class _Copies:
    def __init__(self, entries, send_sems, recv_sems):
        self.place = _place()
        self.entries, self.send_sems, self.recv_sems = entries, send_sems, recv_sems

    def _copy(self, i, arrival=False):
        src, dst, rel = self.entries[i]
        return pltpu.make_async_remote_copy(
            src_ref=dst if arrival else src, dst_ref=dst, send_sem=self.send_sems.at[i],
            recv_sem=self.recv_sems.at[i], device_id=_peer(*self.place, rel), device_id_type=MESH)

    def start(self, *which):
        for i in which:
            self._copy(i).start()

    def wait_recv(self, *which):
        for i in which:
            self._copy(i, arrival=True).wait_recv()

    def wait_send(self, *which):
        for i in which:
            self._copy(i).wait_send()


TAIL_STEPS = 3


def _tail_comm(parts, small, head, row_chunk):
    n = len(parts)

    def body(*refs):
        p_refs, small_ref, head_ref = refs[:n], refs[n], refs[n + 1]
        outs = refs[n + 2:]
        g_refs, total_ref = outs[:n], outs[n]
        scr = outs[n + 1:]
        from_sib = scr[0:n]
        chip_out = scr[n:2 * n]
        chip_in = scr[2 * n:3 * n]
        pack, pack_sib, fine, bulk, total_scr = scr[3 * n:3 * n + 5]
        send_a, recv_a, send_b, recv_b, send_s, recv_s = scr[3 * n + 5:]
        step = pl.program_id(0)
        x, y, c = _place()
        me = _index((x, y, c))
        sibling = (x, y, 1 - c)
        my_chip = 2 * x + y
        others = [(1 - x, y), (x, 1 - y), (1 - x, 1 - y)]
        my_half = pl.ds(pl.multiple_of(PACK_HALF * c, 8), PACK_HALF)

        def pack_to_sibling():
            return pltpu.make_async_remote_copy(
                src_ref=pack, dst_ref=pack_sib, send_sem=send_s.at[0], recv_sem=recv_s.at[0],
                device_id=sibling, device_id_type=MESH)

        def half_to_chip(r, part):
            buf = (fine, bulk)[part]
            return pltpu.make_async_remote_copy(
                src_ref=buf.at[my_chip], dst_ref=buf.at[my_chip],
                send_sem=send_s.at[1 + 3 * part + r], recv_sem=recv_s.at[1 + 3 * part + r],
                device_id=(*others[r], c), device_id_type=MESH)

        def half_from_chip(r, part):
            k = 2 * others[r][0] + others[r][1]
            buf = (fine, bulk)[part]
            return pltpu.make_async_remote_copy(
                src_ref=buf.at[k], dst_ref=buf.at[k],
                send_sem=send_s.at[1 + 3 * part + r], recv_sem=recv_s.at[1 + 3 * part + r],
                device_id=(*others[r], c), device_id_type=MESH)

        def total_to_sibling():
            return pltpu.make_async_remote_copy(
                src_ref=total_scr.at[my_half], dst_ref=total_scr.at[my_half],
                send_sem=send_s.at[7], recv_sem=recv_s.at[7], device_id=sibling, device_id_type=MESH)

        def total_from_sibling():
            sib_half = pl.ds(pl.multiple_of(PACK_HALF * (1 - c), 8), PACK_HALF)
            return pltpu.make_async_remote_copy(
                src_ref=total_scr.at[sib_half], dst_ref=total_scr.at[sib_half],
                send_sem=send_s.at[7], recv_sem=recv_s.at[7], device_id=sibling, device_id_type=MESH)

        def to_sibling(a, k):
            return pltpu.make_async_remote_copy(
                src_ref=p_refs[a].at[2 * k + (1 - c)], dst_ref=from_sib[a].at[k],
                send_sem=send_a.at[a], recv_sem=recv_a.at[a], device_id=sibling, device_id_type=MESH)

        def all_from_sibling(a):
            return pltpu.make_async_remote_copy(
                src_ref=from_sib[a], dst_ref=from_sib[a], send_sem=send_a.at[a], recv_sem=recv_a.at[a],
                device_id=sibling, device_id_type=MESH)

        def to_chip(a, r):
            return pltpu.make_async_remote_copy(
                src_ref=chip_out[a].at[r], dst_ref=chip_in[a].at[r],
                send_sem=send_b.at[3 * a + r], recv_sem=recv_b.at[3 * a + r],
                device_id=(*others[r], c), device_id_type=MESH)

        @pl.when(step == 0)
        def _():
            pack[0:PACK_FINE, :] = jnp.zeros((PACK_FINE, D), F32)
            pack[PK_TABLE_B:PK_MISC, :] = jnp.zeros((PK_MISC - PK_TABLE_B, D), F32)
            pack[pl.ds(pl.multiple_of(_table_row(me), 8), 8), :] = small_ref[0:8, :] + head_ref[0:8, :]
            pack[PK_WS:PK_WS + 64, :] = small_ref[ROW_WS:ROW_WS + 64, :]
            pack[PK_MISC:PK_MISC + 8, :] = small_ref[ROW_N1PRE:ROW_N1PRE + 8, :] + head_ref[8:16, :]
            pack[PK_BS:PK_BS + 8, :] = small_ref[ROW_BS:ROW_BS + 8, :]
            pack[PK_WP:PK_WP + 64, :] = small_ref[ROW_WP:ROW_WP + 64, :]
            pack_to_sibling().start()
            for a in range(n):
                for k in range(4):
                    to_sibling(a, k).start()

        @pl.when(step == 1)
        def _():
            pack_to_sibling().wait_recv()
            chip_sum = pack[my_half, :] + pack_sib[my_half, :]
            fine[my_chip] = chip_sum[:PACK_FINE, :]
            bulk[my_chip] = chip_sum[PACK_FINE:, :].astype(BF16)
            for r in range(3):
                half_to_chip(r, 0).start()
                half_to_chip(r, 1).start()
            for a in range(n):
                all_from_sibling(a).wait_recv()
                rows = p_refs[a].shape[1]
                for r in range(3):
                    k = 2 * others[r][0] + others[r][1]
                    for s in range(0, rows, row_chunk):
                        sl = pl.ds(s, row_chunk)
                        chip_out[a][r, sl, :] = (p_refs[a][2 * k + c, sl, :].astype(F32)
                                                 + from_sib[a][k, sl, :].astype(F32)).astype(BF16)
                    to_chip(a, r).start()
                for s in range(0, rows, row_chunk):
                    sl = pl.ds(s, row_chunk)
                    g_refs[a][sl, :] = (p_refs[a][2 * my_chip + c, sl, :].astype(F32)
                                        + from_sib[a][my_chip, sl, :].astype(F32))

        @pl.when(step == TAIL_STEPS - 1)
        def _():
            for r in range(3):
                half_from_chip(r, 0).wait_recv()
                half_from_chip(r, 1).wait_recv()
            half_start = pl.multiple_of(PACK_HALF * c, 8)
            total_scr[pl.ds(half_start, PACK_FINE), :] = ((fine[0] + fine[1]) + fine[2]) + fine[3]
            total_scr[pl.ds(half_start + PACK_FINE, PACK_HALF - PACK_FINE), :] = (
                (bulk[0].astype(F32) + bulk[1].astype(F32)) + bulk[2].astype(F32)) + bulk[3].astype(F32)
            total_to_sibling().start()
            for a in range(n):
                rows = p_refs[a].shape[1]
                for r in range(3):
                    to_chip(a, r).wait_recv()
                    for s in range(0, rows, row_chunk):
                        sl = pl.ds(s, row_chunk)
                        g_refs[a][sl, :] = g_refs[a][sl, :] + chip_in[a][r, sl, :].astype(F32)
            total_from_sibling().wait_recv()
            total_ref[...] = total_scr[...]
            for a in range(n):
                all_from_sibling(a).wait_send()
                for r in range(3):
                    to_chip(a, r).wait_send()
            pack_to_sibling().wait_send()
            for r in range(3):
                half_to_chip(r, 0).wait_send()
                half_to_chip(r, 1).wait_send()
            total_to_sibling().wait_send()

    return pl.pallas_call(
        body, name="tail_comm", grid=(TAIL_STEPS,),
        out_shape=tuple([jax.ShapeDtypeStruct(p.shape[1:], F32) for p in parts]
                        + [jax.ShapeDtypeStruct((PACK_ROWS, D), F32)]),
        in_specs=[_resident(p.shape) for p in parts] + [_resident(small.shape), _resident(head.shape)],
        out_specs=tuple([_full(p.shape[1:]) for p in parts] + [_full((PACK_ROWS, D))]),
        scratch_shapes=(
            [pltpu.VMEM((4,) + p.shape[1:], BF16) for p in parts]
            + [pltpu.VMEM((3,) + p.shape[1:], BF16) for p in parts]
            + [pltpu.VMEM((3,) + p.shape[1:], BF16) for p in parts]
            + [pltpu.VMEM((PACK_ROWS, D), F32), pltpu.VMEM((PACK_ROWS, D), F32),
               pltpu.VMEM((4, PACK_FINE, D), F32), pltpu.VMEM((4, PACK_HALF - PACK_FINE, D), BF16),
               pltpu.VMEM((PACK_ROWS, D), F32)]
            + [pltpu.SemaphoreType.DMA((n,)), pltpu.SemaphoreType.DMA((n,)),
               pltpu.SemaphoreType.DMA((3 * n,)), pltpu.SemaphoreType.DMA((3 * n,)),
               pltpu.SemaphoreType.DMA((8,)), pltpu.SemaphoreType.DMA((8,))]),
        compiler_params=pltpu.CompilerParams(dimension_semantics=("arbitrary",), vmem_limit_bytes=VMEM_LIMIT),
    )(*parts, small, head)


def _tril_mask():
    row = lax.broadcasted_iota(jnp.int32, (CHUNK, CHUNK), 0)
    col = lax.broadcasted_iota(jnp.int32, (CHUNK, CHUNK), 1)
    return (col <= row).astype(F32)


def _window_sums(ext):
    s2 = ext + pltpu.roll(ext, 1, 0)
    t4 = s2[:, GROUP:]
    s4 = t4 + pltpu.roll(t4, 2, 0)
    t8 = s4[:, GROUP:]
    s8 = t8 + pltpu.roll(t8, 4, 0)
    t16 = s8[:, GROUP:]
    s16 = t16 + pltpu.roll(t16, 8, 0)
    return [s2[:, :GROUP], s4[:, :GROUP], s8[:, :GROUP], s16]


def _inv_counts(first_pos, rows):
    pos = first_pos + lax.broadcasted_iota(jnp.int32, (rows, 1), 0)
    return [1.0 / jnp.minimum(pos + 1, w).astype(F32) for w in WINDOWS]


def _pool_diff(zb, halo, first_pos):
    tt = zb.shape[0]
    sums = _window_sums(jnp.concatenate([halo, zb], axis=0))
    inv = _inv_counts(first_pos, tt)
    return [sums[g][HALO:, :] * inv[g] - zb[:, g * GROUP:(g + 1) * GROUP] for g in range(len(WINDOWS))]


def _row_blocks(scr, rows, place):
    def block(rel):
        start = pl.multiple_of(rows * _index(_peer(*place, rel)), rows)
        return scr.at[pl.ds(start, rows), :]

    def entries(shard_ref):
        return ([(shard_ref, block(0), rel) for rel in (1, 2, 4, 6)]
                + [(block(rel), block(rel), 1) for rel in (2, 4, 6)])
    return block, entries


def _attn_fwd(tt, x, c8, w_ada, b_my, n1pre, n1post, w_in_shard, w_out_shard, w_sp, bs_rows, ln_g, ln_b, w_pool,
              b_pool, pool_scale, fc_shards, n2pre):
    t_len = x.shape[0]
    nt = t_len // tt
    ncol = w_ada.shape[1]

    def body(x_ref, xb_ref, c_ref, wada_ref, b_ref, n1pre_ref, n1post_ref, wi_ref, wo_ref, wsp_ref, bs_ref,
             lng_ref, lnb_ref, wp_ref, bp_ref, ps_ref, w1_ref, w2_ref, n2pre_ref,
             z_ref, cat_ref, mix_ref, x1_ref, h2_ref, r_ref, f_ref, mod_out, sc_out, e1_ref, e2_ref, wout_ref,
             win_out, carry, land1, land2, sib1, sib2, cat_keep, wout_scr, win_ref, mod_ref, cg, mg, part,
             send_sems, recv_sems, local_sems, wo_send, wo_recv, wi_send, wi_recv, ada_send, ada_recv):
        i = pl.program_id(0)
        place = px, py, pc = _place()
        me = _index(place)
        wo_block, wo_entries = _row_blocks(wout_scr, w_out_shard.shape[0], place)
        wi_block, wi_entries = _row_blocks(win_ref, w_in_shard.shape[0], place)
        wo_copies = _Copies(wo_entries(wo_ref), wo_send, wo_recv)
        wi_copies = _Copies(wi_entries(wi_ref), wi_send, wi_recv)
        wo_keep = pltpu.make_async_copy(wout_scr, wout_ref, local_sems.at[8])
        wi_keep = pltpu.make_async_copy(win_ref, win_out, local_sems.at[9])
        ada = _Copies([(c_ref, cg.at[me], k) for k in range(1, N_DEV)]
                      + [(part, mg.at[me], k) for k in range(1, N_DEV)], ada_send, ada_recv)
        copies = _Copies(
            [(w1_ref, sib1, 1), (w2_ref, sib2, 1),
             (w1_ref, land1.at[0], 2), (w2_ref, land2.at[0], 2),
             (w1_ref, land1.at[1], 4), (w2_ref, land2.at[1], 4),
             (land1.at[0], e1_ref.at[3], 1), (land2.at[0], e2_ref.at[3], 1),
             (land1.at[1], e1_ref.at[5], 1), (land2.at[1], e2_ref.at[5], 1)],
            send_sems, recv_sems)
        keep = [pltpu.make_async_copy(w1_ref, e1_ref.at[0], local_sems.at[0]),
                pltpu.make_async_copy(w2_ref, e2_ref.at[0], local_sems.at[1]),
                pltpu.make_async_copy(land1.at[0], e1_ref.at[2], local_sems.at[2]),
                pltpu.make_async_copy(land1.at[1], e1_ref.at[4], local_sems.at[3]),
                pltpu.make_async_copy(land2.at[0], e2_ref.at[2], local_sems.at[4]),
                pltpu.make_async_copy(land2.at[1], e2_ref.at[4], local_sems.at[5]),
                pltpu.make_async_copy(sib1, e1_ref.at[1], local_sems.at[6]),
                pltpu.make_async_copy(sib2, e2_ref.at[1], local_sems.at[7])]

        @pl.when(i == 0)
        def _():
            ada.start(*range(N_DEV - 1))
            wi_copies.start(0, 1, 2, 3)
            cg[me] = c_ref[...]
            wi_rows, wo_rows = w_in_shard.shape[0], w_out_shard.shape[0]
            win_ref[pl.ds(pl.multiple_of(wi_rows * me, wi_rows), wi_rows), :] = wi_ref[...]
            wout_scr[pl.ds(pl.multiple_of(wo_rows * me, wo_rows), wo_rows), :] = wo_ref[...]
            carry[...] = jnp.zeros_like(carry)

            ada.wait_recv(*range(N_DEV - 1))
            c_all = jnp.concatenate([cg[j, 0:1, :] for j in range(N_DEV)], axis=0)
            sc = c_all * jax.nn.sigmoid(c_all)
            sc_out[...] = sc
            part[...] = _dot(sc.astype(BF16), wada_ref[...].astype(BF16)) + b_ref[...]
            ada.start(*range(N_DEV - 1, 2 * (N_DEV - 1)))
            wo_copies.start(0, 1, 2, 3)
            copies.start(0, 1, 2, 4, 3, 5)
            keep[0].start()
            keep[1].start()
            mg[me] = part[...]
            ada.wait_recv(*range(N_DEV - 1, 2 * (N_DEV - 1)))
            mod_ref[...] = jnp.zeros_like(mod_ref)
            for j in range(N_DEV):
                for m in range(6):
                    lo, hi = max(ncol * j, D * m), min(ncol * (j + 1), D * (m + 1))
                    if lo < hi:
                        mod_ref[m:m + 1, lo - D * m:hi - D * m] = mg[j, pl.ds(me, 1), lo - ncol * j:hi - ncol * j]
            mod_out[...] = mod_ref[...]

            wi_copies.wait_recv(1, 2, 3)
            wi_copies.start(4, 5, 6)
            wi_copies.wait_recv(0, 4, 5, 6)
            wi_keep.start()

        @pl.when(i == nt // 2 + ATTN_LAG)
        def _():
            copies.wait_recv(2, 4)
            copies.start(6, 8)
            keep[2].start()
            keep[3].start()

        @pl.when(i == nt - 1 + ATTN_LAG)
        def _():
            copies.wait_recv(3, 5)
            copies.start(7, 9)
            keep[4].start()
            keep[5].start()

        shift1, scale1, gate1 = mod_ref[0:1, :], mod_ref[1:2, :], mod_ref[2:3, :]

        @pl.when(i < nt)
        def _():
            xv = x_ref[...]
            h1 = (xv * _rstd(xv)) * (n1pre_ref[...] * (1.0 + scale1)) + shift1
            z = _dot_nt(h1.astype(BF16), win_ref[...])
            z_ref[...] = z

            _, ga = _gelu_parts(z[:, :2 * D_A])
            u, vr = ga[:, :D_A], ga[:, D_A:]
            dv = vr - jnp.mean(vr, axis=-1, keepdims=True)
            v = (dv * lax.rsqrt(jnp.mean(dv * dv, axis=-1, keepdims=True) + EPS)) * lng_ref[...] + lnb_ref[...]
            vb = v.astype(BF16)
            mask = _tril_mask()
            wc = [(wsp_ref[h] * mask).astype(BF16) for h in range(N_HEADS)]
            for ch in range(tt // CHUNK):
                rows = slice(ch * CHUNK, (ch + 1) * CHUNK)
                for h in range(N_HEADS):
                    cols = slice(h * GROUP, (h + 1) * GROUP)
                    mixed = _dot(wc[h], vb[rows, cols]) + bs_ref[:, cols]
                    cat_ref[rows, cols] = (u[rows, cols] * mixed).astype(BF16)

            zb = z[:, 2 * D_A:]
            diff = _pool_diff(zb, carry[...], i * tt)
            carry[...] = zb[tt - HALO:, :]
            for g in range(len(WINDOWS)):
                cols = slice(g * GROUP, (g + 1) * GROUP)
                pre = _dot(diff[g].astype(BF16), wp_ref[g].astype(BF16)) + bp_ref[:, cols]
                cat_ref[:, D_A + g * GROUP:D_A + (g + 1) * GROUP] = (pre * ps_ref[:, cols]).astype(BF16)
            cat_keep[i % (ATTN_LAG + 1)] = cat_ref[...]

        @pl.when(i == 0)
        def _():
            copies.wait_recv(0, 1)
            keep[6].start()
            keep[7].start()

        @pl.when(i == 1)
        def _():
            wo_copies.wait_recv(1, 2, 3)
            wo_copies.start(4, 5, 6)

        @pl.when(i == ATTN_LAG)
        def _():
            wo_copies.wait_recv(0, 4, 5, 6)
            wo_keep.start()

        @pl.when(i >= ATTN_LAG)
        def _():
            xv = xb_ref[...]
            mix = _dot(cat_keep[(i - ATTN_LAG) % (ATTN_LAG + 1)], wout_scr[...])
            mix_ref[...] = mix
            x1v = xv + (mix * _rstd(mix)) * (gate1 * n1post_ref[...])
            x1_ref[...] = x1v
            shift2, scale2 = mod_ref[3:4, :], mod_ref[4:5, :]
            h2 = ((x1v * _rstd(x1v)) * (n2pre_ref[...] * (1.0 + scale2)) + shift2).astype(BF16)
            h2_ref[...] = h2
            for j, (w1, w2) in enumerate(((w1_ref, w2_ref), (sib1, sib2))):
                ra = jnp.maximum(_dot(h2, w1[...]), 0.0)
                r = (ra * ra).astype(BF16)
                r_ref[:, j * FF_BLK:(j + 1) * FF_BLK] = r
                if j == 0:
                    f_ref[...] = _dot(r, w2[...])
                else:
                    f_ref[...] += _dot(r, w2[...])

        @pl.when(i == nt - 1 + ATTN_LAG)
        def _():
            copies.wait_recv(6, 7, 8, 9)
            copies.wait_send(*range(10))
            wo_copies.wait_send(*range(7))
            wi_copies.wait_send(*range(7))
            ada.wait_send(*range(2 * (N_DEV - 1)))
            for cp in keep:
                cp.wait()
            wo_keep.wait()
            wi_keep.wait()

    first = lambda w: pl.BlockSpec((tt, w), lambda i: (jnp.minimum(i, nt - 1), 0))
    second = lambda w: pl.BlockSpec((tt, w), lambda i: (jnp.maximum(i - ATTN_LAG, 0), 0))
    r_head = pl.BlockSpec((tt, FC_HEAD * FF_BLK),
                          lambda i: (jnp.maximum(i - ATTN_LAG, 0), R_HEAD_COLS // (FC_HEAD * FF_BLK)))
    hbm = pl.BlockSpec(memory_space=pl.ANY)
    outs = pl.pallas_call(
        body, name="attn_fwd", grid=(nt + ATTN_LAG,),
        out_shape=tuple([jax.ShapeDtypeStruct((t_len, D_Z), F32), jax.ShapeDtypeStruct((t_len, D), BF16),
                         jax.ShapeDtypeStruct((t_len, D), F32), jax.ShapeDtypeStruct((t_len, D), F32),
                         jax.ShapeDtypeStruct((t_len, D), BF16),
                         jax.ShapeDtypeStruct((t_len, FC_EARLY * FF_BLK), BF16),
                         jax.ShapeDtypeStruct((t_len, D), F32)]
                        + [jax.ShapeDtypeStruct((8, D), F32), jax.ShapeDtypeStruct((N_DEV, D), F32)]
                        + [jax.ShapeDtypeStruct((FC_EARLY,) + s.shape, BF16) for s in fc_shards]
                        + [jax.ShapeDtypeStruct((D, D), BF16), jax.ShapeDtypeStruct((D_Z, D), BF16)]),
        in_specs=[first(D), second(D), _full((8, D)), _resident(w_ada.shape), _full((1, ncol)), _full((1, D)),
                  _full((1, D)), _resident(w_in_shard.shape), _resident(w_out_shard.shape),
                  _full((N_HEADS, CHUNK, CHUNK)), _full((CHUNK, D_A)), _full((1, D_A)), _full((1, D_A)),
                  _full((len(WINDOWS), GROUP, GROUP)), _full((1, D_B)), _full((1, D_B)),
                  _resident(fc_shards[0].shape), _resident(fc_shards[1].shape), _full((1, D))],
        out_specs=(first(D_Z), first(D), second(D), second(D), second(D), r_head, second(D),
                   _full((8, D)), _full((N_DEV, D)), hbm, hbm, hbm, hbm),
        scratch_shapes=[pltpu.VMEM((HALO, D_B), F32),
                        pltpu.VMEM((2,) + fc_shards[0].shape, BF16), pltpu.VMEM((2,) + fc_shards[1].shape, BF16),
                        pltpu.VMEM(fc_shards[0].shape, BF16), pltpu.VMEM(fc_shards[1].shape, BF16),
                        pltpu.VMEM((ATTN_LAG + 1, tt, D), BF16), pltpu.VMEM((D, D), BF16),
                        pltpu.VMEM((D_Z, D), BF16), pltpu.VMEM((8, D), F32),
                        pltpu.VMEM((N_DEV, 8, D), F32), pltpu.VMEM((N_DEV, N_DEV, ncol), F32),
                        pltpu.VMEM((N_DEV, ncol), F32),
                        pltpu.SemaphoreType.DMA((10,)), pltpu.SemaphoreType.DMA((10,)),
                        pltpu.SemaphoreType.DMA((10,)),
                        pltpu.SemaphoreType.DMA((7,)), pltpu.SemaphoreType.DMA((7,)),
                        pltpu.SemaphoreType.DMA((7,)), pltpu.SemaphoreType.DMA((7,)),
                        pltpu.SemaphoreType.DMA((2 * (N_DEV - 1),)), pltpu.SemaphoreType.DMA((2 * (N_DEV - 1),))],
        compiler_params=pltpu.CompilerParams(dimension_semantics=("arbitrary",), vmem_limit_bytes=VMEM_LIMIT),
    )(x, x, c8, w_ada, b_my, n1pre, n1post, w_in_shard, w_out_shard, w_sp, bs_rows, ln_g, ln_b, w_pool, b_pool,
      pool_scale, *fc_shards, n2pre)
    return outs[:9], outs[9:11], outs[11], outs[12]


def _mlp_fwd_early(tt, r_begun, h2, f_head, w1_early, w2_early):
    t_len = h2.shape[0]
    nt = t_len // tt
    n_late = N_DEV - FC_EARLY

    def body(r_begun_ref, h2_ref, fh_ref, w1_ref, w2_ref, r_ref, f_ref, l1_ref, l2_ref,
             land1, land2, send_sems, recv_sems, local_sems):
        i = pl.program_id(0)
        copies = _Copies(
            [(w1_ref.at[2], land1, 4), (w2_ref.at[4], land2, 2),
             (land1, l1_ref.at[1], 1), (land2, l2_ref.at[1], 1)],
            send_sems, recv_sems)
        keep = [pltpu.make_async_copy(land1, l1_ref.at[0], local_sems.at[0]),
                pltpu.make_async_copy(land2, l2_ref.at[0], local_sems.at[1])]

        @pl.when(i == 0)
        def _():
            copies.start(0, 1)

        @pl.when(i == nt - 1)
        def _():
            copies.wait_recv(0, 1)
            copies.start(2, 3)
            for cp in keep:
                cp.start()

        h2 = h2_ref[...]
        f_ref[...] = fh_ref[...]
        for j in range(FC_HEAD, FC_EARLY):
            ra = jnp.maximum(_dot(h2, w1_ref[j]), 0.0)
            r = (ra * ra).astype(BF16)
            r_ref[:, _early_col(j):_early_col(j) + FF_BLK] = r
            f_ref[...] += _dot(r, w2_ref[j])

        @pl.when(i == nt - 1)
        def _():
            copies.wait_recv(2, 3)
            copies.wait_send(0, 1, 2, 3)
            for cp in keep:
                cp.wait()

    tile = lambda w: pl.BlockSpec((tt, w), lambda i: (i, 0))
    hbm = pl.BlockSpec(memory_space=pl.ANY)
    outs = pl.pallas_call(
        body, name="mlp_fwd_early", grid=(nt,),
        out_shape=(jax.ShapeDtypeStruct((t_len, FC_EARLY * FF_BLK), BF16), jax.ShapeDtypeStruct((t_len, D), F32),
                   jax.ShapeDtypeStruct((n_late,) + w1_early.shape[1:], BF16),
                   jax.ShapeDtypeStruct((n_late,) + w2_early.shape[1:], BF16)),
        in_specs=[hbm, tile(D), tile(D), _resident((FC_EARLY, D, FF_BLK)), _resident((FC_EARLY, FF_BLK, D))],
        out_specs=(tile(R_HEAD_COLS), tile(D), hbm, hbm),
        input_output_aliases={0: 0},
        scratch_shapes=[pltpu.VMEM(w1_early.shape[1:], BF16), pltpu.VMEM(w2_early.shape[1:], BF16),
                        pltpu.SemaphoreType.DMA((4,)), pltpu.SemaphoreType.DMA((4,)),
                        pltpu.SemaphoreType.DMA((2,))],
        compiler_params=pltpu.CompilerParams(dimension_semantics=("arbitrary",), vmem_limit_bytes=VMEM_LIMIT),
    )(r_begun, h2, f_head, w1_early, w2_early)
    return outs[:2], outs[2:]


def _mlp_late_bwd(tt, r_early, x1, h2, f_early, tgt, mix, mod, n2pre, n2post, n1post,
                  w1_early, w2_early, w1_late, w2_late):
    t_len = x1.shape[0]
    nt = t_len // tt
    n_late = N_DEV - FC_EARLY
    late_cols = n_late * FF_BLK

    def body(re_ref, x1_ref, h2_ref, fe_ref, tgt_ref, mix_ref, mod_ref, n2pre_ref, n2post_ref,
             n1post_ref, w1e_ref, w2e_ref, w1l_ref, w2l_ref,
             rl_ref, df_ref, da_ref, dmix_ref, dx1_ref, redf_ref, redb_ref, dh2_acc):
        i = pl.program_id(0)

        @pl.when(i == 0)
        def _():
            redf_ref[...] = jnp.zeros_like(redf_ref)
            redb_ref[...] = jnp.zeros_like(redb_ref)

        x1v = x1_ref[...]
        gate1, scale2, gate2 = mod_ref[2:3, :], mod_ref[4:5, :], mod_ref[5:6, :]
        h2 = h2_ref[...]
        f = fe_ref[...]
        for j in range(n_late):
            cols = slice(j * FF_BLK, (j + 1) * FF_BLK)
            ra = jnp.maximum(_dot(h2, w1l_ref[j]), 0.0)
            r = (ra * ra).astype(BF16)
            rl_ref[:, cols] = r
            f = f + _dot(r, w2l_ref[j])
        post2 = n2post_ref[...]
        gate_post2 = gate2 * post2
        rf = _rstd(f)
        fhat = f * rf
        err = (x1v + fhat * gate_post2) - tgt_ref[...]
        dy = err * (1.0 / D)
        d_f, sum_f = _rms_bwd_gained(dy, gate_post2, fhat, rf)
        dfv = d_f.astype(BF16)
        df_ref[...] = dfv
        redf_ref[0:1, :] += post2 * sum_f
        redf_ref[1:2, :] += gate2 * sum_f
        redf_ref[2:3, :] += _colsum(err * err)

        for j in range(N_DEV):
            cols = slice(j * FF_BLK, (j + 1) * FF_BLK)
            if j < FC_EARLY:
                w1, w2, r = w1e_ref[j], w2e_ref[j], re_ref[:, _early_col(j):_early_col(j) + FF_BLK]
            else:
                jl = j - FC_EARLY
                w1, w2, r = w1l_ref[jl], w2l_ref[jl], rl_ref[:, jl * FF_BLK:(jl + 1) * FF_BLK]
            dr = _dot_nt(dfv, w2)
            da = (dr * (2.0 * jnp.sqrt(r.astype(F32)))).astype(BF16)
            da_ref[:, cols] = da
            contrib = _dot_nt(da, w1)
            if j == 0:
                dh2_acc[...] = contrib
            else:
                dh2_acc[...] += contrib
        dh2 = dh2_acc[...]
        pre2, post1 = n2pre_ref[...], n1post_ref[...]
        r2 = _rstd(x1v)
        xhat = x1v * r2
        d_x1, sum_h = _rms_bwd_gained(dh2, pre2 * (1.0 + scale2), xhat, r2)
        dx1 = dy + d_x1
        dx1_ref[...] = dx1
        mixv = mix_ref[...]
        rm = _rstd(mixv)
        mhat = mixv * rm
        d_mix, sum_m = _rms_bwd_gained(dx1, gate1 * post1, mhat, rm)
        dmix_ref[...] = d_mix.astype(BF16)
        redb_ref[0:1, :] += _colsum(dh2)
        redb_ref[1:2, :] += pre2 * sum_h
        redb_ref[2:3, :] += (1.0 + scale2) * sum_h
        redb_ref[3:4, :] += post1 * sum_m
        redb_ref[4:5, :] += gate1 * sum_m

    tile = lambda w: pl.BlockSpec((tt, w), lambda i: (i, 0))
    return pl.pallas_call(
        body, name="mlp_late_bwd", grid=(nt,),
        out_shape=(jax.ShapeDtypeStruct((t_len, late_cols), BF16), jax.ShapeDtypeStruct((t_len, D), BF16),
                   jax.ShapeDtypeStruct((t_len, D_FF), BF16), jax.ShapeDtypeStruct((t_len, D), BF16),
                   jax.ShapeDtypeStruct((t_len, D), F32), jax.ShapeDtypeStruct((8, D), F32),
                   jax.ShapeDtypeStruct((8, D), F32)),
        in_specs=[tile(FC_EARLY * FF_BLK), tile(D), tile(D), tile(D), tile(D),
                  tile(D), _full((8, D)), _full((1, D)), _full((1, D)), _full((1, D)),
                  _resident((FC_EARLY, D, FF_BLK)), _resident((FC_EARLY, FF_BLK, D)),
                  _resident((n_late, D, FF_BLK)), _resident((n_late, FF_BLK, D))],
        out_specs=(tile(late_cols), tile(D), tile(D_FF), tile(D), tile(D), _full((8, D)), _full((8, D))),
        scratch_shapes=[pltpu.VMEM((tt, D), F32)],
        compiler_params=pltpu.CompilerParams(dimension_semantics=("arbitrary",), vmem_limit_bytes=VMEM_LIMIT),
    )(r_early, x1, h2, f_early, tgt, mix, mod, n2pre, n2post, n1post, w1_early, w2_early, w1_late, w2_late)


def _mlp_wgrad(tt, r_early, r_late, da, df, h2):
    t_len = df.shape[0]
    nt = t_len // tt
    odd_steps = [j for j, rel in enumerate(WGRAD_ORDER) if rel % 2]

    def relation(j):
        rel = jnp.int32(WGRAD_ORDER[-1])
        for step in range(N_DEV - 2, -1, -1):
            rel = jnp.where(j == step, WGRAD_ORDER[step], rel)
        return rel

    def body(re_ref, rl_ref, da_ref, df_ref, h2_ref, own1_ref, own2_ref, out1_ref, out2_ref, diag1_ref, diag2_ref,
             acc1, acc2, snd1, snd2, sib1, sib2, dsnd1, dsnd2, send_sems, recv_sems):
        j, t = pl.program_id(0), pl.program_id(1)
        rows = pl.ds(pl.multiple_of(t * tt, tt), tt)
        x, y, c = _place()
        accs, snds, sibs = (acc1, acc2), (snd1, snd2), (sib1, sib2)
        dsnds, diags = (dsnd1, dsnd2), (diag1_ref, diag2_ref)

        def to_sibling(a, jj, buf=0):
            return pltpu.make_async_remote_copy(
                src_ref=snds[a].at[buf], dst_ref=sibs[a].at[jj],
                send_sem=send_sems.at[4 * a + jj], recv_sem=recv_sems.at[4 * a + jj],
                device_id=(x, y, 1 - c), device_id_type=MESH)

        def to_diagonal(a):
            return pltpu.make_async_remote_copy(
                src_ref=dsnds[a], dst_ref=diags[a], send_sem=send_sems.at[8 + a], recv_sem=recv_sems.at[8 + a],
                device_id=_peer(x, y, c, 6), device_id_type=MESH)

        @pl.when(t == 0)
        def _():
            acc2[...] = jnp.zeros_like(acc2)
            acc1[...] = jnp.zeros_like(acc1)

        for r_ref, mine in ((re_ref, relation(j) < FC_EARLY), (rl_ref, relation(j) >= FC_EARLY)):
            @pl.when(mine)
            def _():
                acc2[...] += _dot_tn(r_ref[...], df_ref[rows, :])
                acc1[...] += _dot_tn(h2_ref[rows, :], da_ref[...])

        for step, rel in enumerate(WGRAD_ORDER):
            jj = rel // 2

            @pl.when((t == nt - 1) & (j == step))
            def _():
                for a, (own_ref, out_ref) in enumerate(((own1_ref, out1_ref), (own2_ref, out2_ref))):
                    if rel % 2:
                        q = odd_steps.index(step)
                        if q >= 2:
                            to_sibling(a, WGRAD_ORDER[odd_steps[q - 2]] // 2).wait_send()
                        snds[a][q % 2] = accs[a][...].astype(BF16)
                        to_sibling(a, jj, q % 2).start()
                        continue
                    to_sibling(a, jj).wait_recv()
                    chip_sum = accs[a][...] + sibs[a][jj].astype(F32)
                    if rel == 6:
                        dsnds[a][...] = chip_sum.astype(BF16)
                        to_diagonal(a).start()
                    elif rel == 0:
                        own_ref[...] = chip_sum
                    else:
                        out_ref[0] = chip_sum.astype(BF16)
                    if step == N_DEV - 1:
                        for q in (2, 3):
                            to_sibling(a, WGRAD_ORDER[odd_steps[q]] // 2).wait_send()
                        to_diagonal(a).wait_recv()
                        to_diagonal(a).wait_send()

    assert WGRAD_ORDER[-1] == 0 and WGRAD_ORDER[-3:-1] == (2, 4)
    blk = pl.BlockSpec((tt, FF_BLK), lambda j, t: (t, relation(j)))
    early_block = lambda rel: jnp.where(rel < FC_HEAD, rel + FC_EARLY - FC_HEAD, rel - FC_HEAD)
    early = lambda j, t: (jnp.where(relation(j) < FC_EARLY, t, 0),
                          jnp.where(relation(j) < FC_EARLY, early_block(relation(j)), 0))
    late = lambda j, t: (jnp.where(relation(j) < FC_EARLY, 0, t), jnp.maximum(relation(j) - FC_EARLY, 0))
    chip = lambda j, t: (jnp.clip(j - 5, 0, 1), 0, 0)
    hbm = pl.BlockSpec(memory_space=pl.ANY)
    return pl.pallas_call(
        body, name="mlp_wgrad", grid=(N_DEV, nt),
        out_shape=(jax.ShapeDtypeStruct((D, FF_BLK), F32), jax.ShapeDtypeStruct((FF_BLK, D), F32),
                   jax.ShapeDtypeStruct((2, D, FF_BLK), BF16), jax.ShapeDtypeStruct((2, FF_BLK, D), BF16),
                   jax.ShapeDtypeStruct((D, FF_BLK), BF16), jax.ShapeDtypeStruct((FF_BLK, D), BF16)),
        in_specs=[pl.BlockSpec((tt, FF_BLK), early), pl.BlockSpec((tt, FF_BLK), late), blk,
                  _resident((t_len, D)), _resident((t_len, D))],
        out_specs=(_full((D, FF_BLK)), _full((FF_BLK, D)),
                   pl.BlockSpec((1, D, FF_BLK), chip), pl.BlockSpec((1, FF_BLK, D), chip), hbm, hbm),
        scratch_shapes=[pltpu.VMEM((D, FF_BLK), F32), pltpu.VMEM((FF_BLK, D), F32),
                        pltpu.VMEM((2, D, FF_BLK), BF16), pltpu.VMEM((2, FF_BLK, D), BF16),
                        pltpu.VMEM((4, D, FF_BLK), BF16), pltpu.VMEM((4, FF_BLK, D), BF16),
                        pltpu.VMEM((D, FF_BLK), BF16), pltpu.VMEM((FF_BLK, D), BF16),
                        pltpu.SemaphoreType.DMA((10,)), pltpu.SemaphoreType.DMA((10,))],
        compiler_params=pltpu.CompilerParams(dimension_semantics=("arbitrary", "arbitrary"),
                                             vmem_limit_bytes=VMEM_LIMIT),
    )(r_early, r_late, da, df, h2)


def _acc_rows(ref, row0, k, val):
    half = CHUNK // 2
    ref[row0:row0 + half, k * GROUP:(k + 1) * GROUP] += val[:half, :]
    ref[row0:row0 + half, D_A + k * GROUP:D_A + (k + 1) * GROUP] += val[half:, :]


def _attn_bwd(tt, dmix, x, z, cat, mod, n1pre, w_out, w_sp, bs_rows, ln_g, ln_b, w_pool, b_pool, pool_scale,
              red_fwd, red_bwd, chip_sums):
    t_len = z.shape[0]
    nt = t_len // tt
    hb = tt // HALO
    n_sums = len(chip_sums)

    def body(dmix_ref, x_ref, z_ref, zprev_ref, cat_ref, mod_ref, n1pre_ref, wout_ref, wsp_ref,
             bs_ref, lng_ref, lnb_ref, wp_ref, bp_ref, ps_ref, redf_ref, redb_ref, *rest):
        sum_out = rest[:n_sums]
        dz_ref, gwin_ref, gwout_ref, small_ref = rest[n_sums:n_sums + 4]
        sum_in = rest[n_sums + 4:2 * n_sums + 4]
        carry, acc_in, acc_out, dz_scr, bs_acc, send_sems, recv_sems = rest[2 * n_sums + 4:]
        s = pl.program_id(0)
        i = nt - 1 - s
        px, py, pc = _place()

        def chip_copy(a, r):
            return pltpu.make_async_remote_copy(
                src_ref=sum_out[a].at[r], dst_ref=sum_in[a].at[r],
                send_sem=send_sems.at[2 * a + r], recv_sem=recv_sems.at[2 * a + r],
                device_id=_peer(px, py, pc, 2 * (r + 1)), device_id_type=MESH)

        @pl.when(s == 0)
        def _():
            for a in range(n_sums):
                for r in range(2):
                    chip_copy(a, r).start()
            carry[...] = jnp.zeros_like(carry)
            acc_in[...] = jnp.zeros_like(acc_in)
            acc_out[...] = jnp.zeros_like(acc_out)
            bs_acc[...] = jnp.zeros_like(bs_acc)
            small_ref[...] = jnp.zeros_like(small_ref)
            small_ref[ROW_DMOD + 2:ROW_DMOD + 3, :] = redb_ref[3:4, :]
            small_ref[ROW_DMOD + 3:ROW_DMOD + 5, :] = redb_ref[0:2, :]
            small_ref[ROW_DMOD + 5:ROW_DMOD + 6, :] = redf_ref[0:1, :]
            small_ref[ROW_N1POST:ROW_N1POST + 1, :] = redb_ref[4:5, :]
            small_ref[ROW_N2PRE:ROW_N2PRE + 1, :] = redb_ref[2:3, :]
            small_ref[ROW_N2POST:ROW_N2POST + 1, :] = redf_ref[1:2, :]
            small_ref[ROW_LOSS:ROW_LOSS + 1, :] = redf_ref[2:3, :]

        dmixv = dmix_ref[...]
        dcat = _dot_nt(dmixv, wout_ref[...])
        acc_out[...] += _dot_tn(cat_ref[...], dmixv)

        z = z_ref[...]
        t_g, ga = _gelu_parts(z[:, :2 * D_A])
        u, vr = ga[:, :D_A], ga[:, D_A:]
        dv0 = vr - jnp.mean(vr, axis=-1, keepdims=True)
        rv = lax.rsqrt(jnp.mean(dv0 * dv0, axis=-1, keepdims=True) + EPS)
        vhat = dv0 * rv
        vb = (vhat * lng_ref[...] + lnb_ref[...]).astype(BF16)
        mask = _tril_mask()
        wc = [(wsp_ref[h] * mask).astype(BF16) for h in range(N_HEADS)]

        dya = dcat[:, :D_A]
        for h in range(N_HEADS):
            cols = slice(h * GROUP, (h + 1) * GROUP)
            bs_sum = jnp.zeros((CHUNK, GROUP), F32)
            ws_sum = jnp.zeros((CHUNK, CHUNK), F32)
            for ch in range(tt // CHUNK):
                rows = slice(ch * CHUNK, (ch + 1) * CHUNK)
                v_ch = vb[rows, cols]
                mixed = _dot(wc[h], v_ch) + bs_ref[:, cols]
                dy_ch = dya[rows, cols]
                dz_scr[rows, cols] = dy_ch * mixed
                dmixed = dy_ch * u[rows, cols]
                dmb = dmixed.astype(BF16)
                dz_scr[rows, D_A + h * GROUP:D_A + (h + 1) * GROUP] = _dot_tn(wc[h], dmb)
                bs_sum = bs_sum + dmixed
                ws_sum = ws_sum + _dot_nt(dmb, v_ch)
            _acc_rows(bs_acc, 0, h, bs_sum)
            _acc_rows(small_ref, ROW_WS, h, ws_sum)

        dvl = dz_scr[:, D_A:2 * D_A]
        dvhat = dvl * lng_ref[...]
        dvl_vhat = dvl * vhat
        dvr = rv * (dvhat - jnp.mean(dvhat, axis=-1, keepdims=True)
                    - vhat * jnp.mean(dvl_vhat * lng_ref[...], axis=-1, keepdims=True))
        small_ref[ROW_LN:ROW_LN + 1, 0:D_A] += _colsum(dvl_vhat)
        small_ref[ROW_LN:ROW_LN + 1, D_A:D] += _colsum(dvl)
        dga = jnp.concatenate([dz_scr[:, :D_A], dvr], axis=1)
        dza = dga * _gelu_grad(z[:, :2 * D_A], t_g)

        zb = z[:, 2 * D_A:]
        halo_prev = jnp.where(i == 0, 0.0, zprev_ref[...])
        diff = _pool_diff(zb, halo_prev, i * tt)
        dyb = dcat[:, D_A:]
        inv = _inv_counts(i * tt, tt)
        scaled, ddiffs = [], []
        for g in range(len(WINDOWS)):
            cols = slice(g * GROUP, (g + 1) * GROUP)
            db = diff[g].astype(BF16)
            wpg = wp_ref[g].astype(BF16)
            pre = _dot(db, wpg) + bp_ref[:, cols]
            small_ref[ROW_POOL:ROW_POOL + 1, cols] += _colsum(dyb[:, cols] * pre)
            dpre = dyb[:, cols] * ps_ref[:, cols]
            small_ref[ROW_POOL:ROW_POOL + 1, D_B + g * GROUP:D_B + (g + 1) * GROUP] += _colsum(dpre)
            dpb = dpre.astype(BF16)
            _acc_rows(small_ref, ROW_WP, g, _dot_tn(db, dpb))
            ddiff = _dot_nt(dpb, wpg)
            ddiffs.append(ddiff)
            scaled.append(ddiff * inv[g])
        scaled_all = jnp.concatenate(scaled, axis=1)
        ext = jnp.concatenate([scaled_all, carry[...]], axis=0)
        n_ext = tt + HALO
        s2 = ext + pltpu.roll(ext, n_ext - 1, 0)
        t4 = s2[:, GROUP:]
        s4 = t4 + pltpu.roll(t4, n_ext - 2, 0)
        t8 = s4[:, GROUP:]
        s8 = t8 + pltpu.roll(t8, n_ext - 4, 0)
        t16 = s8[:, GROUP:]
        s16 = t16 + pltpu.roll(t16, n_ext - 8, 0)
        back = [s2[:, :GROUP], s4[:, :GROUP], s8[:, :GROUP], s16]
        carry[...] = scaled_all[:HALO, :]
        dzb = jnp.concatenate([back[g][:tt, :] - ddiffs[g] for g in range(len(WINDOWS))], axis=1)

        dzv = jnp.concatenate([dza, dzb], axis=1).astype(BF16)
        dz_ref[...] = dzv
        xv = x_ref[...]
        h1 = (xv * _rstd(xv) * (n1pre_ref[...] * (1.0 + mod_ref[1:2, :])) + mod_ref[0:1, :]).astype(BF16)
        acc_in[...] += _dot_tn(dzv, h1)

        @pl.when(s == nt - 1)
        def _():
            gwin_ref[...] = acc_in[...].astype(BF16)
            gwout_ref[...] = acc_out[...].astype(BF16)
            bs = _unfold(bs_acc[...])
            for h in range(N_HEADS):
                small_ref[ROW_BS + h:ROW_BS + h + 1, 0:GROUP] = jnp.sum(
                    bs[:, h * GROUP:(h + 1) * GROUP].T, axis=0, keepdims=True)
            for a in range(n_sums):
                for r in range(2):
                    chip_copy(a, r).wait_recv()
                    chip_copy(a, r).wait_send()

    rev = lambda w: pl.BlockSpec((tt, w), lambda s: (nt - 1 - s, 0))
    zprev = pl.BlockSpec((HALO, D_B), lambda s: (jnp.maximum((nt - 1 - s) * hb - 1, 0), 2))
    hbm = pl.BlockSpec(memory_space=pl.ANY)
    outs = pl.pallas_call(
        body, name="attn_bwd", grid=(nt,),
        out_shape=tuple([jax.ShapeDtypeStruct((t_len, D_Z), BF16), jax.ShapeDtypeStruct((D_Z, D), BF16),
                         jax.ShapeDtypeStruct((D, D), BF16), jax.ShapeDtypeStruct((SMALL_ROWS, D), F32)]
                        + [jax.ShapeDtypeStruct(cs.shape, cs.dtype) for cs in chip_sums]),
        in_specs=[rev(D), rev(D), rev(D_Z), zprev, rev(D), _full((8, D)), _full((1, D)),
                  _resident((D, D)), _full((N_HEADS, CHUNK, CHUNK)), _full((CHUNK, D_A)),
                  _full((1, D_A)), _full((1, D_A)), _full((len(WINDOWS), GROUP, GROUP)), _full((1, D_B)),
                  _full((1, D_B)), _full((8, D)), _full((8, D))] + [_resident(cs.shape) for cs in chip_sums],
        out_specs=tuple([rev(D_Z), _resident((D_Z, D)), _resident((D, D)), _full((SMALL_ROWS, D))]
                        + [hbm] * n_sums),
        scratch_shapes=[pltpu.VMEM((HALO, D_B), F32), pltpu.VMEM((D_Z, D), F32), pltpu.VMEM((D, D), F32),
                        pltpu.VMEM((tt, 2 * D_A), F32), pltpu.VMEM((CHUNK // 2, D), F32),
                        pltpu.SemaphoreType.DMA((2 * n_sums,)), pltpu.SemaphoreType.DMA((2 * n_sums,))],
        compiler_params=pltpu.CompilerParams(dimension_semantics=("arbitrary",), vmem_limit_bytes=VMEM_LIMIT),
    )(dmix, x, z, z, cat, mod, n1pre, w_out, w_sp, bs_rows, ln_g, ln_b, w_pool, b_pool, pool_scale,
      red_fwd, red_bwd, *chip_sums)
    return outs[:4], outs[4:]


def _in_proj_bwd(tt, dz, dx1, x, mod, n1pre, w_in_t):
    t_len = x.shape[0]
    nt = t_len // tt

    def body(dz_ref, dx1_ref, x_ref, mod_ref, n1pre_ref, win_ref, gx_ref, sums_ref):
        i = pl.program_id(0)

        @pl.when(i == 0)
        def _():
            sums_ref[...] = jnp.zeros_like(sums_ref)

        dh1 = _dot(dz_ref[...], win_ref[...])
        xv = x_ref[...]
        r1 = _rstd(xv)
        xhat = xv * r1
        scale1 = mod_ref[1:2, :]
        pre1 = n1pre_ref[...]
        gain1 = pre1 * (1.0 + scale1)
        d_x, sum_h = _rms_bwd_gained(dh1, gain1, xhat, r1)
        gx_ref[...] = dx1_ref[...] + d_x
        sums_ref[ROW_DMOD:ROW_DMOD + 1, :] += _colsum(dh1)
        sums_ref[ROW_DMOD + 1:ROW_DMOD + 2, :] += pre1 * sum_h
        sums_ref[ROW_N1PRE:ROW_N1PRE + 1, :] += (1.0 + scale1) * sum_h

    tile = lambda w: pl.BlockSpec((tt, w), lambda i: (i, 0))
    return pl.pallas_call(
        body, name="in_proj_bwd", grid=(nt,),
        out_shape=(jax.ShapeDtypeStruct((t_len, D), F32), jax.ShapeDtypeStruct((16, D), F32)),
        in_specs=[tile(D_Z), tile(D), tile(D), _full((8, D)), _full((1, D)), _resident((D_Z, D))],
        out_specs=(tile(D), _full((16, D))),
        compiler_params=pltpu.CompilerParams(dimension_semantics=("arbitrary",), vmem_limit_bytes=VMEM_LIMIT),
    )(dz, dx1, x, mod, n1pre, w_in_t)


def _adam(w, g, m, v):
    m2 = ADAM_B1 * m + (1.0 - ADAM_B1) * g
    v2 = ADAM_B2 * v + (1.0 - ADAM_B2) * (g * g)
    m_hat = m2 / (1.0 - ADAM_B1 ** ADAM_STEP)
    v_hat = v2 / (1.0 - ADAM_B2 ** ADAM_STEP)
    delta = -ADAM_LR * (m_hat / (jnp.sqrt(v_hat) + ADAM_EPS) + ADAM_WD * w)
    return delta, m2, v2


def _adamw_shard(name, rb, w, g, m, v):
    rows, cols = w.shape

    def body(w_ref, g_ref, m_ref, v_ref, d_ref, m2_ref, v2_ref):
        d_ref[...], m2_ref[...], v2_ref[...] = _adam(w_ref[...], g_ref[...], m_ref[...], v_ref[...])

    blk = pl.BlockSpec((rb, cols), lambda i: (i, 0))
    shp = jax.ShapeDtypeStruct((rows, cols), F32)
    return pl.pallas_call(
        body, name=name, grid=(rows // rb,), out_shape=(shp, shp, shp),
        in_specs=[blk] * 4, out_specs=(blk, blk, blk),
        compiler_params=pltpu.CompilerParams(dimension_semantics=("arbitrary",)),
    )(w, g, m, v)


def _adamw_fc(steps, fc):
    n_fc = len(fc)

    def body(*refs):
        ins, outs = refs[:6 * n_fc], refs[6 * n_fc:]
        for k in range(n_fc):
            w_ref, own_ref, arr_ref, diag_ref, m_ref, v_ref = ins[6 * k:6 * k + 6]
            g = ((own_ref[...] + arr_ref[0].astype(F32)) + arr_ref[1].astype(F32)) + diag_ref[...].astype(F32)
            outs[4 * k][...] = g
            outs[4 * k + 1][...], outs[4 * k + 2][...], outs[4 * k + 3][...] = _adam(
                w_ref[...], g, m_ref[...], v_ref[...])

    specs_in, specs_out, shapes, args = [], [], [], []
    for w, own, arrived, diagonal, m, v in fc:
        rows, cols = w.shape
        blk = pl.BlockSpec((rows // steps, cols), lambda i: (i, 0))
        specs_in += [blk, blk, pl.BlockSpec((2, rows // steps, cols), lambda i: (0, i, 0)), blk, blk, blk]
        specs_out += [blk] * 4
        shapes += [jax.ShapeDtypeStruct((rows, cols), F32)] * 4
        args += [w, own, arrived, diagonal, m, v]
    outs = pl.pallas_call(
        body, name="adamw_fc", grid=(steps,), out_shape=tuple(shapes), in_specs=specs_in, out_specs=tuple(specs_out),
        compiler_params=pltpu.CompilerParams(dimension_semantics=("arbitrary",), vmem_limit_bytes=VMEM_LIMIT),
    )(*args)
    return [outs[4 * k:4 * k + 4] for k in range(n_fc)]


def _adamw_ada(rb, w, sc, dmod_cols, m, v):
    rows, cols = w.shape

    def body(w_ref, sc_ref, dm_ref, m_ref, v_ref, g_ref, d_ref, m2_ref, v2_ref):
        g = _dot_tn(sc_ref[...].astype(BF16), dm_ref[...].astype(BF16))
        g_ref[...] = g
        d_ref[...], m2_ref[...], v2_ref[...] = _adam(w_ref[...], g, m_ref[...], v_ref[...])

    blk = pl.BlockSpec((rb, cols), lambda i: (i, 0))
    shp = jax.ShapeDtypeStruct((rows, cols), F32)
    return pl.pallas_call(
        body, name="adamw_ada", grid=(rows // rb,), out_shape=(shp, shp, shp, shp),
        in_specs=[blk, pl.BlockSpec((N_DEV, rb), lambda i: (0, i)), _full((N_DEV, cols)), blk, blk],
        out_specs=(blk, blk, blk, blk),
        compiler_params=pltpu.CompilerParams(dimension_semantics=("arbitrary",)),
    )(w, sc, dmod_cols, m, v)


def _unfold(acc_rows):
    return jnp.concatenate([acc_rows[:, :D_A], acc_rows[:, D_A:]], axis=0)


def _adamw_small(total, params):
    n = len(params)
    flat = [a for p in params for a in p]

    def body(*refs):
        s_ref = refs[0]
        p_refs = refs[1:1 + 3 * n]
        loss_ref = refs[1 + 3 * n]
        o_refs = refs[2 + 3 * n:]
        d_b_ada = s_ref[0:6, :]
        for b in range(1, N_DEV):
            d_b_ada = d_b_ada + s_ref[_table_row(b):_table_row(b) + 6, :]
        misc = lambda r: s_ref[PK_MISC + r - ROW_N1PRE:PK_MISC + r - ROW_N1PRE + 1, :]
        loss = jnp.sum(misc(ROW_LOSS), axis=-1, keepdims=True) * (0.5 / D)
        loss_ref[...] = loss
        mask = _tril_mask()
        ws = _unfold(s_ref[PK_WS:PK_WS + 64, :])
        wp = _unfold(s_ref[PK_WP:PK_WP + 64, :])
        grads = [
            d_b_ada,
            misc(ROW_N1PRE), misc(ROW_N1POST), misc(ROW_N2PRE), misc(ROW_N2POST),
            misc(ROW_LN)[:, :D_A], misc(ROW_LN)[:, D_A:],
            misc(ROW_POOL)[:, :D_B], misc(ROW_POOL)[:, D_B:],
            s_ref[PK_BS:PK_BS + N_HEADS, 0:GROUP],
            jnp.stack([ws[:, h * GROUP:(h + 1) * GROUP] * mask for h in range(N_HEADS)]),
            jnp.stack([wp[:, g * GROUP:(g + 1) * GROUP] for g in range(len(WINDOWS))]),
        ]
        for k in range(n):
            w_ref, m_ref, v_ref = p_refs[3 * k:3 * k + 3]
            g = grads[k]
            if k == 0:
                for j in range(6):
                    o_refs[0][j] = g[j:j + 1, :]
                    o_refs[1][j], o_refs[2][j], o_refs[3][j] = _adam(w_ref[j], g[j:j + 1, :], m_ref[j], v_ref[j])
                continue
            o_refs[4 * k][...] = g
            o_refs[4 * k + 1][...], o_refs[4 * k + 2][...], o_refs[4 * k + 3][...] = _adam(
                w_ref[...], g, m_ref[...], v_ref[...])

    vm = pl.BlockSpec(memory_space=pltpu.VMEM)
    out_shape = [jax.ShapeDtypeStruct((1, 1), F32)]
    for w, _, _ in params:
        out_shape += [jax.ShapeDtypeStruct(w.shape, F32)] * 4
    return pl.pallas_call(
        body, name="adamw_small", out_shape=tuple(out_shape),
        in_specs=[vm] * (1 + 3 * n), out_specs=tuple([vm] * len(out_shape)),
    )(total, *flat)


TT_ATTN_FWD = 512
ATTN_LAG = 2
TT_MLP_FWD = 512
TT_MLP = 256
TT_WGRAD = 2048
TT_ATTN_BWD = 512
TT_IN_PROJ_BWD = 512


def kernel(x, c, w_ada, b_ada, norm1_pre, norm1_post, w_in, w_spatial, b_spatial, ln_v_gain, ln_v_bias, w_pool, b_pool, pool_scale, w_out, norm2_pre, norm2_post, w_fc1, w_fc2, loss_target, m_w_ada, m_b_ada, m_norm1_pre, m_norm1_post, m_w_in, m_w_spatial, m_b_spatial, m_ln_v_gain, m_ln_v_bias, m_w_pool, m_b_pool, m_pool_scale, m_w_out, m_norm2_pre, m_norm2_post, m_w_fc1, m_w_fc2, v_w_ada, v_b_ada, v_norm1_pre, v_norm1_post, v_w_in, v_w_spatial, v_b_spatial, v_ln_v_gain, v_ln_v_bias, v_w_pool, v_b_pool, v_pool_scale, v_w_out, v_norm2_pre, v_norm2_post, v_w_fc1, v_w_fc2):
    t_len = x.shape[1]
    me = 4 * lax.axis_index("x") + 2 * lax.axis_index("y") + lax.axis_index("c")
    ada_cols = w_ada.shape[1]
    tt = lambda want: min(want, t_len)

    x2 = x.reshape(t_len, D)
    tgt = loss_target.reshape(t_len, D)
    row = lambda a: a.reshape(1, -1)

    b_my = lax.dynamic_slice_in_dim(b_ada, me * ada_cols, ada_cols).reshape(1, ada_cols)
    w_in_shard, w_out_shard, w1_shard, w2_shard = _cast_shards([w_in.T, w_out, w_fc1, w_fc2])

    bs_rows = jnp.repeat(b_spatial.T, GROUP, axis=1)
    attn_consts = (w_spatial, bs_rows, row(ln_v_gain), row(ln_v_bias), w_pool, row(b_pool), row(pool_scale))

    (z, cat, mix, x1, h2, r_begun, f_head, mod, sc), (w1_early, w2_early), w_out_all, w_in_t = _attn_fwd(
        tt(TT_ATTN_FWD), x2, jnp.broadcast_to(c, (8, D)), w_ada, b_my, row(norm1_pre), row(norm1_post),
        w_in_shard, w_out_shard, *attn_consts, (w1_shard, w2_shard), row(norm2_pre))
    (r_early, f_early), (w1_late, w2_late) = _mlp_fwd_early(
        tt(TT_MLP_FWD), r_begun, h2, f_head, w1_early, w2_early)
    r_late, df, da, dmix, dx1, red_fwd, red_bwd = _mlp_late_bwd(
        tt(TT_MLP), r_early, x1, h2, f_early, tgt, mix, mod, row(norm2_pre), row(norm2_post), row(norm1_post),
        w1_early, w2_early, w1_late, w2_late)
    own_w1, own_w2, sums_w1, sums_w2, diag_w1, diag_w2 = _mlp_wgrad(tt(TT_WGRAD), r_early, r_late, da, df, h2)
    (dz, p_in, p_out, small), (arr_w1, arr_w2) = _attn_bwd(
        tt(TT_ATTN_BWD), dmix, x2, z, cat, mod, row(norm1_pre), w_out_all, *attn_consts, red_fwd, red_bwd,
        [sums_w1, sums_w2])
    grad_x, small_head = _in_proj_bwd(tt(TT_IN_PROJ_BWD), dz, dx1, x2, mod, row(norm1_pre), w_in_t)
    (grad_w1, d_w1, m_w1, v_w1), (grad_w2, d_w2, m_w2, v_w2) = _adamw_fc(
        4, [(w_fc1, own_w1, arr_w1, diag_w1, m_w_fc1, v_w_fc1), (w_fc2, own_w2, arr_w2, diag_w2, m_w_fc2, v_w_fc2)])
    grad_in_t, grad_out, total = _tail_comm(
        [p_in.reshape(N_DEV, D_Z // N_DEV, D), p_out.reshape(N_DEV, D // N_DEV, D)], small, small_head, 64)

    d_out, m_out, v_out = _adamw_shard("adamw_out", 128, w_out, grad_out, m_w_out, v_w_out)
    d_in_t, m_in_t, v_in_t = _adamw_shard("adamw_in", D_Z // N_DEV, w_in.T, grad_in_t, m_w_in.T, v_w_in.T)
    table = jnp.concatenate([total[0:PACK_FINE, :], total[PK_TABLE_B:PK_MISC, :]], axis=0)
    dmod_all = table.reshape(N_DEV, 8, D)[:, :6, :].reshape(N_DEV, 6 * D)
    dmod_cols = lax.dynamic_slice_in_dim(dmod_all, me * ada_cols, ada_cols, axis=1)
    grad_ada, d_ada, m_ada, v_ada = _adamw_ada(256, w_ada, sc, dmod_cols, m_w_ada, v_w_ada)

    six = lambda a: a.reshape(6, 1, D)
    small_params = [
        (six(b_ada), six(m_b_ada), six(v_b_ada)),
        (row(norm1_pre), row(m_norm1_pre), row(v_norm1_pre)),
        (row(norm1_post), row(m_norm1_post), row(v_norm1_post)),
        (row(norm2_pre), row(m_norm2_pre), row(v_norm2_pre)),
        (row(norm2_post), row(m_norm2_post), row(v_norm2_post)),
        (row(ln_v_gain), row(m_ln_v_gain), row(v_ln_v_gain)),
        (row(ln_v_bias), row(m_ln_v_bias), row(v_ln_v_bias)),
        (row(pool_scale), row(m_pool_scale), row(v_pool_scale)),
        (row(b_pool), row(m_b_pool), row(v_b_pool)),
        (b_spatial, m_b_spatial, v_b_spatial),
        (w_spatial, m_w_spatial, v_w_spatial),
        (w_pool, m_w_pool, v_w_pool),
    ]
    outs = _adamw_small(total, small_params)
    loss = outs[0].reshape(())
    names = ["b_ada", "norm1_pre", "norm1_post", "norm2_pre", "norm2_post", "ln_v_gain", "ln_v_bias", "pool_scale",
             "b_pool", "b_spatial", "w_spatial", "w_pool"]
    shapes = dict(b_ada=b_ada.shape, norm1_pre=norm1_pre.shape, norm1_post=norm1_post.shape,
                  norm2_pre=norm2_pre.shape, norm2_post=norm2_post.shape, ln_v_gain=ln_v_gain.shape,
                  ln_v_bias=ln_v_bias.shape, pool_scale=pool_scale.shape, b_pool=b_pool.shape,
                  b_spatial=b_spatial.shape, w_spatial=w_spatial.shape, w_pool=w_pool.shape)
    res = {}
    for k, nm in enumerate(names):
        res[nm] = tuple(o.reshape(shapes[nm]) for o in outs[1 + 4 * k:5 + 4 * k])
    res["w_ada"] = (grad_ada, d_ada, m_ada, v_ada)
    res["w_in"] = (grad_in_t.T, d_in_t.T, m_in_t.T, v_in_t.T)
    res["w_out"] = (grad_out, d_out, m_out, v_out)
    res["w_fc1"] = (grad_w1, d_w1, m_w1, v_w1)
    res["w_fc2"] = (grad_w2, d_w2, m_w2, v_w2)

    order = ["w_ada", "b_ada", "norm1_pre", "norm1_post", "w_in", "w_spatial", "b_spatial", "ln_v_gain", "ln_v_bias",
             "w_pool", "b_pool", "pool_scale", "w_out", "norm2_pre", "norm2_post", "w_fc1", "w_fc2"]
    return (loss, grad_x.reshape(x.shape),
            *[res[nm][0] for nm in order], *[res[nm][1] for nm in order],
            *[res[nm][2] for nm in order], *[res[nm][3] for nm in order])
```

```python
import functools

import jax
import jax.numpy as jnp
from jax import lax
from jax.experimental import pallas as pl
from jax.experimental.pallas import tpu as pltpu

F32 = jnp.float32
BF16 = jnp.bfloat16
MESH = pl.DeviceIdType.MESH

N_DEV = 8
D = 1024
D_A = 512
D_B = 512
D_Z = 2 * D_A + D_B
N_HEADS = 4
CHUNK = 128
WINDOWS = (2, 4, 8, 16)
GROUP = 128
D_FF = 4096
FF_BLK = D_FF // N_DEV
HALO = 16
EPS = 1e-6
VMEM_LIMIT = 60 * 1024 * 1024

ADAM_LR = 0.001
ADAM_B1 = 0.9
ADAM_B2 = 0.999
ADAM_EPS = 1e-08
ADAM_WD = 0.01
ADAM_STEP = 10

ROW_DMOD = 0
ROW_N1PRE, ROW_N1POST, ROW_N2PRE, ROW_N2POST = 8, 9, 10, 11
ROW_LN = 12
ROW_POOL = 13
ROW_LOSS = 14
ROW_BS = 16
ROW_WS = 24
ROW_WP = 88
SMALL_ROWS = 152
PACK_FINE = 40
PACK_HALF = PACK_FINE + 64
PACK_ROWS = 2 * PACK_HALF
PK_WS = PACK_FINE
PK_TABLE_B = PACK_HALF
PK_MISC = PK_TABLE_B + 24
PK_BS = PK_MISC + 8
PK_WP = PK_BS + 8


def _table_row(b):
    if isinstance(b, int):
        return 8 * b if 8 * b < PACK_FINE else 8 * b + PK_TABLE_B - PACK_FINE
    return 8 * b + jnp.where(8 * b < PACK_FINE, 0, PK_TABLE_B - PACK_FINE)


def _dot(a, b):
    return jnp.dot(a, b, preferred_element_type=F32)


def _dot_nt(a, b):
    return lax.dot_general(a, b, (((1,), (1,)), ((), ())), preferred_element_type=F32)


def _dot_tn(a, b):
    return lax.dot_general(a, b, (((0,), (0,)), ((), ())), preferred_element_type=F32)


def _rstd(v):
    return lax.rsqrt(jnp.mean(v * v, axis=-1, keepdims=True) + EPS)


def _rms_bwd(d_hat, hat, rstd):
    return rstd * (d_hat - hat * jnp.mean(d_hat * hat, axis=-1, keepdims=True))


def _rms_bwd_gained(g, gain, hat, rstd):
    g_hat = g * hat
    d_v = rstd * (g * gain - hat * jnp.mean(g_hat * gain, axis=-1, keepdims=True))
    return d_v, _colsum(g_hat)


_K0 = 0.7978845608028654
_K1 = 0.044715


def _gelu_parts(v):
    t = jnp.tanh(v * (_K0 + (_K0 * _K1) * (v * v)))
    return t, v * (0.5 + 0.5 * t)


def _gelu_grad(v, t):
    return (0.5 + 0.5 * t) + (0.5 * v) * (1.0 - t * t) * (_K0 + (3.0 * _K0 * _K1) * (v * v))


def _colsum(v):
    return jnp.sum(v, axis=0, keepdims=True)


def _full(shape):
    n = len(shape)
    return pl.BlockSpec(shape, lambda *_: (0,) * n)


def _resident(shape):
    n = len(shape)
    return pl.BlockSpec(shape, lambda *_: (0,) * n, pipeline_mode=pl.Buffered(1))


def _place():
    x, y, c = lax.axis_index("x"), lax.axis_index("y"), lax.axis_index("c")
    return x, y, c


def _flip(v, bit):
    return 1 - v if bit else v


def _peer(x, y, c, k):
    return (_flip(x, (k >> 2) & 1), _flip(y, (k >> 1) & 1), _flip(c, k & 1))


def _index(p):
    return 4 * p[0] + 2 * p[1] + p[2]


def _cast_shards(shards):
    def body(*refs):
        for src, dst in zip(refs[:len(shards)], refs[len(shards):]):
            dst[...] = src[...].astype(BF16)

    vm = pl.BlockSpec(memory_space=pltpu.VMEM)
    return pl.pallas_call(
        body, name="cast_shards", out_shape=tuple(jax.ShapeDtypeStruct(s.shape, BF16) for s in shards),
        in_specs=[vm] * len(shards), out_specs=tuple([vm] * len(shards)),
    )(*shards)


FC_EARLY = 6
FC_HEAD = 2
R_HEAD_COLS = (FC_EARLY - FC_HEAD) * FF_BLK
WGRAD_ORDER = (7, 6, 1, 3, 5, 2, 4, 0)


def _early_col(j):
    return R_HEAD_COLS + j * FF_BLK if j < FC_HEAD else (j - FC_HEAD) * FF_BLK


class _Copies:
    def __init__(self, entries, send_sems, recv_sems):
        self.place = _place()
        self.entries, self.send_sems, self.recv_sems = entries, send_sems, recv_sems

    def _copy(self, i, arrival=False):
        src, dst, rel = self.entries[i]
        return pltpu.make_async_remote_copy(
            src_ref=dst if arrival else src, dst_ref=dst, send_sem=self.send_sems.at[i],
            recv_sem=self.recv_sems.at[i], device_id=_peer(*self.place, rel), device_id_type=MESH)

    def start(self, *which):
        for i in which:
            self._copy(i).start()

    def wait_recv(self, *which):
        for i in which:
            self._copy(i, arrival=True).wait_recv()

    def wait_send(self, *which):
        for i in which:
            self._copy(i).wait_send()


TAIL_STEPS = 3


def _tail_comm(parts, small, head, row_chunk):
    n = len(parts)

    def body(*refs):
        p_refs, small_ref, head_ref = refs[:n], refs[n], refs[n + 1]
        outs = refs[n + 2:]
        g_refs, total_ref = outs[:n], outs[n]
        scr = outs[n + 1:]
        from_sib = scr[0:n]
        chip_out = scr[n:2 * n]
        chip_in = scr[2 * n:3 * n]
        pack, pack_sib, fine, bulk, total_scr = scr[3 * n:3 * n + 5]
        send_a, recv_a, send_b, recv_b, send_s, recv_s = scr[3 * n + 5:]
        step = pl.program_id(0)
        x, y, c = _place()
        me = _index((x, y, c))
        sibling = (x, y, 1 - c)
        my_chip = 2 * x + y
        others = [(1 - x, y), (x, 1 - y), (1 - x, 1 - y)]
        my_half = pl.ds(pl.multiple_of(PACK_HALF * c, 8), PACK_HALF)

        def pack_to_sibling():
            return pltpu.make_async_remote_copy(
                src_ref=pack, dst_ref=pack_sib, send_sem=send_s.at[0], recv_sem=recv_s.at[0],
                device_id=sibling, device_id_type=MESH)

        def half_to_chip(r, part):
            buf = (fine, bulk)[part]
            return pltpu.make_async_remote_copy(
                src_ref=buf.at[my_chip], dst_ref=buf.at[my_chip],
                send_sem=send_s.at[1 + 3 * part + r], recv_sem=recv_s.at[1 + 3 * part + r],
                device_id=(*others[r], c), device_id_type=MESH)

        def half_from_chip(r, part):
            k = 2 * others[r][0] + others[r][1]
            buf = (fine, bulk)[part]
            return pltpu.make_async_remote_copy(
                src_ref=buf.at[k], dst_ref=buf.at[k],
                send_sem=send_s.at[1 + 3 * part + r], recv_sem=recv_s.at[1 + 3 * part + r],
                device_id=(*others[r], c), device_id_type=MESH)

        def total_to_sibling():
            return pltpu.make_async_remote_copy(
                src_ref=total_scr.at[my_half], dst_ref=total_scr.at[my_half],
                send_sem=send_s.at[7], recv_sem=recv_s.at[7], device_id=sibling, device_id_type=MESH)

        def total_from_sibling():
            sib_half = pl.ds(pl.multiple_of(PACK_HALF * (1 - c), 8), PACK_HALF)
            return pltpu.make_async_remote_copy(
                src_ref=total_scr.at[sib_half], dst_ref=total_scr.at[sib_half],
                send_sem=send_s.at[7], recv_sem=recv_s.at[7], device_id=sibling, device_id_type=MESH)

        def to_sibling(a, k):
            return pltpu.make_async_remote_copy(
                src_ref=p_refs[a].at[2 * k + (1 - c)], dst_ref=from_sib[a].at[k],
                send_sem=send_a.at[a], recv_sem=recv_a.at[a], device_id=sibling, device_id_type=MESH)

        def all_from_sibling(a):
            return pltpu.make_async_remote_copy(
                src_ref=from_sib[a], dst_ref=from_sib[a], send_sem=send_a.at[a], recv_sem=recv_a.at[a],
                device_id=sibling, device_id_type=MESH)

        def to_chip(a, r):
            return pltpu.make_async_remote_copy(
                src_ref=chip_out[a].at[r], dst_ref=chip_in[a].at[r],
                send_sem=send_b.at[3 * a + r], recv_sem=recv_b.at[3 * a + r],
                device_id=(*others[r], c), device_id_type=MESH)

        @pl.when(step == 0)
        def _():
            pack[0:PACK_FINE, :] = jnp.zeros((PACK_FINE, D), F32)
            pack[PK_TABLE_B:PK_MISC, :] = jnp.zeros((PK_MISC - PK_TABLE_B, D), F32)
            pack[pl.ds(pl.multiple_of(_table_row(me), 8), 8), :] = small_ref[0:8, :] + head_ref[0:8, :]
            pack[PK_WS:PK_WS + 64, :] = small_ref[ROW_WS:ROW_WS + 64, :]
            pack[PK_MISC:PK_MISC + 8, :] = small_ref[ROW_N1PRE:ROW_N1PRE + 8, :] + head_ref[8:16, :]
            pack[PK_BS:PK_BS + 8, :] = small_ref[ROW_BS:ROW_BS + 8, :]
            pack[PK_WP:PK_WP + 64, :] = small_ref[ROW_WP:ROW_WP + 64, :]
            pack_to_sibling().start()
            for a in range(n):
                for k in range(4):
                    to_sibling(a, k).start()

        @pl.when(step == 1)
        def _():
            pack_to_sibling().wait_recv()
            chip_sum = pack[my_half, :] + pack_sib[my_half, :]
            fine[my_chip] = chip_sum[:PACK_FINE, :]
            bulk[my_chip] = chip_sum[PACK_FINE:, :].astype(BF16)
            for r in range(3):
                half_to_chip(r, 0).start()
                half_to_chip(r, 1).start()
            for a in range(n):
                all_from_sibling(a).wait_recv()
                rows = p_refs[a].shape[1]
                for r in range(3):
                    k = 2 * others[r][0] + others[r][1]
                    for s in range(0, rows, row_chunk):
                        sl = pl.ds(s, row_chunk)
                        chip_out[a][r, sl, :] = (p_refs[a][2 * k + c, sl, :].astype(F32)
                                                 + from_sib[a][k, sl, :].astype(F32)).astype(BF16)
                    to_chip(a, r).start()
                for s in range(0, rows, row_chunk):
                    sl = pl.ds(s, row_chunk)
                    g_refs[a][sl, :] = (p_refs[a][2 * my_chip + c, sl, :].astype(F32)
                                        + from_sib[a][my_chip, sl, :].astype(F32))

        @pl.when(step == TAIL_STEPS - 1)
        def _():
            for r in range(3):
                half_from_chip(r, 0).wait_recv()
                half_from_chip(r, 1).wait_recv()
            half_start = pl.multiple_of(PACK_HALF * c, 8)
            total_scr[pl.ds(half_start, PACK_FINE), :] = ((fine[0] + fine[1]) + fine[2]) + fine[3]
            total_scr[pl.ds(half_start + PACK_FINE, PACK_HALF - PACK_FINE), :] = (
                (bulk[0].astype(F32) + bulk[1].astype(F32)) + bulk[2].astype(F32)) + bulk[3].astype(F32)
            total_to_sibling().start()
            for a in range(n):
                rows = p_refs[a].shape[1]
                for r in range(3):
                    to_chip(a, r).wait_recv()
                    for s in range(0, rows, row_chunk):
                        sl = pl.ds(s, row_chunk)
                        g_refs[a][sl, :] = g_refs[a][sl, :] + chip_in[a][r, sl, :].astype(F32)
            total_from_sibling().wait_recv()
            total_ref[...] = total_scr[...]
            for a in range(n):
                all_from_sibling(a).wait_send()
                for r in range(3):
                    to_chip(a, r).wait_send()
            pack_to_sibling().wait_send()
            for r in range(3):
                half_to_chip(r, 0).wait_send()
                half_to_chip(r, 1).wait_send()
            total_to_sibling().wait_send()

    return pl.pallas_call(
        body, name="tail_comm", grid=(TAIL_STEPS,),
        out_shape=tuple([jax.ShapeDtypeStruct(p.shape[1:], F32) for p in parts]
                        + [jax.ShapeDtypeStruct((PACK_ROWS, D), F32)]),
        in_specs=[_resident(p.shape) for p in parts] + [_resident(small.shape), _resident(head.shape)],
        out_specs=tuple([_full(p.shape[1:]) for p in parts] + [_full((PACK_ROWS, D))]),
        scratch_shapes=(
            [pltpu.VMEM((4,) + p.shape[1:], BF16) for p in parts]
            + [pltpu.VMEM((3,) + p.shape[1:], BF16) for p in parts]
            + [pltpu.VMEM((3,) + p.shape[1:], BF16) for p in parts]
            + [pltpu.VMEM((PACK_ROWS, D), F32), pltpu.VMEM((PACK_ROWS, D), F32),
               pltpu.VMEM((4, PACK_FINE, D), F32), pltpu.VMEM((4, PACK_HALF - PACK_FINE, D), BF16),
               pltpu.VMEM((PACK_ROWS, D), F32)]
            + [pltpu.SemaphoreType.DMA((n,)), pltpu.SemaphoreType.DMA((n,)),
               pltpu.SemaphoreType.DMA((3 * n,)), pltpu.SemaphoreType.DMA((3 * n,)),
               pltpu.SemaphoreType.DMA((8,)), pltpu.SemaphoreType.DMA((8,))]),
        compiler_params=pltpu.CompilerParams(dimension_semantics=("arbitrary",), vmem_limit_bytes=VMEM_LIMIT),
    )(*parts, small, head)


def _tril_mask():
    row = lax.broadcasted_iota(jnp.int32, (CHUNK, CHUNK), 0)
    col = lax.broadcasted_iota(jnp.int32, (CHUNK, CHUNK), 1)
    return (col <= row).astype(F32)


def _window_sums(ext):
    s2 = ext + pltpu.roll(ext, 1, 0)
    t4 = s2[:, GROUP:]
    s4 = t4 + pltpu.roll(t4, 2, 0)
    t8 = s4[:, GROUP:]
    s8 = t8 + pltpu.roll(t8, 4, 0)
    t16 = s8[:, GROUP:]
    s16 = t16 + pltpu.roll(t16, 8, 0)
    return [s2[:, :GROUP], s4[:, :GROUP], s8[:, :GROUP], s16]


def _inv_counts(first_pos, rows):
    pos = first_pos + lax.broadcasted_iota(jnp.int32, (rows, 1), 0)
    return [1.0 / jnp.minimum(pos + 1, w).astype(F32) for w in WINDOWS]


def _pool_diff(zb, halo, first_pos):
    tt = zb.shape[0]
    sums = _window_sums(jnp.concatenate([halo, zb], axis=0))
    inv = _inv_counts(first_pos, tt)
    return [sums[g][HALO:, :] * inv[g] - zb[:, g * GROUP:(g + 1) * GROUP] for g in range(len(WINDOWS))]


def _row_blocks(scr, rows, place):
    def block(rel):
        start = pl.multiple_of(rows * _index(_peer(*place, rel)), rows)
        return scr.at[pl.ds(start, rows), :]

    def entries(shard_ref):
        return ([(shard_ref, block(0), rel) for rel in (1, 2, 4, 6)]
                + [(block(rel), block(rel), 1) for rel in (2, 4, 6)])
    return block, entries


def _attn_fwd(tt, x, c_row, w_ada, b_pieces, n1pre, n1post, w_in_shard, w_out_shard, w_sp, bs_rows, ln_g, ln_b,
              w_pool, b_pool, pool_scale, fc_shards, n2pre):
    t_len = x.shape[0]
    nt = t_len // tt
    ncol = w_ada.shape[1]

    def body(x_ref, xb_ref, c_ref, wada_ref, b_ref, n1pre_ref, n1post_ref, wi_ref, wo_ref, wsp_ref, bs_ref,
             lng_ref, lnb_ref, wp_ref, bp_ref, ps_ref, w1_ref, w2_ref, n2pre_ref,
             z_ref, cat_ref, mix_ref, x1_ref, h2_ref, r_ref, f_ref, mod_out, sc_out, e1_ref, e2_ref, wout_ref,
             win_out, carry, land1, land2, sib1, sib2, cat_keep, wout_scr, win_ref, mod_ref, cg, mg, part,
             send_sems, recv_sems, local_sems, wo_send, wo_recv, wi_send, wi_recv, ada_send, ada_recv):
        i = pl.program_id(0)
        place = px, py, pc = _place()
        me = _index(place)
        wo_block, wo_entries = _row_blocks(wout_scr, w_out_shard.shape[0], place)
        wi_block, wi_entries = _row_blocks(win_ref, w_in_shard.shape[0], place)
        wo_copies = _Copies(wo_entries(wo_ref), wo_send, wo_recv)
        wi_copies = _Copies(wi_entries(wi_ref), wi_send, wi_recv)
        wo_keep = pltpu.make_async_copy(wout_scr, wout_ref, local_sems.at[8])
        wi_keep = pltpu.make_async_copy(win_ref, win_out, local_sems.at[9])
        ada = _Copies([(cg.at[me], cg.at[me], k) for k in range(1, N_DEV)]
                      + [(part, mg.at[me], k) for k in range(1, N_DEV)], ada_send, ada_recv)
        copies = _Copies(
            [(w1_ref, sib1, 1), (w2_ref, sib2, 1),
             (w1_ref, land1.at[0], 2), (w2_ref, land2.at[0], 2),
             (w1_ref, land1.at[1], 4), (w2_ref, land2.at[1], 4),
             (land1.at[0], e1_ref.at[3], 1), (land2.at[0], e2_ref.at[3], 1),
             (land1.at[1], e1_ref.at[5], 1), (land2.at[1], e2_ref.at[5], 1)],
            send_sems, recv_sems)
        keep = [pltpu.make_async_copy(w1_ref, e1_ref.at[0], local_sems.at[0]),
                pltpu.make_async_copy(w2_ref, e2_ref.at[0], local_sems.at[1]),
                pltpu.make_async_copy(land1.at[0], e1_ref.at[2], local_sems.at[2]),
                pltpu.make_async_copy(land1.at[1], e1_ref.at[4], local_sems.at[3]),
                pltpu.make_async_copy(land2.at[0], e2_ref.at[2], local_sems.at[4]),
                pltpu.make_async_copy(land2.at[1], e2_ref.at[4], local_sems.at[5]),
                pltpu.make_async_copy(sib1, e1_ref.at[1], local_sems.at[6]),
                pltpu.make_async_copy(sib2, e2_ref.at[1], local_sems.at[7])]

        @pl.when(i == 0)
        def _():
            cg[me] = jnp.broadcast_to(c_ref[...], (8, D))
            ada.start(*range(N_DEV - 1))
            wi_copies.start(0, 1, 2, 3)
            wi_rows, wo_rows = w_in_shard.shape[0], w_out_shard.shape[0]
            win_ref[pl.ds(pl.multiple_of(wi_rows * me, wi_rows), wi_rows), :] = wi_ref[...]
            wout_scr[pl.ds(pl.multiple_of(wo_rows * me, wo_rows), wo_rows), :] = wo_ref[...]
            carry[...] = jnp.zeros_like(carry)

            ada.wait_recv(*range(N_DEV - 1))
            c_all = jnp.concatenate([cg[j, 0:1, :] for j in range(N_DEV)], axis=0)
            sc = c_all * jax.nn.sigmoid(c_all)
            sc_out[...] = sc
            part[...] = _dot(sc.astype(BF16), wada_ref[...].astype(BF16)) + b_ref[me]
            ada.start(*range(N_DEV - 1, 2 * (N_DEV - 1)))
            wo_copies.start(0, 1, 2, 3)
            copies.start(0, 1, 2, 4, 3, 5)
            keep[0].start()
            keep[1].start()
            mg[me] = part[...]
            ada.wait_recv(*range(N_DEV - 1, 2 * (N_DEV - 1)))
            mod_ref[...] = jnp.zeros_like(mod_ref)
            for j in range(N_DEV):
                for m in range(6):
                    lo, hi = max(ncol * j, D * m), min(ncol * (j + 1), D * (m + 1))
                    if lo < hi:
                        mod_ref[m:m + 1, lo - D * m:hi - D * m] = mg[j, pl.ds(me, 1), lo - ncol * j:hi - ncol * j]
            mod_out[...] = mod_ref[...]

            wi_copies.wait_recv(1, 2, 3)
            wi_copies.start(4, 5, 6)
            wi_copies.wait_recv(0, 4, 5, 6)
            wi_keep.start()

        @pl.when(i == nt // 2 + ATTN_LAG)
        def _():
            copies.wait_recv(2, 4)
            copies.start(6, 8)
            keep[2].start()
            keep[3].start()

        @pl.when(i == nt - 1 + ATTN_LAG)
        def _():
            copies.wait_recv(3, 5)
            copies.start(7, 9)
            keep[4].start()
            keep[5].start()

        shift1, scale1, gate1 = mod_ref[0:1, :], mod_ref[1:2, :], mod_ref[2:3, :]

        @pl.when(i < nt)
        def _():
            xv = x_ref[...]
            h1 = (xv * _rstd(xv)) * (n1pre_ref[...] * (1.0 + scale1)) + shift1
            z = _dot_nt(h1.astype(BF16), win_ref[...])
            z_ref[...] = z

            _, ga = _gelu_parts(z[:, :2 * D_A])
            u, vr = ga[:, :D_A], ga[:, D_A:]
            dv = vr - jnp.mean(vr, axis=-1, keepdims=True)
            v = (dv * lax.rsqrt(jnp.mean(dv * dv, axis=-1, keepdims=True) + EPS)) * lng_ref[...] + lnb_ref[...]
            vb = v.astype(BF16)
            mask = _tril_mask()
            wc = [(wsp_ref[h] * mask).astype(BF16) for h in range(N_HEADS)]
            for ch in range(tt // CHUNK):
                rows = slice(ch * CHUNK, (ch + 1) * CHUNK)
                for h in range(N_HEADS):
                    cols = slice(h * GROUP, (h + 1) * GROUP)
                    mixed = _dot(wc[h], vb[rows, cols]) + bs_ref[:, cols]
                    cat_ref[rows, cols] = (u[rows, cols] * mixed).astype(BF16)

            zb = z[:, 2 * D_A:]
            diff = _pool_diff(zb, carry[...], i * tt)
            carry[...] = zb[tt - HALO:, :]
            for g in range(len(WINDOWS)):
                cols = slice(g * GROUP, (g + 1) * GROUP)
                pre = _dot(diff[g].astype(BF16), wp_ref[g].astype(BF16)) + bp_ref[:, cols]
                cat_ref[:, D_A + g * GROUP:D_A + (g + 1) * GROUP] = (pre * ps_ref[:, cols]).astype(BF16)
            cat_keep[i % (ATTN_LAG + 1)] = cat_ref[...]

        @pl.when(i == 0)
        def _():
            copies.wait_recv(0, 1)
            keep[6].start()
            keep[7].start()

        @pl.when(i == 1)
        def _():
            wo_copies.wait_recv(1, 2, 3)
            wo_copies.start(4, 5, 6)

        @pl.when(i == ATTN_LAG)
        def _():
            wo_copies.wait_recv(0, 4, 5, 6)
            wo_keep.start()

        @pl.when(i >= ATTN_LAG)
        def _():
            xv = xb_ref[...]
            mix = _dot(cat_keep[(i - ATTN_LAG) % (ATTN_LAG + 1)], wout_scr[...])
            mix_ref[...] = mix
            x1v = xv + (mix * _rstd(mix)) * (gate1 * n1post_ref[...])
            x1_ref[...] = x1v
            shift2, scale2 = mod_ref[3:4, :], mod_ref[4:5, :]
            h2 = ((x1v * _rstd(x1v)) * (n2pre_ref[...] * (1.0 + scale2)) + shift2).astype(BF16)
            h2_ref[...] = h2
            for j, (w1, w2) in enumerate(((w1_ref, w2_ref), (sib1, sib2))):
                ra = jnp.maximum(_dot(h2, w1[...]), 0.0)
                r = (ra * ra).astype(BF16)
                r_ref[:, j * FF_BLK:(j + 1) * FF_BLK] = r
                if j == 0:
                    f_ref[...] = _dot(r, w2[...])
                else:
                    f_ref[...] += _dot(r, w2[...])

        @pl.when(i == nt - 1 + ATTN_LAG)
        def _():
            copies.wait_recv(6, 7, 8, 9)
            copies.wait_send(*range(10))
            wo_copies.wait_send(*range(7))
            wi_copies.wait_send(*range(7))
            ada.wait_send(*range(2 * (N_DEV - 1)))
            for cp in keep:
                cp.wait()
            wo_keep.wait()
            wi_keep.wait()

    first = lambda w: pl.BlockSpec((tt, w), lambda i: (jnp.minimum(i, nt - 1), 0))
    second = lambda w: pl.BlockSpec((tt, w), lambda i: (jnp.maximum(i - ATTN_LAG, 0), 0))
    r_head = pl.BlockSpec((tt, FC_HEAD * FF_BLK),
                          lambda i: (jnp.maximum(i - ATTN_LAG, 0), R_HEAD_COLS // (FC_HEAD * FF_BLK)))
    hbm = pl.BlockSpec(memory_space=pl.ANY)
    outs = pl.pallas_call(
        body, name="attn_fwd", grid=(nt + ATTN_LAG,),
        out_shape=tuple([jax.ShapeDtypeStruct((t_len, D_Z), F32), jax.ShapeDtypeStruct((t_len, D), BF16),
                         jax.ShapeDtypeStruct((t_len, D), F32), jax.ShapeDtypeStruct((t_len, D), F32),
                         jax.ShapeDtypeStruct((t_len, D), BF16),
                         jax.ShapeDtypeStruct((t_len, FC_EARLY * FF_BLK), BF16),
                         jax.ShapeDtypeStruct((t_len, D), F32)]
                        + [jax.ShapeDtypeStruct((8, D), F32), jax.ShapeDtypeStruct((N_DEV, D), F32)]
                        + [jax.ShapeDtypeStruct((FC_EARLY,) + s.shape, BF16) for s in fc_shards]
                        + [jax.ShapeDtypeStruct((D, D), BF16), jax.ShapeDtypeStruct((D_Z, D), BF16)]),
        in_specs=[first(D), second(D), _full((1, D)), _resident(w_ada.shape), _full((N_DEV, 1, ncol)), _full((1, D)),
                  _full((1, D)), _resident(w_in_shard.shape), _resident(w_out_shard.shape),
                  _full((N_HEADS, CHUNK, CHUNK)), _full((CHUNK, D_A)), _full((1, D_A)), _full((1, D_A)),
                  _full((len(WINDOWS), GROUP, GROUP)), _full((1, D_B)), _full((1, D_B)),
                  _resident(fc_shards[0].shape), _resident(fc_shards[1].shape), _full((1, D))],
        out_specs=(first(D_Z), first(D), second(D), second(D), second(D), r_head, second(D),
                   _full((8, D)), _full((N_DEV, D)), hbm, hbm, hbm, hbm),
        scratch_shapes=[pltpu.VMEM((HALO, D_B), F32),
                        pltpu.VMEM((2,) + fc_shards[0].shape, BF16), pltpu.VMEM((2,) + fc_shards[1].shape, BF16),
                        pltpu.VMEM(fc_shards[0].shape, BF16), pltpu.VMEM(fc_shards[1].shape, BF16),
                        pltpu.VMEM((ATTN_LAG + 1, tt, D), BF16), pltpu.VMEM((D, D), BF16),
                        pltpu.VMEM((D_Z, D), BF16), pltpu.VMEM((8, D), F32),
                        pltpu.VMEM((N_DEV, 8, D), F32), pltpu.VMEM((N_DEV, N_DEV, ncol), F32),
                        pltpu.VMEM((N_DEV, ncol), F32),
                        pltpu.SemaphoreType.DMA((10,)), pltpu.SemaphoreType.DMA((10,)),
                        pltpu.SemaphoreType.DMA((10,)),
                        pltpu.SemaphoreType.DMA((7,)), pltpu.SemaphoreType.DMA((7,)),
                        pltpu.SemaphoreType.DMA((7,)), pltpu.SemaphoreType.DMA((7,)),
                        pltpu.SemaphoreType.DMA((2 * (N_DEV - 1),)), pltpu.SemaphoreType.DMA((2 * (N_DEV - 1),))],
        compiler_params=pltpu.CompilerParams(dimension_semantics=("arbitrary",), vmem_limit_bytes=VMEM_LIMIT),
    )(x, x, c_row, w_ada, b_pieces, n1pre, n1post, w_in_shard, w_out_shard, w_sp, bs_rows, ln_g, ln_b, w_pool, b_pool,
      pool_scale, *fc_shards, n2pre)
    return outs[:9], outs[9:11], outs[11], outs[12]


def _mlp_fwd_early(tt, r_begun, h2, f_head, w1_early, w2_early):
    t_len = h2.shape[0]
    nt = t_len // tt
    n_late = N_DEV - FC_EARLY

    def body(r_begun_ref, h2_ref, fh_ref, w1_ref, w2_ref, r_ref, f_ref, l1_ref, l2_ref,
             land1, land2, send_sems, recv_sems, local_sems):
        i = pl.program_id(0)
        copies = _Copies(
            [(w1_ref.at[2], land1, 4), (w2_ref.at[4], land2, 2),
             (land1, l1_ref.at[1], 1), (land2, l2_ref.at[1], 1)],
            send_sems, recv_sems)
        keep = [pltpu.make_async_copy(land1, l1_ref.at[0], local_sems.at[0]),
                pltpu.make_async_copy(land2, l2_ref.at[0], local_sems.at[1])]

        @pl.when(i == 0)
        def _():
            copies.start(0, 1)

        @pl.when(i == nt - 1)
        def _():
            copies.wait_recv(0, 1)
            copies.start(2, 3)
            for cp in keep:
                cp.start()

        h2 = h2_ref[...]
        f_ref[...] = fh_ref[...]
        for j in range(FC_HEAD, FC_EARLY):
            ra = jnp.maximum(_dot(h2, w1_ref[j]), 0.0)
            r = (ra * ra).astype(BF16)
            r_ref[:, _early_col(j):_early_col(j) + FF_BLK] = r
            f_ref[...] += _dot(r, w2_ref[j])

        @pl.when(i == nt - 1)
        def _():
            copies.wait_recv(2, 3)
            copies.wait_send(0, 1, 2, 3)
            for cp in keep:
                cp.wait()

    tile = lambda w: pl.BlockSpec((tt, w), lambda i: (i, 0))
    hbm = pl.BlockSpec(memory_space=pl.ANY)
    outs = pl.pallas_call(
        body, name="mlp_fwd_early", grid=(nt,),
        out_shape=(jax.ShapeDtypeStruct((t_len, FC_EARLY * FF_BLK), BF16), jax.ShapeDtypeStruct((t_len, D), F32),
                   jax.ShapeDtypeStruct((n_late,) + w1_early.shape[1:], BF16),
                   jax.ShapeDtypeStruct((n_late,) + w2_early.shape[1:], BF16)),
        in_specs=[hbm, tile(D), tile(D), _resident((FC_EARLY, D, FF_BLK)), _resident((FC_EARLY, FF_BLK, D))],
        out_specs=(tile(R_HEAD_COLS), tile(D), hbm, hbm),
        input_output_aliases={0: 0},
        scratch_shapes=[pltpu.VMEM(w1_early.shape[1:], BF16), pltpu.VMEM(w2_early.shape[1:], BF16),
                        pltpu.SemaphoreType.DMA((4,)), pltpu.SemaphoreType.DMA((4,)),
                        pltpu.SemaphoreType.DMA((2,))],
        compiler_params=pltpu.CompilerParams(dimension_semantics=("arbitrary",), vmem_limit_bytes=VMEM_LIMIT),
    )(r_begun, h2, f_head, w1_early, w2_early)
    return outs[:2], outs[2:]


def _mlp_late_bwd(tt, r_early, x1, h2, f_early, tgt, mix, mod, n2pre, n2post, n1post,
                  w1_early, w2_early, w1_late, w2_late):
    t_len = x1.shape[0]
    nt = t_len // tt
    n_late = N_DEV - FC_EARLY
    late_cols = n_late * FF_BLK

    def body(re_ref, x1_ref, h2_ref, fe_ref, tgt_ref, mix_ref, mod_ref, n2pre_ref, n2post_ref,
             n1post_ref, w1e_ref, w2e_ref, w1l_ref, w2l_ref,
             rl_ref, df_ref, da_ref, dmix_ref, dx1_ref, redf_ref, redb_ref, dh2_acc):
        i = pl.program_id(0)

        @pl.when(i == 0)
        def _():
            redf_ref[...] = jnp.zeros_like(redf_ref)
            redb_ref[...] = jnp.zeros_like(redb_ref)

        x1v = x1_ref[...]
        gate1, scale2, gate2 = mod_ref[2:3, :], mod_ref[4:5, :], mod_ref[5:6, :]
        h2 = h2_ref[...]
        f = fe_ref[...]
        for j in range(n_late):
            cols = slice(j * FF_BLK, (j + 1) * FF_BLK)
            ra = jnp.maximum(_dot(h2, w1l_ref[j]), 0.0)
            r = (ra * ra).astype(BF16)
            rl_ref[:, cols] = r
            f = f + _dot(r, w2l_ref[j])
        post2 = n2post_ref[...]
        gate_post2 = gate2 * post2
        rf = _rstd(f)
        fhat = f * rf
        err = (x1v + fhat * gate_post2) - tgt_ref[...]
        dy = err * (1.0 / D)
        d_f, sum_f = _rms_bwd_gained(dy, gate_post2, fhat, rf)
        dfv = d_f.astype(BF16)
        df_ref[...] = dfv
        redf_ref[0:1, :] += post2 * sum_f
        redf_ref[1:2, :] += gate2 * sum_f
        redf_ref[2:3, :] += _colsum(err * err)

        for j in range(N_DEV):
            cols = slice(j * FF_BLK, (j + 1) * FF_BLK)
            if j < FC_EARLY:
                w1, w2, r = w1e_ref[j], w2e_ref[j], re_ref[:, _early_col(j):_early_col(j) + FF_BLK]
            else:
                jl = j - FC_EARLY
                w1, w2, r = w1l_ref[jl], w2l_ref[jl], rl_ref[:, jl * FF_BLK:(jl + 1) * FF_BLK]
            dr = _dot_nt(dfv, w2)
            da = (dr * (2.0 * jnp.sqrt(r.astype(F32)))).astype(BF16)
            da_ref[:, cols] = da
            contrib = _dot_nt(da, w1)
            if j == 0:
                dh2_acc[...] = contrib
            else:
                dh2_acc[...] += contrib
        dh2 = dh2_acc[...]
        pre2, post1 = n2pre_ref[...], n1post_ref[...]
        r2 = _rstd(x1v)
        xhat = x1v * r2
        d_x1, sum_h = _rms_bwd_gained(dh2, pre2 * (1.0 + scale2), xhat, r2)
        dx1 = dy + d_x1
        dx1_ref[...] = dx1
        mixv = mix_ref[...]
        rm = _rstd(mixv)
        mhat = mixv * rm
        d_mix, sum_m = _rms_bwd_gained(dx1, gate1 * post1, mhat, rm)
        dmix_ref[...] = d_mix.astype(BF16)
        redb_ref[0:1, :] += _colsum(dh2)
        redb_ref[1:2, :] += pre2 * sum_h
        redb_ref[2:3, :] += (1.0 + scale2) * sum_h
        redb_ref[3:4, :] += post1 * sum_m
        redb_ref[4:5, :] += gate1 * sum_m

    tile = lambda w: pl.BlockSpec((tt, w), lambda i: (i, 0))
    return pl.pallas_call(
        body, name="mlp_late_bwd", grid=(nt,),
        out_shape=(jax.ShapeDtypeStruct((t_len, late_cols), BF16), jax.ShapeDtypeStruct((t_len, D), BF16),
                   jax.ShapeDtypeStruct((t_len, D_FF), BF16), jax.ShapeDtypeStruct((t_len, D), BF16),
                   jax.ShapeDtypeStruct((t_len, D), F32), jax.ShapeDtypeStruct((8, D), F32),
                   jax.ShapeDtypeStruct((8, D), F32)),
        in_specs=[tile(FC_EARLY * FF_BLK), tile(D), tile(D), tile(D), tile(D),
                  tile(D), _full((8, D)), _full((1, D)), _full((1, D)), _full((1, D)),
                  _resident((FC_EARLY, D, FF_BLK)), _resident((FC_EARLY, FF_BLK, D)),
                  _resident((n_late, D, FF_BLK)), _resident((n_late, FF_BLK, D))],
        out_specs=(tile(late_cols), tile(D), tile(D_FF), tile(D), tile(D), _full((8, D)), _full((8, D))),
        scratch_shapes=[pltpu.VMEM((tt, D), F32)],
        compiler_params=pltpu.CompilerParams(dimension_semantics=("arbitrary",), vmem_limit_bytes=VMEM_LIMIT),
    )(r_early, x1, h2, f_early, tgt, mix, mod, n2pre, n2post, n1post, w1_early, w2_early, w1_late, w2_late)


def _mlp_wgrad(tt, r_early, r_late, da, df, h2):
    t_len = df.shape[0]
    nt = t_len // tt
    odd_steps = [j for j, rel in enumerate(WGRAD_ORDER) if rel % 2]

    def relation(j):
        rel = jnp.int32(WGRAD_ORDER[-1])
        for step in range(N_DEV - 2, -1, -1):
            rel = jnp.where(j == step, WGRAD_ORDER[step], rel)
        return rel

    def body(re_ref, rl_ref, da_ref, df_ref, h2_ref, own1_ref, own2_ref, out1_ref, out2_ref, diag1_ref, diag2_ref,
             acc1, acc2, snd1, snd2, sib1, sib2, dsnd1, dsnd2, send_sems, recv_sems):
        j, t = pl.program_id(0), pl.program_id(1)
        rows = pl.ds(pl.multiple_of(t * tt, tt), tt)
        x, y, c = _place()
        accs, snds, sibs = (acc1, acc2), (snd1, snd2), (sib1, sib2)
        dsnds, diags = (dsnd1, dsnd2), (diag1_ref, diag2_ref)

        def to_sibling(a, jj, buf=0):
            return pltpu.make_async_remote_copy(
                src_ref=snds[a].at[buf], dst_ref=sibs[a].at[jj],
                send_sem=send_sems.at[4 * a + jj], recv_sem=recv_sems.at[4 * a + jj],
                device_id=(x, y, 1 - c), device_id_type=MESH)

        def to_diagonal(a):
            return pltpu.make_async_remote_copy(
                src_ref=dsnds[a], dst_ref=diags[a], send_sem=send_sems.at[8 + a], recv_sem=recv_sems.at[8 + a],
                device_id=_peer(x, y, c, 6), device_id_type=MESH)

        @pl.when(t == 0)
        def _():
            acc2[...] = jnp.zeros_like(acc2)
            acc1[...] = jnp.zeros_like(acc1)

        for r_ref, mine in ((re_ref, relation(j) < FC_EARLY), (rl_ref, relation(j) >= FC_EARLY)):
            @pl.when(mine)
            def _():
                acc2[...] += _dot_tn(r_ref[...], df_ref[rows, :])
                acc1[...] += _dot_tn(h2_ref[rows, :], da_ref[...])

        for step, rel in enumerate(WGRAD_ORDER):
            jj = rel // 2

            @pl.when((t == nt - 1) & (j == step))
            def _():
                for a, (own_ref, out_ref) in enumerate(((own1_ref, out1_ref), (own2_ref, out2_ref))):
                    if rel % 2:
                        q = odd_steps.index(step)
                        if q >= 2:
                            to_sibling(a, WGRAD_ORDER[odd_steps[q - 2]] // 2).wait_send()
                        snds[a][q % 2] = accs[a][...].astype(BF16)
                        to_sibling(a, jj, q % 2).start()
                        continue
                    to_sibling(a, jj).wait_recv()
                    chip_sum = accs[a][...] + sibs[a][jj].astype(F32)
                    if rel == 6:
                        dsnds[a][...] = chip_sum.astype(BF16)
                        to_diagonal(a).start()
                    elif rel == 0:
                        own_ref[...] = chip_sum
                    else:
                        out_ref[0] = chip_sum.astype(BF16)
                    if step == N_DEV - 1:
                        for q in (2, 3):
                            to_sibling(a, WGRAD_ORDER[odd_steps[q]] // 2).wait_send()
                        to_diagonal(a).wait_recv()
                        to_diagonal(a).wait_send()

    assert WGRAD_ORDER[-1] == 0 and WGRAD_ORDER[-3:-1] == (2, 4)
    blk = pl.BlockSpec((tt, FF_BLK), lambda j, t: (t, relation(j)))
    early_block = lambda rel: jnp.where(rel < FC_HEAD, rel + FC_EARLY - FC_HEAD, rel - FC_HEAD)
    early = lambda j, t: (jnp.where(relation(j) < FC_EARLY, t, 0),
                          jnp.where(relation(j) < FC_EARLY, early_block(relation(j)), 0))
    late = lambda j, t: (jnp.where(relation(j) < FC_EARLY, 0, t), jnp.maximum(relation(j) - FC_EARLY, 0))
    chip = lambda j, t: (jnp.clip(j - 5, 0, 1), 0, 0)
    hbm = pl.BlockSpec(memory_space=pl.ANY)
    return pl.pallas_call(
        body, name="mlp_wgrad", grid=(N_DEV, nt),
        out_shape=(jax.ShapeDtypeStruct((D, FF_BLK), F32), jax.ShapeDtypeStruct((FF_BLK, D), F32),
                   jax.ShapeDtypeStruct((2, D, FF_BLK), BF16), jax.ShapeDtypeStruct((2, FF_BLK, D), BF16),
                   jax.ShapeDtypeStruct((D, FF_BLK), BF16), jax.ShapeDtypeStruct((FF_BLK, D), BF16)),
        in_specs=[pl.BlockSpec((tt, FF_BLK), early), pl.BlockSpec((tt, FF_BLK), late), blk,
                  _resident((t_len, D)), _resident((t_len, D))],
        out_specs=(_full((D, FF_BLK)), _full((FF_BLK, D)),
                   pl.BlockSpec((1, D, FF_BLK), chip), pl.BlockSpec((1, FF_BLK, D), chip), hbm, hbm),
        scratch_shapes=[pltpu.VMEM((D, FF_BLK), F32), pltpu.VMEM((FF_BLK, D), F32),
                        pltpu.VMEM((2, D, FF_BLK), BF16), pltpu.VMEM((2, FF_BLK, D), BF16),
                        pltpu.VMEM((4, D, FF_BLK), BF16), pltpu.VMEM((4, FF_BLK, D), BF16),
                        pltpu.VMEM((D, FF_BLK), BF16), pltpu.VMEM((FF_BLK, D), BF16),
                        pltpu.SemaphoreType.DMA((10,)), pltpu.SemaphoreType.DMA((10,))],
        compiler_params=pltpu.CompilerParams(dimension_semantics=("arbitrary", "arbitrary"),
                                             vmem_limit_bytes=VMEM_LIMIT),
    )(r_early, r_late, da, df, h2)


def _acc_rows(ref, row0, k, val):
    half = CHUNK // 2
    ref[row0:row0 + half, k * GROUP:(k + 1) * GROUP] += val[:half, :]
    ref[row0:row0 + half, D_A + k * GROUP:D_A + (k + 1) * GROUP] += val[half:, :]


def _attn_bwd(tt, dmix, x, z, cat, mod, n1pre, w_out, w_sp, bs_rows, ln_g, ln_b, w_pool, b_pool, pool_scale,
              red_fwd, red_bwd, chip_sums):
    t_len = z.shape[0]
    nt = t_len // tt
    hb = tt // HALO
    n_sums = len(chip_sums)

    def body(dmix_ref, x_ref, z_ref, zprev_ref, cat_ref, mod_ref, n1pre_ref, wout_ref, wsp_ref,
             bs_ref, lng_ref, lnb_ref, wp_ref, bp_ref, ps_ref, redf_ref, redb_ref, *rest):
        sum_out = rest[:n_sums]
        dz_ref, gwin_ref, gwout_ref, small_ref = rest[n_sums:n_sums + 4]
        sum_in = rest[n_sums + 4:2 * n_sums + 4]
        carry, acc_in, acc_out, dz_scr, bs_acc, send_sems, recv_sems = rest[2 * n_sums + 4:]
        s = pl.program_id(0)
        i = nt - 1 - s
        px, py, pc = _place()

        def chip_copy(a, r):
            return pltpu.make_async_remote_copy(
                src_ref=sum_out[a].at[r], dst_ref=sum_in[a].at[r],
                send_sem=send_sems.at[2 * a + r], recv_sem=recv_sems.at[2 * a + r],
                device_id=_peer(px, py, pc, 2 * (r + 1)), device_id_type=MESH)

        @pl.when(s == 0)
        def _():
            for a in range(n_sums):
                for r in range(2):
                    chip_copy(a, r).start()
            carry[...] = jnp.zeros_like(carry)
            acc_in[...] = jnp.zeros_like(acc_in)
            acc_out[...] = jnp.zeros_like(acc_out)
            bs_acc[...] = jnp.zeros_like(bs_acc)
            small_ref[...] = jnp.zeros_like(small_ref)
            small_ref[ROW_DMOD + 2:ROW_DMOD + 3, :] = redb_ref[3:4, :]
            small_ref[ROW_DMOD + 3:ROW_DMOD + 5, :] = redb_ref[0:2, :]
            small_ref[ROW_DMOD + 5:ROW_DMOD + 6, :] = redf_ref[0:1, :]
            small_ref[ROW_N1POST:ROW_N1POST + 1, :] = redb_ref[4:5, :]
            small_ref[ROW_N2PRE:ROW_N2PRE + 1, :] = redb_ref[2:3, :]
            small_ref[ROW_N2POST:ROW_N2POST + 1, :] = redf_ref[1:2, :]
            small_ref[ROW_LOSS:ROW_LOSS + 1, :] = redf_ref[2:3, :]

        dmixv = dmix_ref[...]
        dcat = _dot_nt(dmixv, wout_ref[...])
        acc_out[...] += _dot_tn(cat_ref[...], dmixv)

        z = z_ref[...]
        t_g, ga = _gelu_parts(z[:, :2 * D_A])
        u, vr = ga[:, :D_A], ga[:, D_A:]
        dv0 = vr - jnp.mean(vr, axis=-1, keepdims=True)
        rv = lax.rsqrt(jnp.mean(dv0 * dv0, axis=-1, keepdims=True) + EPS)
        vhat = dv0 * rv
        vb = (vhat * lng_ref[...] + lnb_ref[...]).astype(BF16)
        mask = _tril_mask()
        wc = [(wsp_ref[h] * mask).astype(BF16) for h in range(N_HEADS)]

        dya = dcat[:, :D_A]
        for h in range(N_HEADS):
            cols = slice(h * GROUP, (h + 1) * GROUP)
            bs_sum = jnp.zeros((CHUNK, GROUP), F32)
            ws_sum = jnp.zeros((CHUNK, CHUNK), F32)
            for ch in range(tt // CHUNK):
                rows = slice(ch * CHUNK, (ch + 1) * CHUNK)
                v_ch = vb[rows, cols]
                mixed = _dot(wc[h], v_ch) + bs_ref[:, cols]
                dy_ch = dya[rows, cols]
                dz_scr[rows, cols] = dy_ch * mixed
                dmixed = dy_ch * u[rows, cols]
                dmb = dmixed.astype(BF16)
                dz_scr[rows, D_A + h * GROUP:D_A + (h + 1) * GROUP] = _dot_tn(wc[h], dmb)
                bs_sum = bs_sum + dmixed
                ws_sum = ws_sum + _dot_nt(dmb, v_ch)
            _acc_rows(bs_acc, 0, h, bs_sum)
            _acc_rows(small_ref, ROW_WS, h, ws_sum)

        dvl = dz_scr[:, D_A:2 * D_A]
        dvhat = dvl * lng_ref[...]
        dvl_vhat = dvl * vhat
        dvr = rv * (dvhat - jnp.mean(dvhat, axis=-1, keepdims=True)
                    - vhat * jnp.mean(dvl_vhat * lng_ref[...], axis=-1, keepdims=True))
        small_ref[ROW_LN:ROW_LN + 1, 0:D_A] += _colsum(dvl_vhat)
        small_ref[ROW_LN:ROW_LN + 1, D_A:D] += _colsum(dvl)
        dga = jnp.concatenate([dz_scr[:, :D_A], dvr], axis=1)
        dza = dga * _gelu_grad(z[:, :2 * D_A], t_g)

        zb = z[:, 2 * D_A:]
        halo_prev = jnp.where(i == 0, 0.0, zprev_ref[...])
        diff = _pool_diff(zb, halo_prev, i * tt)
        dyb = dcat[:, D_A:]
        inv = _inv_counts(i * tt, tt)
        scaled, ddiffs = [], []
        for g in range(len(WINDOWS)):
            cols = slice(g * GROUP, (g + 1) * GROUP)
            db = diff[g].astype(BF16)
            wpg = wp_ref[g].astype(BF16)
            pre = _dot(db, wpg) + bp_ref[:, cols]
            small_ref[ROW_POOL:ROW_POOL + 1, cols] += _colsum(dyb[:, cols] * pre)
            dpre = dyb[:, cols] * ps_ref[:, cols]
            small_ref[ROW_POOL:ROW_POOL + 1, D_B + g * GROUP:D_B + (g + 1) * GROUP] += _colsum(dpre)
            dpb = dpre.astype(BF16)
            _acc_rows(small_ref, ROW_WP, g, _dot_tn(db, dpb))
            ddiff = _dot_nt(dpb, wpg)
            ddiffs.append(ddiff)
            scaled.append(ddiff * inv[g])
        scaled_all = jnp.concatenate(scaled, axis=1)
        ext = jnp.concatenate([scaled_all, carry[...]], axis=0)
        n_ext = tt + HALO
        s2 = ext + pltpu.roll(ext, n_ext - 1, 0)
        t4 = s2[:, GROUP:]
        s4 = t4 + pltpu.roll(t4, n_ext - 2, 0)
        t8 = s4[:, GROUP:]
        s8 = t8 + pltpu.roll(t8, n_ext - 4, 0)
        t16 = s8[:, GROUP:]
        s16 = t16 + pltpu.roll(t16, n_ext - 8, 0)
        back = [s2[:, :GROUP], s4[:, :GROUP], s8[:, :GROUP], s16]
        carry[...] = scaled_all[:HALO, :]
        dzb = jnp.concatenate([back[g][:tt, :] - ddiffs[g] for g in range(len(WINDOWS))], axis=1)

        dzv = jnp.concatenate([dza, dzb], axis=1).astype(BF16)
        dz_ref[...] = dzv
        xv = x_ref[...]
        h1 = (xv * _rstd(xv) * (n1pre_ref[...] * (1.0 + mod_ref[1:2, :])) + mod_ref[0:1, :]).astype(BF16)
        acc_in[...] += _dot_tn(dzv, h1)

        @pl.when(s == nt - 1)
        def _():
            gwin_ref[...] = acc_in[...].astype(BF16)
            gwout_ref[...] = acc_out[...].astype(BF16)
            bs = _unfold(bs_acc[...])
            for h in range(N_HEADS):
                small_ref[ROW_BS + h:ROW_BS + h + 1, 0:GROUP] = jnp.sum(
                    bs[:, h * GROUP:(h + 1) * GROUP].T, axis=0, keepdims=True)
            for a in range(n_sums):
                for r in range(2):
                    chip_copy(a, r).wait_recv()
                    chip_copy(a, r).wait_send()

    rev = lambda w: pl.BlockSpec((tt, w), lambda s: (nt - 1 - s, 0))
    zprev = pl.BlockSpec((HALO, D_B), lambda s: (jnp.maximum((nt - 1 - s) * hb - 1, 0), 2))
    hbm = pl.BlockSpec(memory_space=pl.ANY)
    outs = pl.pallas_call(
        body, name="attn_bwd", grid=(nt,),
        out_shape=tuple([jax.ShapeDtypeStruct((t_len, D_Z), BF16), jax.ShapeDtypeStruct((D_Z, D), BF16),
                         jax.ShapeDtypeStruct((D, D), BF16), jax.ShapeDtypeStruct((SMALL_ROWS, D), F32)]
                        + [jax.ShapeDtypeStruct(cs.shape, cs.dtype) for cs in chip_sums]),
        in_specs=[rev(D), rev(D), rev(D_Z), zprev, rev(D), _full((8, D)), _full((1, D)),
                  _resident((D, D)), _full((N_HEADS, CHUNK, CHUNK)), _full((CHUNK, D_A)),
                  _full((1, D_A)), _full((1, D_A)), _full((len(WINDOWS), GROUP, GROUP)), _full((1, D_B)),
                  _full((1, D_B)), _full((8, D)), _full((8, D))] + [_resident(cs.shape) for cs in chip_sums],
        out_specs=tuple([rev(D_Z), _resident((D_Z, D)), _resident((D, D)), _full((SMALL_ROWS, D))]
                        + [hbm] * n_sums),
        scratch_shapes=[pltpu.VMEM((HALO, D_B), F32), pltpu.VMEM((D_Z, D), F32), pltpu.VMEM((D, D), F32),
                        pltpu.VMEM((tt, 2 * D_A), F32), pltpu.VMEM((CHUNK // 2, D), F32),
                        pltpu.SemaphoreType.DMA((2 * n_sums,)), pltpu.SemaphoreType.DMA((2 * n_sums,))],
        compiler_params=pltpu.CompilerParams(dimension_semantics=("arbitrary",), vmem_limit_bytes=VMEM_LIMIT),
    )(dmix, x, z, z, cat, mod, n1pre, w_out, w_sp, bs_rows, ln_g, ln_b, w_pool, b_pool, pool_scale,
      red_fwd, red_bwd, *chip_sums)
    return outs[:4], outs[4:]


def _in_proj_bwd(tt, dz, dx1, x, mod, n1pre, w_in_t):
    t_len = x.shape[0]
    nt = t_len // tt

    def body(dz_ref, dx1_ref, x_ref, mod_ref, n1pre_ref, win_ref, gx_ref, sums_ref):
        i = pl.program_id(0)

        @pl.when(i == 0)
        def _():
            sums_ref[...] = jnp.zeros_like(sums_ref)

        dh1 = _dot(dz_ref[...], win_ref[...])
        xv = x_ref[...]
        r1 = _rstd(xv)
        xhat = xv * r1
        scale1 = mod_ref[1:2, :]
        pre1 = n1pre_ref[...]
        gain1 = pre1 * (1.0 + scale1)
        d_x, sum_h = _rms_bwd_gained(dh1, gain1, xhat, r1)
        gx_ref[...] = dx1_ref[...] + d_x
        sums_ref[ROW_DMOD:ROW_DMOD + 1, :] += _colsum(dh1)
        sums_ref[ROW_DMOD + 1:ROW_DMOD + 2, :] += pre1 * sum_h
        sums_ref[ROW_N1PRE:ROW_N1PRE + 1, :] += (1.0 + scale1) * sum_h

    tile = lambda w: pl.BlockSpec((tt, w), lambda i: (i, 0))
    return pl.pallas_call(
        body, name="in_proj_bwd", grid=(nt,),
        out_shape=(jax.ShapeDtypeStruct((t_len, D), F32), jax.ShapeDtypeStruct((16, D), F32)),
        in_specs=[tile(D_Z), tile(D), tile(D), _full((8, D)), _full((1, D)), _resident((D_Z, D))],
        out_specs=(tile(D), _full((16, D))),
        compiler_params=pltpu.CompilerParams(dimension_semantics=("arbitrary",), vmem_limit_bytes=VMEM_LIMIT),
    )(dz, dx1, x, mod, n1pre, w_in_t)


def _adam(w, g, m, v):
    m2 = ADAM_B1 * m + (1.0 - ADAM_B1) * g
    v2 = ADAM_B2 * v + (1.0 - ADAM_B2) * (g * g)
    m_hat = m2 / (1.0 - ADAM_B1 ** ADAM_STEP)
    v_hat = v2 / (1.0 - ADAM_B2 ** ADAM_STEP)
    delta = -ADAM_LR * (m_hat / (jnp.sqrt(v_hat) + ADAM_EPS) + ADAM_WD * w)
    return delta, m2, v2


def _adamw_fc(steps, fc):
    n_fc = len(fc)

    def body(*refs):
        ins, outs = refs[:6 * n_fc], refs[6 * n_fc:]
        for k in range(n_fc):
            w_ref, own_ref, arr_ref, diag_ref, m_ref, v_ref = ins[6 * k:6 * k + 6]
            g = ((own_ref[...] + arr_ref[0].astype(F32)) + arr_ref[1].astype(F32)) + diag_ref[...].astype(F32)
            outs[4 * k][...] = g
            outs[4 * k + 1][...], outs[4 * k + 2][...], outs[4 * k + 3][...] = _adam(
                w_ref[...], g, m_ref[...], v_ref[...])

    specs_in, specs_out, shapes, args = [], [], [], []
    for w, own, arrived, diagonal, m, v in fc:
        rows, cols = w.shape
        blk = pl.BlockSpec((rows // steps, cols), lambda i: (i, 0))
        specs_in += [blk, blk, pl.BlockSpec((2, rows // steps, cols), lambda i: (0, i, 0)), blk, blk, blk]
        specs_out += [blk] * 4
        shapes += [jax.ShapeDtypeStruct((rows, cols), F32)] * 4
        args += [w, own, arrived, diagonal, m, v]
    outs = pl.pallas_call(
        body, name="adamw_fc", grid=(steps,), out_shape=tuple(shapes), in_specs=specs_in, out_specs=tuple(specs_out),
        compiler_params=pltpu.CompilerParams(dimension_semantics=("arbitrary",), vmem_limit_bytes=VMEM_LIMIT),
    )(*args)
    return [outs[4 * k:4 * k + 4] for k in range(n_fc)]


def _adamw_ada(rb, w, sc, dmod_cols, m, v):
    rows, cols = w.shape

    def body(w_ref, sc_ref, dm_ref, m_ref, v_ref, g_ref, d_ref, m2_ref, v2_ref):
        g = _dot_tn(sc_ref[...].astype(BF16), dm_ref[...].astype(BF16))
        g_ref[...] = g
        d_ref[...], m2_ref[...], v2_ref[...] = _adam(w_ref[...], g, m_ref[...], v_ref[...])

    blk = pl.BlockSpec((rb, cols), lambda i: (i, 0))
    shp = jax.ShapeDtypeStruct((rows, cols), F32)
    return pl.pallas_call(
        body, name="adamw_ada", grid=(rows // rb,), out_shape=(shp, shp, shp, shp),
        in_specs=[blk, pl.BlockSpec((N_DEV, rb), lambda i: (0, i)), _full((N_DEV, cols)), blk, blk],
        out_specs=(blk, blk, blk, blk),
        compiler_params=pltpu.CompilerParams(dimension_semantics=("arbitrary",)),
    )(w, sc, dmod_cols, m, v)


def _unfold(acc_rows):
    return jnp.concatenate([acc_rows[:, :D_A], acc_rows[:, D_A:]], axis=0)


def _adamw_small(total, params, shards):
    n = len(params)
    flat = [a for p in params for a in p]

    def body(*refs):
        s_ref = refs[0]
        p_refs = refs[1:1 + 3 * n]
        s_refs = refs[1 + 3 * n:1 + 3 * n + 4 * len(shards)]
        loss_ref = refs[1 + 3 * n + 4 * len(shards)]
        o_refs = refs[2 + 3 * n + 4 * len(shards):2 + 3 * n + 4 * len(shards) + 4 * n]
        so_refs = refs[2 + 3 * n + 4 * len(shards) + 4 * n:]
        d_b_ada = s_ref[0:6, :]
        for b in range(1, N_DEV):
            d_b_ada = d_b_ada + s_ref[_table_row(b):_table_row(b) + 6, :]
        misc = lambda r: s_ref[PK_MISC + r - ROW_N1PRE:PK_MISC + r - ROW_N1PRE + 1, :]
        loss = jnp.sum(misc(ROW_LOSS), axis=-1, keepdims=True) * (0.5 / D)
        loss_ref[...] = loss
        mask = _tril_mask()
        ws = _unfold(s_ref[PK_WS:PK_WS + 64, :])
        wp = _unfold(s_ref[PK_WP:PK_WP + 64, :])
        grads = [
            d_b_ada,
            misc(ROW_N1PRE), misc(ROW_N1POST), misc(ROW_N2PRE), misc(ROW_N2POST),
            misc(ROW_LN)[:, :D_A], misc(ROW_LN)[:, D_A:],
            misc(ROW_POOL)[:, :D_B], misc(ROW_POOL)[:, D_B:],
            s_ref[PK_BS:PK_BS + N_HEADS, 0:GROUP],
            jnp.stack([ws[:, h * GROUP:(h + 1) * GROUP] * mask for h in range(N_HEADS)]),
            jnp.stack([wp[:, g * GROUP:(g + 1) * GROUP] for g in range(len(WINDOWS))]),
        ]
        for k in range(n):
            w_ref, m_ref, v_ref = p_refs[3 * k:3 * k + 3]
            g = grads[k]
            if k == 0:
                for j in range(6):
                    o_refs[0][j] = g[j:j + 1, :]
                    o_refs[1][j], o_refs[2][j], o_refs[3][j] = _adam(w_ref[j], g[j:j + 1, :], m_ref[j], v_ref[j])
                continue
            o_refs[4 * k][...] = g
            o_refs[4 * k + 1][...], o_refs[4 * k + 2][...], o_refs[4 * k + 3][...] = _adam(
                w_ref[...], g, m_ref[...], v_ref[...])
        for k in range(len(shards)):
            w_ref, g_ref, m_ref, v_ref = s_refs[4 * k:4 * k + 4]
            so_refs[3 * k][...], so_refs[3 * k + 1][...], so_refs[3 * k + 2][...] = _adam(
                w_ref[...], g_ref[...], m_ref[...], v_ref[...])

    vm = pl.BlockSpec(memory_space=pltpu.VMEM)
    out_shape = [jax.ShapeDtypeStruct((1, 1), F32)]
    for w, _, _ in params:
        out_shape += [jax.ShapeDtypeStruct(w.shape, F32)] * 4
    for w, _, _, _ in shards:
        out_shape += [jax.ShapeDtypeStruct(w.shape, F32)] * 3
    return pl.pallas_call(
        body, name="adamw_small", out_shape=tuple(out_shape),
        in_specs=[vm] * (1 + 3 * n + 4 * len(shards)), out_specs=tuple([vm] * len(out_shape)),
        compiler_params=pltpu.CompilerParams(vmem_limit_bytes=VMEM_LIMIT),
    )(total, *flat, *[a for s in shards for a in s])


TT_ATTN_FWD = 512
ATTN_LAG = 2
TT_MLP_FWD = 512
TT_MLP = 256
TT_WGRAD = 2048
TT_ATTN_BWD = 512
TT_IN_PROJ_BWD = 512


def kernel(x, c, w_ada, b_ada, norm1_pre, norm1_post, w_in, w_spatial, b_spatial, ln_v_gain, ln_v_bias, w_pool, b_pool, pool_scale, w_out, norm2_pre, norm2_post, w_fc1, w_fc2, loss_target, m_w_ada, m_b_ada, m_norm1_pre, m_norm1_post, m_w_in, m_w_spatial, m_b_spatial, m_ln_v_gain, m_ln_v_bias, m_w_pool, m_b_pool, m_pool_scale, m_w_out, m_norm2_pre, m_norm2_post, m_w_fc1, m_w_fc2, v_w_ada, v_b_ada, v_norm1_pre, v_norm1_post, v_w_in, v_w_spatial, v_b_spatial, v_ln_v_gain, v_ln_v_bias, v_w_pool, v_b_pool, v_pool_scale, v_w_out, v_norm2_pre, v_norm2_post, v_w_fc1, v_w_fc2):
    t_len = x.shape[1]
    me = 4 * lax.axis_index("x") + 2 * lax.axis_index("y") + lax.axis_index("c")
    ada_cols = w_ada.shape[1]
    tt = lambda want: min(want, t_len)

    x2 = x.reshape(t_len, D)
    tgt = loss_target.reshape(t_len, D)
    row = lambda a: a.reshape(1, -1)

    w_in_shard, w_out_shard, w1_shard, w2_shard = _cast_shards([w_in.T, w_out, w_fc1, w_fc2])

    bs_rows = jnp.repeat(b_spatial.T, GROUP, axis=1)
    attn_consts = (w_spatial, bs_rows, row(ln_v_gain), row(ln_v_bias), w_pool, row(b_pool), row(pool_scale))

    (z, cat, mix, x1, h2, r_begun, f_head, mod, sc), (w1_early, w2_early), w_out_all, w_in_t = _attn_fwd(
        tt(TT_ATTN_FWD), x2, c.reshape(1, D), w_ada, b_ada.reshape(N_DEV, 1, ada_cols), row(norm1_pre),
        row(norm1_post),
        w_in_shard, w_out_shard, *attn_consts, (w1_shard, w2_shard), row(norm2_pre))
    (r_early, f_early), (w1_late, w2_late) = _mlp_fwd_early(
        tt(TT_MLP_FWD), r_begun, h2, f_head, w1_early, w2_early)
    r_late, df, da, dmix, dx1, red_fwd, red_bwd = _mlp_late_bwd(
        tt(TT_MLP), r_early, x1, h2, f_early, tgt, mix, mod, row(norm2_pre), row(norm2_post), row(norm1_post),
        w1_early, w2_early, w1_late, w2_late)
    own_w1, own_w2, sums_w1, sums_w2, diag_w1, diag_w2 = _mlp_wgrad(tt(TT_WGRAD), r_early, r_late, da, df, h2)
    (dz, p_in, p_out, small), (arr_w1, arr_w2) = _attn_bwd(
        tt(TT_ATTN_BWD), dmix, x2, z, cat, mod, row(norm1_pre), w_out_all, *attn_consts, red_fwd, red_bwd,
        [sums_w1, sums_w2])
    grad_x, small_head = _in_proj_bwd(tt(TT_IN_PROJ_BWD), dz, dx1, x2, mod, row(norm1_pre), w_in_t)
    (grad_w1, d_w1, m_w1, v_w1), (grad_w2, d_w2, m_w2, v_w2) = _adamw_fc(
        4, [(w_fc1, own_w1, arr_w1, diag_w1, m_w_fc1, v_w_fc1), (w_fc2, own_w2, arr_w2, diag_w2, m_w_fc2, v_w_fc2)])
    grad_in_t, grad_out, total = _tail_comm(
        [p_in.reshape(N_DEV, D_Z // N_DEV, D), p_out.reshape(N_DEV, D // N_DEV, D)], small, small_head, 64)

    table = jnp.concatenate([total[0:PACK_FINE, :], total[PK_TABLE_B:PK_MISC, :]], axis=0)
    dmod_all = table.reshape(N_DEV, 8, D)[:, :6, :].reshape(N_DEV, 6 * D)
    dmod_cols = lax.dynamic_slice_in_dim(dmod_all, me * ada_cols, ada_cols, axis=1)
    grad_ada, d_ada, m_ada, v_ada = _adamw_ada(256, w_ada, sc, dmod_cols, m_w_ada, v_w_ada)

    six = lambda a: a.reshape(6, 1, D)
    small_params = [
        (six(b_ada), six(m_b_ada), six(v_b_ada)),
        (row(norm1_pre), row(m_norm1_pre), row(v_norm1_pre)),
        (row(norm1_post), row(m_norm1_post), row(v_norm1_post)),
        (row(norm2_pre), row(m_norm2_pre), row(v_norm2_pre)),
        (row(norm2_post), row(m_norm2_post), row(v_norm2_post)),
        (row(ln_v_gain), row(m_ln_v_gain), row(v_ln_v_gain)),
        (row(ln_v_bias), row(m_ln_v_bias), row(v_ln_v_bias)),
        (row(pool_scale), row(m_pool_scale), row(v_pool_scale)),
        (row(b_pool), row(m_b_pool), row(v_b_pool)),
        (b_spatial, m_b_spatial, v_b_spatial),
        (w_spatial, m_w_spatial, v_w_spatial),
        (w_pool, m_w_pool, v_w_pool),
    ]
    outs = _adamw_small(total, small_params, [(w_out, grad_out, m_w_out, v_w_out),
                                              (w_in.T, grad_in_t, m_w_in.T, v_w_in.T)])
    d_out, m_out, v_out, d_in_t, m_in_t, v_in_t = outs[1 + 4 * len(small_params):]
    loss = outs[0].reshape(())
    names = ["b_ada", "norm1_pre", "norm1_post", "norm2_pre", "norm2_post", "ln_v_gain", "ln_v_bias", "pool_scale",
             "b_pool", "b_spatial", "w_spatial", "w_pool"]
    shapes = dict(b_ada=b_ada.shape, norm1_pre=norm1_pre.shape, norm1_post=norm1_post.shape,
                  norm2_pre=norm2_pre.shape, norm2_post=norm2_post.shape, ln_v_gain=ln_v_gain.shape,
                  ln_v_bias=ln_v_bias.shape, pool_scale=pool_scale.shape, b_pool=b_pool.shape,
                  b_spatial=b_spatial.shape, w_spatial=w_spatial.shape, w_pool=w_pool.shape)
    res = {}
    for k, nm in enumerate(names):
        res[nm] = tuple(o.reshape(shapes[nm]) for o in outs[1 + 4 * k:5 + 4 * k])
    res["w_ada"] = (grad_ada, d_ada, m_ada, v_ada)
    res["w_in"] = (grad_in_t.T, d_in_t.T, m_in_t.T, v_in_t.T)
    res["w_out"] = (grad_out, d_out, m_out, v_out)
    res["w_fc1"] = (grad_w1, d_w1, m_w1, v_w1)
    res["w_fc2"] = (grad_w2, d_w2, m_w2, v_w2)

    order = ["w_ada", "b_ada", "norm1_pre", "norm1_post", "w_in", "w_spatial", "b_spatial", "ln_v_gain", "ln_v_bias",
             "w_pool", "b_pool", "pool_scale", "w_out", "norm2_pre", "norm2_post", "w_fc1", "w_fc2"]
    return (loss, grad_x.reshape(x.shape),
            *[res[nm][0] for nm in order], *[res[nm][1] for nm in order],
            *[res[nm][2] for nm in order], *[res[nm][3] for nm in order])
```

```python
import functools

import jax
import jax.numpy as jnp
from jax import lax
from jax.experimental import pallas as pl
from jax.experimental.pallas import tpu as pltpu

F32 = jnp.float32
BF16 = jnp.bfloat16
MESH = pl.DeviceIdType.MESH

N_DEV = 8
D = 1024
D_A = 512
D_B = 512
D_Z = 2 * D_A + D_B
N_HEADS = 4
CHUNK = 128
WINDOWS = (2, 4, 8, 16)
GROUP = 128
D_FF = 4096
FF_BLK = D_FF // N_DEV
HALO = 16
EPS = 1e-6
VMEM_LIMIT = 60 * 1024 * 1024

ADAM_LR = 0.001
ADAM_B1 = 0.9
ADAM_B2 = 0.999
ADAM_EPS = 1e-08
ADAM_WD = 0.01
ADAM_STEP = 10

ROW_DMOD = 0
ROW_N1PRE, ROW_N1POST, ROW_N2PRE, ROW_N2POST = 8, 9, 10, 11
ROW_LN = 12
ROW_POOL = 13
ROW_LOSS = 14
ROW_BS = 16
ROW_WS = 24
ROW_WP = 88
SMALL_ROWS = 152
PACK_FINE = 40
PACK_HALF = PACK_FINE + 64
PACK_ROWS = 2 * PACK_HALF
PK_WS = PACK_FINE
PK_TABLE_B = PACK_HALF
PK_MISC = PK_TABLE_B + 24
PK_BS = PK_MISC + 8
PK_WP = PK_BS + 8


def _table_row(b):
    if isinstance(b, int):
        return 8 * b if 8 * b < PACK_FINE else 8 * b + PK_TABLE_B - PACK_FINE
    return 8 * b + jnp.where(8 * b < PACK_FINE, 0, PK_TABLE_B - PACK_FINE)


def _dot(a, b):
    return jnp.dot(a, b, preferred_element_type=F32)


def _dot_nt(a, b):
    return lax.dot_general(a, b, (((1,), (1,)), ((), ())), preferred_element_type=F32)


def _dot_tn(a, b):
    return lax.dot_general(a, b, (((0,), (0,)), ((), ())), preferred_element_type=F32)


def _rstd(v):
    return lax.rsqrt(jnp.mean(v * v, axis=-1, keepdims=True) + EPS)


def _rms_bwd(d_hat, hat, rstd):
    return rstd * (d_hat - hat * jnp.mean(d_hat * hat, axis=-1, keepdims=True))


def _rms_bwd_gained(g, gain, hat, rstd):
    g_hat = g * hat
    d_v = rstd * (g * gain - hat * jnp.mean(g_hat * gain, axis=-1, keepdims=True))
    return d_v, _colsum(g_hat)


_K0 = 0.7978845608028654
_K1 = 0.044715


def _gelu_parts(v):
    t = jnp.tanh(v * (_K0 + (_K0 * _K1) * (v * v)))
    return t, v * (0.5 + 0.5 * t)


def _gelu_grad(v, t):
    return (0.5 + 0.5 * t) + (0.5 * v) * (1.0 - t * t) * (_K0 + (3.0 * _K0 * _K1) * (v * v))


def _colsum(v):
    return jnp.sum(v, axis=0, keepdims=True)


def _full(shape):
    n = len(shape)
    return pl.BlockSpec(shape, lambda *_: (0,) * n)


def _resident(shape):
    n = len(shape)
    return pl.BlockSpec(shape, lambda *_: (0,) * n, pipeline_mode=pl.Buffered(1))


def _place():
    x, y, c = lax.axis_index("x"), lax.axis_index("y"), lax.axis_index("c")
    return x, y, c


def _flip(v, bit):
    return 1 - v if bit else v


def _peer(x, y, c, k):
    return (_flip(x, (k >> 2) & 1), _flip(y, (k >> 1) & 1), _flip(c, k & 1))


def _index(p):
    return 4 * p[0] + 2 * p[1] + p[2]


def _cast_shards(shards):
    def body(*refs):
        for src, dst in zip(refs[:len(shards)], refs[len(shards):]):
            dst[...] = src[...].astype(BF16)

    vm = pl.BlockSpec(memory_space=pltpu.VMEM)
    return pl.pallas_call(
        body, name="cast_shards", out_shape=tuple(jax.ShapeDtypeStruct(s.shape, BF16) for s in shards),
        in_specs=[vm] * len(shards), out_specs=tuple([vm] * len(shards)),
    )(*shards)


FC_EARLY = 6
FC_HEAD = 2
R_HEAD_COLS = (FC_EARLY - FC_HEAD) * FF_BLK
WGRAD_ORDER = (7, 6, 1, 3, 5, 2, 4, 0)


def _early_col(j):
    return R_HEAD_COLS + j * FF_BLK if j < FC_HEAD else (j - FC_HEAD) * FF_BLK


class _Copies:
    def __init__(self, entries, send_sems, recv_sems):
        self.place = _place()
        self.entries, self.send_sems, self.recv_sems = entries, send_sems, recv_sems

    def _copy(self, i, arrival=False):
        src, dst, rel = self.entries[i]
        return pltpu.make_async_remote_copy(
            src_ref=dst if arrival else src, dst_ref=dst, send_sem=self.send_sems.at[i],
            recv_sem=self.recv_sems.at[i], device_id=_peer(*self.place, rel), device_id_type=MESH)

    def start(self, *which):
        for i in which:
            self._copy(i).start()

    def wait_recv(self, *which):
        for i in which:
            self._copy(i, arrival=True).wait_recv()

    def wait_send(self, *which):
        for i in which:
            self._copy(i).wait_send()


TAIL_STEPS = 3


def _tail_comm(parts, small, head, row_chunk):
    n = len(parts)

    def body(*refs):
        p_refs, small_ref, head_ref = refs[:n], refs[n], refs[n + 1]
        outs = refs[n + 2:]
        g_refs, total_ref = outs[:n], outs[n]
        scr = outs[n + 1:]
        from_sib = scr[0:n]
        chip_out = scr[n:2 * n]
        chip_in = scr[2 * n:3 * n]
        pack, pack_sib, fine, bulk, total_scr = scr[3 * n:3 * n + 5]
        send_a, recv_a, send_b, recv_b, send_s, recv_s = scr[3 * n + 5:]
        step = pl.program_id(0)
        x, y, c = _place()
        me = _index((x, y, c))
        sibling = (x, y, 1 - c)
        my_chip = 2 * x + y
        others = [(1 - x, y), (x, 1 - y), (1 - x, 1 - y)]
        my_half = pl.ds(pl.multiple_of(PACK_HALF * c, 8), PACK_HALF)

        def pack_to_sibling():
            return pltpu.make_async_remote_copy(
                src_ref=pack, dst_ref=pack_sib, send_sem=send_s.at[0], recv_sem=recv_s.at[0],
                device_id=sibling, device_id_type=MESH)

        def half_to_chip(r, part):
            buf = (fine, bulk)[part]
            return pltpu.make_async_remote_copy(
                src_ref=buf.at[my_chip], dst_ref=buf.at[my_chip],
                send_sem=send_s.at[1 + 3 * part + r], recv_sem=recv_s.at[1 + 3 * part + r],
                device_id=(*others[r], c), device_id_type=MESH)

        def half_from_chip(r, part):
            k = 2 * others[r][0] + others[r][1]
            buf = (fine, bulk)[part]
            return pltpu.make_async_remote_copy(
                src_ref=buf.at[k], dst_ref=buf.at[k],
                send_sem=send_s.at[1 + 3 * part + r], recv_sem=recv_s.at[1 + 3 * part + r],
                device_id=(*others[r], c), device_id_type=MESH)

        def total_to_sibling():
            return pltpu.make_async_remote_copy(
                src_ref=total_scr.at[my_half], dst_ref=total_scr.at[my_half],
                send_sem=send_s.at[7], recv_sem=recv_s.at[7], device_id=sibling, device_id_type=MESH)

        def total_from_sibling():
            sib_half = pl.ds(pl.multiple_of(PACK_HALF * (1 - c), 8), PACK_HALF)
            return pltpu.make_async_remote_copy(
                src_ref=total_scr.at[sib_half], dst_ref=total_scr.at[sib_half],
                send_sem=send_s.at[7], recv_sem=recv_s.at[7], device_id=sibling, device_id_type=MESH)

        def to_sibling(a, k):
            return pltpu.make_async_remote_copy(
                src_ref=p_refs[a].at[2 * k + (1 - c)], dst_ref=from_sib[a].at[k],
                send_sem=send_a.at[a], recv_sem=recv_a.at[a], device_id=sibling, device_id_type=MESH)

        def all_from_sibling(a):
            return pltpu.make_async_remote_copy(
                src_ref=from_sib[a], dst_ref=from_sib[a], send_sem=send_a.at[a], recv_sem=recv_a.at[a],
                device_id=sibling, device_id_type=MESH)

        def to_chip(a, r):
            return pltpu.make_async_remote_copy(
                src_ref=chip_out[a].at[r], dst_ref=chip_in[a].at[r],
                send_sem=send_b.at[3 * a + r], recv_sem=recv_b.at[3 * a + r],
                device_id=(*others[r], c), device_id_type=MESH)

        @pl.when(step == 0)
        def _():
            pack[0:PACK_FINE, :] = jnp.zeros((PACK_FINE, D), F32)
            pack[PK_TABLE_B:PK_MISC, :] = jnp.zeros((PK_MISC - PK_TABLE_B, D), F32)
            pack[pl.ds(pl.multiple_of(_table_row(me), 8), 8), :] = small_ref[0:8, :] + head_ref[0:8, :]
            pack[PK_WS:PK_WS + 64, :] = small_ref[ROW_WS:ROW_WS + 64, :]
            pack[PK_MISC:PK_MISC + 8, :] = small_ref[ROW_N1PRE:ROW_N1PRE + 8, :] + head_ref[8:16, :]
            pack[PK_BS:PK_BS + 8, :] = small_ref[ROW_BS:ROW_BS + 8, :]
            pack[PK_WP:PK_WP + 64, :] = small_ref[ROW_WP:ROW_WP + 64, :]
            pack_to_sibling().start()
            for a in range(n):
                for k in range(4):
                    to_sibling(a, k).start()

        @pl.when(step == 1)
        def _():
            pack_to_sibling().wait_recv()
            chip_sum = pack[my_half, :] + pack_sib[my_half, :]
            fine[my_chip] = chip_sum[:PACK_FINE, :]
            bulk[my_chip] = chip_sum[PACK_FINE:, :].astype(BF16)
            for r in range(3):
                half_to_chip(r, 0).start()
                half_to_chip(r, 1).start()
            for a in range(n):
                all_from_sibling(a).wait_recv()
                rows = p_refs[a].shape[1]
                for r in range(3):
                    k = 2 * others[r][0] + others[r][1]
                    for s in range(0, rows, row_chunk):
                        sl = pl.ds(s, row_chunk)
                        chip_out[a][r, sl, :] = (p_refs[a][2 * k + c, sl, :].astype(F32)
                                                 + from_sib[a][k, sl, :].astype(F32)).astype(BF16)
                    to_chip(a, r).start()
                for s in range(0, rows, row_chunk):
                    sl = pl.ds(s, row_chunk)
                    g_refs[a][sl, :] = (p_refs[a][2 * my_chip + c, sl, :].astype(F32)
                                        + from_sib[a][my_chip, sl, :].astype(F32))

        @pl.when(step == TAIL_STEPS - 1)
        def _():
            for r in range(3):
                half_from_chip(r, 0).wait_recv()
                half_from_chip(r, 1).wait_recv()
            half_start = pl.multiple_of(PACK_HALF * c, 8)
            total_scr[pl.ds(half_start, PACK_FINE), :] = ((fine[0] + fine[1]) + fine[2]) + fine[3]
            total_scr[pl.ds(half_start + PACK_FINE, PACK_HALF - PACK_FINE), :] = (
                (bulk[0].astype(F32) + bulk[1].astype(F32)) + bulk[2].astype(F32)) + bulk[3].astype(F32)
            total_to_sibling().start()
            for a in range(n):
                rows = p_refs[a].shape[1]
                for r in range(3):
                    to_chip(a, r).wait_recv()
                    for s in range(0, rows, row_chunk):
                        sl = pl.ds(s, row_chunk)
                        g_refs[a][sl, :] = g_refs[a][sl, :] + chip_in[a][r, sl, :].astype(F32)
            total_from_sibling().wait_recv()
            total_ref[...] = total_scr[...]
            for a in range(n):
                all_from_sibling(a).wait_send()
                for r in range(3):
                    to_chip(a, r).wait_send()
            pack_to_sibling().wait_send()
            for r in range(3):
                half_to_chip(r, 0).wait_send()
                half_to_chip(r, 1).wait_send()
            total_to_sibling().wait_send()

    return pl.pallas_call(
        body, name="tail_comm", grid=(TAIL_STEPS,),
        out_shape=tuple([jax.ShapeDtypeStruct(p.shape[1:], F32) for p in parts]
                        + [jax.ShapeDtypeStruct((PACK_ROWS, D), F32)]),
        in_specs=[_resident(p.shape) for p in parts] + [_resident(small.shape), _resident(head.shape)],
        out_specs=tuple([_full(p.shape[1:]) for p in parts] + [_full((PACK_ROWS, D))]),
        scratch_shapes=(
            [pltpu.VMEM((4,) + p.shape[1:], BF16) for p in parts]
            + [pltpu.VMEM((3,) + p.shape[1:], BF16) for p in parts]
            + [pltpu.VMEM((3,) + p.shape[1:], BF16) for p in parts]
            + [pltpu.VMEM((PACK_ROWS, D), F32), pltpu.VMEM((PACK_ROWS, D), F32),
               pltpu.VMEM((4, PACK_FINE, D), F32), pltpu.VMEM((4, PACK_HALF - PACK_FINE, D), BF16),
               pltpu.VMEM((PACK_ROWS, D), F32)]
            + [pltpu.SemaphoreType.DMA((n,)), pltpu.SemaphoreType.DMA((n,)),
               pltpu.SemaphoreType.DMA((3 * n,)), pltpu.SemaphoreType.DMA((3 * n,)),
               pltpu.SemaphoreType.DMA((8,)), pltpu.SemaphoreType.DMA((8,))]),
        compiler_params=pltpu.CompilerParams(dimension_semantics=("arbitrary",), vmem_limit_bytes=VMEM_LIMIT),
    )(*parts, small, head)


def _tril_mask():
    row = lax.broadcasted_iota(jnp.int32, (CHUNK, CHUNK), 0)
    col = lax.broadcasted_iota(jnp.int32, (CHUNK, CHUNK), 1)
    return (col <= row).astype(F32)


def _window_sums(ext):
    s2 = ext + pltpu.roll(ext, 1, 0)
    t4 = s2[:, GROUP:]
    s4 = t4 + pltpu.roll(t4, 2, 0)
    t8 = s4[:, GROUP:]
    s8 = t8 + pltpu.roll(t8, 4, 0)
    t16 = s8[:, GROUP:]
    s16 = t16 + pltpu.roll(t16, 8, 0)
    return [s2[:, :GROUP], s4[:, :GROUP], s8[:, :GROUP], s16]


def _inv_counts(first_pos, rows):
    pos = first_pos + lax.broadcasted_iota(jnp.int32, (rows, 1), 0)
    return [1.0 / jnp.minimum(pos + 1, w).astype(F32) for w in WINDOWS]


def _pool_diff(zb, halo, first_pos):
    tt = zb.shape[0]
    sums = _window_sums(jnp.concatenate([halo, zb], axis=0))
    inv = _inv_counts(first_pos, tt)
    return [sums[g][HALO:, :] * inv[g] - zb[:, g * GROUP:(g + 1) * GROUP] for g in range(len(WINDOWS))]


def _row_blocks(scr, rows, place):
    def block(rel):
        start = pl.multiple_of(rows * _index(_peer(*place, rel)), rows)
        return scr.at[pl.ds(start, rows), :]

    def entries(shard_ref):
        return ([(shard_ref, block(0), rel) for rel in (1, 2, 4, 6)]
                + [(block(rel), block(rel), 1) for rel in (2, 4, 6)])
    return block, entries


def _attn_fwd(tt, x, c_row, w_ada, b_pieces, n1pre, n1post, w_in_shard, w_out_shard, w_sp, bs_rows, ln_g, ln_b,
              w_pool, b_pool, pool_scale, fc_shards, n2pre):
    t_len = x.shape[0]
    nt = t_len // tt
    ncol = w_ada.shape[1]

    def body(x_ref, xb_ref, c_ref, wada_ref, b_ref, n1pre_ref, n1post_ref, wi_ref, wo_ref, wsp_ref, bs_ref,
             lng_ref, lnb_ref, wp_ref, bp_ref, ps_ref, w1_ref, w2_ref, n2pre_ref,
             z_ref, cat_ref, mix_ref, x1_ref, h2_ref, r_ref, f_ref, mod_out, sc_out, e1_ref, e2_ref, wout_ref,
             win_out, carry, land1, land2, sib1, sib2, cat_keep, wout_scr, win_ref, mod_ref, cg, mg, part,
             send_sems, recv_sems, local_sems, wo_send, wo_recv, wi_send, wi_recv, ada_send, ada_recv):
        i = pl.program_id(0)
        place = px, py, pc = _place()
        me = _index(place)
        wo_block, wo_entries = _row_blocks(wout_scr, w_out_shard.shape[0], place)
        wi_block, wi_entries = _row_blocks(win_ref, w_in_shard.shape[0], place)
        wo_copies = _Copies(wo_entries(wo_ref), wo_send, wo_recv)
        wi_copies = _Copies(wi_entries(wi_ref), wi_send, wi_recv)
        wo_keep = pltpu.make_async_copy(wout_scr, wout_ref, local_sems.at[8])
        wi_keep = pltpu.make_async_copy(win_ref, win_out, local_sems.at[9])
        ada = _Copies([(cg.at[me], cg.at[me], k) for k in range(1, N_DEV)]
                      + [(part, mg.at[me], k) for k in range(1, N_DEV)], ada_send, ada_recv)
        copies = _Copies(
            [(w1_ref, sib1, 1), (w2_ref, sib2, 1),
             (w1_ref, land1.at[0], 2), (w2_ref, land2.at[0], 2),
             (w1_ref, land1.at[1], 4), (w2_ref, land2.at[1], 4),
             (land1.at[0], e1_ref.at[3], 1), (land2.at[0], e2_ref.at[3], 1),
             (land1.at[1], e1_ref.at[5], 1), (land2.at[1], e2_ref.at[5], 1)],
            send_sems, recv_sems)
        keep = [pltpu.make_async_copy(w1_ref, e1_ref.at[0], local_sems.at[0]),
                pltpu.make_async_copy(w2_ref, e2_ref.at[0], local_sems.at[1]),
                pltpu.make_async_copy(land1.at[0], e1_ref.at[2], local_sems.at[2]),
                pltpu.make_async_copy(land1.at[1], e1_ref.at[4], local_sems.at[3]),
                pltpu.make_async_copy(land2.at[0], e2_ref.at[2], local_sems.at[4]),
                pltpu.make_async_copy(land2.at[1], e2_ref.at[4], local_sems.at[5]),
                pltpu.make_async_copy(sib1, e1_ref.at[1], local_sems.at[6]),
                pltpu.make_async_copy(sib2, e2_ref.at[1], local_sems.at[7])]

        @pl.when(i == 0)
        def _():
            cg[me] = jnp.broadcast_to(c_ref[...], (8, D))
            ada.start(*range(N_DEV - 1))
            wi_copies.start(0, 1, 2, 3)
            wi_rows, wo_rows = w_in_shard.shape[0], w_out_shard.shape[0]
            win_ref[pl.ds(pl.multiple_of(wi_rows * me, wi_rows), wi_rows), :] = wi_ref[...]
            wout_scr[pl.ds(pl.multiple_of(wo_rows * me, wo_rows), wo_rows), :] = wo_ref[...]
            carry[...] = jnp.zeros_like(carry)

            ada.wait_recv(*range(N_DEV - 1))
            c_all = jnp.concatenate([cg[j, 0:1, :] for j in range(N_DEV)], axis=0)
            sc = c_all * jax.nn.sigmoid(c_all)
            sc_out[...] = sc
            part[...] = _dot(sc.astype(BF16), wada_ref[...].astype(BF16)) + b_ref[me]
            ada.start(*range(N_DEV - 1, 2 * (N_DEV - 1)))
            wo_copies.start(0, 1, 2, 3)
            copies.start(0, 1, 2, 4, 3, 5)
            keep[0].start()
            keep[1].start()
            mg[me] = part[...]
            ada.wait_recv(*range(N_DEV - 1, 2 * (N_DEV - 1)))
            mod_ref[...] = jnp.zeros_like(mod_ref)
            for j in range(N_DEV):
                for m in range(6):
                    lo, hi = max(ncol * j, D * m), min(ncol * (j + 1), D * (m + 1))
                    if lo < hi:
                        mod_ref[m:m + 1, lo - D * m:hi - D * m] = mg[j, pl.ds(me, 1), lo - ncol * j:hi - ncol * j]
            mod_out[...] = mod_ref[...]

            wi_copies.wait_recv(1, 2, 3)
            wi_copies.start(4, 5, 6)
            wi_copies.wait_recv(0, 4, 5, 6)
            wi_keep.start()

        @pl.when(i == nt // 2 + ATTN_LAG)
        def _():
            copies.wait_recv(2, 4)
            copies.start(6, 8)
            keep[2].start()
            keep[3].start()

        @pl.when(i == nt - 1 + ATTN_LAG)
        def _():
            copies.wait_recv(3, 5)
            copies.start(7, 9)
            keep[4].start()
            keep[5].start()

        shift1, scale1, gate1 = mod_ref[0:1, :], mod_ref[1:2, :], mod_ref[2:3, :]

        @pl.when(i < nt)
        def _():
            xv = x_ref[...]
            h1 = (xv * _rstd(xv)) * (n1pre_ref[...] * (1.0 + scale1)) + shift1
            z = _dot_nt(h1.astype(BF16), win_ref[...])
            z_ref[...] = z

            _, ga = _gelu_parts(z[:, :2 * D_A])
            u, vr = ga[:, :D_A], ga[:, D_A:]
            dv = vr - jnp.mean(vr, axis=-1, keepdims=True)
            v = (dv * lax.rsqrt(jnp.mean(dv * dv, axis=-1, keepdims=True) + EPS)) * lng_ref[...] + lnb_ref[...]
            vb = v.astype(BF16)
            mask = _tril_mask()
            wc = [(wsp_ref[h] * mask).astype(BF16) for h in range(N_HEADS)]
            for ch in range(tt // CHUNK):
                rows = slice(ch * CHUNK, (ch + 1) * CHUNK)
                for h in range(N_HEADS):
                    cols = slice(h * GROUP, (h + 1) * GROUP)
                    mixed = _dot(wc[h], vb[rows, cols]) + bs_ref[:, cols]
                    cat_ref[rows, cols] = (u[rows, cols] * mixed).astype(BF16)

            zb = z[:, 2 * D_A:]
            diff = _pool_diff(zb, carry[...], i * tt)
            carry[...] = zb[tt - HALO:, :]
            for g in range(len(WINDOWS)):
                cols = slice(g * GROUP, (g + 1) * GROUP)
                pre = _dot(diff[g].astype(BF16), wp_ref[g].astype(BF16)) + bp_ref[:, cols]
                cat_ref[:, D_A + g * GROUP:D_A + (g + 1) * GROUP] = (pre * ps_ref[:, cols]).astype(BF16)
            cat_keep[i % (ATTN_LAG + 1)] = cat_ref[...]

        @pl.when(i == 0)
        def _():
            copies.wait_recv(0, 1)
            keep[6].start()
            keep[7].start()

        @pl.when(i == 1)
        def _():
            wo_copies.wait_recv(1, 2, 3)
            wo_copies.start(4, 5, 6)

        @pl.when(i == ATTN_LAG)
        def _():
            wo_copies.wait_recv(0, 4, 5, 6)
            wo_keep.start()

        @pl.when(i >= ATTN_LAG)
        def _():
            xv = xb_ref[...]
            mix = _dot(cat_keep[(i - ATTN_LAG) % (ATTN_LAG + 1)], wout_scr[...])
            mix_ref[...] = mix
            x1v = xv + (mix * _rstd(mix)) * (gate1 * n1post_ref[...])
            x1_ref[...] = x1v
            shift2, scale2 = mod_ref[3:4, :], mod_ref[4:5, :]
            h2 = ((x1v * _rstd(x1v)) * (n2pre_ref[...] * (1.0 + scale2)) + shift2).astype(BF16)
            h2_ref[...] = h2
            for j, (w1, w2) in enumerate(((w1_ref, w2_ref), (sib1, sib2))):
                ra = jnp.maximum(_dot(h2, w1[...]), 0.0)
                r = (ra * ra).astype(BF16)
                r_ref[:, j * FF_BLK:(j + 1) * FF_BLK] = r
                if j == 0:
                    f_ref[...] = _dot(r, w2[...])
                else:
                    f_ref[...] += _dot(r, w2[...])

        @pl.when(i == nt - 1 + ATTN_LAG)
        def _():
            copies.wait_recv(6, 7, 8, 9)
            copies.wait_send(*range(10))
            wo_copies.wait_send(*range(7))
            wi_copies.wait_send(*range(7))
            ada.wait_send(*range(2 * (N_DEV - 1)))
            for cp in keep:
                cp.wait()
            wo_keep.wait()
            wi_keep.wait()

    first = lambda w: pl.BlockSpec((tt, w), lambda i: (jnp.minimum(i, nt - 1), 0))
    second = lambda w: pl.BlockSpec((tt, w), lambda i: (jnp.maximum(i - ATTN_LAG, 0), 0))
    r_head = pl.BlockSpec((tt, FC_HEAD * FF_BLK),
                          lambda i: (jnp.maximum(i - ATTN_LAG, 0), R_HEAD_COLS // (FC_HEAD * FF_BLK)))
    hbm = pl.BlockSpec(memory_space=pl.ANY)
    outs = pl.pallas_call(
        body, name="attn_fwd", grid=(nt + ATTN_LAG,),
        out_shape=tuple([jax.ShapeDtypeStruct((t_len, D_Z), F32), jax.ShapeDtypeStruct((t_len, D), BF16),
                         jax.ShapeDtypeStruct((t_len, D), F32), jax.ShapeDtypeStruct((t_len, D), F32),
                         jax.ShapeDtypeStruct((t_len, D), BF16),
                         jax.ShapeDtypeStruct((t_len, FC_EARLY * FF_BLK), BF16),
                         jax.ShapeDtypeStruct((t_len, D), F32)]
                        + [jax.ShapeDtypeStruct((8, D), F32), jax.ShapeDtypeStruct((N_DEV, D), F32)]
                        + [jax.ShapeDtypeStruct((FC_EARLY,) + s.shape, BF16) for s in fc_shards]
                        + [jax.ShapeDtypeStruct((D, D), BF16), jax.ShapeDtypeStruct((D_Z, D), BF16)]),
        in_specs=[first(D), second(D), _full((1, D)), _resident(w_ada.shape), _full((N_DEV, 1, ncol)), _full((1, D)),
                  _full((1, D)), _resident(w_in_shard.shape), _resident(w_out_shard.shape),
                  _full((N_HEADS, CHUNK, CHUNK)), _full((CHUNK, D_A)), _full((1, D_A)), _full((1, D_A)),
                  _full((len(WINDOWS), GROUP, GROUP)), _full((1, D_B)), _full((1, D_B)),
                  _resident(fc_shards[0].shape), _resident(fc_shards[1].shape), _full((1, D))],
        out_specs=(first(D_Z), first(D), second(D), second(D), second(D), r_head, second(D),
                   _full((8, D)), _full((N_DEV, D)), hbm, hbm, hbm, hbm),
        scratch_shapes=[pltpu.VMEM((HALO, D_B), F32),
                        pltpu.VMEM((2,) + fc_shards[0].shape, BF16), pltpu.VMEM((2,) + fc_shards[1].shape, BF16),
                        pltpu.VMEM(fc_shards[0].shape, BF16), pltpu.VMEM(fc_shards[1].shape, BF16),
                        pltpu.VMEM((ATTN_LAG + 1, tt, D), BF16), pltpu.VMEM((D, D), BF16),
                        pltpu.VMEM((D_Z, D), BF16), pltpu.VMEM((8, D), F32),
                        pltpu.VMEM((N_DEV, 8, D), F32), pltpu.VMEM((N_DEV, N_DEV, ncol), F32),
                        pltpu.VMEM((N_DEV, ncol), F32),
                        pltpu.SemaphoreType.DMA((10,)), pltpu.SemaphoreType.DMA((10,)),
                        pltpu.SemaphoreType.DMA((10,)),
                        pltpu.SemaphoreType.DMA((7,)), pltpu.SemaphoreType.DMA((7,)),
                        pltpu.SemaphoreType.DMA((7,)), pltpu.SemaphoreType.DMA((7,)),
                        pltpu.SemaphoreType.DMA((2 * (N_DEV - 1),)), pltpu.SemaphoreType.DMA((2 * (N_DEV - 1),))],
        compiler_params=pltpu.CompilerParams(dimension_semantics=("arbitrary",), vmem_limit_bytes=VMEM_LIMIT),
    )(x, x, c_row, w_ada, b_pieces, n1pre, n1post, w_in_shard, w_out_shard, w_sp, bs_rows, ln_g, ln_b, w_pool, b_pool,
      pool_scale, *fc_shards, n2pre)
    return outs[:9], outs[9:11], outs[11], outs[12]


def _mlp_fwd_early(tt, r_begun, h2, f_head, w1_early, w2_early):
    t_len = h2.shape[0]
    nt = t_len // tt
    n_late = N_DEV - FC_EARLY

    def body(r_begun_ref, h2_ref, fh_ref, w1_ref, w2_ref, r_ref, f_ref, l1_ref, l2_ref,
             land1, land2, send_sems, recv_sems, local_sems):
        i = pl.program_id(0)
        copies = _Copies(
            [(w1_ref.at[2], land1, 4), (w2_ref.at[4], land2, 2),
             (land1, l1_ref.at[1], 1), (land2, l2_ref.at[1], 1)],
            send_sems, recv_sems)
        keep = [pltpu.make_async_copy(land1, l1_ref.at[0], local_sems.at[0]),
                pltpu.make_async_copy(land2, l2_ref.at[0], local_sems.at[1])]

        @pl.when(i == 0)
        def _():
            copies.start(0, 1)

        @pl.when(i == nt - 1)
        def _():
            copies.wait_recv(0, 1)
            copies.start(2, 3)
            for cp in keep:
                cp.start()

        h2 = h2_ref[...]
        f_ref[...] = fh_ref[...]
        for j in range(FC_HEAD, FC_EARLY):
            ra = jnp.maximum(_dot(h2, w1_ref[j]), 0.0)
            r = (ra * ra).astype(BF16)
            r_ref[:, _early_col(j):_early_col(j) + FF_BLK] = r
            f_ref[...] += _dot(r, w2_ref[j])

        @pl.when(i == nt - 1)
        def _():
            copies.wait_recv(2, 3)
            copies.wait_send(0, 1, 2, 3)
            for cp in keep:
                cp.wait()

    tile = lambda w: pl.BlockSpec((tt, w), lambda i: (i, 0))
    hbm = pl.BlockSpec(memory_space=pl.ANY)
    outs = pl.pallas_call(
        body, name="mlp_fwd_early", grid=(nt,),
        out_shape=(jax.ShapeDtypeStruct((t_len, FC_EARLY * FF_BLK), BF16), jax.ShapeDtypeStruct((t_len, D), F32),
                   jax.ShapeDtypeStruct((n_late,) + w1_early.shape[1:], BF16),
                   jax.ShapeDtypeStruct((n_late,) + w2_early.shape[1:], BF16)),
        in_specs=[hbm, tile(D), tile(D), _resident((FC_EARLY, D, FF_BLK)), _resident((FC_EARLY, FF_BLK, D))],
        out_specs=(tile(R_HEAD_COLS), tile(D), hbm, hbm),
        input_output_aliases={0: 0},
        scratch_shapes=[pltpu.VMEM(w1_early.shape[1:], BF16), pltpu.VMEM(w2_early.shape[1:], BF16),
                        pltpu.SemaphoreType.DMA((4,)), pltpu.SemaphoreType.DMA((4,)),
                        pltpu.SemaphoreType.DMA((2,))],
        compiler_params=pltpu.CompilerParams(dimension_semantics=("arbitrary",), vmem_limit_bytes=VMEM_LIMIT),
    )(r_begun, h2, f_head, w1_early, w2_early)
    return outs[:2], outs[2:]


def _mlp_late_bwd(tt, r_early, x1, h2, f_early, tgt, mix, mod, n2pre, n2post, n1post,
                  w1_early, w2_early, w1_late, w2_late):
    t_len = x1.shape[0]
    nt = t_len // tt
    n_late = N_DEV - FC_EARLY
    late_cols = n_late * FF_BLK

    def body(re_ref, x1_ref, h2_ref, fe_ref, tgt_ref, mix_ref, mod_ref, n2pre_ref, n2post_ref,
             n1post_ref, w1e_ref, w2e_ref, w1l_ref, w2l_ref,
             rl_ref, df_ref, da_ref, dmix_ref, dx1_ref, redf_ref, redb_ref, dh2_acc):
        i = pl.program_id(0)

        @pl.when(i == 0)
        def _():
            redf_ref[...] = jnp.zeros_like(redf_ref)
            redb_ref[...] = jnp.zeros_like(redb_ref)

        x1v = x1_ref[...]
        gate1, scale2, gate2 = mod_ref[2:3, :], mod_ref[4:5, :], mod_ref[5:6, :]
        h2 = h2_ref[...]
        f = fe_ref[...]
        for j in range(n_late):
            cols = slice(j * FF_BLK, (j + 1) * FF_BLK)
            ra = jnp.maximum(_dot(h2, w1l_ref[j]), 0.0)
            r = (ra * ra).astype(BF16)
            rl_ref[:, cols] = r
            f = f + _dot(r, w2l_ref[j])
        post2 = n2post_ref[...]
        gate_post2 = gate2 * post2
        rf = _rstd(f)
        fhat = f * rf
        err = (x1v + fhat * gate_post2) - tgt_ref[...]
        dy = err * (1.0 / D)
        d_f, sum_f = _rms_bwd_gained(dy, gate_post2, fhat, rf)
        dfv = d_f.astype(BF16)
        df_ref[...] = dfv
        redf_ref[0:1, :] += post2 * sum_f
        redf_ref[1:2, :] += gate2 * sum_f
        redf_ref[2:3, :] += _colsum(err * err)

        for j in range(N_DEV):
            cols = slice(j * FF_BLK, (j + 1) * FF_BLK)
            if j < FC_EARLY:
                w1, w2, r = w1e_ref[j], w2e_ref[j], re_ref[:, _early_col(j):_early_col(j) + FF_BLK]
            else:
                jl = j - FC_EARLY
                w1, w2, r = w1l_ref[jl], w2l_ref[jl], rl_ref[:, jl * FF_BLK:(jl + 1) * FF_BLK]
            dr = _dot_nt(dfv, w2)
            da = (dr * (2.0 * jnp.sqrt(r.astype(F32)))).astype(BF16)
            da_ref[:, cols] = da
            contrib = _dot_nt(da, w1)
            if j == 0:
                dh2_acc[...] = contrib
            else:
                dh2_acc[...] += contrib
        dh2 = dh2_acc[...]
        pre2, post1 = n2pre_ref[...], n1post_ref[...]
        r2 = _rstd(x1v)
        xhat = x1v * r2
        d_x1, sum_h = _rms_bwd_gained(dh2, pre2 * (1.0 + scale2), xhat, r2)
        dx1 = dy + d_x1
        dx1_ref[...] = dx1
        mixv = mix_ref[...]
        rm = _rstd(mixv)
        mhat = mixv * rm
        d_mix, sum_m = _rms_bwd_gained(dx1, gate1 * post1, mhat, rm)
        dmix_ref[...] = d_mix.astype(BF16)
        redb_ref[0:1, :] += _colsum(dh2)
        redb_ref[1:2, :] += pre2 * sum_h
        redb_ref[2:3, :] += (1.0 + scale2) * sum_h
        redb_ref[3:4, :] += post1 * sum_m
        redb_ref[4:5, :] += gate1 * sum_m

    tile = lambda w: pl.BlockSpec((tt, w), lambda i: (i, 0))
    return pl.pallas_call(
        body, name="mlp_late_bwd", grid=(nt,),
        out_shape=(jax.ShapeDtypeStruct((t_len, late_cols), BF16), jax.ShapeDtypeStruct((t_len, D), BF16),
                   jax.ShapeDtypeStruct((t_len, D_FF), BF16), jax.ShapeDtypeStruct((t_len, D), BF16),
                   jax.ShapeDtypeStruct((t_len, D), F32), jax.ShapeDtypeStruct((8, D), F32),
                   jax.ShapeDtypeStruct((8, D), F32)),
        in_specs=[tile(FC_EARLY * FF_BLK), tile(D), tile(D), tile(D), tile(D),
                  tile(D), _full((8, D)), _full((1, D)), _full((1, D)), _full((1, D)),
                  _resident((FC_EARLY, D, FF_BLK)), _resident((FC_EARLY, FF_BLK, D)),
                  _resident((n_late, D, FF_BLK)), _resident((n_late, FF_BLK, D))],
        out_specs=(tile(late_cols), tile(D), tile(D_FF), tile(D), tile(D), _full((8, D)), _full((8, D))),
        scratch_shapes=[pltpu.VMEM((tt, D), F32)],
        compiler_params=pltpu.CompilerParams(dimension_semantics=("arbitrary",), vmem_limit_bytes=VMEM_LIMIT),
    )(r_early, x1, h2, f_early, tgt, mix, mod, n2pre, n2post, n1post, w1_early, w2_early, w1_late, w2_late)


def _mlp_wgrad(tt, r_early, r_late, da, df, h2):
    t_len = df.shape[0]
    nt = t_len // tt
    odd_steps = [j for j, rel in enumerate(WGRAD_ORDER) if rel % 2]

    def relation(j):
        rel = jnp.int32(WGRAD_ORDER[-1])
        for step in range(N_DEV - 2, -1, -1):
            rel = jnp.where(j == step, WGRAD_ORDER[step], rel)
        return rel

    def body(re_ref, rl_ref, da_ref, df_ref, h2_ref, own1_ref, own2_ref, out1_ref, out2_ref, diag1_ref, diag2_ref,
             acc1, acc2, snd1, snd2, sib1, sib2, dsnd1, dsnd2, send_sems, recv_sems):
        j, t = pl.program_id(0), pl.program_id(1)
        rows = pl.ds(pl.multiple_of(t * tt, tt), tt)
        x, y, c = _place()
        accs, snds, sibs = (acc1, acc2), (snd1, snd2), (sib1, sib2)
        dsnds, diags = (dsnd1, dsnd2), (diag1_ref, diag2_ref)

        def to_sibling(a, jj, buf=0):
            return pltpu.make_async_remote_copy(
                src_ref=snds[a].at[buf], dst_ref=sibs[a].at[jj],
                send_sem=send_sems.at[4 * a + jj], recv_sem=recv_sems.at[4 * a + jj],
                device_id=(x, y, 1 - c), device_id_type=MESH)

        def to_diagonal(a):
            return pltpu.make_async_remote_copy(
                src_ref=dsnds[a], dst_ref=diags[a], send_sem=send_sems.at[8 + a], recv_sem=recv_sems.at[8 + a],
                device_id=_peer(x, y, c, 6), device_id_type=MESH)

        @pl.when(t == 0)
        def _():
            acc2[...] = jnp.zeros_like(acc2)
            acc1[...] = jnp.zeros_like(acc1)

        for r_ref, mine in ((re_ref, relation(j) < FC_EARLY), (rl_ref, relation(j) >= FC_EARLY)):
            @pl.when(mine)
            def _():
                acc2[...] += _dot_tn(r_ref[...], df_ref[rows, :])
                acc1[...] += _dot_tn(h2_ref[rows, :], da_ref[...])

        for step, rel in enumerate(WGRAD_ORDER):
            jj = rel // 2

            @pl.when((t == nt - 1) & (j == step))
            def _():
                for a, (own_ref, out_ref) in enumerate(((own1_ref, out1_ref), (own2_ref, out2_ref))):
                    if rel % 2:
                        q = odd_steps.index(step)
                        if q >= 2:
                            to_sibling(a, WGRAD_ORDER[odd_steps[q - 2]] // 2).wait_send()
                        snds[a][q % 2] = accs[a][...].astype(BF16)
                        to_sibling(a, jj, q % 2).start()
                        continue
                    to_sibling(a, jj).wait_recv()
                    chip_sum = accs[a][...] + sibs[a][jj].astype(F32)
                    if rel == 6:
                        dsnds[a][...] = chip_sum.astype(BF16)
                        to_diagonal(a).start()
                    elif rel == 0:
                        own_ref[...] = chip_sum
                    else:
                        out_ref[0] = chip_sum.astype(BF16)
                    if step == N_DEV - 1:
                        for q in (2, 3):
                            to_sibling(a, WGRAD_ORDER[odd_steps[q]] // 2).wait_send()
                        to_diagonal(a).wait_recv()
                        to_diagonal(a).wait_send()

    assert WGRAD_ORDER[-1] == 0 and WGRAD_ORDER[-3:-1] == (2, 4)
    blk = pl.BlockSpec((tt, FF_BLK), lambda j, t: (t, relation(j)))
    early_block = lambda rel: jnp.where(rel < FC_HEAD, rel + FC_EARLY - FC_HEAD, rel - FC_HEAD)
    early = lambda j, t: (jnp.where(relation(j) < FC_EARLY, t, 0),
                          jnp.where(relation(j) < FC_EARLY, early_block(relation(j)), 0))
    late = lambda j, t: (jnp.where(relation(j) < FC_EARLY, 0, t), jnp.maximum(relation(j) - FC_EARLY, 0))
    chip = lambda j, t: (jnp.clip(j - 5, 0, 1), 0, 0)
    hbm = pl.BlockSpec(memory_space=pl.ANY)
    return pl.pallas_call(
        body, name="mlp_wgrad", grid=(N_DEV, nt),
        out_shape=(jax.ShapeDtypeStruct((D, FF_BLK), F32), jax.ShapeDtypeStruct((FF_BLK, D), F32),
                   jax.ShapeDtypeStruct((2, D, FF_BLK), BF16), jax.ShapeDtypeStruct((2, FF_BLK, D), BF16),
                   jax.ShapeDtypeStruct((D, FF_BLK), BF16), jax.ShapeDtypeStruct((FF_BLK, D), BF16)),
        in_specs=[pl.BlockSpec((tt, FF_BLK), early), pl.BlockSpec((tt, FF_BLK), late), blk,
                  _resident((t_len, D)), _resident((t_len, D))],
        out_specs=(_full((D, FF_BLK)), _full((FF_BLK, D)),
                   pl.BlockSpec((1, D, FF_BLK), chip), pl.BlockSpec((1, FF_BLK, D), chip), hbm, hbm),
        scratch_shapes=[pltpu.VMEM((D, FF_BLK), F32), pltpu.VMEM((FF_BLK, D), F32),
                        pltpu.VMEM((2, D, FF_BLK), BF16), pltpu.VMEM((2, FF_BLK, D), BF16),
                        pltpu.VMEM((4, D, FF_BLK), BF16), pltpu.VMEM((4, FF_BLK, D), BF16),
                        pltpu.VMEM((D, FF_BLK), BF16), pltpu.VMEM((FF_BLK, D), BF16),
                        pltpu.SemaphoreType.DMA((10,)), pltpu.SemaphoreType.DMA((10,))],
        compiler_params=pltpu.CompilerParams(dimension_semantics=("arbitrary", "arbitrary"),
                                             vmem_limit_bytes=VMEM_LIMIT),
    )(r_early, r_late, da, df, h2)


def _acc_rows(ref, row0, k, val):
    half = CHUNK // 2
    ref[row0:row0 + half, k * GROUP:(k + 1) * GROUP] += val[:half, :]
    ref[row0:row0 + half, D_A + k * GROUP:D_A + (k + 1) * GROUP] += val[half:, :]


def _attn_bwd(tt, dmix, x, z, cat, mod, n1pre, w_out, w_sp, bs_rows, ln_g, ln_b, w_pool, b_pool, pool_scale,
              red_fwd, red_bwd, chip_sums):
    t_len = z.shape[0]
    nt = t_len // tt
    hb = tt // HALO
    n_sums = len(chip_sums)

    def body(dmix_ref, x_ref, z_ref, zprev_ref, cat_ref, mod_ref, n1pre_ref, wout_ref, wsp_ref,
             bs_ref, lng_ref, lnb_ref, wp_ref, bp_ref, ps_ref, redf_ref, redb_ref, *rest):
        sum_out = rest[:n_sums]
        dz_ref, gwin_ref, gwout_ref, small_ref = rest[n_sums:n_sums + 4]
        sum_in = rest[n_sums + 4:2 * n_sums + 4]
        carry, acc_in, acc_out, dz_scr, bs_acc, send_sems, recv_sems = rest[2 * n_sums + 4:]
        s = pl.program_id(0)
        i = nt - 1 - s
        px, py, pc = _place()

        def chip_copy(a, r):
            return pltpu.make_async_remote_copy(
                src_ref=sum_out[a].at[r], dst_ref=sum_in[a].at[r],
                send_sem=send_sems.at[2 * a + r], recv_sem=recv_sems.at[2 * a + r],
                device_id=_peer(px, py, pc, 2 * (r + 1)), device_id_type=MESH)

        @pl.when(s == 0)
        def _():
            for a in range(n_sums):
                for r in range(2):
                    chip_copy(a, r).start()
            carry[...] = jnp.zeros_like(carry)
            acc_in[...] = jnp.zeros_like(acc_in)
            acc_out[...] = jnp.zeros_like(acc_out)
            bs_acc[...] = jnp.zeros_like(bs_acc)
            small_ref[...] = jnp.zeros_like(small_ref)
            small_ref[ROW_DMOD + 2:ROW_DMOD + 3, :] = redb_ref[3:4, :]
            small_ref[ROW_DMOD + 3:ROW_DMOD + 5, :] = redb_ref[0:2, :]
            small_ref[ROW_DMOD + 5:ROW_DMOD + 6, :] = redf_ref[0:1, :]
            small_ref[ROW_N1POST:ROW_N1POST + 1, :] = redb_ref[4:5, :]
            small_ref[ROW_N2PRE:ROW_N2PRE + 1, :] = redb_ref[2:3, :]
            small_ref[ROW_N2POST:ROW_N2POST + 1, :] = redf_ref[1:2, :]
            small_ref[ROW_LOSS:ROW_LOSS + 1, :] = redf_ref[2:3, :]

        dmixv = dmix_ref[...]
        dcat = _dot_nt(dmixv, wout_ref[...])
        acc_out[...] += _dot_tn(cat_ref[...], dmixv)

        z = z_ref[...]
        t_g, ga = _gelu_parts(z[:, :2 * D_A])
        u, vr = ga[:, :D_A], ga[:, D_A:]
        dv0 = vr - jnp.mean(vr, axis=-1, keepdims=True)
        rv = lax.rsqrt(jnp.mean(dv0 * dv0, axis=-1, keepdims=True) + EPS)
        vhat = dv0 * rv
        vb = (vhat * lng_ref[...] + lnb_ref[...]).astype(BF16)
        mask = _tril_mask()
        wc = [(wsp_ref[h] * mask).astype(BF16) for h in range(N_HEADS)]

        dya = dcat[:, :D_A]
        for h in range(N_HEADS):
            cols = slice(h * GROUP, (h + 1) * GROUP)
            bs_sum = jnp.zeros((CHUNK, GROUP), F32)
            ws_sum = jnp.zeros((CHUNK, CHUNK), F32)
            for ch in range(tt // CHUNK):
                rows = slice(ch * CHUNK, (ch + 1) * CHUNK)
                v_ch = vb[rows, cols]
                mixed = _dot(wc[h], v_ch) + bs_ref[:, cols]
                dy_ch = dya[rows, cols]
                dz_scr[rows, cols] = dy_ch * mixed
                dmixed = dy_ch * u[rows, cols]
                dmb = dmixed.astype(BF16)
                dz_scr[rows, D_A + h * GROUP:D_A + (h + 1) * GROUP] = _dot_tn(wc[h], dmb)
                bs_sum = bs_sum + dmixed
                ws_sum = ws_sum + _dot_nt(dmb, v_ch)
            _acc_rows(bs_acc, 0, h, bs_sum)
            _acc_rows(small_ref, ROW_WS, h, ws_sum)

        dvl = dz_scr[:, D_A:2 * D_A]
        dvhat = dvl * lng_ref[...]
        dvl_vhat = dvl * vhat
        dvr = rv * (dvhat - jnp.mean(dvhat, axis=-1, keepdims=True)
                    - vhat * jnp.mean(dvl_vhat * lng_ref[...], axis=-1, keepdims=True))
        small_ref[ROW_LN:ROW_LN + 1, 0:D_A] += _colsum(dvl_vhat)
        small_ref[ROW_LN:ROW_LN + 1, D_A:D] += _colsum(dvl)
        dga = jnp.concatenate([dz_scr[:, :D_A], dvr], axis=1)
        dza = dga * _gelu_grad(z[:, :2 * D_A], t_g)

        zb = z[:, 2 * D_A:]
        halo_prev = jnp.where(i == 0, 0.0, zprev_ref[...])
        diff = _pool_diff(zb, halo_prev, i * tt)
        dyb = dcat[:, D_A:]
        inv = _inv_counts(i * tt, tt)
        scaled, ddiffs = [], []
        for g in range(len(WINDOWS)):
            cols = slice(g * GROUP, (g + 1) * GROUP)
            db = diff[g].astype(BF16)
            wpg = wp_ref[g].astype(BF16)
            pre = _dot(db, wpg) + bp_ref[:, cols]
            small_ref[ROW_POOL:ROW_POOL + 1, cols] += _colsum(dyb[:, cols] * pre)
            dpre = dyb[:, cols] * ps_ref[:, cols]
            small_ref[ROW_POOL:ROW_POOL + 1, D_B + g * GROUP:D_B + (g + 1) * GROUP] += _colsum(dpre)
            dpb = dpre.astype(BF16)
            _acc_rows(small_ref, ROW_WP, g, _dot_tn(db, dpb))
            ddiff = _dot_nt(dpb, wpg)
            ddiffs.append(ddiff)
            scaled.append(ddiff * inv[g])
        scaled_all = jnp.concatenate(scaled, axis=1)
        ext = jnp.concatenate([scaled_all, carry[...]], axis=0)
        n_ext = tt + HALO
        s2 = ext + pltpu.roll(ext, n_ext - 1, 0)
        t4 = s2[:, GROUP:]
        s4 = t4 + pltpu.roll(t4, n_ext - 2, 0)
        t8 = s4[:, GROUP:]
        s8 = t8 + pltpu.roll(t8, n_ext - 4, 0)
        t16 = s8[:, GROUP:]
        s16 = t16 + pltpu.roll(t16, n_ext - 8, 0)
        back = [s2[:, :GROUP], s4[:, :GROUP], s8[:, :GROUP], s16]
        carry[...] = scaled_all[:HALO, :]
        dzb = jnp.concatenate([back[g][:tt, :] - ddiffs[g] for g in range(len(WINDOWS))], axis=1)

        dzv = jnp.concatenate([dza, dzb], axis=1).astype(BF16)
        dz_ref[...] = dzv
        xv = x_ref[...]
        h1 = (xv * _rstd(xv) * (n1pre_ref[...] * (1.0 + mod_ref[1:2, :])) + mod_ref[0:1, :]).astype(BF16)
        acc_in[...] += _dot_tn(dzv, h1)

        @pl.when(s == nt - 1)
        def _():
            gwin_ref[...] = acc_in[...].astype(BF16)
            gwout_ref[...] = acc_out[...].astype(BF16)
            bs = _unfold(bs_acc[...])
            for h in range(N_HEADS):
                small_ref[ROW_BS + h:ROW_BS + h + 1, 0:GROUP] = jnp.sum(
                    bs[:, h * GROUP:(h + 1) * GROUP].T, axis=0, keepdims=True)
            for a in range(n_sums):
                for r in range(2):
                    chip_copy(a, r).wait_recv()
                    chip_copy(a, r).wait_send()

    rev = lambda w: pl.BlockSpec((tt, w), lambda s: (nt - 1 - s, 0))
    zprev = pl.BlockSpec((HALO, D_B), lambda s: (jnp.maximum((nt - 1 - s) * hb - 1, 0), 2))
    hbm = pl.BlockSpec(memory_space=pl.ANY)
    outs = pl.pallas_call(
        body, name="attn_bwd", grid=(nt,),
        out_shape=tuple([jax.ShapeDtypeStruct((t_len, D_Z), BF16), jax.ShapeDtypeStruct((D_Z, D), BF16),
                         jax.ShapeDtypeStruct((D, D), BF16), jax.ShapeDtypeStruct((SMALL_ROWS, D), F32)]
                        + [jax.ShapeDtypeStruct(cs.shape, cs.dtype) for cs in chip_sums]),
        in_specs=[rev(D), rev(D), rev(D_Z), zprev, rev(D), _full((8, D)), _full((1, D)),
                  _resident((D, D)), _full((N_HEADS, CHUNK, CHUNK)), _full((CHUNK, D_A)),
                  _full((1, D_A)), _full((1, D_A)), _full((len(WINDOWS), GROUP, GROUP)), _full((1, D_B)),
                  _full((1, D_B)), _full((8, D)), _full((8, D))] + [_resident(cs.shape) for cs in chip_sums],
        out_specs=tuple([rev(D_Z), _resident((D_Z, D)), _resident((D, D)), _full((SMALL_ROWS, D))]
                        + [hbm] * n_sums),
        scratch_shapes=[pltpu.VMEM((HALO, D_B), F32), pltpu.VMEM((D_Z, D), F32), pltpu.VMEM((D, D), F32),
                        pltpu.VMEM((tt, 2 * D_A), F32), pltpu.VMEM((CHUNK // 2, D), F32),
                        pltpu.SemaphoreType.DMA((2 * n_sums,)), pltpu.SemaphoreType.DMA((2 * n_sums,))],
        compiler_params=pltpu.CompilerParams(dimension_semantics=("arbitrary",), vmem_limit_bytes=VMEM_LIMIT),
    )(dmix, x, z, z, cat, mod, n1pre, w_out, w_sp, bs_rows, ln_g, ln_b, w_pool, b_pool, pool_scale,
      red_fwd, red_bwd, *chip_sums)
    return outs[:4], outs[4:]


def _in_proj_bwd(tt, dz, dx1, x, mod, n1pre, w_in_t):
    t_len = x.shape[0]
    nt = t_len // tt

    def body(dz_ref, dx1_ref, x_ref, mod_ref, n1pre_ref, win_ref, gx_ref, sums_ref):
        i = pl.program_id(0)

        @pl.when(i == 0)
        def _():
            sums_ref[...] = jnp.zeros_like(sums_ref)

        dh1 = _dot(dz_ref[...], win_ref[...])
        xv = x_ref[...]
        r1 = _rstd(xv)
        xhat = xv * r1
        scale1 = mod_ref[1:2, :]
        pre1 = n1pre_ref[...]
        gain1 = pre1 * (1.0 + scale1)
        d_x, sum_h = _rms_bwd_gained(dh1, gain1, xhat, r1)
        gx_ref[...] = dx1_ref[...] + d_x
        sums_ref[ROW_DMOD:ROW_DMOD + 1, :] += _colsum(dh1)
        sums_ref[ROW_DMOD + 1:ROW_DMOD + 2, :] += pre1 * sum_h
        sums_ref[ROW_N1PRE:ROW_N1PRE + 1, :] += (1.0 + scale1) * sum_h

    tile = lambda w: pl.BlockSpec((tt, w), lambda i: (i, 0))
    return pl.pallas_call(
        body, name="in_proj_bwd", grid=(nt,),
        out_shape=(jax.ShapeDtypeStruct((t_len, D), F32), jax.ShapeDtypeStruct((16, D), F32)),
        in_specs=[tile(D_Z), tile(D), tile(D), _full((8, D)), _full((1, D)), _resident((D_Z, D))],
        out_specs=(tile(D), _full((16, D))),
        compiler_params=pltpu.CompilerParams(dimension_semantics=("arbitrary",), vmem_limit_bytes=VMEM_LIMIT),
    )(dz, dx1, x, mod, n1pre, w_in_t)


def _adam(w, g, m, v):
    m2 = ADAM_B1 * m + (1.0 - ADAM_B1) * g
    v2 = ADAM_B2 * v + (1.0 - ADAM_B2) * (g * g)
    m_hat = m2 / (1.0 - ADAM_B1 ** ADAM_STEP)
    v_hat = v2 / (1.0 - ADAM_B2 ** ADAM_STEP)
    delta = -ADAM_LR * (m_hat / (jnp.sqrt(v_hat) + ADAM_EPS) + ADAM_WD * w)
    return delta, m2, v2


def _adamw_fc(steps, fc):
    n_fc = len(fc)

    def body(*refs):
        ins, outs = refs[:6 * n_fc], refs[6 * n_fc:]
        for k in range(n_fc):
            w_ref, own_ref, arr_ref, diag_ref, m_ref, v_ref = ins[6 * k:6 * k + 6]
            g = ((own_ref[...] + arr_ref[0].astype(F32)) + arr_ref[1].astype(F32)) + diag_ref[...].astype(F32)
            outs[4 * k][...] = g
            outs[4 * k + 1][...], outs[4 * k + 2][...], outs[4 * k + 3][...] = _adam(
                w_ref[...], g, m_ref[...], v_ref[...])

    specs_in, specs_out, shapes, args = [], [], [], []
    for w, own, arrived, diagonal, m, v in fc:
        rows, cols = w.shape
        blk = pl.BlockSpec((rows // steps, cols), lambda i: (i, 0))
        specs_in += [blk, blk, pl.BlockSpec((2, rows // steps, cols), lambda i: (0, i, 0)), blk, blk, blk]
        specs_out += [blk] * 4
        shapes += [jax.ShapeDtypeStruct((rows, cols), F32)] * 4
        args += [w, own, arrived, diagonal, m, v]
    outs = pl.pallas_call(
        body, name="adamw_fc", grid=(steps,), out_shape=tuple(shapes), in_specs=specs_in, out_specs=tuple(specs_out),
        compiler_params=pltpu.CompilerParams(dimension_semantics=("arbitrary",), vmem_limit_bytes=VMEM_LIMIT),
    )(*args)
    return [outs[4 * k:4 * k + 4] for k in range(n_fc)]


def _adamw_ada(rb, w, sc, total, m, v):
    rows, cols = w.shape

    def body(w_ref, sc_ref, t_ref, m_ref, v_ref, g_ref, d_ref, m2_ref, v2_ref, dm):
        me = _index(_place())
        for dev in range(N_DEV):
            @pl.when((pl.program_id(0) == 0) & (me == dev))
            def _():
                for b in range(N_DEV):
                    for k in range(6):
                        lo, hi = max(cols * dev, D * k), min(cols * (dev + 1), D * (k + 1))
                        if lo < hi:
                            dm[b:b + 1, lo - cols * dev:hi - cols * dev] = t_ref[
                                _table_row(b) + k:_table_row(b) + k + 1, lo - D * k:hi - D * k]

        g = _dot_tn(sc_ref[...].astype(BF16), dm[...].astype(BF16))
        g_ref[...] = g
        d_ref[...], m2_ref[...], v2_ref[...] = _adam(w_ref[...], g, m_ref[...], v_ref[...])

    blk = pl.BlockSpec((rb, cols), lambda i: (i, 0))
    shp = jax.ShapeDtypeStruct((rows, cols), F32)
    return pl.pallas_call(
        body, name="adamw_ada", grid=(rows // rb,), out_shape=(shp, shp, shp, shp),
        in_specs=[blk, pl.BlockSpec((N_DEV, rb), lambda i: (0, i)), _full(total.shape), blk, blk],
        out_specs=(blk, blk, blk, blk),
        scratch_shapes=[pltpu.VMEM((N_DEV, cols), F32)],
        compiler_params=pltpu.CompilerParams(dimension_semantics=("arbitrary",)),
    )(w, sc, total, m, v)


def _unfold(acc_rows):
    return jnp.concatenate([acc_rows[:, :D_A], acc_rows[:, D_A:]], axis=0)


def _adamw_small(total, params, shards):
    n = len(params)
    flat = [a for p in params for a in p]

    def body(*refs):
        s_ref = refs[0]
        p_refs = refs[1:1 + 3 * n]
        s_refs = refs[1 + 3 * n:1 + 3 * n + 4 * len(shards)]
        loss_ref = refs[1 + 3 * n + 4 * len(shards)]
        o_refs = refs[2 + 3 * n + 4 * len(shards):2 + 3 * n + 4 * len(shards) + 4 * n]
        so_refs = refs[2 + 3 * n + 4 * len(shards) + 4 * n:]
        d_b_ada = s_ref[0:6, :]
        for b in range(1, N_DEV):
            d_b_ada = d_b_ada + s_ref[_table_row(b):_table_row(b) + 6, :]
        misc = lambda r: s_ref[PK_MISC + r - ROW_N1PRE:PK_MISC + r - ROW_N1PRE + 1, :]
        loss = jnp.sum(misc(ROW_LOSS), axis=-1, keepdims=True) * (0.5 / D)
        loss_ref[...] = loss
        mask = _tril_mask()
        ws = _unfold(s_ref[PK_WS:PK_WS + 64, :])
        wp = _unfold(s_ref[PK_WP:PK_WP + 64, :])
        grads = [
            d_b_ada,
            misc(ROW_N1PRE), misc(ROW_N1POST), misc(ROW_N2PRE), misc(ROW_N2POST),
            misc(ROW_LN)[:, :D_A], misc(ROW_LN)[:, D_A:],
            misc(ROW_POOL)[:, :D_B], misc(ROW_POOL)[:, D_B:],
            s_ref[PK_BS:PK_BS + N_HEADS, 0:GROUP],
            jnp.stack([ws[:, h * GROUP:(h + 1) * GROUP] * mask for h in range(N_HEADS)]),
            jnp.stack([wp[:, g * GROUP:(g + 1) * GROUP] for g in range(len(WINDOWS))]),
        ]
        for k in range(n):
            w_ref, m_ref, v_ref = p_refs[3 * k:3 * k + 3]
            g = grads[k]
            if k == 0:
                for j in range(6):
                    o_refs[0][j] = g[j:j + 1, :]
                    o_refs[1][j], o_refs[2][j], o_refs[3][j] = _adam(w_ref[j], g[j:j + 1, :], m_ref[j], v_ref[j])
                continue
            o_refs[4 * k][...] = g
            o_refs[4 * k + 1][...], o_refs[4 * k + 2][...], o_refs[4 * k + 3][...] = _adam(
                w_ref[...], g, m_ref[...], v_ref[...])
        for k in range(len(shards)):
            w_ref, g_ref, m_ref, v_ref = s_refs[4 * k:4 * k + 4]
            so_refs[3 * k][...], so_refs[3 * k + 1][...], so_refs[3 * k + 2][...] = _adam(
                w_ref[...], g_ref[...], m_ref[...], v_ref[...])

    vm = pl.BlockSpec(memory_space=pltpu.VMEM)
    out_shape = [jax.ShapeDtypeStruct((1, 1), F32)]
    for w, _, _ in params:
        out_shape += [jax.ShapeDtypeStruct(w.shape, F32)] * 4
    for w, _, _, _ in shards:
        out_shape += [jax.ShapeDtypeStruct(w.shape, F32)] * 3
    return pl.pallas_call(
        body, name="adamw_small", out_shape=tuple(out_shape),
        in_specs=[vm] * (1 + 3 * n + 4 * len(shards)), out_specs=tuple([vm] * len(out_shape)),
        compiler_params=pltpu.CompilerParams(vmem_limit_bytes=VMEM_LIMIT),
    )(total, *flat, *[a for s in shards for a in s])


TT_ATTN_FWD = 512
ATTN_LAG = 2
TT_MLP_FWD = 512
TT_MLP = 256
TT_WGRAD = 2048
TT_ATTN_BWD = 512
TT_IN_PROJ_BWD = 512


def kernel(x, c, w_ada, b_ada, norm1_pre, norm1_post, w_in, w_spatial, b_spatial, ln_v_gain, ln_v_bias, w_pool, b_pool, pool_scale, w_out, norm2_pre, norm2_post, w_fc1, w_fc2, loss_target, m_w_ada, m_b_ada, m_norm1_pre, m_norm1_post, m_w_in, m_w_spatial, m_b_spatial, m_ln_v_gain, m_ln_v_bias, m_w_pool, m_b_pool, m_pool_scale, m_w_out, m_norm2_pre, m_norm2_post, m_w_fc1, m_w_fc2, v_w_ada, v_b_ada, v_norm1_pre, v_norm1_post, v_w_in, v_w_spatial, v_b_spatial, v_ln_v_gain, v_ln_v_bias, v_w_pool, v_b_pool, v_pool_scale, v_w_out, v_norm2_pre, v_norm2_post, v_w_fc1, v_w_fc2):
    t_len = x.shape[1]
    ada_cols = w_ada.shape[1]
    tt = lambda want: min(want, t_len)

    x2 = x.reshape(t_len, D)
    tgt = loss_target.reshape(t_len, D)
    row = lambda a: a.reshape(1, -1)

    w_in_shard, w_out_shard, w1_shard, w2_shard = _cast_shards([w_in.T, w_out, w_fc1, w_fc2])

    bs_rows = jnp.repeat(b_spatial.T, GROUP, axis=1)
    attn_consts = (w_spatial, bs_rows, row(ln_v_gain), row(ln_v_bias), w_pool, row(b_pool), row(pool_scale))

    (z, cat, mix, x1, h2, r_begun, f_head, mod, sc), (w1_early, w2_early), w_out_all, w_in_t = _attn_fwd(
        tt(TT_ATTN_FWD), x2, c.reshape(1, D), w_ada, b_ada.reshape(N_DEV, 1, ada_cols), row(norm1_pre),
        row(norm1_post),
        w_in_shard, w_out_shard, *attn_consts, (w1_shard, w2_shard), row(norm2_pre))
    (r_early, f_early), (w1_late, w2_late) = _mlp_fwd_early(
        tt(TT_MLP_FWD), r_begun, h2, f_head, w1_early, w2_early)
    r_late, df, da, dmix, dx1, red_fwd, red_bwd = _mlp_late_bwd(
        tt(TT_MLP), r_early, x1, h2, f_early, tgt, mix, mod, row(norm2_pre), row(norm2_post), row(norm1_post),
        w1_early, w2_early, w1_late, w2_late)
    own_w1, own_w2, sums_w1, sums_w2, diag_w1, diag_w2 = _mlp_wgrad(tt(TT_WGRAD), r_early, r_late, da, df, h2)
    (dz, p_in, p_out, small), (arr_w1, arr_w2) = _attn_bwd(
        tt(TT_ATTN_BWD), dmix, x2, z, cat, mod, row(norm1_pre), w_out_all, *attn_consts, red_fwd, red_bwd,
        [sums_w1, sums_w2])
    grad_x, small_head = _in_proj_bwd(tt(TT_IN_PROJ_BWD), dz, dx1, x2, mod, row(norm1_pre), w_in_t)
    (grad_w1, d_w1, m_w1, v_w1), (grad_w2, d_w2, m_w2, v_w2) = _adamw_fc(
        4, [(w_fc1, own_w1, arr_w1, diag_w1, m_w_fc1, v_w_fc1), (w_fc2, own_w2, arr_w2, diag_w2, m_w_fc2, v_w_fc2)])
    grad_in_t, grad_out, total = _tail_comm(
        [p_in.reshape(N_DEV, D_Z // N_DEV, D), p_out.reshape(N_DEV, D // N_DEV, D)], small, small_head, 64)

    grad_ada, d_ada, m_ada, v_ada = _adamw_ada(256, w_ada, sc, total, m_w_ada, v_w_ada)

    six = lambda a: a.reshape(6, 1, D)
    small_params = [
        (six(b_ada), six(m_b_ada), six(v_b_ada)),
        (row(norm1_pre), row(m_norm1_pre), row(v_norm1_pre)),
        (row(norm1_post), row(m_norm1_post), row(v_norm1_post)),
        (row(norm2_pre), row(m_norm2_pre), row(v_norm2_pre)),
        (row(norm2_post), row(m_norm2_post), row(v_norm2_post)),
        (row(ln_v_gain), row(m_ln_v_gain), row(v_ln_v_gain)),
        (row(ln_v_bias), row(m_ln_v_bias), row(v_ln_v_bias)),
        (row(pool_scale), row(m_pool_scale), row(v_pool_scale)),
        (row(b_pool), row(m_b_pool), row(v_b_pool)),
        (b_spatial, m_b_spatial, v_b_spatial),
        (w_spatial, m_w_spatial, v_w_spatial),
        (w_pool, m_w_pool, v_w_pool),
    ]
    outs = _adamw_small(total, small_params, [(w_out, grad_out, m_w_out, v_w_out),
                                              (w_in.T, grad_in_t, m_w_in.T, v_w_in.T)])
    d_out, m_out, v_out, d_in_t, m_in_t, v_in_t = outs[1 + 4 * len(small_params):]
    loss = outs[0].reshape(())
    names = ["b_ada", "norm1_pre", "norm1_post", "norm2_pre", "norm2_post", "ln_v_gain", "ln_v_bias", "pool_scale",
             "b_pool", "b_spatial", "w_spatial", "w_pool"]
    shapes = dict(b_ada=b_ada.shape, norm1_pre=norm1_pre.shape, norm1_post=norm1_post.shape,
                  norm2_pre=norm2_pre.shape, norm2_post=norm2_post.shape, ln_v_gain=ln_v_gain.shape,
                  ln_v_bias=ln_v_bias.shape, pool_scale=pool_scale.shape, b_pool=b_pool.shape,
                  b_spatial=b_spatial.shape, w_spatial=w_spatial.shape, w_pool=w_pool.shape)
    res = {}
    for k, nm in enumerate(names):
        res[nm] = tuple(o.reshape(shapes[nm]) for o in outs[1 + 4 * k:5 + 4 * k])
    res["w_ada"] = (grad_ada, d_ada, m_ada, v_ada)
    res["w_in"] = (grad_in_t.T, d_in_t.T, m_in_t.T, v_in_t.T)
    res["w_out"] = (grad_out, d_out, m_out, v_out)
    res["w_fc1"] = (grad_w1, d_w1, m_w1, v_w1)
    res["w_fc2"] = (grad_w2, d_w2, m_w2, v_w2)

    order = ["w_ada", "b_ada", "norm1_pre", "norm1_post", "w_in", "w_spatial", "b_spatial", "ln_v_gain", "ln_v_bias",
             "w_pool", "b_pool", "pool_scale", "w_out", "norm2_pre", "norm2_post", "w_fc1", "w_fc2"]
    return (loss, grad_x.reshape(x.shape),
            *[res[nm][0] for nm in order], *[res[nm][1] for nm in order],
            *[res[nm][2] for nm in order], *[res[nm][3] for nm in order])
```

```python
import functools

import jax
import jax.numpy as jnp
from jax import lax
from jax.experimental import pallas as pl
from jax.experimental.pallas import tpu as pltpu

F32 = jnp.float32
BF16 = jnp.bfloat16
MESH = pl.DeviceIdType.MESH

N_DEV = 8
D = 1024
D_A = 512
D_B = 512
D_Z = 2 * D_A + D_B
N_HEADS = 4
CHUNK = 128
WINDOWS = (2, 4, 8, 16)
GROUP = 128
D_FF = 4096
FF_BLK = D_FF // N_DEV
HALO = 16
EPS = 1e-6
VMEM_LIMIT = 60 * 1024 * 1024

ADAM_LR = 0.001
ADAM_B1 = 0.9
ADAM_B2 = 0.999
ADAM_EPS = 1e-08
ADAM_WD = 0.01
ADAM_STEP = 10

ROW_DMOD = 0
ROW_N1PRE, ROW_N1POST, ROW_N2PRE, ROW_N2POST = 8, 9, 10, 11
ROW_LN = 12
ROW_POOL = 13
ROW_LOSS = 14
ROW_BS = 16
ROW_WS = 24
ROW_WP = 88
SMALL_ROWS = 152
PACK_FINE = 40
PACK_HALF = PACK_FINE + 64
PACK_ROWS = 2 * PACK_HALF
PK_WS = PACK_FINE
PK_TABLE_B = PACK_HALF
PK_MISC = PK_TABLE_B + 24
PK_BS = PK_MISC + 8
PK_WP = PK_BS + 8


def _table_row(b):
    if isinstance(b, int):
        return 8 * b if 8 * b < PACK_FINE else 8 * b + PK_TABLE_B - PACK_FINE
    return 8 * b + jnp.where(8 * b < PACK_FINE, 0, PK_TABLE_B - PACK_FINE)


def _dot(a, b):
    return jnp.dot(a, b, preferred_element_type=F32)


def _dot_nt(a, b):
    return lax.dot_general(a, b, (((1,), (1,)), ((), ())), preferred_element_type=F32)


def _dot_tn(a, b):
    return lax.dot_general(a, b, (((0,), (0,)), ((), ())), preferred_element_type=F32)


def _rstd(v):
    return lax.rsqrt(jnp.mean(v * v, axis=-1, keepdims=True) + EPS)


def _rms_bwd(d_hat, hat, rstd):
    return rstd * (d_hat - hat * jnp.mean(d_hat * hat, axis=-1, keepdims=True))


def _rms_bwd_gained(g, gain, hat, rstd):
    g_hat = g * hat
    d_v = rstd * (g * gain - hat * jnp.mean(g_hat * gain, axis=-1, keepdims=True))
    return d_v, _colsum(g_hat)


_K0 = 0.7978845608028654
_K1 = 0.044715


def _gelu_parts(v):
    t = jnp.tanh(v * (_K0 + (_K0 * _K1) * (v * v)))
    return t, v * (0.5 + 0.5 * t)


def _gelu_grad(v, t):
    return (0.5 + 0.5 * t) + (0.5 * v) * (1.0 - t * t) * (_K0 + (3.0 * _K0 * _K1) * (v * v))


def _colsum(v):
    return jnp.sum(v, axis=0, keepdims=True)


def _full(shape):
    n = len(shape)
    return pl.BlockSpec(shape, lambda *_: (0,) * n)


def _resident(shape):
    n = len(shape)
    return pl.BlockSpec(shape, lambda *_: (0,) * n, pipeline_mode=pl.Buffered(1))


def _place():
    x, y, c = lax.axis_index("x"), lax.axis_index("y"), lax.axis_index("c")
    return x, y, c


def _flip(v, bit):
    return 1 - v if bit else v


def _peer(x, y, c, k):
    return (_flip(x, (k >> 2) & 1), _flip(y, (k >> 1) & 1), _flip(c, k & 1))


def _index(p):
    return 4 * p[0] + 2 * p[1] + p[2]


def _cast_shards(shards):
    def body(*refs):
        for src, dst in zip(refs[:len(shards)], refs[len(shards):]):
            dst[...] = src[...].astype(BF16)

    steps = 4
    specs = [pl.BlockSpec((s.shape[0] // steps, s.shape[1]), lambda i: (i, 0)) for s in shards]
    return pl.pallas_call(
        body, name="cast_shards", grid=(steps,),
        out_shape=tuple(jax.ShapeDtypeStruct(s.shape, BF16) for s in shards),
        in_specs=specs, out_specs=tuple(specs),
        compiler_params=pltpu.CompilerParams(dimension_semantics=("arbitrary",)),
    )(*shards)


FC_EARLY = 6
FC_HEAD = 2
R_HEAD_COLS = (FC_EARLY - FC_HEAD) * FF_BLK
WGRAD_ORDER = (7, 6, 1, 3, 5, 2, 4, 0)


def _early_col(j):
    return R_HEAD_COLS + j * FF_BLK if j < FC_HEAD else (j - FC_HEAD) * FF_BLK


class _Copies:
    def __init__(self, entries, send_sems, recv_sems):
        self.place = _place()
        self.entries, self.send_sems, self.recv_sems = entries, send_sems, recv_sems

    def _copy(self, i, arrival=False):
        src, dst, rel = self.entries[i]
        return pltpu.make_async_remote_copy(
            src_ref=dst if arrival else src, dst_ref=dst, send_sem=self.send_sems.at[i],
            recv_sem=self.recv_sems.at[i], device_id=_peer(*self.place, rel), device_id_type=MESH)

    def start(self, *which):
        for i in which:
            self._copy(i).start()

    def wait_recv(self, *which):
        for i in which:
            self._copy(i, arrival=True).wait_recv()

    def wait_send(self, *which):
        for i in which:
            self._copy(i).wait_send()


TAIL_STEPS = 3


def _tail_comm(parts, small, head, row_chunk):
    n = len(parts)

    def body(*refs):
        p_refs, small_ref, head_ref = refs[:n], refs[n], refs[n + 1]
        outs = refs[n + 2:]
        g_refs, total_ref = outs[:n], outs[n]
        scr = outs[n + 1:]
        from_sib = scr[0:n]
        chip_out = scr[n:2 * n]
        chip_in = scr[2 * n:3 * n]
        pack, pack_sib, fine, bulk, total_scr = scr[3 * n:3 * n + 5]
        send_a, recv_a, send_b, recv_b, send_s, recv_s = scr[3 * n + 5:]
        step = pl.program_id(0)
        x, y, c = _place()
        me = _index((x, y, c))
        sibling = (x, y, 1 - c)
        my_chip = 2 * x + y
        others = [(1 - x, y), (x, 1 - y), (1 - x, 1 - y)]
        my_half = pl.ds(pl.multiple_of(PACK_HALF * c, 8), PACK_HALF)

        def pack_to_sibling():
            return pltpu.make_async_remote_copy(
                src_ref=pack, dst_ref=pack_sib, send_sem=send_s.at[0], recv_sem=recv_s.at[0],
                device_id=sibling, device_id_type=MESH)

        def half_to_chip(r, part):
            buf = (fine, bulk)[part]
            return pltpu.make_async_remote_copy(
                src_ref=buf.at[my_chip], dst_ref=buf.at[my_chip],
                send_sem=send_s.at[1 + 3 * part + r], recv_sem=recv_s.at[1 + 3 * part + r],
                device_id=(*others[r], c), device_id_type=MESH)

        def half_from_chip(r, part):
            k = 2 * others[r][0] + others[r][1]
            buf = (fine, bulk)[part]
            return pltpu.make_async_remote_copy(
                src_ref=buf.at[k], dst_ref=buf.at[k],
                send_sem=send_s.at[1 + 3 * part + r], recv_sem=recv_s.at[1 + 3 * part + r],
                device_id=(*others[r], c), device_id_type=MESH)

        def total_to_sibling():
            return pltpu.make_async_remote_copy(
                src_ref=total_scr.at[my_half], dst_ref=total_scr.at[my_half],
                send_sem=send_s.at[7], recv_sem=recv_s.at[7], device_id=sibling, device_id_type=MESH)

        def total_from_sibling():
            sib_half = pl.ds(pl.multiple_of(PACK_HALF * (1 - c), 8), PACK_HALF)
            return pltpu.make_async_remote_copy(
                src_ref=total_scr.at[sib_half], dst_ref=total_scr.at[sib_half],
                send_sem=send_s.at[7], recv_sem=recv_s.at[7], device_id=sibling, device_id_type=MESH)

        def to_sibling(a, k):
            return pltpu.make_async_remote_copy(
                src_ref=p_refs[a].at[2 * k + (1 - c)], dst_ref=from_sib[a].at[k],
                send_sem=send_a.at[a], recv_sem=recv_a.at[a], device_id=sibling, device_id_type=MESH)

        def all_from_sibling(a):
            return pltpu.make_async_remote_copy(
                src_ref=from_sib[a], dst_ref=from_sib[a], send_sem=send_a.at[a], recv_sem=recv_a.at[a],
                device_id=sibling, device_id_type=MESH)

        def to_chip(a, r):
            return pltpu.make_async_remote_copy(
                src_ref=chip_out[a].at[r], dst_ref=chip_in[a].at[r],
                send_sem=send_b.at[3 * a + r], recv_sem=recv_b.at[3 * a + r],
                device_id=(*others[r], c), device_id_type=MESH)

        @pl.when(step == 0)
        def _():
            pack[0:PACK_FINE, :] = jnp.zeros((PACK_FINE, D), F32)
            pack[PK_TABLE_B:PK_MISC, :] = jnp.zeros((PK_MISC - PK_TABLE_B, D), F32)
            pack[pl.ds(pl.multiple_of(_table_row(me), 8), 8), :] = small_ref[0:8, :] + head_ref[0:8, :]
            pack[PK_WS:PK_WS + 64, :] = small_ref[ROW_WS:ROW_WS + 64, :]
            pack[PK_MISC:PK_MISC + 8, :] = small_ref[ROW_N1PRE:ROW_N1PRE + 8, :] + head_ref[8:16, :]
            pack[PK_BS:PK_BS + 8, :] = small_ref[ROW_BS:ROW_BS + 8, :]
            pack[PK_WP:PK_WP + 64, :] = small_ref[ROW_WP:ROW_WP + 64, :]
            pack_to_sibling().start()
            for a in range(n):
                for k in range(4):
                    to_sibling(a, k).start()

        @pl.when(step == 1)
        def _():
            pack_to_sibling().wait_recv()
            chip_sum = pack[my_half, :] + pack_sib[my_half, :]
            fine[my_chip] = chip_sum[:PACK_FINE, :]
            bulk[my_chip] = chip_sum[PACK_FINE:, :].astype(BF16)
            for r in range(3):
                half_to_chip(r, 0).start()
                half_to_chip(r, 1).start()
            for a in range(n):
                all_from_sibling(a).wait_recv()
                rows = p_refs[a].shape[1]
                for r in range(3):
                    k = 2 * others[r][0] + others[r][1]
                    for s in range(0, rows, row_chunk):
                        sl = pl.ds(s, row_chunk)
                        chip_out[a][r, sl, :] = (p_refs[a][2 * k + c, sl, :].astype(F32)
                                                 + from_sib[a][k, sl, :].astype(F32)).astype(BF16)
                    to_chip(a, r).start()
                for s in range(0, rows, row_chunk):
                    sl = pl.ds(s, row_chunk)
                    g_refs[a][sl, :] = (p_refs[a][2 * my_chip + c, sl, :].astype(F32)
                                        + from_sib[a][my_chip, sl, :].astype(F32))

        @pl.when(step == TAIL_STEPS - 1)
        def _():
            for r in range(3):
                half_from_chip(r, 0).wait_recv()
                half_from_chip(r, 1).wait_recv()
            half_start = pl.multiple_of(PACK_HALF * c, 8)
            total_scr[pl.ds(half_start, PACK_FINE), :] = ((fine[0] + fine[1]) + fine[2]) + fine[3]
            total_scr[pl.ds(half_start + PACK_FINE, PACK_HALF - PACK_FINE), :] = (
                (bulk[0].astype(F32) + bulk[1].astype(F32)) + bulk[2].astype(F32)) + bulk[3].astype(F32)
            total_to_sibling().start()
            for a in range(n):
                rows = p_refs[a].shape[1]
                for r in range(3):
                    to_chip(a, r).wait_recv()
                    for s in range(0, rows, row_chunk):
                        sl = pl.ds(s, row_chunk)
                        g_refs[a][sl, :] = g_refs[a][sl, :] + chip_in[a][r, sl, :].astype(F32)
            total_from_sibling().wait_recv()
            total_ref[...] = total_scr[...]
            for a in range(n):
                all_from_sibling(a).wait_send()
                for r in range(3):
                    to_chip(a, r).wait_send()
            pack_to_sibling().wait_send()
            for r in range(3):
                half_to_chip(r, 0).wait_send()
                half_to_chip(r, 1).wait_send()
            total_to_sibling().wait_send()

    return pl.pallas_call(
        body, name="tail_comm", grid=(TAIL_STEPS,),
        out_shape=tuple([jax.ShapeDtypeStruct(p.shape[1:], F32) for p in parts]
                        + [jax.ShapeDtypeStruct((PACK_ROWS, D), F32)]),
        in_specs=[_resident(p.shape) for p in parts] + [_resident(small.shape), _resident(head.shape)],
        out_specs=tuple([_full(p.shape[1:]) for p in parts] + [_full((PACK_ROWS, D))]),
        scratch_shapes=(
            [pltpu.VMEM((4,) + p.shape[1:], BF16) for p in parts]
            + [pltpu.VMEM((3,) + p.shape[1:], BF16) for p in parts]
            + [pltpu.VMEM((3,) + p.shape[1:], BF16) for p in parts]
            + [pltpu.VMEM((PACK_ROWS, D), F32), pltpu.VMEM((PACK_ROWS, D), F32),
               pltpu.VMEM((4, PACK_FINE, D), F32), pltpu.VMEM((4, PACK_HALF - PACK_FINE, D), BF16),
               pltpu.VMEM((PACK_ROWS, D), F32)]
            + [pltpu.SemaphoreType.DMA((n,)), pltpu.SemaphoreType.DMA((n,)),
               pltpu.SemaphoreType.DMA((3 * n,)), pltpu.SemaphoreType.DMA((3 * n,)),
               pltpu.SemaphoreType.DMA((8,)), pltpu.SemaphoreType.DMA((8,))]),
        compiler_params=pltpu.CompilerParams(dimension_semantics=("arbitrary",), vmem_limit_bytes=VMEM_LIMIT),
    )(*parts, small, head)


def _tril_mask():
    row = lax.broadcasted_iota(jnp.int32, (CHUNK, CHUNK), 0)
    col = lax.broadcasted_iota(jnp.int32, (CHUNK, CHUNK), 1)
    return (col <= row).astype(F32)


def _window_sums(ext):
    s2 = ext + pltpu.roll(ext, 1, 0)
    t4 = s2[:, GROUP:]
    s4 = t4 + pltpu.roll(t4, 2, 0)
    t8 = s4[:, GROUP:]
    s8 = t8 + pltpu.roll(t8, 4, 0)
    t16 = s8[:, GROUP:]
    s16 = t16 + pltpu.roll(t16, 8, 0)
    return [s2[:, :GROUP], s4[:, :GROUP], s8[:, :GROUP], s16]


def _inv_counts(first_pos, rows):
    pos = first_pos + lax.broadcasted_iota(jnp.int32, (rows, 1), 0)
    return [1.0 / jnp.minimum(pos + 1, w).astype(F32) for w in WINDOWS]


def _pool_diff(zb, halo, first_pos):
    tt = zb.shape[0]
    sums = _window_sums(jnp.concatenate([halo, zb], axis=0))
    inv = _inv_counts(first_pos, tt)
    return [sums[g][HALO:, :] * inv[g] - zb[:, g * GROUP:(g + 1) * GROUP] for g in range(len(WINDOWS))]


def _row_blocks(scr, rows, place):
    def block(rel):
        start = pl.multiple_of(rows * _index(_peer(*place, rel)), rows)
        return scr.at[pl.ds(start, rows), :]

    def entries(shard_ref):
        return ([(shard_ref, block(0), rel) for rel in (1, 2, 4, 6)]
                + [(block(rel), block(rel), 1) for rel in (2, 4, 6)])
    return block, entries


def _attn_fwd(tt, x, c_row, w_ada, b_pieces, n1pre, n1post, w_in_shard, w_out_shard, w_sp, bs_rows, ln_g, ln_b,
              w_pool, b_pool, pool_scale, fc_shards, n2pre):
    t_len = x.shape[0]
    nt = t_len // tt
    ncol = w_ada.shape[1]

    def body(x_ref, xb_ref, c_ref, wada_ref, b_ref, n1pre_ref, n1post_ref, wi_ref, wo_ref, wsp_ref, bs_ref,
             lng_ref, lnb_ref, wp_ref, bp_ref, ps_ref, w1_ref, w2_ref, n2pre_ref,
             z_ref, cat_ref, mix_ref, x1_ref, h2_ref, r_ref, f_ref, mod_out, sc_out, e1_ref, e2_ref, wout_ref,
             win_out, carry, land1, land2, sib1, sib2, cat_keep, wout_scr, win_ref, mod_ref, cg, mg, part,
             send_sems, recv_sems, local_sems, wo_send, wo_recv, wi_send, wi_recv, ada_send, ada_recv):
        i = pl.program_id(0)
        place = px, py, pc = _place()
        me = _index(place)
        wo_block, wo_entries = _row_blocks(wout_scr, w_out_shard.shape[0], place)
        wi_block, wi_entries = _row_blocks(win_ref, w_in_shard.shape[0], place)
        wo_copies = _Copies(wo_entries(wo_ref), wo_send, wo_recv)
        wi_copies = _Copies(wi_entries(wi_ref), wi_send, wi_recv)
        wo_keep = pltpu.make_async_copy(wout_scr, wout_ref, local_sems.at[8])
        wi_keep = pltpu.make_async_copy(win_ref, win_out, local_sems.at[9])
        ada = _Copies([(cg.at[me], cg.at[me], k) for k in range(1, N_DEV)]
                      + [(part, mg.at[me], k) for k in range(1, N_DEV)], ada_send, ada_recv)
        copies = _Copies(
            [(w1_ref, sib1, 1), (w2_ref, sib2, 1),
             (w1_ref, land1.at[0], 2), (w2_ref, land2.at[0], 2),
             (w1_ref, land1.at[1], 4), (w2_ref, land2.at[1], 4),
             (land1.at[0], e1_ref.at[3], 1), (land2.at[0], e2_ref.at[3], 1),
             (land1.at[1], e1_ref.at[5], 1), (land2.at[1], e2_ref.at[5], 1)],
            send_sems, recv_sems)
        keep = [pltpu.make_async_copy(w1_ref, e1_ref.at[0], local_sems.at[0]),
                pltpu.make_async_copy(w2_ref, e2_ref.at[0], local_sems.at[1]),
                pltpu.make_async_copy(land1.at[0], e1_ref.at[2], local_sems.at[2]),
                pltpu.make_async_copy(land1.at[1], e1_ref.at[4], local_sems.at[3]),
                pltpu.make_async_copy(land2.at[0], e2_ref.at[2], local_sems.at[4]),
                pltpu.make_async_copy(land2.at[1], e2_ref.at[4], local_sems.at[5]),
                pltpu.make_async_copy(sib1, e1_ref.at[1], local_sems.at[6]),
                pltpu.make_async_copy(sib2, e2_ref.at[1], local_sems.at[7])]

        @pl.when(i == 0)
        def _():
            cg[me] = jnp.broadcast_to(c_ref[...], (8, D))
            ada.start(*range(N_DEV - 1))
            wi_copies.start(0, 1, 2, 3)
            wi_rows, wo_rows = w_in_shard.shape[0], w_out_shard.shape[0]
            win_ref[pl.ds(pl.multiple_of(wi_rows * me, wi_rows), wi_rows), :] = wi_ref[...]
            wout_scr[pl.ds(pl.multiple_of(wo_rows * me, wo_rows), wo_rows), :] = wo_ref[...]
            carry[...] = jnp.zeros_like(carry)

            ada.wait_recv(*range(N_DEV - 1))
            c_all = jnp.concatenate([cg[j, 0:1, :] for j in range(N_DEV)], axis=0)
            sc = c_all * jax.nn.sigmoid(c_all)
            sc_out[...] = sc
            part[...] = _dot(sc.astype(BF16), wada_ref[...].astype(BF16)) + b_ref[me]
            ada.start(*range(N_DEV - 1, 2 * (N_DEV - 1)))
            wo_copies.start(0, 1, 2, 3)
            copies.start(0, 1, 2, 4, 3, 5)
            keep[0].start()
            keep[1].start()
            mg[me] = part[...]
            ada.wait_recv(*range(N_DEV - 1, 2 * (N_DEV - 1)))
            mod_ref[...] = jnp.zeros_like(mod_ref)
            for j in range(N_DEV):
                for m in range(6):
                    lo, hi = max(ncol * j, D * m), min(ncol * (j + 1), D * (m + 1))
                    if lo < hi:
                        mod_ref[m:m + 1, lo - D * m:hi - D * m] = mg[j, pl.ds(me, 1), lo - ncol * j:hi - ncol * j]
            mod_out[...] = mod_ref[...]

            wi_copies.wait_recv(1, 2, 3)
            wi_copies.start(4, 5, 6)
            wi_copies.wait_recv(0, 4, 5, 6)
            wi_keep.start()

        @pl.when(i == nt // 2 + ATTN_LAG)
        def _():
            copies.wait_recv(2, 4)
            copies.start(6, 8)
            keep[2].start()
            keep[3].start()

        @pl.when(i == nt - 1 + ATTN_LAG)
        def _():
            copies.wait_recv(3, 5)
            copies.start(7, 9)
            keep[4].start()
            keep[5].start()

        shift1, scale1, gate1 = mod_ref[0:1, :], mod_ref[1:2, :], mod_ref[2:3, :]

        @pl.when(i < nt)
        def _():
            xv = x_ref[...]
            h1 = (xv * _rstd(xv)) * (n1pre_ref[...] * (1.0 + scale1)) + shift1
            z = _dot_nt(h1.astype(BF16), win_ref[...])
            z_ref[...] = z

            _, ga = _gelu_parts(z[:, :2 * D_A])
            u, vr = ga[:, :D_A], ga[:, D_A:]
            dv = vr - jnp.mean(vr, axis=-1, keepdims=True)
            v = (dv * lax.rsqrt(jnp.mean(dv * dv, axis=-1, keepdims=True) + EPS)) * lng_ref[...] + lnb_ref[...]
            vb = v.astype(BF16)
            mask = _tril_mask()
            wc = [(wsp_ref[h] * mask).astype(BF16) for h in range(N_HEADS)]
            for ch in range(tt // CHUNK):
                rows = slice(ch * CHUNK, (ch + 1) * CHUNK)
                for h in range(N_HEADS):
                    cols = slice(h * GROUP, (h + 1) * GROUP)
                    mixed = _dot(wc[h], vb[rows, cols]) + bs_ref[:, cols]
                    cat_ref[rows, cols] = (u[rows, cols] * mixed).astype(BF16)

            zb = z[:, 2 * D_A:]
            diff = _pool_diff(zb, carry[...], i * tt)
            carry[...] = zb[tt - HALO:, :]
            for g in range(len(WINDOWS)):
                cols = slice(g * GROUP, (g + 1) * GROUP)
                pre = _dot(diff[g].astype(BF16), wp_ref[g].astype(BF16)) + bp_ref[:, cols]
                cat_ref[:, D_A + g * GROUP:D_A + (g + 1) * GROUP] = (pre * ps_ref[:, cols]).astype(BF16)
            cat_keep[i % (ATTN_LAG + 1)] = cat_ref[...]

        @pl.when(i == 0)
        def _():
            copies.wait_recv(0, 1)
            keep[6].start()
            keep[7].start()

        @pl.when(i == 1)
        def _():
            wo_copies.wait_recv(1, 2, 3)
            wo_copies.start(4, 5, 6)

        @pl.when(i == ATTN_LAG)
        def _():
            wo_copies.wait_recv(0, 4, 5, 6)
            wo_keep.start()

        @pl.when(i >= ATTN_LAG)
        def _():
            xv = xb_ref[...]
            mix = _dot(cat_keep[(i - ATTN_LAG) % (ATTN_LAG + 1)], wout_scr[...])
            mix_ref[...] = mix
            x1v = xv + (mix * _rstd(mix)) * (gate1 * n1post_ref[...])
            x1_ref[...] = x1v
            shift2, scale2 = mod_ref[3:4, :], mod_ref[4:5, :]
            h2 = ((x1v * _rstd(x1v)) * (n2pre_ref[...] * (1.0 + scale2)) + shift2).astype(BF16)
            h2_ref[...] = h2
            for j, (w1, w2) in enumerate(((w1_ref, w2_ref), (sib1, sib2))):
                ra = jnp.maximum(_dot(h2, w1[...]), 0.0)
                r = (ra * ra).astype(BF16)
                r_ref[:, j * FF_BLK:(j + 1) * FF_BLK] = r
                if j == 0:
                    f_ref[...] = _dot(r, w2[...])
                else:
                    f_ref[...] += _dot(r, w2[...])

        @pl.when(i == nt - 1 + ATTN_LAG)
        def _():
            copies.wait_recv(6, 7, 8, 9)
            copies.wait_send(*range(10))
            wo_copies.wait_send(*range(7))
            wi_copies.wait_send(*range(7))
            ada.wait_send(*range(2 * (N_DEV - 1)))
            for cp in keep:
                cp.wait()
            wo_keep.wait()
            wi_keep.wait()

    first = lambda w: pl.BlockSpec((tt, w), lambda i: (jnp.minimum(i, nt - 1), 0))
    second = lambda w: pl.BlockSpec((tt, w), lambda i: (jnp.maximum(i - ATTN_LAG, 0), 0))
    r_head = pl.BlockSpec((tt, FC_HEAD * FF_BLK),
                          lambda i: (jnp.maximum(i - ATTN_LAG, 0), R_HEAD_COLS // (FC_HEAD * FF_BLK)))
    hbm = pl.BlockSpec(memory_space=pl.ANY)
    outs = pl.pallas_call(
        body, name="attn_fwd", grid=(nt + ATTN_LAG,),
        out_shape=tuple([jax.ShapeDtypeStruct((t_len, D_Z), F32), jax.ShapeDtypeStruct((t_len, D), BF16),
                         jax.ShapeDtypeStruct((t_len, D), F32), jax.ShapeDtypeStruct((t_len, D), F32),
                         jax.ShapeDtypeStruct((t_len, D), BF16),
                         jax.ShapeDtypeStruct((t_len, FC_EARLY * FF_BLK), BF16),
                         jax.ShapeDtypeStruct((t_len, D), F32)]
                        + [jax.ShapeDtypeStruct((8, D), F32), jax.ShapeDtypeStruct((N_DEV, D), F32)]
                        + [jax.ShapeDtypeStruct((FC_EARLY,) + s.shape, BF16) for s in fc_shards]
                        + [jax.ShapeDtypeStruct((D, D), BF16), jax.ShapeDtypeStruct((D_Z, D), BF16)]),
        in_specs=[first(D), second(D), _full((1, D)), _resident(w_ada.shape), _full((N_DEV, 1, ncol)), _full((1, D)),
                  _full((1, D)), _resident(w_in_shard.shape), _resident(w_out_shard.shape),
                  _full((N_HEADS, CHUNK, CHUNK)), _full((CHUNK, D_A)), _full((1, D_A)), _full((1, D_A)),
                  _full((len(WINDOWS), GROUP, GROUP)), _full((1, D_B)), _full((1, D_B)),
                  _resident(fc_shards[0].shape), _resident(fc_shards[1].shape), _full((1, D))],
        out_specs=(first(D_Z), first(D), second(D), second(D), second(D), r_head, second(D),
                   _full((8, D)), _full((N_DEV, D)), hbm, hbm, hbm, hbm),
        scratch_shapes=[pltpu.VMEM((HALO, D_B), F32),
                        pltpu.VMEM((2,) + fc_shards[0].shape, BF16), pltpu.VMEM((2,) + fc_shards[1].shape, BF16),
                        pltpu.VMEM(fc_shards[0].shape, BF16), pltpu.VMEM(fc_shards[1].shape, BF16),
                        pltpu.VMEM((ATTN_LAG + 1, tt, D), BF16), pltpu.VMEM((D, D), BF16),
                        pltpu.VMEM((D_Z, D), BF16), pltpu.VMEM((8, D), F32),
                        pltpu.VMEM((N_DEV, 8, D), F32), pltpu.VMEM((N_DEV, N_DEV, ncol), F32),
                        pltpu.VMEM((N_DEV, ncol), F32),
                        pltpu.SemaphoreType.DMA((10,)), pltpu.SemaphoreType.DMA((10,)),
                        pltpu.SemaphoreType.DMA((10,)),
                        pltpu.SemaphoreType.DMA((7,)), pltpu.SemaphoreType.DMA((7,)),
                        pltpu.SemaphoreType.DMA((7,)), pltpu.SemaphoreType.DMA((7,)),
                        pltpu.SemaphoreType.DMA((2 * (N_DEV - 1),)), pltpu.SemaphoreType.DMA((2 * (N_DEV - 1),))],
        compiler_params=pltpu.CompilerParams(dimension_semantics=("arbitrary",), vmem_limit_bytes=VMEM_LIMIT),
    )(x, x, c_row, w_ada, b_pieces, n1pre, n1post, w_in_shard, w_out_shard, w_sp, bs_rows, ln_g, ln_b, w_pool, b_pool,
      pool_scale, *fc_shards, n2pre)
    return outs[:9], outs[9:11], outs[11], outs[12]


def _mlp_fwd_early(tt, r_begun, h2, f_head, w1_early, w2_early):
    t_len = h2.shape[0]
    nt = t_len // tt
    n_late = N_DEV - FC_EARLY

    def body(r_begun_ref, h2_ref, fh_ref, w1_ref, w2_ref, r_ref, f_ref, l1_ref, l2_ref,
             land1, land2, send_sems, recv_sems, local_sems):
        i = pl.program_id(0)
        copies = _Copies(
            [(w1_ref.at[2], land1, 4), (w2_ref.at[4], land2, 2),
             (land1, l1_ref.at[1], 1), (land2, l2_ref.at[1], 1)],
            send_sems, recv_sems)
        keep = [pltpu.make_async_copy(land1, l1_ref.at[0], local_sems.at[0]),
                pltpu.make_async_copy(land2, l2_ref.at[0], local_sems.at[1])]

        @pl.when(i == 0)
        def _():
            copies.start(0, 1)

        @pl.when(i == nt - 1)
        def _():
            copies.wait_recv(0, 1)
            copies.start(2, 3)
            for cp in keep:
                cp.start()

        h2 = h2_ref[...]
        f_ref[...] = fh_ref[...]
        for j in range(FC_HEAD, FC_EARLY):
            ra = jnp.maximum(_dot(h2, w1_ref[j]), 0.0)
            r = (ra * ra).astype(BF16)
            r_ref[:, _early_col(j):_early_col(j) + FF_BLK] = r
            f_ref[...] += _dot(r, w2_ref[j])

        @pl.when(i == nt - 1)
        def _():
            copies.wait_recv(2, 3)
            copies.wait_send(0, 1, 2, 3)
            for cp in keep:
                cp.wait()

    tile = lambda w: pl.BlockSpec((tt, w), lambda i: (i, 0))
    hbm = pl.BlockSpec(memory_space=pl.ANY)
    outs = pl.pallas_call(
        body, name="mlp_fwd_early", grid=(nt,),
        out_shape=(jax.ShapeDtypeStruct((t_len, FC_EARLY * FF_BLK), BF16), jax.ShapeDtypeStruct((t_len, D), F32),
                   jax.ShapeDtypeStruct((n_late,) + w1_early.shape[1:], BF16),
                   jax.ShapeDtypeStruct((n_late,) + w2_early.shape[1:], BF16)),
        in_specs=[hbm, tile(D), tile(D), _resident((FC_EARLY, D, FF_BLK)), _resident((FC_EARLY, FF_BLK, D))],
        out_specs=(tile(R_HEAD_COLS), tile(D), hbm, hbm),
        input_output_aliases={0: 0},
        scratch_shapes=[pltpu.VMEM(w1_early.shape[1:], BF16), pltpu.VMEM(w2_early.shape[1:], BF16),
                        pltpu.SemaphoreType.DMA((4,)), pltpu.SemaphoreType.DMA((4,)),
                        pltpu.SemaphoreType.DMA((2,))],
        compiler_params=pltpu.CompilerParams(dimension_semantics=("arbitrary",), vmem_limit_bytes=VMEM_LIMIT),
    )(r_begun, h2, f_head, w1_early, w2_early)
    return outs[:2], outs[2:]


def _mlp_late_bwd(tt, r_early, x1, h2, f_early, tgt, mix, mod, n2pre, n2post, n1post,
                  w1_early, w2_early, w1_late, w2_late):
    t_len = x1.shape[0]
    nt = t_len // tt
    n_late = N_DEV - FC_EARLY
    late_cols = n_late * FF_BLK

    def body(re_ref, x1_ref, h2_ref, fe_ref, tgt_ref, mix_ref, mod_ref, n2pre_ref, n2post_ref,
             n1post_ref, w1e_hbm, w2e_hbm, w1l_ref, w2l_ref,
             rl_ref, df_ref, da_ref, dmix_ref, dx1_ref, redf_ref, redb_ref, dh2_acc, w1e_ref, w2e_ref, w_sems):
        i = pl.program_id(0)
        fetch = [pltpu.make_async_copy(w2e_hbm, w2e_ref, w_sems.at[0]),
                 pltpu.make_async_copy(w1e_hbm, w1e_ref, w_sems.at[1])]

        @pl.when(i == 0)
        def _():
            for cp in fetch:
                cp.start()
            redf_ref[...] = jnp.zeros_like(redf_ref)
            redb_ref[...] = jnp.zeros_like(redb_ref)

        x1v = x1_ref[...]
        gate1, scale2, gate2 = mod_ref[2:3, :], mod_ref[4:5, :], mod_ref[5:6, :]
        h2 = h2_ref[...]
        f = fe_ref[...]
        for j in range(n_late):
            cols = slice(j * FF_BLK, (j + 1) * FF_BLK)
            ra = jnp.maximum(_dot(h2, w1l_ref[j]), 0.0)
            r = (ra * ra).astype(BF16)
            rl_ref[:, cols] = r
            f = f + _dot(r, w2l_ref[j])
        post2 = n2post_ref[...]
        gate_post2 = gate2 * post2
        rf = _rstd(f)
        fhat = f * rf
        err = (x1v + fhat * gate_post2) - tgt_ref[...]
        dy = err * (1.0 / D)
        d_f, sum_f = _rms_bwd_gained(dy, gate_post2, fhat, rf)
        dfv = d_f.astype(BF16)
        df_ref[...] = dfv
        redf_ref[0:1, :] += post2 * sum_f
        redf_ref[1:2, :] += gate2 * sum_f
        redf_ref[2:3, :] += _colsum(err * err)

        @pl.when(i == 0)
        def _():
            for cp in fetch:
                cp.wait()

        for j in range(N_DEV):
            cols = slice(j * FF_BLK, (j + 1) * FF_BLK)
            if j < FC_EARLY:
                w1, w2, r = w1e_ref[j], w2e_ref[j], re_ref[:, _early_col(j):_early_col(j) + FF_BLK]
            else:
                jl = j - FC_EARLY
                w1, w2, r = w1l_ref[jl], w2l_ref[jl], rl_ref[:, jl * FF_BLK:(jl + 1) * FF_BLK]
            dr = _dot_nt(dfv, w2)
            da = (dr * (2.0 * jnp.sqrt(r.astype(F32)))).astype(BF16)
            da_ref[:, cols] = da
            contrib = _dot_nt(da, w1)
            if j == 0:
                dh2_acc[...] = contrib
            else:
                dh2_acc[...] += contrib
        dh2 = dh2_acc[...]
        pre2, post1 = n2pre_ref[...], n1post_ref[...]
        r2 = _rstd(x1v)
        xhat = x1v * r2
        d_x1, sum_h = _rms_bwd_gained(dh2, pre2 * (1.0 + scale2), xhat, r2)
        dx1 = dy + d_x1
        dx1_ref[...] = dx1
        mixv = mix_ref[...]
        rm = _rstd(mixv)
        mhat = mixv * rm
        d_mix, sum_m = _rms_bwd_gained(dx1, gate1 * post1, mhat, rm)
        dmix_ref[...] = d_mix.astype(BF16)
        redb_ref[0:1, :] += _colsum(dh2)
        redb_ref[1:2, :] += pre2 * sum_h
        redb_ref[2:3, :] += (1.0 + scale2) * sum_h
        redb_ref[3:4, :] += post1 * sum_m
        redb_ref[4:5, :] += gate1 * sum_m

    tile = lambda w: pl.BlockSpec((tt, w), lambda i: (i, 0))
    return pl.pallas_call(
        body, name="mlp_late_bwd", grid=(nt,),
        out_shape=(jax.ShapeDtypeStruct((t_len, late_cols), BF16), jax.ShapeDtypeStruct((t_len, D), BF16),
                   jax.ShapeDtypeStruct((t_len, D_FF), BF16), jax.ShapeDtypeStruct((t_len, D), BF16),
                   jax.ShapeDtypeStruct((t_len, D), F32), jax.ShapeDtypeStruct((8, D), F32),
                   jax.ShapeDtypeStruct((8, D), F32)),
        in_specs=[tile(FC_EARLY * FF_BLK), tile(D), tile(D), tile(D), tile(D),
                  tile(D), _full((8, D)), _full((1, D)), _full((1, D)), _full((1, D)),
                  pl.BlockSpec(memory_space=pl.ANY), pl.BlockSpec(memory_space=pl.ANY),
                  _resident((n_late, D, FF_BLK)), _resident((n_late, FF_BLK, D))],
        out_specs=(tile(late_cols), tile(D), tile(D_FF), tile(D), tile(D), _full((8, D)), _full((8, D))),
        scratch_shapes=[pltpu.VMEM((tt, D), F32), pltpu.VMEM((FC_EARLY, D, FF_BLK), BF16),
                        pltpu.VMEM((FC_EARLY, FF_BLK, D), BF16), pltpu.SemaphoreType.DMA((2,))],
        compiler_params=pltpu.CompilerParams(dimension_semantics=("arbitrary",), vmem_limit_bytes=VMEM_LIMIT),
    )(r_early, x1, h2, f_early, tgt, mix, mod, n2pre, n2post, n1post, w1_early, w2_early, w1_late, w2_late)


def _mlp_wgrad(tt, r_early, r_late, da, df, h2):
    t_len = df.shape[0]
    nt = t_len // tt
    odd_steps = [j for j, rel in enumerate(WGRAD_ORDER) if rel % 2]

    def relation(j):
        rel = jnp.int32(WGRAD_ORDER[-1])
        for step in range(N_DEV - 2, -1, -1):
            rel = jnp.where(j == step, WGRAD_ORDER[step], rel)
        return rel

    def body(re_ref, rl_ref, da_ref, df_hbm, h2_hbm, own1_ref, own2_ref, out1_ref, out2_ref, diag1_ref, diag2_ref,
             acc1, acc2, snd1, snd2, sib1, sib2, dsnd1, dsnd2, send_sems, recv_sems, df_ref, h2_ref, fetch_sems):
        j, t = pl.program_id(0), pl.program_id(1)
        rows = pl.ds(pl.multiple_of(t * tt, tt), tt)
        fetch = [pltpu.make_async_copy(src.at[pl.ds(k * tt, tt)], dst.at[pl.ds(k * tt, tt)], fetch_sems.at[2 * k + a])
                 for k in range(nt) for a, (src, dst) in enumerate(((df_hbm, df_ref), (h2_hbm, h2_ref)))]

        @pl.when((j == 0) & (t == 0))
        def _():
            for cp in fetch:
                cp.start()

        for k in range(nt):
            @pl.when((j == 0) & (t == k))
            def _():
                fetch[2 * k].wait()
                fetch[2 * k + 1].wait()
        x, y, c = _place()
        accs, snds, sibs = (acc1, acc2), (snd1, snd2), (sib1, sib2)
        dsnds, diags = (dsnd1, dsnd2), (diag1_ref, diag2_ref)

        def to_sibling(a, jj, buf=0):
            return pltpu.make_async_remote_copy(
                src_ref=snds[a].at[buf], dst_ref=sibs[a].at[jj],
                send_sem=send_sems.at[4 * a + jj], recv_sem=recv_sems.at[4 * a + jj],
                device_id=(x, y, 1 - c), device_id_type=MESH)

        def to_diagonal(a):
            return pltpu.make_async_remote_copy(
                src_ref=dsnds[a], dst_ref=diags[a], send_sem=send_sems.at[8 + a], recv_sem=recv_sems.at[8 + a],
                device_id=_peer(x, y, c, 6), device_id_type=MESH)

        @pl.when(t == 0)
        def _():
            acc2[...] = jnp.zeros_like(acc2)
            acc1[...] = jnp.zeros_like(acc1)

        for r_ref, mine in ((re_ref, relation(j) < FC_EARLY), (rl_ref, relation(j) >= FC_EARLY)):
            @pl.when(mine)
            def _():
                acc2[...] += _dot_tn(r_ref[...], df_ref[rows, :])
                acc1[...] += _dot_tn(h2_ref[rows, :], da_ref[...])

        for step, rel in enumerate(WGRAD_ORDER):
            jj = rel // 2

            @pl.when((t == nt - 1) & (j == step))
            def _():
                for a, (own_ref, out_ref) in enumerate(((own1_ref, out1_ref), (own2_ref, out2_ref))):
                    if rel % 2:
                        q = odd_steps.index(step)
                        if q >= 2:
                            to_sibling(a, WGRAD_ORDER[odd_steps[q - 2]] // 2).wait_send()
                        snds[a][q % 2] = accs[a][...].astype(BF16)
                        to_sibling(a, jj, q % 2).start()
                        continue
                    to_sibling(a, jj).wait_recv()
                    chip_sum = accs[a][...] + sibs[a][jj].astype(F32)
                    if rel == 6:
                        dsnds[a][...] = chip_sum.astype(BF16)
                        to_diagonal(a).start()
                    elif rel == 0:
                        own_ref[...] = chip_sum
                    else:
                        out_ref[0] = chip_sum.astype(BF16)
                    if step == N_DEV - 1:
                        for q in (2, 3):
                            to_sibling(a, WGRAD_ORDER[odd_steps[q]] // 2).wait_send()
                        to_diagonal(a).wait_recv()
                        to_diagonal(a).wait_send()

    assert WGRAD_ORDER[-1] == 0 and WGRAD_ORDER[-3:-1] == (2, 4)
    blk = pl.BlockSpec((tt, FF_BLK), lambda j, t: (t, relation(j)))
    early_block = lambda rel: jnp.where(rel < FC_HEAD, rel + FC_EARLY - FC_HEAD, rel - FC_HEAD)
    early = lambda j, t: (jnp.where(relation(j) < FC_EARLY, t, 0),
                          jnp.where(relation(j) < FC_EARLY, early_block(relation(j)), 0))
    late = lambda j, t: (jnp.where(relation(j) < FC_EARLY, 0, t), jnp.maximum(relation(j) - FC_EARLY, 0))
    chip = lambda j, t: (jnp.clip(j - 5, 0, 1), 0, 0)
    hbm = pl.BlockSpec(memory_space=pl.ANY)
    return pl.pallas_call(
        body, name="mlp_wgrad", grid=(N_DEV, nt),
        out_shape=(jax.ShapeDtypeStruct((D, FF_BLK), F32), jax.ShapeDtypeStruct((FF_BLK, D), F32),
                   jax.ShapeDtypeStruct((2, D, FF_BLK), BF16), jax.ShapeDtypeStruct((2, FF_BLK, D), BF16),
                   jax.ShapeDtypeStruct((D, FF_BLK), BF16), jax.ShapeDtypeStruct((FF_BLK, D), BF16)),
        in_specs=[pl.BlockSpec((tt, FF_BLK), early), pl.BlockSpec((tt, FF_BLK), late), blk, hbm, hbm],
        out_specs=(_full((D, FF_BLK)), _full((FF_BLK, D)),
                   pl.BlockSpec((1, D, FF_BLK), chip), pl.BlockSpec((1, FF_BLK, D), chip), hbm, hbm),
        scratch_shapes=[pltpu.VMEM((D, FF_BLK), F32), pltpu.VMEM((FF_BLK, D), F32),
                        pltpu.VMEM((2, D, FF_BLK), BF16), pltpu.VMEM((2, FF_BLK, D), BF16),
                        pltpu.VMEM((4, D, FF_BLK), BF16), pltpu.VMEM((4, FF_BLK, D), BF16),
                        pltpu.VMEM((D, FF_BLK), BF16), pltpu.VMEM((FF_BLK, D), BF16),
                        pltpu.SemaphoreType.DMA((10,)), pltpu.SemaphoreType.DMA((10,)),
                        pltpu.VMEM((t_len, D), BF16), pltpu.VMEM((t_len, D), BF16),
                        pltpu.SemaphoreType.DMA((2 * nt,))],
        compiler_params=pltpu.CompilerParams(dimension_semantics=("arbitrary", "arbitrary"),
                                             vmem_limit_bytes=VMEM_LIMIT),
    )(r_early, r_late, da, df, h2)


def _acc_rows(ref, row0, k, val):
    half = CHUNK // 2
    ref[row0:row0 + half, k * GROUP:(k + 1) * GROUP] += val[:half, :]
    ref[row0:row0 + half, D_A + k * GROUP:D_A + (k + 1) * GROUP] += val[half:, :]


def _attn_bwd(tt, dmix, x, z, cat, mod, n1pre, w_out, w_sp, bs_rows, ln_g, ln_b, w_pool, b_pool, pool_scale,
              red_fwd, red_bwd, chip_sums):
    t_len = z.shape[0]
    nt = t_len // tt
    hb = tt // HALO
    n_sums = len(chip_sums)

    def body(dmix_ref, x_ref, z_ref, zprev_ref, cat_ref, mod_ref, n1pre_ref, wout_ref, wsp_ref,
             bs_ref, lng_ref, lnb_ref, wp_ref, bp_ref, ps_ref, redf_ref, redb_ref, *rest):
        sum_out = rest[:n_sums]
        dz_ref, gwin_ref, gwout_ref, small_ref = rest[n_sums:n_sums + 4]
        sum_in = rest[n_sums + 4:2 * n_sums + 4]
        carry, acc_in, acc_out, dz_scr, bs_acc, send_sems, recv_sems = rest[2 * n_sums + 4:]
        s = pl.program_id(0)
        i = nt - 1 - s
        px, py, pc = _place()

        def chip_copy(a, r):
            return pltpu.make_async_remote_copy(
                src_ref=sum_out[a].at[r], dst_ref=sum_in[a].at[r],
                send_sem=send_sems.at[2 * a + r], recv_sem=recv_sems.at[2 * a + r],
                device_id=_peer(px, py, pc, 2 * (r + 1)), device_id_type=MESH)

        @pl.when(s == 0)
        def _():
            for a in range(n_sums):
                for r in range(2):
                    chip_copy(a, r).start()
            carry[...] = jnp.zeros_like(carry)
            acc_in[...] = jnp.zeros_like(acc_in)
            acc_out[...] = jnp.zeros_like(acc_out)
            bs_acc[...] = jnp.zeros_like(bs_acc)
            small_ref[...] = jnp.zeros_like(small_ref)
            small_ref[ROW_DMOD + 2:ROW_DMOD + 3, :] = redb_ref[3:4, :]
            small_ref[ROW_DMOD + 3:ROW_DMOD + 5, :] = redb_ref[0:2, :]
            small_ref[ROW_DMOD + 5:ROW_DMOD + 6, :] = redf_ref[0:1, :]
            small_ref[ROW_N1POST:ROW_N1POST + 1, :] = redb_ref[4:5, :]
            small_ref[ROW_N2PRE:ROW_N2PRE + 1, :] = redb_ref[2:3, :]
            small_ref[ROW_N2POST:ROW_N2POST + 1, :] = redf_ref[1:2, :]
            small_ref[ROW_LOSS:ROW_LOSS + 1, :] = redf_ref[2:3, :]

        dmixv = dmix_ref[...]
        dcat = _dot_nt(dmixv, wout_ref[...])
        acc_out[...] += _dot_tn(cat_ref[...], dmixv)

        z = z_ref[...]
        t_g, ga = _gelu_parts(z[:, :2 * D_A])
        u, vr = ga[:, :D_A], ga[:, D_A:]
        dv0 = vr - jnp.mean(vr, axis=-1, keepdims=True)
        rv = lax.rsqrt(jnp.mean(dv0 * dv0, axis=-1, keepdims=True) + EPS)
        vhat = dv0 * rv
        vb = (vhat * lng_ref[...] + lnb_ref[...]).astype(BF16)
        mask = _tril_mask()
        wc = [(wsp_ref[h] * mask).astype(BF16) for h in range(N_HEADS)]

        dya = dcat[:, :D_A]
        for h in range(N_HEADS):
            cols = slice(h * GROUP, (h + 1) * GROUP)
            bs_sum = jnp.zeros((CHUNK, GROUP), F32)
            ws_sum = jnp.zeros((CHUNK, CHUNK), F32)
            for ch in range(tt // CHUNK):
                rows = slice(ch * CHUNK, (ch + 1) * CHUNK)
                v_ch = vb[rows, cols]
                mixed = _dot(wc[h], v_ch) + bs_ref[:, cols]
                dy_ch = dya[rows, cols]
                dz_scr[rows, cols] = dy_ch * mixed
                dmixed = dy_ch * u[rows, cols]
                dmb = dmixed.astype(BF16)
                dz_scr[rows, D_A + h * GROUP:D_A + (h + 1) * GROUP] = _dot_tn(wc[h], dmb)
                bs_sum = bs_sum + dmixed
                ws_sum = ws_sum + _dot_nt(dmb, v_ch)
            _acc_rows(bs_acc, 0, h, bs_sum)
            _acc_rows(small_ref, ROW_WS, h, ws_sum)

        dvl = dz_scr[:, D_A:2 * D_A]
        dvhat = dvl * lng_ref[...]
        dvl_vhat = dvl * vhat
        dvr = rv * (dvhat - jnp.mean(dvhat, axis=-1, keepdims=True)
                    - vhat * jnp.mean(dvl_vhat * lng_ref[...], axis=-1, keepdims=True))
        small_ref[ROW_LN:ROW_LN + 1, 0:D_A] += _colsum(dvl_vhat)
        small_ref[ROW_LN:ROW_LN + 1, D_A:D] += _colsum(dvl)
        dga = jnp.concatenate([dz_scr[:, :D_A], dvr], axis=1)
        dza = dga * _gelu_grad(z[:, :2 * D_A], t_g)

        zb = z[:, 2 * D_A:]
        halo_prev = jnp.where(i == 0, 0.0, zprev_ref[...])
        diff = _pool_diff(zb, halo_prev, i * tt)
        dyb = dcat[:, D_A:]
        inv = _inv_counts(i * tt, tt)
        scaled, ddiffs = [], []
        for g in range(len(WINDOWS)):
            cols = slice(g * GROUP, (g + 1) * GROUP)
            db = diff[g].astype(BF16)
            wpg = wp_ref[g].astype(BF16)
            pre = _dot(db, wpg) + bp_ref[:, cols]
            small_ref[ROW_POOL:ROW_POOL + 1, cols] += _colsum(dyb[:, cols] * pre)
            dpre = dyb[:, cols] * ps_ref[:, cols]
            small_ref[ROW_POOL:ROW_POOL + 1, D_B + g * GROUP:D_B + (g + 1) * GROUP] += _colsum(dpre)
            dpb = dpre.astype(BF16)
            _acc_rows(small_ref, ROW_WP, g, _dot_tn(db, dpb))
            ddiff = _dot_nt(dpb, wpg)
            ddiffs.append(ddiff)
            scaled.append(ddiff * inv[g])
        scaled_all = jnp.concatenate(scaled, axis=1)
        ext = jnp.concatenate([scaled_all, carry[...]], axis=0)
        n_ext = tt + HALO
        s2 = ext + pltpu.roll(ext, n_ext - 1, 0)
        t4 = s2[:, GROUP:]
        s4 = t4 + pltpu.roll(t4, n_ext - 2, 0)
        t8 = s4[:, GROUP:]
        s8 = t8 + pltpu.roll(t8, n_ext - 4, 0)
        t16 = s8[:, GROUP:]
        s16 = t16 + pltpu.roll(t16, n_ext - 8, 0)
        back = [s2[:, :GROUP], s4[:, :GROUP], s8[:, :GROUP], s16]
        carry[...] = scaled_all[:HALO, :]
        dzb = jnp.concatenate([back[g][:tt, :] - ddiffs[g] for g in range(len(WINDOWS))], axis=1)

        dzv = jnp.concatenate([dza, dzb], axis=1).astype(BF16)
        dz_ref[...] = dzv
        xv = x_ref[...]
        h1 = (xv * _rstd(xv) * (n1pre_ref[...] * (1.0 + mod_ref[1:2, :])) + mod_ref[0:1, :]).astype(BF16)
        acc_in[...] += _dot_tn(dzv, h1)

        @pl.when(s == nt - 1)
        def _():
            gwin_ref[...] = acc_in[...].astype(BF16)
            gwout_ref[...] = acc_out[...].astype(BF16)
            bs = _unfold(bs_acc[...])
            for h in range(N_HEADS):
                small_ref[ROW_BS + h:ROW_BS + h + 1, 0:GROUP] = jnp.sum(
                    bs[:, h * GROUP:(h + 1) * GROUP].T, axis=0, keepdims=True)
            for a in range(n_sums):
                for r in range(2):
                    chip_copy(a, r).wait_recv()
                    chip_copy(a, r).wait_send()

    rev = lambda w: pl.BlockSpec((tt, w), lambda s: (nt - 1 - s, 0))
    zprev = pl.BlockSpec((HALO, D_B), lambda s: (jnp.maximum((nt - 1 - s) * hb - 1, 0), 2))
    hbm = pl.BlockSpec(memory_space=pl.ANY)
    outs = pl.pallas_call(
        body, name="attn_bwd", grid=(nt,),
        out_shape=tuple([jax.ShapeDtypeStruct((t_len, D_Z), BF16), jax.ShapeDtypeStruct((D_Z, D), BF16),
                         jax.ShapeDtypeStruct((D, D), BF16), jax.ShapeDtypeStruct((SMALL_ROWS, D), F32)]
                        + [jax.ShapeDtypeStruct(cs.shape, cs.dtype) for cs in chip_sums]),
        in_specs=[rev(D), rev(D), rev(D_Z), zprev, rev(D), _full((8, D)), _full((1, D)),
                  _resident((D, D)), _full((N_HEADS, CHUNK, CHUNK)), _full((CHUNK, D_A)),
                  _full((1, D_A)), _full((1, D_A)), _full((len(WINDOWS), GROUP, GROUP)), _full((1, D_B)),
                  _full((1, D_B)), _full((8, D)), _full((8, D))] + [_resident(cs.shape) for cs in chip_sums],
        out_specs=tuple([rev(D_Z), _resident((D_Z, D)), _resident((D, D)), _full((SMALL_ROWS, D))]
                        + [hbm] * n_sums),
        scratch_shapes=[pltpu.VMEM((HALO, D_B), F32), pltpu.VMEM((D_Z, D), F32), pltpu.VMEM((D, D), F32),
                        pltpu.VMEM((tt, 2 * D_A), F32), pltpu.VMEM((CHUNK // 2, D), F32),
                        pltpu.SemaphoreType.DMA((2 * n_sums,)), pltpu.SemaphoreType.DMA((2 * n_sums,))],
        compiler_params=pltpu.CompilerParams(dimension_semantics=("arbitrary",), vmem_limit_bytes=VMEM_LIMIT),
    )(dmix, x, z, z, cat, mod, n1pre, w_out, w_sp, bs_rows, ln_g, ln_b, w_pool, b_pool, pool_scale,
      red_fwd, red_bwd, *chip_sums)
    return outs[:4], outs[4:]


def _in_proj_bwd(tt, dz, dx1, x, mod, n1pre, w_in_t):
    t_len = x.shape[0]
    nt = t_len // tt

    def body(dz_ref, dx1_ref, x_ref, mod_ref, n1pre_ref, win_ref, gx_ref, sums_ref):
        i = pl.program_id(0)

        @pl.when(i == 0)
        def _():
            sums_ref[...] = jnp.zeros_like(sums_ref)

        dh1 = _dot(dz_ref[...], win_ref[...])
        xv = x_ref[...]
        r1 = _rstd(xv)
        xhat = xv * r1
        scale1 = mod_ref[1:2, :]
        pre1 = n1pre_ref[...]
        gain1 = pre1 * (1.0 + scale1)
        d_x, sum_h = _rms_bwd_gained(dh1, gain1, xhat, r1)
        gx_ref[...] = dx1_ref[...] + d_x
        sums_ref[ROW_DMOD:ROW_DMOD + 1, :] += _colsum(dh1)
        sums_ref[ROW_DMOD + 1:ROW_DMOD + 2, :] += pre1 * sum_h
        sums_ref[ROW_N1PRE:ROW_N1PRE + 1, :] += (1.0 + scale1) * sum_h

    tile = lambda w: pl.BlockSpec((tt, w), lambda i: (i, 0))
    return pl.pallas_call(
        body, name="in_proj_bwd", grid=(nt,),
        out_shape=(jax.ShapeDtypeStruct((t_len, D), F32), jax.ShapeDtypeStruct((16, D), F32)),
        in_specs=[tile(D_Z), tile(D), tile(D), _full((8, D)), _full((1, D)), _resident((D_Z, D))],
        out_specs=(tile(D), _full((16, D))),
        compiler_params=pltpu.CompilerParams(dimension_semantics=("arbitrary",), vmem_limit_bytes=VMEM_LIMIT),
    )(dz, dx1, x, mod, n1pre, w_in_t)


def _adam(w, g, m, v):
    m2 = ADAM_B1 * m + (1.0 - ADAM_B1) * g
    v2 = ADAM_B2 * v + (1.0 - ADAM_B2) * (g * g)
    m_hat = m2 / (1.0 - ADAM_B1 ** ADAM_STEP)
    v_hat = v2 / (1.0 - ADAM_B2 ** ADAM_STEP)
    delta = -ADAM_LR * (m_hat / (jnp.sqrt(v_hat) + ADAM_EPS) + ADAM_WD * w)
    return delta, m2, v2


def _adamw_fc(steps, fc):
    n_fc = len(fc)

    def body(*refs):
        ins, outs = refs[:6 * n_fc], refs[6 * n_fc:]
        for k in range(n_fc):
            w_ref, own_ref, arr_ref, diag_ref, m_ref, v_ref = ins[6 * k:6 * k + 6]
            g = ((own_ref[...] + arr_ref[0].astype(F32)) + arr_ref[1].astype(F32)) + diag_ref[...].astype(F32)
            outs[4 * k][...] = g
            outs[4 * k + 1][...], outs[4 * k + 2][...], outs[4 * k + 3][...] = _adam(
                w_ref[...], g, m_ref[...], v_ref[...])

    specs_in, specs_out, shapes, args = [], [], [], []
    for w, own, arrived, diagonal, m, v in fc:
        rows, cols = w.shape
        blk = pl.BlockSpec((rows // steps, cols), lambda i: (i, 0))
        specs_in += [blk, blk, pl.BlockSpec((2, rows // steps, cols), lambda i: (0, i, 0)), blk, blk, blk]
        specs_out += [blk] * 4
        shapes += [jax.ShapeDtypeStruct((rows, cols), F32)] * 4
        args += [w, own, arrived, diagonal, m, v]
    outs = pl.pallas_call(
        body, name="adamw_fc", grid=(steps,), out_shape=tuple(shapes), in_specs=specs_in, out_specs=tuple(specs_out),
        compiler_params=pltpu.CompilerParams(dimension_semantics=("arbitrary",), vmem_limit_bytes=VMEM_LIMIT),
    )(*args)
    return [outs[4 * k:4 * k + 4] for k in range(n_fc)]


def _adamw_ada(rb, w, sc, total, m, v):
    rows, cols = w.shape

    def body(w_ref, sc_ref, t_ref, m_ref, v_ref, g_ref, d_ref, m2_ref, v2_ref, dm):
        me = _index(_place())
        for dev in range(N_DEV):
            @pl.when((pl.program_id(0) == 0) & (me == dev))
            def _():
                for b in range(N_DEV):
                    for k in range(6):
                        lo, hi = max(cols * dev, D * k), min(cols * (dev + 1), D * (k + 1))
                        if lo < hi:
                            dm[b:b + 1, lo - cols * dev:hi - cols * dev] = t_ref[
                                _table_row(b) + k:_table_row(b) + k + 1, lo - D * k:hi - D * k]

        g = _dot_tn(sc_ref[...].astype(BF16), dm[...].astype(BF16))
        g_ref[...] = g
        d_ref[...], m2_ref[...], v2_ref[...] = _adam(w_ref[...], g, m_ref[...], v_ref[...])

    blk = pl.BlockSpec((rb, cols), lambda i: (i, 0))
    shp = jax.ShapeDtypeStruct((rows, cols), F32)
    return pl.pallas_call(
        body, name="adamw_ada", grid=(rows // rb,), out_shape=(shp, shp, shp, shp),
        in_specs=[blk, pl.BlockSpec((N_DEV, rb), lambda i: (0, i)), _full(total.shape), blk, blk],
        out_specs=(blk, blk, blk, blk),
        scratch_shapes=[pltpu.VMEM((N_DEV, cols), F32)],
        compiler_params=pltpu.CompilerParams(dimension_semantics=("arbitrary",)),
    )(w, sc, total, m, v)


def _unfold(acc_rows):
    return jnp.concatenate([acc_rows[:, :D_A], acc_rows[:, D_A:]], axis=0)


def _adamw_small(total, params, shards):
    n = len(params)
    flat = [a for p in params for a in p]

    def body(*refs):
        s_ref = refs[0]
        p_refs = refs[1:1 + 3 * n]
        s_refs = refs[1 + 3 * n:1 + 3 * n + 4 * len(shards)]
        loss_ref = refs[1 + 3 * n + 4 * len(shards)]
        o_refs = refs[2 + 3 * n + 4 * len(shards):2 + 3 * n + 4 * len(shards) + 4 * n]
        so_refs = refs[2 + 3 * n + 4 * len(shards) + 4 * n:]
        d_b_ada = s_ref[0:6, :]
        for b in range(1, N_DEV):
            d_b_ada = d_b_ada + s_ref[_table_row(b):_table_row(b) + 6, :]
        misc = lambda r: s_ref[PK_MISC + r - ROW_N1PRE:PK_MISC + r - ROW_N1PRE + 1, :]
        loss = jnp.sum(misc(ROW_LOSS), axis=-1, keepdims=True) * (0.5 / D)
        loss_ref[...] = loss
        mask = _tril_mask()
        ws = _unfold(s_ref[PK_WS:PK_WS + 64, :])
        wp = _unfold(s_ref[PK_WP:PK_WP + 64, :])
        grads = [
            d_b_ada,
            misc(ROW_N1PRE), misc(ROW_N1POST), misc(ROW_N2PRE), misc(ROW_N2POST),
            misc(ROW_LN)[:, :D_A], misc(ROW_LN)[:, D_A:],
            misc(ROW_POOL)[:, :D_B], misc(ROW_POOL)[:, D_B:],
            s_ref[PK_BS:PK_BS + N_HEADS, 0:GROUP],
            jnp.stack([ws[:, h * GROUP:(h + 1) * GROUP] * mask for h in range(N_HEADS)]),
            jnp.stack([wp[:, g * GROUP:(g + 1) * GROUP] for g in range(len(WINDOWS))]),
        ]
        for k in range(n):
            w_ref, m_ref, v_ref = p_refs[3 * k:3 * k + 3]
            g = grads[k]
            if k == 0:
                for j in range(6):
                    o_refs[0][j] = g[j:j + 1, :]
                    o_refs[1][j], o_refs[2][j], o_refs[3][j] = _adam(w_ref[j], g[j:j + 1, :], m_ref[j], v_ref[j])
                continue
            o_refs[4 * k][...] = g
            o_refs[4 * k + 1][...], o_refs[4 * k + 2][...], o_refs[4 * k + 3][...] = _adam(
                w_ref[...], g, m_ref[...], v_ref[...])
        for k in range(len(shards)):
            w_ref, g_ref, m_ref, v_ref = s_refs[4 * k:4 * k + 4]
            so_refs[3 * k][...], so_refs[3 * k + 1][...], so_refs[3 * k + 2][...] = _adam(
                w_ref[...], g_ref[...], m_ref[...], v_ref[...])

    vm = pl.BlockSpec(memory_space=pltpu.VMEM)
    out_shape = [jax.ShapeDtypeStruct((1, 1), F32)]
    for w, _, _ in params:
        out_shape += [jax.ShapeDtypeStruct(w.shape, F32)] * 4
    for w, _, _, _ in shards:
        out_shape += [jax.ShapeDtypeStruct(w.shape, F32)] * 3
    return pl.pallas_call(
        body, name="adamw_small", out_shape=tuple(out_shape),
        in_specs=[vm] * (1 + 3 * n + 4 * len(shards)), out_specs=tuple([vm] * len(out_shape)),
        compiler_params=pltpu.CompilerParams(vmem_limit_bytes=VMEM_LIMIT),
    )(total, *flat, *[a for s in shards for a in s])


TT_ATTN_FWD = 512
ATTN_LAG = 2
TT_MLP_FWD = 512
TT_MLP = 256
TT_WGRAD = 2048
TT_ATTN_BWD = 512
TT_IN_PROJ_BWD = 512


def kernel(x, c, w_ada, b_ada, norm1_pre, norm1_post, w_in, w_spatial, b_spatial, ln_v_gain, ln_v_bias, w_pool, b_pool, pool_scale, w_out, norm2_pre, norm2_post, w_fc1, w_fc2, loss_target, m_w_ada, m_b_ada, m_norm1_pre, m_norm1_post, m_w_in, m_w_spatial, m_b_spatial, m_ln_v_gain, m_ln_v_bias, m_w_pool, m_b_pool, m_pool_scale, m_w_out, m_norm2_pre, m_norm2_post, m_w_fc1, m_w_fc2, v_w_ada, v_b_ada, v_norm1_pre, v_norm1_post, v_w_in, v_w_spatial, v_b_spatial, v_ln_v_gain, v_ln_v_bias, v_w_pool, v_b_pool, v_pool_scale, v_w_out, v_norm2_pre, v_norm2_post, v_w_fc1, v_w_fc2):
    t_len = x.shape[1]
    ada_cols = w_ada.shape[1]
    tt = lambda want: min(want, t_len)

    x2 = x.reshape(t_len, D)
    tgt = loss_target.reshape(t_len, D)
    row = lambda a: a.reshape(1, -1)

    w_in_shard, w_out_shard, w1_shard, w2_shard = _cast_shards([w_in.T, w_out, w_fc1, w_fc2])

    bs_rows = jnp.repeat(b_spatial.T, GROUP, axis=1)
    attn_consts = (w_spatial, bs_rows, row(ln_v_gain), row(ln_v_bias), w_pool, row(b_pool), row(pool_scale))

    (z, cat, mix, x1, h2, r_begun, f_head, mod, sc), (w1_early, w2_early), w_out_all, w_in_t = _attn_fwd(
        tt(TT_ATTN_FWD), x2, c.reshape(1, D), w_ada, b_ada.reshape(N_DEV, 1, ada_cols), row(norm1_pre),
        row(norm1_post),
        w_in_shard, w_out_shard, *attn_consts, (w1_shard, w2_shard), row(norm2_pre))
    (r_early, f_early), (w1_late, w2_late) = _mlp_fwd_early(
        tt(TT_MLP_FWD), r_begun, h2, f_head, w1_early, w2_early)
    r_late, df, da, dmix, dx1, red_fwd, red_bwd = _mlp_late_bwd(
        tt(TT_MLP), r_early, x1, h2, f_early, tgt, mix, mod, row(norm2_pre), row(norm2_post), row(norm1_post),
        w1_early, w2_early, w1_late, w2_late)
    own_w1, own_w2, sums_w1, sums_w2, diag_w1, diag_w2 = _mlp_wgrad(tt(TT_WGRAD), r_early, r_late, da, df, h2)
    (dz, p_in, p_out, small), (arr_w1, arr_w2) = _attn_bwd(
        tt(TT_ATTN_BWD), dmix, x2, z, cat, mod, row(norm1_pre), w_out_all, *attn_consts, red_fwd, red_bwd,
        [sums_w1, sums_w2])
    grad_x, small_head = _in_proj_bwd(tt(TT_IN_PROJ_BWD), dz, dx1, x2, mod, row(norm1_pre), w_in_t)
    (grad_w1, d_w1, m_w1, v_w1), (grad_w2, d_w2, m_w2, v_w2) = _adamw_fc(
        4, [(w_fc1, own_w1, arr_w1, diag_w1, m_w_fc1, v_w_fc1), (w_fc2, own_w2, arr_w2, diag_w2, m_w_fc2, v_w_fc2)])
    grad_in_t, grad_out, total = _tail_comm(
        [p_in.reshape(N_DEV, D_Z // N_DEV, D), p_out.reshape(N_DEV, D // N_DEV, D)], small, small_head, 64)

    grad_ada, d_ada, m_ada, v_ada = _adamw_ada(256, w_ada, sc, total, m_w_ada, v_w_ada)

    six = lambda a: a.reshape(6, 1, D)
    small_params = [
        (six(b_ada), six(m_b_ada), six(v_b_ada)),
        (row(norm1_pre), row(m_norm1_pre), row(v_norm1_pre)),
        (row(norm1_post), row(m_norm1_post), row(v_norm1_post)),
        (row(norm2_pre), row(m_norm2_pre), row(v_norm2_pre)),
        (row(norm2_post), row(m_norm2_post), row(v_norm2_post)),
        (row(ln_v_gain), row(m_ln_v_gain), row(v_ln_v_gain)),
        (row(ln_v_bias), row(m_ln_v_bias), row(v_ln_v_bias)),
        (row(pool_scale), row(m_pool_scale), row(v_pool_scale)),
        (row(b_pool), row(m_b_pool), row(v_b_pool)),
        (b_spatial, m_b_spatial, v_b_spatial),
        (w_spatial, m_w_spatial, v_w_spatial),
        (w_pool, m_w_pool, v_w_pool),
    ]
    outs = _adamw_small(total, small_params, [(w_out, grad_out, m_w_out, v_w_out),
                                              (w_in.T, grad_in_t, m_w_in.T, v_w_in.T)])
    d_out, m_out, v_out, d_in_t, m_in_t, v_in_t = outs[1 + 4 * len(small_params):]
    loss = outs[0].reshape(())
    names = ["b_ada", "norm1_pre", "norm1_post", "norm2_pre", "norm2_post", "ln_v_gain", "ln_v_bias", "pool_scale",
             "b_pool", "b_spatial", "w_spatial", "w_pool"]
    shapes = dict(b_ada=b_ada.shape, norm1_pre=norm1_pre.shape, norm1_post=norm1_post.shape,
                  norm2_pre=norm2_pre.shape, norm2_post=norm2_post.shape, ln_v_gain=ln_v_gain.shape,
                  ln_v_bias=ln_v_bias.shape, pool_scale=pool_scale.shape, b_pool=b_pool.shape,
                  b_spatial=b_spatial.shape, w_spatial=w_spatial.shape, w_pool=w_pool.shape)
    res = {}
    for k, nm in enumerate(names):
        res[nm] = tuple(o.reshape(shapes[nm]) for o in outs[1 + 4 * k:5 + 4 * k])
    res["w_ada"] = (grad_ada, d_ada, m_ada, v_ada)
    res["w_in"] = (grad_in_t.T, d_in_t.T, m_in_t.T, v_in_t.T)
    res["w_out"] = (grad_out, d_out, m_out, v_out)
    res["w_fc1"] = (grad_w1, d_w1, m_w1, v_w1)
    res["w_fc2"] = (grad_w2, d_w2, m_w2, v_w2)

    order = ["w_ada", "b_ada", "norm1_pre", "norm1_post", "w_in", "w_spatial", "b_spatial", "ln_v_gain", "ln_v_bias",
             "w_pool", "b_pool", "pool_scale", "w_out", "norm2_pre", "norm2_post", "w_fc1", "w_fc2"]
    return (loss, grad_x.reshape(x.shape),
            *[res[nm][0] for nm in order], *[res[nm][1] for nm in order],
            *[res[nm][2] for nm in order], *[res[nm][3] for nm in order])
```

```python
import functools

import jax
import jax.numpy as jnp
from jax import lax
from jax.experimental import pallas as pl
from jax.experimental.pallas import tpu as pltpu

F32 = jnp.float32
BF16 = jnp.bfloat16
MESH = pl.DeviceIdType.MESH

N_DEV = 8
D = 1024
D_A = 512
D_B = 512
D_Z = 2 * D_A + D_B
N_HEADS = 4
CHUNK = 128
WINDOWS = (2, 4, 8, 16)
GROUP = 128
D_FF = 4096
FF_BLK = D_FF // N_DEV
HALO = 16
EPS = 1e-6
VMEM_LIMIT = 60 * 1024 * 1024

ADAM_LR = 0.001
ADAM_B1 = 0.9
ADAM_B2 = 0.999
ADAM_EPS = 1e-08
ADAM_WD = 0.01
ADAM_STEP = 10

ROW_DMOD = 0
ROW_N1PRE, ROW_N1POST, ROW_N2PRE, ROW_N2POST = 8, 9, 10, 11
ROW_LN = 12
ROW_POOL = 13
ROW_LOSS = 14
ROW_BS = 16
ROW_WS = 24
ROW_WP = 88
SMALL_ROWS = 152
PACK_FINE = 40
PACK_HALF = PACK_FINE + 64
PACK_ROWS = 2 * PACK_HALF
PK_WS = PACK_FINE
PK_TABLE_B = PACK_HALF
PK_MISC = PK_TABLE_B + 24
PK_BS = PK_MISC + 8
PK_WP = PK_BS + 8


def _table_row(b):
    if isinstance(b, int):
        return 8 * b if 8 * b < PACK_FINE else 8 * b + PK_TABLE_B - PACK_FINE
    return 8 * b + jnp.where(8 * b < PACK_FINE, 0, PK_TABLE_B - PACK_FINE)


def _dot(a, b):
    return jnp.dot(a, b, preferred_element_type=F32)


def _dot_nt(a, b):
    return lax.dot_general(a, b, (((1,), (1,)), ((), ())), preferred_element_type=F32)


def _dot_tn(a, b):
    return lax.dot_general(a, b, (((0,), (0,)), ((), ())), preferred_element_type=F32)


def _rstd(v):
    return lax.rsqrt(jnp.mean(v * v, axis=-1, keepdims=True) + EPS)


def _rms_bwd(d_hat, hat, rstd):
    return rstd * (d_hat - hat * jnp.mean(d_hat * hat, axis=-1, keepdims=True))


def _rms_bwd_gained(g, gain, hat, rstd):
    g_hat = g * hat
    d_v = rstd * (g * gain - hat * jnp.mean(g_hat * gain, axis=-1, keepdims=True))
    return d_v, _colsum(g_hat)


_K0 = 0.7978845608028654
_K1 = 0.044715


def _gelu_parts(v):
    t = jnp.tanh(v * (_K0 + (_K0 * _K1) * (v * v)))
    return t, v * (0.5 + 0.5 * t)


def _gelu_grad(v, t):
    return (0.5 + 0.5 * t) + (0.5 * v) * (1.0 - t * t) * (_K0 + (3.0 * _K0 * _K1) * (v * v))


def _colsum(v):
    return jnp.sum(v, axis=0, keepdims=True)


def _full(shape):
    n = len(shape)
    return pl.BlockSpec(shape, lambda *_: (0,) * n)


def _resident(shape):
    n = len(shape)
    return pl.BlockSpec(shape, lambda *_: (0,) * n, pipeline_mode=pl.Buffered(1))


def _place():
    x, y, c = lax.axis_index("x"), lax.axis_index("y"), lax.axis_index("c")
    return x, y, c


def _flip(v, bit):
    return 1 - v if bit else v


def _peer(x, y, c, k):
    return (_flip(x, (k >> 2) & 1), _flip(y, (k >> 1) & 1), _flip(c, k & 1))


def _index(p):
    return 4 * p[0] + 2 * p[1] + p[2]


def _cast_shards(shards):
    def body(*refs):
        for src, dst in zip(refs[:len(shards)], refs[len(shards):]):
            dst[...] = src[...].astype(BF16)

    vm = pl.BlockSpec(memory_space=pltpu.VMEM)
    return pl.pallas_call(
        body, name="cast_shards", out_shape=tuple(jax.ShapeDtypeStruct(s.shape, BF16) for s in shards),
        in_specs=[vm] * len(shards), out_specs=tuple([vm] * len(shards)),
    )(*shards)


FC_EARLY = 6
FC_HEAD = 2
R_HEAD_COLS = (FC_EARLY - FC_HEAD) * FF_BLK
WGRAD_ORDER = (7, 6, 1, 3, 5, 2, 4, 0)


def _early_col(j):
    return R_HEAD_COLS + j * FF_BLK if j < FC_HEAD else (j - FC_HEAD) * FF_BLK


class _Copies:
    def __init__(self, entries, send_sems, recv_sems):
        self.place = _place()
        self.entries, self.send_sems, self.recv_sems = entries, send_sems, recv_sems

    def _copy(self, i, arrival=False):
        src, dst, rel = self.entries[i]
        return pltpu.make_async_remote_copy(
            src_ref=dst if arrival else src, dst_ref=dst, send_sem=self.send_sems.at[i],
            recv_sem=self.recv_sems.at[i], device_id=_peer(*self.place, rel), device_id_type=MESH)

    def start(self, *which):
        for i in which:
            self._copy(i).start()

    def wait_recv(self, *which):
        for i in which:
            self._copy(i, arrival=True).wait_recv()

    def wait_send(self, *which):
        for i in which:
            self._copy(i).wait_send()


def _tail_comm(parts, small, row_chunk, tt, dz, dx1, x, mod, n1pre, w_in_t):
    n = len(parts)
    t_len = x.shape[0]
    nt = t_len // tt

    def body(*refs):
        p_refs, small_ref = refs[:n], refs[n]
        dz_ref, dx1_ref, x_ref, mod_ref, n1pre_ref, win_ref = refs[n + 1:n + 7]
        outs = refs[n + 7:]
        g_refs, total_ref, gx_ref = outs[:n], outs[n], outs[n + 1]
        scr = outs[n + 2:]
        from_sib = scr[0:n]
        chip_out = scr[n:2 * n]
        chip_in = scr[2 * n:3 * n]
        pack, pack_sib, fine, bulk, total_scr, head = scr[3 * n:3 * n + 6]
        send_a, recv_a, send_b, recv_b, send_s, recv_s = scr[3 * n + 6:]
        step = pl.program_id(0)
        x, y, c = _place()
        me = _index((x, y, c))
        sibling = (x, y, 1 - c)
        my_chip = 2 * x + y
        others = [(1 - x, y), (x, 1 - y), (1 - x, 1 - y)]
        my_half = pl.ds(pl.multiple_of(PACK_HALF * c, 8), PACK_HALF)

        def pack_to_sibling():
            return pltpu.make_async_remote_copy(
                src_ref=pack, dst_ref=pack_sib, send_sem=send_s.at[0], recv_sem=recv_s.at[0],
                device_id=sibling, device_id_type=MESH)

        def half_to_chip(r, part):
            buf = (fine, bulk)[part]
            return pltpu.make_async_remote_copy(
                src_ref=buf.at[my_chip], dst_ref=buf.at[my_chip],
                send_sem=send_s.at[1 + 3 * part + r], recv_sem=recv_s.at[1 + 3 * part + r],
                device_id=(*others[r], c), device_id_type=MESH)

        def half_from_chip(r, part):
            k = 2 * others[r][0] + others[r][1]
            buf = (fine, bulk)[part]
            return pltpu.make_async_remote_copy(
                src_ref=buf.at[k], dst_ref=buf.at[k],
                send_sem=send_s.at[1 + 3 * part + r], recv_sem=recv_s.at[1 + 3 * part + r],
                device_id=(*others[r], c), device_id_type=MESH)

        def total_to_sibling():
            return pltpu.make_async_remote_copy(
                src_ref=total_scr.at[my_half], dst_ref=total_scr.at[my_half],
                send_sem=send_s.at[7], recv_sem=recv_s.at[7], device_id=sibling, device_id_type=MESH)

        def total_from_sibling():
            sib_half = pl.ds(pl.multiple_of(PACK_HALF * (1 - c), 8), PACK_HALF)
            return pltpu.make_async_remote_copy(
                src_ref=total_scr.at[sib_half], dst_ref=total_scr.at[sib_half],
                send_sem=send_s.at[7], recv_sem=recv_s.at[7], device_id=sibling, device_id_type=MESH)

        def to_sibling(a, k):
            return pltpu.make_async_remote_copy(
                src_ref=p_refs[a].at[2 * k + (1 - c)], dst_ref=from_sib[a].at[k],
                send_sem=send_a.at[a], recv_sem=recv_a.at[a], device_id=sibling, device_id_type=MESH)

        def all_from_sibling(a):
            return pltpu.make_async_remote_copy(
                src_ref=from_sib[a], dst_ref=from_sib[a], send_sem=send_a.at[a], recv_sem=recv_a.at[a],
                device_id=sibling, device_id_type=MESH)

        def to_chip(a, r):
            return pltpu.make_async_remote_copy(
                src_ref=chip_out[a].at[r], dst_ref=chip_in[a].at[r],
                send_sem=send_b.at[3 * a + r], recv_sem=recv_b.at[3 * a + r],
                device_id=(*others[r], c), device_id_type=MESH)

        @pl.when(step == 0)
        def _():
            head[...] = jnp.zeros_like(head)
            for a in range(n):
                for k in range(4):
                    to_sibling(a, k).start()

        @pl.when(step == min(1, nt - 1))
        def _():
            for a in range(n):
                all_from_sibling(a).wait_recv()
                rows = p_refs[a].shape[1]
                for r in range(3):
                    k = 2 * others[r][0] + others[r][1]
                    for s in range(0, rows, row_chunk):
                        sl = pl.ds(s, row_chunk)
                        chip_out[a][r, sl, :] = (p_refs[a][2 * k + c, sl, :].astype(F32)
                                                 + from_sib[a][k, sl, :].astype(F32)).astype(BF16)
                    to_chip(a, r).start()
                for s in range(0, rows, row_chunk):
                    sl = pl.ds(s, row_chunk)
                    g_refs[a][sl, :] = (p_refs[a][2 * my_chip + c, sl, :].astype(F32)
                                        + from_sib[a][my_chip, sl, :].astype(F32))

        dh1 = _dot(dz_ref[...], win_ref[...])
        xv = x_ref[...]
        r1 = _rstd(xv)
        xhat = xv * r1
        scale1 = mod_ref[1:2, :]
        pre1 = n1pre_ref[...]
        d_x, sum_h = _rms_bwd_gained(dh1, pre1 * (1.0 + scale1), xhat, r1)
        gx_ref[...] = dx1_ref[...] + d_x
        head[ROW_DMOD:ROW_DMOD + 1, :] += _colsum(dh1)
        head[ROW_DMOD + 1:ROW_DMOD + 2, :] += pre1 * sum_h
        head[ROW_N1PRE:ROW_N1PRE + 1, :] += (1.0 + scale1) * sum_h

        @pl.when(step == nt - 1)
        def _():
            pack[0:PACK_FINE, :] = jnp.zeros((PACK_FINE, D), F32)
            pack[PK_TABLE_B:PK_MISC, :] = jnp.zeros((PK_MISC - PK_TABLE_B, D), F32)
            pack[pl.ds(pl.multiple_of(_table_row(me), 8), 8), :] = small_ref[0:8, :] + head[0:8, :]
            pack[PK_WS:PK_WS + 64, :] = small_ref[ROW_WS:ROW_WS + 64, :]
            pack[PK_MISC:PK_MISC + 8, :] = small_ref[ROW_N1PRE:ROW_N1PRE + 8, :] + head[8:16, :]
            pack[PK_BS:PK_BS + 8, :] = small_ref[ROW_BS:ROW_BS + 8, :]
            pack[PK_WP:PK_WP + 64, :] = small_ref[ROW_WP:ROW_WP + 64, :]
            pack_to_sibling().start()
            pack_to_sibling().wait_recv()
            chip_sum = pack[my_half, :] + pack_sib[my_half, :]
            fine[my_chip] = chip_sum[:PACK_FINE, :]
            bulk[my_chip] = chip_sum[PACK_FINE:, :].astype(BF16)
            for r in range(3):
                half_to_chip(r, 0).start()
                half_to_chip(r, 1).start()
            for a in range(n):
                rows = p_refs[a].shape[1]
                for r in range(3):
                    to_chip(a, r).wait_recv()
                    for s in range(0, rows, row_chunk):
                        sl = pl.ds(s, row_chunk)
                        g_refs[a][sl, :] = g_refs[a][sl, :] + chip_in[a][r, sl, :].astype(F32)
            for r in range(3):
                half_from_chip(r, 0).wait_recv()
                half_from_chip(r, 1).wait_recv()
            half_start = pl.multiple_of(PACK_HALF * c, 8)
            total_scr[pl.ds(half_start, PACK_FINE), :] = ((fine[0] + fine[1]) + fine[2]) + fine[3]
            total_scr[pl.ds(half_start + PACK_FINE, PACK_HALF - PACK_FINE), :] = (
                (bulk[0].astype(F32) + bulk[1].astype(F32)) + bulk[2].astype(F32)) + bulk[3].astype(F32)
            total_to_sibling().start()
            total_from_sibling().wait_recv()
            total_ref[...] = total_scr[...]
            for a in range(n):
                all_from_sibling(a).wait_send()
                for r in range(3):
                    to_chip(a, r).wait_send()
            pack_to_sibling().wait_send()
            for r in range(3):
                half_to_chip(r, 0).wait_send()
                half_to_chip(r, 1).wait_send()
            total_to_sibling().wait_send()

    tile = lambda w: pl.BlockSpec((tt, w), lambda i: (i, 0))
    return pl.pallas_call(
        body, name="tail_comm", grid=(nt,),
        out_shape=tuple([jax.ShapeDtypeStruct(p.shape[1:], F32) for p in parts]
                        + [jax.ShapeDtypeStruct((PACK_ROWS, D), F32), jax.ShapeDtypeStruct((t_len, D), F32)]),
        in_specs=([_resident(p.shape) for p in parts] + [_resident(small.shape)]
                  + [tile(D_Z), tile(D), tile(D), _full((8, D)), _full((1, D)), _resident((D_Z, D))]),
        out_specs=tuple([_full(p.shape[1:]) for p in parts] + [_full((PACK_ROWS, D)), tile(D)]),
        scratch_shapes=(
            [pltpu.VMEM((4,) + p.shape[1:], BF16) for p in parts]
            + [pltpu.VMEM((3,) + p.shape[1:], BF16) for p in parts]
            + [pltpu.VMEM((3,) + p.shape[1:], BF16) for p in parts]
            + [pltpu.VMEM((PACK_ROWS, D), F32), pltpu.VMEM((PACK_ROWS, D), F32),
               pltpu.VMEM((4, PACK_FINE, D), F32), pltpu.VMEM((4, PACK_HALF - PACK_FINE, D), BF16),
               pltpu.VMEM((PACK_ROWS, D), F32), pltpu.VMEM((16, D), F32)]
            + [pltpu.SemaphoreType.DMA((n,)), pltpu.SemaphoreType.DMA((n,)),
               pltpu.SemaphoreType.DMA((3 * n,)), pltpu.SemaphoreType.DMA((3 * n,)),
               pltpu.SemaphoreType.DMA((8,)), pltpu.SemaphoreType.DMA((8,))]),
        compiler_params=pltpu.CompilerParams(dimension_semantics=("arbitrary",), vmem_limit_bytes=VMEM_LIMIT),
    )(*parts, small, dz, dx1, x, mod, n1pre, w_in_t)


def _tril_mask():
    row = lax.broadcasted_iota(jnp.int32, (CHUNK, CHUNK), 0)
    col = lax.broadcasted_iota(jnp.int32, (CHUNK, CHUNK), 1)
    return (col <= row).astype(F32)


def _window_sums(ext):
    s2 = ext + pltpu.roll(ext, 1, 0)
    t4 = s2[:, GROUP:]
    s4 = t4 + pltpu.roll(t4, 2, 0)
    t8 = s4[:, GROUP:]
    s8 = t8 + pltpu.roll(t8, 4, 0)
    t16 = s8[:, GROUP:]
    s16 = t16 + pltpu.roll(t16, 8, 0)
    return [s2[:, :GROUP], s4[:, :GROUP], s8[:, :GROUP], s16]


def _inv_counts(first_pos, rows):
    pos = first_pos + lax.broadcasted_iota(jnp.int32, (rows, 1), 0)
    return [1.0 / jnp.minimum(pos + 1, w).astype(F32) for w in WINDOWS]


def _pool_diff(zb, halo, first_pos):
    tt = zb.shape[0]
    sums = _window_sums(jnp.concatenate([halo, zb], axis=0))
    inv = _inv_counts(first_pos, tt)
    return [sums[g][HALO:, :] * inv[g] - zb[:, g * GROUP:(g + 1) * GROUP] for g in range(len(WINDOWS))]


def _row_blocks(scr, rows, place):
    def block(rel):
        start = pl.multiple_of(rows * _index(_peer(*place, rel)), rows)
        return scr.at[pl.ds(start, rows), :]

    def entries(shard_ref):
        return ([(shard_ref, block(0), rel) for rel in (1, 2, 4, 6)]
                + [(block(rel), block(rel), 1) for rel in (2, 4, 6)])
    return block, entries


def _attn_fwd(tt, x, c_row, w_ada, b_pieces, n1pre, n1post, w_in_shard, w_out_shard, w_sp, bs_rows, ln_g, ln_b,
              w_pool, b_pool, pool_scale, fc_shards, n2pre):
    t_len = x.shape[0]
    nt = t_len // tt
    ncol = w_ada.shape[1]

    def body(x_ref, xb_ref, c_ref, wada_ref, b_ref, n1pre_ref, n1post_ref, wi_ref, wo_ref, wsp_ref, bs_ref,
             lng_ref, lnb_ref, wp_ref, bp_ref, ps_ref, w1_ref, w2_ref, n2pre_ref,
             z_ref, cat_ref, mix_ref, x1_ref, h2_ref, r_ref, f_ref, mod_out, sc_out, e1_ref, e2_ref, wout_ref,
             win_out, carry, land1, land2, sib1, sib2, cat_keep, wout_scr, win_ref, mod_ref, cg, mg, part,
             send_sems, recv_sems, local_sems, wo_send, wo_recv, wi_send, wi_recv, ada_send, ada_recv):
        i = pl.program_id(0)
        place = px, py, pc = _place()
        me = _index(place)
        wo_block, wo_entries = _row_blocks(wout_scr, w_out_shard.shape[0], place)
        wi_block, wi_entries = _row_blocks(win_ref, w_in_shard.shape[0], place)
        wo_copies = _Copies(wo_entries(wo_ref), wo_send, wo_recv)
        wi_copies = _Copies(wi_entries(wi_ref), wi_send, wi_recv)
        wo_keep = pltpu.make_async_copy(wout_scr, wout_ref, local_sems.at[8])
        wi_keep = pltpu.make_async_copy(win_ref, win_out, local_sems.at[9])
        ada = _Copies([(cg.at[me], cg.at[me], k) for k in range(1, N_DEV)]
                      + [(part, mg.at[me], k) for k in range(1, N_DEV)], ada_send, ada_recv)
        copies = _Copies(
            [(w1_ref, sib1, 1), (w2_ref, sib2, 1),
             (w1_ref, land1.at[0], 2), (w2_ref, land2.at[0], 2),
             (w1_ref, land1.at[1], 4), (w2_ref, land2.at[1], 4),
             (land1.at[0], e1_ref.at[3], 1), (land2.at[0], e2_ref.at[3], 1),
             (land1.at[1], e1_ref.at[5], 1), (land2.at[1], e2_ref.at[5], 1)],
            send_sems, recv_sems)
        keep = [pltpu.make_async_copy(w1_ref, e1_ref.at[0], local_sems.at[0]),
                pltpu.make_async_copy(w2_ref, e2_ref.at[0], local_sems.at[1]),
                pltpu.make_async_copy(land1.at[0], e1_ref.at[2], local_sems.at[2]),
                pltpu.make_async_copy(land1.at[1], e1_ref.at[4], local_sems.at[3]),
                pltpu.make_async_copy(land2.at[0], e2_ref.at[2], local_sems.at[4]),
                pltpu.make_async_copy(land2.at[1], e2_ref.at[4], local_sems.at[5]),
                pltpu.make_async_copy(sib1, e1_ref.at[1], local_sems.at[6]),
                pltpu.make_async_copy(sib2, e2_ref.at[1], local_sems.at[7])]

        @pl.when(i == 0)
        def _():
            cg[me] = jnp.broadcast_to(c_ref[...], (8, D))
            ada.start(*range(N_DEV - 1))
            wi_copies.start(0, 1, 2, 3)
            wi_rows, wo_rows = w_in_shard.shape[0], w_out_shard.shape[0]
            win_ref[pl.ds(pl.multiple_of(wi_rows * me, wi_rows), wi_rows), :] = wi_ref[...]
            wout_scr[pl.ds(pl.multiple_of(wo_rows * me, wo_rows), wo_rows), :] = wo_ref[...]
            carry[...] = jnp.zeros_like(carry)

            ada.wait_recv(*range(N_DEV - 1))
            c_all = jnp.concatenate([cg[j, 0:1, :] for j in range(N_DEV)], axis=0)
            sc = c_all * jax.nn.sigmoid(c_all)
            sc_out[...] = sc
            part[...] = _dot(sc.astype(BF16), wada_ref[...].astype(BF16)) + b_ref[me]
            ada.start(*range(N_DEV - 1, 2 * (N_DEV - 1)))
            wo_copies.start(0, 1, 2, 3)
            copies.start(0, 1, 2, 4, 3, 5)
            keep[0].start()
            keep[1].start()
            mg[me] = part[...]
            ada.wait_recv(*range(N_DEV - 1, 2 * (N_DEV - 1)))
            mod_ref[...] = jnp.zeros_like(mod_ref)
            for j in range(N_DEV):
                for m in range(6):
                    lo, hi = max(ncol * j, D * m), min(ncol * (j + 1), D * (m + 1))
                    if lo < hi:
                        mod_ref[m:m + 1, lo - D * m:hi - D * m] = mg[j, pl.ds(me, 1), lo - ncol * j:hi - ncol * j]
            mod_out[...] = mod_ref[...]

            wi_copies.wait_recv(1, 2, 3)
            wi_copies.start(4, 5, 6)
            wi_copies.wait_recv(0, 4, 5, 6)
            wi_keep.start()

        @pl.when(i == nt // 2 + ATTN_LAG)
        def _():
            copies.wait_recv(2, 4)
            copies.start(6, 8)
            keep[2].start()
            keep[3].start()

        @pl.when(i == nt - 1 + ATTN_LAG)
        def _():
            copies.wait_recv(3, 5)
            copies.start(7, 9)
            keep[4].start()
            keep[5].start()

        shift1, scale1, gate1 = mod_ref[0:1, :], mod_ref[1:2, :], mod_ref[2:3, :]

        @pl.when(i < nt)
        def _():
            xv = x_ref[...]
            h1 = (xv * _rstd(xv)) * (n1pre_ref[...] * (1.0 + scale1)) + shift1
            z = _dot_nt(h1.astype(BF16), win_ref[...])
            z_ref[...] = z

            _, ga = _gelu_parts(z[:, :2 * D_A])
            u, vr = ga[:, :D_A], ga[:, D_A:]
            dv = vr - jnp.mean(vr, axis=-1, keepdims=True)
            v = (dv * lax.rsqrt(jnp.mean(dv * dv, axis=-1, keepdims=True) + EPS)) * lng_ref[...] + lnb_ref[...]
            vb = v.astype(BF16)
            mask = _tril_mask()
            wc = [(wsp_ref[h] * mask).astype(BF16) for h in range(N_HEADS)]
            for ch in range(tt // CHUNK):
                rows = slice(ch * CHUNK, (ch + 1) * CHUNK)
                for h in range(N_HEADS):
                    cols = slice(h * GROUP, (h + 1) * GROUP)
                    mixed = _dot(wc[h], vb[rows, cols]) + bs_ref[:, cols]
                    cat_ref[rows, cols] = (u[rows, cols] * mixed).astype(BF16)

            zb = z[:, 2 * D_A:]
            diff = _pool_diff(zb, carry[...], i * tt)
            carry[...] = zb[tt - HALO:, :]
            for g in range(len(WINDOWS)):
                cols = slice(g * GROUP, (g + 1) * GROUP)
                pre = _dot(diff[g].astype(BF16), wp_ref[g].astype(BF16)) + bp_ref[:, cols]
                cat_ref[:, D_A + g * GROUP:D_A + (g + 1) * GROUP] = (pre * ps_ref[:, cols]).astype(BF16)
            cat_keep[i % (ATTN_LAG + 1)] = cat_ref[...]

        @pl.when(i == 0)
        def _():
            copies.wait_recv(0, 1)
            keep[6].start()
            keep[7].start()

        @pl.when(i == 1)
        def _():
            wo_copies.wait_recv(1, 2, 3)
            wo_copies.start(4, 5, 6)

        @pl.when(i == ATTN_LAG)
        def _():
            wo_copies.wait_recv(0, 4, 5, 6)
            wo_keep.start()

        @pl.when(i >= ATTN_LAG)
        def _():
            xv = xb_ref[...]
            mix = _dot(cat_keep[(i - ATTN_LAG) % (ATTN_LAG + 1)], wout_scr[...])
            mix_ref[...] = mix
            x1v = xv + (mix * _rstd(mix)) * (gate1 * n1post_ref[...])
            x1_ref[...] = x1v
            shift2, scale2 = mod_ref[3:4, :], mod_ref[4:5, :]
            h2 = ((x1v * _rstd(x1v)) * (n2pre_ref[...] * (1.0 + scale2)) + shift2).astype(BF16)
            h2_ref[...] = h2
            for j, (w1, w2) in enumerate(((w1_ref, w2_ref), (sib1, sib2))):
                ra = jnp.maximum(_dot(h2, w1[...]), 0.0)
                r = (ra * ra).astype(BF16)
                r_ref[:, j * FF_BLK:(j + 1) * FF_BLK] = r
                if j == 0:
                    f_ref[...] = _dot(r, w2[...])
                else:
                    f_ref[...] += _dot(r, w2[...])

        @pl.when(i == nt - 1 + ATTN_LAG)
        def _():
            copies.wait_recv(6, 7, 8, 9)
            copies.wait_send(*range(10))
            wo_copies.wait_send(*range(7))
            wi_copies.wait_send(*range(7))
            ada.wait_send(*range(2 * (N_DEV - 1)))
            for cp in keep:
                cp.wait()
            wo_keep.wait()
            wi_keep.wait()

    first = lambda w: pl.BlockSpec((tt, w), lambda i: (jnp.minimum(i, nt - 1), 0))
    second = lambda w: pl.BlockSpec((tt, w), lambda i: (jnp.maximum(i - ATTN_LAG, 0), 0))
    r_head = pl.BlockSpec((tt, FC_HEAD * FF_BLK),
                          lambda i: (jnp.maximum(i - ATTN_LAG, 0), R_HEAD_COLS // (FC_HEAD * FF_BLK)))
    hbm = pl.BlockSpec(memory_space=pl.ANY)
    outs = pl.pallas_call(
        body, name="attn_fwd", grid=(nt + ATTN_LAG,),
        out_shape=tuple([jax.ShapeDtypeStruct((t_len, D_Z), F32), jax.ShapeDtypeStruct((t_len, D), BF16),
                         jax.ShapeDtypeStruct((t_len, D), F32), jax.ShapeDtypeStruct((t_len, D), F32),
                         jax.ShapeDtypeStruct((t_len, D), BF16),
                         jax.ShapeDtypeStruct((t_len, FC_EARLY * FF_BLK), BF16),
                         jax.ShapeDtypeStruct((t_len, D), F32)]
                        + [jax.ShapeDtypeStruct((8, D), F32), jax.ShapeDtypeStruct((N_DEV, D), F32)]
                        + [jax.ShapeDtypeStruct((FC_EARLY,) + s.shape, BF16) for s in fc_shards]
                        + [jax.ShapeDtypeStruct((D, D), BF16), jax.ShapeDtypeStruct((D_Z, D), BF16)]),
        in_specs=[first(D), second(D), _full((1, D)), _resident(w_ada.shape), _full((N_DEV, 1, ncol)), _full((1, D)),
                  _full((1, D)), _resident(w_in_shard.shape), _resident(w_out_shard.shape),
                  _full((N_HEADS, CHUNK, CHUNK)), _full((CHUNK, D_A)), _full((1, D_A)), _full((1, D_A)),
                  _full((len(WINDOWS), GROUP, GROUP)), _full((1, D_B)), _full((1, D_B)),
                  _resident(fc_shards[0].shape), _resident(fc_shards[1].shape), _full((1, D))],
        out_specs=(first(D_Z), first(D), second(D), second(D), second(D), r_head, second(D),
                   _full((8, D)), _full((N_DEV, D)), hbm, hbm, hbm, hbm),
        scratch_shapes=[pltpu.VMEM((HALO, D_B), F32),
                        pltpu.VMEM((2,) + fc_shards[0].shape, BF16), pltpu.VMEM((2,) + fc_shards[1].shape, BF16),
                        pltpu.VMEM(fc_shards[0].shape, BF16), pltpu.VMEM(fc_shards[1].shape, BF16),
                        pltpu.VMEM((ATTN_LAG + 1, tt, D), BF16), pltpu.VMEM((D, D), BF16),
                        pltpu.VMEM((D_Z, D), BF16), pltpu.VMEM((8, D), F32),
                        pltpu.VMEM((N_DEV, 8, D), F32), pltpu.VMEM((N_DEV, N_DEV, ncol), F32),
                        pltpu.VMEM((N_DEV, ncol), F32),
                        pltpu.SemaphoreType.DMA((10,)), pltpu.SemaphoreType.DMA((10,)),
                        pltpu.SemaphoreType.DMA((10,)),
                        pltpu.SemaphoreType.DMA((7,)), pltpu.SemaphoreType.DMA((7,)),
                        pltpu.SemaphoreType.DMA((7,)), pltpu.SemaphoreType.DMA((7,)),
                        pltpu.SemaphoreType.DMA((2 * (N_DEV - 1),)), pltpu.SemaphoreType.DMA((2 * (N_DEV - 1),))],
        compiler_params=pltpu.CompilerParams(dimension_semantics=("arbitrary",), vmem_limit_bytes=VMEM_LIMIT),
    )(x, x, c_row, w_ada, b_pieces, n1pre, n1post, w_in_shard, w_out_shard, w_sp, bs_rows, ln_g, ln_b, w_pool, b_pool,
      pool_scale, *fc_shards, n2pre)
    return outs[:9], outs[9:11], outs[11], outs[12]


def _mlp_fwd_early(tt, r_begun, h2, f_head, w1_early, w2_early):
    t_len = h2.shape[0]
    nt = t_len // tt
    n_late = N_DEV - FC_EARLY

    def body(r_begun_ref, h2_ref, fh_ref, w1_ref, w2_ref, r_ref, f_ref, l1_ref, l2_ref,
             land1, land2, send_sems, recv_sems, local_sems):
        i = pl.program_id(0)
        copies = _Copies(
            [(w1_ref.at[2], land1, 4), (w2_ref.at[4], land2, 2),
             (land1, l1_ref.at[1], 1), (land2, l2_ref.at[1], 1)],
            send_sems, recv_sems)
        keep = [pltpu.make_async_copy(land1, l1_ref.at[0], local_sems.at[0]),
                pltpu.make_async_copy(land2, l2_ref.at[0], local_sems.at[1])]

        @pl.when(i == 0)
        def _():
            copies.start(0, 1)

        @pl.when(i == nt - 1)
        def _():
            copies.wait_recv(0, 1)
            copies.start(2, 3)
            for cp in keep:
                cp.start()

        h2 = h2_ref[...]
        f_ref[...] = fh_ref[...]
        for j in range(FC_HEAD, FC_EARLY):
            ra = jnp.maximum(_dot(h2, w1_ref[j]), 0.0)
            r = (ra * ra).astype(BF16)
            r_ref[:, _early_col(j):_early_col(j) + FF_BLK] = r
            f_ref[...] += _dot(r, w2_ref[j])

        @pl.when(i == nt - 1)
        def _():
            copies.wait_recv(2, 3)
            copies.wait_send(0, 1, 2, 3)
            for cp in keep:
                cp.wait()

    tile = lambda w: pl.BlockSpec((tt, w), lambda i: (i, 0))
    hbm = pl.BlockSpec(memory_space=pl.ANY)
    outs = pl.pallas_call(
        body, name="mlp_fwd_early", grid=(nt,),
        out_shape=(jax.ShapeDtypeStruct((t_len, FC_EARLY * FF_BLK), BF16), jax.ShapeDtypeStruct((t_len, D), F32),
                   jax.ShapeDtypeStruct((n_late,) + w1_early.shape[1:], BF16),
                   jax.ShapeDtypeStruct((n_late,) + w2_early.shape[1:], BF16)),
        in_specs=[hbm, tile(D), tile(D), _resident((FC_EARLY, D, FF_BLK)), _resident((FC_EARLY, FF_BLK, D))],
        out_specs=(tile(R_HEAD_COLS), tile(D), hbm, hbm),
        input_output_aliases={0: 0},
        scratch_shapes=[pltpu.VMEM(w1_early.shape[1:], BF16), pltpu.VMEM(w2_early.shape[1:], BF16),
                        pltpu.SemaphoreType.DMA((4,)), pltpu.SemaphoreType.DMA((4,)),
                        pltpu.SemaphoreType.DMA((2,))],
        compiler_params=pltpu.CompilerParams(dimension_semantics=("arbitrary",), vmem_limit_bytes=VMEM_LIMIT),
    )(r_begun, h2, f_head, w1_early, w2_early)
    return outs[:2], outs[2:]


def _mlp_late_bwd(tt, r_early, x1, h2, f_early, tgt, mix, mod, n2pre, n2post, n1post,
                  w1_early, w2_early, w1_late, w2_late):
    t_len = x1.shape[0]
    nt = t_len // tt
    n_late = N_DEV - FC_EARLY
    late_cols = n_late * FF_BLK

    def body(re_ref, x1_ref, h2_ref, fe_ref, tgt_ref, mix_ref, mod_ref, n2pre_ref, n2post_ref,
             n1post_ref, w1e_ref, w2e_ref, w1l_ref, w2l_ref,
             rl_ref, df_ref, da_ref, dmix_ref, dx1_ref, redf_ref, redb_ref, dh2_acc):
        i = pl.program_id(0)

        @pl.when(i == 0)
        def _():
            redf_ref[...] = jnp.zeros_like(redf_ref)
            redb_ref[...] = jnp.zeros_like(redb_ref)

        x1v = x1_ref[...]
        gate1, scale2, gate2 = mod_ref[2:3, :], mod_ref[4:5, :], mod_ref[5:6, :]
        h2 = h2_ref[...]
        f = fe_ref[...]
        for j in range(n_late):
            cols = slice(j * FF_BLK, (j + 1) * FF_BLK)
            ra = jnp.maximum(_dot(h2, w1l_ref[j]), 0.0)
            r = (ra * ra).astype(BF16)
            rl_ref[:, cols] = r
            f = f + _dot(r, w2l_ref[j])
        post2 = n2post_ref[...]
        gate_post2 = gate2 * post2
        rf = _rstd(f)
        fhat = f * rf
        err = (x1v + fhat * gate_post2) - tgt_ref[...]
        dy = err * (1.0 / D)
        d_f, sum_f = _rms_bwd_gained(dy, gate_post2, fhat, rf)
        dfv = d_f.astype(BF16)
        df_ref[...] = dfv
        redf_ref[0:1, :] += post2 * sum_f
        redf_ref[1:2, :] += gate2 * sum_f
        redf_ref[2:3, :] += _colsum(err * err)

        for j in range(N_DEV):
            cols = slice(j * FF_BLK, (j + 1) * FF_BLK)
            if j < FC_EARLY:
                w1, w2, r = w1e_ref[j], w2e_ref[j], re_ref[:, _early_col(j):_early_col(j) + FF_BLK]
            else:
                jl = j - FC_EARLY
                w1, w2, r = w1l_ref[jl], w2l_ref[jl], rl_ref[:, jl * FF_BLK:(jl + 1) * FF_BLK]
            dr = _dot_nt(dfv, w2)
            da = (dr * (2.0 * jnp.sqrt(r.astype(F32)))).astype(BF16)
            da_ref[:, cols] = da
            contrib = _dot_nt(da, w1)
            if j == 0:
                dh2_acc[...] = contrib
            else:
                dh2_acc[...] += contrib
        dh2 = dh2_acc[...]
        pre2, post1 = n2pre_ref[...], n1post_ref[...]
        r2 = _rstd(x1v)
        xhat = x1v * r2
        d_x1, sum_h = _rms_bwd_gained(dh2, pre2 * (1.0 + scale2), xhat, r2)
        dx1 = dy + d_x1
        dx1_ref[...] = dx1
        mixv = mix_ref[...]
        rm = _rstd(mixv)
        mhat = mixv * rm
        d_mix, sum_m = _rms_bwd_gained(dx1, gate1 * post1, mhat, rm)
        dmix_ref[...] = d_mix.astype(BF16)
        redb_ref[0:1, :] += _colsum(dh2)
        redb_ref[1:2, :] += pre2 * sum_h
        redb_ref[2:3, :] += (1.0 + scale2) * sum_h
        redb_ref[3:4, :] += post1 * sum_m
        redb_ref[4:5, :] += gate1 * sum_m

    tile = lambda w: pl.BlockSpec((tt, w), lambda i: (i, 0))
    return pl.pallas_call(
        body, name="mlp_late_bwd", grid=(nt,),
        out_shape=(jax.ShapeDtypeStruct((t_len, late_cols), BF16), jax.ShapeDtypeStruct((t_len, D), BF16),
                   jax.ShapeDtypeStruct((t_len, D_FF), BF16), jax.ShapeDtypeStruct((t_len, D), BF16),
                   jax.ShapeDtypeStruct((t_len, D), F32), jax.ShapeDtypeStruct((8, D), F32),
                   jax.ShapeDtypeStruct((8, D), F32)),
        in_specs=[tile(FC_EARLY * FF_BLK), tile(D), tile(D), tile(D), tile(D),
                  tile(D), _full((8, D)), _full((1, D)), _full((1, D)), _full((1, D)),
                  _resident((FC_EARLY, D, FF_BLK)), _resident((FC_EARLY, FF_BLK, D)),
                  _resident((n_late, D, FF_BLK)), _resident((n_late, FF_BLK, D))],
        out_specs=(tile(late_cols), tile(D), tile(D_FF), tile(D), tile(D), _full((8, D)), _full((8, D))),
        scratch_shapes=[pltpu.VMEM((tt, D), F32)],
        compiler_params=pltpu.CompilerParams(dimension_semantics=("arbitrary",), vmem_limit_bytes=VMEM_LIMIT),
    )(r_early, x1, h2, f_early, tgt, mix, mod, n2pre, n2post, n1post, w1_early, w2_early, w1_late, w2_late)


def _mlp_wgrad(tt, r_early, r_late, da, df, h2):
    t_len = df.shape[0]
    nt = t_len // tt
    odd_steps = [j for j, rel in enumerate(WGRAD_ORDER) if rel % 2]

    def relation(j):
        rel = jnp.int32(WGRAD_ORDER[-1])
        for step in range(N_DEV - 2, -1, -1):
            rel = jnp.where(j == step, WGRAD_ORDER[step], rel)
        return rel

    def body(re_ref, rl_ref, da_ref, df_ref, h2_ref, own1_ref, own2_ref, out1_ref, out2_ref, diag1_ref, diag2_ref,
             acc1, acc2, snd1, snd2, sib1, sib2, dsnd1, dsnd2, send_sems, recv_sems):
        j, t = pl.program_id(0), pl.program_id(1)
        rows = pl.ds(pl.multiple_of(t * tt, tt), tt)
        x, y, c = _place()
        accs, snds, sibs = (acc1, acc2), (snd1, snd2), (sib1, sib2)
        dsnds, diags = (dsnd1, dsnd2), (diag1_ref, diag2_ref)

        def to_sibling(a, jj, buf=0):
            return pltpu.make_async_remote_copy(
                src_ref=snds[a].at[buf], dst_ref=sibs[a].at[jj],
                send_sem=send_sems.at[4 * a + jj], recv_sem=recv_sems.at[4 * a + jj],
                device_id=(x, y, 1 - c), device_id_type=MESH)

        def to_diagonal(a):
            return pltpu.make_async_remote_copy(
                src_ref=dsnds[a], dst_ref=diags[a], send_sem=send_sems.at[8 + a], recv_sem=recv_sems.at[8 + a],
                device_id=_peer(x, y, c, 6), device_id_type=MESH)

        @pl.when(t == 0)
        def _():
            acc2[...] = jnp.zeros_like(acc2)
            acc1[...] = jnp.zeros_like(acc1)

        for r_ref, mine in ((re_ref, relation(j) < FC_EARLY), (rl_ref, relation(j) >= FC_EARLY)):
            @pl.when(mine)
            def _():
                acc2[...] += _dot_tn(r_ref[...], df_ref[rows, :])
                acc1[...] += _dot_tn(h2_ref[rows, :], da_ref[...])

        for step, rel in enumerate(WGRAD_ORDER):
            jj = rel // 2

            @pl.when((t == nt - 1) & (j == step))
            def _():
                for a, (own_ref, out_ref) in enumerate(((own1_ref, out1_ref), (own2_ref, out2_ref))):
                    if rel % 2:
                        q = odd_steps.index(step)
                        if q >= 2:
                            to_sibling(a, WGRAD_ORDER[odd_steps[q - 2]] // 2).wait_send()
                        snds[a][q % 2] = accs[a][...].astype(BF16)
                        to_sibling(a, jj, q % 2).start()
                        continue
                    to_sibling(a, jj).wait_recv()
                    chip_sum = accs[a][...] + sibs[a][jj].astype(F32)
                    if rel == 6:
                        dsnds[a][...] = chip_sum.astype(BF16)
                        to_diagonal(a).start()
                    elif rel == 0:
                        own_ref[...] = chip_sum
                    else:
                        out_ref[0] = chip_sum.astype(BF16)
                    if step == N_DEV - 1:
                        for q in (2, 3):
                            to_sibling(a, WGRAD_ORDER[odd_steps[q]] // 2).wait_send()
                        to_diagonal(a).wait_recv()
                        to_diagonal(a).wait_send()

    assert WGRAD_ORDER[-1] == 0 and WGRAD_ORDER[-3:-1] == (2, 4)
    blk = pl.BlockSpec((tt, FF_BLK), lambda j, t: (t, relation(j)))
    early_block = lambda rel: jnp.where(rel < FC_HEAD, rel + FC_EARLY - FC_HEAD, rel - FC_HEAD)
    early = lambda j, t: (jnp.where(relation(j) < FC_EARLY, t, 0),
                          jnp.where(relation(j) < FC_EARLY, early_block(relation(j)), 0))
    late = lambda j, t: (jnp.where(relation(j) < FC_EARLY, 0, t), jnp.maximum(relation(j) - FC_EARLY, 0))
    chip = lambda j, t: (jnp.clip(j - 5, 0, 1), 0, 0)
    hbm = pl.BlockSpec(memory_space=pl.ANY)
    return pl.pallas_call(
        body, name="mlp_wgrad", grid=(N_DEV, nt),
        out_shape=(jax.ShapeDtypeStruct((D, FF_BLK), F32), jax.ShapeDtypeStruct((FF_BLK, D), F32),
                   jax.ShapeDtypeStruct((2, D, FF_BLK), BF16), jax.ShapeDtypeStruct((2, FF_BLK, D), BF16),
                   jax.ShapeDtypeStruct((D, FF_BLK), BF16), jax.ShapeDtypeStruct((FF_BLK, D), BF16)),
        in_specs=[pl.BlockSpec((tt, FF_BLK), early), pl.BlockSpec((tt, FF_BLK), late), blk,
                  _resident((t_len, D)), _resident((t_len, D))],
        out_specs=(_full((D, FF_BLK)), _full((FF_BLK, D)),
                   pl.BlockSpec((1, D, FF_BLK), chip), pl.BlockSpec((1, FF_BLK, D), chip), hbm, hbm),
        scratch_shapes=[pltpu.VMEM((D, FF_BLK), F32), pltpu.VMEM((FF_BLK, D), F32),
                        pltpu.VMEM((2, D, FF_BLK), BF16), pltpu.VMEM((2, FF_BLK, D), BF16),
                        pltpu.VMEM((4, D, FF_BLK), BF16), pltpu.VMEM((4, FF_BLK, D), BF16),
                        pltpu.VMEM((D, FF_BLK), BF16), pltpu.VMEM((FF_BLK, D), BF16),
                        pltpu.SemaphoreType.DMA((10,)), pltpu.SemaphoreType.DMA((10,))],
        compiler_params=pltpu.CompilerParams(dimension_semantics=("arbitrary", "arbitrary"),
                                             vmem_limit_bytes=VMEM_LIMIT),
    )(r_early, r_late, da, df, h2)


def _acc_rows(ref, row0, k, val):
    half = CHUNK // 2
    ref[row0:row0 + half, k * GROUP:(k + 1) * GROUP] += val[:half, :]
    ref[row0:row0 + half, D_A + k * GROUP:D_A + (k + 1) * GROUP] += val[half:, :]


def _attn_bwd(tt, dmix, x, z, cat, mod, n1pre, w_out, w_sp, bs_rows, ln_g, ln_b, w_pool, b_pool, pool_scale,
              red_fwd, red_bwd, chip_sums):
    t_len = z.shape[0]
    nt = t_len // tt
    hb = tt // HALO
    n_sums = len(chip_sums)

    def body(dmix_ref, x_ref, z_ref, zprev_ref, cat_ref, mod_ref, n1pre_ref, wout_ref, wsp_ref,
             bs_ref, lng_ref, lnb_ref, wp_ref, bp_ref, ps_ref, redf_ref, redb_ref, *rest):
        sum_out = rest[:n_sums]
        dz_ref, gwin_ref, gwout_ref, small_ref = rest[n_sums:n_sums + 4]
        sum_in = rest[n_sums + 4:2 * n_sums + 4]
        carry, acc_in, acc_out, dz_scr, bs_acc, send_sems, recv_sems = rest[2 * n_sums + 4:]
        s = pl.program_id(0)
        i = nt - 1 - s
        px, py, pc = _place()

        def chip_copy(a, r):
            return pltpu.make_async_remote_copy(
                src_ref=sum_out[a].at[r], dst_ref=sum_in[a].at[r],
                send_sem=send_sems.at[2 * a + r], recv_sem=recv_sems.at[2 * a + r],
                device_id=_peer(px, py, pc, 2 * (r + 1)), device_id_type=MESH)

        @pl.when(s == 0)
        def _():
            for a in range(n_sums):
                for r in range(2):
                    chip_copy(a, r).start()
            carry[...] = jnp.zeros_like(carry)
            acc_in[...] = jnp.zeros_like(acc_in)
            acc_out[...] = jnp.zeros_like(acc_out)
            bs_acc[...] = jnp.zeros_like(bs_acc)
            small_ref[...] = jnp.zeros_like(small_ref)
            small_ref[ROW_DMOD + 2:ROW_DMOD + 3, :] = redb_ref[3:4, :]
            small_ref[ROW_DMOD + 3:ROW_DMOD + 5, :] = redb_ref[0:2, :]
            small_ref[ROW_DMOD + 5:ROW_DMOD + 6, :] = redf_ref[0:1, :]
            small_ref[ROW_N1POST:ROW_N1POST + 1, :] = redb_ref[4:5, :]
            small_ref[ROW_N2PRE:ROW_N2PRE + 1, :] = redb_ref[2:3, :]
            small_ref[ROW_N2POST:ROW_N2POST + 1, :] = redf_ref[1:2, :]
            small_ref[ROW_LOSS:ROW_LOSS + 1, :] = redf_ref[2:3, :]

        dmixv = dmix_ref[...]
        dcat = _dot_nt(dmixv, wout_ref[...])
        acc_out[...] += _dot_tn(cat_ref[...], dmixv)

        z = z_ref[...]
        t_g, ga = _gelu_parts(z[:, :2 * D_A])
        u, vr = ga[:, :D_A], ga[:, D_A:]
        dv0 = vr - jnp.mean(vr, axis=-1, keepdims=True)
        rv = lax.rsqrt(jnp.mean(dv0 * dv0, axis=-1, keepdims=True) + EPS)
        vhat = dv0 * rv
        vb = (vhat * lng_ref[...] + lnb_ref[...]).astype(BF16)
        mask = _tril_mask()
        wc = [(wsp_ref[h] * mask).astype(BF16) for h in range(N_HEADS)]

        dya = dcat[:, :D_A]
        for h in range(N_HEADS):
            cols = slice(h * GROUP, (h + 1) * GROUP)
            bs_sum = jnp.zeros((CHUNK, GROUP), F32)
            ws_sum = jnp.zeros((CHUNK, CHUNK), F32)
            for ch in range(tt // CHUNK):
                rows = slice(ch * CHUNK, (ch + 1) * CHUNK)
                v_ch = vb[rows, cols]
                mixed = _dot(wc[h], v_ch) + bs_ref[:, cols]
                dy_ch = dya[rows, cols]
                dz_scr[rows, cols] = dy_ch * mixed
                dmixed = dy_ch * u[rows, cols]
                dmb = dmixed.astype(BF16)
                dz_scr[rows, D_A + h * GROUP:D_A + (h + 1) * GROUP] = _dot_tn(wc[h], dmb)
                bs_sum = bs_sum + dmixed
                ws_sum = ws_sum + _dot_nt(dmb, v_ch)
            _acc_rows(bs_acc, 0, h, bs_sum)
            _acc_rows(small_ref, ROW_WS, h, ws_sum)

        dvl = dz_scr[:, D_A:2 * D_A]
        dvhat = dvl * lng_ref[...]
        dvl_vhat = dvl * vhat
        dvr = rv * (dvhat - jnp.mean(dvhat, axis=-1, keepdims=True)
                    - vhat * jnp.mean(dvl_vhat * lng_ref[...], axis=-1, keepdims=True))
        small_ref[ROW_LN:ROW_LN + 1, 0:D_A] += _colsum(dvl_vhat)
        small_ref[ROW_LN:ROW_LN + 1, D_A:D] += _colsum(dvl)
        dga = jnp.concatenate([dz_scr[:, :D_A], dvr], axis=1)
        dza = dga * _gelu_grad(z[:, :2 * D_A], t_g)

        zb = z[:, 2 * D_A:]
        halo_prev = jnp.where(i == 0, 0.0, zprev_ref[...])
        diff = _pool_diff(zb, halo_prev, i * tt)
        dyb = dcat[:, D_A:]
        inv = _inv_counts(i * tt, tt)
        scaled, ddiffs = [], []
        for g in range(len(WINDOWS)):
            cols = slice(g * GROUP, (g + 1) * GROUP)
            db = diff[g].astype(BF16)
            wpg = wp_ref[g].astype(BF16)
            pre = _dot(db, wpg) + bp_ref[:, cols]
            small_ref[ROW_POOL:ROW_POOL + 1, cols] += _colsum(dyb[:, cols] * pre)
            dpre = dyb[:, cols] * ps_ref[:, cols]
            small_ref[ROW_POOL:ROW_POOL + 1, D_B + g * GROUP:D_B + (g + 1) * GROUP] += _colsum(dpre)
            dpb = dpre.astype(BF16)
            _acc_rows(small_ref, ROW_WP, g, _dot_tn(db, dpb))
            ddiff = _dot_nt(dpb, wpg)
            ddiffs.append(ddiff)
            scaled.append(ddiff * inv[g])
        scaled_all = jnp.concatenate(scaled, axis=1)
        ext = jnp.concatenate([scaled_all, carry[...]], axis=0)
        n_ext = tt + HALO
        s2 = ext + pltpu.roll(ext, n_ext - 1, 0)
        t4 = s2[:, GROUP:]
        s4 = t4 + pltpu.roll(t4, n_ext - 2, 0)
        t8 = s4[:, GROUP:]
        s8 = t8 + pltpu.roll(t8, n_ext - 4, 0)
        t16 = s8[:, GROUP:]
        s16 = t16 + pltpu.roll(t16, n_ext - 8, 0)
        back = [s2[:, :GROUP], s4[:, :GROUP], s8[:, :GROUP], s16]
        carry[...] = scaled_all[:HALO, :]
        dzb = jnp.concatenate([back[g][:tt, :] - ddiffs[g] for g in range(len(WINDOWS))], axis=1)

        dzv = jnp.concatenate([dza, dzb], axis=1).astype(BF16)
        dz_ref[...] = dzv
        xv = x_ref[...]
        h1 = (xv * _rstd(xv) * (n1pre_ref[...] * (1.0 + mod_ref[1:2, :])) + mod_ref[0:1, :]).astype(BF16)
        acc_in[...] += _dot_tn(dzv, h1)

        @pl.when(s == nt - 1)
        def _():
            gwin_ref[...] = acc_in[...].astype(BF16)
            gwout_ref[...] = acc_out[...].astype(BF16)
            bs = _unfold(bs_acc[...])
            for h in range(N_HEADS):
                small_ref[ROW_BS + h:ROW_BS + h + 1, 0:GROUP] = jnp.sum(
                    bs[:, h * GROUP:(h + 1) * GROUP].T, axis=0, keepdims=True)
            for a in range(n_sums):
                for r in range(2):
                    chip_copy(a, r).wait_recv()
                    chip_copy(a, r).wait_send()

    rev = lambda w: pl.BlockSpec((tt, w), lambda s: (nt - 1 - s, 0))
    zprev = pl.BlockSpec((HALO, D_B), lambda s: (jnp.maximum((nt - 1 - s) * hb - 1, 0), 2))
    hbm = pl.BlockSpec(memory_space=pl.ANY)
    outs = pl.pallas_call(
        body, name="attn_bwd", grid=(nt,),
        out_shape=tuple([jax.ShapeDtypeStruct((t_len, D_Z), BF16), jax.ShapeDtypeStruct((D_Z, D), BF16),
                         jax.ShapeDtypeStruct((D, D), BF16), jax.ShapeDtypeStruct((SMALL_ROWS, D), F32)]
                        + [jax.ShapeDtypeStruct(cs.shape, cs.dtype) for cs in chip_sums]),
        in_specs=[rev(D), rev(D), rev(D_Z), zprev, rev(D), _full((8, D)), _full((1, D)),
                  _resident((D, D)), _full((N_HEADS, CHUNK, CHUNK)), _full((CHUNK, D_A)),
                  _full((1, D_A)), _full((1, D_A)), _full((len(WINDOWS), GROUP, GROUP)), _full((1, D_B)),
                  _full((1, D_B)), _full((8, D)), _full((8, D))] + [_resident(cs.shape) for cs in chip_sums],
        out_specs=tuple([rev(D_Z), _resident((D_Z, D)), _resident((D, D)), _full((SMALL_ROWS, D))]
                        + [hbm] * n_sums),
        scratch_shapes=[pltpu.VMEM((HALO, D_B), F32), pltpu.VMEM((D_Z, D), F32), pltpu.VMEM((D, D), F32),
                        pltpu.VMEM((tt, 2 * D_A), F32), pltpu.VMEM((CHUNK // 2, D), F32),
                        pltpu.SemaphoreType.DMA((2 * n_sums,)), pltpu.SemaphoreType.DMA((2 * n_sums,))],
        compiler_params=pltpu.CompilerParams(dimension_semantics=("arbitrary",), vmem_limit_bytes=VMEM_LIMIT),
    )(dmix, x, z, z, cat, mod, n1pre, w_out, w_sp, bs_rows, ln_g, ln_b, w_pool, b_pool, pool_scale,
      red_fwd, red_bwd, *chip_sums)
    return outs[:4], outs[4:]


def _adam(w, g, m, v):
    m2 = ADAM_B1 * m + (1.0 - ADAM_B1) * g
    v2 = ADAM_B2 * v + (1.0 - ADAM_B2) * (g * g)
    m_hat = m2 / (1.0 - ADAM_B1 ** ADAM_STEP)
    v_hat = v2 / (1.0 - ADAM_B2 ** ADAM_STEP)
    delta = -ADAM_LR * (m_hat / (jnp.sqrt(v_hat) + ADAM_EPS) + ADAM_WD * w)
    return delta, m2, v2


def _adamw_fc(steps, fc):
    n_fc = len(fc)

    def body(*refs):
        ins, outs = refs[:6 * n_fc], refs[6 * n_fc:]
        for k in range(n_fc):
            w_ref, own_ref, arr_ref, diag_ref, m_ref, v_ref = ins[6 * k:6 * k + 6]
            g = ((own_ref[...] + arr_ref[0].astype(F32)) + arr_ref[1].astype(F32)) + diag_ref[...].astype(F32)
            outs[4 * k][...] = g
            outs[4 * k + 1][...], outs[4 * k + 2][...], outs[4 * k + 3][...] = _adam(
                w_ref[...], g, m_ref[...], v_ref[...])

    specs_in, specs_out, shapes, args = [], [], [], []
    for w, own, arrived, diagonal, m, v in fc:
        rows, cols = w.shape
        blk = pl.BlockSpec((rows // steps, cols), lambda i: (i, 0))
        specs_in += [blk, blk, pl.BlockSpec((2, rows // steps, cols), lambda i: (0, i, 0)), blk, blk, blk]
        specs_out += [blk] * 4
        shapes += [jax.ShapeDtypeStruct((rows, cols), F32)] * 4
        args += [w, own, arrived, diagonal, m, v]
    outs = pl.pallas_call(
        body, name="adamw_fc", grid=(steps,), out_shape=tuple(shapes), in_specs=specs_in, out_specs=tuple(specs_out),
        compiler_params=pltpu.CompilerParams(dimension_semantics=("arbitrary",), vmem_limit_bytes=VMEM_LIMIT),
    )(*args)
    return [outs[4 * k:4 * k + 4] for k in range(n_fc)]


def _adamw_ada(rb, w, sc, total, m, v):
    rows, cols = w.shape

    def body(w_ref, sc_ref, t_ref, m_ref, v_ref, g_ref, d_ref, m2_ref, v2_ref, dm):
        me = _index(_place())
        for dev in range(N_DEV):
            @pl.when((pl.program_id(0) == 0) & (me == dev))
            def _():
                for b in range(N_DEV):
                    for k in range(6):
                        lo, hi = max(cols * dev, D * k), min(cols * (dev + 1), D * (k + 1))
                        if lo < hi:
                            dm[b:b + 1, lo - cols * dev:hi - cols * dev] = t_ref[
                                _table_row(b) + k:_table_row(b) + k + 1, lo - D * k:hi - D * k]

        g = _dot_tn(sc_ref[...].astype(BF16), dm[...].astype(BF16))
        g_ref[...] = g
        d_ref[...], m2_ref[...], v2_ref[...] = _adam(w_ref[...], g, m_ref[...], v_ref[...])

    blk = pl.BlockSpec((rb, cols), lambda i: (i, 0))
    shp = jax.ShapeDtypeStruct((rows, cols), F32)
    return pl.pallas_call(
        body, name="adamw_ada", grid=(rows // rb,), out_shape=(shp, shp, shp, shp),
        in_specs=[blk, pl.BlockSpec((N_DEV, rb), lambda i: (0, i)), _full(total.shape), blk, blk],
        out_specs=(blk, blk, blk, blk),
        scratch_shapes=[pltpu.VMEM((N_DEV, cols), F32)],
        compiler_params=pltpu.CompilerParams(dimension_semantics=("arbitrary",)),
    )(w, sc, total, m, v)


def _unfold(acc_rows):
    return jnp.concatenate([acc_rows[:, :D_A], acc_rows[:, D_A:]], axis=0)


def _adamw_small(total, params, shards):
    n = len(params)
    flat = [a for p in params for a in p]

    def body(*refs):
        s_ref = refs[0]
        p_refs = refs[1:1 + 3 * n]
        s_refs = refs[1 + 3 * n:1 + 3 * n + 4 * len(shards)]
        loss_ref = refs[1 + 3 * n + 4 * len(shards)]
        o_refs = refs[2 + 3 * n + 4 * len(shards):2 + 3 * n + 4 * len(shards) + 4 * n]
        so_refs = refs[2 + 3 * n + 4 * len(shards) + 4 * n:]
        d_b_ada = s_ref[0:6, :]
        for b in range(1, N_DEV):
            d_b_ada = d_b_ada + s_ref[_table_row(b):_table_row(b) + 6, :]
        misc = lambda r: s_ref[PK_MISC + r - ROW_N1PRE:PK_MISC + r - ROW_N1PRE + 1, :]
        loss = jnp.sum(misc(ROW_LOSS), axis=-1, keepdims=True) * (0.5 / D)
        loss_ref[...] = loss
        mask = _tril_mask()
        ws = _unfold(s_ref[PK_WS:PK_WS + 64, :])
        wp = _unfold(s_ref[PK_WP:PK_WP + 64, :])
        grads = [
            d_b_ada,
            misc(ROW_N1PRE), misc(ROW_N1POST), misc(ROW_N2PRE), misc(ROW_N2POST),
            misc(ROW_LN)[:, :D_A], misc(ROW_LN)[:, D_A:],
            misc(ROW_POOL)[:, :D_B], misc(ROW_POOL)[:, D_B:],
            s_ref[PK_BS:PK_BS + N_HEADS, 0:GROUP],
            jnp.stack([ws[:, h * GROUP:(h + 1) * GROUP] * mask for h in range(N_HEADS)]),
            jnp.stack([wp[:, g * GROUP:(g + 1) * GROUP] for g in range(len(WINDOWS))]),
        ]
        for k in range(n):
            w_ref, m_ref, v_ref = p_refs[3 * k:3 * k + 3]
            g = grads[k]
            if k == 0:
                for j in range(6):
                    o_refs[0][j] = g[j:j + 1, :]
                    o_refs[1][j], o_refs[2][j], o_refs[3][j] = _adam(w_ref[j], g[j:j + 1, :], m_ref[j], v_ref[j])
                continue
            o_refs[4 * k][...] = g
            o_refs[4 * k + 1][...], o_refs[4 * k + 2][...], o_refs[4 * k + 3][...] = _adam(
                w_ref[...], g, m_ref[...], v_ref[...])
        for k in range(len(shards)):
            w_ref, g_ref, m_ref, v_ref = s_refs[4 * k:4 * k + 4]
            so_refs[3 * k][...], so_refs[3 * k + 1][...], so_refs[3 * k + 2][...] = _adam(
                w_ref[...], g_ref[...], m_ref[...], v_ref[...])

    vm = pl.BlockSpec(memory_space=pltpu.VMEM)
    out_shape = [jax.ShapeDtypeStruct((1, 1), F32)]
    for w, _, _ in params:
        out_shape += [jax.ShapeDtypeStruct(w.shape, F32)] * 4
    for w, _, _, _ in shards:
        out_shape += [jax.ShapeDtypeStruct(w.shape, F32)] * 3
    return pl.pallas_call(
        body, name="adamw_small", out_shape=tuple(out_shape),
        in_specs=[vm] * (1 + 3 * n + 4 * len(shards)), out_specs=tuple([vm] * len(out_shape)),
        compiler_params=pltpu.CompilerParams(vmem_limit_bytes=VMEM_LIMIT),
    )(total, *flat, *[a for s in shards for a in s])


TT_ATTN_FWD = 512
ATTN_LAG = 2
TT_MLP_FWD = 512
TT_MLP = 256
TT_WGRAD = 2048
TT_ATTN_BWD = 512
TT_TAIL = 512


def kernel(x, c, w_ada, b_ada, norm1_pre, norm1_post, w_in, w_spatial, b_spatial, ln_v_gain, ln_v_bias, w_pool, b_pool, pool_scale, w_out, norm2_pre, norm2_post, w_fc1, w_fc2, loss_target, m_w_ada, m_b_ada, m_norm1_pre, m_norm1_post, m_w_in, m_w_spatial, m_b_spatial, m_ln_v_gain, m_ln_v_bias, m_w_pool, m_b_pool, m_pool_scale, m_w_out, m_norm2_pre, m_norm2_post, m_w_fc1, m_w_fc2, v_w_ada, v_b_ada, v_norm1_pre, v_norm1_post, v_w_in, v_w_spatial, v_b_spatial, v_ln_v_gain, v_ln_v_bias, v_w_pool, v_b_pool, v_pool_scale, v_w_out, v_norm2_pre, v_norm2_post, v_w_fc1, v_w_fc2):
    t_len = x.shape[1]
    ada_cols = w_ada.shape[1]
    tt = lambda want: min(want, t_len)

    x2 = x.reshape(t_len, D)
    tgt = loss_target.reshape(t_len, D)
    row = lambda a: a.reshape(1, -1)

    w_in_shard, w_out_shard, w1_shard, w2_shard = _cast_shards([w_in.T, w_out, w_fc1, w_fc2])

    bs_rows = jnp.repeat(b_spatial.T, GROUP, axis=1)
    attn_consts = (w_spatial, bs_rows, row(ln_v_gain), row(ln_v_bias), w_pool, row(b_pool), row(pool_scale))

    (z, cat, mix, x1, h2, r_begun, f_head, mod, sc), (w1_early, w2_early), w_out_all, w_in_t = _attn_fwd(
        tt(TT_ATTN_FWD), x2, c.reshape(1, D), w_ada, b_ada.reshape(N_DEV, 1, ada_cols), row(norm1_pre),
        row(norm1_post),
        w_in_shard, w_out_shard, *attn_consts, (w1_shard, w2_shard), row(norm2_pre))
    (r_early, f_early), (w1_late, w2_late) = _mlp_fwd_early(
        tt(TT_MLP_FWD), r_begun, h2, f_head, w1_early, w2_early)
    r_late, df, da, dmix, dx1, red_fwd, red_bwd = _mlp_late_bwd(
        tt(TT_MLP), r_early, x1, h2, f_early, tgt, mix, mod, row(norm2_pre), row(norm2_post), row(norm1_post),
        w1_early, w2_early, w1_late, w2_late)
    own_w1, own_w2, sums_w1, sums_w2, diag_w1, diag_w2 = _mlp_wgrad(tt(TT_WGRAD), r_early, r_late, da, df, h2)
    (dz, p_in, p_out, small), (arr_w1, arr_w2) = _attn_bwd(
        tt(TT_ATTN_BWD), dmix, x2, z, cat, mod, row(norm1_pre), w_out_all, *attn_consts, red_fwd, red_bwd,
        [sums_w1, sums_w2])
    (grad_w1, d_w1, m_w1, v_w1), (grad_w2, d_w2, m_w2, v_w2) = _adamw_fc(
        4, [(w_fc1, own_w1, arr_w1, diag_w1, m_w_fc1, v_w_fc1), (w_fc2, own_w2, arr_w2, diag_w2, m_w_fc2, v_w_fc2)])
    grad_in_t, grad_out, total, grad_x = _tail_comm(
        [p_in.reshape(N_DEV, D_Z // N_DEV, D), p_out.reshape(N_DEV, D // N_DEV, D)], small, 64,
        tt(TT_TAIL), dz, dx1, x2, mod, row(norm1_pre), w_in_t)

    grad_ada, d_ada, m_ada, v_ada = _adamw_ada(256, w_ada, sc, total, m_w_ada, v_w_ada)

    six = lambda a: a.reshape(6, 1, D)
    small_params = [
        (six(b_ada), six(m_b_ada), six(v_b_ada)),
        (row(norm1_pre), row(m_norm1_pre), row(v_norm1_pre)),
        (row(norm1_post), row(m_norm1_post), row(v_norm1_post)),
        (row(norm2_pre), row(m_norm2_pre), row(v_norm2_pre)),
        (row(norm2_post), row(m_norm2_post), row(v_norm2_post)),
        (row(ln_v_gain), row(m_ln_v_gain), row(v_ln_v_gain)),
        (row(ln_v_bias), row(m_ln_v_bias), row(v_ln_v_bias)),
        (row(pool_scale), row(m_pool_scale), row(v_pool_scale)),
        (row(b_pool), row(m_b_pool), row(v_b_pool)),
        (b_spatial, m_b_spatial, v_b_spatial),
        (w_spatial, m_w_spatial, v_w_spatial),
        (w_pool, m_w_pool, v_w_pool),
    ]
    outs = _adamw_small(total, small_params, [(w_out, grad_out, m_w_out, v_w_out),
                                              (w_in.T, grad_in_t, m_w_in.T, v_w_in.T)])
    d_out, m_out, v_out, d_in_t, m_in_t, v_in_t = outs[1 + 4 * len(small_params):]
    loss = outs[0].reshape(())
    names = ["b_ada", "norm1_pre", "norm1_post", "norm2_pre", "norm2_post", "ln_v_gain", "ln_v_bias", "pool_scale",
             "b_pool", "b_spatial", "w_spatial", "w_pool"]
    shapes = dict(b_ada=b_ada.shape, norm1_pre=norm1_pre.shape, norm1_post=norm1_post.shape,
                  norm2_pre=norm2_pre.shape, norm2_post=norm2_post.shape, ln_v_gain=ln_v_gain.shape,
                  ln_v_bias=ln_v_bias.shape, pool_scale=pool_scale.shape, b_pool=b_pool.shape,
                  b_spatial=b_spatial.shape, w_spatial=w_spatial.shape, w_pool=w_pool.shape)
    res = {}
    for k, nm in enumerate(names):
        res[nm] = tuple(o.reshape(shapes[nm]) for o in outs[1 + 4 * k:5 + 4 * k])
    res["w_ada"] = (grad_ada, d_ada, m_ada, v_ada)
    res["w_in"] = (grad_in_t.T, d_in_t.T, m_in_t.T, v_in_t.T)
    res["w_out"] = (grad_out, d_out, m_out, v_out)
    res["w_fc1"] = (grad_w1, d_w1, m_w1, v_w1)
    res["w_fc2"] = (grad_w2, d_w2, m_w2, v_w2)

    order = ["w_ada", "b_ada", "norm1_pre", "norm1_post", "w_in", "w_spatial", "b_spatial", "ln_v_gain", "ln_v_bias",
             "w_pool", "b_pool", "pool_scale", "w_out", "norm2_pre", "norm2_post", "w_fc1", "w_fc2"]
    return (loss, grad_x.reshape(x.shape),
            *[res[nm][0] for nm in order], *[res[nm][1] for nm in order],
            *[res[nm][2] for nm in order], *[res[nm][3] for nm in order])
```

```python
import functools

import jax
import jax.numpy as jnp
from jax import lax
from jax.experimental import pallas as pl
from jax.experimental.pallas import tpu as pltpu

F32 = jnp.float32
BF16 = jnp.bfloat16
MESH = pl.DeviceIdType.MESH

N_DEV = 8
D = 1024
D_A = 512
D_B = 512
D_Z = 2 * D_A + D_B
N_HEADS = 4
CHUNK = 128
WINDOWS = (2, 4, 8, 16)
GROUP = 128
D_FF = 4096
FF_BLK = D_FF // N_DEV
HALO = 16
EPS = 1e-6
VMEM_LIMIT = 60 * 1024 * 1024

ADAM_LR = 0.001
ADAM_B1 = 0.9
ADAM_B2 = 0.999
ADAM_EPS = 1e-08
ADAM_WD = 0.01
ADAM_STEP = 10

ROW_DMOD = 0
ROW_N1PRE, ROW_N1POST, ROW_N2PRE, ROW_N2POST = 8, 9, 10, 11
ROW_LN = 12
ROW_POOL = 13
ROW_LOSS = 14
ROW_BS = 16
ROW_WS = 24
ROW_WP = 88
SMALL_ROWS = 152
PACK_FINE = 40
PACK_HALF = PACK_FINE + 64
PACK_ROWS = 2 * PACK_HALF
PK_WS = PACK_FINE
PK_TABLE_B = PACK_HALF
PK_MISC = PK_TABLE_B + 24
PK_BS = PK_MISC + 8
PK_WP = PK_BS + 8


def _table_row(b):
    if isinstance(b, int):
        return 8 * b if 8 * b < PACK_FINE else 8 * b + PK_TABLE_B - PACK_FINE
    return 8 * b + jnp.where(8 * b < PACK_FINE, 0, PK_TABLE_B - PACK_FINE)


def _dot(a, b):
    return jnp.dot(a, b, preferred_element_type=F32)


def _dot_nt(a, b):
    return lax.dot_general(a, b, (((1,), (1,)), ((), ())), preferred_element_type=F32)


def _dot_tn(a, b):
    return lax.dot_general(a, b, (((0,), (0,)), ((), ())), preferred_element_type=F32)


def _rstd(v):
    return lax.rsqrt(jnp.mean(v * v, axis=-1, keepdims=True) + EPS)


def _rms_bwd(d_hat, hat, rstd):
    return rstd * (d_hat - hat * jnp.mean(d_hat * hat, axis=-1, keepdims=True))


def _rms_bwd_gained(g, gain, hat, rstd):
    g_hat = g * hat
    d_v = rstd * (g * gain - hat * jnp.mean(g_hat * gain, axis=-1, keepdims=True))
    return d_v, _colsum(g_hat)


_K0 = 0.7978845608028654
_K1 = 0.044715


def _gelu_parts(v):
    t = jnp.tanh(v * (_K0 + (_K0 * _K1) * (v * v)))
    return t, v * (0.5 + 0.5 * t)


def _gelu_grad(v, t):
    return (0.5 + 0.5 * t) + (0.5 * v) * (1.0 - t * t) * (_K0 + (3.0 * _K0 * _K1) * (v * v))


def _colsum(v):
    return jnp.sum(v, axis=0, keepdims=True)


def _full(shape):
    n = len(shape)
    return pl.BlockSpec(shape, lambda *_: (0,) * n)


def _resident(shape):
    n = len(shape)
    return pl.BlockSpec(shape, lambda *_: (0,) * n, pipeline_mode=pl.Buffered(1))


def _place():
    x, y, c = lax.axis_index("x"), lax.axis_index("y"), lax.axis_index("c")
    return x, y, c


def _flip(v, bit):
    return 1 - v if bit else v


def _peer(x, y, c, k):
    return (_flip(x, (k >> 2) & 1), _flip(y, (k >> 1) & 1), _flip(c, k & 1))


def _index(p):
    return 4 * p[0] + 2 * p[1] + p[2]


def _cast_shards(shards):
    def body(*refs):
        for src, dst in zip(refs[:len(shards)], refs[len(shards):]):
            dst[...] = src[...].astype(BF16)

    vm = pl.BlockSpec(memory_space=pltpu.VMEM)
    return pl.pallas_call(
        body, name="cast_shards", out_shape=tuple(jax.ShapeDtypeStruct(s.shape, BF16) for s in shards),
        in_specs=[vm] * len(shards), out_specs=tuple([vm] * len(shards)),
    )(*shards)


FC_EARLY = 6
FC_HEAD = 2
R_HEAD_COLS = (FC_EARLY - FC_HEAD) * FF_BLK
WGRAD_ORDER = (7, 6, 1, 3, 5, 2, 4, 0)


def _early_col(j):
    return R_HEAD_COLS + j * FF_BLK if j < FC_HEAD else (j - FC_HEAD) * FF_BLK


class _Copies:
    def __init__(self, entries, send_sems, recv_sems):
        self.place = _place()
        self.entries, self.send_sems, self.recv_sems = entries, send_sems, recv_sems

    def _copy(self, i, arrival=False):
        src, dst, rel = self.entries[i]
        return pltpu.make_async_remote_copy(
            src_ref=dst if arrival else src, dst_ref=dst, send_sem=self.send_sems.at[i],
            recv_sem=self.recv_sems.at[i], device_id=_peer(*self.place, rel), device_id_type=MESH)

    def start(self, *which):
        for i in which:
            self._copy(i).start()

    def wait_recv(self, *which):
        for i in which:
            self._copy(i, arrival=True).wait_recv()

    def wait_send(self, *which):
        for i in which:
            self._copy(i).wait_send()


def _tail_comm(parts, small, row_chunk, tt, dz, dx1, x, mod, n1pre, w_in_t):
    n = len(parts)
    t_len = x.shape[0]
    nt = t_len // tt

    def body(*refs):
        p_refs, small_ref = refs[:n], refs[n]
        dz_ref, dx1_ref, x_ref, mod_ref, n1pre_ref, win_ref = refs[n + 1:n + 7]
        outs = refs[n + 7:]
        g_refs, total_ref, gx_ref = outs[:n], outs[n], outs[n + 1]
        scr = outs[n + 2:]
        from_sib = scr[0:n]
        chip_out = scr[n:2 * n]
        chip_in = scr[2 * n:3 * n]
        pack, pack_sib, fine, bulk, total_scr, head, pack_sib2 = scr[3 * n:3 * n + 7]
        send_a, recv_a, send_b, recv_b, send_s, recv_s = scr[3 * n + 7:]
        step = pl.program_id(0)
        x, y, c = _place()
        me = _index((x, y, c))
        sibling = (x, y, 1 - c)
        my_chip = 2 * x + y
        others = [(1 - x, y), (x, 1 - y), (1 - x, 1 - y)]
        my_half = pl.ds(pl.multiple_of(PACK_HALF * c, 8), PACK_HALF)

        def pack_to_sibling(final=False):
            k = 8 if final else 0
            return pltpu.make_async_remote_copy(
                src_ref=pack, dst_ref=pack_sib2 if final else pack_sib, send_sem=send_s.at[k],
                recv_sem=recv_s.at[k], device_id=sibling, device_id_type=MESH)

        def fill_pack_head():
            pack[pl.ds(pl.multiple_of(_table_row(me), 8), 8), :] = small_ref[0:8, :] + head[0:8, :]
            pack[PK_MISC:PK_MISC + 8, :] = small_ref[ROW_N1PRE:ROW_N1PRE + 8, :] + head[8:16, :]

        def half_to_chip(r, part):
            buf = (fine, bulk)[part]
            return pltpu.make_async_remote_copy(
                src_ref=buf.at[my_chip], dst_ref=buf.at[my_chip],
                send_sem=send_s.at[1 + 3 * part + r], recv_sem=recv_s.at[1 + 3 * part + r],
                device_id=(*others[r], c), device_id_type=MESH)

        def half_from_chip(r, part):
            k = 2 * others[r][0] + others[r][1]
            buf = (fine, bulk)[part]
            return pltpu.make_async_remote_copy(
                src_ref=buf.at[k], dst_ref=buf.at[k],
                send_sem=send_s.at[1 + 3 * part + r], recv_sem=recv_s.at[1 + 3 * part + r],
                device_id=(*others[r], c), device_id_type=MESH)

        def total_to_sibling():
            return pltpu.make_async_remote_copy(
                src_ref=total_scr.at[my_half], dst_ref=total_scr.at[my_half],
                send_sem=send_s.at[7], recv_sem=recv_s.at[7], device_id=sibling, device_id_type=MESH)

        def total_from_sibling():
            sib_half = pl.ds(pl.multiple_of(PACK_HALF * (1 - c), 8), PACK_HALF)
            return pltpu.make_async_remote_copy(
                src_ref=total_scr.at[sib_half], dst_ref=total_scr.at[sib_half],
                send_sem=send_s.at[7], recv_sem=recv_s.at[7], device_id=sibling, device_id_type=MESH)

        def to_sibling(a, k):
            return pltpu.make_async_remote_copy(
                src_ref=p_refs[a].at[2 * k + (1 - c)], dst_ref=from_sib[a].at[k],
                send_sem=send_a.at[a], recv_sem=recv_a.at[a], device_id=sibling, device_id_type=MESH)

        def all_from_sibling(a):
            return pltpu.make_async_remote_copy(
                src_ref=from_sib[a], dst_ref=from_sib[a], send_sem=send_a.at[a], recv_sem=recv_a.at[a],
                device_id=sibling, device_id_type=MESH)

        def to_chip(a, r):
            return pltpu.make_async_remote_copy(
                src_ref=chip_out[a].at[r], dst_ref=chip_in[a].at[r],
                send_sem=send_b.at[3 * a + r], recv_sem=recv_b.at[3 * a + r],
                device_id=(*others[r], c), device_id_type=MESH)

        @pl.when(step == 0)
        def _():
            head[...] = jnp.zeros_like(head)
            for a in range(n):
                for k in range(4):
                    to_sibling(a, k).start()

        @pl.when(step == min(1, nt - 1))
        def _():
            pack[0:PACK_FINE, :] = jnp.zeros((PACK_FINE, D), F32)
            pack[PK_TABLE_B:PK_MISC, :] = jnp.zeros((PK_MISC - PK_TABLE_B, D), F32)
            fill_pack_head()
            pack[PK_WS:PK_WS + 64, :] = small_ref[ROW_WS:ROW_WS + 64, :]
            pack[PK_BS:PK_BS + 8, :] = small_ref[ROW_BS:ROW_BS + 8, :]
            pack[PK_WP:PK_WP + 64, :] = small_ref[ROW_WP:ROW_WP + 64, :]
            pack_to_sibling().start()
            pack_to_sibling().wait_recv()
            bulk_rows = pl.ds(pl.multiple_of(PACK_HALF * c + PACK_FINE, 8), PACK_HALF - PACK_FINE)
            bulk[my_chip] = (pack[bulk_rows, :] + pack_sib[bulk_rows, :]).astype(BF16)
            for r in range(3):
                half_to_chip(r, 1).start()
            for a in range(n):
                all_from_sibling(a).wait_recv()
                rows = p_refs[a].shape[1]
                for r in range(3):
                    k = 2 * others[r][0] + others[r][1]
                    for s in range(0, rows, row_chunk):
                        sl = pl.ds(s, row_chunk)
                        chip_out[a][r, sl, :] = (p_refs[a][2 * k + c, sl, :].astype(F32)
                                                 + from_sib[a][k, sl, :].astype(F32)).astype(BF16)
                    to_chip(a, r).start()
                for s in range(0, rows, row_chunk):
                    sl = pl.ds(s, row_chunk)
                    g_refs[a][sl, :] = (p_refs[a][2 * my_chip + c, sl, :].astype(F32)
                                        + from_sib[a][my_chip, sl, :].astype(F32))

        dh1 = _dot(dz_ref[...], win_ref[...])
        xv = x_ref[...]
        r1 = _rstd(xv)
        xhat = xv * r1
        scale1 = mod_ref[1:2, :]
        pre1 = n1pre_ref[...]
        d_x, sum_h = _rms_bwd_gained(dh1, pre1 * (1.0 + scale1), xhat, r1)
        gx_ref[...] = dx1_ref[...] + d_x
        head[ROW_DMOD:ROW_DMOD + 1, :] += _colsum(dh1)
        head[ROW_DMOD + 1:ROW_DMOD + 2, :] += pre1 * sum_h
        head[ROW_N1PRE:ROW_N1PRE + 1, :] += (1.0 + scale1) * sum_h

        @pl.when(step == nt - 1)
        def _():
            pack_to_sibling().wait_send()
            fill_pack_head()
            pack_to_sibling(final=True).start()
            pack_to_sibling(final=True).wait_recv()
            fine_rows = pl.ds(pl.multiple_of(PACK_HALF * c, 8), PACK_FINE)
            fine[my_chip] = pack[fine_rows, :] + pack_sib2[fine_rows, :]
            for r in range(3):
                half_to_chip(r, 0).start()
            for a in range(n):
                rows = p_refs[a].shape[1]
                for r in range(3):
                    to_chip(a, r).wait_recv()
                    for s in range(0, rows, row_chunk):
                        sl = pl.ds(s, row_chunk)
                        g_refs[a][sl, :] = g_refs[a][sl, :] + chip_in[a][r, sl, :].astype(F32)
            for r in range(3):
                half_from_chip(r, 0).wait_recv()
                half_from_chip(r, 1).wait_recv()
            half_start = pl.multiple_of(PACK_HALF * c, 8)
            total_scr[pl.ds(half_start, PACK_FINE), :] = ((fine[0] + fine[1]) + fine[2]) + fine[3]
            total_scr[pl.ds(half_start + PACK_FINE, PACK_HALF - PACK_FINE), :] = (
                (bulk[0].astype(F32) + bulk[1].astype(F32)) + bulk[2].astype(F32)) + bulk[3].astype(F32)
            total_to_sibling().start()
            total_from_sibling().wait_recv()
            total_ref[...] = total_scr[...]
            for a in range(n):
                all_from_sibling(a).wait_send()
                for r in range(3):
                    to_chip(a, r).wait_send()
            pack_to_sibling(final=True).wait_send()
            for r in range(3):
                half_to_chip(r, 0).wait_send()
                half_to_chip(r, 1).wait_send()
            total_to_sibling().wait_send()

    tile = lambda w: pl.BlockSpec((tt, w), lambda i: (i, 0))
    return pl.pallas_call(
        body, name="tail_comm", grid=(nt,),
        out_shape=tuple([jax.ShapeDtypeStruct(p.shape[1:], F32) for p in parts]
                        + [jax.ShapeDtypeStruct((PACK_ROWS, D), F32), jax.ShapeDtypeStruct((t_len, D), F32)]),
        in_specs=([_resident(p.shape) for p in parts] + [_resident(small.shape)]
                  + [tile(D_Z), tile(D), tile(D), _full((8, D)), _full((1, D)), _resident((D_Z, D))]),
        out_specs=tuple([_full(p.shape[1:]) for p in parts] + [_full((PACK_ROWS, D)), tile(D)]),
        scratch_shapes=(
            [pltpu.VMEM((4,) + p.shape[1:], BF16) for p in parts]
            + [pltpu.VMEM((3,) + p.shape[1:], BF16) for p in parts]
            + [pltpu.VMEM((3,) + p.shape[1:], BF16) for p in parts]
            + [pltpu.VMEM((PACK_ROWS, D), F32), pltpu.VMEM((PACK_ROWS, D), F32),
               pltpu.VMEM((4, PACK_FINE, D), F32), pltpu.VMEM((4, PACK_HALF - PACK_FINE, D), BF16),
               pltpu.VMEM((PACK_ROWS, D), F32), pltpu.VMEM((16, D), F32), pltpu.VMEM((PACK_ROWS, D), F32)]
            + [pltpu.SemaphoreType.DMA((n,)), pltpu.SemaphoreType.DMA((n,)),
               pltpu.SemaphoreType.DMA((3 * n,)), pltpu.SemaphoreType.DMA((3 * n,)),
               pltpu.SemaphoreType.DMA((9,)), pltpu.SemaphoreType.DMA((9,))]),
        compiler_params=pltpu.CompilerParams(dimension_semantics=("arbitrary",), vmem_limit_bytes=VMEM_LIMIT),
    )(*parts, small, dz, dx1, x, mod, n1pre, w_in_t)


def _tril_mask():
    row = lax.broadcasted_iota(jnp.int32, (CHUNK, CHUNK), 0)
    col = lax.broadcasted_iota(jnp.int32, (CHUNK, CHUNK), 1)
    return (col <= row).astype(F32)


def _window_sums(ext):
    s2 = ext + pltpu.roll(ext, 1, 0)
    t4 = s2[:, GROUP:]
    s4 = t4 + pltpu.roll(t4, 2, 0)
    t8 = s4[:, GROUP:]
    s8 = t8 + pltpu.roll(t8, 4, 0)
    t16 = s8[:, GROUP:]
    s16 = t16 + pltpu.roll(t16, 8, 0)
    return [s2[:, :GROUP], s4[:, :GROUP], s8[:, :GROUP], s16]


def _inv_counts(first_pos, rows):
    pos = first_pos + lax.broadcasted_iota(jnp.int32, (rows, 1), 0)
    return [1.0 / jnp.minimum(pos + 1, w).astype(F32) for w in WINDOWS]


def _pool_diff(zb, halo, first_pos):
    tt = zb.shape[0]
    sums = _window_sums(jnp.concatenate([halo, zb], axis=0))
    inv = _inv_counts(first_pos, tt)
    return [sums[g][HALO:, :] * inv[g] - zb[:, g * GROUP:(g + 1) * GROUP] for g in range(len(WINDOWS))]


def _row_blocks(scr, rows, place):
    def block(rel):
        start = pl.multiple_of(rows * _index(_peer(*place, rel)), rows)
        return scr.at[pl.ds(start, rows), :]

    def entries(shard_ref):
        return ([(shard_ref, block(0), rel) for rel in (1, 2, 4, 6)]
                + [(block(rel), block(rel), 1) for rel in (2, 4, 6)])
    return block, entries


def _attn_fwd(tt, x, c_row, w_ada, b_pieces, n1pre, n1post, w_in_shard, w_out_shard, w_sp, bs_rows, ln_g, ln_b,
              w_pool, b_pool, pool_scale, fc_shards, n2pre):
    t_len = x.shape[0]
    nt = t_len // tt
    ncol = w_ada.shape[1]

    def body(x_ref, xb_ref, c_ref, wada_ref, b_ref, n1pre_ref, n1post_ref, wi_ref, wo_ref, wsp_ref, bs_ref,
             lng_ref, lnb_ref, wp_ref, bp_ref, ps_ref, w1_ref, w2_ref, n2pre_ref,
             z_ref, cat_ref, mix_ref, x1_ref, h2_ref, r_ref, f_ref, mod_out, sc_out, e1_ref, e2_ref, wout_ref,
             win_out, carry, land1, land2, sib1, sib2, cat_keep, wout_scr, win_ref, mod_ref, cg, mg, part,
             send_sems, recv_sems, local_sems, wo_send, wo_recv, wi_send, wi_recv, ada_send, ada_recv):
        i = pl.program_id(0)
        place = px, py, pc = _place()
        me = _index(place)
        wo_block, wo_entries = _row_blocks(wout_scr, w_out_shard.shape[0], place)
        wi_block, wi_entries = _row_blocks(win_ref, w_in_shard.shape[0], place)
        wo_copies = _Copies(wo_entries(wo_ref), wo_send, wo_recv)
        wi_copies = _Copies(wi_entries(wi_ref), wi_send, wi_recv)
        wo_keep = pltpu.make_async_copy(wout_scr, wout_ref, local_sems.at[8])
        wi_keep = pltpu.make_async_copy(win_ref, win_out, local_sems.at[9])
        ada = _Copies([(cg.at[me], cg.at[me], k) for k in range(1, N_DEV)]
                      + [(part, mg.at[me], k) for k in range(1, N_DEV)], ada_send, ada_recv)
        copies = _Copies(
            [(w1_ref, sib1, 1), (w2_ref, sib2, 1),
             (w1_ref, land1.at[0], 2), (w2_ref, land2.at[0], 2),
             (w1_ref, land1.at[1], 4), (w2_ref, land2.at[1], 4),
             (land1.at[0], e1_ref.at[3], 1), (land2.at[0], e2_ref.at[3], 1),
             (land1.at[1], e1_ref.at[5], 1), (land2.at[1], e2_ref.at[5], 1)],
            send_sems, recv_sems)
        keep = [pltpu.make_async_copy(w1_ref, e1_ref.at[0], local_sems.at[0]),
                pltpu.make_async_copy(w2_ref, e2_ref.at[0], local_sems.at[1]),
                pltpu.make_async_copy(land1.at[0], e1_ref.at[2], local_sems.at[2]),
                pltpu.make_async_copy(land1.at[1], e1_ref.at[4], local_sems.at[3]),
                pltpu.make_async_copy(land2.at[0], e2_ref.at[2], local_sems.at[4]),
                pltpu.make_async_copy(land2.at[1], e2_ref.at[4], local_sems.at[5]),
                pltpu.make_async_copy(sib1, e1_ref.at[1], local_sems.at[6]),
                pltpu.make_async_copy(sib2, e2_ref.at[1], local_sems.at[7])]

        @pl.when(i == 0)
        def _():
            cg[me] = jnp.broadcast_to(c_ref[...], (8, D))
            ada.start(*range(N_DEV - 1))
            wi_copies.start(0, 1, 2, 3)
            wi_rows, wo_rows = w_in_shard.shape[0], w_out_shard.shape[0]
            win_ref[pl.ds(pl.multiple_of(wi_rows * me, wi_rows), wi_rows), :] = wi_ref[...]
            wout_scr[pl.ds(pl.multiple_of(wo_rows * me, wo_rows), wo_rows), :] = wo_ref[...]
            carry[...] = jnp.zeros_like(carry)

            ada.wait_recv(*range(N_DEV - 1))
            c_all = jnp.concatenate([cg[j, 0:1, :] for j in range(N_DEV)], axis=0)
            sc = c_all * jax.nn.sigmoid(c_all)
            sc_out[...] = sc
            part[...] = _dot(sc.astype(BF16), wada_ref[...].astype(BF16)) + b_ref[me]
            ada.start(*range(N_DEV - 1, 2 * (N_DEV - 1)))
            wo_copies.start(0, 1, 2, 3)
            copies.start(0, 1, 2, 4, 3, 5)
            keep[0].start()
            keep[1].start()
            mg[me] = part[...]
            ada.wait_recv(*range(N_DEV - 1, 2 * (N_DEV - 1)))
            mod_ref[...] = jnp.zeros_like(mod_ref)
            for j in range(N_DEV):
                for m in range(6):
                    lo, hi = max(ncol * j, D * m), min(ncol * (j + 1), D * (m + 1))
                    if lo < hi:
                        mod_ref[m:m + 1, lo - D * m:hi - D * m] = mg[j, pl.ds(me, 1), lo - ncol * j:hi - ncol * j]
            mod_out[...] = mod_ref[...]

            wi_copies.wait_recv(1, 2, 3)
            wi_copies.start(4, 5, 6)
            wi_copies.wait_recv(0, 4, 5, 6)
            wi_keep.start()

        @pl.when(i == nt // 2 + ATTN_LAG)
        def _():
            copies.wait_recv(2, 4)
            copies.start(6, 8)
            keep[2].start()
            keep[3].start()

        @pl.when(i == nt - 1 + ATTN_LAG)
        def _():
            copies.wait_recv(3, 5)
            copies.start(7, 9)
            keep[4].start()
            keep[5].start()

        shift1, scale1, gate1 = mod_ref[0:1, :], mod_ref[1:2, :], mod_ref[2:3, :]

        @pl.when(i < nt)
        def _():
            xv = x_ref[...]
            h1 = (xv * _rstd(xv)) * (n1pre_ref[...] * (1.0 + scale1)) + shift1
            z = _dot_nt(h1.astype(BF16), win_ref[...])
            z_ref[...] = z

            _, ga = _gelu_parts(z[:, :2 * D_A])
            u, vr = ga[:, :D_A], ga[:, D_A:]
            dv = vr - jnp.mean(vr, axis=-1, keepdims=True)
            v = (dv * lax.rsqrt(jnp.mean(dv * dv, axis=-1, keepdims=True) + EPS)) * lng_ref[...] + lnb_ref[...]
            vb = v.astype(BF16)
            mask = _tril_mask()
            wc = [(wsp_ref[h] * mask).astype(BF16) for h in range(N_HEADS)]
            for ch in range(tt // CHUNK):
                rows = slice(ch * CHUNK, (ch + 1) * CHUNK)
                for h in range(N_HEADS):
                    cols = slice(h * GROUP, (h + 1) * GROUP)
                    mixed = _dot(wc[h], vb[rows, cols]) + bs_ref[:, cols]
                    cat_ref[rows, cols] = (u[rows, cols] * mixed).astype(BF16)

            zb = z[:, 2 * D_A:]
            diff = _pool_diff(zb, carry[...], i * tt)
            carry[...] = zb[tt - HALO:, :]
            for g in range(len(WINDOWS)):
                cols = slice(g * GROUP, (g + 1) * GROUP)
                pre = _dot(diff[g].astype(BF16), wp_ref[g].astype(BF16)) + bp_ref[:, cols]
                cat_ref[:, D_A + g * GROUP:D_A + (g + 1) * GROUP] = (pre * ps_ref[:, cols]).astype(BF16)
            cat_keep[i % (ATTN_LAG + 1)] = cat_ref[...]

        @pl.when(i == 0)
        def _():
            copies.wait_recv(0, 1)
            keep[6].start()
            keep[7].start()

        @pl.when(i == 1)
        def _():
            wo_copies.wait_recv(1, 2, 3)
            wo_copies.start(4, 5, 6)

        @pl.when(i == ATTN_LAG)
        def _():
            wo_copies.wait_recv(0, 4, 5, 6)
            wo_keep.start()

        @pl.when(i >= ATTN_LAG)
        def _():
            xv = xb_ref[...]
            mix = _dot(cat_keep[(i - ATTN_LAG) % (ATTN_LAG + 1)], wout_scr[...])
            mix_ref[...] = mix
            x1v = xv + (mix * _rstd(mix)) * (gate1 * n1post_ref[...])
            x1_ref[...] = x1v
            shift2, scale2 = mod_ref[3:4, :], mod_ref[4:5, :]
            h2 = ((x1v * _rstd(x1v)) * (n2pre_ref[...] * (1.0 + scale2)) + shift2).astype(BF16)
            h2_ref[...] = h2
            for j, (w1, w2) in enumerate(((w1_ref, w2_ref), (sib1, sib2))):
                ra = jnp.maximum(_dot(h2, w1[...]), 0.0)
                r = (ra * ra).astype(BF16)
                r_ref[:, j * FF_BLK:(j + 1) * FF_BLK] = r
                if j == 0:
                    f_ref[...] = _dot(r, w2[...])
                else:
                    f_ref[...] += _dot(r, w2[...])

        @pl.when(i == nt - 1 + ATTN_LAG)
        def _():
            copies.wait_recv(6, 7, 8, 9)
            copies.wait_send(*range(10))
            wo_copies.wait_send(*range(7))
            wi_copies.wait_send(*range(7))
            ada.wait_send(*range(2 * (N_DEV - 1)))
            for cp in keep:
                cp.wait()
            wo_keep.wait()
            wi_keep.wait()

    first = lambda w: pl.BlockSpec((tt, w), lambda i: (jnp.minimum(i, nt - 1), 0))
    second = lambda w: pl.BlockSpec((tt, w), lambda i: (jnp.maximum(i - ATTN_LAG, 0), 0))
    r_head = pl.BlockSpec((tt, FC_HEAD * FF_BLK),
                          lambda i: (jnp.maximum(i - ATTN_LAG, 0), R_HEAD_COLS // (FC_HEAD * FF_BLK)))
    hbm = pl.BlockSpec(memory_space=pl.ANY)
    outs = pl.pallas_call(
        body, name="attn_fwd", grid=(nt + ATTN_LAG,),
        out_shape=tuple([jax.ShapeDtypeStruct((t_len, D_Z), F32), jax.ShapeDtypeStruct((t_len, D), BF16),
                         jax.ShapeDtypeStruct((t_len, D), F32), jax.ShapeDtypeStruct((t_len, D), F32),
                         jax.ShapeDtypeStruct((t_len, D), BF16),
                         jax.ShapeDtypeStruct((t_len, FC_EARLY * FF_BLK), BF16),
                         jax.ShapeDtypeStruct((t_len, D), F32)]
                        + [jax.ShapeDtypeStruct((8, D), F32), jax.ShapeDtypeStruct((N_DEV, D), F32)]
                        + [jax.ShapeDtypeStruct((FC_EARLY,) + s.shape, BF16) for s in fc_shards]
                        + [jax.ShapeDtypeStruct((D, D), BF16), jax.ShapeDtypeStruct((D_Z, D), BF16)]),
        in_specs=[first(D), second(D), _full((1, D)), _resident(w_ada.shape), _full((N_DEV, 1, ncol)), _full((1, D)),
                  _full((1, D)), _resident(w_in_shard.shape), _resident(w_out_shard.shape),
                  _full((N_HEADS, CHUNK, CHUNK)), _full((CHUNK, D_A)), _full((1, D_A)), _full((1, D_A)),
                  _full((len(WINDOWS), GROUP, GROUP)), _full((1, D_B)), _full((1, D_B)),
                  _resident(fc_shards[0].shape), _resident(fc_shards[1].shape), _full((1, D))],
        out_specs=(first(D_Z), first(D), second(D), second(D), second(D), r_head, second(D),
                   _full((8, D)), _full((N_DEV, D)), hbm, hbm, hbm, hbm),
        scratch_shapes=[pltpu.VMEM((HALO, D_B), F32),
                        pltpu.VMEM((2,) + fc_shards[0].shape, BF16), pltpu.VMEM((2,) + fc_shards[1].shape, BF16),
                        pltpu.VMEM(fc_shards[0].shape, BF16), pltpu.VMEM(fc_shards[1].shape, BF16),
                        pltpu.VMEM((ATTN_LAG + 1, tt, D), BF16), pltpu.VMEM((D, D), BF16),
                        pltpu.VMEM((D_Z, D), BF16), pltpu.VMEM((8, D), F32),
                        pltpu.VMEM((N_DEV, 8, D), F32), pltpu.VMEM((N_DEV, N_DEV, ncol), F32),
                        pltpu.VMEM((N_DEV, ncol), F32),
                        pltpu.SemaphoreType.DMA((10,)), pltpu.SemaphoreType.DMA((10,)),
                        pltpu.SemaphoreType.DMA((10,)),
                        pltpu.SemaphoreType.DMA((7,)), pltpu.SemaphoreType.DMA((7,)),
                        pltpu.SemaphoreType.DMA((7,)), pltpu.SemaphoreType.DMA((7,)),
                        pltpu.SemaphoreType.DMA((2 * (N_DEV - 1),)), pltpu.SemaphoreType.DMA((2 * (N_DEV - 1),))],
        compiler_params=pltpu.CompilerParams(dimension_semantics=("arbitrary",), vmem_limit_bytes=VMEM_LIMIT),
    )(x, x, c_row, w_ada, b_pieces, n1pre, n1post, w_in_shard, w_out_shard, w_sp, bs_rows, ln_g, ln_b, w_pool, b_pool,
      pool_scale, *fc_shards, n2pre)
    return outs[:9], outs[9:11], outs[11], outs[12]


def _mlp_fwd_early(tt, r_begun, h2, f_head, w1_early, w2_early):
    t_len = h2.shape[0]
    nt = t_len // tt
    n_late = N_DEV - FC_EARLY

    def body(r_begun_ref, h2_ref, fh_ref, w1_ref, w2_ref, r_ref, f_ref, l1_ref, l2_ref,
             land1, land2, send_sems, recv_sems, local_sems):
        i = pl.program_id(0)
        copies = _Copies(
            [(w1_ref.at[2], land1, 4), (w2_ref.at[4], land2, 2),
             (land1, l1_ref.at[1], 1), (land2, l2_ref.at[1], 1)],
            send_sems, recv_sems)
        keep = [pltpu.make_async_copy(land1, l1_ref.at[0], local_sems.at[0]),
                pltpu.make_async_copy(land2, l2_ref.at[0], local_sems.at[1])]

        @pl.when(i == 0)
        def _():
            copies.start(0, 1)

        @pl.when(i == nt - 1)
        def _():
            copies.wait_recv(0, 1)
            copies.start(2, 3)
            for cp in keep:
                cp.start()

        h2 = h2_ref[...]
        f_ref[...] = fh_ref[...]
        for j in range(FC_HEAD, FC_EARLY):
            ra = jnp.maximum(_dot(h2, w1_ref[j]), 0.0)
            r = (ra * ra).astype(BF16)
            r_ref[:, _early_col(j):_early_col(j) + FF_BLK] = r
            f_ref[...] += _dot(r, w2_ref[j])

        @pl.when(i == nt - 1)
        def _():
            copies.wait_recv(2, 3)
            copies.wait_send(0, 1, 2, 3)
            for cp in keep:
                cp.wait()

    tile = lambda w: pl.BlockSpec((tt, w), lambda i: (i, 0))
    hbm = pl.BlockSpec(memory_space=pl.ANY)
    outs = pl.pallas_call(
        body, name="mlp_fwd_early", grid=(nt,),
        out_shape=(jax.ShapeDtypeStruct((t_len, FC_EARLY * FF_BLK), BF16), jax.ShapeDtypeStruct((t_len, D), F32),
                   jax.ShapeDtypeStruct((n_late,) + w1_early.shape[1:], BF16),
                   jax.ShapeDtypeStruct((n_late,) + w2_early.shape[1:], BF16)),
        in_specs=[hbm, tile(D), tile(D), _resident((FC_EARLY, D, FF_BLK)), _resident((FC_EARLY, FF_BLK, D))],
        out_specs=(tile(R_HEAD_COLS), tile(D), hbm, hbm),
        input_output_aliases={0: 0},
        scratch_shapes=[pltpu.VMEM(w1_early.shape[1:], BF16), pltpu.VMEM(w2_early.shape[1:], BF16),
                        pltpu.SemaphoreType.DMA((4,)), pltpu.SemaphoreType.DMA((4,)),
                        pltpu.SemaphoreType.DMA((2,))],
        compiler_params=pltpu.CompilerParams(dimension_semantics=("arbitrary",), vmem_limit_bytes=VMEM_LIMIT),
    )(r_begun, h2, f_head, w1_early, w2_early)
    return outs[:2], outs[2:]


def _mlp_late_bwd(tt, r_early, x1, h2, f_early, tgt, mix, mod, n2pre, n2post, n1post,
                  w1_early, w2_early, w1_late, w2_late):
    t_len = x1.shape[0]
    nt = t_len // tt
    n_late = N_DEV - FC_EARLY
    late_cols = n_late * FF_BLK

    def body(re_ref, x1_ref, h2_ref, fe_ref, tgt_ref, mix_ref, mod_ref, n2pre_ref, n2post_ref,
             n1post_ref, w1e_ref, w2e_ref, w1l_ref, w2l_ref,
             rl_ref, df_ref, da_ref, dmix_ref, dx1_ref, redf_ref, redb_ref, dh2_acc):
        i = pl.program_id(0)

        @pl.when(i == 0)
        def _():
            redf_ref[...] = jnp.zeros_like(redf_ref)
            redb_ref[...] = jnp.zeros_like(redb_ref)

        x1v = x1_ref[...]
        gate1, scale2, gate2 = mod_ref[2:3, :], mod_ref[4:5, :], mod_ref[5:6, :]
        h2 = h2_ref[...]
        f = fe_ref[...]
        for j in range(n_late):
            cols = slice(j * FF_BLK, (j + 1) * FF_BLK)
            ra = jnp.maximum(_dot(h2, w1l_ref[j]), 0.0)
            r = (ra * ra).astype(BF16)
            rl_ref[:, cols] = r
            f = f + _dot(r, w2l_ref[j])
        post2 = n2post_ref[...]
        gate_post2 = gate2 * post2
        rf = _rstd(f)
        fhat = f * rf
        err = (x1v + fhat * gate_post2) - tgt_ref[...]
        dy = err * (1.0 / D)
        d_f, sum_f = _rms_bwd_gained(dy, gate_post2, fhat, rf)
        dfv = d_f.astype(BF16)
        df_ref[...] = dfv
        redf_ref[0:1, :] += post2 * sum_f
        redf_ref[1:2, :] += gate2 * sum_f
        redf_ref[2:3, :] += _colsum(err * err)

        for j in range(N_DEV):
            cols = slice(j * FF_BLK, (j + 1) * FF_BLK)
            if j < FC_EARLY:
                w1, w2, r = w1e_ref[j], w2e_ref[j], re_ref[:, _early_col(j):_early_col(j) + FF_BLK]
            else:
                jl = j - FC_EARLY
                w1, w2, r = w1l_ref[jl], w2l_ref[jl], rl_ref[:, jl * FF_BLK:(jl + 1) * FF_BLK]
            dr = _dot_nt(dfv, w2)
            da = (dr * (2.0 * jnp.sqrt(r.astype(F32)))).astype(BF16)
            da_ref[:, cols] = da
            contrib = _dot_nt(da, w1)
            if j == 0:
                dh2_acc[...] = contrib
            else:
                dh2_acc[...] += contrib
        dh2 = dh2_acc[...]
        pre2, post1 = n2pre_ref[...], n1post_ref[...]
        r2 = _rstd(x1v)
        xhat = x1v * r2
        d_x1, sum_h = _rms_bwd_gained(dh2, pre2 * (1.0 + scale2), xhat, r2)
        dx1 = dy + d_x1
        dx1_ref[...] = dx1
        mixv = mix_ref[...]
        rm = _rstd(mixv)
        mhat = mixv * rm
        d_mix, sum_m = _rms_bwd_gained(dx1, gate1 * post1, mhat, rm)
        dmix_ref[...] = d_mix.astype(BF16)
        redb_ref[0:1, :] += _colsum(dh2)
        redb_ref[1:2, :] += pre2 * sum_h
        redb_ref[2:3, :] += (1.0 + scale2) * sum_h
        redb_ref[3:4, :] += post1 * sum_m
        redb_ref[4:5, :] += gate1 * sum_m

    tile = lambda w: pl.BlockSpec((tt, w), lambda i: (i, 0))
    return pl.pallas_call(
        body, name="mlp_late_bwd", grid=(nt,),
        out_shape=(jax.ShapeDtypeStruct((t_len, late_cols), BF16), jax.ShapeDtypeStruct((t_len, D), BF16),
                   jax.ShapeDtypeStruct((t_len, D_FF), BF16), jax.ShapeDtypeStruct((t_len, D), BF16),
                   jax.ShapeDtypeStruct((t_len, D), F32), jax.ShapeDtypeStruct((8, D), F32),
                   jax.ShapeDtypeStruct((8, D), F32)),
        in_specs=[tile(FC_EARLY * FF_BLK), tile(D), tile(D), tile(D), tile(D),
                  tile(D), _full((8, D)), _full((1, D)), _full((1, D)), _full((1, D)),
                  _resident((FC_EARLY, D, FF_BLK)), _resident((FC_EARLY, FF_BLK, D)),
                  _resident((n_late, D, FF_BLK)), _resident((n_late, FF_BLK, D))],
        out_specs=(tile(late_cols), tile(D), tile(D_FF), tile(D), tile(D), _full((8, D)), _full((8, D))),
        scratch_shapes=[pltpu.VMEM((tt, D), F32)],
        compiler_params=pltpu.CompilerParams(dimension_semantics=("arbitrary",), vmem_limit_bytes=VMEM_LIMIT),
    )(r_early, x1, h2, f_early, tgt, mix, mod, n2pre, n2post, n1post, w1_early, w2_early, w1_late, w2_late)


def _mlp_wgrad(tt, r_early, r_late, da, df, h2):
    t_len = df.shape[0]
    nt = t_len // tt
    odd_steps = [j for j, rel in enumerate(WGRAD_ORDER) if rel % 2]

    def relation(j):
        rel = jnp.int32(WGRAD_ORDER[-1])
        for step in range(N_DEV - 2, -1, -1):
            rel = jnp.where(j == step, WGRAD_ORDER[step], rel)
        return rel

    def body(re_ref, rl_ref, da_ref, df_ref, h2_ref, own1_ref, own2_ref, out1_ref, out2_ref, diag1_ref, diag2_ref,
             acc1, acc2, snd1, snd2, sib1, sib2, dsnd1, dsnd2, send_sems, recv_sems):
        j, t = pl.program_id(0), pl.program_id(1)
        rows = pl.ds(pl.multiple_of(t * tt, tt), tt)
        x, y, c = _place()
        accs, snds, sibs = (acc1, acc2), (snd1, snd2), (sib1, sib2)
        dsnds, diags = (dsnd1, dsnd2), (diag1_ref, diag2_ref)

        def to_sibling(a, jj, buf=0):
            return pltpu.make_async_remote_copy(
                src_ref=snds[a].at[buf], dst_ref=sibs[a].at[jj],
                send_sem=send_sems.at[4 * a + jj], recv_sem=recv_sems.at[4 * a + jj],
                device_id=(x, y, 1 - c), device_id_type=MESH)

        def to_diagonal(a):
            return pltpu.make_async_remote_copy(
                src_ref=dsnds[a], dst_ref=diags[a], send_sem=send_sems.at[8 + a], recv_sem=recv_sems.at[8 + a],
                device_id=_peer(x, y, c, 6), device_id_type=MESH)

        @pl.when(t == 0)
        def _():
            acc2[...] = jnp.zeros_like(acc2)
            acc1[...] = jnp.zeros_like(acc1)

        for r_ref, mine in ((re_ref, relation(j) < FC_EARLY), (rl_ref, relation(j) >= FC_EARLY)):
            @pl.when(mine)
            def _():
                acc2[...] += _dot_tn(r_ref[...], df_ref[rows, :])
                acc1[...] += _dot_tn(h2_ref[rows, :], da_ref[...])

        for step, rel in enumerate(WGRAD_ORDER):
            jj = rel // 2

            @pl.when((t == nt - 1) & (j == step))
            def _():
                for a, (own_ref, out_ref) in enumerate(((own1_ref, out1_ref), (own2_ref, out2_ref))):
                    if rel % 2:
                        q = odd_steps.index(step)
                        if q >= 2:
                            to_sibling(a, WGRAD_ORDER[odd_steps[q - 2]] // 2).wait_send()
                        snds[a][q % 2] = accs[a][...].astype(BF16)
                        to_sibling(a, jj, q % 2).start()
                        continue
                    to_sibling(a, jj).wait_recv()
                    chip_sum = accs[a][...] + sibs[a][jj].astype(F32)
                    if rel == 6:
                        dsnds[a][...] = chip_sum.astype(BF16)
                        to_diagonal(a).start()
                    elif rel == 0:
                        own_ref[...] = chip_sum
                    else:
                        out_ref[0] = chip_sum.astype(BF16)
                    if step == N_DEV - 1:
                        for q in (2, 3):
                            to_sibling(a, WGRAD_ORDER[odd_steps[q]] // 2).wait_send()
                        to_diagonal(a).wait_recv()
                        to_diagonal(a).wait_send()

    assert WGRAD_ORDER[-1] == 0 and WGRAD_ORDER[-3:-1] == (2, 4)
    blk = pl.BlockSpec((tt, FF_BLK), lambda j, t: (t, relation(j)))
    early_block = lambda rel: jnp.where(rel < FC_HEAD, rel + FC_EARLY - FC_HEAD, rel - FC_HEAD)
    early = lambda j, t: (jnp.where(relation(j) < FC_EARLY, t, 0),
                          jnp.where(relation(j) < FC_EARLY, early_block(relation(j)), 0))
    late = lambda j, t: (jnp.where(relation(j) < FC_EARLY, 0, t), jnp.maximum(relation(j) - FC_EARLY, 0))
    chip = lambda j, t: (jnp.clip(j - 5, 0, 1), 0, 0)
    hbm = pl.BlockSpec(memory_space=pl.ANY)
    return pl.pallas_call(
        body, name="mlp_wgrad", grid=(N_DEV, nt),
        out_shape=(jax.ShapeDtypeStruct((D, FF_BLK), F32), jax.ShapeDtypeStruct((FF_BLK, D), F32),
                   jax.ShapeDtypeStruct((2, D, FF_BLK), BF16), jax.ShapeDtypeStruct((2, FF_BLK, D), BF16),
                   jax.ShapeDtypeStruct((D, FF_BLK), BF16), jax.ShapeDtypeStruct((FF_BLK, D), BF16)),
        in_specs=[pl.BlockSpec((tt, FF_BLK), early), pl.BlockSpec((tt, FF_BLK), late), blk,
                  _resident((t_len, D)), _resident((t_len, D))],
        out_specs=(_full((D, FF_BLK)), _full((FF_BLK, D)),
                   pl.BlockSpec((1, D, FF_BLK), chip), pl.BlockSpec((1, FF_BLK, D), chip), hbm, hbm),
        scratch_shapes=[pltpu.VMEM((D, FF_BLK), F32), pltpu.VMEM((FF_BLK, D), F32),
                        pltpu.VMEM((2, D, FF_BLK), BF16), pltpu.VMEM((2, FF_BLK, D), BF16),
                        pltpu.VMEM((4, D, FF_BLK), BF16), pltpu.VMEM((4, FF_BLK, D), BF16),
                        pltpu.VMEM((D, FF_BLK), BF16), pltpu.VMEM((FF_BLK, D), BF16),
                        pltpu.SemaphoreType.DMA((10,)), pltpu.SemaphoreType.DMA((10,))],
        compiler_params=pltpu.CompilerParams(dimension_semantics=("arbitrary", "arbitrary"),
                                             vmem_limit_bytes=VMEM_LIMIT),
    )(r_early, r_late, da, df, h2)


def _acc_rows(ref, row0, k, val):
    half = CHUNK // 2
    ref[row0:row0 + half, k * GROUP:(k + 1) * GROUP] += val[:half, :]
    ref[row0:row0 + half, D_A + k * GROUP:D_A + (k + 1) * GROUP] += val[half:, :]


def _attn_bwd(tt, dmix, x, z, cat, mod, n1pre, w_out, w_sp, bs_rows, ln_g, ln_b, w_pool, b_pool, pool_scale,
              red_fwd, red_bwd, chip_sums):
    t_len = z.shape[0]
    nt = t_len // tt
    hb = tt // HALO
    n_sums = len(chip_sums)

    def body(dmix_ref, x_ref, z_ref, zprev_ref, cat_ref, mod_ref, n1pre_ref, wout_ref, wsp_ref,
             bs_ref, lng_ref, lnb_ref, wp_ref, bp_ref, ps_ref, redf_ref, redb_ref, *rest):
        sum_out = rest[:n_sums]
        dz_ref, gwin_ref, gwout_ref, small_ref = rest[n_sums:n_sums + 4]
        sum_in = rest[n_sums + 4:2 * n_sums + 4]
        carry, acc_in, acc_out, dz_scr, bs_acc, send_sems, recv_sems = rest[2 * n_sums + 4:]
        s = pl.program_id(0)
        i = nt - 1 - s
        px, py, pc = _place()

        def chip_copy(a, r):
            return pltpu.make_async_remote_copy(
                src_ref=sum_out[a].at[r], dst_ref=sum_in[a].at[r],
                send_sem=send_sems.at[2 * a + r], recv_sem=recv_sems.at[2 * a + r],
                device_id=_peer(px, py, pc, 2 * (r + 1)), device_id_type=MESH)

        @pl.when(s == 0)
        def _():
            for a in range(n_sums):
                for r in range(2):
                    chip_copy(a, r).start()
            carry[...] = jnp.zeros_like(carry)
            acc_in[...] = jnp.zeros_like(acc_in)
            acc_out[...] = jnp.zeros_like(acc_out)
            bs_acc[...] = jnp.zeros_like(bs_acc)
            small_ref[...] = jnp.zeros_like(small_ref)
            small_ref[ROW_DMOD + 2:ROW_DMOD + 3, :] = redb_ref[3:4, :]
            small_ref[ROW_DMOD + 3:ROW_DMOD + 5, :] = redb_ref[0:2, :]
            small_ref[ROW_DMOD + 5:ROW_DMOD + 6, :] = redf_ref[0:1, :]
            small_ref[ROW_N1POST:ROW_N1POST + 1, :] = redb_ref[4:5, :]
            small_ref[ROW_N2PRE:ROW_N2PRE + 1, :] = redb_ref[2:3, :]
            small_ref[ROW_N2POST:ROW_N2POST + 1, :] = redf_ref[1:2, :]
            small_ref[ROW_LOSS:ROW_LOSS + 1, :] = redf_ref[2:3, :]

        dmixv = dmix_ref[...]
        dcat = _dot_nt(dmixv, wout_ref[...])
        acc_out[...] += _dot_tn(cat_ref[...], dmixv)

        z = z_ref[...]
        t_g, ga = _gelu_parts(z[:, :2 * D_A])
        u, vr = ga[:, :D_A], ga[:, D_A:]
        dv0 = vr - jnp.mean(vr, axis=-1, keepdims=True)
        rv = lax.rsqrt(jnp.mean(dv0 * dv0, axis=-1, keepdims=True) + EPS)
        vhat = dv0 * rv
        vb = (vhat * lng_ref[...] + lnb_ref[...]).astype(BF16)
        mask = _tril_mask()
        wc = [(wsp_ref[h] * mask).astype(BF16) for h in range(N_HEADS)]

        dya = dcat[:, :D_A]
        for h in range(N_HEADS):
            cols = slice(h * GROUP, (h + 1) * GROUP)
            bs_sum = jnp.zeros((CHUNK, GROUP), F32)
            ws_sum = jnp.zeros((CHUNK, CHUNK), F32)
            for ch in range(tt // CHUNK):
                rows = slice(ch * CHUNK, (ch + 1) * CHUNK)
                v_ch = vb[rows, cols]
                mixed = _dot(wc[h], v_ch) + bs_ref[:, cols]
                dy_ch = dya[rows, cols]
                dz_scr[rows, cols] = dy_ch * mixed
                dmixed = dy_ch * u[rows, cols]
                dmb = dmixed.astype(BF16)
                dz_scr[rows, D_A + h * GROUP:D_A + (h + 1) * GROUP] = _dot_tn(wc[h], dmb)
                bs_sum = bs_sum + dmixed
                ws_sum = ws_sum + _dot_nt(dmb, v_ch)
            _acc_rows(bs_acc, 0, h, bs_sum)
            _acc_rows(small_ref, ROW_WS, h, ws_sum)

        dvl = dz_scr[:, D_A:2 * D_A]
        dvhat = dvl * lng_ref[...]
        dvl_vhat = dvl * vhat
        dvr = rv * (dvhat - jnp.mean(dvhat, axis=-1, keepdims=True)
                    - vhat * jnp.mean(dvl_vhat * lng_ref[...], axis=-1, keepdims=True))
        small_ref[ROW_LN:ROW_LN + 1, 0:D_A] += _colsum(dvl_vhat)
        small_ref[ROW_LN:ROW_LN + 1, D_A:D] += _colsum(dvl)
        dga = jnp.concatenate([dz_scr[:, :D_A], dvr], axis=1)
        dza = dga * _gelu_grad(z[:, :2 * D_A], t_g)

        zb = z[:, 2 * D_A:]
        halo_prev = jnp.where(i == 0, 0.0, zprev_ref[...])
        diff = _pool_diff(zb, halo_prev, i * tt)
        dyb = dcat[:, D_A:]
        inv = _inv_counts(i * tt, tt)
        scaled, ddiffs = [], []
        for g in range(len(WINDOWS)):
            cols = slice(g * GROUP, (g + 1) * GROUP)
            db = diff[g].astype(BF16)
            wpg = wp_ref[g].astype(BF16)
            pre = _dot(db, wpg) + bp_ref[:, cols]
            small_ref[ROW_POOL:ROW_POOL + 1, cols] += _colsum(dyb[:, cols] * pre)
            dpre = dyb[:, cols] * ps_ref[:, cols]
            small_ref[ROW_POOL:ROW_POOL + 1, D_B + g * GROUP:D_B + (g + 1) * GROUP] += _colsum(dpre)
            dpb = dpre.astype(BF16)
            _acc_rows(small_ref, ROW_WP, g, _dot_tn(db, dpb))
            ddiff = _dot_nt(dpb, wpg)
            ddiffs.append(ddiff)
            scaled.append(ddiff * inv[g])
        scaled_all = jnp.concatenate(scaled, axis=1)
        ext = jnp.concatenate([scaled_all, carry[...]], axis=0)
        n_ext = tt + HALO
        s2 = ext + pltpu.roll(ext, n_ext - 1, 0)
        t4 = s2[:, GROUP:]
        s4 = t4 + pltpu.roll(t4, n_ext - 2, 0)
        t8 = s4[:, GROUP:]
        s8 = t8 + pltpu.roll(t8, n_ext - 4, 0)
        t16 = s8[:, GROUP:]
        s16 = t16 + pltpu.roll(t16, n_ext - 8, 0)
        back = [s2[:, :GROUP], s4[:, :GROUP], s8[:, :GROUP], s16]
        carry[...] = scaled_all[:HALO, :]
        dzb = jnp.concatenate([back[g][:tt, :] - ddiffs[g] for g in range(len(WINDOWS))], axis=1)

        dzv = jnp.concatenate([dza, dzb], axis=1).astype(BF16)
        dz_ref[...] = dzv
        xv = x_ref[...]
        h1 = (xv * _rstd(xv) * (n1pre_ref[...] * (1.0 + mod_ref[1:2, :])) + mod_ref[0:1, :]).astype(BF16)
        acc_in[...] += _dot_tn(dzv, h1)

        @pl.when(s == nt - 1)
        def _():
            gwin_ref[...] = acc_in[...].astype(BF16)
            gwout_ref[...] = acc_out[...].astype(BF16)
            bs = _unfold(bs_acc[...])
            for h in range(N_HEADS):
                small_ref[ROW_BS + h:ROW_BS + h + 1, 0:GROUP] = jnp.sum(
                    bs[:, h * GROUP:(h + 1) * GROUP].T, axis=0, keepdims=True)
            for a in range(n_sums):
                for r in range(2):
                    chip_copy(a, r).wait_recv()
                    chip_copy(a, r).wait_send()

    rev = lambda w: pl.BlockSpec((tt, w), lambda s: (nt - 1 - s, 0))
    zprev = pl.BlockSpec((HALO, D_B), lambda s: (jnp.maximum((nt - 1 - s) * hb - 1, 0), 2))
    hbm = pl.BlockSpec(memory_space=pl.ANY)
    outs = pl.pallas_call(
        body, name="attn_bwd", grid=(nt,),
        out_shape=tuple([jax.ShapeDtypeStruct((t_len, D_Z), BF16), jax.ShapeDtypeStruct((D_Z, D), BF16),
                         jax.ShapeDtypeStruct((D, D), BF16), jax.ShapeDtypeStruct((SMALL_ROWS, D), F32)]
                        + [jax.ShapeDtypeStruct(cs.shape, cs.dtype) for cs in chip_sums]),
        in_specs=[rev(D), rev(D), rev(D_Z), zprev, rev(D), _full((8, D)), _full((1, D)),
                  _resident((D, D)), _full((N_HEADS, CHUNK, CHUNK)), _full((CHUNK, D_A)),
                  _full((1, D_A)), _full((1, D_A)), _full((len(WINDOWS), GROUP, GROUP)), _full((1, D_B)),
                  _full((1, D_B)), _full((8, D)), _full((8, D))] + [_resident(cs.shape) for cs in chip_sums],
        out_specs=tuple([rev(D_Z), _resident((D_Z, D)), _resident((D, D)), _full((SMALL_ROWS, D))]
                        + [hbm] * n_sums),
        scratch_shapes=[pltpu.VMEM((HALO, D_B), F32), pltpu.VMEM((D_Z, D), F32), pltpu.VMEM((D, D), F32),
                        pltpu.VMEM((tt, 2 * D_A), F32), pltpu.VMEM((CHUNK // 2, D), F32),
                        pltpu.SemaphoreType.DMA((2 * n_sums,)), pltpu.SemaphoreType.DMA((2 * n_sums,))],
        compiler_params=pltpu.CompilerParams(dimension_semantics=("arbitrary",), vmem_limit_bytes=VMEM_LIMIT),
    )(dmix, x, z, z, cat, mod, n1pre, w_out, w_sp, bs_rows, ln_g, ln_b, w_pool, b_pool, pool_scale,
      red_fwd, red_bwd, *chip_sums)
    return outs[:4], outs[4:]


def _adam(w, g, m, v):
    m2 = ADAM_B1 * m + (1.0 - ADAM_B1) * g
    v2 = ADAM_B2 * v + (1.0 - ADAM_B2) * (g * g)
    m_hat = m2 / (1.0 - ADAM_B1 ** ADAM_STEP)
    v_hat = v2 / (1.0 - ADAM_B2 ** ADAM_STEP)
    delta = -ADAM_LR * (m_hat / (jnp.sqrt(v_hat) + ADAM_EPS) + ADAM_WD * w)
    return delta, m2, v2


def _adamw_fc(steps, fc):
    n_fc = len(fc)

    def body(*refs):
        ins, outs = refs[:6 * n_fc], refs[6 * n_fc:]
        for k in range(n_fc):
            w_ref, own_ref, arr_ref, diag_ref, m_ref, v_ref = ins[6 * k:6 * k + 6]
            g = ((own_ref[...] + arr_ref[0].astype(F32)) + arr_ref[1].astype(F32)) + diag_ref[...].astype(F32)
            outs[4 * k][...] = g
            outs[4 * k + 1][...], outs[4 * k + 2][...], outs[4 * k + 3][...] = _adam(
                w_ref[...], g, m_ref[...], v_ref[...])

    specs_in, specs_out, shapes, args = [], [], [], []
    for w, own, arrived, diagonal, m, v in fc:
        rows, cols = w.shape
        blk = pl.BlockSpec((rows // steps, cols), lambda i: (i, 0))
        specs_in += [blk, blk, pl.BlockSpec((2, rows // steps, cols), lambda i: (0, i, 0)), blk, blk, blk]
        specs_out += [blk] * 4
        shapes += [jax.ShapeDtypeStruct((rows, cols), F32)] * 4
        args += [w, own, arrived, diagonal, m, v]
    outs = pl.pallas_call(
        body, name="adamw_fc", grid=(steps,), out_shape=tuple(shapes), in_specs=specs_in, out_specs=tuple(specs_out),
        compiler_params=pltpu.CompilerParams(dimension_semantics=("arbitrary",), vmem_limit_bytes=VMEM_LIMIT),
    )(*args)
    return [outs[4 * k:4 * k + 4] for k in range(n_fc)]


def _adamw_ada(rb, w, sc, total, m, v):
    rows, cols = w.shape

    def body(w_ref, sc_ref, t_ref, m_ref, v_ref, g_ref, d_ref, m2_ref, v2_ref, dm):
        me = _index(_place())
        for dev in range(N_DEV):
            @pl.when((pl.program_id(0) == 0) & (me == dev))
            def _():
                for b in range(N_DEV):
                    for k in range(6):
                        lo, hi = max(cols * dev, D * k), min(cols * (dev + 1), D * (k + 1))
                        if lo < hi:
                            dm[b:b + 1, lo - cols * dev:hi - cols * dev] = t_ref[
                                _table_row(b) + k:_table_row(b) + k + 1, lo - D * k:hi - D * k]

        g = _dot_tn(sc_ref[...].astype(BF16), dm[...].astype(BF16))
        g_ref[...] = g
        d_ref[...], m2_ref[...], v2_ref[...] = _adam(w_ref[...], g, m_ref[...], v_ref[...])

    blk = pl.BlockSpec((rb, cols), lambda i: (i, 0))
    shp = jax.ShapeDtypeStruct((rows, cols), F32)
    return pl.pallas_call(
        body, name="adamw_ada", grid=(rows // rb,), out_shape=(shp, shp, shp, shp),
        in_specs=[blk, pl.BlockSpec((N_DEV, rb), lambda i: (0, i)), _full(total.shape), blk, blk],
        out_specs=(blk, blk, blk, blk),
        scratch_shapes=[pltpu.VMEM((N_DEV, cols), F32)],
        compiler_params=pltpu.CompilerParams(dimension_semantics=("arbitrary",)),
    )(w, sc, total, m, v)


def _unfold(acc_rows):
    return jnp.concatenate([acc_rows[:, :D_A], acc_rows[:, D_A:]], axis=0)


def _adamw_small(total, params, shards):
    n = len(params)
    flat = [a for p in params for a in p]

    def body(*refs):
        s_ref = refs[0]
        p_refs = refs[1:1 + 3 * n]
        s_refs = refs[1 + 3 * n:1 + 3 * n + 4 * len(shards)]
        loss_ref = refs[1 + 3 * n + 4 * len(shards)]
        o_refs = refs[2 + 3 * n + 4 * len(shards):2 + 3 * n + 4 * len(shards) + 4 * n]
        so_refs = refs[2 + 3 * n + 4 * len(shards) + 4 * n:]
        d_b_ada = s_ref[0:6, :]
        for b in range(1, N_DEV):
            d_b_ada = d_b_ada + s_ref[_table_row(b):_table_row(b) + 6, :]
        misc = lambda r: s_ref[PK_MISC + r - ROW_N1PRE:PK_MISC + r - ROW_N1PRE + 1, :]
        loss = jnp.sum(misc(ROW_LOSS), axis=-1, keepdims=True) * (0.5 / D)
        loss_ref[...] = loss
        mask = _tril_mask()
        ws = _unfold(s_ref[PK_WS:PK_WS + 64, :])
        wp = _unfold(s_ref[PK_WP:PK_WP + 64, :])
        grads = [
            d_b_ada,
            misc(ROW_N1PRE), misc(ROW_N1POST), misc(ROW_N2PRE), misc(ROW_N2POST),
            misc(ROW_LN)[:, :D_A], misc(ROW_LN)[:, D_A:],
            misc(ROW_POOL)[:, :D_B], misc(ROW_POOL)[:, D_B:],
            s_ref[PK_BS:PK_BS + N_HEADS, 0:GROUP],
            jnp.stack([ws[:, h * GROUP:(h + 1) * GROUP] * mask for h in range(N_HEADS)]),
            jnp.stack([wp[:, g * GROUP:(g + 1) * GROUP] for g in range(len(WINDOWS))]),
        ]
        for k in range(n):
            w_ref, m_ref, v_ref = p_refs[3 * k:3 * k + 3]
            g = grads[k]
            if k == 0:
                for j in range(6):
                    o_refs[0][j] = g[j:j + 1, :]
                    o_refs[1][j], o_refs[2][j], o_refs[3][j] = _adam(w_ref[j], g[j:j + 1, :], m_ref[j], v_ref[j])
                continue
            o_refs[4 * k][...] = g
            o_refs[4 * k + 1][...], o_refs[4 * k + 2][...], o_refs[4 * k + 3][...] = _adam(
                w_ref[...], g, m_ref[...], v_ref[...])
        for k in range(len(shards)):
            w_ref, g_ref, m_ref, v_ref = s_refs[4 * k:4 * k + 4]
            so_refs[3 * k][...], so_refs[3 * k + 1][...], so_refs[3 * k + 2][...] = _adam(
                w_ref[...], g_ref[...], m_ref[...], v_ref[...])

    vm = pl.BlockSpec(memory_space=pltpu.VMEM)
    out_shape = [jax.ShapeDtypeStruct((1, 1), F32)]
    for w, _, _ in params:
        out_shape += [jax.ShapeDtypeStruct(w.shape, F32)] * 4
    for w, _, _, _ in shards:
        out_shape += [jax.ShapeDtypeStruct(w.shape, F32)] * 3
    return pl.pallas_call(
        body, name="adamw_small", out_shape=tuple(out_shape),
        in_specs=[vm] * (1 + 3 * n + 4 * len(shards)), out_specs=tuple([vm] * len(out_shape)),
        compiler_params=pltpu.CompilerParams(vmem_limit_bytes=VMEM_LIMIT),
    )(total, *flat, *[a for s in shards for a in s])


TT_ATTN_FWD = 512
ATTN_LAG = 2
TT_MLP_FWD = 512
TT_MLP = 256
TT_WGRAD = 2048
TT_ATTN_BWD = 512
TT_TAIL = 512


def kernel(x, c, w_ada, b_ada, norm1_pre, norm1_post, w_in, w_spatial, b_spatial, ln_v_gain, ln_v_bias, w_pool, b_pool, pool_scale, w_out, norm2_pre, norm2_post, w_fc1, w_fc2, loss_target, m_w_ada, m_b_ada, m_norm1_pre, m_norm1_post, m_w_in, m_w_spatial, m_b_spatial, m_ln_v_gain, m_ln_v_bias, m_w_pool, m_b_pool, m_pool_scale, m_w_out, m_norm2_pre, m_norm2_post, m_w_fc1, m_w_fc2, v_w_ada, v_b_ada, v_norm1_pre, v_norm1_post, v_w_in, v_w_spatial, v_b_spatial, v_ln_v_gain, v_ln_v_bias, v_w_pool, v_b_pool, v_pool_scale, v_w_out, v_norm2_pre, v_norm2_post, v_w_fc1, v_w_fc2):
    t_len = x.shape[1]
    ada_cols = w_ada.shape[1]
    tt = lambda want: min(want, t_len)

    x2 = x.reshape(t_len, D)
    tgt = loss_target.reshape(t_len, D)
    row = lambda a: a.reshape(1, -1)

    w_in_shard, w_out_shard, w1_shard, w2_shard = _cast_shards([w_in.T, w_out, w_fc1, w_fc2])

    bs_rows = jnp.repeat(b_spatial.T, GROUP, axis=1)
    attn_consts = (w_spatial, bs_rows, row(ln_v_gain), row(ln_v_bias), w_pool, row(b_pool), row(pool_scale))

    (z, cat, mix, x1, h2, r_begun, f_head, mod, sc), (w1_early, w2_early), w_out_all, w_in_t = _attn_fwd(
        tt(TT_ATTN_FWD), x2, c.reshape(1, D), w_ada, b_ada.reshape(N_DEV, 1, ada_cols), row(norm1_pre),
        row(norm1_post),
        w_in_shard, w_out_shard, *attn_consts, (w1_shard, w2_shard), row(norm2_pre))
    (r_early, f_early), (w1_late, w2_late) = _mlp_fwd_early(
        tt(TT_MLP_FWD), r_begun, h2, f_head, w1_early, w2_early)
    r_late, df, da, dmix, dx1, red_fwd, red_bwd = _mlp_late_bwd(
        tt(TT_MLP), r_early, x1, h2, f_early, tgt, mix, mod, row(norm2_pre), row(norm2_post), row(norm1_post),
        w1_early, w2_early, w1_late, w2_late)
    own_w1, own_w2, sums_w1, sums_w2, diag_w1, diag_w2 = _mlp_wgrad(tt(TT_WGRAD), r_early, r_late, da, df, h2)
    (dz, p_in, p_out, small), (arr_w1, arr_w2) = _attn_bwd(
        tt(TT_ATTN_BWD), dmix, x2, z, cat, mod, row(norm1_pre), w_out_all, *attn_consts, red_fwd, red_bwd,
        [sums_w1, sums_w2])
    (grad_w1, d_w1, m_w1, v_w1), (grad_w2, d_w2, m_w2, v_w2) = _adamw_fc(
        4, [(w_fc1, own_w1, arr_w1, diag_w1, m_w_fc1, v_w_fc1), (w_fc2, own_w2, arr_w2, diag_w2, m_w_fc2, v_w_fc2)])
    grad_in_t, grad_out, total, grad_x = _tail_comm(
        [p_in.reshape(N_DEV, D_Z // N_DEV, D), p_out.reshape(N_DEV, D // N_DEV, D)], small, 64,
        tt(TT_TAIL), dz, dx1, x2, mod, row(norm1_pre), w_in_t)

    grad_ada, d_ada, m_ada, v_ada = _adamw_ada(256, w_ada, sc, total, m_w_ada, v_w_ada)

    six = lambda a: a.reshape(6, 1, D)
    small_params = [
        (six(b_ada), six(m_b_ada), six(v_b_ada)),
        (row(norm1_pre), row(m_norm1_pre), row(v_norm1_pre)),
        (row(norm1_post), row(m_norm1_post), row(v_norm1_post)),
        (row(norm2_pre), row(m_norm2_pre), row(v_norm2_pre)),
        (row(norm2_post), row(m_norm2_post), row(v_norm2_post)),
        (row(ln_v_gain), row(m_ln_v_gain), row(v_ln_v_gain)),
        (row(ln_v_bias), row(m_ln_v_bias), row(v_ln_v_bias)),
        (row(pool_scale), row(m_pool_scale), row(v_pool_scale)),
        (row(b_pool), row(m_b_pool), row(v_b_pool)),
        (b_spatial, m_b_spatial, v_b_spatial),
        (w_spatial, m_w_spatial, v_w_spatial),
        (w_pool, m_w_pool, v_w_pool),
    ]
    outs = _adamw_small(total, small_params, [(w_out, grad_out, m_w_out, v_w_out),
                                              (w_in.T, grad_in_t, m_w_in.T, v_w_in.T)])
    d_out, m_out, v_out, d_in_t, m_in_t, v_in_t = outs[1 + 4 * len(small_params):]
    loss = outs[0].reshape(())
    names = ["b_ada", "norm1_pre", "norm1_post", "norm2_pre", "norm2_post", "ln_v_gain", "ln_v_bias", "pool_scale",
             "b_pool", "b_spatial", "w_spatial", "w_pool"]
    shapes = dict(b_ada=b_ada.shape, norm1_pre=norm1_pre.shape, norm1_post=norm1_post.shape,
                  norm2_pre=norm2_pre.shape, norm2_post=norm2_post.shape, ln_v_gain=ln_v_gain.shape,
                  ln_v_bias=ln_v_bias.shape, pool_scale=pool_scale.shape, b_pool=b_pool.shape,
                  b_spatial=b_spatial.shape, w_spatial=w_spatial.shape, w_pool=w_pool.shape)
    res = {}
    for k, nm in enumerate(names):
        res[nm] = tuple(o.reshape(shapes[nm]) for o in outs[1 + 4 * k:5 + 4 * k])
    res["w_ada"] = (grad_ada, d_ada, m_ada, v_ada)
    res["w_in"] = (grad_in_t.T, d_in_t.T, m_in_t.T, v_in_t.T)
    res["w_out"] = (grad_out, d_out, m_out, v_out)
    res["w_fc1"] = (grad_w1, d_w1, m_w1, v_w1)
    res["w_fc2"] = (grad_w2, d_w2, m_w2, v_w2)

    order = ["w_ada", "b_ada", "norm1_pre", "norm1_post", "w_in", "w_spatial", "b_spatial", "ln_v_gain", "ln_v_bias",
             "w_pool", "b_pool", "pool_scale", "w_out", "norm2_pre", "norm2_post", "w_fc1", "w_fc2"]
    return (loss, grad_x.reshape(x.shape),
            *[res[nm][0] for nm in order], *[res[nm][1] for nm in order],
            *[res[nm][2] for nm in order], *[res[nm][3] for nm in order])
```

```python
import functools

import jax
import jax.numpy as jnp
from jax import lax
from jax.experimental import pallas as pl
from jax.experimental.pallas import tpu as pltpu

F32 = jnp.float32
BF16 = jnp.bfloat16
MESH = pl.DeviceIdType.MESH

N_DEV = 8
D = 1024
D_A = 512
D_B = 512
D_Z = 2 * D_A + D_B
N_HEADS = 4
CHUNK = 128
WINDOWS = (2, 4, 8, 16)
GROUP = 128
D_FF = 4096
FF_BLK = D_FF // N_DEV
HALO = 16
EPS = 1e-6
VMEM_LIMIT = 60 * 1024 * 1024

ADAM_LR = 0.001
ADAM_B1 = 0.9
ADAM_B2 = 0.999
ADAM_EPS = 1e-08
ADAM_WD = 0.01
ADAM_STEP = 10

ROW_DMOD = 0
ROW_N1PRE, ROW_N1POST, ROW_N2PRE, ROW_N2POST = 8, 9, 10, 11
ROW_LN = 12
ROW_POOL = 13
ROW_LOSS = 14
ROW_BS = 16
ROW_WS = 24
ROW_WP = 88
SMALL_ROWS = 152
PACK_FINE = 40
PACK_HALF = PACK_FINE + 64
PACK_ROWS = 2 * PACK_HALF
PK_WS = PACK_FINE
PK_TABLE_B = PACK_HALF
PK_MISC = PK_TABLE_B + 24
PK_BS = PK_MISC + 8
PK_WP = PK_BS + 8


def _table_row(b):
    if isinstance(b, int):
        return 8 * b if 8 * b < PACK_FINE else 8 * b + PK_TABLE_B - PACK_FINE
    return 8 * b + jnp.where(8 * b < PACK_FINE, 0, PK_TABLE_B - PACK_FINE)


def _dot(a, b):
    return jnp.dot(a, b, preferred_element_type=F32)


def _dot_nt(a, b):
    return lax.dot_general(a, b, (((1,), (1,)), ((), ())), preferred_element_type=F32)


def _dot_tn(a, b):
    return lax.dot_general(a, b, (((0,), (0,)), ((), ())), preferred_element_type=F32)


def _rstd(v):
    return lax.rsqrt(jnp.mean(v * v, axis=-1, keepdims=True) + EPS)


def _rms_bwd(d_hat, hat, rstd):
    return rstd * (d_hat - hat * jnp.mean(d_hat * hat, axis=-1, keepdims=True))


def _rms_bwd_gained(g, gain, hat, rstd):
    g_hat = g * hat
    d_v = rstd * (g * gain - hat * jnp.mean(g_hat * gain, axis=-1, keepdims=True))
    return d_v, _colsum(g_hat)


_K0 = 0.7978845608028654
_K1 = 0.044715


def _gelu_parts(v):
    t = jnp.tanh(v * (_K0 + (_K0 * _K1) * (v * v)))
    return t, v * (0.5 + 0.5 * t)


def _gelu_grad(v, t):
    return (0.5 + 0.5 * t) + (0.5 * v) * (1.0 - t * t) * (_K0 + (3.0 * _K0 * _K1) * (v * v))


def _colsum(v):
    return jnp.sum(v, axis=0, keepdims=True)


def _full(shape):
    n = len(shape)
    return pl.BlockSpec(shape, lambda *_: (0,) * n)


def _resident(shape):
    n = len(shape)
    return pl.BlockSpec(shape, lambda *_: (0,) * n, pipeline_mode=pl.Buffered(1))


def _place():
    x, y, c = lax.axis_index("x"), lax.axis_index("y"), lax.axis_index("c")
    return x, y, c


def _flip(v, bit):
    return 1 - v if bit else v


def _peer(x, y, c, k):
    return (_flip(x, (k >> 2) & 1), _flip(y, (k >> 1) & 1), _flip(c, k & 1))


def _index(p):
    return 4 * p[0] + 2 * p[1] + p[2]


def _cast_shards(shards):
    def body(*refs):
        for src, dst in zip(refs[:len(shards)], refs[len(shards):]):
            dst[...] = src[...].astype(BF16)

    vm = pl.BlockSpec(memory_space=pltpu.VMEM)
    return pl.pallas_call(
        body, name="cast_shards", out_shape=tuple(jax.ShapeDtypeStruct(s.shape, BF16) for s in shards),
        in_specs=[vm] * len(shards), out_specs=tuple([vm] * len(shards)),
    )(*shards)


FC_EARLY = 6
FC_HEAD = 2
R_HEAD_COLS = (FC_EARLY - FC_HEAD) * FF_BLK
WGRAD_ORDER = (7, 6, 1, 3, 5, 2, 4, 0)


def _early_col(j):
    return R_HEAD_COLS + j * FF_BLK if j < FC_HEAD else (j - FC_HEAD) * FF_BLK


class _Copies:
    def __init__(self, entries, send_sems, recv_sems):
        self.place = _place()
        self.entries, self.send_sems, self.recv_sems = entries, send_sems, recv_sems

    def _copy(self, i, arrival=False):
        src, dst, rel = self.entries[i]
        return pltpu.make_async_remote_copy(
            src_ref=dst if arrival else src, dst_ref=dst, send_sem=self.send_sems.at[i],
            recv_sem=self.recv_sems.at[i], device_id=_peer(*self.place, rel), device_id_type=MESH)

    def start(self, *which):
        for i in which:
            self._copy(i).start()

    def wait_recv(self, *which):
        for i in which:
            self._copy(i, arrival=True).wait_recv()

    def wait_send(self, *which):
        for i in which:
            self._copy(i).wait_send()


def _tail_comm(parts, small, row_chunk, tt, dz, dx1, x, mod, n1pre, w_in_t):
    n = len(parts)
    t_len = x.shape[0]
    nt = t_len // tt

    def body(*refs):
        p_refs, small_ref = refs[:n], refs[n]
        dz_ref, dx1_ref, x_ref, mod_ref, n1pre_ref, win_ref = refs[n + 1:n + 7]
        outs = refs[n + 7:]
        g_refs, total_ref, gx_ref = outs[:n], outs[n], outs[n + 1]
        scr = outs[n + 2:]
        from_sib = scr[0:n]
        chip_out = scr[n:2 * n]
        chip_in = scr[2 * n:3 * n]
        pack, pack_sib, fine, bulk, total_scr, head = scr[3 * n:3 * n + 6]
        send_a, recv_a, send_b, recv_b, send_s, recv_s = scr[3 * n + 6:]
        step = pl.program_id(0)
        x, y, c = _place()
        me = _index((x, y, c))
        sibling = (x, y, 1 - c)
        my_chip = 2 * x + y
        others = [(1 - x, y), (x, 1 - y), (1 - x, 1 - y)]
        my_half = pl.ds(pl.multiple_of(PACK_HALF * c, 8), PACK_HALF)

        def pack_to_sibling():
            return pltpu.make_async_remote_copy(
                src_ref=pack, dst_ref=pack_sib, send_sem=send_s.at[0], recv_sem=recv_s.at[0],
                device_id=sibling, device_id_type=MESH)

        def half_to_chip(r, part):
            buf = (fine, bulk)[part]
            return pltpu.make_async_remote_copy(
                src_ref=buf.at[my_chip], dst_ref=buf.at[my_chip],
                send_sem=send_s.at[1 + 3 * part + r], recv_sem=recv_s.at[1 + 3 * part + r],
                device_id=(*others[r], c), device_id_type=MESH)

        def half_from_chip(r, part):
            k = 2 * others[r][0] + others[r][1]
            buf = (fine, bulk)[part]
            return pltpu.make_async_remote_copy(
                src_ref=buf.at[k], dst_ref=buf.at[k],
                send_sem=send_s.at[1 + 3 * part + r], recv_sem=recv_s.at[1 + 3 * part + r],
                device_id=(*others[r], c), device_id_type=MESH)

        def total_to_sibling():
            return pltpu.make_async_remote_copy(
                src_ref=total_scr.at[my_half], dst_ref=total_scr.at[my_half],
                send_sem=send_s.at[7], recv_sem=recv_s.at[7], device_id=sibling, device_id_type=MESH)

        def total_from_sibling():
            sib_half = pl.ds(pl.multiple_of(PACK_HALF * (1 - c), 8), PACK_HALF)
            return pltpu.make_async_remote_copy(
                src_ref=total_scr.at[sib_half], dst_ref=total_scr.at[sib_half],
                send_sem=send_s.at[7], recv_sem=recv_s.at[7], device_id=sibling, device_id_type=MESH)

        def to_sibling(a, k):
            return pltpu.make_async_remote_copy(
                src_ref=p_refs[a].at[2 * k + (1 - c)], dst_ref=from_sib[a].at[k],
                send_sem=send_a.at[a], recv_sem=recv_a.at[a], device_id=sibling, device_id_type=MESH)

        def all_from_sibling(a):
            return pltpu.make_async_remote_copy(
                src_ref=from_sib[a], dst_ref=from_sib[a], send_sem=send_a.at[a], recv_sem=recv_a.at[a],
                device_id=sibling, device_id_type=MESH)

        def to_chip(a, r):
            return pltpu.make_async_remote_copy(
                src_ref=chip_out[a].at[r], dst_ref=chip_in[a].at[r],
                send_sem=send_b.at[3 * a + r], recv_sem=recv_b.at[3 * a + r],
                device_id=(*others[r], c), device_id_type=MESH)

        @pl.when(step == 0)
        def _():
            head[...] = jnp.zeros_like(head)
            for a in range(n):
                for k in range(4):
                    to_sibling(a, k).start()

        @pl.when(step == min(1, nt - 1))
        def _():
            for a in range(n):
                all_from_sibling(a).wait_recv()
                rows = p_refs[a].shape[1]
                for r in range(3):
                    k = 2 * others[r][0] + others[r][1]
                    for s in range(0, rows, row_chunk):
                        sl = pl.ds(s, row_chunk)
                        chip_out[a][r, sl, :] = (p_refs[a][2 * k + c, sl, :].astype(F32)
                                                 + from_sib[a][k, sl, :].astype(F32)).astype(BF16)
                    to_chip(a, r).start()
                for s in range(0, rows, row_chunk):
                    sl = pl.ds(s, row_chunk)
                    g_refs[a][sl, :] = (p_refs[a][2 * my_chip + c, sl, :].astype(F32)
                                        + from_sib[a][my_chip, sl, :].astype(F32))

        dh1 = _dot(dz_ref[...], win_ref[...])
        xv = x_ref[...]
        r1 = _rstd(xv)
        xhat = xv * r1
        scale1 = mod_ref[1:2, :]
        pre1 = n1pre_ref[...]
        d_x, sum_h = _rms_bwd_gained(dh1, pre1 * (1.0 + scale1), xhat, r1)
        gx_ref[...] = dx1_ref[...] + d_x
        head[ROW_DMOD:ROW_DMOD + 1, :] += _colsum(dh1)
        head[ROW_DMOD + 1:ROW_DMOD + 2, :] += pre1 * sum_h
        head[ROW_N1PRE:ROW_N1PRE + 1, :] += (1.0 + scale1) * sum_h

        @pl.when(step == nt - 1)
        def _():
            pack[0:PACK_FINE, :] = jnp.zeros((PACK_FINE, D), F32)
            pack[PK_TABLE_B:PK_MISC, :] = jnp.zeros((PK_MISC - PK_TABLE_B, D), F32)
            pack[pl.ds(pl.multiple_of(_table_row(me), 8), 8), :] = small_ref[0:8, :] + head[0:8, :]
            pack[PK_WS:PK_WS + 64, :] = small_ref[ROW_WS:ROW_WS + 64, :]
            pack[PK_MISC:PK_MISC + 8, :] = small_ref[ROW_N1PRE:ROW_N1PRE + 8, :] + head[8:16, :]
            pack[PK_BS:PK_BS + 8, :] = small_ref[ROW_BS:ROW_BS + 8, :]
            pack[PK_WP:PK_WP + 64, :] = small_ref[ROW_WP:ROW_WP + 64, :]
            pack_to_sibling().start()
            pack_to_sibling().wait_recv()
            chip_sum = pack[my_half, :] + pack_sib[my_half, :]
            fine[my_chip] = chip_sum[:PACK_FINE, :]
            bulk[my_chip] = chip_sum[PACK_FINE:, :].astype(BF16)
            for r in range(3):
                half_to_chip(r, 0).start()
                half_to_chip(r, 1).start()
            for a in range(n):
                rows = p_refs[a].shape[1]
                for r in range(3):
                    to_chip(a, r).wait_recv()
                    for s in range(0, rows, row_chunk):
                        sl = pl.ds(s, row_chunk)
                        g_refs[a][sl, :] = g_refs[a][sl, :] + chip_in[a][r, sl, :].astype(F32)
            for r in range(3):
                half_from_chip(r, 0).wait_recv()
                half_from_chip(r, 1).wait_recv()
            half_start = pl.multiple_of(PACK_HALF * c, 8)
            total_scr[pl.ds(half_start, PACK_FINE), :] = ((fine[0] + fine[1]) + fine[2]) + fine[3]
            total_scr[pl.ds(half_start + PACK_FINE, PACK_HALF - PACK_FINE), :] = (
                (bulk[0].astype(F32) + bulk[1].astype(F32)) + bulk[2].astype(F32)) + bulk[3].astype(F32)
            total_to_sibling().start()
            total_from_sibling().wait_recv()
            total_ref[...] = total_scr[...]
            for a in range(n):
                all_from_sibling(a).wait_send()
                for r in range(3):
                    to_chip(a, r).wait_send()
            pack_to_sibling().wait_send()
            for r in range(3):
                half_to_chip(r, 0).wait_send()
                half_to_chip(r, 1).wait_send()
            total_to_sibling().wait_send()

    tile = lambda w: pl.BlockSpec((tt, w), lambda i: (i, 0))
    return pl.pallas_call(
        body, name="tail_comm", grid=(nt,),
        out_shape=tuple([jax.ShapeDtypeStruct(p.shape[1:], F32) for p in parts]
                        + [jax.ShapeDtypeStruct((PACK_ROWS, D), F32), jax.ShapeDtypeStruct((t_len, D), F32)]),
        in_specs=([_resident(p.shape) for p in parts] + [_resident(small.shape)]
                  + [tile(D_Z), tile(D), tile(D), _full((8, D)), _full((1, D)), _resident((D_Z, D))]),
        out_specs=tuple([_full(p.shape[1:]) for p in parts] + [_full((PACK_ROWS, D)), tile(D)]),
        scratch_shapes=(
            [pltpu.VMEM((4,) + p.shape[1:], BF16) for p in parts]
            + [pltpu.VMEM((3,) + p.shape[1:], BF16) for p in parts]
            + [pltpu.VMEM((3,) + p.shape[1:], BF16) for p in parts]
            + [pltpu.VMEM((PACK_ROWS, D), F32), pltpu.VMEM((PACK_ROWS, D), F32),
               pltpu.VMEM((4, PACK_FINE, D), F32), pltpu.VMEM((4, PACK_HALF - PACK_FINE, D), BF16),
               pltpu.VMEM((PACK_ROWS, D), F32), pltpu.VMEM((16, D), F32)]
            + [pltpu.SemaphoreType.DMA((n,)), pltpu.SemaphoreType.DMA((n,)),
               pltpu.SemaphoreType.DMA((3 * n,)), pltpu.SemaphoreType.DMA((3 * n,)),
               pltpu.SemaphoreType.DMA((8,)), pltpu.SemaphoreType.DMA((8,))]),
        compiler_params=pltpu.CompilerParams(dimension_semantics=("arbitrary",), vmem_limit_bytes=VMEM_LIMIT),
    )(*parts, small, dz, dx1, x, mod, n1pre, w_in_t)


def _tril_mask():
    row = lax.broadcasted_iota(jnp.int32, (CHUNK, CHUNK), 0)
    col = lax.broadcasted_iota(jnp.int32, (CHUNK, CHUNK), 1)
    return (col <= row).astype(F32)


def _window_sums(ext):
    s2 = ext + pltpu.roll(ext, 1, 0)
    t4 = s2[:, GROUP:]
    s4 = t4 + pltpu.roll(t4, 2, 0)
    t8 = s4[:, GROUP:]
    s8 = t8 + pltpu.roll(t8, 4, 0)
    t16 = s8[:, GROUP:]
    s16 = t16 + pltpu.roll(t16, 8, 0)
    return [s2[:, :GROUP], s4[:, :GROUP], s8[:, :GROUP], s16]


def _inv_counts(first_pos, rows):
    pos = first_pos + lax.broadcasted_iota(jnp.int32, (rows, 1), 0)
    return [1.0 / jnp.minimum(pos + 1, w).astype(F32) for w in WINDOWS]


def _pool_diff(zb, halo, first_pos):
    tt = zb.shape[0]
    sums = _window_sums(jnp.concatenate([halo, zb], axis=0))
    inv = _inv_counts(first_pos, tt)
    return [sums[g][HALO:, :] * inv[g] - zb[:, g * GROUP:(g + 1) * GROUP] for g in range(len(WINDOWS))]


def _row_blocks(scr, rows, place):
    def block(rel):
        start = pl.multiple_of(rows * _index(_peer(*place, rel)), rows)
        return scr.at[pl.ds(start, rows), :]

    def entries(shard_ref):
        return ([(shard_ref, block(0), rel) for rel in (1, 2, 4, 6)]
                + [(block(rel), block(rel), 1) for rel in (2, 4, 6)])
    return block, entries


def _attn_fwd(tt, x, c_row, w_ada, b_pieces, n1pre, n1post, w_in_shard, w_out_shard, w_sp, bs_rows, ln_g, ln_b,
              w_pool, b_pool, pool_scale, fc_shards, n2pre):
    t_len = x.shape[0]
    nt = t_len // tt
    ncol = w_ada.shape[1]

    def body(x_ref, xb_ref, c_ref, wada_ref, b_ref, n1pre_ref, n1post_ref, wi_ref, wo_ref, wsp_ref, bs_ref,
             lng_ref, lnb_ref, wp_ref, bp_ref, ps_ref, w1_ref, w2_ref, n2pre_ref,
             z_ref, cat_ref, mix_ref, x1_ref, h2_ref, r_ref, f_ref, mod_out, sc_out, e1_ref, e2_ref, wout_ref,
             win_out, carry, land1, land2, sib1, sib2, cat_keep, wout_scr, win_ref, mod_ref, cg, mg, part,
             send_sems, recv_sems, local_sems, wo_send, wo_recv, wi_send, wi_recv, ada_send, ada_recv):
        i = pl.program_id(0)
        place = px, py, pc = _place()
        me = _index(place)
        wo_block, wo_entries = _row_blocks(wout_scr, w_out_shard.shape[0], place)
        wi_block, wi_entries = _row_blocks(win_ref, w_in_shard.shape[0], place)
        wo_copies = _Copies(wo_entries(wo_ref), wo_send, wo_recv)
        wi_copies = _Copies(wi_entries(wi_ref), wi_send, wi_recv)
        wo_keep = pltpu.make_async_copy(wout_scr, wout_ref, local_sems.at[8])
        wi_keep = pltpu.make_async_copy(win_ref, win_out, local_sems.at[9])
        ada = _Copies([(cg.at[me], cg.at[me], k) for k in range(1, N_DEV)]
                      + [(part, mg.at[me], k) for k in range(1, N_DEV)], ada_send, ada_recv)
        copies = _Copies(
            [(w1_ref, sib1, 1), (w2_ref, sib2, 1),
             (w1_ref, land1.at[0], 2), (w2_ref, land2.at[0], 2),
             (w1_ref, land1.at[1], 4), (w2_ref, land2.at[1], 4),
             (land1.at[0], e1_ref.at[3], 1), (land2.at[0], e2_ref.at[3], 1),
             (land1.at[1], e1_ref.at[5], 1), (land2.at[1], e2_ref.at[5], 1)],
            send_sems, recv_sems)
        keep = [pltpu.make_async_copy(w1_ref, e1_ref.at[0], local_sems.at[0]),
                pltpu.make_async_copy(w2_ref, e2_ref.at[0], local_sems.at[1]),
                pltpu.make_async_copy(land1.at[0], e1_ref.at[2], local_sems.at[2]),
                pltpu.make_async_copy(land1.at[1], e1_ref.at[4], local_sems.at[3]),
                pltpu.make_async_copy(land2.at[0], e2_ref.at[2], local_sems.at[4]),
                pltpu.make_async_copy(land2.at[1], e2_ref.at[4], local_sems.at[5]),
                pltpu.make_async_copy(sib1, e1_ref.at[1], local_sems.at[6]),
                pltpu.make_async_copy(sib2, e2_ref.at[1], local_sems.at[7])]

        @pl.when(i == 0)
        def _():
            cg[me] = jnp.broadcast_to(c_ref[...], (8, D))
            ada.start(*range(N_DEV - 1))
            wi_copies.start(0, 1, 2, 3)
            wi_rows, wo_rows = w_in_shard.shape[0], w_out_shard.shape[0]
            win_ref[pl.ds(pl.multiple_of(wi_rows * me, wi_rows), wi_rows), :] = wi_ref[...]
            wout_scr[pl.ds(pl.multiple_of(wo_rows * me, wo_rows), wo_rows), :] = wo_ref[...]
            carry[...] = jnp.zeros_like(carry)

            ada.wait_recv(*range(N_DEV - 1))
            c_all = jnp.concatenate([cg[j, 0:1, :] for j in range(N_DEV)], axis=0)
            sc = c_all * jax.nn.sigmoid(c_all)
            sc_out[...] = sc
            part[...] = _dot(sc.astype(BF16), wada_ref[...].astype(BF16)) + b_ref[me]
            ada.start(*range(N_DEV - 1, 2 * (N_DEV - 1)))
            wo_copies.start(0, 1, 2, 3)
            copies.start(0, 1, 2, 4, 3, 5)
            keep[0].start()
            keep[1].start()
            mg[me] = part[...]
            ada.wait_recv(*range(N_DEV - 1, 2 * (N_DEV - 1)))
            mod_ref[...] = jnp.zeros_like(mod_ref)
            for j in range(N_DEV):
                for m in range(6):
                    lo, hi = max(ncol * j, D * m), min(ncol * (j + 1), D * (m + 1))
                    if lo < hi:
                        mod_ref[m:m + 1, lo - D * m:hi - D * m] = mg[j, pl.ds(me, 1), lo - ncol * j:hi - ncol * j]
            mod_out[...] = mod_ref[...]

            wi_copies.wait_recv(1, 2, 3)
            wi_copies.start(4, 5, 6)
            wi_copies.wait_recv(0, 4, 5, 6)
            wi_keep.start()

        @pl.when(i == nt // 2 + ATTN_LAG)
        def _():
            copies.wait_recv(2, 4)
            copies.start(6, 8)
            keep[2].start()
            keep[3].start()

        @pl.when(i == nt - 1 + ATTN_LAG)
        def _():
            copies.wait_recv(3, 5)
            copies.start(7, 9)
            keep[4].start()
            keep[5].start()

        shift1, scale1, gate1 = mod_ref[0:1, :], mod_ref[1:2, :], mod_ref[2:3, :]

        @pl.when(i < nt)
        def _():
            xv = x_ref[...]
            h1 = (xv * _rstd(xv)) * (n1pre_ref[...] * (1.0 + scale1)) + shift1
            z = _dot_nt(h1.astype(BF16), win_ref[...])
            z_ref[...] = z

            _, ga = _gelu_parts(z[:, :2 * D_A])
            u, vr = ga[:, :D_A], ga[:, D_A:]
            dv = vr - jnp.mean(vr, axis=-1, keepdims=True)
            v = (dv * lax.rsqrt(jnp.mean(dv * dv, axis=-1, keepdims=True) + EPS)) * lng_ref[...] + lnb_ref[...]
            vb = v.astype(BF16)
            mask = _tril_mask()
            wc = [(wsp_ref[h] * mask).astype(BF16) for h in range(N_HEADS)]
            for ch in range(tt // CHUNK):
                rows = slice(ch * CHUNK, (ch + 1) * CHUNK)
                for h in range(N_HEADS):
                    cols = slice(h * GROUP, (h + 1) * GROUP)
                    mixed = _dot(wc[h], vb[rows, cols]) + bs_ref[:, cols]
                    cat_ref[rows, cols] = (u[rows, cols] * mixed).astype(BF16)

            zb = z[:, 2 * D_A:]
            diff = _pool_diff(zb, carry[...], i * tt)
            carry[...] = zb[tt - HALO:, :]
            for g in range(len(WINDOWS)):
                cols = slice(g * GROUP, (g + 1) * GROUP)
                pre = _dot(diff[g].astype(BF16), wp_ref[g].astype(BF16)) + bp_ref[:, cols]
                cat_ref[:, D_A + g * GROUP:D_A + (g + 1) * GROUP] = (pre * ps_ref[:, cols]).astype(BF16)
            cat_keep[i % (ATTN_LAG + 1)] = cat_ref[...]

        @pl.when(i == 0)
        def _():
            copies.wait_recv(0, 1)
            keep[6].start()
            keep[7].start()

        @pl.when(i == 1)
        def _():
            wo_copies.wait_recv(1, 2, 3)
            wo_copies.start(4, 5, 6)

        @pl.when(i == ATTN_LAG)
        def _():
            wo_copies.wait_recv(0, 4, 5, 6)
            wo_keep.start()

        @pl.when(i >= ATTN_LAG)
        def _():
            xv = xb_ref[...]
            mix = _dot(cat_keep[(i - ATTN_LAG) % (ATTN_LAG + 1)], wout_scr[...])
            mix_ref[...] = mix
            x1v = xv + (mix * _rstd(mix)) * (gate1 * n1post_ref[...])
            x1_ref[...] = x1v
            shift2, scale2 = mod_ref[3:4, :], mod_ref[4:5, :]
            h2 = ((x1v * _rstd(x1v)) * (n2pre_ref[...] * (1.0 + scale2)) + shift2).astype(BF16)
            h2_ref[...] = h2
            for j, (w1, w2) in enumerate(((w1_ref, w2_ref), (sib1, sib2))):
                ra = jnp.maximum(_dot(h2, w1[...]), 0.0)
                r = (ra * ra).astype(BF16)
                r_ref[:, j * FF_BLK:(j + 1) * FF_BLK] = r
                if j == 0:
                    f_ref[...] = _dot(r, w2[...])
                else:
                    f_ref[...] += _dot(r, w2[...])

        @pl.when(i == nt - 1 + ATTN_LAG)
        def _():
            copies.wait_recv(6, 7, 8, 9)
            copies.wait_send(*range(10))
            wo_copies.wait_send(*range(7))
            wi_copies.wait_send(*range(7))
            ada.wait_send(*range(2 * (N_DEV - 1)))
            for cp in keep:
                cp.wait()
            wo_keep.wait()
            wi_keep.wait()

    first = lambda w: pl.BlockSpec((tt, w), lambda i: (jnp.minimum(i, nt - 1), 0))
    second = lambda w: pl.BlockSpec((tt, w), lambda i: (jnp.maximum(i - ATTN_LAG, 0), 0))
    r_head = pl.BlockSpec((tt, FC_HEAD * FF_BLK),
                          lambda i: (jnp.maximum(i - ATTN_LAG, 0), R_HEAD_COLS // (FC_HEAD * FF_BLK)))
    hbm = pl.BlockSpec(memory_space=pl.ANY)
    outs = pl.pallas_call(
        body, name="attn_fwd", grid=(nt + ATTN_LAG,),
        out_shape=tuple([jax.ShapeDtypeStruct((t_len, D_Z), F32), jax.ShapeDtypeStruct((t_len, D), BF16),
                         jax.ShapeDtypeStruct((t_len, D), F32), jax.ShapeDtypeStruct((t_len, D), F32),
                         jax.ShapeDtypeStruct((t_len, D), BF16),
                         jax.ShapeDtypeStruct((t_len, FC_EARLY * FF_BLK), BF16),
                         jax.ShapeDtypeStruct((t_len, D), F32)]
                        + [jax.ShapeDtypeStruct((8, D), F32), jax.ShapeDtypeStruct((N_DEV, D), F32)]
                        + [jax.ShapeDtypeStruct((FC_EARLY,) + s.shape, BF16) for s in fc_shards]
                        + [jax.ShapeDtypeStruct((D, D), BF16), jax.ShapeDtypeStruct((D_Z, D), BF16)]),
        in_specs=[first(D), second(D), _full((1, D)), _resident(w_ada.shape), _full((N_DEV, 1, ncol)), _full((1, D)),
                  _full((1, D)), _resident(w_in_shard.shape), _resident(w_out_shard.shape),
                  _full((N_HEADS, CHUNK, CHUNK)), _full((CHUNK, D_A)), _full((1, D_A)), _full((1, D_A)),
                  _full((len(WINDOWS), GROUP, GROUP)), _full((1, D_B)), _full((1, D_B)),
                  _resident(fc_shards[0].shape), _resident(fc_shards[1].shape), _full((1, D))],
        out_specs=(first(D_Z), first(D), second(D), second(D), second(D), r_head, second(D),
                   _full((8, D)), _full((N_DEV, D)), hbm, hbm, hbm, hbm),
        scratch_shapes=[pltpu.VMEM((HALO, D_B), F32),
                        pltpu.VMEM((2,) + fc_shards[0].shape, BF16), pltpu.VMEM((2,) + fc_shards[1].shape, BF16),
                        pltpu.VMEM(fc_shards[0].shape, BF16), pltpu.VMEM(fc_shards[1].shape, BF16),
                        pltpu.VMEM((ATTN_LAG + 1, tt, D), BF16), pltpu.VMEM((D, D), BF16),
                        pltpu.VMEM((D_Z, D), BF16), pltpu.VMEM((8, D), F32),
                        pltpu.VMEM((N_DEV, 8, D), F32), pltpu.VMEM((N_DEV, N_DEV, ncol), F32),
                        pltpu.VMEM((N_DEV, ncol), F32),
                        pltpu.SemaphoreType.DMA((10,)), pltpu.SemaphoreType.DMA((10,)),
                        pltpu.SemaphoreType.DMA((10,)),
                        pltpu.SemaphoreType.DMA((7,)), pltpu.SemaphoreType.DMA((7,)),
                        pltpu.SemaphoreType.DMA((7,)), pltpu.SemaphoreType.DMA((7,)),
                        pltpu.SemaphoreType.DMA((2 * (N_DEV - 1),)), pltpu.SemaphoreType.DMA((2 * (N_DEV - 1),))],
        compiler_params=pltpu.CompilerParams(dimension_semantics=("arbitrary",), vmem_limit_bytes=VMEM_LIMIT),
    )(x, x, c_row, w_ada, b_pieces, n1pre, n1post, w_in_shard, w_out_shard, w_sp, bs_rows, ln_g, ln_b, w_pool, b_pool,
      pool_scale, *fc_shards, n2pre)
    return outs[:9], outs[9:11], outs[11], outs[12]


def _mlp_fwd_early(tt, r_begun, h2, f_head, w1_early, w2_early):
    t_len = h2.shape[0]
    nt = t_len // tt
    n_late = N_DEV - FC_EARLY

    def body(r_begun_ref, h2_ref, fh_ref, w1_ref, w2_ref, r_ref, f_ref, l1_ref, l2_ref,
             land1, land2, send_sems, recv_sems, local_sems):
        i = pl.program_id(0)
        copies = _Copies(
            [(w1_ref.at[2], land1, 4), (w2_ref.at[4], land2, 2),
             (land1, l1_ref.at[1], 1), (land2, l2_ref.at[1], 1)],
            send_sems, recv_sems)
        keep = [pltpu.make_async_copy(land1, l1_ref.at[0], local_sems.at[0]),
                pltpu.make_async_copy(land2, l2_ref.at[0], local_sems.at[1])]

        @pl.when(i == 0)
        def _():
            copies.start(0, 1)

        @pl.when(i == nt - 1)
        def _():
            copies.wait_recv(0, 1)
            copies.start(2, 3)
            for cp in keep:
                cp.start()

        h2 = h2_ref[...]
        f_ref[...] = fh_ref[...]
        for j in range(FC_HEAD, FC_EARLY):
            ra = jnp.maximum(_dot(h2, w1_ref[j]), 0.0)
            r = (ra * ra).astype(BF16)
            r_ref[:, _early_col(j):_early_col(j) + FF_BLK] = r
            f_ref[...] += _dot(r, w2_ref[j])

        @pl.when(i == nt - 1)
        def _():
            copies.wait_recv(2, 3)
            copies.wait_send(0, 1, 2, 3)
            for cp in keep:
                cp.wait()

    tile = lambda w: pl.BlockSpec((tt, w), lambda i: (i, 0))
    hbm = pl.BlockSpec(memory_space=pl.ANY)
    outs = pl.pallas_call(
        body, name="mlp_fwd_early", grid=(nt,),
        out_shape=(jax.ShapeDtypeStruct((t_len, FC_EARLY * FF_BLK), BF16), jax.ShapeDtypeStruct((t_len, D), F32),
                   jax.ShapeDtypeStruct((n_late,) + w1_early.shape[1:], BF16),
                   jax.ShapeDtypeStruct((n_late,) + w2_early.shape[1:], BF16)),
        in_specs=[hbm, tile(D), tile(D), _resident((FC_EARLY, D, FF_BLK)), _resident((FC_EARLY, FF_BLK, D))],
        out_specs=(tile(R_HEAD_COLS), tile(D), hbm, hbm),
        input_output_aliases={0: 0},
        scratch_shapes=[pltpu.VMEM(w1_early.shape[1:], BF16), pltpu.VMEM(w2_early.shape[1:], BF16),
                        pltpu.SemaphoreType.DMA((4,)), pltpu.SemaphoreType.DMA((4,)),
                        pltpu.SemaphoreType.DMA((2,))],
        compiler_params=pltpu.CompilerParams(dimension_semantics=("arbitrary",), vmem_limit_bytes=VMEM_LIMIT),
    )(r_begun, h2, f_head, w1_early, w2_early)
    return outs[:2], outs[2:]


def _mlp_late_bwd(tt, r_early, x1, h2, f_early, tgt, mix, mod, n2pre, n2post, n1post,
                  w1_early, w2_early, w1_late, w2_late):
    t_len = x1.shape[0]
    nt = t_len // tt
    n_late = N_DEV - FC_EARLY
    late_cols = n_late * FF_BLK

    def body(re_ref, x1_ref, h2_ref, fe_ref, tgt_ref, mix_ref, mod_ref, n2pre_ref, n2post_ref,
             n1post_ref, w1e_ref, w2e_ref, w1l_ref, w2l_ref,
             rl_ref, df_ref, da_ref, dmix_ref, dx1_ref, redf_ref, redb_ref, dh2_acc):
        i = pl.program_id(0)

        @pl.when(i == 0)
        def _():
            redf_ref[...] = jnp.zeros_like(redf_ref)
            redb_ref[...] = jnp.zeros_like(redb_ref)

        x1v = x1_ref[...]
        gate1, scale2, gate2 = mod_ref[2:3, :], mod_ref[4:5, :], mod_ref[5:6, :]
        h2 = h2_ref[...]
        f = fe_ref[...]
        for j in range(n_late):
            cols = slice(j * FF_BLK, (j + 1) * FF_BLK)
            ra = jnp.maximum(_dot(h2, w1l_ref[j]), 0.0)
            r = (ra * ra).astype(BF16)
            rl_ref[:, cols] = r
            f = f + _dot(r, w2l_ref[j])
        post2 = n2post_ref[...]
        gate_post2 = gate2 * post2
        rf = _rstd(f)
        fhat = f * rf
        err = (x1v + fhat * gate_post2) - tgt_ref[...]
        dy = err * (1.0 / D)
        d_f, sum_f = _rms_bwd_gained(dy, gate_post2, fhat, rf)
        dfv = d_f.astype(BF16)
        df_ref[...] = dfv
        redf_ref[0:1, :] += post2 * sum_f
        redf_ref[1:2, :] += gate2 * sum_f
        redf_ref[2:3, :] += _colsum(err * err)

        for j in range(N_DEV):
            cols = slice(j * FF_BLK, (j + 1) * FF_BLK)
            if j < FC_EARLY:
                w1, w2, r = w1e_ref[j], w2e_ref[j], re_ref[:, _early_col(j):_early_col(j) + FF_BLK]
            else:
                jl = j - FC_EARLY
                w1, w2, r = w1l_ref[jl], w2l_ref[jl], rl_ref[:, jl * FF_BLK:(jl + 1) * FF_BLK]
            dr = _dot_nt(dfv, w2)
            da = (dr * (2.0 * jnp.sqrt(r.astype(F32)))).astype(BF16)
            da_ref[:, cols] = da
            contrib = _dot_nt(da, w1)
            if j == 0:
                dh2_acc[...] = contrib
            else:
                dh2_acc[...] += contrib
        dh2 = dh2_acc[...]
        pre2, post1 = n2pre_ref[...], n1post_ref[...]
        r2 = _rstd(x1v)
        xhat = x1v * r2
        d_x1, sum_h = _rms_bwd_gained(dh2, pre2 * (1.0 + scale2), xhat, r2)
        dx1 = dy + d_x1
        dx1_ref[...] = dx1
        mixv = mix_ref[...]
        rm = _rstd(mixv)
        mhat = mixv * rm
        d_mix, sum_m = _rms_bwd_gained(dx1, gate1 * post1, mhat, rm)
        dmix_ref[...] = d_mix.astype(BF16)
        redb_ref[0:1, :] += _colsum(dh2)
        redb_ref[1:2, :] += pre2 * sum_h
        redb_ref[2:3, :] += (1.0 + scale2) * sum_h
        redb_ref[3:4, :] += post1 * sum_m
        redb_ref[4:5, :] += gate1 * sum_m

    tile = lambda w: pl.BlockSpec((tt, w), lambda i: (i, 0))
    return pl.pallas_call(
        body, name="mlp_late_bwd", grid=(nt,),
        out_shape=(jax.ShapeDtypeStruct((t_len, late_cols), BF16), jax.ShapeDtypeStruct((t_len, D), BF16),
                   jax.ShapeDtypeStruct((t_len, D_FF), BF16), jax.ShapeDtypeStruct((t_len, D), BF16),
                   jax.ShapeDtypeStruct((t_len, D), F32), jax.ShapeDtypeStruct((8, D), F32),
                   jax.ShapeDtypeStruct((8, D), F32)),
        in_specs=[tile(FC_EARLY * FF_BLK), tile(D), tile(D), tile(D), tile(D),
                  tile(D), _full((8, D)), _full((1, D)), _full((1, D)), _full((1, D)),
                  _resident((FC_EARLY, D, FF_BLK)), _resident((FC_EARLY, FF_BLK, D)),
                  _resident((n_late, D, FF_BLK)), _resident((n_late, FF_BLK, D))],
        out_specs=(tile(late_cols), tile(D), tile(D_FF), tile(D), tile(D), _full((8, D)), _full((8, D))),
        scratch_shapes=[pltpu.VMEM((tt, D), F32)],
        compiler_params=pltpu.CompilerParams(dimension_semantics=("arbitrary",), vmem_limit_bytes=VMEM_LIMIT),
    )(r_early, x1, h2, f_early, tgt, mix, mod, n2pre, n2post, n1post, w1_early, w2_early, w1_late, w2_late)


def _mlp_wgrad(tt, r_early, r_late, da, df, h2):
    t_len = df.shape[0]
    nt = t_len // tt
    odd_steps = [j for j, rel in enumerate(WGRAD_ORDER) if rel % 2]

    def relation(j):
        rel = jnp.int32(WGRAD_ORDER[-1])
        for step in range(N_DEV - 2, -1, -1):
            rel = jnp.where(j == step, WGRAD_ORDER[step], rel)
        return rel

    def body(re_ref, rl_ref, da_ref, df_ref, h2_ref, own1_ref, own2_ref, out1_ref, out2_ref, diag1_ref, diag2_ref,
             acc1, acc2, snd1, snd2, sib1, sib2, dsnd1, dsnd2, send_sems, recv_sems):
        j, t = pl.program_id(0), pl.program_id(1)
        rows = pl.ds(pl.multiple_of(t * tt, tt), tt)
        x, y, c = _place()
        accs, snds, sibs = (acc1, acc2), (snd1, snd2), (sib1, sib2)
        dsnds, diags = (dsnd1, dsnd2), (diag1_ref, diag2_ref)

        def to_sibling(a, jj, buf=0):
            return pltpu.make_async_remote_copy(
                src_ref=snds[a].at[buf], dst_ref=sibs[a].at[jj],
                send_sem=send_sems.at[4 * a + jj], recv_sem=recv_sems.at[4 * a + jj],
                device_id=(x, y, 1 - c), device_id_type=MESH)

        def to_diagonal(a):
            return pltpu.make_async_remote_copy(
                src_ref=dsnds[a], dst_ref=diags[a], send_sem=send_sems.at[8 + a], recv_sem=recv_sems.at[8 + a],
                device_id=_peer(x, y, c, 6), device_id_type=MESH)

        @pl.when(t == 0)
        def _():
            acc2[...] = jnp.zeros_like(acc2)
            acc1[...] = jnp.zeros_like(acc1)

        for r_ref, mine in ((re_ref, relation(j) < FC_EARLY), (rl_ref, relation(j) >= FC_EARLY)):
            @pl.when(mine)
            def _():
                acc2[...] += _dot_tn(r_ref[...], df_ref[rows, :])
                acc1[...] += _dot_tn(h2_ref[rows, :], da_ref[...])

        for step, rel in enumerate(WGRAD_ORDER):
            jj = rel // 2

            @pl.when((t == nt - 1) & (j == step))
            def _():
                for a, (own_ref, out_ref) in enumerate(((own1_ref, out1_ref), (own2_ref, out2_ref))):
                    if rel % 2:
                        q = odd_steps.index(step)
                        if q >= 2:
                            to_sibling(a, WGRAD_ORDER[odd_steps[q - 2]] // 2).wait_send()
                        snds[a][q % 2] = accs[a][...].astype(BF16)
                        to_sibling(a, jj, q % 2).start()
                        continue
                    to_sibling(a, jj).wait_recv()
                    chip_sum = accs[a][...] + sibs[a][jj].astype(F32)
                    if rel == 6:
                        dsnds[a][...] = chip_sum.astype(BF16)
                        to_diagonal(a).start()
                    elif rel == 0:
                        own_ref[...] = chip_sum
                    else:
                        out_ref[0] = chip_sum.astype(BF16)
                    if step == N_DEV - 1:
                        for q in (2, 3):
                            to_sibling(a, WGRAD_ORDER[odd_steps[q]] // 2).wait_send()
                        to_diagonal(a).wait_recv()
                        to_diagonal(a).wait_send()

    assert WGRAD_ORDER[-1] == 0 and WGRAD_ORDER[-3:-1] == (2, 4)
    blk = pl.BlockSpec((tt, FF_BLK), lambda j, t: (t, relation(j)))
    early_block = lambda rel: jnp.where(rel < FC_HEAD, rel + FC_EARLY - FC_HEAD, rel - FC_HEAD)
    early = lambda j, t: (jnp.where(relation(j) < FC_EARLY, t, 0),
                          jnp.where(relation(j) < FC_EARLY, early_block(relation(j)), 0))
    late = lambda j, t: (jnp.where(relation(j) < FC_EARLY, 0, t), jnp.maximum(relation(j) - FC_EARLY, 0))
    chip = lambda j, t: (jnp.clip(j - 5, 0, 1), 0, 0)
    hbm = pl.BlockSpec(memory_space=pl.ANY)
    return pl.pallas_call(
        body, name="mlp_wgrad", grid=(N_DEV, nt),
        out_shape=(jax.ShapeDtypeStruct((D, FF_BLK), F32), jax.ShapeDtypeStruct((FF_BLK, D), F32),
                   jax.ShapeDtypeStruct((2, D, FF_BLK), BF16), jax.ShapeDtypeStruct((2, FF_BLK, D), BF16),
                   jax.ShapeDtypeStruct((D, FF_BLK), BF16), jax.ShapeDtypeStruct((FF_BLK, D), BF16)),
        in_specs=[pl.BlockSpec((tt, FF_BLK), early), pl.BlockSpec((tt, FF_BLK), late), blk,
                  _resident((t_len, D)), _resident((t_len, D))],
        out_specs=(_full((D, FF_BLK)), _full((FF_BLK, D)),
                   pl.BlockSpec((1, D, FF_BLK), chip), pl.BlockSpec((1, FF_BLK, D), chip), hbm, hbm),
        scratch_shapes=[pltpu.VMEM((D, FF_BLK), F32), pltpu.VMEM((FF_BLK, D), F32),
                        pltpu.VMEM((2, D, FF_BLK), BF16), pltpu.VMEM((2, FF_BLK, D), BF16),
                        pltpu.VMEM((4, D, FF_BLK), BF16), pltpu.VMEM((4, FF_BLK, D), BF16),
                        pltpu.VMEM((D, FF_BLK), BF16), pltpu.VMEM((FF_BLK, D), BF16),
                        pltpu.SemaphoreType.DMA((10,)), pltpu.SemaphoreType.DMA((10,))],
        compiler_params=pltpu.CompilerParams(dimension_semantics=("arbitrary", "arbitrary"),
                                             vmem_limit_bytes=VMEM_LIMIT),
    )(r_early, r_late, da, df, h2)


def _acc_rows(ref, row0, k, val):
    half = CHUNK // 2
    ref[row0:row0 + half, k * GROUP:(k + 1) * GROUP] += val[:half, :]
    ref[row0:row0 + half, D_A + k * GROUP:D_A + (k + 1) * GROUP] += val[half:, :]


def _attn_bwd(tt, dmix, x, z, cat, mod, n1pre, w_out, w_sp, bs_rows, ln_g, ln_b, w_pool, b_pool, pool_scale,
              red_fwd, red_bwd, chip_sums):
    t_len = z.shape[0]
    nt = t_len // tt
    hb = tt // HALO
    n_sums = len(chip_sums)

    def body(dmix_ref, x_ref, z_ref, zprev_ref, cat_ref, mod_ref, n1pre_ref, wout_ref, wsp_ref,
             bs_ref, lng_ref, lnb_ref, wp_ref, bp_ref, ps_ref, redf_ref, redb_ref, *rest):
        sum_out = rest[:n_sums]
        dz_ref, gwin_ref, gwout_ref, small_ref = rest[n_sums:n_sums + 4]
        sum_in = rest[n_sums + 4:2 * n_sums + 4]
        carry, acc_in, acc_out, dz_scr, bs_acc, send_sems, recv_sems = rest[2 * n_sums + 4:]
        s = pl.program_id(0)
        i = nt - 1 - s
        px, py, pc = _place()

        def chip_copy(a, r):
            return pltpu.make_async_remote_copy(
                src_ref=sum_out[a].at[r], dst_ref=sum_in[a].at[r],
                send_sem=send_sems.at[2 * a + r], recv_sem=recv_sems.at[2 * a + r],
                device_id=_peer(px, py, pc, 2 * (r + 1)), device_id_type=MESH)

        @pl.when(s == 0)
        def _():
            for a in range(n_sums):
                for r in range(2):
                    chip_copy(a, r).start()
            carry[...] = jnp.zeros_like(carry)
            acc_in[...] = jnp.zeros_like(acc_in)
            acc_out[...] = jnp.zeros_like(acc_out)
            bs_acc[...] = jnp.zeros_like(bs_acc)
            small_ref[...] = jnp.zeros_like(small_ref)
            small_ref[ROW_DMOD + 2:ROW_DMOD + 3, :] = redb_ref[3:4, :]
            small_ref[ROW_DMOD + 3:ROW_DMOD + 5, :] = redb_ref[0:2, :]
            small_ref[ROW_DMOD + 5:ROW_DMOD + 6, :] = redf_ref[0:1, :]
            small_ref[ROW_N1POST:ROW_N1POST + 1, :] = redb_ref[4:5, :]
            small_ref[ROW_N2PRE:ROW_N2PRE + 1, :] = redb_ref[2:3, :]
            small_ref[ROW_N2POST:ROW_N2POST + 1, :] = redf_ref[1:2, :]
            small_ref[ROW_LOSS:ROW_LOSS + 1, :] = redf_ref[2:3, :]

        dmixv = dmix_ref[...]
        dcat = _dot_nt(dmixv, wout_ref[...])
        acc_out[...] += _dot_tn(cat_ref[...], dmixv)

        z = z_ref[...]
        t_g, ga = _gelu_parts(z[:, :2 * D_A])
        u, vr = ga[:, :D_A], ga[:, D_A:]
        dv0 = vr - jnp.mean(vr, axis=-1, keepdims=True)
        rv = lax.rsqrt(jnp.mean(dv0 * dv0, axis=-1, keepdims=True) + EPS)
        vhat = dv0 * rv
        vb = (vhat * lng_ref[...] + lnb_ref[...]).astype(BF16)
        mask = _tril_mask()
        wc = [(wsp_ref[h] * mask).astype(BF16) for h in range(N_HEADS)]

        dya = dcat[:, :D_A]
        for h in range(N_HEADS):
            cols = slice(h * GROUP, (h + 1) * GROUP)
            bs_sum = jnp.zeros((CHUNK, GROUP), F32)
            ws_sum = jnp.zeros((CHUNK, CHUNK), F32)
            for ch in range(tt // CHUNK):
                rows = slice(ch * CHUNK, (ch + 1) * CHUNK)
                v_ch = vb[rows, cols]
                mixed = _dot(wc[h], v_ch) + bs_ref[:, cols]
                dy_ch = dya[rows, cols]
                dz_scr[rows, cols] = dy_ch * mixed
                dmixed = dy_ch * u[rows, cols]
                dmb = dmixed.astype(BF16)
                dz_scr[rows, D_A + h * GROUP:D_A + (h + 1) * GROUP] = _dot_tn(wc[h], dmb)
                bs_sum = bs_sum + dmixed
                ws_sum = ws_sum + _dot_nt(dmb, v_ch)
            _acc_rows(bs_acc, 0, h, bs_sum)
            _acc_rows(small_ref, ROW_WS, h, ws_sum)

        dvl = dz_scr[:, D_A:2 * D_A]
        dvhat = dvl * lng_ref[...]
        dvl_vhat = dvl * vhat
        dvr = rv * (dvhat - jnp.mean(dvhat, axis=-1, keepdims=True)
                    - vhat * jnp.mean(dvl_vhat * lng_ref[...], axis=-1, keepdims=True))
        small_ref[ROW_LN:ROW_LN + 1, 0:D_A] += _colsum(dvl_vhat)
        small_ref[ROW_LN:ROW_LN + 1, D_A:D] += _colsum(dvl)
        dga = jnp.concatenate([dz_scr[:, :D_A], dvr], axis=1)
        dza = dga * _gelu_grad(z[:, :2 * D_A], t_g)

        zb = z[:, 2 * D_A:]
        halo_prev = jnp.where(i == 0, 0.0, zprev_ref[...])
        diff = _pool_diff(zb, halo_prev, i * tt)
        dyb = dcat[:, D_A:]
        inv = _inv_counts(i * tt, tt)
        scaled, ddiffs = [], []
        for g in range(len(WINDOWS)):
            cols = slice(g * GROUP, (g + 1) * GROUP)
            db = diff[g].astype(BF16)
            wpg = wp_ref[g].astype(BF16)
            pre = _dot(db, wpg) + bp_ref[:, cols]
            small_ref[ROW_POOL:ROW_POOL + 1, cols] += _colsum(dyb[:, cols] * pre)
            dpre = dyb[:, cols] * ps_ref[:, cols]
            small_ref[ROW_POOL:ROW_POOL + 1, D_B + g * GROUP:D_B + (g + 1) * GROUP] += _colsum(dpre)
            dpb = dpre.astype(BF16)
            _acc_rows(small_ref, ROW_WP, g, _dot_tn(db, dpb))
            ddiff = _dot_nt(dpb, wpg)
            ddiffs.append(ddiff)
            scaled.append(ddiff * inv[g])
        scaled_all = jnp.concatenate(scaled, axis=1)
        ext = jnp.concatenate([scaled_all, carry[...]], axis=0)
        n_ext = tt + HALO
        s2 = ext + pltpu.roll(ext, n_ext - 1, 0)
        t4 = s2[:, GROUP:]
        s4 = t4 + pltpu.roll(t4, n_ext - 2, 0)
        t8 = s4[:, GROUP:]
        s8 = t8 + pltpu.roll(t8, n_ext - 4, 0)
        t16 = s8[:, GROUP:]
        s16 = t16 + pltpu.roll(t16, n_ext - 8, 0)
        back = [s2[:, :GROUP], s4[:, :GROUP], s8[:, :GROUP], s16]
        carry[...] = scaled_all[:HALO, :]
        dzb = jnp.concatenate([back[g][:tt, :] - ddiffs[g] for g in range(len(WINDOWS))], axis=1)

        dzv = jnp.concatenate([dza, dzb], axis=1).astype(BF16)
        dz_ref[...] = dzv
        xv = x_ref[...]
        h1 = (xv * _rstd(xv) * (n1pre_ref[...] * (1.0 + mod_ref[1:2, :])) + mod_ref[0:1, :]).astype(BF16)
        acc_in[...] += _dot_tn(dzv, h1)

        @pl.when(s == nt - 1)
        def _():
            gwin_ref[...] = acc_in[...].astype(BF16)
            gwout_ref[...] = acc_out[...].astype(BF16)
            bs = _unfold(bs_acc[...])
            for h in range(N_HEADS):
                small_ref[ROW_BS + h:ROW_BS + h + 1, 0:GROUP] = jnp.sum(
                    bs[:, h * GROUP:(h + 1) * GROUP].T, axis=0, keepdims=True)
            for a in range(n_sums):
                for r in range(2):
                    chip_copy(a, r).wait_recv()
                    chip_copy(a, r).wait_send()

    rev = lambda w: pl.BlockSpec((tt, w), lambda s: (nt - 1 - s, 0))
    zprev = pl.BlockSpec((HALO, D_B), lambda s: (jnp.maximum((nt - 1 - s) * hb - 1, 0), 2))
    hbm = pl.BlockSpec(memory_space=pl.ANY)
    outs = pl.pallas_call(
        body, name="attn_bwd", grid=(nt,),
        out_shape=tuple([jax.ShapeDtypeStruct((t_len, D_Z), BF16), jax.ShapeDtypeStruct((D_Z, D), BF16),
                         jax.ShapeDtypeStruct((D, D), BF16), jax.ShapeDtypeStruct((SMALL_ROWS, D), F32)]
                        + [jax.ShapeDtypeStruct(cs.shape, cs.dtype) for cs in chip_sums]),
        in_specs=[rev(D), rev(D), rev(D_Z), zprev, rev(D), _full((8, D)), _full((1, D)),
                  _resident((D, D)), _full((N_HEADS, CHUNK, CHUNK)), _full((CHUNK, D_A)),
                  _full((1, D_A)), _full((1, D_A)), _full((len(WINDOWS), GROUP, GROUP)), _full((1, D_B)),
                  _full((1, D_B)), _full((8, D)), _full((8, D))] + [_resident(cs.shape) for cs in chip_sums],
        out_specs=tuple([rev(D_Z), _resident((D_Z, D)), _resident((D, D)), _full((SMALL_ROWS, D))]
                        + [hbm] * n_sums),
        scratch_shapes=[pltpu.VMEM((HALO, D_B), F32), pltpu.VMEM((D_Z, D), F32), pltpu.VMEM((D, D), F32),
                        pltpu.VMEM((tt, 2 * D_A), F32), pltpu.VMEM((CHUNK // 2, D), F32),
                        pltpu.SemaphoreType.DMA((2 * n_sums,)), pltpu.SemaphoreType.DMA((2 * n_sums,))],
        compiler_params=pltpu.CompilerParams(dimension_semantics=("arbitrary",), vmem_limit_bytes=VMEM_LIMIT),
    )(dmix, x, z, z, cat, mod, n1pre, w_out, w_sp, bs_rows, ln_g, ln_b, w_pool, b_pool, pool_scale,
      red_fwd, red_bwd, *chip_sums)
    return outs[:4], outs[4:]


def _adam(w, g, m, v):
    m2 = ADAM_B1 * m + (1.0 - ADAM_B1) * g
    v2 = ADAM_B2 * v + (1.0 - ADAM_B2) * (g * g)
    m_hat = m2 / (1.0 - ADAM_B1 ** ADAM_STEP)
    v_hat = v2 / (1.0 - ADAM_B2 ** ADAM_STEP)
    delta = -ADAM_LR * (m_hat / (jnp.sqrt(v_hat) + ADAM_EPS) + ADAM_WD * w)
    return delta, m2, v2


def _adamw_fc(steps, fc):
    n_fc = len(fc)

    def body(*refs):
        ins, outs = refs[:6 * n_fc], refs[6 * n_fc:]
        for k in range(n_fc):
            w_ref, own_ref, arr_ref, diag_ref, m_ref, v_ref = ins[6 * k:6 * k + 6]
            g = ((own_ref[...] + arr_ref[0].astype(F32)) + arr_ref[1].astype(F32)) + diag_ref[...].astype(F32)
            outs[4 * k][...] = g
            outs[4 * k + 1][...], outs[4 * k + 2][...], outs[4 * k + 3][...] = _adam(
                w_ref[...], g, m_ref[...], v_ref[...])

    specs_in, specs_out, shapes, args = [], [], [], []
    for w, own, arrived, diagonal, m, v in fc:
        rows, cols = w.shape
        blk = pl.BlockSpec((rows // steps, cols), lambda i: (i, 0))
        specs_in += [blk, blk, pl.BlockSpec((2, rows // steps, cols), lambda i: (0, i, 0)), blk, blk, blk]
        specs_out += [blk] * 4
        shapes += [jax.ShapeDtypeStruct((rows, cols), F32)] * 4
        args += [w, own, arrived, diagonal, m, v]
    outs = pl.pallas_call(
        body, name="adamw_fc", grid=(steps,), out_shape=tuple(shapes), in_specs=specs_in, out_specs=tuple(specs_out),
        compiler_params=pltpu.CompilerParams(dimension_semantics=("arbitrary",), vmem_limit_bytes=VMEM_LIMIT),
    )(*args)
    return [outs[4 * k:4 * k + 4] for k in range(n_fc)]


def _adamw_ada(rb, w, sc, total, m, v):
    rows, cols = w.shape

    def body(w_ref, sc_ref, t_ref, m_ref, v_ref, g_ref, d_ref, m2_ref, v2_ref, dm):
        me = _index(_place())
        for dev in range(N_DEV):
            @pl.when((pl.program_id(0) == 0) & (me == dev))
            def _():
                for b in range(N_DEV):
                    for k in range(6):
                        lo, hi = max(cols * dev, D * k), min(cols * (dev + 1), D * (k + 1))
                        if lo < hi:
                            dm[b:b + 1, lo - cols * dev:hi - cols * dev] = t_ref[
                                _table_row(b) + k:_table_row(b) + k + 1, lo - D * k:hi - D * k]

        g = _dot_tn(sc_ref[...].astype(BF16), dm[...].astype(BF16))
        g_ref[...] = g
        d_ref[...], m2_ref[...], v2_ref[...] = _adam(w_ref[...], g, m_ref[...], v_ref[...])

    blk = pl.BlockSpec((rb, cols), lambda i: (i, 0))
    shp = jax.ShapeDtypeStruct((rows, cols), F32)
    return pl.pallas_call(
        body, name="adamw_ada", grid=(rows // rb,), out_shape=(shp, shp, shp, shp),
        in_specs=[blk, pl.BlockSpec((N_DEV, rb), lambda i: (0, i)), _full(total.shape), blk, blk],
        out_specs=(blk, blk, blk, blk),
        scratch_shapes=[pltpu.VMEM((N_DEV, cols), F32)],
        compiler_params=pltpu.CompilerParams(dimension_semantics=("arbitrary",), vmem_limit_bytes=VMEM_LIMIT),
    )(w, sc, total, m, v)


def _unfold(acc_rows):
    return jnp.concatenate([acc_rows[:, :D_A], acc_rows[:, D_A:]], axis=0)


def _adamw_small(total, params, shards):
    n = len(params)
    flat = [a for p in params for a in p]

    def body(*refs):
        s_ref = refs[0]
        p_refs = refs[1:1 + 3 * n]
        s_refs = refs[1 + 3 * n:1 + 3 * n + 4 * len(shards)]
        loss_ref = refs[1 + 3 * n + 4 * len(shards)]
        o_refs = refs[2 + 3 * n + 4 * len(shards):2 + 3 * n + 4 * len(shards) + 4 * n]
        so_refs = refs[2 + 3 * n + 4 * len(shards) + 4 * n:]
        d_b_ada = s_ref[0:6, :]
        for b in range(1, N_DEV):
            d_b_ada = d_b_ada + s_ref[_table_row(b):_table_row(b) + 6, :]
        misc = lambda r: s_ref[PK_MISC + r - ROW_N1PRE:PK_MISC + r - ROW_N1PRE + 1, :]
        loss = jnp.sum(misc(ROW_LOSS), axis=-1, keepdims=True) * (0.5 / D)
        loss_ref[...] = loss
        mask = _tril_mask()
        ws = _unfold(s_ref[PK_WS:PK_WS + 64, :])
        wp = _unfold(s_ref[PK_WP:PK_WP + 64, :])
        grads = [
            d_b_ada,
            misc(ROW_N1PRE), misc(ROW_N1POST), misc(ROW_N2PRE), misc(ROW_N2POST),
            misc(ROW_LN)[:, :D_A], misc(ROW_LN)[:, D_A:],
            misc(ROW_POOL)[:, :D_B], misc(ROW_POOL)[:, D_B:],
            s_ref[PK_BS:PK_BS + N_HEADS, 0:GROUP],
            jnp.stack([ws[:, h * GROUP:(h + 1) * GROUP] * mask for h in range(N_HEADS)]),
            jnp.stack([wp[:, g * GROUP:(g + 1) * GROUP] for g in range(len(WINDOWS))]),
        ]
        for k in range(n):
            w_ref, m_ref, v_ref = p_refs[3 * k:3 * k + 3]
            g = grads[k]
            if k == 0:
                for j in range(6):
                    o_refs[0][j] = g[j:j + 1, :]
                    o_refs[1][j], o_refs[2][j], o_refs[3][j] = _adam(w_ref[j], g[j:j + 1, :], m_ref[j], v_ref[j])
                continue
            o_refs[4 * k][...] = g
            o_refs[4 * k + 1][...], o_refs[4 * k + 2][...], o_refs[4 * k + 3][...] = _adam(
                w_ref[...], g, m_ref[...], v_ref[...])
        for k in range(len(shards)):
            w_ref, g_ref, m_ref, v_ref = s_refs[4 * k:4 * k + 4]
            so_refs[3 * k][...], so_refs[3 * k + 1][...], so_refs[3 * k + 2][...] = _adam(
                w_ref[...], g_ref[...], m_ref[...], v_ref[...])

    vm = pl.BlockSpec(memory_space=pltpu.VMEM)
    out_shape = [jax.ShapeDtypeStruct((1, 1), F32)]
    for w, _, _ in params:
        out_shape += [jax.ShapeDtypeStruct(w.shape, F32)] * 4
    for w, _, _, _ in shards:
        out_shape += [jax.ShapeDtypeStruct(w.shape, F32)] * 3
    return pl.pallas_call(
        body, name="adamw_small", out_shape=tuple(out_shape),
        in_specs=[vm] * (1 + 3 * n + 4 * len(shards)), out_specs=tuple([vm] * len(out_shape)),
        compiler_params=pltpu.CompilerParams(vmem_limit_bytes=VMEM_LIMIT),
    )(total, *flat, *[a for s in shards for a in s])


TT_ATTN_FWD = 512
ATTN_LAG = 2
TT_MLP_FWD = 512
TT_MLP = 256
TT_WGRAD = 2048
TT_ATTN_BWD = 512
TT_TAIL = 1024


def kernel(x, c, w_ada, b_ada, norm1_pre, norm1_post, w_in, w_spatial, b_spatial, ln_v_gain, ln_v_bias, w_pool, b_pool, pool_scale, w_out, norm2_pre, norm2_post, w_fc1, w_fc2, loss_target, m_w_ada, m_b_ada, m_norm1_pre, m_norm1_post, m_w_in, m_w_spatial, m_b_spatial, m_ln_v_gain, m_ln_v_bias, m_w_pool, m_b_pool, m_pool_scale, m_w_out, m_norm2_pre, m_norm2_post, m_w_fc1, m_w_fc2, v_w_ada, v_b_ada, v_norm1_pre, v_norm1_post, v_w_in, v_w_spatial, v_b_spatial, v_ln_v_gain, v_ln_v_bias, v_w_pool, v_b_pool, v_pool_scale, v_w_out, v_norm2_pre, v_norm2_post, v_w_fc1, v_w_fc2):
    t_len = x.shape[1]
    ada_cols = w_ada.shape[1]
    tt = lambda want: min(want, t_len)

    x2 = x.reshape(t_len, D)
    tgt = loss_target.reshape(t_len, D)
    row = lambda a: a.reshape(1, -1)

    w_in_shard, w_out_shard, w1_shard, w2_shard = _cast_shards([w_in.T, w_out, w_fc1, w_fc2])

    bs_rows = jnp.repeat(b_spatial.T, GROUP, axis=1)
    attn_consts = (w_spatial, bs_rows, row(ln_v_gain), row(ln_v_bias), w_pool, row(b_pool), row(pool_scale))

    (z, cat, mix, x1, h2, r_begun, f_head, mod, sc), (w1_early, w2_early), w_out_all, w_in_t = _attn_fwd(
        tt(TT_ATTN_FWD), x2, c.reshape(1, D), w_ada, b_ada.reshape(N_DEV, 1, ada_cols), row(norm1_pre),
        row(norm1_post),
        w_in_shard, w_out_shard, *attn_consts, (w1_shard, w2_shard), row(norm2_pre))
    (r_early, f_early), (w1_late, w2_late) = _mlp_fwd_early(
        tt(TT_MLP_FWD), r_begun, h2, f_head, w1_early, w2_early)
    r_late, df, da, dmix, dx1, red_fwd, red_bwd = _mlp_late_bwd(
        tt(TT_MLP), r_early, x1, h2, f_early, tgt, mix, mod, row(norm2_pre), row(norm2_post), row(norm1_post),
        w1_early, w2_early, w1_late, w2_late)
    own_w1, own_w2, sums_w1, sums_w2, diag_w1, diag_w2 = _mlp_wgrad(tt(TT_WGRAD), r_early, r_late, da, df, h2)
    (dz, p_in, p_out, small), (arr_w1, arr_w2) = _attn_bwd(
        tt(TT_ATTN_BWD), dmix, x2, z, cat, mod, row(norm1_pre), w_out_all, *attn_consts, red_fwd, red_bwd,
        [sums_w1, sums_w2])
    (grad_w1, d_w1, m_w1, v_w1), (grad_w2, d_w2, m_w2, v_w2) = _adamw_fc(
        4, [(w_fc1, own_w1, arr_w1, diag_w1, m_w_fc1, v_w_fc1), (w_fc2, own_w2, arr_w2, diag_w2, m_w_fc2, v_w_fc2)])
    grad_in_t, grad_out, total, grad_x = _tail_comm(
        [p_in.reshape(N_DEV, D_Z // N_DEV, D), p_out.reshape(N_DEV, D // N_DEV, D)], small, 64,
        tt(TT_TAIL), dz, dx1, x2, mod, row(norm1_pre), w_in_t)

    grad_ada, d_ada, m_ada, v_ada = _adamw_ada(512, w_ada, sc, total, m_w_ada, v_w_ada)

    six = lambda a: a.reshape(6, 1, D)
    small_params = [
        (six(b_ada), six(m_b_ada), six(v_b_ada)),
        (row(norm1_pre), row(m_norm1_pre), row(v_norm1_pre)),
        (row(norm1_post), row(m_norm1_post), row(v_norm1_post)),
        (row(norm2_pre), row(m_norm2_pre), row(v_norm2_pre)),
        (row(norm2_post), row(m_norm2_post), row(v_norm2_post)),
        (row(ln_v_gain), row(m_ln_v_gain), row(v_ln_v_gain)),
        (row(ln_v_bias), row(m_ln_v_bias), row(v_ln_v_bias)),
        (row(pool_scale), row(m_pool_scale), row(v_pool_scale)),
        (row(b_pool), row(m_b_pool), row(v_b_pool)),
        (b_spatial, m_b_spatial, v_b_spatial),
        (w_spatial, m_w_spatial, v_w_spatial),
        (w_pool, m_w_pool, v_w_pool),
    ]
    outs = _adamw_small(total, small_params, [(w_out, grad_out, m_w_out, v_w_out),
                                              (w_in.T, grad_in_t, m_w_in.T, v_w_in.T)])
    d_out, m_out, v_out, d_in_t, m_in_t, v_in_t = outs[1 + 4 * len(small_params):]
    loss = outs[0].reshape(())
    names = ["b_ada", "norm1_pre", "norm1_post", "norm2_pre", "norm2_post", "ln_v_gain", "ln_v_bias", "pool_scale",
             "b_pool", "b_spatial", "w_spatial", "w_pool"]
    shapes = dict(b_ada=b_ada.shape, norm1_pre=norm1_pre.shape, norm1_post=norm1_post.shape,
                  norm2_pre=norm2_pre.shape, norm2_post=norm2_post.shape, ln_v_gain=ln_v_gain.shape,
                  ln_v_bias=ln_v_bias.shape, pool_scale=pool_scale.shape, b_pool=b_pool.shape,
                  b_spatial=b_spatial.shape, w_spatial=w_spatial.shape, w_pool=w_pool.shape)
    res = {}
    for k, nm in enumerate(names):
        res[nm] = tuple(o.reshape(shapes[nm]) for o in outs[1 + 4 * k:5 + 4 * k])
    res["w_ada"] = (grad_ada, d_ada, m_ada, v_ada)
    res["w_in"] = (grad_in_t.T, d_in_t.T, m_in_t.T, v_in_t.T)
    res["w_out"] = (grad_out, d_out, m_out, v_out)
    res["w_fc1"] = (grad_w1, d_w1, m_w1, v_w1)
    res["w_fc2"] = (grad_w2, d_w2, m_w2, v_w2)

    order = ["w_ada", "b_ada", "norm1_pre", "norm1_post", "w_in", "w_spatial", "b_spatial", "ln_v_gain", "ln_v_bias",
             "w_pool", "b_pool", "pool_scale", "w_out", "norm2_pre", "norm2_post", "w_fc1", "w_fc2"]
    return (loss, grad_x.reshape(x.shape),
            *[res[nm][0] for nm in order], *[res[nm][1] for nm in order],
            *[res[nm][2] for nm in order], *[res[nm][3] for nm in order])
```

```python
import functools

import jax
import jax.numpy as jnp
from jax import lax
from jax.experimental import pallas as pl
from jax.experimental.pallas import tpu as pltpu

F32 = jnp.float32
BF16 = jnp.bfloat16
MESH = pl.DeviceIdType.MESH

N_DEV = 8
D = 1024
D_A = 512
D_B = 512
D_Z = 2 * D_A + D_B
N_HEADS = 4
CHUNK = 128
WINDOWS = (2, 4, 8, 16)
GROUP = 128
D_FF = 4096
FF_BLK = D_FF // N_DEV
HALO = 16
EPS = 1e-6
VMEM_LIMIT = 60 * 1024 * 1024

ADAM_LR = 0.001
ADAM_B1 = 0.9
ADAM_B2 = 0.999
ADAM_EPS = 1e-08
ADAM_WD = 0.01
ADAM_STEP = 10

ROW_DMOD = 0
ROW_N1PRE, ROW_N1POST, ROW_N2PRE, ROW_N2POST = 8, 9, 10, 11
ROW_LN = 12
ROW_POOL = 13
ROW_LOSS = 14
ROW_BS = 16
ROW_WS = 24
ROW_WP = 88
SMALL_ROWS = 152
PACK_FINE = 40
PACK_HALF = PACK_FINE + 64
PACK_ROWS = 2 * PACK_HALF
PK_WS = PACK_FINE
PK_TABLE_B = PACK_HALF
PK_MISC = PK_TABLE_B + 24
PK_BS = PK_MISC + 8
PK_WP = PK_BS + 8


def _table_row(b):
    if isinstance(b, int):
        return 8 * b if 8 * b < PACK_FINE else 8 * b + PK_TABLE_B - PACK_FINE
    return 8 * b + jnp.where(8 * b < PACK_FINE, 0, PK_TABLE_B - PACK_FINE)


def _dot(a, b):
    return jnp.dot(a, b, preferred_element_type=F32)


def _dot_nt(a, b):
    return lax.dot_general(a, b, (((1,), (1,)), ((), ())), preferred_element_type=F32)


def _dot_tn(a, b):
    return lax.dot_general(a, b, (((0,), (0,)), ((), ())), preferred_element_type=F32)


def _rstd(v):
    return lax.rsqrt(jnp.mean(v * v, axis=-1, keepdims=True) + EPS)


def _rms_bwd(d_hat, hat, rstd):
    return rstd * (d_hat - hat * jnp.mean(d_hat * hat, axis=-1, keepdims=True))


def _rms_bwd_gained(g, gain, hat, rstd):
    g_hat = g * hat
    d_v = rstd * (g * gain - hat * jnp.mean(g_hat * gain, axis=-1, keepdims=True))
    return d_v, _colsum(g_hat)


_K0 = 0.7978845608028654
_K1 = 0.044715


def _gelu_parts(v):
    t = jnp.tanh(v * (_K0 + (_K0 * _K1) * (v * v)))
    return t, v * (0.5 + 0.5 * t)


def _gelu_grad(v, t):
    return (0.5 + 0.5 * t) + (0.5 * v) * (1.0 - t * t) * (_K0 + (3.0 * _K0 * _K1) * (v * v))


def _colsum(v):
    return jnp.sum(v, axis=0, keepdims=True)


def _full(shape):
    n = len(shape)
    return pl.BlockSpec(shape, lambda *_: (0,) * n)


def _resident(shape):
    n = len(shape)
    return pl.BlockSpec(shape, lambda *_: (0,) * n, pipeline_mode=pl.Buffered(1))


def _place():
    x, y, c = lax.axis_index("x"), lax.axis_index("y"), lax.axis_index("c")
    return x, y, c


def _flip(v, bit):
    return 1 - v if bit else v


def _peer(x, y, c, k):
    return (_flip(x, (k >> 2) & 1), _flip(y, (k >> 1) & 1), _flip(c, k & 1))


def _index(p):
    return 4 * p[0] + 2 * p[1] + p[2]


def _cast_shards(shards):
    def body(*refs):
        for src, dst in zip(refs[:len(shards)], refs[len(shards):]):
            dst[...] = src[...].astype(BF16)

    vm = pl.BlockSpec(memory_space=pltpu.VMEM)
    return pl.pallas_call(
        body, name="cast_shards", out_shape=tuple(jax.ShapeDtypeStruct(s.shape, BF16) for s in shards),
        in_specs=[vm] * len(shards), out_specs=tuple([vm] * len(shards)),
    )(*shards)


FC_EARLY = 6
FC_HEAD = 2
R_HEAD_COLS = (FC_EARLY - FC_HEAD) * FF_BLK
WGRAD_ORDER = (7, 6, 1, 3, 5, 2, 4, 0)


def _early_col(j):
    return R_HEAD_COLS + j * FF_BLK if j < FC_HEAD else (j - FC_HEAD) * FF_BLK


class _Copies:
    def __init__(self, entries, send_sems, recv_sems):
        self.place = _place()
        self.entries, self.send_sems, self.recv_sems = entries, send_sems, recv_sems

    def _copy(self, i, arrival=False):
        src, dst, rel = self.entries[i]
        return pltpu.make_async_remote_copy(
            src_ref=dst if arrival else src, dst_ref=dst, send_sem=self.send_sems.at[i],
            recv_sem=self.recv_sems.at[i], device_id=_peer(*self.place, rel), device_id_type=MESH)

    def start(self, *which):
        for i in which:
            self._copy(i).start()

    def wait_recv(self, *which):
        for i in which:
            self._copy(i, arrival=True).wait_recv()

    def wait_send(self, *which):
        for i in which:
            self._copy(i).wait_send()


def _tail_comm(parts, small, row_chunk, tt, dz, dx1, x, mod, n1pre, w_in_t):
    n = len(parts)
    t_len = x.shape[0]
    nt = t_len // tt

    def body(*refs):
        p_refs, small_ref = refs[:n], refs[n]
        dz_ref, dx1_ref, x_ref, mod_ref, n1pre_ref, win_ref = refs[n + 1:n + 7]
        outs = refs[n + 7:]
        g_refs, total_ref, gx_ref = outs[:n], outs[n], outs[n + 1]
        scr = outs[n + 2:]
        from_sib = scr[0:n]
        chip_out = scr[n:2 * n]
        chip_in = scr[2 * n:3 * n]
        pack, pack_sib, fine, bulk, total_scr, head, pack_sib2 = scr[3 * n:3 * n + 7]
        send_a, recv_a, send_b, recv_b, send_s, recv_s = scr[3 * n + 7:]
        step = pl.program_id(0)
        x, y, c = _place()
        me = _index((x, y, c))
        sibling = (x, y, 1 - c)
        my_chip = 2 * x + y
        others = [(1 - x, y), (x, 1 - y), (1 - x, 1 - y)]
        my_half = pl.ds(pl.multiple_of(PACK_HALF * c, 8), PACK_HALF)

        def pack_to_sibling(final=False):
            k = 8 if final else 0
            return pltpu.make_async_remote_copy(
                src_ref=pack, dst_ref=pack_sib2 if final else pack_sib, send_sem=send_s.at[k],
                recv_sem=recv_s.at[k], device_id=sibling, device_id_type=MESH)

        def fill_pack_head():
            pack[pl.ds(pl.multiple_of(_table_row(me), 8), 8), :] = small_ref[0:8, :] + head[0:8, :]
            pack[PK_MISC:PK_MISC + 8, :] = small_ref[ROW_N1PRE:ROW_N1PRE + 8, :] + head[8:16, :]

        def half_to_chip(r, part):
            buf = (fine, bulk)[part]
            return pltpu.make_async_remote_copy(
                src_ref=buf.at[my_chip], dst_ref=buf.at[my_chip],
                send_sem=send_s.at[1 + 3 * part + r], recv_sem=recv_s.at[1 + 3 * part + r],
                device_id=(*others[r], c), device_id_type=MESH)

        def half_from_chip(r, part):
            k = 2 * others[r][0] + others[r][1]
            buf = (fine, bulk)[part]
            return pltpu.make_async_remote_copy(
                src_ref=buf.at[k], dst_ref=buf.at[k],
                send_sem=send_s.at[1 + 3 * part + r], recv_sem=recv_s.at[1 + 3 * part + r],
                device_id=(*others[r], c), device_id_type=MESH)

        def total_to_sibling():
            return pltpu.make_async_remote_copy(
                src_ref=total_scr.at[my_half], dst_ref=total_scr.at[my_half],
                send_sem=send_s.at[7], recv_sem=recv_s.at[7], device_id=sibling, device_id_type=MESH)

        def total_from_sibling():
            sib_half = pl.ds(pl.multiple_of(PACK_HALF * (1 - c), 8), PACK_HALF)
            return pltpu.make_async_remote_copy(
                src_ref=total_scr.at[sib_half], dst_ref=total_scr.at[sib_half],
                send_sem=send_s.at[7], recv_sem=recv_s.at[7], device_id=sibling, device_id_type=MESH)

        def to_sibling(a, k):
            return pltpu.make_async_remote_copy(
                src_ref=p_refs[a].at[2 * k + (1 - c)], dst_ref=from_sib[a].at[k],
                send_sem=send_a.at[a], recv_sem=recv_a.at[a], device_id=sibling, device_id_type=MESH)

        def all_from_sibling(a):
            return pltpu.make_async_remote_copy(
                src_ref=from_sib[a], dst_ref=from_sib[a], send_sem=send_a.at[a], recv_sem=recv_a.at[a],
                device_id=sibling, device_id_type=MESH)

        def to_chip(a, r):
            return pltpu.make_async_remote_copy(
                src_ref=chip_out[a].at[r], dst_ref=chip_in[a].at[r],
                send_sem=send_b.at[3 * a + r], recv_sem=recv_b.at[3 * a + r],
                device_id=(*others[r], c), device_id_type=MESH)

        @pl.when(step == 0)
        def _():
            head[...] = jnp.zeros_like(head)
            for a in range(n):
                for k in range(4):
                    to_sibling(a, k).start()

        @pl.when(step == min(1, nt - 1))
        def _():
            pack[0:PACK_FINE, :] = jnp.zeros((PACK_FINE, D), F32)
            pack[PK_TABLE_B:PK_MISC, :] = jnp.zeros((PK_MISC - PK_TABLE_B, D), F32)
            fill_pack_head()
            pack[PK_WS:PK_WS + 64, :] = small_ref[ROW_WS:ROW_WS + 64, :]
            pack[PK_BS:PK_BS + 8, :] = small_ref[ROW_BS:ROW_BS + 8, :]
            pack[PK_WP:PK_WP + 64, :] = small_ref[ROW_WP:ROW_WP + 64, :]
            pack_to_sibling().start()
            for a in range(n):
                all_from_sibling(a).wait_recv()
                rows = p_refs[a].shape[1]
                for r in range(3):
                    k = 2 * others[r][0] + others[r][1]
                    for s in range(0, rows, row_chunk):
                        sl = pl.ds(s, row_chunk)
                        chip_out[a][r, sl, :] = (p_refs[a][2 * k + c, sl, :].astype(F32)
                                                 + from_sib[a][k, sl, :].astype(F32)).astype(BF16)
                    to_chip(a, r).start()
                for s in range(0, rows, row_chunk):
                    sl = pl.ds(s, row_chunk)
                    g_refs[a][sl, :] = (p_refs[a][2 * my_chip + c, sl, :].astype(F32)
                                        + from_sib[a][my_chip, sl, :].astype(F32))

        @pl.when(step == min(2, nt - 1))
        def _():
            pack_to_sibling().wait_recv()
            bulk_rows = pl.ds(pl.multiple_of(PACK_HALF * c + PACK_FINE, 8), PACK_HALF - PACK_FINE)
            bulk[my_chip] = (pack[bulk_rows, :] + pack_sib[bulk_rows, :]).astype(BF16)
            for r in range(3):
                half_to_chip(r, 1).start()

        dh1 = _dot(dz_ref[...], win_ref[...])
        xv = x_ref[...]
        r1 = _rstd(xv)
        xhat = xv * r1
        scale1 = mod_ref[1:2, :]
        pre1 = n1pre_ref[...]
        d_x, sum_h = _rms_bwd_gained(dh1, pre1 * (1.0 + scale1), xhat, r1)
        gx_ref[...] = dx1_ref[...] + d_x
        head[ROW_DMOD:ROW_DMOD + 1, :] += _colsum(dh1)
        head[ROW_DMOD + 1:ROW_DMOD + 2, :] += pre1 * sum_h
        head[ROW_N1PRE:ROW_N1PRE + 1, :] += (1.0 + scale1) * sum_h

        @pl.when(step == nt - 1)
        def _():
            pack_to_sibling().wait_send()
            fill_pack_head()
            pack_to_sibling(final=True).start()
            pack_to_sibling(final=True).wait_recv()
            fine_rows = pl.ds(pl.multiple_of(PACK_HALF * c, 8), PACK_FINE)
            fine[my_chip] = pack[fine_rows, :] + pack_sib2[fine_rows, :]
            for r in range(3):
                half_to_chip(r, 0).start()
            for a in range(n):
                rows = p_refs[a].shape[1]
                for r in range(3):
                    to_chip(a, r).wait_recv()
                    for s in range(0, rows, row_chunk):
                        sl = pl.ds(s, row_chunk)
                        g_refs[a][sl, :] = g_refs[a][sl, :] + chip_in[a][r, sl, :].astype(F32)
            for r in range(3):
                half_from_chip(r, 0).wait_recv()
                half_from_chip(r, 1).wait_recv()
            half_start = pl.multiple_of(PACK_HALF * c, 8)
            total_scr[pl.ds(half_start, PACK_FINE), :] = ((fine[0] + fine[1]) + fine[2]) + fine[3]
            total_scr[pl.ds(half_start + PACK_FINE, PACK_HALF - PACK_FINE), :] = (
                (bulk[0].astype(F32) + bulk[1].astype(F32)) + bulk[2].astype(F32)) + bulk[3].astype(F32)
            total_to_sibling().start()
            total_from_sibling().wait_recv()
            total_ref[...] = total_scr[...]
            for a in range(n):
                all_from_sibling(a).wait_send()
                for r in range(3):
                    to_chip(a, r).wait_send()
            pack_to_sibling(final=True).wait_send()
            for r in range(3):
                half_to_chip(r, 0).wait_send()
                half_to_chip(r, 1).wait_send()
            total_to_sibling().wait_send()

    tile = lambda w: pl.BlockSpec((tt, w), lambda i: (i, 0))
    return pl.pallas_call(
        body, name="tail_comm", grid=(nt,),
        out_shape=tuple([jax.ShapeDtypeStruct(p.shape[1:], F32) for p in parts]
                        + [jax.ShapeDtypeStruct((PACK_ROWS, D), F32), jax.ShapeDtypeStruct((t_len, D), F32)]),
        in_specs=([_resident(p.shape) for p in parts] + [_resident(small.shape)]
                  + [tile(D_Z), tile(D), tile(D), _full((8, D)), _full((1, D)), _resident((D_Z, D))]),
        out_specs=tuple([_full(p.shape[1:]) for p in parts] + [_full((PACK_ROWS, D)), tile(D)]),
        scratch_shapes=(
            [pltpu.VMEM((4,) + p.shape[1:], BF16) for p in parts]
            + [pltpu.VMEM((3,) + p.shape[1:], BF16) for p in parts]
            + [pltpu.VMEM((3,) + p.shape[1:], BF16) for p in parts]
            + [pltpu.VMEM((PACK_ROWS, D), F32), pltpu.VMEM((PACK_ROWS, D), F32),
               pltpu.VMEM((4, PACK_FINE, D), F32), pltpu.VMEM((4, PACK_HALF - PACK_FINE, D), BF16),
               pltpu.VMEM((PACK_ROWS, D), F32), pltpu.VMEM((16, D), F32), pltpu.VMEM((PACK_ROWS, D), F32)]
            + [pltpu.SemaphoreType.DMA((n,)), pltpu.SemaphoreType.DMA((n,)),
               pltpu.SemaphoreType.DMA((3 * n,)), pltpu.SemaphoreType.DMA((3 * n,)),
               pltpu.SemaphoreType.DMA((9,)), pltpu.SemaphoreType.DMA((9,))]),
        compiler_params=pltpu.CompilerParams(dimension_semantics=("arbitrary",), vmem_limit_bytes=VMEM_LIMIT),
    )(*parts, small, dz, dx1, x, mod, n1pre, w_in_t)


def _tril_mask():
    row = lax.broadcasted_iota(jnp.int32, (CHUNK, CHUNK), 0)
    col = lax.broadcasted_iota(jnp.int32, (CHUNK, CHUNK), 1)
    return (col <= row).astype(F32)


def _window_sums(ext):
    s2 = ext + pltpu.roll(ext, 1, 0)
    t4 = s2[:, GROUP:]
    s4 = t4 + pltpu.roll(t4, 2, 0)
    t8 = s4[:, GROUP:]
    s8 = t8 + pltpu.roll(t8, 4, 0)
    t16 = s8[:, GROUP:]
    s16 = t16 + pltpu.roll(t16, 8, 0)
    return [s2[:, :GROUP], s4[:, :GROUP], s8[:, :GROUP], s16]


def _inv_counts(first_pos, rows):
    pos = first_pos + lax.broadcasted_iota(jnp.int32, (rows, 1), 0)
    return [1.0 / jnp.minimum(pos + 1, w).astype(F32) for w in WINDOWS]


def _pool_diff(zb, halo, first_pos):
    tt = zb.shape[0]
    sums = _window_sums(jnp.concatenate([halo, zb], axis=0))
    inv = _inv_counts(first_pos, tt)
    return [sums[g][HALO:, :] * inv[g] - zb[:, g * GROUP:(g + 1) * GROUP] for g in range(len(WINDOWS))]


def _row_blocks(scr, rows, place):
    def block(rel):
        start = pl.multiple_of(rows * _index(_peer(*place, rel)), rows)
        return scr.at[pl.ds(start, rows), :]

    def entries(shard_ref):
        return ([(shard_ref, block(0), rel) for rel in (1, 2, 4, 6)]
                + [(block(rel), block(rel), 1) for rel in (2, 4, 6)])
    return block, entries


def _attn_fwd(tt, x, c_row, w_ada, b_pieces, n1pre, n1post, w_in_shard, w_out_shard, w_sp, bs_rows, ln_g, ln_b,
              w_pool, b_pool, pool_scale, fc_shards, n2pre):
    t_len = x.shape[0]
    nt = t_len // tt
    ncol = w_ada.shape[1]

    def body(x_ref, xb_ref, c_ref, wada_ref, b_ref, n1pre_ref, n1post_ref, wi_ref, wo_ref, wsp_ref, bs_ref,
             lng_ref, lnb_ref, wp_ref, bp_ref, ps_ref, w1_ref, w2_ref, n2pre_ref,
             z_ref, cat_ref, mix_ref, x1_ref, h2_ref, r_ref, f_ref, mod_out, sc_out, e1_ref, e2_ref, wout_ref,
             win_out, carry, land1, land2, sib1, sib2, cat_keep, wout_scr, win_ref, mod_ref, cg, mg, part,
             send_sems, recv_sems, local_sems, wo_send, wo_recv, wi_send, wi_recv, ada_send, ada_recv):
        i = pl.program_id(0)
        place = px, py, pc = _place()
        me = _index(place)
        wo_block, wo_entries = _row_blocks(wout_scr, w_out_shard.shape[0], place)
        wi_block, wi_entries = _row_blocks(win_ref, w_in_shard.shape[0], place)
        wo_copies = _Copies(wo_entries(wo_ref), wo_send, wo_recv)
        wi_copies = _Copies(wi_entries(wi_ref), wi_send, wi_recv)
        wo_keep = pltpu.make_async_copy(wout_scr, wout_ref, local_sems.at[8])
        wi_keep = pltpu.make_async_copy(win_ref, win_out, local_sems.at[9])
        ada = _Copies([(cg.at[me], cg.at[me], k) for k in range(1, N_DEV)]
                      + [(part, mg.at[me], k) for k in range(1, N_DEV)], ada_send, ada_recv)
        copies = _Copies(
            [(w1_ref, sib1, 1), (w2_ref, sib2, 1),
             (w1_ref, land1.at[0], 2), (w2_ref, land2.at[0], 2),
             (w1_ref, land1.at[1], 4), (w2_ref, land2.at[1], 4),
             (land1.at[0], e1_ref.at[3], 1), (land2.at[0], e2_ref.at[3], 1),
             (land1.at[1], e1_ref.at[5], 1), (land2.at[1], e2_ref.at[5], 1)],
            send_sems, recv_sems)
        keep = [pltpu.make_async_copy(w1_ref, e1_ref.at[0], local_sems.at[0]),
                pltpu.make_async_copy(w2_ref, e2_ref.at[0], local_sems.at[1]),
                pltpu.make_async_copy(land1.at[0], e1_ref.at[2], local_sems.at[2]),
                pltpu.make_async_copy(land1.at[1], e1_ref.at[4], local_sems.at[3]),
                pltpu.make_async_copy(land2.at[0], e2_ref.at[2], local_sems.at[4]),
                pltpu.make_async_copy(land2.at[1], e2_ref.at[4], local_sems.at[5]),
                pltpu.make_async_copy(sib1, e1_ref.at[1], local_sems.at[6]),
                pltpu.make_async_copy(sib2, e2_ref.at[1], local_sems.at[7])]

        @pl.when(i == 0)
        def _():
            cg[me] = jnp.broadcast_to(c_ref[...], (8, D))
            ada.start(*range(N_DEV - 1))
            wi_copies.start(0, 1, 2, 3)
            wi_rows, wo_rows = w_in_shard.shape[0], w_out_shard.shape[0]
            win_ref[pl.ds(pl.multiple_of(wi_rows * me, wi_rows), wi_rows), :] = wi_ref[...]
            wout_scr[pl.ds(pl.multiple_of(wo_rows * me, wo_rows), wo_rows), :] = wo_ref[...]
            carry[...] = jnp.zeros_like(carry)

            ada.wait_recv(*range(N_DEV - 1))
            c_all = jnp.concatenate([cg[j, 0:1, :] for j in range(N_DEV)], axis=0)
            sc = c_all * jax.nn.sigmoid(c_all)
            sc_out[...] = sc
            part[...] = _dot(sc.astype(BF16), wada_ref[...].astype(BF16)) + b_ref[me]
            ada.start(*range(N_DEV - 1, 2 * (N_DEV - 1)))
            wo_copies.start(0, 1, 2, 3)
            copies.start(0, 1, 2, 4, 3, 5)
            keep[0].start()
            keep[1].start()
            mg[me] = part[...]
            ada.wait_recv(*range(N_DEV - 1, 2 * (N_DEV - 1)))
            mod_ref[...] = jnp.zeros_like(mod_ref)
            for j in range(N_DEV):
                for m in range(6):
                    lo, hi = max(ncol * j, D * m), min(ncol * (j + 1), D * (m + 1))
                    if lo < hi:
                        mod_ref[m:m + 1, lo - D * m:hi - D * m] = mg[j, pl.ds(me, 1), lo - ncol * j:hi - ncol * j]
            mod_out[...] = mod_ref[...]

            wi_copies.wait_recv(1, 2, 3)
            wi_copies.start(4, 5, 6)
            wi_copies.wait_recv(0, 4, 5, 6)
            wi_keep.start()

        @pl.when(i == nt // 2 + ATTN_LAG)
        def _():
            copies.wait_recv(2, 4)
            copies.start(6, 8)
            keep[2].start()
            keep[3].start()

        @pl.when(i == nt - 1 + ATTN_LAG)
        def _():
            copies.wait_recv(3, 5)
            copies.start(7, 9)
            keep[4].start()
            keep[5].start()

        shift1, scale1, gate1 = mod_ref[0:1, :], mod_ref[1:2, :], mod_ref[2:3, :]

        @pl.when(i < nt)
        def _():
            xv = x_ref[...]
            h1 = (xv * _rstd(xv)) * (n1pre_ref[...] * (1.0 + scale1)) + shift1
            z = _dot_nt(h1.astype(BF16), win_ref[...])
            z_ref[...] = z

            _, ga = _gelu_parts(z[:, :2 * D_A])
            u, vr = ga[:, :D_A], ga[:, D_A:]
            dv = vr - jnp.mean(vr, axis=-1, keepdims=True)
            v = (dv * lax.rsqrt(jnp.mean(dv * dv, axis=-1, keepdims=True) + EPS)) * lng_ref[...] + lnb_ref[...]
            vb = v.astype(BF16)
            mask = _tril_mask()
            wc = [(wsp_ref[h] * mask).astype(BF16) for h in range(N_HEADS)]
            for ch in range(tt // CHUNK):
                rows = slice(ch * CHUNK, (ch + 1) * CHUNK)
                for h in range(N_HEADS):
                    cols = slice(h * GROUP, (h + 1) * GROUP)
                    mixed = _dot(wc[h], vb[rows, cols]) + bs_ref[:, cols]
                    cat_ref[rows, cols] = (u[rows, cols] * mixed).astype(BF16)

            zb = z[:, 2 * D_A:]
            diff = _pool_diff(zb, carry[...], i * tt)
            carry[...] = zb[tt - HALO:, :]
            for g in range(len(WINDOWS)):
                cols = slice(g * GROUP, (g + 1) * GROUP)
                pre = _dot(diff[g].astype(BF16), wp_ref[g].astype(BF16)) + bp_ref[:, cols]
                cat_ref[:, D_A + g * GROUP:D_A + (g + 1) * GROUP] = (pre * ps_ref[:, cols]).astype(BF16)
            cat_keep[i % (ATTN_LAG + 1)] = cat_ref[...]

        @pl.when(i == 0)
        def _():
            copies.wait_recv(0, 1)
            keep[6].start()
            keep[7].start()

        @pl.when(i == 1)
        def _():
            wo_copies.wait_recv(1, 2, 3)
            wo_copies.start(4, 5, 6)

        @pl.when(i == ATTN_LAG)
        def _():
            wo_copies.wait_recv(0, 4, 5, 6)
            wo_keep.start()

        @pl.when(i >= ATTN_LAG)
        def _():
            xv = xb_ref[...]
            mix = _dot(cat_keep[(i - ATTN_LAG) % (ATTN_LAG + 1)], wout_scr[...])
            mix_ref[...] = mix
            x1v = xv + (mix * _rstd(mix)) * (gate1 * n1post_ref[...])
            x1_ref[...] = x1v
            shift2, scale2 = mod_ref[3:4, :], mod_ref[4:5, :]
            h2 = ((x1v * _rstd(x1v)) * (n2pre_ref[...] * (1.0 + scale2)) + shift2).astype(BF16)
            h2_ref[...] = h2
            for j, (w1, w2) in enumerate(((w1_ref, w2_ref), (sib1, sib2))):
                ra = jnp.maximum(_dot(h2, w1[...]), 0.0)
                r = (ra * ra).astype(BF16)
                r_ref[:, j * FF_BLK:(j + 1) * FF_BLK] = r
                if j == 0:
                    f_ref[...] = _dot(r, w2[...])
                else:
                    f_ref[...] += _dot(r, w2[...])

        @pl.when(i == nt - 1 + ATTN_LAG)
        def _():
            copies.wait_recv(6, 7, 8, 9)
            copies.wait_send(*range(10))
            wo_copies.wait_send(*range(7))
            wi_copies.wait_send(*range(7))
            ada.wait_send(*range(2 * (N_DEV - 1)))
            for cp in keep:
                cp.wait()
            wo_keep.wait()
            wi_keep.wait()

    first = lambda w: pl.BlockSpec((tt, w), lambda i: (jnp.minimum(i, nt - 1), 0))
    second = lambda w: pl.BlockSpec((tt, w), lambda i: (jnp.maximum(i - ATTN_LAG, 0), 0))
    r_head = pl.BlockSpec((tt, FC_HEAD * FF_BLK),
                          lambda i: (jnp.maximum(i - ATTN_LAG, 0), R_HEAD_COLS // (FC_HEAD * FF_BLK)))
    hbm = pl.BlockSpec(memory_space=pl.ANY)
    outs = pl.pallas_call(
        body, name="attn_fwd", grid=(nt + ATTN_LAG,),
        out_shape=tuple([jax.ShapeDtypeStruct((t_len, D_Z), F32), jax.ShapeDtypeStruct((t_len, D), BF16),
                         jax.ShapeDtypeStruct((t_len, D), F32), jax.ShapeDtypeStruct((t_len, D), F32),
                         jax.ShapeDtypeStruct((t_len, D), BF16),
                         jax.ShapeDtypeStruct((t_len, FC_EARLY * FF_BLK), BF16),
                         jax.ShapeDtypeStruct((t_len, D), F32)]
                        + [jax.ShapeDtypeStruct((8, D), F32), jax.ShapeDtypeStruct((N_DEV, D), F32)]
                        + [jax.ShapeDtypeStruct((FC_EARLY,) + s.shape, BF16) for s in fc_shards]
                        + [jax.ShapeDtypeStruct((D, D), BF16), jax.ShapeDtypeStruct((D_Z, D), BF16)]),
        in_specs=[first(D), second(D), _full((1, D)), _resident(w_ada.shape), _full((N_DEV, 1, ncol)), _full((1, D)),
                  _full((1, D)), _resident(w_in_shard.shape), _resident(w_out_shard.shape),
                  _full((N_HEADS, CHUNK, CHUNK)), _full((CHUNK, D_A)), _full((1, D_A)), _full((1, D_A)),
                  _full((len(WINDOWS), GROUP, GROUP)), _full((1, D_B)), _full((1, D_B)),
                  _resident(fc_shards[0].shape), _resident(fc_shards[1].shape), _full((1, D))],
        out_specs=(first(D_Z), first(D), second(D), second(D), second(D), r_head, second(D),
                   _full((8, D)), _full((N_DEV, D)), hbm, hbm, hbm, hbm),
        scratch_shapes=[pltpu.VMEM((HALO, D_B), F32),
                        pltpu.VMEM((2,) + fc_shards[0].shape, BF16), pltpu.VMEM((2,) + fc_shards[1].shape, BF16),
                        pltpu.VMEM(fc_shards[0].shape, BF16), pltpu.VMEM(fc_shards[1].shape, BF16),
                        pltpu.VMEM((ATTN_LAG + 1, tt, D), BF16), pltpu.VMEM((D, D), BF16),
                        pltpu.VMEM((D_Z, D), BF16), pltpu.VMEM((8, D), F32),
                        pltpu.VMEM((N_DEV, 8, D), F32), pltpu.VMEM((N_DEV, N_DEV, ncol), F32),
                        pltpu.VMEM((N_DEV, ncol), F32),
                        pltpu.SemaphoreType.DMA((10,)), pltpu.SemaphoreType.DMA((10,)),
                        pltpu.SemaphoreType.DMA((10,)),
                        pltpu.SemaphoreType.DMA((7,)), pltpu.SemaphoreType.DMA((7,)),
                        pltpu.SemaphoreType.DMA((7,)), pltpu.SemaphoreType.DMA((7,)),
                        pltpu.SemaphoreType.DMA((2 * (N_DEV - 1),)), pltpu.SemaphoreType.DMA((2 * (N_DEV - 1),))],
        compiler_params=pltpu.CompilerParams(dimension_semantics=("arbitrary",), vmem_limit_bytes=VMEM_LIMIT),
    )(x, x, c_row, w_ada, b_pieces, n1pre, n1post, w_in_shard, w_out_shard, w_sp, bs_rows, ln_g, ln_b, w_pool, b_pool,
      pool_scale, *fc_shards, n2pre)
    return outs[:9], outs[9:11], outs[11], outs[12]


def _mlp_fwd_early(tt, r_begun, h2, f_head, w1_early, w2_early):
    t_len = h2.shape[0]
    nt = t_len // tt
    n_late = N_DEV - FC_EARLY

    def body(r_begun_ref, h2_ref, fh_ref, w1_ref, w2_ref, r_ref, f_ref, l1_ref, l2_ref,
             land1, land2, send_sems, recv_sems, local_sems):
        i = pl.program_id(0)
        copies = _Copies(
            [(w1_ref.at[2], land1, 4), (w2_ref.at[4], land2, 2),
             (land1, l1_ref.at[1], 1), (land2, l2_ref.at[1], 1)],
            send_sems, recv_sems)
        keep = [pltpu.make_async_copy(land1, l1_ref.at[0], local_sems.at[0]),
                pltpu.make_async_copy(land2, l2_ref.at[0], local_sems.at[1])]

        @pl.when(i == 0)
        def _():
            copies.start(0, 1)

        @pl.when(i == nt - 1)
        def _():
            copies.wait_recv(0, 1)
            copies.start(2, 3)
            for cp in keep:
                cp.start()

        h2 = h2_ref[...]
        f_ref[...] = fh_ref[...]
        for j in range(FC_HEAD, FC_EARLY):
            ra = jnp.maximum(_dot(h2, w1_ref[j]), 0.0)
            r = (ra * ra).astype(BF16)
            r_ref[:, _early_col(j):_early_col(j) + FF_BLK] = r
            f_ref[...] += _dot(r, w2_ref[j])

        @pl.when(i == nt - 1)
        def _():
            copies.wait_recv(2, 3)
            copies.wait_send(0, 1, 2, 3)
            for cp in keep:
                cp.wait()

    tile = lambda w: pl.BlockSpec((tt, w), lambda i: (i, 0))
    hbm = pl.BlockSpec(memory_space=pl.ANY)
    outs = pl.pallas_call(
        body, name="mlp_fwd_early", grid=(nt,),
        out_shape=(jax.ShapeDtypeStruct((t_len, FC_EARLY * FF_BLK), BF16), jax.ShapeDtypeStruct((t_len, D), F32),
                   jax.ShapeDtypeStruct((n_late,) + w1_early.shape[1:], BF16),
                   jax.ShapeDtypeStruct((n_late,) + w2_early.shape[1:], BF16)),
        in_specs=[hbm, tile(D), tile(D), _resident((FC_EARLY, D, FF_BLK)), _resident((FC_EARLY, FF_BLK, D))],
        out_specs=(tile(R_HEAD_COLS), tile(D), hbm, hbm),
        input_output_aliases={0: 0},
        scratch_shapes=[pltpu.VMEM(w1_early.shape[1:], BF16), pltpu.VMEM(w2_early.shape[1:], BF16),
                        pltpu.SemaphoreType.DMA((4,)), pltpu.SemaphoreType.DMA((4,)),
                        pltpu.SemaphoreType.DMA((2,))],
        compiler_params=pltpu.CompilerParams(dimension_semantics=("arbitrary",), vmem_limit_bytes=VMEM_LIMIT),
    )(r_begun, h2, f_head, w1_early, w2_early)
    return outs[:2], outs[2:]


def _mlp_late_bwd(tt, r_early, x1, h2, f_early, tgt, mix, mod, n2pre, n2post, n1post,
                  w1_early, w2_early, w1_late, w2_late):
    t_len = x1.shape[0]
    nt = t_len // tt
    n_late = N_DEV - FC_EARLY
    late_cols = n_late * FF_BLK

    def body(re_ref, x1_ref, h2_ref, fe_ref, tgt_ref, mix_ref, mod_ref, n2pre_ref, n2post_ref,
             n1post_ref, w1e_ref, w2e_ref, w1l_ref, w2l_ref,
             rl_ref, df_ref, da_ref, dmix_ref, dx1_ref, redf_ref, redb_ref, dh2_acc):
        i = pl.program_id(0)

        @pl.when(i == 0)
        def _():
            redf_ref[...] = jnp.zeros_like(redf_ref)
            redb_ref[...] = jnp.zeros_like(redb_ref)

        x1v = x1_ref[...]
        gate1, scale2, gate2 = mod_ref[2:3, :], mod_ref[4:5, :], mod_ref[5:6, :]
        h2 = h2_ref[...]
        f = fe_ref[...]
        for j in range(n_late):
            cols = slice(j * FF_BLK, (j + 1) * FF_BLK)
            ra = jnp.maximum(_dot(h2, w1l_ref[j]), 0.0)
            r = (ra * ra).astype(BF16)
            rl_ref[:, cols] = r
            f = f + _dot(r, w2l_ref[j])
        post2 = n2post_ref[...]
        gate_post2 = gate2 * post2
        rf = _rstd(f)
        fhat = f * rf
        err = (x1v + fhat * gate_post2) - tgt_ref[...]
        dy = err * (1.0 / D)
        d_f, sum_f = _rms_bwd_gained(dy, gate_post2, fhat, rf)
        dfv = d_f.astype(BF16)
        df_ref[...] = dfv
        redf_ref[0:1, :] += post2 * sum_f
        redf_ref[1:2, :] += gate2 * sum_f
        redf_ref[2:3, :] += _colsum(err * err)

        for j in range(N_DEV):
            cols = slice(j * FF_BLK, (j + 1) * FF_BLK)
            if j < FC_EARLY:
                w1, w2, r = w1e_ref[j], w2e_ref[j], re_ref[:, _early_col(j):_early_col(j) + FF_BLK]
            else:
                jl = j - FC_EARLY
                w1, w2, r = w1l_ref[jl], w2l_ref[jl], rl_ref[:, jl * FF_BLK:(jl + 1) * FF_BLK]
            dr = _dot_nt(dfv, w2)
            da = (dr * (2.0 * jnp.sqrt(r.astype(F32)))).astype(BF16)
            da_ref[:, cols] = da
            contrib = _dot_nt(da, w1)
            if j == 0:
                dh2_acc[...] = contrib
            else:
                dh2_acc[...] += contrib
        dh2 = dh2_acc[...]
        pre2, post1 = n2pre_ref[...], n1post_ref[...]
        r2 = _rstd(x1v)
        xhat = x1v * r2
        d_x1, sum_h = _rms_bwd_gained(dh2, pre2 * (1.0 + scale2), xhat, r2)
        dx1 = dy + d_x1
        dx1_ref[...] = dx1
        mixv = mix_ref[...]
        rm = _rstd(mixv)
        mhat = mixv * rm
        d_mix, sum_m = _rms_bwd_gained(dx1, gate1 * post1, mhat, rm)
        dmix_ref[...] = d_mix.astype(BF16)
        redb_ref[0:1, :] += _colsum(dh2)
        redb_ref[1:2, :] += pre2 * sum_h
        redb_ref[2:3, :] += (1.0 + scale2) * sum_h
        redb_ref[3:4, :] += post1 * sum_m
        redb_ref[4:5, :] += gate1 * sum_m

    tile = lambda w: pl.BlockSpec((tt, w), lambda i: (i, 0))
    return pl.pallas_call(
        body, name="mlp_late_bwd", grid=(nt,),
        out_shape=(jax.ShapeDtypeStruct((t_len, late_cols), BF16), jax.ShapeDtypeStruct((t_len, D), BF16),
                   jax.ShapeDtypeStruct((t_len, D_FF), BF16), jax.ShapeDtypeStruct((t_len, D), BF16),
                   jax.ShapeDtypeStruct((t_len, D), F32), jax.ShapeDtypeStruct((8, D), F32),
                   jax.ShapeDtypeStruct((8, D), F32)),
        in_specs=[tile(FC_EARLY * FF_BLK), tile(D), tile(D), tile(D), tile(D),
                  tile(D), _full((8, D)), _full((1, D)), _full((1, D)), _full((1, D)),
                  _resident((FC_EARLY, D, FF_BLK)), _resident((FC_EARLY, FF_BLK, D)),
                  _resident((n_late, D, FF_BLK)), _resident((n_late, FF_BLK, D))],
        out_specs=(tile(late_cols), tile(D), tile(D_FF), tile(D), tile(D), _full((8, D)), _full((8, D))),
        scratch_shapes=[pltpu.VMEM((tt, D), F32)],
        compiler_params=pltpu.CompilerParams(dimension_semantics=("arbitrary",), vmem_limit_bytes=VMEM_LIMIT),
    )(r_early, x1, h2, f_early, tgt, mix, mod, n2pre, n2post, n1post, w1_early, w2_early, w1_late, w2_late)


def _mlp_wgrad(tt, r_early, r_late, da, df, h2):
    t_len = df.shape[0]
    nt = t_len // tt
    odd_steps = [j for j, rel in enumerate(WGRAD_ORDER) if rel % 2]

    def relation(j):
        rel = jnp.int32(WGRAD_ORDER[-1])
        for step in range(N_DEV - 2, -1, -1):
            rel = jnp.where(j == step, WGRAD_ORDER[step], rel)
        return rel

    def body(re_ref, rl_ref, da_ref, df_ref, h2_ref, own1_ref, own2_ref, out1_ref, out2_ref, diag1_ref, diag2_ref,
             acc1, acc2, snd1, snd2, sib1, sib2, dsnd1, dsnd2, send_sems, recv_sems):
        j, t = pl.program_id(0), pl.program_id(1)
        rows = pl.ds(pl.multiple_of(t * tt, tt), tt)
        x, y, c = _place()
        accs, snds, sibs = (acc1, acc2), (snd1, snd2), (sib1, sib2)
        dsnds, diags = (dsnd1, dsnd2), (diag1_ref, diag2_ref)

        def to_sibling(a, jj, buf=0):
            return pltpu.make_async_remote_copy(
                src_ref=snds[a].at[buf], dst_ref=sibs[a].at[jj],
                send_sem=send_sems.at[4 * a + jj], recv_sem=recv_sems.at[4 * a + jj],
                device_id=(x, y, 1 - c), device_id_type=MESH)

        def to_diagonal(a):
            return pltpu.make_async_remote_copy(
                src_ref=dsnds[a], dst_ref=diags[a], send_sem=send_sems.at[8 + a], recv_sem=recv_sems.at[8 + a],
                device_id=_peer(x, y, c, 6), device_id_type=MESH)

        @pl.when(t == 0)
        def _():
            acc2[...] = jnp.zeros_like(acc2)
            acc1[...] = jnp.zeros_like(acc1)

        for r_ref, mine in ((re_ref, relation(j) < FC_EARLY), (rl_ref, relation(j) >= FC_EARLY)):
            @pl.when(mine)
            def _():
                acc2[...] += _dot_tn(r_ref[...], df_ref[rows, :])
                acc1[...] += _dot_tn(h2_ref[rows, :], da_ref[...])

        for step, rel in enumerate(WGRAD_ORDER):
            jj = rel // 2

            @pl.when((t == nt - 1) & (j == step))
            def _():
                for a, (own_ref, out_ref) in enumerate(((own1_ref, out1_ref), (own2_ref, out2_ref))):
                    if rel % 2:
                        q = odd_steps.index(step)
                        if q >= 2:
                            to_sibling(a, WGRAD_ORDER[odd_steps[q - 2]] // 2).wait_send()
                        snds[a][q % 2] = accs[a][...].astype(BF16)
                        to_sibling(a, jj, q % 2).start()
                        continue
                    to_sibling(a, jj).wait_recv()
                    chip_sum = accs[a][...] + sibs[a][jj].astype(F32)
                    if rel == 6:
                        dsnds[a][...] = chip_sum.astype(BF16)
                        to_diagonal(a).start()
                    elif rel == 0:
                        own_ref[...] = chip_sum
                    else:
                        out_ref[0] = chip_sum.astype(BF16)
                    if step == N_DEV - 1:
                        for q in (2, 3):
                            to_sibling(a, WGRAD_ORDER[odd_steps[q]] // 2).wait_send()
                        to_diagonal(a).wait_recv()
                        to_diagonal(a).wait_send()

    assert WGRAD_ORDER[-1] == 0 and WGRAD_ORDER[-3:-1] == (2, 4)
    blk = pl.BlockSpec((tt, FF_BLK), lambda j, t: (t, relation(j)))
    early_block = lambda rel: jnp.where(rel < FC_HEAD, rel + FC_EARLY - FC_HEAD, rel - FC_HEAD)
    early = lambda j, t: (jnp.where(relation(j) < FC_EARLY, t, 0),
                          jnp.where(relation(j) < FC_EARLY, early_block(relation(j)), 0))
    late = lambda j, t: (jnp.where(relation(j) < FC_EARLY, 0, t), jnp.maximum(relation(j) - FC_EARLY, 0))
    chip = lambda j, t: (jnp.clip(j - 5, 0, 1), 0, 0)
    hbm = pl.BlockSpec(memory_space=pl.ANY)
    return pl.pallas_call(
        body, name="mlp_wgrad", grid=(N_DEV, nt),
        out_shape=(jax.ShapeDtypeStruct((D, FF_BLK), F32), jax.ShapeDtypeStruct((FF_BLK, D), F32),
                   jax.ShapeDtypeStruct((2, D, FF_BLK), BF16), jax.ShapeDtypeStruct((2, FF_BLK, D), BF16),
                   jax.ShapeDtypeStruct((D, FF_BLK), BF16), jax.ShapeDtypeStruct((FF_BLK, D), BF16)),
        in_specs=[pl.BlockSpec((tt, FF_BLK), early), pl.BlockSpec((tt, FF_BLK), late), blk,
                  _resident((t_len, D)), _resident((t_len, D))],
        out_specs=(_full((D, FF_BLK)), _full((FF_BLK, D)),
                   pl.BlockSpec((1, D, FF_BLK), chip), pl.BlockSpec((1, FF_BLK, D), chip), hbm, hbm),
        scratch_shapes=[pltpu.VMEM((D, FF_BLK), F32), pltpu.VMEM((FF_BLK, D), F32),
                        pltpu.VMEM((2, D, FF_BLK), BF16), pltpu.VMEM((2, FF_BLK, D), BF16),
                        pltpu.VMEM((4, D, FF_BLK), BF16), pltpu.VMEM((4, FF_BLK, D), BF16),
                        pltpu.VMEM((D, FF_BLK), BF16), pltpu.VMEM((FF_BLK, D), BF16),
                        pltpu.SemaphoreType.DMA((10,)), pltpu.SemaphoreType.DMA((10,))],
        compiler_params=pltpu.CompilerParams(dimension_semantics=("arbitrary", "arbitrary"),
                                             vmem_limit_bytes=VMEM_LIMIT),
    )(r_early, r_late, da, df, h2)


def _acc_rows(ref, row0, k, val):
    half = CHUNK // 2
    ref[row0:row0 + half, k * GROUP:(k + 1) * GROUP] += val[:half, :]
    ref[row0:row0 + half, D_A + k * GROUP:D_A + (k + 1) * GROUP] += val[half:, :]


def _attn_bwd(tt, dmix, x, z, cat, mod, n1pre, w_out, w_sp, bs_rows, ln_g, ln_b, w_pool, b_pool, pool_scale,
              red_fwd, red_bwd, chip_sums):
    t_len = z.shape[0]
    nt = t_len // tt
    hb = tt // HALO
    n_sums = len(chip_sums)

    def body(dmix_ref, x_ref, z_ref, zprev_ref, cat_ref, mod_ref, n1pre_ref, wout_ref, wsp_ref,
             bs_ref, lng_ref, lnb_ref, wp_ref, bp_ref, ps_ref, redf_ref, redb_ref, *rest):
        sum_out = rest[:n_sums]
        dz_ref, gwin_ref, gwout_ref, small_ref = rest[n_sums:n_sums + 4]
        sum_in = rest[n_sums + 4:2 * n_sums + 4]
        carry, acc_in, acc_out, dz_scr, bs_acc, send_sems, recv_sems = rest[2 * n_sums + 4:]
        s = pl.program_id(0)
        i = nt - 1 - s
        px, py, pc = _place()

        def chip_copy(a, r):
            return pltpu.make_async_remote_copy(
                src_ref=sum_out[a].at[r], dst_ref=sum_in[a].at[r],
                send_sem=send_sems.at[2 * a + r], recv_sem=recv_sems.at[2 * a + r],
                device_id=_peer(px, py, pc, 2 * (r + 1)), device_id_type=MESH)

        @pl.when(s == 0)
        def _():
            for a in range(n_sums):
                for r in range(2):
                    chip_copy(a, r).start()
            carry[...] = jnp.zeros_like(carry)
            acc_in[...] = jnp.zeros_like(acc_in)
            acc_out[...] = jnp.zeros_like(acc_out)
            bs_acc[...] = jnp.zeros_like(bs_acc)
            small_ref[...] = jnp.zeros_like(small_ref)
            small_ref[ROW_DMOD + 2:ROW_DMOD + 3, :] = redb_ref[3:4, :]
            small_ref[ROW_DMOD + 3:ROW_DMOD + 5, :] = redb_ref[0:2, :]
            small_ref[ROW_DMOD + 5:ROW_DMOD + 6, :] = redf_ref[0:1, :]
            small_ref[ROW_N1POST:ROW_N1POST + 1, :] = redb_ref[4:5, :]
            small_ref[ROW_N2PRE:ROW_N2PRE + 1, :] = redb_ref[2:3, :]
            small_ref[ROW_N2POST:ROW_N2POST + 1, :] = redf_ref[1:2, :]
            small_ref[ROW_LOSS:ROW_LOSS + 1, :] = redf_ref[2:3, :]

        dmixv = dmix_ref[...]
        dcat = _dot_nt(dmixv, wout_ref[...])
        acc_out[...] += _dot_tn(cat_ref[...], dmixv)

        z = z_ref[...]
        t_g, ga = _gelu_parts(z[:, :2 * D_A])
        u, vr = ga[:, :D_A], ga[:, D_A:]
        dv0 = vr - jnp.mean(vr, axis=-1, keepdims=True)
        rv = lax.rsqrt(jnp.mean(dv0 * dv0, axis=-1, keepdims=True) + EPS)
        vhat = dv0 * rv
        vb = (vhat * lng_ref[...] + lnb_ref[...]).astype(BF16)
        mask = _tril_mask()
        wc = [(wsp_ref[h] * mask).astype(BF16) for h in range(N_HEADS)]

        dya = dcat[:, :D_A]
        for h in range(N_HEADS):
            cols = slice(h * GROUP, (h + 1) * GROUP)
            bs_sum = jnp.zeros((CHUNK, GROUP), F32)
            ws_sum = jnp.zeros((CHUNK, CHUNK), F32)
            for ch in range(tt // CHUNK):
                rows = slice(ch * CHUNK, (ch + 1) * CHUNK)
                v_ch = vb[rows, cols]
                mixed = _dot(wc[h], v_ch) + bs_ref[:, cols]
                dy_ch = dya[rows, cols]
                dz_scr[rows, cols] = dy_ch * mixed
                dmixed = dy_ch * u[rows, cols]
                dmb = dmixed.astype(BF16)
                dz_scr[rows, D_A + h * GROUP:D_A + (h + 1) * GROUP] = _dot_tn(wc[h], dmb)
                bs_sum = bs_sum + dmixed
                ws_sum = ws_sum + _dot_nt(dmb, v_ch)
            _acc_rows(bs_acc, 0, h, bs_sum)
            _acc_rows(small_ref, ROW_WS, h, ws_sum)

        dvl = dz_scr[:, D_A:2 * D_A]
        dvhat = dvl * lng_ref[...]
        dvl_vhat = dvl * vhat
        dvr = rv * (dvhat - jnp.mean(dvhat, axis=-1, keepdims=True)
                    - vhat * jnp.mean(dvl_vhat * lng_ref[...], axis=-1, keepdims=True))
        small_ref[ROW_LN:ROW_LN + 1, 0:D_A] += _colsum(dvl_vhat)
        small_ref[ROW_LN:ROW_LN + 1, D_A:D] += _colsum(dvl)
        dga = jnp.concatenate([dz_scr[:, :D_A], dvr], axis=1)
        dza = dga * _gelu_grad(z[:, :2 * D_A], t_g)

        zb = z[:, 2 * D_A:]
        halo_prev = jnp.where(i == 0, 0.0, zprev_ref[...])
        diff = _pool_diff(zb, halo_prev, i * tt)
        dyb = dcat[:, D_A:]
        inv = _inv_counts(i * tt, tt)
        scaled, ddiffs = [], []
        for g in range(len(WINDOWS)):
            cols = slice(g * GROUP, (g + 1) * GROUP)
            db = diff[g].astype(BF16)
            wpg = wp_ref[g].astype(BF16)
            pre = _dot(db, wpg) + bp_ref[:, cols]
            small_ref[ROW_POOL:ROW_POOL + 1, cols] += _colsum(dyb[:, cols] * pre)
            dpre = dyb[:, cols] * ps_ref[:, cols]
            small_ref[ROW_POOL:ROW_POOL + 1, D_B + g * GROUP:D_B + (g + 1) * GROUP] += _colsum(dpre)
            dpb = dpre.astype(BF16)
            _acc_rows(small_ref, ROW_WP, g, _dot_tn(db, dpb))
            ddiff = _dot_nt(dpb, wpg)
            ddiffs.append(ddiff)
            scaled.append(ddiff * inv[g])
        scaled_all = jnp.concatenate(scaled, axis=1)
        ext = jnp.concatenate([scaled_all, carry[...]], axis=0)
        n_ext = tt + HALO
        s2 = ext + pltpu.roll(ext, n_ext - 1, 0)
        t4 = s2[:, GROUP:]
        s4 = t4 + pltpu.roll(t4, n_ext - 2, 0)
        t8 = s4[:, GROUP:]
        s8 = t8 + pltpu.roll(t8, n_ext - 4, 0)
        t16 = s8[:, GROUP:]
        s16 = t16 + pltpu.roll(t16, n_ext - 8, 0)
        back = [s2[:, :GROUP], s4[:, :GROUP], s8[:, :GROUP], s16]
        carry[...] = scaled_all[:HALO, :]
        dzb = jnp.concatenate([back[g][:tt, :] - ddiffs[g] for g in range(len(WINDOWS))], axis=1)

        dzv = jnp.concatenate([dza, dzb], axis=1).astype(BF16)
        dz_ref[...] = dzv
        xv = x_ref[...]
        h1 = (xv * _rstd(xv) * (n1pre_ref[...] * (1.0 + mod_ref[1:2, :])) + mod_ref[0:1, :]).astype(BF16)
        acc_in[...] += _dot_tn(dzv, h1)

        @pl.when(s == nt - 1)
        def _():
            gwin_ref[...] = acc_in[...].astype(BF16)
            gwout_ref[...] = acc_out[...].astype(BF16)
            bs = _unfold(bs_acc[...])
            for h in range(N_HEADS):
                small_ref[ROW_BS + h:ROW_BS + h + 1, 0:GROUP] = jnp.sum(
                    bs[:, h * GROUP:(h + 1) * GROUP].T, axis=0, keepdims=True)
            for a in range(n_sums):
                for r in range(2):
                    chip_copy(a, r).wait_recv()
                    chip_copy(a, r).wait_send()

    rev = lambda w: pl.BlockSpec((tt, w), lambda s: (nt - 1 - s, 0))
    zprev = pl.BlockSpec((HALO, D_B), lambda s: (jnp.maximum((nt - 1 - s) * hb - 1, 0), 2))
    hbm = pl.BlockSpec(memory_space=pl.ANY)
    outs = pl.pallas_call(
        body, name="attn_bwd", grid=(nt,),
        out_shape=tuple([jax.ShapeDtypeStruct((t_len, D_Z), BF16), jax.ShapeDtypeStruct((D_Z, D), BF16),
                         jax.ShapeDtypeStruct((D, D), BF16), jax.ShapeDtypeStruct((SMALL_ROWS, D), F32)]
                        + [jax.ShapeDtypeStruct(cs.shape, cs.dtype) for cs in chip_sums]),
        in_specs=[rev(D), rev(D), rev(D_Z), zprev, rev(D), _full((8, D)), _full((1, D)),
                  _resident((D, D)), _full((N_HEADS, CHUNK, CHUNK)), _full((CHUNK, D_A)),
                  _full((1, D_A)), _full((1, D_A)), _full((len(WINDOWS), GROUP, GROUP)), _full((1, D_B)),
                  _full((1, D_B)), _full((8, D)), _full((8, D))] + [_resident(cs.shape) for cs in chip_sums],
        out_specs=tuple([rev(D_Z), _resident((D_Z, D)), _resident((D, D)), _full((SMALL_ROWS, D))]
                        + [hbm] * n_sums),
        scratch_shapes=[pltpu.VMEM((HALO, D_B), F32), pltpu.VMEM((D_Z, D), F32), pltpu.VMEM((D, D), F32),
                        pltpu.VMEM((tt, 2 * D_A), F32), pltpu.VMEM((CHUNK // 2, D), F32),
                        pltpu.SemaphoreType.DMA((2 * n_sums,)), pltpu.SemaphoreType.DMA((2 * n_sums,))],
        compiler_params=pltpu.CompilerParams(dimension_semantics=("arbitrary",), vmem_limit_bytes=VMEM_LIMIT),
    )(dmix, x, z, z, cat, mod, n1pre, w_out, w_sp, bs_rows, ln_g, ln_b, w_pool, b_pool, pool_scale,
      red_fwd, red_bwd, *chip_sums)
    return outs[:4], outs[4:]


def _adam(w, g, m, v):
    m2 = ADAM_B1 * m + (1.0 - ADAM_B1) * g
    v2 = ADAM_B2 * v + (1.0 - ADAM_B2) * (g * g)
    m_hat = m2 / (1.0 - ADAM_B1 ** ADAM_STEP)
    v_hat = v2 / (1.0 - ADAM_B2 ** ADAM_STEP)
    delta = -ADAM_LR * (m_hat / (jnp.sqrt(v_hat) + ADAM_EPS) + ADAM_WD * w)
    return delta, m2, v2


def _adamw_fc(steps, fc):
    n_fc = len(fc)

    def body(*refs):
        ins, outs = refs[:6 * n_fc], refs[6 * n_fc:]
        for k in range(n_fc):
            w_ref, own_ref, arr_ref, diag_ref, m_ref, v_ref = ins[6 * k:6 * k + 6]
            g = ((own_ref[...] + arr_ref[0].astype(F32)) + arr_ref[1].astype(F32)) + diag_ref[...].astype(F32)
            outs[4 * k][...] = g
            outs[4 * k + 1][...], outs[4 * k + 2][...], outs[4 * k + 3][...] = _adam(
                w_ref[...], g, m_ref[...], v_ref[...])

    specs_in, specs_out, shapes, args = [], [], [], []
    for w, own, arrived, diagonal, m, v in fc:
        rows, cols = w.shape
        blk = pl.BlockSpec((rows // steps, cols), lambda i: (i, 0))
        specs_in += [blk, blk, pl.BlockSpec((2, rows // steps, cols), lambda i: (0, i, 0)), blk, blk, blk]
        specs_out += [blk] * 4
        shapes += [jax.ShapeDtypeStruct((rows, cols), F32)] * 4
        args += [w, own, arrived, diagonal, m, v]
    outs = pl.pallas_call(
        body, name="adamw_fc", grid=(steps,), out_shape=tuple(shapes), in_specs=specs_in, out_specs=tuple(specs_out),
        compiler_params=pltpu.CompilerParams(dimension_semantics=("arbitrary",), vmem_limit_bytes=VMEM_LIMIT),
    )(*args)
    return [outs[4 * k:4 * k + 4] for k in range(n_fc)]


def _adamw_ada(rb, w, sc, total, m, v):
    rows, cols = w.shape

    def body(w_ref, sc_ref, t_ref, m_ref, v_ref, g_ref, d_ref, m2_ref, v2_ref, dm):
        me = _index(_place())
        for dev in range(N_DEV):
            @pl.when((pl.program_id(0) == 0) & (me == dev))
            def _():
                for b in range(N_DEV):
                    for k in range(6):
                        lo, hi = max(cols * dev, D * k), min(cols * (dev + 1), D * (k + 1))
                        if lo < hi:
                            dm[b:b + 1, lo - cols * dev:hi - cols * dev] = t_ref[
                                _table_row(b) + k:_table_row(b) + k + 1, lo - D * k:hi - D * k]

        g = _dot_tn(sc_ref[...].astype(BF16), dm[...].astype(BF16))
        g_ref[...] = g
        d_ref[...], m2_ref[...], v2_ref[...] = _adam(w_ref[...], g, m_ref[...], v_ref[...])

    blk = pl.BlockSpec((rb, cols), lambda i: (i, 0))
    shp = jax.ShapeDtypeStruct((rows, cols), F32)
    return pl.pallas_call(
        body, name="adamw_ada", grid=(rows // rb,), out_shape=(shp, shp, shp, shp),
        in_specs=[blk, pl.BlockSpec((N_DEV, rb), lambda i: (0, i)), _full(total.shape), blk, blk],
        out_specs=(blk, blk, blk, blk),
        scratch_shapes=[pltpu.VMEM((N_DEV, cols), F32)],
        compiler_params=pltpu.CompilerParams(dimension_semantics=("arbitrary",)),
    )(w, sc, total, m, v)


def _unfold(acc_rows):
    return jnp.concatenate([acc_rows[:, :D_A], acc_rows[:, D_A:]], axis=0)


def _adamw_small(total, params, shards):
    n = len(params)
    flat = [a for p in params for a in p]

    def body(*refs):
        s_ref = refs[0]
        p_refs = refs[1:1 + 3 * n]
        s_refs = refs[1 + 3 * n:1 + 3 * n + 4 * len(shards)]
        loss_ref = refs[1 + 3 * n + 4 * len(shards)]
        o_refs = refs[2 + 3 * n + 4 * len(shards):2 + 3 * n + 4 * len(shards) + 4 * n]
        so_refs = refs[2 + 3 * n + 4 * len(shards) + 4 * n:]
        d_b_ada = s_ref[0:6, :]
        for b in range(1, N_DEV):
            d_b_ada = d_b_ada + s_ref[_table_row(b):_table_row(b) + 6, :]
        misc = lambda r: s_ref[PK_MISC + r - ROW_N1PRE:PK_MISC + r - ROW_N1PRE + 1, :]
        loss = jnp.sum(misc(ROW_LOSS), axis=-1, keepdims=True) * (0.5 / D)
        loss_ref[...] = loss
        mask = _tril_mask()
        ws = _unfold(s_ref[PK_WS:PK_WS + 64, :])
        wp = _unfold(s_ref[PK_WP:PK_WP + 64, :])
        grads = [
            d_b_ada,
            misc(ROW_N1PRE), misc(ROW_N1POST), misc(ROW_N2PRE), misc(ROW_N2POST),
            misc(ROW_LN)[:, :D_A], misc(ROW_LN)[:, D_A:],
            misc(ROW_POOL)[:, :D_B], misc(ROW_POOL)[:, D_B:],
            s_ref[PK_BS:PK_BS + N_HEADS, 0:GROUP],
            jnp.stack([ws[:, h * GROUP:(h + 1) * GROUP] * mask for h in range(N_HEADS)]),
            jnp.stack([wp[:, g * GROUP:(g + 1) * GROUP] for g in range(len(WINDOWS))]),
        ]
        for k in range(n):
            w_ref, m_ref, v_ref = p_refs[3 * k:3 * k + 3]
            g = grads[k]
            if k == 0:
                for j in range(6):
                    o_refs[0][j] = g[j:j + 1, :]
                    o_refs[1][j], o_refs[2][j], o_refs[3][j] = _adam(w_ref[j], g[j:j + 1, :], m_ref[j], v_ref[j])
                continue
            o_refs[4 * k][...] = g
            o_refs[4 * k + 1][...], o_refs[4 * k + 2][...], o_refs[4 * k + 3][...] = _adam(
                w_ref[...], g, m_ref[...], v_ref[...])
        for k in range(len(shards)):
            w_ref, g_ref, m_ref, v_ref = s_refs[4 * k:4 * k + 4]
            so_refs[3 * k][...], so_refs[3 * k + 1][...], so_refs[3 * k + 2][...] = _adam(
                w_ref[...], g_ref[...], m_ref[...], v_ref[...])

    vm = pl.BlockSpec(memory_space=pltpu.VMEM)
    out_shape = [jax.ShapeDtypeStruct((1, 1), F32)]
    for w, _, _ in params:
        out_shape += [jax.ShapeDtypeStruct(w.shape, F32)] * 4
    for w, _, _, _ in shards:
        out_shape += [jax.ShapeDtypeStruct(w.shape, F32)] * 3
    return pl.pallas_call(
        body, name="adamw_small", out_shape=tuple(out_shape),
        in_specs=[vm] * (1 + 3 * n + 4 * len(shards)), out_specs=tuple([vm] * len(out_shape)),
        compiler_params=pltpu.CompilerParams(vmem_limit_bytes=VMEM_LIMIT),
    )(total, *flat, *[a for s in shards for a in s])


TT_ATTN_FWD = 512
ATTN_LAG = 2
TT_MLP_FWD = 512
TT_MLP = 256
TT_WGRAD = 2048
TT_ATTN_BWD = 512
TT_TAIL = 512


def kernel(x, c, w_ada, b_ada, norm1_pre, norm1_post, w_in, w_spatial, b_spatial, ln_v_gain, ln_v_bias, w_pool, b_pool, pool_scale, w_out, norm2_pre, norm2_post, w_fc1, w_fc2, loss_target, m_w_ada, m_b_ada, m_norm1_pre, m_norm1_post, m_w_in, m_w_spatial, m_b_spatial, m_ln_v_gain, m_ln_v_bias, m_w_pool, m_b_pool, m_pool_scale, m_w_out, m_norm2_pre, m_norm2_post, m_w_fc1, m_w_fc2, v_w_ada, v_b_ada, v_norm1_pre, v_norm1_post, v_w_in, v_w_spatial, v_b_spatial, v_ln_v_gain, v_ln_v_bias, v_w_pool, v_b_pool, v_pool_scale, v_w_out, v_norm2_pre, v_norm2_post, v_w_fc1, v_w_fc2):
    t_len = x.shape[1]
    ada_cols = w_ada.shape[1]
    tt = lambda want: min(want, t_len)

    x2 = x.reshape(t_len, D)
    tgt = loss_target.reshape(t_len, D)
    row = lambda a: a.reshape(1, -1)

    w_in_shard, w_out_shard, w1_shard, w2_shard = _cast_shards([w_in.T, w_out, w_fc1, w_fc2])

    bs_rows = jnp.repeat(b_spatial.T, GROUP, axis=1)
    attn_consts = (w_spatial, bs_rows, row(ln_v_gain), row(ln_v_bias), w_pool, row(b_pool), row(pool_scale))

    (z, cat, mix, x1, h2, r_begun, f_head, mod, sc), (w1_early, w2_early), w_out_all, w_in_t = _attn_fwd(
        tt(TT_ATTN_FWD), x2, c.reshape(1, D), w_ada, b_ada.reshape(N_DEV, 1, ada_cols), row(norm1_pre),
        row(norm1_post),
        w_in_shard, w_out_shard, *attn_consts, (w1_shard, w2_shard), row(norm2_pre))
    (r_early, f_early), (w1_late, w2_late) = _mlp_fwd_early(
        tt(TT_MLP_FWD), r_begun, h2, f_head, w1_early, w2_early)
    r_late, df, da, dmix, dx1, red_fwd, red_bwd = _mlp_late_bwd(
        tt(TT_MLP), r_early, x1, h2, f_early, tgt, mix, mod, row(norm2_pre), row(norm2_post), row(norm1_post),
        w1_early, w2_early, w1_late, w2_late)
    own_w1, own_w2, sums_w1, sums_w2, diag_w1, diag_w2 = _mlp_wgrad(tt(TT_WGRAD), r_early, r_late, da, df, h2)
    (dz, p_in, p_out, small), (arr_w1, arr_w2) = _attn_bwd(
        tt(TT_ATTN_BWD), dmix, x2, z, cat, mod, row(norm1_pre), w_out_all, *attn_consts, red_fwd, red_bwd,
        [sums_w1, sums_w2])
    (grad_w1, d_w1, m_w1, v_w1), (grad_w2, d_w2, m_w2, v_w2) = _adamw_fc(
        4, [(w_fc1, own_w1, arr_w1, diag_w1, m_w_fc1, v_w_fc1), (w_fc2, own_w2, arr_w2, diag_w2, m_w_fc2, v_w_fc2)])
    grad_in_t, grad_out, total, grad_x = _tail_comm(
        [p_in.reshape(N_DEV, D_Z // N_DEV, D), p_out.reshape(N_DEV, D // N_DEV, D)], small, 64,
        tt(TT_TAIL), dz, dx1, x2, mod, row(norm1_pre), w_in_t)

    grad_ada, d_ada, m_ada, v_ada = _adamw_ada(256, w_ada, sc, total, m_w_ada, v_w_ada)

    six = lambda a: a.reshape(6, 1, D)
    small_params = [
        (six(b_ada), six(m_b_ada), six(v_b_ada)),
        (row(norm1_pre), row(m_norm1_pre), row(v_norm1_pre)),
        (row(norm1_post), row(m_norm1_post), row(v_norm1_post)),
        (row(norm2_pre), row(m_norm2_pre), row(v_norm2_pre)),
        (row(norm2_post), row(m_norm2_post), row(v_norm2_post)),
        (row(ln_v_gain), row(m_ln_v_gain), row(v_ln_v_gain)),
        (row(ln_v_bias), row(m_ln_v_bias), row(v_ln_v_bias)),
        (row(pool_scale), row(m_pool_scale), row(v_pool_scale)),
        (row(b_pool), row(m_b_pool), row(v_b_pool)),
        (b_spatial, m_b_spatial, v_b_spatial),
        (w_spatial, m_w_spatial, v_w_spatial),
        (w_pool, m_w_pool, v_w_pool),
    ]
    outs = _adamw_small(total, small_params, [(w_out, grad_out, m_w_out, v_w_out),
                                              (w_in.T, grad_in_t, m_w_in.T, v_w_in.T)])
    d_out, m_out, v_out, d_in_t, m_in_t, v_in_t = outs[1 + 4 * len(small_params):]
    loss = outs[0].reshape(())
    names = ["b_ada", "norm1_pre", "norm1_post", "norm2_pre", "norm2_post", "ln_v_gain", "ln_v_bias", "pool_scale",
             "b_pool", "b_spatial", "w_spatial", "w_pool"]
    shapes = dict(b_ada=b_ada.shape, norm1_pre=norm1_pre.shape, norm1_post=norm1_post.shape,
                  norm2_pre=norm2_pre.shape, norm2_post=norm2_post.shape, ln_v_gain=ln_v_gain.shape,
                  ln_v_bias=ln_v_bias.shape, pool_scale=pool_scale.shape, b_pool=b_pool.shape,
                  b_spatial=b_spatial.shape, w_spatial=w_spatial.shape, w_pool=w_pool.shape)
    res = {}
    for k, nm in enumerate(names):
        res[nm] = tuple(o.reshape(shapes[nm]) for o in outs[1 + 4 * k:5 + 4 * k])
    res["w_ada"] = (grad_ada, d_ada, m_ada, v_ada)
    res["w_in"] = (grad_in_t.T, d_in_t.T, m_in_t.T, v_in_t.T)
    res["w_out"] = (grad_out, d_out, m_out, v_out)
    res["w_fc1"] = (grad_w1, d_w1, m_w1, v_w1)
    res["w_fc2"] = (grad_w2, d_w2, m_w2, v_w2)

    order = ["w_ada", "b_ada", "norm1_pre", "norm1_post", "w_in", "w_spatial", "b_spatial", "ln_v_gain", "ln_v_bias",
             "w_pool", "b_pool", "pool_scale", "w_out", "norm2_pre", "norm2_post", "w_fc1", "w_fc2"]
    return (loss, grad_x.reshape(x.shape),
            *[res[nm][0] for nm in order], *[res[nm][1] for nm in order],
            *[res[nm][2] for nm in order], *[res[nm][3] for nm in order])
```

```python
import functools

import jax
import jax.numpy as jnp
from jax import lax
from jax.experimental import pallas as pl
from jax.experimental.pallas import tpu as pltpu

F32 = jnp.float32
BF16 = jnp.bfloat16
MESH = pl.DeviceIdType.MESH

N_DEV = 8
D = 1024
D_A = 512
D_B = 512
D_Z = 2 * D_A + D_B
N_HEADS = 4
CHUNK = 128
WINDOWS = (2, 4, 8, 16)
GROUP = 128
D_FF = 4096
FF_BLK = D_FF // N_DEV
HALO = 16
EPS = 1e-6
VMEM_LIMIT = 60 * 1024 * 1024

ADAM_LR = 0.001
ADAM_B1 = 0.9
ADAM_B2 = 0.999
ADAM_EPS = 1e-08
ADAM_WD = 0.01
ADAM_STEP = 10

ROW_DMOD = 0
ROW_N1PRE, ROW_N1POST, ROW_N2PRE, ROW_N2POST = 8, 9, 10, 11
ROW_LN = 12
ROW_POOL = 13
ROW_LOSS = 14
ROW_BS = 16
ROW_WS = 24
ROW_WP = 88
SMALL_ROWS = 152
PACK_FINE = 40
PACK_HALF = PACK_FINE + 64
PACK_ROWS = 2 * PACK_HALF
PK_WS = PACK_FINE
PK_TABLE_B = PACK_HALF
PK_MISC = PK_TABLE_B + 24
PK_BS = PK_MISC + 8
PK_WP = PK_BS + 8


def _table_row(b):
    if isinstance(b, int):
        return 8 * b if 8 * b < PACK_FINE else 8 * b + PK_TABLE_B - PACK_FINE
    return 8 * b + jnp.where(8 * b < PACK_FINE, 0, PK_TABLE_B - PACK_FINE)


def _dot(a, b):
    return jnp.dot(a, b, preferred_element_type=F32)


def _dot_nt(a, b):
    return lax.dot_general(a, b, (((1,), (1,)), ((), ())), preferred_element_type=F32)


def _dot_tn(a, b):
    return lax.dot_general(a, b, (((0,), (0,)), ((), ())), preferred_element_type=F32)


def _rstd(v):
    return lax.rsqrt(jnp.mean(v * v, axis=-1, keepdims=True) + EPS)


def _rms_bwd(d_hat, hat, rstd):
    return rstd * (d_hat - hat * jnp.mean(d_hat * hat, axis=-1, keepdims=True))


def _rms_bwd_gained(g, gain, hat, rstd):
    g_hat = g * hat
    d_v = rstd * (g * gain - hat * jnp.mean(g_hat * gain, axis=-1, keepdims=True))
    return d_v, _colsum(g_hat)


_K0 = 0.7978845608028654
_K1 = 0.044715


def _gelu_parts(v):
    t = jnp.tanh(v * (_K0 + (_K0 * _K1) * (v * v)))
    return t, v * (0.5 + 0.5 * t)


def _gelu_grad(v, t):
    return (0.5 + 0.5 * t) + (0.5 * v) * (1.0 - t * t) * (_K0 + (3.0 * _K0 * _K1) * (v * v))


def _colsum(v):
    return jnp.sum(v, axis=0, keepdims=True)


def _full(shape):
    n = len(shape)
    return pl.BlockSpec(shape, lambda *_: (0,) * n)


def _resident(shape):
    n = len(shape)
    return pl.BlockSpec(shape, lambda *_: (0,) * n, pipeline_mode=pl.Buffered(1))


def _place():
    x, y, c = lax.axis_index("x"), lax.axis_index("y"), lax.axis_index("c")
    return x, y, c


def _flip(v, bit):
    return 1 - v if bit else v


def _peer(x, y, c, k):
    return (_flip(x, (k >> 2) & 1), _flip(y, (k >> 1) & 1), _flip(c, k & 1))


def _index(p):
    return 4 * p[0] + 2 * p[1] + p[2]


def _cast_shards(shards):
    def body(*refs):
        for src, dst in zip(refs[:len(shards)], refs[len(shards):]):
            dst[...] = src[...].astype(BF16)

    vm = pl.BlockSpec(memory_space=pltpu.VMEM)
    return pl.pallas_call(
        body, name="cast_shards", out_shape=tuple(jax.ShapeDtypeStruct(s.shape, BF16) for s in shards),
        in_specs=[vm] * len(shards), out_specs=tuple([vm] * len(shards)),
    )(*shards)


FC_EARLY = 6
FC_HEAD = 2
R_HEAD_COLS = (FC_EARLY - FC_HEAD) * FF_BLK
WGRAD_ORDER = (7, 6, 1, 3, 5, 2, 4, 0)


def _early_col(j):
    return R_HEAD_COLS + j * FF_BLK if j < FC_HEAD else (j - FC_HEAD) * FF_BLK


class _Copies:
    def __init__(self, entries, send_sems, recv_sems):
        self.place = _place()
        self.entries, self.send_sems, self.recv_sems = entries, send_sems, recv_sems

    def _copy(self, i, arrival=False):
        src, dst, rel = self.entries[i]
        return pltpu.make_async_remote_copy(
            src_ref=dst if arrival else src, dst_ref=dst, send_sem=self.send_sems.at[i],
            recv_sem=self.recv_sems.at[i], device_id=_peer(*self.place, rel), device_id_type=MESH)

    def start(self, *which):
        for i in which:
            self._copy(i).start()

    def wait_recv(self, *which):
        for i in which:
            self._copy(i, arrival=True).wait_recv()

    def wait_send(self, *which):
        for i in which:
            self._copy(i).wait_send()


RING = 3


def _tail_comm(parts, small, row_chunk, tt, dz, dx1, x, mod, n1pre, w_in_t):
    n = len(parts)
    t_len = x.shape[0]
    nt = t_len // tt

    def body(*refs):
        p_refs, small_ref = refs[:n], refs[n]
        streams = refs[n + 1:n + 4]
        mod_ref, n1pre_ref, win_ref = refs[n + 4:n + 7]
        outs = refs[n + 7:]
        g_refs, total_ref, gx_ref = outs[:n], outs[n], outs[n + 1]
        scr = outs[n + 2:]
        rings, ring_sems = scr[-4:-1], scr[-1]
        scr = scr[:-4]
        from_sib = scr[0:n]
        chip_out = scr[n:2 * n]
        chip_in = scr[2 * n:3 * n]
        pack, pack_sib, fine, bulk, total_scr, head = scr[3 * n:3 * n + 6]
        send_a, recv_a, send_b, recv_b, send_s, recv_s = scr[3 * n + 6:]
        step = pl.program_id(0)
        x, y, c = _place()
        me = _index((x, y, c))
        sibling = (x, y, 1 - c)
        my_chip = 2 * x + y
        others = [(1 - x, y), (x, 1 - y), (1 - x, 1 - y)]
        my_half = pl.ds(pl.multiple_of(PACK_HALF * c, 8), PACK_HALF)

        def pack_to_sibling():
            return pltpu.make_async_remote_copy(
                src_ref=pack, dst_ref=pack_sib, send_sem=send_s.at[0], recv_sem=recv_s.at[0],
                device_id=sibling, device_id_type=MESH)

        def half_to_chip(r, part):
            buf = (fine, bulk)[part]
            return pltpu.make_async_remote_copy(
                src_ref=buf.at[my_chip], dst_ref=buf.at[my_chip],
                send_sem=send_s.at[1 + 3 * part + r], recv_sem=recv_s.at[1 + 3 * part + r],
                device_id=(*others[r], c), device_id_type=MESH)

        def half_from_chip(r, part):
            k = 2 * others[r][0] + others[r][1]
            buf = (fine, bulk)[part]
            return pltpu.make_async_remote_copy(
                src_ref=buf.at[k], dst_ref=buf.at[k],
                send_sem=send_s.at[1 + 3 * part + r], recv_sem=recv_s.at[1 + 3 * part + r],
                device_id=(*others[r], c), device_id_type=MESH)

        def total_to_sibling():
            return pltpu.make_async_remote_copy(
                src_ref=total_scr.at[my_half], dst_ref=total_scr.at[my_half],
                send_sem=send_s.at[7], recv_sem=recv_s.at[7], device_id=sibling, device_id_type=MESH)

        def total_from_sibling():
            sib_half = pl.ds(pl.multiple_of(PACK_HALF * (1 - c), 8), PACK_HALF)
            return pltpu.make_async_remote_copy(
                src_ref=total_scr.at[sib_half], dst_ref=total_scr.at[sib_half],
                send_sem=send_s.at[7], recv_sem=recv_s.at[7], device_id=sibling, device_id_type=MESH)

        def to_sibling(a, k):
            return pltpu.make_async_remote_copy(
                src_ref=p_refs[a].at[2 * k + (1 - c)], dst_ref=from_sib[a].at[k],
                send_sem=send_a.at[a], recv_sem=recv_a.at[a], device_id=sibling, device_id_type=MESH)

        def all_from_sibling(a):
            return pltpu.make_async_remote_copy(
                src_ref=from_sib[a], dst_ref=from_sib[a], send_sem=send_a.at[a], recv_sem=recv_a.at[a],
                device_id=sibling, device_id_type=MESH)

        def to_chip(a, r):
            return pltpu.make_async_remote_copy(
                src_ref=chip_out[a].at[r], dst_ref=chip_in[a].at[r],
                send_sem=send_b.at[3 * a + r], recv_sem=recv_b.at[3 * a + r],
                device_id=(*others[r], c), device_id_type=MESH)

        @pl.when(step == 0)
        def _():
            head[...] = jnp.zeros_like(head)
            for a in range(n):
                for k in range(4):
                    to_sibling(a, k).start()

        @pl.when(step == min(1, nt - 1))
        def _():
            for a in range(n):
                all_from_sibling(a).wait_recv()
                rows = p_refs[a].shape[1]
                for r in range(3):
                    k = 2 * others[r][0] + others[r][1]
                    for s in range(0, rows, row_chunk):
                        sl = pl.ds(s, row_chunk)
                        chip_out[a][r, sl, :] = (p_refs[a][2 * k + c, sl, :].astype(F32)
                                                 + from_sib[a][k, sl, :].astype(F32)).astype(BF16)
                    to_chip(a, r).start()
                for s in range(0, rows, row_chunk):
                    sl = pl.ds(s, row_chunk)
                    g_refs[a][sl, :] = (p_refs[a][2 * my_chip + c, sl, :].astype(F32)
                                        + from_sib[a][my_chip, sl, :].astype(F32))

        def fetch(tile_index):
            slot = lax.rem(tile_index, RING)
            rows = pl.ds(pl.multiple_of(tile_index * tt, tt), tt)
            return [pltpu.make_async_copy(src.at[rows], ring.at[slot], ring_sems.at[a, slot])
                    for a, (src, ring) in enumerate(zip(streams, rings))]

        @pl.when(step == 0)
        def _():
            for k in range(min(RING - 1, nt)):
                for cp in fetch(jnp.int32(k)):
                    cp.start()

        @pl.when(step + RING - 1 < nt)
        def _():
            for cp in fetch(step + RING - 1):
                cp.start()

        for cp in fetch(step):
            cp.wait()
        slot = lax.rem(step, RING)
        dz_ref, dx1_ref, x_ref = (ring.at[slot] for ring in rings)

        dh1 = _dot(dz_ref[...], win_ref[...])
        xv = x_ref[...]
        r1 = _rstd(xv)
        xhat = xv * r1
        scale1 = mod_ref[1:2, :]
        pre1 = n1pre_ref[...]
        d_x, sum_h = _rms_bwd_gained(dh1, pre1 * (1.0 + scale1), xhat, r1)
        gx_ref[...] = dx1_ref[...] + d_x
        head[ROW_DMOD:ROW_DMOD + 1, :] += _colsum(dh1)
        head[ROW_DMOD + 1:ROW_DMOD + 2, :] += pre1 * sum_h
        head[ROW_N1PRE:ROW_N1PRE + 1, :] += (1.0 + scale1) * sum_h

        @pl.when(step == nt - 1)
        def _():
            pack[0:PACK_FINE, :] = jnp.zeros((PACK_FINE, D), F32)
            pack[PK_TABLE_B:PK_MISC, :] = jnp.zeros((PK_MISC - PK_TABLE_B, D), F32)
            pack[pl.ds(pl.multiple_of(_table_row(me), 8), 8), :] = small_ref[0:8, :] + head[0:8, :]
            pack[PK_WS:PK_WS + 64, :] = small_ref[ROW_WS:ROW_WS + 64, :]
            pack[PK_MISC:PK_MISC + 8, :] = small_ref[ROW_N1PRE:ROW_N1PRE + 8, :] + head[8:16, :]
            pack[PK_BS:PK_BS + 8, :] = small_ref[ROW_BS:ROW_BS + 8, :]
            pack[PK_WP:PK_WP + 64, :] = small_ref[ROW_WP:ROW_WP + 64, :]
            pack_to_sibling().start()
            pack_to_sibling().wait_recv()
            chip_sum = pack[my_half, :] + pack_sib[my_half, :]
            fine[my_chip] = chip_sum[:PACK_FINE, :]
            bulk[my_chip] = chip_sum[PACK_FINE:, :].astype(BF16)
            for r in range(3):
                half_to_chip(r, 0).start()
                half_to_chip(r, 1).start()
            for a in range(n):
                rows = p_refs[a].shape[1]
                for r in range(3):
                    to_chip(a, r).wait_recv()
                    for s in range(0, rows, row_chunk):
                        sl = pl.ds(s, row_chunk)
                        g_refs[a][sl, :] = g_refs[a][sl, :] + chip_in[a][r, sl, :].astype(F32)
            for r in range(3):
                half_from_chip(r, 0).wait_recv()
                half_from_chip(r, 1).wait_recv()
            half_start = pl.multiple_of(PACK_HALF * c, 8)
            total_scr[pl.ds(half_start, PACK_FINE), :] = ((fine[0] + fine[1]) + fine[2]) + fine[3]
            total_scr[pl.ds(half_start + PACK_FINE, PACK_HALF - PACK_FINE), :] = (
                (bulk[0].astype(F32) + bulk[1].astype(F32)) + bulk[2].astype(F32)) + bulk[3].astype(F32)
            total_to_sibling().start()
            total_from_sibling().wait_recv()
            total_ref[...] = total_scr[...]
            for a in range(n):
                all_from_sibling(a).wait_send()
                for r in range(3):
                    to_chip(a, r).wait_send()
            pack_to_sibling().wait_send()
            for r in range(3):
                half_to_chip(r, 0).wait_send()
                half_to_chip(r, 1).wait_send()
            total_to_sibling().wait_send()

    tile = lambda w: pl.BlockSpec((tt, w), lambda i: (i, 0))
    hbm = pl.BlockSpec(memory_space=pl.ANY)
    return pl.pallas_call(
        body, name="tail_comm", grid=(nt,),
        out_shape=tuple([jax.ShapeDtypeStruct(p.shape[1:], F32) for p in parts]
                        + [jax.ShapeDtypeStruct((PACK_ROWS, D), F32), jax.ShapeDtypeStruct((t_len, D), F32)]),
        in_specs=([_resident(p.shape) for p in parts] + [_resident(small.shape)]
                  + [hbm, hbm, hbm, _full((8, D)), _full((1, D)), _resident((D_Z, D))]),
        out_specs=tuple([_full(p.shape[1:]) for p in parts] + [_full((PACK_ROWS, D)), tile(D)]),
        scratch_shapes=(
            [pltpu.VMEM((4,) + p.shape[1:], BF16) for p in parts]
            + [pltpu.VMEM((3,) + p.shape[1:], BF16) for p in parts]
            + [pltpu.VMEM((3,) + p.shape[1:], BF16) for p in parts]
            + [pltpu.VMEM((PACK_ROWS, D), F32), pltpu.VMEM((PACK_ROWS, D), F32),
               pltpu.VMEM((4, PACK_FINE, D), F32), pltpu.VMEM((4, PACK_HALF - PACK_FINE, D), BF16),
               pltpu.VMEM((PACK_ROWS, D), F32), pltpu.VMEM((16, D), F32)]
            + [pltpu.SemaphoreType.DMA((n,)), pltpu.SemaphoreType.DMA((n,)),
               pltpu.SemaphoreType.DMA((3 * n,)), pltpu.SemaphoreType.DMA((3 * n,)),
               pltpu.SemaphoreType.DMA((8,)), pltpu.SemaphoreType.DMA((8,))]
            + [pltpu.VMEM((RING, tt, D_Z), BF16), pltpu.VMEM((RING, tt, D), F32), pltpu.VMEM((RING, tt, D), F32),
               pltpu.SemaphoreType.DMA((3, RING))]),
        compiler_params=pltpu.CompilerParams(dimension_semantics=("arbitrary",), vmem_limit_bytes=VMEM_LIMIT),
    )(*parts, small, dz, dx1, x, mod, n1pre, w_in_t)


def _tril_mask():
    row = lax.broadcasted_iota(jnp.int32, (CHUNK, CHUNK), 0)
    col = lax.broadcasted_iota(jnp.int32, (CHUNK, CHUNK), 1)
    return (col <= row).astype(F32)


def _window_sums(ext):
    s2 = ext + pltpu.roll(ext, 1, 0)
    t4 = s2[:, GROUP:]
    s4 = t4 + pltpu.roll(t4, 2, 0)
    t8 = s4[:, GROUP:]
    s8 = t8 + pltpu.roll(t8, 4, 0)
    t16 = s8[:, GROUP:]
    s16 = t16 + pltpu.roll(t16, 8, 0)
    return [s2[:, :GROUP], s4[:, :GROUP], s8[:, :GROUP], s16]


def _inv_counts(first_pos, rows):
    pos = first_pos + lax.broadcasted_iota(jnp.int32, (rows, 1), 0)
    return [1.0 / jnp.minimum(pos + 1, w).astype(F32) for w in WINDOWS]


def _pool_diff(zb, halo, first_pos):
    tt = zb.shape[0]
    sums = _window_sums(jnp.concatenate([halo, zb], axis=0))
    inv = _inv_counts(first_pos, tt)
    return [sums[g][HALO:, :] * inv[g] - zb[:, g * GROUP:(g + 1) * GROUP] for g in range(len(WINDOWS))]


def _row_blocks(scr, rows, place):
    def block(rel):
        start = pl.multiple_of(rows * _index(_peer(*place, rel)), rows)
        return scr.at[pl.ds(start, rows), :]

    def entries(shard_ref):
        return ([(shard_ref, block(0), rel) for rel in (1, 2, 4, 6)]
                + [(block(rel), block(rel), 1) for rel in (2, 4, 6)])
    return block, entries


def _attn_fwd(tt, x, c_row, w_ada, b_pieces, n1pre, n1post, w_in_shard, w_out_shard, w_sp, bs_rows, ln_g, ln_b,
              w_pool, b_pool, pool_scale, fc_shards, n2pre):
    t_len = x.shape[0]
    nt = t_len // tt
    ncol = w_ada.shape[1]

    def body(x_ref, xb_ref, c_ref, wada_ref, b_ref, n1pre_ref, n1post_ref, wi_ref, wo_ref, wsp_ref, bs_ref,
             lng_ref, lnb_ref, wp_ref, bp_ref, ps_ref, w1_ref, w2_ref, n2pre_ref,
             z_ref, cat_ref, mix_ref, x1_ref, h2_ref, r_ref, f_ref, mod_out, sc_out, e1_ref, e2_ref, wout_ref,
             win_out, carry, land1, land2, sib1, sib2, cat_keep, wout_scr, win_ref, mod_ref, cg, mg, part,
             send_sems, recv_sems, local_sems, wo_send, wo_recv, wi_send, wi_recv, ada_send, ada_recv):
        i = pl.program_id(0)
        place = px, py, pc = _place()
        me = _index(place)
        wo_block, wo_entries = _row_blocks(wout_scr, w_out_shard.shape[0], place)
        wi_block, wi_entries = _row_blocks(win_ref, w_in_shard.shape[0], place)
        wo_copies = _Copies(wo_entries(wo_ref), wo_send, wo_recv)
        wi_copies = _Copies(wi_entries(wi_ref), wi_send, wi_recv)
        wo_keep = pltpu.make_async_copy(wout_scr, wout_ref, local_sems.at[8])
        wi_keep = pltpu.make_async_copy(win_ref, win_out, local_sems.at[9])
        ada = _Copies([(cg.at[me], cg.at[me], k) for k in range(1, N_DEV)]
                      + [(part, mg.at[me], k) for k in range(1, N_DEV)], ada_send, ada_recv)
        copies = _Copies(
            [(w1_ref, sib1, 1), (w2_ref, sib2, 1),
             (w1_ref, land1.at[0], 2), (w2_ref, land2.at[0], 2),
             (w1_ref, land1.at[1], 4), (w2_ref, land2.at[1], 4),
             (land1.at[0], e1_ref.at[3], 1), (land2.at[0], e2_ref.at[3], 1),
             (land1.at[1], e1_ref.at[5], 1), (land2.at[1], e2_ref.at[5], 1)],
            send_sems, recv_sems)
        keep = [pltpu.make_async_copy(w1_ref, e1_ref.at[0], local_sems.at[0]),
                pltpu.make_async_copy(w2_ref, e2_ref.at[0], local_sems.at[1]),
                pltpu.make_async_copy(land1.at[0], e1_ref.at[2], local_sems.at[2]),
                pltpu.make_async_copy(land1.at[1], e1_ref.at[4], local_sems.at[3]),
                pltpu.make_async_copy(land2.at[0], e2_ref.at[2], local_sems.at[4]),
                pltpu.make_async_copy(land2.at[1], e2_ref.at[4], local_sems.at[5]),
                pltpu.make_async_copy(sib1, e1_ref.at[1], local_sems.at[6]),
                pltpu.make_async_copy(sib2, e2_ref.at[1], local_sems.at[7])]

        @pl.when(i == 0)
        def _():
            cg[me] = jnp.broadcast_to(c_ref[...], (8, D))
            ada.start(*range(N_DEV - 1))
            wi_copies.start(0, 1, 2, 3)
            wi_rows, wo_rows = w_in_shard.shape[0], w_out_shard.shape[0]
            win_ref[pl.ds(pl.multiple_of(wi_rows * me, wi_rows), wi_rows), :] = wi_ref[...]
            wout_scr[pl.ds(pl.multiple_of(wo_rows * me, wo_rows), wo_rows), :] = wo_ref[...]
            carry[...] = jnp.zeros_like(carry)

            ada.wait_recv(*range(N_DEV - 1))
            c_all = jnp.concatenate([cg[j, 0:1, :] for j in range(N_DEV)], axis=0)
            sc = c_all * jax.nn.sigmoid(c_all)
            sc_out[...] = sc
            part[...] = _dot(sc.astype(BF16), wada_ref[...].astype(BF16)) + b_ref[me]
            ada.start(*range(N_DEV - 1, 2 * (N_DEV - 1)))
            wo_copies.start(0, 1, 2, 3)
            copies.start(0, 1, 2, 4, 3, 5)
            keep[0].start()
            keep[1].start()
            mg[me] = part[...]
            ada.wait_recv(*range(N_DEV - 1, 2 * (N_DEV - 1)))
            mod_ref[...] = jnp.zeros_like(mod_ref)
            for j in range(N_DEV):
                for m in range(6):
                    lo, hi = max(ncol * j, D * m), min(ncol * (j + 1), D * (m + 1))
                    if lo < hi:
                        mod_ref[m:m + 1, lo - D * m:hi - D * m] = mg[j, pl.ds(me, 1), lo - ncol * j:hi - ncol * j]
            mod_out[...] = mod_ref[...]

            wi_copies.wait_recv(1, 2, 3)
            wi_copies.start(4, 5, 6)
            wi_copies.wait_recv(0, 4, 5, 6)
            wi_keep.start()

        @pl.when(i == nt // 2 + ATTN_LAG)
        def _():
            copies.wait_recv(2, 4)
            copies.start(6, 8)
            keep[2].start()
            keep[3].start()

        @pl.when(i == nt - 1 + ATTN_LAG)
        def _():
            copies.wait_recv(3, 5)
            copies.start(7, 9)
            keep[4].start()
            keep[5].start()

        shift1, scale1, gate1 = mod_ref[0:1, :], mod_ref[1:2, :], mod_ref[2:3, :]

        @pl.when(i < nt)
        def _():
            xv = x_ref[...]
            h1 = (xv * _rstd(xv)) * (n1pre_ref[...] * (1.0 + scale1)) + shift1
            z = _dot_nt(h1.astype(BF16), win_ref[...])
            z_ref[...] = z

            _, ga = _gelu_parts(z[:, :2 * D_A])
            u, vr = ga[:, :D_A], ga[:, D_A:]
            dv = vr - jnp.mean(vr, axis=-1, keepdims=True)
            v = (dv * lax.rsqrt(jnp.mean(dv * dv, axis=-1, keepdims=True) + EPS)) * lng_ref[...] + lnb_ref[...]
            vb = v.astype(BF16)
            mask = _tril_mask()
            wc = [(wsp_ref[h] * mask).astype(BF16) for h in range(N_HEADS)]
            for ch in range(tt // CHUNK):
                rows = slice(ch * CHUNK, (ch + 1) * CHUNK)
                for h in range(N_HEADS):
                    cols = slice(h * GROUP, (h + 1) * GROUP)
                    mixed = _dot(wc[h], vb[rows, cols]) + bs_ref[:, cols]
                    cat_ref[rows, cols] = (u[rows, cols] * mixed).astype(BF16)

            zb = z[:, 2 * D_A:]
            diff = _pool_diff(zb, carry[...], i * tt)
            carry[...] = zb[tt - HALO:, :]
            for g in range(len(WINDOWS)):
                cols = slice(g * GROUP, (g + 1) * GROUP)
                pre = _dot(diff[g].astype(BF16), wp_ref[g].astype(BF16)) + bp_ref[:, cols]
                cat_ref[:, D_A + g * GROUP:D_A + (g + 1) * GROUP] = (pre * ps_ref[:, cols]).astype(BF16)
            cat_keep[i % (ATTN_LAG + 1)] = cat_ref[...]

        @pl.when(i == 0)
        def _():
            copies.wait_recv(0, 1)
            keep[6].start()
            keep[7].start()

        @pl.when(i == 1)
        def _():
            wo_copies.wait_recv(1, 2, 3)
            wo_copies.start(4, 5, 6)

        @pl.when(i == ATTN_LAG)
        def _():
            wo_copies.wait_recv(0, 4, 5, 6)
            wo_keep.start()

        @pl.when(i >= ATTN_LAG)
        def _():
            xv = xb_ref[...]
            mix = _dot(cat_keep[(i - ATTN_LAG) % (ATTN_LAG + 1)], wout_scr[...])
            mix_ref[...] = mix
            x1v = xv + (mix * _rstd(mix)) * (gate1 * n1post_ref[...])
            x1_ref[...] = x1v
            shift2, scale2 = mod_ref[3:4, :], mod_ref[4:5, :]
            h2 = ((x1v * _rstd(x1v)) * (n2pre_ref[...] * (1.0 + scale2)) + shift2).astype(BF16)
            h2_ref[...] = h2
            for j, (w1, w2) in enumerate(((w1_ref, w2_ref), (sib1, sib2))):
                ra = jnp.maximum(_dot(h2, w1[...]), 0.0)
                r = (ra * ra).astype(BF16)
                r_ref[:, j * FF_BLK:(j + 1) * FF_BLK] = r
                if j == 0:
                    f_ref[...] = _dot(r, w2[...])
                else:
                    f_ref[...] += _dot(r, w2[...])

        @pl.when(i == nt - 1 + ATTN_LAG)
        def _():
            copies.wait_recv(6, 7, 8, 9)
            copies.wait_send(*range(10))
            wo_copies.wait_send(*range(7))
            wi_copies.wait_send(*range(7))
            ada.wait_send(*range(2 * (N_DEV - 1)))
            for cp in keep:
                cp.wait()
            wo_keep.wait()
            wi_keep.wait()

    first = lambda w: pl.BlockSpec((tt, w), lambda i: (jnp.minimum(i, nt - 1), 0))
    second = lambda w: pl.BlockSpec((tt, w), lambda i: (jnp.maximum(i - ATTN_LAG, 0), 0))
    r_head = pl.BlockSpec((tt, FC_HEAD * FF_BLK),
                          lambda i: (jnp.maximum(i - ATTN_LAG, 0), R_HEAD_COLS // (FC_HEAD * FF_BLK)))
    hbm = pl.BlockSpec(memory_space=pl.ANY)
    outs = pl.pallas_call(
        body, name="attn_fwd", grid=(nt + ATTN_LAG,),
        out_shape=tuple([jax.ShapeDtypeStruct((t_len, D_Z), F32), jax.ShapeDtypeStruct((t_len, D), BF16),
                         jax.ShapeDtypeStruct((t_len, D), F32), jax.ShapeDtypeStruct((t_len, D), F32),
                         jax.ShapeDtypeStruct((t_len, D), BF16),
                         jax.ShapeDtypeStruct((t_len, FC_EARLY * FF_BLK), BF16),
                         jax.ShapeDtypeStruct((t_len, D), F32)]
                        + [jax.ShapeDtypeStruct((8, D), F32), jax.ShapeDtypeStruct((N_DEV, D), F32)]
                        + [jax.ShapeDtypeStruct((FC_EARLY,) + s.shape, BF16) for s in fc_shards]
                        + [jax.ShapeDtypeStruct((D, D), BF16), jax.ShapeDtypeStruct((D_Z, D), BF16)]),
        in_specs=[first(D), second(D), _full((1, D)), _resident(w_ada.shape), _full((N_DEV, 1, ncol)), _full((1, D)),
                  _full((1, D)), _resident(w_in_shard.shape), _resident(w_out_shard.shape),
                  _full((N_HEADS, CHUNK, CHUNK)), _full((CHUNK, D_A)), _full((1, D_A)), _full((1, D_A)),
                  _full((len(WINDOWS), GROUP, GROUP)), _full((1, D_B)), _full((1, D_B)),
                  _resident(fc_shards[0].shape), _resident(fc_shards[1].shape), _full((1, D))],
        out_specs=(first(D_Z), first(D), second(D), second(D), second(D), r_head, second(D),
                   _full((8, D)), _full((N_DEV, D)), hbm, hbm, hbm, hbm),
        scratch_shapes=[pltpu.VMEM((HALO, D_B), F32),
                        pltpu.VMEM((2,) + fc_shards[0].shape, BF16), pltpu.VMEM((2,) + fc_shards[1].shape, BF16),
                        pltpu.VMEM(fc_shards[0].shape, BF16), pltpu.VMEM(fc_shards[1].shape, BF16),
                        pltpu.VMEM((ATTN_LAG + 1, tt, D), BF16), pltpu.VMEM((D, D), BF16),
                        pltpu.VMEM((D_Z, D), BF16), pltpu.VMEM((8, D), F32),
                        pltpu.VMEM((N_DEV, 8, D), F32), pltpu.VMEM((N_DEV, N_DEV, ncol), F32),
                        pltpu.VMEM((N_DEV, ncol), F32),
                        pltpu.SemaphoreType.DMA((10,)), pltpu.SemaphoreType.DMA((10,)),
                        pltpu.SemaphoreType.DMA((10,)),
                        pltpu.SemaphoreType.DMA((7,)), pltpu.SemaphoreType.DMA((7,)),
                        pltpu.SemaphoreType.DMA((7,)), pltpu.SemaphoreType.DMA((7,)),
                        pltpu.SemaphoreType.DMA((2 * (N_DEV - 1),)), pltpu.SemaphoreType.DMA((2 * (N_DEV - 1),))],
        compiler_params=pltpu.CompilerParams(dimension_semantics=("arbitrary",), vmem_limit_bytes=VMEM_LIMIT),
    )(x, x, c_row, w_ada, b_pieces, n1pre, n1post, w_in_shard, w_out_shard, w_sp, bs_rows, ln_g, ln_b, w_pool, b_pool,
      pool_scale, *fc_shards, n2pre)
    return outs[:9], outs[9:11], outs[11], outs[12]


def _mlp_fwd_early(tt, r_begun, h2, f_head, w1_early, w2_early):
    t_len = h2.shape[0]
    nt = t_len // tt
    n_late = N_DEV - FC_EARLY

    def body(r_begun_ref, h2_ref, fh_ref, w1_ref, w2_ref, r_ref, f_ref, l1_ref, l2_ref,
             land1, land2, send_sems, recv_sems, local_sems):
        i = pl.program_id(0)
        copies = _Copies(
            [(w1_ref.at[2], land1, 4), (w2_ref.at[4], land2, 2),
             (land1, l1_ref.at[1], 1), (land2, l2_ref.at[1], 1)],
            send_sems, recv_sems)
        keep = [pltpu.make_async_copy(land1, l1_ref.at[0], local_sems.at[0]),
                pltpu.make_async_copy(land2, l2_ref.at[0], local_sems.at[1])]

        @pl.when(i == 0)
        def _():
            copies.start(0, 1)

        @pl.when(i == nt - 1)
        def _():
            copies.wait_recv(0, 1)
            copies.start(2, 3)
            for cp in keep:
                cp.start()

        h2 = h2_ref[...]
        f_ref[...] = fh_ref[...]
        for j in range(FC_HEAD, FC_EARLY):
            ra = jnp.maximum(_dot(h2, w1_ref[j]), 0.0)
            r = (ra * ra).astype(BF16)
            r_ref[:, _early_col(j):_early_col(j) + FF_BLK] = r
            f_ref[...] += _dot(r, w2_ref[j])

        @pl.when(i == nt - 1)
        def _():
            copies.wait_recv(2, 3)
            copies.wait_send(0, 1, 2, 3)
            for cp in keep:
                cp.wait()

    tile = lambda w: pl.BlockSpec((tt, w), lambda i: (i, 0))
    hbm = pl.BlockSpec(memory_space=pl.ANY)
    outs = pl.pallas_call(
        body, name="mlp_fwd_early", grid=(nt,),
        out_shape=(jax.ShapeDtypeStruct((t_len, FC_EARLY * FF_BLK), BF16), jax.ShapeDtypeStruct((t_len, D), F32),
                   jax.ShapeDtypeStruct((n_late,) + w1_early.shape[1:], BF16),
                   jax.ShapeDtypeStruct((n_late,) + w2_early.shape[1:], BF16)),
        in_specs=[hbm, tile(D), tile(D), _resident((FC_EARLY, D, FF_BLK)), _resident((FC_EARLY, FF_BLK, D))],
        out_specs=(tile(R_HEAD_COLS), tile(D), hbm, hbm),
        input_output_aliases={0: 0},
        scratch_shapes=[pltpu.VMEM(w1_early.shape[1:], BF16), pltpu.VMEM(w2_early.shape[1:], BF16),
                        pltpu.SemaphoreType.DMA((4,)), pltpu.SemaphoreType.DMA((4,)),
                        pltpu.SemaphoreType.DMA((2,))],
        compiler_params=pltpu.CompilerParams(dimension_semantics=("arbitrary",), vmem_limit_bytes=VMEM_LIMIT),
    )(r_begun, h2, f_head, w1_early, w2_early)
    return outs[:2], outs[2:]


def _mlp_late_bwd(tt, r_early, x1, h2, f_early, tgt, mix, mod, n2pre, n2post, n1post,
                  w1_early, w2_early, w1_late, w2_late):
    t_len = x1.shape[0]
    nt = t_len // tt
    n_late = N_DEV - FC_EARLY
    late_cols = n_late * FF_BLK

    def body(re_ref, x1_ref, h2_ref, fe_ref, tgt_ref, mix_ref, mod_ref, n2pre_ref, n2post_ref,
             n1post_ref, w1e_ref, w2e_ref, w1l_ref, w2l_ref,
             rl_ref, df_ref, da_ref, dmix_ref, dx1_ref, redf_ref, redb_ref, dh2_acc):
        i = pl.program_id(0)

        @pl.when(i == 0)
        def _():
            redf_ref[...] = jnp.zeros_like(redf_ref)
            redb_ref[...] = jnp.zeros_like(redb_ref)

        x1v = x1_ref[...]
        gate1, scale2, gate2 = mod_ref[2:3, :], mod_ref[4:5, :], mod_ref[5:6, :]
        h2 = h2_ref[...]
        f = fe_ref[...]
        for j in range(n_late):
            cols = slice(j * FF_BLK, (j + 1) * FF_BLK)
            ra = jnp.maximum(_dot(h2, w1l_ref[j]), 0.0)
            r = (ra * ra).astype(BF16)
            rl_ref[:, cols] = r
            f = f + _dot(r, w2l_ref[j])
        post2 = n2post_ref[...]
        gate_post2 = gate2 * post2
        rf = _rstd(f)
        fhat = f * rf
        err = (x1v + fhat * gate_post2) - tgt_ref[...]
        dy = err * (1.0 / D)
        d_f, sum_f = _rms_bwd_gained(dy, gate_post2, fhat, rf)
        dfv = d_f.astype(BF16)
        df_ref[...] = dfv
        redf_ref[0:1, :] += post2 * sum_f
        redf_ref[1:2, :] += gate2 * sum_f
        redf_ref[2:3, :] += _colsum(err * err)

        for j in range(N_DEV):
            cols = slice(j * FF_BLK, (j + 1) * FF_BLK)
            if j < FC_EARLY:
                w1, w2, r = w1e_ref[j], w2e_ref[j], re_ref[:, _early_col(j):_early_col(j) + FF_BLK]
            else:
                jl = j - FC_EARLY
                w1, w2, r = w1l_ref[jl], w2l_ref[jl], rl_ref[:, jl * FF_BLK:(jl + 1) * FF_BLK]
            dr = _dot_nt(dfv, w2)
            da = (dr * (2.0 * jnp.sqrt(r.astype(F32)))).astype(BF16)
            da_ref[:, cols] = da
            contrib = _dot_nt(da, w1)
            if j == 0:
                dh2_acc[...] = contrib
            else:
                dh2_acc[...] += contrib
        dh2 = dh2_acc[...]
        pre2, post1 = n2pre_ref[...], n1post_ref[...]
        r2 = _rstd(x1v)
        xhat = x1v * r2
        d_x1, sum_h = _rms_bwd_gained(dh2, pre2 * (1.0 + scale2), xhat, r2)
        dx1 = dy + d_x1
        dx1_ref[...] = dx1
        mixv = mix_ref[...]
        rm = _rstd(mixv)
        mhat = mixv * rm
        d_mix, sum_m = _rms_bwd_gained(dx1, gate1 * post1, mhat, rm)
        dmix_ref[...] = d_mix.astype(BF16)
        redb_ref[0:1, :] += _colsum(dh2)
        redb_ref[1:2, :] += pre2 * sum_h
        redb_ref[2:3, :] += (1.0 + scale2) * sum_h
        redb_ref[3:4, :] += post1 * sum_m
        redb_ref[4:5, :] += gate1 * sum_m

    tile = lambda w: pl.BlockSpec((tt, w), lambda i: (i, 0))
    return pl.pallas_call(
        body, name="mlp_late_bwd", grid=(nt,),
        out_shape=(jax.ShapeDtypeStruct((t_len, late_cols), BF16), jax.ShapeDtypeStruct((t_len, D), BF16),
                   jax.ShapeDtypeStruct((t_len, D_FF), BF16), jax.ShapeDtypeStruct((t_len, D), BF16),
                   jax.ShapeDtypeStruct((t_len, D), F32), jax.ShapeDtypeStruct((8, D), F32),
                   jax.ShapeDtypeStruct((8, D), F32)),
        in_specs=[tile(FC_EARLY * FF_BLK), tile(D), tile(D), tile(D), tile(D),
                  tile(D), _full((8, D)), _full((1, D)), _full((1, D)), _full((1, D)),
                  _resident((FC_EARLY, D, FF_BLK)), _resident((FC_EARLY, FF_BLK, D)),
                  _resident((n_late, D, FF_BLK)), _resident((n_late, FF_BLK, D))],
        out_specs=(tile(late_cols), tile(D), tile(D_FF), tile(D), tile(D), _full((8, D)), _full((8, D))),
        scratch_shapes=[pltpu.VMEM((tt, D), F32)],
        compiler_params=pltpu.CompilerParams(dimension_semantics=("arbitrary",), vmem_limit_bytes=VMEM_LIMIT),
    )(r_early, x1, h2, f_early, tgt, mix, mod, n2pre, n2post, n1post, w1_early, w2_early, w1_late, w2_late)


def _mlp_wgrad(tt, r_early, r_late, da, df, h2):
    t_len = df.shape[0]
    nt = t_len // tt
    odd_steps = [j for j, rel in enumerate(WGRAD_ORDER) if rel % 2]

    def relation(j):
        rel = jnp.int32(WGRAD_ORDER[-1])
        for step in range(N_DEV - 2, -1, -1):
            rel = jnp.where(j == step, WGRAD_ORDER[step], rel)
        return rel

    def body(re_ref, rl_ref, da_ref, df_ref, h2_ref, own1_ref, own2_ref, out1_ref, out2_ref, diag1_ref, diag2_ref,
             acc1, acc2, snd1, snd2, sib1, sib2, dsnd1, dsnd2, send_sems, recv_sems):
        j, t = pl.program_id(0), pl.program_id(1)
        rows = pl.ds(pl.multiple_of(t * tt, tt), tt)
        x, y, c = _place()
        accs, snds, sibs = (acc1, acc2), (snd1, snd2), (sib1, sib2)
        dsnds, diags = (dsnd1, dsnd2), (diag1_ref, diag2_ref)

        def to_sibling(a, jj, buf=0):
            return pltpu.make_async_remote_copy(
                src_ref=snds[a].at[buf], dst_ref=sibs[a].at[jj],
                send_sem=send_sems.at[4 * a + jj], recv_sem=recv_sems.at[4 * a + jj],
                device_id=(x, y, 1 - c), device_id_type=MESH)

        def to_diagonal(a):
            return pltpu.make_async_remote_copy(
                src_ref=dsnds[a], dst_ref=diags[a], send_sem=send_sems.at[8 + a], recv_sem=recv_sems.at[8 + a],
                device_id=_peer(x, y, c, 6), device_id_type=MESH)

        @pl.when(t == 0)
        def _():
            acc2[...] = jnp.zeros_like(acc2)
            acc1[...] = jnp.zeros_like(acc1)

        for r_ref, mine in ((re_ref, relation(j) < FC_EARLY), (rl_ref, relation(j) >= FC_EARLY)):
            @pl.when(mine)
            def _():
                acc2[...] += _dot_tn(r_ref[...], df_ref[rows, :])
                acc1[...] += _dot_tn(h2_ref[rows, :], da_ref[...])

        for step, rel in enumerate(WGRAD_ORDER):
            jj = rel // 2

            @pl.when((t == nt - 1) & (j == step))
            def _():
                for a, (own_ref, out_ref) in enumerate(((own1_ref, out1_ref), (own2_ref, out2_ref))):
                    if rel % 2:
                        q = odd_steps.index(step)
                        if q >= 2:
                            to_sibling(a, WGRAD_ORDER[odd_steps[q - 2]] // 2).wait_send()
                        snds[a][q % 2] = accs[a][...].astype(BF16)
                        to_sibling(a, jj, q % 2).start()
                        continue
                    to_sibling(a, jj).wait_recv()
                    chip_sum = accs[a][...] + sibs[a][jj].astype(F32)
                    if rel == 6:
                        dsnds[a][...] = chip_sum.astype(BF16)
                        to_diagonal(a).start()
                    elif rel == 0:
                        own_ref[...] = chip_sum
                    else:
                        out_ref[0] = chip_sum.astype(BF16)
                    if step == N_DEV - 1:
                        for q in (2, 3):
                            to_sibling(a, WGRAD_ORDER[odd_steps[q]] // 2).wait_send()
                        to_diagonal(a).wait_recv()
                        to_diagonal(a).wait_send()

    assert WGRAD_ORDER[-1] == 0 and WGRAD_ORDER[-3:-1] == (2, 4)
    blk = pl.BlockSpec((tt, FF_BLK), lambda j, t: (t, relation(j)))
    early_block = lambda rel: jnp.where(rel < FC_HEAD, rel + FC_EARLY - FC_HEAD, rel - FC_HEAD)
    early = lambda j, t: (jnp.where(relation(j) < FC_EARLY, t, 0),
                          jnp.where(relation(j) < FC_EARLY, early_block(relation(j)), 0))
    late = lambda j, t: (jnp.where(relation(j) < FC_EARLY, 0, t), jnp.maximum(relation(j) - FC_EARLY, 0))
    chip = lambda j, t: (jnp.clip(j - 5, 0, 1), 0, 0)
    hbm = pl.BlockSpec(memory_space=pl.ANY)
    return pl.pallas_call(
        body, name="mlp_wgrad", grid=(N_DEV, nt),
        out_shape=(jax.ShapeDtypeStruct((D, FF_BLK), F32), jax.ShapeDtypeStruct((FF_BLK, D), F32),
                   jax.ShapeDtypeStruct((2, D, FF_BLK), BF16), jax.ShapeDtypeStruct((2, FF_BLK, D), BF16),
                   jax.ShapeDtypeStruct((D, FF_BLK), BF16), jax.ShapeDtypeStruct((FF_BLK, D), BF16)),
        in_specs=[pl.BlockSpec((tt, FF_BLK), early), pl.BlockSpec((tt, FF_BLK), late), blk,
                  _resident((t_len, D)), _resident((t_len, D))],
        out_specs=(_full((D, FF_BLK)), _full((FF_BLK, D)),
                   pl.BlockSpec((1, D, FF_BLK), chip), pl.BlockSpec((1, FF_BLK, D), chip), hbm, hbm),
        scratch_shapes=[pltpu.VMEM((D, FF_BLK), F32), pltpu.VMEM((FF_BLK, D), F32),
                        pltpu.VMEM((2, D, FF_BLK), BF16), pltpu.VMEM((2, FF_BLK, D), BF16),
                        pltpu.VMEM((4, D, FF_BLK), BF16), pltpu.VMEM((4, FF_BLK, D), BF16),
                        pltpu.VMEM((D, FF_BLK), BF16), pltpu.VMEM((FF_BLK, D), BF16),
                        pltpu.SemaphoreType.DMA((10,)), pltpu.SemaphoreType.DMA((10,))],
        compiler_params=pltpu.CompilerParams(dimension_semantics=("arbitrary", "arbitrary"),
                                             vmem_limit_bytes=VMEM_LIMIT),
    )(r_early, r_late, da, df, h2)


def _acc_rows(ref, row0, k, val):
    half = CHUNK // 2
    ref[row0:row0 + half, k * GROUP:(k + 1) * GROUP] += val[:half, :]
    ref[row0:row0 + half, D_A + k * GROUP:D_A + (k + 1) * GROUP] += val[half:, :]


def _attn_bwd(tt, dmix, x, z, cat, mod, n1pre, w_out, w_sp, bs_rows, ln_g, ln_b, w_pool, b_pool, pool_scale,
              red_fwd, red_bwd, chip_sums):
    t_len = z.shape[0]
    nt = t_len // tt
    hb = tt // HALO
    n_sums = len(chip_sums)

    def body(dmix_ref, x_ref, z_ref, zprev_ref, cat_ref, mod_ref, n1pre_ref, wout_ref, wsp_ref,
             bs_ref, lng_ref, lnb_ref, wp_ref, bp_ref, ps_ref, redf_ref, redb_ref, *rest):
        sum_out = rest[:n_sums]
        dz_ref, gwin_ref, gwout_ref, small_ref = rest[n_sums:n_sums + 4]
        sum_in = rest[n_sums + 4:2 * n_sums + 4]
        carry, acc_in, acc_out, dz_scr, bs_acc, send_sems, recv_sems = rest[2 * n_sums + 4:]
        s = pl.program_id(0)
        i = nt - 1 - s
        px, py, pc = _place()

        def chip_copy(a, r):
            return pltpu.make_async_remote_copy(
                src_ref=sum_out[a].at[r], dst_ref=sum_in[a].at[r],
                send_sem=send_sems.at[2 * a + r], recv_sem=recv_sems.at[2 * a + r],
                device_id=_peer(px, py, pc, 2 * (r + 1)), device_id_type=MESH)

        @pl.when(s == 0)
        def _():
            for a in range(n_sums):
                for r in range(2):
                    chip_copy(a, r).start()
            carry[...] = jnp.zeros_like(carry)
            acc_in[...] = jnp.zeros_like(acc_in)
            acc_out[...] = jnp.zeros_like(acc_out)
            bs_acc[...] = jnp.zeros_like(bs_acc)
            small_ref[...] = jnp.zeros_like(small_ref)
            small_ref[ROW_DMOD + 2:ROW_DMOD + 3, :] = redb_ref[3:4, :]
            small_ref[ROW_DMOD + 3:ROW_DMOD + 5, :] = redb_ref[0:2, :]
            small_ref[ROW_DMOD + 5:ROW_DMOD + 6, :] = redf_ref[0:1, :]
            small_ref[ROW_N1POST:ROW_N1POST + 1, :] = redb_ref[4:5, :]
            small_ref[ROW_N2PRE:ROW_N2PRE + 1, :] = redb_ref[2:3, :]
            small_ref[ROW_N2POST:ROW_N2POST + 1, :] = redf_ref[1:2, :]
            small_ref[ROW_LOSS:ROW_LOSS + 1, :] = redf_ref[2:3, :]

        dmixv = dmix_ref[...]
        dcat = _dot_nt(dmixv, wout_ref[...])
        acc_out[...] += _dot_tn(cat_ref[...], dmixv)

        z = z_ref[...]
        t_g, ga = _gelu_parts(z[:, :2 * D_A])
        u, vr = ga[:, :D_A], ga[:, D_A:]
        dv0 = vr - jnp.mean(vr, axis=-1, keepdims=True)
        rv = lax.rsqrt(jnp.mean(dv0 * dv0, axis=-1, keepdims=True) + EPS)
        vhat = dv0 * rv
        vb = (vhat * lng_ref[...] + lnb_ref[...]).astype(BF16)
        mask = _tril_mask()
        wc = [(wsp_ref[h] * mask).astype(BF16) for h in range(N_HEADS)]

        dya = dcat[:, :D_A]
        for h in range(N_HEADS):
            cols = slice(h * GROUP, (h + 1) * GROUP)
            bs_sum = jnp.zeros((CHUNK, GROUP), F32)
            ws_sum = jnp.zeros((CHUNK, CHUNK), F32)
            for ch in range(tt // CHUNK):
                rows = slice(ch * CHUNK, (ch + 1) * CHUNK)
                v_ch = vb[rows, cols]
                mixed = _dot(wc[h], v_ch) + bs_ref[:, cols]
                dy_ch = dya[rows, cols]
                dz_scr[rows, cols] = dy_ch * mixed
                dmixed = dy_ch * u[rows, cols]
                dmb = dmixed.astype(BF16)
                dz_scr[rows, D_A + h * GROUP:D_A + (h + 1) * GROUP] = _dot_tn(wc[h], dmb)
                bs_sum = bs_sum + dmixed
                ws_sum = ws_sum + _dot_nt(dmb, v_ch)
            _acc_rows(bs_acc, 0, h, bs_sum)
            _acc_rows(small_ref, ROW_WS, h, ws_sum)

        dvl = dz_scr[:, D_A:2 * D_A]
        dvhat = dvl * lng_ref[...]
        dvl_vhat = dvl * vhat
        dvr = rv * (dvhat - jnp.mean(dvhat, axis=-1, keepdims=True)
                    - vhat * jnp.mean(dvl_vhat * lng_ref[...], axis=-1, keepdims=True))
        small_ref[ROW_LN:ROW_LN + 1, 0:D_A] += _colsum(dvl_vhat)
        small_ref[ROW_LN:ROW_LN + 1, D_A:D] += _colsum(dvl)
        dga = jnp.concatenate([dz_scr[:, :D_A], dvr], axis=1)
        dza = dga * _gelu_grad(z[:, :2 * D_A], t_g)

        zb = z[:, 2 * D_A:]
        halo_prev = jnp.where(i == 0, 0.0, zprev_ref[...])
        diff = _pool_diff(zb, halo_prev, i * tt)
        dyb = dcat[:, D_A:]
        inv = _inv_counts(i * tt, tt)
        scaled, ddiffs = [], []
        for g in range(len(WINDOWS)):
            cols = slice(g * GROUP, (g + 1) * GROUP)
            db = diff[g].astype(BF16)
            wpg = wp_ref[g].astype(BF16)
            pre = _dot(db, wpg) + bp_ref[:, cols]
            small_ref[ROW_POOL:ROW_POOL + 1, cols] += _colsum(dyb[:, cols] * pre)
            dpre = dyb[:, cols] * ps_ref[:, cols]
            small_ref[ROW_POOL:ROW_POOL + 1, D_B + g * GROUP:D_B + (g + 1) * GROUP] += _colsum(dpre)
            dpb = dpre.astype(BF16)
            _acc_rows(small_ref, ROW_WP, g, _dot_tn(db, dpb))
            ddiff = _dot_nt(dpb, wpg)
            ddiffs.append(ddiff)
            scaled.append(ddiff * inv[g])
        scaled_all = jnp.concatenate(scaled, axis=1)
        ext = jnp.concatenate([scaled_all, carry[...]], axis=0)
        n_ext = tt + HALO
        s2 = ext + pltpu.roll(ext, n_ext - 1, 0)
        t4 = s2[:, GROUP:]
        s4 = t4 + pltpu.roll(t4, n_ext - 2, 0)
        t8 = s4[:, GROUP:]
        s8 = t8 + pltpu.roll(t8, n_ext - 4, 0)
        t16 = s8[:, GROUP:]
        s16 = t16 + pltpu.roll(t16, n_ext - 8, 0)
        back = [s2[:, :GROUP], s4[:, :GROUP], s8[:, :GROUP], s16]
        carry[...] = scaled_all[:HALO, :]
        dzb = jnp.concatenate([back[g][:tt, :] - ddiffs[g] for g in range(len(WINDOWS))], axis=1)

        dzv = jnp.concatenate([dza, dzb], axis=1).astype(BF16)
        dz_ref[...] = dzv
        xv = x_ref[...]
        h1 = (xv * _rstd(xv) * (n1pre_ref[...] * (1.0 + mod_ref[1:2, :])) + mod_ref[0:1, :]).astype(BF16)
        acc_in[...] += _dot_tn(dzv, h1)

        @pl.when(s == nt - 1)
        def _():
            gwin_ref[...] = acc_in[...].astype(BF16)
            gwout_ref[...] = acc_out[...].astype(BF16)
            bs = _unfold(bs_acc[...])
            for h in range(N_HEADS):
                small_ref[ROW_BS + h:ROW_BS + h + 1, 0:GROUP] = jnp.sum(
                    bs[:, h * GROUP:(h + 1) * GROUP].T, axis=0, keepdims=True)
            for a in range(n_sums):
                for r in range(2):
                    chip_copy(a, r).wait_recv()
                    chip_copy(a, r).wait_send()

    rev = lambda w: pl.BlockSpec((tt, w), lambda s: (nt - 1 - s, 0))
    zprev = pl.BlockSpec((HALO, D_B), lambda s: (jnp.maximum((nt - 1 - s) * hb - 1, 0), 2))
    hbm = pl.BlockSpec(memory_space=pl.ANY)
    outs = pl.pallas_call(
        body, name="attn_bwd", grid=(nt,),
        out_shape=tuple([jax.ShapeDtypeStruct((t_len, D_Z), BF16), jax.ShapeDtypeStruct((D_Z, D), BF16),
                         jax.ShapeDtypeStruct((D, D), BF16), jax.ShapeDtypeStruct((SMALL_ROWS, D), F32)]
                        + [jax.ShapeDtypeStruct(cs.shape, cs.dtype) for cs in chip_sums]),
        in_specs=[rev(D), rev(D), rev(D_Z), zprev, rev(D), _full((8, D)), _full((1, D)),
                  _resident((D, D)), _full((N_HEADS, CHUNK, CHUNK)), _full((CHUNK, D_A)),
                  _full((1, D_A)), _full((1, D_A)), _full((len(WINDOWS), GROUP, GROUP)), _full((1, D_B)),
                  _full((1, D_B)), _full((8, D)), _full((8, D))] + [_resident(cs.shape) for cs in chip_sums],
        out_specs=tuple([rev(D_Z), _resident((D_Z, D)), _resident((D, D)), _full((SMALL_ROWS, D))]
                        + [hbm] * n_sums),
        scratch_shapes=[pltpu.VMEM((HALO, D_B), F32), pltpu.VMEM((D_Z, D), F32), pltpu.VMEM((D, D), F32),
                        pltpu.VMEM((tt, 2 * D_A), F32), pltpu.VMEM((CHUNK // 2, D), F32),
                        pltpu.SemaphoreType.DMA((2 * n_sums,)), pltpu.SemaphoreType.DMA((2 * n_sums,))],
        compiler_params=pltpu.CompilerParams(dimension_semantics=("arbitrary",), vmem_limit_bytes=VMEM_LIMIT),
    )(dmix, x, z, z, cat, mod, n1pre, w_out, w_sp, bs_rows, ln_g, ln_b, w_pool, b_pool, pool_scale,
      red_fwd, red_bwd, *chip_sums)
    return outs[:4], outs[4:]


def _adam(w, g, m, v):
    m2 = ADAM_B1 * m + (1.0 - ADAM_B1) * g
    v2 = ADAM_B2 * v + (1.0 - ADAM_B2) * (g * g)
    m_hat = m2 / (1.0 - ADAM_B1 ** ADAM_STEP)
    v_hat = v2 / (1.0 - ADAM_B2 ** ADAM_STEP)
    delta = -ADAM_LR * (m_hat / (jnp.sqrt(v_hat) + ADAM_EPS) + ADAM_WD * w)
    return delta, m2, v2


def _adamw_fc(steps, fc):
    n_fc = len(fc)

    def body(*refs):
        ins, outs = refs[:6 * n_fc], refs[6 * n_fc:]
        for k in range(n_fc):
            w_ref, own_ref, arr_ref, diag_ref, m_ref, v_ref = ins[6 * k:6 * k + 6]
            g = ((own_ref[...] + arr_ref[0].astype(F32)) + arr_ref[1].astype(F32)) + diag_ref[...].astype(F32)
            outs[4 * k][...] = g
            outs[4 * k + 1][...], outs[4 * k + 2][...], outs[4 * k + 3][...] = _adam(
                w_ref[...], g, m_ref[...], v_ref[...])

    specs_in, specs_out, shapes, args = [], [], [], []
    for w, own, arrived, diagonal, m, v in fc:
        rows, cols = w.shape
        blk = pl.BlockSpec((rows // steps, cols), lambda i: (i, 0))
        specs_in += [blk, blk, pl.BlockSpec((2, rows // steps, cols), lambda i: (0, i, 0)), blk, blk, blk]
        specs_out += [blk] * 4
        shapes += [jax.ShapeDtypeStruct((rows, cols), F32)] * 4
        args += [w, own, arrived, diagonal, m, v]
    outs = pl.pallas_call(
        body, name="adamw_fc", grid=(steps,), out_shape=tuple(shapes), in_specs=specs_in, out_specs=tuple(specs_out),
        compiler_params=pltpu.CompilerParams(dimension_semantics=("arbitrary",), vmem_limit_bytes=VMEM_LIMIT),
    )(*args)
    return [outs[4 * k:4 * k + 4] for k in range(n_fc)]


def _adamw_ada(rb, w, sc, total, m, v):
    rows, cols = w.shape

    def body(w_ref, sc_ref, t_ref, m_ref, v_ref, g_ref, d_ref, m2_ref, v2_ref, dm):
        me = _index(_place())
        for dev in range(N_DEV):
            @pl.when((pl.program_id(0) == 0) & (me == dev))
            def _():
                for b in range(N_DEV):
                    for k in range(6):
                        lo, hi = max(cols * dev, D * k), min(cols * (dev + 1), D * (k + 1))
                        if lo < hi:
                            dm[b:b + 1, lo - cols * dev:hi - cols * dev] = t_ref[
                                _table_row(b) + k:_table_row(b) + k + 1, lo - D * k:hi - D * k]

        g = _dot_tn(sc_ref[...].astype(BF16), dm[...].astype(BF16))
        g_ref[...] = g
        d_ref[...], m2_ref[...], v2_ref[...] = _adam(w_ref[...], g, m_ref[...], v_ref[...])

    blk = pl.BlockSpec((rb, cols), lambda i: (i, 0))
    shp = jax.ShapeDtypeStruct((rows, cols), F32)
    return pl.pallas_call(
        body, name="adamw_ada", grid=(rows // rb,), out_shape=(shp, shp, shp, shp),
        in_specs=[blk, pl.BlockSpec((N_DEV, rb), lambda i: (0, i)), _full(total.shape), blk, blk],
        out_specs=(blk, blk, blk, blk),
        scratch_shapes=[pltpu.VMEM((N_DEV, cols), F32)],
        compiler_params=pltpu.CompilerParams(dimension_semantics=("arbitrary",)),
    )(w, sc, total, m, v)


def _unfold(acc_rows):
    return jnp.concatenate([acc_rows[:, :D_A], acc_rows[:, D_A:]], axis=0)


def _adamw_small(total, params, shards):
    n = len(params)
    flat = [a for p in params for a in p]

    def body(*refs):
        s_ref = refs[0]
        p_refs = refs[1:1 + 3 * n]
        s_refs = refs[1 + 3 * n:1 + 3 * n + 4 * len(shards)]
        loss_ref = refs[1 + 3 * n + 4 * len(shards)]
        o_refs = refs[2 + 3 * n + 4 * len(shards):2 + 3 * n + 4 * len(shards) + 4 * n]
        so_refs = refs[2 + 3 * n + 4 * len(shards) + 4 * n:]
        d_b_ada = s_ref[0:6, :]
        for b in range(1, N_DEV):
            d_b_ada = d_b_ada + s_ref[_table_row(b):_table_row(b) + 6, :]
        misc = lambda r: s_ref[PK_MISC + r - ROW_N1PRE:PK_MISC + r - ROW_N1PRE + 1, :]
        loss = jnp.sum(misc(ROW_LOSS), axis=-1, keepdims=True) * (0.5 / D)
        loss_ref[...] = loss
        mask = _tril_mask()
        ws = _unfold(s_ref[PK_WS:PK_WS + 64, :])
        wp = _unfold(s_ref[PK_WP:PK_WP + 64, :])
        grads = [
            d_b_ada,
            misc(ROW_N1PRE), misc(ROW_N1POST), misc(ROW_N2PRE), misc(ROW_N2POST),
            misc(ROW_LN)[:, :D_A], misc(ROW_LN)[:, D_A:],
            misc(ROW_POOL)[:, :D_B], misc(ROW_POOL)[:, D_B:],
            s_ref[PK_BS:PK_BS + N_HEADS, 0:GROUP],
            jnp.stack([ws[:, h * GROUP:(h + 1) * GROUP] * mask for h in range(N_HEADS)]),
            jnp.stack([wp[:, g * GROUP:(g + 1) * GROUP] for g in range(len(WINDOWS))]),
        ]
        for k in range(n):
            w_ref, m_ref, v_ref = p_refs[3 * k:3 * k + 3]
            g = grads[k]
            if k == 0:
                for j in range(6):
                    o_refs[0][j] = g[j:j + 1, :]
                    o_refs[1][j], o_refs[2][j], o_refs[3][j] = _adam(w_ref[j], g[j:j + 1, :], m_ref[j], v_ref[j])
                continue
            o_refs[4 * k][...] = g
            o_refs[4 * k + 1][...], o_refs[4 * k + 2][...], o_refs[4 * k + 3][...] = _adam(
                w_ref[...], g, m_ref[...], v_ref[...])
        for k in range(len(shards)):
            w_ref, g_ref, m_ref, v_ref = s_refs[4 * k:4 * k + 4]
            so_refs[3 * k][...], so_refs[3 * k + 1][...], so_refs[3 * k + 2][...] = _adam(
                w_ref[...], g_ref[...], m_ref[...], v_ref[...])

    vm = pl.BlockSpec(memory_space=pltpu.VMEM)
    out_shape = [jax.ShapeDtypeStruct((1, 1), F32)]
    for w, _, _ in params:
        out_shape += [jax.ShapeDtypeStruct(w.shape, F32)] * 4
    for w, _, _, _ in shards:
        out_shape += [jax.ShapeDtypeStruct(w.shape, F32)] * 3
    return pl.pallas_call(
        body, name="adamw_small", out_shape=tuple(out_shape),
        in_specs=[vm] * (1 + 3 * n + 4 * len(shards)), out_specs=tuple([vm] * len(out_shape)),
        compiler_params=pltpu.CompilerParams(vmem_limit_bytes=VMEM_LIMIT),
    )(total, *flat, *[a for s in shards for a in s])


TT_ATTN_FWD = 512
ATTN_LAG = 2
TT_MLP_FWD = 512
TT_MLP = 256
TT_WGRAD = 2048
TT_ATTN_BWD = 512
TT_TAIL = 512


def kernel(x, c, w_ada, b_ada, norm1_pre, norm1_post, w_in, w_spatial, b_spatial, ln_v_gain, ln_v_bias, w_pool, b_pool, pool_scale, w_out, norm2_pre, norm2_post, w_fc1, w_fc2, loss_target, m_w_ada, m_b_ada, m_norm1_pre, m_norm1_post, m_w_in, m_w_spatial, m_b_spatial, m_ln_v_gain, m_ln_v_bias, m_w_pool, m_b_pool, m_pool_scale, m_w_out, m_norm2_pre, m_norm2_post, m_w_fc1, m_w_fc2, v_w_ada, v_b_ada, v_norm1_pre, v_norm1_post, v_w_in, v_w_spatial, v_b_spatial, v_ln_v_gain, v_ln_v_bias, v_w_pool, v_b_pool, v_pool_scale, v_w_out, v_norm2_pre, v_norm2_post, v_w_fc1, v_w_fc2):
    t_len = x.shape[1]
    ada_cols = w_ada.shape[1]
    tt = lambda want: min(want, t_len)

    x2 = x.reshape(t_len, D)
    tgt = loss_target.reshape(t_len, D)
    row = lambda a: a.reshape(1, -1)

    w_in_shard, w_out_shard, w1_shard, w2_shard = _cast_shards([w_in.T, w_out, w_fc1, w_fc2])

    bs_rows = jnp.repeat(b_spatial.T, GROUP, axis=1)
    attn_consts = (w_spatial, bs_rows, row(ln_v_gain), row(ln_v_bias), w_pool, row(b_pool), row(pool_scale))

    (z, cat, mix, x1, h2, r_begun, f_head, mod, sc), (w1_early, w2_early), w_out_all, w_in_t = _attn_fwd(
        tt(TT_ATTN_FWD), x2, c.reshape(1, D), w_ada, b_ada.reshape(N_DEV, 1, ada_cols), row(norm1_pre),
        row(norm1_post),
        w_in_shard, w_out_shard, *attn_consts, (w1_shard, w2_shard), row(norm2_pre))
    (r_early, f_early), (w1_late, w2_late) = _mlp_fwd_early(
        tt(TT_MLP_FWD), r_begun, h2, f_head, w1_early, w2_early)
    r_late, df, da, dmix, dx1, red_fwd, red_bwd = _mlp_late_bwd(
        tt(TT_MLP), r_early, x1, h2, f_early, tgt, mix, mod, row(norm2_pre), row(norm2_post), row(norm1_post),
        w1_early, w2_early, w1_late, w2_late)
    own_w1, own_w2, sums_w1, sums_w2, diag_w1, diag_w2 = _mlp_wgrad(tt(TT_WGRAD), r_early, r_late, da, df, h2)
    (dz, p_in, p_out, small), (arr_w1, arr_w2) = _attn_bwd(
        tt(TT_ATTN_BWD), dmix, x2, z, cat, mod, row(norm1_pre), w_out_all, *attn_consts, red_fwd, red_bwd,
        [sums_w1, sums_w2])
    (grad_w1, d_w1, m_w1, v_w1), (grad_w2, d_w2, m_w2, v_w2) = _adamw_fc(
        4, [(w_fc1, own_w1, arr_w1, diag_w1, m_w_fc1, v_w_fc1), (w_fc2, own_w2, arr_w2, diag_w2, m_w_fc2, v_w_fc2)])
    grad_in_t, grad_out, total, grad_x = _tail_comm(
        [p_in.reshape(N_DEV, D_Z // N_DEV, D), p_out.reshape(N_DEV, D // N_DEV, D)], small, 64,
        tt(TT_TAIL), dz, dx1, x2, mod, row(norm1_pre), w_in_t)

    grad_ada, d_ada, m_ada, v_ada = _adamw_ada(256, w_ada, sc, total, m_w_ada, v_w_ada)

    six = lambda a: a.reshape(6, 1, D)
    small_params = [
        (six(b_ada), six(m_b_ada), six(v_b_ada)),
        (row(norm1_pre), row(m_norm1_pre), row(v_norm1_pre)),
        (row(norm1_post), row(m_norm1_post), row(v_norm1_post)),
        (row(norm2_pre), row(m_norm2_pre), row(v_norm2_pre)),
        (row(norm2_post), row(m_norm2_post), row(v_norm2_post)),
        (row(ln_v_gain), row(m_ln_v_gain), row(v_ln_v_gain)),
        (row(ln_v_bias), row(m_ln_v_bias), row(v_ln_v_bias)),
        (row(pool_scale), row(m_pool_scale), row(v_pool_scale)),
        (row(b_pool), row(m_b_pool), row(v_b_pool)),
        (b_spatial, m_b_spatial, v_b_spatial),
        (w_spatial, m_w_spatial, v_w_spatial),
        (w_pool, m_w_pool, v_w_pool),
    ]
    outs = _adamw_small(total, small_params, [(w_out, grad_out, m_w_out, v_w_out),
                                              (w_in.T, grad_in_t, m_w_in.T, v_w_in.T)])
    d_out, m_out, v_out, d_in_t, m_in_t, v_in_t = outs[1 + 4 * len(small_params):]
    loss = outs[0].reshape(())
    names = ["b_ada", "norm1_pre", "norm1_post", "norm2_pre", "norm2_post", "ln_v_gain", "ln_v_bias", "pool_scale",
             "b_pool", "b_spatial", "w_spatial", "w_pool"]
    shapes = dict(b_ada=b_ada.shape, norm1_pre=norm1_pre.shape, norm1_post=norm1_post.shape,
                  norm2_pre=norm2_pre.shape, norm2_post=norm2_post.shape, ln_v_gain=ln_v_gain.shape,
                  ln_v_bias=ln_v_bias.shape, pool_scale=pool_scale.shape, b_pool=b_pool.shape,
                  b_spatial=b_spatial.shape, w_spatial=w_spatial.shape, w_pool=w_pool.shape)
    res = {}
    for k, nm in enumerate(names):
        res[nm] = tuple(o.reshape(shapes[nm]) for o in outs[1 + 4 * k:5 + 4 * k])
    res["w_ada"] = (grad_ada, d_ada, m_ada, v_ada)
    res["w_in"] = (grad_in_t.T, d_in_t.T, m_in_t.T, v_in_t.T)
    res["w_out"] = (grad_out, d_out, m_out, v_out)
    res["w_fc1"] = (grad_w1, d_w1, m_w1, v_w1)
    res["w_fc2"] = (grad_w2, d_w2, m_w2, v_w2)

    order = ["w_ada", "b_ada", "norm1_pre", "norm1_post", "w_in", "w_spatial", "b_spatial", "ln_v_gain", "ln_v_bias",
             "w_pool", "b_pool", "pool_scale", "w_out", "norm2_pre", "norm2_post", "w_fc1", "w_fc2"]
    return (loss, grad_x.reshape(x.shape),
            *[res[nm][0] for nm in order], *[res[nm][1] for nm in order],
            *[res[nm][2] for nm in order], *[res[nm][3] for nm in order])
```

```python
import functools

import jax
import jax.numpy as jnp
from jax import lax
from jax.experimental import pallas as pl
from jax.experimental.pallas import tpu as pltpu

F32 = jnp.float32
BF16 = jnp.bfloat16
MESH = pl.DeviceIdType.MESH

N_DEV = 8
D = 1024
D_A = 512
D_B = 512
D_Z = 2 * D_A + D_B
N_HEADS = 4
CHUNK = 128
WINDOWS = (2, 4, 8, 16)
GROUP = 128
D_FF = 4096
FF_BLK = D_FF // N_DEV
HALO = 16
EPS = 1e-6
VMEM_LIMIT = 60 * 1024 * 1024

ADAM_LR = 0.001
ADAM_B1 = 0.9
ADAM_B2 = 0.999
ADAM_EPS = 1e-08
ADAM_WD = 0.01
ADAM_STEP = 10

ROW_DMOD = 0
ROW_N1PRE, ROW_N1POST, ROW_N2PRE, ROW_N2POST = 8, 9, 10, 11
ROW_LN = 12
ROW_POOL = 13
ROW_LOSS = 14
ROW_BS = 16
ROW_WS = 24
ROW_WP = 88
SMALL_ROWS = 152
PACK_FINE = 40
PACK_HALF = PACK_FINE + 64
PACK_ROWS = 2 * PACK_HALF
PK_WS = PACK_FINE
PK_TABLE_B = PACK_HALF
PK_MISC = PK_TABLE_B + 24
PK_BS = PK_MISC + 8
PK_WP = PK_BS + 8


def _table_row(b):
    if isinstance(b, int):
        return 8 * b if 8 * b < PACK_FINE else 8 * b + PK_TABLE_B - PACK_FINE
    return 8 * b + jnp.where(8 * b < PACK_FINE, 0, PK_TABLE_B - PACK_FINE)


def _dot(a, b):
    return jnp.dot(a, b, preferred_element_type=F32)


def _dot_nt(a, b):
    return lax.dot_general(a, b, (((1,), (1,)), ((), ())), preferred_element_type=F32)


def _dot_tn(a, b):
    return lax.dot_general(a, b, (((0,), (0,)), ((), ())), preferred_element_type=F32)


def _rstd(v):
    return lax.rsqrt(jnp.mean(v * v, axis=-1, keepdims=True) + EPS)


def _rms_bwd(d_hat, hat, rstd):
    return rstd * (d_hat - hat * jnp.mean(d_hat * hat, axis=-1, keepdims=True))


def _rms_bwd_gained(g, gain, hat, rstd):
    g_hat = g * hat
    d_v = rstd * (g * gain - hat * jnp.mean(g_hat * gain, axis=-1, keepdims=True))
    return d_v, _colsum(g_hat)


_K0 = 0.7978845608028654
_K1 = 0.044715


def _gelu_parts(v):
    t = jnp.tanh(v * (_K0 + (_K0 * _K1) * (v * v)))
    return t, v * (0.5 + 0.5 * t)


def _gelu_grad(v, t):
    return (0.5 + 0.5 * t) + (0.5 * v) * (1.0 - t * t) * (_K0 + (3.0 * _K0 * _K1) * (v * v))


def _colsum(v):
    return jnp.sum(v, axis=0, keepdims=True)


def _full(shape):
    n = len(shape)
    return pl.BlockSpec(shape, lambda *_: (0,) * n)


def _resident(shape):
    n = len(shape)
    return pl.BlockSpec(shape, lambda *_: (0,) * n, pipeline_mode=pl.Buffered(1))


def _place():
    x, y, c = lax.axis_index("x"), lax.axis_index("y"), lax.axis_index("c")
    return x, y, c


def _flip(v, bit):
    return 1 - v if bit else v


def _peer(x, y, c, k):
    return (_flip(x, (k >> 2) & 1), _flip(y, (k >> 1) & 1), _flip(c, k & 1))


def _index(p):
    return 4 * p[0] + 2 * p[1] + p[2]


def _cast_shards(shards):
    def body(*refs):
        for src, dst in zip(refs[:len(shards)], refs[len(shards):]):
            dst[...] = src[...].astype(BF16)

    vm = pl.BlockSpec(memory_space=pltpu.VMEM)
    return pl.pallas_call(
        body, name="cast_shards", out_shape=tuple(jax.ShapeDtypeStruct(s.shape, BF16) for s in shards),
        in_specs=[vm] * len(shards), out_specs=tuple([vm] * len(shards)),
    )(*shards)


FC_EARLY = 6
FC_HEAD = 2
R_HEAD_COLS = (FC_EARLY - FC_HEAD) * FF_BLK
WGRAD_ORDER = (7, 6, 1, 3, 5, 2, 4, 0)


def _early_col(j):
    return R_HEAD_COLS + j * FF_BLK if j < FC_HEAD else (j - FC_HEAD) * FF_BLK


class _Copies:
    def __init__(self, entries, send_sems, recv_sems):
        self.place = _place()
        self.entries, self.send_sems, self.recv_sems = entries, send_sems, recv_sems

    def _copy(self, i, arrival=False):
        src, dst, rel = self.entries[i]
        return pltpu.make_async_remote_copy(
            src_ref=dst if arrival else src, dst_ref=dst, send_sem=self.send_sems.at[i],
            recv_sem=self.recv_sems.at[i], device_id=_peer(*self.place, rel), device_id_type=MESH)

    def start(self, *which):
        for i in which:
            self._copy(i).start()

    def wait_recv(self, *which):
        for i in which:
            self._copy(i, arrival=True).wait_recv()

    def wait_send(self, *which):
        for i in which:
            self._copy(i).wait_send()


def _tail_comm(parts, small, row_chunk, tt, dz, dx1, x, mod, n1pre, w_in_t):
    n = len(parts)
    t_len = x.shape[0]
    nt = t_len // tt

    def body(*refs):
        p_refs, small_ref = refs[:n], refs[n]
        dz_ref, dx1_ref, x_ref, mod_ref, n1pre_ref, win_ref = refs[n + 1:n + 7]
        outs = refs[n + 7:]
        g_refs, total_ref, gx_ref = outs[:n], outs[n], outs[n + 1]
        scr = outs[n + 2:]
        from_sib = scr[0:n]
        chip_out = scr[n:2 * n]
        chip_in = scr[2 * n:3 * n]
        pack, pack_sib, fine, bulk, total_scr, head = scr[3 * n:3 * n + 6]
        send_a, recv_a, send_b, recv_b, send_s, recv_s = scr[3 * n + 6:]
        step = pl.program_id(0)
        x, y, c = _place()
        me = _index((x, y, c))
        sibling = (x, y, 1 - c)
        my_chip = 2 * x + y
        others = [(1 - x, y), (x, 1 - y), (1 - x, 1 - y)]
        my_half = pl.ds(pl.multiple_of(PACK_HALF * c, 8), PACK_HALF)

        def pack_to_sibling():
            return pltpu.make_async_remote_copy(
                src_ref=pack, dst_ref=pack_sib, send_sem=send_s.at[0], recv_sem=recv_s.at[0],
                device_id=sibling, device_id_type=MESH)

        def half_to_chip(r, part):
            buf = (fine, bulk)[part]
            return pltpu.make_async_remote_copy(
                src_ref=buf.at[my_chip], dst_ref=buf.at[my_chip],
                send_sem=send_s.at[1 + 3 * part + r], recv_sem=recv_s.at[1 + 3 * part + r],
                device_id=(*others[r], c), device_id_type=MESH)

        def half_from_chip(r, part):
            k = 2 * others[r][0] + others[r][1]
            buf = (fine, bulk)[part]
            return pltpu.make_async_remote_copy(
                src_ref=buf.at[k], dst_ref=buf.at[k],
                send_sem=send_s.at[1 + 3 * part + r], recv_sem=recv_s.at[1 + 3 * part + r],
                device_id=(*others[r], c), device_id_type=MESH)

        def total_to_sibling():
            return pltpu.make_async_remote_copy(
                src_ref=total_scr.at[my_half], dst_ref=total_scr.at[my_half],
                send_sem=send_s.at[7], recv_sem=recv_s.at[7], device_id=sibling, device_id_type=MESH)

        def total_from_sibling():
            sib_half = pl.ds(pl.multiple_of(PACK_HALF * (1 - c), 8), PACK_HALF)
            return pltpu.make_async_remote_copy(
                src_ref=total_scr.at[sib_half], dst_ref=total_scr.at[sib_half],
                send_sem=send_s.at[7], recv_sem=recv_s.at[7], device_id=sibling, device_id_type=MESH)

        def to_sibling(a, k):
            return pltpu.make_async_remote_copy(
                src_ref=p_refs[a].at[2 * k + (1 - c)], dst_ref=from_sib[a].at[k],
                send_sem=send_a.at[a], recv_sem=recv_a.at[a], device_id=sibling, device_id_type=MESH)

        def all_from_sibling(a):
            return pltpu.make_async_remote_copy(
                src_ref=from_sib[a], dst_ref=from_sib[a], send_sem=send_a.at[a], recv_sem=recv_a.at[a],
                device_id=sibling, device_id_type=MESH)

        def to_chip(a, r):
            return pltpu.make_async_remote_copy(
                src_ref=chip_out[a].at[r], dst_ref=chip_in[a].at[r],
                send_sem=send_b.at[3 * a + r], recv_sem=recv_b.at[3 * a + r],
                device_id=(*others[r], c), device_id_type=MESH)

        @pl.when(step == 0)
        def _():
            head[...] = jnp.zeros_like(head)
            for a in range(n):
                for k in range(4):
                    to_sibling(a, k).start()

        @pl.when(step == min(1, nt - 1))
        def _():
            for a in range(n):
                all_from_sibling(a).wait_recv()
                rows = p_refs[a].shape[1]
                for r in range(3):
                    k = 2 * others[r][0] + others[r][1]
                    for s in range(0, rows, row_chunk):
                        sl = pl.ds(s, row_chunk)
                        chip_out[a][r, sl, :] = (p_refs[a][2 * k + c, sl, :].astype(F32)
                                                 + from_sib[a][k, sl, :].astype(F32)).astype(BF16)
                    to_chip(a, r).start()
                for s in range(0, rows, row_chunk):
                    sl = pl.ds(s, row_chunk)
                    g_refs[a][sl, :] = (p_refs[a][2 * my_chip + c, sl, :].astype(F32)
                                        + from_sib[a][my_chip, sl, :].astype(F32))

        dh1 = _dot(dz_ref[...], win_ref[...])
        xv = x_ref[...]
        r1 = _rstd(xv)
        xhat = xv * r1
        scale1 = mod_ref[1:2, :]
        pre1 = n1pre_ref[...]
        d_x, sum_h = _rms_bwd_gained(dh1, pre1 * (1.0 + scale1), xhat, r1)
        gx_ref[...] = dx1_ref[...] + d_x
        head[ROW_DMOD:ROW_DMOD + 1, :] += _colsum(dh1)
        head[ROW_DMOD + 1:ROW_DMOD + 2, :] += pre1 * sum_h
        head[ROW_N1PRE:ROW_N1PRE + 1, :] += (1.0 + scale1) * sum_h

        @pl.when(step == nt - 1)
        def _():
            pack[0:PACK_FINE, :] = jnp.zeros((PACK_FINE, D), F32)
            pack[PK_TABLE_B:PK_MISC, :] = jnp.zeros((PK_MISC - PK_TABLE_B, D), F32)
            pack[pl.ds(pl.multiple_of(_table_row(me), 8), 8), :] = small_ref[0:8, :] + head[0:8, :]
            pack[PK_WS:PK_WS + 64, :] = small_ref[ROW_WS:ROW_WS + 64, :]
            pack[PK_MISC:PK_MISC + 8, :] = small_ref[ROW_N1PRE:ROW_N1PRE + 8, :] + head[8:16, :]
            pack[PK_BS:PK_BS + 8, :] = small_ref[ROW_BS:ROW_BS + 8, :]
            pack[PK_WP:PK_WP + 64, :] = small_ref[ROW_WP:ROW_WP + 64, :]
            pack_to_sibling().start()
            pack_to_sibling().wait_recv()
            chip_sum = pack[my_half, :] + pack_sib[my_half, :]
            fine[my_chip] = chip_sum[:PACK_FINE, :]
            bulk[my_chip] = chip_sum[PACK_FINE:, :].astype(BF16)
            for r in range(3):
                half_to_chip(r, 0).start()
                half_to_chip(r, 1).start()
            for a in range(n):
                rows = p_refs[a].shape[1]
                for r in range(3):
                    to_chip(a, r).wait_recv()
                    for s in range(0, rows, row_chunk):
                        sl = pl.ds(s, row_chunk)
                        g_refs[a][sl, :] = g_refs[a][sl, :] + chip_in[a][r, sl, :].astype(F32)
            for r in range(3):
                half_from_chip(r, 0).wait_recv()
                half_from_chip(r, 1).wait_recv()
            half_start = pl.multiple_of(PACK_HALF * c, 8)
            total_scr[pl.ds(half_start, PACK_FINE), :] = ((fine[0] + fine[1]) + fine[2]) + fine[3]
            total_scr[pl.ds(half_start + PACK_FINE, PACK_HALF - PACK_FINE), :] = (
                (bulk[0].astype(F32) + bulk[1].astype(F32)) + bulk[2].astype(F32)) + bulk[3].astype(F32)
            total_to_sibling().start()
            total_from_sibling().wait_recv()
            total_ref[...] = total_scr[...]
            for a in range(n):
                all_from_sibling(a).wait_send()
                for r in range(3):
                    to_chip(a, r).wait_send()
            pack_to_sibling().wait_send()
            for r in range(3):
                half_to_chip(r, 0).wait_send()
                half_to_chip(r, 1).wait_send()
            total_to_sibling().wait_send()

    tile = lambda w: pl.BlockSpec((tt, w), lambda i: (i, 0))
    return pl.pallas_call(
        body, name="tail_comm", grid=(nt,),
        out_shape=tuple([jax.ShapeDtypeStruct(p.shape[1:], F32) for p in parts]
                        + [jax.ShapeDtypeStruct((PACK_ROWS, D), F32), jax.ShapeDtypeStruct((t_len, D), F32)]),
        in_specs=([_resident(p.shape) for p in parts] + [_resident(small.shape)]
                  + [tile(D_Z), tile(D), tile(D), _full((8, D)), _full((1, D)), _resident((D_Z, D))]),
        out_specs=tuple([_full(p.shape[1:]) for p in parts] + [_full((PACK_ROWS, D)), tile(D)]),
        scratch_shapes=(
            [pltpu.VMEM((4,) + p.shape[1:], BF16) for p in parts]
            + [pltpu.VMEM((3,) + p.shape[1:], BF16) for p in parts]
            + [pltpu.VMEM((3,) + p.shape[1:], BF16) for p in parts]
            + [pltpu.VMEM((PACK_ROWS, D), F32), pltpu.VMEM((PACK_ROWS, D), F32),
               pltpu.VMEM((4, PACK_FINE, D), F32), pltpu.VMEM((4, PACK_HALF - PACK_FINE, D), BF16),
               pltpu.VMEM((PACK_ROWS, D), F32), pltpu.VMEM((16, D), F32)]
            + [pltpu.SemaphoreType.DMA((n,)), pltpu.SemaphoreType.DMA((n,)),
               pltpu.SemaphoreType.DMA((3 * n,)), pltpu.SemaphoreType.DMA((3 * n,)),
               pltpu.SemaphoreType.DMA((8,)), pltpu.SemaphoreType.DMA((8,))]),
        compiler_params=pltpu.CompilerParams(dimension_semantics=("arbitrary",), vmem_limit_bytes=VMEM_LIMIT),
    )(*parts, small, dz, dx1, x, mod, n1pre, w_in_t)


def _tril_mask():
    row = lax.broadcasted_iota(jnp.int32, (CHUNK, CHUNK), 0)
    col = lax.broadcasted_iota(jnp.int32, (CHUNK, CHUNK), 1)
    return (col <= row).astype(F32)


def _window_sums(ext):
    s2 = ext + pltpu.roll(ext, 1, 0)
    t4 = s2[:, GROUP:]
    s4 = t4 + pltpu.roll(t4, 2, 0)
    t8 = s4[:, GROUP:]
    s8 = t8 + pltpu.roll(t8, 4, 0)
    t16 = s8[:, GROUP:]
    s16 = t16 + pltpu.roll(t16, 8, 0)
    return [s2[:, :GROUP], s4[:, :GROUP], s8[:, :GROUP], s16]


def _inv_counts(first_pos, rows):
    pos = first_pos + lax.broadcasted_iota(jnp.int32, (rows, 1), 0)
    return [1.0 / jnp.minimum(pos + 1, w).astype(F32) for w in WINDOWS]


def _pool_diff(zb, halo, first_pos):
    tt = zb.shape[0]
    sums = _window_sums(jnp.concatenate([halo, zb], axis=0))
    inv = _inv_counts(first_pos, tt)
    return [sums[g][HALO:, :] * inv[g] - zb[:, g * GROUP:(g + 1) * GROUP] for g in range(len(WINDOWS))]


def _row_blocks(scr, rows, place):
    def block(rel):
        start = pl.multiple_of(rows * _index(_peer(*place, rel)), rows)
        return scr.at[pl.ds(start, rows), :]

    def entries(shard_ref):
        return ([(shard_ref, block(0), rel) for rel in (1, 2, 4, 6)]
                + [(block(rel), block(rel), 1) for rel in (2, 4, 6)])
    return block, entries


def _attn_fwd(tt, x, c_row, w_ada, b_pieces, n1pre, n1post, w_in_shard, w_out_shard, w_sp, bs_rows, ln_g, ln_b,
              w_pool, b_pool, pool_scale, fc_shards, n2pre):
    t_len = x.shape[0]
    nt = t_len // tt
    ncol = w_ada.shape[1]

    def body(x_ref, xb_ref, c_ref, wada_ref, b_ref, n1pre_ref, n1post_ref, wi_ref, wo_ref, wsp_ref, bs_ref,
             lng_ref, lnb_ref, wp_ref, bp_ref, ps_ref, w1_ref, w2_ref, n2pre_ref,
             z_ref, cat_ref, mix_ref, x1_ref, h2_ref, r_ref, f_ref, mod_out, sc_out, e1_ref, e2_ref, wout_ref,
             win_out, carry, land1, land2, sib1, sib2, cat_keep, wout_scr, win_ref, mod_ref, cg, mg, part,
             send_sems, recv_sems, local_sems, wo_send, wo_recv, wi_send, wi_recv, ada_send, ada_recv):
        i = pl.program_id(0)
        place = px, py, pc = _place()
        me = _index(place)
        wo_block, wo_entries = _row_blocks(wout_scr, w_out_shard.shape[0], place)
        wi_block, wi_entries = _row_blocks(win_ref, w_in_shard.shape[0], place)
        wo_copies = _Copies(wo_entries(wo_ref), wo_send, wo_recv)
        wi_copies = _Copies(wi_entries(wi_ref), wi_send, wi_recv)
        wo_keep = pltpu.make_async_copy(wout_scr, wout_ref, local_sems.at[8])
        wi_keep = pltpu.make_async_copy(win_ref, win_out, local_sems.at[9])
        ada = _Copies([(cg.at[me], cg.at[me], k) for k in range(1, N_DEV)]
                      + [(part, mg.at[me], k) for k in range(1, N_DEV)], ada_send, ada_recv)
        copies = _Copies(
            [(w1_ref, sib1, 1), (w2_ref, sib2, 1),
             (w1_ref, land1.at[0], 2), (w2_ref, land2.at[0], 2),
             (w1_ref, land1.at[1], 4), (w2_ref, land2.at[1], 4),
             (land1.at[0], e1_ref.at[3], 1), (land2.at[0], e2_ref.at[3], 1),
             (land1.at[1], e1_ref.at[5], 1), (land2.at[1], e2_ref.at[5], 1)],
            send_sems, recv_sems)
        keep = [pltpu.make_async_copy(w1_ref, e1_ref.at[0], local_sems.at[0]),
                pltpu.make_async_copy(w2_ref, e2_ref.at[0], local_sems.at[1]),
                pltpu.make_async_copy(land1.at[0], e1_ref.at[2], local_sems.at[2]),
                pltpu.make_async_copy(land1.at[1], e1_ref.at[4], local_sems.at[3]),
                pltpu.make_async_copy(land2.at[0], e2_ref.at[2], local_sems.at[4]),
                pltpu.make_async_copy(land2.at[1], e2_ref.at[4], local_sems.at[5]),
                pltpu.make_async_copy(sib1, e1_ref.at[1], local_sems.at[6]),
                pltpu.make_async_copy(sib2, e2_ref.at[1], local_sems.at[7])]

        @pl.when(i == 0)
        def _():
            cg[me] = jnp.broadcast_to(c_ref[...], (8, D))
            ada.start(*range(N_DEV - 1))
            wi_copies.start(0, 1, 2, 3)
            wi_rows, wo_rows = w_in_shard.shape[0], w_out_shard.shape[0]
            win_ref[pl.ds(pl.multiple_of(wi_rows * me, wi_rows), wi_rows), :] = wi_ref[...]
            wout_scr[pl.ds(pl.multiple_of(wo_rows * me, wo_rows), wo_rows), :] = wo_ref[...]
            carry[...] = jnp.zeros_like(carry)

            ada.wait_recv(*range(N_DEV - 1))
            c_all = jnp.concatenate([cg[j, 0:1, :] for j in range(N_DEV)], axis=0)
            sc = c_all * jax.nn.sigmoid(c_all)
            sc_out[...] = sc
            part[...] = _dot(sc.astype(BF16), wada_ref[...].astype(BF16)) + b_ref[me]
            ada.start(*range(N_DEV - 1, 2 * (N_DEV - 1)))
            wo_copies.start(0, 1, 2, 3)
            copies.start(0, 1, 2, 4, 3, 5)
            keep[0].start()
            keep[1].start()
            mg[me] = part[...]
            ada.wait_recv(*range(N_DEV - 1, 2 * (N_DEV - 1)))
            mod_ref[...] = jnp.zeros_like(mod_ref)
            for j in range(N_DEV):
                for m in range(6):
                    lo, hi = max(ncol * j, D * m), min(ncol * (j + 1), D * (m + 1))
                    if lo < hi:
                        mod_ref[m:m + 1, lo - D * m:hi - D * m] = mg[j, pl.ds(me, 1), lo - ncol * j:hi - ncol * j]
            mod_out[...] = mod_ref[...]

            wi_copies.wait_recv(1, 2, 3)
            wi_copies.start(4, 5, 6)
            wi_copies.wait_recv(0, 4, 5, 6)
            wi_keep.start()

        @pl.when(i == nt // 2 + ATTN_LAG)
        def _():
            copies.wait_recv(2, 4)
            copies.start(6, 8)
            keep[2].start()
            keep[3].start()

        @pl.when(i == nt - 1 + ATTN_LAG)
        def _():
            copies.wait_recv(3, 5)
            copies.start(7, 9)
            keep[4].start()
            keep[5].start()

        shift1, scale1, gate1 = mod_ref[0:1, :], mod_ref[1:2, :], mod_ref[2:3, :]

        @pl.when(i < nt)
        def _():
            xv = x_ref[...]
            h1 = (xv * _rstd(xv)) * (n1pre_ref[...] * (1.0 + scale1)) + shift1
            z = _dot_nt(h1.astype(BF16), win_ref[...])
            z_ref[...] = z

            _, ga = _gelu_parts(z[:, :2 * D_A])
            u, vr = ga[:, :D_A], ga[:, D_A:]
            dv = vr - jnp.mean(vr, axis=-1, keepdims=True)
            v = (dv * lax.rsqrt(jnp.mean(dv * dv, axis=-1, keepdims=True) + EPS)) * lng_ref[...] + lnb_ref[...]
            vb = v.astype(BF16)
            mask = _tril_mask()
            wc = [(wsp_ref[h] * mask).astype(BF16) for h in range(N_HEADS)]
            for ch in range(tt // CHUNK):
                rows = slice(ch * CHUNK, (ch + 1) * CHUNK)
                for h in range(N_HEADS):
                    cols = slice(h * GROUP, (h + 1) * GROUP)
                    mixed = _dot(wc[h], vb[rows, cols]) + bs_ref[:, cols]
                    cat_ref[rows, cols] = (u[rows, cols] * mixed).astype(BF16)

            zb = z[:, 2 * D_A:]
            diff = _pool_diff(zb, carry[...], i * tt)
            carry[...] = zb[tt - HALO:, :]
            for g in range(len(WINDOWS)):
                cols = slice(g * GROUP, (g + 1) * GROUP)
                pre = _dot(diff[g].astype(BF16), wp_ref[g].astype(BF16)) + bp_ref[:, cols]
                cat_ref[:, D_A + g * GROUP:D_A + (g + 1) * GROUP] = (pre * ps_ref[:, cols]).astype(BF16)
            cat_keep[i % (ATTN_LAG + 1)] = cat_ref[...]

        @pl.when(i == 0)
        def _():
            copies.wait_recv(0, 1)
            keep[6].start()
            keep[7].start()

        @pl.when(i == 1)
        def _():
            wo_copies.wait_recv(1, 2, 3)
            wo_copies.start(4, 5, 6)

        @pl.when(i == ATTN_LAG)
        def _():
            wo_copies.wait_recv(0, 4, 5, 6)
            wo_keep.start()

        @pl.when(i >= ATTN_LAG)
        def _():
            xv = xb_ref[...]
            mix = _dot(cat_keep[(i - ATTN_LAG) % (ATTN_LAG + 1)], wout_scr[...])
            mix_ref[...] = mix
            x1v = xv + (mix * _rstd(mix)) * (gate1 * n1post_ref[...])
            x1_ref[...] = x1v
            shift2, scale2 = mod_ref[3:4, :], mod_ref[4:5, :]
            h2 = ((x1v * _rstd(x1v)) * (n2pre_ref[...] * (1.0 + scale2)) + shift2).astype(BF16)
            h2_ref[...] = h2
            for j, (w1, w2) in enumerate(((w1_ref, w2_ref), (sib1, sib2))):
                ra = jnp.maximum(_dot(h2, w1[...]), 0.0)
                r = (ra * ra).astype(BF16)
                r_ref[:, j * FF_BLK:(j + 1) * FF_BLK] = r
                if j == 0:
                    f_ref[...] = _dot(r, w2[...])
                else:
                    f_ref[...] += _dot(r, w2[...])

        @pl.when(i == nt - 1 + ATTN_LAG)
        def _():
            copies.wait_recv(6, 7, 8, 9)
            copies.wait_send(*range(10))
            wo_copies.wait_send(*range(7))
            wi_copies.wait_send(*range(7))
            ada.wait_send(*range(2 * (N_DEV - 1)))
            for cp in keep:
                cp.wait()
            wo_keep.wait()
            wi_keep.wait()

    first = lambda w: pl.BlockSpec((tt, w), lambda i: (jnp.minimum(i, nt - 1), 0))
    second = lambda w: pl.BlockSpec((tt, w), lambda i: (jnp.maximum(i - ATTN_LAG, 0), 0))
    r_head = pl.BlockSpec((tt, FC_HEAD * FF_BLK),
                          lambda i: (jnp.maximum(i - ATTN_LAG, 0), R_HEAD_COLS // (FC_HEAD * FF_BLK)))
    hbm = pl.BlockSpec(memory_space=pl.ANY)
    outs = pl.pallas_call(
        body, name="attn_fwd", grid=(nt + ATTN_LAG,),
        out_shape=tuple([jax.ShapeDtypeStruct((t_len, D_Z), F32), jax.ShapeDtypeStruct((t_len, D), BF16),
                         jax.ShapeDtypeStruct((t_len, D), F32), jax.ShapeDtypeStruct((t_len, D), F32),
                         jax.ShapeDtypeStruct((t_len, D), BF16),
                         jax.ShapeDtypeStruct((t_len, FC_EARLY * FF_BLK), BF16),
                         jax.ShapeDtypeStruct((t_len, D), F32)]
                        + [jax.ShapeDtypeStruct((8, D), F32), jax.ShapeDtypeStruct((N_DEV, D), F32)]
                        + [jax.ShapeDtypeStruct((FC_EARLY,) + s.shape, BF16) for s in fc_shards]
                        + [jax.ShapeDtypeStruct((D, D), BF16), jax.ShapeDtypeStruct((D_Z, D), BF16)]),
        in_specs=[first(D), second(D), _full((1, D)), _resident(w_ada.shape), _full((N_DEV, 1, ncol)), _full((1, D)),
                  _full((1, D)), _resident(w_in_shard.shape), _resident(w_out_shard.shape),
                  _full((N_HEADS, CHUNK, CHUNK)), _full((CHUNK, D_A)), _full((1, D_A)), _full((1, D_A)),
                  _full((len(WINDOWS), GROUP, GROUP)), _full((1, D_B)), _full((1, D_B)),
                  _resident(fc_shards[0].shape), _resident(fc_shards[1].shape), _full((1, D))],
        out_specs=(first(D_Z), first(D), second(D), second(D), second(D), r_head, second(D),
                   _full((8, D)), _full((N_DEV, D)), hbm, hbm, hbm, hbm),
        scratch_shapes=[pltpu.VMEM((HALO, D_B), F32),
                        pltpu.VMEM((2,) + fc_shards[0].shape, BF16), pltpu.VMEM((2,) + fc_shards[1].shape, BF16),
                        pltpu.VMEM(fc_shards[0].shape, BF16), pltpu.VMEM(fc_shards[1].shape, BF16),
                        pltpu.VMEM((ATTN_LAG + 1, tt, D), BF16), pltpu.VMEM((D, D), BF16),
                        pltpu.VMEM((D_Z, D), BF16), pltpu.VMEM((8, D), F32),
                        pltpu.VMEM((N_DEV, 8, D), F32), pltpu.VMEM((N_DEV, N_DEV, ncol), F32),
                        pltpu.VMEM((N_DEV, ncol), F32),
                        pltpu.SemaphoreType.DMA((10,)), pltpu.SemaphoreType.DMA((10,)),
                        pltpu.SemaphoreType.DMA((10,)),
                        pltpu.SemaphoreType.DMA((7,)), pltpu.SemaphoreType.DMA((7,)),
                        pltpu.SemaphoreType.DMA((7,)), pltpu.SemaphoreType.DMA((7,)),
                        pltpu.SemaphoreType.DMA((2 * (N_DEV - 1),)), pltpu.SemaphoreType.DMA((2 * (N_DEV - 1),))],
        compiler_params=pltpu.CompilerParams(dimension_semantics=("arbitrary",), vmem_limit_bytes=VMEM_LIMIT),
    )(x, x, c_row, w_ada, b_pieces, n1pre, n1post, w_in_shard, w_out_shard, w_sp, bs_rows, ln_g, ln_b, w_pool, b_pool,
      pool_scale, *fc_shards, n2pre)
    return outs[:9], outs[9:11], outs[11], outs[12]


def _mlp_fwd_early(tt, r_begun, h2, f_head, w1_early, w2_early):
    t_len = h2.shape[0]
    nt = t_len // tt
    n_late = N_DEV - FC_EARLY

    def body(r_begun_ref, h2_ref, fh_ref, w1_ref, w2_ref, r_ref, f_ref, l1_ref, l2_ref,
             land1, land2, send_sems, recv_sems, local_sems):
        i = pl.program_id(0)
        copies = _Copies(
            [(w1_ref.at[2], land1, 4), (w2_ref.at[4], land2, 2),
             (land1, l1_ref.at[1], 1), (land2, l2_ref.at[1], 1)],
            send_sems, recv_sems)
        keep = [pltpu.make_async_copy(land1, l1_ref.at[0], local_sems.at[0]),
                pltpu.make_async_copy(land2, l2_ref.at[0], local_sems.at[1])]

        @pl.when(i == 0)
        def _():
            copies.start(0, 1)

        @pl.when(i == nt - 1)
        def _():
            copies.wait_recv(0, 1)
            copies.start(2, 3)
            for cp in keep:
                cp.start()

        h2 = h2_ref[...]
        f_ref[...] = fh_ref[...]
        for j in range(FC_HEAD, FC_EARLY):
            ra = jnp.maximum(_dot(h2, w1_ref[j]), 0.0)
            r = (ra * ra).astype(BF16)
            r_ref[:, _early_col(j):_early_col(j) + FF_BLK] = r
            f_ref[...] += _dot(r, w2_ref[j])

        @pl.when(i == nt - 1)
        def _():
            copies.wait_recv(2, 3)
            copies.wait_send(0, 1, 2, 3)
            for cp in keep:
                cp.wait()

    tile = lambda w: pl.BlockSpec((tt, w), lambda i: (i, 0))
    hbm = pl.BlockSpec(memory_space=pl.ANY)
    outs = pl.pallas_call(
        body, name="mlp_fwd_early", grid=(nt,),
        out_shape=(jax.ShapeDtypeStruct((t_len, FC_EARLY * FF_BLK), BF16), jax.ShapeDtypeStruct((t_len, D), F32),
                   jax.ShapeDtypeStruct((n_late,) + w1_early.shape[1:], BF16),
                   jax.ShapeDtypeStruct((n_late,) + w2_early.shape[1:], BF16)),
        in_specs=[hbm, tile(D), tile(D), _resident((FC_EARLY, D, FF_BLK)), _resident((FC_EARLY, FF_BLK, D))],
        out_specs=(tile(R_HEAD_COLS), tile(D), hbm, hbm),
        input_output_aliases={0: 0},
        scratch_shapes=[pltpu.VMEM(w1_early.shape[1:], BF16), pltpu.VMEM(w2_early.shape[1:], BF16),
                        pltpu.SemaphoreType.DMA((4,)), pltpu.SemaphoreType.DMA((4,)),
                        pltpu.SemaphoreType.DMA((2,))],
        compiler_params=pltpu.CompilerParams(dimension_semantics=("arbitrary",), vmem_limit_bytes=VMEM_LIMIT),
    )(r_begun, h2, f_head, w1_early, w2_early)
    return outs[:2], outs[2:]


def _mlp_late_bwd(tt, r_early, x1, h2, f_early, tgt, mix, mod, n2pre, n2post, n1post,
                  w1_early, w2_early, w1_late, w2_late):
    t_len = x1.shape[0]
    nt = t_len // tt
    n_late = N_DEV - FC_EARLY
    late_cols = n_late * FF_BLK

    def body(re_ref, x1_ref, h2_ref, fe_ref, tgt_ref, mix_ref, mod_ref, n2pre_ref, n2post_ref,
             n1post_ref, w1e_ref, w2e_ref, w1l_ref, w2l_ref,
             rl_ref, df_ref, da_ref, dmix_ref, dx1_ref, redf_ref, redb_ref, dh2_acc):
        i = pl.program_id(0)

        @pl.when(i == 0)
        def _():
            redf_ref[...] = jnp.zeros_like(redf_ref)
            redb_ref[...] = jnp.zeros_like(redb_ref)

        x1v = x1_ref[...]
        gate1, scale2, gate2 = mod_ref[2:3, :], mod_ref[4:5, :], mod_ref[5:6, :]
        h2 = h2_ref[...]
        f = fe_ref[...]
        for j in range(n_late):
            cols = slice(j * FF_BLK, (j + 1) * FF_BLK)
            ra = jnp.maximum(_dot(h2, w1l_ref[j]), 0.0)
            r = (ra * ra).astype(BF16)
            rl_ref[:, cols] = r
            f = f + _dot(r, w2l_ref[j])
        post2 = n2post_ref[...]
        gate_post2 = gate2 * post2
        rf = _rstd(f)
        fhat = f * rf
        err = (x1v + fhat * gate_post2) - tgt_ref[...]
        dy = err * (1.0 / D)
        d_f, sum_f = _rms_bwd_gained(dy, gate_post2, fhat, rf)
        dfv = d_f.astype(BF16)
        df_ref[...] = dfv
        redf_ref[0:1, :] += post2 * sum_f
        redf_ref[1:2, :] += gate2 * sum_f
        redf_ref[2:3, :] += _colsum(err * err)

        for j in range(N_DEV):
            cols = slice(j * FF_BLK, (j + 1) * FF_BLK)
            if j < FC_EARLY:
                w1, w2, r = w1e_ref[j], w2e_ref[j], re_ref[:, _early_col(j):_early_col(j) + FF_BLK]
            else:
                jl = j - FC_EARLY
                w1, w2, r = w1l_ref[jl], w2l_ref[jl], rl_ref[:, jl * FF_BLK:(jl + 1) * FF_BLK]
            dr = _dot_nt(dfv, w2)
            da = (dr * (2.0 * jnp.sqrt(r.astype(F32)))).astype(BF16)
            da_ref[:, cols] = da
            contrib = _dot_nt(da, w1)
            if j == 0:
                dh2_acc[...] = contrib
            else:
                dh2_acc[...] += contrib
        dh2 = dh2_acc[...]
        pre2, post1 = n2pre_ref[...], n1post_ref[...]
        r2 = _rstd(x1v)
        xhat = x1v * r2
        d_x1, sum_h = _rms_bwd_gained(dh2, pre2 * (1.0 + scale2), xhat, r2)
        dx1 = dy + d_x1
        dx1_ref[...] = dx1
        mixv = mix_ref[...]
        rm = _rstd(mixv)
        mhat = mixv * rm
        d_mix, sum_m = _rms_bwd_gained(dx1, gate1 * post1, mhat, rm)
        dmix_ref[...] = d_mix.astype(BF16)
        redb_ref[0:1, :] += _colsum(dh2)
        redb_ref[1:2, :] += pre2 * sum_h
        redb_ref[2:3, :] += (1.0 + scale2) * sum_h
        redb_ref[3:4, :] += post1 * sum_m
        redb_ref[4:5, :] += gate1 * sum_m

    tile = lambda w: pl.BlockSpec((tt, w), lambda i: (i, 0))
    return pl.pallas_call(
        body, name="mlp_late_bwd", grid=(nt,),
        out_shape=(jax.ShapeDtypeStruct((t_len, late_cols), BF16), jax.ShapeDtypeStruct((t_len, D), BF16),
                   jax.ShapeDtypeStruct((t_len, D_FF), BF16), jax.ShapeDtypeStruct((t_len, D), BF16),
                   jax.ShapeDtypeStruct((t_len, D), F32), jax.ShapeDtypeStruct((8, D), F32),
                   jax.ShapeDtypeStruct((8, D), F32)),
        in_specs=[tile(FC_EARLY * FF_BLK), tile(D), tile(D), tile(D), tile(D),
                  tile(D), _full((8, D)), _full((1, D)), _full((1, D)), _full((1, D)),
                  _resident((FC_EARLY, D, FF_BLK)), _resident((FC_EARLY, FF_BLK, D)),
                  _resident((n_late, D, FF_BLK)), _resident((n_late, FF_BLK, D))],
        out_specs=(tile(late_cols), tile(D), tile(D_FF), tile(D), tile(D), _full((8, D)), _full((8, D))),
        scratch_shapes=[pltpu.VMEM((tt, D), F32)],
        compiler_params=pltpu.CompilerParams(dimension_semantics=("arbitrary",), vmem_limit_bytes=VMEM_LIMIT),
    )(r_early, x1, h2, f_early, tgt, mix, mod, n2pre, n2post, n1post, w1_early, w2_early, w1_late, w2_late)


def _mlp_wgrad(tt, r_early, r_late, da, df, h2):
    t_len = df.shape[0]
    nt = t_len // tt
    odd_steps = [j for j, rel in enumerate(WGRAD_ORDER) if rel % 2]

    def relation(j):
        rel = jnp.int32(WGRAD_ORDER[-1])
        for step in range(N_DEV - 2, -1, -1):
            rel = jnp.where(j == step, WGRAD_ORDER[step], rel)
        return rel

    def body(re_ref, rl_ref, da_ref, df_ref, h2_ref, own1_ref, own2_ref, out1_ref, out2_ref, diag1_ref, diag2_ref,
             acc1, acc2, snd1, snd2, sib1, sib2, dsnd1, dsnd2, send_sems, recv_sems):
        j, t = pl.program_id(0), pl.program_id(1)
        rows = pl.ds(pl.multiple_of(t * tt, tt), tt)
        x, y, c = _place()
        accs, snds, sibs = (acc1, acc2), (snd1, snd2), (sib1, sib2)
        dsnds, diags = (dsnd1, dsnd2), (diag1_ref, diag2_ref)

        def to_sibling(a, jj, buf=0):
            return pltpu.make_async_remote_copy(
                src_ref=snds[a].at[buf], dst_ref=sibs[a].at[jj],
                send_sem=send_sems.at[4 * a + jj], recv_sem=recv_sems.at[4 * a + jj],
                device_id=(x, y, 1 - c), device_id_type=MESH)

        def to_diagonal(a):
            return pltpu.make_async_remote_copy(
                src_ref=dsnds[a], dst_ref=diags[a], send_sem=send_sems.at[8 + a], recv_sem=recv_sems.at[8 + a],
                device_id=_peer(x, y, c, 6), device_id_type=MESH)

        @pl.when(t == 0)
        def _():
            acc2[...] = jnp.zeros_like(acc2)
            acc1[...] = jnp.zeros_like(acc1)

        for r_ref, mine in ((re_ref, relation(j) < FC_EARLY), (rl_ref, relation(j) >= FC_EARLY)):
            @pl.when(mine)
            def _():
                acc2[...] += _dot_tn(r_ref[...], df_ref[rows, :])
                acc1[...] += _dot_tn(h2_ref[rows, :], da_ref[...])

        for step, rel in enumerate(WGRAD_ORDER):
            jj = rel // 2

            @pl.when((t == nt - 1) & (j == step))
            def _():
                for a, (own_ref, out_ref) in enumerate(((own1_ref, out1_ref), (own2_ref, out2_ref))):
                    if rel % 2:
                        q = odd_steps.index(step)
                        if q >= 2:
                            to_sibling(a, WGRAD_ORDER[odd_steps[q - 2]] // 2).wait_send()
                        snds[a][q % 2] = accs[a][...].astype(BF16)
                        to_sibling(a, jj, q % 2).start()
                        continue
                    to_sibling(a, jj).wait_recv()
                    chip_sum = accs[a][...] + sibs[a][jj].astype(F32)
                    if rel == 6:
                        dsnds[a][...] = chip_sum.astype(BF16)
                        to_diagonal(a).start()
                    elif rel == 0:
                        own_ref[...] = chip_sum
                    else:
                        out_ref[0] = chip_sum.astype(BF16)
                    if step == N_DEV - 1:
                        for q in (2, 3):
                            to_sibling(a, WGRAD_ORDER[odd_steps[q]] // 2).wait_send()
                        to_diagonal(a).wait_recv()
                        to_diagonal(a).wait_send()

    assert WGRAD_ORDER[-1] == 0 and WGRAD_ORDER[-3:-1] == (2, 4)
    blk = pl.BlockSpec((tt, FF_BLK), lambda j, t: (t, relation(j)))
    early_block = lambda rel: jnp.where(rel < FC_HEAD, rel + FC_EARLY - FC_HEAD, rel - FC_HEAD)
    early = lambda j, t: (jnp.where(relation(j) < FC_EARLY, t, 0),
                          jnp.where(relation(j) < FC_EARLY, early_block(relation(j)), 0))
    late = lambda j, t: (jnp.where(relation(j) < FC_EARLY, 0, t), jnp.maximum(relation(j) - FC_EARLY, 0))
    chip = lambda j, t: (jnp.clip(j - 5, 0, 1), 0, 0)
    hbm = pl.BlockSpec(memory_space=pl.ANY)
    return pl.pallas_call(
        body, name="mlp_wgrad", grid=(N_DEV, nt),
        out_shape=(jax.ShapeDtypeStruct((D, FF_BLK), F32), jax.ShapeDtypeStruct((FF_BLK, D), F32),
                   jax.ShapeDtypeStruct((2, D, FF_BLK), BF16), jax.ShapeDtypeStruct((2, FF_BLK, D), BF16),
                   jax.ShapeDtypeStruct((D, FF_BLK), BF16), jax.ShapeDtypeStruct((FF_BLK, D), BF16)),
        in_specs=[pl.BlockSpec((tt, FF_BLK), early), pl.BlockSpec((tt, FF_BLK), late), blk,
                  _resident((t_len, D)), _resident((t_len, D))],
        out_specs=(_full((D, FF_BLK)), _full((FF_BLK, D)),
                   pl.BlockSpec((1, D, FF_BLK), chip), pl.BlockSpec((1, FF_BLK, D), chip), hbm, hbm),
        scratch_shapes=[pltpu.VMEM((D, FF_BLK), F32), pltpu.VMEM((FF_BLK, D), F32),
                        pltpu.VMEM((2, D, FF_BLK), BF16), pltpu.VMEM((2, FF_BLK, D), BF16),
                        pltpu.VMEM((4, D, FF_BLK), BF16), pltpu.VMEM((4, FF_BLK, D), BF16),
                        pltpu.VMEM((D, FF_BLK), BF16), pltpu.VMEM((FF_BLK, D), BF16),
                        pltpu.SemaphoreType.DMA((10,)), pltpu.SemaphoreType.DMA((10,))],
        compiler_params=pltpu.CompilerParams(dimension_semantics=("arbitrary", "arbitrary"),
                                             vmem_limit_bytes=VMEM_LIMIT),
    )(r_early, r_late, da, df, h2)


def _acc_rows(ref, row0, k, val):
    half = CHUNK // 2
    ref[row0:row0 + half, k * GROUP:(k + 1) * GROUP] += val[:half, :]
    ref[row0:row0 + half, D_A + k * GROUP:D_A + (k + 1) * GROUP] += val[half:, :]


def _attn_bwd(tt, dmix, x, z, cat, mod, n1pre, w_out, w_sp, bs_rows, ln_g, ln_b, w_pool, b_pool, pool_scale,
              red_fwd, red_bwd, chip_sums):
    t_len = z.shape[0]
    nt = t_len // tt
    hb = tt // HALO
    n_sums = len(chip_sums)

    def body(dmix_ref, x_ref, z_ref, zprev_ref, cat_ref, mod_ref, n1pre_ref, wout_ref, wsp_ref,
             bs_ref, lng_ref, lnb_ref, wp_ref, bp_ref, ps_ref, redf_ref, redb_ref, *rest):
        sum_out = rest[:n_sums]
        dz_ref, gwin_ref, gwout_ref, small_ref = rest[n_sums:n_sums + 4]
        sum_in = rest[n_sums + 4:2 * n_sums + 4]
        carry, acc_in, acc_out, dz_scr, bs_acc, send_sems, recv_sems = rest[2 * n_sums + 4:]
        s = pl.program_id(0)
        i = nt - 1 - s
        px, py, pc = _place()

        def chip_copy(a, r):
            return pltpu.make_async_remote_copy(
                src_ref=sum_out[a].at[r], dst_ref=sum_in[a].at[r],
                send_sem=send_sems.at[2 * a + r], recv_sem=recv_sems.at[2 * a + r],
                device_id=_peer(px, py, pc, 2 * (r + 1)), device_id_type=MESH)

        @pl.when(s == 0)
        def _():
            for a in range(n_sums):
                for r in range(2):
                    chip_copy(a, r).start()
            carry[...] = jnp.zeros_like(carry)
            acc_in[...] = jnp.zeros_like(acc_in)
            acc_out[...] = jnp.zeros_like(acc_out)
            bs_acc[...] = jnp.zeros_like(bs_acc)
            small_ref[...] = jnp.zeros_like(small_ref)
            small_ref[ROW_DMOD + 2:ROW_DMOD + 3, :] = redb_ref[3:4, :]
            small_ref[ROW_DMOD + 3:ROW_DMOD + 5, :] = redb_ref[0:2, :]
            small_ref[ROW_DMOD + 5:ROW_DMOD + 6, :] = redf_ref[0:1, :]
            small_ref[ROW_N1POST:ROW_N1POST + 1, :] = redb_ref[4:5, :]
            small_ref[ROW_N2PRE:ROW_N2PRE + 1, :] = redb_ref[2:3, :]
            small_ref[ROW_N2POST:ROW_N2POST + 1, :] = redf_ref[1:2, :]
            small_ref[ROW_LOSS:ROW_LOSS + 1, :] = redf_ref[2:3, :]

        dmixv = dmix_ref[...]
        dcat = _dot_nt(dmixv, wout_ref[...])
        acc_out[...] += _dot_tn(cat_ref[...], dmixv)

        z = z_ref[...]
        t_g, ga = _gelu_parts(z[:, :2 * D_A])
        u, vr = ga[:, :D_A], ga[:, D_A:]
        dv0 = vr - jnp.mean(vr, axis=-1, keepdims=True)
        rv = lax.rsqrt(jnp.mean(dv0 * dv0, axis=-1, keepdims=True) + EPS)
        vhat = dv0 * rv
        vb = (vhat * lng_ref[...] + lnb_ref[...]).astype(BF16)
        mask = _tril_mask()
        wc = [(wsp_ref[h] * mask).astype(BF16) for h in range(N_HEADS)]

        dya = dcat[:, :D_A]
        for h in range(N_HEADS):
            cols = slice(h * GROUP, (h + 1) * GROUP)
            bs_sum = jnp.zeros((CHUNK, GROUP), F32)
            ws_sum = jnp.zeros((CHUNK, CHUNK), F32)
            for ch in range(tt // CHUNK):
                rows = slice(ch * CHUNK, (ch + 1) * CHUNK)
                v_ch = vb[rows, cols]
                mixed = _dot(wc[h], v_ch) + bs_ref[:, cols]
                dy_ch = dya[rows, cols]
                dz_scr[rows, cols] = dy_ch * mixed
                dmixed = dy_ch * u[rows, cols]
                dmb = dmixed.astype(BF16)
                dz_scr[rows, D_A + h * GROUP:D_A + (h + 1) * GROUP] = _dot_tn(wc[h], dmb)
                bs_sum = bs_sum + dmixed
                ws_sum = ws_sum + _dot_nt(dmb, v_ch)
            _acc_rows(bs_acc, 0, h, bs_sum)
            _acc_rows(small_ref, ROW_WS, h, ws_sum)

        dvl = dz_scr[:, D_A:2 * D_A]
        dvhat = dvl * lng_ref[...]
        dvl_vhat = dvl * vhat
        dvr = rv * (dvhat - jnp.mean(dvhat, axis=-1, keepdims=True)
                    - vhat * jnp.mean(dvl_vhat * lng_ref[...], axis=-1, keepdims=True))
        small_ref[ROW_LN:ROW_LN + 1, 0:D_A] += _colsum(dvl_vhat)
        small_ref[ROW_LN:ROW_LN + 1, D_A:D] += _colsum(dvl)
        dga = jnp.concatenate([dz_scr[:, :D_A], dvr], axis=1)
        dza = dga * _gelu_grad(z[:, :2 * D_A], t_g)

        zb = z[:, 2 * D_A:]
        halo_prev = jnp.where(i == 0, 0.0, zprev_ref[...])
        diff = _pool_diff(zb, halo_prev, i * tt)
        dyb = dcat[:, D_A:]
        inv = _inv_counts(i * tt, tt)
        scaled, ddiffs = [], []
        for g in range(len(WINDOWS)):
            cols = slice(g * GROUP, (g + 1) * GROUP)
            db = diff[g].astype(BF16)
            wpg = wp_ref[g].astype(BF16)
            pre = _dot(db, wpg) + bp_ref[:, cols]
            small_ref[ROW_POOL:ROW_POOL + 1, cols] += _colsum(dyb[:, cols] * pre)
            dpre = dyb[:, cols] * ps_ref[:, cols]
            small_ref[ROW_POOL:ROW_POOL + 1, D_B + g * GROUP:D_B + (g + 1) * GROUP] += _colsum(dpre)
            dpb = dpre.astype(BF16)
            _acc_rows(small_ref, ROW_WP, g, _dot_tn(db, dpb))
            ddiff = _dot_nt(dpb, wpg)
            ddiffs.append(ddiff)
            scaled.append(ddiff * inv[g])
        scaled_all = jnp.concatenate(scaled, axis=1)
        ext = jnp.concatenate([scaled_all, carry[...]], axis=0)
        n_ext = tt + HALO
        s2 = ext + pltpu.roll(ext, n_ext - 1, 0)
        t4 = s2[:, GROUP:]
        s4 = t4 + pltpu.roll(t4, n_ext - 2, 0)
        t8 = s4[:, GROUP:]
        s8 = t8 + pltpu.roll(t8, n_ext - 4, 0)
        t16 = s8[:, GROUP:]
        s16 = t16 + pltpu.roll(t16, n_ext - 8, 0)
        back = [s2[:, :GROUP], s4[:, :GROUP], s8[:, :GROUP], s16]
        carry[...] = scaled_all[:HALO, :]
        dzb = jnp.concatenate([back[g][:tt, :] - ddiffs[g] for g in range(len(WINDOWS))], axis=1)

        dzv = jnp.concatenate([dza, dzb], axis=1).astype(BF16)
        dz_ref[...] = dzv
        xv = x_ref[...]
        h1 = (xv * _rstd(xv) * (n1pre_ref[...] * (1.0 + mod_ref[1:2, :])) + mod_ref[0:1, :]).astype(BF16)
        acc_in[...] += _dot_tn(dzv, h1)

        @pl.when(s == nt - 1)
        def _():
            gwin_ref[...] = acc_in[...].astype(BF16)
            gwout_ref[...] = acc_out[...].astype(BF16)
            bs = _unfold(bs_acc[...])
            for h in range(N_HEADS):
                small_ref[ROW_BS + h:ROW_BS + h + 1, 0:GROUP] = jnp.sum(
                    bs[:, h * GROUP:(h + 1) * GROUP].T, axis=0, keepdims=True)
            for a in range(n_sums):
                for r in range(2):
                    chip_copy(a, r).wait_recv()
                    chip_copy(a, r).wait_send()

    rev = lambda w: pl.BlockSpec((tt, w), lambda s: (nt - 1 - s, 0))
    zprev = pl.BlockSpec((HALO, D_B), lambda s: (jnp.maximum((nt - 1 - s) * hb - 1, 0), 2))
    hbm = pl.BlockSpec(memory_space=pl.ANY)
    outs = pl.pallas_call(
        body, name="attn_bwd", grid=(nt,),
        out_shape=tuple([jax.ShapeDtypeStruct((t_len, D_Z), BF16), jax.ShapeDtypeStruct((D_Z, D), BF16),
                         jax.ShapeDtypeStruct((D, D), BF16), jax.ShapeDtypeStruct((SMALL_ROWS, D), F32)]
                        + [jax.ShapeDtypeStruct(cs.shape, cs.dtype) for cs in chip_sums]),
        in_specs=[rev(D), rev(D), rev(D_Z), zprev, rev(D), _full((8, D)), _full((1, D)),
                  _resident((D, D)), _full((N_HEADS, CHUNK, CHUNK)), _full((CHUNK, D_A)),
                  _full((1, D_A)), _full((1, D_A)), _full((len(WINDOWS), GROUP, GROUP)), _full((1, D_B)),
                  _full((1, D_B)), _full((8, D)), _full((8, D))] + [_resident(cs.shape) for cs in chip_sums],
        out_specs=tuple([rev(D_Z), _resident((D_Z, D)), _resident((D, D)), _full((SMALL_ROWS, D))]
                        + [hbm] * n_sums),
        scratch_shapes=[pltpu.VMEM((HALO, D_B), F32), pltpu.VMEM((D_Z, D), F32), pltpu.VMEM((D, D), F32),
                        pltpu.VMEM((tt, 2 * D_A), F32), pltpu.VMEM((CHUNK // 2, D), F32),
                        pltpu.SemaphoreType.DMA((2 * n_sums,)), pltpu.SemaphoreType.DMA((2 * n_sums,))],
        compiler_params=pltpu.CompilerParams(dimension_semantics=("arbitrary",), vmem_limit_bytes=VMEM_LIMIT),
    )(dmix, x, z, z, cat, mod, n1pre, w_out, w_sp, bs_rows, ln_g, ln_b, w_pool, b_pool, pool_scale,
      red_fwd, red_bwd, *chip_sums)
    return outs[:4], outs[4:]


def _adam(w, g, m, v):
    m2 = ADAM_B1 * m + (1.0 - ADAM_B1) * g
    v2 = ADAM_B2 * v + (1.0 - ADAM_B2) * (g * g)
    m_hat = m2 / (1.0 - ADAM_B1 ** ADAM_STEP)
    v_hat = v2 / (1.0 - ADAM_B2 ** ADAM_STEP)
    delta = -ADAM_LR * (m_hat / (jnp.sqrt(v_hat) + ADAM_EPS) + ADAM_WD * w)
    return delta, m2, v2


def _adamw_large(steps, fc, ada):
    n_fc = len(fc)
    w_ada, sc, total, m_ada, v_ada = ada
    ada_rows, cols = w_ada.shape

    def body(*refs):
        ins, outs = refs[:6 * n_fc], refs[6 * n_fc + 5:-1]
        for k in range(n_fc):
            w_ref, own_ref, arr_ref, diag_ref, m_ref, v_ref = ins[6 * k:6 * k + 6]
            g = ((own_ref[...] + arr_ref[0].astype(F32)) + arr_ref[1].astype(F32)) + diag_ref[...].astype(F32)
            outs[4 * k][...] = g
            outs[4 * k + 1][...], outs[4 * k + 2][...], outs[4 * k + 3][...] = _adam(
                w_ref[...], g, m_ref[...], v_ref[...])

        w_ref, sc_ref, t_ref, m_ref, v_ref = refs[6 * n_fc:6 * n_fc + 5]
        g_ref, d_ref, m2_ref, v2_ref = outs[4 * n_fc:]
        dm = refs[-1]
        me = _index(_place())
        for dev in range(N_DEV):
            @pl.when((pl.program_id(0) == 0) & (me == dev))
            def _():
                for b in range(N_DEV):
                    for k in range(6):
                        lo, hi = max(cols * dev, D * k), min(cols * (dev + 1), D * (k + 1))
                        if lo < hi:
                            dm[b:b + 1, lo - cols * dev:hi - cols * dev] = t_ref[
                                _table_row(b) + k:_table_row(b) + k + 1, lo - D * k:hi - D * k]

        g = _dot_tn(sc_ref[...].astype(BF16), dm[...].astype(BF16))
        g_ref[...] = g
        d_ref[...], m2_ref[...], v2_ref[...] = _adam(w_ref[...], g, m_ref[...], v_ref[...])

    specs_in, specs_out, shapes, args = [], [], [], []
    for w, own, arrived, diagonal, m, v in fc:
        rows, width = w.shape
        blk = pl.BlockSpec((rows // steps, width), lambda i: (i, 0))
        specs_in += [blk, blk, pl.BlockSpec((2, rows // steps, width), lambda i: (0, i, 0)), blk, blk, blk]
        specs_out += [blk] * 4
        shapes += [jax.ShapeDtypeStruct((rows, width), F32)] * 4
        args += [w, own, arrived, diagonal, m, v]
    rb = ada_rows // steps
    blk = pl.BlockSpec((rb, cols), lambda i: (i, 0))
    specs_in += [blk, pl.BlockSpec((N_DEV, rb), lambda i: (0, i)), _full(total.shape), blk, blk]
    specs_out += [blk] * 4
    shapes += [jax.ShapeDtypeStruct((ada_rows, cols), F32)] * 4
    outs = pl.pallas_call(
        body, name="adamw_large", grid=(steps,), out_shape=tuple(shapes), in_specs=specs_in,
        out_specs=tuple(specs_out), scratch_shapes=[pltpu.VMEM((N_DEV, cols), F32)],
        compiler_params=pltpu.CompilerParams(dimension_semantics=("arbitrary",), vmem_limit_bytes=VMEM_LIMIT),
    )(*args, w_ada, sc, total, m_ada, v_ada)
    return [outs[4 * k:4 * k + 4] for k in range(n_fc + 1)]


def _unfold(acc_rows):
    return jnp.concatenate([acc_rows[:, :D_A], acc_rows[:, D_A:]], axis=0)


def _adamw_small(total, params, shards):
    n = len(params)
    flat = [a for p in params for a in p]

    def body(*refs):
        s_ref = refs[0]
        p_refs = refs[1:1 + 3 * n]
        s_refs = refs[1 + 3 * n:1 + 3 * n + 4 * len(shards)]
        loss_ref = refs[1 + 3 * n + 4 * len(shards)]
        o_refs = refs[2 + 3 * n + 4 * len(shards):2 + 3 * n + 4 * len(shards) + 4 * n]
        so_refs = refs[2 + 3 * n + 4 * len(shards) + 4 * n:]
        d_b_ada = s_ref[0:6, :]
        for b in range(1, N_DEV):
            d_b_ada = d_b_ada + s_ref[_table_row(b):_table_row(b) + 6, :]
        misc = lambda r: s_ref[PK_MISC + r - ROW_N1PRE:PK_MISC + r - ROW_N1PRE + 1, :]
        loss = jnp.sum(misc(ROW_LOSS), axis=-1, keepdims=True) * (0.5 / D)
        loss_ref[...] = loss
        mask = _tril_mask()
        ws = _unfold(s_ref[PK_WS:PK_WS + 64, :])
        wp = _unfold(s_ref[PK_WP:PK_WP + 64, :])
        grads = [
            d_b_ada,
            misc(ROW_N1PRE), misc(ROW_N1POST), misc(ROW_N2PRE), misc(ROW_N2POST),
            misc(ROW_LN)[:, :D_A], misc(ROW_LN)[:, D_A:],
            misc(ROW_POOL)[:, :D_B], misc(ROW_POOL)[:, D_B:],
            s_ref[PK_BS:PK_BS + N_HEADS, 0:GROUP],
            jnp.stack([ws[:, h * GROUP:(h + 1) * GROUP] * mask for h in range(N_HEADS)]),
            jnp.stack([wp[:, g * GROUP:(g + 1) * GROUP] for g in range(len(WINDOWS))]),
        ]
        for k in range(n):
            w_ref, m_ref, v_ref = p_refs[3 * k:3 * k + 3]
            g = grads[k]
            if k == 0:
                for j in range(6):
                    o_refs[0][j] = g[j:j + 1, :]
                    o_refs[1][j], o_refs[2][j], o_refs[3][j] = _adam(w_ref[j], g[j:j + 1, :], m_ref[j], v_ref[j])
                continue
            o_refs[4 * k][...] = g
            o_refs[4 * k + 1][...], o_refs[4 * k + 2][...], o_refs[4 * k + 3][...] = _adam(
                w_ref[...], g, m_ref[...], v_ref[...])
        for k in range(len(shards)):
            w_ref, g_ref, m_ref, v_ref = s_refs[4 * k:4 * k + 4]
            so_refs[3 * k][...], so_refs[3 * k + 1][...], so_refs[3 * k + 2][...] = _adam(
                w_ref[...], g_ref[...], m_ref[...], v_ref[...])

    vm = pl.BlockSpec(memory_space=pltpu.VMEM)
    out_shape = [jax.ShapeDtypeStruct((1, 1), F32)]
    for w, _, _ in params:
        out_shape += [jax.ShapeDtypeStruct(w.shape, F32)] * 4
    for w, _, _, _ in shards:
        out_shape += [jax.ShapeDtypeStruct(w.shape, F32)] * 3
    return pl.pallas_call(
        body, name="adamw_small", out_shape=tuple(out_shape),
        in_specs=[vm] * (1 + 3 * n + 4 * len(shards)), out_specs=tuple([vm] * len(out_shape)),
        compiler_params=pltpu.CompilerParams(vmem_limit_bytes=VMEM_LIMIT),
    )(total, *flat, *[a for s in shards for a in s])


TT_ATTN_FWD = 512
ATTN_LAG = 2
TT_MLP_FWD = 512
TT_MLP = 256
TT_WGRAD = 2048
TT_ATTN_BWD = 512
TT_TAIL = 512


def kernel(x, c, w_ada, b_ada, norm1_pre, norm1_post, w_in, w_spatial, b_spatial, ln_v_gain, ln_v_bias, w_pool, b_pool, pool_scale, w_out, norm2_pre, norm2_post, w_fc1, w_fc2, loss_target, m_w_ada, m_b_ada, m_norm1_pre, m_norm1_post, m_w_in, m_w_spatial, m_b_spatial, m_ln_v_gain, m_ln_v_bias, m_w_pool, m_b_pool, m_pool_scale, m_w_out, m_norm2_pre, m_norm2_post, m_w_fc1, m_w_fc2, v_w_ada, v_b_ada, v_norm1_pre, v_norm1_post, v_w_in, v_w_spatial, v_b_spatial, v_ln_v_gain, v_ln_v_bias, v_w_pool, v_b_pool, v_pool_scale, v_w_out, v_norm2_pre, v_norm2_post, v_w_fc1, v_w_fc2):
    t_len = x.shape[1]
    ada_cols = w_ada.shape[1]
    tt = lambda want: min(want, t_len)

    x2 = x.reshape(t_len, D)
    tgt = loss_target.reshape(t_len, D)
    row = lambda a: a.reshape(1, -1)

    w_in_shard, w_out_shard, w1_shard, w2_shard = _cast_shards([w_in.T, w_out, w_fc1, w_fc2])

    bs_rows = jnp.repeat(b_spatial.T, GROUP, axis=1)
    attn_consts = (w_spatial, bs_rows, row(ln_v_gain), row(ln_v_bias), w_pool, row(b_pool), row(pool_scale))

    (z, cat, mix, x1, h2, r_begun, f_head, mod, sc), (w1_early, w2_early), w_out_all, w_in_t = _attn_fwd(
        tt(TT_ATTN_FWD), x2, c.reshape(1, D), w_ada, b_ada.reshape(N_DEV, 1, ada_cols), row(norm1_pre),
        row(norm1_post),
        w_in_shard, w_out_shard, *attn_consts, (w1_shard, w2_shard), row(norm2_pre))
    (r_early, f_early), (w1_late, w2_late) = _mlp_fwd_early(
        tt(TT_MLP_FWD), r_begun, h2, f_head, w1_early, w2_early)
    r_late, df, da, dmix, dx1, red_fwd, red_bwd = _mlp_late_bwd(
        tt(TT_MLP), r_early, x1, h2, f_early, tgt, mix, mod, row(norm2_pre), row(norm2_post), row(norm1_post),
        w1_early, w2_early, w1_late, w2_late)
    own_w1, own_w2, sums_w1, sums_w2, diag_w1, diag_w2 = _mlp_wgrad(tt(TT_WGRAD), r_early, r_late, da, df, h2)
    (dz, p_in, p_out, small), (arr_w1, arr_w2) = _attn_bwd(
        tt(TT_ATTN_BWD), dmix, x2, z, cat, mod, row(norm1_pre), w_out_all, *attn_consts, red_fwd, red_bwd,
        [sums_w1, sums_w2])
    grad_in_t, grad_out, total, grad_x = _tail_comm(
        [p_in.reshape(N_DEV, D_Z // N_DEV, D), p_out.reshape(N_DEV, D // N_DEV, D)], small, 64,
        tt(TT_TAIL), dz, dx1, x2, mod, row(norm1_pre), w_in_t)
    (grad_w1, d_w1, m_w1, v_w1), (grad_w2, d_w2, m_w2, v_w2), (grad_ada, d_ada, m_ada, v_ada) = _adamw_large(
        4, [(w_fc1, own_w1, arr_w1, diag_w1, m_w_fc1, v_w_fc1), (w_fc2, own_w2, arr_w2, diag_w2, m_w_fc2, v_w_fc2)],
        (w_ada, sc, total, m_w_ada, v_w_ada))

    six = lambda a: a.reshape(6, 1, D)
    small_params = [
        (six(b_ada), six(m_b_ada), six(v_b_ada)),
        (row(norm1_pre), row(m_norm1_pre), row(v_norm1_pre)),
        (row(norm1_post), row(m_norm1_post), row(v_norm1_post)),
        (row(norm2_pre), row(m_norm2_pre), row(v_norm2_pre)),
        (row(norm2_post), row(m_norm2_post), row(v_norm2_post)),
        (row(ln_v_gain), row(m_ln_v_gain), row(v_ln_v_gain)),
        (row(ln_v_bias), row(m_ln_v_bias), row(v_ln_v_bias)),
        (row(pool_scale), row(m_pool_scale), row(v_pool_scale)),
        (row(b_pool), row(m_b_pool), row(v_b_pool)),
        (b_spatial, m_b_spatial, v_b_spatial),
        (w_spatial, m_w_spatial, v_w_spatial),
        (w_pool, m_w_pool, v_w_pool),
    ]
    outs = _adamw_small(total, small_params, [(w_out, grad_out, m_w_out, v_w_out),
                                              (w_in.T, grad_in_t, m_w_in.T, v_w_in.T)])
    d_out, m_out, v_out, d_in_t, m_in_t, v_in_t = outs[1 + 4 * len(small_params):]
    loss = outs[0].reshape(())
    names = ["b_ada", "norm1_pre", "norm1_post", "norm2_pre", "norm2_post", "ln_v_gain", "ln_v_bias", "pool_scale",
             "b_pool", "b_spatial", "w_spatial", "w_pool"]
    shapes = dict(b_ada=b_ada.shape, norm1_pre=norm1_pre.shape, norm1_post=norm1_post.shape,
                  norm2_pre=norm2_pre.shape, norm2_post=norm2_post.shape, ln_v_gain=ln_v_gain.shape,
                  ln_v_bias=ln_v_bias.shape, pool_scale=pool_scale.shape, b_pool=b_pool.shape,
                  b_spatial=b_spatial.shape, w_spatial=w_spatial.shape, w_pool=w_pool.shape)
    res = {}
    for k, nm in enumerate(names):
        res[nm] = tuple(o.reshape(shapes[nm]) for o in outs[1 + 4 * k:5 + 4 * k])
    res["w_ada"] = (grad_ada, d_ada, m_ada, v_ada)
    res["w_in"] = (grad_in_t.T, d_in_t.T, m_in_t.T, v_in_t.T)
    res["w_out"] = (grad_out, d_out, m_out, v_out)
    res["w_fc1"] = (grad_w1, d_w1, m_w1, v_w1)
    res["w_fc2"] = (grad_w2, d_w2, m_w2, v_w2)

    order = ["w_ada", "b_ada", "norm1_pre", "norm1_post", "w_in", "w_spatial", "b_spatial", "ln_v_gain", "ln_v_bias",
             "w_pool", "b_pool", "pool_scale", "w_out", "norm2_pre", "norm2_post", "w_fc1", "w_fc2"]
    return (loss, grad_x.reshape(x.shape),
            *[res[nm][0] for nm in order], *[res[nm][1] for nm in order],
            *[res[nm][2] for nm in order], *[res[nm][3] for nm in order])
```

```python
import functools

import jax
import jax.numpy as jnp
from jax import lax
from jax.experimental import pallas as pl
from jax.experimental.pallas import tpu as pltpu

F32 = jnp.float32
BF16 = jnp.bfloat16
MESH = pl.DeviceIdType.MESH

N_DEV = 8
D = 1024
D_A = 512
D_B = 512
D_Z = 2 * D_A + D_B
N_HEADS = 4
CHUNK = 128
WINDOWS = (2, 4, 8, 16)
GROUP = 128
D_FF = 4096
FF_BLK = D_FF // N_DEV
HALO = 16
EPS = 1e-6
VMEM_LIMIT = 60 * 1024 * 1024

ADAM_LR = 0.001
ADAM_B1 = 0.9
ADAM_B2 = 0.999
ADAM_EPS = 1e-08
ADAM_WD = 0.01
ADAM_STEP = 10

ROW_DMOD = 0
ROW_N1PRE, ROW_N1POST, ROW_N2PRE, ROW_N2POST = 8, 9, 10, 11
ROW_LN = 12
ROW_POOL = 13
ROW_LOSS = 14
ROW_BS = 16
ROW_WS = 24
ROW_WP = 88
SMALL_ROWS = 152
PACK_FINE = 40
PACK_HALF = PACK_FINE + 64
PACK_ROWS = 2 * PACK_HALF
PK_WS = PACK_FINE
PK_TABLE_B = PACK_HALF
PK_MISC = PK_TABLE_B + 24
PK_BS = PK_MISC + 8
PK_WP = PK_BS + 8


def _table_row(b):
    if isinstance(b, int):
        return 8 * b if 8 * b < PACK_FINE else 8 * b + PK_TABLE_B - PACK_FINE
    return 8 * b + jnp.where(8 * b < PACK_FINE, 0, PK_TABLE_B - PACK_FINE)


def _dot(a, b):
    return jnp.dot(a, b, preferred_element_type=F32)


def _dot_nt(a, b):
    return lax.dot_general(a, b, (((1,), (1,)), ((), ())), preferred_element_type=F32)


def _dot_tn(a, b):
    return lax.dot_general(a, b, (((0,), (0,)), ((), ())), preferred_element_type=F32)


def _rstd(v):
    return lax.rsqrt(jnp.mean(v * v, axis=-1, keepdims=True) + EPS)


def _rms_bwd(d_hat, hat, rstd):
    return rstd * (d_hat - hat * jnp.mean(d_hat * hat, axis=-1, keepdims=True))


def _rms_bwd_gained(g, gain, hat, rstd):
    g_hat = g * hat
    d_v = rstd * (g * gain - hat * jnp.mean(g_hat * gain, axis=-1, keepdims=True))
    return d_v, _colsum(g_hat)


_K0 = 0.7978845608028654
_K1 = 0.044715


def _gelu_parts(v):
    t = jnp.tanh(v * (_K0 + (_K0 * _K1) * (v * v)))
    return t, v * (0.5 + 0.5 * t)


def _gelu_grad(v, t):
    return (0.5 + 0.5 * t) + (0.5 * v) * (1.0 - t * t) * (_K0 + (3.0 * _K0 * _K1) * (v * v))


def _colsum(v):
    return jnp.sum(v, axis=0, keepdims=True)


def _full(shape):
    n = len(shape)
    return pl.BlockSpec(shape, lambda *_: (0,) * n)


def _resident(shape):
    n = len(shape)
    return pl.BlockSpec(shape, lambda *_: (0,) * n, pipeline_mode=pl.Buffered(1))


def _place():
    x, y, c = lax.axis_index("x"), lax.axis_index("y"), lax.axis_index("c")
    return x, y, c


def _flip(v, bit):
    return 1 - v if bit else v


def _peer(x, y, c, k):
    return (_flip(x, (k >> 2) & 1), _flip(y, (k >> 1) & 1), _flip(c, k & 1))


def _index(p):
    return 4 * p[0] + 2 * p[1] + p[2]


def _cast_shards(shards):
    def body(*refs):
        for src, dst in zip(refs[:len(shards)], refs[len(shards):]):
            dst[...] = src[...].astype(BF16)

    vm = pl.BlockSpec(memory_space=pltpu.VMEM)
    return pl.pallas_call(
        body, name="cast_shards", out_shape=tuple(jax.ShapeDtypeStruct(s.shape, BF16) for s in shards),
        in_specs=[vm] * len(shards), out_specs=tuple([vm] * len(shards)),
    )(*shards)


FC_EARLY = 6
FC_HEAD = 2
R_HEAD_COLS = (FC_EARLY - FC_HEAD) * FF_BLK
WGRAD_ORDER = (7, 6, 1, 3, 5, 2, 4, 0)


def _early_col(j):
    return R_HEAD_COLS + j * FF_BLK if j < FC_HEAD else (j - FC_HEAD) * FF_BLK


class _Copies:
    def __init__(self, entries, send_sems, recv_sems):
        self.place = _place()
        self.entries, self.send_sems, self.recv_sems = entries, send_sems, recv_sems

    def _copy(self, i, arrival=False):
        src, dst, rel = self.entries[i]
        return pltpu.make_async_remote_copy(
            src_ref=dst if arrival else src, dst_ref=dst, send_sem=self.send_sems.at[i],
            recv_sem=self.recv_sems.at[i], device_id=_peer(*self.place, rel), device_id_type=MESH)

    def start(self, *which):
        for i in which:
            self._copy(i).start()

    def wait_recv(self, *which):
        for i in which:
            self._copy(i, arrival=True).wait_recv()

    def wait_send(self, *which):
        for i in which:
            self._copy(i).wait_send()


def _tail_comm(parts, small, row_chunk, tt, dz, dx1, x, mod, n1pre, w_in_t):
    n = len(parts)
    t_len = x.shape[0]
    nt = t_len // tt

    def body(*refs):
        p_refs, small_ref = refs[:n], refs[n]
        dz_ref, dx1_ref, x_ref, mod_ref, n1pre_ref, win_ref = refs[n + 1:n + 7]
        outs = refs[n + 7:]
        g_refs, total_ref, gx_ref = outs[:n], outs[n], outs[n + 1]
        scr = outs[n + 2:]
        from_sib = scr[0:n]
        chip_out = scr[n:2 * n]
        chip_in = scr[2 * n:3 * n]
        pack, pack_sib, fine, bulk, total_scr, head = scr[3 * n:3 * n + 6]
        send_a, recv_a, send_b, recv_b, send_s, recv_s = scr[3 * n + 6:]
        step = pl.program_id(0)
        x, y, c = _place()
        me = _index((x, y, c))
        sibling = (x, y, 1 - c)
        my_chip = 2 * x + y
        others = [(1 - x, y), (x, 1 - y), (1 - x, 1 - y)]
        my_half = pl.ds(pl.multiple_of(PACK_HALF * c, 8), PACK_HALF)

        def pack_to_sibling():
            return pltpu.make_async_remote_copy(
                src_ref=pack, dst_ref=pack_sib, send_sem=send_s.at[0], recv_sem=recv_s.at[0],
                device_id=sibling, device_id_type=MESH)

        def half_to_chip(r, part):
            buf = (fine, bulk)[part]
            return pltpu.make_async_remote_copy(
                src_ref=buf.at[my_chip], dst_ref=buf.at[my_chip],
                send_sem=send_s.at[1 + 3 * part + r], recv_sem=recv_s.at[1 + 3 * part + r],
                device_id=(*others[r], c), device_id_type=MESH)

        def half_from_chip(r, part):
            k = 2 * others[r][0] + others[r][1]
            buf = (fine, bulk)[part]
            return pltpu.make_async_remote_copy(
                src_ref=buf.at[k], dst_ref=buf.at[k],
                send_sem=send_s.at[1 + 3 * part + r], recv_sem=recv_s.at[1 + 3 * part + r],
                device_id=(*others[r], c), device_id_type=MESH)

        def total_to_sibling():
            return pltpu.make_async_remote_copy(
                src_ref=total_scr.at[my_half], dst_ref=total_scr.at[my_half],
                send_sem=send_s.at[7], recv_sem=recv_s.at[7], device_id=sibling, device_id_type=MESH)

        def total_from_sibling():
            sib_half = pl.ds(pl.multiple_of(PACK_HALF * (1 - c), 8), PACK_HALF)
            return pltpu.make_async_remote_copy(
                src_ref=total_scr.at[sib_half], dst_ref=total_scr.at[sib_half],
                send_sem=send_s.at[7], recv_sem=recv_s.at[7], device_id=sibling, device_id_type=MESH)

        def to_sibling(a, k):
            return pltpu.make_async_remote_copy(
                src_ref=p_refs[a].at[2 * k + (1 - c)], dst_ref=from_sib[a].at[k],
                send_sem=send_a.at[a], recv_sem=recv_a.at[a], device_id=sibling, device_id_type=MESH)

        def all_from_sibling(a):
            return pltpu.make_async_remote_copy(
                src_ref=from_sib[a], dst_ref=from_sib[a], send_sem=send_a.at[a], recv_sem=recv_a.at[a],
                device_id=sibling, device_id_type=MESH)

        def to_chip(a, r):
            return pltpu.make_async_remote_copy(
                src_ref=chip_out[a].at[r], dst_ref=chip_in[a].at[r],
                send_sem=send_b.at[3 * a + r], recv_sem=recv_b.at[3 * a + r],
                device_id=(*others[r], c), device_id_type=MESH)

        @pl.when(step == 0)
        def _():
            head[...] = jnp.zeros_like(head)
            for a in range(n):
                for k in range(4):
                    to_sibling(a, k).start()

        @pl.when(step == min(1, nt - 1))
        def _():
            for a in range(n):
                all_from_sibling(a).wait_recv()
                rows = p_refs[a].shape[1]
                for r in range(3):
                    k = 2 * others[r][0] + others[r][1]
                    for s in range(0, rows, row_chunk):
                        sl = pl.ds(s, row_chunk)
                        chip_out[a][r, sl, :] = (p_refs[a][2 * k + c, sl, :].astype(F32)
                                                 + from_sib[a][k, sl, :].astype(F32)).astype(BF16)
                    to_chip(a, r).start()
                for s in range(0, rows, row_chunk):
                    sl = pl.ds(s, row_chunk)
                    g_refs[a][sl, :] = (p_refs[a][2 * my_chip + c, sl, :].astype(F32)
                                        + from_sib[a][my_chip, sl, :].astype(F32))

        dh1 = _dot(dz_ref[...], win_ref[...])
        xv = x_ref[...]
        r1 = _rstd(xv)
        xhat = xv * r1
        scale1 = mod_ref[1:2, :]
        pre1 = n1pre_ref[...]
        d_x, sum_h = _rms_bwd_gained(dh1, pre1 * (1.0 + scale1), xhat, r1)
        gx_ref[...] = dx1_ref[...] + d_x
        head[ROW_DMOD:ROW_DMOD + 1, :] += _colsum(dh1)
        head[ROW_DMOD + 1:ROW_DMOD + 2, :] += pre1 * sum_h
        head[ROW_N1PRE:ROW_N1PRE + 1, :] += (1.0 + scale1) * sum_h

        @pl.when(step == nt - 1)
        def _():
            pack[0:PACK_FINE, :] = jnp.zeros((PACK_FINE, D), F32)
            pack[PK_TABLE_B:PK_MISC, :] = jnp.zeros((PK_MISC - PK_TABLE_B, D), F32)
            pack[pl.ds(pl.multiple_of(_table_row(me), 8), 8), :] = small_ref[0:8, :] + head[0:8, :]
            pack[PK_WS:PK_WS + 64, :] = small_ref[ROW_WS:ROW_WS + 64, :]
            pack[PK_MISC:PK_MISC + 8, :] = small_ref[ROW_N1PRE:ROW_N1PRE + 8, :] + head[8:16, :]
            pack[PK_BS:PK_BS + 8, :] = small_ref[ROW_BS:ROW_BS + 8, :]
            pack[PK_WP:PK_WP + 64, :] = small_ref[ROW_WP:ROW_WP + 64, :]
            pack_to_sibling().start()
            pack_to_sibling().wait_recv()
            chip_sum = pack[my_half, :] + pack_sib[my_half, :]
            fine[my_chip] = chip_sum[:PACK_FINE, :]
            bulk[my_chip] = chip_sum[PACK_FINE:, :].astype(BF16)
            for r in range(3):
                half_to_chip(r, 0).start()
                half_to_chip(r, 1).start()
            for a in range(n):
                rows = p_refs[a].shape[1]
                for r in range(3):
                    to_chip(a, r).wait_recv()
                    for s in range(0, rows, row_chunk):
                        sl = pl.ds(s, row_chunk)
                        g_refs[a][sl, :] = g_refs[a][sl, :] + chip_in[a][r, sl, :].astype(F32)
            for r in range(3):
                half_from_chip(r, 0).wait_recv()
                half_from_chip(r, 1).wait_recv()
            half_start = pl.multiple_of(PACK_HALF * c, 8)
            total_scr[pl.ds(half_start, PACK_FINE), :] = ((fine[0] + fine[1]) + fine[2]) + fine[3]
            total_scr[pl.ds(half_start + PACK_FINE, PACK_HALF - PACK_FINE), :] = (
                (bulk[0].astype(F32) + bulk[1].astype(F32)) + bulk[2].astype(F32)) + bulk[3].astype(F32)
            total_to_sibling().start()
            total_from_sibling().wait_recv()
            total_ref[...] = total_scr[...]
            for a in range(n):
                all_from_sibling(a).wait_send()
                for r in range(3):
                    to_chip(a, r).wait_send()
            pack_to_sibling().wait_send()
            for r in range(3):
                half_to_chip(r, 0).wait_send()
                half_to_chip(r, 1).wait_send()
            total_to_sibling().wait_send()

    tile = lambda w: pl.BlockSpec((tt, w), lambda i: (i, 0))
    return pl.pallas_call(
        body, name="tail_comm", grid=(nt,),
        out_shape=tuple([jax.ShapeDtypeStruct(p.shape[1:], F32) for p in parts]
                        + [jax.ShapeDtypeStruct((PACK_ROWS, D), F32), jax.ShapeDtypeStruct((t_len, D), F32)]),
        in_specs=([_resident(p.shape) for p in parts] + [_resident(small.shape)]
                  + [tile(D_Z), tile(D), tile(D), _full((8, D)), _full((1, D)), _resident((D_Z, D))]),
        out_specs=tuple([_full(p.shape[1:]) for p in parts] + [_full((PACK_ROWS, D)), tile(D)]),
        scratch_shapes=(
            [pltpu.VMEM((4,) + p.shape[1:], BF16) for p in parts]
            + [pltpu.VMEM((3,) + p.shape[1:], BF16) for p in parts]
            + [pltpu.VMEM((3,) + p.shape[1:], BF16) for p in parts]
            + [pltpu.VMEM((PACK_ROWS, D), F32), pltpu.VMEM((PACK_ROWS, D), F32),
               pltpu.VMEM((4, PACK_FINE, D), F32), pltpu.VMEM((4, PACK_HALF - PACK_FINE, D), BF16),
               pltpu.VMEM((PACK_ROWS, D), F32), pltpu.VMEM((16, D), F32)]
            + [pltpu.SemaphoreType.DMA((n,)), pltpu.SemaphoreType.DMA((n,)),
               pltpu.SemaphoreType.DMA((3 * n,)), pltpu.SemaphoreType.DMA((3 * n,)),
               pltpu.SemaphoreType.DMA((8,)), pltpu.SemaphoreType.DMA((8,))]),
        compiler_params=pltpu.CompilerParams(dimension_semantics=("arbitrary",), vmem_limit_bytes=VMEM_LIMIT),
    )(*parts, small, dz, dx1, x, mod, n1pre, w_in_t)


def _tril_mask():
    row = lax.broadcasted_iota(jnp.int32, (CHUNK, CHUNK), 0)
    col = lax.broadcasted_iota(jnp.int32, (CHUNK, CHUNK), 1)
    return (col <= row).astype(F32)


def _window_sums(ext):
    s2 = ext + pltpu.roll(ext, 1, 0)
    t4 = s2[:, GROUP:]
    s4 = t4 + pltpu.roll(t4, 2, 0)
    t8 = s4[:, GROUP:]
    s8 = t8 + pltpu.roll(t8, 4, 0)
    t16 = s8[:, GROUP:]
    s16 = t16 + pltpu.roll(t16, 8, 0)
    return [s2[:, :GROUP], s4[:, :GROUP], s8[:, :GROUP], s16]


def _inv_counts(first_pos, rows):
    pos = first_pos + lax.broadcasted_iota(jnp.int32, (rows, 1), 0)
    return [1.0 / jnp.minimum(pos + 1, w).astype(F32) for w in WINDOWS]


def _pool_diff(zb, halo, first_pos):
    tt = zb.shape[0]
    sums = _window_sums(jnp.concatenate([halo, zb], axis=0))
    inv = _inv_counts(first_pos, tt)
    return [sums[g][HALO:, :] * inv[g] - zb[:, g * GROUP:(g + 1) * GROUP] for g in range(len(WINDOWS))]


def _row_blocks(scr, rows, place):
    def block(rel):
        start = pl.multiple_of(rows * _index(_peer(*place, rel)), rows)
        return scr.at[pl.ds(start, rows), :]

    def entries(shard_ref):
        return ([(shard_ref, block(0), rel) for rel in (1, 2, 4, 6)]
                + [(block(rel), block(rel), 1) for rel in (2, 4, 6)])
    return block, entries


def _attn_fwd(tt, x, c_row, w_ada, b_pieces, n1pre, n1post, w_in_shard, w_out_shard, w_sp, bs_rows, ln_g, ln_b,
              w_pool, b_pool, pool_scale, fc_shards, n2pre):
    t_len = x.shape[0]
    nt = t_len // tt
    ncol = w_ada.shape[1]

    def body(x_ref, xb_ref, c_ref, wada_ref, b_ref, n1pre_ref, n1post_ref, wi_ref, wo_ref, wsp_ref, bs_ref,
             lng_ref, lnb_ref, wp_ref, bp_ref, ps_ref, w1_ref, w2_ref, n2pre_ref,
             z_ref, cat_ref, mix_ref, x1_ref, h2_ref, r_ref, f_ref, mod_out, sc_out, e1_ref, e2_ref, wout_ref,
             win_out, carry, land1, land2, sib1, sib2, cat_keep, wout_scr, win_ref, mod_ref, cg, mg, part,
             send_sems, recv_sems, local_sems, wo_send, wo_recv, wi_send, wi_recv, ada_send, ada_recv):
        i = pl.program_id(0)
        place = px, py, pc = _place()
        me = _index(place)
        wo_block, wo_entries = _row_blocks(wout_scr, w_out_shard.shape[0], place)
        wi_block, wi_entries = _row_blocks(win_ref, w_in_shard.shape[0], place)
        wo_copies = _Copies(wo_entries(wo_ref), wo_send, wo_recv)
        wi_copies = _Copies(wi_entries(wi_ref), wi_send, wi_recv)
        wo_keep = pltpu.make_async_copy(wout_scr, wout_ref, local_sems.at[8])
        wi_keep = pltpu.make_async_copy(win_ref, win_out, local_sems.at[9])
        ada = _Copies([(cg.at[me], cg.at[me], k) for k in range(1, N_DEV)]
                      + [(part, mg.at[me], k) for k in range(1, N_DEV)], ada_send, ada_recv)
        copies = _Copies(
            [(w1_ref, sib1, 1), (w2_ref, sib2, 1),
             (w1_ref, land1.at[0], 2), (w2_ref, land2.at[0], 2),
             (w1_ref, land1.at[1], 4), (w2_ref, land2.at[1], 4),
             (land1.at[0], e1_ref.at[3], 1), (land2.at[0], e2_ref.at[3], 1),
             (land1.at[1], e1_ref.at[5], 1), (land2.at[1], e2_ref.at[5], 1)],
            send_sems, recv_sems)
        keep = [pltpu.make_async_copy(w1_ref, e1_ref.at[0], local_sems.at[0]),
                pltpu.make_async_copy(w2_ref, e2_ref.at[0], local_sems.at[1]),
                pltpu.make_async_copy(land1.at[0], e1_ref.at[2], local_sems.at[2]),
                pltpu.make_async_copy(land1.at[1], e1_ref.at[4], local_sems.at[3]),
                pltpu.make_async_copy(land2.at[0], e2_ref.at[2], local_sems.at[4]),
                pltpu.make_async_copy(land2.at[1], e2_ref.at[4], local_sems.at[5]),
                pltpu.make_async_copy(sib1, e1_ref.at[1], local_sems.at[6]),
                pltpu.make_async_copy(sib2, e2_ref.at[1], local_sems.at[7])]

        @pl.when(i == 0)
        def _():
            cg[me] = jnp.broadcast_to(c_ref[...], (8, D))
            ada.start(*range(N_DEV - 1))
            wi_copies.start(0, 1, 2, 3)
            wi_rows, wo_rows = w_in_shard.shape[0], w_out_shard.shape[0]
            win_ref[pl.ds(pl.multiple_of(wi_rows * me, wi_rows), wi_rows), :] = wi_ref[...]
            wout_scr[pl.ds(pl.multiple_of(wo_rows * me, wo_rows), wo_rows), :] = wo_ref[...]
            carry[...] = jnp.zeros_like(carry)

            ada.wait_recv(*range(N_DEV - 1))
            c_all = jnp.concatenate([cg[j, 0:1, :] for j in range(N_DEV)], axis=0)
            sc = c_all * jax.nn.sigmoid(c_all)
            sc_out[...] = sc
            part[...] = _dot(sc.astype(BF16), wada_ref[...].astype(BF16)) + b_ref[me]
            ada.start(*range(N_DEV - 1, 2 * (N_DEV - 1)))
            wo_copies.start(0, 1, 2, 3)
            copies.start(0, 1, 2, 4, 3, 5)
            keep[0].start()
            keep[1].start()
            mg[me] = part[...]
            ada.wait_recv(*range(N_DEV - 1, 2 * (N_DEV - 1)))
            mod_ref[...] = jnp.zeros_like(mod_ref)
            for j in range(N_DEV):
                for m in range(6):
                    lo, hi = max(ncol * j, D * m), min(ncol * (j + 1), D * (m + 1))
                    if lo < hi:
                        mod_ref[m:m + 1, lo - D * m:hi - D * m] = mg[j, pl.ds(me, 1), lo - ncol * j:hi - ncol * j]
            mod_out[...] = mod_ref[...]

            wi_copies.wait_recv(1, 2, 3)
            wi_copies.start(4, 5, 6)
            wi_copies.wait_recv(0, 4, 5, 6)
            wi_keep.start()

        @pl.when(i == nt // 2 + ATTN_LAG)
        def _():
            copies.wait_recv(2, 4)
            copies.start(6, 8)
            keep[2].start()
            keep[3].start()

        @pl.when(i == nt - 1 + ATTN_LAG)
        def _():
            copies.wait_recv(3, 5)
            copies.start(7, 9)
            keep[4].start()
            keep[5].start()

        shift1, scale1, gate1 = mod_ref[0:1, :], mod_ref[1:2, :], mod_ref[2:3, :]

        @pl.when(i < nt)
        def _():
            xv = x_ref[...]
            h1 = (xv * _rstd(xv)) * (n1pre_ref[...] * (1.0 + scale1)) + shift1
            z = _dot_nt(h1.astype(BF16), win_ref[...])
            z_ref[...] = z

            _, ga = _gelu_parts(z[:, :2 * D_A])
            u, vr = ga[:, :D_A], ga[:, D_A:]
            dv = vr - jnp.mean(vr, axis=-1, keepdims=True)
            v = (dv * lax.rsqrt(jnp.mean(dv * dv, axis=-1, keepdims=True) + EPS)) * lng_ref[...] + lnb_ref[...]
            vb = v.astype(BF16)
            mask = _tril_mask()
            wc = [(wsp_ref[h] * mask).astype(BF16) for h in range(N_HEADS)]
            for ch in range(tt // CHUNK):
                rows = slice(ch * CHUNK, (ch + 1) * CHUNK)
                for h in range(N_HEADS):
                    cols = slice(h * GROUP, (h + 1) * GROUP)
                    mixed = _dot(wc[h], vb[rows, cols]) + bs_ref[:, cols]
                    cat_ref[rows, cols] = (u[rows, cols] * mixed).astype(BF16)

            zb = z[:, 2 * D_A:]
            diff = _pool_diff(zb, carry[...], i * tt)
            carry[...] = zb[tt - HALO:, :]
            for g in range(len(WINDOWS)):
                cols = slice(g * GROUP, (g + 1) * GROUP)
                pre = _dot(diff[g].astype(BF16), wp_ref[g].astype(BF16)) + bp_ref[:, cols]
                cat_ref[:, D_A + g * GROUP:D_A + (g + 1) * GROUP] = (pre * ps_ref[:, cols]).astype(BF16)
            cat_keep[i % (ATTN_LAG + 1)] = cat_ref[...]

        @pl.when(i == 0)
        def _():
            copies.wait_recv(0, 1)
            keep[6].start()
            keep[7].start()

        @pl.when(i == 1)
        def _():
            wo_copies.wait_recv(1, 2, 3)
            wo_copies.start(4, 5, 6)

        @pl.when(i == ATTN_LAG)
        def _():
            wo_copies.wait_recv(0, 4, 5, 6)
            wo_keep.start()

        @pl.when(i >= ATTN_LAG)
        def _():
            xv = xb_ref[...]
            mix = _dot(cat_keep[(i - ATTN_LAG) % (ATTN_LAG + 1)], wout_scr[...])
            mix_ref[...] = mix
            x1v = xv + (mix * _rstd(mix)) * (gate1 * n1post_ref[...])
            x1_ref[...] = x1v
            shift2, scale2 = mod_ref[3:4, :], mod_ref[4:5, :]
            h2 = ((x1v * _rstd(x1v)) * (n2pre_ref[...] * (1.0 + scale2)) + shift2).astype(BF16)
            h2_ref[...] = h2
            for j, (w1, w2) in enumerate(((w1_ref, w2_ref), (sib1, sib2))):
                ra = jnp.maximum(_dot(h2, w1[...]), 0.0)
                r = (ra * ra).astype(BF16)
                r_ref[:, j * FF_BLK:(j + 1) * FF_BLK] = r
                if j == 0:
                    f_ref[...] = _dot(r, w2[...])
                else:
                    f_ref[...] += _dot(r, w2[...])

        @pl.when(i == nt - 1 + ATTN_LAG)
        def _():
            copies.wait_recv(6, 7, 8, 9)
            copies.wait_send(*range(10))
            wo_copies.wait_send(*range(7))
            wi_copies.wait_send(*range(7))
            ada.wait_send(*range(2 * (N_DEV - 1)))
            for cp in keep:
                cp.wait()
            wo_keep.wait()
            wi_keep.wait()

    first = lambda w: pl.BlockSpec((tt, w), lambda i: (jnp.minimum(i, nt - 1), 0))
    second = lambda w: pl.BlockSpec((tt, w), lambda i: (jnp.maximum(i - ATTN_LAG, 0), 0))
    r_head = pl.BlockSpec((tt, FC_HEAD * FF_BLK),
                          lambda i: (jnp.maximum(i - ATTN_LAG, 0), R_HEAD_COLS // (FC_HEAD * FF_BLK)))
    hbm = pl.BlockSpec(memory_space=pl.ANY)
    outs = pl.pallas_call(
        body, name="attn_fwd", grid=(nt + ATTN_LAG,),
        out_shape=tuple([jax.ShapeDtypeStruct((t_len, D_Z), F32), jax.ShapeDtypeStruct((t_len, D), BF16),
                         jax.ShapeDtypeStruct((t_len, D), F32), jax.ShapeDtypeStruct((t_len, D), F32),
                         jax.ShapeDtypeStruct((t_len, D), BF16),
                         jax.ShapeDtypeStruct((t_len, FC_EARLY * FF_BLK), BF16),
                         jax.ShapeDtypeStruct((t_len, D), F32)]
                        + [jax.ShapeDtypeStruct((8, D), F32), jax.ShapeDtypeStruct((N_DEV, D), F32)]
                        + [jax.ShapeDtypeStruct((FC_EARLY,) + s.shape, BF16) for s in fc_shards]
                        + [jax.ShapeDtypeStruct((D, D), BF16), jax.ShapeDtypeStruct((D_Z, D), BF16)]),
        in_specs=[first(D), second(D), _full((1, D)), _resident(w_ada.shape), _full((N_DEV, 1, ncol)), _full((1, D)),
                  _full((1, D)), _resident(w_in_shard.shape), _resident(w_out_shard.shape),
                  _full((N_HEADS, CHUNK, CHUNK)), _full((CHUNK, D_A)), _full((1, D_A)), _full((1, D_A)),
                  _full((len(WINDOWS), GROUP, GROUP)), _full((1, D_B)), _full((1, D_B)),
                  _resident(fc_shards[0].shape), _resident(fc_shards[1].shape), _full((1, D))],
        out_specs=(first(D_Z), first(D), second(D), second(D), second(D), r_head, second(D),
                   _full((8, D)), _full((N_DEV, D)), hbm, hbm, hbm, hbm),
        scratch_shapes=[pltpu.VMEM((HALO, D_B), F32),
                        pltpu.VMEM((2,) + fc_shards[0].shape, BF16), pltpu.VMEM((2,) + fc_shards[1].shape, BF16),
                        pltpu.VMEM(fc_shards[0].shape, BF16), pltpu.VMEM(fc_shards[1].shape, BF16),
                        pltpu.VMEM((ATTN_LAG + 1, tt, D), BF16), pltpu.VMEM((D, D), BF16),
                        pltpu.VMEM((D_Z, D), BF16), pltpu.VMEM((8, D), F32),
                        pltpu.VMEM((N_DEV, 8, D), F32), pltpu.VMEM((N_DEV, N_DEV, ncol), F32),
                        pltpu.VMEM((N_DEV, ncol), F32),
                        pltpu.SemaphoreType.DMA((10,)), pltpu.SemaphoreType.DMA((10,)),
                        pltpu.SemaphoreType.DMA((10,)),
                        pltpu.SemaphoreType.DMA((7,)), pltpu.SemaphoreType.DMA((7,)),
                        pltpu.SemaphoreType.DMA((7,)), pltpu.SemaphoreType.DMA((7,)),
                        pltpu.SemaphoreType.DMA((2 * (N_DEV - 1),)), pltpu.SemaphoreType.DMA((2 * (N_DEV - 1),))],
        compiler_params=pltpu.CompilerParams(dimension_semantics=("arbitrary",), vmem_limit_bytes=VMEM_LIMIT),
    )(x, x, c_row, w_ada, b_pieces, n1pre, n1post, w_in_shard, w_out_shard, w_sp, bs_rows, ln_g, ln_b, w_pool, b_pool,
      pool_scale, *fc_shards, n2pre)
    return outs[:9], outs[9:11], outs[11], outs[12]


def _mlp_fwd_early(tt, r_begun, h2, f_head, w1_early, w2_early):
    t_len = h2.shape[0]
    nt = t_len // tt
    n_late = N_DEV - FC_EARLY

    def body(r_begun_ref, h2_ref, fh_ref, w1_ref, w2_ref, r_ref, f_ref, l1_ref, l2_ref,
             land1, land2, send_sems, recv_sems, local_sems):
        i = pl.program_id(0)
        copies = _Copies(
            [(w1_ref.at[2], land1, 4), (w2_ref.at[4], land2, 2),
             (land1, l1_ref.at[1], 1), (land2, l2_ref.at[1], 1)],
            send_sems, recv_sems)
        keep = [pltpu.make_async_copy(land1, l1_ref.at[0], local_sems.at[0]),
                pltpu.make_async_copy(land2, l2_ref.at[0], local_sems.at[1])]

        @pl.when(i == 0)
        def _():
            copies.start(0, 1)

        @pl.when(i == nt - 1)
        def _():
            copies.wait_recv(0, 1)
            copies.start(2, 3)
            for cp in keep:
                cp.start()

        h2 = h2_ref[...]
        f_ref[...] = fh_ref[...]
        for j in range(FC_HEAD, FC_EARLY):
            ra = jnp.maximum(_dot(h2, w1_ref[j]), 0.0)
            r = (ra * ra).astype(BF16)
            r_ref[:, _early_col(j):_early_col(j) + FF_BLK] = r
            f_ref[...] += _dot(r, w2_ref[j])

        @pl.when(i == nt - 1)
        def _():
            copies.wait_recv(2, 3)
            copies.wait_send(0, 1, 2, 3)
            for cp in keep:
                cp.wait()

    tile = lambda w: pl.BlockSpec((tt, w), lambda i: (i, 0))
    hbm = pl.BlockSpec(memory_space=pl.ANY)
    outs = pl.pallas_call(
        body, name="mlp_fwd_early", grid=(nt,),
        out_shape=(jax.ShapeDtypeStruct((t_len, FC_EARLY * FF_BLK), BF16), jax.ShapeDtypeStruct((t_len, D), F32),
                   jax.ShapeDtypeStruct((n_late,) + w1_early.shape[1:], BF16),
                   jax.ShapeDtypeStruct((n_late,) + w2_early.shape[1:], BF16)),
        in_specs=[hbm, tile(D), tile(D), _resident((FC_EARLY, D, FF_BLK)), _resident((FC_EARLY, FF_BLK, D))],
        out_specs=(tile(R_HEAD_COLS), tile(D), hbm, hbm),
        input_output_aliases={0: 0},
        scratch_shapes=[pltpu.VMEM(w1_early.shape[1:], BF16), pltpu.VMEM(w2_early.shape[1:], BF16),
                        pltpu.SemaphoreType.DMA((4,)), pltpu.SemaphoreType.DMA((4,)),
                        pltpu.SemaphoreType.DMA((2,))],
        compiler_params=pltpu.CompilerParams(dimension_semantics=("arbitrary",), vmem_limit_bytes=VMEM_LIMIT),
    )(r_begun, h2, f_head, w1_early, w2_early)
    return outs[:2], outs[2:]


def _mlp_late_bwd(tt, r_early, x1, h2, f_early, tgt, mix, mod, n2pre, n2post, n1post,
                  w1_early, w2_early, w1_late, w2_late):
    t_len = x1.shape[0]
    nt = t_len // tt
    n_late = N_DEV - FC_EARLY
    late_cols = n_late * FF_BLK

    def body(re_ref, x1_ref, h2_ref, fe_ref, tgt_ref, mix_ref, mod_ref, n2pre_ref, n2post_ref,
             n1post_ref, w1e_ref, w2e_ref, w1l_ref, w2l_ref,
             rl_ref, df_ref, da_ref, dmix_ref, dx1_ref, redf_ref, redb_ref, dh2_acc):
        i = pl.program_id(0)

        @pl.when(i == 0)
        def _():
            redf_ref[...] = jnp.zeros_like(redf_ref)
            redb_ref[...] = jnp.zeros_like(redb_ref)

        x1v = x1_ref[...]
        gate1, scale2, gate2 = mod_ref[2:3, :], mod_ref[4:5, :], mod_ref[5:6, :]
        h2 = h2_ref[...]
        f = fe_ref[...]
        for j in range(n_late):
            cols = slice(j * FF_BLK, (j + 1) * FF_BLK)
            ra = jnp.maximum(_dot(h2, w1l_ref[j]), 0.0)
            r = (ra * ra).astype(BF16)
            rl_ref[:, cols] = r
            f = f + _dot(r, w2l_ref[j])
        post2 = n2post_ref[...]
        gate_post2 = gate2 * post2
        rf = _rstd(f)
        fhat = f * rf
        err = (x1v + fhat * gate_post2) - tgt_ref[...]
        dy = err * (1.0 / D)
        d_f, sum_f = _rms_bwd_gained(dy, gate_post2, fhat, rf)
        dfv = d_f.astype(BF16)
        df_ref[...] = dfv
        redf_ref[0:1, :] += post2 * sum_f
        redf_ref[1:2, :] += gate2 * sum_f
        redf_ref[2:3, :] += _colsum(err * err)

        for j in range(N_DEV):
            cols = slice(j * FF_BLK, (j + 1) * FF_BLK)
            if j < FC_EARLY:
                w1, w2, r = w1e_ref[j], w2e_ref[j], re_ref[:, _early_col(j):_early_col(j) + FF_BLK]
            else:
                jl = j - FC_EARLY
                w1, w2, r = w1l_ref[jl], w2l_ref[jl], rl_ref[:, jl * FF_BLK:(jl + 1) * FF_BLK]
            dr = _dot_nt(dfv, w2)
            da = (dr * (2.0 * jnp.sqrt(r.astype(F32)))).astype(BF16)
            da_ref[:, cols] = da
            contrib = _dot_nt(da, w1)
            if j == 0:
                dh2_acc[...] = contrib
            else:
                dh2_acc[...] += contrib
        dh2 = dh2_acc[...]
        pre2, post1 = n2pre_ref[...], n1post_ref[...]
        r2 = _rstd(x1v)
        xhat = x1v * r2
        d_x1, sum_h = _rms_bwd_gained(dh2, pre2 * (1.0 + scale2), xhat, r2)
        dx1 = dy + d_x1
        dx1_ref[...] = dx1
        mixv = mix_ref[...]
        rm = _rstd(mixv)
        mhat = mixv * rm
        d_mix, sum_m = _rms_bwd_gained(dx1, gate1 * post1, mhat, rm)
        dmix_ref[...] = d_mix.astype(BF16)
        redb_ref[0:1, :] += _colsum(dh2)
        redb_ref[1:2, :] += pre2 * sum_h
        redb_ref[2:3, :] += (1.0 + scale2) * sum_h
        redb_ref[3:4, :] += post1 * sum_m
        redb_ref[4:5, :] += gate1 * sum_m

    tile = lambda w: pl.BlockSpec((tt, w), lambda i: (i, 0))
    return pl.pallas_call(
        body, name="mlp_late_bwd", grid=(nt,),
        out_shape=(jax.ShapeDtypeStruct((t_len, late_cols), BF16), jax.ShapeDtypeStruct((t_len, D), BF16),
                   jax.ShapeDtypeStruct((t_len, D_FF), BF16), jax.ShapeDtypeStruct((t_len, D), BF16),
                   jax.ShapeDtypeStruct((t_len, D), F32), jax.ShapeDtypeStruct((8, D), F32),
                   jax.ShapeDtypeStruct((8, D), F32)),
        in_specs=[tile(FC_EARLY * FF_BLK), tile(D), tile(D), tile(D), tile(D),
                  tile(D), _full((8, D)), _full((1, D)), _full((1, D)), _full((1, D)),
                  _resident((FC_EARLY, D, FF_BLK)), _resident((FC_EARLY, FF_BLK, D)),
                  _resident((n_late, D, FF_BLK)), _resident((n_late, FF_BLK, D))],
        out_specs=(tile(late_cols), tile(D), tile(D_FF), tile(D), tile(D), _full((8, D)), _full((8, D))),
        scratch_shapes=[pltpu.VMEM((tt, D), F32)],
        compiler_params=pltpu.CompilerParams(dimension_semantics=("arbitrary",), vmem_limit_bytes=VMEM_LIMIT),
    )(r_early, x1, h2, f_early, tgt, mix, mod, n2pre, n2post, n1post, w1_early, w2_early, w1_late, w2_late)


def _mlp_wgrad(tt, r_early, r_late, da, df, h2):
    t_len = df.shape[0]
    nt = t_len // tt
    odd_steps = [j for j, rel in enumerate(WGRAD_ORDER) if rel % 2]

    def relation(j):
        rel = jnp.int32(WGRAD_ORDER[-1])
        for step in range(N_DEV - 2, -1, -1):
            rel = jnp.where(j == step, WGRAD_ORDER[step], rel)
        return rel

    def body(re_ref, rl_ref, da_ref, df_ref, h2_ref, own1_ref, own2_ref, out1_ref, out2_ref, diag1_ref, diag2_ref,
             acc1, acc2, snd1, snd2, sib1, sib2, dsnd1, dsnd2, send_sems, recv_sems):
        j, t = pl.program_id(0), pl.program_id(1)
        rows = pl.ds(pl.multiple_of(t * tt, tt), tt)
        x, y, c = _place()
        accs, snds, sibs = (acc1, acc2), (snd1, snd2), (sib1, sib2)
        dsnds, diags = (dsnd1, dsnd2), (diag1_ref, diag2_ref)

        def to_sibling(a, jj, buf=0):
            return pltpu.make_async_remote_copy(
                src_ref=snds[a].at[buf], dst_ref=sibs[a].at[jj],
                send_sem=send_sems.at[4 * a + jj], recv_sem=recv_sems.at[4 * a + jj],
                device_id=(x, y, 1 - c), device_id_type=MESH)

        def to_diagonal(a):
            return pltpu.make_async_remote_copy(
                src_ref=dsnds[a], dst_ref=diags[a], send_sem=send_sems.at[8 + a], recv_sem=recv_sems.at[8 + a],
                device_id=_peer(x, y, c, 6), device_id_type=MESH)

        @pl.when(t == 0)
        def _():
            acc2[...] = jnp.zeros_like(acc2)
            acc1[...] = jnp.zeros_like(acc1)

        for r_ref, mine in ((re_ref, relation(j) < FC_EARLY), (rl_ref, relation(j) >= FC_EARLY)):
            @pl.when(mine)
            def _():
                acc2[...] += _dot_tn(r_ref[...], df_ref[rows, :])
                acc1[...] += _dot_tn(h2_ref[rows, :], da_ref[...])

        for step, rel in enumerate(WGRAD_ORDER):
            jj = rel // 2

            @pl.when((t == nt - 1) & (j == step))
            def _():
                for a, (own_ref, out_ref) in enumerate(((own1_ref, out1_ref), (own2_ref, out2_ref))):
                    if rel % 2:
                        q = odd_steps.index(step)
                        if q >= 2:
                            to_sibling(a, WGRAD_ORDER[odd_steps[q - 2]] // 2).wait_send()
                        snds[a][q % 2] = accs[a][...].astype(BF16)
                        to_sibling(a, jj, q % 2).start()
                        continue
                    to_sibling(a, jj).wait_recv()
                    chip_sum = accs[a][...] + sibs[a][jj].astype(F32)
                    if rel == 6:
                        dsnds[a][...] = chip_sum.astype(BF16)
                        to_diagonal(a).start()
                    elif rel == 0:
                        own_ref[...] = chip_sum
                    else:
                        out_ref[0] = chip_sum.astype(BF16)
                    if step == N_DEV - 1:
                        for q in (2, 3):
                            to_sibling(a, WGRAD_ORDER[odd_steps[q]] // 2).wait_send()
                        to_diagonal(a).wait_recv()
                        to_diagonal(a).wait_send()

    assert WGRAD_ORDER[-1] == 0 and WGRAD_ORDER[-3:-1] == (2, 4)
    blk = pl.BlockSpec((tt, FF_BLK), lambda j, t: (t, relation(j)))
    early_block = lambda rel: jnp.where(rel < FC_HEAD, rel + FC_EARLY - FC_HEAD, rel - FC_HEAD)
    early = lambda j, t: (jnp.where(relation(j) < FC_EARLY, t, 0),
                          jnp.where(relation(j) < FC_EARLY, early_block(relation(j)), 0))
    late = lambda j, t: (jnp.where(relation(j) < FC_EARLY, 0, t), jnp.maximum(relation(j) - FC_EARLY, 0))
    chip = lambda j, t: (jnp.clip(j - 5, 0, 1), 0, 0)
    hbm = pl.BlockSpec(memory_space=pl.ANY)
    return pl.pallas_call(
        body, name="mlp_wgrad", grid=(N_DEV, nt),
        out_shape=(jax.ShapeDtypeStruct((D, FF_BLK), F32), jax.ShapeDtypeStruct((FF_BLK, D), F32),
                   jax.ShapeDtypeStruct((2, D, FF_BLK), BF16), jax.ShapeDtypeStruct((2, FF_BLK, D), BF16),
                   jax.ShapeDtypeStruct((D, FF_BLK), BF16), jax.ShapeDtypeStruct((FF_BLK, D), BF16)),
        in_specs=[pl.BlockSpec((tt, FF_BLK), early), pl.BlockSpec((tt, FF_BLK), late), blk,
                  _resident((t_len, D)), _resident((t_len, D))],
        out_specs=(_full((D, FF_BLK)), _full((FF_BLK, D)),
                   pl.BlockSpec((1, D, FF_BLK), chip), pl.BlockSpec((1, FF_BLK, D), chip), hbm, hbm),
        scratch_shapes=[pltpu.VMEM((D, FF_BLK), F32), pltpu.VMEM((FF_BLK, D), F32),
                        pltpu.VMEM((2, D, FF_BLK), BF16), pltpu.VMEM((2, FF_BLK, D), BF16),
                        pltpu.VMEM((4, D, FF_BLK), BF16), pltpu.VMEM((4, FF_BLK, D), BF16),
                        pltpu.VMEM((D, FF_BLK), BF16), pltpu.VMEM((FF_BLK, D), BF16),
                        pltpu.SemaphoreType.DMA((10,)), pltpu.SemaphoreType.DMA((10,))],
        compiler_params=pltpu.CompilerParams(dimension_semantics=("arbitrary", "arbitrary"),
                                             vmem_limit_bytes=VMEM_LIMIT),
    )(r_early, r_late, da, df, h2)


def _acc_rows(ref, row0, k, val):
    half = CHUNK // 2
    ref[row0:row0 + half, k * GROUP:(k + 1) * GROUP] += val[:half, :]
    ref[row0:row0 + half, D_A + k * GROUP:D_A + (k + 1) * GROUP] += val[half:, :]


def _attn_bwd(tt, dmix, x, z, cat, mod, n1pre, w_out, w_sp, bs_rows, ln_g, ln_b, w_pool, b_pool, pool_scale,
              red_fwd, red_bwd, chip_sums):
    t_len = z.shape[0]
    nt = t_len // tt
    hb = tt // HALO
    n_sums = len(chip_sums)

    def body(dmix_ref, x_ref, z_ref, zprev_ref, cat_ref, mod_ref, n1pre_ref, wout_ref, wsp_ref,
             bs_ref, lng_ref, lnb_ref, wp_ref, bp_ref, ps_ref, redf_ref, redb_ref, *rest):
        sum_out = rest[:n_sums]
        dz_ref, gwin_ref, gwout_ref, small_ref = rest[n_sums:n_sums + 4]
        sum_in = rest[n_sums + 4:2 * n_sums + 4]
        carry, acc_in, acc_out, dz_scr, bs_acc, send_sems, recv_sems = rest[2 * n_sums + 4:]
        s = pl.program_id(0)
        i = nt - 1 - s
        px, py, pc = _place()

        def chip_copy(a, r):
            return pltpu.make_async_remote_copy(
                src_ref=sum_out[a].at[r], dst_ref=sum_in[a].at[r],
                send_sem=send_sems.at[2 * a + r], recv_sem=recv_sems.at[2 * a + r],
                device_id=_peer(px, py, pc, 2 * (r + 1)), device_id_type=MESH)

        @pl.when(s == 0)
        def _():
            for a in range(n_sums):
                for r in range(2):
                    chip_copy(a, r).start()
            carry[...] = jnp.zeros_like(carry)
            acc_in[...] = jnp.zeros_like(acc_in)
            acc_out[...] = jnp.zeros_like(acc_out)
            bs_acc[...] = jnp.zeros_like(bs_acc)
            small_ref[...] = jnp.zeros_like(small_ref)
            small_ref[ROW_DMOD + 2:ROW_DMOD + 3, :] = redb_ref[3:4, :]
            small_ref[ROW_DMOD + 3:ROW_DMOD + 5, :] = redb_ref[0:2, :]
            small_ref[ROW_DMOD + 5:ROW_DMOD + 6, :] = redf_ref[0:1, :]
            small_ref[ROW_N1POST:ROW_N1POST + 1, :] = redb_ref[4:5, :]
            small_ref[ROW_N2PRE:ROW_N2PRE + 1, :] = redb_ref[2:3, :]
            small_ref[ROW_N2POST:ROW_N2POST + 1, :] = redf_ref[1:2, :]
            small_ref[ROW_LOSS:ROW_LOSS + 1, :] = redf_ref[2:3, :]

        dmixv = dmix_ref[...]
        dcat = _dot_nt(dmixv, wout_ref[...])
        acc_out[...] += _dot_tn(cat_ref[...], dmixv)

        z = z_ref[...]
        t_g, ga = _gelu_parts(z[:, :2 * D_A])
        u, vr = ga[:, :D_A], ga[:, D_A:]
        dv0 = vr - jnp.mean(vr, axis=-1, keepdims=True)
        rv = lax.rsqrt(jnp.mean(dv0 * dv0, axis=-1, keepdims=True) + EPS)
        vhat = dv0 * rv
        vb = (vhat * lng_ref[...] + lnb_ref[...]).astype(BF16)
        mask = _tril_mask()
        wc = [(wsp_ref[h] * mask).astype(BF16) for h in range(N_HEADS)]

        dya = dcat[:, :D_A]
        for h in range(N_HEADS):
            cols = slice(h * GROUP, (h + 1) * GROUP)
            bs_sum = jnp.zeros((CHUNK, GROUP), F32)
            ws_sum = jnp.zeros((CHUNK, CHUNK), F32)
            for ch in range(tt // CHUNK):
                rows = slice(ch * CHUNK, (ch + 1) * CHUNK)
                v_ch = vb[rows, cols]
                mixed = _dot(wc[h], v_ch) + bs_ref[:, cols]
                dy_ch = dya[rows, cols]
                dz_scr[rows, cols] = dy_ch * mixed
                dmixed = dy_ch * u[rows, cols]
                dmb = dmixed.astype(BF16)
                dz_scr[rows, D_A + h * GROUP:D_A + (h + 1) * GROUP] = _dot_tn(wc[h], dmb)
                bs_sum = bs_sum + dmixed
                ws_sum = ws_sum + _dot_nt(dmb, v_ch)
            _acc_rows(bs_acc, 0, h, bs_sum)
            _acc_rows(small_ref, ROW_WS, h, ws_sum)

        dvl = dz_scr[:, D_A:2 * D_A]
        dvhat = dvl * lng_ref[...]
        dvl_vhat = dvl * vhat
        dvr = rv * (dvhat - jnp.mean(dvhat, axis=-1, keepdims=True)
                    - vhat * jnp.mean(dvl_vhat * lng_ref[...], axis=-1, keepdims=True))
        small_ref[ROW_LN:ROW_LN + 1, 0:D_A] += _colsum(dvl_vhat)
        small_ref[ROW_LN:ROW_LN + 1, D_A:D] += _colsum(dvl)
        dga = jnp.concatenate([dz_scr[:, :D_A], dvr], axis=1)
        dza = dga * _gelu_grad(z[:, :2 * D_A], t_g)

        zb = z[:, 2 * D_A:]
        halo_prev = jnp.where(i == 0, 0.0, zprev_ref[...])
        diff = _pool_diff(zb, halo_prev, i * tt)
        dyb = dcat[:, D_A:]
        inv = _inv_counts(i * tt, tt)
        scaled, ddiffs = [], []
        for g in range(len(WINDOWS)):
            cols = slice(g * GROUP, (g + 1) * GROUP)
            db = diff[g].astype(BF16)
            wpg = wp_ref[g].astype(BF16)
            pre = _dot(db, wpg) + bp_ref[:, cols]
            small_ref[ROW_POOL:ROW_POOL + 1, cols] += _colsum(dyb[:, cols] * pre)
            dpre = dyb[:, cols] * ps_ref[:, cols]
            small_ref[ROW_POOL:ROW_POOL + 1, D_B + g * GROUP:D_B + (g + 1) * GROUP] += _colsum(dpre)
            dpb = dpre.astype(BF16)
            _acc_rows(small_ref, ROW_WP, g, _dot_tn(db, dpb))
            ddiff = _dot_nt(dpb, wpg)
            ddiffs.append(ddiff)
            scaled.append(ddiff * inv[g])
        scaled_all = jnp.concatenate(scaled, axis=1)
        ext = jnp.concatenate([scaled_all, carry[...]], axis=0)
        n_ext = tt + HALO
        s2 = ext + pltpu.roll(ext, n_ext - 1, 0)
        t4 = s2[:, GROUP:]
        s4 = t4 + pltpu.roll(t4, n_ext - 2, 0)
        t8 = s4[:, GROUP:]
        s8 = t8 + pltpu.roll(t8, n_ext - 4, 0)
        t16 = s8[:, GROUP:]
        s16 = t16 + pltpu.roll(t16, n_ext - 8, 0)
        back = [s2[:, :GROUP], s4[:, :GROUP], s8[:, :GROUP], s16]
        carry[...] = scaled_all[:HALO, :]
        dzb = jnp.concatenate([back[g][:tt, :] - ddiffs[g] for g in range(len(WINDOWS))], axis=1)

        dzv = jnp.concatenate([dza, dzb], axis=1).astype(BF16)
        dz_ref[...] = dzv
        xv = x_ref[...]
        h1 = (xv * _rstd(xv) * (n1pre_ref[...] * (1.0 + mod_ref[1:2, :])) + mod_ref[0:1, :]).astype(BF16)
        acc_in[...] += _dot_tn(dzv, h1)

        @pl.when(s == nt - 1)
        def _():
            gwin_ref[...] = acc_in[...].astype(BF16)
            gwout_ref[...] = acc_out[...].astype(BF16)
            bs = _unfold(bs_acc[...])
            for h in range(N_HEADS):
                small_ref[ROW_BS + h:ROW_BS + h + 1, 0:GROUP] = jnp.sum(
                    bs[:, h * GROUP:(h + 1) * GROUP].T, axis=0, keepdims=True)
            for a in range(n_sums):
                for r in range(2):
                    chip_copy(a, r).wait_recv()
                    chip_copy(a, r).wait_send()

    rev = lambda w: pl.BlockSpec((tt, w), lambda s: (nt - 1 - s, 0))
    zprev = pl.BlockSpec((HALO, D_B), lambda s: (jnp.maximum((nt - 1 - s) * hb - 1, 0), 2))
    hbm = pl.BlockSpec(memory_space=pl.ANY)
    outs = pl.pallas_call(
        body, name="attn_bwd", grid=(nt,),
        out_shape=tuple([jax.ShapeDtypeStruct((t_len, D_Z), BF16), jax.ShapeDtypeStruct((D_Z, D), BF16),
                         jax.ShapeDtypeStruct((D, D), BF16), jax.ShapeDtypeStruct((SMALL_ROWS, D), F32)]
                        + [jax.ShapeDtypeStruct(cs.shape, cs.dtype) for cs in chip_sums]),
        in_specs=[rev(D), rev(D), rev(D_Z), zprev, rev(D), _full((8, D)), _full((1, D)),
                  _resident((D, D)), _full((N_HEADS, CHUNK, CHUNK)), _full((CHUNK, D_A)),
                  _full((1, D_A)), _full((1, D_A)), _full((len(WINDOWS), GROUP, GROUP)), _full((1, D_B)),
                  _full((1, D_B)), _full((8, D)), _full((8, D))] + [_resident(cs.shape) for cs in chip_sums],
        out_specs=tuple([rev(D_Z), _resident((D_Z, D)), _resident((D, D)), _full((SMALL_ROWS, D))]
                        + [hbm] * n_sums),
        scratch_shapes=[pltpu.VMEM((HALO, D_B), F32), pltpu.VMEM((D_Z, D), F32), pltpu.VMEM((D, D), F32),
                        pltpu.VMEM((tt, 2 * D_A), F32), pltpu.VMEM((CHUNK // 2, D), F32),
                        pltpu.SemaphoreType.DMA((2 * n_sums,)), pltpu.SemaphoreType.DMA((2 * n_sums,))],
        compiler_params=pltpu.CompilerParams(dimension_semantics=("arbitrary",), vmem_limit_bytes=VMEM_LIMIT),
    )(dmix, x, z, z, cat, mod, n1pre, w_out, w_sp, bs_rows, ln_g, ln_b, w_pool, b_pool, pool_scale,
      red_fwd, red_bwd, *chip_sums)
    return outs[:4], outs[4:]


def _adam(w, g, m, v):
    m2 = ADAM_B1 * m + (1.0 - ADAM_B1) * g
    v2 = ADAM_B2 * v + (1.0 - ADAM_B2) * (g * g)
    m_hat = m2 / (1.0 - ADAM_B1 ** ADAM_STEP)
    v_hat = v2 / (1.0 - ADAM_B2 ** ADAM_STEP)
    delta = -ADAM_LR * (m_hat / (jnp.sqrt(v_hat) + ADAM_EPS) + ADAM_WD * w)
    return delta, m2, v2


def _adamw_large(steps, fc, ada):
    n_fc = len(fc)
    w_ada, sc, total, m_ada, v_ada = ada
    ada_rows, cols = w_ada.shape

    def body(*refs):
        ins, outs = refs[:6 * n_fc], refs[6 * n_fc + 5:-1]
        for k in range(n_fc):
            w_ref, own_ref, arr_ref, diag_ref, m_ref, v_ref = ins[6 * k:6 * k + 6]
            g = ((own_ref[...] + arr_ref[0].astype(F32)) + arr_ref[1].astype(F32)) + diag_ref[...].astype(F32)
            outs[4 * k][...] = g
            outs[4 * k + 1][...], outs[4 * k + 2][...], outs[4 * k + 3][...] = _adam(
                w_ref[...], g, m_ref[...], v_ref[...])

        w_ref, sc_ref, t_ref, m_ref, v_ref = refs[6 * n_fc:6 * n_fc + 5]
        g_ref, d_ref, m2_ref, v2_ref = outs[4 * n_fc:]
        dm = refs[-1]
        me = _index(_place())
        for dev in range(N_DEV):
            @pl.when((pl.program_id(0) == 0) & (me == dev))
            def _():
                for b in range(N_DEV):
                    for k in range(6):
                        lo, hi = max(cols * dev, D * k), min(cols * (dev + 1), D * (k + 1))
                        if lo < hi:
                            dm[b:b + 1, lo - cols * dev:hi - cols * dev] = t_ref[
                                _table_row(b) + k:_table_row(b) + k + 1, lo - D * k:hi - D * k]

        g = _dot_tn(sc_ref[...].astype(BF16), dm[...].astype(BF16))
        g_ref[...] = g
        d_ref[...], m2_ref[...], v2_ref[...] = _adam(w_ref[...], g, m_ref[...], v_ref[...])

    specs_in, specs_out, shapes, args = [], [], [], []
    for w, own, arrived, diagonal, m, v in fc:
        rows, width = w.shape
        blk = pl.BlockSpec((rows // steps, width), lambda i: (i, 0))
        specs_in += [blk, blk, pl.BlockSpec((2, rows // steps, width), lambda i: (0, i, 0)), blk, blk, blk]
        specs_out += [blk] * 4
        shapes += [jax.ShapeDtypeStruct((rows, width), F32)] * 4
        args += [w, own, arrived, diagonal, m, v]
    rb = ada_rows // steps
    blk = pl.BlockSpec((rb, cols), lambda i: (i, 0))
    specs_in += [blk, pl.BlockSpec((N_DEV, rb), lambda i: (0, i)), _full(total.shape), blk, blk]
    specs_out += [blk] * 4
    shapes += [jax.ShapeDtypeStruct((ada_rows, cols), F32)] * 4
    outs = pl.pallas_call(
        body, name="adamw_large", grid=(steps,), out_shape=tuple(shapes), in_specs=specs_in,
        out_specs=tuple(specs_out), scratch_shapes=[pltpu.VMEM((N_DEV, cols), F32)],
        compiler_params=pltpu.CompilerParams(dimension_semantics=("arbitrary",), vmem_limit_bytes=VMEM_LIMIT),
    )(*args, w_ada, sc, total, m_ada, v_ada)
    return [outs[4 * k:4 * k + 4] for k in range(n_fc + 1)]


def _unfold(acc_rows):
    return jnp.concatenate([acc_rows[:, :D_A], acc_rows[:, D_A:]], axis=0)


def _adamw_small(total, params, shards):
    n = len(params)
    flat = [a for p in params for a in p]

    def body(*refs):
        s_ref = refs[0]
        p_refs = refs[1:1 + 3 * n]
        s_refs = refs[1 + 3 * n:1 + 3 * n + 4 * len(shards)]
        loss_ref = refs[1 + 3 * n + 4 * len(shards)]
        o_refs = refs[2 + 3 * n + 4 * len(shards):2 + 3 * n + 4 * len(shards) + 4 * n]
        so_refs = refs[2 + 3 * n + 4 * len(shards) + 4 * n:]
        d_b_ada = s_ref[0:6, :]
        for b in range(1, N_DEV):
            d_b_ada = d_b_ada + s_ref[_table_row(b):_table_row(b) + 6, :]
        misc = lambda r: s_ref[PK_MISC + r - ROW_N1PRE:PK_MISC + r - ROW_N1PRE + 1, :]
        loss = jnp.sum(misc(ROW_LOSS), axis=-1, keepdims=True) * (0.5 / D)
        loss_ref[...] = loss
        mask = _tril_mask()
        ws = _unfold(s_ref[PK_WS:PK_WS + 64, :])
        wp = _unfold(s_ref[PK_WP:PK_WP + 64, :])
        grads = [
            d_b_ada,
            misc(ROW_N1PRE), misc(ROW_N1POST), misc(ROW_N2PRE), misc(ROW_N2POST),
            misc(ROW_LN)[:, :D_A], misc(ROW_LN)[:, D_A:],
            misc(ROW_POOL)[:, :D_B], misc(ROW_POOL)[:, D_B:],
            s_ref[PK_BS:PK_BS + N_HEADS, 0:GROUP],
            jnp.stack([ws[:, h * GROUP:(h + 1) * GROUP] * mask for h in range(N_HEADS)]),
            jnp.stack([wp[:, g * GROUP:(g + 1) * GROUP] for g in range(len(WINDOWS))]),
        ]
        for k in range(n):
            w_ref, m_ref, v_ref = p_refs[3 * k:3 * k + 3]
            g = grads[k]
            if k == 0:
                for j in range(6):
                    o_refs[0][j] = g[j:j + 1, :]
                    o_refs[1][j], o_refs[2][j], o_refs[3][j] = _adam(w_ref[j], g[j:j + 1, :], m_ref[j], v_ref[j])
                continue
            o_refs[4 * k][...] = g
            o_refs[4 * k + 1][...], o_refs[4 * k + 2][...], o_refs[4 * k + 3][...] = _adam(
                w_ref[...], g, m_ref[...], v_ref[...])
        for k in range(len(shards)):
            w_ref, g_ref, m_ref, v_ref = s_refs[4 * k:4 * k + 4]
            so_refs[3 * k][...], so_refs[3 * k + 1][...], so_refs[3 * k + 2][...] = _adam(
                w_ref[...], g_ref[...], m_ref[...], v_ref[...])

    vm = pl.BlockSpec(memory_space=pltpu.VMEM)
    out_shape = [jax.ShapeDtypeStruct((1, 1), F32)]
    for w, _, _ in params:
        out_shape += [jax.ShapeDtypeStruct(w.shape, F32)] * 4
    for w, _, _, _ in shards:
        out_shape += [jax.ShapeDtypeStruct(w.shape, F32)] * 3
    return pl.pallas_call(
        body, name="adamw_small", out_shape=tuple(out_shape),
        in_specs=[vm] * (1 + 3 * n + 4 * len(shards)), out_specs=tuple([vm] * len(out_shape)),
        compiler_params=pltpu.CompilerParams(vmem_limit_bytes=VMEM_LIMIT),
    )(total, *flat, *[a for s in shards for a in s])


TT_ATTN_FWD = 512
ATTN_LAG = 2
TT_MLP_FWD = 512
TT_MLP = 256
TT_WGRAD = 2048
TT_ATTN_BWD = 512
TT_TAIL = 256


def kernel(x, c, w_ada, b_ada, norm1_pre, norm1_post, w_in, w_spatial, b_spatial, ln_v_gain, ln_v_bias, w_pool, b_pool, pool_scale, w_out, norm2_pre, norm2_post, w_fc1, w_fc2, loss_target, m_w_ada, m_b_ada, m_norm1_pre, m_norm1_post, m_w_in, m_w_spatial, m_b_spatial, m_ln_v_gain, m_ln_v_bias, m_w_pool, m_b_pool, m_pool_scale, m_w_out, m_norm2_pre, m_norm2_post, m_w_fc1, m_w_fc2, v_w_ada, v_b_ada, v_norm1_pre, v_norm1_post, v_w_in, v_w_spatial, v_b_spatial, v_ln_v_gain, v_ln_v_bias, v_w_pool, v_b_pool, v_pool_scale, v_w_out, v_norm2_pre, v_norm2_post, v_w_fc1, v_w_fc2):
    t_len = x.shape[1]
    ada_cols = w_ada.shape[1]
    tt = lambda want: min(want, t_len)

    x2 = x.reshape(t_len, D)
    tgt = loss_target.reshape(t_len, D)
    row = lambda a: a.reshape(1, -1)

    w_in_shard, w_out_shard, w1_shard, w2_shard = _cast_shards([w_in.T, w_out, w_fc1, w_fc2])

    bs_rows = jnp.repeat(b_spatial.T, GROUP, axis=1)
    attn_consts = (w_spatial, bs_rows, row(ln_v_gain), row(ln_v_bias), w_pool, row(b_pool), row(pool_scale))

    (z, cat, mix, x1, h2, r_begun, f_head, mod, sc), (w1_early, w2_early), w_out_all, w_in_t = _attn_fwd(
        tt(TT_ATTN_FWD), x2, c.reshape(1, D), w_ada, b_ada.reshape(N_DEV, 1, ada_cols), row(norm1_pre),
        row(norm1_post),
        w_in_shard, w_out_shard, *attn_consts, (w1_shard, w2_shard), row(norm2_pre))
    (r_early, f_early), (w1_late, w2_late) = _mlp_fwd_early(
        tt(TT_MLP_FWD), r_begun, h2, f_head, w1_early, w2_early)
    r_late, df, da, dmix, dx1, red_fwd, red_bwd = _mlp_late_bwd(
        tt(TT_MLP), r_early, x1, h2, f_early, tgt, mix, mod, row(norm2_pre), row(norm2_post), row(norm1_post),
        w1_early, w2_early, w1_late, w2_late)
    own_w1, own_w2, sums_w1, sums_w2, diag_w1, diag_w2 = _mlp_wgrad(tt(TT_WGRAD), r_early, r_late, da, df, h2)
    (dz, p_in, p_out, small), (arr_w1, arr_w2) = _attn_bwd(
        tt(TT_ATTN_BWD), dmix, x2, z, cat, mod, row(norm1_pre), w_out_all, *attn_consts, red_fwd, red_bwd,
        [sums_w1, sums_w2])
    grad_in_t, grad_out, total, grad_x = _tail_comm(
        [p_in.reshape(N_DEV, D_Z // N_DEV, D), p_out.reshape(N_DEV, D // N_DEV, D)], small, 64,
        tt(TT_TAIL), dz, dx1, x2, mod, row(norm1_pre), w_in_t)
    (grad_w1, d_w1, m_w1, v_w1), (grad_w2, d_w2, m_w2, v_w2), (grad_ada, d_ada, m_ada, v_ada) = _adamw_large(
        4, [(w_fc1, own_w1, arr_w1, diag_w1, m_w_fc1, v_w_fc1), (w_fc2, own_w2, arr_w2, diag_w2, m_w_fc2, v_w_fc2)],
        (w_ada, sc, total, m_w_ada, v_w_ada))

    six = lambda a: a.reshape(6, 1, D)
    small_params = [
        (six(b_ada), six(m_b_ada), six(v_b_ada)),
        (row(norm1_pre), row(m_norm1_pre), row(v_norm1_pre)),
        (row(norm1_post), row(m_norm1_post), row(v_norm1_post)),
        (row(norm2_pre), row(m_norm2_pre), row(v_norm2_pre)),
        (row(norm2_post), row(m_norm2_post), row(v_norm2_post)),
        (row(ln_v_gain), row(m_ln_v_gain), row(v_ln_v_gain)),
        (row(ln_v_bias), row(m_ln_v_bias), row(v_ln_v_bias)),
        (row(pool_scale), row(m_pool_scale), row(v_pool_scale)),
        (row(b_pool), row(m_b_pool), row(v_b_pool)),
        (b_spatial, m_b_spatial, v_b_spatial),
        (w_spatial, m_w_spatial, v_w_spatial),
        (w_pool, m_w_pool, v_w_pool),
    ]
    outs = _adamw_small(total, small_params, [(w_out, grad_out, m_w_out, v_w_out),
                                              (w_in.T, grad_in_t, m_w_in.T, v_w_in.T)])
    d_out, m_out, v_out, d_in_t, m_in_t, v_in_t = outs[1 + 4 * len(small_params):]
    loss = outs[0].reshape(())
    names = ["b_ada", "norm1_pre", "norm1_post", "norm2_pre", "norm2_post", "ln_v_gain", "ln_v_bias", "pool_scale",
             "b_pool", "b_spatial", "w_spatial", "w_pool"]
    shapes = dict(b_ada=b_ada.shape, norm1_pre=norm1_pre.shape, norm1_post=norm1_post.shape,
                  norm2_pre=norm2_pre.shape, norm2_post=norm2_post.shape, ln_v_gain=ln_v_gain.shape,
                  ln_v_bias=ln_v_bias.shape, pool_scale=pool_scale.shape, b_pool=b_pool.shape,
                  b_spatial=b_spatial.shape, w_spatial=w_spatial.shape, w_pool=w_pool.shape)
    res = {}
    for k, nm in enumerate(names):
        res[nm] = tuple(o.reshape(shapes[nm]) for o in outs[1 + 4 * k:5 + 4 * k])
    res["w_ada"] = (grad_ada, d_ada, m_ada, v_ada)
    res["w_in"] = (grad_in_t.T, d_in_t.T, m_in_t.T, v_in_t.T)
    res["w_out"] = (grad_out, d_out, m_out, v_out)
    res["w_fc1"] = (grad_w1, d_w1, m_w1, v_w1)
    res["w_fc2"] = (grad_w2, d_w2, m_w2, v_w2)

    order = ["w_ada", "b_ada", "norm1_pre", "norm1_post", "w_in", "w_spatial", "b_spatial", "ln_v_gain", "ln_v_bias",
             "w_pool", "b_pool", "pool_scale", "w_out", "norm2_pre", "norm2_post", "w_fc1", "w_fc2"]
    return (loss, grad_x.reshape(x.shape),
            *[res[nm][0] for nm in order], *[res[nm][1] for nm in order],
            *[res[nm][2] for nm in order], *[res[nm][3] for nm in order])
```
